```python
import math
import jax, jax.numpy as jnp
from jax import lax
import numpy as np

D_MODEL = 1024
BATCH = 8
SEQ = 2048
DEPTH = 4

N_META = 16
MIX_W = D_MODEL
ATT_W = MIX_W // 2
REC_W = MIX_W - ATT_W
ATT_HEAD_DIM = 64
N_ATT_HEADS = ATT_W // ATT_HEAD_DIM
N_REC_BLOCKS = 8
REC_BLOCK = REC_W // N_REC_BLOCKS
CONV_WIDTH = 4
RG_C = 8.0
D_FF = 4 * D_MODEL
Q_BLOCK = 128
NORM_EPS = 1e-6
D_IN = 3 * ATT_W + N_ATT_HEADS + 2 * REC_W

kernel_name = "hymba_fox_rglru_hybrid"


def rmsnorm(x, g):
    xf = x.astype(jnp.float32)
    y = xf * lax.rsqrt(jnp.mean(xf * xf, axis=-1, keepdims=True) + NORM_EPS)
    return (y * g.astype(jnp.float32)).astype(x.dtype)


def forgetting_attention(q, k, v, log_f):
    T = q.shape[1]
    scale = 1.0 / math.sqrt(q.shape[-1])
    c = jnp.cumsum(log_f, axis=1)
    c_bh = jnp.transpose(c, (0, 2, 1))
    starts = [0] + list(range(N_META, T, Q_BLOCK))
    ends = starts[1:] + [T]
    pos = jnp.arange(T)
    outs = []
    for qs, qe in zip(starts, ends):
        qb = q[:, qs:qe]
        kb = k[:, :qe]
        vb = v[:, :qe]
        s = jnp.einsum('bqhd,bkhd->bhqk', qb, kb).astype(jnp.float32) * scale
        bias = c_bh[:, :, qs:qe, None] - c_bh[:, :, None, :qe]
        mask = pos[qs:qe, None] >= pos[None, :qe]
        s = jnp.where(mask[None, None], s + bias, -jnp.inf)
        p = jax.nn.softmax(s, axis=-1).astype(vb.dtype)
        outs.append(jnp.einsum('bhqk,bkhd->bqhd', p, vb))
    return jnp.concatenate(outs, axis=1)


def causal_depthwise_conv(x, w, b):
    C = x.shape[-1]
    y = lax.conv_general_dilated(
        x, w[:, None, :].astype(x.dtype), window_strides=(1,),
        padding=[(CONV_WIDTH - 1, 0)], dimension_numbers=('NWC', 'WIO', 'NWC'),
        feature_group_count=C)
    return y + b.astype(x.dtype)


def block_diag_linear(x, w, b):
    B, T, C = x.shape
    xb = x.reshape(B, T, N_REC_BLOCKS, REC_BLOCK)
    y = jnp.einsum('btnd,nde->btne', xb, w.astype(x.dtype)).reshape(B, T, C)
    return y + b.astype(x.dtype)


def rg_lru(x, w_ga, b_ga, w_gx, b_gx, lru_L):
    r = jax.nn.sigmoid(block_diag_linear(x, w_ga, b_ga).astype(jnp.float32))
    i = jax.nn.sigmoid(block_diag_linear(x, w_gx, b_gx).astype(jnp.float32))
    log_a = RG_C * r * jax.nn.log_sigmoid(lru_L.astype(jnp.float32))
    a = jnp.exp(log_a)
    mult = jnp.sqrt(-jnp.expm1(2.0 * log_a))
    u = mult * i * x.astype(jnp.float32)

    def combine(e1, e2):
        a1, b1 = e1
        a2, b2 = e2
        return a1 * a2, a2 * b1 + b2

    _, h = lax.associative_scan(combine, (a, u), axis=1)
    return h.astype(x.dtype)


def _fwd_setup_inputs(seed: int = 0) -> dict:
    key = jax.random.key(seed)
    ks = jax.random.split(key, 24)
    f32 = jnp.float32
    L = DEPTH
    nrm = lambda k, shape, s: (jax.random.normal(k, shape, f32) * s)
    x = jax.random.normal(ks[0], (BATCH, SEQ, D_MODEL), f32)
    meta = nrm(ks[1], (N_META, D_MODEL), 1.0)
    attn_norm_g = 1.0 + nrm(ks[2], (L, D_MODEL), 0.02)
    w_in = nrm(ks[3], (L, D_MODEL, D_IN), D_MODEL ** -0.5)
    b_f = jax.random.uniform(ks[4], (L, N_ATT_HEADS), f32, 1.0, 5.0)
    conv_w = nrm(ks[5], (L, CONV_WIDTH, REC_W), CONV_WIDTH ** -0.5)
    conv_b = nrm(ks[6], (L, REC_W), 0.02)
    w_gate_a = nrm(ks[7], (L, N_REC_BLOCKS, REC_BLOCK, REC_BLOCK), REC_BLOCK ** -0.5)
    b_gate_a = nrm(ks[8], (L, REC_W), 0.02)
    w_gate_x = nrm(ks[9], (L, N_REC_BLOCKS, REC_BLOCK, REC_BLOCK), REC_BLOCK ** -0.5)
    b_gate_x = nrm(ks[10], (L, REC_W), 0.02)
    u = jax.random.uniform(ks[11], (L, REC_W), f32, 0.9, 0.999)
    a_base = u ** (1.0 / RG_C)
    lru_L = jnp.log(a_base) - jnp.log1p(-a_base)
    attn_out_g = 1.0 + nrm(ks[12], (L, ATT_W), 0.02)
    rec_out_g = 1.0 + nrm(ks[13], (L, REC_W), 0.02)
    w_out = nrm(ks[14], (L, MIX_W, D_MODEL), (MIX_W * 2 * DEPTH) ** -0.5)
    mlp_norm_g = 1.0 + nrm(ks[15], (L, D_MODEL), 0.02)
    w_up = nrm(ks[16], (L, D_MODEL, D_FF), D_MODEL ** -0.5)
    w_down = nrm(ks[17], (L, D_FF, D_MODEL), (D_FF * 2 * DEPTH) ** -0.5)
    final_g = 1.0 + nrm(ks[18], (D_MODEL,), 0.02)
    return {"x": x, "meta": meta, "attn_norm_g": attn_norm_g, "w_in": w_in,
            "b_f": b_f, "conv_w": conv_w, "conv_b": conv_b,
            "w_gate_a": w_gate_a, "b_gate_a": b_gate_a,
            "w_gate_x": w_gate_x, "b_gate_x": b_gate_x, "lru_L": lru_L,
            "attn_out_g": attn_out_g, "rec_out_g": rec_out_g, "w_out": w_out,
            "mlp_norm_g": mlp_norm_g, "w_up": w_up, "w_down": w_down,
            "final_g": final_g}


def _fwd_reference(x, meta, attn_norm_g, w_in, b_f, conv_w, conv_b, w_gate_a, b_gate_a,
              w_gate_x, b_gate_x, lru_L, attn_out_g, rec_out_g, w_out,
              mlp_norm_g, w_up, w_down, final_g):
    B, S, D = x.shape
    meta_b = jnp.broadcast_to(meta.astype(x.dtype)[None], (B, N_META, D))
    h = jnp.concatenate([meta_b, x], axis=1)
    T = h.shape[1]
    splits = np.cumsum([ATT_W, ATT_W, ATT_W, N_ATT_HEADS, REC_W]).tolist()
    for l in range(DEPTH):
        z = rmsnorm(h, attn_norm_g[l])
        proj = z @ w_in[l]
        q, k, v, f_logit, xr, yr = jnp.split(proj, splits, axis=-1)
        log_f = jax.nn.log_sigmoid(f_logit.astype(jnp.float32) + b_f[l].astype(jnp.float32))
        qh = q.reshape(B, T, N_ATT_HEADS, ATT_HEAD_DIM)
        kh = k.reshape(B, T, N_ATT_HEADS, ATT_HEAD_DIM)
        vh = v.reshape(B, T, N_ATT_HEADS, ATT_HEAD_DIM)
        attn = forgetting_attention(qh, kh, vh, log_f).reshape(B, T, ATT_W)
        xc = causal_depthwise_conv(xr, conv_w[l], conv_b[l])
        hr = rg_lru(xc, w_gate_a[l], b_gate_a[l], w_gate_x[l], b_gate_x[l], lru_L[l])
        rec = hr * jax.nn.gelu(yr)
        mix = jnp.concatenate([rmsnorm(attn, attn_out_g[l]), rmsnorm(rec, rec_out_g[l])], axis=-1)
        h = h + mix @ w_out[l]
        z = rmsnorm(h, mlp_norm_g[l])
        u = jax.nn.relu(z @ w_up[l])
        h = h + (u * u) @ w_down[l]
    h = rmsnorm(h, final_g)
    return h[:, N_META:]


import jax as _jax
import jax.numpy as _jnp

TWIN_FORMAT = 'train_step'
FWD_PARAMS = ['x', 'meta', 'attn_norm_g', 'w_in', 'b_f', 'conv_w', 'conv_b', 'w_gate_a', 'b_gate_a', 'w_gate_x', 'b_gate_x', 'lru_L', 'attn_out_g', 'rec_out_g', 'w_out', 'mlp_norm_g', 'w_up', 'w_down', 'final_g']
TWIN_WEIGHTS = ['meta', 'attn_norm_g', 'w_in', 'b_f', 'conv_w', 'conv_b', 'w_gate_a', 'b_gate_a', 'w_gate_x', 'b_gate_x', 'lru_L', 'attn_out_g', 'rec_out_g', 'w_out', 'mlp_norm_g', 'w_up', 'w_down', 'final_g']
TWIN_DIFF_INPUT = 'x'
TWIN_INPUTS = ['x', 'meta', 'attn_norm_g', 'w_in', 'b_f', 'conv_w', 'conv_b', 'w_gate_a', 'b_gate_a', 'w_gate_x', 'b_gate_x', 'lru_L', 'attn_out_g', 'rec_out_g', 'w_out', 'mlp_norm_g', 'w_up', 'w_down', 'final_g', 'loss_target', 'm_meta', 'm_attn_norm_g', 'm_w_in', 'm_b_f', 'm_conv_w', 'm_conv_b', 'm_w_gate_a', 'm_b_gate_a', 'm_w_gate_x', 'm_b_gate_x', 'm_lru_L', 'm_attn_out_g', 'm_rec_out_g', 'm_w_out', 'm_mlp_norm_g', 'm_w_up', 'm_w_down', 'm_final_g', 'v_meta', 'v_attn_norm_g', 'v_w_in', 'v_b_f', 'v_conv_w', 'v_conv_b', 'v_w_gate_a', 'v_b_gate_a', 'v_w_gate_x', 'v_b_gate_x', 'v_lru_L', 'v_attn_out_g', 'v_rec_out_g', 'v_w_out', 'v_mlp_norm_g', 'v_w_up', 'v_w_down', 'v_final_g']
TWIN_OUTPUTS = ['loss', 'grad_x', 'grad_meta', 'grad_attn_norm_g', 'grad_w_in', 'grad_b_f', 'grad_conv_w', 'grad_conv_b', 'grad_w_gate_a', 'grad_b_gate_a', 'grad_w_gate_x', 'grad_b_gate_x', 'grad_lru_L', 'grad_attn_out_g', 'grad_rec_out_g', 'grad_w_out', 'grad_mlp_norm_g', 'grad_w_up', 'grad_w_down', 'grad_final_g', 'delta_meta', 'delta_attn_norm_g', 'delta_w_in', 'delta_b_f', 'delta_conv_w', 'delta_conv_b', 'delta_w_gate_a', 'delta_b_gate_a', 'delta_w_gate_x', 'delta_b_gate_x', 'delta_lru_L', 'delta_attn_out_g', 'delta_rec_out_g', 'delta_w_out', 'delta_mlp_norm_g', 'delta_w_up', 'delta_w_down', 'delta_final_g', 'new_m_meta', 'new_m_attn_norm_g', 'new_m_w_in', 'new_m_b_f', 'new_m_conv_w', 'new_m_conv_b', 'new_m_w_gate_a', 'new_m_b_gate_a', 'new_m_w_gate_x', 'new_m_b_gate_x', 'new_m_lru_L', 'new_m_attn_out_g', 'new_m_rec_out_g', 'new_m_w_out', 'new_m_mlp_norm_g', 'new_m_w_up', 'new_m_w_down', 'new_m_final_g', 'new_v_meta', 'new_v_attn_norm_g', 'new_v_w_in', 'new_v_b_f', 'new_v_conv_w', 'new_v_conv_b', 'new_v_w_gate_a', 'new_v_b_gate_a', 'new_v_w_gate_x', 'new_v_b_gate_x', 'new_v_lru_L', 'new_v_attn_out_g', 'new_v_rec_out_g', 'new_v_w_out', 'new_v_mlp_norm_g', 'new_v_w_up', 'new_v_w_down', 'new_v_final_g']
TWIN_LEAF_KINDS = {'loss': 'loss', 'grad_x': 'grad_x', 'grad_meta': 'grad_w', 'grad_attn_norm_g': 'grad_w', 'grad_w_in': 'grad_w', 'grad_b_f': 'grad_w', 'grad_conv_w': 'grad_w', 'grad_conv_b': 'grad_w', 'grad_w_gate_a': 'grad_w', 'grad_b_gate_a': 'grad_w', 'grad_w_gate_x': 'grad_w', 'grad_b_gate_x': 'grad_w', 'grad_lru_L': 'grad_w', 'grad_attn_out_g': 'grad_w', 'grad_rec_out_g': 'grad_w', 'grad_w_out': 'grad_w', 'grad_mlp_norm_g': 'grad_w', 'grad_w_up': 'grad_w', 'grad_w_down': 'grad_w', 'grad_final_g': 'grad_w', 'delta_meta': 'delta_w', 'delta_attn_norm_g': 'delta_w', 'delta_w_in': 'delta_w', 'delta_b_f': 'delta_w', 'delta_conv_w': 'delta_w', 'delta_conv_b': 'delta_w', 'delta_w_gate_a': 'delta_w', 'delta_b_gate_a': 'delta_w', 'delta_w_gate_x': 'delta_w', 'delta_b_gate_x': 'delta_w', 'delta_lru_L': 'delta_w', 'delta_attn_out_g': 'delta_w', 'delta_rec_out_g': 'delta_w', 'delta_w_out': 'delta_w', 'delta_mlp_norm_g': 'delta_w', 'delta_w_up': 'delta_w', 'delta_w_down': 'delta_w', 'delta_final_g': 'delta_w', 'new_m_meta': 'new_m', 'new_m_attn_norm_g': 'new_m', 'new_m_w_in': 'new_m', 'new_m_b_f': 'new_m', 'new_m_conv_w': 'new_m', 'new_m_conv_b': 'new_m', 'new_m_w_gate_a': 'new_m', 'new_m_b_gate_a': 'new_m', 'new_m_w_gate_x': 'new_m', 'new_m_b_gate_x': 'new_m', 'new_m_lru_L': 'new_m', 'new_m_attn_out_g': 'new_m', 'new_m_rec_out_g': 'new_m', 'new_m_w_out': 'new_m', 'new_m_mlp_norm_g': 'new_m', 'new_m_w_up': 'new_m', 'new_m_w_down': 'new_m', 'new_m_final_g': 'new_m', 'new_v_meta': 'new_v', 'new_v_attn_norm_g': 'new_v', 'new_v_w_in': 'new_v', 'new_v_b_f': 'new_v', 'new_v_conv_w': 'new_v', 'new_v_conv_b': 'new_v', 'new_v_w_gate_a': 'new_v', 'new_v_b_gate_a': 'new_v', 'new_v_w_gate_x': 'new_v', 'new_v_b_gate_x': 'new_v', 'new_v_lru_L': 'new_v', 'new_v_attn_out_g': 'new_v', 'new_v_rec_out_g': 'new_v', 'new_v_w_out': 'new_v', 'new_v_mlp_norm_g': 'new_v', 'new_v_w_up': 'new_v', 'new_v_w_down': 'new_v', 'new_v_final_g': 'new_v'}


def _forward(args):
    return _fwd_reference(*[args[k] for k in FWD_PARAMS])


def _output_shape():
    out = _jax.eval_shape(lambda: _forward(_fwd_setup_inputs(0)))
    return out.shape, out.dtype

N_MICROBATCH = 1
ADAM_LR = 0.001
ADAM_B1 = 0.9
ADAM_B2 = 0.999
ADAM_EPS = 1e-08
ADAM_WD = 0.01
ADAM_STEP = 10
PER_EXAMPLE_BATCH_AXIS = {'x': 0, 'loss_target': 0}
SHARED_INPUTS = []
_WEIGHT_DTYPES = {'meta': _jnp.float32, 'attn_norm_g': _jnp.float32, 'w_in': _jnp.float32, 'b_f': _jnp.float32, 'conv_w': _jnp.float32, 'conv_b': _jnp.float32, 'w_gate_a': _jnp.float32, 'b_gate_a': _jnp.float32, 'w_gate_x': _jnp.float32, 'b_gate_x': _jnp.float32, 'lru_L': _jnp.float32, 'attn_out_g': _jnp.float32, 'rec_out_g': _jnp.float32, 'w_out': _jnp.float32, 'mlp_norm_g': _jnp.float32, 'w_up': _jnp.float32, 'w_down': _jnp.float32, 'final_g': _jnp.float32}
MOMENT_SCALE = {'meta': 4.979077e-03, 'attn_norm_g': 5.970126e-02, 'w_in': 3.648541e-02, 'b_f': 1.478234e-01, 'conv_w': 4.519801e-02, 'conv_b': 3.443253e-01, 'w_gate_a': 1.358518e-02, 'b_gate_a': 1.027041e-02, 'w_gate_x': 2.472739e-02, 'b_gate_x': 1.797018e-02, 'lru_L': 2.122631e-02, 'attn_out_g': 4.315280e-02, 'rec_out_g': 5.147395e-02, 'w_out': 1.257493e-01, 'mlp_norm_g': 5.147715e-02, 'w_up': 2.543263e-02, 'w_down': 1.521914e-01, 'final_g': 1.615832e+01}


def _to_microbatches(a, axis):
    t = _jnp.moveaxis(a, axis, 0)
    t = t.reshape((N_MICROBATCH, t.shape[0] // N_MICROBATCH) + t.shape[1:])
    return _jnp.moveaxis(t, 1, axis + 1)


def setup_inputs(seed: int = 0) -> dict:
    inp = _fwd_setup_inputs(seed)
    key = _jax.random.fold_in(_jax.random.key(seed), 7919)
    shape, _ = _output_shape()
    out = dict(inp)
    out["loss_target"] = _jax.random.normal(_jax.random.fold_in(key, 0), shape, _jnp.float32)
    for i, name in enumerate(TWIN_WEIGHTS):
        w = inp[name].astype(_jnp.float32)
        if MOMENT_SCALE is None:
            s = _jnp.sqrt(_jnp.mean(_jnp.square(w)) + 1e-30)
        else:
            s = MOMENT_SCALE[name]
        km, kv = _jax.random.split(_jax.random.fold_in(key, i + 1))
        out[name] = w
        out["m_" + name] = s * _jax.random.normal(km, w.shape, _jnp.float32)
        out["v_" + name] = (s * s) * _jax.random.uniform(kv, w.shape, _jnp.float32, 0.5, 1.5)
    if N_MICROBATCH > 1:
        for name, axis in PER_EXAMPLE_BATCH_AXIS.items():
            out[name] = _to_microbatches(out[name], axis)
    return {'x': out['x'], 'meta': out['meta'], 'attn_norm_g': out['attn_norm_g'], 'w_in': out['w_in'], 'b_f': out['b_f'], 'conv_w': out['conv_w'], 'conv_b': out['conv_b'], 'w_gate_a': out['w_gate_a'], 'b_gate_a': out['b_gate_a'], 'w_gate_x': out['w_gate_x'], 'b_gate_x': out['b_gate_x'], 'lru_L': out['lru_L'], 'attn_out_g': out['attn_out_g'], 'rec_out_g': out['rec_out_g'], 'w_out': out['w_out'], 'mlp_norm_g': out['mlp_norm_g'], 'w_up': out['w_up'], 'w_down': out['w_down'], 'final_g': out['final_g'], 'loss_target': out['loss_target'], 'm_meta': out['m_meta'], 'm_attn_norm_g': out['m_attn_norm_g'], 'm_w_in': out['m_w_in'], 'm_b_f': out['m_b_f'], 'm_conv_w': out['m_conv_w'], 'm_conv_b': out['m_conv_b'], 'm_w_gate_a': out['m_w_gate_a'], 'm_b_gate_a': out['m_b_gate_a'], 'm_w_gate_x': out['m_w_gate_x'], 'm_b_gate_x': out['m_b_gate_x'], 'm_lru_L': out['m_lru_L'], 'm_attn_out_g': out['m_attn_out_g'], 'm_rec_out_g': out['m_rec_out_g'], 'm_w_out': out['m_w_out'], 'm_mlp_norm_g': out['m_mlp_norm_g'], 'm_w_up': out['m_w_up'], 'm_w_down': out['m_w_down'], 'm_final_g': out['m_final_g'], 'v_meta': out['v_meta'], 'v_attn_norm_g': out['v_attn_norm_g'], 'v_w_in': out['v_w_in'], 'v_b_f': out['v_b_f'], 'v_conv_w': out['v_conv_w'], 'v_conv_b': out['v_conv_b'], 'v_w_gate_a': out['v_w_gate_a'], 'v_b_gate_a': out['v_b_gate_a'], 'v_w_gate_x': out['v_w_gate_x'], 'v_b_gate_x': out['v_b_gate_x'], 'v_lru_L': out['v_lru_L'], 'v_attn_out_g': out['v_attn_out_g'], 'v_rec_out_g': out['v_rec_out_g'], 'v_w_out': out['v_w_out'], 'v_mlp_norm_g': out['v_mlp_norm_g'], 'v_w_up': out['v_w_up'], 'v_w_down': out['v_w_down'], 'v_final_g': out['v_final_g']}


def _loss(weights, diff, rest, loss_target):
    with _jax.named_scope("forward"):
        args = {**rest, TWIN_DIFF_INPUT: diff, **{k: w.astype(_WEIGHT_DTYPES[k]) for k, w in weights.items()}}
        y = _forward(args)
    with _jax.named_scope("loss_head"):
        err = _jnp.square(y.astype(_jnp.float32) - loss_target)
        return 0.5 * _jnp.sum(_jnp.mean(err, axis=-1)) if err.ndim else 0.5 * err


def _adamw(w, g, m, v):
    m = ADAM_B1 * m + (1.0 - ADAM_B1) * g
    v = ADAM_B2 * v + (1.0 - ADAM_B2) * _jnp.square(g)
    m_hat = m / (1.0 - ADAM_B1 ** ADAM_STEP)
    v_hat = v / (1.0 - ADAM_B2 ** ADAM_STEP)
    delta = -ADAM_LR * (m_hat / (_jnp.sqrt(v_hat) + ADAM_EPS) + ADAM_WD * w)
    return delta, m, v


def reference(x, meta, attn_norm_g, w_in, b_f, conv_w, conv_b, w_gate_a, b_gate_a, w_gate_x, b_gate_x, lru_L, attn_out_g, rec_out_g, w_out, mlp_norm_g, w_up, w_down, final_g, loss_target, m_meta, m_attn_norm_g, m_w_in, m_b_f, m_conv_w, m_conv_b, m_w_gate_a, m_b_gate_a, m_w_gate_x, m_b_gate_x, m_lru_L, m_attn_out_g, m_rec_out_g, m_w_out, m_mlp_norm_g, m_w_up, m_w_down, m_final_g, v_meta, v_attn_norm_g, v_w_in, v_b_f, v_conv_w, v_conv_b, v_w_gate_a, v_b_gate_a, v_w_gate_x, v_b_gate_x, v_lru_L, v_attn_out_g, v_rec_out_g, v_w_out, v_mlp_norm_g, v_w_up, v_w_down, v_final_g):
    given = dict(x=x, meta=meta, attn_norm_g=attn_norm_g, w_in=w_in, b_f=b_f, conv_w=conv_w, conv_b=conv_b, w_gate_a=w_gate_a, b_gate_a=b_gate_a, w_gate_x=w_gate_x, b_gate_x=b_gate_x, lru_L=lru_L, attn_out_g=attn_out_g, rec_out_g=rec_out_g, w_out=w_out, mlp_norm_g=mlp_norm_g, w_up=w_up, w_down=w_down, final_g=final_g, loss_target=loss_target, m_meta=m_meta, m_attn_norm_g=m_attn_norm_g, m_w_in=m_w_in, m_b_f=m_b_f, m_conv_w=m_conv_w, m_conv_b=m_conv_b, m_w_gate_a=m_w_gate_a, m_b_gate_a=m_b_gate_a, m_w_gate_x=m_w_gate_x, m_b_gate_x=m_b_gate_x, m_lru_L=m_lru_L, m_attn_out_g=m_attn_out_g, m_rec_out_g=m_rec_out_g, m_w_out=m_w_out, m_mlp_norm_g=m_mlp_norm_g, m_w_up=m_w_up, m_w_down=m_w_down, m_final_g=m_final_g, v_meta=v_meta, v_attn_norm_g=v_attn_norm_g, v_w_in=v_w_in, v_b_f=v_b_f, v_conv_w=v_conv_w, v_conv_b=v_conv_b, v_w_gate_a=v_w_gate_a, v_b_gate_a=v_b_gate_a, v_w_gate_x=v_w_gate_x, v_b_gate_x=v_b_gate_x, v_lru_L=v_lru_L, v_attn_out_g=v_attn_out_g, v_rec_out_g=v_rec_out_g, v_w_out=v_w_out, v_mlp_norm_g=v_mlp_norm_g, v_w_up=v_w_up, v_w_down=v_w_down, v_final_g=v_final_g)
    weights = {n: given[n] for n in TWIN_WEIGHTS}
    shared = {n: given[n] for n in SHARED_INPUTS}
    per_example = {n: given[n] for n in ['x']}
    grad_fn = _jax.value_and_grad(_loss, argnums=(0, 1))

    def one_microbatch(ex, loss_target):
        ex = dict(ex)
        diff = ex.pop(TWIN_DIFF_INPUT)
        return grad_fn(weights, diff, {**shared, **ex}, loss_target)

    if N_MICROBATCH == 1:
        loss, (grad_w, grad_x) = one_microbatch(per_example, given["loss_target"])
    else:
        def body(carry, xs):
            loss_sum, grad_sum = carry
            l_k, (gw_k, gx_k) = one_microbatch(xs[0], xs[1])
            with _jax.named_scope("update"):
                return (loss_sum + l_k, _jax.tree.map(_jnp.add, grad_sum, gw_k)), gx_k

        init = (_jnp.zeros((), _jnp.float32), _jax.tree.map(_jnp.zeros_like, weights))
        (loss, grad_w), grad_x = _jax.lax.scan(body, init, (per_example, given["loss_target"]))
    with _jax.named_scope("update"):
        delta_w, new_m, new_v = {}, {}, {}
        for n in TWIN_WEIGHTS:
            delta_w[n], new_m[n], new_v[n] = _adamw(weights[n], grad_w[n], given["m_" + n], given["v_" + n])
    return (loss, grad_x, *[grad_w[n] for n in TWIN_WEIGHTS], *[delta_w[n] for n in TWIN_WEIGHTS],
            *[new_m[n] for n in TWIN_WEIGHTS], *[new_v[n] for n in TWIN_WEIGHTS])
```

```python
import functools
import math

import jax
import jax.numpy as jnp
from jax import lax
from jax.experimental import pallas as pl
from jax.experimental.pallas import tpu as pltpu

F32 = jnp.float32
BF16 = jnp.bfloat16

N_DEV = 8
N_META = 16
HEAD_DIM = 64
CONV_WIDTH = 4
RG_C = 8.0
NORM_EPS = 1e-6
LANES = 128
SUBLANES = 8
ATT_BLOCK = 128
NEG_BIG = -1e30
ATT_SCALE = 1.0 / math.sqrt(HEAD_DIM)

ADAM_LR = 0.001
ADAM_B1 = 0.9
ADAM_B2 = 0.999
ADAM_EPS = 1e-08
ADAM_WD = 0.01
ADAM_STEP = 10

VMEM_LIMIT_BYTES = 56 * 1024 * 1024
MESH = pl.DeviceIdType.MESH
ANY = pl.BlockSpec(memory_space=pl.ANY)


def _cparams(*sem):
    return pltpu.CompilerParams(dimension_semantics=sem if sem else None,
                                vmem_limit_bytes=VMEM_LIMIT_BYTES)


def _dot(a, b):
    return jnp.dot(a, b, preferred_element_type=F32)


def _dot_nt(a, b):
    return lax.dot_general(a, b, (((1,), (1,)), ((), ())), preferred_element_type=F32)


def _dot_tn(a, b):
    return lax.dot_general(a, b, (((0,), (0,)), ((), ())), preferred_element_type=F32)


def _sigmoid(x):
    return 1.0 / (1.0 + jnp.exp(-x))


def _log_sigmoid(x):
    return jnp.minimum(x, 0.0) - jnp.log(1.0 + jnp.exp(-jnp.abs(x)))


def _expm1(x):
    series = x * (1.0 + x * (0.5 + x * (1.0 / 6.0 + x * (1.0 / 24.0))))
    return jnp.where(jnp.abs(x) < 1e-2, series, jnp.exp(x) - 1.0)


_GELU_K = math.sqrt(2.0 / math.pi)
_GELU_C = 0.044715


def _gelu(x):
    t = jnp.tanh(_GELU_K * (x + _GELU_C * x * x * x))
    return 0.5 * x * (1.0 + t)


def _gelu_grad(x):
    t = jnp.tanh(_GELU_K * (x + _GELU_C * x * x * x))
    return 0.5 * (1.0 + t) + 0.5 * x * (1.0 - t * t) * _GELU_K * (1.0 + 3.0 * _GELU_C * x * x)


def _split3_dot(tri, x):
    hi = x.astype(BF16)
    r1 = x - hi.astype(F32)
    mid = r1.astype(BF16)
    lo = (r1 - mid.astype(F32)).astype(BF16)
    return _dot(tri, hi) + _dot(tri, mid) + _dot(tri, lo)


def _dot_split3(x, sel):
    hi = x.astype(BF16)
    r1 = x - hi.astype(F32)
    mid = r1.astype(BF16)
    lo = (r1 - mid.astype(F32)).astype(BF16)
    return _dot(hi, sel) + _dot(mid, sel) + _dot(lo, sel)


def _rms_fwd(x, g):
    r = lax.rsqrt(jnp.mean(x * x, axis=-1, keepdims=True) + NORM_EPS)
    return x * r * g


def _rms_bwd(x, g, dy):
    r = lax.rsqrt(jnp.mean(x * x, axis=-1, keepdims=True) + NORM_EPS)
    xn = x * r
    dxn = dy * g
    dx = r * (dxn - xn * jnp.mean(dxn * xn, axis=-1, keepdims=True))
    return dx, jnp.sum(dy * xn, axis=0, keepdims=True)


def _accumulate(ref, val, first):
    @pl.when(first)
    def _():
        ref[...] = val

    @pl.when(jnp.logical_not(first))
    def _():
        ref[...] += val


def in_proj(h, g1, wqkv, wrest, tm):
    tp, d = h.shape
    nq, nr = wqkv.shape[1], wrest.shape[1]

    def body(h_ref, g_ref, wq_ref, wr_ref, z_ref, qkv_ref, rest_ref):
        z = _rms_fwd(h_ref[...], g_ref[...]).astype(BF16)
        z_ref[...] = z
        qkv_ref[...] = _dot(z, wq_ref[...]).astype(BF16)
        rest_ref[...] = _dot(z, wr_ref[...])

    return pl.pallas_call(
        body, name="in_proj", grid=(tp // tm,),
        in_specs=[pl.BlockSpec((tm, d), lambda i: (i, 0)),
                  pl.BlockSpec((1, d), lambda i: (0, 0)),
                  pl.BlockSpec((d, nq), lambda i: (0, 0)),
                  pl.BlockSpec((d, nr), lambda i: (0, 0))],
        out_specs=[pl.BlockSpec((tm, d), lambda i: (i, 0)),
                   pl.BlockSpec((tm, nq), lambda i: (i, 0)),
                   pl.BlockSpec((tm, nr), lambda i: (i, 0))],
        out_shape=[jax.ShapeDtypeStruct((tp, d), BF16),
                   jax.ShapeDtypeStruct((tp, nq), BF16),
                   jax.ShapeDtypeStruct((tp, nr), F32)],
        compiler_params=_cparams("parallel"),
    )(h, g1, wqkv, wrest)


def fgate_fwd(rest, bf_pad, fcol):
    tp = rest.shape[0]
    nb = tp // ATT_BLOCK

    def body(f_ref, b_ref, c_ref, ct_ref):
        r_i = lax.broadcasted_iota(jnp.int32, (ATT_BLOCK, ATT_BLOCK), 0)
        c_i = lax.broadcasted_iota(jnp.int32, (ATT_BLOCK, ATT_BLOCK), 1)
        tri = (r_i >= c_i).astype(BF16)
        carry = jnp.zeros((1, LANES), F32)
        for i in range(nb):
            sl = slice(i * ATT_BLOCK, (i + 1) * ATT_BLOCK)
            lf = _log_sigmoid(f_ref[sl, :] + b_ref[...])
            cs = _split3_dot(tri, lf) + carry
            carry = cs[ATT_BLOCK - 1:ATT_BLOCK, :]
            c_ref[sl, :] = cs
            ct_ref[:, sl] = cs.T[0:SUBLANES, :]

    return pl.pallas_call(
        body, name="fgate_fwd", grid=(1,),
        in_specs=[pl.BlockSpec((tp, LANES), lambda i: (0, fcol)),
                  pl.BlockSpec((1, LANES), lambda i: (0, 0))],
        out_specs=[pl.BlockSpec((tp, LANES), lambda i: (0, 0)),
                   pl.BlockSpec((SUBLANES, tp), lambda i: (0, 0))],
        out_shape=[jax.ShapeDtypeStruct((tp, LANES), F32),
                   jax.ShapeDtypeStruct((SUBLANES, tp), F32)],
        compiler_params=_cparams("arbitrary"),
    )(rest, bf_pad)


def _pick_col(blk, head):
    lane = lax.broadcasted_iota(jnp.int32, blk.shape, 1)
    return jnp.sum(jnp.where(lane == head, blk, 0.0), axis=1, keepdims=True)


def _pick_row(blk, head):
    sub = lax.broadcasted_iota(jnp.int32, blk.shape, 0)
    return jnp.sum(jnp.where(sub == head, blk, 0.0), axis=0, keepdims=True)


def attn_fwd(qkv, c, ct, nh):
    tp = qkv.shape[0]
    npair = nh // 2
    nq = tp // ATT_BLOCK
    B = ATT_BLOCK

    def body(q_ref, k_ref, v_ref, c_ref, ct_ref, o_ref, lse_ref):
        p = pl.program_id(0)
        i = pl.program_id(1)
        rows = i * B + lax.broadcasted_iota(jnp.int32, (B, B), 0)
        lane = lax.broadcasted_iota(jnp.int32, (B, B), 1)
        cblk = c_ref[...]
        for hh in range(2):
            head = 2 * p + hh
            hs = slice(hh * HEAD_DIM, (hh + 1) * HEAD_DIM)
            cq = _pick_col(cblk, head)
            q = q_ref[:, hs]

            def kv_step(j, carry, hs=hs, head=head, cq=cq, q=q):
                m, l, acc = carry
                off = pl.multiple_of(j * B, B)
                k = k_ref[pl.ds(off, B), hs]
                v = v_ref[pl.ds(off, B), hs]
                ck = _pick_row(ct_ref[:, pl.ds(off, B)], head)
                s = _dot_nt(q, k) * ATT_SCALE + (cq - ck)
                s = jnp.where(rows >= j * B + lane, s, NEG_BIG)
                m_new = jnp.maximum(m, jnp.max(s, axis=1, keepdims=True))
                alpha = jnp.exp(m - m_new)
                pm = jnp.exp(s - m_new)
                l = alpha * l + jnp.sum(pm, axis=1, keepdims=True)
                acc = alpha * acc + _dot(pm.astype(BF16), v)
                return m_new, l, acc

            init = (jnp.full((B, 1), NEG_BIG, F32), jnp.zeros((B, 1), F32), jnp.zeros((B, HEAD_DIM), F32))
            m, l, acc = lax.fori_loop(0, i + 1, kv_step, init)
            o_ref[:, hs] = acc / l
            lse_ref[:, hs] = jnp.broadcast_to(m + jnp.log(l), (B, HEAD_DIM))

    return pl.pallas_call(
        body, name="attn_fwd", grid=(npair, nq),
        in_specs=[pl.BlockSpec((B, LANES), lambda p, i: (i, p)),
                  pl.BlockSpec((tp, LANES), lambda p, i: (0, npair + p)),
                  pl.BlockSpec((tp, LANES), lambda p, i: (0, 2 * npair + p)),
                  pl.BlockSpec((B, LANES), lambda p, i: (i, 0)),
                  pl.BlockSpec((SUBLANES, tp), lambda p, i: (0, 0))],
        out_specs=[pl.BlockSpec((B, LANES), lambda p, i: (i, p)),
                   pl.BlockSpec((B, LANES), lambda p, i: (i, p))],
        out_shape=[jax.ShapeDtypeStruct((tp, nh * HEAD_DIM), F32),
                   jax.ShapeDtypeStruct((tp, nh * HEAD_DIM), F32)],
        compiler_params=_cparams("parallel", "parallel"),
    )(qkv, qkv, qkv, c, ct)


def _shift_down(x, k, n):
    if k == 0:
        return x
    rows = lax.broadcasted_iota(jnp.int32, x.shape, 0)
    return jnp.where(rows >= k, pltpu.roll(x, k, 0), 0.0)


def _shift_up(x, k, n):
    if k == 0:
        return x
    rows = lax.broadcasted_iota(jnp.int32, x.shape, 0)
    return jnp.where(rows < n - k, pltpu.roll(x, n - k, 0), 0.0)


def _conv_fwd(xr, cw_ref, cb_ref, n):
    xc = cw_ref[CONV_WIDTH - 1:CONV_WIDTH, :] * xr + cb_ref[...]
    for k in range(1, CONV_WIDTH):
        xc = xc + cw_ref[CONV_WIDTH - 1 - k:CONV_WIDTH - k, :] * _shift_down(xr, k, n)
    return xc


def _gates(xc, wga_ref, bga_ref, wgx_ref, bgx_ref, l_ref):
    xcb = xc.astype(BF16)
    r = _sigmoid(_dot(xcb, wga_ref[...]) + bga_ref[...])
    ig = _sigmoid(_dot(xcb, wgx_ref[...]) + bgx_ref[...])
    ls = _log_sigmoid(l_ref[...])
    log_a = RG_C * r * ls
    a = jnp.exp(log_a)
    mult = jnp.sqrt(-_expm1(2.0 * log_a))
    return xcb, r, ig, ls, log_a, a, mult


def _scan_rows(a_s, u_s, out_ref, n, reverse):
    nt = n // SUBLANES
    row = lax.broadcasted_iota(jnp.int32, (SUBLANES, LANES), 0)

    def step(t, carry):
        tt = (nt - 1 - t) if reverse else t
        off = pl.multiple_of(tt * SUBLANES, SUBLANES)
        a = a_s[pl.ds(off, SUBLANES), :]
        u = u_s[pl.ds(off, SUBLANES), :]
        for d in (1, 2, 4):
            if reverse:
                keep = row < SUBLANES - d
                sh = SUBLANES - d
            else:
                keep = row >= d
                sh = d
            a_sh = jnp.where(keep, pltpu.roll(a, sh, 0), 1.0)
            u_sh = jnp.where(keep, pltpu.roll(u, sh, 0), 0.0)
            u = a * u_sh + u
            a = a * a_sh
        h = u + a * carry
        out_ref[pl.ds(off, SUBLANES), :] = h
        return h[0:1, :] if reverse else h[SUBLANES - 1:SUBLANES, :]

    lax.fori_loop(0, nt, step, jnp.zeros((1, LANES), F32))


def rec_fwd(rest, convw, convb, wga, bga, wgx, bgx, lru, rw):
    tp = rest.shape[0]
    ng = rw // LANES

    def body(xr_ref, yr_ref, cw_ref, cb_ref, wga_ref, bga_ref, wgx_ref, bgx_ref, l_ref,
             rec_ref, hr_ref, xc_ref, a_s, u_s):
        xc = _conv_fwd(xr_ref[...], cw_ref, cb_ref, tp)
        xc_ref[...] = xc
        _, r, ig, ls, log_a, a, mult = _gates(xc, wga_ref, bga_ref, wgx_ref, bgx_ref, l_ref)
        a_s[...] = a
        u_s[...] = mult * ig * xc
        _scan_rows(a_s, u_s, hr_ref, tp, reverse=False)
        rec_ref[...] = hr_ref[...] * _gelu(yr_ref[...])

    col = lambda g: (0, g)
    vec = pl.BlockSpec((1, LANES), col)
    big = pl.BlockSpec((tp, LANES), col)
    return pl.pallas_call(
        body, name="rec_fwd", grid=(ng,),
        in_specs=[big, pl.BlockSpec((tp, LANES), lambda g: (0, ng + g)),
                  pl.BlockSpec((CONV_WIDTH, LANES), col), vec,
                  pl.BlockSpec((None, LANES, LANES), lambda g: (g, 0, 0)), vec,
                  pl.BlockSpec((None, LANES, LANES), lambda g: (g, 0, 0)), vec, vec],
        out_specs=[big, big, big],
        out_shape=[jax.ShapeDtypeStruct((tp, rw), F32)] * 3,
        scratch_shapes=[pltpu.VMEM((tp, LANES), F32), pltpu.VMEM((tp, LANES), F32)],
        compiler_params=_cparams("parallel"),
    )(rest, rest, convw, convb, wga, bga, wgx, bgx, lru)


def out_proj(h, o, rec, ga, gr, wout, g2, tm):
    tp, d = h.shape
    aw, rw = o.shape[1], rec.shape[1]

    def body(h_ref, o_ref, rec_ref, ga_ref, gr_ref, w_ref, g2_ref, h2_ref, mix_ref, z2_ref):
        mix_ref[:, 0:aw] = _rms_fwd(o_ref[...], ga_ref[...]).astype(BF16)
        mix_ref[:, aw:aw + rw] = _rms_fwd(rec_ref[...], gr_ref[...]).astype(BF16)
        h2 = h_ref[...] + _dot(mix_ref[...], w_ref[...])
        h2_ref[...] = h2
        z2_ref[...] = _rms_fwd(h2, g2_ref[...]).astype(BF16)

    row = lambda i: (i, 0)
    fix = lambda i: (0, 0)
    return pl.pallas_call(
        body, name="out_proj", grid=(tp // tm,),
        in_specs=[pl.BlockSpec((tm, d), row), pl.BlockSpec((tm, aw), row), pl.BlockSpec((tm, rw), row),
                  pl.BlockSpec((1, aw), fix), pl.BlockSpec((1, rw), fix),
                  pl.BlockSpec((d, d), fix), pl.BlockSpec((1, d), fix)],
        out_specs=[pl.BlockSpec((tm, d), row)] * 3,
        out_shape=[jax.ShapeDtypeStruct((tp, d), F32), jax.ShapeDtypeStruct((tp, d), BF16),
                   jax.ShapeDtypeStruct((tp, d), BF16)],
        compiler_params=_cparams("parallel"),
    )(h, o, rec, ga, gr, wout, g2)


def mlp_fwd(z2, h2, gup, gdown, layer, tm):
    tp, d = h2.shape
    nf = gup.shape[0]
    tf = gup.shape[3]

    def body(z_ref, h_ref, wu_ref, wd_ref, u_ref, h3_ref, acc):
        j = pl.program_id(1)
        u = jnp.maximum(_dot(z_ref[...], wu_ref[...]), 0.0)
        u_ref[...] = u.astype(BF16)
        part = _dot((u * u).astype(BF16), wd_ref[...])

        @pl.when(j == 0)
        def _():
            acc[...] = h_ref[...] + part

        @pl.when(j > 0)
        def _():
            acc[...] += part

        @pl.when(j == nf - 1)
        def _():
            h3_ref[...] = acc[...]

    return pl.pallas_call(
        body, name="mlp_fwd", grid=(tp // tm, nf),
        in_specs=[pl.BlockSpec((tm, d), lambda i, j: (i, 0)),
                  pl.BlockSpec((tm, d), lambda i, j: (i, 0)),
                  pl.BlockSpec((None, None, d, tf), lambda i, j: (j, layer, 0, 0)),
                  pl.BlockSpec((None, None, tf, d), lambda i, j: (j, layer, 0, 0))],
        out_specs=[pl.BlockSpec((tm, tf), lambda i, j: (i, j)),
                   pl.BlockSpec((tm, d), lambda i, j: (i, 0))],
        out_shape=[jax.ShapeDtypeStruct((tp, nf * tf), BF16), jax.ShapeDtypeStruct((tp, d), F32)],
        scratch_shapes=[pltpu.VMEM((tm, d), F32)],
        compiler_params=_cparams("parallel", "arbitrary"),
    )(z2, h2, gup, gdown)


def loss_head(h, gf, tgt, t_real, tm):
    tp, d = h.shape

    def body(h_ref, g_ref, t_ref, dh_ref, dg_ref, loss_ref):
        i = pl.program_id(0)
        x = h_ref[...]
        g = g_ref[...]
        r = lax.rsqrt(jnp.mean(x * x, axis=-1, keepdims=True) + NORM_EPS)
        xn = x * r
        rows = i * tm + lax.broadcasted_iota(jnp.int32, (tm, 1), 0)
        valid = jnp.logical_and(rows >= N_META, rows < t_real)
        e = jnp.where(valid, xn * g - t_ref[...], 0.0)
        part = 0.5 * jnp.sum(jnp.sum(e * e, axis=1, keepdims=True) / d, axis=0, keepdims=True)
        dy = e / d
        dxn = dy * g
        dh_ref[...] = r * (dxn - xn * jnp.mean(dxn * xn, axis=-1, keepdims=True))
        _accumulate(dg_ref, jnp.sum(dy * xn, axis=0, keepdims=True), i == 0)
        _accumulate(loss_ref, jnp.broadcast_to(part, (1, LANES)), i == 0)

    row = lambda i: (i, 0)
    fix = lambda i: (0, 0)
    return pl.pallas_call(
        body, name="loss_head", grid=(tp // tm,),
        in_specs=[pl.BlockSpec((tm, d), row), pl.BlockSpec((1, d), fix), pl.BlockSpec((tm, d), row)],
        out_specs=[pl.BlockSpec((tm, d), row), pl.BlockSpec((1, d), fix), pl.BlockSpec((1, LANES), fix)],
        out_shape=[jax.ShapeDtypeStruct((tp, d), F32), jax.ShapeDtypeStruct((1, d), F32),
                   jax.ShapeDtypeStruct((1, LANES), F32)],
        compiler_params=_cparams("arbitrary"),
    )(h, gf, tgt)


def mlp_bwd(dh, u, h2, g2, gup, gdown, layer, tm):
    tp, d = dh.shape
    nf = gup.shape[0]
    tf = gup.shape[3]
    ni = tp // tm

    def body(dh_ref, u_ref, h2_ref, g_ref, wu_ref, wd_ref, dup_ref, dh2_ref, dg_ref, acc, dhb):
        i = pl.program_id(0)
        j = pl.program_id(1)

        @pl.when(j == 0)
        def _():
            dhb[...] = dh_ref[...].astype(BF16)

        dup = (_dot_nt(dhb[...], wd_ref[...]) * (2.0 * u_ref[...].astype(F32))).astype(BF16)
        dup_ref[...] = dup
        _accumulate(acc, _dot_nt(dup, wu_ref[...]), j == 0)

        @pl.when(j == nf - 1)
        def _():
            dx, dg = _rms_bwd(h2_ref[...], g_ref[...], acc[...])
            dh2_ref[...] = dh_ref[...] + dx
            _accumulate(dg_ref, dg, i == 0)

    return pl.pallas_call(
        body, name="mlp_bwd", grid=(ni, nf),
        in_specs=[pl.BlockSpec((tm, d), lambda i, j: (i, 0)),
                  pl.BlockSpec((tm, tf), lambda i, j: (i, j)),
                  pl.BlockSpec((tm, d), lambda i, j: (i, 0)),
                  pl.BlockSpec((1, d), lambda i, j: (0, 0)),
                  pl.BlockSpec((None, None, d, tf), lambda i, j: (j, layer, 0, 0)),
                  pl.BlockSpec((None, None, tf, d), lambda i, j: (j, layer, 0, 0))],
        out_specs=[pl.BlockSpec((tm, tf), lambda i, j: (i, j)),
                   pl.BlockSpec((tm, d), lambda i, j: (i, 0)),
                   pl.BlockSpec((1, d), lambda i, j: (0, 0))],
        out_shape=[jax.ShapeDtypeStruct((tp, nf * tf), BF16), jax.ShapeDtypeStruct((tp, d), F32),
                   jax.ShapeDtypeStruct((1, d), F32)],
        scratch_shapes=[pltpu.VMEM((tm, d), F32), pltpu.VMEM((tm, d), BF16)],
        compiler_params=_cparams("arbitrary", "arbitrary"),
    )(dh, u, h2, g2, gup, gdown)


def mm_tn(a, b, *, tk, tn, tr, out_dtype, name, square_a=False, blocked_n=False):
    rows, kk = a.shape
    nn = b.shape[1]
    nr = rows // tr

    def body(a_ref, b_ref, o_ref, acc):
        r = pl.program_id(2)
        av = a_ref[...]
        if square_a:
            af = av.astype(F32)
            av = (af * af).astype(BF16)
        prod = _dot_tn(av.astype(BF16), b_ref[...].astype(BF16))
        _accumulate(acc, prod, r == 0)

        @pl.when(r == nr - 1)
        def _():
            o_ref[...] = acc[...].astype(out_dtype)

    if blocked_n:
        out_spec = pl.BlockSpec((None, tk, tn), lambda k, n, r: (n, k, 0))
        out_shape = jax.ShapeDtypeStruct((nn // tn, kk, tn), out_dtype)
    else:
        out_spec = pl.BlockSpec((tk, tn), lambda k, n, r: (k, n))
        out_shape = jax.ShapeDtypeStruct((kk, nn), out_dtype)
    return pl.pallas_call(
        body, name=name, grid=(kk // tk, nn // tn, nr),
        in_specs=[pl.BlockSpec((tr, tk), lambda k, n, r: (r, k)),
                  pl.BlockSpec((tr, tn), lambda k, n, r: (r, n))],
        out_specs=out_spec, out_shape=out_shape,
        scratch_shapes=[pltpu.VMEM((tk, tn), F32)],
        compiler_params=_cparams("parallel", "parallel", "arbitrary"),
    )(a, b)


def out_proj_bwd(dh2, o, rec, ga, gr, wout, tm):
    tp, d = dh2.shape
    aw, rw = o.shape[1], rec.shape[1]

    def body(dh_ref, o_ref, rec_ref, ga_ref, gr_ref, w_ref, do_ref, drec_ref, dga_ref, dgr_ref):
        i = pl.program_id(0)
        dmix = _dot_nt(dh_ref[...].astype(BF16), w_ref[...])
        do, dga = _rms_bwd(o_ref[...], ga_ref[...], dmix[:, 0:aw])
        drec, dgr = _rms_bwd(rec_ref[...], gr_ref[...], dmix[:, aw:aw + rw])
        do_ref[...] = do
        drec_ref[...] = drec
        _accumulate(dga_ref, dga, i == 0)
        _accumulate(dgr_ref, dgr, i == 0)

    row = lambda i: (i, 0)
    fix = lambda i: (0, 0)
    return pl.pallas_call(
        body, name="out_proj_bwd", grid=(tp // tm,),
        in_specs=[pl.BlockSpec((tm, d), row), pl.BlockSpec((tm, aw), row), pl.BlockSpec((tm, rw), row),
                  pl.BlockSpec((1, aw), fix), pl.BlockSpec((1, rw), fix), pl.BlockSpec((d, d), fix)],
        out_specs=[pl.BlockSpec((tm, aw), row), pl.BlockSpec((tm, rw), row),
                   pl.BlockSpec((1, aw), fix), pl.BlockSpec((1, rw), fix)],
        out_shape=[jax.ShapeDtypeStruct((tp, aw), F32), jax.ShapeDtypeStruct((tp, rw), F32),
                   jax.ShapeDtypeStruct((1, aw), F32), jax.ShapeDtypeStruct((1, rw), F32)],
        compiler_params=_cparams("arbitrary"),
    )(dh2, o, rec, ga, gr, wout)


def rec_bwd(drec, hr, xc, rest, convw, convb, wga, bga, wgx, bgx, lru, rw):
    tp = rest.shape[0]
    ng = rw // LANES

    def body(drec_ref, hr_ref, xc_ref, xr_ref, yr_ref, cw_ref, cb_ref, wga_ref, bga_ref, wgx_ref, bgx_ref, l_ref,
             dxr_ref, dyr_ref, dwga_ref, dwgx_ref, vec_ref, a_s, u_s, lam_s):
        xc = xc_ref[...]
        h = hr_ref[...]
        yr = yr_ref[...]
        drec = drec_ref[...]
        xcb, r, ig, ls, log_a, a, mult = _gates(xc, wga_ref, bga_ref, wgx_ref, bgx_ref, l_ref)
        dyr_ref[...] = (drec * h * _gelu_grad(yr)).astype(BF16)
        a_s[...] = _shift_up(a, 1, tp)
        u_s[...] = drec * _gelu(yr)
        _scan_rows(a_s, u_s, lam_s, tp, reverse=True)
        lam = lam_s[...]
        da = lam * _shift_down(h, 1, tp)
        dmult = lam * ig * xc
        dig = lam * mult * xc
        dxc = lam * mult * ig
        a2 = jnp.exp(2.0 * log_a)
        dlog_a = da * a - dmult * a2 / mult
        dr = dlog_a * (RG_C * ls)
        dl = jnp.sum(dlog_a * (RG_C * r), axis=0, keepdims=True) * _sigmoid(-l_ref[...])
        dpa = dr * r * (1.0 - r)
        dpx = dig * ig * (1.0 - ig)
        dpab = dpa.astype(BF16)
        dpxb = dpx.astype(BF16)
        dxc = dxc + _dot_nt(dpab, wga_ref[...]) + _dot_nt(dpxb, wgx_ref[...])
        dwga_ref[...] = _dot_tn(xcb, dpab)
        dwgx_ref[...] = _dot_tn(xcb, dpxb)
        xr = xr_ref[...]
        dxr = cw_ref[CONV_WIDTH - 1:CONV_WIDTH, :] * dxc
        for k in range(1, CONV_WIDTH):
            dxr = dxr + cw_ref[CONV_WIDTH - 1 - k:CONV_WIDTH - k, :] * _shift_up(dxc, k, tp)
        dxr_ref[...] = dxr.astype(BF16)
        for k in range(CONV_WIDTH):
            vec_ref[k:k + 1, :] = jnp.sum(dxc * _shift_down(xr, CONV_WIDTH - 1 - k, tp), axis=0, keepdims=True)
        vec_ref[4:5, :] = jnp.sum(dxc, axis=0, keepdims=True)
        vec_ref[5:6, :] = jnp.sum(dpa, axis=0, keepdims=True)
        vec_ref[6:7, :] = jnp.sum(dpx, axis=0, keepdims=True)
        vec_ref[7:8, :] = dl

    col = lambda g: (0, g)
    vec = pl.BlockSpec((1, LANES), col)
    big = pl.BlockSpec((tp, LANES), col)
    sq = pl.BlockSpec((None, LANES, LANES), lambda g: (g, 0, 0))
    return pl.pallas_call(
        body, name="rec_bwd", grid=(ng,),
        in_specs=[big, big, big, big, pl.BlockSpec((tp, LANES), lambda g: (0, ng + g)),
                  pl.BlockSpec((CONV_WIDTH, LANES), col), vec, sq, vec, sq, vec, vec],
        out_specs=[big, big, sq, sq, pl.BlockSpec((None, SUBLANES, LANES), lambda g: (g, 0, 0))],
        out_shape=[jax.ShapeDtypeStruct((tp, rw), BF16), jax.ShapeDtypeStruct((tp, rw), BF16),
                   jax.ShapeDtypeStruct((ng, LANES, LANES), F32), jax.ShapeDtypeStruct((ng, LANES, LANES), F32),
                   jax.ShapeDtypeStruct((ng, SUBLANES, LANES), F32)],
        scratch_shapes=[pltpu.VMEM((tp, LANES), F32)] * 3,
        compiler_params=_cparams("parallel"),
    )(drec, hr, xc, rest, rest, convw, convb, wga, bga, wgx, bgx, lru)


def attn_bwd(qkv, do, o, lse, c, ct, nh):
    tp = qkv.shape[0]
    npair = nh // 2
    nb = tp // ATT_BLOCK
    B = ATT_BLOCK
    aw = nh * HEAD_DIM

    def body(q_ref, k_ref, v_ref, do_ref, o_ref, lse_ref, c_ref, ct_ref,
             dq_ref, dk_ref, dv_ref, dct_ref, drs_ref, dq_acc, drs_acc):
        p = pl.program_id(0)
        j = pl.program_id(1)
        lane = lax.broadcasted_iota(jnp.int32, (B, B), 1)
        sub = lax.broadcasted_iota(jnp.int32, (B, B), 0)
        cols = j * B + lane

        @pl.when(j == 0)
        def _():
            dq_acc[...] = jnp.zeros_like(dq_acc)
            drs_acc[...] = jnp.zeros_like(drs_acc)

        dct_ref[...] = jnp.zeros_like(dct_ref)
        for hh in range(2):
            head = 2 * p + hh
            hs = slice(hh * HEAD_DIM, (hh + 1) * HEAD_DIM)
            k = k_ref[:, hs]
            v = v_ref[:, hs]
            ck = _pick_row(ct_ref[...], head)

            def q_step(i, carry, hs=hs, head=head, k=k, v=v, ck=ck, hh=hh):
                dk, dv, dcs = carry
                off = pl.multiple_of(i * B, B)
                q = q_ref[pl.ds(off, B), hs]
                dob = do_ref[pl.ds(off, B), hs]
                ob = o_ref[pl.ds(off, B), hs]
                lse_i = lse_ref[pl.ds(off, B), hh * HEAD_DIM:hh * HEAD_DIM + 1]
                cq = _pick_col(c_ref[pl.ds(off, B), :], head)
                s = _dot_nt(q, k) * ATT_SCALE + (cq - ck)
                pm = jnp.where(i * B + sub >= cols, jnp.exp(s - lse_i), 0.0)
                dob16 = dob.astype(BF16)
                dv = dv + _dot_tn(pm.astype(BF16), dob16)
                dp = _dot_nt(dob16, v)
                delta = jnp.sum(dob * ob, axis=1, keepdims=True)
                ds = pm * (dp - delta)
                dcs = dcs - jnp.sum(ds, axis=0, keepdims=True)
                drs_acc[pl.ds(off, B), hs] += jnp.broadcast_to(jnp.sum(ds, axis=1, keepdims=True), (B, HEAD_DIM))
                dsb = ds.astype(BF16)
                dk = dk + _dot_tn(dsb, q) * ATT_SCALE
                dq_acc[pl.ds(off, B), hs] += _dot(dsb, k) * ATT_SCALE
                return dk, dv, dcs

            init = (jnp.zeros((B, HEAD_DIM), F32), jnp.zeros((B, HEAD_DIM), F32), jnp.zeros((1, B), F32))
            dk, dv, dcs = lax.fori_loop(j, nb, q_step, init)
            dk_ref[:, hs] = dk.astype(BF16)
            dv_ref[:, hs] = dv.astype(BF16)
            dct_ref[hh:hh + 1, :] = dcs

        @pl.when(j == nb - 1)
        def _():
            dq_ref[...] = dq_acc[...].astype(BF16)
            drs_ref[...] = drs_acc[...]

    full = lambda p, j: (0, p)
    return pl.pallas_call(
        body, name="attn_bwd", grid=(npair, nb),
        in_specs=[pl.BlockSpec((tp, LANES), full),
                  pl.BlockSpec((B, LANES), lambda p, j: (j, npair + p)),
                  pl.BlockSpec((B, LANES), lambda p, j: (j, 2 * npair + p)),
                  pl.BlockSpec((tp, LANES), full),
                  pl.BlockSpec((tp, LANES), full),
                  pl.BlockSpec((tp, LANES), full),
                  pl.BlockSpec((tp, LANES), lambda p, j: (0, 0)),
                  pl.BlockSpec((SUBLANES, B), lambda p, j: (0, j))],
        out_specs=[pl.BlockSpec((tp, LANES), full),
                   pl.BlockSpec((B, LANES), lambda p, j: (j, p)),
                   pl.BlockSpec((B, LANES), lambda p, j: (j, p)),
                   pl.BlockSpec((None, SUBLANES, B), lambda p, j: (p, 0, j)),
                   pl.BlockSpec((tp, LANES), full)],
        out_shape=[jax.ShapeDtypeStruct((tp, aw), BF16), jax.ShapeDtypeStruct((tp, aw), BF16),
                   jax.ShapeDtypeStruct((tp, aw), BF16),
                   jax.ShapeDtypeStruct((npair, SUBLANES, tp), F32),
                   jax.ShapeDtypeStruct((tp, aw), F32)],
        scratch_shapes=[pltpu.VMEM((tp, LANES), F32), pltpu.VMEM((tp, LANES), F32)],
        compiler_params=_cparams("parallel", "arbitrary"),
    )(qkv, qkv, qkv, do, o, lse, c, ct)


def fgate_bwd(dct8, drs, rest, bf_pad, fcol):
    tp = rest.shape[0]
    aw = drs.shape[1]
    nb = tp // ATT_BLOCK
    B = ATT_BLOCK

    def body(d_ref, drs_ref, f_ref, b_ref, dfl_ref, db_ref, pad_s):
        r_i = lax.broadcasted_iota(jnp.int32, (B, B), 0)
        c_i = lax.broadcasted_iota(jnp.int32, (B, B), 1)
        triu = (c_i >= r_i).astype(BF16)
        sel = (lax.broadcasted_iota(jnp.int32, (aw, LANES), 0)
               == HEAD_DIM * lax.broadcasted_iota(jnp.int32, (aw, LANES), 1)).astype(BF16)
        carry = jnp.zeros((1, LANES), F32)
        db = jnp.zeros((1, LANES), F32)
        pad_s[...] = jnp.zeros_like(pad_s)
        for i in range(nb - 1, -1, -1):
            sl = slice(i * B, (i + 1) * B)
            pad_s[0:SUBLANES, :] = d_ref[:, sl]
            dc = pad_s[...].T + _dot_split3(drs_ref[sl, :], sel)
            rc = _split3_dot(triu, dc)
            dlf = rc + carry
            carry = carry + rc[0:1, :]
            dfl = dlf * _sigmoid(-(f_ref[sl, :] + b_ref[...]))
            dfl_ref[sl, :] = dfl.astype(BF16)
            db = db + jnp.sum(dfl, axis=0, keepdims=True)
        db_ref[...] = db

    return pl.pallas_call(
        body, name="fgate_bwd", grid=(1,),
        in_specs=[pl.BlockSpec((SUBLANES, tp), lambda i: (0, 0)),
                  pl.BlockSpec((tp, aw), lambda i: (0, 0)),
                  pl.BlockSpec((tp, LANES), lambda i: (0, fcol)),
                  pl.BlockSpec((1, LANES), lambda i: (0, 0))],
        out_specs=[pl.BlockSpec((tp, LANES), lambda i: (0, 0)),
                   pl.BlockSpec((1, LANES), lambda i: (0, 0))],
        out_shape=[jax.ShapeDtypeStruct((tp, LANES), BF16), jax.ShapeDtypeStruct((1, LANES), F32)],
        scratch_shapes=[pltpu.VMEM((B, B), F32)],
        compiler_params=_cparams("arbitrary"),
    )(dct8, drs, rest, bf_pad)


def in_proj_bwd(dh2, parts, wqkv, wrest, h, g1, tm):
    tp, d = h.shape
    dq, dk, dv, dxr, dyr, dfl = parts
    aw, rw = dq.shape[1], dxr.shape[1]

    def body(dh2_ref, dq_ref, dk_ref, dv_ref, dxr_ref, dyr_ref, dfl_ref, wq_ref, wr_ref, h_ref, g_ref,
             dh_ref, dg_ref):
        i = pl.program_id(0)
        dz = _dot_nt(dq_ref[...], wq_ref[:, 0:aw])
        dz += _dot_nt(dk_ref[...], wq_ref[:, aw:2 * aw])
        dz += _dot_nt(dv_ref[...], wq_ref[:, 2 * aw:3 * aw])
        dz += _dot_nt(dxr_ref[...], wr_ref[:, 0:rw])
        dz += _dot_nt(dyr_ref[...], wr_ref[:, rw:2 * rw])
        dz += _dot_nt(dfl_ref[...], wr_ref[:, 2 * rw:2 * rw + LANES])
        dx, dg = _rms_bwd(h_ref[...], g_ref[...], dz)
        dh_ref[...] = dh2_ref[...] + dx
        _accumulate(dg_ref, dg, i == 0)

    row = lambda i: (i, 0)
    fix = lambda i: (0, 0)
    return pl.pallas_call(
        body, name="in_proj_bwd", grid=(tp // tm,),
        in_specs=[pl.BlockSpec((tm, d), row),
                  pl.BlockSpec((tm, aw), row), pl.BlockSpec((tm, aw), row), pl.BlockSpec((tm, aw), row),
                  pl.BlockSpec((tm, rw), row), pl.BlockSpec((tm, rw), row), pl.BlockSpec((tm, LANES), row),
                  pl.BlockSpec(wqkv.shape, fix), pl.BlockSpec(wrest.shape, fix),
                  pl.BlockSpec((tm, d), row), pl.BlockSpec((1, d), fix)],
        out_specs=[pl.BlockSpec((tm, d), row), pl.BlockSpec((1, d), fix)],
        out_shape=[jax.ShapeDtypeStruct((tp, d), F32), jax.ShapeDtypeStruct((1, d), F32)],
        compiler_params=_cparams("arbitrary"),
    )(dh2, dq, dk, dv, dxr, dyr, dfl, wqkv, wrest, h, g1)


def _place():
    return lax.axis_index("x"), lax.axis_index("y"), lax.axis_index("c")


def all_gather(arrs, name):
    n = len(arrs)

    def body(*refs):
        ins, outs = refs[:n], refs[n:2 * n]
        send_sems, recv_sems, local_sems = refs[2 * n:]
        x, y, c = _place()
        me, sibling = (x, y, c), (x, y, 1 - c)
        chips = [(1 - x, y), (x, 1 - y), (1 - x, 1 - y)]

        def slot(a, blk):
            return outs[a].at[4 * blk[0] + 2 * blk[1] + blk[2]]

        def copy(a, k, blk, to, src=None):
            return pltpu.make_async_remote_copy(
                src_ref=slot(a, blk) if src is None else src, dst_ref=slot(a, blk),
                send_sem=send_sems.at[a, k], recv_sem=recv_sems.at[a, k],
                device_id=to, device_id_type=MESH)

        mine = [pltpu.make_async_copy(ins[a], slot(a, me), local_sems.at[a]) for a in range(n)]
        for cp in mine:
            cp.start()
        first = []
        for a in range(n):
            first.append(copy(a, 0, me, sibling, src=ins[a]))
            first += [copy(a, 1 + j, me, (*chip, c), src=ins[a]) for j, chip in enumerate(chips)]
        for cp in first:
            cp.start()
        passed = []
        for j, chip in enumerate(chips):
            for a in range(n):
                copy(a, 1 + j, (*chip, c), me).wait_recv()
                fwd = copy(a, 4 + j, (*chip, c), sibling)
                fwd.start()
                passed.append(fwd)
        for a in range(n):
            copy(a, 0, sibling, me).wait_recv()
            for j, chip in enumerate(chips):
                copy(a, 4 + j, (*chip, 1 - c), me).wait_recv()
        for cp in first + passed:
            cp.wait_send()
        for cp in mine:
            cp.wait()

    return pl.pallas_call(
        body, name=name,
        in_specs=[ANY] * n, out_specs=[ANY] * n,
        out_shape=[jax.ShapeDtypeStruct((N_DEV,) + a.shape, a.dtype) for a in arrs],
        scratch_shapes=[pltpu.SemaphoreType.DMA((n, 7)), pltpu.SemaphoreType.DMA((n, 7)),
                        pltpu.SemaphoreType.DMA((n,))],
    )(*arrs)


def grad_exchange(per_layer, name):
    nw = len(per_layer)
    nl = len(per_layer[0])
    flat = [a for w in per_layer for a in w]
    n = len(flat)

    def body(*refs):
        ins, outs = refs[:n], refs[n:n + nw]
        send_sems, recv_sems, local_sems = refs[n + nw:]
        x, y, c = _place()
        me = 4 * x + 2 * y + c
        rel = [(dx, dy, dc) for dx in (0, 1) for dy in (0, 1) for dc in (0, 1) if dx + dy + dc]
        local = []
        for w in range(nw):
            for l in range(nl):
                cp = pltpu.make_async_copy(ins[w * nl + l].at[me], outs[w].at[me, l], local_sems.at[w * nl + l])
                cp.start()
                local.append(cp)
        sends, recvs = [], []
        for k, (dx, dy, dc) in enumerate(rel):
            px = (1 - x) if dx else x
            py = (1 - y) if dy else y
            pc = (1 - c) if dc else c
            peer = 4 * px + 2 * py + pc
            for w in range(nw):
                for l in range(nl):
                    a = w * nl + l
                    sends.append(pltpu.make_async_remote_copy(
                        src_ref=ins[a].at[peer], dst_ref=outs[w].at[me, l],
                        send_sem=send_sems.at[a, k], recv_sem=recv_sems.at[a, k],
                        device_id=(px, py, pc), device_id_type=MESH))
                    recvs.append(pltpu.make_async_remote_copy(
                        src_ref=ins[a].at[peer], dst_ref=outs[w].at[peer, l],
                        send_sem=send_sems.at[a, k], recv_sem=recv_sems.at[a, k],
                        device_id=(px, py, pc), device_id_type=MESH))
        for cp in sends:
            cp.start()
        for cp in recvs:
            cp.wait_recv()
        for cp in sends:
            cp.wait_send()
        for cp in local:
            cp.wait()

    return pl.pallas_call(
        body, name=name,
        in_specs=[ANY] * n, out_specs=[ANY] * nw,
        out_shape=[jax.ShapeDtypeStruct((N_DEV, nl) + w[0].shape[1:], w[0].dtype) for w in per_layer],
        scratch_shapes=[pltpu.SemaphoreType.DMA((n, 7)), pltpu.SemaphoreType.DMA((n, 7)),
                        pltpu.SemaphoreType.DMA((n,))],
    )(*flat)


def _adamw_math(g, w, m, v):
    m = ADAM_B1 * m + (1.0 - ADAM_B1) * g
    v = ADAM_B2 * v + (1.0 - ADAM_B2) * (g * g)
    m_hat = m / (1.0 - ADAM_B1 ** ADAM_STEP)
    v_hat = v / (1.0 - ADAM_B2 ** ADAM_STEP)
    delta = -ADAM_LR * (m_hat / (jnp.sqrt(v_hat) + ADAM_EPS) + ADAM_WD * w)
    return delta, m, v


def sum_adamw(parts, w, m, v, tr, name):
    npart, rows, cols = parts.shape

    def body(p_ref, w_ref, m_ref, v_ref, g_ref, d_ref, nm_ref, nv_ref):
        g = p_ref[0].astype(F32)
        for p in range(1, npart):
            g = g + p_ref[p].astype(F32)
        delta, nm, nv = _adamw_math(g, w_ref[...], m_ref[...], v_ref[...])
        g_ref[...] = g
        d_ref[...] = delta
        nm_ref[...] = nm
        nv_ref[...] = nv

    blk = pl.BlockSpec((tr, cols), lambda i: (i, 0))
    return pl.pallas_call(
        body, name=name, grid=(rows // tr,),
        in_specs=[pl.BlockSpec((npart, tr, cols), lambda i: (0, i, 0)), blk, blk, blk],
        out_specs=[blk] * 4,
        out_shape=[jax.ShapeDtypeStruct((rows, cols), F32)] * 4,
        compiler_params=_cparams("parallel"),
    )(parts, w, m, v)


def sum_parts(parts, name):
    npart, rows, cols = parts.shape

    def body(p_ref, g_ref):
        g = p_ref[0]
        for p in range(1, npart):
            g = g + p_ref[p]
        g_ref[...] = g

    return pl.pallas_call(
        body, name=name, grid=(1,),
        in_specs=[pl.BlockSpec((npart, rows, cols), lambda i: (0, 0, 0))],
        out_specs=pl.BlockSpec((rows, cols), lambda i: (0, 0)),
        out_shape=jax.ShapeDtypeStruct((rows, cols), F32),
        compiler_params=_cparams("arbitrary"),
    )(parts)


def _round_up(n, m):
    return (n + m - 1) // m * m


def _block_diag_pairs(w):
    nb, b, _ = w.shape
    per = LANES // b
    ng = nb // per
    w = w.reshape(ng, per, b, b)
    eye = jnp.eye(per, dtype=w.dtype)
    out = jnp.einsum('gpij,pq->gpiqj', w, eye).reshape(ng, LANES, LANES)
    return out.astype(BF16)


def _block_diag_extract(g, b):
    ng = g.shape[0]
    per = LANES // b
    g = g.reshape(ng, per, b, per, b)
    idx = jnp.arange(per)
    return g[:, idx, :, idx, :].transpose(1, 0, 2, 3).reshape(ng * per, b, b)


def _tiles(v):
    v = v.reshape(-1)
    n = _round_up(v.shape[0], SUBLANES * LANES)
    return jnp.pad(v, (0, n - v.shape[0])).reshape(-1, LANES)


SMALL = ['attn_norm_g', 'b_f', 'conv_w', 'conv_b', 'w_gate_a', 'b_gate_a', 'w_gate_x', 'b_gate_x',
         'lru_L', 'attn_out_g', 'rec_out_g', 'mlp_norm_g', 'final_g', 'meta']


def _pack(d):
    return jnp.concatenate([_tiles(d[n]) for n in SMALL], axis=0)


def _unpack(vec, shapes):
    out, r = {}, 0
    for n in SMALL:
        size = math.prod(shapes[n])
        nr = _round_up(size, SUBLANES * LANES) // LANES
        out[n] = vec[r:r + nr].reshape(-1)[:size].reshape(shapes[n])
        r += nr
    return out


def _row_tile(tp):
    return tp // 4 if (tp // 4) % 16 == 0 else tp


def local_step(x, tgt, meta, small, wqkv, wrest, wout, gup, gdown):
    s, d = x.shape
    t_real = s + N_META
    tp = _round_up(t_real, ATT_BLOCK)
    depth = wqkv.shape[0]
    aw = wqkv.shape[2] // 3
    rw = (wrest.shape[2] - LANES) // 2
    nh = aw // HEAD_DIM
    nblk, blk = small['w_gate_a'].shape[1], small['w_gate_a'].shape[2]
    tm = _row_tile(tp)
    tm2 = tp // 2
    fcol = 2 * rw // LANES
    tf = gup.shape[3]
    dff = gup.shape[0] * tf

    h = jnp.concatenate([meta, x, jnp.zeros((tp - t_real, d), F32)], axis=0)
    tgt_p = jnp.pad(tgt, ((N_META, tp - t_real), (0, 0)))
    row = lambda v: v.reshape(1, -1)
    bf_pad = jnp.pad(small['b_f'], ((0, 0), (0, LANES - nh)))

    saved = []
    for l in range(depth):
        wga = _block_diag_pairs(small['w_gate_a'][l])
        wgx = _block_diag_pairs(small['w_gate_x'][l])
        z, qkv, rest = in_proj(h, row(small['attn_norm_g'][l]), wqkv[l], wrest[l], tm)
        c, ct = fgate_fwd(rest, bf_pad[l:l + 1], fcol)
        o, lse = attn_fwd(qkv, c, ct, nh)
        rec, hr, xc = rec_fwd(rest, small['conv_w'][l], row(small['conv_b'][l]), wga, row(small['b_gate_a'][l]),
                              wgx, row(small['b_gate_x'][l]), row(small['lru_L'][l]), rw)
        h2, mix, z2 = out_proj(h, o, rec, row(small['attn_out_g'][l]), row(small['rec_out_g'][l]), wout[l],
                               row(small['mlp_norm_g'][l]), tm)
        u, h3 = mlp_fwd(z2, h2, gup, gdown, l, tm2)
        saved.append(dict(h=h, z=z, qkv=qkv, rest=rest, c=c, ct=ct, o=o, lse=lse, rec=rec, hr=hr, xc=xc,
                          h2=h2, mix=mix, z2=z2, u=u, wga=wga, wgx=wgx))
        h = h3

    dh, dgf, loss = loss_head(h, row(small['final_g']), tgt_p, t_real, tm)

    gs = {n: [None] * depth for n in SMALL if n not in ('final_g', 'meta')}
    big = {n: [None] * depth for n in ('w_in', 'w_out', 'w_up', 'w_down')}
    tk_in = d
    for l in reversed(range(depth)):
        sv = saved[l]
        dup, dh2, dg2 = mlp_bwd(dh, sv['u'], sv['h2'], row(small['mlp_norm_g'][l]), gup, gdown, l, tm2)
        big['w_down'][l] = mm_tn(sv['u'], dh, tk=tf, tn=d, tr=tm, out_dtype=BF16, name="dw_down",
                                 square_a=True).reshape(N_DEV, tf, d)
        big['w_up'][l] = mm_tn(sv['z2'], dup, tk=d, tn=tf, tr=tm, out_dtype=BF16, name="dw_up", blocked_n=True)
        gs['mlp_norm_g'][l] = dg2[0]
        do, drec, dga, dgr = out_proj_bwd(dh2, sv['o'], sv['rec'], row(small['attn_out_g'][l]),
                                          row(small['rec_out_g'][l]), wout[l], tm)
        big['w_out'][l] = mm_tn(sv['mix'], dh2, tk=d, tn=d // 2, tr=tm, out_dtype=BF16,
                                name="dw_out").reshape(N_DEV, d // N_DEV, d)
        gs['attn_out_g'][l] = dga[0]
        gs['rec_out_g'][l] = dgr[0]
        dxr, dyr, dwga, dwgx, vec = rec_bwd(drec, sv['hr'], sv['xc'], sv['rest'], small['conv_w'][l],
                                            row(small['conv_b'][l]), sv['wga'], row(small['b_gate_a'][l]),
                                            sv['wgx'], row(small['b_gate_x'][l]), row(small['lru_L'][l]), rw)
        gs['w_gate_a'][l] = _block_diag_extract(dwga, blk)
        gs['w_gate_x'][l] = _block_diag_extract(dwgx, blk)
        vec = vec.transpose(1, 0, 2).reshape(SUBLANES, rw)
        gs['conv_w'][l] = vec[0:CONV_WIDTH]
        gs['conv_b'][l] = vec[4]
        gs['b_gate_a'][l] = vec[5]
        gs['b_gate_x'][l] = vec[6]
        gs['lru_L'][l] = vec[7]
        dq, dk, dv, dct, drs = attn_bwd(sv['qkv'], do, sv['o'], sv['lse'], sv['c'], sv['ct'], nh)
        dct8 = jnp.pad(dct[:, 0:2, :].reshape(nh, tp), ((0, SUBLANES - nh), (0, 0))) if nh < SUBLANES \
            else dct[:, 0:2, :].reshape(nh, tp)
        dfl, dbf = fgate_bwd(dct8, drs, sv['rest'], bf_pad[l:l + 1], fcol)
        gs['b_f'][l] = dbf[0, 0:nh]
        parts = (dq, dk, dv, dxr, dyr, dfl)
        dh, dg1 = in_proj_bwd(dh2, parts, wqkv[l], wrest[l], sv['h'], row(small['attn_norm_g'][l]), tm)
        gs['attn_norm_g'][l] = dg1[0]
        dws = [mm_tn(sv['z'], pt, tk=tk_in, tn=pt.shape[1], tr=tm, out_dtype=BF16, name="dw_in_%d" % i)
               for i, pt in enumerate(parts)]
        dw_in = jnp.concatenate([dws[0], dws[1], dws[2], dws[5][:, 0:nh], dws[3], dws[4]], axis=1)
        big['w_in'][l] = dw_in.reshape(d, N_DEV, -1).transpose(1, 0, 2)

    grads = {n: jnp.stack(v) for n, v in gs.items()}
    grads['final_g'] = dgf[0]
    grads['meta'] = dh[0:N_META]
    return loss[0, 0], dh, grads, big


def prep_weights(g_in, g_out, nh, rw):
    _, depth, d, _ = g_in.shape
    w_in = g_in.transpose(1, 2, 0, 3).reshape(depth, d, -1)
    aw = nh * HEAD_DIM
    wqkv = w_in[:, :, 0:3 * aw]
    f0 = 3 * aw
    wrest = jnp.concatenate([w_in[:, :, f0 + nh:f0 + nh + 2 * rw], w_in[:, :, f0:f0 + nh],
                             jnp.zeros((depth, d, LANES - nh), w_in.dtype)], axis=2)
    wout = g_out.transpose(1, 0, 2, 3).reshape(depth, d, d)
    return wqkv, wrest, wout


BIG = ['w_in', 'w_out', 'w_up', 'w_down']
WEIGHTS = ['meta', 'attn_norm_g', 'w_in', 'b_f', 'conv_w', 'conv_b', 'w_gate_a', 'b_gate_a', 'w_gate_x', 'b_gate_x',
           'lru_L', 'attn_out_g', 'rec_out_g', 'w_out', 'mlp_norm_g', 'w_up', 'w_down', 'final_g']


def kernel(x, meta, attn_norm_g, w_in, b_f, conv_w, conv_b, w_gate_a, b_gate_a, w_gate_x, b_gate_x, lru_L, attn_out_g, rec_out_g, w_out, mlp_norm_g, w_up, w_down, final_g, loss_target, m_meta, m_attn_norm_g, m_w_in, m_b_f, m_conv_w, m_conv_b, m_w_gate_a, m_b_gate_a, m_w_gate_x, m_b_gate_x, m_lru_L, m_attn_out_g, m_rec_out_g, m_w_out, m_mlp_norm_g, m_w_up, m_w_down, m_final_g, v_meta, v_attn_norm_g, v_w_in, v_b_f, v_conv_w, v_conv_b, v_w_gate_a, v_b_gate_a, v_w_gate_x, v_b_gate_x, v_lru_L, v_attn_out_g, v_rec_out_g, v_w_out, v_mlp_norm_g, v_w_up, v_w_down, v_final_g):
    w = dict(meta=meta, attn_norm_g=attn_norm_g, w_in=w_in, b_f=b_f, conv_w=conv_w, conv_b=conv_b,
             w_gate_a=w_gate_a, b_gate_a=b_gate_a, w_gate_x=w_gate_x, b_gate_x=b_gate_x, lru_L=lru_L,
             attn_out_g=attn_out_g, rec_out_g=rec_out_g, w_out=w_out, mlp_norm_g=mlp_norm_g, w_up=w_up,
             w_down=w_down, final_g=final_g)
    mo = dict(meta=m_meta, attn_norm_g=m_attn_norm_g, w_in=m_w_in, b_f=m_b_f, conv_w=m_conv_w, conv_b=m_conv_b,
              w_gate_a=m_w_gate_a, b_gate_a=m_b_gate_a, w_gate_x=m_w_gate_x, b_gate_x=m_b_gate_x, lru_L=m_lru_L,
              attn_out_g=m_attn_out_g, rec_out_g=m_rec_out_g, w_out=m_w_out, mlp_norm_g=m_mlp_norm_g,
              w_up=m_w_up, w_down=m_w_down, final_g=m_final_g)
    vo = dict(meta=v_meta, attn_norm_g=v_attn_norm_g, w_in=v_w_in, b_f=v_b_f, conv_w=v_conv_w, conv_b=v_conv_b,
              w_gate_a=v_w_gate_a, b_gate_a=v_b_gate_a, w_gate_x=v_w_gate_x, b_gate_x=v_b_gate_x, lru_L=v_lru_L,
              attn_out_g=v_attn_out_g, rec_out_g=v_rec_out_g, w_out=v_w_out, mlp_norm_g=v_mlp_norm_g,
              w_up=v_w_up, w_down=v_w_down, final_g=v_final_g)
    depth = w_in.shape[0]
    nh = b_f.shape[1]
    rw = conv_b.shape[1]
    me = 4 * lax.axis_index("x") + 2 * lax.axis_index("y") + lax.axis_index("c")

    g_in, g_out, g_up, g_down, g_meta, g_conv = all_gather(
        [w_in.astype(BF16), w_out.astype(BF16), w_up.astype(BF16), w_down.astype(BF16), meta, conv_w],
        "gather_weights")
    wqkv, wrest, wout = prep_weights(g_in, g_out, nh, rw)
    meta_full = g_meta.transpose(1, 0, 2).reshape(N_META, -1)
    conv_full = g_conv.transpose(1, 2, 0, 3).reshape(depth, CONV_WIDTH, rw)
    small = {n: w[n] for n in SMALL}
    small['conv_w'] = conv_full

    loss_part, dh0, grads, big = local_step(x[0], loss_target[0], meta_full, small, wqkv, wrest, wout, g_up, g_down)
    loss = lax.psum(loss_part, ("x", "y", "c"))
    grad_x = dh0[N_META:N_META + x.shape[1]][None]

    recv = grad_exchange([big[n] for n in BIG], "exchange_grads")
    out_g, out_d, out_m, out_v = {}, {}, {}, {}
    for n, r in zip(BIG, recv):
        shp = w[n].shape
        rows, cols = shp[0] * shp[1], shp[2]
        tr = 512 if cols <= 512 else 256
        tr = min(tr, rows)
        res = sum_adamw(r.reshape(N_DEV, rows, cols), w[n].reshape(rows, cols), mo[n].reshape(rows, cols),
                        vo[n].reshape(rows, cols), tr, "adamw_" + n)
        out_g[n], out_d[n], out_m[n], out_v[n] = [a.reshape(shp) for a in res]

    (g_small,) = all_gather([_pack(grads)], "gather_small_grads")
    gsum = _unpack(sum_parts(g_small, "sum_small_grads"), {n: grads[n].shape for n in SMALL})
    gsum['meta'] = lax.dynamic_slice_in_dim(gsum['meta'], me * meta.shape[1], meta.shape[1], axis=1)
    gsum['conv_w'] = lax.dynamic_slice_in_dim(gsum['conv_w'], me * conv_w.shape[2], conv_w.shape[2], axis=2)
    packed = [_pack(t) for t in (gsum, {n: w[n] for n in SMALL}, {n: mo[n] for n in SMALL},
                                 {n: vo[n] for n in SMALL})]
    res = sum_adamw(packed[0][None], packed[1], packed[2], packed[3], packed[1].shape[0], "adamw_small")
    shapes = {n: w[n].shape for n in SMALL}
    for dst, vec in zip((out_g, out_d, out_m, out_v), res):
        dst.update(_unpack(vec, shapes))

    return (loss, grad_x, *[out_g[n] for n in WEIGHTS], *[out_d[n] for n in WEIGHTS],
            *[out_m[n] for n in WEIGHTS], *[out_v[n] for n in WEIGHTS])
```

```python
import functools
import math

import jax
import jax.numpy as jnp
from jax import lax
from jax.experimental import pallas as pl
from jax.experimental.pallas import tpu as pltpu

F32 = jnp.float32
BF16 = jnp.bfloat16

N_DEV = 8
N_META = 16
HEAD_DIM = 64
CONV_WIDTH = 4
RG_C = 8.0
NORM_EPS = 1e-6
LANES = 128
SUBLANES = 8
ATT_BLOCK = 128
ATT_TQ = 512
NEG_BIG = -1e30
ATT_SCALE = 1.0 / math.sqrt(HEAD_DIM)

ADAM_LR = 0.001
ADAM_B1 = 0.9
ADAM_B2 = 0.999
ADAM_EPS = 1e-08
ADAM_WD = 0.01
ADAM_STEP = 10

VMEM_LIMIT_BYTES = 56 * 1024 * 1024
MESH = pl.DeviceIdType.MESH
ANY = pl.BlockSpec(memory_space=pl.ANY)


def _cparams(*sem):
    return pltpu.CompilerParams(dimension_semantics=sem if sem else None,
                                vmem_limit_bytes=VMEM_LIMIT_BYTES)


def _dot(a, b):
    return jnp.dot(a, b, preferred_element_type=F32)


def _dot_nt(a, b):
    return lax.dot_general(a, b, (((1,), (1,)), ((), ())), preferred_element_type=F32)


def _dot_tn(a, b):
    return lax.dot_general(a, b, (((0,), (0,)), ((), ())), preferred_element_type=F32)


def _sigmoid(x):
    return 1.0 / (1.0 + jnp.exp(-x))


def _log_sigmoid(x):
    return jnp.minimum(x, 0.0) - jnp.log(1.0 + jnp.exp(-jnp.abs(x)))


def _expm1(x):
    series = x * (1.0 + x * (0.5 + x * (1.0 / 6.0 + x * (1.0 / 24.0))))
    return jnp.where(jnp.abs(x) < 1e-2, series, jnp.exp(x) - 1.0)


_GELU_K = math.sqrt(2.0 / math.pi)
_GELU_C = 0.044715


def _gelu(x):
    t = jnp.tanh(_GELU_K * (x + _GELU_C * x * x * x))
    return 0.5 * x * (1.0 + t)


def _gelu_grad(x):
    t = jnp.tanh(_GELU_K * (x + _GELU_C * x * x * x))
    return 0.5 * (1.0 + t) + 0.5 * x * (1.0 - t * t) * _GELU_K * (1.0 + 3.0 * _GELU_C * x * x)


def _split3_dot(tri, x):
    hi = x.astype(BF16)
    r1 = x - hi.astype(F32)
    mid = r1.astype(BF16)
    lo = (r1 - mid.astype(F32)).astype(BF16)
    return _dot(tri, hi) + _dot(tri, mid) + _dot(tri, lo)


def _dot_split3(x, sel):
    hi = x.astype(BF16)
    r1 = x - hi.astype(F32)
    mid = r1.astype(BF16)
    lo = (r1 - mid.astype(F32)).astype(BF16)
    return _dot(hi, sel) + _dot(mid, sel) + _dot(lo, sel)


def _rms_fwd(x, g):
    r = lax.rsqrt(jnp.mean(x * x, axis=-1, keepdims=True) + NORM_EPS)
    return x * r * g


def _rms_bwd(x, g, dy):
    r = lax.rsqrt(jnp.mean(x * x, axis=-1, keepdims=True) + NORM_EPS)
    xn = x * r
    dxn = dy * g
    dx = r * (dxn - xn * jnp.mean(dxn * xn, axis=-1, keepdims=True))
    return dx, jnp.sum(dy * xn, axis=0, keepdims=True)


def _accumulate(ref, val, first):
    @pl.when(first)
    def _():
        ref[...] = val

    @pl.when(jnp.logical_not(first))
    def _():
        ref[...] += val


def in_proj(h, g1, wqkv, wrest, tm):
    tp, d = h.shape
    nq, nr = wqkv.shape[1], wrest.shape[1]

    def body(h_ref, g_ref, wq_ref, wr_ref, z_ref, qkv_ref, rest_ref):
        z = _rms_fwd(h_ref[...], g_ref[...]).astype(BF16)
        z_ref[...] = z
        qkv_ref[...] = _dot(z, wq_ref[...]).astype(BF16)
        rest_ref[...] = _dot(z, wr_ref[...])

    return pl.pallas_call(
        body, name="in_proj", grid=(tp // tm,),
        in_specs=[pl.BlockSpec((tm, d), lambda i: (i, 0)),
                  pl.BlockSpec((1, d), lambda i: (0, 0)),
                  pl.BlockSpec((d, nq), lambda i: (0, 0)),
                  pl.BlockSpec((d, nr), lambda i: (0, 0))],
        out_specs=[pl.BlockSpec((tm, d), lambda i: (i, 0)),
                   pl.BlockSpec((tm, nq), lambda i: (i, 0)),
                   pl.BlockSpec((tm, nr), lambda i: (i, 0))],
        out_shape=[jax.ShapeDtypeStruct((tp, d), BF16),
                   jax.ShapeDtypeStruct((tp, nq), BF16),
                   jax.ShapeDtypeStruct((tp, nr), F32)],
        compiler_params=_cparams("parallel"),
    )(h, g1, wqkv, wrest)


def fgate_fwd(rest, bf_pad, fcol):
    tp = rest.shape[0]
    nb = tp // ATT_BLOCK

    def body(f_ref, b_ref, c_ref, ct_ref):
        r_i = lax.broadcasted_iota(jnp.int32, (ATT_BLOCK, ATT_BLOCK), 0)
        c_i = lax.broadcasted_iota(jnp.int32, (ATT_BLOCK, ATT_BLOCK), 1)
        tri = (r_i >= c_i).astype(BF16)
        carry = jnp.zeros((1, LANES), F32)
        for i in range(nb):
            sl = slice(i * ATT_BLOCK, (i + 1) * ATT_BLOCK)
            lf = _log_sigmoid(f_ref[sl, :] + b_ref[...])
            cs = _split3_dot(tri, lf) + carry
            carry = cs[ATT_BLOCK - 1:ATT_BLOCK, :]
            c_ref[sl, :] = cs
            ct_ref[:, sl] = cs.T[0:SUBLANES, :]

    return pl.pallas_call(
        body, name="fgate_fwd", grid=(1,),
        in_specs=[pl.BlockSpec((tp, LANES), lambda i: (0, fcol)),
                  pl.BlockSpec((1, LANES), lambda i: (0, 0))],
        out_specs=[pl.BlockSpec((tp, LANES), lambda i: (0, 0)),
                   pl.BlockSpec((SUBLANES, tp), lambda i: (0, 0))],
        out_shape=[jax.ShapeDtypeStruct((tp, LANES), F32),
                   jax.ShapeDtypeStruct((SUBLANES, tp), F32)],
        compiler_params=_cparams("arbitrary"),
    )(rest, bf_pad)


def _pick_col(blk, head):
    lane = lax.broadcasted_iota(jnp.int32, blk.shape, 1)
    return jnp.sum(jnp.where(lane == head, blk, 0.0), axis=1, keepdims=True)


def _pick_row(blk, head):
    sub = lax.broadcasted_iota(jnp.int32, blk.shape, 0)
    return jnp.sum(jnp.where(sub == head, blk, 0.0), axis=0, keepdims=True)


def _att_tiles(tp):
    out, r0 = [], 0
    while r0 < tp:
        rows = min(ATT_TQ, tp - r0)
        out.append((r0, rows, r0 + rows))
        r0 += rows
    return out


def attn_fwd(qkv, c, ct, nh):
    tp = qkv.shape[0]
    npair = nh // 2
    tiles = _att_tiles(tp)

    def body(q_ref, k_ref, v_ref, c_ref, ct_ref, o_ref, lset_ref):
        p = pl.program_id(0)
        lset_ref[...] = jnp.zeros_like(lset_ref)
        for r0, nr, nk in tiles:
            rs = slice(r0, r0 + nr)
            causal = (r0 + lax.broadcasted_iota(jnp.int32, (nr, nk), 0)
                      >= lax.broadcasted_iota(jnp.int32, (nr, nk), 1))
            cblk = c_ref[rs, :]
            ctb = ct_ref[:, 0:nk]
            for hh in range(2):
                head = 2 * p + hh
                hs = slice(hh * HEAD_DIM, (hh + 1) * HEAD_DIM)
                q = q_ref[rs, hs] * ATT_SCALE
                s = _dot_nt(q, k_ref[0:nk, hs]) + (_pick_col(cblk, head) - _pick_row(ctb, head))
                s = jnp.where(causal, s, NEG_BIG)
                m = jnp.max(s, axis=1, keepdims=True)
                pm = jnp.exp(s - m)
                l = jnp.sum(pm, axis=1, keepdims=True)
                o_ref[rs, hs] = _dot(pm.astype(BF16), v_ref[0:nk, hs]) / l
                lse = m + jnp.log(l)
                lset_ref[hh:hh + 1, rs] = jnp.broadcast_to(lse, (nr, LANES)).T[0:1, :]

    pair = lambda p: (0, p)
    return pl.pallas_call(
        body, name="attn_fwd", grid=(npair,),
        in_specs=[pl.BlockSpec((tp, LANES), pair),
                  pl.BlockSpec((tp, LANES), lambda p: (0, npair + p)),
                  pl.BlockSpec((tp, LANES), lambda p: (0, 2 * npair + p)),
                  pl.BlockSpec((tp, LANES), lambda p: (0, 0)),
                  pl.BlockSpec((SUBLANES, tp), lambda p: (0, 0))],
        out_specs=[pl.BlockSpec((tp, LANES), pair),
                   pl.BlockSpec((None, SUBLANES, tp), lambda p: (p, 0, 0))],
        out_shape=[jax.ShapeDtypeStruct((tp, nh * HEAD_DIM), F32),
                   jax.ShapeDtypeStruct((npair, SUBLANES, tp), F32)],
        compiler_params=_cparams("parallel"),
    )(qkv, qkv, qkv, c, ct)


def _shift_down(x, k, n):
    if k == 0:
        return x
    rows = lax.broadcasted_iota(jnp.int32, x.shape, 0)
    return jnp.where(rows >= k, pltpu.roll(x, k, 0), 0.0)


def _shift_up(x, k, n):
    if k == 0:
        return x
    rows = lax.broadcasted_iota(jnp.int32, x.shape, 0)
    return jnp.where(rows < n - k, pltpu.roll(x, n - k, 0), 0.0)


def _conv_fwd(xr, cw_ref, cb_ref, n):
    xc = cw_ref[CONV_WIDTH - 1:CONV_WIDTH, :] * xr + cb_ref[...]
    for k in range(1, CONV_WIDTH):
        xc = xc + cw_ref[CONV_WIDTH - 1 - k:CONV_WIDTH - k, :] * _shift_down(xr, k, n)
    return xc


def _gates(xc, wga_ref, bga_ref, wgx_ref, bgx_ref, l_ref):
    xcb = xc.astype(BF16)
    r = _sigmoid(_dot(xcb, wga_ref[...]) + bga_ref[...])
    ig = _sigmoid(_dot(xcb, wgx_ref[...]) + bgx_ref[...])
    ls = _log_sigmoid(l_ref[...])
    log_a = RG_C * r * ls
    a = jnp.exp(log_a)
    mult = jnp.sqrt(-_expm1(2.0 * log_a))
    return xcb, r, ig, ls, log_a, a, mult


def _scan_rows(a_s, u_s, out_ref, n, reverse):
    nt = n // SUBLANES
    row = lax.broadcasted_iota(jnp.int32, (SUBLANES, LANES), 0)

    def step(t, carry):
        tt = (nt - 1 - t) if reverse else t
        off = pl.multiple_of(tt * SUBLANES, SUBLANES)
        a = a_s[pl.ds(off, SUBLANES), :]
        u = u_s[pl.ds(off, SUBLANES), :]
        for d in (1, 2, 4):
            if reverse:
                keep = row < SUBLANES - d
                sh = SUBLANES - d
            else:
                keep = row >= d
                sh = d
            a_sh = jnp.where(keep, pltpu.roll(a, sh, 0), 1.0)
            u_sh = jnp.where(keep, pltpu.roll(u, sh, 0), 0.0)
            u = a * u_sh + u
            a = a * a_sh
        h = u + a * carry
        out_ref[pl.ds(off, SUBLANES), :] = h
        return h[0:1, :] if reverse else h[SUBLANES - 1:SUBLANES, :]

    lax.fori_loop(0, nt, step, jnp.zeros((1, LANES), F32))


def rec_fwd(rest, convw, convb, wga, bga, wgx, bgx, lru, rw):
    tp = rest.shape[0]
    ng = rw // LANES

    def body(xr_ref, yr_ref, cw_ref, cb_ref, wga_ref, bga_ref, wgx_ref, bgx_ref, l_ref,
             rec_ref, hr_ref, xc_ref, a_s, u_s):
        xc = _conv_fwd(xr_ref[...], cw_ref, cb_ref, tp)
        xc_ref[...] = xc
        _, r, ig, ls, log_a, a, mult = _gates(xc, wga_ref, bga_ref, wgx_ref, bgx_ref, l_ref)
        a_s[...] = a
        u_s[...] = mult * ig * xc
        _scan_rows(a_s, u_s, hr_ref, tp, reverse=False)
        rec_ref[...] = hr_ref[...] * _gelu(yr_ref[...])

    col = lambda g: (0, g)
    vec = pl.BlockSpec((1, LANES), col)
    big = pl.BlockSpec((tp, LANES), col)
    return pl.pallas_call(
        body, name="rec_fwd", grid=(ng,),
        in_specs=[big, pl.BlockSpec((tp, LANES), lambda g: (0, ng + g)),
                  pl.BlockSpec((CONV_WIDTH, LANES), col), vec,
                  pl.BlockSpec((None, LANES, LANES), lambda g: (g, 0, 0)), vec,
                  pl.BlockSpec((None, LANES, LANES), lambda g: (g, 0, 0)), vec, vec],
        out_specs=[big, big, big],
        out_shape=[jax.ShapeDtypeStruct((tp, rw), F32)] * 3,
        scratch_shapes=[pltpu.VMEM((tp, LANES), F32), pltpu.VMEM((tp, LANES), F32)],
        compiler_params=_cparams("parallel"),
    )(rest, rest, convw, convb, wga, bga, wgx, bgx, lru)


def out_proj(h, o, rec, ga, gr, wout, g2, tm):
    tp, d = h.shape
    aw, rw = o.shape[1], rec.shape[1]

    def body(h_ref, o_ref, rec_ref, ga_ref, gr_ref, w_ref, g2_ref, h2_ref, mix_ref, z2_ref):
        mix_ref[:, 0:aw] = _rms_fwd(o_ref[...], ga_ref[...]).astype(BF16)
        mix_ref[:, aw:aw + rw] = _rms_fwd(rec_ref[...], gr_ref[...]).astype(BF16)
        h2 = h_ref[...] + _dot(mix_ref[...], w_ref[...])
        h2_ref[...] = h2
        z2_ref[...] = _rms_fwd(h2, g2_ref[...]).astype(BF16)

    row = lambda i: (i, 0)
    fix = lambda i: (0, 0)
    return pl.pallas_call(
        body, name="out_proj", grid=(tp // tm,),
        in_specs=[pl.BlockSpec((tm, d), row), pl.BlockSpec((tm, aw), row), pl.BlockSpec((tm, rw), row),
                  pl.BlockSpec((1, aw), fix), pl.BlockSpec((1, rw), fix),
                  pl.BlockSpec((d, d), fix), pl.BlockSpec((1, d), fix)],
        out_specs=[pl.BlockSpec((tm, d), row)] * 3,
        out_shape=[jax.ShapeDtypeStruct((tp, d), F32), jax.ShapeDtypeStruct((tp, d), BF16),
                   jax.ShapeDtypeStruct((tp, d), BF16)],
        compiler_params=_cparams("parallel"),
    )(h, o, rec, ga, gr, wout, g2)


def mlp_fwd(z2, h2, gup, gdown, layer, tm):
    tp, d = h2.shape
    nf = gup.shape[0]
    tf = gup.shape[3]

    def body(z_ref, h_ref, wu_ref, wd_ref, u_ref, h3_ref, acc):
        j = pl.program_id(1)
        u = jnp.maximum(_dot(z_ref[...], wu_ref[...]), 0.0)
        u_ref[...] = u.astype(BF16)
        part = _dot((u * u).astype(BF16), wd_ref[...])

        @pl.when(j == 0)
        def _():
            acc[...] = h_ref[...] + part

        @pl.when(j > 0)
        def _():
            acc[...] += part

        @pl.when(j == nf - 1)
        def _():
            h3_ref[...] = acc[...]

    return pl.pallas_call(
        body, name="mlp_fwd", grid=(tp // tm, nf),
        in_specs=[pl.BlockSpec((tm, d), lambda i, j: (i, 0)),
                  pl.BlockSpec((tm, d), lambda i, j: (i, 0)),
                  pl.BlockSpec((None, None, d, tf), lambda i, j: (j, layer, 0, 0)),
                  pl.BlockSpec((None, None, tf, d), lambda i, j: (j, layer, 0, 0))],
        out_specs=[pl.BlockSpec((tm, tf), lambda i, j: (i, j)),
                   pl.BlockSpec((tm, d), lambda i, j: (i, 0))],
        out_shape=[jax.ShapeDtypeStruct((tp, nf * tf), BF16), jax.ShapeDtypeStruct((tp, d), F32)],
        scratch_shapes=[pltpu.VMEM((tm, d), F32)],
        compiler_params=_cparams("parallel", "arbitrary"),
    )(z2, h2, gup, gdown)


def loss_head(h, gf, tgt, t_real, tm):
    tp, d = h.shape

    def body(h_ref, g_ref, t_ref, dh_ref, dg_ref, loss_ref):
        i = pl.program_id(0)
        x = h_ref[...]
        g = g_ref[...]
        r = lax.rsqrt(jnp.mean(x * x, axis=-1, keepdims=True) + NORM_EPS)
        xn = x * r
        rows = i * tm + lax.broadcasted_iota(jnp.int32, (tm, 1), 0)
        valid = jnp.logical_and(rows >= N_META, rows < t_real)
        e = jnp.where(valid, xn * g - t_ref[...], 0.0)
        part = 0.5 * jnp.sum(jnp.sum(e * e, axis=1, keepdims=True) / d, axis=0, keepdims=True)
        dy = e / d
        dxn = dy * g
        dh_ref[...] = r * (dxn - xn * jnp.mean(dxn * xn, axis=-1, keepdims=True))
        _accumulate(dg_ref, jnp.sum(dy * xn, axis=0, keepdims=True), i == 0)
        _accumulate(loss_ref, jnp.broadcast_to(part, (1, LANES)), i == 0)

    row = lambda i: (i, 0)
    fix = lambda i: (0, 0)
    return pl.pallas_call(
        body, name="loss_head", grid=(tp // tm,),
        in_specs=[pl.BlockSpec((tm, d), row), pl.BlockSpec((1, d), fix), pl.BlockSpec((tm, d), row)],
        out_specs=[pl.BlockSpec((tm, d), row), pl.BlockSpec((1, d), fix), pl.BlockSpec((1, LANES), fix)],
        out_shape=[jax.ShapeDtypeStruct((tp, d), F32), jax.ShapeDtypeStruct((1, d), F32),
                   jax.ShapeDtypeStruct((1, LANES), F32)],
        compiler_params=_cparams("arbitrary"),
    )(h, gf, tgt)


def mlp_bwd(dh, u, h2, g2, gup, gdown, layer, tm):
    tp, d = dh.shape
    nf = gup.shape[0]
    tf = gup.shape[3]
    ni = tp // tm

    def body(dh_ref, u_ref, h2_ref, g_ref, wu_ref, wd_ref, dup_ref, dh2_ref, dg_ref, acc, dhb):
        i = pl.program_id(0)
        j = pl.program_id(1)

        @pl.when(j == 0)
        def _():
            dhb[...] = dh_ref[...].astype(BF16)

        dup = (_dot_nt(dhb[...], wd_ref[...]) * (2.0 * u_ref[...].astype(F32))).astype(BF16)
        dup_ref[...] = dup
        _accumulate(acc, _dot_nt(dup, wu_ref[...]), j == 0)

        @pl.when(j == nf - 1)
        def _():
            dx, dg = _rms_bwd(h2_ref[...], g_ref[...], acc[...])
            dh2_ref[...] = dh_ref[...] + dx
            _accumulate(dg_ref, dg, i == 0)

    return pl.pallas_call(
        body, name="mlp_bwd", grid=(ni, nf),
        in_specs=[pl.BlockSpec((tm, d), lambda i, j: (i, 0)),
                  pl.BlockSpec((tm, tf), lambda i, j: (i, j)),
                  pl.BlockSpec((tm, d), lambda i, j: (i, 0)),
                  pl.BlockSpec((1, d), lambda i, j: (0, 0)),
                  pl.BlockSpec((None, None, d, tf), lambda i, j: (j, layer, 0, 0)),
                  pl.BlockSpec((None, None, tf, d), lambda i, j: (j, layer, 0, 0))],
        out_specs=[pl.BlockSpec((tm, tf), lambda i, j: (i, j)),
                   pl.BlockSpec((tm, d), lambda i, j: (i, 0)),
                   pl.BlockSpec((1, d), lambda i, j: (0, 0))],
        out_shape=[jax.ShapeDtypeStruct((tp, nf * tf), BF16), jax.ShapeDtypeStruct((tp, d), F32),
                   jax.ShapeDtypeStruct((1, d), F32)],
        scratch_shapes=[pltpu.VMEM((tm, d), F32), pltpu.VMEM((tm, d), BF16)],
        compiler_params=_cparams("arbitrary", "arbitrary"),
    )(dh, u, h2, g2, gup, gdown)


def mm_tn(a, b, *, tk, tn, tr, out_dtype, name, square_a=False, blocked_n=False):
    rows, kk = a.shape
    nn = b.shape[1]
    nr = rows // tr

    def body(a_ref, b_ref, o_ref, acc):
        r = pl.program_id(2)
        av = a_ref[...]
        if square_a:
            af = av.astype(F32)
            av = (af * af).astype(BF16)
        prod = _dot_tn(av.astype(BF16), b_ref[...].astype(BF16))
        _accumulate(acc, prod, r == 0)

        @pl.when(r == nr - 1)
        def _():
            o_ref[...] = acc[...].astype(out_dtype)

    if blocked_n:
        out_spec = pl.BlockSpec((None, tk, tn), lambda k, n, r: (n, k, 0))
        out_shape = jax.ShapeDtypeStruct((nn // tn, kk, tn), out_dtype)
    else:
        out_spec = pl.BlockSpec((tk, tn), lambda k, n, r: (k, n))
        out_shape = jax.ShapeDtypeStruct((kk, nn), out_dtype)
    return pl.pallas_call(
        body, name=name, grid=(kk // tk, nn // tn, nr),
        in_specs=[pl.BlockSpec((tr, tk), lambda k, n, r: (r, k)),
                  pl.BlockSpec((tr, tn), lambda k, n, r: (r, n))],
        out_specs=out_spec, out_shape=out_shape,
        scratch_shapes=[pltpu.VMEM((tk, tn), F32)],
        compiler_params=_cparams("parallel", "parallel", "arbitrary"),
    )(a, b)


def out_proj_bwd(dh2, o, rec, ga, gr, wout, tm):
    tp, d = dh2.shape
    aw, rw = o.shape[1], rec.shape[1]

    def body(dh_ref, o_ref, rec_ref, ga_ref, gr_ref, w_ref, do_ref, drec_ref, dga_ref, dgr_ref):
        i = pl.program_id(0)
        dmix = _dot_nt(dh_ref[...].astype(BF16), w_ref[...])
        do, dga = _rms_bwd(o_ref[...], ga_ref[...], dmix[:, 0:aw])
        drec, dgr = _rms_bwd(rec_ref[...], gr_ref[...], dmix[:, aw:aw + rw])
        do_ref[...] = do
        drec_ref[...] = drec
        _accumulate(dga_ref, dga, i == 0)
        _accumulate(dgr_ref, dgr, i == 0)

    row = lambda i: (i, 0)
    fix = lambda i: (0, 0)
    return pl.pallas_call(
        body, name="out_proj_bwd", grid=(tp // tm,),
        in_specs=[pl.BlockSpec((tm, d), row), pl.BlockSpec((tm, aw), row), pl.BlockSpec((tm, rw), row),
                  pl.BlockSpec((1, aw), fix), pl.BlockSpec((1, rw), fix), pl.BlockSpec((d, d), fix)],
        out_specs=[pl.BlockSpec((tm, aw), row), pl.BlockSpec((tm, rw), row),
                   pl.BlockSpec((1, aw), fix), pl.BlockSpec((1, rw), fix)],
        out_shape=[jax.ShapeDtypeStruct((tp, aw), F32), jax.ShapeDtypeStruct((tp, rw), F32),
                   jax.ShapeDtypeStruct((1, aw), F32), jax.ShapeDtypeStruct((1, rw), F32)],
        compiler_params=_cparams("arbitrary"),
    )(dh2, o, rec, ga, gr, wout)


def rec_bwd(drec, hr, xc, rest, convw, convb, wga, bga, wgx, bgx, lru, rw):
    tp = rest.shape[0]
    ng = rw // LANES

    def body(drec_ref, hr_ref, xc_ref, xr_ref, yr_ref, cw_ref, cb_ref, wga_ref, bga_ref, wgx_ref, bgx_ref, l_ref,
             dxr_ref, dyr_ref, dwga_ref, dwgx_ref, vec_ref, a_s, u_s, lam_s):
        xc = xc_ref[...]
        h = hr_ref[...]
        yr = yr_ref[...]
        drec = drec_ref[...]
        xcb, r, ig, ls, log_a, a, mult = _gates(xc, wga_ref, bga_ref, wgx_ref, bgx_ref, l_ref)
        dyr_ref[...] = (drec * h * _gelu_grad(yr)).astype(BF16)
        a_s[...] = _shift_up(a, 1, tp)
        u_s[...] = drec * _gelu(yr)
        _scan_rows(a_s, u_s, lam_s, tp, reverse=True)
        lam = lam_s[...]
        da = lam * _shift_down(h, 1, tp)
        dmult = lam * ig * xc
        dig = lam * mult * xc
        dxc = lam * mult * ig
        a2 = jnp.exp(2.0 * log_a)
        dlog_a = da * a - dmult * a2 / mult
        dr = dlog_a * (RG_C * ls)
        dl = jnp.sum(dlog_a * (RG_C * r), axis=0, keepdims=True) * _sigmoid(-l_ref[...])
        dpa = dr * r * (1.0 - r)
        dpx = dig * ig * (1.0 - ig)
        dpab = dpa.astype(BF16)
        dpxb = dpx.astype(BF16)
        dxc = dxc + _dot_nt(dpab, wga_ref[...]) + _dot_nt(dpxb, wgx_ref[...])
        dwga_ref[...] = _dot_tn(xcb, dpab)
        dwgx_ref[...] = _dot_tn(xcb, dpxb)
        xr = xr_ref[...]
        dxr = cw_ref[CONV_WIDTH - 1:CONV_WIDTH, :] * dxc
        for k in range(1, CONV_WIDTH):
            dxr = dxr + cw_ref[CONV_WIDTH - 1 - k:CONV_WIDTH - k, :] * _shift_up(dxc, k, tp)
        dxr_ref[...] = dxr.astype(BF16)
        for k in range(CONV_WIDTH):
            vec_ref[k:k + 1, :] = jnp.sum(dxc * _shift_down(xr, CONV_WIDTH - 1 - k, tp), axis=0, keepdims=True)
        vec_ref[4:5, :] = jnp.sum(dxc, axis=0, keepdims=True)
        vec_ref[5:6, :] = jnp.sum(dpa, axis=0, keepdims=True)
        vec_ref[6:7, :] = jnp.sum(dpx, axis=0, keepdims=True)
        vec_ref[7:8, :] = dl

    col = lambda g: (0, g)
    vec = pl.BlockSpec((1, LANES), col)
    big = pl.BlockSpec((tp, LANES), col)
    sq = pl.BlockSpec((None, LANES, LANES), lambda g: (g, 0, 0))
    return pl.pallas_call(
        body, name="rec_bwd", grid=(ng,),
        in_specs=[big, big, big, big, pl.BlockSpec((tp, LANES), lambda g: (0, ng + g)),
                  pl.BlockSpec((CONV_WIDTH, LANES), col), vec, sq, vec, sq, vec, vec],
        out_specs=[big, big, sq, sq, pl.BlockSpec((None, SUBLANES, LANES), lambda g: (g, 0, 0))],
        out_shape=[jax.ShapeDtypeStruct((tp, rw), BF16), jax.ShapeDtypeStruct((tp, rw), BF16),
                   jax.ShapeDtypeStruct((ng, LANES, LANES), F32), jax.ShapeDtypeStruct((ng, LANES, LANES), F32),
                   jax.ShapeDtypeStruct((ng, SUBLANES, LANES), F32)],
        scratch_shapes=[pltpu.VMEM((tp, LANES), F32)] * 3,
        compiler_params=_cparams("parallel"),
    )(drec, hr, xc, rest, rest, convw, convb, wga, bga, wgx, bgx, lru)


def attn_bwd(qkv, do, o, lset, c, ct, nh):
    tp = qkv.shape[0]
    npair = nh // 2
    aw = nh * HEAD_DIM
    tiles = _att_tiles(tp)

    def body(q_ref, k_ref, v_ref, do_ref, o_ref, lset_ref, c_ref, ct_ref,
             dq_ref, dk_ref, dv_ref, drow_ref, dcol_ref, dk_acc, dv_acc):
        p = pl.program_id(0)
        dk_acc[...] = jnp.zeros_like(dk_acc)
        dv_acc[...] = jnp.zeros_like(dv_acc)
        dcol_ref[...] = jnp.zeros_like(dcol_ref)
        drow_ref[...] = jnp.zeros_like(drow_ref)
        for r0, nr, nk in tiles:
            rs = slice(r0, r0 + nr)
            causal = (r0 + lax.broadcasted_iota(jnp.int32, (nk, nr), 1)
                      >= lax.broadcasted_iota(jnp.int32, (nk, nr), 0))
            cblk = c_ref[0:nk, :]
            ctb = ct_ref[:, rs]
            for hh in range(2):
                head = 2 * p + hh
                hs = slice(hh * HEAD_DIM, (hh + 1) * HEAD_DIM)
                q = q_ref[rs, hs]
                k = k_ref[0:nk, hs]
                dof = do_ref[rs, hs]
                do16 = dof.astype(BF16)
                delta = jnp.sum(dof * o_ref[rs, hs], axis=1, keepdims=True)
                delta_row = jnp.broadcast_to(delta, (nr, LANES)).T[0:1, :]
                s_t = _dot_nt(k, q * ATT_SCALE) + (_pick_row(ctb, head) - _pick_col(cblk, head))
                p_t = jnp.where(causal, jnp.exp(s_t - lset_ref[hh:hh + 1, rs]), 0.0)
                ds_t = p_t * (_dot_nt(v_ref[0:nk, hs], do16) - delta_row)
                p16 = p_t.astype(BF16)
                ds16 = ds_t.astype(BF16)
                dv_acc[0:nk, hs] += _dot(p16, do16)
                dk_acc[0:nk, hs] += _dot(ds16, q) * ATT_SCALE
                dq_ref[rs, hs] = (_dot_tn(ds16, k) * ATT_SCALE).astype(BF16)
                drow_ref[hh:hh + 1, rs] = jnp.sum(ds_t, axis=0, keepdims=True)
                dcol_ref[0:nk, hs] -= jnp.broadcast_to(jnp.sum(ds_t, axis=1, keepdims=True), (nk, HEAD_DIM))
        dk_ref[...] = dk_acc[...].astype(BF16)
        dv_ref[...] = dv_acc[...].astype(BF16)

    pair = lambda p: (0, p)
    return pl.pallas_call(
        body, name="attn_bwd", grid=(npair,),
        in_specs=[pl.BlockSpec((tp, LANES), pair),
                  pl.BlockSpec((tp, LANES), lambda p: (0, npair + p)),
                  pl.BlockSpec((tp, LANES), lambda p: (0, 2 * npair + p)),
                  pl.BlockSpec((tp, LANES), pair),
                  pl.BlockSpec((tp, LANES), pair),
                  pl.BlockSpec((None, SUBLANES, tp), lambda p: (p, 0, 0)),
                  pl.BlockSpec((tp, LANES), lambda p: (0, 0)),
                  pl.BlockSpec((SUBLANES, tp), lambda p: (0, 0))],
        out_specs=[pl.BlockSpec((tp, LANES), pair), pl.BlockSpec((tp, LANES), pair),
                   pl.BlockSpec((tp, LANES), pair),
                   pl.BlockSpec((None, SUBLANES, tp), lambda p: (p, 0, 0)),
                   pl.BlockSpec((tp, LANES), pair)],
        out_shape=[jax.ShapeDtypeStruct((tp, aw), BF16), jax.ShapeDtypeStruct((tp, aw), BF16),
                   jax.ShapeDtypeStruct((tp, aw), BF16),
                   jax.ShapeDtypeStruct((npair, SUBLANES, tp), F32),
                   jax.ShapeDtypeStruct((tp, aw), F32)],
        scratch_shapes=[pltpu.VMEM((tp, LANES), F32), pltpu.VMEM((tp, LANES), F32)],
        compiler_params=_cparams("parallel"),
    )(qkv, qkv, qkv, do, o, lset, c, ct)


def fgate_bwd(dct8, drs, rest, bf_pad, fcol):
    tp = rest.shape[0]
    aw = drs.shape[1]
    nb = tp // ATT_BLOCK
    B = ATT_BLOCK

    def body(d_ref, drs_ref, f_ref, b_ref, dfl_ref, db_ref, pad_s):
        r_i = lax.broadcasted_iota(jnp.int32, (B, B), 0)
        c_i = lax.broadcasted_iota(jnp.int32, (B, B), 1)
        triu = (c_i >= r_i).astype(BF16)
        sel = (lax.broadcasted_iota(jnp.int32, (aw, LANES), 0)
               == HEAD_DIM * lax.broadcasted_iota(jnp.int32, (aw, LANES), 1)).astype(BF16)
        carry = jnp.zeros((1, LANES), F32)
        db = jnp.zeros((1, LANES), F32)
        pad_s[...] = jnp.zeros_like(pad_s)
        for i in range(nb - 1, -1, -1):
            sl = slice(i * B, (i + 1) * B)
            pad_s[0:SUBLANES, :] = d_ref[:, sl]
            dc = pad_s[...].T + _dot_split3(drs_ref[sl, :], sel)
            rc = _split3_dot(triu, dc)
            dlf = rc + carry
            carry = carry + rc[0:1, :]
            dfl = dlf * _sigmoid(-(f_ref[sl, :] + b_ref[...]))
            dfl_ref[sl, :] = dfl.astype(BF16)
            db = db + jnp.sum(dfl, axis=0, keepdims=True)
        db_ref[...] = db

    return pl.pallas_call(
        body, name="fgate_bwd", grid=(1,),
        in_specs=[pl.BlockSpec((SUBLANES, tp), lambda i: (0, 0)),
                  pl.BlockSpec((tp, aw), lambda i: (0, 0)),
                  pl.BlockSpec((tp, LANES), lambda i: (0, fcol)),
                  pl.BlockSpec((1, LANES), lambda i: (0, 0))],
        out_specs=[pl.BlockSpec((tp, LANES), lambda i: (0, 0)),
                   pl.BlockSpec((1, LANES), lambda i: (0, 0))],
        out_shape=[jax.ShapeDtypeStruct((tp, LANES), BF16), jax.ShapeDtypeStruct((1, LANES), F32)],
        scratch_shapes=[pltpu.VMEM((B, B), F32)],
        compiler_params=_cparams("arbitrary"),
    )(dct8, drs, rest, bf_pad)


def in_proj_bwd(dh2, parts, wqkv, wrest, h, g1, tm):
    tp, d = h.shape
    dq, dk, dv, dxr, dyr, dfl = parts
    aw, rw = dq.shape[1], dxr.shape[1]

    def body(dh2_ref, dq_ref, dk_ref, dv_ref, dxr_ref, dyr_ref, dfl_ref, wq_ref, wr_ref, h_ref, g_ref,
             dh_ref, dg_ref):
        i = pl.program_id(0)
        dz = _dot_nt(dq_ref[...], wq_ref[:, 0:aw])
        dz += _dot_nt(dk_ref[...], wq_ref[:, aw:2 * aw])
        dz += _dot_nt(dv_ref[...], wq_ref[:, 2 * aw:3 * aw])
        dz += _dot_nt(dxr_ref[...], wr_ref[:, 0:rw])
        dz += _dot_nt(dyr_ref[...], wr_ref[:, rw:2 * rw])
        dz += _dot_nt(dfl_ref[...], wr_ref[:, 2 * rw:2 * rw + LANES])
        dx, dg = _rms_bwd(h_ref[...], g_ref[...], dz)
        dh_ref[...] = dh2_ref[...] + dx
        _accumulate(dg_ref, dg, i == 0)

    row = lambda i: (i, 0)
    fix = lambda i: (0, 0)
    return pl.pallas_call(
        body, name="in_proj_bwd", grid=(tp // tm,),
        in_specs=[pl.BlockSpec((tm, d), row),
                  pl.BlockSpec((tm, aw), row), pl.BlockSpec((tm, aw), row), pl.BlockSpec((tm, aw), row),
                  pl.BlockSpec((tm, rw), row), pl.BlockSpec((tm, rw), row), pl.BlockSpec((tm, LANES), row),
                  pl.BlockSpec(wqkv.shape, fix), pl.BlockSpec(wrest.shape, fix),
                  pl.BlockSpec((tm, d), row), pl.BlockSpec((1, d), fix)],
        out_specs=[pl.BlockSpec((tm, d), row), pl.BlockSpec((1, d), fix)],
        out_shape=[jax.ShapeDtypeStruct((tp, d), F32), jax.ShapeDtypeStruct((1, d), F32)],
        compiler_params=_cparams("arbitrary"),
    )(dh2, dq, dk, dv, dxr, dyr, dfl, wqkv, wrest, h, g1)


def _place():
    return lax.axis_index("x"), lax.axis_index("y"), lax.axis_index("c")


def all_gather(arrs, name):
    n = len(arrs)

    def body(*refs):
        ins, outs = refs[:n], refs[n:2 * n]
        send_sems, recv_sems, local_sems = refs[2 * n:]
        x, y, c = _place()
        me, sibling = (x, y, c), (x, y, 1 - c)
        chips = [(1 - x, y), (x, 1 - y), (1 - x, 1 - y)]

        def slot(a, blk):
            return outs[a].at[4 * blk[0] + 2 * blk[1] + blk[2]]

        def copy(a, k, blk, to, src=None):
            return pltpu.make_async_remote_copy(
                src_ref=slot(a, blk) if src is None else src, dst_ref=slot(a, blk),
                send_sem=send_sems.at[a, k], recv_sem=recv_sems.at[a, k],
                device_id=to, device_id_type=MESH)

        mine = [pltpu.make_async_copy(ins[a], slot(a, me), local_sems.at[a]) for a in range(n)]
        for cp in mine:
            cp.start()
        first = []
        for a in range(n):
            first.append(copy(a, 0, me, sibling, src=ins[a]))
            first += [copy(a, 1 + j, me, (*chip, c), src=ins[a]) for j, chip in enumerate(chips)]
        for cp in first:
            cp.start()
        passed = []
        for j, chip in enumerate(chips):
            for a in range(n):
                copy(a, 1 + j, (*chip, c), me).wait_recv()
                fwd = copy(a, 4 + j, (*chip, c), sibling)
                fwd.start()
                passed.append(fwd)
        for a in range(n):
            copy(a, 0, sibling, me).wait_recv()
            for j, chip in enumerate(chips):
                copy(a, 4 + j, (*chip, 1 - c), me).wait_recv()
        for cp in first + passed:
            cp.wait_send()
        for cp in mine:
            cp.wait()

    return pl.pallas_call(
        body, name=name,
        in_specs=[ANY] * n, out_specs=[ANY] * n,
        out_shape=[jax.ShapeDtypeStruct((N_DEV,) + a.shape, a.dtype) for a in arrs],
        scratch_shapes=[pltpu.SemaphoreType.DMA((n, 7)), pltpu.SemaphoreType.DMA((n, 7)),
                        pltpu.SemaphoreType.DMA((n,))],
    )(*arrs)


def grad_exchange(per_layer, name):
    nw = len(per_layer)
    nl = len(per_layer[0])
    flat = [a for w in per_layer for a in w]
    n = len(flat)

    def body(*refs):
        ins, outs = refs[:n], refs[n:n + nw]
        send_sems, recv_sems, local_sems = refs[n + nw:]
        x, y, c = _place()
        me = 4 * x + 2 * y + c
        rel = [(dx, dy, dc) for dx in (0, 1) for dy in (0, 1) for dc in (0, 1) if dx + dy + dc]
        local = []
        for w in range(nw):
            for l in range(nl):
                cp = pltpu.make_async_copy(ins[w * nl + l].at[me], outs[w].at[me, l], local_sems.at[w * nl + l])
                cp.start()
                local.append(cp)
        sends, recvs = [], []
        for k, (dx, dy, dc) in enumerate(rel):
            px = (1 - x) if dx else x
            py = (1 - y) if dy else y
            pc = (1 - c) if dc else c
            peer = 4 * px + 2 * py + pc
            for w in range(nw):
                for l in range(nl):
                    a = w * nl + l
                    sends.append(pltpu.make_async_remote_copy(
                        src_ref=ins[a].at[peer], dst_ref=outs[w].at[me, l],
                        send_sem=send_sems.at[a, k], recv_sem=recv_sems.at[a, k],
                        device_id=(px, py, pc), device_id_type=MESH))
                    recvs.append(pltpu.make_async_remote_copy(
                        src_ref=ins[a].at[peer], dst_ref=outs[w].at[peer, l],
                        send_sem=send_sems.at[a, k], recv_sem=recv_sems.at[a, k],
                        device_id=(px, py, pc), device_id_type=MESH))
        for cp in sends:
            cp.start()
        for cp in recvs:
            cp.wait_recv()
        for cp in sends:
            cp.wait_send()
        for cp in local:
            cp.wait()

    return pl.pallas_call(
        body, name=name,
        in_specs=[ANY] * n, out_specs=[ANY] * nw,
        out_shape=[jax.ShapeDtypeStruct((N_DEV, nl) + w[0].shape[1:], w[0].dtype) for w in per_layer],
        scratch_shapes=[pltpu.SemaphoreType.DMA((n, 7)), pltpu.SemaphoreType.DMA((n, 7)),
                        pltpu.SemaphoreType.DMA((n,))],
    )(*flat)


def _adamw_math(g, w, m, v):
    m = ADAM_B1 * m + (1.0 - ADAM_B1) * g
    v = ADAM_B2 * v + (1.0 - ADAM_B2) * (g * g)
    m_hat = m / (1.0 - ADAM_B1 ** ADAM_STEP)
    v_hat = v / (1.0 - ADAM_B2 ** ADAM_STEP)
    delta = -ADAM_LR * (m_hat / (jnp.sqrt(v_hat) + ADAM_EPS) + ADAM_WD * w)
    return delta, m, v


def sum_adamw(parts, w, m, v, tr, name):
    npart, rows, cols = parts.shape

    def body(p_ref, w_ref, m_ref, v_ref, g_ref, d_ref, nm_ref, nv_ref):
        g = p_ref[0].astype(F32)
        for p in range(1, npart):
            g = g + p_ref[p].astype(F32)
        delta, nm, nv = _adamw_math(g, w_ref[...], m_ref[...], v_ref[...])
        g_ref[...] = g
        d_ref[...] = delta
        nm_ref[...] = nm
        nv_ref[...] = nv

    blk = pl.BlockSpec((tr, cols), lambda i: (i, 0))
    return pl.pallas_call(
        body, name=name, grid=(rows // tr,),
        in_specs=[pl.BlockSpec((npart, tr, cols), lambda i: (0, i, 0)), blk, blk, blk],
        out_specs=[blk] * 4,
        out_shape=[jax.ShapeDtypeStruct((rows, cols), F32)] * 4,
        compiler_params=_cparams("parallel"),
    )(parts, w, m, v)


def sum_parts(parts, name):
    npart, rows, cols = parts.shape

    def body(p_ref, g_ref):
        g = p_ref[0]
        for p in range(1, npart):
            g = g + p_ref[p]
        g_ref[...] = g

    return pl.pallas_call(
        body, name=name, grid=(1,),
        in_specs=[pl.BlockSpec((npart, rows, cols), lambda i: (0, 0, 0))],
        out_specs=pl.BlockSpec((rows, cols), lambda i: (0, 0)),
        out_shape=jax.ShapeDtypeStruct((rows, cols), F32),
        compiler_params=_cparams("arbitrary"),
    )(parts)


def _round_up(n, m):
    return (n + m - 1) // m * m


def _block_diag_pairs(w):
    nb, b, _ = w.shape
    per = LANES // b
    ng = nb // per
    w = w.reshape(ng, per, b, b)
    eye = jnp.eye(per, dtype=w.dtype)
    out = jnp.einsum('gpij,pq->gpiqj', w, eye).reshape(ng, LANES, LANES)
    return out.astype(BF16)


def _block_diag_extract(g, b):
    ng = g.shape[0]
    per = LANES // b
    g = g.reshape(ng, per, b, per, b)
    idx = jnp.arange(per)
    return g[:, idx, :, idx, :].transpose(1, 0, 2, 3).reshape(ng * per, b, b)


def _tiles(v):
    v = v.reshape(-1)
    n = _round_up(v.shape[0], SUBLANES * LANES)
    return jnp.pad(v, (0, n - v.shape[0])).reshape(-1, LANES)


SMALL = ['attn_norm_g', 'b_f', 'conv_w', 'conv_b', 'w_gate_a', 'b_gate_a', 'w_gate_x', 'b_gate_x',
         'lru_L', 'attn_out_g', 'rec_out_g', 'mlp_norm_g', 'final_g', 'meta']


def _pack(d):
    return jnp.concatenate([_tiles(d[n]) for n in SMALL], axis=0)


def _unpack(vec, shapes):
    out, r = {}, 0
    for n in SMALL:
        size = math.prod(shapes[n])
        nr = _round_up(size, SUBLANES * LANES) // LANES
        out[n] = vec[r:r + nr].reshape(-1)[:size].reshape(shapes[n])
        r += nr
    return out


def _row_tile(tp):
    return tp // 4 if (tp // 4) % 16 == 0 else tp


def local_step(x, tgt, meta, small, wqkv, wrest, wout, gup, gdown):
    s, d = x.shape
    t_real = s + N_META
    tp = _round_up(t_real, ATT_BLOCK)
    depth = wqkv.shape[0]
    aw = wqkv.shape[2] // 3
    rw = (wrest.shape[2] - LANES) // 2
    nh = aw // HEAD_DIM
    nblk, blk = small['w_gate_a'].shape[1], small['w_gate_a'].shape[2]
    tm = _row_tile(tp)
    tm2 = tp // 2
    fcol = 2 * rw // LANES
    tf = gup.shape[3]
    dff = gup.shape[0] * tf

    h = jnp.concatenate([meta, x, jnp.zeros((tp - t_real, d), F32)], axis=0)
    tgt_p = jnp.pad(tgt, ((N_META, tp - t_real), (0, 0)))
    row = lambda v: v.reshape(1, -1)
    bf_pad = jnp.pad(small['b_f'], ((0, 0), (0, LANES - nh)))

    saved = []
    for l in range(depth):
        wga = _block_diag_pairs(small['w_gate_a'][l])
        wgx = _block_diag_pairs(small['w_gate_x'][l])
        z, qkv, rest = in_proj(h, row(small['attn_norm_g'][l]), wqkv[l], wrest[l], tm)
        c, ct = fgate_fwd(rest, bf_pad[l:l + 1], fcol)
        o, lset = attn_fwd(qkv, c, ct, nh)
        rec, hr, xc = rec_fwd(rest, small['conv_w'][l], row(small['conv_b'][l]), wga, row(small['b_gate_a'][l]),
                              wgx, row(small['b_gate_x'][l]), row(small['lru_L'][l]), rw)
        h2, mix, z2 = out_proj(h, o, rec, row(small['attn_out_g'][l]), row(small['rec_out_g'][l]), wout[l],
                               row(small['mlp_norm_g'][l]), tm)
        u, h3 = mlp_fwd(z2, h2, gup, gdown, l, tm2)
        saved.append(dict(h=h, z=z, qkv=qkv, rest=rest, c=c, ct=ct, o=o, lset=lset, rec=rec, hr=hr, xc=xc,
                          h2=h2, mix=mix, z2=z2, u=u, wga=wga, wgx=wgx))
        h = h3

    dh, dgf, loss = loss_head(h, row(small['final_g']), tgt_p, t_real, tm)

    gs = {n: [None] * depth for n in SMALL if n not in ('final_g', 'meta')}
    big = {n: [None] * depth for n in ('w_in', 'w_out', 'w_up', 'w_down')}
    tk_in = d
    for l in reversed(range(depth)):
        sv = saved[l]
        dup, dh2, dg2 = mlp_bwd(dh, sv['u'], sv['h2'], row(small['mlp_norm_g'][l]), gup, gdown, l, tm2)
        big['w_down'][l] = mm_tn(sv['u'], dh, tk=tf, tn=d, tr=tm, out_dtype=BF16, name="dw_down",
                                 square_a=True).reshape(N_DEV, tf, d)
        big['w_up'][l] = mm_tn(sv['z2'], dup, tk=d, tn=tf, tr=tm, out_dtype=BF16, name="dw_up", blocked_n=True)
        gs['mlp_norm_g'][l] = dg2[0]
        do, drec, dga, dgr = out_proj_bwd(dh2, sv['o'], sv['rec'], row(small['attn_out_g'][l]),
                                          row(small['rec_out_g'][l]), wout[l], tm)
        big['w_out'][l] = mm_tn(sv['mix'], dh2, tk=d, tn=d // 2, tr=tm, out_dtype=BF16,
                                name="dw_out").reshape(N_DEV, d // N_DEV, d)
        gs['attn_out_g'][l] = dga[0]
        gs['rec_out_g'][l] = dgr[0]
        dxr, dyr, dwga, dwgx, vec = rec_bwd(drec, sv['hr'], sv['xc'], sv['rest'], small['conv_w'][l],
                                            row(small['conv_b'][l]), sv['wga'], row(small['b_gate_a'][l]),
                                            sv['wgx'], row(small['b_gate_x'][l]), row(small['lru_L'][l]), rw)
        gs['w_gate_a'][l] = _block_diag_extract(dwga, blk)
        gs['w_gate_x'][l] = _block_diag_extract(dwgx, blk)
        vec = vec.transpose(1, 0, 2).reshape(SUBLANES, rw)
        gs['conv_w'][l] = vec[0:CONV_WIDTH]
        gs['conv_b'][l] = vec[4]
        gs['b_gate_a'][l] = vec[5]
        gs['b_gate_x'][l] = vec[6]
        gs['lru_L'][l] = vec[7]
        dq, dk, dv, dct, drs = attn_bwd(sv['qkv'], do, sv['o'], sv['lset'], sv['c'], sv['ct'], nh)
        dct8 = jnp.pad(dct[:, 0:2, :].reshape(nh, tp), ((0, SUBLANES - nh), (0, 0))) if nh < SUBLANES \
            else dct[:, 0:2, :].reshape(nh, tp)
        dfl, dbf = fgate_bwd(dct8, drs, sv['rest'], bf_pad[l:l + 1], fcol)
        gs['b_f'][l] = dbf[0, 0:nh]
        parts = (dq, dk, dv, dxr, dyr, dfl)
        dh, dg1 = in_proj_bwd(dh2, parts, wqkv[l], wrest[l], sv['h'], row(small['attn_norm_g'][l]), tm)
        gs['attn_norm_g'][l] = dg1[0]
        dws = [mm_tn(sv['z'], pt, tk=tk_in, tn=pt.shape[1], tr=tm, out_dtype=BF16, name="dw_in_%d" % i)
               for i, pt in enumerate(parts)]
        dw_in = jnp.concatenate([dws[0], dws[1], dws[2], dws[5][:, 0:nh], dws[3], dws[4]], axis=1)
        big['w_in'][l] = dw_in.reshape(d, N_DEV, -1).transpose(1, 0, 2)

    grads = {n: jnp.stack(v) for n, v in gs.items()}
    grads['final_g'] = dgf[0]
    grads['meta'] = dh[0:N_META]
    return loss[0, 0], dh, grads, big


def prep_weights(g_in, g_out, nh, rw):
    _, depth, d, _ = g_in.shape
    w_in = g_in.transpose(1, 2, 0, 3).reshape(depth, d, -1)
    aw = nh * HEAD_DIM
    wqkv = w_in[:, :, 0:3 * aw]
    f0 = 3 * aw
    wrest = jnp.concatenate([w_in[:, :, f0 + nh:f0 + nh + 2 * rw], w_in[:, :, f0:f0 + nh],
                             jnp.zeros((depth, d, LANES - nh), w_in.dtype)], axis=2)
    wout = g_out.transpose(1, 0, 2, 3).reshape(depth, d, d)
    return wqkv, wrest, wout


BIG = ['w_in', 'w_out', 'w_up', 'w_down']
WEIGHTS = ['meta', 'attn_norm_g', 'w_in', 'b_f', 'conv_w', 'conv_b', 'w_gate_a', 'b_gate_a', 'w_gate_x', 'b_gate_x',
           'lru_L', 'attn_out_g', 'rec_out_g', 'w_out', 'mlp_norm_g', 'w_up', 'w_down', 'final_g']


def kernel(x, meta, attn_norm_g, w_in, b_f, conv_w, conv_b, w_gate_a, b_gate_a, w_gate_x, b_gate_x, lru_L, attn_out_g, rec_out_g, w_out, mlp_norm_g, w_up, w_down, final_g, loss_target, m_meta, m_attn_norm_g, m_w_in, m_b_f, m_conv_w, m_conv_b, m_w_gate_a, m_b_gate_a, m_w_gate_x, m_b_gate_x, m_lru_L, m_attn_out_g, m_rec_out_g, m_w_out, m_mlp_norm_g, m_w_up, m_w_down, m_final_g, v_meta, v_attn_norm_g, v_w_in, v_b_f, v_conv_w, v_conv_b, v_w_gate_a, v_b_gate_a, v_w_gate_x, v_b_gate_x, v_lru_L, v_attn_out_g, v_rec_out_g, v_w_out, v_mlp_norm_g, v_w_up, v_w_down, v_final_g):
    w = dict(meta=meta, attn_norm_g=attn_norm_g, w_in=w_in, b_f=b_f, conv_w=conv_w, conv_b=conv_b,
             w_gate_a=w_gate_a, b_gate_a=b_gate_a, w_gate_x=w_gate_x, b_gate_x=b_gate_x, lru_L=lru_L,
             attn_out_g=attn_out_g, rec_out_g=rec_out_g, w_out=w_out, mlp_norm_g=mlp_norm_g, w_up=w_up,
             w_down=w_down, final_g=final_g)
    mo = dict(meta=m_meta, attn_norm_g=m_attn_norm_g, w_in=m_w_in, b_f=m_b_f, conv_w=m_conv_w, conv_b=m_conv_b,
              w_gate_a=m_w_gate_a, b_gate_a=m_b_gate_a, w_gate_x=m_w_gate_x, b_gate_x=m_b_gate_x, lru_L=m_lru_L,
              attn_out_g=m_attn_out_g, rec_out_g=m_rec_out_g, w_out=m_w_out, mlp_norm_g=m_mlp_norm_g,
              w_up=m_w_up, w_down=m_w_down, final_g=m_final_g)
    vo = dict(meta=v_meta, attn_norm_g=v_attn_norm_g, w_in=v_w_in, b_f=v_b_f, conv_w=v_conv_w, conv_b=v_conv_b,
              w_gate_a=v_w_gate_a, b_gate_a=v_b_gate_a, w_gate_x=v_w_gate_x, b_gate_x=v_b_gate_x, lru_L=v_lru_L,
              attn_out_g=v_attn_out_g, rec_out_g=v_rec_out_g, w_out=v_w_out, mlp_norm_g=v_mlp_norm_g,
              w_up=v_w_up, w_down=v_w_down, final_g=v_final_g)
    depth = w_in.shape[0]
    nh = b_f.shape[1]
    rw = conv_b.shape[1]
    me = 4 * lax.axis_index("x") + 2 * lax.axis_index("y") + lax.axis_index("c")

    g_in, g_out, g_up, g_down, g_meta, g_conv = all_gather(
        [w_in.astype(BF16), w_out.astype(BF16), w_up.astype(BF16), w_down.astype(BF16), meta, conv_w],
        "gather_weights")
    wqkv, wrest, wout = prep_weights(g_in, g_out, nh, rw)
    meta_full = g_meta.transpose(1, 0, 2).reshape(N_META, -1)
    conv_full = g_conv.transpose(1, 2, 0, 3).reshape(depth, CONV_WIDTH, rw)
    small = {n: w[n] for n in SMALL}
    small['conv_w'] = conv_full

    loss_part, dh0, grads, big = local_step(x[0], loss_target[0], meta_full, small, wqkv, wrest, wout, g_up, g_down)
    loss = lax.psum(loss_part, ("x", "y", "c"))
    grad_x = dh0[N_META:N_META + x.shape[1]][None]

    recv = grad_exchange([big[n] for n in BIG], "exchange_grads")
    out_g, out_d, out_m, out_v = {}, {}, {}, {}
    for n, r in zip(BIG, recv):
        shp = w[n].shape
        rows, cols = shp[0] * shp[1], shp[2]
        tr = 512 if cols <= 512 else 256
        tr = min(tr, rows)
        res = sum_adamw(r.reshape(N_DEV, rows, cols), w[n].reshape(rows, cols), mo[n].reshape(rows, cols),
                        vo[n].reshape(rows, cols), tr, "adamw_" + n)
        out_g[n], out_d[n], out_m[n], out_v[n] = [a.reshape(shp) for a in res]

    (g_small,) = all_gather([_pack(grads)], "gather_small_grads")
    gsum = _unpack(sum_parts(g_small, "sum_small_grads"), {n: grads[n].shape for n in SMALL})
    gsum['meta'] = lax.dynamic_slice_in_dim(gsum['meta'], me * meta.shape[1], meta.shape[1], axis=1)
    gsum['conv_w'] = lax.dynamic_slice_in_dim(gsum['conv_w'], me * conv_w.shape[2], conv_w.shape[2], axis=2)
    packed = [_pack(t) for t in (gsum, {n: w[n] for n in SMALL}, {n: mo[n] for n in SMALL},
                                 {n: vo[n] for n in SMALL})]
    res = sum_adamw(packed[0][None], packed[1], packed[2], packed[3], packed[1].shape[0], "adamw_small")
    shapes = {n: w[n].shape for n in SMALL}
    for dst, vec in zip((out_g, out_d, out_m, out_v), res):
        dst.update(_unpack(vec, shapes))

    return (loss, grad_x, *[out_g[n] for n in WEIGHTS], *[out_d[n] for n in WEIGHTS],
            *[out_m[n] for n in WEIGHTS], *[out_v[n] for n in WEIGHTS])
```

```python
import functools
import math

import jax
import jax.numpy as jnp
from jax import lax
from jax.experimental import pallas as pl
from jax.experimental.pallas import tpu as pltpu

F32 = jnp.float32
BF16 = jnp.bfloat16

N_DEV = 8
N_META = 16
HEAD_DIM = 64
CONV_WIDTH = 4
RG_C = 8.0
NORM_EPS = 1e-6
LANES = 128
SUBLANES = 8
ATT_BLOCK = 128
ATT_TQ = 512
NEG_BIG = -1e30
ATT_SCALE = 1.0 / math.sqrt(HEAD_DIM)

ADAM_LR = 0.001
ADAM_B1 = 0.9
ADAM_B2 = 0.999
ADAM_EPS = 1e-08
ADAM_WD = 0.01
ADAM_STEP = 10

VMEM_LIMIT_BYTES = 56 * 1024 * 1024
MESH = pl.DeviceIdType.MESH
ANY = pl.BlockSpec(memory_space=pl.ANY)


def _cparams(*sem):
    return pltpu.CompilerParams(dimension_semantics=sem if sem else None,
                                vmem_limit_bytes=VMEM_LIMIT_BYTES)


def _dot(a, b):
    return jnp.dot(a, b, preferred_element_type=F32)


def _dot_nt(a, b):
    return lax.dot_general(a, b, (((1,), (1,)), ((), ())), preferred_element_type=F32)


def _dot_tn(a, b):
    return lax.dot_general(a, b, (((0,), (0,)), ((), ())), preferred_element_type=F32)


def _sigmoid(x):
    return 1.0 / (1.0 + jnp.exp(-x))


def _log_sigmoid(x):
    return jnp.minimum(x, 0.0) - jnp.log(1.0 + jnp.exp(-jnp.abs(x)))


def _expm1(x):
    series = x * (1.0 + x * (0.5 + x * (1.0 / 6.0 + x * (1.0 / 24.0))))
    return jnp.where(jnp.abs(x) < 1e-2, series, jnp.exp(x) - 1.0)


_GELU_K = math.sqrt(2.0 / math.pi)
_GELU_C = 0.044715


def _gelu(x):
    t = jnp.tanh(_GELU_K * (x + _GELU_C * x * x * x))
    return 0.5 * x * (1.0 + t)


def _gelu_grad(x):
    t = jnp.tanh(_GELU_K * (x + _GELU_C * x * x * x))
    return 0.5 * (1.0 + t) + 0.5 * x * (1.0 - t * t) * _GELU_K * (1.0 + 3.0 * _GELU_C * x * x)


def _split3_dot(tri, x):
    hi = x.astype(BF16)
    r1 = x - hi.astype(F32)
    mid = r1.astype(BF16)
    lo = (r1 - mid.astype(F32)).astype(BF16)
    return _dot(tri, hi) + _dot(tri, mid) + _dot(tri, lo)


def _dot_split3(x, sel):
    hi = x.astype(BF16)
    r1 = x - hi.astype(F32)
    mid = r1.astype(BF16)
    lo = (r1 - mid.astype(F32)).astype(BF16)
    return _dot(hi, sel) + _dot(mid, sel) + _dot(lo, sel)


def _rms_fwd(x, g):
    r = lax.rsqrt(jnp.mean(x * x, axis=-1, keepdims=True) + NORM_EPS)
    return x * r * g


def _rms_bwd(x, g, dy):
    r = lax.rsqrt(jnp.mean(x * x, axis=-1, keepdims=True) + NORM_EPS)
    xn = x * r
    dxn = dy * g
    dx = r * (dxn - xn * jnp.mean(dxn * xn, axis=-1, keepdims=True))
    return dx, jnp.sum(dy * xn, axis=0, keepdims=True)


def _accumulate(ref, val, first):
    @pl.when(first)
    def _():
        ref[...] = val

    @pl.when(jnp.logical_not(first))
    def _():
        ref[...] += val


def in_proj(h, g1, wqkv, wrest, tm):
    tp, d = h.shape
    nq, nr = wqkv.shape[1], wrest.shape[1]

    def body(h_ref, g_ref, wq_ref, wr_ref, z_ref, qkv_ref, rest_ref):
        z = _rms_fwd(h_ref[...], g_ref[...]).astype(BF16)
        z_ref[...] = z
        qkv_ref[...] = _dot(z, wq_ref[...]).astype(BF16)
        rest_ref[...] = _dot(z, wr_ref[...])

    return pl.pallas_call(
        body, name="in_proj", grid=(tp // tm,),
        in_specs=[pl.BlockSpec((tm, d), lambda i: (i, 0)),
                  pl.BlockSpec((1, d), lambda i: (0, 0)),
                  pl.BlockSpec((d, nq), lambda i: (0, 0)),
                  pl.BlockSpec((d, nr), lambda i: (0, 0))],
        out_specs=[pl.BlockSpec((tm, d), lambda i: (i, 0)),
                   pl.BlockSpec((tm, nq), lambda i: (i, 0)),
                   pl.BlockSpec((tm, nr), lambda i: (i, 0))],
        out_shape=[jax.ShapeDtypeStruct((tp, d), BF16),
                   jax.ShapeDtypeStruct((tp, nq), BF16),
                   jax.ShapeDtypeStruct((tp, nr), F32)],
        compiler_params=_cparams("parallel"),
    )(h, g1, wqkv, wrest)


def fgate_fwd(rest, bf_pad, fcol):
    tp = rest.shape[0]
    nb = tp // ATT_BLOCK

    def body(f_ref, b_ref, c_ref, ct_ref):
        r_i = lax.broadcasted_iota(jnp.int32, (ATT_BLOCK, ATT_BLOCK), 0)
        c_i = lax.broadcasted_iota(jnp.int32, (ATT_BLOCK, ATT_BLOCK), 1)
        tri = (r_i >= c_i).astype(BF16)
        carry = jnp.zeros((1, LANES), F32)
        for i in range(nb):
            sl = slice(i * ATT_BLOCK, (i + 1) * ATT_BLOCK)
            lf = _log_sigmoid(f_ref[sl, :] + b_ref[...])
            cs = _split3_dot(tri, lf) + carry
            carry = cs[ATT_BLOCK - 1:ATT_BLOCK, :]
            c_ref[sl, :] = cs
            ct_ref[:, sl] = cs.T[0:SUBLANES, :]

    return pl.pallas_call(
        body, name="fgate_fwd", grid=(1,),
        in_specs=[pl.BlockSpec((tp, LANES), lambda i: (0, fcol)),
                  pl.BlockSpec((1, LANES), lambda i: (0, 0))],
        out_specs=[pl.BlockSpec((tp, LANES), lambda i: (0, 0)),
                   pl.BlockSpec((SUBLANES, tp), lambda i: (0, 0))],
        out_shape=[jax.ShapeDtypeStruct((tp, LANES), F32),
                   jax.ShapeDtypeStruct((SUBLANES, tp), F32)],
        compiler_params=_cparams("arbitrary"),
    )(rest, bf_pad)


def _pick_col(blk, head):
    lane = lax.broadcasted_iota(jnp.int32, blk.shape, 1)
    return jnp.sum(jnp.where(lane == head, blk, 0.0), axis=1, keepdims=True)


def _pick_row(blk, head):
    sub = lax.broadcasted_iota(jnp.int32, blk.shape, 0)
    return jnp.sum(jnp.where(sub == head, blk, 0.0), axis=0, keepdims=True)


def _att_tiles(tp):
    out, r0 = [], 0
    while r0 < tp:
        rows = min(ATT_TQ, tp - r0)
        out.append((r0, rows, r0 + rows))
        r0 += rows
    return out


def attn_fwd(qkv, c, ct, nh):
    tp = qkv.shape[0]
    npair = nh // 2
    tiles = _att_tiles(tp)

    def body(q_ref, k_ref, v_ref, c_ref, ct_ref, o_ref, lset_ref):
        p = pl.program_id(0)
        lset_ref[...] = jnp.zeros_like(lset_ref)
        for r0, nr, nk in tiles:
            rs = slice(r0, r0 + nr)
            causal = (r0 + lax.broadcasted_iota(jnp.int32, (nr, nk), 0)
                      >= lax.broadcasted_iota(jnp.int32, (nr, nk), 1))
            cblk = c_ref[rs, :]
            ctb = ct_ref[:, 0:nk]
            for hh in range(2):
                head = 2 * p + hh
                hs = slice(hh * HEAD_DIM, (hh + 1) * HEAD_DIM)
                q = q_ref[rs, hs] * ATT_SCALE
                s = _dot_nt(q, k_ref[0:nk, hs]) + (_pick_col(cblk, head) - _pick_row(ctb, head))
                s = jnp.where(causal, s, NEG_BIG)
                m = jnp.max(s, axis=1, keepdims=True)
                pm = jnp.exp(s - m)
                l = jnp.sum(pm, axis=1, keepdims=True)
                o_ref[rs, hs] = _dot(pm.astype(BF16), v_ref[0:nk, hs]) / l
                lse = m + jnp.log(l)
                lset_ref[hh:hh + 1, rs] = jnp.broadcast_to(lse, (nr, LANES)).T[0:1, :]

    pair = lambda p: (0, p)
    return pl.pallas_call(
        body, name="attn_fwd", grid=(npair,),
        in_specs=[pl.BlockSpec((tp, LANES), pair),
                  pl.BlockSpec((tp, LANES), lambda p: (0, npair + p)),
                  pl.BlockSpec((tp, LANES), lambda p: (0, 2 * npair + p)),
                  pl.BlockSpec((tp, LANES), lambda p: (0, 0)),
                  pl.BlockSpec((SUBLANES, tp), lambda p: (0, 0))],
        out_specs=[pl.BlockSpec((tp, LANES), pair),
                   pl.BlockSpec((None, SUBLANES, tp), lambda p: (p, 0, 0))],
        out_shape=[jax.ShapeDtypeStruct((tp, nh * HEAD_DIM), F32),
                   jax.ShapeDtypeStruct((npair, SUBLANES, tp), F32)],
        compiler_params=_cparams("parallel"),
    )(qkv, qkv, qkv, c, ct)


def _shift_down(x, k, n):
    if k == 0:
        return x
    rows = lax.broadcasted_iota(jnp.int32, x.shape, 0)
    return jnp.where(rows >= k, pltpu.roll(x, k, 0), 0.0)


def _shift_up(x, k, n):
    if k == 0:
        return x
    rows = lax.broadcasted_iota(jnp.int32, x.shape, 0)
    return jnp.where(rows < n - k, pltpu.roll(x, n - k, 0), 0.0)


def _conv_fwd(xr, cw_ref, cb_ref, n):
    xc = cw_ref[CONV_WIDTH - 1:CONV_WIDTH, :] * xr + cb_ref[...]
    for k in range(1, CONV_WIDTH):
        xc = xc + cw_ref[CONV_WIDTH - 1 - k:CONV_WIDTH - k, :] * _shift_down(xr, k, n)
    return xc


def _gates(xc, wga_ref, bga_ref, wgx_ref, bgx_ref, l_ref):
    xcb = xc.astype(BF16)
    r = _sigmoid(_dot(xcb, wga_ref[...]) + bga_ref[...])
    ig = _sigmoid(_dot(xcb, wgx_ref[...]) + bgx_ref[...])
    ls = _log_sigmoid(l_ref[...])
    log_a = RG_C * r * ls
    a = jnp.exp(log_a)
    mult = jnp.sqrt(-_expm1(2.0 * log_a))
    return xcb, r, ig, ls, log_a, a, mult


def _scan_rows(a_s, u_s, out_ref, n, reverse):
    nt = n // SUBLANES
    row = lax.broadcasted_iota(jnp.int32, (SUBLANES, LANES), 0)

    def step(t, carry):
        tt = (nt - 1 - t) if reverse else t
        off = pl.multiple_of(tt * SUBLANES, SUBLANES)
        a = a_s[pl.ds(off, SUBLANES), :]
        u = u_s[pl.ds(off, SUBLANES), :]
        for d in (1, 2, 4):
            if reverse:
                keep = row < SUBLANES - d
                sh = SUBLANES - d
            else:
                keep = row >= d
                sh = d
            a_sh = jnp.where(keep, pltpu.roll(a, sh, 0), 1.0)
            u_sh = jnp.where(keep, pltpu.roll(u, sh, 0), 0.0)
            u = a * u_sh + u
            a = a * a_sh
        h = u + a * carry
        out_ref[pl.ds(off, SUBLANES), :] = h
        return h[0:1, :] if reverse else h[SUBLANES - 1:SUBLANES, :]

    lax.fori_loop(0, nt, step, jnp.zeros((1, LANES), F32))


def rec_fwd(rest, convw, convb, wga, bga, wgx, bgx, lru, rw):
    tp = rest.shape[0]
    ng = rw // LANES

    def body(xr_ref, yr_ref, cw_ref, cb_ref, wga_ref, bga_ref, wgx_ref, bgx_ref, l_ref,
             rec_ref, hr_ref, xc_ref, a_s, u_s):
        xc = _conv_fwd(xr_ref[...], cw_ref, cb_ref, tp)
        xc_ref[...] = xc
        _, r, ig, ls, log_a, a, mult = _gates(xc, wga_ref, bga_ref, wgx_ref, bgx_ref, l_ref)
        a_s[...] = a
        u_s[...] = mult * ig * xc
        _scan_rows(a_s, u_s, hr_ref, tp, reverse=False)
        rec_ref[...] = hr_ref[...] * _gelu(yr_ref[...])

    col = lambda g: (0, g)
    vec = pl.BlockSpec((1, LANES), col)
    big = pl.BlockSpec((tp, LANES), col)
    return pl.pallas_call(
        body, name="rec_fwd", grid=(ng,),
        in_specs=[big, pl.BlockSpec((tp, LANES), lambda g: (0, ng + g)),
                  pl.BlockSpec((CONV_WIDTH, LANES), col), vec,
                  pl.BlockSpec((None, LANES, LANES), lambda g: (g, 0, 0)), vec,
                  pl.BlockSpec((None, LANES, LANES), lambda g: (g, 0, 0)), vec, vec],
        out_specs=[big, big, big],
        out_shape=[jax.ShapeDtypeStruct((tp, rw), F32)] * 3,
        scratch_shapes=[pltpu.VMEM((tp, LANES), F32), pltpu.VMEM((tp, LANES), F32)],
        compiler_params=_cparams("parallel"),
    )(rest, rest, convw, convb, wga, bga, wgx, bgx, lru)


def out_proj(h, o, rec, ga, gr, wout, g2, tm):
    tp, d = h.shape
    aw, rw = o.shape[1], rec.shape[1]

    def body(h_ref, o_ref, rec_ref, ga_ref, gr_ref, w_ref, g2_ref, h2_ref, mix_ref, z2_ref):
        mix_ref[:, 0:aw] = _rms_fwd(o_ref[...], ga_ref[...]).astype(BF16)
        mix_ref[:, aw:aw + rw] = _rms_fwd(rec_ref[...], gr_ref[...]).astype(BF16)
        h2 = h_ref[...] + _dot(mix_ref[...], w_ref[...])
        h2_ref[...] = h2
        z2_ref[...] = _rms_fwd(h2, g2_ref[...]).astype(BF16)

    row = lambda i: (i, 0)
    fix = lambda i: (0, 0)
    return pl.pallas_call(
        body, name="out_proj", grid=(tp // tm,),
        in_specs=[pl.BlockSpec((tm, d), row), pl.BlockSpec((tm, aw), row), pl.BlockSpec((tm, rw), row),
                  pl.BlockSpec((1, aw), fix), pl.BlockSpec((1, rw), fix),
                  pl.BlockSpec((d, d), fix), pl.BlockSpec((1, d), fix)],
        out_specs=[pl.BlockSpec((tm, d), row)] * 3,
        out_shape=[jax.ShapeDtypeStruct((tp, d), F32), jax.ShapeDtypeStruct((tp, d), BF16),
                   jax.ShapeDtypeStruct((tp, d), BF16)],
        compiler_params=_cparams("parallel"),
    )(h, o, rec, ga, gr, wout, g2)


def mlp_fwd(z2, h2, gup, gdown, tm):
    tp, d = h2.shape
    nf = gup.shape[0]
    tf = gup.shape[2]

    def body(z_ref, h_ref, wu_ref, wd_ref, u_ref, h3_ref, acc):
        j = pl.program_id(1)
        u = jnp.maximum(_dot(z_ref[...], wu_ref[...]), 0.0)
        u_ref[...] = u.astype(BF16)
        part = _dot((u * u).astype(BF16), wd_ref[...])

        @pl.when(j == 0)
        def _():
            acc[...] = h_ref[...] + part

        @pl.when(j > 0)
        def _():
            acc[...] += part

        @pl.when(j == nf - 1)
        def _():
            h3_ref[...] = acc[...]

    return pl.pallas_call(
        body, name="mlp_fwd", grid=(tp // tm, nf),
        in_specs=[pl.BlockSpec((tm, d), lambda i, j: (i, 0)),
                  pl.BlockSpec((tm, d), lambda i, j: (i, 0)),
                  pl.BlockSpec((None, d, tf), lambda i, j: (j, 0, 0)),
                  pl.BlockSpec((None, tf, d), lambda i, j: (j, 0, 0))],
        out_specs=[pl.BlockSpec((tm, tf), lambda i, j: (i, j)),
                   pl.BlockSpec((tm, d), lambda i, j: (i, 0))],
        out_shape=[jax.ShapeDtypeStruct((tp, nf * tf), BF16), jax.ShapeDtypeStruct((tp, d), F32)],
        scratch_shapes=[pltpu.VMEM((tm, d), F32)],
        compiler_params=_cparams("parallel", "arbitrary"),
    )(z2, h2, gup, gdown)


def loss_head(h, gf, tgt, t_real, tm):
    tp, d = h.shape

    def body(h_ref, g_ref, t_ref, dh_ref, dg_ref, loss_ref):
        i = pl.program_id(0)
        x = h_ref[...]
        g = g_ref[...]
        r = lax.rsqrt(jnp.mean(x * x, axis=-1, keepdims=True) + NORM_EPS)
        xn = x * r
        rows = i * tm + lax.broadcasted_iota(jnp.int32, (tm, 1), 0)
        valid = jnp.logical_and(rows >= N_META, rows < t_real)
        e = jnp.where(valid, xn * g - t_ref[...], 0.0)
        part = 0.5 * jnp.sum(jnp.sum(e * e, axis=1, keepdims=True) / d, axis=0, keepdims=True)
        dy = e / d
        dxn = dy * g
        dh_ref[...] = r * (dxn - xn * jnp.mean(dxn * xn, axis=-1, keepdims=True))
        _accumulate(dg_ref, jnp.sum(dy * xn, axis=0, keepdims=True), i == 0)
        _accumulate(loss_ref, jnp.broadcast_to(part, (1, LANES)), i == 0)

    row = lambda i: (i, 0)
    fix = lambda i: (0, 0)
    return pl.pallas_call(
        body, name="loss_head", grid=(tp // tm,),
        in_specs=[pl.BlockSpec((tm, d), row), pl.BlockSpec((1, d), fix), pl.BlockSpec((tm, d), row)],
        out_specs=[pl.BlockSpec((tm, d), row), pl.BlockSpec((1, d), fix), pl.BlockSpec((1, LANES), fix)],
        out_shape=[jax.ShapeDtypeStruct((tp, d), F32), jax.ShapeDtypeStruct((1, d), F32),
                   jax.ShapeDtypeStruct((1, LANES), F32)],
        compiler_params=_cparams("arbitrary"),
    )(h, gf, tgt)


def mlp_bwd(dh, u, h2, g2, gup, gdown, tm):
    tp, d = dh.shape
    nf = gup.shape[0]
    tf = gup.shape[2]
    ni = tp // tm

    def body(dh_ref, u_ref, h2_ref, g_ref, wu_ref, wd_ref, dup_ref, dh2_ref, dg_ref, acc, dhb):
        i = pl.program_id(0)
        j = pl.program_id(1)

        @pl.when(j == 0)
        def _():
            dhb[...] = dh_ref[...].astype(BF16)

        dup = (_dot_nt(dhb[...], wd_ref[...]) * (2.0 * u_ref[...].astype(F32))).astype(BF16)
        dup_ref[...] = dup
        _accumulate(acc, _dot_nt(dup, wu_ref[...]), j == 0)

        @pl.when(j == nf - 1)
        def _():
            dx, dg = _rms_bwd(h2_ref[...], g_ref[...], acc[...])
            dh2_ref[...] = dh_ref[...] + dx
            _accumulate(dg_ref, dg, i == 0)

    return pl.pallas_call(
        body, name="mlp_bwd", grid=(ni, nf),
        in_specs=[pl.BlockSpec((tm, d), lambda i, j: (i, 0)),
                  pl.BlockSpec((tm, tf), lambda i, j: (i, j)),
                  pl.BlockSpec((tm, d), lambda i, j: (i, 0)),
                  pl.BlockSpec((1, d), lambda i, j: (0, 0)),
                  pl.BlockSpec((None, d, tf), lambda i, j: (j, 0, 0)),
                  pl.BlockSpec((None, tf, d), lambda i, j: (j, 0, 0))],
        out_specs=[pl.BlockSpec((tm, tf), lambda i, j: (i, j)),
                   pl.BlockSpec((tm, d), lambda i, j: (i, 0)),
                   pl.BlockSpec((1, d), lambda i, j: (0, 0))],
        out_shape=[jax.ShapeDtypeStruct((tp, nf * tf), BF16), jax.ShapeDtypeStruct((tp, d), F32),
                   jax.ShapeDtypeStruct((1, d), F32)],
        scratch_shapes=[pltpu.VMEM((tm, d), F32), pltpu.VMEM((tm, d), BF16)],
        compiler_params=_cparams("arbitrary", "arbitrary"),
    )(dh, u, h2, g2, gup, gdown)


def mm_tn(a, b, *, tk, tn, tr, out_dtype, name, square_a=False, blocked_n=False):
    rows, kk = a.shape
    nn = b.shape[1]
    nr = rows // tr

    def body(a_ref, b_ref, o_ref, acc):
        r = pl.program_id(2)
        av = a_ref[...]
        if square_a:
            af = av.astype(F32)
            av = (af * af).astype(BF16)
        prod = _dot_tn(av.astype(BF16), b_ref[...].astype(BF16))
        _accumulate(acc, prod, r == 0)

        @pl.when(r == nr - 1)
        def _():
            o_ref[...] = acc[...].astype(out_dtype)

    if blocked_n:
        out_spec = pl.BlockSpec((None, tk, tn), lambda k, n, r: (n, k, 0))
        out_shape = jax.ShapeDtypeStruct((nn // tn, kk, tn), out_dtype)
    else:
        out_spec = pl.BlockSpec((tk, tn), lambda k, n, r: (k, n))
        out_shape = jax.ShapeDtypeStruct((kk, nn), out_dtype)
    return pl.pallas_call(
        body, name=name, grid=(kk // tk, nn // tn, nr),
        in_specs=[pl.BlockSpec((tr, tk), lambda k, n, r: (r, k)),
                  pl.BlockSpec((tr, tn), lambda k, n, r: (r, n))],
        out_specs=out_spec, out_shape=out_shape,
        scratch_shapes=[pltpu.VMEM((tk, tn), F32)],
        compiler_params=_cparams("parallel", "parallel", "arbitrary"),
    )(a, b)


def out_proj_bwd(dh2, o, rec, ga, gr, wout, tm):
    tp, d = dh2.shape
    aw, rw = o.shape[1], rec.shape[1]

    def body(dh_ref, o_ref, rec_ref, ga_ref, gr_ref, w_ref, do_ref, drec_ref, dga_ref, dgr_ref):
        i = pl.program_id(0)
        dmix = _dot_nt(dh_ref[...].astype(BF16), w_ref[...])
        do, dga = _rms_bwd(o_ref[...], ga_ref[...], dmix[:, 0:aw])
        drec, dgr = _rms_bwd(rec_ref[...], gr_ref[...], dmix[:, aw:aw + rw])
        do_ref[...] = do
        drec_ref[...] = drec
        _accumulate(dga_ref, dga, i == 0)
        _accumulate(dgr_ref, dgr, i == 0)

    row = lambda i: (i, 0)
    fix = lambda i: (0, 0)
    return pl.pallas_call(
        body, name="out_proj_bwd", grid=(tp // tm,),
        in_specs=[pl.BlockSpec((tm, d), row), pl.BlockSpec((tm, aw), row), pl.BlockSpec((tm, rw), row),
                  pl.BlockSpec((1, aw), fix), pl.BlockSpec((1, rw), fix), pl.BlockSpec((d, d), fix)],
        out_specs=[pl.BlockSpec((tm, aw), row), pl.BlockSpec((tm, rw), row),
                   pl.BlockSpec((1, aw), fix), pl.BlockSpec((1, rw), fix)],
        out_shape=[jax.ShapeDtypeStruct((tp, aw), F32), jax.ShapeDtypeStruct((tp, rw), F32),
                   jax.ShapeDtypeStruct((1, aw), F32), jax.ShapeDtypeStruct((1, rw), F32)],
        compiler_params=_cparams("arbitrary"),
    )(dh2, o, rec, ga, gr, wout)


def rec_bwd(drec, hr, xc, rest, convw, convb, wga, bga, wgx, bgx, lru, rw):
    tp = rest.shape[0]
    ng = rw // LANES

    def body(drec_ref, hr_ref, xc_ref, xr_ref, yr_ref, cw_ref, cb_ref, wga_ref, bga_ref, wgx_ref, bgx_ref, l_ref,
             dxr_ref, dyr_ref, dwga_ref, dwgx_ref, vec_ref, a_s, u_s, lam_s):
        xc = xc_ref[...]
        h = hr_ref[...]
        yr = yr_ref[...]
        drec = drec_ref[...]
        xcb, r, ig, ls, log_a, a, mult = _gates(xc, wga_ref, bga_ref, wgx_ref, bgx_ref, l_ref)
        dyr_ref[...] = (drec * h * _gelu_grad(yr)).astype(BF16)
        a_s[...] = _shift_up(a, 1, tp)
        u_s[...] = drec * _gelu(yr)
        _scan_rows(a_s, u_s, lam_s, tp, reverse=True)
        lam = lam_s[...]
        da = lam * _shift_down(h, 1, tp)
        dmult = lam * ig * xc
        dig = lam * mult * xc
        dxc = lam * mult * ig
        a2 = jnp.exp(2.0 * log_a)
        dlog_a = da * a - dmult * a2 / mult
        dr = dlog_a * (RG_C * ls)
        dl = jnp.sum(dlog_a * (RG_C * r), axis=0, keepdims=True) * _sigmoid(-l_ref[...])
        dpa = dr * r * (1.0 - r)
        dpx = dig * ig * (1.0 - ig)
        dpab = dpa.astype(BF16)
        dpxb = dpx.astype(BF16)
        dxc = dxc + _dot_nt(dpab, wga_ref[...]) + _dot_nt(dpxb, wgx_ref[...])
        dwga_ref[...] = _dot_tn(xcb, dpab)
        dwgx_ref[...] = _dot_tn(xcb, dpxb)
        xr = xr_ref[...]
        dxr = cw_ref[CONV_WIDTH - 1:CONV_WIDTH, :] * dxc
        for k in range(1, CONV_WIDTH):
            dxr = dxr + cw_ref[CONV_WIDTH - 1 - k:CONV_WIDTH - k, :] * _shift_up(dxc, k, tp)
        dxr_ref[...] = dxr.astype(BF16)
        for k in range(CONV_WIDTH):
            vec_ref[k:k + 1, :] = jnp.sum(dxc * _shift_down(xr, CONV_WIDTH - 1 - k, tp), axis=0, keepdims=True)
        vec_ref[4:5, :] = jnp.sum(dxc, axis=0, keepdims=True)
        vec_ref[5:6, :] = jnp.sum(dpa, axis=0, keepdims=True)
        vec_ref[6:7, :] = jnp.sum(dpx, axis=0, keepdims=True)
        vec_ref[7:8, :] = dl

    col = lambda g: (0, g)
    vec = pl.BlockSpec((1, LANES), col)
    big = pl.BlockSpec((tp, LANES), col)
    sq = pl.BlockSpec((None, LANES, LANES), lambda g: (g, 0, 0))
    return pl.pallas_call(
        body, name="rec_bwd", grid=(ng,),
        in_specs=[big, big, big, big, pl.BlockSpec((tp, LANES), lambda g: (0, ng + g)),
                  pl.BlockSpec((CONV_WIDTH, LANES), col), vec, sq, vec, sq, vec, vec],
        out_specs=[big, big, sq, sq, pl.BlockSpec((None, SUBLANES, LANES), lambda g: (g, 0, 0))],
        out_shape=[jax.ShapeDtypeStruct((tp, rw), BF16), jax.ShapeDtypeStruct((tp, rw), BF16),
                   jax.ShapeDtypeStruct((ng, LANES, LANES), F32), jax.ShapeDtypeStruct((ng, LANES, LANES), F32),
                   jax.ShapeDtypeStruct((ng, SUBLANES, LANES), F32)],
        scratch_shapes=[pltpu.VMEM((tp, LANES), F32)] * 3,
        compiler_params=_cparams("parallel"),
    )(drec, hr, xc, rest, rest, convw, convb, wga, bga, wgx, bgx, lru)


def attn_bwd(qkv, do, o, lset, c, ct, nh):
    tp = qkv.shape[0]
    npair = nh // 2
    aw = nh * HEAD_DIM
    tiles = _att_tiles(tp)

    def body(q_ref, k_ref, v_ref, do_ref, o_ref, lset_ref, c_ref, ct_ref,
             dq_ref, dk_ref, dv_ref, drow_ref, dcol_ref, dk_acc, dv_acc):
        p = pl.program_id(0)
        dk_acc[...] = jnp.zeros_like(dk_acc)
        dv_acc[...] = jnp.zeros_like(dv_acc)
        dcol_ref[...] = jnp.zeros_like(dcol_ref)
        drow_ref[...] = jnp.zeros_like(drow_ref)
        for r0, nr, nk in tiles:
            rs = slice(r0, r0 + nr)
            causal = (r0 + lax.broadcasted_iota(jnp.int32, (nk, nr), 1)
                      >= lax.broadcasted_iota(jnp.int32, (nk, nr), 0))
            cblk = c_ref[0:nk, :]
            ctb = ct_ref[:, rs]
            for hh in range(2):
                head = 2 * p + hh
                hs = slice(hh * HEAD_DIM, (hh + 1) * HEAD_DIM)
                q = q_ref[rs, hs]
                k = k_ref[0:nk, hs]
                dof = do_ref[rs, hs]
                do16 = dof.astype(BF16)
                delta = jnp.sum(dof * o_ref[rs, hs], axis=1, keepdims=True)
                delta_row = jnp.broadcast_to(delta, (nr, LANES)).T[0:1, :]
                s_t = _dot_nt(k, q * ATT_SCALE) + (_pick_row(ctb, head) - _pick_col(cblk, head))
                p_t = jnp.where(causal, jnp.exp(s_t - lset_ref[hh:hh + 1, rs]), 0.0)
                ds_t = p_t * (_dot_nt(v_ref[0:nk, hs], do16) - delta_row)
                p16 = p_t.astype(BF16)
                ds16 = ds_t.astype(BF16)
                dv_acc[0:nk, hs] += _dot(p16, do16)
                dk_acc[0:nk, hs] += _dot(ds16, q) * ATT_SCALE
                dq_ref[rs, hs] = (_dot_tn(ds16, k) * ATT_SCALE).astype(BF16)
                drow_ref[hh:hh + 1, rs] = jnp.sum(ds_t, axis=0, keepdims=True)
                dcol_ref[0:nk, hs] -= jnp.broadcast_to(jnp.sum(ds_t, axis=1, keepdims=True), (nk, HEAD_DIM))
        dk_ref[...] = dk_acc[...].astype(BF16)
        dv_ref[...] = dv_acc[...].astype(BF16)

    pair = lambda p: (0, p)
    return pl.pallas_call(
        body, name="attn_bwd", grid=(npair,),
        in_specs=[pl.BlockSpec((tp, LANES), pair),
                  pl.BlockSpec((tp, LANES), lambda p: (0, npair + p)),
                  pl.BlockSpec((tp, LANES), lambda p: (0, 2 * npair + p)),
                  pl.BlockSpec((tp, LANES), pair),
                  pl.BlockSpec((tp, LANES), pair),
                  pl.BlockSpec((None, SUBLANES, tp), lambda p: (p, 0, 0)),
                  pl.BlockSpec((tp, LANES), lambda p: (0, 0)),
                  pl.BlockSpec((SUBLANES, tp), lambda p: (0, 0))],
        out_specs=[pl.BlockSpec((tp, LANES), pair), pl.BlockSpec((tp, LANES), pair),
                   pl.BlockSpec((tp, LANES), pair),
                   pl.BlockSpec((None, SUBLANES, tp), lambda p: (p, 0, 0)),
                   pl.BlockSpec((tp, LANES), pair)],
        out_shape=[jax.ShapeDtypeStruct((tp, aw), BF16), jax.ShapeDtypeStruct((tp, aw), BF16),
                   jax.ShapeDtypeStruct((tp, aw), BF16),
                   jax.ShapeDtypeStruct((npair, SUBLANES, tp), F32),
                   jax.ShapeDtypeStruct((tp, aw), F32)],
        scratch_shapes=[pltpu.VMEM((tp, LANES), F32), pltpu.VMEM((tp, LANES), F32)],
        compiler_params=_cparams("parallel"),
    )(qkv, qkv, qkv, do, o, lset, c, ct)


def fgate_bwd(dct8, drs, rest, bf_pad, fcol):
    tp = rest.shape[0]
    aw = drs.shape[1]
    nb = tp // ATT_BLOCK
    B = ATT_BLOCK

    def body(d_ref, drs_ref, f_ref, b_ref, dfl_ref, db_ref, pad_s):
        r_i = lax.broadcasted_iota(jnp.int32, (B, B), 0)
        c_i = lax.broadcasted_iota(jnp.int32, (B, B), 1)
        triu = (c_i >= r_i).astype(BF16)
        sel = (lax.broadcasted_iota(jnp.int32, (aw, LANES), 0)
               == HEAD_DIM * lax.broadcasted_iota(jnp.int32, (aw, LANES), 1)).astype(BF16)
        carry = jnp.zeros((1, LANES), F32)
        db = jnp.zeros((1, LANES), F32)
        pad_s[...] = jnp.zeros_like(pad_s)
        for i in range(nb - 1, -1, -1):
            sl = slice(i * B, (i + 1) * B)
            pad_s[0:SUBLANES, :] = d_ref[:, sl]
            dc = pad_s[...].T + _dot_split3(drs_ref[sl, :], sel)
            rc = _split3_dot(triu, dc)
            dlf = rc + carry
            carry = carry + rc[0:1, :]
            dfl = dlf * _sigmoid(-(f_ref[sl, :] + b_ref[...]))
            dfl_ref[sl, :] = dfl.astype(BF16)
            db = db + jnp.sum(dfl, axis=0, keepdims=True)
        db_ref[...] = db

    return pl.pallas_call(
        body, name="fgate_bwd", grid=(1,),
        in_specs=[pl.BlockSpec((SUBLANES, tp), lambda i: (0, 0)),
                  pl.BlockSpec((tp, aw), lambda i: (0, 0)),
                  pl.BlockSpec((tp, LANES), lambda i: (0, fcol)),
                  pl.BlockSpec((1, LANES), lambda i: (0, 0))],
        out_specs=[pl.BlockSpec((tp, LANES), lambda i: (0, 0)),
                   pl.BlockSpec((1, LANES), lambda i: (0, 0))],
        out_shape=[jax.ShapeDtypeStruct((tp, LANES), BF16), jax.ShapeDtypeStruct((1, LANES), F32)],
        scratch_shapes=[pltpu.VMEM((B, B), F32)],
        compiler_params=_cparams("arbitrary"),
    )(dct8, drs, rest, bf_pad)


def in_proj_bwd(dh2, parts, wqkv, wrest, h, g1, tm):
    tp, d = h.shape
    dq, dk, dv, dxr, dyr, dfl = parts
    aw, rw = dq.shape[1], dxr.shape[1]

    def body(dh2_ref, dq_ref, dk_ref, dv_ref, dxr_ref, dyr_ref, dfl_ref, wq_ref, wr_ref, h_ref, g_ref,
             dh_ref, dg_ref):
        i = pl.program_id(0)
        dz = _dot_nt(dq_ref[...], wq_ref[:, 0:aw])
        dz += _dot_nt(dk_ref[...], wq_ref[:, aw:2 * aw])
        dz += _dot_nt(dv_ref[...], wq_ref[:, 2 * aw:3 * aw])
        dz += _dot_nt(dxr_ref[...], wr_ref[:, 0:rw])
        dz += _dot_nt(dyr_ref[...], wr_ref[:, rw:2 * rw])
        dz += _dot_nt(dfl_ref[...], wr_ref[:, 2 * rw:2 * rw + LANES])
        dx, dg = _rms_bwd(h_ref[...], g_ref[...], dz)
        dh_ref[...] = dh2_ref[...] + dx
        _accumulate(dg_ref, dg, i == 0)

    row = lambda i: (i, 0)
    fix = lambda i: (0, 0)
    return pl.pallas_call(
        body, name="in_proj_bwd", grid=(tp // tm,),
        in_specs=[pl.BlockSpec((tm, d), row),
                  pl.BlockSpec((tm, aw), row), pl.BlockSpec((tm, aw), row), pl.BlockSpec((tm, aw), row),
                  pl.BlockSpec((tm, rw), row), pl.BlockSpec((tm, rw), row), pl.BlockSpec((tm, LANES), row),
                  pl.BlockSpec(wqkv.shape, fix), pl.BlockSpec(wrest.shape, fix),
                  pl.BlockSpec((tm, d), row), pl.BlockSpec((1, d), fix)],
        out_specs=[pl.BlockSpec((tm, d), row), pl.BlockSpec((1, d), fix)],
        out_shape=[jax.ShapeDtypeStruct((tp, d), F32), jax.ShapeDtypeStruct((1, d), F32)],
        compiler_params=_cparams("arbitrary"),
    )(dh2, dq, dk, dv, dxr, dyr, dfl, wqkv, wrest, h, g1)


def _place():
    return lax.axis_index("x"), lax.axis_index("y"), lax.axis_index("c")


HBM = pl.BlockSpec(memory_space=pltpu.HBM)
SEM = pl.BlockSpec(memory_space=pltpu.SEMAPHORE)
EFFECT = pltpu.SideEffectType.DATAFLOW_SIDE_EFFECTING


def _in_hbm(a):
    return pltpu.with_memory_space_constraint(a, pltpu.HBM)


def _gather_targets(x, y, c):
    return [(x, y, 1 - c), (1 - x, y, c), (x, 1 - y, c), (1 - x, 1 - y, c)]


def _slot(t):
    return 4 * t[0] + 2 * t[1] + t[2]


def gather_start(groups):
    flat = [a for g in groups for a in g]
    n = len(flat)
    ng = len(groups)
    lands = [lax.empty((N_DEV,) + a.shape, a.dtype) for a in flat]

    def body(*refs):
        src, land = refs[:n], refs[n:2 * n]
        sems = refs[2 * n:2 * n + 2 * ng]
        token = refs[-1]
        x, y, c = _place()
        me = 4 * x + 2 * y + c
        i = 0
        for gi, g in enumerate(groups):
            for a in range(len(g)):
                for k, t in enumerate(_gather_targets(x, y, c)):
                    pltpu.make_async_remote_copy(
                        src_ref=src[i], dst_ref=land[i].at[me],
                        send_sem=sems[2 * gi].at[4 * a + k], recv_sem=sems[2 * gi + 1].at[4 * a + k],
                        device_id=t, device_id_type=MESH).start()
                i += 1
        token[...] = jnp.zeros_like(token)

    sem_shapes = []
    for g in groups:
        sem_shapes += [pltpu.SemaphoreType.DMA((4 * len(g),)), pltpu.SemaphoreType.DMA((4 * len(g),))]
    out = pl.pallas_call(
        body, name="gather_start",
        out_shape=sem_shapes + [pltpu.HBM(a.shape, a.dtype) for a in flat + lands]
        + [jax.ShapeDtypeStruct((SUBLANES, LANES), F32)],
        in_specs=[HBM] * (2 * n),
        out_specs=[SEM] * (2 * ng) + [HBM] * (2 * n) + [pl.BlockSpec(memory_space=pltpu.VMEM)],
        input_output_aliases={i: 2 * ng + i for i in range(2 * n)},
        compiler_params=pltpu.CompilerParams(has_side_effects=EFFECT),
    )(*[_in_hbm(a) for a in flat + lands])
    sems = out[:2 * ng]
    thru = out[2 * ng:2 * ng + 2 * n]
    srcs_t, lands_t = thru[:n], thru[n:]
    res, i = [], 0
    for gi, g in enumerate(groups):
        res.append((sems[2 * gi], sems[2 * gi + 1], srcs_t[i:i + len(g)], lands_t[i:i + len(g)]))
        i += len(g)
    return res, out[-1]


def gather_wait(send, recv, srcs, lands, after, name):
    n = len(srcs)

    def body(*refs):
        src, land = refs[:n], refs[n:2 * n]
        send_sem, recv_sem = refs[2 * n], refs[2 * n + 1]
        x, y, c = _place()
        for a in range(n):
            for k, t in enumerate(_gather_targets(x, y, c)):
                cp = pltpu.make_async_remote_copy(
                    src_ref=src[a], dst_ref=land[a].at[_slot(t)],
                    send_sem=send_sem.at[4 * a + k], recv_sem=recv_sem.at[4 * a + k],
                    device_id=t, device_id_type=MESH)
                cp.wait_send()
                cp.wait_recv()

    out = pl.pallas_call(
        body, name=name,
        out_shape=[pltpu.HBM(a.shape, a.dtype) for a in list(srcs) + list(lands)],
        in_specs=[HBM] * (2 * n) + [SEM, SEM, ANY],
        out_specs=[HBM] * (2 * n),
        input_output_aliases={i: i for i in range(2 * n)},
        compiler_params=pltpu.CompilerParams(has_side_effects=EFFECT),
    )(*srcs, *lands, send, recv, after)
    return out[:n], out[n:]


def gather_forward(srcs, lands, name):
    n = len(srcs)

    def body(*refs):
        src, land, out = refs[:n], refs[n:2 * n], refs[2 * n:3 * n]
        send_sems, recv_sems, local_sems = refs[3 * n:]
        x, y, c = _place()
        sibling = (x, y, 1 - c)
        chips = [(1 - x, y), (x, 1 - y), (1 - x, 1 - y)]
        mine = [pltpu.make_async_copy(src[a], out[a].at[_slot((x, y, c))], local_sems.at[a]) for a in range(n)]
        for cp in mine:
            cp.start()

        def fwd(a, j, core):
            s = _slot((*chips[j], core))
            return pltpu.make_async_remote_copy(src_ref=land[a].at[s], dst_ref=out[a].at[s],
                                                send_sem=send_sems.at[a, j], recv_sem=recv_sems.at[a, j],
                                                device_id=sibling, device_id_type=MESH)

        sends = [fwd(a, j, c) for a in range(n) for j in range(3)]
        for cp in sends:
            cp.start()
        for a in range(n):
            for j in range(3):
                fwd(a, j, 1 - c).wait_recv()
        for cp in sends:
            cp.wait_send()
        for cp in mine:
            cp.wait()

    out = pl.pallas_call(
        body, name=name,
        in_specs=[ANY] * (2 * n), out_specs=[ANY] * n,
        out_shape=[jax.ShapeDtypeStruct(a.shape, a.dtype) for a in lands],
        input_output_aliases={n + i: i for i in range(n)},
        scratch_shapes=[pltpu.SemaphoreType.DMA((n, 3)), pltpu.SemaphoreType.DMA((n, 3)),
                        pltpu.SemaphoreType.DMA((n,))],
    )(*srcs, *lands)
    return out


def _relations():
    return [(dx, dy, dc) for dx in (0, 1) for dy in (0, 1) for dc in (0, 1) if dx + dy + dc]


def _peer(x, y, c, rel):
    return ((1 - x) if rel[0] else x, (1 - y) if rel[1] else y, (1 - c) if rel[2] else c)


def exchange_start(srcs, lands, layer):
    n = len(srcs)

    def body(*refs):
        src, land = refs[:n], refs[n:2 * n]
        send_sem, recv_sem = refs[2 * n], refs[2 * n + 1]
        token = refs[-1]
        x, y, c = _place()
        me = 4 * x + 2 * y + c
        for k, rel in enumerate(_relations()):
            peer = _peer(x, y, c, rel)
            for a in range(n):
                pltpu.make_async_remote_copy(
                    src_ref=src[a].at[_slot(peer)], dst_ref=land[a].at[me, layer],
                    send_sem=send_sem.at[7 * a + k], recv_sem=recv_sem.at[7 * a + k],
                    device_id=peer, device_id_type=MESH).start()
        token[...] = jnp.zeros_like(token)

    out = pl.pallas_call(
        body, name="exchange_start_%d" % layer,
        out_shape=[pltpu.SemaphoreType.DMA((7 * n,)), pltpu.SemaphoreType.DMA((7 * n,))]
        + [pltpu.HBM(a.shape, a.dtype) for a in list(srcs) + list(lands)]
        + [jax.ShapeDtypeStruct((SUBLANES, LANES), F32)],
        in_specs=[HBM] * (2 * n),
        out_specs=[SEM, SEM] + [HBM] * (2 * n) + [pl.BlockSpec(memory_space=pltpu.VMEM)],
        input_output_aliases={i: 2 + i for i in range(2 * n)},
        compiler_params=pltpu.CompilerParams(has_side_effects=EFFECT),
    )(*[_in_hbm(a) for a in list(srcs) + list(lands)])
    return out[0], out[1], out[2:2 + n], out[2 + n:2 + 2 * n], out[-1]


def exchange_wait(send, recv, srcs, lands, after, layer):
    n = len(srcs)

    def body(*refs):
        src, land = refs[:n], refs[n:2 * n]
        send_sem, recv_sem = refs[2 * n], refs[2 * n + 1]
        x, y, c = _place()
        for k, rel in enumerate(_relations()):
            peer = _peer(x, y, c, rel)
            for a in range(n):
                cp = pltpu.make_async_remote_copy(
                    src_ref=src[a].at[_slot(peer)], dst_ref=land[a].at[_slot(peer), layer],
                    send_sem=send_sem.at[7 * a + k], recv_sem=recv_sem.at[7 * a + k],
                    device_id=peer, device_id_type=MESH)
                cp.wait_send()
                cp.wait_recv()

    out = pl.pallas_call(
        body, name="exchange_wait_%d" % layer,
        out_shape=[pltpu.HBM(a.shape, a.dtype) for a in list(srcs) + list(lands)],
        in_specs=[HBM] * (2 * n) + [SEM, SEM, ANY],
        out_specs=[HBM] * (2 * n),
        input_output_aliases={i: i for i in range(2 * n)},
        compiler_params=pltpu.CompilerParams(has_side_effects=EFFECT),
    )(*srcs, *lands, send, recv, after)
    return out[:n], out[n:]


def place_own(per_layer, lands):
    nw = len(lands)
    nl = len(per_layer)
    flat = [a for lay in per_layer for a in lay]
    n = len(flat)

    def body(*refs):
        src, land = refs[:n], refs[n + nw:n + 2 * nw]
        sems = refs[-1]
        x, y, c = _place()
        me = 4 * x + 2 * y + c
        cps = []
        for l in range(nl):
            for w in range(nw):
                cps.append(pltpu.make_async_copy(src[l * nw + w].at[me], land[w].at[me, l], sems.at[l * nw + w]))
        for cp in cps:
            cp.start()
        for cp in cps:
            cp.wait()

    return pl.pallas_call(
        body, name="place_own",
        in_specs=[ANY] * (n + nw), out_specs=[ANY] * nw,
        out_shape=[jax.ShapeDtypeStruct(a.shape, a.dtype) for a in lands],
        input_output_aliases={n + i: i for i in range(nw)},
        scratch_shapes=[pltpu.SemaphoreType.DMA((n,))],
    )(*flat, *lands)


def all_gather(arrs, name):
    n = len(arrs)

    def body(*refs):
        ins, outs = refs[:n], refs[n:2 * n]
        send_sems, recv_sems, local_sems = refs[2 * n:]
        x, y, c = _place()
        me, sibling = (x, y, c), (x, y, 1 - c)
        chips = [(1 - x, y), (x, 1 - y), (1 - x, 1 - y)]

        def slot(a, blk):
            return outs[a].at[_slot(blk)]

        def copy(a, k, blk, to, src=None):
            return pltpu.make_async_remote_copy(
                src_ref=slot(a, blk) if src is None else src, dst_ref=slot(a, blk),
                send_sem=send_sems.at[a, k], recv_sem=recv_sems.at[a, k],
                device_id=to, device_id_type=MESH)

        mine = [pltpu.make_async_copy(ins[a], slot(a, me), local_sems.at[a]) for a in range(n)]
        for cp in mine:
            cp.start()
        first = []
        for a in range(n):
            first.append(copy(a, 0, me, sibling, src=ins[a]))
            first += [copy(a, 1 + j, me, (*chip, c), src=ins[a]) for j, chip in enumerate(chips)]
        for cp in first:
            cp.start()
        passed = []
        for j, chip in enumerate(chips):
            for a in range(n):
                copy(a, 1 + j, (*chip, c), me).wait_recv()
                fwd = copy(a, 4 + j, (*chip, c), sibling)
                fwd.start()
                passed.append(fwd)
        for a in range(n):
            copy(a, 0, sibling, me).wait_recv()
            for j, chip in enumerate(chips):
                copy(a, 4 + j, (*chip, 1 - c), me).wait_recv()
        for cp in first + passed:
            cp.wait_send()
        for cp in mine:
            cp.wait()

    return pl.pallas_call(
        body, name=name,
        in_specs=[ANY] * n, out_specs=[ANY] * n,
        out_shape=[jax.ShapeDtypeStruct((N_DEV,) + a.shape, a.dtype) for a in arrs],
        scratch_shapes=[pltpu.SemaphoreType.DMA((n, 7)), pltpu.SemaphoreType.DMA((n, 7)),
                        pltpu.SemaphoreType.DMA((n,))],
    )(*arrs)


def _adamw_math(g, w, m, v):
    m = ADAM_B1 * m + (1.0 - ADAM_B1) * g
    v = ADAM_B2 * v + (1.0 - ADAM_B2) * (g * g)
    m_hat = m / (1.0 - ADAM_B1 ** ADAM_STEP)
    v_hat = v / (1.0 - ADAM_B2 ** ADAM_STEP)
    delta = -ADAM_LR * (m_hat / (jnp.sqrt(v_hat) + ADAM_EPS) + ADAM_WD * w)
    return delta, m, v


def sum_adamw(parts, w, m, v, tr, name):
    npart, rows, cols = parts.shape

    def body(p_ref, w_ref, m_ref, v_ref, g_ref, d_ref, nm_ref, nv_ref):
        g = p_ref[0].astype(F32)
        for p in range(1, npart):
            g = g + p_ref[p].astype(F32)
        delta, nm, nv = _adamw_math(g, w_ref[...], m_ref[...], v_ref[...])
        g_ref[...] = g
        d_ref[...] = delta
        nm_ref[...] = nm
        nv_ref[...] = nv

    blk = pl.BlockSpec((tr, cols), lambda i: (i, 0))
    return pl.pallas_call(
        body, name=name, grid=(rows // tr,),
        in_specs=[pl.BlockSpec((npart, tr, cols), lambda i: (0, i, 0)), blk, blk, blk],
        out_specs=[blk] * 4,
        out_shape=[jax.ShapeDtypeStruct((rows, cols), F32)] * 4,
        compiler_params=_cparams("parallel"),
    )(parts, w, m, v)


def sum_parts(parts, name):
    npart, rows, cols = parts.shape

    def body(p_ref, g_ref):
        g = p_ref[0]
        for p in range(1, npart):
            g = g + p_ref[p]
        g_ref[...] = g

    return pl.pallas_call(
        body, name=name, grid=(1,),
        in_specs=[pl.BlockSpec((npart, rows, cols), lambda i: (0, 0, 0))],
        out_specs=pl.BlockSpec((rows, cols), lambda i: (0, 0)),
        out_shape=jax.ShapeDtypeStruct((rows, cols), F32),
        compiler_params=_cparams("arbitrary"),
    )(parts)


def _round_up(n, m):
    return (n + m - 1) // m * m


def _block_diag_pairs(w):
    nb, b, _ = w.shape
    per = LANES // b
    ng = nb // per
    w = w.reshape(ng, per, b, b)
    eye = jnp.eye(per, dtype=w.dtype)
    out = jnp.einsum('gpij,pq->gpiqj', w, eye).reshape(ng, LANES, LANES)
    return out.astype(BF16)


def _block_diag_extract(g, b):
    ng = g.shape[0]
    per = LANES // b
    g = g.reshape(ng, per, b, per, b)
    idx = jnp.arange(per)
    return g[:, idx, :, idx, :].transpose(1, 0, 2, 3).reshape(ng * per, b, b)


def _tiles(v):
    v = v.reshape(-1)
    n = _round_up(v.shape[0], SUBLANES * LANES)
    return jnp.pad(v, (0, n - v.shape[0])).reshape(-1, LANES)


SMALL = ['attn_norm_g', 'b_f', 'conv_w', 'conv_b', 'w_gate_a', 'b_gate_a', 'w_gate_x', 'b_gate_x',
         'lru_L', 'attn_out_g', 'rec_out_g', 'mlp_norm_g', 'final_g', 'meta']


def _pack(d):
    return jnp.concatenate([_tiles(d[n]) for n in SMALL], axis=0)


def _unpack(vec, shapes):
    out, r = {}, 0
    for n in SMALL:
        size = math.prod(shapes[n])
        nr = _round_up(size, SUBLANES * LANES) // LANES
        out[n] = vec[r:r + nr].reshape(-1)[:size].reshape(shapes[n])
        r += nr
    return out


def _row_tile(tp):
    return tp // 4 if (tp // 4) % 16 == 0 else tp


def local_step(x, tgt, meta, small, weights_of, grads_ready):
    s, d = x.shape
    t_real = s + N_META
    tp = _round_up(t_real, ATT_BLOCK)
    depth = small['attn_norm_g'].shape[0]
    nh = small['b_f'].shape[1]
    rw = small['conv_b'].shape[1]
    aw = nh * HEAD_DIM
    blk = small['w_gate_a'].shape[2]
    tm = _row_tile(tp)
    tm2 = tp // 2
    fcol = 2 * rw // LANES

    h = jnp.concatenate([meta, x, jnp.zeros((tp - t_real, d), F32)], axis=0)
    tgt_p = jnp.pad(tgt, ((N_META, tp - t_real), (0, 0)))
    row = lambda v: v.reshape(1, -1)
    bf_pad = jnp.pad(small['b_f'], ((0, 0), (0, LANES - nh)))

    saved = []
    for l in range(depth):
        wqkv, wrest, wout, gup, gdown = weights_of(l, h)
        wga = _block_diag_pairs(small['w_gate_a'][l])
        wgx = _block_diag_pairs(small['w_gate_x'][l])
        z, qkv, rest = in_proj(h, row(small['attn_norm_g'][l]), wqkv, wrest, tm)
        c, ct = fgate_fwd(rest, bf_pad[l:l + 1], fcol)
        o, lset = attn_fwd(qkv, c, ct, nh)
        rec, hr, xc = rec_fwd(rest, small['conv_w'][l], row(small['conv_b'][l]), wga, row(small['b_gate_a'][l]),
                              wgx, row(small['b_gate_x'][l]), row(small['lru_L'][l]), rw)
        h2, mix, z2 = out_proj(h, o, rec, row(small['attn_out_g'][l]), row(small['rec_out_g'][l]), wout,
                               row(small['mlp_norm_g'][l]), tm)
        u, h3 = mlp_fwd(z2, h2, gup, gdown, tm2)
        saved.append(dict(h=h, z=z, qkv=qkv, rest=rest, c=c, ct=ct, o=o, lset=lset, rec=rec, hr=hr, xc=xc,
                          h2=h2, mix=mix, z2=z2, u=u, wga=wga, wgx=wgx,
                          wqkv=wqkv, wrest=wrest, wout=wout, gup=gup, gdown=gdown))
        h = h3

    dh, dgf, loss = loss_head(h, row(small['final_g']), tgt_p, t_real, tm)

    gs = {n: [None] * depth for n in SMALL if n not in ('final_g', 'meta')}
    tok = jnp.zeros((), F32)
    for l in reversed(range(depth)):
        sv = saved[l]
        gup, gdown = sv['gup'], sv['gdown']
        tf = gup.shape[2]
        dup, dh2, dg2 = mlp_bwd(dh, sv['u'], sv['h2'], row(small['mlp_norm_g'][l]) + tok, gup, gdown, tm2)
        big = {}
        big['w_down'] = mm_tn(sv['u'], dh, tk=tf, tn=d, tr=tm, out_dtype=BF16, name="dw_down",
                              square_a=True).reshape(N_DEV, tf, d)
        big['w_up'] = mm_tn(sv['z2'], dup, tk=d, tn=tf, tr=tm, out_dtype=BF16, name="dw_up", blocked_n=True)
        gs['mlp_norm_g'][l] = dg2[0]
        do, drec, dga, dgr = out_proj_bwd(dh2, sv['o'], sv['rec'], row(small['attn_out_g'][l]),
                                          row(small['rec_out_g'][l]), sv['wout'], tm)
        big['w_out'] = mm_tn(sv['mix'], dh2, tk=d, tn=d // 2, tr=tm, out_dtype=BF16,
                             name="dw_out").reshape(N_DEV, d // N_DEV, d)
        gs['attn_out_g'][l] = dga[0]
        gs['rec_out_g'][l] = dgr[0]
        dxr, dyr, dwga, dwgx, vec = rec_bwd(drec, sv['hr'], sv['xc'], sv['rest'], small['conv_w'][l],
                                            row(small['conv_b'][l]), sv['wga'], row(small['b_gate_a'][l]),
                                            sv['wgx'], row(small['b_gate_x'][l]), row(small['lru_L'][l]), rw)
        gs['w_gate_a'][l] = _block_diag_extract(dwga, blk)
        gs['w_gate_x'][l] = _block_diag_extract(dwgx, blk)
        vec = vec.transpose(1, 0, 2).reshape(SUBLANES, rw)
        gs['conv_w'][l] = vec[0:CONV_WIDTH]
        gs['conv_b'][l] = vec[4]
        gs['b_gate_a'][l] = vec[5]
        gs['b_gate_x'][l] = vec[6]
        gs['lru_L'][l] = vec[7]
        dq, dk, dv, drow, dcol = attn_bwd(sv['qkv'], do, sv['o'], sv['lset'], sv['c'], sv['ct'], nh)
        drow8 = drow[:, 0:2, :].reshape(nh, tp)
        if nh < SUBLANES:
            drow8 = jnp.pad(drow8, ((0, SUBLANES - nh), (0, 0)))
        dfl, dbf = fgate_bwd(drow8, dcol, sv['rest'], bf_pad[l:l + 1], fcol)
        gs['b_f'][l] = dbf[0, 0:nh]
        parts = (dq, dk, dv, dxr, dyr, dfl)
        dh, dg1 = in_proj_bwd(dh2, parts, sv['wqkv'], sv['wrest'], sv['h'], row(small['attn_norm_g'][l]), tm)
        gs['attn_norm_g'][l] = dg1[0]
        dws = [mm_tn(sv['z'], pt, tk=d, tn=pt.shape[1], tr=tm, out_dtype=BF16, name="dw_in_%d" % i)
               for i, pt in enumerate(parts)]
        dw_in = jnp.concatenate([dws[0], dws[1], dws[2], dws[5][:, 0:nh], dws[3], dws[4]], axis=1)
        big['w_in'] = dw_in.reshape(d, N_DEV, -1).transpose(1, 0, 2)
        tok = grads_ready(l, big)

    grads = {n: jnp.stack(v) for n, v in gs.items()}
    grads['final_g'] = dgf[0]
    grads['meta'] = dh[0:N_META]
    return loss[0, 0], dh, grads, tok


def prep_weights(g_in, g_out, nh, rw):
    _, d, _ = g_in.shape
    w_in = g_in.transpose(1, 0, 2).reshape(d, -1)
    aw = nh * HEAD_DIM
    wqkv = w_in[:, 0:3 * aw]
    f0 = 3 * aw
    wrest = jnp.concatenate([w_in[:, f0 + nh:f0 + nh + 2 * rw], w_in[:, f0:f0 + nh],
                             jnp.zeros((d, LANES - nh), w_in.dtype)], axis=1)
    wout = g_out.reshape(d, d)
    return wqkv, wrest, wout


BIG = ['w_in', 'w_out', 'w_up', 'w_down']
WEIGHTS = ['meta', 'attn_norm_g', 'w_in', 'b_f', 'conv_w', 'conv_b', 'w_gate_a', 'b_gate_a', 'w_gate_x', 'b_gate_x',
           'lru_L', 'attn_out_g', 'rec_out_g', 'w_out', 'mlp_norm_g', 'w_up', 'w_down', 'final_g']


def kernel(x, meta, attn_norm_g, w_in, b_f, conv_w, conv_b, w_gate_a, b_gate_a, w_gate_x, b_gate_x, lru_L, attn_out_g, rec_out_g, w_out, mlp_norm_g, w_up, w_down, final_g, loss_target, m_meta, m_attn_norm_g, m_w_in, m_b_f, m_conv_w, m_conv_b, m_w_gate_a, m_b_gate_a, m_w_gate_x, m_b_gate_x, m_lru_L, m_attn_out_g, m_rec_out_g, m_w_out, m_mlp_norm_g, m_w_up, m_w_down, m_final_g, v_meta, v_attn_norm_g, v_w_in, v_b_f, v_conv_w, v_conv_b, v_w_gate_a, v_b_gate_a, v_w_gate_x, v_b_gate_x, v_lru_L, v_attn_out_g, v_rec_out_g, v_w_out, v_mlp_norm_g, v_w_up, v_w_down, v_final_g):
    w = dict(meta=meta, attn_norm_g=attn_norm_g, w_in=w_in, b_f=b_f, conv_w=conv_w, conv_b=conv_b,
             w_gate_a=w_gate_a, b_gate_a=b_gate_a, w_gate_x=w_gate_x, b_gate_x=b_gate_x, lru_L=lru_L,
             attn_out_g=attn_out_g, rec_out_g=rec_out_g, w_out=w_out, mlp_norm_g=mlp_norm_g, w_up=w_up,
             w_down=w_down, final_g=final_g)
    mo = dict(meta=m_meta, attn_norm_g=m_attn_norm_g, w_in=m_w_in, b_f=m_b_f, conv_w=m_conv_w, conv_b=m_conv_b,
              w_gate_a=m_w_gate_a, b_gate_a=m_b_gate_a, w_gate_x=m_w_gate_x, b_gate_x=m_b_gate_x, lru_L=m_lru_L,
              attn_out_g=m_attn_out_g, rec_out_g=m_rec_out_g, w_out=m_w_out, mlp_norm_g=m_mlp_norm_g,
              w_up=m_w_up, w_down=m_w_down, final_g=m_final_g)
    vo = dict(meta=v_meta, attn_norm_g=v_attn_norm_g, w_in=v_w_in, b_f=v_b_f, conv_w=v_conv_w, conv_b=v_conv_b,
              w_gate_a=v_w_gate_a, b_gate_a=v_b_gate_a, w_gate_x=v_w_gate_x, b_gate_x=v_b_gate_x, lru_L=v_lru_L,
              attn_out_g=v_attn_out_g, rec_out_g=v_rec_out_g, w_out=v_w_out, mlp_norm_g=v_mlp_norm_g,
              w_up=v_w_up, w_down=v_w_down, final_g=v_final_g)
    depth = w_in.shape[0]
    nh = b_f.shape[1]
    rw = conv_b.shape[1]
    me = 4 * lax.axis_index("x") + 2 * lax.axis_index("y") + lax.axis_index("c")

    groups = [[w[n][l].astype(BF16) for n in BIG] for l in range(depth)]
    groups[0] = groups[0] + [meta, conv_w]
    pending, _ = gather_start(groups)
    gathered = {}

    def gathered_group(l, after):
        if l not in gathered:
            send, recv, srcs, lands = pending[l]
            srcs, lands = gather_wait(send, recv, srcs, lands, after, "gather_wait_%d" % l)
            gathered[l] = gather_forward(srcs, lands, "gather_forward_%d" % l)
        return gathered[l]

    def weights_of(l, after):
        g = gathered_group(l, after)
        wqkv, wrest, wout = prep_weights(g[0], g[1], nh, rw)
        return wqkv, wrest, wout, g[2], g[3]

    g0 = gathered_group(0, meta)
    meta_full = g0[4].transpose(1, 0, 2).reshape(N_META, -1)
    conv_full = g0[5].transpose(1, 2, 0, 3).reshape(depth, CONV_WIDTH, rw)
    small = {n: w[n] for n in SMALL}
    small['conv_w'] = conv_full

    lands = [lax.empty((N_DEV,) + w[n].shape, BF16) for n in BIG]
    started = {}

    def grads_ready(l, blocks):
        nonlocal lands
        send, recv, srcs, lands, token = exchange_start([blocks[n] for n in BIG], lands, l)
        started[l] = (send, recv, srcs)
        return token[0, 0]

    loss_part, dh0, grads, tok = local_step(x[0], loss_target[0], meta_full, small, weights_of, grads_ready)
    loss = lax.psum(loss_part, ("x", "y", "c"))
    grad_x = dh0[N_META:N_META + x.shape[1]][None]

    (g_small,) = all_gather([_pack(grads) + tok], "gather_small_grads")
    gsum = _unpack(sum_parts(g_small, "sum_small_grads"), {n: grads[n].shape for n in SMALL})
    gsum['meta'] = lax.dynamic_slice_in_dim(gsum['meta'], me * meta.shape[1], meta.shape[1], axis=1)
    gsum['conv_w'] = lax.dynamic_slice_in_dim(gsum['conv_w'], me * conv_w.shape[2], conv_w.shape[2], axis=2)
    packed = [_pack(t) for t in (gsum, {n: w[n] for n in SMALL}, {n: mo[n] for n in SMALL},
                                 {n: vo[n] for n in SMALL})]
    res = sum_adamw(packed[0][None], packed[1], packed[2], packed[3], packed[1].shape[0], "adamw_small")
    out_g, out_d, out_m, out_v = {}, {}, {}, {}
    shapes = {n: w[n].shape for n in SMALL}
    for dst, vec in zip((out_g, out_d, out_m, out_v), res):
        dst.update(_unpack(vec, shapes))

    own = [None] * depth
    for l in reversed(range(depth)):
        send, recv, srcs = started[l]
        own[l], lands = exchange_wait(send, recv, srcs, lands, res[0], l)
    lands = place_own(own, lands)
    for n, r in zip(BIG, lands):
        shp = w[n].shape
        rows, cols = shp[0] * shp[1], shp[2]
        tr = min(512 if cols <= 512 else 256, rows)
        out = sum_adamw(r.reshape(N_DEV, rows, cols), w[n].reshape(rows, cols), mo[n].reshape(rows, cols),
                        vo[n].reshape(rows, cols), tr, "adamw_" + n)
        out_g[n], out_d[n], out_m[n], out_v[n] = [a.reshape(shp) for a in out]

    return (loss, grad_x, *[out_g[n] for n in WEIGHTS], *[out_d[n] for n in WEIGHTS],
            *[out_m[n] for n in WEIGHTS], *[out_v[n] for n in WEIGHTS])
```

```python
import functools
import math

import jax
import jax.numpy as jnp
from jax import lax
from jax.experimental import pallas as pl
from jax.experimental.pallas import tpu as pltpu

F32 = jnp.float32
BF16 = jnp.bfloat16

N_DEV = 8
N_META = 16
HEAD_DIM = 64
CONV_WIDTH = 4
RG_C = 8.0
NORM_EPS = 1e-6
LANES = 128
SUBLANES = 8
ATT_BLOCK = 128
ATT_TQ = 512
NEG_BIG = -1e30
ATT_SCALE = 1.0 / math.sqrt(HEAD_DIM)

ADAM_LR = 0.001
ADAM_B1 = 0.9
ADAM_B2 = 0.999
ADAM_EPS = 1e-08
ADAM_WD = 0.01
ADAM_STEP = 10

VMEM_LIMIT_BYTES = 56 * 1024 * 1024
MESH = pl.DeviceIdType.MESH
ANY = pl.BlockSpec(memory_space=pl.ANY)


def _cparams(*sem):
    return pltpu.CompilerParams(dimension_semantics=sem if sem else None,
                                vmem_limit_bytes=VMEM_LIMIT_BYTES)


def _dot(a, b):
    return jnp.dot(a, b, preferred_element_type=F32)


def _dot_nt(a, b):
    return lax.dot_general(a, b, (((1,), (1,)), ((), ())), preferred_element_type=F32)


def _dot_tn(a, b):
    return lax.dot_general(a, b, (((0,), (0,)), ((), ())), preferred_element_type=F32)


def _sigmoid(x):
    return 1.0 / (1.0 + jnp.exp(-x))


def _log_sigmoid(x):
    return jnp.minimum(x, 0.0) - jnp.log(1.0 + jnp.exp(-jnp.abs(x)))


def _expm1(x):
    series = x * (1.0 + x * (0.5 + x * (1.0 / 6.0 + x * (1.0 / 24.0))))
    return jnp.where(jnp.abs(x) < 1e-2, series, jnp.exp(x) - 1.0)


_GELU_K = math.sqrt(2.0 / math.pi)
_GELU_C = 0.044715


def _gelu(x):
    t = jnp.tanh(_GELU_K * (x + _GELU_C * x * x * x))
    return 0.5 * x * (1.0 + t)


def _gelu_grad(x):
    t = jnp.tanh(_GELU_K * (x + _GELU_C * x * x * x))
    return 0.5 * (1.0 + t) + 0.5 * x * (1.0 - t * t) * _GELU_K * (1.0 + 3.0 * _GELU_C * x * x)


def _split3_dot(tri, x):
    hi = x.astype(BF16)
    r1 = x - hi.astype(F32)
    mid = r1.astype(BF16)
    lo = (r1 - mid.astype(F32)).astype(BF16)
    return _dot(tri, hi) + _dot(tri, mid) + _dot(tri, lo)


def _dot_split3(x, sel):
    hi = x.astype(BF16)
    r1 = x - hi.astype(F32)
    mid = r1.astype(BF16)
    lo = (r1 - mid.astype(F32)).astype(BF16)
    return _dot(hi, sel) + _dot(mid, sel) + _dot(lo, sel)


def _rms_fwd(x, g):
    r = lax.rsqrt(jnp.mean(x * x, axis=-1, keepdims=True) + NORM_EPS)
    return x * r * g


def _rms_bwd(x, g, dy):
    r = lax.rsqrt(jnp.mean(x * x, axis=-1, keepdims=True) + NORM_EPS)
    xn = x * r
    dxn = dy * g
    dx = r * (dxn - xn * jnp.mean(dxn * xn, axis=-1, keepdims=True))
    return dx, jnp.sum(dy * xn, axis=0, keepdims=True)


def _accumulate(ref, val, first):
    @pl.when(first)
    def _():
        ref[...] = val

    @pl.when(jnp.logical_not(first))
    def _():
        ref[...] += val


def in_proj(h, g1, wqkv, wrest, tm):
    tp, d = h.shape
    nq, nr = wqkv.shape[1], wrest.shape[1]

    def body(h_ref, g_ref, wq_ref, wr_ref, z_ref, qkv_ref, rest_ref):
        z = _rms_fwd(h_ref[...], g_ref[...]).astype(BF16)
        z_ref[...] = z
        qkv_ref[...] = _dot(z, wq_ref[...]).astype(BF16)
        rest_ref[...] = _dot(z, wr_ref[...])

    return pl.pallas_call(
        body, name="in_proj", grid=(tp // tm,),
        in_specs=[pl.BlockSpec((tm, d), lambda i: (i, 0)),
                  pl.BlockSpec((1, d), lambda i: (0, 0)),
                  pl.BlockSpec((d, nq), lambda i: (0, 0)),
                  pl.BlockSpec((d, nr), lambda i: (0, 0))],
        out_specs=[pl.BlockSpec((tm, d), lambda i: (i, 0)),
                   pl.BlockSpec((tm, nq), lambda i: (i, 0)),
                   pl.BlockSpec((tm, nr), lambda i: (i, 0))],
        out_shape=[jax.ShapeDtypeStruct((tp, d), BF16),
                   jax.ShapeDtypeStruct((tp, nq), BF16),
                   jax.ShapeDtypeStruct((tp, nr), F32)],
        compiler_params=_cparams("parallel"),
    )(h, g1, wqkv, wrest)


def fgate_fwd(rest, bf_pad, fcol):
    tp = rest.shape[0]
    nb = tp // ATT_BLOCK

    def body(f_ref, b_ref, c_ref, ct_ref):
        r_i = lax.broadcasted_iota(jnp.int32, (ATT_BLOCK, ATT_BLOCK), 0)
        c_i = lax.broadcasted_iota(jnp.int32, (ATT_BLOCK, ATT_BLOCK), 1)
        tri = (r_i >= c_i).astype(BF16)
        carry = jnp.zeros((1, LANES), F32)
        for i in range(nb):
            sl = slice(i * ATT_BLOCK, (i + 1) * ATT_BLOCK)
            lf = _log_sigmoid(f_ref[sl, :] + b_ref[...])
            cs = _split3_dot(tri, lf) + carry
            carry = cs[ATT_BLOCK - 1:ATT_BLOCK, :]
            c_ref[sl, :] = cs
            ct_ref[:, sl] = cs.T[0:SUBLANES, :]

    return pl.pallas_call(
        body, name="fgate_fwd", grid=(1,),
        in_specs=[pl.BlockSpec((tp, LANES), lambda i: (0, fcol)),
                  pl.BlockSpec((1, LANES), lambda i: (0, 0))],
        out_specs=[pl.BlockSpec((tp, LANES), lambda i: (0, 0)),
                   pl.BlockSpec((SUBLANES, tp), lambda i: (0, 0))],
        out_shape=[jax.ShapeDtypeStruct((tp, LANES), F32),
                   jax.ShapeDtypeStruct((SUBLANES, tp), F32)],
        compiler_params=_cparams("arbitrary"),
    )(rest, bf_pad)


def _pick_col(blk, head):
    lane = lax.broadcasted_iota(jnp.int32, blk.shape, 1)
    return jnp.sum(jnp.where(lane == head, blk, 0.0), axis=1, keepdims=True)


def _pick_row(blk, head):
    sub = lax.broadcasted_iota(jnp.int32, blk.shape, 0)
    return jnp.sum(jnp.where(sub == head, blk, 0.0), axis=0, keepdims=True)


def _att_tiles(tp):
    out, r0 = [], 0
    while r0 < tp:
        rows = min(ATT_TQ, tp - r0)
        out.append((r0, rows, r0 + rows))
        r0 += rows
    return out


def attn_fwd(qkv, c, ct, nh):
    tp = qkv.shape[0]
    npair = nh // 2
    tiles = _att_tiles(tp)

    def body(q_ref, k_ref, v_ref, c_ref, ct_ref, o_ref, lset_ref):
        p = pl.program_id(0)
        lset_ref[...] = jnp.zeros_like(lset_ref)
        for r0, nr, nk in tiles:
            rs = slice(r0, r0 + nr)
            causal = (r0 + lax.broadcasted_iota(jnp.int32, (nr, nk), 0)
                      >= lax.broadcasted_iota(jnp.int32, (nr, nk), 1))
            cblk = c_ref[rs, :]
            ctb = ct_ref[:, 0:nk]
            for hh in range(2):
                head = 2 * p + hh
                hs = slice(hh * HEAD_DIM, (hh + 1) * HEAD_DIM)
                q = q_ref[rs, hs] * ATT_SCALE
                s = _dot_nt(q, k_ref[0:nk, hs]) + (_pick_col(cblk, head) - _pick_row(ctb, head))
                s = jnp.where(causal, s, NEG_BIG)
                m = jnp.max(s, axis=1, keepdims=True)
                pm = jnp.exp(s - m)
                l = jnp.sum(pm, axis=1, keepdims=True)
                o_ref[rs, hs] = _dot(pm.astype(BF16), v_ref[0:nk, hs]) / l
                lse = m + jnp.log(l)
                lset_ref[hh:hh + 1, rs] = jnp.broadcast_to(lse, (nr, LANES)).T[0:1, :]

    pair = lambda p: (0, p)
    return pl.pallas_call(
        body, name="attn_fwd", grid=(npair,),
        in_specs=[pl.BlockSpec((tp, LANES), pair),
                  pl.BlockSpec((tp, LANES), lambda p: (0, npair + p)),
                  pl.BlockSpec((tp, LANES), lambda p: (0, 2 * npair + p)),
                  pl.BlockSpec((tp, LANES), lambda p: (0, 0)),
                  pl.BlockSpec((SUBLANES, tp), lambda p: (0, 0))],
        out_specs=[pl.BlockSpec((tp, LANES), pair),
                   pl.BlockSpec((None, SUBLANES, tp), lambda p: (p, 0, 0))],
        out_shape=[jax.ShapeDtypeStruct((tp, nh * HEAD_DIM), F32),
                   jax.ShapeDtypeStruct((npair, SUBLANES, tp), F32)],
        compiler_params=_cparams("parallel"),
    )(qkv, qkv, qkv, c, ct)


def _shift_down(x, k, n):
    if k == 0:
        return x
    rows = lax.broadcasted_iota(jnp.int32, x.shape, 0)
    return jnp.where(rows >= k, pltpu.roll(x, k, 0), 0.0)


def _shift_up(x, k, n):
    if k == 0:
        return x
    rows = lax.broadcasted_iota(jnp.int32, x.shape, 0)
    return jnp.where(rows < n - k, pltpu.roll(x, n - k, 0), 0.0)


def _conv_fwd(xr, cw_ref, cb_ref, n):
    xc = cw_ref[CONV_WIDTH - 1:CONV_WIDTH, :] * xr + cb_ref[...]
    for k in range(1, CONV_WIDTH):
        xc = xc + cw_ref[CONV_WIDTH - 1 - k:CONV_WIDTH - k, :] * _shift_down(xr, k, n)
    return xc


def _gates(xc, wga_ref, bga_ref, wgx_ref, bgx_ref, l_ref):
    xcb = xc.astype(BF16)
    r = _sigmoid(_dot(xcb, wga_ref[...]) + bga_ref[...])
    ig = _sigmoid(_dot(xcb, wgx_ref[...]) + bgx_ref[...])
    ls = _log_sigmoid(l_ref[...])
    log_a = RG_C * r * ls
    a = jnp.exp(log_a)
    mult = jnp.sqrt(-_expm1(2.0 * log_a))
    return xcb, r, ig, ls, log_a, a, mult


def _scan_rows(a_s, u_s, out_ref, n, reverse):
    nt = n // SUBLANES
    row = lax.broadcasted_iota(jnp.int32, (SUBLANES, LANES), 0)

    def step(t, carry):
        tt = (nt - 1 - t) if reverse else t
        off = pl.multiple_of(tt * SUBLANES, SUBLANES)
        a = a_s[pl.ds(off, SUBLANES), :]
        u = u_s[pl.ds(off, SUBLANES), :]
        for d in (1, 2, 4):
            if reverse:
                keep = row < SUBLANES - d
                sh = SUBLANES - d
            else:
                keep = row >= d
                sh = d
            a_sh = jnp.where(keep, pltpu.roll(a, sh, 0), 1.0)
            u_sh = jnp.where(keep, pltpu.roll(u, sh, 0), 0.0)
            u = a * u_sh + u
            a = a * a_sh
        h = u + a * carry
        out_ref[pl.ds(off, SUBLANES), :] = h
        return h[0:1, :] if reverse else h[SUBLANES - 1:SUBLANES, :]

    lax.fori_loop(0, nt, step, jnp.zeros((1, LANES), F32))


def rec_fwd(rest, convw, convb, wga, bga, wgx, bgx, lru, rw):
    tp = rest.shape[0]
    ng = rw // LANES

    def body(xr_ref, yr_ref, cw_ref, cb_ref, wga_ref, bga_ref, wgx_ref, bgx_ref, l_ref,
             rec_ref, hr_ref, xc_ref, a_s, u_s):
        xc = _conv_fwd(xr_ref[...], cw_ref, cb_ref, tp)
        xc_ref[...] = xc
        _, r, ig, ls, log_a, a, mult = _gates(xc, wga_ref, bga_ref, wgx_ref, bgx_ref, l_ref)
        a_s[...] = a
        u_s[...] = mult * ig * xc
        _scan_rows(a_s, u_s, hr_ref, tp, reverse=False)
        rec_ref[...] = hr_ref[...] * _gelu(yr_ref[...])

    col = lambda g: (0, g)
    vec = pl.BlockSpec((1, LANES), col)
    big = pl.BlockSpec((tp, LANES), col)
    return pl.pallas_call(
        body, name="rec_fwd", grid=(ng,),
        in_specs=[big, pl.BlockSpec((tp, LANES), lambda g: (0, ng + g)),
                  pl.BlockSpec((CONV_WIDTH, LANES), col), vec,
                  pl.BlockSpec((None, LANES, LANES), lambda g: (g, 0, 0)), vec,
                  pl.BlockSpec((None, LANES, LANES), lambda g: (g, 0, 0)), vec, vec],
        out_specs=[big, big, big],
        out_shape=[jax.ShapeDtypeStruct((tp, rw), F32)] * 3,
        scratch_shapes=[pltpu.VMEM((tp, LANES), F32), pltpu.VMEM((tp, LANES), F32)],
        compiler_params=_cparams("parallel"),
    )(rest, rest, convw, convb, wga, bga, wgx, bgx, lru)


def out_proj(h, o, rec, ga, gr, wout, g2, tm):
    tp, d = h.shape
    aw, rw = o.shape[1], rec.shape[1]

    def body(h_ref, o_ref, rec_ref, ga_ref, gr_ref, w_ref, g2_ref, h2_ref, mix_ref, z2_ref):
        mix_ref[:, 0:aw] = _rms_fwd(o_ref[...], ga_ref[...]).astype(BF16)
        mix_ref[:, aw:aw + rw] = _rms_fwd(rec_ref[...], gr_ref[...]).astype(BF16)
        h2 = h_ref[...] + _dot(mix_ref[...], w_ref[...])
        h2_ref[...] = h2
        z2_ref[...] = _rms_fwd(h2, g2_ref[...]).astype(BF16)

    row = lambda i: (i, 0)
    fix = lambda i: (0, 0)
    return pl.pallas_call(
        body, name="out_proj", grid=(tp // tm,),
        in_specs=[pl.BlockSpec((tm, d), row), pl.BlockSpec((tm, aw), row), pl.BlockSpec((tm, rw), row),
                  pl.BlockSpec((1, aw), fix), pl.BlockSpec((1, rw), fix),
                  pl.BlockSpec((d, d), fix), pl.BlockSpec((1, d), fix)],
        out_specs=[pl.BlockSpec((tm, d), row)] * 3,
        out_shape=[jax.ShapeDtypeStruct((tp, d), F32), jax.ShapeDtypeStruct((tp, d), BF16),
                   jax.ShapeDtypeStruct((tp, d), BF16)],
        compiler_params=_cparams("parallel"),
    )(h, o, rec, ga, gr, wout, g2)


def mlp_fwd(z2, h2, gup, gdown, tm):
    tp, d = h2.shape
    nf = gup.shape[0]
    tf = gup.shape[2]

    def body(z_ref, h_ref, wu_ref, wd_ref, u_ref, h3_ref, acc):
        j = pl.program_id(1)
        u = jnp.maximum(_dot(z_ref[...], wu_ref[...]), 0.0)
        u_ref[...] = u.astype(BF16)
        part = _dot((u * u).astype(BF16), wd_ref[...])

        @pl.when(j == 0)
        def _():
            acc[...] = h_ref[...] + part

        @pl.when(j > 0)
        def _():
            acc[...] += part

        @pl.when(j == nf - 1)
        def _():
            h3_ref[...] = acc[...]

    return pl.pallas_call(
        body, name="mlp_fwd", grid=(tp // tm, nf),
        in_specs=[pl.BlockSpec((tm, d), lambda i, j: (i, 0)),
                  pl.BlockSpec((tm, d), lambda i, j: (i, 0)),
                  pl.BlockSpec((None, d, tf), lambda i, j: (j, 0, 0)),
                  pl.BlockSpec((None, tf, d), lambda i, j: (j, 0, 0))],
        out_specs=[pl.BlockSpec((tm, tf), lambda i, j: (i, j)),
                   pl.BlockSpec((tm, d), lambda i, j: (i, 0))],
        out_shape=[jax.ShapeDtypeStruct((tp, nf * tf), BF16), jax.ShapeDtypeStruct((tp, d), F32)],
        scratch_shapes=[pltpu.VMEM((tm, d), F32)],
        compiler_params=_cparams("parallel", "arbitrary"),
    )(z2, h2, gup, gdown)


def loss_head(h, gf, tgt, t_real, tm):
    tp, d = h.shape

    def body(h_ref, g_ref, t_ref, dh_ref, dg_ref, loss_ref):
        i = pl.program_id(0)
        x = h_ref[...]
        g = g_ref[...]
        r = lax.rsqrt(jnp.mean(x * x, axis=-1, keepdims=True) + NORM_EPS)
        xn = x * r
        rows = i * tm + lax.broadcasted_iota(jnp.int32, (tm, 1), 0)
        valid = jnp.logical_and(rows >= N_META, rows < t_real)
        e = jnp.where(valid, xn * g - t_ref[...], 0.0)
        part = 0.5 * jnp.sum(jnp.sum(e * e, axis=1, keepdims=True) / d, axis=0, keepdims=True)
        dy = e / d
        dxn = dy * g
        dh_ref[...] = r * (dxn - xn * jnp.mean(dxn * xn, axis=-1, keepdims=True))
        _accumulate(dg_ref, jnp.sum(dy * xn, axis=0, keepdims=True), i == 0)
        _accumulate(loss_ref, jnp.broadcast_to(part, (1, LANES)), i == 0)

    row = lambda i: (i, 0)
    fix = lambda i: (0, 0)
    return pl.pallas_call(
        body, name="loss_head", grid=(tp // tm,),
        in_specs=[pl.BlockSpec((tm, d), row), pl.BlockSpec((1, d), fix), pl.BlockSpec((tm, d), row)],
        out_specs=[pl.BlockSpec((tm, d), row), pl.BlockSpec((1, d), fix), pl.BlockSpec((1, LANES), fix)],
        out_shape=[jax.ShapeDtypeStruct((tp, d), F32), jax.ShapeDtypeStruct((1, d), F32),
                   jax.ShapeDtypeStruct((1, LANES), F32)],
        compiler_params=_cparams("arbitrary"),
    )(h, gf, tgt)


def mlp_bwd(dh, u, h2, g2, gup, gdown, tm):
    tp, d = dh.shape
    nf = gup.shape[0]
    tf = gup.shape[2]
    ni = tp // tm

    def body(dh_ref, u_ref, h2_ref, g_ref, wu_ref, wd_ref, dup_ref, dh2_ref, dg_ref, acc, dhb):
        i = pl.program_id(0)
        j = pl.program_id(1)

        @pl.when(j == 0)
        def _():
            dhb[...] = dh_ref[...].astype(BF16)

        dup = (_dot_nt(dhb[...], wd_ref[...]) * (2.0 * u_ref[...].astype(F32))).astype(BF16)
        dup_ref[...] = dup
        _accumulate(acc, _dot_nt(dup, wu_ref[...]), j == 0)

        @pl.when(j == nf - 1)
        def _():
            dx, dg = _rms_bwd(h2_ref[...], g_ref[...], acc[...])
            dh2_ref[...] = dh_ref[...] + dx
            _accumulate(dg_ref, dg, i == 0)

    return pl.pallas_call(
        body, name="mlp_bwd", grid=(ni, nf),
        in_specs=[pl.BlockSpec((tm, d), lambda i, j: (i, 0)),
                  pl.BlockSpec((tm, tf), lambda i, j: (i, j)),
                  pl.BlockSpec((tm, d), lambda i, j: (i, 0)),
                  pl.BlockSpec((1, d), lambda i, j: (0, 0)),
                  pl.BlockSpec((None, d, tf), lambda i, j: (j, 0, 0)),
                  pl.BlockSpec((None, tf, d), lambda i, j: (j, 0, 0))],
        out_specs=[pl.BlockSpec((tm, tf), lambda i, j: (i, j)),
                   pl.BlockSpec((tm, d), lambda i, j: (i, 0)),
                   pl.BlockSpec((1, d), lambda i, j: (0, 0))],
        out_shape=[jax.ShapeDtypeStruct((tp, nf * tf), BF16), jax.ShapeDtypeStruct((tp, d), F32),
                   jax.ShapeDtypeStruct((1, d), F32)],
        scratch_shapes=[pltpu.VMEM((tm, d), F32), pltpu.VMEM((tm, d), BF16)],
        compiler_params=_cparams("arbitrary", "arbitrary"),
    )(dh, u, h2, g2, gup, gdown)


def mm_tn(a, b, *, tk, tn, tr, out_dtype, name, square_a=False, blocked_n=False):
    rows, kk = a.shape
    nn = b.shape[1]
    nr = rows // tr

    def body(a_ref, b_ref, o_ref, acc):
        r = pl.program_id(2)
        av = a_ref[...]
        if square_a:
            af = av.astype(F32)
            av = (af * af).astype(BF16)
        prod = _dot_tn(av.astype(BF16), b_ref[...].astype(BF16))
        _accumulate(acc, prod, r == 0)

        @pl.when(r == nr - 1)
        def _():
            o_ref[...] = acc[...].astype(out_dtype)

    if blocked_n:
        out_spec = pl.BlockSpec((None, tk, tn), lambda k, n, r: (n, k, 0))
        out_shape = jax.ShapeDtypeStruct((nn // tn, kk, tn), out_dtype)
    else:
        out_spec = pl.BlockSpec((tk, tn), lambda k, n, r: (k, n))
        out_shape = jax.ShapeDtypeStruct((kk, nn), out_dtype)
    return pl.pallas_call(
        body, name=name, grid=(kk // tk, nn // tn, nr),
        in_specs=[pl.BlockSpec((tr, tk), lambda k, n, r: (r, k)),
                  pl.BlockSpec((tr, tn), lambda k, n, r: (r, n))],
        out_specs=out_spec, out_shape=out_shape,
        scratch_shapes=[pltpu.VMEM((tk, tn), F32)],
        compiler_params=_cparams("parallel", "parallel", "arbitrary"),
    )(a, b)


def out_proj_bwd(dh2, o, rec, ga, gr, wout, tm):
    tp, d = dh2.shape
    aw, rw = o.shape[1], rec.shape[1]

    def body(dh_ref, o_ref, rec_ref, ga_ref, gr_ref, w_ref, do_ref, drec_ref, dga_ref, dgr_ref):
        i = pl.program_id(0)
        dmix = _dot_nt(dh_ref[...].astype(BF16), w_ref[...])
        do, dga = _rms_bwd(o_ref[...], ga_ref[...], dmix[:, 0:aw])
        drec, dgr = _rms_bwd(rec_ref[...], gr_ref[...], dmix[:, aw:aw + rw])
        do_ref[...] = do
        drec_ref[...] = drec
        _accumulate(dga_ref, dga, i == 0)
        _accumulate(dgr_ref, dgr, i == 0)

    row = lambda i: (i, 0)
    fix = lambda i: (0, 0)
    return pl.pallas_call(
        body, name="out_proj_bwd", grid=(tp // tm,),
        in_specs=[pl.BlockSpec((tm, d), row), pl.BlockSpec((tm, aw), row), pl.BlockSpec((tm, rw), row),
                  pl.BlockSpec((1, aw), fix), pl.BlockSpec((1, rw), fix), pl.BlockSpec((d, d), fix)],
        out_specs=[pl.BlockSpec((tm, aw), row), pl.BlockSpec((tm, rw), row),
                   pl.BlockSpec((1, aw), fix), pl.BlockSpec((1, rw), fix)],
        out_shape=[jax.ShapeDtypeStruct((tp, aw), F32), jax.ShapeDtypeStruct((tp, rw), F32),
                   jax.ShapeDtypeStruct((1, aw), F32), jax.ShapeDtypeStruct((1, rw), F32)],
        compiler_params=_cparams("arbitrary"),
    )(dh2, o, rec, ga, gr, wout)


def rec_bwd(drec, hr, xc, rest, convw, convb, wga, bga, wgx, bgx, lru, rw):
    tp = rest.shape[0]
    ng = rw // LANES

    def body(drec_ref, hr_ref, xc_ref, xr_ref, yr_ref, cw_ref, cb_ref, wga_ref, bga_ref, wgx_ref, bgx_ref, l_ref,
             dxr_ref, dyr_ref, dwga_ref, dwgx_ref, vec_ref, a_s, u_s, lam_s):
        xc = xc_ref[...]
        h = hr_ref[...]
        yr = yr_ref[...]
        drec = drec_ref[...]
        xcb, r, ig, ls, log_a, a, mult = _gates(xc, wga_ref, bga_ref, wgx_ref, bgx_ref, l_ref)
        dyr_ref[...] = (drec * h * _gelu_grad(yr)).astype(BF16)
        a_s[...] = _shift_up(a, 1, tp)
        u_s[...] = drec * _gelu(yr)
        _scan_rows(a_s, u_s, lam_s, tp, reverse=True)
        lam = lam_s[...]
        da = lam * _shift_down(h, 1, tp)
        dmult = lam * ig * xc
        dig = lam * mult * xc
        dxc = lam * mult * ig
        a2 = jnp.exp(2.0 * log_a)
        dlog_a = da * a - dmult * a2 / mult
        dr = dlog_a * (RG_C * ls)
        dl = jnp.sum(dlog_a * (RG_C * r), axis=0, keepdims=True) * _sigmoid(-l_ref[...])
        dpa = dr * r * (1.0 - r)
        dpx = dig * ig * (1.0 - ig)
        dpab = dpa.astype(BF16)
        dpxb = dpx.astype(BF16)
        dxc = dxc + _dot_nt(dpab, wga_ref[...]) + _dot_nt(dpxb, wgx_ref[...])
        dwga_ref[...] = _dot_tn(xcb, dpab)
        dwgx_ref[...] = _dot_tn(xcb, dpxb)
        xr = xr_ref[...]
        dxr = cw_ref[CONV_WIDTH - 1:CONV_WIDTH, :] * dxc
        for k in range(1, CONV_WIDTH):
            dxr = dxr + cw_ref[CONV_WIDTH - 1 - k:CONV_WIDTH - k, :] * _shift_up(dxc, k, tp)
        dxr_ref[...] = dxr.astype(BF16)
        for k in range(CONV_WIDTH):
            vec_ref[k:k + 1, :] = jnp.sum(dxc * _shift_down(xr, CONV_WIDTH - 1 - k, tp), axis=0, keepdims=True)
        vec_ref[4:5, :] = jnp.sum(dxc, axis=0, keepdims=True)
        vec_ref[5:6, :] = jnp.sum(dpa, axis=0, keepdims=True)
        vec_ref[6:7, :] = jnp.sum(dpx, axis=0, keepdims=True)
        vec_ref[7:8, :] = dl

    col = lambda g: (0, g)
    vec = pl.BlockSpec((1, LANES), col)
    big = pl.BlockSpec((tp, LANES), col)
    sq = pl.BlockSpec((None, LANES, LANES), lambda g: (g, 0, 0))
    return pl.pallas_call(
        body, name="rec_bwd", grid=(ng,),
        in_specs=[big, big, big, big, pl.BlockSpec((tp, LANES), lambda g: (0, ng + g)),
                  pl.BlockSpec((CONV_WIDTH, LANES), col), vec, sq, vec, sq, vec, vec],
        out_specs=[big, big, sq, sq, pl.BlockSpec((None, SUBLANES, LANES), lambda g: (g, 0, 0))],
        out_shape=[jax.ShapeDtypeStruct((tp, rw), BF16), jax.ShapeDtypeStruct((tp, rw), BF16),
                   jax.ShapeDtypeStruct((ng, LANES, LANES), F32), jax.ShapeDtypeStruct((ng, LANES, LANES), F32),
                   jax.ShapeDtypeStruct((ng, SUBLANES, LANES), F32)],
        scratch_shapes=[pltpu.VMEM((tp, LANES), F32)] * 3,
        compiler_params=_cparams("parallel"),
    )(drec, hr, xc, rest, rest, convw, convb, wga, bga, wgx, bgx, lru)


def attn_bwd(qkv, do, o, lset, c, ct, nh):
    tp = qkv.shape[0]
    npair = nh // 2
    aw = nh * HEAD_DIM
    tiles = _att_tiles(tp)

    def body(q_ref, k_ref, v_ref, do_ref, o_ref, lset_ref, c_ref, ct_ref,
             dq_ref, dk_ref, dv_ref, drow_ref, dcol_ref, dk_acc, dv_acc):
        p = pl.program_id(0)
        dk_acc[...] = jnp.zeros_like(dk_acc)
        dv_acc[...] = jnp.zeros_like(dv_acc)
        dcol_ref[...] = jnp.zeros_like(dcol_ref)
        drow_ref[...] = jnp.zeros_like(drow_ref)
        for r0, nr, nk in tiles:
            rs = slice(r0, r0 + nr)
            causal = (r0 + lax.broadcasted_iota(jnp.int32, (nk, nr), 1)
                      >= lax.broadcasted_iota(jnp.int32, (nk, nr), 0))
            cblk = c_ref[0:nk, :]
            ctb = ct_ref[:, rs]
            for hh in range(2):
                head = 2 * p + hh
                hs = slice(hh * HEAD_DIM, (hh + 1) * HEAD_DIM)
                q = q_ref[rs, hs]
                k = k_ref[0:nk, hs]
                dof = do_ref[rs, hs]
                do16 = dof.astype(BF16)
                delta = jnp.sum(dof * o_ref[rs, hs], axis=1, keepdims=True)
                delta_row = jnp.broadcast_to(delta, (nr, LANES)).T[0:1, :]
                s_t = _dot_nt(k, q * ATT_SCALE) + (_pick_row(ctb, head) - _pick_col(cblk, head))
                p_t = jnp.where(causal, jnp.exp(s_t - lset_ref[hh:hh + 1, rs]), 0.0)
                ds_t = p_t * (_dot_nt(v_ref[0:nk, hs], do16) - delta_row)
                p16 = p_t.astype(BF16)
                ds16 = ds_t.astype(BF16)
                dv_acc[0:nk, hs] += _dot(p16, do16)
                dk_acc[0:nk, hs] += _dot(ds16, q) * ATT_SCALE
                dq_ref[rs, hs] = (_dot_tn(ds16, k) * ATT_SCALE).astype(BF16)
                drow_ref[hh:hh + 1, rs] = jnp.sum(ds_t, axis=0, keepdims=True)
                dcol_ref[0:nk, hs] -= jnp.broadcast_to(jnp.sum(ds_t, axis=1, keepdims=True), (nk, HEAD_DIM))
        dk_ref[...] = dk_acc[...].astype(BF16)
        dv_ref[...] = dv_acc[...].astype(BF16)

    pair = lambda p: (0, p)
    return pl.pallas_call(
        body, name="attn_bwd", grid=(npair,),
        in_specs=[pl.BlockSpec((tp, LANES), pair),
                  pl.BlockSpec((tp, LANES), lambda p: (0, npair + p)),
                  pl.BlockSpec((tp, LANES), lambda p: (0, 2 * npair + p)),
                  pl.BlockSpec((tp, LANES), pair),
                  pl.BlockSpec((tp, LANES), pair),
                  pl.BlockSpec((None, SUBLANES, tp), lambda p: (p, 0, 0)),
                  pl.BlockSpec((tp, LANES), lambda p: (0, 0)),
                  pl.BlockSpec((SUBLANES, tp), lambda p: (0, 0))],
        out_specs=[pl.BlockSpec((tp, LANES), pair), pl.BlockSpec((tp, LANES), pair),
                   pl.BlockSpec((tp, LANES), pair),
                   pl.BlockSpec((None, SUBLANES, tp), lambda p: (p, 0, 0)),
                   pl.BlockSpec((tp, LANES), pair)],
        out_shape=[jax.ShapeDtypeStruct((tp, aw), BF16), jax.ShapeDtypeStruct((tp, aw), BF16),
                   jax.ShapeDtypeStruct((tp, aw), BF16),
                   jax.ShapeDtypeStruct((npair, SUBLANES, tp), F32),
                   jax.ShapeDtypeStruct((tp, aw), F32)],
        scratch_shapes=[pltpu.VMEM((tp, LANES), F32), pltpu.VMEM((tp, LANES), F32)],
        compiler_params=_cparams("parallel"),
    )(qkv, qkv, qkv, do, o, lset, c, ct)


def fgate_bwd(dct8, drs, rest, bf_pad, fcol):
    tp = rest.shape[0]
    aw = drs.shape[1]
    nb = tp // ATT_BLOCK
    B = ATT_BLOCK

    def body(d_ref, drs_ref, f_ref, b_ref, dfl_ref, db_ref, pad_s):
        r_i = lax.broadcasted_iota(jnp.int32, (B, B), 0)
        c_i = lax.broadcasted_iota(jnp.int32, (B, B), 1)
        triu = (c_i >= r_i).astype(BF16)
        sel = (lax.broadcasted_iota(jnp.int32, (aw, LANES), 0)
               == HEAD_DIM * lax.broadcasted_iota(jnp.int32, (aw, LANES), 1)).astype(BF16)
        carry = jnp.zeros((1, LANES), F32)
        db = jnp.zeros((1, LANES), F32)
        pad_s[...] = jnp.zeros_like(pad_s)
        for i in range(nb - 1, -1, -1):
            sl = slice(i * B, (i + 1) * B)
            pad_s[0:SUBLANES, :] = d_ref[:, sl]
            dc = pad_s[...].T + _dot_split3(drs_ref[sl, :], sel)
            rc = _split3_dot(triu, dc)
            dlf = rc + carry
            carry = carry + rc[0:1, :]
            dfl = dlf * _sigmoid(-(f_ref[sl, :] + b_ref[...]))
            dfl_ref[sl, :] = dfl.astype(BF16)
            db = db + jnp.sum(dfl, axis=0, keepdims=True)
        db_ref[...] = db

    return pl.pallas_call(
        body, name="fgate_bwd", grid=(1,),
        in_specs=[pl.BlockSpec((SUBLANES, tp), lambda i: (0, 0)),
                  pl.BlockSpec((tp, aw), lambda i: (0, 0)),
                  pl.BlockSpec((tp, LANES), lambda i: (0, fcol)),
                  pl.BlockSpec((1, LANES), lambda i: (0, 0))],
        out_specs=[pl.BlockSpec((tp, LANES), lambda i: (0, 0)),
                   pl.BlockSpec((1, LANES), lambda i: (0, 0))],
        out_shape=[jax.ShapeDtypeStruct((tp, LANES), BF16), jax.ShapeDtypeStruct((1, LANES), F32)],
        scratch_shapes=[pltpu.VMEM((B, B), F32)],
        compiler_params=_cparams("arbitrary"),
    )(dct8, drs, rest, bf_pad)


def in_proj_bwd(dh2, parts, wqkv, wrest, h, g1, tm):
    tp, d = h.shape
    dq, dk, dv, dxr, dyr, dfl = parts
    aw, rw = dq.shape[1], dxr.shape[1]

    def body(dh2_ref, dq_ref, dk_ref, dv_ref, dxr_ref, dyr_ref, dfl_ref, wq_ref, wr_ref, h_ref, g_ref,
             dh_ref, dg_ref):
        i = pl.program_id(0)
        dz = _dot_nt(dq_ref[...], wq_ref[:, 0:aw])
        dz += _dot_nt(dk_ref[...], wq_ref[:, aw:2 * aw])
        dz += _dot_nt(dv_ref[...], wq_ref[:, 2 * aw:3 * aw])
        dz += _dot_nt(dxr_ref[...], wr_ref[:, 0:rw])
        dz += _dot_nt(dyr_ref[...], wr_ref[:, rw:2 * rw])
        dz += _dot_nt(dfl_ref[...], wr_ref[:, 2 * rw:2 * rw + LANES])
        dx, dg = _rms_bwd(h_ref[...], g_ref[...], dz)
        dh_ref[...] = dh2_ref[...] + dx
        _accumulate(dg_ref, dg, i == 0)

    row = lambda i: (i, 0)
    fix = lambda i: (0, 0)
    return pl.pallas_call(
        body, name="in_proj_bwd", grid=(tp // tm,),
        in_specs=[pl.BlockSpec((tm, d), row),
                  pl.BlockSpec((tm, aw), row), pl.BlockSpec((tm, aw), row), pl.BlockSpec((tm, aw), row),
                  pl.BlockSpec((tm, rw), row), pl.BlockSpec((tm, rw), row), pl.BlockSpec((tm, LANES), row),
                  pl.BlockSpec(wqkv.shape, fix), pl.BlockSpec(wrest.shape, fix),
                  pl.BlockSpec((tm, d), row), pl.BlockSpec((1, d), fix)],
        out_specs=[pl.BlockSpec((tm, d), row), pl.BlockSpec((1, d), fix)],
        out_shape=[jax.ShapeDtypeStruct((tp, d), F32), jax.ShapeDtypeStruct((1, d), F32)],
        compiler_params=_cparams("arbitrary"),
    )(dh2, dq, dk, dv, dxr, dyr, dfl, wqkv, wrest, h, g1)


def _place():
    return lax.axis_index("x"), lax.axis_index("y"), lax.axis_index("c")


HBM = pl.BlockSpec(memory_space=pltpu.HBM)
SEM = pl.BlockSpec(memory_space=pltpu.SEMAPHORE)
EFFECT = pltpu.SideEffectType.DATAFLOW_SIDE_EFFECTING


def _in_hbm(a):
    return pltpu.with_memory_space_constraint(a, pltpu.HBM)


def _gather_targets(x, y, c):
    return [(x, y, 1 - c), (1 - x, y, c), (x, 1 - y, c), (1 - x, 1 - y, c)]


def _slot(t):
    return 4 * t[0] + 2 * t[1] + t[2]


def gather_start(groups):
    flat = [a for g in groups for a in g]
    n = len(flat)
    ng = len(groups)
    lands = [lax.empty((N_DEV,) + a.shape, a.dtype) for a in flat]

    def body(*refs):
        src, land = refs[:n], refs[n:2 * n]
        sems = refs[2 * n:2 * n + 2 * ng]
        token = refs[-1]
        x, y, c = _place()
        me = 4 * x + 2 * y + c
        i = 0
        for gi, g in enumerate(groups):
            for a in range(len(g)):
                for k, t in enumerate(_gather_targets(x, y, c)):
                    pltpu.make_async_remote_copy(
                        src_ref=src[i], dst_ref=land[i].at[me],
                        send_sem=sems[2 * gi].at[4 * a + k], recv_sem=sems[2 * gi + 1].at[4 * a + k],
                        device_id=t, device_id_type=MESH).start()
                i += 1
        token[...] = jnp.zeros_like(token)

    sem_shapes = []
    for g in groups:
        sem_shapes += [pltpu.SemaphoreType.DMA((4 * len(g),)), pltpu.SemaphoreType.DMA((4 * len(g),))]
    out = pl.pallas_call(
        body, name="gather_start",
        out_shape=sem_shapes + [pltpu.HBM(a.shape, a.dtype) for a in flat + lands]
        + [jax.ShapeDtypeStruct((SUBLANES, LANES), F32)],
        in_specs=[HBM] * (2 * n),
        out_specs=[SEM] * (2 * ng) + [HBM] * (2 * n) + [pl.BlockSpec(memory_space=pltpu.VMEM)],
        input_output_aliases={i: 2 * ng + i for i in range(2 * n)},
        compiler_params=pltpu.CompilerParams(has_side_effects=EFFECT),
    )(*[_in_hbm(a) for a in flat + lands])
    sems = out[:2 * ng]
    thru = out[2 * ng:2 * ng + 2 * n]
    srcs_t, lands_t = thru[:n], thru[n:]
    res, i = [], 0
    for gi, g in enumerate(groups):
        res.append((sems[2 * gi], sems[2 * gi + 1], srcs_t[i:i + len(g)], lands_t[i:i + len(g)]))
        i += len(g)
    return res, out[-1]


def gather_wait(send, recv, srcs, lands, after, name):
    n = len(srcs)

    def body(*refs):
        src, land = refs[:n], refs[n:2 * n]
        send_sem, recv_sem = refs[2 * n], refs[2 * n + 1]
        x, y, c = _place()
        for a in range(n):
            for k, t in enumerate(_gather_targets(x, y, c)):
                cp = pltpu.make_async_remote_copy(
                    src_ref=src[a], dst_ref=land[a].at[_slot(t)],
                    send_sem=send_sem.at[4 * a + k], recv_sem=recv_sem.at[4 * a + k],
                    device_id=t, device_id_type=MESH)
                cp.wait_send()
                cp.wait_recv()

    out = pl.pallas_call(
        body, name=name,
        out_shape=[pltpu.HBM(a.shape, a.dtype) for a in list(srcs) + list(lands)],
        in_specs=[HBM] * (2 * n) + [SEM, SEM, ANY],
        out_specs=[HBM] * (2 * n),
        input_output_aliases={i: i for i in range(2 * n)},
        compiler_params=pltpu.CompilerParams(has_side_effects=EFFECT),
    )(*srcs, *lands, send, recv, after)
    return out[:n], out[n:]


def gather_forward(lands, name):
    n = len(lands)

    def body(*refs):
        land, out = refs[:n], refs[n:2 * n]
        send_sems, recv_sems = refs[2 * n:]
        x, y, c = _place()
        sibling = (x, y, 1 - c)
        chips = [(1 - x, y), (x, 1 - y), (1 - x, 1 - y)]

        def fwd(a, j, core):
            s = _slot((*chips[j], core))
            return pltpu.make_async_remote_copy(src_ref=land[a].at[s], dst_ref=out[a].at[s],
                                                send_sem=send_sems.at[a, j], recv_sem=recv_sems.at[a, j],
                                                device_id=sibling, device_id_type=MESH)

        sends = [fwd(a, j, c) for a in range(n) for j in range(3)]
        for cp in sends:
            cp.start()
        for a in range(n):
            for j in range(3):
                fwd(a, j, 1 - c).wait_recv()
        for cp in sends:
            cp.wait_send()

    return pl.pallas_call(
        body, name=name,
        in_specs=[ANY] * n, out_specs=[ANY] * n,
        out_shape=[jax.ShapeDtypeStruct(a.shape, a.dtype) for a in lands],
        input_output_aliases={i: i for i in range(n)},
        scratch_shapes=[pltpu.SemaphoreType.DMA((n, 3)), pltpu.SemaphoreType.DMA((n, 3))],
    )(*lands)


def _relations():
    return [(dx, dy, dc) for dx in (0, 1) for dy in (0, 1) for dc in (0, 1) if dx + dy + dc]


def _peer(x, y, c, rel):
    return ((1 - x) if rel[0] else x, (1 - y) if rel[1] else y, (1 - c) if rel[2] else c)


def exchange_start(srcs, lands, layer, name):
    n = len(srcs)

    def body(*refs):
        src, land = refs[:n], refs[n:2 * n]
        send_sem, recv_sem = refs[2 * n], refs[2 * n + 1]
        token = refs[-1]
        x, y, c = _place()
        me = 4 * x + 2 * y + c
        for k, rel in enumerate(_relations()):
            peer = _peer(x, y, c, rel)
            for a in range(n):
                pltpu.make_async_remote_copy(
                    src_ref=src[a] if layer is None else src[a].at[_slot(peer)],
                    dst_ref=land[a].at[me] if layer is None else land[a].at[me, layer],
                    send_sem=send_sem.at[7 * a + k], recv_sem=recv_sem.at[7 * a + k],
                    device_id=peer, device_id_type=MESH).start()
        token[...] = jnp.zeros_like(token)

    out = pl.pallas_call(
        body, name=name,
        out_shape=[pltpu.SemaphoreType.DMA((7 * n,)), pltpu.SemaphoreType.DMA((7 * n,))]
        + [pltpu.HBM(a.shape, a.dtype) for a in list(srcs) + list(lands)]
        + [jax.ShapeDtypeStruct((SUBLANES, LANES), F32)],
        in_specs=[HBM] * (2 * n),
        out_specs=[SEM, SEM] + [HBM] * (2 * n) + [pl.BlockSpec(memory_space=pltpu.VMEM)],
        input_output_aliases={i: 2 + i for i in range(2 * n)},
        compiler_params=pltpu.CompilerParams(has_side_effects=EFFECT),
    )(*[_in_hbm(a) for a in list(srcs) + list(lands)])
    return out[0], out[1], out[2:2 + n], out[2 + n:2 + 2 * n], out[-1][0, 0]


def exchange_wait(send, recv, srcs, lands, after, layer, name):
    n = len(srcs)

    def body(*refs):
        src, land = refs[:n], refs[n:2 * n]
        send_sem, recv_sem = refs[2 * n], refs[2 * n + 1]
        x, y, c = _place()
        for k, rel in enumerate(_relations()):
            peer = _peer(x, y, c, rel)
            for a in range(n):
                cp = pltpu.make_async_remote_copy(
                    src_ref=src[a] if layer is None else src[a].at[_slot(peer)],
                    dst_ref=land[a].at[_slot(peer)] if layer is None else land[a].at[_slot(peer), layer],
                    send_sem=send_sem.at[7 * a + k], recv_sem=recv_sem.at[7 * a + k],
                    device_id=peer, device_id_type=MESH)
                cp.wait_send()
                cp.wait_recv()

    out = pl.pallas_call(
        body, name=name,
        out_shape=[pltpu.HBM(a.shape, a.dtype) for a in list(srcs) + list(lands)],
        in_specs=[HBM] * (2 * n) + [SEM, SEM, ANY],
        out_specs=[HBM] * (2 * n),
        input_output_aliases={i: i for i in range(2 * n)},
        compiler_params=pltpu.CompilerParams(has_side_effects=EFFECT),
    )(*srcs, *lands, send, recv, after)
    return out[:n], out[n:]


def _adamw_math(g, w, m, v):
    m = ADAM_B1 * m + (1.0 - ADAM_B1) * g
    v = ADAM_B2 * v + (1.0 - ADAM_B2) * (g * g)
    m_hat = m / (1.0 - ADAM_B1 ** ADAM_STEP)
    v_hat = v / (1.0 - ADAM_B2 ** ADAM_STEP)
    delta = -ADAM_LR * (m_hat / (jnp.sqrt(v_hat) + ADAM_EPS) + ADAM_WD * w)
    return delta, m, v


def sum_adamw(parts, w, m, v, tr, name):
    npart, rows, cols = parts.shape

    def body(p_ref, w_ref, m_ref, v_ref, g_ref, d_ref, nm_ref, nv_ref):
        g = p_ref[0].astype(F32)
        for p in range(1, npart):
            g = g + p_ref[p].astype(F32)
        delta, nm, nv = _adamw_math(g, w_ref[...], m_ref[...], v_ref[...])
        g_ref[...] = g
        d_ref[...] = delta
        nm_ref[...] = nm
        nv_ref[...] = nv

    blk = pl.BlockSpec((tr, cols), lambda i: (i, 0))
    return pl.pallas_call(
        body, name=name, grid=(rows // tr,),
        in_specs=[pl.BlockSpec((npart, tr, cols), lambda i: (0, i, 0)), blk, blk, blk],
        out_specs=[blk] * 4,
        out_shape=[jax.ShapeDtypeStruct((rows, cols), F32)] * 4,
        compiler_params=_cparams("parallel"),
    )(parts, w, m, v)


def sum_parts(parts, name):
    npart, rows, cols = parts.shape

    def body(p_ref, g_ref):
        g = p_ref[0].astype(F32)
        for p in range(1, npart):
            g = g + p_ref[p].astype(F32)
        g_ref[...] = g

    return pl.pallas_call(
        body, name=name, grid=(1,),
        in_specs=[pl.BlockSpec((npart, rows, cols), lambda i: (0, 0, 0))],
        out_specs=pl.BlockSpec((rows, cols), lambda i: (0, 0)),
        out_shape=jax.ShapeDtypeStruct((rows, cols), F32),
        compiler_params=_cparams("arbitrary"),
    )(parts)


def _round_up(n, m):
    return (n + m - 1) // m * m


def _block_diag_pairs(w):
    nb, b, _ = w.shape
    per = LANES // b
    ng = nb // per
    w = w.reshape(ng, per, b, b)
    eye = jnp.eye(per, dtype=w.dtype)
    out = jnp.einsum('gpij,pq->gpiqj', w, eye).reshape(ng, LANES, LANES)
    return out.astype(BF16)


def _block_diag_extract(g, b):
    ng = g.shape[0]
    per = LANES // b
    g = g.reshape(ng, per, b, per, b)
    idx = jnp.arange(per)
    return g[:, idx, :, idx, :].transpose(1, 0, 2, 3).reshape(ng * per, b, b)


def _tiles(v):
    v = v.reshape(-1)
    n = _round_up(v.shape[0], SUBLANES * LANES)
    return jnp.pad(v, (0, n - v.shape[0])).reshape(-1, LANES)


SMALL = ['attn_norm_g', 'b_f', 'conv_w', 'conv_b', 'w_gate_a', 'b_gate_a', 'w_gate_x', 'b_gate_x',
         'lru_L', 'attn_out_g', 'rec_out_g', 'mlp_norm_g', 'final_g', 'meta']


def _pack(d):
    return jnp.concatenate([_tiles(d[n]) for n in SMALL], axis=0)


def _unpack(vec, shapes):
    out, r = {}, 0
    for n in SMALL:
        size = math.prod(shapes[n])
        nr = _round_up(size, SUBLANES * LANES) // LANES
        out[n] = vec[r:r + nr].reshape(-1)[:size].reshape(shapes[n])
        r += nr
    return out


def _row_tile(tp):
    return tp // 4 if (tp // 4) % 16 == 0 else tp


def local_step(x, tgt, meta, small, hooks):
    s, d = x.shape
    t_real = s + N_META
    tp = _round_up(t_real, ATT_BLOCK)
    depth = small['attn_norm_g'].shape[0]
    nh = small['b_f'].shape[1]
    rw = small['conv_b'].shape[1]
    blk = small['w_gate_a'].shape[2]
    tm = _row_tile(tp)
    tm2 = tp // 2
    fcol = 2 * rw // LANES

    h = jnp.concatenate([meta, x, jnp.zeros((tp - t_real, d), F32)], axis=0)
    tgt_p = jnp.pad(tgt, ((N_META, tp - t_real), (0, 0)))
    row = lambda v: v.reshape(1, -1)
    bf_pad = jnp.pad(small['b_f'], ((0, 0), (0, LANES - nh)))

    saved = []
    for l in range(depth):
        wqkv, wrest, wout = hooks.mixer_weights(l, h)
        wga = _block_diag_pairs(small['w_gate_a'][l])
        wgx = _block_diag_pairs(small['w_gate_x'][l])
        z, qkv, rest = in_proj(h, row(small['attn_norm_g'][l]), wqkv, wrest, tm)
        c, ct = fgate_fwd(rest, bf_pad[l:l + 1], fcol)
        o, lset = attn_fwd(qkv, c, ct, nh)
        rec, hr, xc = rec_fwd(rest, small['conv_w'][l], row(small['conv_b'][l]), wga, row(small['b_gate_a'][l]),
                              wgx, row(small['b_gate_x'][l]), row(small['lru_L'][l]), rw)
        h2, mix, z2 = out_proj(h, o, rec, row(small['attn_out_g'][l]), row(small['rec_out_g'][l]), wout,
                               row(small['mlp_norm_g'][l]), tm)
        gup, gdown = hooks.mlp_weights(l, z2)
        u, h3 = mlp_fwd(z2, h2, gup, gdown, tm2)
        saved.append(dict(h=h, z=z, qkv=qkv, rest=rest, c=c, ct=ct, o=o, lset=lset, rec=rec, hr=hr, xc=xc,
                          h2=h2, mix=mix, z2=z2, u=u, wga=wga, wgx=wgx,
                          wqkv=wqkv, wrest=wrest, wout=wout, gup=gup, gdown=gdown))
        h = h3

    dh, dgf, loss = loss_head(h, row(small['final_g']), tgt_p, t_real, tm)

    gs = {n: [None] * depth for n in SMALL if n not in ('final_g', 'meta')}
    tok = jnp.zeros((), F32)
    for l in reversed(range(depth)):
        sv = saved[l]
        gup, gdown = sv['gup'], sv['gdown']
        tf = gup.shape[2]
        dup, dh2, dg2 = mlp_bwd(dh, sv['u'], sv['h2'], row(small['mlp_norm_g'][l]) + tok, gup, gdown, tm2)
        gs['mlp_norm_g'][l] = dg2[0]
        do, drec, dga, dgr = out_proj_bwd(dh2, sv['o'], sv['rec'], row(small['attn_out_g'][l]),
                                          row(small['rec_out_g'][l]), sv['wout'], tm)
        gs['attn_out_g'][l] = dga[0]
        gs['rec_out_g'][l] = dgr[0]
        blocks = dict(
            w_down=mm_tn(sv['u'], dh, tk=tf, tn=d, tr=tm, out_dtype=BF16, name="dw_down",
                         square_a=True).reshape(N_DEV, tf, d),
            w_up=mm_tn(sv['z2'], dup, tk=d, tn=tf, tr=tm, out_dtype=BF16, name="dw_up", blocked_n=True),
            w_out=mm_tn(sv['mix'], dh2, tk=d, tn=d // 2, tr=tm, out_dtype=BF16,
                        name="dw_out").reshape(N_DEV, d // N_DEV, d))
        tok = hooks.grads_ready(l, 'mlp', blocks)
        dxr, dyr, dwga, dwgx, vec = rec_bwd(drec, sv['hr'], sv['xc'], sv['rest'], small['conv_w'][l],
                                            row(small['conv_b'][l]) + tok, sv['wga'], row(small['b_gate_a'][l]),
                                            sv['wgx'], row(small['b_gate_x'][l]), row(small['lru_L'][l]), rw)
        gs['w_gate_a'][l] = _block_diag_extract(dwga, blk)
        gs['w_gate_x'][l] = _block_diag_extract(dwgx, blk)
        vec = vec.transpose(1, 0, 2).reshape(SUBLANES, rw)
        gs['conv_w'][l] = vec[0:CONV_WIDTH]
        gs['conv_b'][l] = vec[4]
        gs['b_gate_a'][l] = vec[5]
        gs['b_gate_x'][l] = vec[6]
        gs['lru_L'][l] = vec[7]
        dq, dk, dv, drow, dcol = attn_bwd(sv['qkv'], do, sv['o'], sv['lset'], sv['c'], sv['ct'] + tok, nh)
        drow8 = drow[:, 0:2, :].reshape(nh, tp)
        if nh < SUBLANES:
            drow8 = jnp.pad(drow8, ((0, SUBLANES - nh), (0, 0)))
        dfl, dbf = fgate_bwd(drow8, dcol, sv['rest'], bf_pad[l:l + 1], fcol)
        gs['b_f'][l] = dbf[0, 0:nh]
        parts = (dq, dk, dv, dxr, dyr, dfl)
        dh, dg1 = in_proj_bwd(dh2, parts, sv['wqkv'], sv['wrest'], sv['h'], row(small['attn_norm_g'][l]), tm)
        gs['attn_norm_g'][l] = dg1[0]
        tok = jnp.zeros((), F32)
        if l == 0:
            grads = {n: jnp.stack(v) for n, v in gs.items()}
            grads['final_g'] = dgf[0]
            grads['meta'] = dh[0:N_META]
            tok = hooks.small_ready(grads)
        dws = [mm_tn(sv['z'], pt, tk=d, tn=pt.shape[1], tr=tm, out_dtype=BF16, name="dw_in_%d" % i)
               for i, pt in enumerate(parts)]
        dw_in = jnp.concatenate([dws[0], dws[1], dws[2], dws[5][:, 0:nh], dws[3], dws[4]], axis=1)
        dw_in = dw_in.reshape(d, N_DEV, -1).transpose(1, 0, 2) + tok.astype(BF16)
        tok = hooks.grads_ready(l, 'in', dict(w_in=dw_in))

    return loss[0, 0], dh


def prep_weights(g_in, g_out, nh, rw):
    _, d, _ = g_in.shape
    w_in = g_in.transpose(1, 0, 2).reshape(d, -1)
    aw = nh * HEAD_DIM
    wqkv = w_in[:, 0:3 * aw]
    f0 = 3 * aw
    wrest = jnp.concatenate([w_in[:, f0 + nh:f0 + nh + 2 * rw], w_in[:, f0:f0 + nh],
                             jnp.zeros((d, LANES - nh), w_in.dtype)], axis=1)
    wout = g_out.reshape(d, d)
    return wqkv, wrest, wout


BIG = ['w_in', 'w_out', 'w_up', 'w_down']
EXCHANGE_GROUPS = {'mlp': ['w_down', 'w_up', 'w_out'], 'in': ['w_in']}
WEIGHTS = ['meta', 'attn_norm_g', 'w_in', 'b_f', 'conv_w', 'conv_b', 'w_gate_a', 'b_gate_a', 'w_gate_x', 'b_gate_x',
           'lru_L', 'attn_out_g', 'rec_out_g', 'w_out', 'mlp_norm_g', 'w_up', 'w_down', 'final_g']


def _set_own(arr, own, me):
    return lax.dynamic_update_slice_in_dim(arr, own[None], me, axis=0)


class _Step:
    def __init__(self, w, nh, rw, me):
        self.w, self.nh, self.rw, self.me = w, nh, rw, me
        depth = w['w_in'].shape[0]
        groups = []
        for l in range(depth):
            groups.append([w['w_in'][l].astype(BF16), w['w_out'][l].astype(BF16)])
            groups.append([w['w_up'][l].astype(BF16), w['w_down'][l].astype(BF16)])
        groups[0] = groups[0] + [w['meta'], w['conv_w']]
        self.pending, _ = gather_start(groups)
        self.gathered = {}
        self.lands = {n: lax.empty((N_DEV,) + w[n].shape, BF16) for n in BIG}
        self.started = []
        self.small = None

    def group(self, gi, after):
        if gi not in self.gathered:
            send, recv, srcs, lands = self.pending[gi]
            srcs, lands = gather_wait(send, recv, srcs, lands, after, "gather_wait_%d" % gi)
            lands = gather_forward(lands, "gather_forward_%d" % gi)
            self.gathered[gi] = [_set_own(g, own, self.me) for g, own in zip(lands, srcs)]
        return self.gathered[gi]

    def mixer_weights(self, l, after):
        g = self.group(2 * l, after)
        return prep_weights(g[0], g[1], self.nh, self.rw)

    def mlp_weights(self, l, after):
        g = self.group(2 * l + 1, after)
        return g[0], g[1]

    def grads_ready(self, l, group, blocks):
        names = EXCHANGE_GROUPS[group]
        send, recv, srcs, lands, token = exchange_start(
            [blocks[n] for n in names], [self.lands[n] for n in names], l, "exchange_start_%s_%d" % (group, l))
        for n, a in zip(names, lands):
            self.lands[n] = a
        self.started.append((l, group, send, recv, srcs))
        return token

    def small_ready(self, grads):
        self.small_shapes = {n: grads[n].shape for n in SMALL}
        packed = _pack(grads).astype(BF16)
        send, recv, srcs, lands, token = exchange_start(
            [packed], [lax.empty((N_DEV,) + packed.shape, BF16)], None, "small_start")
        self.small = (send, recv, srcs, lands)
        return token

    def small_sum(self, after):
        send, recv, srcs, lands = self.small
        srcs, lands = exchange_wait(send, recv, srcs, lands, after, None, "small_wait")
        parts = _set_own(lands[0], srcs[0], self.me)
        return _unpack(sum_parts(parts, "sum_small_grads"), self.small_shapes)

    def received(self, group, after):
        names = EXCHANGE_GROUPS[group]
        own = {n: [None] * self.w[n].shape[0] for n in names}
        for l, grp, send, recv, srcs in self.started:
            if grp != group:
                continue
            srcs, lands = exchange_wait(send, recv, srcs, [self.lands[n] for n in names], after, l,
                                        "exchange_wait_%s_%d" % (group, l))
            for n, a, sr in zip(names, lands, srcs):
                self.lands[n] = a
                own[n][l] = lax.dynamic_index_in_dim(sr, self.me, 0, keepdims=False)
        return {n: _set_own(self.lands[n], jnp.stack(own[n]), self.me) for n in names}


def kernel(x, meta, attn_norm_g, w_in, b_f, conv_w, conv_b, w_gate_a, b_gate_a, w_gate_x, b_gate_x, lru_L, attn_out_g, rec_out_g, w_out, mlp_norm_g, w_up, w_down, final_g, loss_target, m_meta, m_attn_norm_g, m_w_in, m_b_f, m_conv_w, m_conv_b, m_w_gate_a, m_b_gate_a, m_w_gate_x, m_b_gate_x, m_lru_L, m_attn_out_g, m_rec_out_g, m_w_out, m_mlp_norm_g, m_w_up, m_w_down, m_final_g, v_meta, v_attn_norm_g, v_w_in, v_b_f, v_conv_w, v_conv_b, v_w_gate_a, v_b_gate_a, v_w_gate_x, v_b_gate_x, v_lru_L, v_attn_out_g, v_rec_out_g, v_w_out, v_mlp_norm_g, v_w_up, v_w_down, v_final_g):
    w = dict(meta=meta, attn_norm_g=attn_norm_g, w_in=w_in, b_f=b_f, conv_w=conv_w, conv_b=conv_b,
             w_gate_a=w_gate_a, b_gate_a=b_gate_a, w_gate_x=w_gate_x, b_gate_x=b_gate_x, lru_L=lru_L,
             attn_out_g=attn_out_g, rec_out_g=rec_out_g, w_out=w_out, mlp_norm_g=mlp_norm_g, w_up=w_up,
             w_down=w_down, final_g=final_g)
    mo = dict(meta=m_meta, attn_norm_g=m_attn_norm_g, w_in=m_w_in, b_f=m_b_f, conv_w=m_conv_w, conv_b=m_conv_b,
              w_gate_a=m_w_gate_a, b_gate_a=m_b_gate_a, w_gate_x=m_w_gate_x, b_gate_x=m_b_gate_x, lru_L=m_lru_L,
              attn_out_g=m_attn_out_g, rec_out_g=m_rec_out_g, w_out=m_w_out, mlp_norm_g=m_mlp_norm_g,
              w_up=m_w_up, w_down=m_w_down, final_g=m_final_g)
    vo = dict(meta=v_meta, attn_norm_g=v_attn_norm_g, w_in=v_w_in, b_f=v_b_f, conv_w=v_conv_w, conv_b=v_conv_b,
              w_gate_a=v_w_gate_a, b_gate_a=v_b_gate_a, w_gate_x=v_w_gate_x, b_gate_x=v_b_gate_x, lru_L=v_lru_L,
              attn_out_g=v_attn_out_g, rec_out_g=v_rec_out_g, w_out=v_w_out, mlp_norm_g=v_mlp_norm_g,
              w_up=v_w_up, w_down=v_w_down, final_g=v_final_g)
    depth = w_in.shape[0]
    nh = b_f.shape[1]
    rw = conv_b.shape[1]
    me = 4 * lax.axis_index("x") + 2 * lax.axis_index("y") + lax.axis_index("c")

    step = _Step(w, nh, rw, me)
    g0 = step.group(0, meta)
    meta_full = g0[2].transpose(1, 0, 2).reshape(N_META, -1)
    conv_full = g0[3].transpose(1, 2, 0, 3).reshape(depth, CONV_WIDTH, rw)
    small = {n: w[n] for n in SMALL}
    small['conv_w'] = conv_full

    loss_part, dh0 = local_step(x[0], loss_target[0], meta_full, small, step)
    loss = lax.psum(loss_part, ("x", "y", "c"))
    grad_x = dh0[N_META:N_META + x.shape[1]][None]

    gsum = step.small_sum(dh0)
    gsum['meta'] = lax.dynamic_slice_in_dim(gsum['meta'], me * meta.shape[1], meta.shape[1], axis=1)
    gsum['conv_w'] = lax.dynamic_slice_in_dim(gsum['conv_w'], me * conv_w.shape[2], conv_w.shape[2], axis=2)
    packed = [_pack(t) for t in (gsum, {n: w[n] for n in SMALL}, {n: mo[n] for n in SMALL},
                                 {n: vo[n] for n in SMALL})]
    res = sum_adamw(packed[0][None], packed[1], packed[2], packed[3], packed[1].shape[0], "adamw_small")
    out_g, out_d, out_m, out_v = {}, {}, {}, {}
    shapes = {n: w[n].shape for n in SMALL}
    for dst, vec in zip((out_g, out_d, out_m, out_v), res):
        dst.update(_unpack(vec, shapes))

    after = res[0]
    for group in ('mlp', 'in'):
        for n, r in step.received(group, after).items():
            shp = w[n].shape
            rows, cols = shp[0] * shp[1], shp[2]
            tr = min(512 if cols <= 512 else 256, rows)
            out = sum_adamw(r.reshape(N_DEV, rows, cols), w[n].reshape(rows, cols), mo[n].reshape(rows, cols),
                            vo[n].reshape(rows, cols), tr, "adamw_" + n)
            out_g[n], out_d[n], out_m[n], out_v[n] = [a.reshape(shp) for a in out]
            after = out[0]

    return (loss, grad_x, *[out_g[n] for n in WEIGHTS], *[out_d[n] for n in WEIGHTS],
            *[out_m[n] for n in WEIGHTS], *[out_v[n] for n in WEIGHTS])
```

```python
import functools
import math

import jax
import jax.numpy as jnp
from jax import lax
from jax.experimental import pallas as pl
from jax.experimental.pallas import tpu as pltpu

F32 = jnp.float32
BF16 = jnp.bfloat16

N_DEV = 8
N_META = 16
HEAD_DIM = 64
CONV_WIDTH = 4
RG_C = 8.0
NORM_EPS = 1e-6
LANES = 128
SUBLANES = 8
ATT_BLOCK = 128
ATT_TQ = 512
NEG_BIG = -1e30
ATT_SCALE = 1.0 / math.sqrt(HEAD_DIM)

ADAM_LR = 0.001
ADAM_B1 = 0.9
ADAM_B2 = 0.999
ADAM_EPS = 1e-08
ADAM_WD = 0.01
ADAM_STEP = 10

VMEM_LIMIT_BYTES = 56 * 1024 * 1024
MESH = pl.DeviceIdType.MESH
ANY = pl.BlockSpec(memory_space=pl.ANY)


def _cparams(*sem):
    return pltpu.CompilerParams(dimension_semantics=sem if sem else None,
                                vmem_limit_bytes=VMEM_LIMIT_BYTES)


def _dot(a, b):
    return jnp.dot(a, b, preferred_element_type=F32)


def _dot_nt(a, b):
    return lax.dot_general(a, b, (((1,), (1,)), ((), ())), preferred_element_type=F32)


def _dot_tn(a, b):
    return lax.dot_general(a, b, (((0,), (0,)), ((), ())), preferred_element_type=F32)


def _sigmoid(x):
    return 1.0 / (1.0 + jnp.exp(-x))


def _log_sigmoid(x):
    return jnp.minimum(x, 0.0) - jnp.log(1.0 + jnp.exp(-jnp.abs(x)))


def _expm1(x):
    series = x * (1.0 + x * (0.5 + x * (1.0 / 6.0 + x * (1.0 / 24.0))))
    return jnp.where(jnp.abs(x) < 1e-2, series, jnp.exp(x) - 1.0)


_GELU_K = math.sqrt(2.0 / math.pi)
_GELU_C = 0.044715


def _gelu(x):
    t = jnp.tanh(_GELU_K * (x + _GELU_C * x * x * x))
    return 0.5 * x * (1.0 + t)


def _gelu_grad(x):
    t = jnp.tanh(_GELU_K * (x + _GELU_C * x * x * x))
    return 0.5 * (1.0 + t) + 0.5 * x * (1.0 - t * t) * _GELU_K * (1.0 + 3.0 * _GELU_C * x * x)


def _split3_dot(tri, x):
    hi = x.astype(BF16)
    r1 = x - hi.astype(F32)
    mid = r1.astype(BF16)
    lo = (r1 - mid.astype(F32)).astype(BF16)
    return _dot(tri, hi) + _dot(tri, mid) + _dot(tri, lo)


def _dot_split3(x, sel):
    hi = x.astype(BF16)
    r1 = x - hi.astype(F32)
    mid = r1.astype(BF16)
    lo = (r1 - mid.astype(F32)).astype(BF16)
    return _dot(hi, sel) + _dot(mid, sel) + _dot(lo, sel)


def _rms_fwd(x, g):
    r = lax.rsqrt(jnp.mean(x * x, axis=-1, keepdims=True) + NORM_EPS)
    return x * r * g


def _rms_bwd(x, g, dy):
    r = lax.rsqrt(jnp.mean(x * x, axis=-1, keepdims=True) + NORM_EPS)
    xn = x * r
    dxn = dy * g
    dx = r * (dxn - xn * jnp.mean(dxn * xn, axis=-1, keepdims=True))
    return dx, jnp.sum(dy * xn, axis=0, keepdims=True)


def _accumulate(ref, val, first):
    @pl.when(first)
    def _():
        ref[...] = val

    @pl.when(jnp.logical_not(first))
    def _():
        ref[...] += val


def in_proj(h, g1, w_in_t, wrest_t, nq, tm):
    tp, d = h.shape
    nr = wrest_t.shape[0]

    def body(h_ref, g_ref, wq_ref, wr_ref, z_ref, qkv_ref, rest_ref):
        z = _rms_fwd(h_ref[...], g_ref[...]).astype(BF16)
        z_ref[...] = z
        qkv_ref[...] = _dot_nt(z, wq_ref[...]).astype(BF16)
        rest_ref[...] = _dot_nt(z, wr_ref[...])

    return pl.pallas_call(
        body, name="in_proj", grid=(tp // tm,),
        in_specs=[pl.BlockSpec((tm, d), lambda i: (i, 0)),
                  pl.BlockSpec((1, d), lambda i: (0, 0)),
                  pl.BlockSpec((nq, d), lambda i: (0, 0)),
                  pl.BlockSpec((nr, d), lambda i: (0, 0))],
        out_specs=[pl.BlockSpec((tm, d), lambda i: (i, 0)),
                   pl.BlockSpec((tm, nq), lambda i: (i, 0)),
                   pl.BlockSpec((tm, nr), lambda i: (i, 0))],
        out_shape=[jax.ShapeDtypeStruct((tp, d), BF16),
                   jax.ShapeDtypeStruct((tp, nq), BF16),
                   jax.ShapeDtypeStruct((tp, nr), F32)],
        compiler_params=_cparams("parallel"),
    )(h, g1, w_in_t, wrest_t)


def fgate_fwd(rest, bf_pad, fcol):
    tp = rest.shape[0]
    nb = tp // ATT_BLOCK

    def body(f_ref, b_ref, c_ref, ct_ref):
        r_i = lax.broadcasted_iota(jnp.int32, (ATT_BLOCK, ATT_BLOCK), 0)
        c_i = lax.broadcasted_iota(jnp.int32, (ATT_BLOCK, ATT_BLOCK), 1)
        tri = (r_i >= c_i).astype(BF16)
        carry = jnp.zeros((1, LANES), F32)
        for i in range(nb):
            sl = slice(i * ATT_BLOCK, (i + 1) * ATT_BLOCK)
            lf = _log_sigmoid(f_ref[sl, :] + b_ref[...])
            cs = _split3_dot(tri, lf) + carry
            carry = cs[ATT_BLOCK - 1:ATT_BLOCK, :]
            c_ref[sl, :] = cs
            ct_ref[:, sl] = cs.T[0:SUBLANES, :]

    return pl.pallas_call(
        body, name="fgate_fwd", grid=(1,),
        in_specs=[pl.BlockSpec((tp, LANES), lambda i: (0, fcol)),
                  pl.BlockSpec((1, LANES), lambda i: (0, 0))],
        out_specs=[pl.BlockSpec((tp, LANES), lambda i: (0, 0)),
                   pl.BlockSpec((SUBLANES, tp), lambda i: (0, 0))],
        out_shape=[jax.ShapeDtypeStruct((tp, LANES), F32),
                   jax.ShapeDtypeStruct((SUBLANES, tp), F32)],
        compiler_params=_cparams("arbitrary"),
    )(rest, bf_pad)


def _pick_col(blk, head):
    lane = lax.broadcasted_iota(jnp.int32, blk.shape, 1)
    return jnp.sum(jnp.where(lane == head, blk, 0.0), axis=1, keepdims=True)


def _pick_row(blk, head):
    sub = lax.broadcasted_iota(jnp.int32, blk.shape, 0)
    return jnp.sum(jnp.where(sub == head, blk, 0.0), axis=0, keepdims=True)


def _att_tiles(tp):
    out, r0 = [], 0
    while r0 < tp:
        rows = min(ATT_TQ, tp - r0)
        out.append((r0, rows, r0 + rows))
        r0 += rows
    return out


def attn_fwd(qkv, c, ct, nh):
    tp = qkv.shape[0]
    npair = nh // 2
    tiles = _att_tiles(tp)

    def body(q_ref, k_ref, v_ref, c_ref, ct_ref, o_ref, lset_ref):
        p = pl.program_id(0)
        lset_ref[...] = jnp.zeros_like(lset_ref)
        for r0, nr, nk in tiles:
            rs = slice(r0, r0 + nr)
            causal = (r0 + lax.broadcasted_iota(jnp.int32, (nr, nk), 0)
                      >= lax.broadcasted_iota(jnp.int32, (nr, nk), 1))
            cblk = c_ref[rs, :]
            ctb = ct_ref[:, 0:nk]
            for hh in range(2):
                head = 2 * p + hh
                hs = slice(hh * HEAD_DIM, (hh + 1) * HEAD_DIM)
                q = q_ref[rs, hs] * ATT_SCALE
                s = _dot_nt(q, k_ref[0:nk, hs]) + (_pick_col(cblk, head) - _pick_row(ctb, head))
                s = jnp.where(causal, s, NEG_BIG)
                m = jnp.max(s, axis=1, keepdims=True)
                pm = jnp.exp(s - m)
                l = jnp.sum(pm, axis=1, keepdims=True)
                o_ref[rs, hs] = _dot(pm.astype(BF16), v_ref[0:nk, hs]) / l
                lse = m + jnp.log(l)
                lset_ref[hh:hh + 1, rs] = jnp.broadcast_to(lse, (nr, LANES)).T[0:1, :]

    pair = lambda p: (0, p)
    return pl.pallas_call(
        body, name="attn_fwd", grid=(npair,),
        in_specs=[pl.BlockSpec((tp, LANES), pair),
                  pl.BlockSpec((tp, LANES), lambda p: (0, npair + p)),
                  pl.BlockSpec((tp, LANES), lambda p: (0, 2 * npair + p)),
                  pl.BlockSpec((tp, LANES), lambda p: (0, 0)),
                  pl.BlockSpec((SUBLANES, tp), lambda p: (0, 0))],
        out_specs=[pl.BlockSpec((tp, LANES), pair),
                   pl.BlockSpec((None, SUBLANES, tp), lambda p: (p, 0, 0))],
        out_shape=[jax.ShapeDtypeStruct((tp, nh * HEAD_DIM), F32),
                   jax.ShapeDtypeStruct((npair, SUBLANES, tp), F32)],
        compiler_params=_cparams("parallel"),
    )(qkv, qkv, qkv, c, ct)


def _shift_down(x, k, n):
    if k == 0:
        return x
    rows = lax.broadcasted_iota(jnp.int32, x.shape, 0)
    return jnp.where(rows >= k, pltpu.roll(x, k, 0), 0.0)


def _shift_up(x, k, n):
    if k == 0:
        return x
    rows = lax.broadcasted_iota(jnp.int32, x.shape, 0)
    return jnp.where(rows < n - k, pltpu.roll(x, n - k, 0), 0.0)


def _conv_fwd(xr, cw_ref, cb_ref, n):
    xc = cw_ref[CONV_WIDTH - 1:CONV_WIDTH, :] * xr + cb_ref[...]
    for k in range(1, CONV_WIDTH):
        xc = xc + cw_ref[CONV_WIDTH - 1 - k:CONV_WIDTH - k, :] * _shift_down(xr, k, n)
    return xc


def _gates(xc, wga_ref, bga_ref, wgx_ref, bgx_ref, l_ref):
    xcb = xc.astype(BF16)
    r = _sigmoid(_dot(xcb, wga_ref[...]) + bga_ref[...])
    ig = _sigmoid(_dot(xcb, wgx_ref[...]) + bgx_ref[...])
    ls = _log_sigmoid(l_ref[...])
    log_a = RG_C * r * ls
    a = jnp.exp(log_a)
    mult = jnp.sqrt(-_expm1(2.0 * log_a))
    return xcb, r, ig, ls, log_a, a, mult


def _scan_rows(a_s, u_s, out_ref, n, reverse):
    nt = n // SUBLANES
    row = lax.broadcasted_iota(jnp.int32, (SUBLANES, LANES), 0)

    def step(t, carry):
        tt = (nt - 1 - t) if reverse else t
        off = pl.multiple_of(tt * SUBLANES, SUBLANES)
        a = a_s[pl.ds(off, SUBLANES), :]
        u = u_s[pl.ds(off, SUBLANES), :]
        for d in (1, 2, 4):
            if reverse:
                keep = row < SUBLANES - d
                sh = SUBLANES - d
            else:
                keep = row >= d
                sh = d
            a_sh = jnp.where(keep, pltpu.roll(a, sh, 0), 1.0)
            u_sh = jnp.where(keep, pltpu.roll(u, sh, 0), 0.0)
            u = a * u_sh + u
            a = a * a_sh
        h = u + a * carry
        out_ref[pl.ds(off, SUBLANES), :] = h
        return h[0:1, :] if reverse else h[SUBLANES - 1:SUBLANES, :]

    lax.fori_loop(0, nt, step, jnp.zeros((1, LANES), F32))


def rec_fwd(rest, convw, convb, wga, bga, wgx, bgx, lru, rw):
    tp = rest.shape[0]
    ng = rw // LANES

    def body(xr_ref, yr_ref, cw_ref, cb_ref, wga_ref, bga_ref, wgx_ref, bgx_ref, l_ref,
             rec_ref, hr_ref, xc_ref, a_s, u_s):
        xc = _conv_fwd(xr_ref[...], cw_ref, cb_ref, tp)
        xc_ref[...] = xc
        _, r, ig, ls, log_a, a, mult = _gates(xc, wga_ref, bga_ref, wgx_ref, bgx_ref, l_ref)
        a_s[...] = a
        u_s[...] = mult * ig * xc
        _scan_rows(a_s, u_s, hr_ref, tp, reverse=False)
        rec_ref[...] = hr_ref[...] * _gelu(yr_ref[...])

    col = lambda g: (0, g)
    vec = pl.BlockSpec((1, LANES), col)
    big = pl.BlockSpec((tp, LANES), col)
    return pl.pallas_call(
        body, name="rec_fwd", grid=(ng,),
        in_specs=[big, pl.BlockSpec((tp, LANES), lambda g: (0, ng + g)),
                  pl.BlockSpec((CONV_WIDTH, LANES), col), vec,
                  pl.BlockSpec((None, LANES, LANES), lambda g: (g, 0, 0)), vec,
                  pl.BlockSpec((None, LANES, LANES), lambda g: (g, 0, 0)), vec, vec],
        out_specs=[big, big, big],
        out_shape=[jax.ShapeDtypeStruct((tp, rw), F32)] * 3,
        scratch_shapes=[pltpu.VMEM((tp, LANES), F32), pltpu.VMEM((tp, LANES), F32)],
        compiler_params=_cparams("parallel"),
    )(rest, rest, convw, convb, wga, bga, wgx, bgx, lru)


def out_proj(h, o, rec, ga, gr, wout, g2, tm):
    tp, d = h.shape
    aw, rw = o.shape[1], rec.shape[1]

    def body(h_ref, o_ref, rec_ref, ga_ref, gr_ref, w_ref, g2_ref, h2_ref, mix_ref, z2_ref):
        mix_ref[:, 0:aw] = _rms_fwd(o_ref[...], ga_ref[...]).astype(BF16)
        mix_ref[:, aw:aw + rw] = _rms_fwd(rec_ref[...], gr_ref[...]).astype(BF16)
        h2 = h_ref[...] + _dot(mix_ref[...], w_ref[...])
        h2_ref[...] = h2
        z2_ref[...] = _rms_fwd(h2, g2_ref[...]).astype(BF16)

    row = lambda i: (i, 0)
    fix = lambda i: (0, 0)
    return pl.pallas_call(
        body, name="out_proj", grid=(tp // tm,),
        in_specs=[pl.BlockSpec((tm, d), row), pl.BlockSpec((tm, aw), row), pl.BlockSpec((tm, rw), row),
                  pl.BlockSpec((1, aw), fix), pl.BlockSpec((1, rw), fix),
                  pl.BlockSpec((d, d), fix), pl.BlockSpec((1, d), fix)],
        out_specs=[pl.BlockSpec((tm, d), row)] * 3,
        out_shape=[jax.ShapeDtypeStruct((tp, d), F32), jax.ShapeDtypeStruct((tp, d), BF16),
                   jax.ShapeDtypeStruct((tp, d), BF16)],
        compiler_params=_cparams("parallel"),
    )(h, o, rec, ga, gr, wout, g2)


def mlp_fwd(z2, h2, gup, gdown, tm):
    tp, d = h2.shape
    nf = gup.shape[0]
    tf = gup.shape[2]

    def body(z_ref, h_ref, wu_ref, wd_ref, u_ref, h3_ref, acc):
        j = pl.program_id(1)
        u = jnp.maximum(_dot(z_ref[...], wu_ref[...]), 0.0)
        u_ref[...] = u.astype(BF16)
        part = _dot((u * u).astype(BF16), wd_ref[...])

        @pl.when(j == 0)
        def _():
            acc[...] = h_ref[...] + part

        @pl.when(j > 0)
        def _():
            acc[...] += part

        @pl.when(j == nf - 1)
        def _():
            h3_ref[...] = acc[...]

    return pl.pallas_call(
        body, name="mlp_fwd", grid=(tp // tm, nf),
        in_specs=[pl.BlockSpec((tm, d), lambda i, j: (i, 0)),
                  pl.BlockSpec((tm, d), lambda i, j: (i, 0)),
                  pl.BlockSpec((None, d, tf), lambda i, j: (j, 0, 0)),
                  pl.BlockSpec((None, tf, d), lambda i, j: (j, 0, 0))],
        out_specs=[pl.BlockSpec((tm, tf), lambda i, j: (i, j)),
                   pl.BlockSpec((tm, d), lambda i, j: (i, 0))],
        out_shape=[jax.ShapeDtypeStruct((tp, nf * tf), BF16), jax.ShapeDtypeStruct((tp, d), F32)],
        scratch_shapes=[pltpu.VMEM((tm, d), F32)],
        compiler_params=_cparams("parallel", "arbitrary"),
    )(z2, h2, gup, gdown)


def loss_head(h, gf, tgt, t_real, tm):
    tp, d = h.shape

    def body(h_ref, g_ref, t_ref, dh_ref, dg_ref, loss_ref):
        i = pl.program_id(0)
        x = h_ref[...]
        g = g_ref[...]
        r = lax.rsqrt(jnp.mean(x * x, axis=-1, keepdims=True) + NORM_EPS)
        xn = x * r
        rows = i * tm + lax.broadcasted_iota(jnp.int32, (tm, 1), 0)
        valid = jnp.logical_and(rows >= N_META, rows < t_real)
        e = jnp.where(valid, xn * g - t_ref[...], 0.0)
        part = 0.5 * jnp.sum(jnp.sum(e * e, axis=1, keepdims=True) / d, axis=0, keepdims=True)
        dy = e / d
        dxn = dy * g
        dh_ref[...] = r * (dxn - xn * jnp.mean(dxn * xn, axis=-1, keepdims=True))
        _accumulate(dg_ref, jnp.sum(dy * xn, axis=0, keepdims=True), i == 0)
        _accumulate(loss_ref, jnp.broadcast_to(part, (1, LANES)), i == 0)

    row = lambda i: (i, 0)
    fix = lambda i: (0, 0)
    return pl.pallas_call(
        body, name="loss_head", grid=(tp // tm,),
        in_specs=[pl.BlockSpec((tm, d), row), pl.BlockSpec((1, d), fix), pl.BlockSpec((tm, d), row)],
        out_specs=[pl.BlockSpec((tm, d), row), pl.BlockSpec((1, d), fix), pl.BlockSpec((1, LANES), fix)],
        out_shape=[jax.ShapeDtypeStruct((tp, d), F32), jax.ShapeDtypeStruct((1, d), F32),
                   jax.ShapeDtypeStruct((1, LANES), F32)],
        compiler_params=_cparams("arbitrary"),
    )(h, gf, tgt)


def mlp_bwd(dh, u, h2, g2, gup, gdown, tm):
    tp, d = dh.shape
    nf = gup.shape[0]
    tf = gup.shape[2]
    ni = tp // tm

    def body(dh_ref, u_ref, h2_ref, g_ref, wu_ref, wd_ref, dup_ref, dh2_ref, dg_ref, acc, dhb):
        i = pl.program_id(0)
        j = pl.program_id(1)

        @pl.when(j == 0)
        def _():
            dhb[...] = dh_ref[...].astype(BF16)

        dup = (_dot_nt(dhb[...], wd_ref[...]) * (2.0 * u_ref[...].astype(F32))).astype(BF16)
        dup_ref[...] = dup
        _accumulate(acc, _dot_nt(dup, wu_ref[...]), j == 0)

        @pl.when(j == nf - 1)
        def _():
            dx, dg = _rms_bwd(h2_ref[...], g_ref[...], acc[...])
            dh2_ref[...] = dh_ref[...] + dx
            _accumulate(dg_ref, dg, i == 0)

    return pl.pallas_call(
        body, name="mlp_bwd", grid=(ni, nf),
        in_specs=[pl.BlockSpec((tm, d), lambda i, j: (i, 0)),
                  pl.BlockSpec((tm, tf), lambda i, j: (i, j)),
                  pl.BlockSpec((tm, d), lambda i, j: (i, 0)),
                  pl.BlockSpec((1, d), lambda i, j: (0, 0)),
                  pl.BlockSpec((None, d, tf), lambda i, j: (j, 0, 0)),
                  pl.BlockSpec((None, tf, d), lambda i, j: (j, 0, 0))],
        out_specs=[pl.BlockSpec((tm, tf), lambda i, j: (i, j)),
                   pl.BlockSpec((tm, d), lambda i, j: (i, 0)),
                   pl.BlockSpec((1, d), lambda i, j: (0, 0))],
        out_shape=[jax.ShapeDtypeStruct((tp, nf * tf), BF16), jax.ShapeDtypeStruct((tp, d), F32),
                   jax.ShapeDtypeStruct((1, d), F32)],
        scratch_shapes=[pltpu.VMEM((tm, d), F32), pltpu.VMEM((tm, d), BF16)],
        compiler_params=_cparams("arbitrary", "arbitrary"),
    )(dh, u, h2, g2, gup, gdown)


def mm_tn(a, b, *, tk, tn, tr, out_dtype, name, square_a=False, blocked_n=False):
    rows, kk = a.shape
    nn = b.shape[1]
    nr = rows // tr

    def body(a_ref, b_ref, o_ref, acc):
        r = pl.program_id(2)
        av = a_ref[...]
        if square_a:
            af = av.astype(F32)
            av = (af * af).astype(BF16)
        prod = _dot_tn(av.astype(BF16), b_ref[...].astype(BF16))
        _accumulate(acc, prod, r == 0)

        @pl.when(r == nr - 1)
        def _():
            o_ref[...] = acc[...].astype(out_dtype)

    if blocked_n:
        out_spec = pl.BlockSpec((None, tk, tn), lambda k, n, r: (n, k, 0))
        out_shape = jax.ShapeDtypeStruct((nn // tn, kk, tn), out_dtype)
    else:
        out_spec = pl.BlockSpec((tk, tn), lambda k, n, r: (k, n))
        out_shape = jax.ShapeDtypeStruct((kk, nn), out_dtype)
    return pl.pallas_call(
        body, name=name, grid=(kk // tk, nn // tn, nr),
        in_specs=[pl.BlockSpec((tr, tk), lambda k, n, r: (r, k)),
                  pl.BlockSpec((tr, tn), lambda k, n, r: (r, n))],
        out_specs=out_spec, out_shape=out_shape,
        scratch_shapes=[pltpu.VMEM((tk, tn), F32)],
        compiler_params=_cparams("parallel", "parallel", "arbitrary"),
    )(a, b)


def out_proj_bwd(dh2, o, rec, ga, gr, wout, tm):
    tp, d = dh2.shape
    aw, rw = o.shape[1], rec.shape[1]

    def body(dh_ref, o_ref, rec_ref, ga_ref, gr_ref, w_ref, do_ref, drec_ref, dga_ref, dgr_ref):
        i = pl.program_id(0)
        dmix = _dot_nt(dh_ref[...].astype(BF16), w_ref[...])
        do, dga = _rms_bwd(o_ref[...], ga_ref[...], dmix[:, 0:aw])
        drec, dgr = _rms_bwd(rec_ref[...], gr_ref[...], dmix[:, aw:aw + rw])
        do_ref[...] = do
        drec_ref[...] = drec
        _accumulate(dga_ref, dga, i == 0)
        _accumulate(dgr_ref, dgr, i == 0)

    row = lambda i: (i, 0)
    fix = lambda i: (0, 0)
    return pl.pallas_call(
        body, name="out_proj_bwd", grid=(tp // tm,),
        in_specs=[pl.BlockSpec((tm, d), row), pl.BlockSpec((tm, aw), row), pl.BlockSpec((tm, rw), row),
                  pl.BlockSpec((1, aw), fix), pl.BlockSpec((1, rw), fix), pl.BlockSpec((d, d), fix)],
        out_specs=[pl.BlockSpec((tm, aw), row), pl.BlockSpec((tm, rw), row),
                   pl.BlockSpec((1, aw), fix), pl.BlockSpec((1, rw), fix)],
        out_shape=[jax.ShapeDtypeStruct((tp, aw), F32), jax.ShapeDtypeStruct((tp, rw), F32),
                   jax.ShapeDtypeStruct((1, aw), F32), jax.ShapeDtypeStruct((1, rw), F32)],
        compiler_params=_cparams("arbitrary"),
    )(dh2, o, rec, ga, gr, wout)


def rec_bwd(drec, hr, xc, rest, convw, convb, wga, bga, wgx, bgx, lru, rw):
    tp = rest.shape[0]
    ng = rw // LANES

    def body(drec_ref, hr_ref, xc_ref, xr_ref, yr_ref, cw_ref, cb_ref, wga_ref, bga_ref, wgx_ref, bgx_ref, l_ref,
             dxr_ref, dyr_ref, dwga_ref, dwgx_ref, vec_ref, a_s, u_s, lam_s):
        xc = xc_ref[...]
        h = hr_ref[...]
        yr = yr_ref[...]
        drec = drec_ref[...]
        xcb, r, ig, ls, log_a, a, mult = _gates(xc, wga_ref, bga_ref, wgx_ref, bgx_ref, l_ref)
        dyr_ref[...] = (drec * h * _gelu_grad(yr)).astype(BF16)
        a_s[...] = _shift_up(a, 1, tp)
        u_s[...] = drec * _gelu(yr)
        _scan_rows(a_s, u_s, lam_s, tp, reverse=True)
        lam = lam_s[...]
        da = lam * _shift_down(h, 1, tp)
        dmult = lam * ig * xc
        dig = lam * mult * xc
        dxc = lam * mult * ig
        a2 = jnp.exp(2.0 * log_a)
        dlog_a = da * a - dmult * a2 / mult
        dr = dlog_a * (RG_C * ls)
        dl = jnp.sum(dlog_a * (RG_C * r), axis=0, keepdims=True) * _sigmoid(-l_ref[...])
        dpa = dr * r * (1.0 - r)
        dpx = dig * ig * (1.0 - ig)
        dpab = dpa.astype(BF16)
        dpxb = dpx.astype(BF16)
        dxc = dxc + _dot_nt(dpab, wga_ref[...]) + _dot_nt(dpxb, wgx_ref[...])
        dwga_ref[...] = _dot_tn(xcb, dpab)
        dwgx_ref[...] = _dot_tn(xcb, dpxb)
        xr = xr_ref[...]
        dxr = cw_ref[CONV_WIDTH - 1:CONV_WIDTH, :] * dxc
        for k in range(1, CONV_WIDTH):
            dxr = dxr + cw_ref[CONV_WIDTH - 1 - k:CONV_WIDTH - k, :] * _shift_up(dxc, k, tp)
        dxr_ref[...] = dxr.astype(BF16)
        for k in range(CONV_WIDTH):
            vec_ref[k:k + 1, :] = jnp.sum(dxc * _shift_down(xr, CONV_WIDTH - 1 - k, tp), axis=0, keepdims=True)
        vec_ref[4:5, :] = jnp.sum(dxc, axis=0, keepdims=True)
        vec_ref[5:6, :] = jnp.sum(dpa, axis=0, keepdims=True)
        vec_ref[6:7, :] = jnp.sum(dpx, axis=0, keepdims=True)
        vec_ref[7:8, :] = dl

    col = lambda g: (0, g)
    vec = pl.BlockSpec((1, LANES), col)
    big = pl.BlockSpec((tp, LANES), col)
    sq = pl.BlockSpec((None, LANES, LANES), lambda g: (g, 0, 0))
    return pl.pallas_call(
        body, name="rec_bwd", grid=(ng,),
        in_specs=[big, big, big, big, pl.BlockSpec((tp, LANES), lambda g: (0, ng + g)),
                  pl.BlockSpec((CONV_WIDTH, LANES), col), vec, sq, vec, sq, vec, vec],
        out_specs=[big, big, sq, sq, pl.BlockSpec((None, SUBLANES, LANES), lambda g: (g, 0, 0))],
        out_shape=[jax.ShapeDtypeStruct((tp, rw), BF16), jax.ShapeDtypeStruct((tp, rw), BF16),
                   jax.ShapeDtypeStruct((ng, LANES, LANES), F32), jax.ShapeDtypeStruct((ng, LANES, LANES), F32),
                   jax.ShapeDtypeStruct((ng, SUBLANES, LANES), F32)],
        scratch_shapes=[pltpu.VMEM((tp, LANES), F32)] * 3,
        compiler_params=_cparams("parallel"),
    )(drec, hr, xc, rest, rest, convw, convb, wga, bga, wgx, bgx, lru)


def attn_bwd(qkv, do, o, lset, c, ct, nh):
    tp = qkv.shape[0]
    npair = nh // 2
    aw = nh * HEAD_DIM
    tiles = _att_tiles(tp)

    def body(q_ref, k_ref, v_ref, do_ref, o_ref, lset_ref, c_ref, ct_ref,
             dq_ref, dk_ref, dv_ref, drow_ref, dcol_ref, dk_acc, dv_acc):
        p = pl.program_id(0)
        dk_acc[...] = jnp.zeros_like(dk_acc)
        dv_acc[...] = jnp.zeros_like(dv_acc)
        dcol_ref[...] = jnp.zeros_like(dcol_ref)
        drow_ref[...] = jnp.zeros_like(drow_ref)
        for r0, nr, nk in tiles:
            rs = slice(r0, r0 + nr)
            causal = (r0 + lax.broadcasted_iota(jnp.int32, (nk, nr), 1)
                      >= lax.broadcasted_iota(jnp.int32, (nk, nr), 0))
            cblk = c_ref[0:nk, :]
            ctb = ct_ref[:, rs]
            for hh in range(2):
                head = 2 * p + hh
                hs = slice(hh * HEAD_DIM, (hh + 1) * HEAD_DIM)
                q = q_ref[rs, hs]
                k = k_ref[0:nk, hs]
                dof = do_ref[rs, hs]
                do16 = dof.astype(BF16)
                delta = jnp.sum(dof * o_ref[rs, hs], axis=1, keepdims=True)
                delta_row = jnp.broadcast_to(delta, (nr, LANES)).T[0:1, :]
                s_t = _dot_nt(k, q * ATT_SCALE) + (_pick_row(ctb, head) - _pick_col(cblk, head))
                p_t = jnp.where(causal, jnp.exp(s_t - lset_ref[hh:hh + 1, rs]), 0.0)
                ds_t = p_t * (_dot_nt(v_ref[0:nk, hs], do16) - delta_row)
                p16 = p_t.astype(BF16)
                ds16 = ds_t.astype(BF16)
                dv_acc[0:nk, hs] += _dot(p16, do16)
                dk_acc[0:nk, hs] += _dot(ds16, q) * ATT_SCALE
                dq_ref[rs, hs] = (_dot_tn(ds16, k) * ATT_SCALE).astype(BF16)
                drow_ref[hh:hh + 1, rs] = jnp.sum(ds_t, axis=0, keepdims=True)
                dcol_ref[0:nk, hs] -= jnp.broadcast_to(jnp.sum(ds_t, axis=1, keepdims=True), (nk, HEAD_DIM))
        dk_ref[...] = dk_acc[...].astype(BF16)
        dv_ref[...] = dv_acc[...].astype(BF16)

    pair = lambda p: (0, p)
    return pl.pallas_call(
        body, name="attn_bwd", grid=(npair,),
        in_specs=[pl.BlockSpec((tp, LANES), pair),
                  pl.BlockSpec((tp, LANES), lambda p: (0, npair + p)),
                  pl.BlockSpec((tp, LANES), lambda p: (0, 2 * npair + p)),
                  pl.BlockSpec((tp, LANES), pair),
                  pl.BlockSpec((tp, LANES), pair),
                  pl.BlockSpec((None, SUBLANES, tp), lambda p: (p, 0, 0)),
                  pl.BlockSpec((tp, LANES), lambda p: (0, 0)),
                  pl.BlockSpec((SUBLANES, tp), lambda p: (0, 0))],
        out_specs=[pl.BlockSpec((tp, LANES), pair), pl.BlockSpec((tp, LANES), pair),
                   pl.BlockSpec((tp, LANES), pair),
                   pl.BlockSpec((None, SUBLANES, tp), lambda p: (p, 0, 0)),
                   pl.BlockSpec((tp, LANES), pair)],
        out_shape=[jax.ShapeDtypeStruct((tp, aw), BF16), jax.ShapeDtypeStruct((tp, aw), BF16),
                   jax.ShapeDtypeStruct((tp, aw), BF16),
                   jax.ShapeDtypeStruct((npair, SUBLANES, tp), F32),
                   jax.ShapeDtypeStruct((tp, aw), F32)],
        scratch_shapes=[pltpu.VMEM((tp, LANES), F32), pltpu.VMEM((tp, LANES), F32)],
        compiler_params=_cparams("parallel"),
    )(qkv, qkv, qkv, do, o, lset, c, ct)


def fgate_bwd(dct8, drs, rest, bf_pad, fcol):
    tp = rest.shape[0]
    aw = drs.shape[1]
    nb = tp // ATT_BLOCK
    B = ATT_BLOCK

    def body(d_ref, drs_ref, f_ref, b_ref, dfl_ref, db_ref, pad_s):
        r_i = lax.broadcasted_iota(jnp.int32, (B, B), 0)
        c_i = lax.broadcasted_iota(jnp.int32, (B, B), 1)
        triu = (c_i >= r_i).astype(BF16)
        sel = (lax.broadcasted_iota(jnp.int32, (aw, LANES), 0)
               == HEAD_DIM * lax.broadcasted_iota(jnp.int32, (aw, LANES), 1)).astype(BF16)
        carry = jnp.zeros((1, LANES), F32)
        db = jnp.zeros((1, LANES), F32)
        pad_s[...] = jnp.zeros_like(pad_s)
        for i in range(nb - 1, -1, -1):
            sl = slice(i * B, (i + 1) * B)
            pad_s[0:SUBLANES, :] = d_ref[:, sl]
            dc = pad_s[...].T + _dot_split3(drs_ref[sl, :], sel)
            rc = _split3_dot(triu, dc)
            dlf = rc + carry
            carry = carry + rc[0:1, :]
            dfl = dlf * _sigmoid(-(f_ref[sl, :] + b_ref[...]))
            dfl_ref[sl, :] = dfl.astype(BF16)
            db = db + jnp.sum(dfl, axis=0, keepdims=True)
        db_ref[...] = db

    return pl.pallas_call(
        body, name="fgate_bwd", grid=(1,),
        in_specs=[pl.BlockSpec((SUBLANES, tp), lambda i: (0, 0)),
                  pl.BlockSpec((tp, aw), lambda i: (0, 0)),
                  pl.BlockSpec((tp, LANES), lambda i: (0, fcol)),
                  pl.BlockSpec((1, LANES), lambda i: (0, 0))],
        out_specs=[pl.BlockSpec((tp, LANES), lambda i: (0, 0)),
                   pl.BlockSpec((1, LANES), lambda i: (0, 0))],
        out_shape=[jax.ShapeDtypeStruct((tp, LANES), BF16), jax.ShapeDtypeStruct((1, LANES), F32)],
        scratch_shapes=[pltpu.VMEM((B, B), F32)],
        compiler_params=_cparams("arbitrary"),
    )(dct8, drs, rest, bf_pad)


def in_proj_bwd(dh2, parts, w_in_t, wrest_t, h, g1, tm):
    tp, d = h.shape
    dq, dk, dv, dxr, dyr, dfl = parts
    aw, rw = dq.shape[1], dxr.shape[1]

    def body(dh2_ref, dq_ref, dk_ref, dv_ref, dxr_ref, dyr_ref, dfl_ref, wq_ref, wr_ref, h_ref, g_ref,
             dh_ref, dg_ref):
        i = pl.program_id(0)
        dz = _dot(dq_ref[...], wq_ref[0:aw, :])
        dz += _dot(dk_ref[...], wq_ref[aw:2 * aw, :])
        dz += _dot(dv_ref[...], wq_ref[2 * aw:3 * aw, :])
        dz += _dot(dxr_ref[...], wr_ref[0:rw, :])
        dz += _dot(dyr_ref[...], wr_ref[rw:2 * rw, :])
        dz += _dot(dfl_ref[...], wr_ref[2 * rw:2 * rw + LANES, :])
        dx, dg = _rms_bwd(h_ref[...], g_ref[...], dz)
        dh_ref[...] = dh2_ref[...] + dx
        _accumulate(dg_ref, dg, i == 0)

    row = lambda i: (i, 0)
    fix = lambda i: (0, 0)
    return pl.pallas_call(
        body, name="in_proj_bwd", grid=(tp // tm,),
        in_specs=[pl.BlockSpec((tm, d), row),
                  pl.BlockSpec((tm, aw), row), pl.BlockSpec((tm, aw), row), pl.BlockSpec((tm, aw), row),
                  pl.BlockSpec((tm, rw), row), pl.BlockSpec((tm, rw), row), pl.BlockSpec((tm, LANES), row),
                  pl.BlockSpec((3 * aw, d), fix), pl.BlockSpec(wrest_t.shape, fix),
                  pl.BlockSpec((tm, d), row), pl.BlockSpec((1, d), fix)],
        out_specs=[pl.BlockSpec((tm, d), row), pl.BlockSpec((1, d), fix)],
        out_shape=[jax.ShapeDtypeStruct((tp, d), F32), jax.ShapeDtypeStruct((1, d), F32)],
        compiler_params=_cparams("arbitrary"),
    )(dh2, dq, dk, dv, dxr, dyr, dfl, w_in_t, wrest_t, h, g1)


def dw_in_t(z, parts, nh, tr):
    tp, d = z.shape
    dq, dk, dv, dxr, dyr, dfl = parts
    aw, rw = dq.shape[1], dxr.shape[1]
    d_in = 3 * aw + nh + 2 * rw
    nr = tp // tr
    offs = [(0, aw), (aw, aw), (2 * aw, aw), (3 * aw + nh, rw), (3 * aw + nh + rw, rw)]

    def body(z_ref, dq_ref, dk_ref, dv_ref, dxr_ref, dyr_ref, dfl_ref, o_ref, acc):
        r = pl.program_id(0)

        @pl.when(r == 0)
        def _():
            acc[...] = jnp.zeros_like(acc)

        zt = z_ref[...]
        for (o, n), ref in zip(offs, (dq_ref, dk_ref, dv_ref, dxr_ref, dyr_ref)):
            acc[o:o + n, :] += _dot_tn(ref[...], zt)
        acc[3 * aw:3 * aw + nh, :] += _dot_tn(dfl_ref[...], zt)[0:nh, :]

        @pl.when(r == nr - 1)
        def _():
            o_ref[...] = acc[...].astype(BF16)

    row = lambda r: (r, 0)
    return pl.pallas_call(
        body, name="dw_in", grid=(nr,),
        in_specs=[pl.BlockSpec((tr, d), row),
                  pl.BlockSpec((tr, aw), row), pl.BlockSpec((tr, aw), row), pl.BlockSpec((tr, aw), row),
                  pl.BlockSpec((tr, rw), row), pl.BlockSpec((tr, rw), row), pl.BlockSpec((tr, LANES), row)],
        out_specs=pl.BlockSpec((d_in, d), lambda r: (0, 0)),
        out_shape=jax.ShapeDtypeStruct((d_in, d), BF16),
        scratch_shapes=[pltpu.VMEM((d_in, d), F32)],
        compiler_params=_cparams("arbitrary"),
    )(z, dq, dk, dv, dxr, dyr, dfl)


def _place():
    return lax.axis_index("x"), lax.axis_index("y"), lax.axis_index("c")


HBM = pl.BlockSpec(memory_space=pltpu.HBM)
SEM = pl.BlockSpec(memory_space=pltpu.SEMAPHORE)
EFFECT = pltpu.SideEffectType.DATAFLOW_SIDE_EFFECTING


def _in_hbm(a):
    return pltpu.with_memory_space_constraint(a, pltpu.HBM)


def _gather_targets(x, y, c):
    return [(x, y, 1 - c), (1 - x, y, c), (x, 1 - y, c), (1 - x, 1 - y, c)]


def _slot(t):
    return 4 * t[0] + 2 * t[1] + t[2]


def gather_start(groups):
    flat = [a for g in groups for a in g]
    n = len(flat)
    ng = len(groups)
    lands = [lax.empty((N_DEV,) + a.shape, a.dtype) for a in flat]

    def body(*refs):
        src, land = refs[:n], refs[n:2 * n]
        sems = refs[2 * n:2 * n + 2 * ng]
        token = refs[-1]
        x, y, c = _place()
        me = 4 * x + 2 * y + c
        i = 0
        for gi, g in enumerate(groups):
            for a in range(len(g)):
                for k, t in enumerate(_gather_targets(x, y, c)):
                    pltpu.make_async_remote_copy(
                        src_ref=src[i], dst_ref=land[i].at[me],
                        send_sem=sems[2 * gi].at[4 * a + k], recv_sem=sems[2 * gi + 1].at[4 * a + k],
                        device_id=t, device_id_type=MESH).start()
                i += 1
        token[...] = jnp.zeros_like(token)

    sem_shapes = []
    for g in groups:
        sem_shapes += [pltpu.SemaphoreType.DMA((4 * len(g),)), pltpu.SemaphoreType.DMA((4 * len(g),))]
    out = pl.pallas_call(
        body, name="gather_start",
        out_shape=sem_shapes + [pltpu.HBM(a.shape, a.dtype) for a in flat + lands]
        + [jax.ShapeDtypeStruct((SUBLANES, LANES), F32)],
        in_specs=[HBM] * (2 * n),
        out_specs=[SEM] * (2 * ng) + [HBM] * (2 * n) + [pl.BlockSpec(memory_space=pltpu.VMEM)],
        input_output_aliases={i: 2 * ng + i for i in range(2 * n)},
        compiler_params=pltpu.CompilerParams(has_side_effects=EFFECT),
    )(*[_in_hbm(a) for a in flat + lands])
    sems = out[:2 * ng]
    thru = out[2 * ng:2 * ng + 2 * n]
    srcs_t, lands_t = thru[:n], thru[n:]
    res, i = [], 0
    for gi, g in enumerate(groups):
        res.append((sems[2 * gi], sems[2 * gi + 1], srcs_t[i:i + len(g)], lands_t[i:i + len(g)]))
        i += len(g)
    return res, out[-1]


def gather_wait(send, recv, srcs, lands, after, name):
    n = len(srcs)

    def body(*refs):
        src, land = refs[:n], refs[n:2 * n]
        send_sem, recv_sem = refs[2 * n], refs[2 * n + 1]
        x, y, c = _place()
        for a in range(n):
            for k, t in enumerate(_gather_targets(x, y, c)):
                cp = pltpu.make_async_remote_copy(
                    src_ref=src[a], dst_ref=land[a].at[_slot(t)],
                    send_sem=send_sem.at[4 * a + k], recv_sem=recv_sem.at[4 * a + k],
                    device_id=t, device_id_type=MESH)
                cp.wait_send()
                cp.wait_recv()

    out = pl.pallas_call(
        body, name=name,
        out_shape=[pltpu.HBM(a.shape, a.dtype) for a in list(srcs) + list(lands)],
        in_specs=[HBM] * (2 * n) + [SEM, SEM, ANY],
        out_specs=[HBM] * (2 * n),
        input_output_aliases={i: i for i in range(2 * n)},
        compiler_params=pltpu.CompilerParams(has_side_effects=EFFECT),
    )(*srcs, *lands, send, recv, after)
    return out[:n], out[n:]


def gather_forward(lands, name):
    n = len(lands)

    def body(*refs):
        land, out = refs[:n], refs[n:2 * n]
        send_sems, recv_sems = refs[2 * n:]
        x, y, c = _place()
        sibling = (x, y, 1 - c)
        chips = [(1 - x, y), (x, 1 - y), (1 - x, 1 - y)]

        def fwd(a, j, core):
            s = _slot((*chips[j], core))
            return pltpu.make_async_remote_copy(src_ref=land[a].at[s], dst_ref=out[a].at[s],
                                                send_sem=send_sems.at[a, j], recv_sem=recv_sems.at[a, j],
                                                device_id=sibling, device_id_type=MESH)

        sends = [fwd(a, j, c) for a in range(n) for j in range(3)]
        for cp in sends:
            cp.start()
        for a in range(n):
            for j in range(3):
                fwd(a, j, 1 - c).wait_recv()
        for cp in sends:
            cp.wait_send()

    return pl.pallas_call(
        body, name=name,
        in_specs=[ANY] * n, out_specs=[ANY] * n,
        out_shape=[jax.ShapeDtypeStruct(a.shape, a.dtype) for a in lands],
        input_output_aliases={i: i for i in range(n)},
        scratch_shapes=[pltpu.SemaphoreType.DMA((n, 3)), pltpu.SemaphoreType.DMA((n, 3))],
    )(*lands)


def _relations():
    return [(dx, dy, dc) for dx in (0, 1) for dy in (0, 1) for dc in (0, 1) if dx + dy + dc]


def _peer(x, y, c, rel):
    return ((1 - x) if rel[0] else x, (1 - y) if rel[1] else y, (1 - c) if rel[2] else c)


def exchange_start(srcs, lands, layer, name):
    n = len(srcs)

    def body(*refs):
        src, land = refs[:n], refs[n:2 * n]
        send_sem, recv_sem = refs[2 * n], refs[2 * n + 1]
        token = refs[-1]
        x, y, c = _place()
        me = 4 * x + 2 * y + c
        for k, rel in enumerate(_relations()):
            peer = _peer(x, y, c, rel)
            for a in range(n):
                pltpu.make_async_remote_copy(
                    src_ref=src[a] if layer is None else src[a].at[_slot(peer)],
                    dst_ref=land[a].at[me] if layer is None else land[a].at[me, layer],
                    send_sem=send_sem.at[7 * a + k], recv_sem=recv_sem.at[7 * a + k],
                    device_id=peer, device_id_type=MESH).start()
        token[...] = jnp.zeros_like(token)

    out = pl.pallas_call(
        body, name=name,
        out_shape=[pltpu.SemaphoreType.DMA((7 * n,)), pltpu.SemaphoreType.DMA((7 * n,))]
        + [pltpu.HBM(a.shape, a.dtype) for a in list(srcs) + list(lands)]
        + [jax.ShapeDtypeStruct((SUBLANES, LANES), F32)],
        in_specs=[HBM] * (2 * n),
        out_specs=[SEM, SEM] + [HBM] * (2 * n) + [pl.BlockSpec(memory_space=pltpu.VMEM)],
        input_output_aliases={i: 2 + i for i in range(2 * n)},
        compiler_params=pltpu.CompilerParams(has_side_effects=EFFECT),
    )(*[_in_hbm(a) for a in list(srcs) + list(lands)])
    return out[0], out[1], out[2:2 + n], out[2 + n:2 + 2 * n], out[-1][0, 0]


def exchange_wait(send, recv, srcs, lands, after, layer, name):
    n = len(srcs)

    def body(*refs):
        src, land = refs[:n], refs[n:2 * n]
        send_sem, recv_sem = refs[2 * n], refs[2 * n + 1]
        x, y, c = _place()
        for k, rel in enumerate(_relations()):
            peer = _peer(x, y, c, rel)
            for a in range(n):
                cp = pltpu.make_async_remote_copy(
                    src_ref=src[a] if layer is None else src[a].at[_slot(peer)],
                    dst_ref=land[a].at[_slot(peer)] if layer is None else land[a].at[_slot(peer), layer],
                    send_sem=send_sem.at[7 * a + k], recv_sem=recv_sem.at[7 * a + k],
                    device_id=peer, device_id_type=MESH)
                cp.wait_send()
                cp.wait_recv()

    out = pl.pallas_call(
        body, name=name,
        out_shape=[pltpu.HBM(a.shape, a.dtype) for a in list(srcs) + list(lands)],
        in_specs=[HBM] * (2 * n) + [SEM, SEM, ANY],
        out_specs=[HBM] * (2 * n),
        input_output_aliases={i: i for i in range(2 * n)},
        compiler_params=pltpu.CompilerParams(has_side_effects=EFFECT),
    )(*srcs, *lands, send, recv, after)
    return out[:n], out[n:]


def _adamw_math(g, w, m, v):
    m = ADAM_B1 * m + (1.0 - ADAM_B1) * g
    v = ADAM_B2 * v + (1.0 - ADAM_B2) * (g * g)
    m_hat = m / (1.0 - ADAM_B1 ** ADAM_STEP)
    v_hat = v / (1.0 - ADAM_B2 ** ADAM_STEP)
    delta = -ADAM_LR * (m_hat / (jnp.sqrt(v_hat) + ADAM_EPS) + ADAM_WD * w)
    return delta, m, v


def sum_adamw(parts, w, m, v, tr, name):
    npart, rows, cols = parts.shape

    def body(p_ref, w_ref, m_ref, v_ref, g_ref, d_ref, nm_ref, nv_ref):
        g = p_ref[0].astype(F32)
        for p in range(1, npart):
            g = g + p_ref[p].astype(F32)
        delta, nm, nv = _adamw_math(g, w_ref[...], m_ref[...], v_ref[...])
        g_ref[...] = g
        d_ref[...] = delta
        nm_ref[...] = nm
        nv_ref[...] = nv

    blk = pl.BlockSpec((tr, cols), lambda i: (i, 0))
    return pl.pallas_call(
        body, name=name, grid=(rows // tr,),
        in_specs=[pl.BlockSpec((npart, tr, cols), lambda i: (0, i, 0)), blk, blk, blk],
        out_specs=[blk] * 4,
        out_shape=[jax.ShapeDtypeStruct((rows, cols), F32)] * 4,
        compiler_params=_cparams("parallel"),
    )(parts, w, m, v)


def sum_adamw_t(parts, w, m, v, name):
    npart, nl, rows, cols = parts.shape

    def body(p_ref, w_ref, m_ref, v_ref, g_ref, d_ref, nm_ref, nv_ref):
        g = p_ref[0].astype(F32)
        for p in range(1, npart):
            g = g + p_ref[p].astype(F32)
        delta, nm, nv = _adamw_math(g, w_ref[...], m_ref[...], v_ref[...])
        g_ref[...] = g
        d_ref[...] = delta
        nm_ref[...] = nm
        nv_ref[...] = nv

    blk = pl.BlockSpec((None, rows, cols), lambda l: (l, 0, 0))
    return pl.pallas_call(
        body, name=name, grid=(nl,),
        in_specs=[pl.BlockSpec((npart, None, rows, cols), lambda l: (0, l, 0, 0)), blk, blk, blk],
        out_specs=[blk] * 4,
        out_shape=[jax.ShapeDtypeStruct((nl, rows, cols), F32)] * 4,
        compiler_params=_cparams("parallel"),
    )(parts, w, m, v)


def sum_parts(parts, name):
    npart, rows, cols = parts.shape

    def body(p_ref, g_ref):
        g = p_ref[0].astype(F32)
        for p in range(1, npart):
            g = g + p_ref[p].astype(F32)
        g_ref[...] = g

    return pl.pallas_call(
        body, name=name, grid=(1,),
        in_specs=[pl.BlockSpec((npart, rows, cols), lambda i: (0, 0, 0))],
        out_specs=pl.BlockSpec((rows, cols), lambda i: (0, 0)),
        out_shape=jax.ShapeDtypeStruct((rows, cols), F32),
        compiler_params=_cparams("arbitrary"),
    )(parts)


def _round_up(n, m):
    return (n + m - 1) // m * m


def _block_diag_pairs(w):
    nb, b, _ = w.shape
    per = LANES // b
    ng = nb // per
    w = w.reshape(ng, per, b, b)
    eye = jnp.eye(per, dtype=w.dtype)
    out = jnp.einsum('gpij,pq->gpiqj', w, eye).reshape(ng, LANES, LANES)
    return out.astype(BF16)


def _block_diag_extract(g, b):
    ng = g.shape[0]
    per = LANES // b
    g = g.reshape(ng, per, b, per, b)
    idx = jnp.arange(per)
    return g[:, idx, :, idx, :].transpose(1, 0, 2, 3).reshape(ng * per, b, b)


def _tiles(v):
    v = v.reshape(-1)
    n = _round_up(v.shape[0], SUBLANES * LANES)
    return jnp.pad(v, (0, n - v.shape[0])).reshape(-1, LANES)


SMALL = ['attn_norm_g', 'b_f', 'conv_w', 'conv_b', 'w_gate_a', 'b_gate_a', 'w_gate_x', 'b_gate_x',
         'lru_L', 'attn_out_g', 'rec_out_g', 'mlp_norm_g', 'final_g', 'meta']


def _pack(d):
    return jnp.concatenate([_tiles(d[n]) for n in SMALL], axis=0)


def _unpack(vec, shapes):
    out, r = {}, 0
    for n in SMALL:
        size = math.prod(shapes[n])
        nr = _round_up(size, SUBLANES * LANES) // LANES
        out[n] = vec[r:r + nr].reshape(-1)[:size].reshape(shapes[n])
        r += nr
    return out


def _row_tile(tp):
    return tp // 4 if (tp // 4) % 16 == 0 else tp


def local_step(x, tgt, meta, small, hooks):
    s, d = x.shape
    t_real = s + N_META
    tp = _round_up(t_real, ATT_BLOCK)
    depth = small['attn_norm_g'].shape[0]
    nh = small['b_f'].shape[1]
    rw = small['conv_b'].shape[1]
    blk = small['w_gate_a'].shape[2]
    tm = _row_tile(tp)
    tm2 = tp // 2
    fcol = 2 * rw // LANES

    h = jnp.concatenate([meta, x, jnp.zeros((tp - t_real, d), F32)], axis=0)
    tgt_p = jnp.pad(tgt, ((N_META, tp - t_real), (0, 0)))
    row = lambda v: v.reshape(1, -1)
    bf_pad = jnp.pad(small['b_f'], ((0, 0), (0, LANES - nh)))

    saved = []
    for l in range(depth):
        w_in_t, wrest_t, wout = hooks.mixer_weights(l, h)
        wga = _block_diag_pairs(small['w_gate_a'][l])
        wgx = _block_diag_pairs(small['w_gate_x'][l])
        z, qkv, rest = in_proj(h, row(small['attn_norm_g'][l]), w_in_t, wrest_t, 3 * nh * HEAD_DIM, tm)
        c, ct = fgate_fwd(rest, bf_pad[l:l + 1], fcol)
        o, lset = attn_fwd(qkv, c, ct, nh)
        rec, hr, xc = rec_fwd(rest, small['conv_w'][l], row(small['conv_b'][l]), wga, row(small['b_gate_a'][l]),
                              wgx, row(small['b_gate_x'][l]), row(small['lru_L'][l]), rw)
        h2, mix, z2 = out_proj(h, o, rec, row(small['attn_out_g'][l]), row(small['rec_out_g'][l]), wout,
                               row(small['mlp_norm_g'][l]), tm)
        gup, gdown = hooks.mlp_weights(l, z2)
        u, h3 = mlp_fwd(z2, h2, gup, gdown, tm2)
        saved.append(dict(h=h, z=z, qkv=qkv, rest=rest, c=c, ct=ct, o=o, lset=lset, rec=rec, hr=hr, xc=xc,
                          h2=h2, mix=mix, z2=z2, u=u, wga=wga, wgx=wgx,
                          w_in_t=w_in_t, wrest_t=wrest_t, wout=wout, gup=gup, gdown=gdown))
        h = h3

    dh, dgf, loss = loss_head(h, row(small['final_g']), tgt_p, t_real, tm)

    gs = {n: [None] * depth for n in SMALL if n not in ('final_g', 'meta')}
    tok = jnp.zeros((), F32)
    for l in reversed(range(depth)):
        sv = saved[l]
        gup, gdown = sv['gup'], sv['gdown']
        tf = gup.shape[2]
        dup, dh2, dg2 = mlp_bwd(dh, sv['u'], sv['h2'], row(small['mlp_norm_g'][l]) + tok, gup, gdown, tm2)
        gs['mlp_norm_g'][l] = dg2[0]
        do, drec, dga, dgr = out_proj_bwd(dh2, sv['o'], sv['rec'], row(small['attn_out_g'][l]),
                                          row(small['rec_out_g'][l]), sv['wout'], tm)
        gs['attn_out_g'][l] = dga[0]
        gs['rec_out_g'][l] = dgr[0]
        blocks = dict(
            w_down=mm_tn(sv['u'], dh, tk=tf, tn=d, tr=tm, out_dtype=BF16, name="dw_down",
                         square_a=True).reshape(N_DEV, tf, d),
            w_up=mm_tn(sv['z2'], dup, tk=d, tn=tf, tr=tm, out_dtype=BF16, name="dw_up", blocked_n=True),
            w_out=mm_tn(sv['mix'], dh2, tk=d, tn=d // 2, tr=tm, out_dtype=BF16,
                        name="dw_out").reshape(N_DEV, d // N_DEV, d))
        tok = hooks.grads_ready(l, 'mlp', blocks)
        dxr, dyr, dwga, dwgx, vec = rec_bwd(drec, sv['hr'], sv['xc'], sv['rest'], small['conv_w'][l],
                                            row(small['conv_b'][l]) + tok, sv['wga'], row(small['b_gate_a'][l]),
                                            sv['wgx'], row(small['b_gate_x'][l]), row(small['lru_L'][l]), rw)
        gs['w_gate_a'][l] = _block_diag_extract(dwga, blk)
        gs['w_gate_x'][l] = _block_diag_extract(dwgx, blk)
        vec = vec.transpose(1, 0, 2).reshape(SUBLANES, rw)
        gs['conv_w'][l] = vec[0:CONV_WIDTH]
        gs['conv_b'][l] = vec[4]
        gs['b_gate_a'][l] = vec[5]
        gs['b_gate_x'][l] = vec[6]
        gs['lru_L'][l] = vec[7]
        dq, dk, dv, drow, dcol = attn_bwd(sv['qkv'], do, sv['o'], sv['lset'], sv['c'], sv['ct'] + tok, nh)
        drow8 = drow[:, 0:2, :].reshape(nh, tp)
        if nh < SUBLANES:
            drow8 = jnp.pad(drow8, ((0, SUBLANES - nh), (0, 0)))
        dfl, dbf = fgate_bwd(drow8, dcol, sv['rest'], bf_pad[l:l + 1], fcol)
        gs['b_f'][l] = dbf[0, 0:nh]
        parts = (dq, dk, dv, dxr, dyr, dfl)
        dh, dg1 = in_proj_bwd(dh2, parts, sv['w_in_t'], sv['wrest_t'], sv['h'], row(small['attn_norm_g'][l]), tm)
        gs['attn_norm_g'][l] = dg1[0]
        tok = jnp.zeros((), F32)
        if l == 0:
            grads = {n: jnp.stack(v) for n, v in gs.items()}
            grads['final_g'] = dgf[0]
            grads['meta'] = dh[0:N_META]
            tok = hooks.small_ready(grads)
        dw_in = dw_in_t(sv['z'], parts, nh, tm)
        dw_in = dw_in.reshape(N_DEV, dw_in.shape[0] // N_DEV, d) + tok.astype(BF16)
        tok = hooks.grads_ready(l, 'in', dict(w_in=dw_in))

    return loss[0, 0], dh, tok


def prep_weights(g_in, g_out, nh, rw):
    d = g_in.shape[2]
    w_in_t = g_in.reshape(-1, d)
    f0 = 3 * nh * HEAD_DIM
    wrest_t = jnp.concatenate([w_in_t[f0 + nh:f0 + nh + 2 * rw],
                               jnp.pad(w_in_t[f0:f0 + nh], ((0, LANES - nh), (0, 0)))], axis=0)
    return w_in_t, wrest_t, g_out.reshape(d, d)


BIG = ['w_in', 'w_out', 'w_up', 'w_down']
EXCHANGE_GROUPS = {'mlp': ['w_down', 'w_up', 'w_out'], 'in': ['w_in']}
WEIGHTS = ['meta', 'attn_norm_g', 'w_in', 'b_f', 'conv_w', 'conv_b', 'w_gate_a', 'b_gate_a', 'w_gate_x', 'b_gate_x',
           'lru_L', 'attn_out_g', 'rec_out_g', 'w_out', 'mlp_norm_g', 'w_up', 'w_down', 'final_g']


def _set_own(arr, own, me):
    return lax.dynamic_update_slice_in_dim(arr, own[None], me, axis=0)


class _Step:
    def __init__(self, w, nh, rw, me):
        self.w, self.nh, self.rw, self.me = w, nh, rw, me
        depth = w['w_in'].shape[0]
        groups = []
        for l in range(depth):
            groups.append([w['w_in_t'][:, l, :].astype(BF16), w['w_out'][l].astype(BF16)])
            groups.append([w['w_up'][l].astype(BF16), w['w_down'][l].astype(BF16)])
        groups[0] = groups[0] + [w['meta'], w['conv_w']]
        self.pending, _ = gather_start(groups)
        self.gathered = {}
        self.lands = {n: lax.empty((N_DEV,) + w[n].shape, BF16) for n in BIG}
        din8, _, d = w['w_in_t'].shape
        self.lands['w_in'] = lax.empty((N_DEV, depth, din8, d), BF16)
        self.started = []
        self.small = None

    def group(self, gi, after):
        if gi not in self.gathered:
            send, recv, srcs, lands = self.pending[gi]
            srcs, lands = gather_wait(send, recv, srcs, lands, after, "gather_wait_%d" % gi)
            lands = gather_forward(lands, "gather_forward_%d" % gi)
            self.gathered[gi] = [_set_own(g, own, self.me) for g, own in zip(lands, srcs)]
        return self.gathered[gi]

    def mixer_weights(self, l, after):
        g = self.group(2 * l, after)
        return prep_weights(g[0], g[1], self.nh, self.rw)

    def mlp_weights(self, l, after):
        g = self.group(2 * l + 1, after)
        return g[0], g[1]

    def grads_ready(self, l, group, blocks):
        names = EXCHANGE_GROUPS[group]
        send, recv, srcs, lands, token = exchange_start(
            [blocks[n] for n in names], [self.lands[n] for n in names], l, "exchange_start_%s_%d" % (group, l))
        for n, a in zip(names, lands):
            self.lands[n] = a
        self.started.append((l, group, send, recv, srcs))
        return token

    def small_ready(self, grads):
        self.small_shapes = {n: grads[n].shape for n in SMALL}
        packed = _pack(grads).astype(BF16)
        send, recv, srcs, lands, token = exchange_start(
            [packed], [lax.empty((N_DEV,) + packed.shape, BF16)], None, "small_start")
        self.small = (send, recv, srcs, lands)
        return token

    def small_sum(self, after):
        send, recv, srcs, lands = self.small
        srcs, lands = exchange_wait(send, recv, srcs, lands, after, None, "small_wait")
        parts = _set_own(lands[0], srcs[0], self.me)
        return _unpack(sum_parts(parts, "sum_small_grads"), self.small_shapes)

    def received(self, group, after):
        names = EXCHANGE_GROUPS[group]
        own = {n: [None] * self.w[n].shape[0] for n in names}
        for l, grp, send, recv, srcs in self.started:
            if grp != group:
                continue
            srcs, lands = exchange_wait(send, recv, srcs, [self.lands[n] for n in names], after, l,
                                        "exchange_wait_%s_%d" % (group, l))
            for n, a, sr in zip(names, lands, srcs):
                self.lands[n] = a
                own[n][l] = lax.dynamic_index_in_dim(sr, self.me, 0, keepdims=False)
        return {n: _set_own(self.lands[n], jnp.stack(own[n]), self.me) for n in names}


def kernel(x, meta, attn_norm_g, w_in, b_f, conv_w, conv_b, w_gate_a, b_gate_a, w_gate_x, b_gate_x, lru_L, attn_out_g, rec_out_g, w_out, mlp_norm_g, w_up, w_down, final_g, loss_target, m_meta, m_attn_norm_g, m_w_in, m_b_f, m_conv_w, m_conv_b, m_w_gate_a, m_b_gate_a, m_w_gate_x, m_b_gate_x, m_lru_L, m_attn_out_g, m_rec_out_g, m_w_out, m_mlp_norm_g, m_w_up, m_w_down, m_final_g, v_meta, v_attn_norm_g, v_w_in, v_b_f, v_conv_w, v_conv_b, v_w_gate_a, v_b_gate_a, v_w_gate_x, v_b_gate_x, v_lru_L, v_attn_out_g, v_rec_out_g, v_w_out, v_mlp_norm_g, v_w_up, v_w_down, v_final_g):
    w = dict(meta=meta, attn_norm_g=attn_norm_g, w_in=w_in, b_f=b_f, conv_w=conv_w, conv_b=conv_b,
             w_gate_a=w_gate_a, b_gate_a=b_gate_a, w_gate_x=w_gate_x, b_gate_x=b_gate_x, lru_L=lru_L,
             attn_out_g=attn_out_g, rec_out_g=rec_out_g, w_out=w_out, mlp_norm_g=mlp_norm_g, w_up=w_up,
             w_down=w_down, final_g=final_g)
    mo = dict(meta=m_meta, attn_norm_g=m_attn_norm_g, w_in=m_w_in, b_f=m_b_f, conv_w=m_conv_w, conv_b=m_conv_b,
              w_gate_a=m_w_gate_a, b_gate_a=m_b_gate_a, w_gate_x=m_w_gate_x, b_gate_x=m_b_gate_x, lru_L=m_lru_L,
              attn_out_g=m_attn_out_g, rec_out_g=m_rec_out_g, w_out=m_w_out, mlp_norm_g=m_mlp_norm_g,
              w_up=m_w_up, w_down=m_w_down, final_g=m_final_g)
    vo = dict(meta=v_meta, attn_norm_g=v_attn_norm_g, w_in=v_w_in, b_f=v_b_f, conv_w=v_conv_w, conv_b=v_conv_b,
              w_gate_a=v_w_gate_a, b_gate_a=v_b_gate_a, w_gate_x=v_w_gate_x, b_gate_x=v_b_gate_x, lru_L=v_lru_L,
              attn_out_g=v_attn_out_g, rec_out_g=v_rec_out_g, w_out=v_w_out, mlp_norm_g=v_mlp_norm_g,
              w_up=v_w_up, w_down=v_w_down, final_g=v_final_g)
    depth = w_in.shape[0]
    nh = b_f.shape[1]
    rw = conv_b.shape[1]
    me = 4 * lax.axis_index("x") + 2 * lax.axis_index("y") + lax.axis_index("c")

    w['w_in_t'] = jnp.transpose(w_in, (2, 0, 1))
    swap = lambda a: jnp.swapaxes(a, 1, 2)
    step = _Step(w, nh, rw, me)
    g0 = step.group(0, meta)
    meta_full = g0[2].transpose(1, 0, 2).reshape(N_META, -1)
    conv_full = g0[3].transpose(1, 2, 0, 3).reshape(depth, CONV_WIDTH, rw)
    small = {n: w[n] for n in SMALL}
    small['conv_w'] = conv_full

    loss_part, dh0, tok = local_step(x[0], loss_target[0], meta_full, small, step)
    loss = lax.psum(loss_part, ("x", "y", "c"))
    grad_x = dh0[N_META:N_META + x.shape[1]][None]

    gsum = step.small_sum(step.started[-1][4][0])
    gsum['meta'] = lax.dynamic_slice_in_dim(gsum['meta'], me * meta.shape[1], meta.shape[1], axis=1)
    gsum['conv_w'] = lax.dynamic_slice_in_dim(gsum['conv_w'], me * conv_w.shape[2], conv_w.shape[2], axis=2)
    packed = [_pack(t) for t in (gsum, {n: w[n] for n in SMALL}, {n: mo[n] for n in SMALL},
                                 {n: vo[n] for n in SMALL})]
    res = sum_adamw(packed[0][None], packed[1], packed[2], packed[3], packed[1].shape[0], "adamw_small")
    out_g, out_d, out_m, out_v = {}, {}, {}, {}
    shapes = {n: w[n].shape for n in SMALL}
    for dst, vec in zip((out_g, out_d, out_m, out_v), res):
        dst.update(_unpack(vec, shapes))

    after = res[0]
    for group in ('mlp', 'in'):
        for n, r in step.received(group, after).items():
            if n == 'w_in':
                out = sum_adamw_t(r, swap(w[n]), swap(mo[n]), swap(vo[n]), "adamw_w_in")
                out_g[n], out_d[n], out_m[n], out_v[n] = [swap(a) for a in out]
                continue
            shp = w[n].shape
            rows, cols = shp[0] * shp[1], shp[2]
            tr = min(512 if cols <= 512 else 256, rows)
            out = sum_adamw(r.reshape(N_DEV, rows, cols), w[n].reshape(rows, cols), mo[n].reshape(rows, cols),
                            vo[n].reshape(rows, cols), tr, "adamw_" + n)
            out_g[n], out_d[n], out_m[n], out_v[n] = [a.reshape(shp) for a in out]
            after = out[0]

    return (loss, grad_x, *[out_g[n] for n in WEIGHTS], *[out_d[n] for n in WEIGHTS],
            *[out_m[n] for n in WEIGHTS], *[out_v[n] for n in WEIGHTS])
```

```python
import functools
import math

import jax
import jax.numpy as jnp
from jax import lax
from jax.experimental import pallas as pl
from jax.experimental.pallas import tpu as pltpu

F32 = jnp.float32
BF16 = jnp.bfloat16

N_DEV = 8
N_META = 16
HEAD_DIM = 64
CONV_WIDTH = 4
RG_C = 8.0
NORM_EPS = 1e-6
LANES = 128
SUBLANES = 8
ATT_BLOCK = 128
ATT_TQ = 512
NEG_BIG = -1e30
ATT_SCALE = 1.0 / math.sqrt(HEAD_DIM)

ADAM_LR = 0.001
ADAM_B1 = 0.9
ADAM_B2 = 0.999
ADAM_EPS = 1e-08
ADAM_WD = 0.01
ADAM_STEP = 10

VMEM_LIMIT_BYTES = 56 * 1024 * 1024
MESH = pl.DeviceIdType.MESH
ANY = pl.BlockSpec(memory_space=pl.ANY)


def _cparams(*sem):
    return pltpu.CompilerParams(dimension_semantics=sem if sem else None,
                                vmem_limit_bytes=VMEM_LIMIT_BYTES)


def _dot(a, b):
    return jnp.dot(a, b, preferred_element_type=F32)


def _dot_nt(a, b):
    return lax.dot_general(a, b, (((1,), (1,)), ((), ())), preferred_element_type=F32)


def _dot_tn(a, b):
    return lax.dot_general(a, b, (((0,), (0,)), ((), ())), preferred_element_type=F32)


def _sigmoid(x):
    return 0.5 * (1.0 + jnp.tanh(0.5 * x))


def _log_sigmoid(x):
    return jnp.minimum(x, 0.0) - jnp.log(1.0 + jnp.exp(-jnp.abs(x)))


def _expm1(x):
    series = x * (1.0 + x * (0.5 + x * (1.0 / 6.0 + x * (1.0 / 24.0))))
    return jnp.where(jnp.abs(x) < 1e-2, series, jnp.exp(x) - 1.0)


_GELU_K = math.sqrt(2.0 / math.pi)
_GELU_C = 0.044715


def _gelu(x):
    t = jnp.tanh(_GELU_K * (x + _GELU_C * x * x * x))
    return 0.5 * x * (1.0 + t)


def _gelu_grad(x):
    t = jnp.tanh(_GELU_K * (x + _GELU_C * x * x * x))
    return 0.5 * (1.0 + t) + 0.5 * x * (1.0 - t * t) * _GELU_K * (1.0 + 3.0 * _GELU_C * x * x)


def _split3_dot(tri, x):
    hi = x.astype(BF16)
    r1 = x - hi.astype(F32)
    mid = r1.astype(BF16)
    lo = (r1 - mid.astype(F32)).astype(BF16)
    return _dot(tri, hi) + _dot(tri, mid) + _dot(tri, lo)


def _dot_split3(x, sel):
    hi = x.astype(BF16)
    r1 = x - hi.astype(F32)
    mid = r1.astype(BF16)
    lo = (r1 - mid.astype(F32)).astype(BF16)
    return _dot(hi, sel) + _dot(mid, sel) + _dot(lo, sel)


def _rms_fwd(x, g):
    r = lax.rsqrt(jnp.mean(x * x, axis=-1, keepdims=True) + NORM_EPS)
    return x * r * g


def _rms_bwd(x, g, dy):
    r = lax.rsqrt(jnp.mean(x * x, axis=-1, keepdims=True) + NORM_EPS)
    xn = x * r
    dxn = dy * g
    dx = r * (dxn - xn * jnp.mean(dxn * xn, axis=-1, keepdims=True))
    return dx, jnp.sum(dy * xn, axis=0, keepdims=True)


def _accumulate(ref, val, first):
    @pl.when(first)
    def _():
        ref[...] = val

    @pl.when(jnp.logical_not(first))
    def _():
        ref[...] += val


def in_proj(h, g1, w_in_t, wrest_t, nq, tm):
    tp, d = h.shape
    nr = wrest_t.shape[0]

    def body(h_ref, g_ref, wq_ref, wr_ref, z_ref, qkv_ref, rest_ref):
        z = _rms_fwd(h_ref[...], g_ref[...]).astype(BF16)
        z_ref[...] = z
        qkv_ref[...] = _dot_nt(z, wq_ref[...]).astype(BF16)
        rest_ref[...] = _dot_nt(z, wr_ref[...])

    return pl.pallas_call(
        body, name="in_proj", grid=(tp // tm,),
        in_specs=[pl.BlockSpec((tm, d), lambda i: (i, 0)),
                  pl.BlockSpec((1, d), lambda i: (0, 0)),
                  pl.BlockSpec((nq, d), lambda i: (0, 0)),
                  pl.BlockSpec((nr, d), lambda i: (0, 0))],
        out_specs=[pl.BlockSpec((tm, d), lambda i: (i, 0)),
                   pl.BlockSpec((tm, nq), lambda i: (i, 0)),
                   pl.BlockSpec((tm, nr), lambda i: (i, 0))],
        out_shape=[jax.ShapeDtypeStruct((tp, d), BF16),
                   jax.ShapeDtypeStruct((tp, nq), BF16),
                   jax.ShapeDtypeStruct((tp, nr), F32)],
        compiler_params=_cparams("parallel"),
    )(h, g1, w_in_t, wrest_t)


def fgate_fwd(rest, bf_pad, fcol):
    tp = rest.shape[0]
    nb = tp // ATT_BLOCK

    def body(f_ref, b_ref, c_ref, ct_ref):
        r_i = lax.broadcasted_iota(jnp.int32, (ATT_BLOCK, ATT_BLOCK), 0)
        c_i = lax.broadcasted_iota(jnp.int32, (ATT_BLOCK, ATT_BLOCK), 1)
        tri = (r_i >= c_i).astype(BF16)
        carry = jnp.zeros((1, LANES), F32)
        for i in range(nb):
            sl = slice(i * ATT_BLOCK, (i + 1) * ATT_BLOCK)
            lf = _log_sigmoid(f_ref[sl, :] + b_ref[...])
            cs = _split3_dot(tri, lf) + carry
            carry = cs[ATT_BLOCK - 1:ATT_BLOCK, :]
            c_ref[sl, :] = cs
            ct_ref[:, sl] = cs.T[0:SUBLANES, :]

    return pl.pallas_call(
        body, name="fgate_fwd", grid=(1,),
        in_specs=[pl.BlockSpec((tp, LANES), lambda i: (0, fcol)),
                  pl.BlockSpec((1, LANES), lambda i: (0, 0))],
        out_specs=[pl.BlockSpec((tp, LANES), lambda i: (0, 0)),
                   pl.BlockSpec((SUBLANES, tp), lambda i: (0, 0))],
        out_shape=[jax.ShapeDtypeStruct((tp, LANES), F32),
                   jax.ShapeDtypeStruct((SUBLANES, tp), F32)],
        compiler_params=_cparams("arbitrary"),
    )(rest, bf_pad)


def _pick_col(blk, head):
    lane = lax.broadcasted_iota(jnp.int32, blk.shape, 1)
    return jnp.sum(jnp.where(lane == head, blk, 0.0), axis=1, keepdims=True)


def _pick_row(blk, head):
    sub = lax.broadcasted_iota(jnp.int32, blk.shape, 0)
    return jnp.sum(jnp.where(sub == head, blk, 0.0), axis=0, keepdims=True)


def _att_tiles(tp):
    out, r0 = [], 0
    while r0 < tp:
        rows = min(ATT_TQ, tp - r0)
        out.append((r0, rows, r0 + rows))
        r0 += rows
    return out


def attn_fwd(qkv, c, ct, nh):
    tp = qkv.shape[0]
    npair = nh // 2
    tiles = _att_tiles(tp)

    def body(q_ref, k_ref, v_ref, c_ref, ct_ref, o_ref, lset_ref):
        p = pl.program_id(0)
        lset_ref[...] = jnp.zeros_like(lset_ref)
        for r0, nr, nk in tiles:
            rs = slice(r0, r0 + nr)
            causal = (r0 + lax.broadcasted_iota(jnp.int32, (nr, nk), 0)
                      >= lax.broadcasted_iota(jnp.int32, (nr, nk), 1))
            cblk = c_ref[rs, :]
            ctb = ct_ref[:, 0:nk]
            for hh in range(2):
                head = 2 * p + hh
                hs = slice(hh * HEAD_DIM, (hh + 1) * HEAD_DIM)
                q = q_ref[rs, hs] * ATT_SCALE
                s = _dot_nt(q, k_ref[0:nk, hs]) + (_pick_col(cblk, head) - _pick_row(ctb, head))
                s = jnp.where(causal, s, NEG_BIG)
                m = jnp.max(s, axis=1, keepdims=True)
                pm = jnp.exp(s - m)
                l = jnp.sum(pm, axis=1, keepdims=True)
                o_ref[rs, hs] = _dot(pm.astype(BF16), v_ref[0:nk, hs]) / l
                lse = m + jnp.log(l)
                lset_ref[hh:hh + 1, rs] = jnp.broadcast_to(lse, (nr, LANES)).T[0:1, :]

    pair = lambda p: (0, p)
    return pl.pallas_call(
        body, name="attn_fwd", grid=(npair,),
        in_specs=[pl.BlockSpec((tp, LANES), pair),
                  pl.BlockSpec((tp, LANES), lambda p: (0, npair + p)),
                  pl.BlockSpec((tp, LANES), lambda p: (0, 2 * npair + p)),
                  pl.BlockSpec((tp, LANES), lambda p: (0, 0)),
                  pl.BlockSpec((SUBLANES, tp), lambda p: (0, 0))],
        out_specs=[pl.BlockSpec((tp, LANES), pair),
                   pl.BlockSpec((None, SUBLANES, tp), lambda p: (p, 0, 0))],
        out_shape=[jax.ShapeDtypeStruct((tp, nh * HEAD_DIM), F32),
                   jax.ShapeDtypeStruct((npair, SUBLANES, tp), F32)],
        compiler_params=_cparams("parallel"),
    )(qkv, qkv, qkv, c, ct)


def _shift_down(x, k, n):
    if k == 0:
        return x
    rows = lax.broadcasted_iota(jnp.int32, x.shape, 0)
    return jnp.where(rows >= k, pltpu.roll(x, k, 0), 0.0)


def _shift_up(x, k, n):
    if k == 0:
        return x
    rows = lax.broadcasted_iota(jnp.int32, x.shape, 0)
    return jnp.where(rows < n - k, pltpu.roll(x, n - k, 0), 0.0)


def _conv_fwd(xr, cw_ref, cb_ref, n):
    xc = cw_ref[CONV_WIDTH - 1:CONV_WIDTH, :] * xr + cb_ref[...]
    for k in range(1, CONV_WIDTH):
        xc = xc + cw_ref[CONV_WIDTH - 1 - k:CONV_WIDTH - k, :] * _shift_down(xr, k, n)
    return xc


def _gates(xc, wga_ref, bga_ref, wgx_ref, bgx_ref, l_ref):
    xcb = xc.astype(BF16)
    r = _sigmoid(_dot(xcb, wga_ref[...]) + bga_ref[...])
    ig = _sigmoid(_dot(xcb, wgx_ref[...]) + bgx_ref[...])
    ls = _log_sigmoid(l_ref[...])
    log_a = RG_C * r * ls
    a = jnp.exp(log_a)
    mult = jnp.sqrt(-_expm1(2.0 * log_a))
    return xcb, r, ig, ls, log_a, a, mult


def _scan_rows(a_s, u_s, out_ref, n, reverse):
    nt = n // SUBLANES
    row = lax.broadcasted_iota(jnp.int32, (SUBLANES, LANES), 0)

    def step(t, carry):
        tt = (nt - 1 - t) if reverse else t
        off = pl.multiple_of(tt * SUBLANES, SUBLANES)
        a = a_s[pl.ds(off, SUBLANES), :]
        u = u_s[pl.ds(off, SUBLANES), :]
        for d in (1, 2, 4):
            if reverse:
                keep = row < SUBLANES - d
                sh = SUBLANES - d
            else:
                keep = row >= d
                sh = d
            a_sh = jnp.where(keep, pltpu.roll(a, sh, 0), 1.0)
            u_sh = jnp.where(keep, pltpu.roll(u, sh, 0), 0.0)
            u = a * u_sh + u
            a = a * a_sh
        h = u + a * carry
        out_ref[pl.ds(off, SUBLANES), :] = h
        return h[0:1, :] if reverse else h[SUBLANES - 1:SUBLANES, :]

    lax.fori_loop(0, nt, step, jnp.zeros((1, LANES), F32), unroll=4 if nt % 4 == 0 else 1)


def rec_fwd(rest, convw, convb, wga, bga, wgx, bgx, lru, rw):
    tp = rest.shape[0]
    ng = rw // LANES

    def body(xr_ref, yr_ref, cw_ref, cb_ref, wga_ref, bga_ref, wgx_ref, bgx_ref, l_ref,
             rec_ref, hr_ref, xc_ref, a_s, u_s):
        xc = _conv_fwd(xr_ref[...], cw_ref, cb_ref, tp)
        xc_ref[...] = xc
        _, r, ig, ls, log_a, a, mult = _gates(xc, wga_ref, bga_ref, wgx_ref, bgx_ref, l_ref)
        a_s[...] = a
        u_s[...] = mult * ig * xc
        _scan_rows(a_s, u_s, hr_ref, tp, reverse=False)
        rec_ref[...] = hr_ref[...] * _gelu(yr_ref[...])

    col = lambda g: (0, g)
    vec = pl.BlockSpec((1, LANES), col)
    big = pl.BlockSpec((tp, LANES), col)
    return pl.pallas_call(
        body, name="rec_fwd", grid=(ng,),
        in_specs=[big, pl.BlockSpec((tp, LANES), lambda g: (0, ng + g)),
                  pl.BlockSpec((CONV_WIDTH, LANES), col), vec,
                  pl.BlockSpec((None, LANES, LANES), lambda g: (g, 0, 0)), vec,
                  pl.BlockSpec((None, LANES, LANES), lambda g: (g, 0, 0)), vec, vec],
        out_specs=[big, big, big],
        out_shape=[jax.ShapeDtypeStruct((tp, rw), F32)] * 3,
        scratch_shapes=[pltpu.VMEM((tp, LANES), F32), pltpu.VMEM((tp, LANES), F32)],
        compiler_params=_cparams("parallel"),
    )(rest, rest, convw, convb, wga, bga, wgx, bgx, lru)


def out_proj(h, o, rec, ga, gr, wout, g2, tm):
    tp, d = h.shape
    aw, rw = o.shape[1], rec.shape[1]

    def body(h_ref, o_ref, rec_ref, ga_ref, gr_ref, w_ref, g2_ref, h2_ref, mix_ref, z2_ref):
        mix_ref[:, 0:aw] = _rms_fwd(o_ref[...], ga_ref[...]).astype(BF16)
        mix_ref[:, aw:aw + rw] = _rms_fwd(rec_ref[...], gr_ref[...]).astype(BF16)
        h2 = h_ref[...] + _dot(mix_ref[...], w_ref[...])
        h2_ref[...] = h2
        z2_ref[...] = _rms_fwd(h2, g2_ref[...]).astype(BF16)

    row = lambda i: (i, 0)
    fix = lambda i: (0, 0)
    return pl.pallas_call(
        body, name="out_proj", grid=(tp // tm,),
        in_specs=[pl.BlockSpec((tm, d), row), pl.BlockSpec((tm, aw), row), pl.BlockSpec((tm, rw), row),
                  pl.BlockSpec((1, aw), fix), pl.BlockSpec((1, rw), fix),
                  pl.BlockSpec((d, d), fix), pl.BlockSpec((1, d), fix)],
        out_specs=[pl.BlockSpec((tm, d), row)] * 3,
        out_shape=[jax.ShapeDtypeStruct((tp, d), F32), jax.ShapeDtypeStruct((tp, d), BF16),
                   jax.ShapeDtypeStruct((tp, d), BF16)],
        compiler_params=_cparams("parallel"),
    )(h, o, rec, ga, gr, wout, g2)


def mlp_fwd(z2, h2, gup, gdown, tm):
    tp, d = h2.shape
    nf = gup.shape[0]
    tf = gup.shape[2]

    def body(z_ref, h_ref, wu_ref, wd_ref, u_ref, h3_ref, acc):
        j = pl.program_id(1)
        u = jnp.maximum(_dot(z_ref[...], wu_ref[...]), 0.0)
        u_ref[...] = u.astype(BF16)
        part = _dot((u * u).astype(BF16), wd_ref[...])

        @pl.when(j == 0)
        def _():
            acc[...] = h_ref[...] + part

        @pl.when(j > 0)
        def _():
            acc[...] += part

        @pl.when(j == nf - 1)
        def _():
            h3_ref[...] = acc[...]

    return pl.pallas_call(
        body, name="mlp_fwd", grid=(tp // tm, nf),
        in_specs=[pl.BlockSpec((tm, d), lambda i, j: (i, 0)),
                  pl.BlockSpec((tm, d), lambda i, j: (i, 0)),
                  pl.BlockSpec((None, d, tf), lambda i, j: (j, 0, 0)),
                  pl.BlockSpec((None, tf, d), lambda i, j: (j, 0, 0))],
        out_specs=[pl.BlockSpec((tm, tf), lambda i, j: (i, j)),
                   pl.BlockSpec((tm, d), lambda i, j: (i, 0))],
        out_shape=[jax.ShapeDtypeStruct((tp, nf * tf), BF16), jax.ShapeDtypeStruct((tp, d), F32)],
        scratch_shapes=[pltpu.VMEM((tm, d), F32)],
        compiler_params=_cparams("parallel", "arbitrary"),
    )(z2, h2, gup, gdown)


def loss_head(h, gf, tgt, t_real, tm):
    tp, d = h.shape

    def body(h_ref, g_ref, t_ref, dh_ref, dg_ref, loss_ref):
        i = pl.program_id(0)
        x = h_ref[...]
        g = g_ref[...]
        r = lax.rsqrt(jnp.mean(x * x, axis=-1, keepdims=True) + NORM_EPS)
        xn = x * r
        rows = i * tm + lax.broadcasted_iota(jnp.int32, (tm, 1), 0)
        valid = jnp.logical_and(rows >= N_META, rows < t_real)
        e = jnp.where(valid, xn * g - t_ref[...], 0.0)
        part = 0.5 * jnp.sum(jnp.sum(e * e, axis=1, keepdims=True) / d, axis=0, keepdims=True)
        dy = e / d
        dxn = dy * g
        dh_ref[...] = r * (dxn - xn * jnp.mean(dxn * xn, axis=-1, keepdims=True))
        _accumulate(dg_ref, jnp.sum(dy * xn, axis=0, keepdims=True), i == 0)
        _accumulate(loss_ref, jnp.broadcast_to(part, (1, LANES)), i == 0)

    row = lambda i: (i, 0)
    fix = lambda i: (0, 0)
    return pl.pallas_call(
        body, name="loss_head", grid=(tp // tm,),
        in_specs=[pl.BlockSpec((tm, d), row), pl.BlockSpec((1, d), fix), pl.BlockSpec((tm, d), row)],
        out_specs=[pl.BlockSpec((tm, d), row), pl.BlockSpec((1, d), fix), pl.BlockSpec((1, LANES), fix)],
        out_shape=[jax.ShapeDtypeStruct((tp, d), F32), jax.ShapeDtypeStruct((1, d), F32),
                   jax.ShapeDtypeStruct((1, LANES), F32)],
        compiler_params=_cparams("arbitrary"),
    )(h, gf, tgt)


def mlp_bwd(dh, u, h2, g2, gup, gdown, tm):
    tp, d = dh.shape
    nf = gup.shape[0]
    tf = gup.shape[2]
    ni = tp // tm

    def body(dh_ref, u_ref, h2_ref, g_ref, wu_ref, wd_ref, dup_ref, dh2_ref, dg_ref, dhb, acc):
        i = pl.program_id(0)
        j = pl.program_id(1)

        @pl.when(j == 0)
        def _():
            dhb[...] = dh_ref[...].astype(BF16)

        dup = (_dot_nt(dhb[...], wd_ref[...]) * (2.0 * u_ref[...].astype(F32))).astype(BF16)
        dup_ref[...] = dup
        _accumulate(acc, _dot_nt(dup, wu_ref[...]), j == 0)

        @pl.when(j == nf - 1)
        def _():
            dx, dg = _rms_bwd(h2_ref[...], g_ref[...], acc[...])
            dh2_ref[...] = dh_ref[...] + dx
            _accumulate(dg_ref, dg, i == 0)

    return pl.pallas_call(
        body, name="mlp_bwd", grid=(ni, nf),
        in_specs=[pl.BlockSpec((tm, d), lambda i, j: (i, 0)),
                  pl.BlockSpec((tm, tf), lambda i, j: (i, j)),
                  pl.BlockSpec((tm, d), lambda i, j: (i, 0)),
                  pl.BlockSpec((1, d), lambda i, j: (0, 0)),
                  pl.BlockSpec((None, d, tf), lambda i, j: (j, 0, 0)),
                  pl.BlockSpec((None, tf, d), lambda i, j: (j, 0, 0))],
        out_specs=[pl.BlockSpec((tm, tf), lambda i, j: (i, j)),
                   pl.BlockSpec((tm, d), lambda i, j: (i, 0)),
                   pl.BlockSpec((1, d), lambda i, j: (0, 0)),
                   pl.BlockSpec((tm, d), lambda i, j: (i, 0))],
        out_shape=[jax.ShapeDtypeStruct((tp, nf * tf), BF16), jax.ShapeDtypeStruct((tp, d), F32),
                   jax.ShapeDtypeStruct((1, d), F32), jax.ShapeDtypeStruct((tp, d), BF16)],
        scratch_shapes=[pltpu.VMEM((tm, d), F32)],
        compiler_params=_cparams("arbitrary", "arbitrary"),
    )(dh, u, h2, g2, gup, gdown)


def mm_tn(a, b, *, tk, tn, out_dtype, name, square_a=False, blocked_n=False):
    rows, kk = a.shape
    nn = b.shape[1]

    def body(a_ref, b_ref, o_ref):
        av = a_ref[...]
        if square_a:
            af = av.astype(F32)
            av = (af * af).astype(BF16)
        o_ref[...] = _dot_tn(av.astype(BF16), b_ref[...].astype(BF16)).astype(out_dtype)

    if blocked_n:
        out_spec = pl.BlockSpec((None, tk, tn), lambda k, n: (n, k, 0))
        out_shape = jax.ShapeDtypeStruct((nn // tn, kk, tn), out_dtype)
    else:
        out_spec = pl.BlockSpec((tk, tn), lambda k, n: (k, n))
        out_shape = jax.ShapeDtypeStruct((kk, nn), out_dtype)
    return pl.pallas_call(
        body, name=name, grid=(kk // tk, nn // tn),
        in_specs=[pl.BlockSpec((rows, tk), lambda k, n: (0, k)),
                  pl.BlockSpec((rows, tn), lambda k, n: (0, n))],
        out_specs=out_spec, out_shape=out_shape,
        compiler_params=_cparams("parallel", "parallel"),
    )(a, b)


def out_proj_bwd(dh2, o, rec, ga, gr, wout, tm):
    tp, d = dh2.shape
    aw, rw = o.shape[1], rec.shape[1]

    def body(dh_ref, o_ref, rec_ref, ga_ref, gr_ref, w_ref, do_ref, drec_ref, dga_ref, dgr_ref):
        i = pl.program_id(0)
        dmix = _dot_nt(dh_ref[...].astype(BF16), w_ref[...])
        do, dga = _rms_bwd(o_ref[...], ga_ref[...], dmix[:, 0:aw])
        drec, dgr = _rms_bwd(rec_ref[...], gr_ref[...], dmix[:, aw:aw + rw])
        do_ref[...] = do
        drec_ref[...] = drec
        _accumulate(dga_ref, dga, i == 0)
        _accumulate(dgr_ref, dgr, i == 0)

    row = lambda i: (i, 0)
    fix = lambda i: (0, 0)
    return pl.pallas_call(
        body, name="out_proj_bwd", grid=(tp // tm,),
        in_specs=[pl.BlockSpec((tm, d), row), pl.BlockSpec((tm, aw), row), pl.BlockSpec((tm, rw), row),
                  pl.BlockSpec((1, aw), fix), pl.BlockSpec((1, rw), fix), pl.BlockSpec((d, d), fix)],
        out_specs=[pl.BlockSpec((tm, aw), row), pl.BlockSpec((tm, rw), row),
                   pl.BlockSpec((1, aw), fix), pl.BlockSpec((1, rw), fix)],
        out_shape=[jax.ShapeDtypeStruct((tp, aw), F32), jax.ShapeDtypeStruct((tp, rw), F32),
                   jax.ShapeDtypeStruct((1, aw), F32), jax.ShapeDtypeStruct((1, rw), F32)],
        compiler_params=_cparams("arbitrary"),
    )(dh2, o, rec, ga, gr, wout)


def rec_bwd(drec, hr, xc, rest, convw, convb, wga, bga, wgx, bgx, lru, rw):
    tp = rest.shape[0]
    ng = rw // LANES

    def body(drec_ref, hr_ref, xc_ref, xr_ref, yr_ref, cw_ref, cb_ref, wga_ref, bga_ref, wgx_ref, bgx_ref, l_ref,
             dxr_ref, dyr_ref, dwga_ref, dwgx_ref, vec_ref, a_s, u_s, lam_s):
        xc = xc_ref[...]
        h = hr_ref[...]
        yr = yr_ref[...]
        drec = drec_ref[...]
        xcb, r, ig, ls, log_a, a, mult = _gates(xc, wga_ref, bga_ref, wgx_ref, bgx_ref, l_ref)
        dyr_ref[...] = (drec * h * _gelu_grad(yr)).astype(BF16)
        a_s[...] = _shift_up(a, 1, tp)
        u_s[...] = drec * _gelu(yr)
        _scan_rows(a_s, u_s, lam_s, tp, reverse=True)
        lam = lam_s[...]
        da = lam * _shift_down(h, 1, tp)
        dmult = lam * ig * xc
        dig = lam * mult * xc
        dxc = lam * mult * ig
        a2 = jnp.exp(2.0 * log_a)
        dlog_a = da * a - dmult * a2 / mult
        dr = dlog_a * (RG_C * ls)
        dl = jnp.sum(dlog_a * (RG_C * r), axis=0, keepdims=True) * _sigmoid(-l_ref[...])
        dpa = dr * r * (1.0 - r)
        dpx = dig * ig * (1.0 - ig)
        dpab = dpa.astype(BF16)
        dpxb = dpx.astype(BF16)
        dxc = dxc + _dot_nt(dpab, wga_ref[...]) + _dot_nt(dpxb, wgx_ref[...])
        dwga_ref[...] = _dot_tn(xcb, dpab)
        dwgx_ref[...] = _dot_tn(xcb, dpxb)
        xr = xr_ref[...]
        dxr = cw_ref[CONV_WIDTH - 1:CONV_WIDTH, :] * dxc
        for k in range(1, CONV_WIDTH):
            dxr = dxr + cw_ref[CONV_WIDTH - 1 - k:CONV_WIDTH - k, :] * _shift_up(dxc, k, tp)
        dxr_ref[...] = dxr.astype(BF16)
        for k in range(CONV_WIDTH):
            vec_ref[k:k + 1, :] = jnp.sum(dxc * _shift_down(xr, CONV_WIDTH - 1 - k, tp), axis=0, keepdims=True)
        vec_ref[4:5, :] = jnp.sum(dxc, axis=0, keepdims=True)
        vec_ref[5:6, :] = jnp.sum(dpa, axis=0, keepdims=True)
        vec_ref[6:7, :] = jnp.sum(dpx, axis=0, keepdims=True)
        vec_ref[7:8, :] = dl

    col = lambda g: (0, g)
    vec = pl.BlockSpec((1, LANES), col)
    big = pl.BlockSpec((tp, LANES), col)
    sq = pl.BlockSpec((None, LANES, LANES), lambda g: (g, 0, 0))
    return pl.pallas_call(
        body, name="rec_bwd", grid=(ng,),
        in_specs=[big, big, big, big, pl.BlockSpec((tp, LANES), lambda g: (0, ng + g)),
                  pl.BlockSpec((CONV_WIDTH, LANES), col), vec, sq, vec, sq, vec, vec],
        out_specs=[big, big, sq, sq, pl.BlockSpec((None, SUBLANES, LANES), lambda g: (g, 0, 0))],
        out_shape=[jax.ShapeDtypeStruct((tp, rw), BF16), jax.ShapeDtypeStruct((tp, rw), BF16),
                   jax.ShapeDtypeStruct((ng, LANES, LANES), F32), jax.ShapeDtypeStruct((ng, LANES, LANES), F32),
                   jax.ShapeDtypeStruct((ng, SUBLANES, LANES), F32)],
        scratch_shapes=[pltpu.VMEM((tp, LANES), F32)] * 3,
        compiler_params=_cparams("parallel"),
    )(drec, hr, xc, rest, rest, convw, convb, wga, bga, wgx, bgx, lru)


def attn_bwd(qkv, do, o, lset, c, ct, nh):
    tp = qkv.shape[0]
    npair = nh // 2
    aw = nh * HEAD_DIM
    tiles = _att_tiles(tp)

    def body(q_ref, k_ref, v_ref, do_ref, o_ref, lset_ref, c_ref, ct_ref,
             dq_ref, dk_ref, dv_ref, drow_ref, dcol_ref, dk_acc, dv_acc):
        p = pl.program_id(0)
        dk_acc[...] = jnp.zeros_like(dk_acc)
        dv_acc[...] = jnp.zeros_like(dv_acc)
        dcol_ref[...] = jnp.zeros_like(dcol_ref)
        drow_ref[...] = jnp.zeros_like(drow_ref)
        for r0, nr, nk in tiles:
            rs = slice(r0, r0 + nr)
            causal = (r0 + lax.broadcasted_iota(jnp.int32, (nk, nr), 1)
                      >= lax.broadcasted_iota(jnp.int32, (nk, nr), 0))
            cblk = c_ref[0:nk, :]
            ctb = ct_ref[:, rs]
            for hh in range(2):
                head = 2 * p + hh
                hs = slice(hh * HEAD_DIM, (hh + 1) * HEAD_DIM)
                q = q_ref[rs, hs]
                k = k_ref[0:nk, hs]
                dof = do_ref[rs, hs]
                do16 = dof.astype(BF16)
                delta = jnp.sum(dof * o_ref[rs, hs], axis=1, keepdims=True)
                delta_row = jnp.broadcast_to(delta, (nr, LANES)).T[0:1, :]
                s_t = _dot_nt(k, q * ATT_SCALE) + (_pick_row(ctb, head) - _pick_col(cblk, head))
                p_t = jnp.where(causal, jnp.exp(s_t - lset_ref[hh:hh + 1, rs]), 0.0)
                ds_t = p_t * (_dot_nt(v_ref[0:nk, hs], do16) - delta_row)
                p16 = p_t.astype(BF16)
                ds16 = ds_t.astype(BF16)
                dv_acc[0:nk, hs] += _dot(p16, do16)
                dk_acc[0:nk, hs] += _dot(ds16, q) * ATT_SCALE
                dq_ref[rs, hs] = (_dot_tn(ds16, k) * ATT_SCALE).astype(BF16)
                drow_ref[hh:hh + 1, rs] = jnp.sum(ds_t, axis=0, keepdims=True)
                dcol_ref[0:nk, hs] -= jnp.broadcast_to(jnp.sum(ds_t, axis=1, keepdims=True), (nk, HEAD_DIM))
        dk_ref[...] = dk_acc[...].astype(BF16)
        dv_ref[...] = dv_acc[...].astype(BF16)

    pair = lambda p: (0, p)
    return pl.pallas_call(
        body, name="attn_bwd", grid=(npair,),
        in_specs=[pl.BlockSpec((tp, LANES), pair),
                  pl.BlockSpec((tp, LANES), lambda p: (0, npair + p)),
                  pl.BlockSpec((tp, LANES), lambda p: (0, 2 * npair + p)),
                  pl.BlockSpec((tp, LANES), pair),
                  pl.BlockSpec((tp, LANES), pair),
                  pl.BlockSpec((None, SUBLANES, tp), lambda p: (p, 0, 0)),
                  pl.BlockSpec((tp, LANES), lambda p: (0, 0)),
                  pl.BlockSpec((SUBLANES, tp), lambda p: (0, 0))],
        out_specs=[pl.BlockSpec((tp, LANES), pair), pl.BlockSpec((tp, LANES), pair),
                   pl.BlockSpec((tp, LANES), pair),
                   pl.BlockSpec((None, SUBLANES, tp), lambda p: (p, 0, 0)),
                   pl.BlockSpec((tp, LANES), pair)],
        out_shape=[jax.ShapeDtypeStruct((tp, aw), BF16), jax.ShapeDtypeStruct((tp, aw), BF16),
                   jax.ShapeDtypeStruct((tp, aw), BF16),
                   jax.ShapeDtypeStruct((npair, SUBLANES, tp), F32),
                   jax.ShapeDtypeStruct((tp, aw), F32)],
        scratch_shapes=[pltpu.VMEM((tp, LANES), F32), pltpu.VMEM((tp, LANES), F32)],
        compiler_params=_cparams("parallel"),
    )(qkv, qkv, qkv, do, o, lset, c, ct)


def fgate_bwd(dct8, drs, rest, bf_pad, fcol):
    tp = rest.shape[0]
    aw = drs.shape[1]
    nb = tp // ATT_BLOCK
    B = ATT_BLOCK

    def body(d_ref, drs_ref, f_ref, b_ref, dfl_ref, db_ref, pad_s):
        r_i = lax.broadcasted_iota(jnp.int32, (B, B), 0)
        c_i = lax.broadcasted_iota(jnp.int32, (B, B), 1)
        triu = (c_i >= r_i).astype(BF16)
        sel = (lax.broadcasted_iota(jnp.int32, (aw, LANES), 0)
               == HEAD_DIM * lax.broadcasted_iota(jnp.int32, (aw, LANES), 1)).astype(BF16)
        carry = jnp.zeros((1, LANES), F32)
        db = jnp.zeros((1, LANES), F32)
        pad_s[...] = jnp.zeros_like(pad_s)
        for i in range(nb - 1, -1, -1):
            sl = slice(i * B, (i + 1) * B)
            pad_s[0:SUBLANES, :] = d_ref[:, sl]
            dc = pad_s[...].T + _dot_split3(drs_ref[sl, :], sel)
            rc = _split3_dot(triu, dc)
            dlf = rc + carry
            carry = carry + rc[0:1, :]
            dfl = dlf * _sigmoid(-(f_ref[sl, :] + b_ref[...]))
            dfl_ref[sl, :] = dfl.astype(BF16)
            db = db + jnp.sum(dfl, axis=0, keepdims=True)
        db_ref[...] = db

    return pl.pallas_call(
        body, name="fgate_bwd", grid=(1,),
        in_specs=[pl.BlockSpec((SUBLANES, tp), lambda i: (0, 0)),
                  pl.BlockSpec((tp, aw), lambda i: (0, 0)),
                  pl.BlockSpec((tp, LANES), lambda i: (0, fcol)),
                  pl.BlockSpec((1, LANES), lambda i: (0, 0))],
        out_specs=[pl.BlockSpec((tp, LANES), lambda i: (0, 0)),
                   pl.BlockSpec((1, LANES), lambda i: (0, 0))],
        out_shape=[jax.ShapeDtypeStruct((tp, LANES), BF16), jax.ShapeDtypeStruct((1, LANES), F32)],
        scratch_shapes=[pltpu.VMEM((B, B), F32)],
        compiler_params=_cparams("arbitrary"),
    )(dct8, drs, rest, bf_pad)


def in_proj_bwd(dh2, parts, w_in_t, wrest_t, h, g1, tm):
    tp, d = h.shape
    dq, dk, dv, dxr, dyr, dfl = parts
    aw, rw = dq.shape[1], dxr.shape[1]

    def body(dh2_ref, dq_ref, dk_ref, dv_ref, dxr_ref, dyr_ref, dfl_ref, wq_ref, wr_ref, h_ref, g_ref,
             dh_ref, dg_ref):
        i = pl.program_id(0)
        dz = _dot(dq_ref[...], wq_ref[0:aw, :])
        dz += _dot(dk_ref[...], wq_ref[aw:2 * aw, :])
        dz += _dot(dv_ref[...], wq_ref[2 * aw:3 * aw, :])
        dz += _dot(dxr_ref[...], wr_ref[0:rw, :])
        dz += _dot(dyr_ref[...], wr_ref[rw:2 * rw, :])
        dz += _dot(dfl_ref[...], wr_ref[2 * rw:2 * rw + LANES, :])
        dx, dg = _rms_bwd(h_ref[...], g_ref[...], dz)
        dh_ref[...] = dh2_ref[...] + dx
        _accumulate(dg_ref, dg, i == 0)

    row = lambda i: (i, 0)
    fix = lambda i: (0, 0)
    return pl.pallas_call(
        body, name="in_proj_bwd", grid=(tp // tm,),
        in_specs=[pl.BlockSpec((tm, d), row),
                  pl.BlockSpec((tm, aw), row), pl.BlockSpec((tm, aw), row), pl.BlockSpec((tm, aw), row),
                  pl.BlockSpec((tm, rw), row), pl.BlockSpec((tm, rw), row), pl.BlockSpec((tm, LANES), row),
                  pl.BlockSpec((3 * aw, d), fix), pl.BlockSpec(wrest_t.shape, fix),
                  pl.BlockSpec((tm, d), row), pl.BlockSpec((1, d), fix)],
        out_specs=[pl.BlockSpec((tm, d), row), pl.BlockSpec((1, d), fix)],
        out_shape=[jax.ShapeDtypeStruct((tp, d), F32), jax.ShapeDtypeStruct((1, d), F32)],
        compiler_params=_cparams("arbitrary"),
    )(dh2, dq, dk, dv, dxr, dyr, dfl, w_in_t, wrest_t, h, g1)


def dw_in_t(z, parts, nh, tr):
    tp, d = z.shape
    dq, dk, dv, dxr, dyr, dfl = parts
    aw, rw = dq.shape[1], dxr.shape[1]
    d_in = 3 * aw + nh + 2 * rw
    nr = tp // tr
    offs = [(0, aw), (aw, aw), (2 * aw, aw), (3 * aw + nh, rw), (3 * aw + nh + rw, rw)]

    def body(z_ref, dq_ref, dk_ref, dv_ref, dxr_ref, dyr_ref, dfl_ref, o_ref, acc):
        r = pl.program_id(0)

        @pl.when(r == 0)
        def _():
            acc[...] = jnp.zeros_like(acc)

        zt = z_ref[...]
        for (o, n), ref in zip(offs, (dq_ref, dk_ref, dv_ref, dxr_ref, dyr_ref)):
            acc[o:o + n, :] += _dot_tn(ref[...], zt)
        acc[3 * aw:3 * aw + nh, :] += _dot_tn(dfl_ref[...], zt)[0:nh, :]

        @pl.when(r == nr - 1)
        def _():
            o_ref[...] = acc[...].astype(BF16)

    row = lambda r: (r, 0)
    return pl.pallas_call(
        body, name="dw_in", grid=(nr,),
        in_specs=[pl.BlockSpec((tr, d), row),
                  pl.BlockSpec((tr, aw), row), pl.BlockSpec((tr, aw), row), pl.BlockSpec((tr, aw), row),
                  pl.BlockSpec((tr, rw), row), pl.BlockSpec((tr, rw), row), pl.BlockSpec((tr, LANES), row)],
        out_specs=pl.BlockSpec((d_in, d), lambda r: (0, 0)),
        out_shape=jax.ShapeDtypeStruct((d_in, d), BF16),
        scratch_shapes=[pltpu.VMEM((d_in, d), F32)],
        compiler_params=_cparams("arbitrary"),
    )(z, dq, dk, dv, dxr, dyr, dfl)


def _place():
    return lax.axis_index("x"), lax.axis_index("y"), lax.axis_index("c")


HBM = pl.BlockSpec(memory_space=pltpu.HBM)
SEM = pl.BlockSpec(memory_space=pltpu.SEMAPHORE)
EFFECT = pltpu.SideEffectType.DATAFLOW_SIDE_EFFECTING


def _in_hbm(a):
    return pltpu.with_memory_space_constraint(a, pltpu.HBM)


def _gather_targets(x, y, c):
    return [(x, y, 1 - c), (1 - x, y, c), (x, 1 - y, c), (1 - x, 1 - y, c)]


def _slot(t):
    return 4 * t[0] + 2 * t[1] + t[2]


def gather_start(groups):
    flat = [a for g in groups for a in g]
    n = len(flat)
    ng = len(groups)
    lands = [lax.empty((N_DEV,) + a.shape, a.dtype) for a in flat]

    def body(*refs):
        src, land = refs[:n], refs[n:2 * n]
        sems = refs[2 * n:2 * n + 2 * ng]
        token = refs[-1]
        x, y, c = _place()
        me = 4 * x + 2 * y + c
        i = 0
        for gi, g in enumerate(groups):
            for a in range(len(g)):
                for k, t in enumerate(_gather_targets(x, y, c)):
                    pltpu.make_async_remote_copy(
                        src_ref=src[i], dst_ref=land[i].at[me],
                        send_sem=sems[2 * gi].at[4 * a + k], recv_sem=sems[2 * gi + 1].at[4 * a + k],
                        device_id=t, device_id_type=MESH).start()
                i += 1
        token[...] = jnp.zeros_like(token)

    sem_shapes = []
    for g in groups:
        sem_shapes += [pltpu.SemaphoreType.DMA((4 * len(g),)), pltpu.SemaphoreType.DMA((4 * len(g),))]
    out = pl.pallas_call(
        body, name="gather_start",
        out_shape=sem_shapes + [pltpu.HBM(a.shape, a.dtype) for a in flat + lands]
        + [jax.ShapeDtypeStruct((SUBLANES, LANES), F32)],
        in_specs=[HBM] * (2 * n),
        out_specs=[SEM] * (2 * ng) + [HBM] * (2 * n) + [pl.BlockSpec(memory_space=pltpu.VMEM)],
        input_output_aliases={i: 2 * ng + i for i in range(2 * n)},
        compiler_params=pltpu.CompilerParams(has_side_effects=EFFECT),
    )(*[_in_hbm(a) for a in flat + lands])
    sems = out[:2 * ng]
    thru = out[2 * ng:2 * ng + 2 * n]
    srcs_t, lands_t = thru[:n], thru[n:]
    res, i = [], 0
    for gi, g in enumerate(groups):
        res.append((sems[2 * gi], sems[2 * gi + 1], srcs_t[i:i + len(g)], lands_t[i:i + len(g)]))
        i += len(g)
    return res, out[-1]


def gather_wait(send, recv, srcs, lands, after, name):
    n = len(srcs)

    def body(*refs):
        src, land = refs[:n], refs[n:2 * n]
        send_sem, recv_sem = refs[2 * n], refs[2 * n + 1]
        x, y, c = _place()
        for a in range(n):
            for k, t in enumerate(_gather_targets(x, y, c)):
                cp = pltpu.make_async_remote_copy(
                    src_ref=src[a], dst_ref=land[a].at[_slot(t)],
                    send_sem=send_sem.at[4 * a + k], recv_sem=recv_sem.at[4 * a + k],
                    device_id=t, device_id_type=MESH)
                cp.wait_send()
                cp.wait_recv()

    out = pl.pallas_call(
        body, name=name,
        out_shape=[pltpu.HBM(a.shape, a.dtype) for a in list(srcs) + list(lands)],
        in_specs=[HBM] * (2 * n) + [SEM, SEM, ANY],
        out_specs=[HBM] * (2 * n),
        input_output_aliases={i: i for i in range(2 * n)},
        compiler_params=pltpu.CompilerParams(has_side_effects=EFFECT),
    )(*srcs, *lands, send, recv, after)
    return out[:n], out[n:]


def gather_forward(lands, name):
    n = len(lands)

    def body(*refs):
        land, out = refs[:n], refs[n:2 * n]
        send_sems, recv_sems = refs[2 * n:]
        x, y, c = _place()
        sibling = (x, y, 1 - c)
        chips = [(1 - x, y), (x, 1 - y), (1 - x, 1 - y)]

        def fwd(a, j, core):
            s = _slot((*chips[j], core))
            return pltpu.make_async_remote_copy(src_ref=land[a].at[s], dst_ref=out[a].at[s],
                                                send_sem=send_sems.at[a, j], recv_sem=recv_sems.at[a, j],
                                                device_id=sibling, device_id_type=MESH)

        sends = [fwd(a, j, c) for a in range(n) for j in range(3)]
        for cp in sends:
            cp.start()
        for a in range(n):
            for j in range(3):
                fwd(a, j, 1 - c).wait_recv()
        for cp in sends:
            cp.wait_send()

    return pl.pallas_call(
        body, name=name,
        in_specs=[ANY] * n, out_specs=[ANY] * n,
        out_shape=[jax.ShapeDtypeStruct(a.shape, a.dtype) for a in lands],
        input_output_aliases={i: i for i in range(n)},
        scratch_shapes=[pltpu.SemaphoreType.DMA((n, 3)), pltpu.SemaphoreType.DMA((n, 3))],
    )(*lands)


def _relations():
    return [(dx, dy, dc) for dx in (0, 1) for dy in (0, 1) for dc in (0, 1) if dx + dy + dc]


def _peer(x, y, c, rel):
    return ((1 - x) if rel[0] else x, (1 - y) if rel[1] else y, (1 - c) if rel[2] else c)


def exchange_start(srcs, lands, layer, name):
    n = len(srcs)

    def body(*refs):
        src, land = refs[:n], refs[n:2 * n]
        send_sem, recv_sem = refs[2 * n], refs[2 * n + 1]
        token = refs[-1]
        x, y, c = _place()
        me = 4 * x + 2 * y + c
        for k, rel in enumerate(_relations()):
            peer = _peer(x, y, c, rel)
            for a in range(n):
                pltpu.make_async_remote_copy(
                    src_ref=src[a] if layer is None else src[a].at[_slot(peer)],
                    dst_ref=land[a].at[me] if layer is None else land[a].at[me, layer],
                    send_sem=send_sem.at[7 * a + k], recv_sem=recv_sem.at[7 * a + k],
                    device_id=peer, device_id_type=MESH).start()
        token[...] = jnp.zeros_like(token)

    out = pl.pallas_call(
        body, name=name,
        out_shape=[pltpu.SemaphoreType.DMA((7 * n,)), pltpu.SemaphoreType.DMA((7 * n,))]
        + [pltpu.HBM(a.shape, a.dtype) for a in list(srcs) + list(lands)]
        + [jax.ShapeDtypeStruct((SUBLANES, LANES), F32)],
        in_specs=[HBM] * (2 * n),
        out_specs=[SEM, SEM] + [HBM] * (2 * n) + [pl.BlockSpec(memory_space=pltpu.VMEM)],
        input_output_aliases={i: 2 + i for i in range(2 * n)},
        compiler_params=pltpu.CompilerParams(has_side_effects=EFFECT),
    )(*[_in_hbm(a) for a in list(srcs) + list(lands)])
    return out[0], out[1], out[2:2 + n], out[2 + n:2 + 2 * n], out[-1][0, 0]


def exchange_wait(send, recv, srcs, lands, after, layer, name):
    n = len(srcs)

    def body(*refs):
        src, land = refs[:n], refs[n:2 * n]
        send_sem, recv_sem = refs[2 * n], refs[2 * n + 1]
        x, y, c = _place()
        for k, rel in enumerate(_relations()):
            peer = _peer(x, y, c, rel)
            for a in range(n):
                cp = pltpu.make_async_remote_copy(
                    src_ref=src[a] if layer is None else src[a].at[_slot(peer)],
                    dst_ref=land[a].at[_slot(peer)] if layer is None else land[a].at[_slot(peer), layer],
                    send_sem=send_sem.at[7 * a + k], recv_sem=recv_sem.at[7 * a + k],
                    device_id=peer, device_id_type=MESH)
                cp.wait_send()
                cp.wait_recv()

    out = pl.pallas_call(
        body, name=name,
        out_shape=[pltpu.HBM(a.shape, a.dtype) for a in list(srcs) + list(lands)],
        in_specs=[HBM] * (2 * n) + [SEM, SEM, ANY],
        out_specs=[HBM] * (2 * n),
        input_output_aliases={i: i for i in range(2 * n)},
        compiler_params=pltpu.CompilerParams(has_side_effects=EFFECT),
    )(*srcs, *lands, send, recv, after)
    return out[:n], out[n:]


def _adamw_math(g, w, m, v):
    m = ADAM_B1 * m + (1.0 - ADAM_B1) * g
    v = ADAM_B2 * v + (1.0 - ADAM_B2) * (g * g)
    m_hat = m / (1.0 - ADAM_B1 ** ADAM_STEP)
    v_hat = v / (1.0 - ADAM_B2 ** ADAM_STEP)
    delta = -ADAM_LR * (m_hat / (jnp.sqrt(v_hat) + ADAM_EPS) + ADAM_WD * w)
    return delta, m, v


def sum_adamw(parts, w, m, v, tr, name):
    npart, rows, cols = parts.shape

    def body(p_ref, w_ref, m_ref, v_ref, g_ref, d_ref, nm_ref, nv_ref):
        g = p_ref[0].astype(F32)
        for p in range(1, npart):
            g = g + p_ref[p].astype(F32)
        delta, nm, nv = _adamw_math(g, w_ref[...], m_ref[...], v_ref[...])
        g_ref[...] = g
        d_ref[...] = delta
        nm_ref[...] = nm
        nv_ref[...] = nv

    blk = pl.BlockSpec((tr, cols), lambda i: (i, 0))
    return pl.pallas_call(
        body, name=name, grid=(rows // tr,),
        in_specs=[pl.BlockSpec((npart, tr, cols), lambda i: (0, i, 0)), blk, blk, blk],
        out_specs=[blk] * 4,
        out_shape=[jax.ShapeDtypeStruct((rows, cols), F32)] * 4,
        compiler_params=_cparams("parallel"),
    )(parts, w, m, v)


def sum_adamw_t(parts, w, m, v, name):
    npart, nl, rows, cols = parts.shape

    def body(p_ref, w_ref, m_ref, v_ref, g_ref, d_ref, nm_ref, nv_ref):
        g = p_ref[0].astype(F32)
        for p in range(1, npart):
            g = g + p_ref[p].astype(F32)
        delta, nm, nv = _adamw_math(g, w_ref[...], m_ref[...], v_ref[...])
        g_ref[...] = g
        d_ref[...] = delta
        nm_ref[...] = nm
        nv_ref[...] = nv

    blk = pl.BlockSpec((None, rows, cols), lambda l: (l, 0, 0))
    return pl.pallas_call(
        body, name=name, grid=(nl,),
        in_specs=[pl.BlockSpec((npart, None, rows, cols), lambda l: (0, l, 0, 0)), blk, blk, blk],
        out_specs=[blk] * 4,
        out_shape=[jax.ShapeDtypeStruct((nl, rows, cols), F32)] * 4,
        compiler_params=_cparams("parallel"),
    )(parts, w, m, v)


def sum_parts(parts, name):
    npart, rows, cols = parts.shape

    def body(p_ref, g_ref):
        g = p_ref[0].astype(F32)
        for p in range(1, npart):
            g = g + p_ref[p].astype(F32)
        g_ref[...] = g

    return pl.pallas_call(
        body, name=name, grid=(1,),
        in_specs=[pl.BlockSpec((npart, rows, cols), lambda i: (0, 0, 0))],
        out_specs=pl.BlockSpec((rows, cols), lambda i: (0, 0)),
        out_shape=jax.ShapeDtypeStruct((rows, cols), F32),
        compiler_params=_cparams("arbitrary"),
    )(parts)


def _round_up(n, m):
    return (n + m - 1) // m * m


def _block_diag_pairs(w):
    nb, b, _ = w.shape
    per = LANES // b
    ng = nb // per
    w = w.reshape(ng, per, b, b)
    eye = jnp.eye(per, dtype=w.dtype)
    out = jnp.einsum('gpij,pq->gpiqj', w, eye).reshape(ng, LANES, LANES)
    return out.astype(BF16)


def _block_diag_extract(g, b):
    ng = g.shape[0]
    per = LANES // b
    g = g.reshape(ng, per, b, per, b)
    idx = jnp.arange(per)
    return g[:, idx, :, idx, :].transpose(1, 0, 2, 3).reshape(ng * per, b, b)


def _tiles(v):
    v = v.reshape(-1)
    n = _round_up(v.shape[0], SUBLANES * LANES)
    return jnp.pad(v, (0, n - v.shape[0])).reshape(-1, LANES)


SMALL = ['attn_norm_g', 'b_f', 'conv_w', 'conv_b', 'w_gate_a', 'b_gate_a', 'w_gate_x', 'b_gate_x',
         'lru_L', 'attn_out_g', 'rec_out_g', 'mlp_norm_g', 'final_g', 'meta']


def _pack(d):
    return jnp.concatenate([_tiles(d[n]) for n in SMALL], axis=0)


def _unpack(vec, shapes):
    out, r = {}, 0
    for n in SMALL:
        size = math.prod(shapes[n])
        nr = _round_up(size, SUBLANES * LANES) // LANES
        out[n] = vec[r:r + nr].reshape(-1)[:size].reshape(shapes[n])
        r += nr
    return out


def _row_tile(tp):
    return tp // 4 if (tp // 4) % 16 == 0 else tp


def local_step(x, tgt, meta, small, hooks):
    s, d = x.shape
    t_real = s + N_META
    tp = _round_up(t_real, ATT_BLOCK)
    depth = small['attn_norm_g'].shape[0]
    nh = small['b_f'].shape[1]
    rw = small['conv_b'].shape[1]
    blk = small['w_gate_a'].shape[2]
    tm = _row_tile(tp)
    tm2 = tp // 2
    fcol = 2 * rw // LANES

    h = jnp.concatenate([meta, x, jnp.zeros((tp - t_real, d), F32)], axis=0)
    tgt_p = jnp.pad(tgt, ((N_META, tp - t_real), (0, 0)))
    row = lambda v: v.reshape(1, -1)
    bf_pad = jnp.pad(small['b_f'], ((0, 0), (0, LANES - nh)))

    saved = []
    for l in range(depth):
        w_in_t, wrest_t, wout = hooks.mixer_weights(l, h)
        wga = _block_diag_pairs(small['w_gate_a'][l])
        wgx = _block_diag_pairs(small['w_gate_x'][l])
        z, qkv, rest = in_proj(h, row(small['attn_norm_g'][l]), w_in_t, wrest_t, 3 * nh * HEAD_DIM, tm)
        c, ct = fgate_fwd(rest, bf_pad[l:l + 1], fcol)
        o, lset = attn_fwd(qkv, c, ct, nh)
        rec, hr, xc = rec_fwd(rest, small['conv_w'][l], row(small['conv_b'][l]), wga, row(small['b_gate_a'][l]),
                              wgx, row(small['b_gate_x'][l]), row(small['lru_L'][l]), rw)
        h2, mix, z2 = out_proj(h, o, rec, row(small['attn_out_g'][l]), row(small['rec_out_g'][l]), wout,
                               row(small['mlp_norm_g'][l]), tm)
        gup, gdown = hooks.mlp_weights(l, z2)
        u, h3 = mlp_fwd(z2, h2, gup, gdown, tm2)
        saved.append(dict(h=h, z=z, qkv=qkv, rest=rest, c=c, ct=ct, o=o, lset=lset, rec=rec, hr=hr, xc=xc,
                          h2=h2, mix=mix, z2=z2, u=u, wga=wga, wgx=wgx,
                          w_in_t=w_in_t, wrest_t=wrest_t, wout=wout, gup=gup, gdown=gdown))
        h = h3

    dh, dgf, loss = loss_head(h, row(small['final_g']), tgt_p, t_real, tm)

    gs = {n: [None] * depth for n in SMALL if n not in ('final_g', 'meta')}
    tok = jnp.zeros((), F32)
    for l in reversed(range(depth)):
        sv = saved[l]
        gup, gdown = sv['gup'], sv['gdown']
        tf = gup.shape[2]
        dup, dh2, dg2, dhb = mlp_bwd(dh, sv['u'], sv['h2'], row(small['mlp_norm_g'][l]) + tok, gup, gdown, tm2)
        gs['mlp_norm_g'][l] = dg2[0]
        do, drec, dga, dgr = out_proj_bwd(dh2, sv['o'], sv['rec'], row(small['attn_out_g'][l]),
                                          row(small['rec_out_g'][l]), sv['wout'], tm)
        gs['attn_out_g'][l] = dga[0]
        gs['rec_out_g'][l] = dgr[0]
        blocks = dict(
            w_down=mm_tn(sv['u'], dhb, tk=tf, tn=d, out_dtype=BF16, name="dw_down",
                         square_a=True).reshape(N_DEV, tf, d),
            w_up=mm_tn(sv['z2'], dup, tk=d, tn=tf, out_dtype=BF16, name="dw_up", blocked_n=True),
            w_out=mm_tn(sv['mix'], dh2, tk=d, tn=d // 2, out_dtype=BF16,
                        name="dw_out").reshape(N_DEV, d // N_DEV, d))
        tok = hooks.grads_ready(l, 'mlp', blocks)
        dxr, dyr, dwga, dwgx, vec = rec_bwd(drec, sv['hr'], sv['xc'], sv['rest'], small['conv_w'][l],
                                            row(small['conv_b'][l]) + tok, sv['wga'], row(small['b_gate_a'][l]),
                                            sv['wgx'], row(small['b_gate_x'][l]), row(small['lru_L'][l]), rw)
        gs['w_gate_a'][l] = _block_diag_extract(dwga, blk)
        gs['w_gate_x'][l] = _block_diag_extract(dwgx, blk)
        vec = vec.transpose(1, 0, 2).reshape(SUBLANES, rw)
        gs['conv_w'][l] = vec[0:CONV_WIDTH]
        gs['conv_b'][l] = vec[4]
        gs['b_gate_a'][l] = vec[5]
        gs['b_gate_x'][l] = vec[6]
        gs['lru_L'][l] = vec[7]
        dq, dk, dv, drow, dcol = attn_bwd(sv['qkv'], do, sv['o'], sv['lset'], sv['c'], sv['ct'] + tok, nh)
        drow8 = drow[:, 0:2, :].reshape(nh, tp)
        if nh < SUBLANES:
            drow8 = jnp.pad(drow8, ((0, SUBLANES - nh), (0, 0)))
        dfl, dbf = fgate_bwd(drow8, dcol, sv['rest'], bf_pad[l:l + 1], fcol)
        gs['b_f'][l] = dbf[0, 0:nh]
        parts = (dq, dk, dv, dxr, dyr, dfl)
        dh, dg1 = in_proj_bwd(dh2, parts, sv['w_in_t'], sv['wrest_t'], sv['h'], row(small['attn_norm_g'][l]), tm)
        gs['attn_norm_g'][l] = dg1[0]
        tok = jnp.zeros((), F32)
        if l == 0:
            grads = {n: jnp.stack(v) for n, v in gs.items()}
            grads['final_g'] = dgf[0]
            grads['meta'] = dh[0:N_META]
            tok = hooks.small_ready(grads)
        dw_in = dw_in_t(sv['z'], parts, nh, tm2)
        dw_in = dw_in.reshape(N_DEV, dw_in.shape[0] // N_DEV, d) + tok.astype(BF16)
        tok = hooks.grads_ready(l, 'in', dict(w_in=dw_in))

    return loss[0, 0], dh, tok


def prep_weights(g_in, g_out, nh, rw):
    d = g_in.shape[2]
    w_in_t = g_in.reshape(-1, d)
    f0 = 3 * nh * HEAD_DIM
    wrest_t = jnp.concatenate([w_in_t[f0 + nh:f0 + nh + 2 * rw],
                               jnp.pad(w_in_t[f0:f0 + nh], ((0, LANES - nh), (0, 0)))], axis=0)
    return w_in_t, wrest_t, g_out.reshape(d, d)


BIG = ['w_in', 'w_out', 'w_up', 'w_down']
EXCHANGE_GROUPS = {'mlp': ['w_down', 'w_up', 'w_out'], 'in': ['w_in']}
WEIGHTS = ['meta', 'attn_norm_g', 'w_in', 'b_f', 'conv_w', 'conv_b', 'w_gate_a', 'b_gate_a', 'w_gate_x', 'b_gate_x',
           'lru_L', 'attn_out_g', 'rec_out_g', 'w_out', 'mlp_norm_g', 'w_up', 'w_down', 'final_g']


def _set_own(arr, own, me):
    return lax.dynamic_update_slice_in_dim(arr, own[None], me, axis=0)


class _Step:
    def __init__(self, w, nh, rw, me):
        self.w, self.nh, self.rw, self.me = w, nh, rw, me
        depth = w['w_in'].shape[0]
        groups = []
        for l in range(depth):
            groups.append([w['w_in_t'][:, l, :].astype(BF16), w['w_out'][l].astype(BF16)])
            groups.append([w['w_up'][l].astype(BF16), w['w_down'][l].astype(BF16)])
        groups[0] = groups[0] + [w['meta'], w['conv_w']]
        self.pending, _ = gather_start(groups)
        self.gathered = {}
        self.lands = {n: lax.empty((N_DEV,) + w[n].shape, BF16) for n in BIG}
        din8, _, d = w['w_in_t'].shape
        self.lands['w_in'] = lax.empty((N_DEV, depth, din8, d), BF16)
        self.started = []
        self.small = None

    def group(self, gi, after):
        if gi not in self.gathered:
            send, recv, srcs, lands = self.pending[gi]
            srcs, lands = gather_wait(send, recv, srcs, lands, after, "gather_wait_%d" % gi)
            lands = gather_forward(lands, "gather_forward_%d" % gi)
            self.gathered[gi] = [_set_own(g, own, self.me) for g, own in zip(lands, srcs)]
        return self.gathered[gi]

    def mixer_weights(self, l, after):
        g = self.group(2 * l, after)
        return prep_weights(g[0], g[1], self.nh, self.rw)

    def mlp_weights(self, l, after):
        g = self.group(2 * l + 1, after)
        return g[0], g[1]

    def grads_ready(self, l, group, blocks):
        names = EXCHANGE_GROUPS[group]
        send, recv, srcs, lands, token = exchange_start(
            [blocks[n] for n in names], [self.lands[n] for n in names], l, "exchange_start_%s_%d" % (group, l))
        for n, a in zip(names, lands):
            self.lands[n] = a
        self.started.append((l, group, send, recv, srcs))
        return token

    def small_ready(self, grads):
        self.small_shapes = {n: grads[n].shape for n in SMALL}
        packed = _pack(grads).astype(BF16)
        send, recv, srcs, lands, token = exchange_start(
            [packed], [lax.empty((N_DEV,) + packed.shape, BF16)], None, "small_start")
        self.small = (send, recv, srcs, lands)
        return token

    def small_sum(self, after):
        send, recv, srcs, lands = self.small
        srcs, lands = exchange_wait(send, recv, srcs, lands, after, None, "small_wait")
        parts = _set_own(lands[0], srcs[0], self.me)
        return _unpack(sum_parts(parts, "sum_small_grads"), self.small_shapes)

    def received(self, group, after):
        names = EXCHANGE_GROUPS[group]
        own = {n: [None] * self.w[n].shape[0] for n in names}
        for l, grp, send, recv, srcs in self.started:
            if grp != group:
                continue
            srcs, lands = exchange_wait(send, recv, srcs, [self.lands[n] for n in names], after, l,
                                        "exchange_wait_%s_%d" % (group, l))
            for n, a, sr in zip(names, lands, srcs):
                self.lands[n] = a
                own[n][l] = lax.dynamic_index_in_dim(sr, self.me, 0, keepdims=False)
        return {n: _set_own(self.lands[n], jnp.stack(own[n]), self.me) for n in names}


def kernel(x, meta, attn_norm_g, w_in, b_f, conv_w, conv_b, w_gate_a, b_gate_a, w_gate_x, b_gate_x, lru_L, attn_out_g, rec_out_g, w_out, mlp_norm_g, w_up, w_down, final_g, loss_target, m_meta, m_attn_norm_g, m_w_in, m_b_f, m_conv_w, m_conv_b, m_w_gate_a, m_b_gate_a, m_w_gate_x, m_b_gate_x, m_lru_L, m_attn_out_g, m_rec_out_g, m_w_out, m_mlp_norm_g, m_w_up, m_w_down, m_final_g, v_meta, v_attn_norm_g, v_w_in, v_b_f, v_conv_w, v_conv_b, v_w_gate_a, v_b_gate_a, v_w_gate_x, v_b_gate_x, v_lru_L, v_attn_out_g, v_rec_out_g, v_w_out, v_mlp_norm_g, v_w_up, v_w_down, v_final_g):
    w = dict(meta=meta, attn_norm_g=attn_norm_g, w_in=w_in, b_f=b_f, conv_w=conv_w, conv_b=conv_b,
             w_gate_a=w_gate_a, b_gate_a=b_gate_a, w_gate_x=w_gate_x, b_gate_x=b_gate_x, lru_L=lru_L,
             attn_out_g=attn_out_g, rec_out_g=rec_out_g, w_out=w_out, mlp_norm_g=mlp_norm_g, w_up=w_up,
             w_down=w_down, final_g=final_g)
    mo = dict(meta=m_meta, attn_norm_g=m_attn_norm_g, w_in=m_w_in, b_f=m_b_f, conv_w=m_conv_w, conv_b=m_conv_b,
              w_gate_a=m_w_gate_a, b_gate_a=m_b_gate_a, w_gate_x=m_w_gate_x, b_gate_x=m_b_gate_x, lru_L=m_lru_L,
              attn_out_g=m_attn_out_g, rec_out_g=m_rec_out_g, w_out=m_w_out, mlp_norm_g=m_mlp_norm_g,
              w_up=m_w_up, w_down=m_w_down, final_g=m_final_g)
    vo = dict(meta=v_meta, attn_norm_g=v_attn_norm_g, w_in=v_w_in, b_f=v_b_f, conv_w=v_conv_w, conv_b=v_conv_b,
              w_gate_a=v_w_gate_a, b_gate_a=v_b_gate_a, w_gate_x=v_w_gate_x, b_gate_x=v_b_gate_x, lru_L=v_lru_L,
              attn_out_g=v_attn_out_g, rec_out_g=v_rec_out_g, w_out=v_w_out, mlp_norm_g=v_mlp_norm_g,
              w_up=v_w_up, w_down=v_w_down, final_g=v_final_g)
    depth = w_in.shape[0]
    nh = b_f.shape[1]
    rw = conv_b.shape[1]
    me = 4 * lax.axis_index("x") + 2 * lax.axis_index("y") + lax.axis_index("c")

    w['w_in_t'] = jnp.transpose(w_in, (2, 0, 1))
    swap = lambda a: jnp.swapaxes(a, 1, 2)
    step = _Step(w, nh, rw, me)
    g0 = step.group(0, meta)
    meta_full = g0[2].transpose(1, 0, 2).reshape(N_META, -1)
    conv_full = g0[3].transpose(1, 2, 0, 3).reshape(depth, CONV_WIDTH, rw)
    small = {n: w[n] for n in SMALL}
    small['conv_w'] = conv_full

    loss_part, dh0, tok = local_step(x[0], loss_target[0], meta_full, small, step)
    loss = lax.psum(loss_part, ("x", "y", "c"))
    grad_x = dh0[N_META:N_META + x.shape[1]][None]

    out_g, out_d, out_m, out_v = {}, {}, {}, {}

    def update_big(group, after):
        for n, r in step.received(group, after).items():
            if n == 'w_in':
                out = sum_adamw_t(r, swap(w[n]), swap(mo[n]), swap(vo[n]), "adamw_w_in")
                out = [swap(a) for a in out]
            else:
                shp = w[n].shape
                rows, cols = shp[0] * shp[1], shp[2]
                tr = min(512 if cols <= 512 else 256, rows)
                out = sum_adamw(r.reshape(N_DEV, rows, cols), w[n].reshape(rows, cols), mo[n].reshape(rows, cols),
                                vo[n].reshape(rows, cols), tr, "adamw_" + n)
                out = [a.reshape(shp) for a in out]
            out_g[n], out_d[n], out_m[n], out_v[n] = out
            after = out[0]
        return after

    after = update_big('mlp', step.started[-1][4][0])

    gsum = step.small_sum(after)
    gsum['meta'] = lax.dynamic_slice_in_dim(gsum['meta'], me * meta.shape[1], meta.shape[1], axis=1)
    gsum['conv_w'] = lax.dynamic_slice_in_dim(gsum['conv_w'], me * conv_w.shape[2], conv_w.shape[2], axis=2)
    packed = [_pack(t) for t in (gsum, {n: w[n] for n in SMALL}, {n: mo[n] for n in SMALL},
                                 {n: vo[n] for n in SMALL})]
    res = sum_adamw(packed[0][None], packed[1], packed[2], packed[3], packed[1].shape[0], "adamw_small")
    shapes = {n: w[n].shape for n in SMALL}
    for dst, vec in zip((out_g, out_d, out_m, out_v), res):
        dst.update(_unpack(vec, shapes))

    update_big('in', res[0])

    return (loss, grad_x, *[out_g[n] for n in WEIGHTS], *[out_d[n] for n in WEIGHTS],
            *[out_m[n] for n in WEIGHTS], *[out_v[n] for n in WEIGHTS])
```

```python
import functools
import math

import jax
import jax.numpy as jnp
from jax import lax
from jax.experimental import pallas as pl
from jax.experimental.pallas import tpu as pltpu

F32 = jnp.float32
BF16 = jnp.bfloat16

N_DEV = 8
N_META = 16
HEAD_DIM = 64
CONV_WIDTH = 4
RG_C = 8.0
NORM_EPS = 1e-6
LANES = 128
SUBLANES = 8
ATT_BLOCK = 128
ATT_TQ = 256
NEG_BIG = -1e30
ATT_SCALE = 1.0 / math.sqrt(HEAD_DIM)

ADAM_LR = 0.001
ADAM_B1 = 0.9
ADAM_B2 = 0.999
ADAM_EPS = 1e-08
ADAM_WD = 0.01
ADAM_STEP = 10

VMEM_LIMIT_BYTES = 56 * 1024 * 1024
MESH = pl.DeviceIdType.MESH
ANY = pl.BlockSpec(memory_space=pl.ANY)


def _cparams(*sem):
    return pltpu.CompilerParams(dimension_semantics=sem if sem else None,
                                vmem_limit_bytes=VMEM_LIMIT_BYTES)


def _dot(a, b):
    return jnp.dot(a, b, preferred_element_type=F32)


def _dot_nt(a, b):
    return lax.dot_general(a, b, (((1,), (1,)), ((), ())), preferred_element_type=F32)


def _dot_tn(a, b):
    return lax.dot_general(a, b, (((0,), (0,)), ((), ())), preferred_element_type=F32)


def _sigmoid(x):
    return 0.5 * (1.0 + jnp.tanh(0.5 * x))


def _log_sigmoid(x):
    return jnp.minimum(x, 0.0) - jnp.log(1.0 + jnp.exp(-jnp.abs(x)))


def _expm1(x):
    series = x * (1.0 + x * (0.5 + x * (1.0 / 6.0 + x * (1.0 / 24.0))))
    return jnp.where(jnp.abs(x) < 1e-2, series, jnp.exp(x) - 1.0)


_GELU_K = math.sqrt(2.0 / math.pi)
_GELU_C = 0.044715


def _gelu(x):
    t = jnp.tanh(_GELU_K * (x + _GELU_C * x * x * x))
    return 0.5 * x * (1.0 + t)


def _gelu_grad(x):
    t = jnp.tanh(_GELU_K * (x + _GELU_C * x * x * x))
    return 0.5 * (1.0 + t) + 0.5 * x * (1.0 - t * t) * _GELU_K * (1.0 + 3.0 * _GELU_C * x * x)


def _split3_dot(tri, x):
    hi = x.astype(BF16)
    r1 = x - hi.astype(F32)
    mid = r1.astype(BF16)
    lo = (r1 - mid.astype(F32)).astype(BF16)
    return _dot(tri, hi) + _dot(tri, mid) + _dot(tri, lo)


def _dot_split3(x, sel):
    hi = x.astype(BF16)
    r1 = x - hi.astype(F32)
    mid = r1.astype(BF16)
    lo = (r1 - mid.astype(F32)).astype(BF16)
    return _dot(hi, sel) + _dot(mid, sel) + _dot(lo, sel)


def _rms_fwd(x, g):
    r = lax.rsqrt(jnp.mean(x * x, axis=-1, keepdims=True) + NORM_EPS)
    return x * r * g


def _rms_bwd(x, g, dy):
    r = lax.rsqrt(jnp.mean(x * x, axis=-1, keepdims=True) + NORM_EPS)
    xn = x * r
    dxn = dy * g
    dx = r * (dxn - xn * jnp.mean(dxn * xn, axis=-1, keepdims=True))
    return dx, jnp.sum(dy * xn, axis=0, keepdims=True)


def _accumulate(ref, val, first):
    @pl.when(first)
    def _():
        ref[...] = val

    @pl.when(jnp.logical_not(first))
    def _():
        ref[...] += val


def in_proj(h, g1, w_in_t, wrest_t, nq, tm):
    tp, d = h.shape
    nr = wrest_t.shape[0]

    def body(h_ref, g_ref, wq_ref, wr_ref, z_ref, qkv_ref, rest_ref):
        z = _rms_fwd(h_ref[...], g_ref[...]).astype(BF16)
        z_ref[...] = z
        qkv_ref[...] = _dot_nt(z, wq_ref[...]).astype(BF16)
        rest_ref[...] = _dot_nt(z, wr_ref[...])

    return pl.pallas_call(
        body, name="in_proj", grid=(tp // tm,),
        in_specs=[pl.BlockSpec((tm, d), lambda i: (i, 0)),
                  pl.BlockSpec((1, d), lambda i: (0, 0)),
                  pl.BlockSpec((nq, d), lambda i: (0, 0)),
                  pl.BlockSpec((nr, d), lambda i: (0, 0))],
        out_specs=[pl.BlockSpec((tm, d), lambda i: (i, 0)),
                   pl.BlockSpec((tm, nq), lambda i: (i, 0)),
                   pl.BlockSpec((tm, nr), lambda i: (i, 0))],
        out_shape=[jax.ShapeDtypeStruct((tp, d), BF16),
                   jax.ShapeDtypeStruct((tp, nq), BF16),
                   jax.ShapeDtypeStruct((tp, nr), F32)],
        compiler_params=_cparams("parallel"),
    )(h, g1, w_in_t, wrest_t)


def fgate_fwd(rest, bf_pad, fcol):
    tp = rest.shape[0]
    nb = tp // ATT_BLOCK

    def body(f_ref, b_ref, c_ref, ct_ref):
        r_i = lax.broadcasted_iota(jnp.int32, (ATT_BLOCK, ATT_BLOCK), 0)
        c_i = lax.broadcasted_iota(jnp.int32, (ATT_BLOCK, ATT_BLOCK), 1)
        tri = (r_i >= c_i).astype(BF16)
        carry = jnp.zeros((1, LANES), F32)
        for i in range(nb):
            sl = slice(i * ATT_BLOCK, (i + 1) * ATT_BLOCK)
            lf = _log_sigmoid(f_ref[sl, :] + b_ref[...])
            cs = _split3_dot(tri, lf) + carry
            carry = cs[ATT_BLOCK - 1:ATT_BLOCK, :]
            c_ref[sl, :] = cs
            ct_ref[:, sl] = cs.T[0:SUBLANES, :]

    return pl.pallas_call(
        body, name="fgate_fwd", grid=(1,),
        in_specs=[pl.BlockSpec((tp, LANES), lambda i: (0, fcol)),
                  pl.BlockSpec((1, LANES), lambda i: (0, 0))],
        out_specs=[pl.BlockSpec((tp, LANES), lambda i: (0, 0)),
                   pl.BlockSpec((SUBLANES, tp), lambda i: (0, 0))],
        out_shape=[jax.ShapeDtypeStruct((tp, LANES), F32),
                   jax.ShapeDtypeStruct((SUBLANES, tp), F32)],
        compiler_params=_cparams("arbitrary"),
    )(rest, bf_pad)


def _pick_col(blk, head):
    lane = lax.broadcasted_iota(jnp.int32, blk.shape, 1)
    return jnp.sum(jnp.where(lane == head, blk, 0.0), axis=1, keepdims=True)


def _pick_row(blk, head):
    sub = lax.broadcasted_iota(jnp.int32, blk.shape, 0)
    return jnp.sum(jnp.where(sub == head, blk, 0.0), axis=0, keepdims=True)


def _att_tiles(tp):
    out, r0 = [], 0
    while r0 < tp:
        rows = min(ATT_TQ, tp - r0)
        out.append((r0, rows, r0 + rows))
        r0 += rows
    return out


def attn_fwd(qkv, c, ct, nh):
    tp = qkv.shape[0]
    npair = nh // 2
    tiles = _att_tiles(tp)

    def body(q_ref, k_ref, v_ref, c_ref, ct_ref, o_ref, lset_ref):
        p = pl.program_id(0)
        lset_ref[...] = jnp.zeros_like(lset_ref)
        for r0, nr, nk in tiles:
            rs = slice(r0, r0 + nr)
            causal = (r0 + lax.broadcasted_iota(jnp.int32, (nr, nk), 0)
                      >= lax.broadcasted_iota(jnp.int32, (nr, nk), 1))
            cblk = c_ref[rs, :]
            ctb = ct_ref[:, 0:nk]
            for hh in range(2):
                head = 2 * p + hh
                hs = slice(hh * HEAD_DIM, (hh + 1) * HEAD_DIM)
                q = q_ref[rs, hs] * ATT_SCALE
                s = _dot_nt(q, k_ref[0:nk, hs]) + (_pick_col(cblk, head) - _pick_row(ctb, head))
                s = jnp.where(causal, s, NEG_BIG)
                m = jnp.max(s, axis=1, keepdims=True)
                pm = jnp.exp(s - m)
                l = jnp.sum(pm, axis=1, keepdims=True)
                o_ref[rs, hs] = _dot(pm.astype(BF16), v_ref[0:nk, hs]) / l
                lse = m + jnp.log(l)
                lset_ref[hh:hh + 1, rs] = jnp.broadcast_to(lse, (nr, LANES)).T[0:1, :]

    pair = lambda p: (0, p)
    return pl.pallas_call(
        body, name="attn_fwd", grid=(npair,),
        in_specs=[pl.BlockSpec((tp, LANES), pair),
                  pl.BlockSpec((tp, LANES), lambda p: (0, npair + p)),
                  pl.BlockSpec((tp, LANES), lambda p: (0, 2 * npair + p)),
                  pl.BlockSpec((tp, LANES), lambda p: (0, 0)),
                  pl.BlockSpec((SUBLANES, tp), lambda p: (0, 0))],
        out_specs=[pl.BlockSpec((tp, LANES), pair),
                   pl.BlockSpec((None, SUBLANES, tp), lambda p: (p, 0, 0))],
        out_shape=[jax.ShapeDtypeStruct((tp, nh * HEAD_DIM), F32),
                   jax.ShapeDtypeStruct((npair, SUBLANES, tp), F32)],
        compiler_params=_cparams("parallel"),
    )(qkv, qkv, qkv, c, ct)


def _shift_down(x, k, n):
    if k == 0:
        return x
    rows = lax.broadcasted_iota(jnp.int32, x.shape, 0)
    return jnp.where(rows >= k, pltpu.roll(x, k, 0), 0.0)


def _shift_up(x, k, n):
    if k == 0:
        return x
    rows = lax.broadcasted_iota(jnp.int32, x.shape, 0)
    return jnp.where(rows < n - k, pltpu.roll(x, n - k, 0), 0.0)


def _conv_fwd(xr, cw_ref, cb_ref, n):
    xc = cw_ref[CONV_WIDTH - 1:CONV_WIDTH, :] * xr + cb_ref[...]
    for k in range(1, CONV_WIDTH):
        xc = xc + cw_ref[CONV_WIDTH - 1 - k:CONV_WIDTH - k, :] * _shift_down(xr, k, n)
    return xc


def _gates(xc, wga_ref, bga_ref, wgx_ref, bgx_ref, l_ref):
    xcb = xc.astype(BF16)
    r = _sigmoid(_dot(xcb, wga_ref[...]) + bga_ref[...])
    ig = _sigmoid(_dot(xcb, wgx_ref[...]) + bgx_ref[...])
    ls = _log_sigmoid(l_ref[...])
    log_a = RG_C * r * ls
    a = jnp.exp(log_a)
    mult = jnp.sqrt(-_expm1(2.0 * log_a))
    return xcb, r, ig, ls, log_a, a, mult


SCAN_UNROLL = 4


def _scan_rows(a_s, u_s, out_ref, n, reverse):
    nt = n // SUBLANES
    per = SCAN_UNROLL if nt % SCAN_UNROLL == 0 else 1
    row = lax.broadcasted_iota(jnp.int32, (SUBLANES, LANES), 0)
    last = 0 if reverse else SUBLANES - 1

    def tile_scan(a, u):
        for d in (1, 2, 4):
            if reverse:
                keep = row < SUBLANES - d
                sh = SUBLANES - d
            else:
                keep = row >= d
                sh = d
            a_sh = jnp.where(keep, pltpu.roll(a, sh, 0), 1.0)
            u_sh = jnp.where(keep, pltpu.roll(u, sh, 0), 0.0)
            u = a * u_sh + u
            a = a * a_sh
        return a, u

    def step(t, carry):
        tiles = []
        for k in range(per):
            tt = t * per + k
            if reverse:
                tt = nt - 1 - tt
            off = pl.multiple_of(tt * SUBLANES, SUBLANES)
            a, u = tile_scan(a_s[pl.ds(off, SUBLANES), :], u_s[pl.ds(off, SUBLANES), :])
            tiles.append((off, a, u))
        for off, a, u in tiles:
            out_ref[pl.ds(off, SUBLANES), :] = u + a * carry
            carry = u[last:last + 1, :] + a[last:last + 1, :] * carry
        return carry

    lax.fori_loop(0, nt // per, step, jnp.zeros((1, LANES), F32))


def rec_fwd(rest, convw, convb, wga, bga, wgx, bgx, lru, rw):
    tp = rest.shape[0]
    ng = rw // LANES

    def body(xr_ref, yr_ref, cw_ref, cb_ref, wga_ref, bga_ref, wgx_ref, bgx_ref, l_ref,
             rec_ref, hr_ref, xc_ref, a_s, u_s):
        xc = _conv_fwd(xr_ref[...], cw_ref, cb_ref, tp)
        xc_ref[...] = xc
        _, r, ig, ls, log_a, a, mult = _gates(xc, wga_ref, bga_ref, wgx_ref, bgx_ref, l_ref)
        a_s[...] = a
        u_s[...] = mult * ig * xc
        _scan_rows(a_s, u_s, hr_ref, tp, reverse=False)
        rec_ref[...] = hr_ref[...] * _gelu(yr_ref[...])

    col = lambda g: (0, g)
    vec = pl.BlockSpec((1, LANES), col)
    big = pl.BlockSpec((tp, LANES), col)
    return pl.pallas_call(
        body, name="rec_fwd", grid=(ng,),
        in_specs=[big, pl.BlockSpec((tp, LANES), lambda g: (0, ng + g)),
                  pl.BlockSpec((CONV_WIDTH, LANES), col), vec,
                  pl.BlockSpec((None, LANES, LANES), lambda g: (g, 0, 0)), vec,
                  pl.BlockSpec((None, LANES, LANES), lambda g: (g, 0, 0)), vec, vec],
        out_specs=[big, big, big],
        out_shape=[jax.ShapeDtypeStruct((tp, rw), F32)] * 3,
        scratch_shapes=[pltpu.VMEM((tp, LANES), F32), pltpu.VMEM((tp, LANES), F32)],
        compiler_params=_cparams("parallel"),
    )(rest, rest, convw, convb, wga, bga, wgx, bgx, lru)


def out_proj(h, o, rec, ga, gr, wout, g2, tm):
    tp, d = h.shape
    aw, rw = o.shape[1], rec.shape[1]

    def body(h_ref, o_ref, rec_ref, ga_ref, gr_ref, w_ref, g2_ref, h2_ref, mix_ref, z2_ref):
        mix_ref[:, 0:aw] = _rms_fwd(o_ref[...], ga_ref[...]).astype(BF16)
        mix_ref[:, aw:aw + rw] = _rms_fwd(rec_ref[...], gr_ref[...]).astype(BF16)
        h2 = h_ref[...] + _dot(mix_ref[...], w_ref[...])
        h2_ref[...] = h2
        z2_ref[...] = _rms_fwd(h2, g2_ref[...]).astype(BF16)

    row = lambda i: (i, 0)
    fix = lambda i: (0, 0)
    return pl.pallas_call(
        body, name="out_proj", grid=(tp // tm,),
        in_specs=[pl.BlockSpec((tm, d), row), pl.BlockSpec((tm, aw), row), pl.BlockSpec((tm, rw), row),
                  pl.BlockSpec((1, aw), fix), pl.BlockSpec((1, rw), fix),
                  pl.BlockSpec((d, d), fix), pl.BlockSpec((1, d), fix)],
        out_specs=[pl.BlockSpec((tm, d), row)] * 3,
        out_shape=[jax.ShapeDtypeStruct((tp, d), F32), jax.ShapeDtypeStruct((tp, d), BF16),
                   jax.ShapeDtypeStruct((tp, d), BF16)],
        compiler_params=_cparams("parallel"),
    )(h, o, rec, ga, gr, wout, g2)


MLP_BLOCKS = 2


def mlp_fwd(z2, h2, gup, gdown, tm):
    tp, d = h2.shape
    nf = gup.shape[0]
    tf = gup.shape[2]
    nb = MLP_BLOCKS if nf % MLP_BLOCKS == 0 else 1
    nj = nf // nb

    def body(z_ref, h_ref, wu_ref, wd_ref, u_ref, h3_ref, acc):
        j = pl.program_id(1)
        z = z_ref[...]
        part = None
        for b in range(nb):
            u = jnp.maximum(_dot(z, wu_ref[b]), 0.0)
            u_ref[:, b * tf:(b + 1) * tf] = u.astype(BF16)
            p = _dot((u * u).astype(BF16), wd_ref[b])
            part = p if part is None else part + p

        @pl.when(j == 0)
        def _():
            acc[...] = h_ref[...] + part

        @pl.when(j > 0)
        def _():
            acc[...] += part

        @pl.when(j == nj - 1)
        def _():
            h3_ref[...] = acc[...]

    return pl.pallas_call(
        body, name="mlp_fwd", grid=(tp // tm, nj),
        in_specs=[pl.BlockSpec((tm, d), lambda i, j: (i, 0)),
                  pl.BlockSpec((tm, d), lambda i, j: (i, 0)),
                  pl.BlockSpec((nb, d, tf), lambda i, j: (j, 0, 0)),
                  pl.BlockSpec((nb, tf, d), lambda i, j: (j, 0, 0))],
        out_specs=[pl.BlockSpec((tm, nb * tf), lambda i, j: (i, j)),
                   pl.BlockSpec((tm, d), lambda i, j: (i, 0))],
        out_shape=[jax.ShapeDtypeStruct((tp, nf * tf), BF16), jax.ShapeDtypeStruct((tp, d), F32)],
        scratch_shapes=[pltpu.VMEM((tm, d), F32)],
        compiler_params=_cparams("parallel", "arbitrary"),
    )(z2, h2, gup, gdown)


def loss_head(h, gf, tgt, t_real, tm):
    tp, d = h.shape

    def body(h_ref, g_ref, t_ref, dh_ref, dg_ref, loss_ref):
        i = pl.program_id(0)
        x = h_ref[...]
        g = g_ref[...]
        r = lax.rsqrt(jnp.mean(x * x, axis=-1, keepdims=True) + NORM_EPS)
        xn = x * r
        rows = i * tm + lax.broadcasted_iota(jnp.int32, (tm, 1), 0)
        valid = jnp.logical_and(rows >= N_META, rows < t_real)
        e = jnp.where(valid, xn * g - t_ref[...], 0.0)
        part = 0.5 * jnp.sum(jnp.sum(e * e, axis=1, keepdims=True) / d, axis=0, keepdims=True)
        dy = e / d
        dxn = dy * g
        dh_ref[...] = r * (dxn - xn * jnp.mean(dxn * xn, axis=-1, keepdims=True))
        _accumulate(dg_ref, jnp.sum(dy * xn, axis=0, keepdims=True), i == 0)
        _accumulate(loss_ref, jnp.broadcast_to(part, (1, LANES)), i == 0)

    row = lambda i: (i, 0)
    fix = lambda i: (0, 0)
    return pl.pallas_call(
        body, name="loss_head", grid=(tp // tm,),
        in_specs=[pl.BlockSpec((tm, d), row), pl.BlockSpec((1, d), fix), pl.BlockSpec((tm, d), row)],
        out_specs=[pl.BlockSpec((tm, d), row), pl.BlockSpec((1, d), fix), pl.BlockSpec((1, LANES), fix)],
        out_shape=[jax.ShapeDtypeStruct((tp, d), F32), jax.ShapeDtypeStruct((1, d), F32),
                   jax.ShapeDtypeStruct((1, LANES), F32)],
        compiler_params=_cparams("arbitrary"),
    )(h, gf, tgt)


def mlp_bwd(dh, u, h2, g2, gup, gdown, tm):
    tp, d = dh.shape
    nf = gup.shape[0]
    tf = gup.shape[2]
    nb = MLP_BLOCKS if nf % MLP_BLOCKS == 0 else 1
    nj = nf // nb
    ni = tp // tm

    def body(dh_ref, u_ref, h2_ref, g_ref, wu_ref, wd_ref, dup_ref, dh2_ref, dg_ref, dhb, acc):
        i = pl.program_id(0)
        j = pl.program_id(1)

        @pl.when(j == 0)
        def _():
            dhb[...] = dh_ref[...].astype(BF16)

        part = None
        for b in range(nb):
            cols = slice(b * tf, (b + 1) * tf)
            dup = (_dot_nt(dhb[...], wd_ref[b]) * (2.0 * u_ref[:, cols].astype(F32))).astype(BF16)
            dup_ref[:, cols] = dup
            p = _dot_nt(dup, wu_ref[b])
            part = p if part is None else part + p
        _accumulate(acc, part, j == 0)

        @pl.when(j == nj - 1)
        def _():
            dx, dg = _rms_bwd(h2_ref[...], g_ref[...], acc[...])
            dh2_ref[...] = dh_ref[...] + dx
            _accumulate(dg_ref, dg, i == 0)

    return pl.pallas_call(
        body, name="mlp_bwd", grid=(ni, nj),
        in_specs=[pl.BlockSpec((tm, d), lambda i, j: (i, 0)),
                  pl.BlockSpec((tm, nb * tf), lambda i, j: (i, j)),
                  pl.BlockSpec((tm, d), lambda i, j: (i, 0)),
                  pl.BlockSpec((1, d), lambda i, j: (0, 0)),
                  pl.BlockSpec((nb, d, tf), lambda i, j: (j, 0, 0)),
                  pl.BlockSpec((nb, tf, d), lambda i, j: (j, 0, 0))],
        out_specs=[pl.BlockSpec((tm, nb * tf), lambda i, j: (i, j)),
                   pl.BlockSpec((tm, d), lambda i, j: (i, 0)),
                   pl.BlockSpec((1, d), lambda i, j: (0, 0)),
                   pl.BlockSpec((tm, d), lambda i, j: (i, 0))],
        out_shape=[jax.ShapeDtypeStruct((tp, nf * tf), BF16), jax.ShapeDtypeStruct((tp, d), F32),
                   jax.ShapeDtypeStruct((1, d), F32), jax.ShapeDtypeStruct((tp, d), BF16)],
        scratch_shapes=[pltpu.VMEM((tm, d), F32)],
        compiler_params=_cparams("arbitrary", "arbitrary"),
    )(dh, u, h2, g2, gup, gdown)


def mm_tn(a, b, *, tk, tn, out_dtype, name, square_a=False, blocked_n=False):
    rows, kk = a.shape
    nn = b.shape[1]

    def body(a_ref, b_ref, o_ref):
        av = a_ref[...]
        if square_a:
            af = av.astype(F32)
            av = (af * af).astype(BF16)
        o_ref[...] = _dot_tn(av.astype(BF16), b_ref[...].astype(BF16)).astype(out_dtype)

    if blocked_n:
        out_spec = pl.BlockSpec((None, tk, tn), lambda k, n: (n, k, 0))
        out_shape = jax.ShapeDtypeStruct((nn // tn, kk, tn), out_dtype)
    else:
        out_spec = pl.BlockSpec((tk, tn), lambda k, n: (k, n))
        out_shape = jax.ShapeDtypeStruct((kk, nn), out_dtype)
    return pl.pallas_call(
        body, name=name, grid=(kk // tk, nn // tn),
        in_specs=[pl.BlockSpec((rows, tk), lambda k, n: (0, k)),
                  pl.BlockSpec((rows, tn), lambda k, n: (0, n))],
        out_specs=out_spec, out_shape=out_shape,
        compiler_params=_cparams("parallel", "parallel"),
    )(a, b)


def out_proj_bwd(dh2, o, rec, ga, gr, wout, tm):
    tp, d = dh2.shape
    aw, rw = o.shape[1], rec.shape[1]

    def body(dh_ref, o_ref, rec_ref, ga_ref, gr_ref, w_ref, do_ref, drec_ref, dga_ref, dgr_ref):
        i = pl.program_id(0)
        dmix = _dot_nt(dh_ref[...].astype(BF16), w_ref[...])
        do, dga = _rms_bwd(o_ref[...], ga_ref[...], dmix[:, 0:aw])
        drec, dgr = _rms_bwd(rec_ref[...], gr_ref[...], dmix[:, aw:aw + rw])
        do_ref[...] = do
        drec_ref[...] = drec
        _accumulate(dga_ref, dga, i == 0)
        _accumulate(dgr_ref, dgr, i == 0)

    row = lambda i: (i, 0)
    fix = lambda i: (0, 0)
    return pl.pallas_call(
        body, name="out_proj_bwd", grid=(tp // tm,),
        in_specs=[pl.BlockSpec((tm, d), row), pl.BlockSpec((tm, aw), row), pl.BlockSpec((tm, rw), row),
                  pl.BlockSpec((1, aw), fix), pl.BlockSpec((1, rw), fix), pl.BlockSpec((d, d), fix)],
        out_specs=[pl.BlockSpec((tm, aw), row), pl.BlockSpec((tm, rw), row),
                   pl.BlockSpec((1, aw), fix), pl.BlockSpec((1, rw), fix)],
        out_shape=[jax.ShapeDtypeStruct((tp, aw), F32), jax.ShapeDtypeStruct((tp, rw), F32),
                   jax.ShapeDtypeStruct((1, aw), F32), jax.ShapeDtypeStruct((1, rw), F32)],
        compiler_params=_cparams("arbitrary"),
    )(dh2, o, rec, ga, gr, wout)


def rec_bwd(drec, hr, xc, rest, convw, convb, wga, bga, wgx, bgx, lru, rw):
    tp = rest.shape[0]
    ng = rw // LANES

    def body(drec_ref, hr_ref, xc_ref, xr_ref, yr_ref, cw_ref, cb_ref, wga_ref, bga_ref, wgx_ref, bgx_ref, l_ref,
             dxr_ref, dyr_ref, dwga_ref, dwgx_ref, vec_ref, a_s, u_s, lam_s):
        xc = xc_ref[...]
        h = hr_ref[...]
        yr = yr_ref[...]
        drec = drec_ref[...]
        xcb, r, ig, ls, log_a, a, mult = _gates(xc, wga_ref, bga_ref, wgx_ref, bgx_ref, l_ref)
        dyr_ref[...] = (drec * h * _gelu_grad(yr)).astype(BF16)
        a_s[...] = _shift_up(a, 1, tp)
        u_s[...] = drec * _gelu(yr)
        _scan_rows(a_s, u_s, lam_s, tp, reverse=True)
        lam = lam_s[...]
        da = lam * _shift_down(h, 1, tp)
        dmult = lam * ig * xc
        dig = lam * mult * xc
        dxc = lam * mult * ig
        a2 = jnp.exp(2.0 * log_a)
        dlog_a = da * a - dmult * a2 / mult
        dr = dlog_a * (RG_C * ls)
        dl = jnp.sum(dlog_a * (RG_C * r), axis=0, keepdims=True) * _sigmoid(-l_ref[...])
        dpa = dr * r * (1.0 - r)
        dpx = dig * ig * (1.0 - ig)
        dpab = dpa.astype(BF16)
        dpxb = dpx.astype(BF16)
        dxc = dxc + _dot_nt(dpab, wga_ref[...]) + _dot_nt(dpxb, wgx_ref[...])
        dwga_ref[...] = _dot_tn(xcb, dpab)
        dwgx_ref[...] = _dot_tn(xcb, dpxb)
        xr = xr_ref[...]
        dxr = cw_ref[CONV_WIDTH - 1:CONV_WIDTH, :] * dxc
        for k in range(1, CONV_WIDTH):
            dxr = dxr + cw_ref[CONV_WIDTH - 1 - k:CONV_WIDTH - k, :] * _shift_up(dxc, k, tp)
        dxr_ref[...] = dxr.astype(BF16)
        for k in range(CONV_WIDTH):
            vec_ref[k:k + 1, :] = jnp.sum(dxc * _shift_down(xr, CONV_WIDTH - 1 - k, tp), axis=0, keepdims=True)
        vec_ref[4:5, :] = jnp.sum(dxc, axis=0, keepdims=True)
        vec_ref[5:6, :] = jnp.sum(dpa, axis=0, keepdims=True)
        vec_ref[6:7, :] = jnp.sum(dpx, axis=0, keepdims=True)
        vec_ref[7:8, :] = dl

    col = lambda g: (0, g)
    vec = pl.BlockSpec((1, LANES), col)
    big = pl.BlockSpec((tp, LANES), col)
    sq = pl.BlockSpec((None, LANES, LANES), lambda g: (g, 0, 0))
    return pl.pallas_call(
        body, name="rec_bwd", grid=(ng,),
        in_specs=[big, big, big, big, pl.BlockSpec((tp, LANES), lambda g: (0, ng + g)),
                  pl.BlockSpec((CONV_WIDTH, LANES), col), vec, sq, vec, sq, vec, vec],
        out_specs=[big, big, sq, sq, pl.BlockSpec((None, SUBLANES, LANES), lambda g: (g, 0, 0))],
        out_shape=[jax.ShapeDtypeStruct((tp, rw), BF16), jax.ShapeDtypeStruct((tp, rw), BF16),
                   jax.ShapeDtypeStruct((ng, LANES, LANES), F32), jax.ShapeDtypeStruct((ng, LANES, LANES), F32),
                   jax.ShapeDtypeStruct((ng, SUBLANES, LANES), F32)],
        scratch_shapes=[pltpu.VMEM((tp, LANES), F32)] * 3,
        compiler_params=_cparams("parallel"),
    )(drec, hr, xc, rest, rest, convw, convb, wga, bga, wgx, bgx, lru)


def attn_bwd(qkv, do, o, lset, c, ct, nh):
    tp = qkv.shape[0]
    npair = nh // 2
    aw = nh * HEAD_DIM
    tiles = _att_tiles(tp)

    def body(q_ref, k_ref, v_ref, do_ref, o_ref, lset_ref, c_ref, ct_ref,
             dq_ref, dk_ref, dv_ref, drow_ref, dcol_ref, dk_acc, dv_acc):
        p = pl.program_id(0)
        dk_acc[...] = jnp.zeros_like(dk_acc)
        dv_acc[...] = jnp.zeros_like(dv_acc)
        dcol_ref[...] = jnp.zeros_like(dcol_ref)
        drow_ref[...] = jnp.zeros_like(drow_ref)
        for r0, nr, nk in tiles:
            rs = slice(r0, r0 + nr)
            causal = (r0 + lax.broadcasted_iota(jnp.int32, (nk, nr), 1)
                      >= lax.broadcasted_iota(jnp.int32, (nk, nr), 0))
            cblk = c_ref[0:nk, :]
            ctb = ct_ref[:, rs]
            for hh in range(2):
                head = 2 * p + hh
                hs = slice(hh * HEAD_DIM, (hh + 1) * HEAD_DIM)
                q = q_ref[rs, hs]
                k = k_ref[0:nk, hs]
                dof = do_ref[rs, hs]
                do16 = dof.astype(BF16)
                delta = jnp.sum(dof * o_ref[rs, hs], axis=1, keepdims=True)
                delta_row = jnp.broadcast_to(delta, (nr, LANES)).T[0:1, :]
                s_t = _dot_nt(k, q * ATT_SCALE) + (_pick_row(ctb, head) - _pick_col(cblk, head))
                p_t = jnp.where(causal, jnp.exp(s_t - lset_ref[hh:hh + 1, rs]), 0.0)
                ds_t = p_t * (_dot_nt(v_ref[0:nk, hs], do16) - delta_row)
                p16 = p_t.astype(BF16)
                ds16 = ds_t.astype(BF16)
                dv_acc[0:nk, hs] += _dot(p16, do16)
                dk_acc[0:nk, hs] += _dot(ds16, q) * ATT_SCALE
                dq_ref[rs, hs] = (_dot_tn(ds16, k) * ATT_SCALE).astype(BF16)
                drow_ref[hh:hh + 1, rs] = jnp.sum(ds_t, axis=0, keepdims=True)
                dcol_ref[0:nk, hs] -= jnp.broadcast_to(jnp.sum(ds_t, axis=1, keepdims=True), (nk, HEAD_DIM))
        dk_ref[...] = dk_acc[...].astype(BF16)
        dv_ref[...] = dv_acc[...].astype(BF16)

    pair = lambda p: (0, p)
    return pl.pallas_call(
        body, name="attn_bwd", grid=(npair,),
        in_specs=[pl.BlockSpec((tp, LANES), pair),
                  pl.BlockSpec((tp, LANES), lambda p: (0, npair + p)),
                  pl.BlockSpec((tp, LANES), lambda p: (0, 2 * npair + p)),
                  pl.BlockSpec((tp, LANES), pair),
                  pl.BlockSpec((tp, LANES), pair),
                  pl.BlockSpec((None, SUBLANES, tp), lambda p: (p, 0, 0)),
                  pl.BlockSpec((tp, LANES), lambda p: (0, 0)),
                  pl.BlockSpec((SUBLANES, tp), lambda p: (0, 0))],
        out_specs=[pl.BlockSpec((tp, LANES), pair), pl.BlockSpec((tp, LANES), pair),
                   pl.BlockSpec((tp, LANES), pair),
                   pl.BlockSpec((None, SUBLANES, tp), lambda p: (p, 0, 0)),
                   pl.BlockSpec((tp, LANES), pair)],
        out_shape=[jax.ShapeDtypeStruct((tp, aw), BF16), jax.ShapeDtypeStruct((tp, aw), BF16),
                   jax.ShapeDtypeStruct((tp, aw), BF16),
                   jax.ShapeDtypeStruct((npair, SUBLANES, tp), F32),
                   jax.ShapeDtypeStruct((tp, aw), F32)],
        scratch_shapes=[pltpu.VMEM((tp, LANES), F32), pltpu.VMEM((tp, LANES), F32)],
        compiler_params=_cparams("parallel"),
    )(qkv, qkv, qkv, do, o, lset, c, ct)


def fgate_bwd(dct8, drs, rest, bf_pad, fcol):
    tp = rest.shape[0]
    aw = drs.shape[1]
    nb = tp // ATT_BLOCK
    B = ATT_BLOCK

    def body(d_ref, drs_ref, f_ref, b_ref, dfl_ref, db_ref, pad_s):
        r_i = lax.broadcasted_iota(jnp.int32, (B, B), 0)
        c_i = lax.broadcasted_iota(jnp.int32, (B, B), 1)
        triu = (c_i >= r_i).astype(BF16)
        sel = (lax.broadcasted_iota(jnp.int32, (aw, LANES), 0)
               == HEAD_DIM * lax.broadcasted_iota(jnp.int32, (aw, LANES), 1)).astype(BF16)
        carry = jnp.zeros((1, LANES), F32)
        db = jnp.zeros((1, LANES), F32)
        pad_s[...] = jnp.zeros_like(pad_s)
        for i in range(nb - 1, -1, -1):
            sl = slice(i * B, (i + 1) * B)
            pad_s[0:SUBLANES, :] = d_ref[:, sl]
            dc = pad_s[...].T + _dot_split3(drs_ref[sl, :], sel)
            rc = _split3_dot(triu, dc)
            dlf = rc + carry
            carry = carry + rc[0:1, :]
            dfl = dlf * _sigmoid(-(f_ref[sl, :] + b_ref[...]))
            dfl_ref[sl, :] = dfl.astype(BF16)
            db = db + jnp.sum(dfl, axis=0, keepdims=True)
        db_ref[...] = db

    return pl.pallas_call(
        body, name="fgate_bwd", grid=(1,),
        in_specs=[pl.BlockSpec((SUBLANES, tp), lambda i: (0, 0)),
                  pl.BlockSpec((tp, aw), lambda i: (0, 0)),
                  pl.BlockSpec((tp, LANES), lambda i: (0, fcol)),
                  pl.BlockSpec((1, LANES), lambda i: (0, 0))],
        out_specs=[pl.BlockSpec((tp, LANES), lambda i: (0, 0)),
                   pl.BlockSpec((1, LANES), lambda i: (0, 0))],
        out_shape=[jax.ShapeDtypeStruct((tp, LANES), BF16), jax.ShapeDtypeStruct((1, LANES), F32)],
        scratch_shapes=[pltpu.VMEM((B, B), F32)],
        compiler_params=_cparams("arbitrary"),
    )(dct8, drs, rest, bf_pad)


def in_proj_bwd(dh2, parts, w_in_t, wrest_t, h, g1, tm):
    tp, d = h.shape
    dq, dk, dv, dxr, dyr, dfl = parts
    aw, rw = dq.shape[1], dxr.shape[1]

    def body(dh2_ref, dq_ref, dk_ref, dv_ref, dxr_ref, dyr_ref, dfl_ref, wq_ref, wr_ref, h_ref, g_ref,
             dh_ref, dg_ref):
        i = pl.program_id(0)
        dz = _dot(dq_ref[...], wq_ref[0:aw, :])
        dz += _dot(dk_ref[...], wq_ref[aw:2 * aw, :])
        dz += _dot(dv_ref[...], wq_ref[2 * aw:3 * aw, :])
        dz += _dot(dxr_ref[...], wr_ref[0:rw, :])
        dz += _dot(dyr_ref[...], wr_ref[rw:2 * rw, :])
        dz += _dot(dfl_ref[...], wr_ref[2 * rw:2 * rw + LANES, :])
        dx, dg = _rms_bwd(h_ref[...], g_ref[...], dz)
        dh_ref[...] = dh2_ref[...] + dx
        _accumulate(dg_ref, dg, i == 0)

    row = lambda i: (i, 0)
    fix = lambda i: (0, 0)
    return pl.pallas_call(
        body, name="in_proj_bwd", grid=(tp // tm,),
        in_specs=[pl.BlockSpec((tm, d), row),
                  pl.BlockSpec((tm, aw), row), pl.BlockSpec((tm, aw), row), pl.BlockSpec((tm, aw), row),
                  pl.BlockSpec((tm, rw), row), pl.BlockSpec((tm, rw), row), pl.BlockSpec((tm, LANES), row),
                  pl.BlockSpec((3 * aw, d), fix), pl.BlockSpec(wrest_t.shape, fix),
                  pl.BlockSpec((tm, d), row), pl.BlockSpec((1, d), fix)],
        out_specs=[pl.BlockSpec((tm, d), row), pl.BlockSpec((1, d), fix)],
        out_shape=[jax.ShapeDtypeStruct((tp, d), F32), jax.ShapeDtypeStruct((1, d), F32)],
        compiler_params=_cparams("arbitrary"),
    )(dh2, dq, dk, dv, dxr, dyr, dfl, w_in_t, wrest_t, h, g1)


def dw_in_t(z, parts, nh, tr):
    tp, d = z.shape
    dq, dk, dv, dxr, dyr, dfl = parts
    aw, rw = dq.shape[1], dxr.shape[1]
    d_in = 3 * aw + nh + 2 * rw
    nr = tp // tr
    offs = [(0, aw), (aw, aw), (2 * aw, aw), (3 * aw + nh, rw), (3 * aw + nh + rw, rw)]

    def body(z_ref, dq_ref, dk_ref, dv_ref, dxr_ref, dyr_ref, dfl_ref, o_ref, acc):
        r = pl.program_id(0)

        @pl.when(r == 0)
        def _():
            acc[...] = jnp.zeros_like(acc)

        zt = z_ref[...]
        for (o, n), ref in zip(offs, (dq_ref, dk_ref, dv_ref, dxr_ref, dyr_ref)):
            acc[o:o + n, :] += _dot_tn(ref[...], zt)
        acc[3 * aw:3 * aw + nh, :] += _dot_tn(dfl_ref[...], zt)[0:nh, :]

        @pl.when(r == nr - 1)
        def _():
            o_ref[...] = acc[...].astype(BF16)

    row = lambda r: (r, 0)
    return pl.pallas_call(
        body, name="dw_in", grid=(nr,),
        in_specs=[pl.BlockSpec((tr, d), row),
                  pl.BlockSpec((tr, aw), row), pl.BlockSpec((tr, aw), row), pl.BlockSpec((tr, aw), row),
                  pl.BlockSpec((tr, rw), row), pl.BlockSpec((tr, rw), row), pl.BlockSpec((tr, LANES), row)],
        out_specs=pl.BlockSpec((d_in, d), lambda r: (0, 0)),
        out_shape=jax.ShapeDtypeStruct((d_in, d), BF16),
        scratch_shapes=[pltpu.VMEM((d_in, d), F32)],
        compiler_params=_cparams("arbitrary"),
    )(z, dq, dk, dv, dxr, dyr, dfl)


def _place():
    return lax.axis_index("x"), lax.axis_index("y"), lax.axis_index("c")


HBM = pl.BlockSpec(memory_space=pltpu.HBM)
SEM = pl.BlockSpec(memory_space=pltpu.SEMAPHORE)
EFFECT = pltpu.SideEffectType.DATAFLOW_SIDE_EFFECTING


def _in_hbm(a):
    return pltpu.with_memory_space_constraint(a, pltpu.HBM)


def _gather_targets(x, y, c):
    return [(x, y, 1 - c), (1 - x, y, c), (x, 1 - y, c), (1 - x, 1 - y, c)]


def _slot(t):
    return 4 * t[0] + 2 * t[1] + t[2]


def gather_start(groups):
    flat = [a for g in groups for a in g]
    n = len(flat)
    ng = len(groups)
    lands = [lax.empty((N_DEV,) + a.shape, a.dtype) for a in flat]

    def body(*refs):
        src, land = refs[:n], refs[n:2 * n]
        sems = refs[2 * n:2 * n + 2 * ng]
        token = refs[-1]
        x, y, c = _place()
        me = 4 * x + 2 * y + c
        i = 0
        for gi, g in enumerate(groups):
            for a in range(len(g)):
                for k, t in enumerate(_gather_targets(x, y, c)):
                    pltpu.make_async_remote_copy(
                        src_ref=src[i], dst_ref=land[i].at[me],
                        send_sem=sems[2 * gi].at[4 * a + k], recv_sem=sems[2 * gi + 1].at[4 * a + k],
                        device_id=t, device_id_type=MESH).start()
                i += 1
        token[...] = jnp.zeros_like(token)

    sem_shapes = []
    for g in groups:
        sem_shapes += [pltpu.SemaphoreType.DMA((4 * len(g),)), pltpu.SemaphoreType.DMA((4 * len(g),))]
    out = pl.pallas_call(
        body, name="gather_start",
        out_shape=sem_shapes + [pltpu.HBM(a.shape, a.dtype) for a in flat + lands]
        + [jax.ShapeDtypeStruct((SUBLANES, LANES), F32)],
        in_specs=[HBM] * (2 * n),
        out_specs=[SEM] * (2 * ng) + [HBM] * (2 * n) + [pl.BlockSpec(memory_space=pltpu.VMEM)],
        input_output_aliases={i: 2 * ng + i for i in range(2 * n)},
        compiler_params=pltpu.CompilerParams(has_side_effects=EFFECT),
    )(*[_in_hbm(a) for a in flat + lands])
    sems = out[:2 * ng]
    thru = out[2 * ng:2 * ng + 2 * n]
    srcs_t, lands_t = thru[:n], thru[n:]
    res, i = [], 0
    for gi, g in enumerate(groups):
        res.append((sems[2 * gi], sems[2 * gi + 1], srcs_t[i:i + len(g)], lands_t[i:i + len(g)]))
        i += len(g)
    return res, out[-1]


def gather_wait(send, recv, srcs, lands, after, name):
    n = len(srcs)

    def body(*refs):
        src, land = refs[:n], refs[n:2 * n]
        send_sem, recv_sem = refs[2 * n], refs[2 * n + 1]
        x, y, c = _place()
        for a in range(n):
            for k, t in enumerate(_gather_targets(x, y, c)):
                cp = pltpu.make_async_remote_copy(
                    src_ref=src[a], dst_ref=land[a].at[_slot(t)],
                    send_sem=send_sem.at[4 * a + k], recv_sem=recv_sem.at[4 * a + k],
                    device_id=t, device_id_type=MESH)
                cp.wait_send()
                cp.wait_recv()

    out = pl.pallas_call(
        body, name=name,
        out_shape=[pltpu.HBM(a.shape, a.dtype) for a in list(srcs) + list(lands)],
        in_specs=[HBM] * (2 * n) + [SEM, SEM, ANY],
        out_specs=[HBM] * (2 * n),
        input_output_aliases={i: i for i in range(2 * n)},
        compiler_params=pltpu.CompilerParams(has_side_effects=EFFECT),
    )(*srcs, *lands, send, recv, after)
    return out[:n], out[n:]


def forward_start(lands, name):
    n = len(lands)

    def body(*refs):
        land = refs[:n]
        send_sem, recv_sem = refs[n], refs[n + 1]
        token = refs[-1]
        x, y, c = _place()
        for a in range(n):
            for j, chip in enumerate([(1 - x, y), (x, 1 - y), (1 - x, 1 - y)]):
                blk = land[a].at[_slot((*chip, c))]
                pltpu.make_async_remote_copy(src_ref=blk, dst_ref=blk, send_sem=send_sem.at[3 * a + j],
                                             recv_sem=recv_sem.at[3 * a + j], device_id=(x, y, 1 - c),
                                             device_id_type=MESH).start()
        token[...] = jnp.zeros_like(token)

    out = pl.pallas_call(
        body, name=name,
        out_shape=[pltpu.SemaphoreType.DMA((3 * n,)), pltpu.SemaphoreType.DMA((3 * n,))]
        + [pltpu.HBM(a.shape, a.dtype) for a in lands] + [jax.ShapeDtypeStruct((SUBLANES, LANES), F32)],
        in_specs=[HBM] * n,
        out_specs=[SEM, SEM] + [HBM] * n + [pl.BlockSpec(memory_space=pltpu.VMEM)],
        input_output_aliases={i: 2 + i for i in range(n)},
        compiler_params=pltpu.CompilerParams(has_side_effects=EFFECT),
    )(*[_in_hbm(a) for a in lands])
    return out[0], out[1], out[2:2 + n]


def forward_wait(send, recv, lands, after, name):
    n = len(lands)

    def body(*refs):
        land = refs[:n]
        send_sem, recv_sem = refs[n], refs[n + 1]
        x, y, c = _place()
        for a in range(n):
            for j, chip in enumerate([(1 - x, y), (x, 1 - y), (1 - x, 1 - y)]):
                cp = pltpu.make_async_remote_copy(
                    src_ref=land[a].at[_slot((*chip, c))], dst_ref=land[a].at[_slot((*chip, 1 - c))],
                    send_sem=send_sem.at[3 * a + j], recv_sem=recv_sem.at[3 * a + j],
                    device_id=(x, y, 1 - c), device_id_type=MESH)
                cp.wait_send()
                cp.wait_recv()

    return pl.pallas_call(
        body, name=name,
        out_shape=[pltpu.HBM(a.shape, a.dtype) for a in lands],
        in_specs=[HBM] * n + [SEM, SEM, ANY],
        out_specs=[HBM] * n,
        input_output_aliases={i: i for i in range(n)},
        compiler_params=pltpu.CompilerParams(has_side_effects=EFFECT),
    )(*lands, send, recv, after)


def _relations():
    return [(dx, dy, dc) for dx in (0, 1) for dy in (0, 1) for dc in (0, 1) if dx + dy + dc]


def _peer(x, y, c, rel):
    return ((1 - x) if rel[0] else x, (1 - y) if rel[1] else y, (1 - c) if rel[2] else c)


def exchange_start(srcs, lands, layer, name):
    n = len(srcs)

    def body(*refs):
        src, land = refs[:n], refs[n:2 * n]
        send_sem, recv_sem = refs[2 * n], refs[2 * n + 1]
        token = refs[-1]
        x, y, c = _place()
        me = 4 * x + 2 * y + c
        for k, rel in enumerate(_relations()):
            peer = _peer(x, y, c, rel)
            for a in range(n):
                pltpu.make_async_remote_copy(
                    src_ref=src[a] if layer is None else src[a].at[_slot(peer)],
                    dst_ref=land[a].at[me] if layer is None else land[a].at[me, layer],
                    send_sem=send_sem.at[7 * a + k], recv_sem=recv_sem.at[7 * a + k],
                    device_id=peer, device_id_type=MESH).start()
        token[...] = jnp.zeros_like(token)

    out = pl.pallas_call(
        body, name=name,
        out_shape=[pltpu.SemaphoreType.DMA((7 * n,)), pltpu.SemaphoreType.DMA((7 * n,))]
        + [pltpu.HBM(a.shape, a.dtype) for a in list(srcs) + list(lands)]
        + [jax.ShapeDtypeStruct((SUBLANES, LANES), F32)],
        in_specs=[HBM] * (2 * n),
        out_specs=[SEM, SEM] + [HBM] * (2 * n) + [pl.BlockSpec(memory_space=pltpu.VMEM)],
        input_output_aliases={i: 2 + i for i in range(2 * n)},
        compiler_params=pltpu.CompilerParams(has_side_effects=EFFECT),
    )(*[_in_hbm(a) for a in list(srcs) + list(lands)])
    return out[0], out[1], out[2:2 + n], out[2 + n:2 + 2 * n], out[-1][0, 0]


def exchange_wait(send, recv, srcs, lands, after, layer, name):
    n = len(srcs)

    def body(*refs):
        src, land = refs[:n], refs[n:2 * n]
        send_sem, recv_sem = refs[2 * n], refs[2 * n + 1]
        x, y, c = _place()
        for k, rel in enumerate(_relations()):
            peer = _peer(x, y, c, rel)
            for a in range(n):
                cp = pltpu.make_async_remote_copy(
                    src_ref=src[a] if layer is None else src[a].at[_slot(peer)],
                    dst_ref=land[a].at[_slot(peer)] if layer is None else land[a].at[_slot(peer), layer],
                    send_sem=send_sem.at[7 * a + k], recv_sem=recv_sem.at[7 * a + k],
                    device_id=peer, device_id_type=MESH)
                cp.wait_send()
                cp.wait_recv()

    out = pl.pallas_call(
        body, name=name,
        out_shape=[pltpu.HBM(a.shape, a.dtype) for a in list(srcs) + list(lands)],
        in_specs=[HBM] * (2 * n) + [SEM, SEM, ANY],
        out_specs=[HBM] * (2 * n),
        input_output_aliases={i: i for i in range(2 * n)},
        compiler_params=pltpu.CompilerParams(has_side_effects=EFFECT),
    )(*srcs, *lands, send, recv, after)
    return out[:n], out[n:]


def _adamw_math(g, w, m, v):
    m = ADAM_B1 * m + (1.0 - ADAM_B1) * g
    v = ADAM_B2 * v + (1.0 - ADAM_B2) * (g * g)
    m_hat = m / (1.0 - ADAM_B1 ** ADAM_STEP)
    v_hat = v / (1.0 - ADAM_B2 ** ADAM_STEP)
    delta = -ADAM_LR * (m_hat / (jnp.sqrt(v_hat) + ADAM_EPS) + ADAM_WD * w)
    return delta, m, v


def sum_adamw(parts, w, m, v, tr, name):
    npart, rows, cols = parts.shape

    def body(p_ref, w_ref, m_ref, v_ref, g_ref, d_ref, nm_ref, nv_ref):
        g = p_ref[0].astype(F32)
        for p in range(1, npart):
            g = g + p_ref[p].astype(F32)
        delta, nm, nv = _adamw_math(g, w_ref[...], m_ref[...], v_ref[...])
        g_ref[...] = g
        d_ref[...] = delta
        nm_ref[...] = nm
        nv_ref[...] = nv

    blk = pl.BlockSpec((tr, cols), lambda i: (i, 0))
    return pl.pallas_call(
        body, name=name, grid=(rows // tr,),
        in_specs=[pl.BlockSpec((npart, tr, cols), lambda i: (0, i, 0)), blk, blk, blk],
        out_specs=[blk] * 4,
        out_shape=[jax.ShapeDtypeStruct((rows, cols), F32)] * 4,
        compiler_params=_cparams("parallel"),
    )(parts, w, m, v)


def sum_adamw_t(parts, w, m, v, name):
    npart, nl, rows, cols = parts.shape

    def body(p_ref, w_ref, m_ref, v_ref, g_ref, d_ref, nm_ref, nv_ref):
        g = p_ref[0].astype(F32)
        for p in range(1, npart):
            g = g + p_ref[p].astype(F32)
        delta, nm, nv = _adamw_math(g, w_ref[...], m_ref[...], v_ref[...])
        g_ref[...] = g
        d_ref[...] = delta
        nm_ref[...] = nm
        nv_ref[...] = nv

    blk = pl.BlockSpec((None, rows, cols), lambda l: (l, 0, 0))
    return pl.pallas_call(
        body, name=name, grid=(nl,),
        in_specs=[pl.BlockSpec((npart, None, rows, cols), lambda l: (0, l, 0, 0)), blk, blk, blk],
        out_specs=[blk] * 4,
        out_shape=[jax.ShapeDtypeStruct((nl, rows, cols), F32)] * 4,
        compiler_params=_cparams("parallel"),
    )(parts, w, m, v)


def sum_parts(parts, name):
    npart, rows, cols = parts.shape

    def body(p_ref, g_ref):
        g = p_ref[0].astype(F32)
        for p in range(1, npart):
            g = g + p_ref[p].astype(F32)
        g_ref[...] = g

    return pl.pallas_call(
        body, name=name, grid=(1,),
        in_specs=[pl.BlockSpec((npart, rows, cols), lambda i: (0, 0, 0))],
        out_specs=pl.BlockSpec((rows, cols), lambda i: (0, 0)),
        out_shape=jax.ShapeDtypeStruct((rows, cols), F32),
        compiler_params=_cparams("arbitrary"),
    )(parts)


def _round_up(n, m):
    return (n + m - 1) // m * m


def _block_diag_pairs(w):
    nb, b, _ = w.shape
    per = LANES // b
    ng = nb // per
    w = w.reshape(ng, per, b, b)
    eye = jnp.eye(per, dtype=w.dtype)
    out = jnp.einsum('gpij,pq->gpiqj', w, eye).reshape(ng, LANES, LANES)
    return out.astype(BF16)


def _block_diag_extract(g, b):
    ng = g.shape[0]
    per = LANES // b
    g = g.reshape(ng, per, b, per, b)
    idx = jnp.arange(per)
    return g[:, idx, :, idx, :].transpose(1, 0, 2, 3).reshape(ng * per, b, b)


def _tiles(v):
    v = v.reshape(-1)
    n = _round_up(v.shape[0], SUBLANES * LANES)
    return jnp.pad(v, (0, n - v.shape[0])).reshape(-1, LANES)


SMALL = ['attn_norm_g', 'b_f', 'conv_w', 'conv_b', 'w_gate_a', 'b_gate_a', 'w_gate_x', 'b_gate_x',
         'lru_L', 'attn_out_g', 'rec_out_g', 'mlp_norm_g', 'final_g', 'meta']


def _pack(d):
    return jnp.concatenate([_tiles(d[n]) for n in SMALL], axis=0)


def _unpack(vec, shapes):
    out, r = {}, 0
    for n in SMALL:
        size = math.prod(shapes[n])
        nr = _round_up(size, SUBLANES * LANES) // LANES
        out[n] = vec[r:r + nr].reshape(-1)[:size].reshape(shapes[n])
        r += nr
    return out


def _row_tile(tp):
    return tp // 4 if (tp // 4) % 16 == 0 else tp


def local_step(x, tgt, meta, small, hooks):
    s, d = x.shape
    t_real = s + N_META
    tp = _round_up(t_real, ATT_BLOCK)
    depth = small['attn_norm_g'].shape[0]
    nh = small['b_f'].shape[1]
    rw = small['conv_b'].shape[1]
    blk = small['w_gate_a'].shape[2]
    tm = _row_tile(tp)
    tm2 = tp // 2
    fcol = 2 * rw // LANES

    h = jnp.concatenate([meta, x, jnp.zeros((tp - t_real, d), F32)], axis=0)
    tgt_p = jnp.pad(tgt, ((N_META, tp - t_real), (0, 0)))
    row = lambda v: v.reshape(1, -1)
    bf_pad = jnp.pad(small['b_f'], ((0, 0), (0, LANES - nh)))

    saved = []
    for l in range(depth):
        w_in_t, wrest_t, wout = hooks.mixer_weights(l, h)
        wga = _block_diag_pairs(small['w_gate_a'][l])
        wgx = _block_diag_pairs(small['w_gate_x'][l])
        z, qkv, rest = in_proj(h, row(small['attn_norm_g'][l]), w_in_t, wrest_t, 3 * nh * HEAD_DIM, tm)
        c, ct = fgate_fwd(rest, bf_pad[l:l + 1], fcol)
        o, lset = attn_fwd(qkv, c, ct, nh)
        rec, hr, xc = rec_fwd(rest, small['conv_w'][l], row(small['conv_b'][l]), wga, row(small['b_gate_a'][l]),
                              wgx, row(small['b_gate_x'][l]), row(small['lru_L'][l]), rw)
        h2, mix, z2 = out_proj(h, o, rec, row(small['attn_out_g'][l]), row(small['rec_out_g'][l]), wout,
                               row(small['mlp_norm_g'][l]), tm)
        gup, gdown = hooks.mlp_weights(l, z2)
        u, h3 = mlp_fwd(z2, h2, gup, gdown, tm2)
        saved.append(dict(h=h, z=z, qkv=qkv, rest=rest, c=c, ct=ct, o=o, lset=lset, rec=rec, hr=hr, xc=xc,
                          h2=h2, mix=mix, z2=z2, u=u, wga=wga, wgx=wgx,
                          w_in_t=w_in_t, wrest_t=wrest_t, wout=wout, gup=gup, gdown=gdown))
        h = h3

    dh, dgf, loss = loss_head(h, row(small['final_g']), tgt_p, t_real, tm)

    gs = {n: [None] * depth for n in SMALL if n not in ('final_g', 'meta')}
    tok = jnp.zeros((), F32)
    for l in reversed(range(depth)):
        sv = saved[l]
        gup, gdown = sv['gup'], sv['gdown']
        tf = gup.shape[2]
        dup, dh2, dg2, dhb = mlp_bwd(dh, sv['u'], sv['h2'], row(small['mlp_norm_g'][l]) + tok, gup, gdown, tm)
        gs['mlp_norm_g'][l] = dg2[0]
        do, drec, dga, dgr = out_proj_bwd(dh2, sv['o'], sv['rec'], row(small['attn_out_g'][l]),
                                          row(small['rec_out_g'][l]), sv['wout'], tm)
        gs['attn_out_g'][l] = dga[0]
        gs['rec_out_g'][l] = dgr[0]
        blocks = dict(
            w_down=mm_tn(sv['u'], dhb, tk=tf, tn=d, out_dtype=BF16, name="dw_down",
                         square_a=True).reshape(N_DEV, tf, d),
            w_up=mm_tn(sv['z2'], dup, tk=d, tn=tf, out_dtype=BF16, name="dw_up", blocked_n=True),
            w_out=mm_tn(sv['mix'], dh2, tk=d, tn=d // 2, out_dtype=BF16,
                        name="dw_out").reshape(N_DEV, d // N_DEV, d))
        tok = hooks.grads_ready(l, 'mlp', blocks)
        dxr, dyr, dwga, dwgx, vec = rec_bwd(drec, sv['hr'], sv['xc'], sv['rest'], small['conv_w'][l],
                                            row(small['conv_b'][l]) + tok, sv['wga'], row(small['b_gate_a'][l]),
                                            sv['wgx'], row(small['b_gate_x'][l]), row(small['lru_L'][l]), rw)
        gs['w_gate_a'][l] = _block_diag_extract(dwga, blk)
        gs['w_gate_x'][l] = _block_diag_extract(dwgx, blk)
        vec = vec.transpose(1, 0, 2).reshape(SUBLANES, rw)
        gs['conv_w'][l] = vec[0:CONV_WIDTH]
        gs['conv_b'][l] = vec[4]
        gs['b_gate_a'][l] = vec[5]
        gs['b_gate_x'][l] = vec[6]
        gs['lru_L'][l] = vec[7]
        dq, dk, dv, drow, dcol = attn_bwd(sv['qkv'], do, sv['o'], sv['lset'], sv['c'], sv['ct'] + tok, nh)
        drow8 = drow[:, 0:2, :].reshape(nh, tp)
        if nh < SUBLANES:
            drow8 = jnp.pad(drow8, ((0, SUBLANES - nh), (0, 0)))
        dfl, dbf = fgate_bwd(drow8, dcol, sv['rest'], bf_pad[l:l + 1], fcol)
        gs['b_f'][l] = dbf[0, 0:nh]
        parts = (dq, dk, dv, dxr, dyr, dfl)
        dh, dg1 = in_proj_bwd(dh2, parts, sv['w_in_t'], sv['wrest_t'], sv['h'], row(small['attn_norm_g'][l]), tm)
        gs['attn_norm_g'][l] = dg1[0]
        tok = jnp.zeros((), F32)
        if l == 0:
            grads = {n: jnp.stack(v) for n, v in gs.items()}
            grads['final_g'] = dgf[0]
            grads['meta'] = dh[0:N_META]
            tok = hooks.small_ready(grads)
        dw_in = dw_in_t(sv['z'], parts, nh, tm2)
        dw_in = dw_in.reshape(N_DEV, dw_in.shape[0] // N_DEV, d) + tok.astype(BF16)
        tok = hooks.grads_ready(l, 'in', dict(w_in=dw_in))

    return loss[0, 0], dh, tok


def prep_weights(g_in, g_out, nh, rw):
    d = g_in.shape[2]
    w_in_t = g_in.reshape(-1, d)
    f0 = 3 * nh * HEAD_DIM
    wrest_t = jnp.concatenate([w_in_t[f0 + nh:f0 + nh + 2 * rw],
                               jnp.pad(w_in_t[f0:f0 + nh], ((0, LANES - nh), (0, 0)))], axis=0)
    return w_in_t, wrest_t, g_out.reshape(d, d)


BIG = ['w_in', 'w_out', 'w_up', 'w_down']
EXCHANGE_GROUPS = {'mlp': ['w_down', 'w_up', 'w_out'], 'in': ['w_in']}
WEIGHTS = ['meta', 'attn_norm_g', 'w_in', 'b_f', 'conv_w', 'conv_b', 'w_gate_a', 'b_gate_a', 'w_gate_x', 'b_gate_x',
           'lru_L', 'attn_out_g', 'rec_out_g', 'w_out', 'mlp_norm_g', 'w_up', 'w_down', 'final_g']


def _set_own(arr, own, me):
    return lax.dynamic_update_slice_in_dim(arr, own[None], me, axis=0)


class _Step:
    def __init__(self, w, nh, rw, me):
        self.w, self.nh, self.rw, self.me = w, nh, rw, me
        depth = w['w_in'].shape[0]
        groups = []
        for l in range(depth):
            groups.append([w['w_in_t'][:, l, :].astype(BF16), w['w_out'][l].astype(BF16)])
            groups.append([w['w_up'][l].astype(BF16), w['w_down'][l].astype(BF16)])
        groups[0] = groups[0] + [w['meta'], w['conv_w']]
        self.pending, _ = gather_start(groups)
        self.gathered = {}
        self.passing = {}
        self.lands = {n: lax.empty((N_DEV,) + w[n].shape, BF16) for n in BIG}
        din8, _, d = w['w_in_t'].shape
        self.lands['w_in'] = lax.empty((N_DEV, depth, din8, d), BF16)
        self.started = []
        self.small = None

    def _pass_on(self, gi, after):
        if gi < len(self.pending) and gi not in self.passing:
            send, recv, srcs, lands = self.pending[gi]
            srcs, lands = gather_wait(send, recv, srcs, lands, after, "gather_wait_%d" % gi)
            fsend, frecv, lands = forward_start(lands, "forward_start_%d" % gi)
            self.passing[gi] = (fsend, frecv, srcs, lands)

    def group(self, gi, after):
        if gi not in self.gathered:
            self._pass_on(gi, after)
            fsend, frecv, srcs, lands = self.passing[gi]
            lands = forward_wait(fsend, frecv, lands, after, "forward_wait_%d" % gi)
            self.gathered[gi] = [_set_own(g, own, self.me) for g, own in zip(lands, srcs)]
            if gi >= 1:
                self._pass_on(gi + 1, lands[0])
        return self.gathered[gi]

    def mixer_weights(self, l, after):
        g = self.group(2 * l, after)
        return prep_weights(g[0], g[1], self.nh, self.rw)

    def mlp_weights(self, l, after):
        g = self.group(2 * l + 1, after)
        return g[0], g[1]

    def grads_ready(self, l, group, blocks):
        names = EXCHANGE_GROUPS[group]
        send, recv, srcs, lands, token = exchange_start(
            [blocks[n] for n in names], [self.lands[n] for n in names], l, "exchange_start_%s_%d" % (group, l))
        for n, a in zip(names, lands):
            self.lands[n] = a
        self.started.append((l, group, send, recv, srcs))
        return token

    def small_ready(self, grads):
        self.small_shapes = {n: grads[n].shape for n in SMALL}
        packed = _pack(grads).astype(BF16)
        send, recv, srcs, lands, token = exchange_start(
            [packed], [lax.empty((N_DEV,) + packed.shape, BF16)], None, "small_start")
        self.small = (send, recv, srcs, lands)
        return token

    def small_sum(self, after):
        send, recv, srcs, lands = self.small
        srcs, lands = exchange_wait(send, recv, srcs, lands, after, None, "small_wait")
        parts = _set_own(lands[0], srcs[0], self.me)
        return _unpack(sum_parts(parts, "sum_small_grads"), self.small_shapes)

    def received(self, group, after):
        names = EXCHANGE_GROUPS[group]
        own = {n: [None] * self.w[n].shape[0] for n in names}
        for l, grp, send, recv, srcs in self.started:
            if grp != group:
                continue
            srcs, lands = exchange_wait(send, recv, srcs, [self.lands[n] for n in names], after, l,
                                        "exchange_wait_%s_%d" % (group, l))
            for n, a, sr in zip(names, lands, srcs):
                self.lands[n] = a
                own[n][l] = lax.dynamic_index_in_dim(sr, self.me, 0, keepdims=False)
        return {n: _set_own(self.lands[n], jnp.stack(own[n]), self.me) for n in names}


def kernel(x, meta, attn_norm_g, w_in, b_f, conv_w, conv_b, w_gate_a, b_gate_a, w_gate_x, b_gate_x, lru_L, attn_out_g, rec_out_g, w_out, mlp_norm_g, w_up, w_down, final_g, loss_target, m_meta, m_attn_norm_g, m_w_in, m_b_f, m_conv_w, m_conv_b, m_w_gate_a, m_b_gate_a, m_w_gate_x, m_b_gate_x, m_lru_L, m_attn_out_g, m_rec_out_g, m_w_out, m_mlp_norm_g, m_w_up, m_w_down, m_final_g, v_meta, v_attn_norm_g, v_w_in, v_b_f, v_conv_w, v_conv_b, v_w_gate_a, v_b_gate_a, v_w_gate_x, v_b_gate_x, v_lru_L, v_attn_out_g, v_rec_out_g, v_w_out, v_mlp_norm_g, v_w_up, v_w_down, v_final_g):
    w = dict(meta=meta, attn_norm_g=attn_norm_g, w_in=w_in, b_f=b_f, conv_w=conv_w, conv_b=conv_b,
             w_gate_a=w_gate_a, b_gate_a=b_gate_a, w_gate_x=w_gate_x, b_gate_x=b_gate_x, lru_L=lru_L,
             attn_out_g=attn_out_g, rec_out_g=rec_out_g, w_out=w_out, mlp_norm_g=mlp_norm_g, w_up=w_up,
             w_down=w_down, final_g=final_g)
    mo = dict(meta=m_meta, attn_norm_g=m_attn_norm_g, w_in=m_w_in, b_f=m_b_f, conv_w=m_conv_w, conv_b=m_conv_b,
              w_gate_a=m_w_gate_a, b_gate_a=m_b_gate_a, w_gate_x=m_w_gate_x, b_gate_x=m_b_gate_x, lru_L=m_lru_L,
              attn_out_g=m_attn_out_g, rec_out_g=m_rec_out_g, w_out=m_w_out, mlp_norm_g=m_mlp_norm_g,
              w_up=m_w_up, w_down=m_w_down, final_g=m_final_g)
    vo = dict(meta=v_meta, attn_norm_g=v_attn_norm_g, w_in=v_w_in, b_f=v_b_f, conv_w=v_conv_w, conv_b=v_conv_b,
              w_gate_a=v_w_gate_a, b_gate_a=v_b_gate_a, w_gate_x=v_w_gate_x, b_gate_x=v_b_gate_x, lru_L=v_lru_L,
              attn_out_g=v_attn_out_g, rec_out_g=v_rec_out_g, w_out=v_w_out, mlp_norm_g=v_mlp_norm_g,
              w_up=v_w_up, w_down=v_w_down, final_g=v_final_g)
    depth = w_in.shape[0]
    nh = b_f.shape[1]
    rw = conv_b.shape[1]
    me = 4 * lax.axis_index("x") + 2 * lax.axis_index("y") + lax.axis_index("c")

    w['w_in_t'] = jnp.transpose(w_in, (2, 0, 1))
    swap = lambda a: jnp.swapaxes(a, 1, 2)
    step = _Step(w, nh, rw, me)
    g0 = step.group(0, meta)
    meta_full = g0[2].transpose(1, 0, 2).reshape(N_META, -1)
    conv_full = g0[3].transpose(1, 2, 0, 3).reshape(depth, CONV_WIDTH, rw)
    small = {n: w[n] for n in SMALL}
    small['conv_w'] = conv_full

    loss_part, dh0, tok = local_step(x[0], loss_target[0], meta_full, small, step)
    loss = lax.psum(loss_part, ("x", "y", "c"))
    grad_x = dh0[N_META:N_META + x.shape[1]][None]

    out_g, out_d, out_m, out_v = {}, {}, {}, {}

    def update_big(group, after):
        for n, r in step.received(group, after).items():
            if n == 'w_in':
                out = sum_adamw_t(r, swap(w[n]), swap(mo[n]), swap(vo[n]), "adamw_w_in")
                out = [swap(a) for a in out]
            else:
                shp = w[n].shape
                rows, cols = shp[0] * shp[1], shp[2]
                tr = min(512 if cols <= 512 else 256, rows)
                out = sum_adamw(r.reshape(N_DEV, rows, cols), w[n].reshape(rows, cols), mo[n].reshape(rows, cols),
                                vo[n].reshape(rows, cols), tr, "adamw_" + n)
                out = [a.reshape(shp) for a in out]
            out_g[n], out_d[n], out_m[n], out_v[n] = out
            after = out[0]
        return after

    after = update_big('mlp', step.started[-1][4][0])

    gsum = step.small_sum(after)
    gsum['meta'] = lax.dynamic_slice_in_dim(gsum['meta'], me * meta.shape[1], meta.shape[1], axis=1)
    gsum['conv_w'] = lax.dynamic_slice_in_dim(gsum['conv_w'], me * conv_w.shape[2], conv_w.shape[2], axis=2)
    packed = [_pack(t) for t in (gsum, {n: w[n] for n in SMALL}, {n: mo[n] for n in SMALL},
                                 {n: vo[n] for n in SMALL})]
    res = sum_adamw(packed[0][None], packed[1], packed[2], packed[3], packed[1].shape[0], "adamw_small")
    shapes = {n: w[n].shape for n in SMALL}
    for dst, vec in zip((out_g, out_d, out_m, out_v), res):
        dst.update(_unpack(vec, shapes))

    update_big('in', res[0])

    return (loss, grad_x, *[out_g[n] for n in WEIGHTS], *[out_d[n] for n in WEIGHTS],
            *[out_m[n] for n in WEIGHTS], *[out_v[n] for n in WEIGHTS])
```

```python
import functools
import math

import jax
import jax.numpy as jnp
from jax import lax
from jax.experimental import pallas as pl
from jax.experimental.pallas import tpu as pltpu

F32 = jnp.float32
BF16 = jnp.bfloat16

N_DEV = 8
N_META = 16
HEAD_DIM = 64
CONV_WIDTH = 4
RG_C = 8.0
NORM_EPS = 1e-6
LANES = 128
SUBLANES = 8
ATT_BLOCK = 128
ATT_TQ = 512
NEG_BIG = -1e30
ATT_SCALE = 1.0 / math.sqrt(HEAD_DIM)

ADAM_LR = 0.001
ADAM_B1 = 0.9
ADAM_B2 = 0.999
ADAM_EPS = 1e-08
ADAM_WD = 0.01
ADAM_STEP = 10

VMEM_LIMIT_BYTES = 56 * 1024 * 1024
MESH = pl.DeviceIdType.MESH
ANY = pl.BlockSpec(memory_space=pl.ANY)


def _cparams(*sem):
    return pltpu.CompilerParams(dimension_semantics=sem if sem else None,
                                vmem_limit_bytes=VMEM_LIMIT_BYTES)


def _dot(a, b):
    return jnp.dot(a, b, preferred_element_type=F32)


def _dot_nt(a, b):
    return lax.dot_general(a, b, (((1,), (1,)), ((), ())), preferred_element_type=F32)


def _dot_tn(a, b):
    return lax.dot_general(a, b, (((0,), (0,)), ((), ())), preferred_element_type=F32)


def _sigmoid(x):
    return 0.5 * (1.0 + jnp.tanh(0.5 * x))


def _log_sigmoid(x):
    return jnp.minimum(x, 0.0) - jnp.log(1.0 + jnp.exp(-jnp.abs(x)))


def _expm1(x):
    series = x * (1.0 + x * (0.5 + x * (1.0 / 6.0 + x * (1.0 / 24.0))))
    return jnp.where(jnp.abs(x) < 1e-2, series, jnp.exp(x) - 1.0)


_GELU_K = math.sqrt(2.0 / math.pi)
_GELU_C = 0.044715


def _gelu(x):
    t = jnp.tanh(_GELU_K * (x + _GELU_C * x * x * x))
    return 0.5 * x * (1.0 + t)


def _gelu_grad(x):
    t = jnp.tanh(_GELU_K * (x + _GELU_C * x * x * x))
    return 0.5 * (1.0 + t) + 0.5 * x * (1.0 - t * t) * _GELU_K * (1.0 + 3.0 * _GELU_C * x * x)


def _split3_dot(tri, x):
    hi = x.astype(BF16)
    r1 = x - hi.astype(F32)
    mid = r1.astype(BF16)
    lo = (r1 - mid.astype(F32)).astype(BF16)
    return _dot(tri, hi) + _dot(tri, mid) + _dot(tri, lo)


def _dot_split3(x, sel):
    hi = x.astype(BF16)
    r1 = x - hi.astype(F32)
    mid = r1.astype(BF16)
    lo = (r1 - mid.astype(F32)).astype(BF16)
    return _dot(hi, sel) + _dot(mid, sel) + _dot(lo, sel)


def _rms_fwd(x, g):
    r = lax.rsqrt(jnp.mean(x * x, axis=-1, keepdims=True) + NORM_EPS)
    return x * r * g


def _rms_bwd(x, g, dy):
    r = lax.rsqrt(jnp.mean(x * x, axis=-1, keepdims=True) + NORM_EPS)
    xn = x * r
    dxn = dy * g
    dx = r * (dxn - xn * jnp.mean(dxn * xn, axis=-1, keepdims=True))
    return dx, jnp.sum(dy * xn, axis=0, keepdims=True)


def _accumulate(ref, val, first):
    @pl.when(first)
    def _():
        ref[...] = val

    @pl.when(jnp.logical_not(first))
    def _():
        ref[...] += val


def in_proj(h, g1, w_in_t, wrest_t, nq, tm):
    tp, d = h.shape
    nr = wrest_t.shape[0]

    def body(h_ref, g_ref, wq_ref, wr_ref, z_ref, qkv_ref, rest_ref):
        z = _rms_fwd(h_ref[...], g_ref[...]).astype(BF16)
        z_ref[...] = z
        qkv_ref[...] = _dot_nt(z, wq_ref[...]).astype(BF16)
        rest_ref[...] = _dot_nt(z, wr_ref[...])

    return pl.pallas_call(
        body, name="in_proj", grid=(tp // tm,),
        in_specs=[pl.BlockSpec((tm, d), lambda i: (i, 0)),
                  pl.BlockSpec((1, d), lambda i: (0, 0)),
                  pl.BlockSpec((nq, d), lambda i: (0, 0)),
                  pl.BlockSpec((nr, d), lambda i: (0, 0))],
        out_specs=[pl.BlockSpec((tm, d), lambda i: (i, 0)),
                   pl.BlockSpec((tm, nq), lambda i: (i, 0)),
                   pl.BlockSpec((tm, nr), lambda i: (i, 0))],
        out_shape=[jax.ShapeDtypeStruct((tp, d), BF16),
                   jax.ShapeDtypeStruct((tp, nq), BF16),
                   jax.ShapeDtypeStruct((tp, nr), F32)],
        compiler_params=_cparams("parallel"),
    )(h, g1, w_in_t, wrest_t)


def fgate_fwd(rest, bf_pad, fcol):
    tp = rest.shape[0]
    nb = tp // ATT_BLOCK

    def body(f_ref, b_ref, c_ref, ct_ref):
        r_i = lax.broadcasted_iota(jnp.int32, (ATT_BLOCK, ATT_BLOCK), 0)
        c_i = lax.broadcasted_iota(jnp.int32, (ATT_BLOCK, ATT_BLOCK), 1)
        tri = (r_i >= c_i).astype(BF16)
        carry = jnp.zeros((1, LANES), F32)
        for i in range(nb):
            sl = slice(i * ATT_BLOCK, (i + 1) * ATT_BLOCK)
            lf = _log_sigmoid(f_ref[sl, :] + b_ref[...])
            cs = _split3_dot(tri, lf) + carry
            carry = cs[ATT_BLOCK - 1:ATT_BLOCK, :]
            c_ref[sl, :] = cs
            ct_ref[:, sl] = cs.T[0:SUBLANES, :]

    return pl.pallas_call(
        body, name="fgate_fwd", grid=(1,),
        in_specs=[pl.BlockSpec((tp, LANES), lambda i: (0, fcol)),
                  pl.BlockSpec((1, LANES), lambda i: (0, 0))],
        out_specs=[pl.BlockSpec((tp, LANES), lambda i: (0, 0)),
                   pl.BlockSpec((SUBLANES, tp), lambda i: (0, 0))],
        out_shape=[jax.ShapeDtypeStruct((tp, LANES), F32),
                   jax.ShapeDtypeStruct((SUBLANES, tp), F32)],
        compiler_params=_cparams("arbitrary"),
    )(rest, bf_pad)


def _pick_col(blk, head):
    lane = lax.broadcasted_iota(jnp.int32, blk.shape, 1)
    return jnp.sum(jnp.where(lane == head, blk, 0.0), axis=1, keepdims=True)


def _pick_row(blk, head):
    sub = lax.broadcasted_iota(jnp.int32, blk.shape, 0)
    return jnp.sum(jnp.where(sub == head, blk, 0.0), axis=0, keepdims=True)


def _att_tiles(tp):
    out, r0 = [], 0
    while r0 < tp:
        rows = min(ATT_TQ, tp - r0)
        out.append((r0, rows, r0 + rows))
        r0 += rows
    return out


def attn_fwd(qkv, c, ct, nh):
    tp = qkv.shape[0]
    npair = nh // 2
    tiles = _att_tiles(tp)

    def body(q_ref, k_ref, v_ref, c_ref, ct_ref, o_ref, lset_ref):
        p = pl.program_id(0)
        lset_ref[...] = jnp.zeros_like(lset_ref)
        for r0, nr, nk in tiles:
            rs = slice(r0, r0 + nr)
            causal = (r0 + lax.broadcasted_iota(jnp.int32, (nr, nk), 0)
                      >= lax.broadcasted_iota(jnp.int32, (nr, nk), 1))
            cblk = c_ref[rs, :]
            ctb = ct_ref[:, 0:nk]
            for hh in range(2):
                head = 2 * p + hh
                hs = slice(hh * HEAD_DIM, (hh + 1) * HEAD_DIM)
                q = q_ref[rs, hs] * ATT_SCALE
                s = _dot_nt(q, k_ref[0:nk, hs]) + (_pick_col(cblk, head) - _pick_row(ctb, head))
                s = jnp.where(causal, s, NEG_BIG)
                m = jnp.max(s, axis=1, keepdims=True)
                pm = jnp.exp(s - m)
                l = jnp.sum(pm, axis=1, keepdims=True)
                o_ref[rs, hs] = _dot(pm.astype(BF16), v_ref[0:nk, hs]) / l
                lse = m + jnp.log(l)
                lset_ref[hh:hh + 1, rs] = jnp.broadcast_to(lse, (nr, LANES)).T[0:1, :]

    pair = lambda p: (0, p)
    return pl.pallas_call(
        body, name="attn_fwd", grid=(npair,),
        in_specs=[pl.BlockSpec((tp, LANES), pair),
                  pl.BlockSpec((tp, LANES), lambda p: (0, npair + p)),
                  pl.BlockSpec((tp, LANES), lambda p: (0, 2 * npair + p)),
                  pl.BlockSpec((tp, LANES), lambda p: (0, 0)),
                  pl.BlockSpec((SUBLANES, tp), lambda p: (0, 0))],
        out_specs=[pl.BlockSpec((tp, LANES), pair),
                   pl.BlockSpec((None, SUBLANES, tp), lambda p: (p, 0, 0))],
        out_shape=[jax.ShapeDtypeStruct((tp, nh * HEAD_DIM), F32),
                   jax.ShapeDtypeStruct((npair, SUBLANES, tp), F32)],
        compiler_params=_cparams("parallel"),
    )(qkv, qkv, qkv, c, ct)


def _shift_down(x, k, n):
    if k == 0:
        return x
    rows = lax.broadcasted_iota(jnp.int32, x.shape, 0)
    return jnp.where(rows >= k, pltpu.roll(x, k, 0), 0.0)


def _shift_up(x, k, n):
    if k == 0:
        return x
    rows = lax.broadcasted_iota(jnp.int32, x.shape, 0)
    return jnp.where(rows < n - k, pltpu.roll(x, n - k, 0), 0.0)


def _conv_fwd(xr, cw_ref, cb_ref, n):
    xc = cw_ref[CONV_WIDTH - 1:CONV_WIDTH, :] * xr + cb_ref[...]
    for k in range(1, CONV_WIDTH):
        xc = xc + cw_ref[CONV_WIDTH - 1 - k:CONV_WIDTH - k, :] * _shift_down(xr, k, n)
    return xc


def _gates(xc, wga_ref, bga_ref, wgx_ref, bgx_ref, l_ref):
    xcb = xc.astype(BF16)
    r = _sigmoid(_dot(xcb, wga_ref[...]) + bga_ref[...])
    ig = _sigmoid(_dot(xcb, wgx_ref[...]) + bgx_ref[...])
    ls = _log_sigmoid(l_ref[...])
    log_a = RG_C * r * ls
    a = jnp.exp(log_a)
    mult = jnp.sqrt(-_expm1(2.0 * log_a))
    return xcb, r, ig, ls, log_a, a, mult


SCAN_UNROLL = 4


def _scan_rows(a_s, u_s, out_ref, n, reverse):
    nt = n // SUBLANES
    per = SCAN_UNROLL if nt % SCAN_UNROLL == 0 else 1
    row = lax.broadcasted_iota(jnp.int32, (SUBLANES, LANES), 0)
    last = 0 if reverse else SUBLANES - 1

    def tile_scan(a, u):
        for d in (1, 2, 4):
            if reverse:
                keep = row < SUBLANES - d
                sh = SUBLANES - d
            else:
                keep = row >= d
                sh = d
            a_sh = jnp.where(keep, pltpu.roll(a, sh, 0), 1.0)
            u_sh = jnp.where(keep, pltpu.roll(u, sh, 0), 0.0)
            u = a * u_sh + u
            a = a * a_sh
        return a, u

    def step(t, carry):
        tiles = []
        for k in range(per):
            tt = t * per + k
            if reverse:
                tt = nt - 1 - tt
            off = pl.multiple_of(tt * SUBLANES, SUBLANES)
            a, u = tile_scan(a_s[pl.ds(off, SUBLANES), :], u_s[pl.ds(off, SUBLANES), :])
            tiles.append((off, a, u))
        for off, a, u in tiles:
            out_ref[pl.ds(off, SUBLANES), :] = u + a * carry
            carry = u[last:last + 1, :] + a[last:last + 1, :] * carry
        return carry

    lax.fori_loop(0, nt // per, step, jnp.zeros((1, LANES), F32))


def rec_fwd(rest, convw, convb, wga, bga, wgx, bgx, lru, rw):
    tp = rest.shape[0]
    ng = rw // LANES

    def body(xr_ref, yr_ref, cw_ref, cb_ref, wga_ref, bga_ref, wgx_ref, bgx_ref, l_ref,
             rec_ref, hr_ref, xc_ref, a_s, u_s):
        xc = _conv_fwd(xr_ref[...], cw_ref, cb_ref, tp)
        xc_ref[...] = xc
        _, r, ig, ls, log_a, a, mult = _gates(xc, wga_ref, bga_ref, wgx_ref, bgx_ref, l_ref)
        a_s[...] = a
        u_s[...] = mult * ig * xc
        _scan_rows(a_s, u_s, hr_ref, tp, reverse=False)
        rec_ref[...] = hr_ref[...] * _gelu(yr_ref[...])

    col = lambda g: (0, g)
    vec = pl.BlockSpec((1, LANES), col)
    big = pl.BlockSpec((tp, LANES), col)
    return pl.pallas_call(
        body, name="rec_fwd", grid=(ng,),
        in_specs=[big, pl.BlockSpec((tp, LANES), lambda g: (0, ng + g)),
                  pl.BlockSpec((CONV_WIDTH, LANES), col), vec,
                  pl.BlockSpec((None, LANES, LANES), lambda g: (g, 0, 0)), vec,
                  pl.BlockSpec((None, LANES, LANES), lambda g: (g, 0, 0)), vec, vec],
        out_specs=[big, big, big],
        out_shape=[jax.ShapeDtypeStruct((tp, rw), F32)] * 3,
        scratch_shapes=[pltpu.VMEM((tp, LANES), F32), pltpu.VMEM((tp, LANES), F32)],
        compiler_params=_cparams("parallel"),
    )(rest, rest, convw, convb, wga, bga, wgx, bgx, lru)


def out_proj(h, o, rec, ga, gr, wout, g2, tm):
    tp, d = h.shape
    aw, rw = o.shape[1], rec.shape[1]

    def body(h_ref, o_ref, rec_ref, ga_ref, gr_ref, w_ref, g2_ref, h2_ref, mix_ref, z2_ref):
        mix_ref[:, 0:aw] = _rms_fwd(o_ref[...], ga_ref[...]).astype(BF16)
        mix_ref[:, aw:aw + rw] = _rms_fwd(rec_ref[...], gr_ref[...]).astype(BF16)
        h2 = h_ref[...] + _dot(mix_ref[...], w_ref[...])
        h2_ref[...] = h2
        z2_ref[...] = _rms_fwd(h2, g2_ref[...]).astype(BF16)

    row = lambda i: (i, 0)
    fix = lambda i: (0, 0)
    return pl.pallas_call(
        body, name="out_proj", grid=(tp // tm,),
        in_specs=[pl.BlockSpec((tm, d), row), pl.BlockSpec((tm, aw), row), pl.BlockSpec((tm, rw), row),
                  pl.BlockSpec((1, aw), fix), pl.BlockSpec((1, rw), fix),
                  pl.BlockSpec((d, d), fix), pl.BlockSpec((1, d), fix)],
        out_specs=[pl.BlockSpec((tm, d), row)] * 3,
        out_shape=[jax.ShapeDtypeStruct((tp, d), F32), jax.ShapeDtypeStruct((tp, d), BF16),
                   jax.ShapeDtypeStruct((tp, d), BF16)],
        compiler_params=_cparams("parallel"),
    )(h, o, rec, ga, gr, wout, g2)


MLP_BLOCKS = 2


def mlp_fwd(z2, h2, gup, gdown, tm):
    tp, d = h2.shape
    nf = gup.shape[0]
    tf = gup.shape[2]
    nb = MLP_BLOCKS if nf % MLP_BLOCKS == 0 else 1
    nj = nf // nb

    def body(z_ref, h_ref, wu_ref, wd_ref, u_ref, h3_ref, acc):
        j = pl.program_id(1)
        z = z_ref[...]
        part = None
        for b in range(nb):
            u = jnp.maximum(_dot(z, wu_ref[b]), 0.0)
            u_ref[:, b * tf:(b + 1) * tf] = u.astype(BF16)
            p = _dot((u * u).astype(BF16), wd_ref[b])
            part = p if part is None else part + p

        @pl.when(j == 0)
        def _():
            acc[...] = h_ref[...] + part

        @pl.when(j > 0)
        def _():
            acc[...] += part

        @pl.when(j == nj - 1)
        def _():
            h3_ref[...] = acc[...]

    return pl.pallas_call(
        body, name="mlp_fwd", grid=(tp // tm, nj),
        in_specs=[pl.BlockSpec((tm, d), lambda i, j: (i, 0)),
                  pl.BlockSpec((tm, d), lambda i, j: (i, 0)),
                  pl.BlockSpec((nb, d, tf), lambda i, j: (j, 0, 0)),
                  pl.BlockSpec((nb, tf, d), lambda i, j: (j, 0, 0))],
        out_specs=[pl.BlockSpec((tm, nb * tf), lambda i, j: (i, j)),
                   pl.BlockSpec((tm, d), lambda i, j: (i, 0))],
        out_shape=[jax.ShapeDtypeStruct((tp, nf * tf), BF16), jax.ShapeDtypeStruct((tp, d), F32)],
        scratch_shapes=[pltpu.VMEM((tm, d), F32)],
        compiler_params=_cparams("parallel", "arbitrary"),
    )(z2, h2, gup, gdown)


def loss_head(h, gf, tgt, t_real, tm):
    tp, d = h.shape

    def body(h_ref, g_ref, t_ref, dh_ref, dg_ref, loss_ref):
        i = pl.program_id(0)
        x = h_ref[...]
        g = g_ref[...]
        r = lax.rsqrt(jnp.mean(x * x, axis=-1, keepdims=True) + NORM_EPS)
        xn = x * r
        rows = i * tm + lax.broadcasted_iota(jnp.int32, (tm, 1), 0)
        valid = jnp.logical_and(rows >= N_META, rows < t_real)
        e = jnp.where(valid, xn * g - t_ref[...], 0.0)
        part = 0.5 * jnp.sum(jnp.sum(e * e, axis=1, keepdims=True) / d, axis=0, keepdims=True)
        dy = e / d
        dxn = dy * g
        dh_ref[...] = r * (dxn - xn * jnp.mean(dxn * xn, axis=-1, keepdims=True))
        _accumulate(dg_ref, jnp.sum(dy * xn, axis=0, keepdims=True), i == 0)
        _accumulate(loss_ref, jnp.broadcast_to(part, (1, LANES)), i == 0)

    row = lambda i: (i, 0)
    fix = lambda i: (0, 0)
    return pl.pallas_call(
        body, name="loss_head", grid=(tp // tm,),
        in_specs=[pl.BlockSpec((tm, d), row), pl.BlockSpec((1, d), fix), pl.BlockSpec((tm, d), row)],
        out_specs=[pl.BlockSpec((tm, d), row), pl.BlockSpec((1, d), fix), pl.BlockSpec((1, LANES), fix)],
        out_shape=[jax.ShapeDtypeStruct((tp, d), F32), jax.ShapeDtypeStruct((1, d), F32),
                   jax.ShapeDtypeStruct((1, LANES), F32)],
        compiler_params=_cparams("arbitrary"),
    )(h, gf, tgt)


def mlp_bwd(dh, u, h2, g2, gup, gdown, tm):
    tp, d = dh.shape
    nf = gup.shape[0]
    tf = gup.shape[2]
    nb = MLP_BLOCKS if nf % MLP_BLOCKS == 0 else 1
    nj = nf // nb
    ni = tp // tm

    def body(dh_ref, u_ref, h2_ref, g_ref, wu_ref, wd_ref, dup_ref, dh2_ref, dg_ref, dhb, acc):
        i = pl.program_id(0)
        j = pl.program_id(1)

        @pl.when(j == 0)
        def _():
            dhb[...] = dh_ref[...].astype(BF16)

        part = None
        for b in range(nb):
            cols = slice(b * tf, (b + 1) * tf)
            dup = (_dot_nt(dhb[...], wd_ref[b]) * (2.0 * u_ref[:, cols].astype(F32))).astype(BF16)
            dup_ref[:, cols] = dup
            p = _dot_nt(dup, wu_ref[b])
            part = p if part is None else part + p
        _accumulate(acc, part, j == 0)

        @pl.when(j == nj - 1)
        def _():
            dx, dg = _rms_bwd(h2_ref[...], g_ref[...], acc[...])
            dh2_ref[...] = dh_ref[...] + dx
            _accumulate(dg_ref, dg, i == 0)

    return pl.pallas_call(
        body, name="mlp_bwd", grid=(ni, nj),
        in_specs=[pl.BlockSpec((tm, d), lambda i, j: (i, 0)),
                  pl.BlockSpec((tm, nb * tf), lambda i, j: (i, j)),
                  pl.BlockSpec((tm, d), lambda i, j: (i, 0)),
                  pl.BlockSpec((1, d), lambda i, j: (0, 0)),
                  pl.BlockSpec((nb, d, tf), lambda i, j: (j, 0, 0)),
                  pl.BlockSpec((nb, tf, d), lambda i, j: (j, 0, 0))],
        out_specs=[pl.BlockSpec((tm, nb * tf), lambda i, j: (i, j)),
                   pl.BlockSpec((tm, d), lambda i, j: (i, 0)),
                   pl.BlockSpec((1, d), lambda i, j: (0, 0)),
                   pl.BlockSpec((tm, d), lambda i, j: (i, 0))],
        out_shape=[jax.ShapeDtypeStruct((tp, nf * tf), BF16), jax.ShapeDtypeStruct((tp, d), F32),
                   jax.ShapeDtypeStruct((1, d), F32), jax.ShapeDtypeStruct((tp, d), BF16)],
        scratch_shapes=[pltpu.VMEM((tm, d), F32)],
        compiler_params=_cparams("arbitrary", "arbitrary"),
    )(dh, u, h2, g2, gup, gdown)


def mm_tn(a, b, *, tk, tn, out_dtype, name, square_a=False, blocked_n=False):
    rows, kk = a.shape
    nn = b.shape[1]
    keep_at = nn // tn > 1

    def a_tile(a_ref):
        av = a_ref[...]
        if square_a:
            af = av.astype(F32)
            av = af * af
        return av.astype(BF16)

    def body(a_ref, b_ref, o_ref, *scratch):
        if keep_at:
            at, = scratch

            @pl.when(pl.program_id(1) == 0)
            def _():
                at[...] = a_tile(a_ref).T

            o_ref[...] = _dot(at[...], b_ref[...].astype(BF16)).astype(out_dtype)
        else:
            o_ref[...] = _dot_tn(a_tile(a_ref), b_ref[...].astype(BF16)).astype(out_dtype)

    if blocked_n:
        out_spec = pl.BlockSpec((None, tk, tn), lambda k, n: (n, k, 0))
        out_shape = jax.ShapeDtypeStruct((nn // tn, kk, tn), out_dtype)
    else:
        out_spec = pl.BlockSpec((tk, tn), lambda k, n: (k, n))
        out_shape = jax.ShapeDtypeStruct((kk, nn), out_dtype)
    return pl.pallas_call(
        body, name=name, grid=(kk // tk, nn // tn),
        in_specs=[pl.BlockSpec((rows, tk), lambda k, n: (0, k)),
                  pl.BlockSpec((rows, tn), lambda k, n: (0, n))],
        out_specs=out_spec, out_shape=out_shape,
        scratch_shapes=[pltpu.VMEM((tk, rows), BF16)] if keep_at else [],
        compiler_params=_cparams("parallel", "arbitrary"),
    )(a, b)


def out_proj_bwd(dh2, o, rec, ga, gr, wout, tm):
    tp, d = dh2.shape
    aw, rw = o.shape[1], rec.shape[1]

    def body(dh_ref, o_ref, rec_ref, ga_ref, gr_ref, w_ref, do_ref, drec_ref, dga_ref, dgr_ref):
        i = pl.program_id(0)
        dmix = _dot_nt(dh_ref[...].astype(BF16), w_ref[...])
        do, dga = _rms_bwd(o_ref[...], ga_ref[...], dmix[:, 0:aw])
        drec, dgr = _rms_bwd(rec_ref[...], gr_ref[...], dmix[:, aw:aw + rw])
        do_ref[...] = do
        drec_ref[...] = drec
        _accumulate(dga_ref, dga, i == 0)
        _accumulate(dgr_ref, dgr, i == 0)

    row = lambda i: (i, 0)
    fix = lambda i: (0, 0)
    return pl.pallas_call(
        body, name="out_proj_bwd", grid=(tp // tm,),
        in_specs=[pl.BlockSpec((tm, d), row), pl.BlockSpec((tm, aw), row), pl.BlockSpec((tm, rw), row),
                  pl.BlockSpec((1, aw), fix), pl.BlockSpec((1, rw), fix), pl.BlockSpec((d, d), fix)],
        out_specs=[pl.BlockSpec((tm, aw), row), pl.BlockSpec((tm, rw), row),
                   pl.BlockSpec((1, aw), fix), pl.BlockSpec((1, rw), fix)],
        out_shape=[jax.ShapeDtypeStruct((tp, aw), F32), jax.ShapeDtypeStruct((tp, rw), F32),
                   jax.ShapeDtypeStruct((1, aw), F32), jax.ShapeDtypeStruct((1, rw), F32)],
        compiler_params=_cparams("arbitrary"),
    )(dh2, o, rec, ga, gr, wout)


def rec_bwd(drec, hr, xc, rest, convw, convb, wga, bga, wgx, bgx, lru, rw):
    tp = rest.shape[0]
    ng = rw // LANES

    def body(drec_ref, hr_ref, xc_ref, xr_ref, yr_ref, cw_ref, cb_ref, wga_ref, bga_ref, wgx_ref, bgx_ref, l_ref,
             dxr_ref, dyr_ref, dwga_ref, dwgx_ref, vec_ref, a_s, u_s, lam_s):
        xc = xc_ref[...]
        h = hr_ref[...]
        yr = yr_ref[...]
        drec = drec_ref[...]
        xcb, r, ig, ls, log_a, a, mult = _gates(xc, wga_ref, bga_ref, wgx_ref, bgx_ref, l_ref)
        dyr_ref[...] = (drec * h * _gelu_grad(yr)).astype(BF16)
        a_s[...] = _shift_up(a, 1, tp)
        u_s[...] = drec * _gelu(yr)
        _scan_rows(a_s, u_s, lam_s, tp, reverse=True)
        lam = lam_s[...]
        da = lam * _shift_down(h, 1, tp)
        dmult = lam * ig * xc
        dig = lam * mult * xc
        dxc = lam * mult * ig
        a2 = jnp.exp(2.0 * log_a)
        dlog_a = da * a - dmult * a2 / mult
        dr = dlog_a * (RG_C * ls)
        dl = jnp.sum(dlog_a * (RG_C * r), axis=0, keepdims=True) * _sigmoid(-l_ref[...])
        dpa = dr * r * (1.0 - r)
        dpx = dig * ig * (1.0 - ig)
        dpab = dpa.astype(BF16)
        dpxb = dpx.astype(BF16)
        dxc = dxc + _dot_nt(dpab, wga_ref[...]) + _dot_nt(dpxb, wgx_ref[...])
        dwga_ref[...] = _dot_tn(xcb, dpab)
        dwgx_ref[...] = _dot_tn(xcb, dpxb)
        xr = xr_ref[...]
        dxr = cw_ref[CONV_WIDTH - 1:CONV_WIDTH, :] * dxc
        for k in range(1, CONV_WIDTH):
            dxr = dxr + cw_ref[CONV_WIDTH - 1 - k:CONV_WIDTH - k, :] * _shift_up(dxc, k, tp)
        dxr_ref[...] = dxr.astype(BF16)
        for k in range(CONV_WIDTH):
            vec_ref[k:k + 1, :] = jnp.sum(dxc * _shift_down(xr, CONV_WIDTH - 1 - k, tp), axis=0, keepdims=True)
        vec_ref[4:5, :] = jnp.sum(dxc, axis=0, keepdims=True)
        vec_ref[5:6, :] = jnp.sum(dpa, axis=0, keepdims=True)
        vec_ref[6:7, :] = jnp.sum(dpx, axis=0, keepdims=True)
        vec_ref[7:8, :] = dl

    col = lambda g: (0, g)
    vec = pl.BlockSpec((1, LANES), col)
    big = pl.BlockSpec((tp, LANES), col)
    sq = pl.BlockSpec((None, LANES, LANES), lambda g: (g, 0, 0))
    return pl.pallas_call(
        body, name="rec_bwd", grid=(ng,),
        in_specs=[big, big, big, big, pl.BlockSpec((tp, LANES), lambda g: (0, ng + g)),
                  pl.BlockSpec((CONV_WIDTH, LANES), col), vec, sq, vec, sq, vec, vec],
        out_specs=[big, big, sq, sq, pl.BlockSpec((None, SUBLANES, LANES), lambda g: (g, 0, 0))],
        out_shape=[jax.ShapeDtypeStruct((tp, rw), BF16), jax.ShapeDtypeStruct((tp, rw), BF16),
                   jax.ShapeDtypeStruct((ng, LANES, LANES), F32), jax.ShapeDtypeStruct((ng, LANES, LANES), F32),
                   jax.ShapeDtypeStruct((ng, SUBLANES, LANES), F32)],
        scratch_shapes=[pltpu.VMEM((tp, LANES), F32)] * 3,
        compiler_params=_cparams("parallel"),
    )(drec, hr, xc, rest, rest, convw, convb, wga, bga, wgx, bgx, lru)


def attn_bwd(qkv, do, o, lset, c, ct, nh):
    tp = qkv.shape[0]
    npair = nh // 2
    aw = nh * HEAD_DIM
    tiles = _att_tiles(tp)

    def body(q_ref, k_ref, v_ref, do_ref, o_ref, lset_ref, c_ref, ct_ref,
             dq_ref, dk_ref, dv_ref, drow_ref, dcol_ref, dk_acc, dv_acc):
        p = pl.program_id(0)
        dk_acc[...] = jnp.zeros_like(dk_acc)
        dv_acc[...] = jnp.zeros_like(dv_acc)
        dcol_ref[...] = jnp.zeros_like(dcol_ref)
        drow_ref[...] = jnp.zeros_like(drow_ref)
        for r0, nr, nk in tiles:
            rs = slice(r0, r0 + nr)
            causal = (r0 + lax.broadcasted_iota(jnp.int32, (nk, nr), 1)
                      >= lax.broadcasted_iota(jnp.int32, (nk, nr), 0))
            cblk = c_ref[0:nk, :]
            ctb = ct_ref[:, rs]
            for hh in range(2):
                head = 2 * p + hh
                hs = slice(hh * HEAD_DIM, (hh + 1) * HEAD_DIM)
                q = q_ref[rs, hs]
                k = k_ref[0:nk, hs]
                dof = do_ref[rs, hs]
                do16 = dof.astype(BF16)
                delta = jnp.sum(dof * o_ref[rs, hs], axis=1, keepdims=True)
                delta_row = jnp.broadcast_to(delta, (nr, LANES)).T[0:1, :]
                s_t = _dot_nt(k, q * ATT_SCALE) + (_pick_row(ctb, head) - _pick_col(cblk, head))
                p_t = jnp.where(causal, jnp.exp(s_t - lset_ref[hh:hh + 1, rs]), 0.0)
                ds_t = p_t * (_dot_nt(v_ref[0:nk, hs], do16) - delta_row)
                p16 = p_t.astype(BF16)
                ds16 = ds_t.astype(BF16)
                dv_acc[0:nk, hs] += _dot(p16, do16)
                dk_acc[0:nk, hs] += _dot(ds16, q) * ATT_SCALE
                dq_ref[rs, hs] = (_dot_tn(ds16, k) * ATT_SCALE).astype(BF16)
                drow_ref[hh:hh + 1, rs] = jnp.sum(ds_t, axis=0, keepdims=True)
                dcol_ref[0:nk, hs] -= jnp.broadcast_to(jnp.sum(ds_t, axis=1, keepdims=True), (nk, HEAD_DIM))
        dk_ref[...] = dk_acc[...].astype(BF16)
        dv_ref[...] = dv_acc[...].astype(BF16)

    pair = lambda p: (0, p)
    return pl.pallas_call(
        body, name="attn_bwd", grid=(npair,),
        in_specs=[pl.BlockSpec((tp, LANES), pair),
                  pl.BlockSpec((tp, LANES), lambda p: (0, npair + p)),
                  pl.BlockSpec((tp, LANES), lambda p: (0, 2 * npair + p)),
                  pl.BlockSpec((tp, LANES), pair),
                  pl.BlockSpec((tp, LANES), pair),
                  pl.BlockSpec((None, SUBLANES, tp), lambda p: (p, 0, 0)),
                  pl.BlockSpec((tp, LANES), lambda p: (0, 0)),
                  pl.BlockSpec((SUBLANES, tp), lambda p: (0, 0))],
        out_specs=[pl.BlockSpec((tp, LANES), pair), pl.BlockSpec((tp, LANES), pair),
                   pl.BlockSpec((tp, LANES), pair),
                   pl.BlockSpec((None, SUBLANES, tp), lambda p: (p, 0, 0)),
                   pl.BlockSpec((tp, LANES), pair)],
        out_shape=[jax.ShapeDtypeStruct((tp, aw), BF16), jax.ShapeDtypeStruct((tp, aw), BF16),
                   jax.ShapeDtypeStruct((tp, aw), BF16),
                   jax.ShapeDtypeStruct((npair, SUBLANES, tp), F32),
                   jax.ShapeDtypeStruct((tp, aw), F32)],
        scratch_shapes=[pltpu.VMEM((tp, LANES), F32), pltpu.VMEM((tp, LANES), F32)],
        compiler_params=_cparams("parallel"),
    )(qkv, qkv, qkv, do, o, lset, c, ct)


def fgate_bwd(dct8, drs, rest, bf_pad, fcol):
    tp = rest.shape[0]
    aw = drs.shape[1]
    nb = tp // ATT_BLOCK
    B = ATT_BLOCK

    def body(d_ref, drs_ref, f_ref, b_ref, dfl_ref, db_ref, pad_s):
        r_i = lax.broadcasted_iota(jnp.int32, (B, B), 0)
        c_i = lax.broadcasted_iota(jnp.int32, (B, B), 1)
        triu = (c_i >= r_i).astype(BF16)
        sel = (lax.broadcasted_iota(jnp.int32, (aw, LANES), 0)
               == HEAD_DIM * lax.broadcasted_iota(jnp.int32, (aw, LANES), 1)).astype(BF16)
        carry = jnp.zeros((1, LANES), F32)
        db = jnp.zeros((1, LANES), F32)
        pad_s[...] = jnp.zeros_like(pad_s)
        for i in range(nb - 1, -1, -1):
            sl = slice(i * B, (i + 1) * B)
            pad_s[0:SUBLANES, :] = d_ref[:, sl]
            dc = pad_s[...].T + _dot_split3(drs_ref[sl, :], sel)
            rc = _split3_dot(triu, dc)
            dlf = rc + carry
            carry = carry + rc[0:1, :]
            dfl = dlf * _sigmoid(-(f_ref[sl, :] + b_ref[...]))
            dfl_ref[sl, :] = dfl.astype(BF16)
            db = db + jnp.sum(dfl, axis=0, keepdims=True)
        db_ref[...] = db

    return pl.pallas_call(
        body, name="fgate_bwd", grid=(1,),
        in_specs=[pl.BlockSpec((SUBLANES, tp), lambda i: (0, 0)),
                  pl.BlockSpec((tp, aw), lambda i: (0, 0)),
                  pl.BlockSpec((tp, LANES), lambda i: (0, fcol)),
                  pl.BlockSpec((1, LANES), lambda i: (0, 0))],
        out_specs=[pl.BlockSpec((tp, LANES), lambda i: (0, 0)),
                   pl.BlockSpec((1, LANES), lambda i: (0, 0))],
        out_shape=[jax.ShapeDtypeStruct((tp, LANES), BF16), jax.ShapeDtypeStruct((1, LANES), F32)],
        scratch_shapes=[pltpu.VMEM((B, B), F32)],
        compiler_params=_cparams("arbitrary"),
    )(dct8, drs, rest, bf_pad)


def in_proj_bwd(dh2, parts, w_in_t, wrest_t, h, g1, tm):
    tp, d = h.shape
    dq, dk, dv, dxr, dyr, dfl = parts
    aw, rw = dq.shape[1], dxr.shape[1]

    def body(dh2_ref, dq_ref, dk_ref, dv_ref, dxr_ref, dyr_ref, dfl_ref, wq_ref, wr_ref, h_ref, g_ref,
             dh_ref, dg_ref):
        i = pl.program_id(0)
        dz = _dot(dq_ref[...], wq_ref[0:aw, :])
        dz += _dot(dk_ref[...], wq_ref[aw:2 * aw, :])
        dz += _dot(dv_ref[...], wq_ref[2 * aw:3 * aw, :])
        dz += _dot(dxr_ref[...], wr_ref[0:rw, :])
        dz += _dot(dyr_ref[...], wr_ref[rw:2 * rw, :])
        dz += _dot(dfl_ref[...], wr_ref[2 * rw:2 * rw + LANES, :])
        dx, dg = _rms_bwd(h_ref[...], g_ref[...], dz)
        dh_ref[...] = dh2_ref[...] + dx
        _accumulate(dg_ref, dg, i == 0)

    row = lambda i: (i, 0)
    fix = lambda i: (0, 0)
    return pl.pallas_call(
        body, name="in_proj_bwd", grid=(tp // tm,),
        in_specs=[pl.BlockSpec((tm, d), row),
                  pl.BlockSpec((tm, aw), row), pl.BlockSpec((tm, aw), row), pl.BlockSpec((tm, aw), row),
                  pl.BlockSpec((tm, rw), row), pl.BlockSpec((tm, rw), row), pl.BlockSpec((tm, LANES), row),
                  pl.BlockSpec((3 * aw, d), fix), pl.BlockSpec(wrest_t.shape, fix),
                  pl.BlockSpec((tm, d), row), pl.BlockSpec((1, d), fix)],
        out_specs=[pl.BlockSpec((tm, d), row), pl.BlockSpec((1, d), fix)],
        out_shape=[jax.ShapeDtypeStruct((tp, d), F32), jax.ShapeDtypeStruct((1, d), F32)],
        compiler_params=_cparams("arbitrary"),
    )(dh2, dq, dk, dv, dxr, dyr, dfl, w_in_t, wrest_t, h, g1)


def dw_in_t(z, parts, nh, tr):
    tp, d = z.shape
    dq, dk, dv, dxr, dyr, dfl = parts
    aw, rw = dq.shape[1], dxr.shape[1]
    d_in = 3 * aw + nh + 2 * rw
    nr = tp // tr
    offs = [(0, aw), (aw, aw), (2 * aw, aw), (3 * aw + nh, rw), (3 * aw + nh + rw, rw)]

    def body(z_ref, dq_ref, dk_ref, dv_ref, dxr_ref, dyr_ref, dfl_ref, o_ref, acc):
        r = pl.program_id(0)

        @pl.when(r == 0)
        def _():
            acc[...] = jnp.zeros_like(acc)

        zt = z_ref[...]
        for (o, n), ref in zip(offs, (dq_ref, dk_ref, dv_ref, dxr_ref, dyr_ref)):
            acc[o:o + n, :] += _dot_tn(ref[...], zt)
        acc[3 * aw:3 * aw + nh, :] += _dot_tn(dfl_ref[...], zt)[0:nh, :]

        @pl.when(r == nr - 1)
        def _():
            o_ref[...] = acc[...].astype(BF16)

    row = lambda r: (r, 0)
    return pl.pallas_call(
        body, name="dw_in", grid=(nr,),
        in_specs=[pl.BlockSpec((tr, d), row),
                  pl.BlockSpec((tr, aw), row), pl.BlockSpec((tr, aw), row), pl.BlockSpec((tr, aw), row),
                  pl.BlockSpec((tr, rw), row), pl.BlockSpec((tr, rw), row), pl.BlockSpec((tr, LANES), row)],
        out_specs=pl.BlockSpec((d_in, d), lambda r: (0, 0)),
        out_shape=jax.ShapeDtypeStruct((d_in, d), BF16),
        scratch_shapes=[pltpu.VMEM((d_in, d), F32)],
        compiler_params=_cparams("arbitrary"),
    )(z, dq, dk, dv, dxr, dyr, dfl)


def _place():
    return lax.axis_index("x"), lax.axis_index("y"), lax.axis_index("c")


HBM = pl.BlockSpec(memory_space=pltpu.HBM)
SEM = pl.BlockSpec(memory_space=pltpu.SEMAPHORE)
EFFECT = pltpu.SideEffectType.DATAFLOW_SIDE_EFFECTING


def _in_hbm(a):
    return pltpu.with_memory_space_constraint(a, pltpu.HBM)


def _as_list(a):
    return list(a) if isinstance(a, (list, tuple)) else [a]


def _gather_targets(x, y, c):
    return [(x, y, 1 - c), (1 - x, y, c), (x, 1 - y, c), (1 - x, 1 - y, c)]


def _slot(t):
    return 4 * t[0] + 2 * t[1] + t[2]


def gather_start(groups):
    flat = [a for g in groups for a in g]
    n = len(flat)
    ng = len(groups)
    lands = [lax.empty((N_DEV,) + a.shape, a.dtype) for a in flat]

    def body(*refs):
        src, land = refs[:n], refs[n:2 * n]
        sems = refs[2 * n:2 * n + 2 * ng]
        token = refs[-1]
        x, y, c = _place()
        me = 4 * x + 2 * y + c
        i = 0
        for gi, g in enumerate(groups):
            for a in range(len(g)):
                for k, t in enumerate(_gather_targets(x, y, c)):
                    pltpu.make_async_remote_copy(
                        src_ref=src[i], dst_ref=land[i].at[me],
                        send_sem=sems[2 * gi].at[4 * a + k], recv_sem=sems[2 * gi + 1].at[4 * a + k],
                        device_id=t, device_id_type=MESH).start()
                i += 1
        token[...] = jnp.zeros_like(token)

    sem_shapes = []
    for g in groups:
        sem_shapes += [pltpu.SemaphoreType.DMA((4 * len(g),)), pltpu.SemaphoreType.DMA((4 * len(g),))]
    out = pl.pallas_call(
        body, name="gather_start",
        out_shape=sem_shapes + [pltpu.HBM(a.shape, a.dtype) for a in flat + lands]
        + [jax.ShapeDtypeStruct((SUBLANES, LANES), F32)],
        in_specs=[HBM] * (2 * n),
        out_specs=[SEM] * (2 * ng) + [HBM] * (2 * n) + [pl.BlockSpec(memory_space=pltpu.VMEM)],
        input_output_aliases={i: 2 * ng + i for i in range(2 * n)},
        compiler_params=pltpu.CompilerParams(has_side_effects=EFFECT),
    )(*[_in_hbm(a) for a in flat + lands])
    sems = out[:2 * ng]
    thru = out[2 * ng:2 * ng + 2 * n]
    srcs_t, lands_t = thru[:n], thru[n:]
    res, i = [], 0
    for gi, g in enumerate(groups):
        res.append((sems[2 * gi], sems[2 * gi + 1], srcs_t[i:i + len(g)], lands_t[i:i + len(g)]))
        i += len(g)
    return res, out[-1]


def gather_wait(send, recv, srcs, lands, after, name):
    n = len(srcs)

    def body(*refs):
        src, land = refs[:n], refs[n:2 * n]
        send_sem, recv_sem = refs[2 * n], refs[2 * n + 1]
        x, y, c = _place()
        for a in range(n):
            for k, t in enumerate(_gather_targets(x, y, c)):
                cp = pltpu.make_async_remote_copy(
                    src_ref=src[a], dst_ref=land[a].at[_slot(t)],
                    send_sem=send_sem.at[4 * a + k], recv_sem=recv_sem.at[4 * a + k],
                    device_id=t, device_id_type=MESH)
                cp.wait_send()
                cp.wait_recv()

    out = pl.pallas_call(
        body, name=name,
        out_shape=[pltpu.HBM(a.shape, a.dtype) for a in list(srcs) + list(lands)],
        in_specs=[HBM] * (2 * n) + [SEM, SEM] + [ANY] * len(_as_list(after)),
        out_specs=[HBM] * (2 * n),
        input_output_aliases={i: i for i in range(2 * n)},
        compiler_params=pltpu.CompilerParams(has_side_effects=EFFECT),
    )(*srcs, *lands, send, recv, *_as_list(after))
    return out[:n], out[n:]


def forward_start(lands, name):
    n = len(lands)

    def body(*refs):
        land = refs[:n]
        send_sem, recv_sem = refs[n], refs[n + 1]
        token = refs[-1]
        x, y, c = _place()
        for a in range(n):
            for j, chip in enumerate([(1 - x, y), (x, 1 - y), (1 - x, 1 - y)]):
                blk = land[a].at[_slot((*chip, c))]
                pltpu.make_async_remote_copy(src_ref=blk, dst_ref=blk, send_sem=send_sem.at[3 * a + j],
                                             recv_sem=recv_sem.at[3 * a + j], device_id=(x, y, 1 - c),
                                             device_id_type=MESH).start()
        token[...] = jnp.zeros_like(token)

    out = pl.pallas_call(
        body, name=name,
        out_shape=[pltpu.SemaphoreType.DMA((3 * n,)), pltpu.SemaphoreType.DMA((3 * n,))]
        + [pltpu.HBM(a.shape, a.dtype) for a in lands] + [jax.ShapeDtypeStruct((SUBLANES, LANES), F32)],
        in_specs=[HBM] * n,
        out_specs=[SEM, SEM] + [HBM] * n + [pl.BlockSpec(memory_space=pltpu.VMEM)],
        input_output_aliases={i: 2 + i for i in range(n)},
        compiler_params=pltpu.CompilerParams(has_side_effects=EFFECT),
    )(*[_in_hbm(a) for a in lands])
    return out[0], out[1], out[2:2 + n], out[-1][0, 0]


def forward_wait(send, recv, lands, after, name):
    n = len(lands)

    def body(*refs):
        land = refs[:n]
        send_sem, recv_sem = refs[n], refs[n + 1]
        x, y, c = _place()
        for a in range(n):
            for j, chip in enumerate([(1 - x, y), (x, 1 - y), (1 - x, 1 - y)]):
                cp = pltpu.make_async_remote_copy(
                    src_ref=land[a].at[_slot((*chip, c))], dst_ref=land[a].at[_slot((*chip, 1 - c))],
                    send_sem=send_sem.at[3 * a + j], recv_sem=recv_sem.at[3 * a + j],
                    device_id=(x, y, 1 - c), device_id_type=MESH)
                cp.wait_send()
                cp.wait_recv()

    return pl.pallas_call(
        body, name=name,
        out_shape=[pltpu.HBM(a.shape, a.dtype) for a in lands],
        in_specs=[HBM] * n + [SEM, SEM, ANY],
        out_specs=[HBM] * n,
        input_output_aliases={i: i for i in range(n)},
        compiler_params=pltpu.CompilerParams(has_side_effects=EFFECT),
    )(*lands, send, recv, after)


def _relations():
    return [(dx, dy, dc) for dx in (0, 1) for dy in (0, 1) for dc in (0, 1) if dx + dy + dc]


def _peer(x, y, c, rel):
    return ((1 - x) if rel[0] else x, (1 - y) if rel[1] else y, (1 - c) if rel[2] else c)


def exchange_start(srcs, lands, layer, name):
    n = len(srcs)

    def body(*refs):
        src, land = refs[:n], refs[n:2 * n]
        send_sem, recv_sem = refs[2 * n], refs[2 * n + 1]
        token = refs[-1]
        x, y, c = _place()
        me = 4 * x + 2 * y + c
        for k, rel in enumerate(_relations()):
            peer = _peer(x, y, c, rel)
            for a in range(n):
                pltpu.make_async_remote_copy(
                    src_ref=src[a] if layer is None else src[a].at[_slot(peer)],
                    dst_ref=land[a].at[me] if layer is None else land[a].at[me, layer],
                    send_sem=send_sem.at[7 * a + k], recv_sem=recv_sem.at[7 * a + k],
                    device_id=peer, device_id_type=MESH).start()
        token[...] = jnp.zeros_like(token)

    out = pl.pallas_call(
        body, name=name,
        out_shape=[pltpu.SemaphoreType.DMA((7 * n,)), pltpu.SemaphoreType.DMA((7 * n,))]
        + [pltpu.HBM(a.shape, a.dtype) for a in list(srcs) + list(lands)]
        + [jax.ShapeDtypeStruct((SUBLANES, LANES), F32)],
        in_specs=[HBM] * (2 * n),
        out_specs=[SEM, SEM] + [HBM] * (2 * n) + [pl.BlockSpec(memory_space=pltpu.VMEM)],
        input_output_aliases={i: 2 + i for i in range(2 * n)},
        compiler_params=pltpu.CompilerParams(has_side_effects=EFFECT),
    )(*[_in_hbm(a) for a in list(srcs) + list(lands)])
    return out[0], out[1], out[2:2 + n], out[2 + n:2 + 2 * n], out[-1][0, 0]


def exchange_wait(send, recv, srcs, lands, after, layer, name):
    n = len(srcs)

    def body(*refs):
        src, land = refs[:n], refs[n:2 * n]
        send_sem, recv_sem = refs[2 * n], refs[2 * n + 1]
        x, y, c = _place()
        for k, rel in enumerate(_relations()):
            peer = _peer(x, y, c, rel)
            for a in range(n):
                cp = pltpu.make_async_remote_copy(
                    src_ref=src[a] if layer is None else src[a].at[_slot(peer)],
                    dst_ref=land[a].at[_slot(peer)] if layer is None else land[a].at[_slot(peer), layer],
                    send_sem=send_sem.at[7 * a + k], recv_sem=recv_sem.at[7 * a + k],
                    device_id=peer, device_id_type=MESH)
                cp.wait_send()
                cp.wait_recv()

    out = pl.pallas_call(
        body, name=name,
        out_shape=[pltpu.HBM(a.shape, a.dtype) for a in list(srcs) + list(lands)],
        in_specs=[HBM] * (2 * n) + [SEM, SEM] + [ANY] * len(_as_list(after)),
        out_specs=[HBM] * (2 * n),
        input_output_aliases={i: i for i in range(2 * n)},
        compiler_params=pltpu.CompilerParams(has_side_effects=EFFECT),
    )(*srcs, *lands, send, recv, *_as_list(after))
    return out[:n], out[n:]


def _adamw_math(g, w, m, v):
    m = ADAM_B1 * m + (1.0 - ADAM_B1) * g
    v = ADAM_B2 * v + (1.0 - ADAM_B2) * (g * g)
    m_hat = m / (1.0 - ADAM_B1 ** ADAM_STEP)
    v_hat = v / (1.0 - ADAM_B2 ** ADAM_STEP)
    delta = -ADAM_LR * (m_hat / (jnp.sqrt(v_hat) + ADAM_EPS) + ADAM_WD * w)
    return delta, m, v


def sum_adamw(parts, w, m, v, tr, name):
    npart, rows, cols = parts.shape

    def body(p_ref, w_ref, m_ref, v_ref, g_ref, d_ref, nm_ref, nv_ref):
        g = p_ref[0].astype(F32)
        for p in range(1, npart):
            g = g + p_ref[p].astype(F32)
        delta, nm, nv = _adamw_math(g, w_ref[...], m_ref[...], v_ref[...])
        g_ref[...] = g
        d_ref[...] = delta
        nm_ref[...] = nm
        nv_ref[...] = nv

    blk = pl.BlockSpec((tr, cols), lambda i: (i, 0))
    return pl.pallas_call(
        body, name=name, grid=(rows // tr,),
        in_specs=[pl.BlockSpec((npart, tr, cols), lambda i: (0, i, 0)), blk, blk, blk],
        out_specs=[blk] * 4,
        out_shape=[jax.ShapeDtypeStruct((rows, cols), F32)] * 4,
        compiler_params=_cparams("parallel"),
    )(parts, w, m, v)


def sum_adamw_t(parts, w, m, v, name):
    npart, nl, rows, cols = parts.shape

    def body(p_ref, w_ref, m_ref, v_ref, g_ref, d_ref, nm_ref, nv_ref):
        g = p_ref[0].astype(F32)
        for p in range(1, npart):
            g = g + p_ref[p].astype(F32)
        delta, nm, nv = _adamw_math(g, w_ref[...], m_ref[...], v_ref[...])
        g_ref[...] = g
        d_ref[...] = delta
        nm_ref[...] = nm
        nv_ref[...] = nv

    blk = pl.BlockSpec((None, rows, cols), lambda l: (l, 0, 0))
    return pl.pallas_call(
        body, name=name, grid=(nl,),
        in_specs=[pl.BlockSpec((npart, None, rows, cols), lambda l: (0, l, 0, 0)), blk, blk, blk],
        out_specs=[blk] * 4,
        out_shape=[jax.ShapeDtypeStruct((nl, rows, cols), F32)] * 4,
        compiler_params=_cparams("parallel"),
    )(parts, w, m, v)


def sum_parts(parts, name):
    npart, rows, cols = parts.shape

    def body(p_ref, g_ref):
        g = p_ref[0].astype(F32)
        for p in range(1, npart):
            g = g + p_ref[p].astype(F32)
        g_ref[...] = g

    return pl.pallas_call(
        body, name=name, grid=(1,),
        in_specs=[pl.BlockSpec((npart, rows, cols), lambda i: (0, 0, 0))],
        out_specs=pl.BlockSpec((rows, cols), lambda i: (0, 0)),
        out_shape=jax.ShapeDtypeStruct((rows, cols), F32),
        compiler_params=_cparams("arbitrary"),
    )(parts)


def _round_up(n, m):
    return (n + m - 1) // m * m


def _block_diag_pairs(w):
    nb, b, _ = w.shape
    per = LANES // b
    ng = nb // per
    w = w.reshape(ng, per, b, b)
    eye = jnp.eye(per, dtype=w.dtype)
    out = jnp.einsum('gpij,pq->gpiqj', w, eye).reshape(ng, LANES, LANES)
    return out.astype(BF16)


def _block_diag_extract(g, b):
    ng = g.shape[0]
    per = LANES // b
    g = g.reshape(ng, per, b, per, b)
    idx = jnp.arange(per)
    return g[:, idx, :, idx, :].transpose(1, 0, 2, 3).reshape(ng * per, b, b)


def _tiles(v):
    v = v.reshape(-1)
    n = _round_up(v.shape[0], SUBLANES * LANES)
    return jnp.pad(v, (0, n - v.shape[0])).reshape(-1, LANES)


SMALL = ['attn_norm_g', 'b_f', 'conv_w', 'conv_b', 'w_gate_a', 'b_gate_a', 'w_gate_x', 'b_gate_x',
         'lru_L', 'attn_out_g', 'rec_out_g', 'mlp_norm_g', 'final_g', 'meta']


def _pack(d):
    return jnp.concatenate([_tiles(d[n]) for n in SMALL], axis=0)


def _unpack(vec, shapes):
    out, r = {}, 0
    for n in SMALL:
        size = math.prod(shapes[n])
        nr = _round_up(size, SUBLANES * LANES) // LANES
        out[n] = vec[r:r + nr].reshape(-1)[:size].reshape(shapes[n])
        r += nr
    return out


def _row_tile(tp):
    return tp // 4 if (tp // 4) % 16 == 0 else tp


def local_step(x, tgt, meta, small, hooks):
    s, d = x.shape
    t_real = s + N_META
    tp = _round_up(t_real, ATT_BLOCK)
    depth = small['attn_norm_g'].shape[0]
    nh = small['b_f'].shape[1]
    rw = small['conv_b'].shape[1]
    blk = small['w_gate_a'].shape[2]
    tm = _row_tile(tp)
    tm2 = tp // 2
    fcol = 2 * rw // LANES

    h = jnp.concatenate([meta, x, jnp.zeros((tp - t_real, d), F32)], axis=0)
    tgt_p = jnp.pad(tgt, ((N_META, tp - t_real), (0, 0)))
    row = lambda v: v.reshape(1, -1)
    bf_pad = jnp.pad(small['b_f'], ((0, 0), (0, LANES - nh)))

    saved = []
    for l in range(depth):
        w_in_t, wrest_t, wout, tok_w = hooks.mixer_weights(l, h)
        wga = _block_diag_pairs(small['w_gate_a'][l])
        wgx = _block_diag_pairs(small['w_gate_x'][l])
        z, qkv, rest = in_proj(h, row(small['attn_norm_g'][l]) + tok_w, w_in_t, wrest_t, 3 * nh * HEAD_DIM, tm)
        c, ct = fgate_fwd(rest, bf_pad[l:l + 1], fcol)
        o, lset = attn_fwd(qkv, c, ct, nh)
        rec, hr, xc = rec_fwd(rest, small['conv_w'][l], row(small['conv_b'][l]), wga, row(small['b_gate_a'][l]),
                              wgx, row(small['b_gate_x'][l]), row(small['lru_L'][l]), rw)
        gup, gdown, tok_w = hooks.mlp_weights(l, rec)
        h2, mix, z2 = out_proj(h, o, rec, row(small['attn_out_g'][l]), row(small['rec_out_g'][l]), wout,
                               row(small['mlp_norm_g'][l]) + tok_w, tm)
        u, h3 = mlp_fwd(z2, h2, gup, gdown, tm2)
        saved.append(dict(h=h, z=z, qkv=qkv, rest=rest, c=c, ct=ct, o=o, lset=lset, rec=rec, hr=hr, xc=xc,
                          h2=h2, mix=mix, z2=z2, u=u, wga=wga, wgx=wgx,
                          w_in_t=w_in_t, wrest_t=wrest_t, wout=wout, gup=gup, gdown=gdown))
        h = h3

    dh, dgf, loss = loss_head(h, row(small['final_g']), tgt_p, t_real, tm)

    gs = {n: [None] * depth for n in SMALL if n not in ('final_g', 'meta')}
    tok = jnp.zeros((), F32)
    for l in reversed(range(depth)):
        sv = saved[l]
        gup, gdown = sv['gup'], sv['gdown']
        tf = gup.shape[2]
        dup, dh2, dg2, dhb = mlp_bwd(dh, sv['u'], sv['h2'], row(small['mlp_norm_g'][l]) + tok, gup, gdown, tm)
        gs['mlp_norm_g'][l] = dg2[0]
        do, drec, dga, dgr = out_proj_bwd(dh2, sv['o'], sv['rec'], row(small['attn_out_g'][l]),
                                          row(small['rec_out_g'][l]), sv['wout'], tm)
        gs['attn_out_g'][l] = dga[0]
        gs['rec_out_g'][l] = dgr[0]
        blocks = dict(
            w_down=mm_tn(sv['u'], dhb, tk=tf, tn=d, out_dtype=BF16, name="dw_down",
                         square_a=True).reshape(N_DEV, tf, d),
            w_up=mm_tn(sv['z2'], dup, tk=d, tn=tf, out_dtype=BF16, name="dw_up", blocked_n=True),
            w_out=mm_tn(sv['mix'], dh2, tk=d, tn=d // 2, out_dtype=BF16,
                        name="dw_out").reshape(N_DEV, d // N_DEV, d))
        tok = hooks.grads_ready(l, 'mlp', blocks)
        dxr, dyr, dwga, dwgx, vec = rec_bwd(drec, sv['hr'], sv['xc'], sv['rest'], small['conv_w'][l],
                                            row(small['conv_b'][l]) + tok, sv['wga'], row(small['b_gate_a'][l]),
                                            sv['wgx'], row(small['b_gate_x'][l]), row(small['lru_L'][l]), rw)
        gs['w_gate_a'][l] = _block_diag_extract(dwga, blk)
        gs['w_gate_x'][l] = _block_diag_extract(dwgx, blk)
        vec = vec.transpose(1, 0, 2).reshape(SUBLANES, rw)
        gs['conv_w'][l] = vec[0:CONV_WIDTH]
        gs['conv_b'][l] = vec[4]
        gs['b_gate_a'][l] = vec[5]
        gs['b_gate_x'][l] = vec[6]
        gs['lru_L'][l] = vec[7]
        dq, dk, dv, drow, dcol = attn_bwd(sv['qkv'], do, sv['o'], sv['lset'], sv['c'], sv['ct'] + tok, nh)
        drow8 = drow[:, 0:2, :].reshape(nh, tp)
        if nh < SUBLANES:
            drow8 = jnp.pad(drow8, ((0, SUBLANES - nh), (0, 0)))
        dfl, dbf = fgate_bwd(drow8, dcol, sv['rest'], bf_pad[l:l + 1], fcol)
        gs['b_f'][l] = dbf[0, 0:nh]
        parts = (dq, dk, dv, dxr, dyr, dfl)
        dh, dg1 = in_proj_bwd(dh2, parts, sv['w_in_t'], sv['wrest_t'], sv['h'], row(small['attn_norm_g'][l]), tm)
        gs['attn_norm_g'][l] = dg1[0]
        tok = jnp.zeros((), F32)
        if l == 0:
            grads = {n: jnp.stack(v) for n, v in gs.items()}
            grads['final_g'] = dgf[0]
            grads['meta'] = dh[0:N_META]
            tok = hooks.small_ready(grads)
        dw_in = dw_in_t(sv['z'], parts, nh, tm2)
        dw_in = dw_in.reshape(N_DEV, dw_in.shape[0] // N_DEV, d) + tok.astype(BF16)
        tok = hooks.grads_ready(l, 'in', dict(w_in=dw_in))

    return loss[0, 0], dh, tok


def prep_weights(g_in, g_out, nh, rw):
    d = g_in.shape[2]
    w_in_t = g_in.reshape(-1, d)
    f0 = 3 * nh * HEAD_DIM
    wrest_t = jnp.concatenate([w_in_t[f0 + nh:f0 + nh + 2 * rw],
                               jnp.pad(w_in_t[f0:f0 + nh], ((0, LANES - nh), (0, 0)))], axis=0)
    return w_in_t, wrest_t, g_out.reshape(d, d)


BIG = ['w_in', 'w_out', 'w_up', 'w_down']
EXCHANGE_GROUPS = {'mlp': ['w_down', 'w_up', 'w_out'], 'in': ['w_in']}
WEIGHTS = ['meta', 'attn_norm_g', 'w_in', 'b_f', 'conv_w', 'conv_b', 'w_gate_a', 'b_gate_a', 'w_gate_x', 'b_gate_x',
           'lru_L', 'attn_out_g', 'rec_out_g', 'w_out', 'mlp_norm_g', 'w_up', 'w_down', 'final_g']


def _set_own(arr, own, me):
    return lax.dynamic_update_slice_in_dim(arr, own[None], me, axis=0)


class _Step:
    def __init__(self, w, nh, rw, me):
        self.w, self.nh, self.rw, self.me = w, nh, rw, me
        depth = w['w_in'].shape[0]
        groups = []
        for l in range(depth):
            groups.append([w['w_in_t'][:, l, :].astype(BF16), w['w_out'][l].astype(BF16)])
            groups.append([w['w_up'][l].astype(BF16), w['w_down'][l].astype(BF16)])
        groups[0] = groups[0] + [w['meta'], w['conv_w']]
        self.pending, _ = gather_start(groups)
        self.gathered = {}
        self.passing = {}
        self.token = jnp.zeros((), F32)
        self.lands = {n: lax.empty((N_DEV,) + w[n].shape, BF16) for n in BIG}
        din8, _, d = w['w_in_t'].shape
        self.lands['w_in'] = lax.empty((N_DEV, depth, din8, d), BF16)
        self.started = []
        self.small = None

    def _pass_on(self, gi, after):
        if gi < len(self.pending) and gi not in self.passing:
            send, recv, srcs, lands = self.pending[gi]
            srcs, lands = gather_wait(send, recv, srcs, lands, after, "gather_wait_%d" % gi)
            fsend, frecv, lands, token = forward_start(lands, "forward_start_%d" % gi)
            self.passing[gi] = (fsend, frecv, srcs, lands)
            self.token = token

    def group(self, gi, after):
        if gi not in self.gathered:
            self._pass_on(gi, after)
            fsend, frecv, srcs, lands = self.passing[gi]
            lands = forward_wait(fsend, frecv, lands, after, "forward_wait_%d" % gi)
            self.gathered[gi] = [_set_own(g, own, self.me) for g, own in zip(lands, srcs)]
            if gi >= 1:
                self._pass_on(gi + 1, lands[0])
        return self.gathered[gi]

    def mixer_weights(self, l, after):
        g = self.group(2 * l, after)
        return (*prep_weights(g[0], g[1], self.nh, self.rw), self.token)

    def mlp_weights(self, l, after):
        g = self.group(2 * l + 1, after)
        return g[0], g[1], self.token

    def grads_ready(self, l, group, blocks):
        names = EXCHANGE_GROUPS[group]
        send, recv, srcs, lands, token = exchange_start(
            [blocks[n] for n in names], [self.lands[n] for n in names], l, "exchange_start_%s_%d" % (group, l))
        for n, a in zip(names, lands):
            self.lands[n] = a
        self.started.append((l, group, send, recv, srcs))
        return token

    def small_ready(self, grads):
        self.small_shapes = {n: grads[n].shape for n in SMALL}
        packed = _pack(grads).astype(BF16)
        send, recv, srcs, lands, token = exchange_start(
            [packed], [lax.empty((N_DEV,) + packed.shape, BF16)], None, "small_start")
        self.small = (send, recv, srcs, lands)
        return token

    def small_sum(self, after):
        send, recv, srcs, lands = self.small
        srcs, lands = exchange_wait(send, recv, srcs, lands, after, None, "small_wait")
        parts = _set_own(lands[0], srcs[0], self.me)
        return _unpack(sum_parts(parts, "sum_small_grads"), self.small_shapes)

    def received(self, group, after):
        names = EXCHANGE_GROUPS[group]
        own = {n: [None] * self.w[n].shape[0] for n in names}
        for l, grp, send, recv, srcs in self.started:
            if grp != group:
                continue
            srcs, lands = exchange_wait(send, recv, srcs, [self.lands[n] for n in names], after, l,
                                        "exchange_wait_%s_%d" % (group, l))
            for n, a, sr in zip(names, lands, srcs):
                self.lands[n] = a
                own[n][l] = lax.dynamic_index_in_dim(sr, self.me, 0, keepdims=False)
        return {n: _set_own(self.lands[n], jnp.stack(own[n]), self.me) for n in names}


def kernel(x, meta, attn_norm_g, w_in, b_f, conv_w, conv_b, w_gate_a, b_gate_a, w_gate_x, b_gate_x, lru_L, attn_out_g, rec_out_g, w_out, mlp_norm_g, w_up, w_down, final_g, loss_target, m_meta, m_attn_norm_g, m_w_in, m_b_f, m_conv_w, m_conv_b, m_w_gate_a, m_b_gate_a, m_w_gate_x, m_b_gate_x, m_lru_L, m_attn_out_g, m_rec_out_g, m_w_out, m_mlp_norm_g, m_w_up, m_w_down, m_final_g, v_meta, v_attn_norm_g, v_w_in, v_b_f, v_conv_w, v_conv_b, v_w_gate_a, v_b_gate_a, v_w_gate_x, v_b_gate_x, v_lru_L, v_attn_out_g, v_rec_out_g, v_w_out, v_mlp_norm_g, v_w_up, v_w_down, v_final_g):
    w = dict(meta=meta, attn_norm_g=attn_norm_g, w_in=w_in, b_f=b_f, conv_w=conv_w, conv_b=conv_b,
             w_gate_a=w_gate_a, b_gate_a=b_gate_a, w_gate_x=w_gate_x, b_gate_x=b_gate_x, lru_L=lru_L,
             attn_out_g=attn_out_g, rec_out_g=rec_out_g, w_out=w_out, mlp_norm_g=mlp_norm_g, w_up=w_up,
             w_down=w_down, final_g=final_g)
    mo = dict(meta=m_meta, attn_norm_g=m_attn_norm_g, w_in=m_w_in, b_f=m_b_f, conv_w=m_conv_w, conv_b=m_conv_b,
              w_gate_a=m_w_gate_a, b_gate_a=m_b_gate_a, w_gate_x=m_w_gate_x, b_gate_x=m_b_gate_x, lru_L=m_lru_L,
              attn_out_g=m_attn_out_g, rec_out_g=m_rec_out_g, w_out=m_w_out, mlp_norm_g=m_mlp_norm_g,
              w_up=m_w_up, w_down=m_w_down, final_g=m_final_g)
    vo = dict(meta=v_meta, attn_norm_g=v_attn_norm_g, w_in=v_w_in, b_f=v_b_f, conv_w=v_conv_w, conv_b=v_conv_b,
              w_gate_a=v_w_gate_a, b_gate_a=v_b_gate_a, w_gate_x=v_w_gate_x, b_gate_x=v_b_gate_x, lru_L=v_lru_L,
              attn_out_g=v_attn_out_g, rec_out_g=v_rec_out_g, w_out=v_w_out, mlp_norm_g=v_mlp_norm_g,
              w_up=v_w_up, w_down=v_w_down, final_g=v_final_g)
    depth = w_in.shape[0]
    nh = b_f.shape[1]
    rw = conv_b.shape[1]
    me = 4 * lax.axis_index("x") + 2 * lax.axis_index("y") + lax.axis_index("c")

    w['w_in_t'] = jnp.transpose(w_in, (2, 0, 1))
    swap = lambda a: jnp.swapaxes(a, 1, 2)
    step = _Step(w, nh, rw, me)
    g0 = step.group(0, meta)
    meta_full = g0[2].transpose(1, 0, 2).reshape(N_META, -1)
    conv_full = g0[3].transpose(1, 2, 0, 3).reshape(depth, CONV_WIDTH, rw)
    small = {n: w[n] for n in SMALL}
    small['conv_w'] = conv_full

    loss_part, dh0, tok = local_step(x[0], loss_target[0], meta_full, small, step)
    loss = lax.psum(loss_part, ("x", "y", "c"))
    grad_x = dh0[N_META:N_META + x.shape[1]][None]

    out_g, out_d, out_m, out_v = {}, {}, {}, {}

    def update_big(group, after):
        for n, r in step.received(group, after).items():
            if n == 'w_in':
                out = sum_adamw_t(r, swap(w[n]), swap(mo[n]), swap(vo[n]), "adamw_w_in")
                out = [swap(a) for a in out]
            else:
                shp = w[n].shape
                rows, cols = shp[0] * shp[1], shp[2]
                tr = min(512 if cols <= 512 else 256, rows)
                out = sum_adamw(r.reshape(N_DEV, rows, cols), w[n].reshape(rows, cols), mo[n].reshape(rows, cols),
                                vo[n].reshape(rows, cols), tr, "adamw_" + n)
                out = [a.reshape(shp) for a in out]
            out_g[n], out_d[n], out_m[n], out_v[n] = out
            after = out[0]
        return after

    update_big('mlp', step.started[-1][4][0])

    gsum = step.small_sum([out_g[n] for n in EXCHANGE_GROUPS['mlp']])
    gsum['meta'] = lax.dynamic_slice_in_dim(gsum['meta'], me * meta.shape[1], meta.shape[1], axis=1)
    gsum['conv_w'] = lax.dynamic_slice_in_dim(gsum['conv_w'], me * conv_w.shape[2], conv_w.shape[2], axis=2)
    packed = [_pack(t) for t in (gsum, {n: w[n] for n in SMALL}, {n: mo[n] for n in SMALL},
                                 {n: vo[n] for n in SMALL})]
    res = sum_adamw(packed[0][None], packed[1], packed[2], packed[3], packed[1].shape[0], "adamw_small")
    shapes = {n: w[n].shape for n in SMALL}
    for dst, vec in zip((out_g, out_d, out_m, out_v), res):
        dst.update(_unpack(vec, shapes))

    update_big('in', res[0])

    return (loss, grad_x, *[out_g[n] for n in WEIGHTS], *[out_d[n] for n in WEIGHTS],
            *[out_m[n] for n in WEIGHTS], *[out_v[n] for n in WEIGHTS])
```

```python
import functools
import math

import jax
import jax.numpy as jnp
from jax import lax
from jax.experimental import pallas as pl
from jax.experimental.pallas import tpu as pltpu

F32 = jnp.float32
BF16 = jnp.bfloat16

N_DEV = 8
N_META = 16
HEAD_DIM = 64
CONV_WIDTH = 4
RG_C = 8.0
NORM_EPS = 1e-6
LANES = 128
SUBLANES = 8
ATT_BLOCK = 128
ATT_TQ = 512
NEG_BIG = -1e30
ATT_SCALE = 1.0 / math.sqrt(HEAD_DIM)

ADAM_LR = 0.001
ADAM_B1 = 0.9
ADAM_B2 = 0.999
ADAM_EPS = 1e-08
ADAM_WD = 0.01
ADAM_STEP = 10

VMEM_LIMIT_BYTES = 56 * 1024 * 1024
MESH = pl.DeviceIdType.MESH
ANY = pl.BlockSpec(memory_space=pl.ANY)


def _cparams(*sem):
    return pltpu.CompilerParams(dimension_semantics=sem if sem else None,
                                vmem_limit_bytes=VMEM_LIMIT_BYTES)


def _dot(a, b):
    return jnp.dot(a, b, preferred_element_type=F32)


def _dot_nt(a, b):
    return lax.dot_general(a, b, (((1,), (1,)), ((), ())), preferred_element_type=F32)


def _dot_tn(a, b):
    return lax.dot_general(a, b, (((0,), (0,)), ((), ())), preferred_element_type=F32)


def _sigmoid(x):
    return 0.5 * (1.0 + jnp.tanh(0.5 * x))


def _log_sigmoid(x):
    return jnp.minimum(x, 0.0) - jnp.log(1.0 + jnp.exp(-jnp.abs(x)))


def _expm1(x):
    series = x * (1.0 + x * (0.5 + x * (1.0 / 6.0 + x * (1.0 / 24.0))))
    return jnp.where(jnp.abs(x) < 1e-2, series, jnp.exp(x) - 1.0)


_GELU_K = math.sqrt(2.0 / math.pi)
_GELU_C = 0.044715


def _gelu(x):
    t = jnp.tanh(_GELU_K * (x + _GELU_C * x * x * x))
    return 0.5 * x * (1.0 + t)


def _gelu_grad(x):
    t = jnp.tanh(_GELU_K * (x + _GELU_C * x * x * x))
    return 0.5 * (1.0 + t) + 0.5 * x * (1.0 - t * t) * _GELU_K * (1.0 + 3.0 * _GELU_C * x * x)


def _split3_dot(tri, x):
    hi = x.astype(BF16)
    r1 = x - hi.astype(F32)
    mid = r1.astype(BF16)
    lo = (r1 - mid.astype(F32)).astype(BF16)
    return _dot(tri, hi) + _dot(tri, mid) + _dot(tri, lo)


def _dot_split3(x, sel):
    hi = x.astype(BF16)
    r1 = x - hi.astype(F32)
    mid = r1.astype(BF16)
    lo = (r1 - mid.astype(F32)).astype(BF16)
    return _dot(hi, sel) + _dot(mid, sel) + _dot(lo, sel)


def _rms_fwd(x, g):
    r = lax.rsqrt(jnp.mean(x * x, axis=-1, keepdims=True) + NORM_EPS)
    return x * r * g


def _rms_bwd(x, g, dy):
    r = lax.rsqrt(jnp.mean(x * x, axis=-1, keepdims=True) + NORM_EPS)
    xn = x * r
    dxn = dy * g
    dx = r * (dxn - xn * jnp.mean(dxn * xn, axis=-1, keepdims=True))
    return dx, jnp.sum(dy * xn, axis=0, keepdims=True)


def _accumulate(ref, val, first):
    @pl.when(first)
    def _():
        ref[...] = val

    @pl.when(jnp.logical_not(first))
    def _():
        ref[...] += val


def in_proj(h, g1, w_in_t, wrest_t, nq, tm):
    tp, d = h.shape
    nr = wrest_t.shape[0]

    def body(h_ref, g_ref, wq_ref, wr_ref, z_ref, qkv_ref, rest_ref):
        z = _rms_fwd(h_ref[...], g_ref[...]).astype(BF16)
        z_ref[...] = z
        qkv_ref[...] = _dot_nt(z, wq_ref[...]).astype(BF16)
        rest_ref[...] = _dot_nt(z, wr_ref[...])

    return pl.pallas_call(
        body, name="in_proj", grid=(tp // tm,),
        in_specs=[pl.BlockSpec((tm, d), lambda i: (i, 0)),
                  pl.BlockSpec((1, d), lambda i: (0, 0)),
                  pl.BlockSpec((nq, d), lambda i: (0, 0)),
                  pl.BlockSpec((nr, d), lambda i: (0, 0))],
        out_specs=[pl.BlockSpec((tm, d), lambda i: (i, 0)),
                   pl.BlockSpec((tm, nq), lambda i: (i, 0)),
                   pl.BlockSpec((tm, nr), lambda i: (i, 0))],
        out_shape=[jax.ShapeDtypeStruct((tp, d), BF16),
                   jax.ShapeDtypeStruct((tp, nq), BF16),
                   jax.ShapeDtypeStruct((tp, nr), F32)],
        compiler_params=_cparams("parallel"),
    )(h, g1, w_in_t, wrest_t)


def fgate_fwd(rest, bf_pad, fcol):
    tp = rest.shape[0]
    nb = tp // ATT_BLOCK

    def body(f_ref, b_ref, c_ref, ct_ref):
        r_i = lax.broadcasted_iota(jnp.int32, (ATT_BLOCK, ATT_BLOCK), 0)
        c_i = lax.broadcasted_iota(jnp.int32, (ATT_BLOCK, ATT_BLOCK), 1)
        tri = (r_i >= c_i).astype(BF16)
        carry = jnp.zeros((1, LANES), F32)
        for i in range(nb):
            sl = slice(i * ATT_BLOCK, (i + 1) * ATT_BLOCK)
            lf = _log_sigmoid(f_ref[sl, :] + b_ref[...])
            cs = _split3_dot(tri, lf) + carry
            carry = cs[ATT_BLOCK - 1:ATT_BLOCK, :]
            c_ref[sl, :] = cs
            ct_ref[:, sl] = cs.T[0:SUBLANES, :]

    return pl.pallas_call(
        body, name="fgate_fwd", grid=(1,),
        in_specs=[pl.BlockSpec((tp, LANES), lambda i: (0, fcol)),
                  pl.BlockSpec((1, LANES), lambda i: (0, 0))],
        out_specs=[pl.BlockSpec((tp, LANES), lambda i: (0, 0)),
                   pl.BlockSpec((SUBLANES, tp), lambda i: (0, 0))],
        out_shape=[jax.ShapeDtypeStruct((tp, LANES), F32),
                   jax.ShapeDtypeStruct((SUBLANES, tp), F32)],
        compiler_params=_cparams("arbitrary"),
    )(rest, bf_pad)


def _pick_col(blk, head):
    lane = lax.broadcasted_iota(jnp.int32, blk.shape, 1)
    return jnp.sum(jnp.where(lane == head, blk, 0.0), axis=1, keepdims=True)


def _pick_row(blk, head):
    sub = lax.broadcasted_iota(jnp.int32, blk.shape, 0)
    return jnp.sum(jnp.where(sub == head, blk, 0.0), axis=0, keepdims=True)


def _att_tiles(tp):
    out, r0 = [], 0
    while r0 < tp:
        rows = min(ATT_TQ, tp - r0)
        out.append((r0, rows, r0 + rows))
        r0 += rows
    return out


def attn_fwd(qkv, c, ct, nh):
    tp = qkv.shape[0]
    npair = nh // 2
    tiles = _att_tiles(tp)

    def body(q_ref, k_ref, v_ref, c_ref, ct_ref, o_ref, lset_ref):
        p = pl.program_id(0)
        lset_ref[...] = jnp.zeros_like(lset_ref)
        for r0, nr, nk in tiles:
            rs = slice(r0, r0 + nr)
            causal = (r0 + lax.broadcasted_iota(jnp.int32, (nr, nk), 0)
                      >= lax.broadcasted_iota(jnp.int32, (nr, nk), 1))
            cblk = c_ref[rs, :]
            ctb = ct_ref[:, 0:nk]
            for hh in range(2):
                head = 2 * p + hh
                hs = slice(hh * HEAD_DIM, (hh + 1) * HEAD_DIM)
                q = q_ref[rs, hs] * ATT_SCALE
                s = _dot_nt(q, k_ref[0:nk, hs]) + (_pick_col(cblk, head) - _pick_row(ctb, head))
                s = jnp.where(causal, s, NEG_BIG)
                m = jnp.max(s, axis=1, keepdims=True)
                pm = jnp.exp(s - m)
                l = jnp.sum(pm, axis=1, keepdims=True)
                o_ref[rs, hs] = _dot(pm.astype(BF16), v_ref[0:nk, hs]) / l
                lse = m + jnp.log(l)
                lset_ref[hh:hh + 1, rs] = jnp.broadcast_to(lse, (nr, LANES)).T[0:1, :]

    pair = lambda p: (0, p)
    return pl.pallas_call(
        body, name="attn_fwd", grid=(npair,),
        in_specs=[pl.BlockSpec((tp, LANES), pair),
                  pl.BlockSpec((tp, LANES), lambda p: (0, npair + p)),
                  pl.BlockSpec((tp, LANES), lambda p: (0, 2 * npair + p)),
                  pl.BlockSpec((tp, LANES), lambda p: (0, 0)),
                  pl.BlockSpec((SUBLANES, tp), lambda p: (0, 0))],
        out_specs=[pl.BlockSpec((tp, LANES), pair),
                   pl.BlockSpec((None, SUBLANES, tp), lambda p: (p, 0, 0))],
        out_shape=[jax.ShapeDtypeStruct((tp, nh * HEAD_DIM), F32),
                   jax.ShapeDtypeStruct((npair, SUBLANES, tp), F32)],
        compiler_params=_cparams("parallel"),
    )(qkv, qkv, qkv, c, ct)


def _shift_down(x, k, n):
    if k == 0:
        return x
    rows = lax.broadcasted_iota(jnp.int32, x.shape, 0)
    return jnp.where(rows >= k, pltpu.roll(x, k, 0), 0.0)


def _shift_up(x, k, n):
    if k == 0:
        return x
    rows = lax.broadcasted_iota(jnp.int32, x.shape, 0)
    return jnp.where(rows < n - k, pltpu.roll(x, n - k, 0), 0.0)


def _conv_fwd(xr, cw_ref, cb_ref, n):
    xc = cw_ref[CONV_WIDTH - 1:CONV_WIDTH, :] * xr + cb_ref[...]
    for k in range(1, CONV_WIDTH):
        xc = xc + cw_ref[CONV_WIDTH - 1 - k:CONV_WIDTH - k, :] * _shift_down(xr, k, n)
    return xc


def _gates(xc, wga_ref, bga_ref, wgx_ref, bgx_ref, l_ref):
    xcb = xc.astype(BF16)
    r = _sigmoid(_dot(xcb, wga_ref[...]) + bga_ref[...])
    ig = _sigmoid(_dot(xcb, wgx_ref[...]) + bgx_ref[...])
    ls = _log_sigmoid(l_ref[...])
    log_a = RG_C * r * ls
    a = jnp.exp(log_a)
    mult = jnp.sqrt(-_expm1(2.0 * log_a))
    return xcb, r, ig, ls, log_a, a, mult


SCAN_UNROLL = 4


def _scan_rows(a_s, u_s, out_ref, n, reverse):
    nt = n // SUBLANES
    per = SCAN_UNROLL if nt % SCAN_UNROLL == 0 else 1
    row = lax.broadcasted_iota(jnp.int32, (SUBLANES, LANES), 0)
    last = 0 if reverse else SUBLANES - 1

    def tile_scan(a, u):
        for d in (1, 2, 4):
            if reverse:
                keep = row < SUBLANES - d
                sh = SUBLANES - d
            else:
                keep = row >= d
                sh = d
            a_sh = jnp.where(keep, pltpu.roll(a, sh, 0), 1.0)
            u_sh = jnp.where(keep, pltpu.roll(u, sh, 0), 0.0)
            u = a * u_sh + u
            a = a * a_sh
        return a, u

    def step(t, carry):
        tiles = []
        for k in range(per):
            tt = t * per + k
            if reverse:
                tt = nt - 1 - tt
            off = pl.multiple_of(tt * SUBLANES, SUBLANES)
            a, u = tile_scan(a_s[pl.ds(off, SUBLANES), :], u_s[pl.ds(off, SUBLANES), :])
            tiles.append((off, a, u))
        for off, a, u in tiles:
            out_ref[pl.ds(off, SUBLANES), :] = u + a * carry
            carry = u[last:last + 1, :] + a[last:last + 1, :] * carry
        return carry

    lax.fori_loop(0, nt // per, step, jnp.zeros((1, LANES), F32))


def rec_fwd(rest, convw, convb, wga, bga, wgx, bgx, lru, rw):
    tp = rest.shape[0]
    ng = rw // LANES

    def body(xr_ref, yr_ref, cw_ref, cb_ref, wga_ref, bga_ref, wgx_ref, bgx_ref, l_ref,
             rec_ref, hr_ref, xc_ref, a_s, u_s):
        xc = _conv_fwd(xr_ref[...], cw_ref, cb_ref, tp)
        xc_ref[...] = xc
        _, r, ig, ls, log_a, a, mult = _gates(xc, wga_ref, bga_ref, wgx_ref, bgx_ref, l_ref)
        a_s[...] = a
        u_s[...] = mult * ig * xc
        _scan_rows(a_s, u_s, hr_ref, tp, reverse=False)
        rec_ref[...] = hr_ref[...] * _gelu(yr_ref[...])

    col = lambda g: (0, g)
    vec = pl.BlockSpec((1, LANES), col)
    big = pl.BlockSpec((tp, LANES), col)
    return pl.pallas_call(
        body, name="rec_fwd", grid=(ng,),
        in_specs=[big, pl.BlockSpec((tp, LANES), lambda g: (0, ng + g)),
                  pl.BlockSpec((CONV_WIDTH, LANES), col), vec,
                  pl.BlockSpec((None, LANES, LANES), lambda g: (g, 0, 0)), vec,
                  pl.BlockSpec((None, LANES, LANES), lambda g: (g, 0, 0)), vec, vec],
        out_specs=[big, big, big],
        out_shape=[jax.ShapeDtypeStruct((tp, rw), F32)] * 3,
        scratch_shapes=[pltpu.VMEM((tp, LANES), F32), pltpu.VMEM((tp, LANES), F32)],
        compiler_params=_cparams("parallel"),
    )(rest, rest, convw, convb, wga, bga, wgx, bgx, lru)


def out_proj(h, o, rec, ga, gr, wout, g2, tm):
    tp, d = h.shape
    aw, rw = o.shape[1], rec.shape[1]

    def body(h_ref, o_ref, rec_ref, ga_ref, gr_ref, w_ref, g2_ref, h2_ref, mix_ref, z2_ref):
        mix_ref[:, 0:aw] = _rms_fwd(o_ref[...], ga_ref[...]).astype(BF16)
        mix_ref[:, aw:aw + rw] = _rms_fwd(rec_ref[...], gr_ref[...]).astype(BF16)
        h2 = h_ref[...] + _dot(mix_ref[...], w_ref[...])
        h2_ref[...] = h2
        z2_ref[...] = _rms_fwd(h2, g2_ref[...]).astype(BF16)

    row = lambda i: (i, 0)
    fix = lambda i: (0, 0)
    return pl.pallas_call(
        body, name="out_proj", grid=(tp // tm,),
        in_specs=[pl.BlockSpec((tm, d), row), pl.BlockSpec((tm, aw), row), pl.BlockSpec((tm, rw), row),
                  pl.BlockSpec((1, aw), fix), pl.BlockSpec((1, rw), fix),
                  pl.BlockSpec((d, d), fix), pl.BlockSpec((1, d), fix)],
        out_specs=[pl.BlockSpec((tm, d), row)] * 3,
        out_shape=[jax.ShapeDtypeStruct((tp, d), F32), jax.ShapeDtypeStruct((tp, d), BF16),
                   jax.ShapeDtypeStruct((tp, d), BF16)],
        compiler_params=_cparams("parallel"),
    )(h, o, rec, ga, gr, wout, g2)


MLP_BLOCKS = 2


def mlp_fwd(z2, h2, gup, gdown, tm):
    tp, d = h2.shape
    nf = gup.shape[0]
    tf = gup.shape[2]
    nb = MLP_BLOCKS if nf % MLP_BLOCKS == 0 else 1
    nj = nf // nb

    def body(z_ref, h_ref, wu_ref, wd_ref, u_ref, h3_ref, acc):
        j = pl.program_id(1)
        z = z_ref[...]
        part = None
        for b in range(nb):
            u = jnp.maximum(_dot(z, wu_ref[b]), 0.0)
            u_ref[:, b * tf:(b + 1) * tf] = u.astype(BF16)
            p = _dot((u * u).astype(BF16), wd_ref[b])
            part = p if part is None else part + p

        @pl.when(j == 0)
        def _():
            acc[...] = h_ref[...] + part

        @pl.when(j > 0)
        def _():
            acc[...] += part

        @pl.when(j == nj - 1)
        def _():
            h3_ref[...] = acc[...]

    return pl.pallas_call(
        body, name="mlp_fwd", grid=(tp // tm, nj),
        in_specs=[pl.BlockSpec((tm, d), lambda i, j: (i, 0)),
                  pl.BlockSpec((tm, d), lambda i, j: (i, 0)),
                  pl.BlockSpec((nb, d, tf), lambda i, j: (j, 0, 0)),
                  pl.BlockSpec((nb, tf, d), lambda i, j: (j, 0, 0))],
        out_specs=[pl.BlockSpec((tm, nb * tf), lambda i, j: (i, j)),
                   pl.BlockSpec((tm, d), lambda i, j: (i, 0))],
        out_shape=[jax.ShapeDtypeStruct((tp, nf * tf), BF16), jax.ShapeDtypeStruct((tp, d), F32)],
        scratch_shapes=[pltpu.VMEM((tm, d), F32)],
        compiler_params=_cparams("parallel", "arbitrary"),
    )(z2, h2, gup, gdown)


def loss_head(h, gf, tgt, t_real, tm):
    tp, d = h.shape

    def body(h_ref, g_ref, t_ref, dh_ref, dg_ref, loss_ref):
        i = pl.program_id(0)
        x = h_ref[...]
        g = g_ref[...]
        r = lax.rsqrt(jnp.mean(x * x, axis=-1, keepdims=True) + NORM_EPS)
        xn = x * r
        rows = i * tm + lax.broadcasted_iota(jnp.int32, (tm, 1), 0)
        valid = jnp.logical_and(rows >= N_META, rows < t_real)
        e = jnp.where(valid, xn * g - t_ref[...], 0.0)
        part = 0.5 * jnp.sum(jnp.sum(e * e, axis=1, keepdims=True) / d, axis=0, keepdims=True)
        dy = e / d
        dxn = dy * g
        dh_ref[...] = r * (dxn - xn * jnp.mean(dxn * xn, axis=-1, keepdims=True))
        _accumulate(dg_ref, jnp.sum(dy * xn, axis=0, keepdims=True), i == 0)
        _accumulate(loss_ref, jnp.broadcast_to(part, (1, LANES)), i == 0)

    row = lambda i: (i, 0)
    fix = lambda i: (0, 0)
    return pl.pallas_call(
        body, name="loss_head", grid=(tp // tm,),
        in_specs=[pl.BlockSpec((tm, d), row), pl.BlockSpec((1, d), fix), pl.BlockSpec((tm, d), row)],
        out_specs=[pl.BlockSpec((tm, d), row), pl.BlockSpec((1, d), fix), pl.BlockSpec((1, LANES), fix)],
        out_shape=[jax.ShapeDtypeStruct((tp, d), F32), jax.ShapeDtypeStruct((1, d), F32),
                   jax.ShapeDtypeStruct((1, LANES), F32)],
        compiler_params=_cparams("arbitrary"),
    )(h, gf, tgt)


def mlp_bwd(dh, u, h2, g2, gup, gdown, tm):
    tp, d = dh.shape
    nf = gup.shape[0]
    tf = gup.shape[2]
    nb = MLP_BLOCKS if nf % MLP_BLOCKS == 0 else 1
    nj = nf // nb
    ni = tp // tm

    def body(dh_ref, u_ref, h2_ref, g_ref, wu_ref, wd_ref, dup_ref, dh2_ref, dg_ref, dhb, acc):
        i = pl.program_id(0)
        j = pl.program_id(1)

        @pl.when(j == 0)
        def _():
            dhb[...] = dh_ref[...].astype(BF16)

        part = None
        for b in range(nb):
            cols = slice(b * tf, (b + 1) * tf)
            dup = (_dot_nt(dhb[...], wd_ref[b]) * (2.0 * u_ref[:, cols].astype(F32))).astype(BF16)
            dup_ref[:, cols] = dup
            p = _dot_nt(dup, wu_ref[b])
            part = p if part is None else part + p
        _accumulate(acc, part, j == 0)

        @pl.when(j == nj - 1)
        def _():
            dx, dg = _rms_bwd(h2_ref[...], g_ref[...], acc[...])
            dh2_ref[...] = dh_ref[...] + dx
            _accumulate(dg_ref, dg, i == 0)

    return pl.pallas_call(
        body, name="mlp_bwd", grid=(ni, nj),
        in_specs=[pl.BlockSpec((tm, d), lambda i, j: (i, 0)),
                  pl.BlockSpec((tm, nb * tf), lambda i, j: (i, j)),
                  pl.BlockSpec((tm, d), lambda i, j: (i, 0)),
                  pl.BlockSpec((1, d), lambda i, j: (0, 0)),
                  pl.BlockSpec((nb, d, tf), lambda i, j: (j, 0, 0)),
                  pl.BlockSpec((nb, tf, d), lambda i, j: (j, 0, 0))],
        out_specs=[pl.BlockSpec((tm, nb * tf), lambda i, j: (i, j)),
                   pl.BlockSpec((tm, d), lambda i, j: (i, 0)),
                   pl.BlockSpec((1, d), lambda i, j: (0, 0)),
                   pl.BlockSpec((tm, d), lambda i, j: (i, 0))],
        out_shape=[jax.ShapeDtypeStruct((tp, nf * tf), BF16), jax.ShapeDtypeStruct((tp, d), F32),
                   jax.ShapeDtypeStruct((1, d), F32), jax.ShapeDtypeStruct((tp, d), BF16)],
        scratch_shapes=[pltpu.VMEM((tm, d), F32)],
        compiler_params=_cparams("arbitrary", "arbitrary"),
    )(dh, u, h2, g2, gup, gdown)


def mm_tn(a, b, *, tk, tn, out_dtype, name, square_a=False, blocked_n=False):
    rows, kk = a.shape
    nn = b.shape[1]
    keep_at = nn // tn > 1

    def a_tile(a_ref):
        av = a_ref[...]
        if square_a:
            af = av.astype(F32)
            av = af * af
        return av.astype(BF16)

    def body(a_ref, b_ref, o_ref, *scratch):
        if keep_at:
            at, = scratch

            @pl.when(pl.program_id(1) == 0)
            def _():
                at[...] = a_tile(a_ref).T

            o_ref[...] = _dot(at[...], b_ref[...].astype(BF16)).astype(out_dtype)
        else:
            o_ref[...] = _dot_tn(a_tile(a_ref), b_ref[...].astype(BF16)).astype(out_dtype)

    if blocked_n:
        out_spec = pl.BlockSpec((None, tk, tn), lambda k, n: (n, k, 0))
        out_shape = jax.ShapeDtypeStruct((nn // tn, kk, tn), out_dtype)
    else:
        out_spec = pl.BlockSpec((tk, tn), lambda k, n: (k, n))
        out_shape = jax.ShapeDtypeStruct((kk, nn), out_dtype)
    return pl.pallas_call(
        body, name=name, grid=(kk // tk, nn // tn),
        in_specs=[pl.BlockSpec((rows, tk), lambda k, n: (0, k)),
                  pl.BlockSpec((rows, tn), lambda k, n: (0, n))],
        out_specs=out_spec, out_shape=out_shape,
        scratch_shapes=[pltpu.VMEM((tk, rows), BF16)] if keep_at else [],
        compiler_params=_cparams("parallel", "arbitrary"),
    )(a, b)


def out_proj_bwd(dh2, o, rec, ga, gr, wout, tm):
    tp, d = dh2.shape
    aw, rw = o.shape[1], rec.shape[1]

    def body(dh_ref, o_ref, rec_ref, ga_ref, gr_ref, w_ref, do_ref, drec_ref, dga_ref, dgr_ref):
        i = pl.program_id(0)
        dmix = _dot_nt(dh_ref[...].astype(BF16), w_ref[...])
        do, dga = _rms_bwd(o_ref[...], ga_ref[...], dmix[:, 0:aw])
        drec, dgr = _rms_bwd(rec_ref[...], gr_ref[...], dmix[:, aw:aw + rw])
        do_ref[...] = do
        drec_ref[...] = drec
        _accumulate(dga_ref, dga, i == 0)
        _accumulate(dgr_ref, dgr, i == 0)

    row = lambda i: (i, 0)
    fix = lambda i: (0, 0)
    return pl.pallas_call(
        body, name="out_proj_bwd", grid=(tp // tm,),
        in_specs=[pl.BlockSpec((tm, d), row), pl.BlockSpec((tm, aw), row), pl.BlockSpec((tm, rw), row),
                  pl.BlockSpec((1, aw), fix), pl.BlockSpec((1, rw), fix), pl.BlockSpec((d, d), fix)],
        out_specs=[pl.BlockSpec((tm, aw), row), pl.BlockSpec((tm, rw), row),
                   pl.BlockSpec((1, aw), fix), pl.BlockSpec((1, rw), fix)],
        out_shape=[jax.ShapeDtypeStruct((tp, aw), F32), jax.ShapeDtypeStruct((tp, rw), F32),
                   jax.ShapeDtypeStruct((1, aw), F32), jax.ShapeDtypeStruct((1, rw), F32)],
        compiler_params=_cparams("arbitrary"),
    )(dh2, o, rec, ga, gr, wout)


def rec_bwd(drec, hr, xc, rest, convw, convb, wga, bga, wgx, bgx, lru, rw):
    tp = rest.shape[0]
    ng = rw // LANES

    def body(drec_ref, hr_ref, xc_ref, xr_ref, yr_ref, cw_ref, cb_ref, wga_ref, bga_ref, wgx_ref, bgx_ref, l_ref,
             dxr_ref, dyr_ref, dwga_ref, dwgx_ref, vec_ref, a_s, u_s, lam_s):
        xc = xc_ref[...]
        h = hr_ref[...]
        yr = yr_ref[...]
        drec = drec_ref[...]
        xcb, r, ig, ls, log_a, a, mult = _gates(xc, wga_ref, bga_ref, wgx_ref, bgx_ref, l_ref)
        dyr_ref[...] = (drec * h * _gelu_grad(yr)).astype(BF16)
        a_s[...] = _shift_up(a, 1, tp)
        u_s[...] = drec * _gelu(yr)
        _scan_rows(a_s, u_s, lam_s, tp, reverse=True)
        lam = lam_s[...]
        da = lam * _shift_down(h, 1, tp)
        dmult = lam * ig * xc
        dig = lam * mult * xc
        dxc = lam * mult * ig
        a2 = jnp.exp(2.0 * log_a)
        dlog_a = da * a - dmult * a2 / mult
        dr = dlog_a * (RG_C * ls)
        dl = jnp.sum(dlog_a * (RG_C * r), axis=0, keepdims=True) * _sigmoid(-l_ref[...])
        dpa = dr * r * (1.0 - r)
        dpx = dig * ig * (1.0 - ig)
        dpab = dpa.astype(BF16)
        dpxb = dpx.astype(BF16)
        dxc = dxc + _dot_nt(dpab, wga_ref[...]) + _dot_nt(dpxb, wgx_ref[...])
        dwga_ref[...] = _dot_tn(xcb, dpab)
        dwgx_ref[...] = _dot_tn(xcb, dpxb)
        xr = xr_ref[...]
        dxr = cw_ref[CONV_WIDTH - 1:CONV_WIDTH, :] * dxc
        for k in range(1, CONV_WIDTH):
            dxr = dxr + cw_ref[CONV_WIDTH - 1 - k:CONV_WIDTH - k, :] * _shift_up(dxc, k, tp)
        dxr_ref[...] = dxr.astype(BF16)
        for k in range(CONV_WIDTH):
            vec_ref[k:k + 1, :] = jnp.sum(dxc * _shift_down(xr, CONV_WIDTH - 1 - k, tp), axis=0, keepdims=True)
        vec_ref[4:5, :] = jnp.sum(dxc, axis=0, keepdims=True)
        vec_ref[5:6, :] = jnp.sum(dpa, axis=0, keepdims=True)
        vec_ref[6:7, :] = jnp.sum(dpx, axis=0, keepdims=True)
        vec_ref[7:8, :] = dl

    col = lambda g: (0, g)
    vec = pl.BlockSpec((1, LANES), col)
    big = pl.BlockSpec((tp, LANES), col)
    sq = pl.BlockSpec((None, LANES, LANES), lambda g: (g, 0, 0))
    return pl.pallas_call(
        body, name="rec_bwd", grid=(ng,),
        in_specs=[big, big, big, big, pl.BlockSpec((tp, LANES), lambda g: (0, ng + g)),
                  pl.BlockSpec((CONV_WIDTH, LANES), col), vec, sq, vec, sq, vec, vec],
        out_specs=[big, big, sq, sq, pl.BlockSpec((None, SUBLANES, LANES), lambda g: (g, 0, 0))],
        out_shape=[jax.ShapeDtypeStruct((tp, rw), BF16), jax.ShapeDtypeStruct((tp, rw), BF16),
                   jax.ShapeDtypeStruct((ng, LANES, LANES), F32), jax.ShapeDtypeStruct((ng, LANES, LANES), F32),
                   jax.ShapeDtypeStruct((ng, SUBLANES, LANES), F32)],
        scratch_shapes=[pltpu.VMEM((tp, LANES), F32)] * 3,
        compiler_params=_cparams("parallel"),
    )(drec, hr, xc, rest, rest, convw, convb, wga, bga, wgx, bgx, lru)


def attn_bwd(qkv, do, o, lset, c, ct, nh):
    tp = qkv.shape[0]
    npair = nh // 2
    aw = nh * HEAD_DIM
    tiles = _att_tiles(tp)

    def body(q_ref, k_ref, v_ref, do_ref, o_ref, lset_ref, c_ref, ct_ref,
             dq_ref, dk_ref, dv_ref, drow_ref, dcol_ref, dk_acc, dv_acc):
        p = pl.program_id(0)
        dk_acc[...] = jnp.zeros_like(dk_acc)
        dv_acc[...] = jnp.zeros_like(dv_acc)
        dcol_ref[...] = jnp.zeros_like(dcol_ref)
        drow_ref[...] = jnp.zeros_like(drow_ref)
        for r0, nr, nk in tiles:
            rs = slice(r0, r0 + nr)
            causal = (r0 + lax.broadcasted_iota(jnp.int32, (nk, nr), 1)
                      >= lax.broadcasted_iota(jnp.int32, (nk, nr), 0))
            cblk = c_ref[0:nk, :]
            ctb = ct_ref[:, rs]
            for hh in range(2):
                head = 2 * p + hh
                hs = slice(hh * HEAD_DIM, (hh + 1) * HEAD_DIM)
                q = q_ref[rs, hs]
                k = k_ref[0:nk, hs]
                dof = do_ref[rs, hs]
                do16 = dof.astype(BF16)
                delta = jnp.sum(dof * o_ref[rs, hs], axis=1, keepdims=True)
                delta_row = jnp.broadcast_to(delta, (nr, LANES)).T[0:1, :]
                s_t = _dot_nt(k, q * ATT_SCALE) + (_pick_row(ctb, head) - _pick_col(cblk, head))
                p_t = jnp.where(causal, jnp.exp(s_t - lset_ref[hh:hh + 1, rs]), 0.0)
                ds_t = p_t * (_dot_nt(v_ref[0:nk, hs], do16) - delta_row)
                p16 = p_t.astype(BF16)
                ds16 = ds_t.astype(BF16)
                dv_acc[0:nk, hs] += _dot(p16, do16)
                dk_acc[0:nk, hs] += _dot(ds16, q) * ATT_SCALE
                dq_ref[rs, hs] = (_dot_tn(ds16, k) * ATT_SCALE).astype(BF16)
                drow_ref[hh:hh + 1, rs] = jnp.sum(ds_t, axis=0, keepdims=True)
                dcol_ref[0:nk, hs] -= jnp.broadcast_to(jnp.sum(ds_t, axis=1, keepdims=True), (nk, HEAD_DIM))
        dk_ref[...] = dk_acc[...].astype(BF16)
        dv_ref[...] = dv_acc[...].astype(BF16)

    pair = lambda p: (0, p)
    return pl.pallas_call(
        body, name="attn_bwd", grid=(npair,),
        in_specs=[pl.BlockSpec((tp, LANES), pair),
                  pl.BlockSpec((tp, LANES), lambda p: (0, npair + p)),
                  pl.BlockSpec((tp, LANES), lambda p: (0, 2 * npair + p)),
                  pl.BlockSpec((tp, LANES), pair),
                  pl.BlockSpec((tp, LANES), pair),
                  pl.BlockSpec((None, SUBLANES, tp), lambda p: (p, 0, 0)),
                  pl.BlockSpec((tp, LANES), lambda p: (0, 0)),
                  pl.BlockSpec((SUBLANES, tp), lambda p: (0, 0))],
        out_specs=[pl.BlockSpec((tp, LANES), pair), pl.BlockSpec((tp, LANES), pair),
                   pl.BlockSpec((tp, LANES), pair),
                   pl.BlockSpec((None, SUBLANES, tp), lambda p: (p, 0, 0)),
                   pl.BlockSpec((tp, LANES), pair)],
        out_shape=[jax.ShapeDtypeStruct((tp, aw), BF16), jax.ShapeDtypeStruct((tp, aw), BF16),
                   jax.ShapeDtypeStruct((tp, aw), BF16),
                   jax.ShapeDtypeStruct((npair, SUBLANES, tp), F32),
                   jax.ShapeDtypeStruct((tp, aw), F32)],
        scratch_shapes=[pltpu.VMEM((tp, LANES), F32), pltpu.VMEM((tp, LANES), F32)],
        compiler_params=_cparams("parallel"),
    )(qkv, qkv, qkv, do, o, lset, c, ct)


def fgate_bwd(dct8, drs, rest, bf_pad, fcol):
    tp = rest.shape[0]
    aw = drs.shape[1]
    nb = tp // ATT_BLOCK
    B = ATT_BLOCK

    def body(d_ref, drs_ref, f_ref, b_ref, dfl_ref, db_ref, pad_s):
        r_i = lax.broadcasted_iota(jnp.int32, (B, B), 0)
        c_i = lax.broadcasted_iota(jnp.int32, (B, B), 1)
        triu = (c_i >= r_i).astype(BF16)
        sel = (lax.broadcasted_iota(jnp.int32, (aw, LANES), 0)
               == HEAD_DIM * lax.broadcasted_iota(jnp.int32, (aw, LANES), 1)).astype(BF16)
        carry = jnp.zeros((1, LANES), F32)
        db = jnp.zeros((1, LANES), F32)
        pad_s[...] = jnp.zeros_like(pad_s)
        for i in range(nb - 1, -1, -1):
            sl = slice(i * B, (i + 1) * B)
            pad_s[0:SUBLANES, :] = d_ref[:, sl]
            dc = pad_s[...].T + _dot_split3(drs_ref[sl, :], sel)
            rc = _split3_dot(triu, dc)
            dlf = rc + carry
            carry = carry + rc[0:1, :]
            dfl = dlf * _sigmoid(-(f_ref[sl, :] + b_ref[...]))
            dfl_ref[sl, :] = dfl.astype(BF16)
            db = db + jnp.sum(dfl, axis=0, keepdims=True)
        db_ref[...] = db

    return pl.pallas_call(
        body, name="fgate_bwd", grid=(1,),
        in_specs=[pl.BlockSpec((SUBLANES, tp), lambda i: (0, 0)),
                  pl.BlockSpec((tp, aw), lambda i: (0, 0)),
                  pl.BlockSpec((tp, LANES), lambda i: (0, fcol)),
                  pl.BlockSpec((1, LANES), lambda i: (0, 0))],
        out_specs=[pl.BlockSpec((tp, LANES), lambda i: (0, 0)),
                   pl.BlockSpec((1, LANES), lambda i: (0, 0))],
        out_shape=[jax.ShapeDtypeStruct((tp, LANES), BF16), jax.ShapeDtypeStruct((1, LANES), F32)],
        scratch_shapes=[pltpu.VMEM((B, B), F32)],
        compiler_params=_cparams("arbitrary"),
    )(dct8, drs, rest, bf_pad)


def in_proj_bwd(dh2, parts, w_in_t, wrest_t, h, g1, tm):
    tp, d = h.shape
    dq, dk, dv, dxr, dyr, dfl = parts
    aw, rw = dq.shape[1], dxr.shape[1]

    def body(dh2_ref, dq_ref, dk_ref, dv_ref, dxr_ref, dyr_ref, dfl_ref, wq_ref, wr_ref, h_ref, g_ref,
             dh_ref, dg_ref):
        i = pl.program_id(0)
        dz = _dot(dq_ref[...], wq_ref[0:aw, :])
        dz += _dot(dk_ref[...], wq_ref[aw:2 * aw, :])
        dz += _dot(dv_ref[...], wq_ref[2 * aw:3 * aw, :])
        dz += _dot(dxr_ref[...], wr_ref[0:rw, :])
        dz += _dot(dyr_ref[...], wr_ref[rw:2 * rw, :])
        dz += _dot(dfl_ref[...], wr_ref[2 * rw:2 * rw + LANES, :])
        dx, dg = _rms_bwd(h_ref[...], g_ref[...], dz)
        dh_ref[...] = dh2_ref[...] + dx
        _accumulate(dg_ref, dg, i == 0)

    row = lambda i: (i, 0)
    fix = lambda i: (0, 0)
    return pl.pallas_call(
        body, name="in_proj_bwd", grid=(tp // tm,),
        in_specs=[pl.BlockSpec((tm, d), row),
                  pl.BlockSpec((tm, aw), row), pl.BlockSpec((tm, aw), row), pl.BlockSpec((tm, aw), row),
                  pl.BlockSpec((tm, rw), row), pl.BlockSpec((tm, rw), row), pl.BlockSpec((tm, LANES), row),
                  pl.BlockSpec((3 * aw, d), fix), pl.BlockSpec(wrest_t.shape, fix),
                  pl.BlockSpec((tm, d), row), pl.BlockSpec((1, d), fix)],
        out_specs=[pl.BlockSpec((tm, d), row), pl.BlockSpec((1, d), fix)],
        out_shape=[jax.ShapeDtypeStruct((tp, d), F32), jax.ShapeDtypeStruct((1, d), F32)],
        compiler_params=_cparams("arbitrary"),
    )(dh2, dq, dk, dv, dxr, dyr, dfl, w_in_t, wrest_t, h, g1)


def dw_in_t(z, parts, nh, tr):
    tp, d = z.shape
    dq, dk, dv, dxr, dyr, dfl = parts
    aw, rw = dq.shape[1], dxr.shape[1]
    d_in = 3 * aw + nh + 2 * rw
    nr = tp // tr
    offs = [(0, aw), (aw, aw), (2 * aw, aw), (3 * aw + nh, rw), (3 * aw + nh + rw, rw)]

    def body(z_ref, dq_ref, dk_ref, dv_ref, dxr_ref, dyr_ref, dfl_ref, o_ref, acc):
        r = pl.program_id(0)

        @pl.when(r == 0)
        def _():
            acc[...] = jnp.zeros_like(acc)

        zt = z_ref[...]
        for (o, n), ref in zip(offs, (dq_ref, dk_ref, dv_ref, dxr_ref, dyr_ref)):
            acc[o:o + n, :] += _dot_tn(ref[...], zt)
        acc[3 * aw:3 * aw + nh, :] += _dot_tn(dfl_ref[...], zt)[0:nh, :]

        @pl.when(r == nr - 1)
        def _():
            o_ref[...] = acc[...].astype(BF16)

    row = lambda r: (r, 0)
    return pl.pallas_call(
        body, name="dw_in", grid=(nr,),
        in_specs=[pl.BlockSpec((tr, d), row),
                  pl.BlockSpec((tr, aw), row), pl.BlockSpec((tr, aw), row), pl.BlockSpec((tr, aw), row),
                  pl.BlockSpec((tr, rw), row), pl.BlockSpec((tr, rw), row), pl.BlockSpec((tr, LANES), row)],
        out_specs=pl.BlockSpec((d_in, d), lambda r: (0, 0)),
        out_shape=jax.ShapeDtypeStruct((d_in, d), BF16),
        scratch_shapes=[pltpu.VMEM((d_in, d), F32)],
        compiler_params=_cparams("arbitrary"),
    )(z, dq, dk, dv, dxr, dyr, dfl)


def _place():
    return lax.axis_index("x"), lax.axis_index("y"), lax.axis_index("c")


HBM = pl.BlockSpec(memory_space=pltpu.HBM)
SEM = pl.BlockSpec(memory_space=pltpu.SEMAPHORE)
EFFECT = pltpu.SideEffectType.DATAFLOW_SIDE_EFFECTING


def _in_hbm(a):
    return pltpu.with_memory_space_constraint(a, pltpu.HBM)


def _as_list(a):
    return list(a) if isinstance(a, (list, tuple)) else [a]


def _gather_targets(x, y, c):
    return [(x, y, 1 - c), (1 - x, y, c), (x, 1 - y, c), (1 - x, 1 - y, c)]


def _slot(t):
    return 4 * t[0] + 2 * t[1] + t[2]


def gather_start(groups, name):
    flat = [a for g in groups for a in g]
    n = len(flat)
    ng = len(groups)
    lands = [lax.empty((N_DEV,) + a.shape, a.dtype) for a in flat]

    def body(*refs):
        src, land = refs[:n], refs[n:2 * n]
        sems = refs[2 * n:2 * n + 2 * ng]
        token = refs[-1]
        x, y, c = _place()
        me = 4 * x + 2 * y + c
        i = 0
        for gi, g in enumerate(groups):
            for a in range(len(g)):
                for k, t in enumerate(_gather_targets(x, y, c)):
                    pltpu.make_async_remote_copy(
                        src_ref=src[i], dst_ref=land[i].at[me],
                        send_sem=sems[2 * gi].at[4 * a + k], recv_sem=sems[2 * gi + 1].at[4 * a + k],
                        device_id=t, device_id_type=MESH).start()
                i += 1
        token[...] = jnp.zeros_like(token)

    sem_shapes = []
    for g in groups:
        sem_shapes += [pltpu.SemaphoreType.DMA((4 * len(g),)), pltpu.SemaphoreType.DMA((4 * len(g),))]
    out = pl.pallas_call(
        body, name=name,
        out_shape=sem_shapes + [pltpu.HBM(a.shape, a.dtype) for a in flat + lands]
        + [jax.ShapeDtypeStruct((SUBLANES, LANES), F32)],
        in_specs=[HBM] * (2 * n),
        out_specs=[SEM] * (2 * ng) + [HBM] * (2 * n) + [pl.BlockSpec(memory_space=pltpu.VMEM)],
        input_output_aliases={i: 2 * ng + i for i in range(2 * n)},
        compiler_params=pltpu.CompilerParams(has_side_effects=EFFECT),
    )(*[_in_hbm(a) for a in flat + lands])
    sems = out[:2 * ng]
    thru = out[2 * ng:2 * ng + 2 * n]
    srcs_t, lands_t = thru[:n], thru[n:]
    res, i = [], 0
    for gi, g in enumerate(groups):
        res.append((sems[2 * gi], sems[2 * gi + 1], srcs_t[i:i + len(g)], lands_t[i:i + len(g)]))
        i += len(g)
    return res, out[-1]


def gather_wait(send, recv, srcs, lands, after, name):
    n = len(srcs)

    def body(*refs):
        src, land = refs[:n], refs[n:2 * n]
        send_sem, recv_sem = refs[2 * n], refs[2 * n + 1]
        x, y, c = _place()
        for a in range(n):
            for k, t in enumerate(_gather_targets(x, y, c)):
                cp = pltpu.make_async_remote_copy(
                    src_ref=src[a], dst_ref=land[a].at[_slot(t)],
                    send_sem=send_sem.at[4 * a + k], recv_sem=recv_sem.at[4 * a + k],
                    device_id=t, device_id_type=MESH)
                cp.wait_send()
                cp.wait_recv()

    out = pl.pallas_call(
        body, name=name,
        out_shape=[pltpu.HBM(a.shape, a.dtype) for a in list(srcs) + list(lands)],
        in_specs=[HBM] * (2 * n) + [SEM, SEM] + [ANY] * len(_as_list(after)),
        out_specs=[HBM] * (2 * n),
        input_output_aliases={i: i for i in range(2 * n)},
        compiler_params=pltpu.CompilerParams(has_side_effects=EFFECT),
    )(*srcs, *lands, send, recv, *_as_list(after))
    return out[:n], out[n:]


def forward_start(lands, name):
    n = len(lands)

    def body(*refs):
        land = refs[:n]
        send_sem, recv_sem = refs[n], refs[n + 1]
        token = refs[-1]
        x, y, c = _place()
        for a in range(n):
            for j, chip in enumerate([(1 - x, y), (x, 1 - y), (1 - x, 1 - y)]):
                blk = land[a].at[_slot((*chip, c))]
                pltpu.make_async_remote_copy(src_ref=blk, dst_ref=blk, send_sem=send_sem.at[3 * a + j],
                                             recv_sem=recv_sem.at[3 * a + j], device_id=(x, y, 1 - c),
                                             device_id_type=MESH).start()
        token[...] = jnp.zeros_like(token)

    out = pl.pallas_call(
        body, name=name,
        out_shape=[pltpu.SemaphoreType.DMA((3 * n,)), pltpu.SemaphoreType.DMA((3 * n,))]
        + [pltpu.HBM(a.shape, a.dtype) for a in lands] + [jax.ShapeDtypeStruct((SUBLANES, LANES), F32)],
        in_specs=[HBM] * n,
        out_specs=[SEM, SEM] + [HBM] * n + [pl.BlockSpec(memory_space=pltpu.VMEM)],
        input_output_aliases={i: 2 + i for i in range(n)},
        compiler_params=pltpu.CompilerParams(has_side_effects=EFFECT),
    )(*[_in_hbm(a) for a in lands])
    return out[0], out[1], out[2:2 + n], out[-1][0, 0]


def forward_wait(send, recv, lands, after, name):
    n = len(lands)

    def body(*refs):
        land = refs[:n]
        send_sem, recv_sem = refs[n], refs[n + 1]
        x, y, c = _place()
        for a in range(n):
            for j, chip in enumerate([(1 - x, y), (x, 1 - y), (1 - x, 1 - y)]):
                cp = pltpu.make_async_remote_copy(
                    src_ref=land[a].at[_slot((*chip, c))], dst_ref=land[a].at[_slot((*chip, 1 - c))],
                    send_sem=send_sem.at[3 * a + j], recv_sem=recv_sem.at[3 * a + j],
                    device_id=(x, y, 1 - c), device_id_type=MESH)
                cp.wait_send()
                cp.wait_recv()

    return pl.pallas_call(
        body, name=name,
        out_shape=[pltpu.HBM(a.shape, a.dtype) for a in lands],
        in_specs=[HBM] * n + [SEM, SEM, ANY],
        out_specs=[HBM] * n,
        input_output_aliases={i: i for i in range(n)},
        compiler_params=pltpu.CompilerParams(has_side_effects=EFFECT),
    )(*lands, send, recv, after)


def _relations():
    return [(dx, dy, dc) for dx in (0, 1) for dy in (0, 1) for dc in (0, 1) if dx + dy + dc]


def _peer(x, y, c, rel):
    return ((1 - x) if rel[0] else x, (1 - y) if rel[1] else y, (1 - c) if rel[2] else c)


def exchange_start(srcs, lands, layer, name):
    n = len(srcs)

    def body(*refs):
        src, land = refs[:n], refs[n:2 * n]
        send_sem, recv_sem = refs[2 * n], refs[2 * n + 1]
        token = refs[-1]
        x, y, c = _place()
        me = 4 * x + 2 * y + c
        for k, rel in enumerate(_relations()):
            peer = _peer(x, y, c, rel)
            for a in range(n):
                pltpu.make_async_remote_copy(
                    src_ref=src[a] if layer is None else src[a].at[_slot(peer)],
                    dst_ref=land[a].at[me] if layer is None else land[a].at[me, layer],
                    send_sem=send_sem.at[7 * a + k], recv_sem=recv_sem.at[7 * a + k],
                    device_id=peer, device_id_type=MESH).start()
        token[...] = jnp.zeros_like(token)

    out = pl.pallas_call(
        body, name=name,
        out_shape=[pltpu.SemaphoreType.DMA((7 * n,)), pltpu.SemaphoreType.DMA((7 * n,))]
        + [pltpu.HBM(a.shape, a.dtype) for a in list(srcs) + list(lands)]
        + [jax.ShapeDtypeStruct((SUBLANES, LANES), F32)],
        in_specs=[HBM] * (2 * n),
        out_specs=[SEM, SEM] + [HBM] * (2 * n) + [pl.BlockSpec(memory_space=pltpu.VMEM)],
        input_output_aliases={i: 2 + i for i in range(2 * n)},
        compiler_params=pltpu.CompilerParams(has_side_effects=EFFECT),
    )(*[_in_hbm(a) for a in list(srcs) + list(lands)])
    return out[0], out[1], out[2:2 + n], out[2 + n:2 + 2 * n], out[-1][0, 0]


def exchange_wait(send, recv, srcs, lands, after, layer, name):
    n = len(srcs)

    def body(*refs):
        src, land = refs[:n], refs[n:2 * n]
        send_sem, recv_sem = refs[2 * n], refs[2 * n + 1]
        x, y, c = _place()
        for k, rel in enumerate(_relations()):
            peer = _peer(x, y, c, rel)
            for a in range(n):
                cp = pltpu.make_async_remote_copy(
                    src_ref=src[a] if layer is None else src[a].at[_slot(peer)],
                    dst_ref=land[a].at[_slot(peer)] if layer is None else land[a].at[_slot(peer), layer],
                    send_sem=send_sem.at[7 * a + k], recv_sem=recv_sem.at[7 * a + k],
                    device_id=peer, device_id_type=MESH)
                cp.wait_send()
                cp.wait_recv()

    out = pl.pallas_call(
        body, name=name,
        out_shape=[pltpu.HBM(a.shape, a.dtype) for a in list(srcs) + list(lands)],
        in_specs=[HBM] * (2 * n) + [SEM, SEM] + [ANY] * len(_as_list(after)),
        out_specs=[HBM] * (2 * n),
        input_output_aliases={i: i for i in range(2 * n)},
        compiler_params=pltpu.CompilerParams(has_side_effects=EFFECT),
    )(*srcs, *lands, send, recv, *_as_list(after))
    return out[:n], out[n:]


def _adamw_math(g, w, m, v):
    m = ADAM_B1 * m + (1.0 - ADAM_B1) * g
    v = ADAM_B2 * v + (1.0 - ADAM_B2) * (g * g)
    m_hat = m / (1.0 - ADAM_B1 ** ADAM_STEP)
    v_hat = v / (1.0 - ADAM_B2 ** ADAM_STEP)
    delta = -ADAM_LR * (m_hat / (jnp.sqrt(v_hat) + ADAM_EPS) + ADAM_WD * w)
    return delta, m, v


def sum_adamw(parts, w, m, v, tr, name):
    npart, rows, cols = parts.shape

    def body(p_ref, w_ref, m_ref, v_ref, g_ref, d_ref, nm_ref, nv_ref):
        g = p_ref[0].astype(F32)
        for p in range(1, npart):
            g = g + p_ref[p].astype(F32)
        delta, nm, nv = _adamw_math(g, w_ref[...], m_ref[...], v_ref[...])
        g_ref[...] = g
        d_ref[...] = delta
        nm_ref[...] = nm
        nv_ref[...] = nv

    blk = pl.BlockSpec((tr, cols), lambda i: (i, 0))
    return pl.pallas_call(
        body, name=name, grid=(rows // tr,),
        in_specs=[pl.BlockSpec((npart, tr, cols), lambda i: (0, i, 0)), blk, blk, blk],
        out_specs=[blk] * 4,
        out_shape=[jax.ShapeDtypeStruct((rows, cols), F32)] * 4,
        compiler_params=_cparams("parallel"),
    )(parts, w, m, v)


def sum_adamw_t(parts, w, m, v, name):
    npart, nl, rows, cols = parts.shape

    def body(p_ref, w_ref, m_ref, v_ref, g_ref, d_ref, nm_ref, nv_ref):
        g = p_ref[0].astype(F32)
        for p in range(1, npart):
            g = g + p_ref[p].astype(F32)
        delta, nm, nv = _adamw_math(g, w_ref[...], m_ref[...], v_ref[...])
        g_ref[...] = g
        d_ref[...] = delta
        nm_ref[...] = nm
        nv_ref[...] = nv

    blk = pl.BlockSpec((None, rows, cols), lambda l: (l, 0, 0))
    return pl.pallas_call(
        body, name=name, grid=(nl,),
        in_specs=[pl.BlockSpec((npart, None, rows, cols), lambda l: (0, l, 0, 0)), blk, blk, blk],
        out_specs=[blk] * 4,
        out_shape=[jax.ShapeDtypeStruct((nl, rows, cols), F32)] * 4,
        compiler_params=_cparams("parallel"),
    )(parts, w, m, v)


def sum_parts(parts, name):
    npart, rows, cols = parts.shape

    def body(p_ref, g_ref):
        g = p_ref[0].astype(F32)
        for p in range(1, npart):
            g = g + p_ref[p].astype(F32)
        g_ref[...] = g

    return pl.pallas_call(
        body, name=name, grid=(1,),
        in_specs=[pl.BlockSpec((npart, rows, cols), lambda i: (0, 0, 0))],
        out_specs=pl.BlockSpec((rows, cols), lambda i: (0, 0)),
        out_shape=jax.ShapeDtypeStruct((rows, cols), F32),
        compiler_params=_cparams("arbitrary"),
    )(parts)


def _round_up(n, m):
    return (n + m - 1) // m * m


def _block_diag_pairs(w):
    nb, b, _ = w.shape
    per = LANES // b
    ng = nb // per
    w = w.reshape(ng, per, b, b)
    eye = jnp.eye(per, dtype=w.dtype)
    out = jnp.einsum('gpij,pq->gpiqj', w, eye).reshape(ng, LANES, LANES)
    return out.astype(BF16)


def _block_diag_extract(g, b):
    ng = g.shape[0]
    per = LANES // b
    g = g.reshape(ng, per, b, per, b)
    idx = jnp.arange(per)
    return g[:, idx, :, idx, :].transpose(1, 0, 2, 3).reshape(ng * per, b, b)


def _tiles(v):
    v = v.reshape(-1)
    n = _round_up(v.shape[0], SUBLANES * LANES)
    return jnp.pad(v, (0, n - v.shape[0])).reshape(-1, LANES)


SMALL = ['attn_norm_g', 'b_f', 'conv_w', 'conv_b', 'w_gate_a', 'b_gate_a', 'w_gate_x', 'b_gate_x',
         'lru_L', 'attn_out_g', 'rec_out_g', 'mlp_norm_g', 'final_g', 'meta']


def _pack(d):
    return jnp.concatenate([_tiles(d[n]) for n in SMALL], axis=0)


def _unpack(vec, shapes):
    out, r = {}, 0
    for n in SMALL:
        size = math.prod(shapes[n])
        nr = _round_up(size, SUBLANES * LANES) // LANES
        out[n] = vec[r:r + nr].reshape(-1)[:size].reshape(shapes[n])
        r += nr
    return out


def _row_tile(tp):
    return tp // 4 if (tp // 4) % 16 == 0 else tp


def local_step(x, tgt, meta, small, hooks):
    s, d = x.shape
    t_real = s + N_META
    tp = _round_up(t_real, ATT_BLOCK)
    depth = small['attn_norm_g'].shape[0]
    nh = small['b_f'].shape[1]
    rw = small['conv_b'].shape[1]
    blk = small['w_gate_a'].shape[2]
    tm = _row_tile(tp)
    tm2 = tp // 2
    fcol = 2 * rw // LANES

    h = jnp.concatenate([meta, x, jnp.zeros((tp - t_real, d), F32)], axis=0)
    tgt_p = jnp.pad(tgt, ((N_META, tp - t_real), (0, 0)))
    row = lambda v: v.reshape(1, -1)
    bf_pad = jnp.pad(small['b_f'], ((0, 0), (0, LANES - nh)))

    saved = []
    for l in range(depth):
        w_in_t, wrest_t, wout, tok_w = hooks.mixer_weights(l, h)
        wga = _block_diag_pairs(small['w_gate_a'][l])
        wgx = _block_diag_pairs(small['w_gate_x'][l])
        z, qkv, rest = in_proj(h, row(small['attn_norm_g'][l]) + tok_w, w_in_t, wrest_t, 3 * nh * HEAD_DIM, tm)
        c, ct = fgate_fwd(rest, bf_pad[l:l + 1], fcol)
        o, lset = attn_fwd(qkv, c, ct, nh)
        rec, hr, xc = rec_fwd(rest, small['conv_w'][l], row(small['conv_b'][l]), wga, row(small['b_gate_a'][l]),
                              wgx, row(small['b_gate_x'][l]), row(small['lru_L'][l]), rw)
        gup, gdown, tok_w = hooks.mlp_weights(l, rec)
        h2, mix, z2 = out_proj(h, o, rec, row(small['attn_out_g'][l]), row(small['rec_out_g'][l]), wout,
                               row(small['mlp_norm_g'][l]) + tok_w, tm)
        u, h3 = mlp_fwd(z2, h2, gup, gdown, tm2)
        saved.append(dict(h=h, z=z, qkv=qkv, rest=rest, c=c, ct=ct, o=o, lset=lset, rec=rec, hr=hr, xc=xc,
                          h2=h2, mix=mix, z2=z2, u=u, wga=wga, wgx=wgx,
                          w_in_t=w_in_t, wrest_t=wrest_t, wout=wout, gup=gup, gdown=gdown))
        h = h3

    dh, dgf, loss = loss_head(h, row(small['final_g']), tgt_p, t_real, tm)

    gs = {n: [None] * depth for n in SMALL if n not in ('final_g', 'meta')}
    tok = jnp.zeros((), F32)
    for l in reversed(range(depth)):
        sv = saved[l]
        gup, gdown = sv['gup'], sv['gdown']
        tf = gup.shape[2]
        dup, dh2, dg2, dhb = mlp_bwd(dh, sv['u'], sv['h2'], row(small['mlp_norm_g'][l]) + tok, gup, gdown, tm)
        gs['mlp_norm_g'][l] = dg2[0]
        do, drec, dga, dgr = out_proj_bwd(dh2, sv['o'], sv['rec'], row(small['attn_out_g'][l]),
                                          row(small['rec_out_g'][l]), sv['wout'], tm)
        gs['attn_out_g'][l] = dga[0]
        gs['rec_out_g'][l] = dgr[0]
        blocks = dict(
            w_down=mm_tn(sv['u'], dhb, tk=tf, tn=d, out_dtype=BF16, name="dw_down",
                         square_a=True).reshape(N_DEV, tf, d),
            w_up=mm_tn(sv['z2'], dup, tk=d, tn=tf, out_dtype=BF16, name="dw_up", blocked_n=True),
            w_out=mm_tn(sv['mix'], dh2, tk=d, tn=d // 2, out_dtype=BF16,
                        name="dw_out").reshape(N_DEV, d // N_DEV, d))
        tok = hooks.grads_ready(l, 'mlp', blocks)
        dxr, dyr, dwga, dwgx, vec = rec_bwd(drec, sv['hr'], sv['xc'], sv['rest'], small['conv_w'][l],
                                            row(small['conv_b'][l]) + tok, sv['wga'], row(small['b_gate_a'][l]),
                                            sv['wgx'], row(small['b_gate_x'][l]), row(small['lru_L'][l]), rw)
        gs['w_gate_a'][l] = _block_diag_extract(dwga, blk)
        gs['w_gate_x'][l] = _block_diag_extract(dwgx, blk)
        vec = vec.transpose(1, 0, 2).reshape(SUBLANES, rw)
        gs['conv_w'][l] = vec[0:CONV_WIDTH]
        gs['conv_b'][l] = vec[4]
        gs['b_gate_a'][l] = vec[5]
        gs['b_gate_x'][l] = vec[6]
        gs['lru_L'][l] = vec[7]
        dq, dk, dv, drow, dcol = attn_bwd(sv['qkv'], do, sv['o'], sv['lset'], sv['c'], sv['ct'] + tok, nh)
        drow8 = drow[:, 0:2, :].reshape(nh, tp)
        if nh < SUBLANES:
            drow8 = jnp.pad(drow8, ((0, SUBLANES - nh), (0, 0)))
        dfl, dbf = fgate_bwd(drow8, dcol, sv['rest'], bf_pad[l:l + 1], fcol)
        gs['b_f'][l] = dbf[0, 0:nh]
        parts = (dq, dk, dv, dxr, dyr, dfl)
        dh, dg1 = in_proj_bwd(dh2, parts, sv['w_in_t'], sv['wrest_t'], sv['h'], row(small['attn_norm_g'][l]), tm)
        gs['attn_norm_g'][l] = dg1[0]
        tok = jnp.zeros((), F32)
        if l == 0:
            grads = {n: jnp.stack(v) for n, v in gs.items()}
            grads['final_g'] = dgf[0]
            grads['meta'] = dh[0:N_META]
            tok = hooks.small_ready(grads)
        dw_in = dw_in_t(sv['z'], parts, nh, tm2)
        dw_in = dw_in.reshape(N_DEV, dw_in.shape[0] // N_DEV, d) + tok.astype(BF16)
        tok = hooks.grads_ready(l, 'in', dict(w_in=dw_in))

    return loss[0, 0], dh, tok


def prep_weights(g_in, g_out, nh, rw):
    d = g_in.shape[2]
    w_in_t = g_in.reshape(-1, d)
    f0 = 3 * nh * HEAD_DIM
    wrest_t = jnp.concatenate([w_in_t[f0 + nh:f0 + nh + 2 * rw],
                               jnp.pad(w_in_t[f0:f0 + nh], ((0, LANES - nh), (0, 0)))], axis=0)
    return w_in_t, wrest_t, g_out.reshape(d, d)


BIG = ['w_in', 'w_out', 'w_up', 'w_down']
EXCHANGE_GROUPS = {'mlp': ['w_down', 'w_up', 'w_out'], 'in': ['w_in']}
WEIGHTS = ['meta', 'attn_norm_g', 'w_in', 'b_f', 'conv_w', 'conv_b', 'w_gate_a', 'b_gate_a', 'w_gate_x', 'b_gate_x',
           'lru_L', 'attn_out_g', 'rec_out_g', 'w_out', 'mlp_norm_g', 'w_up', 'w_down', 'final_g']


def _set_own(arr, own, me):
    return lax.dynamic_update_slice_in_dim(arr, own[None], me, axis=0)


class _Step:
    def __init__(self, w, nh, rw, me):
        self.w, self.nh, self.rw, self.me = w, nh, rw, me
        depth = w['w_in'].shape[0]
        first = [w['w_in_t'][:, 0, :].astype(BF16), w['w_out'][0].astype(BF16), w['meta'], w['conv_w']]
        self.pending, token = gather_start([first], "gather_start_0")
        zero = token[0, 0].astype(BF16)
        groups = [[w['w_up'][0].astype(BF16) + zero, w['w_down'][0].astype(BF16) + zero]]
        for l in range(1, depth):
            groups.append([w['w_in_t'][:, l, :].astype(BF16) + zero, w['w_out'][l].astype(BF16) + zero])
            groups.append([w['w_up'][l].astype(BF16) + zero, w['w_down'][l].astype(BF16) + zero])
        rest, _ = gather_start(groups, "gather_start_1")
        self.pending += rest
        self.gathered = {}
        self.passing = {}
        self.token = jnp.zeros((), F32)
        self.lands = {n: lax.empty((N_DEV,) + w[n].shape, BF16) for n in BIG}
        din8, _, d = w['w_in_t'].shape
        self.lands['w_in'] = lax.empty((N_DEV, depth, din8, d), BF16)
        self.started = []
        self.small = None

    def _pass_on(self, gi, after):
        if gi < len(self.pending) and gi not in self.passing:
            send, recv, srcs, lands = self.pending[gi]
            srcs, lands = gather_wait(send, recv, srcs, lands, after, "gather_wait_%d" % gi)
            fsend, frecv, lands, token = forward_start(lands, "forward_start_%d" % gi)
            self.passing[gi] = (fsend, frecv, srcs, lands)
            self.token = token

    def group(self, gi, after):
        if gi not in self.gathered:
            self._pass_on(gi, after)
            fsend, frecv, srcs, lands = self.passing[gi]
            lands = forward_wait(fsend, frecv, lands, after, "forward_wait_%d" % gi)
            self.gathered[gi] = [_set_own(g, own, self.me) for g, own in zip(lands, srcs)]
            if gi >= 2:
                self._pass_on(gi + 1, lands[0])
        return self.gathered[gi]

    def mixer_weights(self, l, after):
        g = self.group(2 * l, after)
        return (*prep_weights(g[0], g[1], self.nh, self.rw), self.token)

    def mlp_weights(self, l, after):
        g = self.group(2 * l + 1, after)
        return g[0], g[1], self.token

    def grads_ready(self, l, group, blocks):
        names = EXCHANGE_GROUPS[group]
        send, recv, srcs, lands, token = exchange_start(
            [blocks[n] for n in names], [self.lands[n] for n in names], l, "exchange_start_%s_%d" % (group, l))
        for n, a in zip(names, lands):
            self.lands[n] = a
        self.started.append((l, group, send, recv, srcs))
        return token

    def small_ready(self, grads):
        self.small_shapes = {n: grads[n].shape for n in SMALL}
        packed = _pack(grads).astype(BF16)
        send, recv, srcs, lands, token = exchange_start(
            [packed], [lax.empty((N_DEV,) + packed.shape, BF16)], None, "small_start")
        self.small = (send, recv, srcs, lands)
        return token

    def small_sum(self, after):
        send, recv, srcs, lands = self.small
        srcs, lands = exchange_wait(send, recv, srcs, lands, after, None, "small_wait")
        parts = _set_own(lands[0], srcs[0], self.me)
        return _unpack(sum_parts(parts, "sum_small_grads"), self.small_shapes)

    def received(self, group, after):
        names = EXCHANGE_GROUPS[group]
        own = {n: [None] * self.w[n].shape[0] for n in names}
        for l, grp, send, recv, srcs in self.started:
            if grp != group:
                continue
            srcs, lands = exchange_wait(send, recv, srcs, [self.lands[n] for n in names], after, l,
                                        "exchange_wait_%s_%d" % (group, l))
            for n, a, sr in zip(names, lands, srcs):
                self.lands[n] = a
                own[n][l] = lax.dynamic_index_in_dim(sr, self.me, 0, keepdims=False)
        return {n: _set_own(self.lands[n], jnp.stack(own[n]), self.me) for n in names}


def kernel(x, meta, attn_norm_g, w_in, b_f, conv_w, conv_b, w_gate_a, b_gate_a, w_gate_x, b_gate_x, lru_L, attn_out_g, rec_out_g, w_out, mlp_norm_g, w_up, w_down, final_g, loss_target, m_meta, m_attn_norm_g, m_w_in, m_b_f, m_conv_w, m_conv_b, m_w_gate_a, m_b_gate_a, m_w_gate_x, m_b_gate_x, m_lru_L, m_attn_out_g, m_rec_out_g, m_w_out, m_mlp_norm_g, m_w_up, m_w_down, m_final_g, v_meta, v_attn_norm_g, v_w_in, v_b_f, v_conv_w, v_conv_b, v_w_gate_a, v_b_gate_a, v_w_gate_x, v_b_gate_x, v_lru_L, v_attn_out_g, v_rec_out_g, v_w_out, v_mlp_norm_g, v_w_up, v_w_down, v_final_g):
    w = dict(meta=meta, attn_norm_g=attn_norm_g, w_in=w_in, b_f=b_f, conv_w=conv_w, conv_b=conv_b,
             w_gate_a=w_gate_a, b_gate_a=b_gate_a, w_gate_x=w_gate_x, b_gate_x=b_gate_x, lru_L=lru_L,
             attn_out_g=attn_out_g, rec_out_g=rec_out_g, w_out=w_out, mlp_norm_g=mlp_norm_g, w_up=w_up,
             w_down=w_down, final_g=final_g)
    mo = dict(meta=m_meta, attn_norm_g=m_attn_norm_g, w_in=m_w_in, b_f=m_b_f, conv_w=m_conv_w, conv_b=m_conv_b,
              w_gate_a=m_w_gate_a, b_gate_a=m_b_gate_a, w_gate_x=m_w_gate_x, b_gate_x=m_b_gate_x, lru_L=m_lru_L,
              attn_out_g=m_attn_out_g, rec_out_g=m_rec_out_g, w_out=m_w_out, mlp_norm_g=m_mlp_norm_g,
              w_up=m_w_up, w_down=m_w_down, final_g=m_final_g)
    vo = dict(meta=v_meta, attn_norm_g=v_attn_norm_g, w_in=v_w_in, b_f=v_b_f, conv_w=v_conv_w, conv_b=v_conv_b,
              w_gate_a=v_w_gate_a, b_gate_a=v_b_gate_a, w_gate_x=v_w_gate_x, b_gate_x=v_b_gate_x, lru_L=v_lru_L,
              attn_out_g=v_attn_out_g, rec_out_g=v_rec_out_g, w_out=v_w_out, mlp_norm_g=v_mlp_norm_g,
              w_up=v_w_up, w_down=v_w_down, final_g=v_final_g)
    depth = w_in.shape[0]
    nh = b_f.shape[1]
    rw = conv_b.shape[1]
    me = 4 * lax.axis_index("x") + 2 * lax.axis_index("y") + lax.axis_index("c")

    w['w_in_t'] = jnp.transpose(w_in, (2, 0, 1))
    swap = lambda a: jnp.swapaxes(a, 1, 2)
    step = _Step(w, nh, rw, me)
    g0 = step.group(0, meta)
    meta_full = g0[2].transpose(1, 0, 2).reshape(N_META, -1)
    conv_full = g0[3].transpose(1, 2, 0, 3).reshape(depth, CONV_WIDTH, rw)
    small = {n: w[n] for n in SMALL}
    small['conv_w'] = conv_full

    loss_part, dh0, tok = local_step(x[0], loss_target[0], meta_full, small, step)
    loss = lax.psum(loss_part, ("x", "y", "c"))
    grad_x = dh0[N_META:N_META + x.shape[1]][None]

    out_g, out_d, out_m, out_v = {}, {}, {}, {}

    def update_big(group, after):
        for n, r in step.received(group, after).items():
            if n == 'w_in':
                out = sum_adamw_t(r, swap(w[n]), swap(mo[n]), swap(vo[n]), "adamw_w_in")
                out = [swap(a) for a in out]
            else:
                shp = w[n].shape
                rows, cols = shp[0] * shp[1], shp[2]
                tr = min(512 if cols <= 512 else 256, rows)
                out = sum_adamw(r.reshape(N_DEV, rows, cols), w[n].reshape(rows, cols), mo[n].reshape(rows, cols),
                                vo[n].reshape(rows, cols), tr, "adamw_" + n)
                out = [a.reshape(shp) for a in out]
            out_g[n], out_d[n], out_m[n], out_v[n] = out
            after = out[0]
        return after

    update_big('mlp', step.started[-1][4][0])

    gsum = step.small_sum([out_g[n] for n in EXCHANGE_GROUPS['mlp']])
    gsum['meta'] = lax.dynamic_slice_in_dim(gsum['meta'], me * meta.shape[1], meta.shape[1], axis=1)
    gsum['conv_w'] = lax.dynamic_slice_in_dim(gsum['conv_w'], me * conv_w.shape[2], conv_w.shape[2], axis=2)
    packed = [_pack(t) for t in (gsum, {n: w[n] for n in SMALL}, {n: mo[n] for n in SMALL},
                                 {n: vo[n] for n in SMALL})]
    res = sum_adamw(packed[0][None], packed[1], packed[2], packed[3], packed[1].shape[0], "adamw_small")
    shapes = {n: w[n].shape for n in SMALL}
    for dst, vec in zip((out_g, out_d, out_m, out_v), res):
        dst.update(_unpack(vec, shapes))

    update_big('in', res[0])

    return (loss, grad_x, *[out_g[n] for n in WEIGHTS], *[out_d[n] for n in WEIGHTS],
            *[out_m[n] for n in WEIGHTS], *[out_v[n] for n in WEIGHTS])
```

```python
import functools
import math

import jax
import jax.numpy as jnp
from jax import lax
from jax.experimental import pallas as pl
from jax.experimental.pallas import tpu as pltpu

F32 = jnp.float32
BF16 = jnp.bfloat16

N_DEV = 8
N_META = 16
HEAD_DIM = 64
CONV_WIDTH = 4
RG_C = 8.0
NORM_EPS = 1e-6
LANES = 128
SUBLANES = 8
ATT_BLOCK = 128
ATT_TQ = 512
NEG_BIG = -1e30
ATT_SCALE = 1.0 / math.sqrt(HEAD_DIM)

ADAM_LR = 0.001
ADAM_B1 = 0.9
ADAM_B2 = 0.999
ADAM_EPS = 1e-08
ADAM_WD = 0.01
ADAM_STEP = 10

VMEM_LIMIT_BYTES = 56 * 1024 * 1024
MESH = pl.DeviceIdType.MESH
ANY = pl.BlockSpec(memory_space=pl.ANY)


def _cparams(*sem):
    return pltpu.CompilerParams(dimension_semantics=sem if sem else None,
                                vmem_limit_bytes=VMEM_LIMIT_BYTES)


def _dot(a, b):
    return jnp.dot(a, b, preferred_element_type=F32)


def _dot_nt(a, b):
    return lax.dot_general(a, b, (((1,), (1,)), ((), ())), preferred_element_type=F32)


def _dot_tn(a, b):
    return lax.dot_general(a, b, (((0,), (0,)), ((), ())), preferred_element_type=F32)


def _sigmoid(x):
    return 0.5 * (1.0 + jnp.tanh(0.5 * x))


def _log_sigmoid(x):
    return jnp.minimum(x, 0.0) - jnp.log(1.0 + jnp.exp(-jnp.abs(x)))


def _expm1(x):
    series = x * (1.0 + x * (0.5 + x * (1.0 / 6.0 + x * (1.0 / 24.0))))
    return jnp.where(jnp.abs(x) < 1e-2, series, jnp.exp(x) - 1.0)


_GELU_K = math.sqrt(2.0 / math.pi)
_GELU_C = 0.044715


def _gelu(x):
    t = jnp.tanh(_GELU_K * (x + _GELU_C * x * x * x))
    return 0.5 * x * (1.0 + t)


def _gelu_grad(x):
    t = jnp.tanh(_GELU_K * (x + _GELU_C * x * x * x))
    return 0.5 * (1.0 + t) + 0.5 * x * (1.0 - t * t) * _GELU_K * (1.0 + 3.0 * _GELU_C * x * x)


def _split3_dot(tri, x):
    hi = x.astype(BF16)
    r1 = x - hi.astype(F32)
    mid = r1.astype(BF16)
    lo = (r1 - mid.astype(F32)).astype(BF16)
    return _dot(tri, hi) + _dot(tri, mid) + _dot(tri, lo)


def _dot_split3(x, sel):
    hi = x.astype(BF16)
    r1 = x - hi.astype(F32)
    mid = r1.astype(BF16)
    lo = (r1 - mid.astype(F32)).astype(BF16)
    return _dot(hi, sel) + _dot(mid, sel) + _dot(lo, sel)


def _rms_fwd(x, g):
    r = lax.rsqrt(jnp.mean(x * x, axis=-1, keepdims=True) + NORM_EPS)
    return x * r * g


def _rms_bwd(x, g, dy):
    r = lax.rsqrt(jnp.mean(x * x, axis=-1, keepdims=True) + NORM_EPS)
    xn = x * r
    dxn = dy * g
    dx = r * (dxn - xn * jnp.mean(dxn * xn, axis=-1, keepdims=True))
    return dx, jnp.sum(dy * xn, axis=0, keepdims=True)


def _accumulate(ref, val, first):
    @pl.when(first)
    def _():
        ref[...] = val

    @pl.when(jnp.logical_not(first))
    def _():
        ref[...] += val


def in_proj(h, g1, w_in_t, wrest_t, nq, tm):
    tp, d = h.shape
    nr = wrest_t.shape[0]

    def body(h_ref, g_ref, wq_ref, wr_ref, z_ref, qkv_ref, rest_ref):
        z = _rms_fwd(h_ref[...], g_ref[...]).astype(BF16)
        z_ref[...] = z
        qkv_ref[...] = _dot_nt(z, wq_ref[...]).astype(BF16)
        rest_ref[...] = _dot_nt(z, wr_ref[...])

    return pl.pallas_call(
        body, name="in_proj", grid=(tp // tm,),
        in_specs=[pl.BlockSpec((tm, d), lambda i: (i, 0)),
                  pl.BlockSpec((1, d), lambda i: (0, 0)),
                  pl.BlockSpec((nq, d), lambda i: (0, 0)),
                  pl.BlockSpec((nr, d), lambda i: (0, 0))],
        out_specs=[pl.BlockSpec((tm, d), lambda i: (i, 0)),
                   pl.BlockSpec((tm, nq), lambda i: (i, 0)),
                   pl.BlockSpec((tm, nr), lambda i: (i, 0))],
        out_shape=[jax.ShapeDtypeStruct((tp, d), BF16),
                   jax.ShapeDtypeStruct((tp, nq), BF16),
                   jax.ShapeDtypeStruct((tp, nr), F32)],
        compiler_params=_cparams("parallel"),
    )(h, g1, w_in_t, wrest_t)


def fgate_fwd(rest, bf_pad, fcol):
    tp = rest.shape[0]
    nb = tp // ATT_BLOCK

    def body(f_ref, b_ref, c_ref, ct_ref):
        r_i = lax.broadcasted_iota(jnp.int32, (ATT_BLOCK, ATT_BLOCK), 0)
        c_i = lax.broadcasted_iota(jnp.int32, (ATT_BLOCK, ATT_BLOCK), 1)
        tri = (r_i >= c_i).astype(BF16)
        carry = jnp.zeros((1, LANES), F32)
        for i in range(nb):
            sl = slice(i * ATT_BLOCK, (i + 1) * ATT_BLOCK)
            lf = _log_sigmoid(f_ref[sl, :] + b_ref[...])
            cs = _split3_dot(tri, lf) + carry
            carry = cs[ATT_BLOCK - 1:ATT_BLOCK, :]
            c_ref[sl, :] = cs
            ct_ref[:, sl] = cs.T[0:SUBLANES, :]

    return pl.pallas_call(
        body, name="fgate_fwd", grid=(1,),
        in_specs=[pl.BlockSpec((tp, LANES), lambda i: (0, fcol)),
                  pl.BlockSpec((1, LANES), lambda i: (0, 0))],
        out_specs=[pl.BlockSpec((tp, LANES), lambda i: (0, 0)),
                   pl.BlockSpec((SUBLANES, tp), lambda i: (0, 0))],
        out_shape=[jax.ShapeDtypeStruct((tp, LANES), F32),
                   jax.ShapeDtypeStruct((SUBLANES, tp), F32)],
        compiler_params=_cparams("arbitrary"),
    )(rest, bf_pad)


def _pick_col(blk, head):
    lane = lax.broadcasted_iota(jnp.int32, blk.shape, 1)
    return jnp.sum(jnp.where(lane == head, blk, 0.0), axis=1, keepdims=True)


def _pick_row(blk, head):
    sub = lax.broadcasted_iota(jnp.int32, blk.shape, 0)
    return jnp.sum(jnp.where(sub == head, blk, 0.0), axis=0, keepdims=True)


def _att_tiles(tp):
    out, r0 = [], 0
    while r0 < tp:
        rows = min(ATT_TQ, tp - r0)
        out.append((r0, rows, r0 + rows))
        r0 += rows
    return out


def attn_fwd(qkv, c, ct, nh):
    tp = qkv.shape[0]
    npair = nh // 2
    tiles = _att_tiles(tp)

    def body(q_ref, k_ref, v_ref, c_ref, ct_ref, o_ref, lset_ref):
        p = pl.program_id(0)
        lset_ref[...] = jnp.zeros_like(lset_ref)
        for r0, nr, nk in tiles:
            rs = slice(r0, r0 + nr)
            causal = (r0 + lax.broadcasted_iota(jnp.int32, (nr, nk), 0)
                      >= lax.broadcasted_iota(jnp.int32, (nr, nk), 1))
            cblk = c_ref[rs, :]
            ctb = ct_ref[:, 0:nk]
            for hh in range(2):
                head = 2 * p + hh
                hs = slice(hh * HEAD_DIM, (hh + 1) * HEAD_DIM)
                q = q_ref[rs, hs] * ATT_SCALE
                s = _dot_nt(q, k_ref[0:nk, hs]) + (_pick_col(cblk, head) - _pick_row(ctb, head))
                s = jnp.where(causal, s, NEG_BIG)
                m = jnp.max(s, axis=1, keepdims=True)
                pm = jnp.exp(s - m)
                l = jnp.sum(pm, axis=1, keepdims=True)
                o_ref[rs, hs] = _dot(pm.astype(BF16), v_ref[0:nk, hs]) / l
                lse = m + jnp.log(l)
                lset_ref[hh:hh + 1, rs] = jnp.broadcast_to(lse, (nr, LANES)).T[0:1, :]

    pair = lambda p: (0, p)
    return pl.pallas_call(
        body, name="attn_fwd", grid=(npair,),
        in_specs=[pl.BlockSpec((tp, LANES), pair),
                  pl.BlockSpec((tp, LANES), lambda p: (0, npair + p)),
                  pl.BlockSpec((tp, LANES), lambda p: (0, 2 * npair + p)),
                  pl.BlockSpec((tp, LANES), lambda p: (0, 0)),
                  pl.BlockSpec((SUBLANES, tp), lambda p: (0, 0))],
        out_specs=[pl.BlockSpec((tp, LANES), pair),
                   pl.BlockSpec((None, SUBLANES, tp), lambda p: (p, 0, 0))],
        out_shape=[jax.ShapeDtypeStruct((tp, nh * HEAD_DIM), F32),
                   jax.ShapeDtypeStruct((npair, SUBLANES, tp), F32)],
        compiler_params=_cparams("parallel"),
    )(qkv, qkv, qkv, c, ct)


def _shift_down(x, k, n):
    if k == 0:
        return x
    rows = lax.broadcasted_iota(jnp.int32, x.shape, 0)
    return jnp.where(rows >= k, pltpu.roll(x, k, 0), 0.0)


def _shift_up(x, k, n):
    if k == 0:
        return x
    rows = lax.broadcasted_iota(jnp.int32, x.shape, 0)
    return jnp.where(rows < n - k, pltpu.roll(x, n - k, 0), 0.0)


def _conv_fwd(xr, cw_ref, cb_ref, n):
    xc = cw_ref[CONV_WIDTH - 1:CONV_WIDTH, :] * xr + cb_ref[...]
    for k in range(1, CONV_WIDTH):
        xc = xc + cw_ref[CONV_WIDTH - 1 - k:CONV_WIDTH - k, :] * _shift_down(xr, k, n)
    return xc


def _gates(xc, wga_ref, bga_ref, wgx_ref, bgx_ref, l_ref):
    xcb = xc.astype(BF16)
    r = _sigmoid(_dot(xcb, wga_ref[...]) + bga_ref[...])
    ig = _sigmoid(_dot(xcb, wgx_ref[...]) + bgx_ref[...])
    ls = _log_sigmoid(l_ref[...])
    log_a = RG_C * r * ls
    a = jnp.exp(log_a)
    mult = jnp.sqrt(-_expm1(2.0 * log_a))
    return xcb, r, ig, ls, log_a, a, mult


SCAN_UNROLL = 4


def _scan_rows(a_s, u_s, out_ref, n, reverse):
    nt = n // SUBLANES
    per = SCAN_UNROLL if nt % SCAN_UNROLL == 0 else 1
    row = lax.broadcasted_iota(jnp.int32, (SUBLANES, LANES), 0)
    last = 0 if reverse else SUBLANES - 1

    def tile_scan(a, u):
        for d in (1, 2, 4):
            if reverse:
                keep = row < SUBLANES - d
                sh = SUBLANES - d
            else:
                keep = row >= d
                sh = d
            a_sh = jnp.where(keep, pltpu.roll(a, sh, 0), 1.0)
            u_sh = jnp.where(keep, pltpu.roll(u, sh, 0), 0.0)
            u = a * u_sh + u
            a = a * a_sh
        return a, u

    def step(t, carry):
        tiles = []
        for k in range(per):
            tt = t * per + k
            if reverse:
                tt = nt - 1 - tt
            off = pl.multiple_of(tt * SUBLANES, SUBLANES)
            a, u = tile_scan(a_s[pl.ds(off, SUBLANES), :], u_s[pl.ds(off, SUBLANES), :])
            tiles.append((off, a, u))
        for off, a, u in tiles:
            out_ref[pl.ds(off, SUBLANES), :] = u + a * carry
            carry = u[last:last + 1, :] + a[last:last + 1, :] * carry
        return carry

    lax.fori_loop(0, nt // per, step, jnp.zeros((1, LANES), F32))


def rec_fwd(rest, convw, convb, wga, bga, wgx, bgx, lru, rw):
    tp = rest.shape[0]
    ng = rw // LANES

    def body(xr_ref, yr_ref, cw_ref, cb_ref, wga_ref, bga_ref, wgx_ref, bgx_ref, l_ref,
             rec_ref, hr_ref, xc_ref, a_s, u_s):
        xc = _conv_fwd(xr_ref[...], cw_ref, cb_ref, tp)
        xc_ref[...] = xc
        _, r, ig, ls, log_a, a, mult = _gates(xc, wga_ref, bga_ref, wgx_ref, bgx_ref, l_ref)
        a_s[...] = a
        u_s[...] = mult * ig * xc
        _scan_rows(a_s, u_s, hr_ref, tp, reverse=False)
        rec_ref[...] = hr_ref[...] * _gelu(yr_ref[...])

    col = lambda g: (0, g)
    vec = pl.BlockSpec((1, LANES), col)
    big = pl.BlockSpec((tp, LANES), col)
    return pl.pallas_call(
        body, name="rec_fwd", grid=(ng,),
        in_specs=[big, pl.BlockSpec((tp, LANES), lambda g: (0, ng + g)),
                  pl.BlockSpec((CONV_WIDTH, LANES), col), vec,
                  pl.BlockSpec((None, LANES, LANES), lambda g: (g, 0, 0)), vec,
                  pl.BlockSpec((None, LANES, LANES), lambda g: (g, 0, 0)), vec, vec],
        out_specs=[big, big, big],
        out_shape=[jax.ShapeDtypeStruct((tp, rw), F32)] * 3,
        scratch_shapes=[pltpu.VMEM((tp, LANES), F32), pltpu.VMEM((tp, LANES), F32)],
        compiler_params=_cparams("parallel"),
    )(rest, rest, convw, convb, wga, bga, wgx, bgx, lru)


def out_proj(h, o, rec, ga, gr, wout, g2, tm):
    tp, d = h.shape
    aw, rw = o.shape[1], rec.shape[1]

    def body(h_ref, o_ref, rec_ref, ga_ref, gr_ref, w_ref, g2_ref, h2_ref, mix_ref, z2_ref):
        mix_ref[:, 0:aw] = _rms_fwd(o_ref[...], ga_ref[...]).astype(BF16)
        mix_ref[:, aw:aw + rw] = _rms_fwd(rec_ref[...], gr_ref[...]).astype(BF16)
        h2 = h_ref[...] + _dot(mix_ref[...], w_ref[...])
        h2_ref[...] = h2
        z2_ref[...] = _rms_fwd(h2, g2_ref[...]).astype(BF16)

    row = lambda i: (i, 0)
    fix = lambda i: (0, 0)
    return pl.pallas_call(
        body, name="out_proj", grid=(tp // tm,),
        in_specs=[pl.BlockSpec((tm, d), row), pl.BlockSpec((tm, aw), row), pl.BlockSpec((tm, rw), row),
                  pl.BlockSpec((1, aw), fix), pl.BlockSpec((1, rw), fix),
                  pl.BlockSpec((d, d), fix), pl.BlockSpec((1, d), fix)],
        out_specs=[pl.BlockSpec((tm, d), row)] * 3,
        out_shape=[jax.ShapeDtypeStruct((tp, d), F32), jax.ShapeDtypeStruct((tp, d), BF16),
                   jax.ShapeDtypeStruct((tp, d), BF16)],
        compiler_params=_cparams("parallel"),
    )(h, o, rec, ga, gr, wout, g2)


MLP_BLOCKS = 2


def mlp_fwd(z2, h2, gup, gdown, tm):
    tp, d = h2.shape
    nf = gup.shape[0]
    tf = gup.shape[2]
    nb = MLP_BLOCKS if nf % MLP_BLOCKS == 0 else 1
    nj = nf // nb

    def body(z_ref, h_ref, wu_ref, wd_ref, u_ref, h3_ref, acc):
        j = pl.program_id(1)
        z = z_ref[...]
        part = None
        for b in range(nb):
            u = jnp.maximum(_dot(z, wu_ref[b]), 0.0)
            u_ref[:, b * tf:(b + 1) * tf] = u.astype(BF16)
            p = _dot((u * u).astype(BF16), wd_ref[b])
            part = p if part is None else part + p

        @pl.when(j == 0)
        def _():
            acc[...] = h_ref[...] + part

        @pl.when(j > 0)
        def _():
            acc[...] += part

        @pl.when(j == nj - 1)
        def _():
            h3_ref[...] = acc[...]

    return pl.pallas_call(
        body, name="mlp_fwd", grid=(tp // tm, nj),
        in_specs=[pl.BlockSpec((tm, d), lambda i, j: (i, 0)),
                  pl.BlockSpec((tm, d), lambda i, j: (i, 0)),
                  pl.BlockSpec((nb, d, tf), lambda i, j: (j, 0, 0)),
                  pl.BlockSpec((nb, tf, d), lambda i, j: (j, 0, 0))],
        out_specs=[pl.BlockSpec((tm, nb * tf), lambda i, j: (i, j)),
                   pl.BlockSpec((tm, d), lambda i, j: (i, 0))],
        out_shape=[jax.ShapeDtypeStruct((tp, nf * tf), BF16), jax.ShapeDtypeStruct((tp, d), F32)],
        scratch_shapes=[pltpu.VMEM((tm, d), F32)],
        compiler_params=_cparams("parallel", "arbitrary"),
    )(z2, h2, gup, gdown)


def loss_head(h, gf, tgt, t_real, tm):
    tp, d = h.shape

    def body(h_ref, g_ref, t_ref, dh_ref, dg_ref, loss_ref):
        i = pl.program_id(0)
        x = h_ref[...]
        g = g_ref[...]
        r = lax.rsqrt(jnp.mean(x * x, axis=-1, keepdims=True) + NORM_EPS)
        xn = x * r
        rows = i * tm + lax.broadcasted_iota(jnp.int32, (tm, 1), 0)
        valid = jnp.logical_and(rows >= N_META, rows < t_real)
        e = jnp.where(valid, xn * g - t_ref[...], 0.0)
        part = 0.5 * jnp.sum(jnp.sum(e * e, axis=1, keepdims=True) / d, axis=0, keepdims=True)
        dy = e / d
        dxn = dy * g
        dh_ref[...] = r * (dxn - xn * jnp.mean(dxn * xn, axis=-1, keepdims=True))
        _accumulate(dg_ref, jnp.sum(dy * xn, axis=0, keepdims=True), i == 0)
        _accumulate(loss_ref, jnp.broadcast_to(part, (1, LANES)), i == 0)

    row = lambda i: (i, 0)
    fix = lambda i: (0, 0)
    return pl.pallas_call(
        body, name="loss_head", grid=(tp // tm,),
        in_specs=[pl.BlockSpec((tm, d), row), pl.BlockSpec((1, d), fix), pl.BlockSpec((tm, d), row)],
        out_specs=[pl.BlockSpec((tm, d), row), pl.BlockSpec((1, d), fix), pl.BlockSpec((1, LANES), fix)],
        out_shape=[jax.ShapeDtypeStruct((tp, d), F32), jax.ShapeDtypeStruct((1, d), F32),
                   jax.ShapeDtypeStruct((1, LANES), F32)],
        compiler_params=_cparams("arbitrary"),
    )(h, gf, tgt)


def mlp_bwd(dh, u, h2, g2, gup, gdown, tm):
    tp, d = dh.shape
    nf = gup.shape[0]
    tf = gup.shape[2]
    nb = MLP_BLOCKS if nf % MLP_BLOCKS == 0 else 1
    nj = nf // nb
    ni = tp // tm

    def body(dh_ref, u_ref, h2_ref, g_ref, wu_ref, wd_ref, dup_ref, dh2_ref, dg_ref, dhb, acc):
        i = pl.program_id(0)
        j = pl.program_id(1)

        @pl.when(j == 0)
        def _():
            dhb[...] = dh_ref[...].astype(BF16)

        part = None
        for b in range(nb):
            cols = slice(b * tf, (b + 1) * tf)
            dup = (_dot_nt(dhb[...], wd_ref[b]) * (2.0 * u_ref[:, cols].astype(F32))).astype(BF16)
            dup_ref[:, cols] = dup
            p = _dot_nt(dup, wu_ref[b])
            part = p if part is None else part + p
        _accumulate(acc, part, j == 0)

        @pl.when(j == nj - 1)
        def _():
            dx, dg = _rms_bwd(h2_ref[...], g_ref[...], acc[...])
            dh2_ref[...] = dh_ref[...] + dx
            _accumulate(dg_ref, dg, i == 0)

    return pl.pallas_call(
        body, name="mlp_bwd", grid=(ni, nj),
        in_specs=[pl.BlockSpec((tm, d), lambda i, j: (i, 0)),
                  pl.BlockSpec((tm, nb * tf), lambda i, j: (i, j)),
                  pl.BlockSpec((tm, d), lambda i, j: (i, 0)),
                  pl.BlockSpec((1, d), lambda i, j: (0, 0)),
                  pl.BlockSpec((nb, d, tf), lambda i, j: (j, 0, 0)),
                  pl.BlockSpec((nb, tf, d), lambda i, j: (j, 0, 0))],
        out_specs=[pl.BlockSpec((tm, nb * tf), lambda i, j: (i, j)),
                   pl.BlockSpec((tm, d), lambda i, j: (i, 0)),
                   pl.BlockSpec((1, d), lambda i, j: (0, 0)),
                   pl.BlockSpec((tm, d), lambda i, j: (i, 0))],
        out_shape=[jax.ShapeDtypeStruct((tp, nf * tf), BF16), jax.ShapeDtypeStruct((tp, d), F32),
                   jax.ShapeDtypeStruct((1, d), F32), jax.ShapeDtypeStruct((tp, d), BF16)],
        scratch_shapes=[pltpu.VMEM((tm, d), F32)],
        compiler_params=_cparams("arbitrary", "arbitrary"),
    )(dh, u, h2, g2, gup, gdown)


def mm_tn(a, b, *, tk, tn, out_dtype, name, square_a=False, blocked_n=False):
    rows, kk = a.shape
    nn = b.shape[1]
    keep_at = nn // tn > 1

    def a_tile(a_ref):
        av = a_ref[...]
        if square_a:
            af = av.astype(F32)
            av = af * af
        return av.astype(BF16)

    def body(a_ref, b_ref, o_ref, *scratch):
        if keep_at:
            at, = scratch

            @pl.when(pl.program_id(1) == 0)
            def _():
                at[...] = a_tile(a_ref).T

            o_ref[...] = _dot(at[...], b_ref[...].astype(BF16)).astype(out_dtype)
        else:
            o_ref[...] = _dot_tn(a_tile(a_ref), b_ref[...].astype(BF16)).astype(out_dtype)

    if blocked_n:
        out_spec = pl.BlockSpec((None, tk, tn), lambda k, n: (n, k, 0))
        out_shape = jax.ShapeDtypeStruct((nn // tn, kk, tn), out_dtype)
    else:
        out_spec = pl.BlockSpec((tk, tn), lambda k, n: (k, n))
        out_shape = jax.ShapeDtypeStruct((kk, nn), out_dtype)
    return pl.pallas_call(
        body, name=name, grid=(kk // tk, nn // tn),
        in_specs=[pl.BlockSpec((rows, tk), lambda k, n: (0, k)),
                  pl.BlockSpec((rows, tn), lambda k, n: (0, n))],
        out_specs=out_spec, out_shape=out_shape,
        scratch_shapes=[pltpu.VMEM((tk, rows), BF16)] if keep_at else [],
        compiler_params=_cparams("parallel", "arbitrary"),
    )(a, b)


def out_proj_bwd(dh2, o, rec, ga, gr, wout, tm):
    tp, d = dh2.shape
    aw, rw = o.shape[1], rec.shape[1]

    def body(dh_ref, o_ref, rec_ref, ga_ref, gr_ref, w_ref, do_ref, drec_ref, dga_ref, dgr_ref):
        i = pl.program_id(0)
        dmix = _dot_nt(dh_ref[...].astype(BF16), w_ref[...])
        do, dga = _rms_bwd(o_ref[...], ga_ref[...], dmix[:, 0:aw])
        drec, dgr = _rms_bwd(rec_ref[...], gr_ref[...], dmix[:, aw:aw + rw])
        do_ref[...] = do
        drec_ref[...] = drec
        _accumulate(dga_ref, dga, i == 0)
        _accumulate(dgr_ref, dgr, i == 0)

    row = lambda i: (i, 0)
    fix = lambda i: (0, 0)
    return pl.pallas_call(
        body, name="out_proj_bwd", grid=(tp // tm,),
        in_specs=[pl.BlockSpec((tm, d), row), pl.BlockSpec((tm, aw), row), pl.BlockSpec((tm, rw), row),
                  pl.BlockSpec((1, aw), fix), pl.BlockSpec((1, rw), fix), pl.BlockSpec((d, d), fix)],
        out_specs=[pl.BlockSpec((tm, aw), row), pl.BlockSpec((tm, rw), row),
                   pl.BlockSpec((1, aw), fix), pl.BlockSpec((1, rw), fix)],
        out_shape=[jax.ShapeDtypeStruct((tp, aw), F32), jax.ShapeDtypeStruct((tp, rw), F32),
                   jax.ShapeDtypeStruct((1, aw), F32), jax.ShapeDtypeStruct((1, rw), F32)],
        compiler_params=_cparams("arbitrary"),
    )(dh2, o, rec, ga, gr, wout)


def rec_bwd(drec, hr, xc, rest, convw, convb, wga, bga, wgx, bgx, lru, rw):
    tp = rest.shape[0]
    ng = rw // LANES

    def body(drec_ref, hr_ref, xc_ref, xr_ref, yr_ref, cw_ref, cb_ref, wga_ref, bga_ref, wgx_ref, bgx_ref, l_ref,
             dxr_ref, dyr_ref, dwga_ref, dwgx_ref, vec_ref, a_s, u_s, lam_s):
        xc = xc_ref[...]
        h = hr_ref[...]
        yr = yr_ref[...]
        drec = drec_ref[...]
        xcb, r, ig, ls, log_a, a, mult = _gates(xc, wga_ref, bga_ref, wgx_ref, bgx_ref, l_ref)
        dyr_ref[...] = (drec * h * _gelu_grad(yr)).astype(BF16)
        a_s[...] = _shift_up(a, 1, tp)
        u_s[...] = drec * _gelu(yr)
        _scan_rows(a_s, u_s, lam_s, tp, reverse=True)
        lam = lam_s[...]
        da = lam * _shift_down(h, 1, tp)
        dmult = lam * ig * xc
        dig = lam * mult * xc
        dxc = lam * mult * ig
        a2 = jnp.exp(2.0 * log_a)
        dlog_a = da * a - dmult * a2 / mult
        dr = dlog_a * (RG_C * ls)
        dl = jnp.sum(dlog_a * (RG_C * r), axis=0, keepdims=True) * _sigmoid(-l_ref[...])
        dpa = dr * r * (1.0 - r)
        dpx = dig * ig * (1.0 - ig)
        dpab = dpa.astype(BF16)
        dpxb = dpx.astype(BF16)
        dxc = dxc + _dot_nt(dpab, wga_ref[...]) + _dot_nt(dpxb, wgx_ref[...])
        dwga_ref[...] = _dot_tn(xcb, dpab)
        dwgx_ref[...] = _dot_tn(xcb, dpxb)
        xr = xr_ref[...]
        dxr = cw_ref[CONV_WIDTH - 1:CONV_WIDTH, :] * dxc
        for k in range(1, CONV_WIDTH):
            dxr = dxr + cw_ref[CONV_WIDTH - 1 - k:CONV_WIDTH - k, :] * _shift_up(dxc, k, tp)
        dxr_ref[...] = dxr.astype(BF16)
        for k in range(CONV_WIDTH):
            vec_ref[k:k + 1, :] = jnp.sum(dxc * _shift_down(xr, CONV_WIDTH - 1 - k, tp), axis=0, keepdims=True)
        vec_ref[4:5, :] = jnp.sum(dxc, axis=0, keepdims=True)
        vec_ref[5:6, :] = jnp.sum(dpa, axis=0, keepdims=True)
        vec_ref[6:7, :] = jnp.sum(dpx, axis=0, keepdims=True)
        vec_ref[7:8, :] = dl

    col = lambda g: (0, g)
    vec = pl.BlockSpec((1, LANES), col)
    big = pl.BlockSpec((tp, LANES), col)
    sq = pl.BlockSpec((None, LANES, LANES), lambda g: (g, 0, 0))
    return pl.pallas_call(
        body, name="rec_bwd", grid=(ng,),
        in_specs=[big, big, big, big, pl.BlockSpec((tp, LANES), lambda g: (0, ng + g)),
                  pl.BlockSpec((CONV_WIDTH, LANES), col), vec, sq, vec, sq, vec, vec],
        out_specs=[big, big, sq, sq, pl.BlockSpec((None, SUBLANES, LANES), lambda g: (g, 0, 0))],
        out_shape=[jax.ShapeDtypeStruct((tp, rw), BF16), jax.ShapeDtypeStruct((tp, rw), BF16),
                   jax.ShapeDtypeStruct((ng, LANES, LANES), F32), jax.ShapeDtypeStruct((ng, LANES, LANES), F32),
                   jax.ShapeDtypeStruct((ng, SUBLANES, LANES), F32)],
        scratch_shapes=[pltpu.VMEM((tp, LANES), F32)] * 3,
        compiler_params=_cparams("parallel"),
    )(drec, hr, xc, rest, rest, convw, convb, wga, bga, wgx, bgx, lru)


def attn_bwd(qkv, do, o, lset, c, ct, nh):
    tp = qkv.shape[0]
    npair = nh // 2
    aw = nh * HEAD_DIM
    tiles = _att_tiles(tp)

    def body(q_ref, k_ref, v_ref, do_ref, o_ref, lset_ref, c_ref, ct_ref,
             dq_ref, dk_ref, dv_ref, drow_ref, dcol_ref, dk_acc, dv_acc):
        p = pl.program_id(0)
        dk_acc[...] = jnp.zeros_like(dk_acc)
        dv_acc[...] = jnp.zeros_like(dv_acc)
        dcol_ref[...] = jnp.zeros_like(dcol_ref)
        drow_ref[...] = jnp.zeros_like(drow_ref)
        for r0, nr, nk in tiles:
            rs = slice(r0, r0 + nr)
            causal = (r0 + lax.broadcasted_iota(jnp.int32, (nk, nr), 1)
                      >= lax.broadcasted_iota(jnp.int32, (nk, nr), 0))
            cblk = c_ref[0:nk, :]
            ctb = ct_ref[:, rs]
            for hh in range(2):
                head = 2 * p + hh
                hs = slice(hh * HEAD_DIM, (hh + 1) * HEAD_DIM)
                q = q_ref[rs, hs]
                k = k_ref[0:nk, hs]
                dof = do_ref[rs, hs]
                do16 = dof.astype(BF16)
                delta = jnp.sum(dof * o_ref[rs, hs], axis=1, keepdims=True)
                delta_row = jnp.broadcast_to(delta, (nr, LANES)).T[0:1, :]
                s_t = _dot_nt(k, q * ATT_SCALE) + (_pick_row(ctb, head) - _pick_col(cblk, head))
                p_t = jnp.where(causal, jnp.exp(s_t - lset_ref[hh:hh + 1, rs]), 0.0)
                ds_t = p_t * (_dot_nt(v_ref[0:nk, hs], do16) - delta_row)
                p16 = p_t.astype(BF16)
                ds16 = ds_t.astype(BF16)
                dv_acc[0:nk, hs] += _dot(p16, do16)
                dk_acc[0:nk, hs] += _dot(ds16, q) * ATT_SCALE
                dq_ref[rs, hs] = (_dot_tn(ds16, k) * ATT_SCALE).astype(BF16)
                drow_ref[hh:hh + 1, rs] = jnp.sum(ds_t, axis=0, keepdims=True)
                dcol_ref[0:nk, hs] -= jnp.broadcast_to(jnp.sum(ds_t, axis=1, keepdims=True), (nk, HEAD_DIM))
        dk_ref[...] = dk_acc[...].astype(BF16)
        dv_ref[...] = dv_acc[...].astype(BF16)

    pair = lambda p: (0, p)
    return pl.pallas_call(
        body, name="attn_bwd", grid=(npair,),
        in_specs=[pl.BlockSpec((tp, LANES), pair),
                  pl.BlockSpec((tp, LANES), lambda p: (0, npair + p)),
                  pl.BlockSpec((tp, LANES), lambda p: (0, 2 * npair + p)),
                  pl.BlockSpec((tp, LANES), pair),
                  pl.BlockSpec((tp, LANES), pair),
                  pl.BlockSpec((None, SUBLANES, tp), lambda p: (p, 0, 0)),
                  pl.BlockSpec((tp, LANES), lambda p: (0, 0)),
                  pl.BlockSpec((SUBLANES, tp), lambda p: (0, 0))],
        out_specs=[pl.BlockSpec((tp, LANES), pair), pl.BlockSpec((tp, LANES), pair),
                   pl.BlockSpec((tp, LANES), pair),
                   pl.BlockSpec((None, SUBLANES, tp), lambda p: (p, 0, 0)),
                   pl.BlockSpec((tp, LANES), pair)],
        out_shape=[jax.ShapeDtypeStruct((tp, aw), BF16), jax.ShapeDtypeStruct((tp, aw), BF16),
                   jax.ShapeDtypeStruct((tp, aw), BF16),
                   jax.ShapeDtypeStruct((npair, SUBLANES, tp), F32),
                   jax.ShapeDtypeStruct((tp, aw), F32)],
        scratch_shapes=[pltpu.VMEM((tp, LANES), F32), pltpu.VMEM((tp, LANES), F32)],
        compiler_params=_cparams("parallel"),
    )(qkv, qkv, qkv, do, o, lset, c, ct)


def fgate_bwd(dct8, drs, rest, bf_pad, fcol):
    tp = rest.shape[0]
    aw = drs.shape[1]
    nb = tp // ATT_BLOCK
    B = ATT_BLOCK

    def body(d_ref, drs_ref, f_ref, b_ref, dfl_ref, db_ref, pad_s):
        r_i = lax.broadcasted_iota(jnp.int32, (B, B), 0)
        c_i = lax.broadcasted_iota(jnp.int32, (B, B), 1)
        triu = (c_i >= r_i).astype(BF16)
        sel = (lax.broadcasted_iota(jnp.int32, (aw, LANES), 0)
               == HEAD_DIM * lax.broadcasted_iota(jnp.int32, (aw, LANES), 1)).astype(BF16)
        carry = jnp.zeros((1, LANES), F32)
        db = jnp.zeros((1, LANES), F32)
        pad_s[...] = jnp.zeros_like(pad_s)
        for i in range(nb - 1, -1, -1):
            sl = slice(i * B, (i + 1) * B)
            pad_s[0:SUBLANES, :] = d_ref[:, sl]
            dc = pad_s[...].T + _dot_split3(drs_ref[sl, :], sel)
            rc = _split3_dot(triu, dc)
            dlf = rc + carry
            carry = carry + rc[0:1, :]
            dfl = dlf * _sigmoid(-(f_ref[sl, :] + b_ref[...]))
            dfl_ref[sl, :] = dfl.astype(BF16)
            db = db + jnp.sum(dfl, axis=0, keepdims=True)
        db_ref[...] = db

    return pl.pallas_call(
        body, name="fgate_bwd", grid=(1,),
        in_specs=[pl.BlockSpec((SUBLANES, tp), lambda i: (0, 0)),
                  pl.BlockSpec((tp, aw), lambda i: (0, 0)),
                  pl.BlockSpec((tp, LANES), lambda i: (0, fcol)),
                  pl.BlockSpec((1, LANES), lambda i: (0, 0))],
        out_specs=[pl.BlockSpec((tp, LANES), lambda i: (0, 0)),
                   pl.BlockSpec((1, LANES), lambda i: (0, 0))],
        out_shape=[jax.ShapeDtypeStruct((tp, LANES), BF16), jax.ShapeDtypeStruct((1, LANES), F32)],
        scratch_shapes=[pltpu.VMEM((B, B), F32)],
        compiler_params=_cparams("arbitrary"),
    )(dct8, drs, rest, bf_pad)


def in_proj_bwd(dh2, parts, w_in_t, wrest_t, h, g1, tm):
    tp, d = h.shape
    dq, dk, dv, dxr, dyr, dfl = parts
    aw, rw = dq.shape[1], dxr.shape[1]

    def body(dh2_ref, dq_ref, dk_ref, dv_ref, dxr_ref, dyr_ref, dfl_ref, wq_ref, wr_ref, h_ref, g_ref,
             dh_ref, dg_ref):
        i = pl.program_id(0)
        dz = _dot(dq_ref[...], wq_ref[0:aw, :])
        dz += _dot(dk_ref[...], wq_ref[aw:2 * aw, :])
        dz += _dot(dv_ref[...], wq_ref[2 * aw:3 * aw, :])
        dz += _dot(dxr_ref[...], wr_ref[0:rw, :])
        dz += _dot(dyr_ref[...], wr_ref[rw:2 * rw, :])
        dz += _dot(dfl_ref[...], wr_ref[2 * rw:2 * rw + LANES, :])
        dx, dg = _rms_bwd(h_ref[...], g_ref[...], dz)
        dh_ref[...] = dh2_ref[...] + dx
        _accumulate(dg_ref, dg, i == 0)

    row = lambda i: (i, 0)
    fix = lambda i: (0, 0)
    return pl.pallas_call(
        body, name="in_proj_bwd", grid=(tp // tm,),
        in_specs=[pl.BlockSpec((tm, d), row),
                  pl.BlockSpec((tm, aw), row), pl.BlockSpec((tm, aw), row), pl.BlockSpec((tm, aw), row),
                  pl.BlockSpec((tm, rw), row), pl.BlockSpec((tm, rw), row), pl.BlockSpec((tm, LANES), row),
                  pl.BlockSpec((3 * aw, d), fix), pl.BlockSpec(wrest_t.shape, fix),
                  pl.BlockSpec((tm, d), row), pl.BlockSpec((1, d), fix)],
        out_specs=[pl.BlockSpec((tm, d), row), pl.BlockSpec((1, d), fix)],
        out_shape=[jax.ShapeDtypeStruct((tp, d), F32), jax.ShapeDtypeStruct((1, d), F32)],
        compiler_params=_cparams("arbitrary"),
    )(dh2, dq, dk, dv, dxr, dyr, dfl, w_in_t, wrest_t, h, g1)


def dw_in_t(z, parts, nh, tr):
    tp, d = z.shape
    dq, dk, dv, dxr, dyr, dfl = parts
    aw, rw = dq.shape[1], dxr.shape[1]
    d_in = 3 * aw + nh + 2 * rw
    nr = tp // tr
    offs = [(0, aw), (aw, aw), (2 * aw, aw), (3 * aw + nh, rw), (3 * aw + nh + rw, rw)]

    def body(z_ref, dq_ref, dk_ref, dv_ref, dxr_ref, dyr_ref, dfl_ref, o_ref, acc):
        r = pl.program_id(0)

        @pl.when(r == 0)
        def _():
            acc[...] = jnp.zeros_like(acc)

        zt = z_ref[...]
        for (o, n), ref in zip(offs, (dq_ref, dk_ref, dv_ref, dxr_ref, dyr_ref)):
            acc[o:o + n, :] += _dot_tn(ref[...], zt)
        acc[3 * aw:3 * aw + nh, :] += _dot_tn(dfl_ref[...], zt)[0:nh, :]

        @pl.when(r == nr - 1)
        def _():
            o_ref[...] = acc[...].astype(BF16)

    row = lambda r: (r, 0)
    return pl.pallas_call(
        body, name="dw_in", grid=(nr,),
        in_specs=[pl.BlockSpec((tr, d), row),
                  pl.BlockSpec((tr, aw), row), pl.BlockSpec((tr, aw), row), pl.BlockSpec((tr, aw), row),
                  pl.BlockSpec((tr, rw), row), pl.BlockSpec((tr, rw), row), pl.BlockSpec((tr, LANES), row)],
        out_specs=pl.BlockSpec((d_in, d), lambda r: (0, 0)),
        out_shape=jax.ShapeDtypeStruct((d_in, d), BF16),
        scratch_shapes=[pltpu.VMEM((d_in, d), F32)],
        compiler_params=_cparams("arbitrary"),
    )(z, dq, dk, dv, dxr, dyr, dfl)


def _place():
    return lax.axis_index("x"), lax.axis_index("y"), lax.axis_index("c")


HBM = pl.BlockSpec(memory_space=pltpu.HBM)
SEM = pl.BlockSpec(memory_space=pltpu.SEMAPHORE)
EFFECT = pltpu.SideEffectType.DATAFLOW_SIDE_EFFECTING


def _in_hbm(a):
    return pltpu.with_memory_space_constraint(a, pltpu.HBM)


def _as_list(a):
    return list(a) if isinstance(a, (list, tuple)) else [a]


def _gather_targets(x, y, c):
    return [(x, y, 1 - c), (1 - x, y, c), (x, 1 - y, c), (1 - x, 1 - y, c)]


def _slot(t):
    return 4 * t[0] + 2 * t[1] + t[2]


def gather_start(groups, name):
    flat = [a for g in groups for a in g]
    n = len(flat)
    ng = len(groups)
    lands = [lax.empty((N_DEV,) + a.shape, a.dtype) for a in flat]

    def body(*refs):
        src, land = refs[:n], refs[n:2 * n]
        sems = refs[2 * n:2 * n + 2 * ng]
        token = refs[-1]
        x, y, c = _place()
        me = 4 * x + 2 * y + c
        i = 0
        for gi, g in enumerate(groups):
            for a in range(len(g)):
                for k, t in enumerate(_gather_targets(x, y, c)):
                    pltpu.make_async_remote_copy(
                        src_ref=src[i], dst_ref=land[i].at[me],
                        send_sem=sems[2 * gi].at[4 * a + k], recv_sem=sems[2 * gi + 1].at[4 * a + k],
                        device_id=t, device_id_type=MESH).start()
                i += 1
        token[...] = jnp.zeros_like(token)

    sem_shapes = []
    for g in groups:
        sem_shapes += [pltpu.SemaphoreType.DMA((4 * len(g),)), pltpu.SemaphoreType.DMA((4 * len(g),))]
    out = pl.pallas_call(
        body, name=name,
        out_shape=sem_shapes + [pltpu.HBM(a.shape, a.dtype) for a in flat + lands]
        + [jax.ShapeDtypeStruct((SUBLANES, LANES), F32)],
        in_specs=[HBM] * (2 * n),
        out_specs=[SEM] * (2 * ng) + [HBM] * (2 * n) + [pl.BlockSpec(memory_space=pltpu.VMEM)],
        input_output_aliases={i: 2 * ng + i for i in range(2 * n)},
        compiler_params=pltpu.CompilerParams(has_side_effects=EFFECT),
    )(*[_in_hbm(a) for a in flat + lands])
    sems = out[:2 * ng]
    thru = out[2 * ng:2 * ng + 2 * n]
    srcs_t, lands_t = thru[:n], thru[n:]
    res, i = [], 0
    for gi, g in enumerate(groups):
        res.append((sems[2 * gi], sems[2 * gi + 1], srcs_t[i:i + len(g)], lands_t[i:i + len(g)]))
        i += len(g)
    return res, out[-1]


def gather_wait(send, recv, srcs, lands, after, name):
    n = len(srcs)

    def body(*refs):
        src, land = refs[:n], refs[n:2 * n]
        send_sem, recv_sem = refs[2 * n], refs[2 * n + 1]
        x, y, c = _place()
        for a in range(n):
            for k, t in enumerate(_gather_targets(x, y, c)):
                cp = pltpu.make_async_remote_copy(
                    src_ref=src[a], dst_ref=land[a].at[_slot(t)],
                    send_sem=send_sem.at[4 * a + k], recv_sem=recv_sem.at[4 * a + k],
                    device_id=t, device_id_type=MESH)
                cp.wait_send()
                cp.wait_recv()

    out = pl.pallas_call(
        body, name=name,
        out_shape=[pltpu.HBM(a.shape, a.dtype) for a in list(srcs) + list(lands)],
        in_specs=[HBM] * (2 * n) + [SEM, SEM] + [ANY] * len(_as_list(after)),
        out_specs=[HBM] * (2 * n),
        input_output_aliases={i: i for i in range(2 * n)},
        compiler_params=pltpu.CompilerParams(has_side_effects=EFFECT),
    )(*srcs, *lands, send, recv, *_as_list(after))
    return out[:n], out[n:]


def forward_start(lands, name):
    n = len(lands)

    def body(*refs):
        land = refs[:n]
        send_sem, recv_sem = refs[n], refs[n + 1]
        token = refs[-1]
        x, y, c = _place()
        for a in range(n):
            for j, chip in enumerate([(1 - x, y), (x, 1 - y), (1 - x, 1 - y)]):
                blk = land[a].at[_slot((*chip, c))]
                pltpu.make_async_remote_copy(src_ref=blk, dst_ref=blk, send_sem=send_sem.at[3 * a + j],
                                             recv_sem=recv_sem.at[3 * a + j], device_id=(x, y, 1 - c),
                                             device_id_type=MESH).start()
        token[...] = jnp.zeros_like(token)

    out = pl.pallas_call(
        body, name=name,
        out_shape=[pltpu.SemaphoreType.DMA((3 * n,)), pltpu.SemaphoreType.DMA((3 * n,))]
        + [pltpu.HBM(a.shape, a.dtype) for a in lands] + [jax.ShapeDtypeStruct((SUBLANES, LANES), F32)],
        in_specs=[HBM] * n,
        out_specs=[SEM, SEM] + [HBM] * n + [pl.BlockSpec(memory_space=pltpu.VMEM)],
        input_output_aliases={i: 2 + i for i in range(n)},
        compiler_params=pltpu.CompilerParams(has_side_effects=EFFECT),
    )(*[_in_hbm(a) for a in lands])
    return out[0], out[1], out[2:2 + n], out[-1][0, 0]


def forward_wait(send, recv, lands, after, name):
    n = len(lands)

    def body(*refs):
        land = refs[:n]
        send_sem, recv_sem = refs[n], refs[n + 1]
        x, y, c = _place()
        for a in range(n):
            for j, chip in enumerate([(1 - x, y), (x, 1 - y), (1 - x, 1 - y)]):
                cp = pltpu.make_async_remote_copy(
                    src_ref=land[a].at[_slot((*chip, c))], dst_ref=land[a].at[_slot((*chip, 1 - c))],
                    send_sem=send_sem.at[3 * a + j], recv_sem=recv_sem.at[3 * a + j],
                    device_id=(x, y, 1 - c), device_id_type=MESH)
                cp.wait_send()
                cp.wait_recv()

    return pl.pallas_call(
        body, name=name,
        out_shape=[pltpu.HBM(a.shape, a.dtype) for a in lands],
        in_specs=[HBM] * n + [SEM, SEM, ANY],
        out_specs=[HBM] * n,
        input_output_aliases={i: i for i in range(n)},
        compiler_params=pltpu.CompilerParams(has_side_effects=EFFECT),
    )(*lands, send, recv, after)


def _relations():
    return [(dx, dy, dc) for dx in (0, 1) for dy in (0, 1) for dc in (0, 1) if dx + dy + dc]


def _peer(x, y, c, rel):
    return ((1 - x) if rel[0] else x, (1 - y) if rel[1] else y, (1 - c) if rel[2] else c)


def exchange_start(srcs, lands, layer, name):
    n = len(srcs)

    def body(*refs):
        src, land = refs[:n], refs[n:2 * n]
        send_sem, recv_sem = refs[2 * n], refs[2 * n + 1]
        token = refs[-1]
        x, y, c = _place()
        me = 4 * x + 2 * y + c
        for k, rel in enumerate(_relations()):
            peer = _peer(x, y, c, rel)
            for a in range(n):
                pltpu.make_async_remote_copy(
                    src_ref=src[a] if layer is None else src[a].at[_slot(peer)],
                    dst_ref=land[a].at[me] if layer is None else land[a].at[me, layer],
                    send_sem=send_sem.at[7 * a + k], recv_sem=recv_sem.at[7 * a + k],
                    device_id=peer, device_id_type=MESH).start()
        token[...] = jnp.zeros_like(token)

    out = pl.pallas_call(
        body, name=name,
        out_shape=[pltpu.SemaphoreType.DMA((7 * n,)), pltpu.SemaphoreType.DMA((7 * n,))]
        + [pltpu.HBM(a.shape, a.dtype) for a in list(srcs) + list(lands)]
        + [jax.ShapeDtypeStruct((SUBLANES, LANES), F32)],
        in_specs=[HBM] * (2 * n),
        out_specs=[SEM, SEM] + [HBM] * (2 * n) + [pl.BlockSpec(memory_space=pltpu.VMEM)],
        input_output_aliases={i: 2 + i for i in range(2 * n)},
        compiler_params=pltpu.CompilerParams(has_side_effects=EFFECT),
    )(*[_in_hbm(a) for a in list(srcs) + list(lands)])
    return out[0], out[1], out[2:2 + n], out[2 + n:2 + 2 * n], out[-1][0, 0]


def exchange_wait(send, recv, srcs, lands, after, layer, name):
    n = len(srcs)

    def body(*refs):
        src, land = refs[:n], refs[n:2 * n]
        send_sem, recv_sem = refs[2 * n], refs[2 * n + 1]
        x, y, c = _place()
        for k, rel in enumerate(_relations()):
            peer = _peer(x, y, c, rel)
            for a in range(n):
                cp = pltpu.make_async_remote_copy(
                    src_ref=src[a] if layer is None else src[a].at[_slot(peer)],
                    dst_ref=land[a].at[_slot(peer)] if layer is None else land[a].at[_slot(peer), layer],
                    send_sem=send_sem.at[7 * a + k], recv_sem=recv_sem.at[7 * a + k],
                    device_id=peer, device_id_type=MESH)
                cp.wait_send()
                cp.wait_recv()

    out = pl.pallas_call(
        body, name=name,
        out_shape=[pltpu.HBM(a.shape, a.dtype) for a in list(srcs) + list(lands)],
        in_specs=[HBM] * (2 * n) + [SEM, SEM] + [ANY] * len(_as_list(after)),
        out_specs=[HBM] * (2 * n),
        input_output_aliases={i: i for i in range(2 * n)},
        compiler_params=pltpu.CompilerParams(has_side_effects=EFFECT),
    )(*srcs, *lands, send, recv, *_as_list(after))
    return out[:n], out[n:]


def _adamw_math(g, w, m, v):
    m = ADAM_B1 * m + (1.0 - ADAM_B1) * g
    v = ADAM_B2 * v + (1.0 - ADAM_B2) * (g * g)
    m_hat = m / (1.0 - ADAM_B1 ** ADAM_STEP)
    v_hat = v / (1.0 - ADAM_B2 ** ADAM_STEP)
    delta = -ADAM_LR * (m_hat / (jnp.sqrt(v_hat) + ADAM_EPS) + ADAM_WD * w)
    return delta, m, v


def sum_adamw(parts, w, m, v, tr, name):
    npart, rows, cols = parts.shape

    def body(p_ref, w_ref, m_ref, v_ref, g_ref, d_ref, nm_ref, nv_ref):
        g = p_ref[0].astype(F32)
        for p in range(1, npart):
            g = g + p_ref[p].astype(F32)
        delta, nm, nv = _adamw_math(g, w_ref[...], m_ref[...], v_ref[...])
        g_ref[...] = g
        d_ref[...] = delta
        nm_ref[...] = nm
        nv_ref[...] = nv

    blk = pl.BlockSpec((tr, cols), lambda i: (i, 0))
    return pl.pallas_call(
        body, name=name, grid=(rows // tr,),
        in_specs=[pl.BlockSpec((npart, tr, cols), lambda i: (0, i, 0)), blk, blk, blk],
        out_specs=[blk] * 4,
        out_shape=[jax.ShapeDtypeStruct((rows, cols), F32)] * 4,
        compiler_params=_cparams("parallel"),
    )(parts, w, m, v)


def sum_adamw_t(parts, w, m, v, name):
    npart, nl, rows, cols = parts.shape

    def body(p_ref, w_ref, m_ref, v_ref, g_ref, d_ref, nm_ref, nv_ref):
        g = p_ref[0].astype(F32)
        for p in range(1, npart):
            g = g + p_ref[p].astype(F32)
        delta, nm, nv = _adamw_math(g, w_ref[...], m_ref[...], v_ref[...])
        g_ref[...] = g
        d_ref[...] = delta
        nm_ref[...] = nm
        nv_ref[...] = nv

    blk = pl.BlockSpec((None, rows, cols), lambda l: (l, 0, 0))
    return pl.pallas_call(
        body, name=name, grid=(nl,),
        in_specs=[pl.BlockSpec((npart, None, rows, cols), lambda l: (0, l, 0, 0)), blk, blk, blk],
        out_specs=[blk] * 4,
        out_shape=[jax.ShapeDtypeStruct((nl, rows, cols), F32)] * 4,
        compiler_params=_cparams("parallel"),
    )(parts, w, m, v)


def sum_parts(parts, name):
    npart, rows, cols = parts.shape

    def body(p_ref, g_ref):
        g = p_ref[0].astype(F32)
        for p in range(1, npart):
            g = g + p_ref[p].astype(F32)
        g_ref[...] = g

    return pl.pallas_call(
        body, name=name, grid=(1,),
        in_specs=[pl.BlockSpec((npart, rows, cols), lambda i: (0, 0, 0))],
        out_specs=pl.BlockSpec((rows, cols), lambda i: (0, 0)),
        out_shape=jax.ShapeDtypeStruct((rows, cols), F32),
        compiler_params=_cparams("arbitrary"),
    )(parts)


def _round_up(n, m):
    return (n + m - 1) // m * m


def _block_diag_pairs(w):
    nb, b, _ = w.shape
    per = LANES // b
    ng = nb // per
    w = w.reshape(ng, per, b, b)
    eye = jnp.eye(per, dtype=w.dtype)
    out = jnp.einsum('gpij,pq->gpiqj', w, eye).reshape(ng, LANES, LANES)
    return out.astype(BF16)


def _block_diag_extract(g, b):
    ng = g.shape[0]
    per = LANES // b
    g = g.reshape(ng, per, b, per, b)
    idx = jnp.arange(per)
    return g[:, idx, :, idx, :].transpose(1, 0, 2, 3).reshape(ng * per, b, b)


def _tiles(v):
    v = v.reshape(-1)
    n = _round_up(v.shape[0], SUBLANES * LANES)
    return jnp.pad(v, (0, n - v.shape[0])).reshape(-1, LANES)


SMALL = ['attn_norm_g', 'b_f', 'conv_w', 'conv_b', 'w_gate_a', 'b_gate_a', 'w_gate_x', 'b_gate_x',
         'lru_L', 'attn_out_g', 'rec_out_g', 'mlp_norm_g', 'final_g', 'meta']


def _pack(d):
    return jnp.concatenate([_tiles(d[n]) for n in SMALL], axis=0)


def _unpack(vec, shapes):
    out, r = {}, 0
    for n in SMALL:
        size = math.prod(shapes[n])
        nr = _round_up(size, SUBLANES * LANES) // LANES
        out[n] = vec[r:r + nr].reshape(-1)[:size].reshape(shapes[n])
        r += nr
    return out


def _row_tile(tp):
    return tp // 4 if (tp // 4) % 16 == 0 else tp


def local_step(x, tgt, meta, small, hooks):
    s, d = x.shape
    t_real = s + N_META
    tp = _round_up(t_real, ATT_BLOCK)
    depth = small['attn_norm_g'].shape[0]
    nh = small['b_f'].shape[1]
    rw = small['conv_b'].shape[1]
    blk = small['w_gate_a'].shape[2]
    tm = _row_tile(tp)
    tm2 = tp // 2
    fcol = 2 * rw // LANES

    h = jnp.concatenate([meta, x, jnp.zeros((tp - t_real, d), F32)], axis=0)
    tgt_p = jnp.pad(tgt, ((N_META, tp - t_real), (0, 0)))
    row = lambda v: v.reshape(1, -1)
    bf_pad = jnp.pad(small['b_f'], ((0, 0), (0, LANES - nh)))

    saved = []
    for l in range(depth):
        w_in_t, wrest_t, wout, tok_w = hooks.mixer_weights(l, h)
        wga = _block_diag_pairs(small['w_gate_a'][l])
        wgx = _block_diag_pairs(small['w_gate_x'][l])
        z, qkv, rest = in_proj(h, row(small['attn_norm_g'][l]) + tok_w, w_in_t, wrest_t, 3 * nh * HEAD_DIM, tm)
        c, ct = fgate_fwd(rest, bf_pad[l:l + 1], fcol)
        o, lset = attn_fwd(qkv, c, ct, nh)
        rec, hr, xc = rec_fwd(rest, small['conv_w'][l], row(small['conv_b'][l]), wga, row(small['b_gate_a'][l]),
                              wgx, row(small['b_gate_x'][l]), row(small['lru_L'][l]), rw)
        gup, gdown, tok_w = hooks.mlp_weights(l, rec)
        h2, mix, z2 = out_proj(h, o, rec, row(small['attn_out_g'][l]), row(small['rec_out_g'][l]), wout,
                               row(small['mlp_norm_g'][l]) + tok_w, tm)
        u, h3 = mlp_fwd(z2, h2, gup, gdown, tm2)
        saved.append(dict(h=h, z=z, qkv=qkv, rest=rest, c=c, ct=ct, o=o, lset=lset, rec=rec, hr=hr, xc=xc,
                          h2=h2, mix=mix, z2=z2, u=u, wga=wga, wgx=wgx,
                          w_in_t=w_in_t, wrest_t=wrest_t, wout=wout, gup=gup, gdown=gdown))
        h = h3

    dh, dgf, loss = loss_head(h, row(small['final_g']), tgt_p, t_real, tm)

    gs = {n: [None] * depth for n in SMALL if n not in ('final_g', 'meta')}
    tok = jnp.zeros((), F32)
    for l in reversed(range(depth)):
        sv = saved[l]
        gup, gdown = sv['gup'], sv['gdown']
        tf = gup.shape[2]
        dup, dh2, dg2, dhb = mlp_bwd(dh, sv['u'], sv['h2'], row(small['mlp_norm_g'][l]) + tok, gup, gdown, tm)
        gs['mlp_norm_g'][l] = dg2[0]
        do, drec, dga, dgr = out_proj_bwd(dh2, sv['o'], sv['rec'], row(small['attn_out_g'][l]),
                                          row(small['rec_out_g'][l]), sv['wout'], tm)
        gs['attn_out_g'][l] = dga[0]
        gs['rec_out_g'][l] = dgr[0]
        blocks = dict(
            w_down=mm_tn(sv['u'], dhb, tk=tf, tn=d, out_dtype=BF16, name="dw_down",
                         square_a=True).reshape(N_DEV, tf, d),
            w_up=mm_tn(sv['z2'], dup, tk=d, tn=tf, out_dtype=BF16, name="dw_up", blocked_n=True),
            w_out=mm_tn(sv['mix'], dh2, tk=d, tn=d // 2, out_dtype=BF16,
                        name="dw_out").reshape(N_DEV, d // N_DEV, d))
        tok = hooks.grads_ready(l, 'mlp', blocks)
        dxr, dyr, dwga, dwgx, vec = rec_bwd(drec, sv['hr'], sv['xc'], sv['rest'], small['conv_w'][l],
                                            row(small['conv_b'][l]) + tok, sv['wga'], row(small['b_gate_a'][l]),
                                            sv['wgx'], row(small['b_gate_x'][l]), row(small['lru_L'][l]), rw)
        gs['w_gate_a'][l] = _block_diag_extract(dwga, blk)
        gs['w_gate_x'][l] = _block_diag_extract(dwgx, blk)
        vec = vec.transpose(1, 0, 2).reshape(SUBLANES, rw)
        gs['conv_w'][l] = vec[0:CONV_WIDTH]
        gs['conv_b'][l] = vec[4]
        gs['b_gate_a'][l] = vec[5]
        gs['b_gate_x'][l] = vec[6]
        gs['lru_L'][l] = vec[7]
        dq, dk, dv, drow, dcol = attn_bwd(sv['qkv'], do, sv['o'], sv['lset'], sv['c'], sv['ct'] + tok, nh)
        drow8 = drow[:, 0:2, :].reshape(nh, tp)
        if nh < SUBLANES:
            drow8 = jnp.pad(drow8, ((0, SUBLANES - nh), (0, 0)))
        dfl, dbf = fgate_bwd(drow8, dcol, sv['rest'], bf_pad[l:l + 1], fcol)
        gs['b_f'][l] = dbf[0, 0:nh]
        parts = (dq, dk, dv, dxr, dyr, dfl)
        dh, dg1 = in_proj_bwd(dh2, parts, sv['w_in_t'], sv['wrest_t'], sv['h'], row(small['attn_norm_g'][l]), tm)
        gs['attn_norm_g'][l] = dg1[0]
        tok = jnp.zeros((), F32)
        if l == 0:
            grads = {n: jnp.stack(v) for n, v in gs.items()}
            grads['final_g'] = dgf[0]
            grads['meta'] = dh[0:N_META]
            tok = hooks.small_ready(grads)
        dw_in = dw_in_t(sv['z'], parts, nh, tm2)
        dw_in = dw_in.reshape(N_DEV, dw_in.shape[0] // N_DEV, d) + tok.astype(BF16)
        tok = hooks.grads_ready(l, 'in', dict(w_in=dw_in))

    return loss[0, 0], dh, tok


def prep_weights(g_in, g_out, nh, rw):
    d = g_in.shape[2]
    w_in_t = g_in.reshape(-1, d)
    f0 = 3 * nh * HEAD_DIM
    wrest_t = jnp.concatenate([w_in_t[f0 + nh:f0 + nh + 2 * rw],
                               jnp.pad(w_in_t[f0:f0 + nh], ((0, LANES - nh), (0, 0)))], axis=0)
    return w_in_t, wrest_t, g_out.reshape(d, d)


BIG = ['w_in', 'w_out', 'w_up', 'w_down']
EXCHANGE_GROUPS = {'mlp': ['w_down', 'w_up', 'w_out'], 'in': ['w_in']}
WEIGHTS = ['meta', 'attn_norm_g', 'w_in', 'b_f', 'conv_w', 'conv_b', 'w_gate_a', 'b_gate_a', 'w_gate_x', 'b_gate_x',
           'lru_L', 'attn_out_g', 'rec_out_g', 'w_out', 'mlp_norm_g', 'w_up', 'w_down', 'final_g']


def _set_own(arr, own, me):
    return lax.dynamic_update_slice_in_dim(arr, own[None], me, axis=0)


class _Step:
    def __init__(self, w, nh, rw, me):
        self.w, self.nh, self.rw, self.me = w, nh, rw, me
        depth = w['w_in'].shape[0]
        first = [w['w_in_t'][:, 0, :].astype(BF16), w['w_out'][0].astype(BF16), w['meta'], w['conv_w']]
        self.pending, token = gather_start([first], "gather_start_0")
        zero = token[0, 0].astype(BF16)
        groups = [[w['w_up'][0].astype(BF16) + zero, w['w_down'][0].astype(BF16) + zero]]
        for l in range(1, depth):
            groups.append([w['w_in_t'][:, l, :].astype(BF16) + zero, w['w_out'][l].astype(BF16) + zero])
            groups.append([w['w_up'][l].astype(BF16) + zero, w['w_down'][l].astype(BF16) + zero])
        rest, _ = gather_start(groups, "gather_start_1")
        self.pending += rest
        self.first_after = rest[0][2][0]
        self.gathered = {}
        self.passing = {}
        self.token = jnp.zeros((), F32)
        self.lands = {n: lax.empty((N_DEV,) + w[n].shape, BF16) for n in BIG}
        din8, _, d = w['w_in_t'].shape
        self.lands['w_in'] = lax.empty((N_DEV, depth, din8, d), BF16)
        self.started = []
        self.small = None

    def _pass_on(self, gi, after):
        if gi < len(self.pending) and gi not in self.passing:
            send, recv, srcs, lands = self.pending[gi]
            srcs, lands = gather_wait(send, recv, srcs, lands, after, "gather_wait_%d" % gi)
            fsend, frecv, lands, token = forward_start(lands, "forward_start_%d" % gi)
            self.passing[gi] = (fsend, frecv, srcs, lands)
            self.token = token

    def group(self, gi, after):
        if gi not in self.gathered:
            self._pass_on(gi, after)
            fsend, frecv, srcs, lands = self.passing[gi]
            lands = forward_wait(fsend, frecv, lands, after, "forward_wait_%d" % gi)
            self.gathered[gi] = [_set_own(g, own, self.me) for g, own in zip(lands, srcs)]
            if gi >= 2:
                self._pass_on(gi + 1, lands[0])
        return self.gathered[gi]

    def mixer_weights(self, l, after):
        g = self.group(2 * l, after)
        return (*prep_weights(g[0], g[1], self.nh, self.rw), self.token)

    def mlp_weights(self, l, after):
        g = self.group(2 * l + 1, after)
        return g[0], g[1], self.token

    def grads_ready(self, l, group, blocks):
        names = EXCHANGE_GROUPS[group]
        send, recv, srcs, lands, token = exchange_start(
            [blocks[n] for n in names], [self.lands[n] for n in names], l, "exchange_start_%s_%d" % (group, l))
        for n, a in zip(names, lands):
            self.lands[n] = a
        self.started.append((l, group, send, recv, srcs))
        return token

    def small_ready(self, grads):
        self.small_shapes = {n: grads[n].shape for n in SMALL}
        packed = _pack(grads).astype(BF16)
        send, recv, srcs, lands, token = exchange_start(
            [packed], [lax.empty((N_DEV,) + packed.shape, BF16)], None, "small_start")
        self.small = (send, recv, srcs, lands)
        return token

    def small_sum(self, after):
        send, recv, srcs, lands = self.small
        srcs, lands = exchange_wait(send, recv, srcs, lands, after, None, "small_wait")
        parts = _set_own(lands[0], srcs[0], self.me)
        return _unpack(sum_parts(parts, "sum_small_grads"), self.small_shapes)

    def received(self, group, after):
        names = EXCHANGE_GROUPS[group]
        own = {n: [None] * self.w[n].shape[0] for n in names}
        for l, grp, send, recv, srcs in self.started:
            if grp != group:
                continue
            srcs, lands = exchange_wait(send, recv, srcs, [self.lands[n] for n in names], after, l,
                                        "exchange_wait_%s_%d" % (group, l))
            for n, a, sr in zip(names, lands, srcs):
                self.lands[n] = a
                own[n][l] = lax.dynamic_index_in_dim(sr, self.me, 0, keepdims=False)
        return {n: _set_own(self.lands[n], jnp.stack(own[n]), self.me) for n in names}


def kernel(x, meta, attn_norm_g, w_in, b_f, conv_w, conv_b, w_gate_a, b_gate_a, w_gate_x, b_gate_x, lru_L, attn_out_g, rec_out_g, w_out, mlp_norm_g, w_up, w_down, final_g, loss_target, m_meta, m_attn_norm_g, m_w_in, m_b_f, m_conv_w, m_conv_b, m_w_gate_a, m_b_gate_a, m_w_gate_x, m_b_gate_x, m_lru_L, m_attn_out_g, m_rec_out_g, m_w_out, m_mlp_norm_g, m_w_up, m_w_down, m_final_g, v_meta, v_attn_norm_g, v_w_in, v_b_f, v_conv_w, v_conv_b, v_w_gate_a, v_b_gate_a, v_w_gate_x, v_b_gate_x, v_lru_L, v_attn_out_g, v_rec_out_g, v_w_out, v_mlp_norm_g, v_w_up, v_w_down, v_final_g):
    w = dict(meta=meta, attn_norm_g=attn_norm_g, w_in=w_in, b_f=b_f, conv_w=conv_w, conv_b=conv_b,
             w_gate_a=w_gate_a, b_gate_a=b_gate_a, w_gate_x=w_gate_x, b_gate_x=b_gate_x, lru_L=lru_L,
             attn_out_g=attn_out_g, rec_out_g=rec_out_g, w_out=w_out, mlp_norm_g=mlp_norm_g, w_up=w_up,
             w_down=w_down, final_g=final_g)
    mo = dict(meta=m_meta, attn_norm_g=m_attn_norm_g, w_in=m_w_in, b_f=m_b_f, conv_w=m_conv_w, conv_b=m_conv_b,
              w_gate_a=m_w_gate_a, b_gate_a=m_b_gate_a, w_gate_x=m_w_gate_x, b_gate_x=m_b_gate_x, lru_L=m_lru_L,
              attn_out_g=m_attn_out_g, rec_out_g=m_rec_out_g, w_out=m_w_out, mlp_norm_g=m_mlp_norm_g,
              w_up=m_w_up, w_down=m_w_down, final_g=m_final_g)
    vo = dict(meta=v_meta, attn_norm_g=v_attn_norm_g, w_in=v_w_in, b_f=v_b_f, conv_w=v_conv_w, conv_b=v_conv_b,
              w_gate_a=v_w_gate_a, b_gate_a=v_b_gate_a, w_gate_x=v_w_gate_x, b_gate_x=v_b_gate_x, lru_L=v_lru_L,
              attn_out_g=v_attn_out_g, rec_out_g=v_rec_out_g, w_out=v_w_out, mlp_norm_g=v_mlp_norm_g,
              w_up=v_w_up, w_down=v_w_down, final_g=v_final_g)
    depth = w_in.shape[0]
    nh = b_f.shape[1]
    rw = conv_b.shape[1]
    me = 4 * lax.axis_index("x") + 2 * lax.axis_index("y") + lax.axis_index("c")

    w['w_in_t'] = jnp.transpose(w_in, (2, 0, 1))
    swap = lambda a: jnp.swapaxes(a, 1, 2)
    step = _Step(w, nh, rw, me)
    g0 = step.group(0, step.first_after)
    meta_full = g0[2].transpose(1, 0, 2).reshape(N_META, -1)
    conv_full = g0[3].transpose(1, 2, 0, 3).reshape(depth, CONV_WIDTH, rw)
    small = {n: w[n] for n in SMALL}
    small['conv_w'] = conv_full

    loss_part, dh0, tok = local_step(x[0], loss_target[0], meta_full, small, step)
    loss = lax.psum(loss_part, ("x", "y", "c"))
    grad_x = dh0[N_META:N_META + x.shape[1]][None]

    out_g, out_d, out_m, out_v = {}, {}, {}, {}

    def update_big(group, after):
        for n, r in step.received(group, after).items():
            if n == 'w_in':
                out = sum_adamw_t(r, swap(w[n]), swap(mo[n]), swap(vo[n]), "adamw_w_in")
                out = [swap(a) for a in out]
            else:
                shp = w[n].shape
                rows, cols = shp[0] * shp[1], shp[2]
                tr = min(512 if cols <= 512 else 256, rows)
                out = sum_adamw(r.reshape(N_DEV, rows, cols), w[n].reshape(rows, cols), mo[n].reshape(rows, cols),
                                vo[n].reshape(rows, cols), tr, "adamw_" + n)
                out = [a.reshape(shp) for a in out]
            out_g[n], out_d[n], out_m[n], out_v[n] = out
            after = out[0]
        return after

    update_big('mlp', step.started[-1][4][0])

    gsum = step.small_sum([out_g[n] for n in EXCHANGE_GROUPS['mlp']])
    gsum['meta'] = lax.dynamic_slice_in_dim(gsum['meta'], me * meta.shape[1], meta.shape[1], axis=1)
    gsum['conv_w'] = lax.dynamic_slice_in_dim(gsum['conv_w'], me * conv_w.shape[2], conv_w.shape[2], axis=2)
    packed = [_pack(t) for t in (gsum, {n: w[n] for n in SMALL}, {n: mo[n] for n in SMALL},
                                 {n: vo[n] for n in SMALL})]
    res = sum_adamw(packed[0][None], packed[1], packed[2], packed[3], packed[1].shape[0], "adamw_small")
    shapes = {n: w[n].shape for n in SMALL}
    for dst, vec in zip((out_g, out_d, out_m, out_v), res):
        dst.update(_unpack(vec, shapes))

    update_big('in', res[0])

    return (loss, grad_x, *[out_g[n] for n in WEIGHTS], *[out_d[n] for n in WEIGHTS],
            *[out_m[n] for n in WEIGHTS], *[out_v[n] for n in WEIGHTS])
```

```python
import functools
import math

import jax
import jax.numpy as jnp
from jax import lax
from jax.experimental import pallas as pl
from jax.experimental.pallas import tpu as pltpu

F32 = jnp.float32
BF16 = jnp.bfloat16

N_DEV = 8
N_META = 16
HEAD_DIM = 64
CONV_WIDTH = 4
RG_C = 8.0
NORM_EPS = 1e-6
LANES = 128
SUBLANES = 8
ATT_BLOCK = 128
ATT_TQ = 512
NEG_BIG = -1e30
ATT_SCALE = 1.0 / math.sqrt(HEAD_DIM)

ADAM_LR = 0.001
ADAM_B1 = 0.9
ADAM_B2 = 0.999
ADAM_EPS = 1e-08
ADAM_WD = 0.01
ADAM_STEP = 10

VMEM_LIMIT_BYTES = 56 * 1024 * 1024
MESH = pl.DeviceIdType.MESH
ANY = pl.BlockSpec(memory_space=pl.ANY)


def _cparams(*sem):
    return pltpu.CompilerParams(dimension_semantics=sem if sem else None,
                                vmem_limit_bytes=VMEM_LIMIT_BYTES)


def _dot(a, b):
    return jnp.dot(a, b, preferred_element_type=F32)


def _dot_nt(a, b):
    return lax.dot_general(a, b, (((1,), (1,)), ((), ())), preferred_element_type=F32)


def _dot_tn(a, b):
    return lax.dot_general(a, b, (((0,), (0,)), ((), ())), preferred_element_type=F32)


def _sigmoid(x):
    return 0.5 * (1.0 + jnp.tanh(0.5 * x))


def _log_sigmoid(x):
    return jnp.minimum(x, 0.0) - jnp.log(1.0 + jnp.exp(-jnp.abs(x)))


def _expm1(x):
    series = x * (1.0 + x * (0.5 + x * (1.0 / 6.0 + x * (1.0 / 24.0))))
    return jnp.where(jnp.abs(x) < 1e-2, series, jnp.exp(x) - 1.0)


_GELU_K = math.sqrt(2.0 / math.pi)
_GELU_C = 0.044715


def _gelu(x):
    t = jnp.tanh(_GELU_K * (x + _GELU_C * x * x * x))
    return 0.5 * x * (1.0 + t)


def _gelu_grad(x):
    t = jnp.tanh(_GELU_K * (x + _GELU_C * x * x * x))
    return 0.5 * (1.0 + t) + 0.5 * x * (1.0 - t * t) * _GELU_K * (1.0 + 3.0 * _GELU_C * x * x)


def _split3_dot(tri, x):
    hi = x.astype(BF16)
    r1 = x - hi.astype(F32)
    mid = r1.astype(BF16)
    lo = (r1 - mid.astype(F32)).astype(BF16)
    return _dot(tri, hi) + _dot(tri, mid) + _dot(tri, lo)


def _dot_split3(x, sel):
    hi = x.astype(BF16)
    r1 = x - hi.astype(F32)
    mid = r1.astype(BF16)
    lo = (r1 - mid.astype(F32)).astype(BF16)
    return _dot(hi, sel) + _dot(mid, sel) + _dot(lo, sel)


def _rms_fwd(x, g):
    r = lax.rsqrt(jnp.mean(x * x, axis=-1, keepdims=True) + NORM_EPS)
    return x * r * g


def _rms_bwd(x, g, dy):
    r = lax.rsqrt(jnp.mean(x * x, axis=-1, keepdims=True) + NORM_EPS)
    xn = x * r
    dxn = dy * g
    dx = r * (dxn - xn * jnp.mean(dxn * xn, axis=-1, keepdims=True))
    return dx, jnp.sum(dy * xn, axis=0, keepdims=True)


def _accumulate(ref, val, first):
    @pl.when(first)
    def _():
        ref[...] = val

    @pl.when(jnp.logical_not(first))
    def _():
        ref[...] += val


def in_proj(h, g1, w_in_t, wrest_t, nq, tm):
    tp, d = h.shape
    nr = wrest_t.shape[0]

    def body(h_ref, g_ref, wq_ref, wr_ref, z_ref, qkv_ref, rest_ref):
        z = _rms_fwd(h_ref[...], g_ref[...]).astype(BF16)
        z_ref[...] = z
        qkv_ref[...] = _dot_nt(z, wq_ref[...]).astype(BF16)
        rest_ref[...] = _dot_nt(z, wr_ref[...])

    return pl.pallas_call(
        body, name="in_proj", grid=(tp // tm,),
        in_specs=[pl.BlockSpec((tm, d), lambda i: (i, 0)),
                  pl.BlockSpec((1, d), lambda i: (0, 0)),
                  pl.BlockSpec((nq, d), lambda i: (0, 0)),
                  pl.BlockSpec((nr, d), lambda i: (0, 0))],
        out_specs=[pl.BlockSpec((tm, d), lambda i: (i, 0)),
                   pl.BlockSpec((tm, nq), lambda i: (i, 0)),
                   pl.BlockSpec((tm, nr), lambda i: (i, 0))],
        out_shape=[jax.ShapeDtypeStruct((tp, d), BF16),
                   jax.ShapeDtypeStruct((tp, nq), BF16),
                   jax.ShapeDtypeStruct((tp, nr), F32)],
        compiler_params=_cparams("parallel"),
    )(h, g1, w_in_t, wrest_t)


def fgate_fwd(rest, bf_pad, fcol):
    tp = rest.shape[0]
    nb = tp // ATT_BLOCK

    def body(f_ref, b_ref, c_ref, ct_ref):
        r_i = lax.broadcasted_iota(jnp.int32, (ATT_BLOCK, ATT_BLOCK), 0)
        c_i = lax.broadcasted_iota(jnp.int32, (ATT_BLOCK, ATT_BLOCK), 1)
        tri = (r_i >= c_i).astype(BF16)
        carry = jnp.zeros((1, LANES), F32)
        for i in range(nb):
            sl = slice(i * ATT_BLOCK, (i + 1) * ATT_BLOCK)
            lf = _log_sigmoid(f_ref[sl, :] + b_ref[...])
            cs = _split3_dot(tri, lf) + carry
            carry = cs[ATT_BLOCK - 1:ATT_BLOCK, :]
            c_ref[sl, :] = cs
            ct_ref[:, sl] = cs.T[0:SUBLANES, :]

    return pl.pallas_call(
        body, name="fgate_fwd", grid=(1,),
        in_specs=[pl.BlockSpec((tp, LANES), lambda i: (0, fcol)),
                  pl.BlockSpec((1, LANES), lambda i: (0, 0))],
        out_specs=[pl.BlockSpec((tp, LANES), lambda i: (0, 0)),
                   pl.BlockSpec((SUBLANES, tp), lambda i: (0, 0))],
        out_shape=[jax.ShapeDtypeStruct((tp, LANES), F32),
                   jax.ShapeDtypeStruct((SUBLANES, tp), F32)],
        compiler_params=_cparams("arbitrary"),
    )(rest, bf_pad)


def _pick_col(blk, head):
    lane = lax.broadcasted_iota(jnp.int32, blk.shape, 1)
    return jnp.sum(jnp.where(lane == head, blk, 0.0), axis=1, keepdims=True)


def _pick_row(blk, head):
    sub = lax.broadcasted_iota(jnp.int32, blk.shape, 0)
    return jnp.sum(jnp.where(sub == head, blk, 0.0), axis=0, keepdims=True)


def _att_tiles(tp):
    out, r0 = [], 0
    while r0 < tp:
        rows = min(ATT_TQ, tp - r0)
        out.append((r0, rows, r0 + rows))
        r0 += rows
    return out


def _bias_cols(cvec, as_query):
    n = cvec.shape[0]
    hi = cvec.astype(BF16).astype(F32)
    r1 = cvec - hi
    mid = r1.astype(BF16).astype(F32)
    lo = r1 - mid
    lane = lax.broadcasted_iota(jnp.int32, (n, HEAD_DIM), 1)
    ones = jnp.where(lane < 3, 1.0, 0.0)
    if as_query:
        blk = jnp.where(lane == 0, hi, jnp.where(lane == 1, mid, jnp.where(lane == 2, lo, 0.0)))
        blk = blk + jnp.where(jnp.logical_and(lane >= 3, lane < 6), 1.0, 0.0)
    else:
        blk = jnp.where(lane == 3, -hi, jnp.where(lane == 4, -mid, jnp.where(lane == 5, -lo, 0.0))) + ones
    return blk.astype(BF16)


def attn_fwd(qkv, c, nh):
    tp = qkv.shape[0]
    npair = nh // 2
    tiles = _att_tiles(tp)

    def body(q_ref, k_ref, v_ref, c_ref, o_ref, lset_ref, q_aug, k_aug):
        p = pl.program_id(0)
        lset_ref[...] = jnp.zeros_like(lset_ref)
        for hh in range(2):
            head = 2 * p + hh
            hs = slice(hh * HEAD_DIM, (hh + 1) * HEAD_DIM)
            ccol = _pick_col(c_ref[...], head)
            k_aug[:, 0:HEAD_DIM] = k_ref[:, hs]
            k_aug[:, HEAD_DIM:LANES] = _bias_cols(ccol, False)
            q_aug[:, 0:HEAD_DIM] = q_ref[:, hs] * ATT_SCALE
            q_aug[:, HEAD_DIM:LANES] = _bias_cols(ccol, True)
            for r0, nr, nk in tiles:
                rs = slice(r0, r0 + nr)
                q = q_aug[rs, :]
                s_d = _dot_nt(q, k_aug[r0:nk, :])
                causal = (lax.broadcasted_iota(jnp.int32, (nr, nr), 0)
                          >= lax.broadcasted_iota(jnp.int32, (nr, nr), 1))
                s_d = jnp.where(causal, s_d, NEG_BIG)
                m = jnp.max(s_d, axis=1, keepdims=True)
                if r0:
                    s_o = _dot_nt(q, k_aug[0:r0, :])
                    m = jnp.maximum(m, jnp.max(s_o, axis=1, keepdims=True))
                p_d = jnp.exp(s_d - m)
                l = jnp.sum(p_d, axis=1, keepdims=True)
                acc = _dot(p_d.astype(BF16), v_ref[r0:nk, hs])
                if r0:
                    p_o = jnp.exp(s_o - m)
                    l = l + jnp.sum(p_o, axis=1, keepdims=True)
                    acc = acc + _dot(p_o.astype(BF16), v_ref[0:r0, hs])
                o_ref[rs, hs] = acc / l
                lse = m + jnp.log(l)
                lset_ref[hh:hh + 1, rs] = jnp.broadcast_to(lse, (nr, LANES)).T[0:1, :]

    pair = lambda p: (0, p)
    return pl.pallas_call(
        body, name="attn_fwd", grid=(npair,),
        in_specs=[pl.BlockSpec((tp, LANES), pair),
                  pl.BlockSpec((tp, LANES), lambda p: (0, npair + p)),
                  pl.BlockSpec((tp, LANES), lambda p: (0, 2 * npair + p)),
                  pl.BlockSpec((tp, LANES), lambda p: (0, 0))],
        out_specs=[pl.BlockSpec((tp, LANES), pair),
                   pl.BlockSpec((None, SUBLANES, tp), lambda p: (p, 0, 0))],
        out_shape=[jax.ShapeDtypeStruct((tp, nh * HEAD_DIM), F32),
                   jax.ShapeDtypeStruct((npair, SUBLANES, tp), F32)],
        scratch_shapes=[pltpu.VMEM((tp, LANES), BF16), pltpu.VMEM((tp, LANES), BF16)],
        compiler_params=_cparams("parallel"),
    )(qkv, qkv, qkv, c)


def _shift_down(x, k, n):
    if k == 0:
        return x
    rows = lax.broadcasted_iota(jnp.int32, x.shape, 0)
    return jnp.where(rows >= k, pltpu.roll(x, k, 0), 0.0)


def _shift_up(x, k, n):
    if k == 0:
        return x
    rows = lax.broadcasted_iota(jnp.int32, x.shape, 0)
    return jnp.where(rows < n - k, pltpu.roll(x, n - k, 0), 0.0)


def _conv_fwd(xr, cw_ref, cb_ref, n):
    xc = cw_ref[CONV_WIDTH - 1:CONV_WIDTH, :] * xr + cb_ref[...]
    for k in range(1, CONV_WIDTH):
        xc = xc + cw_ref[CONV_WIDTH - 1 - k:CONV_WIDTH - k, :] * _shift_down(xr, k, n)
    return xc


def _gates(xc, wga_ref, bga_ref, wgx_ref, bgx_ref, l_ref):
    xcb = xc.astype(BF16)
    r = _sigmoid(_dot(xcb, wga_ref[...]) + bga_ref[...])
    ig = _sigmoid(_dot(xcb, wgx_ref[...]) + bgx_ref[...])
    ls = _log_sigmoid(l_ref[...])
    log_a = RG_C * r * ls
    a = jnp.exp(log_a)
    mult = jnp.sqrt(-_expm1(2.0 * log_a))
    return xcb, r, ig, ls, log_a, a, mult


SCAN_UNROLL = 4


def _scan_rows(a_s, u_s, out_ref, n, reverse):
    nt = n // SUBLANES
    per = SCAN_UNROLL if nt % SCAN_UNROLL == 0 else 1
    row = lax.broadcasted_iota(jnp.int32, (SUBLANES, LANES), 0)
    last = 0 if reverse else SUBLANES - 1

    def tile_scan(a, u):
        for d in (1, 2, 4):
            if reverse:
                keep = row < SUBLANES - d
                sh = SUBLANES - d
            else:
                keep = row >= d
                sh = d
            a_sh = jnp.where(keep, pltpu.roll(a, sh, 0), 1.0)
            u_sh = jnp.where(keep, pltpu.roll(u, sh, 0), 0.0)
            u = a * u_sh + u
            a = a * a_sh
        return a, u

    def step(t, carry):
        tiles = []
        for k in range(per):
            tt = t * per + k
            if reverse:
                tt = nt - 1 - tt
            off = pl.multiple_of(tt * SUBLANES, SUBLANES)
            a, u = tile_scan(a_s[pl.ds(off, SUBLANES), :], u_s[pl.ds(off, SUBLANES), :])
            tiles.append((off, a, u))
        for off, a, u in tiles:
            out_ref[pl.ds(off, SUBLANES), :] = u + a * carry
            carry = u[last:last + 1, :] + a[last:last + 1, :] * carry
        return carry

    lax.fori_loop(0, nt // per, step, jnp.zeros((1, LANES), F32))


def rec_fwd(rest, convw, convb, wga, bga, wgx, bgx, lru, rw):
    tp = rest.shape[0]
    ng = rw // LANES

    def body(xr_ref, yr_ref, cw_ref, cb_ref, wga_ref, bga_ref, wgx_ref, bgx_ref, l_ref,
             rec_ref, hr_ref, xc_ref, a_s, u_s):
        xc = _conv_fwd(xr_ref[...], cw_ref, cb_ref, tp)
        xc_ref[...] = xc
        _, r, ig, ls, log_a, a, mult = _gates(xc, wga_ref, bga_ref, wgx_ref, bgx_ref, l_ref)
        a_s[...] = a
        u_s[...] = mult * ig * xc
        _scan_rows(a_s, u_s, hr_ref, tp, reverse=False)
        rec_ref[...] = hr_ref[...] * _gelu(yr_ref[...])

    col = lambda g: (0, g)
    vec = pl.BlockSpec((1, LANES), col)
    big = pl.BlockSpec((tp, LANES), col)
    return pl.pallas_call(
        body, name="rec_fwd", grid=(ng,),
        in_specs=[big, pl.BlockSpec((tp, LANES), lambda g: (0, ng + g)),
                  pl.BlockSpec((CONV_WIDTH, LANES), col), vec,
                  pl.BlockSpec((None, LANES, LANES), lambda g: (g, 0, 0)), vec,
                  pl.BlockSpec((None, LANES, LANES), lambda g: (g, 0, 0)), vec, vec],
        out_specs=[big, big, big],
        out_shape=[jax.ShapeDtypeStruct((tp, rw), F32)] * 3,
        scratch_shapes=[pltpu.VMEM((tp, LANES), F32), pltpu.VMEM((tp, LANES), F32)],
        compiler_params=_cparams("parallel"),
    )(rest, rest, convw, convb, wga, bga, wgx, bgx, lru)


def out_proj(h, o, rec, ga, gr, wout, g2, tm):
    tp, d = h.shape
    aw, rw = o.shape[1], rec.shape[1]

    def body(h_ref, o_ref, rec_ref, ga_ref, gr_ref, w_ref, g2_ref, h2_ref, mix_ref, z2_ref):
        mix_ref[:, 0:aw] = _rms_fwd(o_ref[...], ga_ref[...]).astype(BF16)
        mix_ref[:, aw:aw + rw] = _rms_fwd(rec_ref[...], gr_ref[...]).astype(BF16)
        h2 = h_ref[...] + _dot(mix_ref[...], w_ref[...])
        h2_ref[...] = h2
        z2_ref[...] = _rms_fwd(h2, g2_ref[...]).astype(BF16)

    row = lambda i: (i, 0)
    fix = lambda i: (0, 0)
    return pl.pallas_call(
        body, name="out_proj", grid=(tp // tm,),
        in_specs=[pl.BlockSpec((tm, d), row), pl.BlockSpec((tm, aw), row), pl.BlockSpec((tm, rw), row),
                  pl.BlockSpec((1, aw), fix), pl.BlockSpec((1, rw), fix),
                  pl.BlockSpec((d, d), fix), pl.BlockSpec((1, d), fix)],
        out_specs=[pl.BlockSpec((tm, d), row)] * 3,
        out_shape=[jax.ShapeDtypeStruct((tp, d), F32), jax.ShapeDtypeStruct((tp, d), BF16),
                   jax.ShapeDtypeStruct((tp, d), BF16)],
        compiler_params=_cparams("parallel"),
    )(h, o, rec, ga, gr, wout, g2)


MLP_BLOCKS = 2


def mlp_fwd(z2, h2, gup, gdown, tm):
    tp, d = h2.shape
    nf = gup.shape[0]
    tf = gup.shape[2]
    nb = MLP_BLOCKS if nf % MLP_BLOCKS == 0 else 1
    nj = nf // nb

    def body(z_ref, h_ref, wu_ref, wd_ref, u_ref, h3_ref, acc):
        j = pl.program_id(1)
        z = z_ref[...]
        part = None
        for b in range(nb):
            u = jnp.maximum(_dot(z, wu_ref[b]), 0.0)
            u_ref[:, b * tf:(b + 1) * tf] = u.astype(BF16)
            p = _dot((u * u).astype(BF16), wd_ref[b])
            part = p if part is None else part + p

        @pl.when(j == 0)
        def _():
            acc[...] = h_ref[...] + part

        @pl.when(j > 0)
        def _():
            acc[...] += part

        @pl.when(j == nj - 1)
        def _():
            h3_ref[...] = acc[...]

    return pl.pallas_call(
        body, name="mlp_fwd", grid=(tp // tm, nj),
        in_specs=[pl.BlockSpec((tm, d), lambda i, j: (i, 0)),
                  pl.BlockSpec((tm, d), lambda i, j: (i, 0)),
                  pl.BlockSpec((nb, d, tf), lambda i, j: (j, 0, 0)),
                  pl.BlockSpec((nb, tf, d), lambda i, j: (j, 0, 0))],
        out_specs=[pl.BlockSpec((tm, nb * tf), lambda i, j: (i, j)),
                   pl.BlockSpec((tm, d), lambda i, j: (i, 0))],
        out_shape=[jax.ShapeDtypeStruct((tp, nf * tf), BF16), jax.ShapeDtypeStruct((tp, d), F32)],
        scratch_shapes=[pltpu.VMEM((tm, d), F32)],
        compiler_params=_cparams("parallel", "arbitrary"),
    )(z2, h2, gup, gdown)


def loss_head(h, gf, tgt, t_real, tm):
    tp, d = h.shape

    def body(h_ref, g_ref, t_ref, dh_ref, dg_ref, loss_ref):
        i = pl.program_id(0)
        x = h_ref[...]
        g = g_ref[...]
        r = lax.rsqrt(jnp.mean(x * x, axis=-1, keepdims=True) + NORM_EPS)
        xn = x * r
        rows = i * tm + lax.broadcasted_iota(jnp.int32, (tm, 1), 0)
        valid = jnp.logical_and(rows >= N_META, rows < t_real)
        e = jnp.where(valid, xn * g - t_ref[...], 0.0)
        part = 0.5 * jnp.sum(jnp.sum(e * e, axis=1, keepdims=True) / d, axis=0, keepdims=True)
        dy = e / d
        dxn = dy * g
        dh_ref[...] = r * (dxn - xn * jnp.mean(dxn * xn, axis=-1, keepdims=True))
        _accumulate(dg_ref, jnp.sum(dy * xn, axis=0, keepdims=True), i == 0)
        _accumulate(loss_ref, jnp.broadcast_to(part, (1, LANES)), i == 0)

    row = lambda i: (i, 0)
    fix = lambda i: (0, 0)
    return pl.pallas_call(
        body, name="loss_head", grid=(tp // tm,),
        in_specs=[pl.BlockSpec((tm, d), row), pl.BlockSpec((1, d), fix), pl.BlockSpec((tm, d), row)],
        out_specs=[pl.BlockSpec((tm, d), row), pl.BlockSpec((1, d), fix), pl.BlockSpec((1, LANES), fix)],
        out_shape=[jax.ShapeDtypeStruct((tp, d), F32), jax.ShapeDtypeStruct((1, d), F32),
                   jax.ShapeDtypeStruct((1, LANES), F32)],
        compiler_params=_cparams("arbitrary"),
    )(h, gf, tgt)


def mlp_bwd(dh, u, h2, g2, gup, gdown, tm):
    tp, d = dh.shape
    nf = gup.shape[0]
    tf = gup.shape[2]
    nb = MLP_BLOCKS if nf % MLP_BLOCKS == 0 else 1
    nj = nf // nb
    ni = tp // tm

    def body(dh_ref, u_ref, h2_ref, g_ref, wu_ref, wd_ref, dup_ref, dh2_ref, dg_ref, dhb, acc):
        i = pl.program_id(0)
        j = pl.program_id(1)

        @pl.when(j == 0)
        def _():
            dhb[...] = dh_ref[...].astype(BF16)

        part = None
        for b in range(nb):
            cols = slice(b * tf, (b + 1) * tf)
            dup = (_dot_nt(dhb[...], wd_ref[b]) * (2.0 * u_ref[:, cols].astype(F32))).astype(BF16)
            dup_ref[:, cols] = dup
            p = _dot_nt(dup, wu_ref[b])
            part = p if part is None else part + p
        _accumulate(acc, part, j == 0)

        @pl.when(j == nj - 1)
        def _():
            dx, dg = _rms_bwd(h2_ref[...], g_ref[...], acc[...])
            dh2_ref[...] = dh_ref[...] + dx
            _accumulate(dg_ref, dg, i == 0)

    return pl.pallas_call(
        body, name="mlp_bwd", grid=(ni, nj),
        in_specs=[pl.BlockSpec((tm, d), lambda i, j: (i, 0)),
                  pl.BlockSpec((tm, nb * tf), lambda i, j: (i, j)),
                  pl.BlockSpec((tm, d), lambda i, j: (i, 0)),
                  pl.BlockSpec((1, d), lambda i, j: (0, 0)),
                  pl.BlockSpec((nb, d, tf), lambda i, j: (j, 0, 0)),
                  pl.BlockSpec((nb, tf, d), lambda i, j: (j, 0, 0))],
        out_specs=[pl.BlockSpec((tm, nb * tf), lambda i, j: (i, j)),
                   pl.BlockSpec((tm, d), lambda i, j: (i, 0)),
                   pl.BlockSpec((1, d), lambda i, j: (0, 0)),
                   pl.BlockSpec((tm, d), lambda i, j: (i, 0))],
        out_shape=[jax.ShapeDtypeStruct((tp, nf * tf), BF16), jax.ShapeDtypeStruct((tp, d), F32),
                   jax.ShapeDtypeStruct((1, d), F32), jax.ShapeDtypeStruct((tp, d), BF16)],
        scratch_shapes=[pltpu.VMEM((tm, d), F32)],
        compiler_params=_cparams("arbitrary", "arbitrary"),
    )(dh, u, h2, g2, gup, gdown)


def mm_tn(a, b, *, tk, tn, out_dtype, name, square_a=False, blocked_n=False):
    rows, kk = a.shape
    nn = b.shape[1]
    keep_at = nn // tn > 1

    def a_tile(a_ref):
        av = a_ref[...]
        if square_a:
            af = av.astype(F32)
            av = af * af
        return av.astype(BF16)

    def body(a_ref, b_ref, o_ref, *scratch):
        if keep_at:
            at, = scratch

            @pl.when(pl.program_id(1) == 0)
            def _():
                at[...] = a_tile(a_ref).T

            o_ref[...] = _dot(at[...], b_ref[...].astype(BF16)).astype(out_dtype)
        else:
            o_ref[...] = _dot_tn(a_tile(a_ref), b_ref[...].astype(BF16)).astype(out_dtype)

    if blocked_n:
        out_spec = pl.BlockSpec((None, tk, tn), lambda k, n: (n, k, 0))
        out_shape = jax.ShapeDtypeStruct((nn // tn, kk, tn), out_dtype)
    else:
        out_spec = pl.BlockSpec((tk, tn), lambda k, n: (k, n))
        out_shape = jax.ShapeDtypeStruct((kk, nn), out_dtype)
    return pl.pallas_call(
        body, name=name, grid=(kk // tk, nn // tn),
        in_specs=[pl.BlockSpec((rows, tk), lambda k, n: (0, k)),
                  pl.BlockSpec((rows, tn), lambda k, n: (0, n))],
        out_specs=out_spec, out_shape=out_shape,
        scratch_shapes=[pltpu.VMEM((tk, rows), BF16)] if keep_at else [],
        compiler_params=_cparams("parallel", "arbitrary"),
    )(a, b)


def dw_mlp(u, dhb, z2, dup, tf):
    rows, dff = u.shape
    d = z2.shape[1]
    nf = dff // tf

    def body(u_ref, dh_ref, z_ref, dup_ref, dwd_ref, dwu_ref, zt):
        @pl.when(pl.program_id(0) == 0)
        def _():
            zt[...] = z_ref[...].T

        uf = u_ref[...].astype(F32)
        dwd_ref[...] = _dot_tn((uf * uf).astype(BF16), dh_ref[...]).astype(BF16)
        dwu_ref[...] = _dot(zt[...], dup_ref[...]).astype(BF16)

    col = lambda j: (0, j)
    fix = lambda j: (0, 0)
    return pl.pallas_call(
        body, name="dw_mlp", grid=(nf,),
        in_specs=[pl.BlockSpec((rows, tf), col), pl.BlockSpec((rows, d), fix),
                  pl.BlockSpec((rows, d), fix), pl.BlockSpec((rows, tf), col)],
        out_specs=[pl.BlockSpec((None, tf, d), lambda j: (j, 0, 0)),
                   pl.BlockSpec((None, d, tf), lambda j: (j, 0, 0))],
        out_shape=[jax.ShapeDtypeStruct((nf, tf, d), BF16), jax.ShapeDtypeStruct((nf, d, tf), BF16)],
        scratch_shapes=[pltpu.VMEM((d, rows), BF16)],
        compiler_params=_cparams("arbitrary"),
    )(u, dhb, z2, dup)


def out_proj_bwd(dh2, o, rec, ga, gr, wout, tm):
    tp, d = dh2.shape
    aw, rw = o.shape[1], rec.shape[1]

    def body(dh_ref, o_ref, rec_ref, ga_ref, gr_ref, w_ref, do_ref, drec_ref, dga_ref, dgr_ref):
        i = pl.program_id(0)
        dmix = _dot_nt(dh_ref[...].astype(BF16), w_ref[...])
        do, dga = _rms_bwd(o_ref[...], ga_ref[...], dmix[:, 0:aw])
        drec, dgr = _rms_bwd(rec_ref[...], gr_ref[...], dmix[:, aw:aw + rw])
        do_ref[...] = do
        drec_ref[...] = drec
        _accumulate(dga_ref, dga, i == 0)
        _accumulate(dgr_ref, dgr, i == 0)

    row = lambda i: (i, 0)
    fix = lambda i: (0, 0)
    return pl.pallas_call(
        body, name="out_proj_bwd", grid=(tp // tm,),
        in_specs=[pl.BlockSpec((tm, d), row), pl.BlockSpec((tm, aw), row), pl.BlockSpec((tm, rw), row),
                  pl.BlockSpec((1, aw), fix), pl.BlockSpec((1, rw), fix), pl.BlockSpec((d, d), fix)],
        out_specs=[pl.BlockSpec((tm, aw), row), pl.BlockSpec((tm, rw), row),
                   pl.BlockSpec((1, aw), fix), pl.BlockSpec((1, rw), fix)],
        out_shape=[jax.ShapeDtypeStruct((tp, aw), F32), jax.ShapeDtypeStruct((tp, rw), F32),
                   jax.ShapeDtypeStruct((1, aw), F32), jax.ShapeDtypeStruct((1, rw), F32)],
        compiler_params=_cparams("arbitrary"),
    )(dh2, o, rec, ga, gr, wout)


def rec_bwd(drec, hr, xc, rest, convw, convb, wga, bga, wgx, bgx, lru, rw):
    tp = rest.shape[0]
    ng = rw // LANES

    def body(drec_ref, hr_ref, xc_ref, xr_ref, yr_ref, cw_ref, cb_ref, wga_ref, bga_ref, wgx_ref, bgx_ref, l_ref,
             dxr_ref, dyr_ref, dwga_ref, dwgx_ref, vec_ref, a_s, u_s, lam_s):
        xc = xc_ref[...]
        h = hr_ref[...]
        yr = yr_ref[...]
        drec = drec_ref[...]
        xcb, r, ig, ls, log_a, a, mult = _gates(xc, wga_ref, bga_ref, wgx_ref, bgx_ref, l_ref)
        dyr_ref[...] = (drec * h * _gelu_grad(yr)).astype(BF16)
        a_s[...] = _shift_up(a, 1, tp)
        u_s[...] = drec * _gelu(yr)
        _scan_rows(a_s, u_s, lam_s, tp, reverse=True)
        lam = lam_s[...]
        da = lam * _shift_down(h, 1, tp)
        dmult = lam * ig * xc
        dig = lam * mult * xc
        dxc = lam * mult * ig
        a2 = jnp.exp(2.0 * log_a)
        dlog_a = da * a - dmult * a2 / mult
        dr = dlog_a * (RG_C * ls)
        dl = jnp.sum(dlog_a * (RG_C * r), axis=0, keepdims=True) * _sigmoid(-l_ref[...])
        dpa = dr * r * (1.0 - r)
        dpx = dig * ig * (1.0 - ig)
        dpab = dpa.astype(BF16)
        dpxb = dpx.astype(BF16)
        dxc = dxc + _dot_nt(dpab, wga_ref[...]) + _dot_nt(dpxb, wgx_ref[...])
        dwga_ref[...] = _dot_tn(xcb, dpab)
        dwgx_ref[...] = _dot_tn(xcb, dpxb)
        xr = xr_ref[...]
        dxr = cw_ref[CONV_WIDTH - 1:CONV_WIDTH, :] * dxc
        for k in range(1, CONV_WIDTH):
            dxr = dxr + cw_ref[CONV_WIDTH - 1 - k:CONV_WIDTH - k, :] * _shift_up(dxc, k, tp)
        dxr_ref[...] = dxr.astype(BF16)
        for k in range(CONV_WIDTH):
            vec_ref[k:k + 1, :] = jnp.sum(dxc * _shift_down(xr, CONV_WIDTH - 1 - k, tp), axis=0, keepdims=True)
        vec_ref[4:5, :] = jnp.sum(dxc, axis=0, keepdims=True)
        vec_ref[5:6, :] = jnp.sum(dpa, axis=0, keepdims=True)
        vec_ref[6:7, :] = jnp.sum(dpx, axis=0, keepdims=True)
        vec_ref[7:8, :] = dl

    col = lambda g: (0, g)
    vec = pl.BlockSpec((1, LANES), col)
    big = pl.BlockSpec((tp, LANES), col)
    sq = pl.BlockSpec((None, LANES, LANES), lambda g: (g, 0, 0))
    return pl.pallas_call(
        body, name="rec_bwd", grid=(ng,),
        in_specs=[big, big, big, big, pl.BlockSpec((tp, LANES), lambda g: (0, ng + g)),
                  pl.BlockSpec((CONV_WIDTH, LANES), col), vec, sq, vec, sq, vec, vec],
        out_specs=[big, big, sq, sq, pl.BlockSpec((None, SUBLANES, LANES), lambda g: (g, 0, 0))],
        out_shape=[jax.ShapeDtypeStruct((tp, rw), BF16), jax.ShapeDtypeStruct((tp, rw), BF16),
                   jax.ShapeDtypeStruct((ng, LANES, LANES), F32), jax.ShapeDtypeStruct((ng, LANES, LANES), F32),
                   jax.ShapeDtypeStruct((ng, SUBLANES, LANES), F32)],
        scratch_shapes=[pltpu.VMEM((tp, LANES), F32)] * 3,
        compiler_params=_cparams("parallel"),
    )(drec, hr, xc, rest, rest, convw, convb, wga, bga, wgx, bgx, lru)


def attn_bwd(qkv, do, o, lset, c, ct, nh):
    tp = qkv.shape[0]
    npair = nh // 2
    aw = nh * HEAD_DIM
    tiles = _att_tiles(tp)

    def body(q_ref, k_ref, v_ref, do_ref, o_ref, lset_ref, c_ref, ct_ref,
             dq_ref, dk_ref, dv_ref, drow_ref, dcol_ref, dk_acc, dv_acc):
        p = pl.program_id(0)
        dk_acc[...] = jnp.zeros_like(dk_acc)
        dv_acc[...] = jnp.zeros_like(dv_acc)
        dcol_ref[...] = jnp.zeros_like(dcol_ref)
        drow_ref[...] = jnp.zeros_like(drow_ref)
        for r0, nr, nk in tiles:
            rs = slice(r0, r0 + nr)
            causal = (r0 + lax.broadcasted_iota(jnp.int32, (nk, nr), 1)
                      >= lax.broadcasted_iota(jnp.int32, (nk, nr), 0))
            cblk = c_ref[0:nk, :]
            ctb = ct_ref[:, rs]
            for hh in range(2):
                head = 2 * p + hh
                hs = slice(hh * HEAD_DIM, (hh + 1) * HEAD_DIM)
                q = q_ref[rs, hs]
                k = k_ref[0:nk, hs]
                dof = do_ref[rs, hs]
                do16 = dof.astype(BF16)
                delta = jnp.sum(dof * o_ref[rs, hs], axis=1, keepdims=True)
                delta_row = jnp.broadcast_to(delta, (nr, LANES)).T[0:1, :]
                s_t = _dot_nt(k, q * ATT_SCALE) + (_pick_row(ctb, head) - _pick_col(cblk, head))
                p_t = jnp.where(causal, jnp.exp(s_t - lset_ref[hh:hh + 1, rs]), 0.0)
                ds_t = p_t * (_dot_nt(v_ref[0:nk, hs], do16) - delta_row)
                p16 = p_t.astype(BF16)
                ds16 = ds_t.astype(BF16)
                dv_acc[0:nk, hs] += _dot(p16, do16)
                dk_acc[0:nk, hs] += _dot(ds16, q) * ATT_SCALE
                dq_ref[rs, hs] = (_dot_tn(ds16, k) * ATT_SCALE).astype(BF16)
                drow_ref[hh:hh + 1, rs] = jnp.sum(ds_t, axis=0, keepdims=True)
                dcol_ref[0:nk, hs] -= jnp.broadcast_to(jnp.sum(ds_t, axis=1, keepdims=True), (nk, HEAD_DIM))
        dk_ref[...] = dk_acc[...].astype(BF16)
        dv_ref[...] = dv_acc[...].astype(BF16)

    pair = lambda p: (0, p)
    return pl.pallas_call(
        body, name="attn_bwd", grid=(npair,),
        in_specs=[pl.BlockSpec((tp, LANES), pair),
                  pl.BlockSpec((tp, LANES), lambda p: (0, npair + p)),
                  pl.BlockSpec((tp, LANES), lambda p: (0, 2 * npair + p)),
                  pl.BlockSpec((tp, LANES), pair),
                  pl.BlockSpec((tp, LANES), pair),
                  pl.BlockSpec((None, SUBLANES, tp), lambda p: (p, 0, 0)),
                  pl.BlockSpec((tp, LANES), lambda p: (0, 0)),
                  pl.BlockSpec((SUBLANES, tp), lambda p: (0, 0))],
        out_specs=[pl.BlockSpec((tp, LANES), pair), pl.BlockSpec((tp, LANES), pair),
                   pl.BlockSpec((tp, LANES), pair),
                   pl.BlockSpec((None, SUBLANES, tp), lambda p: (p, 0, 0)),
                   pl.BlockSpec((tp, LANES), pair)],
        out_shape=[jax.ShapeDtypeStruct((tp, aw), BF16), jax.ShapeDtypeStruct((tp, aw), BF16),
                   jax.ShapeDtypeStruct((tp, aw), BF16),
                   jax.ShapeDtypeStruct((npair, SUBLANES, tp), F32),
                   jax.ShapeDtypeStruct((tp, aw), F32)],
        scratch_shapes=[pltpu.VMEM((tp, LANES), F32), pltpu.VMEM((tp, LANES), F32)],
        compiler_params=_cparams("parallel"),
    )(qkv, qkv, qkv, do, o, lset, c, ct)


def fgate_bwd(dct8, drs, rest, bf_pad, fcol):
    tp = rest.shape[0]
    aw = drs.shape[1]
    nb = tp // ATT_BLOCK
    B = ATT_BLOCK

    def body(d_ref, drs_ref, f_ref, b_ref, dfl_ref, db_ref, pad_s):
        r_i = lax.broadcasted_iota(jnp.int32, (B, B), 0)
        c_i = lax.broadcasted_iota(jnp.int32, (B, B), 1)
        triu = (c_i >= r_i).astype(BF16)
        sel = (lax.broadcasted_iota(jnp.int32, (aw, LANES), 0)
               == HEAD_DIM * lax.broadcasted_iota(jnp.int32, (aw, LANES), 1)).astype(BF16)
        carry = jnp.zeros((1, LANES), F32)
        db = jnp.zeros((1, LANES), F32)
        pad_s[...] = jnp.zeros_like(pad_s)
        for i in range(nb - 1, -1, -1):
            sl = slice(i * B, (i + 1) * B)
            pad_s[0:SUBLANES, :] = d_ref[:, sl]
            dc = pad_s[...].T + _dot_split3(drs_ref[sl, :], sel)
            rc = _split3_dot(triu, dc)
            dlf = rc + carry
            carry = carry + rc[0:1, :]
            dfl = dlf * _sigmoid(-(f_ref[sl, :] + b_ref[...]))
            dfl_ref[sl, :] = dfl.astype(BF16)
            db = db + jnp.sum(dfl, axis=0, keepdims=True)
        db_ref[...] = db

    return pl.pallas_call(
        body, name="fgate_bwd", grid=(1,),
        in_specs=[pl.BlockSpec((SUBLANES, tp), lambda i: (0, 0)),
                  pl.BlockSpec((tp, aw), lambda i: (0, 0)),
                  pl.BlockSpec((tp, LANES), lambda i: (0, fcol)),
                  pl.BlockSpec((1, LANES), lambda i: (0, 0))],
        out_specs=[pl.BlockSpec((tp, LANES), lambda i: (0, 0)),
                   pl.BlockSpec((1, LANES), lambda i: (0, 0))],
        out_shape=[jax.ShapeDtypeStruct((tp, LANES), BF16), jax.ShapeDtypeStruct((1, LANES), F32)],
        scratch_shapes=[pltpu.VMEM((B, B), F32)],
        compiler_params=_cparams("arbitrary"),
    )(dct8, drs, rest, bf_pad)


def in_proj_bwd(dh2, parts, w_in_t, wrest_t, h, g1, tm):
    tp, d = h.shape
    dq, dk, dv, dxr, dyr, dfl = parts
    aw, rw = dq.shape[1], dxr.shape[1]

    def body(dh2_ref, dq_ref, dk_ref, dv_ref, dxr_ref, dyr_ref, dfl_ref, wq_ref, wr_ref, h_ref, g_ref,
             dh_ref, dg_ref):
        i = pl.program_id(0)
        dz = _dot(dq_ref[...], wq_ref[0:aw, :])
        dz += _dot(dk_ref[...], wq_ref[aw:2 * aw, :])
        dz += _dot(dv_ref[...], wq_ref[2 * aw:3 * aw, :])
        dz += _dot(dxr_ref[...], wr_ref[0:rw, :])
        dz += _dot(dyr_ref[...], wr_ref[rw:2 * rw, :])
        dz += _dot(dfl_ref[...], wr_ref[2 * rw:2 * rw + LANES, :])
        dx, dg = _rms_bwd(h_ref[...], g_ref[...], dz)
        dh_ref[...] = dh2_ref[...] + dx
        _accumulate(dg_ref, dg, i == 0)

    row = lambda i: (i, 0)
    fix = lambda i: (0, 0)
    return pl.pallas_call(
        body, name="in_proj_bwd", grid=(tp // tm,),
        in_specs=[pl.BlockSpec((tm, d), row),
                  pl.BlockSpec((tm, aw), row), pl.BlockSpec((tm, aw), row), pl.BlockSpec((tm, aw), row),
                  pl.BlockSpec((tm, rw), row), pl.BlockSpec((tm, rw), row), pl.BlockSpec((tm, LANES), row),
                  pl.BlockSpec((3 * aw, d), fix), pl.BlockSpec(wrest_t.shape, fix),
                  pl.BlockSpec((tm, d), row), pl.BlockSpec((1, d), fix)],
        out_specs=[pl.BlockSpec((tm, d), row), pl.BlockSpec((1, d), fix)],
        out_shape=[jax.ShapeDtypeStruct((tp, d), F32), jax.ShapeDtypeStruct((1, d), F32)],
        compiler_params=_cparams("arbitrary"),
    )(dh2, dq, dk, dv, dxr, dyr, dfl, w_in_t, wrest_t, h, g1)


def dw_in_t(z, parts, nh, tr):
    tp, d = z.shape
    dq, dk, dv, dxr, dyr, dfl = parts
    aw, rw = dq.shape[1], dxr.shape[1]
    d_in = 3 * aw + nh + 2 * rw
    nr = tp // tr
    offs = [(0, aw), (aw, aw), (2 * aw, aw), (3 * aw + nh, rw), (3 * aw + nh + rw, rw)]

    def body(z_ref, dq_ref, dk_ref, dv_ref, dxr_ref, dyr_ref, dfl_ref, o_ref, acc):
        r = pl.program_id(0)

        @pl.when(r == 0)
        def _():
            acc[...] = jnp.zeros_like(acc)

        zt = z_ref[...]
        for (o, n), ref in zip(offs, (dq_ref, dk_ref, dv_ref, dxr_ref, dyr_ref)):
            acc[o:o + n, :] += _dot_tn(ref[...], zt)
        acc[3 * aw:3 * aw + nh, :] += _dot_tn(dfl_ref[...], zt)[0:nh, :]

        @pl.when(r == nr - 1)
        def _():
            o_ref[...] = acc[...].astype(BF16)

    row = lambda r: (r, 0)
    return pl.pallas_call(
        body, name="dw_in", grid=(nr,),
        in_specs=[pl.BlockSpec((tr, d), row),
                  pl.BlockSpec((tr, aw), row), pl.BlockSpec((tr, aw), row), pl.BlockSpec((tr, aw), row),
                  pl.BlockSpec((tr, rw), row), pl.BlockSpec((tr, rw), row), pl.BlockSpec((tr, LANES), row)],
        out_specs=pl.BlockSpec((d_in, d), lambda r: (0, 0)),
        out_shape=jax.ShapeDtypeStruct((d_in, d), BF16),
        scratch_shapes=[pltpu.VMEM((d_in, d), F32)],
        compiler_params=_cparams("arbitrary"),
    )(z, dq, dk, dv, dxr, dyr, dfl)


def _place():
    return lax.axis_index("x"), lax.axis_index("y"), lax.axis_index("c")


HBM = pl.BlockSpec(memory_space=pltpu.HBM)
SEM = pl.BlockSpec(memory_space=pltpu.SEMAPHORE)
EFFECT = pltpu.SideEffectType.DATAFLOW_SIDE_EFFECTING


def _in_hbm(a):
    return pltpu.with_memory_space_constraint(a, pltpu.HBM)


def _as_list(a):
    return list(a) if isinstance(a, (list, tuple)) else [a]


def _gather_targets(x, y, c):
    return [(x, y, 1 - c), (1 - x, y, c), (x, 1 - y, c), (1 - x, 1 - y, c)]


def _slot(t):
    return 4 * t[0] + 2 * t[1] + t[2]


def gather_start(groups, name):
    flat = [a for g in groups for a in g]
    n = len(flat)
    ng = len(groups)
    lands = [lax.empty((N_DEV,) + a.shape, a.dtype) for a in flat]

    def body(*refs):
        src, land = refs[:n], refs[n:2 * n]
        sems = refs[2 * n:2 * n + 2 * ng]
        token = refs[-1]
        x, y, c = _place()
        me = 4 * x + 2 * y + c
        i = 0
        for gi, g in enumerate(groups):
            for a in range(len(g)):
                for k, t in enumerate(_gather_targets(x, y, c)):
                    pltpu.make_async_remote_copy(
                        src_ref=src[i], dst_ref=land[i].at[me],
                        send_sem=sems[2 * gi].at[4 * a + k], recv_sem=sems[2 * gi + 1].at[4 * a + k],
                        device_id=t, device_id_type=MESH).start()
                i += 1
        token[...] = jnp.zeros_like(token)

    sem_shapes = []
    for g in groups:
        sem_shapes += [pltpu.SemaphoreType.DMA((4 * len(g),)), pltpu.SemaphoreType.DMA((4 * len(g),))]
    out = pl.pallas_call(
        body, name=name,
        out_shape=sem_shapes + [pltpu.HBM(a.shape, a.dtype) for a in flat + lands]
        + [jax.ShapeDtypeStruct((SUBLANES, LANES), F32)],
        in_specs=[HBM] * (2 * n),
        out_specs=[SEM] * (2 * ng) + [HBM] * (2 * n) + [pl.BlockSpec(memory_space=pltpu.VMEM)],
        input_output_aliases={i: 2 * ng + i for i in range(2 * n)},
        compiler_params=pltpu.CompilerParams(has_side_effects=EFFECT),
    )(*[_in_hbm(a) for a in flat + lands])
    sems = out[:2 * ng]
    thru = out[2 * ng:2 * ng + 2 * n]
    srcs_t, lands_t = thru[:n], thru[n:]
    res, i = [], 0
    for gi, g in enumerate(groups):
        res.append((sems[2 * gi], sems[2 * gi + 1], srcs_t[i:i + len(g)], lands_t[i:i + len(g)]))
        i += len(g)
    return res, out[-1]


def gather_wait(send, recv, srcs, lands, after, name):
    n = len(srcs)

    def body(*refs):
        src, land = refs[:n], refs[n:2 * n]
        send_sem, recv_sem = refs[2 * n], refs[2 * n + 1]
        x, y, c = _place()
        for a in range(n):
            for k, t in enumerate(_gather_targets(x, y, c)):
                cp = pltpu.make_async_remote_copy(
                    src_ref=src[a], dst_ref=land[a].at[_slot(t)],
                    send_sem=send_sem.at[4 * a + k], recv_sem=recv_sem.at[4 * a + k],
                    device_id=t, device_id_type=MESH)
                cp.wait_send()
                cp.wait_recv()

    out = pl.pallas_call(
        body, name=name,
        out_shape=[pltpu.HBM(a.shape, a.dtype) for a in list(srcs) + list(lands)],
        in_specs=[HBM] * (2 * n) + [SEM, SEM] + [ANY] * len(_as_list(after)),
        out_specs=[HBM] * (2 * n),
        input_output_aliases={i: i for i in range(2 * n)},
        compiler_params=pltpu.CompilerParams(has_side_effects=EFFECT),
    )(*srcs, *lands, send, recv, *_as_list(after))
    return out[:n], out[n:]


def forward_start(lands, name):
    n = len(lands)

    def body(*refs):
        land = refs[:n]
        send_sem, recv_sem = refs[n], refs[n + 1]
        token = refs[-1]
        x, y, c = _place()
        for a in range(n):
            for j, chip in enumerate([(1 - x, y), (x, 1 - y), (1 - x, 1 - y)]):
                blk = land[a].at[_slot((*chip, c))]
                pltpu.make_async_remote_copy(src_ref=blk, dst_ref=blk, send_sem=send_sem.at[3 * a + j],
                                             recv_sem=recv_sem.at[3 * a + j], device_id=(x, y, 1 - c),
                                             device_id_type=MESH).start()
        token[...] = jnp.zeros_like(token)

    out = pl.pallas_call(
        body, name=name,
        out_shape=[pltpu.SemaphoreType.DMA((3 * n,)), pltpu.SemaphoreType.DMA((3 * n,))]
        + [pltpu.HBM(a.shape, a.dtype) for a in lands] + [jax.ShapeDtypeStruct((SUBLANES, LANES), F32)],
        in_specs=[HBM] * n,
        out_specs=[SEM, SEM] + [HBM] * n + [pl.BlockSpec(memory_space=pltpu.VMEM)],
        input_output_aliases={i: 2 + i for i in range(n)},
        compiler_params=pltpu.CompilerParams(has_side_effects=EFFECT),
    )(*[_in_hbm(a) for a in lands])
    return out[0], out[1], out[2:2 + n], out[-1][0, 0]


def forward_wait(send, recv, lands, after, name):
    n = len(lands)

    def body(*refs):
        land = refs[:n]
        send_sem, recv_sem = refs[n], refs[n + 1]
        x, y, c = _place()
        for a in range(n):
            for j, chip in enumerate([(1 - x, y), (x, 1 - y), (1 - x, 1 - y)]):
                cp = pltpu.make_async_remote_copy(
                    src_ref=land[a].at[_slot((*chip, c))], dst_ref=land[a].at[_slot((*chip, 1 - c))],
                    send_sem=send_sem.at[3 * a + j], recv_sem=recv_sem.at[3 * a + j],
                    device_id=(x, y, 1 - c), device_id_type=MESH)
                cp.wait_send()
                cp.wait_recv()

    return pl.pallas_call(
        body, name=name,
        out_shape=[pltpu.HBM(a.shape, a.dtype) for a in lands],
        in_specs=[HBM] * n + [SEM, SEM, ANY],
        out_specs=[HBM] * n,
        input_output_aliases={i: i for i in range(n)},
        compiler_params=pltpu.CompilerParams(has_side_effects=EFFECT),
    )(*lands, send, recv, after)


def _relations():
    return [(dx, dy, dc) for dx in (0, 1) for dy in (0, 1) for dc in (0, 1) if dx + dy + dc]


def _peer(x, y, c, rel):
    return ((1 - x) if rel[0] else x, (1 - y) if rel[1] else y, (1 - c) if rel[2] else c)


def exchange_start(srcs, lands, layer, name):
    n = len(srcs)

    def body(*refs):
        src, land = refs[:n], refs[n:2 * n]
        send_sem, recv_sem = refs[2 * n], refs[2 * n + 1]
        token = refs[-1]
        x, y, c = _place()
        me = 4 * x + 2 * y + c
        for k, rel in enumerate(_relations()):
            peer = _peer(x, y, c, rel)
            for a in range(n):
                pltpu.make_async_remote_copy(
                    src_ref=src[a] if layer is None else src[a].at[_slot(peer)],
                    dst_ref=land[a].at[me] if layer is None else land[a].at[me, layer],
                    send_sem=send_sem.at[7 * a + k], recv_sem=recv_sem.at[7 * a + k],
                    device_id=peer, device_id_type=MESH).start()
        token[...] = jnp.zeros_like(token)

    out = pl.pallas_call(
        body, name=name,
        out_shape=[pltpu.SemaphoreType.DMA((7 * n,)), pltpu.SemaphoreType.DMA((7 * n,))]
        + [pltpu.HBM(a.shape, a.dtype) for a in list(srcs) + list(lands)]
        + [jax.ShapeDtypeStruct((SUBLANES, LANES), F32)],
        in_specs=[HBM] * (2 * n),
        out_specs=[SEM, SEM] + [HBM] * (2 * n) + [pl.BlockSpec(memory_space=pltpu.VMEM)],
        input_output_aliases={i: 2 + i for i in range(2 * n)},
        compiler_params=pltpu.CompilerParams(has_side_effects=EFFECT),
    )(*[_in_hbm(a) for a in list(srcs) + list(lands)])
    return out[0], out[1], out[2:2 + n], out[2 + n:2 + 2 * n], out[-1][0, 0]


def exchange_wait(send, recv, srcs, lands, after, layer, name):
    n = len(srcs)

    def body(*refs):
        src, land = refs[:n], refs[n:2 * n]
        send_sem, recv_sem = refs[2 * n], refs[2 * n + 1]
        x, y, c = _place()
        for k, rel in enumerate(_relations()):
            peer = _peer(x, y, c, rel)
            for a in range(n):
                cp = pltpu.make_async_remote_copy(
                    src_ref=src[a] if layer is None else src[a].at[_slot(peer)],
                    dst_ref=land[a].at[_slot(peer)] if layer is None else land[a].at[_slot(peer), layer],
                    send_sem=send_sem.at[7 * a + k], recv_sem=recv_sem.at[7 * a + k],
                    device_id=peer, device_id_type=MESH)
                cp.wait_send()
                cp.wait_recv()

    out = pl.pallas_call(
        body, name=name,
        out_shape=[pltpu.HBM(a.shape, a.dtype) for a in list(srcs) + list(lands)],
        in_specs=[HBM] * (2 * n) + [SEM, SEM] + [ANY] * len(_as_list(after)),
        out_specs=[HBM] * (2 * n),
        input_output_aliases={i: i for i in range(2 * n)},
        compiler_params=pltpu.CompilerParams(has_side_effects=EFFECT),
    )(*srcs, *lands, send, recv, *_as_list(after))
    return out[:n], out[n:]


def _adamw_math(g, w, m, v):
    m = ADAM_B1 * m + (1.0 - ADAM_B1) * g
    v = ADAM_B2 * v + (1.0 - ADAM_B2) * (g * g)
    m_hat = m / (1.0 - ADAM_B1 ** ADAM_STEP)
    v_hat = v / (1.0 - ADAM_B2 ** ADAM_STEP)
    delta = -ADAM_LR * (m_hat / (jnp.sqrt(v_hat) + ADAM_EPS) + ADAM_WD * w)
    return delta, m, v


def sum_adamw(parts, w, m, v, tr, name):
    npart, rows, cols = parts.shape

    def body(p_ref, w_ref, m_ref, v_ref, g_ref, d_ref, nm_ref, nv_ref):
        g = p_ref[0].astype(F32)
        for p in range(1, npart):
            g = g + p_ref[p].astype(F32)
        delta, nm, nv = _adamw_math(g, w_ref[...], m_ref[...], v_ref[...])
        g_ref[...] = g
        d_ref[...] = delta
        nm_ref[...] = nm
        nv_ref[...] = nv

    blk = pl.BlockSpec((tr, cols), lambda i: (i, 0))
    return pl.pallas_call(
        body, name=name, grid=(rows // tr,),
        in_specs=[pl.BlockSpec((npart, tr, cols), lambda i: (0, i, 0)), blk, blk, blk],
        out_specs=[blk] * 4,
        out_shape=[jax.ShapeDtypeStruct((rows, cols), F32)] * 4,
        compiler_params=_cparams("parallel"),
    )(parts, w, m, v)


def sum_adamw_t(parts, w, m, v, name):
    npart, nl, rows, cols = parts.shape

    def body(p_ref, w_ref, m_ref, v_ref, g_ref, d_ref, nm_ref, nv_ref):
        g = p_ref[0].astype(F32)
        for p in range(1, npart):
            g = g + p_ref[p].astype(F32)
        delta, nm, nv = _adamw_math(g, w_ref[...], m_ref[...], v_ref[...])
        g_ref[...] = g
        d_ref[...] = delta
        nm_ref[...] = nm
        nv_ref[...] = nv

    blk = pl.BlockSpec((None, rows, cols), lambda l: (l, 0, 0))
    return pl.pallas_call(
        body, name=name, grid=(nl,),
        in_specs=[pl.BlockSpec((npart, None, rows, cols), lambda l: (0, l, 0, 0)), blk, blk, blk],
        out_specs=[blk] * 4,
        out_shape=[jax.ShapeDtypeStruct((nl, rows, cols), F32)] * 4,
        compiler_params=_cparams("parallel"),
    )(parts, w, m, v)


def adamw_group(gs, ws, ms, vs, name):
    n = len(gs)

    def body(*refs):
        g, w, m, v, outs = refs[:n], refs[n:2 * n], refs[2 * n:3 * n], refs[3 * n:4 * n], refs[4 * n:]
        for i in range(n):
            delta, nm, nv = _adamw_math(g[i][...], w[i][...], m[i][...], v[i][...])
            outs[i][...] = delta
            outs[n + i][...] = nm
            outs[2 * n + i][...] = nv

    vmem = pl.BlockSpec(memory_space=pltpu.VMEM)
    out = pl.pallas_call(
        body, name=name,
        in_specs=[vmem] * (4 * n), out_specs=[vmem] * (3 * n),
        out_shape=[jax.ShapeDtypeStruct(a.shape, F32) for a in list(ws) * 3],
        compiler_params=_cparams(),
    )(*gs, *ws, *ms, *vs)
    return out[:n], out[n:2 * n], out[2 * n:]


def sum_parts(parts, name):
    npart, rows, cols = parts.shape

    def body(p_ref, g_ref):
        g = p_ref[0].astype(F32)
        for p in range(1, npart):
            g = g + p_ref[p].astype(F32)
        g_ref[...] = g

    return pl.pallas_call(
        body, name=name, grid=(1,),
        in_specs=[pl.BlockSpec((npart, rows, cols), lambda i: (0, 0, 0))],
        out_specs=pl.BlockSpec((rows, cols), lambda i: (0, 0)),
        out_shape=jax.ShapeDtypeStruct((rows, cols), F32),
        compiler_params=_cparams("arbitrary"),
    )(parts)


def _round_up(n, m):
    return (n + m - 1) // m * m


def _block_diag_pairs(w):
    nb, b, _ = w.shape
    per = LANES // b
    ng = nb // per
    w = w.reshape(ng, per, b, b)
    eye = jnp.eye(per, dtype=w.dtype)
    out = jnp.einsum('gpij,pq->gpiqj', w, eye).reshape(ng, LANES, LANES)
    return out.astype(BF16)


def _block_diag_extract(g, b):
    ng = g.shape[0]
    per = LANES // b
    g = g.reshape(ng, per, b, per, b)
    idx = jnp.arange(per)
    return g[:, idx, :, idx, :].transpose(1, 0, 2, 3).reshape(ng * per, b, b)


def _tiles(v):
    v = v.reshape(-1)
    n = _round_up(v.shape[0], SUBLANES * LANES)
    return jnp.pad(v, (0, n - v.shape[0])).reshape(-1, LANES)


SMALL = ['attn_norm_g', 'b_f', 'conv_w', 'conv_b', 'w_gate_a', 'b_gate_a', 'w_gate_x', 'b_gate_x',
         'lru_L', 'attn_out_g', 'rec_out_g', 'mlp_norm_g', 'final_g', 'meta']


def _pack(d):
    return jnp.concatenate([_tiles(d[n]) for n in SMALL], axis=0)


def _unpack(vec, shapes):
    out, r = {}, 0
    for n in SMALL:
        size = math.prod(shapes[n])
        nr = _round_up(size, SUBLANES * LANES) // LANES
        out[n] = vec[r:r + nr].reshape(-1)[:size].reshape(shapes[n])
        r += nr
    return out


def _row_tile(tp):
    return tp // 4 if (tp // 4) % 16 == 0 else tp


def local_step(x, tgt, meta, small, hooks):
    s, d = x.shape
    t_real = s + N_META
    tp = _round_up(t_real, ATT_BLOCK)
    depth = small['attn_norm_g'].shape[0]
    nh = small['b_f'].shape[1]
    rw = small['conv_b'].shape[1]
    blk = small['w_gate_a'].shape[2]
    tm = _row_tile(tp)
    tm2 = tp // 2
    fcol = 2 * rw // LANES

    h = jnp.concatenate([meta, x, jnp.zeros((tp - t_real, d), F32)], axis=0)
    tgt_p = jnp.pad(tgt, ((N_META, tp - t_real), (0, 0)))
    row = lambda v: v.reshape(1, -1)
    bf_pad = jnp.pad(small['b_f'], ((0, 0), (0, LANES - nh)))

    saved = []
    for l in range(depth):
        w_in_t, wrest_t, wout, tok_w = hooks.mixer_weights(l, h)
        wga = _block_diag_pairs(small['w_gate_a'][l])
        wgx = _block_diag_pairs(small['w_gate_x'][l])
        z, qkv, rest = in_proj(h, row(small['attn_norm_g'][l]) + tok_w, w_in_t, wrest_t, 3 * nh * HEAD_DIM, tm)
        c, ct = fgate_fwd(rest, bf_pad[l:l + 1], fcol)
        o, lset = attn_fwd(qkv, c, nh)
        rec, hr, xc = rec_fwd(rest, small['conv_w'][l], row(small['conv_b'][l]), wga, row(small['b_gate_a'][l]),
                              wgx, row(small['b_gate_x'][l]), row(small['lru_L'][l]), rw)
        gup, gdown, tok_w = hooks.mlp_weights(l, rec)
        h2, mix, z2 = out_proj(h, o, rec, row(small['attn_out_g'][l]), row(small['rec_out_g'][l]), wout,
                               row(small['mlp_norm_g'][l]) + tok_w, tm)
        u, h3 = mlp_fwd(z2, h2, gup, gdown, tm2)
        saved.append(dict(h=h, z=z, qkv=qkv, rest=rest, c=c, ct=ct, o=o, lset=lset, rec=rec, hr=hr, xc=xc,
                          h2=h2, mix=mix, z2=z2, u=u, wga=wga, wgx=wgx,
                          w_in_t=w_in_t, wrest_t=wrest_t, wout=wout, gup=gup, gdown=gdown))
        h = h3

    dh, dgf, loss = loss_head(h, row(small['final_g']), tgt_p, t_real, tm)

    gs = {n: [None] * depth for n in SMALL if n not in ('final_g', 'meta')}
    tok = jnp.zeros((), F32)
    for l in reversed(range(depth)):
        sv = saved[l]
        gup, gdown = sv['gup'], sv['gdown']
        tf = gup.shape[2]
        dup, dh2, dg2, dhb = mlp_bwd(dh, sv['u'], sv['h2'], row(small['mlp_norm_g'][l]) + tok, gup, gdown, tm)
        gs['mlp_norm_g'][l] = dg2[0]
        do, drec, dga, dgr = out_proj_bwd(dh2, sv['o'], sv['rec'], row(small['attn_out_g'][l]),
                                          row(small['rec_out_g'][l]), sv['wout'], tm)
        gs['attn_out_g'][l] = dga[0]
        gs['rec_out_g'][l] = dgr[0]
        dw_down, dw_up = dw_mlp(sv['u'], dhb, sv['z2'], dup, tf)
        blocks = dict(
            w_down=dw_down, w_up=dw_up,
            w_out=mm_tn(sv['mix'], dh2, tk=d, tn=d // 2, out_dtype=BF16,
                        name="dw_out").reshape(N_DEV, d // N_DEV, d))
        tok = hooks.grads_ready(l, 'mlp', blocks)
        dxr, dyr, dwga, dwgx, vec = rec_bwd(drec, sv['hr'], sv['xc'], sv['rest'], small['conv_w'][l],
                                            row(small['conv_b'][l]) + tok, sv['wga'], row(small['b_gate_a'][l]),
                                            sv['wgx'], row(small['b_gate_x'][l]), row(small['lru_L'][l]), rw)
        gs['w_gate_a'][l] = _block_diag_extract(dwga, blk)
        gs['w_gate_x'][l] = _block_diag_extract(dwgx, blk)
        vec = vec.transpose(1, 0, 2).reshape(SUBLANES, rw)
        gs['conv_w'][l] = vec[0:CONV_WIDTH]
        gs['conv_b'][l] = vec[4]
        gs['b_gate_a'][l] = vec[5]
        gs['b_gate_x'][l] = vec[6]
        gs['lru_L'][l] = vec[7]
        dq, dk, dv, drow, dcol = attn_bwd(sv['qkv'], do, sv['o'], sv['lset'], sv['c'], sv['ct'] + tok, nh)
        drow8 = drow[:, 0:2, :].reshape(nh, tp)
        if nh < SUBLANES:
            drow8 = jnp.pad(drow8, ((0, SUBLANES - nh), (0, 0)))
        dfl, dbf = fgate_bwd(drow8, dcol, sv['rest'], bf_pad[l:l + 1], fcol)
        gs['b_f'][l] = dbf[0, 0:nh]
        parts = (dq, dk, dv, dxr, dyr, dfl)
        dh, dg1 = in_proj_bwd(dh2, parts, sv['w_in_t'], sv['wrest_t'], sv['h'], row(small['attn_norm_g'][l]), tm)
        gs['attn_norm_g'][l] = dg1[0]
        tok = jnp.zeros((), F32)
        if l == 0:
            grads = {n: jnp.stack(v) for n, v in gs.items()}
            grads['final_g'] = dgf[0]
            grads['meta'] = dh[0:N_META]
            tok = hooks.small_ready(grads)
        dw_in = dw_in_t(sv['z'], parts, nh, tm2)
        dw_in = dw_in.reshape(N_DEV, dw_in.shape[0] // N_DEV, d) + tok.astype(BF16)
        tok = hooks.grads_ready(l, 'in', dict(w_in=dw_in))

    return loss[0, 0], dh, tok


def prep_weights(g_in, g_out, nh, rw):
    d = g_in.shape[2]
    w_in_t = g_in.reshape(-1, d)
    f0 = 3 * nh * HEAD_DIM
    wrest_t = jnp.concatenate([w_in_t[f0 + nh:f0 + nh + 2 * rw],
                               jnp.pad(w_in_t[f0:f0 + nh], ((0, LANES - nh), (0, 0)))], axis=0)
    return w_in_t, wrest_t, g_out.reshape(d, d)


BIG = ['w_in', 'w_out', 'w_up', 'w_down']
EXCHANGE_GROUPS = {'mlp': ['w_down', 'w_up', 'w_out'], 'in': ['w_in']}
WEIGHTS = ['meta', 'attn_norm_g', 'w_in', 'b_f', 'conv_w', 'conv_b', 'w_gate_a', 'b_gate_a', 'w_gate_x', 'b_gate_x',
           'lru_L', 'attn_out_g', 'rec_out_g', 'w_out', 'mlp_norm_g', 'w_up', 'w_down', 'final_g']


def _set_own(arr, own, me):
    return lax.dynamic_update_slice_in_dim(arr, own[None], me, axis=0)


class _Step:
    def __init__(self, w, nh, rw, me):
        self.w, self.nh, self.rw, self.me = w, nh, rw, me
        depth = w['w_in'].shape[0]
        first = [w['w_in_t'][:, 0, :].astype(BF16), w['w_out'][0].astype(BF16), w['meta'], w['conv_w']]
        self.pending, token = gather_start([first], "gather_start_0")
        zero = token[0, 0].astype(BF16)
        groups = [[w['w_up'][0].astype(BF16) + zero, w['w_down'][0].astype(BF16) + zero]]
        for l in range(1, depth):
            groups.append([w['w_in_t'][:, l, :].astype(BF16) + zero, w['w_out'][l].astype(BF16) + zero])
            groups.append([w['w_up'][l].astype(BF16) + zero, w['w_down'][l].astype(BF16) + zero])
        rest, _ = gather_start(groups, "gather_start_1")
        self.pending += rest
        self.first_after = rest[0][2][0]
        self.gathered = {}
        self.passing = {}
        self.token = jnp.zeros((), F32)
        self.lands = {n: lax.empty((N_DEV,) + w[n].shape, BF16) for n in BIG}
        din8, _, d = w['w_in_t'].shape
        self.lands['w_in'] = lax.empty((N_DEV, depth, din8, d), BF16)
        self.started = []
        self.small = None

    def _pass_on(self, gi, after):
        if gi < len(self.pending) and gi not in self.passing:
            send, recv, srcs, lands = self.pending[gi]
            srcs, lands = gather_wait(send, recv, srcs, lands, after, "gather_wait_%d" % gi)
            fsend, frecv, lands, token = forward_start(lands, "forward_start_%d" % gi)
            self.passing[gi] = (fsend, frecv, srcs, lands)
            self.token = token

    def group(self, gi, after):
        if gi not in self.gathered:
            self._pass_on(gi, after)
            fsend, frecv, srcs, lands = self.passing[gi]
            lands = forward_wait(fsend, frecv, lands, after, "forward_wait_%d" % gi)
            self.gathered[gi] = [_set_own(g, own, self.me) for g, own in zip(lands, srcs)]
            if gi >= 2:
                self._pass_on(gi + 1, lands[0])
        return self.gathered[gi]

    def mixer_weights(self, l, after):
        g = self.group(2 * l, after)
        return (*prep_weights(g[0], g[1], self.nh, self.rw), self.token)

    def mlp_weights(self, l, after):
        g = self.group(2 * l + 1, after)
        return g[0], g[1], self.token

    def grads_ready(self, l, group, blocks):
        names = EXCHANGE_GROUPS[group]
        send, recv, srcs, lands, token = exchange_start(
            [blocks[n] for n in names], [self.lands[n] for n in names], l, "exchange_start_%s_%d" % (group, l))
        for n, a in zip(names, lands):
            self.lands[n] = a
        self.started.append((l, group, send, recv, srcs))
        return token

    def small_ready(self, grads):
        self.small_shapes = {n: grads[n].shape for n in SMALL}
        packed = _pack(grads).astype(BF16)
        send, recv, srcs, lands, token = exchange_start(
            [packed], [lax.empty((N_DEV,) + packed.shape, BF16)], None, "small_start")
        self.small = (send, recv, srcs, lands)
        return token

    def small_sum(self, after):
        send, recv, srcs, lands = self.small
        srcs, lands = exchange_wait(send, recv, srcs, lands, after, None, "small_wait")
        parts = _set_own(lands[0], srcs[0], self.me)
        return _unpack(sum_parts(parts, "sum_small_grads"), self.small_shapes)

    def received(self, group, after):
        names = EXCHANGE_GROUPS[group]
        own = {n: [None] * self.w[n].shape[0] for n in names}
        for l, grp, send, recv, srcs in self.started:
            if grp != group:
                continue
            srcs, lands = exchange_wait(send, recv, srcs, [self.lands[n] for n in names], after, l,
                                        "exchange_wait_%s_%d" % (group, l))
            for n, a, sr in zip(names, lands, srcs):
                self.lands[n] = a
                own[n][l] = lax.dynamic_index_in_dim(sr, self.me, 0, keepdims=False)
        return {n: _set_own(self.lands[n], jnp.stack(own[n]), self.me) for n in names}


def kernel(x, meta, attn_norm_g, w_in, b_f, conv_w, conv_b, w_gate_a, b_gate_a, w_gate_x, b_gate_x, lru_L, attn_out_g, rec_out_g, w_out, mlp_norm_g, w_up, w_down, final_g, loss_target, m_meta, m_attn_norm_g, m_w_in, m_b_f, m_conv_w, m_conv_b, m_w_gate_a, m_b_gate_a, m_w_gate_x, m_b_gate_x, m_lru_L, m_attn_out_g, m_rec_out_g, m_w_out, m_mlp_norm_g, m_w_up, m_w_down, m_final_g, v_meta, v_attn_norm_g, v_w_in, v_b_f, v_conv_w, v_conv_b, v_w_gate_a, v_b_gate_a, v_w_gate_x, v_b_gate_x, v_lru_L, v_attn_out_g, v_rec_out_g, v_w_out, v_mlp_norm_g, v_w_up, v_w_down, v_final_g):
    w = dict(meta=meta, attn_norm_g=attn_norm_g, w_in=w_in, b_f=b_f, conv_w=conv_w, conv_b=conv_b,
             w_gate_a=w_gate_a, b_gate_a=b_gate_a, w_gate_x=w_gate_x, b_gate_x=b_gate_x, lru_L=lru_L,
             attn_out_g=attn_out_g, rec_out_g=rec_out_g, w_out=w_out, mlp_norm_g=mlp_norm_g, w_up=w_up,
             w_down=w_down, final_g=final_g)
    mo = dict(meta=m_meta, attn_norm_g=m_attn_norm_g, w_in=m_w_in, b_f=m_b_f, conv_w=m_conv_w, conv_b=m_conv_b,
              w_gate_a=m_w_gate_a, b_gate_a=m_b_gate_a, w_gate_x=m_w_gate_x, b_gate_x=m_b_gate_x, lru_L=m_lru_L,
              attn_out_g=m_attn_out_g, rec_out_g=m_rec_out_g, w_out=m_w_out, mlp_norm_g=m_mlp_norm_g,
              w_up=m_w_up, w_down=m_w_down, final_g=m_final_g)
    vo = dict(meta=v_meta, attn_norm_g=v_attn_norm_g, w_in=v_w_in, b_f=v_b_f, conv_w=v_conv_w, conv_b=v_conv_b,
              w_gate_a=v_w_gate_a, b_gate_a=v_b_gate_a, w_gate_x=v_w_gate_x, b_gate_x=v_b_gate_x, lru_L=v_lru_L,
              attn_out_g=v_attn_out_g, rec_out_g=v_rec_out_g, w_out=v_w_out, mlp_norm_g=v_mlp_norm_g,
              w_up=v_w_up, w_down=v_w_down, final_g=v_final_g)
    depth = w_in.shape[0]
    nh = b_f.shape[1]
    rw = conv_b.shape[1]
    me = 4 * lax.axis_index("x") + 2 * lax.axis_index("y") + lax.axis_index("c")

    w['w_in_t'] = jnp.transpose(w_in, (2, 0, 1))
    swap = lambda a: jnp.swapaxes(a, 1, 2)
    step = _Step(w, nh, rw, me)
    g0 = step.group(0, step.first_after)
    meta_full = g0[2].transpose(1, 0, 2).reshape(N_META, -1)
    conv_full = g0[3].transpose(1, 2, 0, 3).reshape(depth, CONV_WIDTH, rw)
    small = {n: w[n] for n in SMALL}
    small['conv_w'] = conv_full

    loss_part, dh0, tok = local_step(x[0], loss_target[0], meta_full, small, step)
    loss = lax.psum(loss_part, ("x", "y", "c"))
    grad_x = dh0[N_META:N_META + x.shape[1]][None]

    out_g, out_d, out_m, out_v = {}, {}, {}, {}

    def update_big(group, after):
        for n, r in step.received(group, after).items():
            if n == 'w_in':
                out = sum_adamw_t(r, swap(w[n]), swap(mo[n]), swap(vo[n]), "adamw_w_in")
                out = [swap(a) for a in out]
            else:
                shp = w[n].shape
                rows, cols = shp[0] * shp[1], shp[2]
                tr = min(512 if cols <= 512 else 256, rows)
                out = sum_adamw(r.reshape(N_DEV, rows, cols), w[n].reshape(rows, cols), mo[n].reshape(rows, cols),
                                vo[n].reshape(rows, cols), tr, "adamw_" + n)
                out = [a.reshape(shp) for a in out]
            out_g[n], out_d[n], out_m[n], out_v[n] = out
            after = out[0]
        return after

    update_big('mlp', step.started[-1][4][0])

    gsum = step.small_sum([out_g[n] for n in EXCHANGE_GROUPS['mlp']])
    gsum['meta'] = lax.dynamic_slice_in_dim(gsum['meta'], me * meta.shape[1], meta.shape[1], axis=1)
    gsum['conv_w'] = lax.dynamic_slice_in_dim(gsum['conv_w'], me * conv_w.shape[2], conv_w.shape[2], axis=2)
    as2d = lambda a: a.reshape(-1, a.shape[-1])
    deltas, new_m, new_v = adamw_group([as2d(gsum[n]) for n in SMALL], [as2d(w[n]) for n in SMALL],
                                       [as2d(mo[n]) for n in SMALL], [as2d(vo[n]) for n in SMALL], "adamw_small")
    for i, n in enumerate(SMALL):
        out_g[n] = gsum[n]
        out_d[n], out_m[n], out_v[n] = [a[i].reshape(w[n].shape) for a in (deltas, new_m, new_v)]

    update_big('in', deltas[0])

    return (loss, grad_x, *[out_g[n] for n in WEIGHTS], *[out_d[n] for n in WEIGHTS],
            *[out_m[n] for n in WEIGHTS], *[out_v[n] for n in WEIGHTS])
```

```python
import functools
import math

import jax
import jax.numpy as jnp
from jax import lax
from jax.experimental import pallas as pl
from jax.experimental.pallas import tpu as pltpu

F32 = jnp.float32
BF16 = jnp.bfloat16

N_DEV = 8
N_META = 16
HEAD_DIM = 64
CONV_WIDTH = 4
RG_C = 8.0
NORM_EPS = 1e-6
LANES = 128
SUBLANES = 8
ATT_BLOCK = 128
ATT_TQ = 512
NEG_BIG = -1e30
ATT_SCALE = 1.0 / math.sqrt(HEAD_DIM)

ADAM_LR = 0.001
ADAM_B1 = 0.9
ADAM_B2 = 0.999
ADAM_EPS = 1e-08
ADAM_WD = 0.01
ADAM_STEP = 10

VMEM_LIMIT_BYTES = 56 * 1024 * 1024
MESH = pl.DeviceIdType.MESH
ANY = pl.BlockSpec(memory_space=pl.ANY)


def _cparams(*sem):
    return pltpu.CompilerParams(dimension_semantics=sem if sem else None,
                                vmem_limit_bytes=VMEM_LIMIT_BYTES)


def _dot(a, b):
    return jnp.dot(a, b, preferred_element_type=F32)


def _dot_nt(a, b):
    return lax.dot_general(a, b, (((1,), (1,)), ((), ())), preferred_element_type=F32)


def _dot_tn(a, b):
    return lax.dot_general(a, b, (((0,), (0,)), ((), ())), preferred_element_type=F32)


def _sigmoid(x):
    return 0.5 * (1.0 + jnp.tanh(0.5 * x))


def _log_sigmoid(x):
    return jnp.minimum(x, 0.0) - jnp.log(1.0 + jnp.exp(-jnp.abs(x)))


def _expm1(x):
    series = x * (1.0 + x * (0.5 + x * (1.0 / 6.0 + x * (1.0 / 24.0))))
    return jnp.where(jnp.abs(x) < 1e-2, series, jnp.exp(x) - 1.0)


_GELU_K = math.sqrt(2.0 / math.pi)
_GELU_C = 0.044715


def _gelu(x):
    t = jnp.tanh(_GELU_K * (x + _GELU_C * x * x * x))
    return 0.5 * x * (1.0 + t)


def _gelu_grad(x):
    t = jnp.tanh(_GELU_K * (x + _GELU_C * x * x * x))
    return 0.5 * (1.0 + t) + 0.5 * x * (1.0 - t * t) * _GELU_K * (1.0 + 3.0 * _GELU_C * x * x)


def _split3_dot(tri, x):
    hi = x.astype(BF16)
    r1 = x - hi.astype(F32)
    mid = r1.astype(BF16)
    lo = (r1 - mid.astype(F32)).astype(BF16)
    return _dot(tri, hi) + _dot(tri, mid) + _dot(tri, lo)


def _dot_split3(x, sel):
    hi = x.astype(BF16)
    r1 = x - hi.astype(F32)
    mid = r1.astype(BF16)
    lo = (r1 - mid.astype(F32)).astype(BF16)
    return _dot(hi, sel) + _dot(mid, sel) + _dot(lo, sel)


def _rms_fwd(x, g):
    r = lax.rsqrt(jnp.mean(x * x, axis=-1, keepdims=True) + NORM_EPS)
    return x * r * g


def _rms_bwd(x, g, dy):
    r = lax.rsqrt(jnp.mean(x * x, axis=-1, keepdims=True) + NORM_EPS)
    xn = x * r
    dxn = dy * g
    dx = r * (dxn - xn * jnp.mean(dxn * xn, axis=-1, keepdims=True))
    return dx, jnp.sum(dy * xn, axis=0, keepdims=True)


def _accumulate(ref, val, first):
    @pl.when(first)
    def _():
        ref[...] = val

    @pl.when(jnp.logical_not(first))
    def _():
        ref[...] += val


def in_proj(h, g1, w_in_t, wrest_t, nq, tm):
    tp, d = h.shape
    nr = wrest_t.shape[0]

    def body(h_ref, g_ref, wq_ref, wr_ref, z_ref, qkv_ref, rest_ref):
        z = _rms_fwd(h_ref[...], g_ref[...]).astype(BF16)
        z_ref[...] = z
        qkv_ref[...] = _dot_nt(z, wq_ref[...]).astype(BF16)
        rest_ref[...] = _dot_nt(z, wr_ref[...])

    return pl.pallas_call(
        body, name="in_proj", grid=(tp // tm,),
        in_specs=[pl.BlockSpec((tm, d), lambda i: (i, 0)),
                  pl.BlockSpec((1, d), lambda i: (0, 0)),
                  pl.BlockSpec((nq, d), lambda i: (0, 0)),
                  pl.BlockSpec((nr, d), lambda i: (0, 0))],
        out_specs=[pl.BlockSpec((tm, d), lambda i: (i, 0)),
                   pl.BlockSpec((tm, nq), lambda i: (i, 0)),
                   pl.BlockSpec((tm, nr), lambda i: (i, 0))],
        out_shape=[jax.ShapeDtypeStruct((tp, d), BF16),
                   jax.ShapeDtypeStruct((tp, nq), BF16),
                   jax.ShapeDtypeStruct((tp, nr), F32)],
        compiler_params=_cparams("parallel"),
    )(h, g1, w_in_t, wrest_t)


def fgate_fwd(rest, bf_pad, fcol):
    tp = rest.shape[0]
    nb = tp // ATT_BLOCK

    def body(f_ref, b_ref, c_ref, ct_ref):
        r_i = lax.broadcasted_iota(jnp.int32, (ATT_BLOCK, ATT_BLOCK), 0)
        c_i = lax.broadcasted_iota(jnp.int32, (ATT_BLOCK, ATT_BLOCK), 1)
        tri = (r_i >= c_i).astype(BF16)
        carry = jnp.zeros((1, LANES), F32)
        for i in range(nb):
            sl = slice(i * ATT_BLOCK, (i + 1) * ATT_BLOCK)
            lf = _log_sigmoid(f_ref[sl, :] + b_ref[...])
            cs = _split3_dot(tri, lf) + carry
            carry = cs[ATT_BLOCK - 1:ATT_BLOCK, :]
            c_ref[sl, :] = cs
            ct_ref[:, sl] = cs.T[0:SUBLANES, :]

    return pl.pallas_call(
        body, name="fgate_fwd", grid=(1,),
        in_specs=[pl.BlockSpec((tp, LANES), lambda i: (0, fcol)),
                  pl.BlockSpec((1, LANES), lambda i: (0, 0))],
        out_specs=[pl.BlockSpec((tp, LANES), lambda i: (0, 0)),
                   pl.BlockSpec((SUBLANES, tp), lambda i: (0, 0))],
        out_shape=[jax.ShapeDtypeStruct((tp, LANES), F32),
                   jax.ShapeDtypeStruct((SUBLANES, tp), F32)],
        compiler_params=_cparams("arbitrary"),
    )(rest, bf_pad)


def _pick_col(blk, head):
    lane = lax.broadcasted_iota(jnp.int32, blk.shape, 1)
    return jnp.sum(jnp.where(lane == head, blk, 0.0), axis=1, keepdims=True)


def _pick_row(blk, head):
    sub = lax.broadcasted_iota(jnp.int32, blk.shape, 0)
    return jnp.sum(jnp.where(sub == head, blk, 0.0), axis=0, keepdims=True)


def _att_tiles(tp):
    out, r0 = [], 0
    while r0 < tp:
        rows = min(ATT_TQ, tp - r0)
        out.append((r0, rows, r0 + rows))
        r0 += rows
    return out


def attn_fwd(qkv, c, ct, nh):
    tp = qkv.shape[0]
    npair = nh // 2
    tiles = _att_tiles(tp)

    def body(q_ref, k_ref, v_ref, c_ref, ct_ref, o_ref, lset_ref):
        p = pl.program_id(0)
        lset_ref[...] = jnp.zeros_like(lset_ref)
        for r0, nr, nk in tiles:
            rs = slice(r0, r0 + nr)
            causal = (r0 + lax.broadcasted_iota(jnp.int32, (nr, nk), 0)
                      >= lax.broadcasted_iota(jnp.int32, (nr, nk), 1))
            cblk = c_ref[rs, :]
            ctb = ct_ref[:, 0:nk]
            for hh in range(2):
                head = 2 * p + hh
                hs = slice(hh * HEAD_DIM, (hh + 1) * HEAD_DIM)
                q = q_ref[rs, hs] * ATT_SCALE
                s = _dot_nt(q, k_ref[0:nk, hs]) + (_pick_col(cblk, head) - _pick_row(ctb, head))
                s = jnp.where(causal, s, NEG_BIG)
                m = jnp.max(s, axis=1, keepdims=True)
                pm = jnp.exp(s - m)
                l = jnp.sum(pm, axis=1, keepdims=True)
                o_ref[rs, hs] = _dot(pm.astype(BF16), v_ref[0:nk, hs]) / l
                lse = m + jnp.log(l)
                lset_ref[hh:hh + 1, rs] = jnp.broadcast_to(lse, (nr, LANES)).T[0:1, :]

    pair = lambda p: (0, p)
    return pl.pallas_call(
        body, name="attn_fwd", grid=(npair,),
        in_specs=[pl.BlockSpec((tp, LANES), pair),
                  pl.BlockSpec((tp, LANES), lambda p: (0, npair + p)),
                  pl.BlockSpec((tp, LANES), lambda p: (0, 2 * npair + p)),
                  pl.BlockSpec((tp, LANES), lambda p: (0, 0)),
                  pl.BlockSpec((SUBLANES, tp), lambda p: (0, 0))],
        out_specs=[pl.BlockSpec((tp, LANES), pair),
                   pl.BlockSpec((None, SUBLANES, tp), lambda p: (p, 0, 0))],
        out_shape=[jax.ShapeDtypeStruct((tp, nh * HEAD_DIM), F32),
                   jax.ShapeDtypeStruct((npair, SUBLANES, tp), F32)],
        compiler_params=_cparams("parallel"),
    )(qkv, qkv, qkv, c, ct)


def _shift_down(x, k, n):
    if k == 0:
        return x
    rows = lax.broadcasted_iota(jnp.int32, x.shape, 0)
    return jnp.where(rows >= k, pltpu.roll(x, k, 0), 0.0)


def _shift_up(x, k, n):
    if k == 0:
        return x
    rows = lax.broadcasted_iota(jnp.int32, x.shape, 0)
    return jnp.where(rows < n - k, pltpu.roll(x, n - k, 0), 0.0)


def _conv_fwd(xr, cw_ref, cb_ref, n):
    xc = cw_ref[CONV_WIDTH - 1:CONV_WIDTH, :] * xr + cb_ref[...]
    for k in range(1, CONV_WIDTH):
        xc = xc + cw_ref[CONV_WIDTH - 1 - k:CONV_WIDTH - k, :] * _shift_down(xr, k, n)
    return xc


def _gates(xc, wga_ref, bga_ref, wgx_ref, bgx_ref, l_ref):
    xcb = xc.astype(BF16)
    r = _sigmoid(_dot(xcb, wga_ref[...]) + bga_ref[...])
    ig = _sigmoid(_dot(xcb, wgx_ref[...]) + bgx_ref[...])
    ls = _log_sigmoid(l_ref[...])
    log_a = RG_C * r * ls
    a = jnp.exp(log_a)
    mult = jnp.sqrt(-_expm1(2.0 * log_a))
    return xcb, r, ig, ls, log_a, a, mult


SCAN_UNROLL = 4


def _scan_rows(a_s, u_s, out_ref, n, reverse):
    nt = n // SUBLANES
    per = SCAN_UNROLL if nt % SCAN_UNROLL == 0 else 1
    row = lax.broadcasted_iota(jnp.int32, (SUBLANES, LANES), 0)
    last = 0 if reverse else SUBLANES - 1

    def tile_scan(a, u):
        for d in (1, 2, 4):
            if reverse:
                keep = row < SUBLANES - d
                sh = SUBLANES - d
            else:
                keep = row >= d
                sh = d
            a_sh = jnp.where(keep, pltpu.roll(a, sh, 0), 1.0)
            u_sh = jnp.where(keep, pltpu.roll(u, sh, 0), 0.0)
            u = a * u_sh + u
            a = a * a_sh
        return a, u

    def step(t, carry):
        tiles = []
        for k in range(per):
            tt = t * per + k
            if reverse:
                tt = nt - 1 - tt
            off = pl.multiple_of(tt * SUBLANES, SUBLANES)
            a, u = tile_scan(a_s[pl.ds(off, SUBLANES), :], u_s[pl.ds(off, SUBLANES), :])
            tiles.append((off, a, u))
        for off, a, u in tiles:
            out_ref[pl.ds(off, SUBLANES), :] = u + a * carry
            carry = u[last:last + 1, :] + a[last:last + 1, :] * carry
        return carry

    lax.fori_loop(0, nt // per, step, jnp.zeros((1, LANES), F32))


def rec_fwd(rest, convw, convb, wga, bga, wgx, bgx, lru, rw):
    tp = rest.shape[0]
    ng = rw // LANES

    def body(xr_ref, yr_ref, cw_ref, cb_ref, wga_ref, bga_ref, wgx_ref, bgx_ref, l_ref,
             rec_ref, hr_ref, xc_ref, a_s, u_s):
        xc = _conv_fwd(xr_ref[...], cw_ref, cb_ref, tp)
        xc_ref[...] = xc
        _, r, ig, ls, log_a, a, mult = _gates(xc, wga_ref, bga_ref, wgx_ref, bgx_ref, l_ref)
        a_s[...] = a
        u_s[...] = mult * ig * xc
        _scan_rows(a_s, u_s, hr_ref, tp, reverse=False)
        rec_ref[...] = hr_ref[...] * _gelu(yr_ref[...])

    col = lambda g: (0, g)
    vec = pl.BlockSpec((1, LANES), col)
    big = pl.BlockSpec((tp, LANES), col)
    return pl.pallas_call(
        body, name="rec_fwd", grid=(ng,),
        in_specs=[big, pl.BlockSpec((tp, LANES), lambda g: (0, ng + g)),
                  pl.BlockSpec((CONV_WIDTH, LANES), col), vec,
                  pl.BlockSpec((None, LANES, LANES), lambda g: (g, 0, 0)), vec,
                  pl.BlockSpec((None, LANES, LANES), lambda g: (g, 0, 0)), vec, vec],
        out_specs=[big, big, big],
        out_shape=[jax.ShapeDtypeStruct((tp, rw), F32)] * 3,
        scratch_shapes=[pltpu.VMEM((tp, LANES), F32), pltpu.VMEM((tp, LANES), F32)],
        compiler_params=_cparams("parallel"),
    )(rest, rest, convw, convb, wga, bga, wgx, bgx, lru)


def out_proj(h, o, rec, ga, gr, wout, g2, tm):
    tp, d = h.shape
    aw, rw = o.shape[1], rec.shape[1]

    def body(h_ref, o_ref, rec_ref, ga_ref, gr_ref, w_ref, g2_ref, h2_ref, mix_ref, z2_ref):
        mix_ref[:, 0:aw] = _rms_fwd(o_ref[...], ga_ref[...]).astype(BF16)
        mix_ref[:, aw:aw + rw] = _rms_fwd(rec_ref[...], gr_ref[...]).astype(BF16)
        h2 = h_ref[...] + _dot(mix_ref[...], w_ref[...])
        h2_ref[...] = h2
        z2_ref[...] = _rms_fwd(h2, g2_ref[...]).astype(BF16)

    row = lambda i: (i, 0)
    fix = lambda i: (0, 0)
    return pl.pallas_call(
        body, name="out_proj", grid=(tp // tm,),
        in_specs=[pl.BlockSpec((tm, d), row), pl.BlockSpec((tm, aw), row), pl.BlockSpec((tm, rw), row),
                  pl.BlockSpec((1, aw), fix), pl.BlockSpec((1, rw), fix),
                  pl.BlockSpec((d, d), fix), pl.BlockSpec((1, d), fix)],
        out_specs=[pl.BlockSpec((tm, d), row)] * 3,
        out_shape=[jax.ShapeDtypeStruct((tp, d), F32), jax.ShapeDtypeStruct((tp, d), BF16),
                   jax.ShapeDtypeStruct((tp, d), BF16)],
        compiler_params=_cparams("parallel"),
    )(h, o, rec, ga, gr, wout, g2)


MLP_BLOCKS = 2


def mlp_fwd(z2, h2, gup, gdown, tm):
    tp, d = h2.shape
    nf = gup.shape[0]
    tf = gup.shape[2]
    nb = MLP_BLOCKS if nf % MLP_BLOCKS == 0 else 1
    nj = nf // nb

    def body(z_ref, h_ref, wu_ref, wd_ref, u_ref, h3_ref, acc):
        j = pl.program_id(1)
        z = z_ref[...]
        part = None
        for b in range(nb):
            u = jnp.maximum(_dot(z, wu_ref[b]), 0.0)
            u_ref[:, b * tf:(b + 1) * tf] = u.astype(BF16)
            p = _dot((u * u).astype(BF16), wd_ref[b])
            part = p if part is None else part + p

        @pl.when(j == 0)
        def _():
            acc[...] = h_ref[...] + part

        @pl.when(j > 0)
        def _():
            acc[...] += part

        @pl.when(j == nj - 1)
        def _():
            h3_ref[...] = acc[...]

    return pl.pallas_call(
        body, name="mlp_fwd", grid=(tp // tm, nj),
        in_specs=[pl.BlockSpec((tm, d), lambda i, j: (i, 0)),
                  pl.BlockSpec((tm, d), lambda i, j: (i, 0)),
                  pl.BlockSpec((nb, d, tf), lambda i, j: (j, 0, 0)),
                  pl.BlockSpec((nb, tf, d), lambda i, j: (j, 0, 0))],
        out_specs=[pl.BlockSpec((tm, nb * tf), lambda i, j: (i, j)),
                   pl.BlockSpec((tm, d), lambda i, j: (i, 0))],
        out_shape=[jax.ShapeDtypeStruct((tp, nf * tf), BF16), jax.ShapeDtypeStruct((tp, d), F32)],
        scratch_shapes=[pltpu.VMEM((tm, d), F32)],
        compiler_params=_cparams("parallel", "arbitrary"),
    )(z2, h2, gup, gdown)


def loss_head(h, gf, tgt, t_real, tm):
    tp, d = h.shape

    def body(h_ref, g_ref, t_ref, dh_ref, dg_ref, loss_ref):
        i = pl.program_id(0)
        x = h_ref[...]
        g = g_ref[...]
        r = lax.rsqrt(jnp.mean(x * x, axis=-1, keepdims=True) + NORM_EPS)
        xn = x * r
        rows = i * tm + lax.broadcasted_iota(jnp.int32, (tm, 1), 0)
        valid = jnp.logical_and(rows >= N_META, rows < t_real)
        e = jnp.where(valid, xn * g - t_ref[...], 0.0)
        part = 0.5 * jnp.sum(jnp.sum(e * e, axis=1, keepdims=True) / d, axis=0, keepdims=True)
        dy = e / d
        dxn = dy * g
        dh_ref[...] = r * (dxn - xn * jnp.mean(dxn * xn, axis=-1, keepdims=True))
        _accumulate(dg_ref, jnp.sum(dy * xn, axis=0, keepdims=True), i == 0)
        _accumulate(loss_ref, jnp.broadcast_to(part, (1, LANES)), i == 0)

    row = lambda i: (i, 0)
    fix = lambda i: (0, 0)
    return pl.pallas_call(
        body, name="loss_head", grid=(tp // tm,),
        in_specs=[pl.BlockSpec((tm, d), row), pl.BlockSpec((1, d), fix), pl.BlockSpec((tm, d), row)],
        out_specs=[pl.BlockSpec((tm, d), row), pl.BlockSpec((1, d), fix), pl.BlockSpec((1, LANES), fix)],
        out_shape=[jax.ShapeDtypeStruct((tp, d), F32), jax.ShapeDtypeStruct((1, d), F32),
                   jax.ShapeDtypeStruct((1, LANES), F32)],
        compiler_params=_cparams("arbitrary"),
    )(h, gf, tgt)


def mlp_bwd(dh, u, h2, g2, gup, gdown, tm):
    tp, d = dh.shape
    nf = gup.shape[0]
    tf = gup.shape[2]
    nb = MLP_BLOCKS if nf % MLP_BLOCKS == 0 else 1
    nj = nf // nb
    ni = tp // tm

    def body(dh_ref, u_ref, h2_ref, g_ref, wu_ref, wd_ref, dup_ref, dh2_ref, dg_ref, dhb, acc):
        i = pl.program_id(0)
        j = pl.program_id(1)

        @pl.when(j == 0)
        def _():
            dhb[...] = dh_ref[...].astype(BF16)

        part = None
        for b in range(nb):
            cols = slice(b * tf, (b + 1) * tf)
            dup = (_dot_nt(dhb[...], wd_ref[b]) * (2.0 * u_ref[:, cols].astype(F32))).astype(BF16)
            dup_ref[:, cols] = dup
            p = _dot_nt(dup, wu_ref[b])
            part = p if part is None else part + p
        _accumulate(acc, part, j == 0)

        @pl.when(j == nj - 1)
        def _():
            dx, dg = _rms_bwd(h2_ref[...], g_ref[...], acc[...])
            dh2_ref[...] = dh_ref[...] + dx
            _accumulate(dg_ref, dg, i == 0)

    return pl.pallas_call(
        body, name="mlp_bwd", grid=(ni, nj),
        in_specs=[pl.BlockSpec((tm, d), lambda i, j: (i, 0)),
                  pl.BlockSpec((tm, nb * tf), lambda i, j: (i, j)),
                  pl.BlockSpec((tm, d), lambda i, j: (i, 0)),
                  pl.BlockSpec((1, d), lambda i, j: (0, 0)),
                  pl.BlockSpec((nb, d, tf), lambda i, j: (j, 0, 0)),
                  pl.BlockSpec((nb, tf, d), lambda i, j: (j, 0, 0))],
        out_specs=[pl.BlockSpec((tm, nb * tf), lambda i, j: (i, j)),
                   pl.BlockSpec((tm, d), lambda i, j: (i, 0)),
                   pl.BlockSpec((1, d), lambda i, j: (0, 0)),
                   pl.BlockSpec((tm, d), lambda i, j: (i, 0))],
        out_shape=[jax.ShapeDtypeStruct((tp, nf * tf), BF16), jax.ShapeDtypeStruct((tp, d), F32),
                   jax.ShapeDtypeStruct((1, d), F32), jax.ShapeDtypeStruct((tp, d), BF16)],
        scratch_shapes=[pltpu.VMEM((tm, d), F32)],
        compiler_params=_cparams("arbitrary", "arbitrary"),
    )(dh, u, h2, g2, gup, gdown)


def mm_tn(a, b, *, tk, tn, out_dtype, name, square_a=False, blocked_n=False):
    rows, kk = a.shape
    nn = b.shape[1]
    keep_at = nn // tn > 1

    def a_tile(a_ref):
        av = a_ref[...]
        if square_a:
            af = av.astype(F32)
            av = af * af
        return av.astype(BF16)

    def body(a_ref, b_ref, o_ref, *scratch):
        if keep_at:
            at, = scratch

            @pl.when(pl.program_id(1) == 0)
            def _():
                at[...] = a_tile(a_ref).T

            o_ref[...] = _dot(at[...], b_ref[...].astype(BF16)).astype(out_dtype)
        else:
            o_ref[...] = _dot_tn(a_tile(a_ref), b_ref[...].astype(BF16)).astype(out_dtype)

    if blocked_n:
        out_spec = pl.BlockSpec((None, tk, tn), lambda k, n: (n, k, 0))
        out_shape = jax.ShapeDtypeStruct((nn // tn, kk, tn), out_dtype)
    else:
        out_spec = pl.BlockSpec((tk, tn), lambda k, n: (k, n))
        out_shape = jax.ShapeDtypeStruct((kk, nn), out_dtype)
    return pl.pallas_call(
        body, name=name, grid=(kk // tk, nn // tn),
        in_specs=[pl.BlockSpec((rows, tk), lambda k, n: (0, k)),
                  pl.BlockSpec((rows, tn), lambda k, n: (0, n))],
        out_specs=out_spec, out_shape=out_shape,
        scratch_shapes=[pltpu.VMEM((tk, rows), BF16)] if keep_at else [],
        compiler_params=_cparams("parallel", "arbitrary"),
    )(a, b)


def dw_mlp(u, dhb, z2, dup, tf):
    rows, dff = u.shape
    d = z2.shape[1]
    nf = dff // tf

    def body(u_ref, dh_ref, z_ref, dup_ref, dwd_ref, dwu_ref, zt):
        @pl.when(pl.program_id(0) == 0)
        def _():
            zt[...] = z_ref[...].T

        uf = u_ref[...].astype(F32)
        dwd_ref[...] = _dot_tn((uf * uf).astype(BF16), dh_ref[...]).astype(BF16)
        dwu_ref[...] = _dot(zt[...], dup_ref[...]).astype(BF16)

    col = lambda j: (0, j)
    fix = lambda j: (0, 0)
    return pl.pallas_call(
        body, name="dw_mlp", grid=(nf,),
        in_specs=[pl.BlockSpec((rows, tf), col), pl.BlockSpec((rows, d), fix),
                  pl.BlockSpec((rows, d), fix), pl.BlockSpec((rows, tf), col)],
        out_specs=[pl.BlockSpec((None, tf, d), lambda j: (j, 0, 0)),
                   pl.BlockSpec((None, d, tf), lambda j: (j, 0, 0))],
        out_shape=[jax.ShapeDtypeStruct((nf, tf, d), BF16), jax.ShapeDtypeStruct((nf, d, tf), BF16)],
        scratch_shapes=[pltpu.VMEM((d, rows), BF16)],
        compiler_params=_cparams("arbitrary"),
    )(u, dhb, z2, dup)


def out_proj_bwd(dh2, o, rec, ga, gr, wout, tm):
    tp, d = dh2.shape
    aw, rw = o.shape[1], rec.shape[1]

    def body(dh_ref, o_ref, rec_ref, ga_ref, gr_ref, w_ref, do_ref, drec_ref, dga_ref, dgr_ref):
        i = pl.program_id(0)
        dmix = _dot_nt(dh_ref[...].astype(BF16), w_ref[...])
        do, dga = _rms_bwd(o_ref[...], ga_ref[...], dmix[:, 0:aw])
        drec, dgr = _rms_bwd(rec_ref[...], gr_ref[...], dmix[:, aw:aw + rw])
        do_ref[...] = do
        drec_ref[...] = drec
        _accumulate(dga_ref, dga, i == 0)
        _accumulate(dgr_ref, dgr, i == 0)

    row = lambda i: (i, 0)
    fix = lambda i: (0, 0)
    return pl.pallas_call(
        body, name="out_proj_bwd", grid=(tp // tm,),
        in_specs=[pl.BlockSpec((tm, d), row), pl.BlockSpec((tm, aw), row), pl.BlockSpec((tm, rw), row),
                  pl.BlockSpec((1, aw), fix), pl.BlockSpec((1, rw), fix), pl.BlockSpec((d, d), fix)],
        out_specs=[pl.BlockSpec((tm, aw), row), pl.BlockSpec((tm, rw), row),
                   pl.BlockSpec((1, aw), fix), pl.BlockSpec((1, rw), fix)],
        out_shape=[jax.ShapeDtypeStruct((tp, aw), F32), jax.ShapeDtypeStruct((tp, rw), F32),
                   jax.ShapeDtypeStruct((1, aw), F32), jax.ShapeDtypeStruct((1, rw), F32)],
        compiler_params=_cparams("arbitrary"),
    )(dh2, o, rec, ga, gr, wout)


def rec_bwd(drec, hr, xc, rest, convw, convb, wga, bga, wgx, bgx, lru, rw):
    tp = rest.shape[0]
    ng = rw // LANES

    def body(drec_ref, hr_ref, xc_ref, xr_ref, yr_ref, cw_ref, cb_ref, wga_ref, bga_ref, wgx_ref, bgx_ref, l_ref,
             dxr_ref, dyr_ref, dwga_ref, dwgx_ref, vec_ref, a_s, u_s, lam_s):
        xc = xc_ref[...]
        h = hr_ref[...]
        yr = yr_ref[...]
        drec = drec_ref[...]
        xcb, r, ig, ls, log_a, a, mult = _gates(xc, wga_ref, bga_ref, wgx_ref, bgx_ref, l_ref)
        dyr_ref[...] = (drec * h * _gelu_grad(yr)).astype(BF16)
        a_s[...] = _shift_up(a, 1, tp)
        u_s[...] = drec * _gelu(yr)
        _scan_rows(a_s, u_s, lam_s, tp, reverse=True)
        lam = lam_s[...]
        da = lam * _shift_down(h, 1, tp)
        dmult = lam * ig * xc
        dig = lam * mult * xc
        dxc = lam * mult * ig
        a2 = jnp.exp(2.0 * log_a)
        dlog_a = da * a - dmult * a2 / mult
        dr = dlog_a * (RG_C * ls)
        dl = jnp.sum(dlog_a * (RG_C * r), axis=0, keepdims=True) * _sigmoid(-l_ref[...])
        dpa = dr * r * (1.0 - r)
        dpx = dig * ig * (1.0 - ig)
        dpab = dpa.astype(BF16)
        dpxb = dpx.astype(BF16)
        dxc = dxc + _dot_nt(dpab, wga_ref[...]) + _dot_nt(dpxb, wgx_ref[...])
        dwga_ref[...] = _dot_tn(xcb, dpab)
        dwgx_ref[...] = _dot_tn(xcb, dpxb)
        xr = xr_ref[...]
        dxr = cw_ref[CONV_WIDTH - 1:CONV_WIDTH, :] * dxc
        for k in range(1, CONV_WIDTH):
            dxr = dxr + cw_ref[CONV_WIDTH - 1 - k:CONV_WIDTH - k, :] * _shift_up(dxc, k, tp)
        dxr_ref[...] = dxr.astype(BF16)
        for k in range(CONV_WIDTH):
            vec_ref[k:k + 1, :] = jnp.sum(dxc * _shift_down(xr, CONV_WIDTH - 1 - k, tp), axis=0, keepdims=True)
        vec_ref[4:5, :] = jnp.sum(dxc, axis=0, keepdims=True)
        vec_ref[5:6, :] = jnp.sum(dpa, axis=0, keepdims=True)
        vec_ref[6:7, :] = jnp.sum(dpx, axis=0, keepdims=True)
        vec_ref[7:8, :] = dl

    col = lambda g: (0, g)
    vec = pl.BlockSpec((1, LANES), col)
    big = pl.BlockSpec((tp, LANES), col)
    sq = pl.BlockSpec((None, LANES, LANES), lambda g: (g, 0, 0))
    return pl.pallas_call(
        body, name="rec_bwd", grid=(ng,),
        in_specs=[big, big, big, big, pl.BlockSpec((tp, LANES), lambda g: (0, ng + g)),
                  pl.BlockSpec((CONV_WIDTH, LANES), col), vec, sq, vec, sq, vec, vec],
        out_specs=[big, big, sq, sq, pl.BlockSpec((None, SUBLANES, LANES), lambda g: (g, 0, 0))],
        out_shape=[jax.ShapeDtypeStruct((tp, rw), BF16), jax.ShapeDtypeStruct((tp, rw), BF16),
                   jax.ShapeDtypeStruct((ng, LANES, LANES), F32), jax.ShapeDtypeStruct((ng, LANES, LANES), F32),
                   jax.ShapeDtypeStruct((ng, SUBLANES, LANES), F32)],
        scratch_shapes=[pltpu.VMEM((tp, LANES), F32)] * 3,
        compiler_params=_cparams("parallel"),
    )(drec, hr, xc, rest, rest, convw, convb, wga, bga, wgx, bgx, lru)


def attn_bwd(qkv, do, o, lset, c, ct, nh):
    tp = qkv.shape[0]
    npair = nh // 2
    aw = nh * HEAD_DIM
    tiles = _att_tiles(tp)

    def body(q_ref, k_ref, v_ref, do_ref, o_ref, lset_ref, c_ref, ct_ref,
             dq_ref, dk_ref, dv_ref, drow_ref, dcol_ref, dk_acc, dv_acc):
        p = pl.program_id(0)
        dk_acc[...] = jnp.zeros_like(dk_acc)
        dv_acc[...] = jnp.zeros_like(dv_acc)
        dcol_ref[...] = jnp.zeros_like(dcol_ref)
        drow_ref[...] = jnp.zeros_like(drow_ref)
        for r0, nr, nk in tiles:
            rs = slice(r0, r0 + nr)
            causal = (r0 + lax.broadcasted_iota(jnp.int32, (nk, nr), 1)
                      >= lax.broadcasted_iota(jnp.int32, (nk, nr), 0))
            cblk = c_ref[0:nk, :]
            ctb = ct_ref[:, rs]
            for hh in range(2):
                head = 2 * p + hh
                hs = slice(hh * HEAD_DIM, (hh + 1) * HEAD_DIM)
                q = q_ref[rs, hs]
                k = k_ref[0:nk, hs]
                dof = do_ref[rs, hs]
                do16 = dof.astype(BF16)
                delta = jnp.sum(dof * o_ref[rs, hs], axis=1, keepdims=True)
                delta_row = jnp.broadcast_to(delta, (nr, LANES)).T[0:1, :]
                s_t = _dot_nt(k, q * ATT_SCALE) + (_pick_row(ctb, head) - _pick_col(cblk, head))
                p_t = jnp.where(causal, jnp.exp(s_t - lset_ref[hh:hh + 1, rs]), 0.0)
                ds_t = p_t * (_dot_nt(v_ref[0:nk, hs], do16) - delta_row)
                p16 = p_t.astype(BF16)
                ds16 = ds_t.astype(BF16)
                dv_acc[0:nk, hs] += _dot(p16, do16)
                dk_acc[0:nk, hs] += _dot(ds16, q) * ATT_SCALE
                dq_ref[rs, hs] = (_dot_tn(ds16, k) * ATT_SCALE).astype(BF16)
                drow_ref[hh:hh + 1, rs] = jnp.sum(ds_t, axis=0, keepdims=True)
                dcol_ref[0:nk, hs] -= jnp.broadcast_to(jnp.sum(ds_t, axis=1, keepdims=True), (nk, HEAD_DIM))
        dk_ref[...] = dk_acc[...].astype(BF16)
        dv_ref[...] = dv_acc[...].astype(BF16)

    pair = lambda p: (0, p)
    return pl.pallas_call(
        body, name="attn_bwd", grid=(npair,),
        in_specs=[pl.BlockSpec((tp, LANES), pair),
                  pl.BlockSpec((tp, LANES), lambda p: (0, npair + p)),
                  pl.BlockSpec((tp, LANES), lambda p: (0, 2 * npair + p)),
                  pl.BlockSpec((tp, LANES), pair),
                  pl.BlockSpec((tp, LANES), pair),
                  pl.BlockSpec((None, SUBLANES, tp), lambda p: (p, 0, 0)),
                  pl.BlockSpec((tp, LANES), lambda p: (0, 0)),
                  pl.BlockSpec((SUBLANES, tp), lambda p: (0, 0))],
        out_specs=[pl.BlockSpec((tp, LANES), pair), pl.BlockSpec((tp, LANES), pair),
                   pl.BlockSpec((tp, LANES), pair),
                   pl.BlockSpec((None, SUBLANES, tp), lambda p: (p, 0, 0)),
                   pl.BlockSpec((tp, LANES), pair)],
        out_shape=[jax.ShapeDtypeStruct((tp, aw), BF16), jax.ShapeDtypeStruct((tp, aw), BF16),
                   jax.ShapeDtypeStruct((tp, aw), BF16),
                   jax.ShapeDtypeStruct((npair, SUBLANES, tp), F32),
                   jax.ShapeDtypeStruct((tp, aw), F32)],
        scratch_shapes=[pltpu.VMEM((tp, LANES), F32), pltpu.VMEM((tp, LANES), F32)],
        compiler_params=_cparams("parallel"),
    )(qkv, qkv, qkv, do, o, lset, c, ct)


def fgate_bwd(dct8, drs, rest, bf_pad, fcol):
    tp = rest.shape[0]
    aw = drs.shape[1]
    nb = tp // ATT_BLOCK
    B = ATT_BLOCK

    def body(d_ref, drs_ref, f_ref, b_ref, dfl_ref, db_ref, pad_s):
        r_i = lax.broadcasted_iota(jnp.int32, (B, B), 0)
        c_i = lax.broadcasted_iota(jnp.int32, (B, B), 1)
        triu = (c_i >= r_i).astype(BF16)
        sel = (lax.broadcasted_iota(jnp.int32, (aw, LANES), 0)
               == HEAD_DIM * lax.broadcasted_iota(jnp.int32, (aw, LANES), 1)).astype(BF16)
        carry = jnp.zeros((1, LANES), F32)
        db = jnp.zeros((1, LANES), F32)
        pad_s[...] = jnp.zeros_like(pad_s)
        for i in range(nb - 1, -1, -1):
            sl = slice(i * B, (i + 1) * B)
            pad_s[0:SUBLANES, :] = d_ref[:, sl]
            dc = pad_s[...].T + _dot_split3(drs_ref[sl, :], sel)
            rc = _split3_dot(triu, dc)
            dlf = rc + carry
            carry = carry + rc[0:1, :]
            dfl = dlf * _sigmoid(-(f_ref[sl, :] + b_ref[...]))
            dfl_ref[sl, :] = dfl.astype(BF16)
            db = db + jnp.sum(dfl, axis=0, keepdims=True)
        db_ref[...] = db

    return pl.pallas_call(
        body, name="fgate_bwd", grid=(1,),
        in_specs=[pl.BlockSpec((SUBLANES, tp), lambda i: (0, 0)),
                  pl.BlockSpec((tp, aw), lambda i: (0, 0)),
                  pl.BlockSpec((tp, LANES), lambda i: (0, fcol)),
                  pl.BlockSpec((1, LANES), lambda i: (0, 0))],
        out_specs=[pl.BlockSpec((tp, LANES), lambda i: (0, 0)),
                   pl.BlockSpec((1, LANES), lambda i: (0, 0))],
        out_shape=[jax.ShapeDtypeStruct((tp, LANES), BF16), jax.ShapeDtypeStruct((1, LANES), F32)],
        scratch_shapes=[pltpu.VMEM((B, B), F32)],
        compiler_params=_cparams("arbitrary"),
    )(dct8, drs, rest, bf_pad)


def in_proj_bwd(dh2, parts, w_in_t, wrest_t, h, g1, tm):
    tp, d = h.shape
    dq, dk, dv, dxr, dyr, dfl = parts
    aw, rw = dq.shape[1], dxr.shape[1]

    def body(dh2_ref, dq_ref, dk_ref, dv_ref, dxr_ref, dyr_ref, dfl_ref, wq_ref, wr_ref, h_ref, g_ref,
             dh_ref, dg_ref):
        i = pl.program_id(0)
        dz = _dot(dq_ref[...], wq_ref[0:aw, :])
        dz += _dot(dk_ref[...], wq_ref[aw:2 * aw, :])
        dz += _dot(dv_ref[...], wq_ref[2 * aw:3 * aw, :])
        dz += _dot(dxr_ref[...], wr_ref[0:rw, :])
        dz += _dot(dyr_ref[...], wr_ref[rw:2 * rw, :])
        dz += _dot(dfl_ref[...], wr_ref[2 * rw:2 * rw + LANES, :])
        dx, dg = _rms_bwd(h_ref[...], g_ref[...], dz)
        dh_ref[...] = dh2_ref[...] + dx
        _accumulate(dg_ref, dg, i == 0)

    row = lambda i: (i, 0)
    fix = lambda i: (0, 0)
    return pl.pallas_call(
        body, name="in_proj_bwd", grid=(tp // tm,),
        in_specs=[pl.BlockSpec((tm, d), row),
                  pl.BlockSpec((tm, aw), row), pl.BlockSpec((tm, aw), row), pl.BlockSpec((tm, aw), row),
                  pl.BlockSpec((tm, rw), row), pl.BlockSpec((tm, rw), row), pl.BlockSpec((tm, LANES), row),
                  pl.BlockSpec((3 * aw, d), fix), pl.BlockSpec(wrest_t.shape, fix),
                  pl.BlockSpec((tm, d), row), pl.BlockSpec((1, d), fix)],
        out_specs=[pl.BlockSpec((tm, d), row), pl.BlockSpec((1, d), fix)],
        out_shape=[jax.ShapeDtypeStruct((tp, d), F32), jax.ShapeDtypeStruct((1, d), F32)],
        compiler_params=_cparams("arbitrary"),
    )(dh2, dq, dk, dv, dxr, dyr, dfl, w_in_t, wrest_t, h, g1)


def dw_in_t(z, parts, nh, tr):
    tp, d = z.shape
    dq, dk, dv, dxr, dyr, dfl = parts
    aw, rw = dq.shape[1], dxr.shape[1]
    d_in = 3 * aw + nh + 2 * rw
    nr = tp // tr
    offs = [(0, aw), (aw, aw), (2 * aw, aw), (3 * aw + nh, rw), (3 * aw + nh + rw, rw)]

    def body(z_ref, dq_ref, dk_ref, dv_ref, dxr_ref, dyr_ref, dfl_ref, o_ref, acc):
        r = pl.program_id(0)

        @pl.when(r == 0)
        def _():
            acc[...] = jnp.zeros_like(acc)

        zt = z_ref[...]
        for (o, n), ref in zip(offs, (dq_ref, dk_ref, dv_ref, dxr_ref, dyr_ref)):
            acc[o:o + n, :] += _dot_tn(ref[...], zt)
        acc[3 * aw:3 * aw + nh, :] += _dot_tn(dfl_ref[...], zt)[0:nh, :]

        @pl.when(r == nr - 1)
        def _():
            o_ref[...] = acc[...].astype(BF16)

    row = lambda r: (r, 0)
    return pl.pallas_call(
        body, name="dw_in", grid=(nr,),
        in_specs=[pl.BlockSpec((tr, d), row),
                  pl.BlockSpec((tr, aw), row), pl.BlockSpec((tr, aw), row), pl.BlockSpec((tr, aw), row),
                  pl.BlockSpec((tr, rw), row), pl.BlockSpec((tr, rw), row), pl.BlockSpec((tr, LANES), row)],
        out_specs=pl.BlockSpec((d_in, d), lambda r: (0, 0)),
        out_shape=jax.ShapeDtypeStruct((d_in, d), BF16),
        scratch_shapes=[pltpu.VMEM((d_in, d), F32)],
        compiler_params=_cparams("arbitrary"),
    )(z, dq, dk, dv, dxr, dyr, dfl)


def _place():
    return lax.axis_index("x"), lax.axis_index("y"), lax.axis_index("c")


HBM = pl.BlockSpec(memory_space=pltpu.HBM)
SEM = pl.BlockSpec(memory_space=pltpu.SEMAPHORE)
EFFECT = pltpu.SideEffectType.DATAFLOW_SIDE_EFFECTING


def _in_hbm(a):
    return pltpu.with_memory_space_constraint(a, pltpu.HBM)


def _as_list(a):
    return list(a) if isinstance(a, (list, tuple)) else [a]


def _gather_targets(x, y, c):
    return [(x, y, 1 - c), (1 - x, y, c), (x, 1 - y, c), (1 - x, 1 - y, c)]


def _slot(t):
    return 4 * t[0] + 2 * t[1] + t[2]


def gather_start(groups, name):
    flat = [a for g in groups for a in g]
    n = len(flat)
    ng = len(groups)
    lands = [lax.empty((N_DEV,) + a.shape, a.dtype) for a in flat]

    def body(*refs):
        src, land = refs[:n], refs[n:2 * n]
        sems = refs[2 * n:2 * n + 2 * ng]
        token = refs[-1]
        x, y, c = _place()
        me = 4 * x + 2 * y + c
        i = 0
        for gi, g in enumerate(groups):
            for a in range(len(g)):
                for k, t in enumerate(_gather_targets(x, y, c)):
                    pltpu.make_async_remote_copy(
                        src_ref=src[i], dst_ref=land[i].at[me],
                        send_sem=sems[2 * gi].at[4 * a + k], recv_sem=sems[2 * gi + 1].at[4 * a + k],
                        device_id=t, device_id_type=MESH).start()
                i += 1
        token[...] = jnp.zeros_like(token)

    sem_shapes = []
    for g in groups:
        sem_shapes += [pltpu.SemaphoreType.DMA((4 * len(g),)), pltpu.SemaphoreType.DMA((4 * len(g),))]
    out = pl.pallas_call(
        body, name=name,
        out_shape=sem_shapes + [pltpu.HBM(a.shape, a.dtype) for a in flat + lands]
        + [jax.ShapeDtypeStruct((SUBLANES, LANES), F32)],
        in_specs=[HBM] * (2 * n),
        out_specs=[SEM] * (2 * ng) + [HBM] * (2 * n) + [pl.BlockSpec(memory_space=pltpu.VMEM)],
        input_output_aliases={i: 2 * ng + i for i in range(2 * n)},
        compiler_params=pltpu.CompilerParams(has_side_effects=EFFECT),
    )(*[_in_hbm(a) for a in flat + lands])
    sems = out[:2 * ng]
    thru = out[2 * ng:2 * ng + 2 * n]
    srcs_t, lands_t = thru[:n], thru[n:]
    res, i = [], 0
    for gi, g in enumerate(groups):
        res.append((sems[2 * gi], sems[2 * gi + 1], srcs_t[i:i + len(g)], lands_t[i:i + len(g)]))
        i += len(g)
    return res, out[-1]


def gather_wait(send, recv, srcs, lands, after, name):
    n = len(srcs)

    def body(*refs):
        src, land = refs[:n], refs[n:2 * n]
        send_sem, recv_sem = refs[2 * n], refs[2 * n + 1]
        x, y, c = _place()
        for a in range(n):
            for k, t in enumerate(_gather_targets(x, y, c)):
                cp = pltpu.make_async_remote_copy(
                    src_ref=src[a], dst_ref=land[a].at[_slot(t)],
                    send_sem=send_sem.at[4 * a + k], recv_sem=recv_sem.at[4 * a + k],
                    device_id=t, device_id_type=MESH)
                cp.wait_send()
                cp.wait_recv()

    out = pl.pallas_call(
        body, name=name,
        out_shape=[pltpu.HBM(a.shape, a.dtype) for a in list(srcs) + list(lands)],
        in_specs=[HBM] * (2 * n) + [SEM, SEM] + [ANY] * len(_as_list(after)),
        out_specs=[HBM] * (2 * n),
        input_output_aliases={i: i for i in range(2 * n)},
        compiler_params=pltpu.CompilerParams(has_side_effects=EFFECT),
    )(*srcs, *lands, send, recv, *_as_list(after))
    return out[:n], out[n:]


def forward_start(lands, name):
    n = len(lands)

    def body(*refs):
        land = refs[:n]
        send_sem, recv_sem = refs[n], refs[n + 1]
        token = refs[-1]
        x, y, c = _place()
        for a in range(n):
            for j, chip in enumerate([(1 - x, y), (x, 1 - y), (1 - x, 1 - y)]):
                blk = land[a].at[_slot((*chip, c))]
                pltpu.make_async_remote_copy(src_ref=blk, dst_ref=blk, send_sem=send_sem.at[3 * a + j],
                                             recv_sem=recv_sem.at[3 * a + j], device_id=(x, y, 1 - c),
                                             device_id_type=MESH).start()
        token[...] = jnp.zeros_like(token)

    out = pl.pallas_call(
        body, name=name,
        out_shape=[pltpu.SemaphoreType.DMA((3 * n,)), pltpu.SemaphoreType.DMA((3 * n,))]
        + [pltpu.HBM(a.shape, a.dtype) for a in lands] + [jax.ShapeDtypeStruct((SUBLANES, LANES), F32)],
        in_specs=[HBM] * n,
        out_specs=[SEM, SEM] + [HBM] * n + [pl.BlockSpec(memory_space=pltpu.VMEM)],
        input_output_aliases={i: 2 + i for i in range(n)},
        compiler_params=pltpu.CompilerParams(has_side_effects=EFFECT),
    )(*[_in_hbm(a) for a in lands])
    return out[0], out[1], out[2:2 + n], out[-1][0, 0]


def forward_wait(send, recv, lands, after, name):
    n = len(lands)

    def body(*refs):
        land = refs[:n]
        send_sem, recv_sem = refs[n], refs[n + 1]
        x, y, c = _place()
        for a in range(n):
            for j, chip in enumerate([(1 - x, y), (x, 1 - y), (1 - x, 1 - y)]):
                cp = pltpu.make_async_remote_copy(
                    src_ref=land[a].at[_slot((*chip, c))], dst_ref=land[a].at[_slot((*chip, 1 - c))],
                    send_sem=send_sem.at[3 * a + j], recv_sem=recv_sem.at[3 * a + j],
                    device_id=(x, y, 1 - c), device_id_type=MESH)
                cp.wait_send()
                cp.wait_recv()

    return pl.pallas_call(
        body, name=name,
        out_shape=[pltpu.HBM(a.shape, a.dtype) for a in lands],
        in_specs=[HBM] * n + [SEM, SEM, ANY],
        out_specs=[HBM] * n,
        input_output_aliases={i: i for i in range(n)},
        compiler_params=pltpu.CompilerParams(has_side_effects=EFFECT),
    )(*lands, send, recv, after)


def _relations():
    return [(dx, dy, dc) for dx in (0, 1) for dy in (0, 1) for dc in (0, 1) if dx + dy + dc]


def _peer(x, y, c, rel):
    return ((1 - x) if rel[0] else x, (1 - y) if rel[1] else y, (1 - c) if rel[2] else c)


def exchange_start(srcs, lands, layer, name):
    n = len(srcs)

    def body(*refs):
        src, land = refs[:n], refs[n:2 * n]
        send_sem, recv_sem = refs[2 * n], refs[2 * n + 1]
        token = refs[-1]
        x, y, c = _place()
        me = 4 * x + 2 * y + c
        for k, rel in enumerate(_relations()):
            peer = _peer(x, y, c, rel)
            for a in range(n):
                pltpu.make_async_remote_copy(
                    src_ref=src[a] if layer is None else src[a].at[_slot(peer)],
                    dst_ref=land[a].at[me] if layer is None else land[a].at[me, layer],
                    send_sem=send_sem.at[7 * a + k], recv_sem=recv_sem.at[7 * a + k],
                    device_id=peer, device_id_type=MESH).start()
        token[...] = jnp.zeros_like(token)

    out = pl.pallas_call(
        body, name=name,
        out_shape=[pltpu.SemaphoreType.DMA((7 * n,)), pltpu.SemaphoreType.DMA((7 * n,))]
        + [pltpu.HBM(a.shape, a.dtype) for a in list(srcs) + list(lands)]
        + [jax.ShapeDtypeStruct((SUBLANES, LANES), F32)],
        in_specs=[HBM] * (2 * n),
        out_specs=[SEM, SEM] + [HBM] * (2 * n) + [pl.BlockSpec(memory_space=pltpu.VMEM)],
        input_output_aliases={i: 2 + i for i in range(2 * n)},
        compiler_params=pltpu.CompilerParams(has_side_effects=EFFECT),
    )(*[_in_hbm(a) for a in list(srcs) + list(lands)])
    return out[0], out[1], out[2:2 + n], out[2 + n:2 + 2 * n], out[-1][0, 0]


def exchange_wait(send, recv, srcs, lands, after, layer, name):
    n = len(srcs)

    def body(*refs):
        src, land = refs[:n], refs[n:2 * n]
        send_sem, recv_sem = refs[2 * n], refs[2 * n + 1]
        x, y, c = _place()
        for k, rel in enumerate(_relations()):
            peer = _peer(x, y, c, rel)
            for a in range(n):
                cp = pltpu.make_async_remote_copy(
                    src_ref=src[a] if layer is None else src[a].at[_slot(peer)],
                    dst_ref=land[a].at[_slot(peer)] if layer is None else land[a].at[_slot(peer), layer],
                    send_sem=send_sem.at[7 * a + k], recv_sem=recv_sem.at[7 * a + k],
                    device_id=peer, device_id_type=MESH)
                cp.wait_send()
                cp.wait_recv()

    out = pl.pallas_call(
        body, name=name,
        out_shape=[pltpu.HBM(a.shape, a.dtype) for a in list(srcs) + list(lands)],
        in_specs=[HBM] * (2 * n) + [SEM, SEM] + [ANY] * len(_as_list(after)),
        out_specs=[HBM] * (2 * n),
        input_output_aliases={i: i for i in range(2 * n)},
        compiler_params=pltpu.CompilerParams(has_side_effects=EFFECT),
    )(*srcs, *lands, send, recv, *_as_list(after))
    return out[:n], out[n:]


def _adamw_math(g, w, m, v):
    m = ADAM_B1 * m + (1.0 - ADAM_B1) * g
    v = ADAM_B2 * v + (1.0 - ADAM_B2) * (g * g)
    m_hat = m / (1.0 - ADAM_B1 ** ADAM_STEP)
    v_hat = v / (1.0 - ADAM_B2 ** ADAM_STEP)
    delta = -ADAM_LR * (m_hat / (jnp.sqrt(v_hat) + ADAM_EPS) + ADAM_WD * w)
    return delta, m, v


def sum_adamw(parts, w, m, v, tr, name):
    npart, rows, cols = parts.shape

    def body(p_ref, w_ref, m_ref, v_ref, g_ref, d_ref, nm_ref, nv_ref):
        g = p_ref[0].astype(F32)
        for p in range(1, npart):
            g = g + p_ref[p].astype(F32)
        delta, nm, nv = _adamw_math(g, w_ref[...], m_ref[...], v_ref[...])
        g_ref[...] = g
        d_ref[...] = delta
        nm_ref[...] = nm
        nv_ref[...] = nv

    blk = pl.BlockSpec((tr, cols), lambda i: (i, 0))
    return pl.pallas_call(
        body, name=name, grid=(rows // tr,),
        in_specs=[pl.BlockSpec((npart, tr, cols), lambda i: (0, i, 0)), blk, blk, blk],
        out_specs=[blk] * 4,
        out_shape=[jax.ShapeDtypeStruct((rows, cols), F32)] * 4,
        compiler_params=_cparams("parallel"),
    )(parts, w, m, v)


def sum_adamw_t(parts, w, m, v, name):
    npart, nl, rows, cols = parts.shape

    def body(p_ref, w_ref, m_ref, v_ref, g_ref, d_ref, nm_ref, nv_ref):
        g = p_ref[0].astype(F32)
        for p in range(1, npart):
            g = g + p_ref[p].astype(F32)
        delta, nm, nv = _adamw_math(g, w_ref[...], m_ref[...], v_ref[...])
        g_ref[...] = g
        d_ref[...] = delta
        nm_ref[...] = nm
        nv_ref[...] = nv

    blk = pl.BlockSpec((None, rows, cols), lambda l: (l, 0, 0))
    return pl.pallas_call(
        body, name=name, grid=(nl,),
        in_specs=[pl.BlockSpec((npart, None, rows, cols), lambda l: (0, l, 0, 0)), blk, blk, blk],
        out_specs=[blk] * 4,
        out_shape=[jax.ShapeDtypeStruct((nl, rows, cols), F32)] * 4,
        compiler_params=_cparams("parallel"),
    )(parts, w, m, v)


def adamw_group(gs, ws, ms, vs, name):
    n = len(gs)

    def body(*refs):
        g, w, m, v, outs = refs[:n], refs[n:2 * n], refs[2 * n:3 * n], refs[3 * n:4 * n], refs[4 * n:]
        for i in range(n):
            delta, nm, nv = _adamw_math(g[i][...], w[i][...], m[i][...], v[i][...])
            outs[i][...] = delta
            outs[n + i][...] = nm
            outs[2 * n + i][...] = nv

    vmem = pl.BlockSpec(memory_space=pltpu.VMEM)
    out = pl.pallas_call(
        body, name=name,
        in_specs=[vmem] * (4 * n), out_specs=[vmem] * (3 * n),
        out_shape=[jax.ShapeDtypeStruct(a.shape, F32) for a in list(ws) * 3],
        compiler_params=_cparams(),
    )(*gs, *ws, *ms, *vs)
    return out[:n], out[n:2 * n], out[2 * n:]


def sum_parts(parts, name):
    npart, rows, cols = parts.shape

    def body(p_ref, g_ref):
        g = p_ref[0].astype(F32)
        for p in range(1, npart):
            g = g + p_ref[p].astype(F32)
        g_ref[...] = g

    return pl.pallas_call(
        body, name=name, grid=(1,),
        in_specs=[pl.BlockSpec((npart, rows, cols), lambda i: (0, 0, 0))],
        out_specs=pl.BlockSpec((rows, cols), lambda i: (0, 0)),
        out_shape=jax.ShapeDtypeStruct((rows, cols), F32),
        compiler_params=_cparams("arbitrary"),
    )(parts)


def _round_up(n, m):
    return (n + m - 1) // m * m


def _block_diag_pairs(w):
    nb, b, _ = w.shape
    per = LANES // b
    ng = nb // per
    w = w.reshape(ng, per, b, b)
    eye = jnp.eye(per, dtype=w.dtype)
    out = jnp.einsum('gpij,pq->gpiqj', w, eye).reshape(ng, LANES, LANES)
    return out.astype(BF16)


def _block_diag_extract(g, b):
    ng = g.shape[0]
    per = LANES // b
    g = g.reshape(ng, per, b, per, b)
    idx = jnp.arange(per)
    return g[:, idx, :, idx, :].transpose(1, 0, 2, 3).reshape(ng * per, b, b)


def _tiles(v):
    v = v.reshape(-1)
    n = _round_up(v.shape[0], SUBLANES * LANES)
    return jnp.pad(v, (0, n - v.shape[0])).reshape(-1, LANES)


SMALL = ['attn_norm_g', 'b_f', 'conv_w', 'conv_b', 'w_gate_a', 'b_gate_a', 'w_gate_x', 'b_gate_x',
         'lru_L', 'attn_out_g', 'rec_out_g', 'mlp_norm_g', 'final_g', 'meta']


def _pack(d):
    return jnp.concatenate([_tiles(d[n]) for n in SMALL], axis=0)


def _unpack(vec, shapes):
    out, r = {}, 0
    for n in SMALL:
        size = math.prod(shapes[n])
        nr = _round_up(size, SUBLANES * LANES) // LANES
        out[n] = vec[r:r + nr].reshape(-1)[:size].reshape(shapes[n])
        r += nr
    return out


def _row_tile(tp):
    return tp // 4 if (tp // 4) % 16 == 0 else tp


def local_step(x, tgt, meta, small, hooks):
    s, d = x.shape
    t_real = s + N_META
    tp = _round_up(t_real, ATT_BLOCK)
    depth = small['attn_norm_g'].shape[0]
    nh = small['b_f'].shape[1]
    rw = small['conv_b'].shape[1]
    blk = small['w_gate_a'].shape[2]
    tm = _row_tile(tp)
    tm2 = tp // 2
    fcol = 2 * rw // LANES

    h = jnp.concatenate([meta, x, jnp.zeros((tp - t_real, d), F32)], axis=0)
    tgt_p = jnp.pad(tgt, ((N_META, tp - t_real), (0, 0)))
    row = lambda v: v.reshape(1, -1)
    bf_pad = jnp.pad(small['b_f'], ((0, 0), (0, LANES - nh)))

    saved = []
    for l in range(depth):
        w_in_t, wrest_t, wout, tok_w = hooks.mixer_weights(l, h)
        wga = _block_diag_pairs(small['w_gate_a'][l])
        wgx = _block_diag_pairs(small['w_gate_x'][l])
        z, qkv, rest = in_proj(h, row(small['attn_norm_g'][l]) + tok_w, w_in_t, wrest_t, 3 * nh * HEAD_DIM, tm)
        c, ct = fgate_fwd(rest, bf_pad[l:l + 1], fcol)
        o, lset = attn_fwd(qkv, c, ct, nh)
        rec, hr, xc = rec_fwd(rest, small['conv_w'][l], row(small['conv_b'][l]), wga, row(small['b_gate_a'][l]),
                              wgx, row(small['b_gate_x'][l]), row(small['lru_L'][l]), rw)
        gup, gdown, tok_w = hooks.mlp_weights(l, rec)
        h2, mix, z2 = out_proj(h, o, rec, row(small['attn_out_g'][l]), row(small['rec_out_g'][l]), wout,
                               row(small['mlp_norm_g'][l]) + tok_w, tm)
        u, h3 = mlp_fwd(z2, h2, gup, gdown, tm2)
        saved.append(dict(h=h, z=z, qkv=qkv, rest=rest, c=c, ct=ct, o=o, lset=lset, rec=rec, hr=hr, xc=xc,
                          h2=h2, mix=mix, z2=z2, u=u, wga=wga, wgx=wgx,
                          w_in_t=w_in_t, wrest_t=wrest_t, wout=wout, gup=gup, gdown=gdown))
        h = h3

    dh, dgf, loss = loss_head(h, row(small['final_g']), tgt_p, t_real, tm)

    gs = {n: [None] * depth for n in SMALL if n not in ('final_g', 'meta')}
    tok = jnp.zeros((), F32)
    for l in reversed(range(depth)):
        sv = saved[l]
        gup, gdown = sv['gup'], sv['gdown']
        tf = gup.shape[2]
        dup, dh2, dg2, dhb = mlp_bwd(dh, sv['u'], sv['h2'], row(small['mlp_norm_g'][l]) + tok, gup, gdown, tm)
        gs['mlp_norm_g'][l] = dg2[0]
        do, drec, dga, dgr = out_proj_bwd(dh2, sv['o'], sv['rec'], row(small['attn_out_g'][l]),
                                          row(small['rec_out_g'][l]), sv['wout'], tm)
        gs['attn_out_g'][l] = dga[0]
        gs['rec_out_g'][l] = dgr[0]
        dw_down, dw_up = dw_mlp(sv['u'], dhb, sv['z2'], dup, tf)
        blocks = dict(
            w_down=dw_down, w_up=dw_up,
            w_out=mm_tn(sv['mix'], dh2, tk=d, tn=d // 2, out_dtype=BF16,
                        name="dw_out").reshape(N_DEV, d // N_DEV, d))
        tok = hooks.grads_ready(l, 'mlp', blocks)
        dxr, dyr, dwga, dwgx, vec = rec_bwd(drec, sv['hr'], sv['xc'], sv['rest'], small['conv_w'][l],
                                            row(small['conv_b'][l]) + tok, sv['wga'], row(small['b_gate_a'][l]),
                                            sv['wgx'], row(small['b_gate_x'][l]), row(small['lru_L'][l]), rw)
        gs['w_gate_a'][l] = _block_diag_extract(dwga, blk)
        gs['w_gate_x'][l] = _block_diag_extract(dwgx, blk)
        vec = vec.transpose(1, 0, 2).reshape(SUBLANES, rw)
        gs['conv_w'][l] = vec[0:CONV_WIDTH]
        gs['conv_b'][l] = vec[4]
        gs['b_gate_a'][l] = vec[5]
        gs['b_gate_x'][l] = vec[6]
        gs['lru_L'][l] = vec[7]
        dq, dk, dv, drow, dcol = attn_bwd(sv['qkv'], do, sv['o'], sv['lset'], sv['c'], sv['ct'] + tok, nh)
        drow8 = drow[:, 0:2, :].reshape(nh, tp)
        if nh < SUBLANES:
            drow8 = jnp.pad(drow8, ((0, SUBLANES - nh), (0, 0)))
        dfl, dbf = fgate_bwd(drow8, dcol, sv['rest'], bf_pad[l:l + 1], fcol)
        gs['b_f'][l] = dbf[0, 0:nh]
        parts = (dq, dk, dv, dxr, dyr, dfl)
        dh, dg1 = in_proj_bwd(dh2, parts, sv['w_in_t'], sv['wrest_t'], sv['h'], row(small['attn_norm_g'][l]), tm)
        gs['attn_norm_g'][l] = dg1[0]
        tok = jnp.zeros((), F32)
        if l == 0:
            grads = {n: jnp.stack(v) for n, v in gs.items()}
            grads['final_g'] = dgf[0]
            grads['meta'] = dh[0:N_META]
            tok = hooks.small_ready(grads)
        dw_in = dw_in_t(sv['z'], parts, nh, tm2)
        dw_in = dw_in.reshape(N_DEV, dw_in.shape[0] // N_DEV, d) + tok.astype(BF16)
        tok = hooks.grads_ready(l, 'in', dict(w_in=dw_in))

    return loss[0, 0], dh, tok


def prep_weights(g_in, g_out, nh, rw):
    d = g_in.shape[2]
    w_in_t = g_in.reshape(-1, d)
    f0 = 3 * nh * HEAD_DIM
    wrest_t = jnp.concatenate([w_in_t[f0 + nh:f0 + nh + 2 * rw],
                               jnp.pad(w_in_t[f0:f0 + nh], ((0, LANES - nh), (0, 0)))], axis=0)
    return w_in_t, wrest_t, g_out.reshape(d, d)


BIG = ['w_in', 'w_out', 'w_up', 'w_down']
EXCHANGE_GROUPS = {'mlp': ['w_down', 'w_up', 'w_out'], 'in': ['w_in']}
WEIGHTS = ['meta', 'attn_norm_g', 'w_in', 'b_f', 'conv_w', 'conv_b', 'w_gate_a', 'b_gate_a', 'w_gate_x', 'b_gate_x',
           'lru_L', 'attn_out_g', 'rec_out_g', 'w_out', 'mlp_norm_g', 'w_up', 'w_down', 'final_g']


def _set_own(arr, own, me):
    return lax.dynamic_update_slice_in_dim(arr, own[None], me, axis=0)


class _Step:
    def __init__(self, w, nh, rw, me):
        self.w, self.nh, self.rw, self.me = w, nh, rw, me
        depth = w['w_in'].shape[0]
        first = [w['w_in_t'][:, 0, :].astype(BF16), w['w_out'][0].astype(BF16), w['meta'], w['conv_w']]
        self.pending, token = gather_start([first], "gather_start_0")
        zero = token[0, 0].astype(BF16)
        groups = [[w['w_up'][0].astype(BF16) + zero, w['w_down'][0].astype(BF16) + zero]]
        for l in range(1, depth):
            groups.append([w['w_in_t'][:, l, :].astype(BF16) + zero, w['w_out'][l].astype(BF16) + zero])
            groups.append([w['w_up'][l].astype(BF16) + zero, w['w_down'][l].astype(BF16) + zero])
        rest, _ = gather_start(groups, "gather_start_1")
        self.pending += rest
        self.first_after = rest[0][2][0]
        self.gathered = {}
        self.passing = {}
        self.token = jnp.zeros((), F32)
        self.lands = {n: lax.empty((N_DEV,) + w[n].shape, BF16) for n in BIG}
        din8, _, d = w['w_in_t'].shape
        self.lands['w_in'] = lax.empty((N_DEV, depth, din8, d), BF16)
        self.started = []
        self.small = None

    def _pass_on(self, gi, after):
        if gi < len(self.pending) and gi not in self.passing:
            send, recv, srcs, lands = self.pending[gi]
            srcs, lands = gather_wait(send, recv, srcs, lands, after, "gather_wait_%d" % gi)
            fsend, frecv, lands, token = forward_start(lands, "forward_start_%d" % gi)
            self.passing[gi] = (fsend, frecv, srcs, lands)
            self.token = token

    def group(self, gi, after):
        if gi not in self.gathered:
            self._pass_on(gi, after)
            fsend, frecv, srcs, lands = self.passing[gi]
            lands = forward_wait(fsend, frecv, lands, after, "forward_wait_%d" % gi)
            self.gathered[gi] = [_set_own(g, own, self.me) for g, own in zip(lands, srcs)]
            if gi >= 2:
                self._pass_on(gi + 1, lands[0])
        return self.gathered[gi]

    def mixer_weights(self, l, after):
        g = self.group(2 * l, after)
        return (*prep_weights(g[0], g[1], self.nh, self.rw), self.token)

    def mlp_weights(self, l, after):
        g = self.group(2 * l + 1, after)
        return g[0], g[1], self.token

    def grads_ready(self, l, group, blocks):
        names = EXCHANGE_GROUPS[group]
        send, recv, srcs, lands, token = exchange_start(
            [blocks[n] for n in names], [self.lands[n] for n in names], l, "exchange_start_%s_%d" % (group, l))
        for n, a in zip(names, lands):
            self.lands[n] = a
        self.started.append((l, group, send, recv, srcs))
        return token

    def small_ready(self, grads):
        self.small_shapes = {n: grads[n].shape for n in SMALL}
        packed = _pack(grads).astype(BF16)
        send, recv, srcs, lands, token = exchange_start(
            [packed], [lax.empty((N_DEV,) + packed.shape, BF16)], None, "small_start")
        self.small = (send, recv, srcs, lands)
        return token

    def small_sum(self, after):
        send, recv, srcs, lands = self.small
        srcs, lands = exchange_wait(send, recv, srcs, lands, after, None, "small_wait")
        parts = _set_own(lands[0], srcs[0], self.me)
        return _unpack(sum_parts(parts, "sum_small_grads"), self.small_shapes)

    def received(self, group, after):
        names = EXCHANGE_GROUPS[group]
        own = {n: [None] * self.w[n].shape[0] for n in names}
        for l, grp, send, recv, srcs in self.started:
            if grp != group:
                continue
            srcs, lands = exchange_wait(send, recv, srcs, [self.lands[n] for n in names], after, l,
                                        "exchange_wait_%s_%d" % (group, l))
            for n, a, sr in zip(names, lands, srcs):
                self.lands[n] = a
                own[n][l] = lax.dynamic_index_in_dim(sr, self.me, 0, keepdims=False)
        return {n: _set_own(self.lands[n], jnp.stack(own[n]), self.me) for n in names}


def kernel(x, meta, attn_norm_g, w_in, b_f, conv_w, conv_b, w_gate_a, b_gate_a, w_gate_x, b_gate_x, lru_L, attn_out_g, rec_out_g, w_out, mlp_norm_g, w_up, w_down, final_g, loss_target, m_meta, m_attn_norm_g, m_w_in, m_b_f, m_conv_w, m_conv_b, m_w_gate_a, m_b_gate_a, m_w_gate_x, m_b_gate_x, m_lru_L, m_attn_out_g, m_rec_out_g, m_w_out, m_mlp_norm_g, m_w_up, m_w_down, m_final_g, v_meta, v_attn_norm_g, v_w_in, v_b_f, v_conv_w, v_conv_b, v_w_gate_a, v_b_gate_a, v_w_gate_x, v_b_gate_x, v_lru_L, v_attn_out_g, v_rec_out_g, v_w_out, v_mlp_norm_g, v_w_up, v_w_down, v_final_g):
    w = dict(meta=meta, attn_norm_g=attn_norm_g, w_in=w_in, b_f=b_f, conv_w=conv_w, conv_b=conv_b,
             w_gate_a=w_gate_a, b_gate_a=b_gate_a, w_gate_x=w_gate_x, b_gate_x=b_gate_x, lru_L=lru_L,
             attn_out_g=attn_out_g, rec_out_g=rec_out_g, w_out=w_out, mlp_norm_g=mlp_norm_g, w_up=w_up,
             w_down=w_down, final_g=final_g)
    mo = dict(meta=m_meta, attn_norm_g=m_attn_norm_g, w_in=m_w_in, b_f=m_b_f, conv_w=m_conv_w, conv_b=m_conv_b,
              w_gate_a=m_w_gate_a, b_gate_a=m_b_gate_a, w_gate_x=m_w_gate_x, b_gate_x=m_b_gate_x, lru_L=m_lru_L,
              attn_out_g=m_attn_out_g, rec_out_g=m_rec_out_g, w_out=m_w_out, mlp_norm_g=m_mlp_norm_g,
              w_up=m_w_up, w_down=m_w_down, final_g=m_final_g)
    vo = dict(meta=v_meta, attn_norm_g=v_attn_norm_g, w_in=v_w_in, b_f=v_b_f, conv_w=v_conv_w, conv_b=v_conv_b,
              w_gate_a=v_w_gate_a, b_gate_a=v_b_gate_a, w_gate_x=v_w_gate_x, b_gate_x=v_b_gate_x, lru_L=v_lru_L,
              attn_out_g=v_attn_out_g, rec_out_g=v_rec_out_g, w_out=v_w_out, mlp_norm_g=v_mlp_norm_g,
              w_up=v_w_up, w_down=v_w_down, final_g=v_final_g)
    depth = w_in.shape[0]
    nh = b_f.shape[1]
    rw = conv_b.shape[1]
    me = 4 * lax.axis_index("x") + 2 * lax.axis_index("y") + lax.axis_index("c")

    w['w_in_t'] = jnp.transpose(w_in, (2, 0, 1))
    swap = lambda a: jnp.swapaxes(a, 1, 2)
    step = _Step(w, nh, rw, me)
    g0 = step.group(0, step.first_after)
    meta_full = g0[2].transpose(1, 0, 2).reshape(N_META, -1)
    conv_full = g0[3].transpose(1, 2, 0, 3).reshape(depth, CONV_WIDTH, rw)
    small = {n: w[n] for n in SMALL}
    small['conv_w'] = conv_full

    loss_part, dh0, tok = local_step(x[0], loss_target[0], meta_full, small, step)
    loss = lax.psum(loss_part, ("x", "y", "c"))
    grad_x = dh0[N_META:N_META + x.shape[1]][None]

    out_g, out_d, out_m, out_v = {}, {}, {}, {}

    def update_big(group, after):
        for n, r in step.received(group, after).items():
            if n == 'w_in':
                out = sum_adamw_t(r, swap(w[n]), swap(mo[n]), swap(vo[n]), "adamw_w_in")
                out = [swap(a) for a in out]
            else:
                shp = w[n].shape
                rows, cols = shp[0] * shp[1], shp[2]
                tr = min(512 if cols <= 512 else 256, rows)
                out = sum_adamw(r.reshape(N_DEV, rows, cols), w[n].reshape(rows, cols), mo[n].reshape(rows, cols),
                                vo[n].reshape(rows, cols), tr, "adamw_" + n)
                out = [a.reshape(shp) for a in out]
            out_g[n], out_d[n], out_m[n], out_v[n] = out
            after = out[0]
        return after

    update_big('mlp', step.started[-1][4][0])

    gsum = step.small_sum([out_g[n] for n in EXCHANGE_GROUPS['mlp']])
    gsum['meta'] = lax.dynamic_slice_in_dim(gsum['meta'], me * meta.shape[1], meta.shape[1], axis=1)
    gsum['conv_w'] = lax.dynamic_slice_in_dim(gsum['conv_w'], me * conv_w.shape[2], conv_w.shape[2], axis=2)
    as2d = lambda a: a.reshape(-1, a.shape[-1])
    deltas, new_m, new_v = adamw_group([as2d(gsum[n]) for n in SMALL], [as2d(w[n]) for n in SMALL],
                                       [as2d(mo[n]) for n in SMALL], [as2d(vo[n]) for n in SMALL], "adamw_small")
    for i, n in enumerate(SMALL):
        out_g[n] = gsum[n]
        out_d[n], out_m[n], out_v[n] = [a[i].reshape(w[n].shape) for a in (deltas, new_m, new_v)]

    update_big('in', deltas[0])

    return (loss, grad_x, *[out_g[n] for n in WEIGHTS], *[out_d[n] for n in WEIGHTS],
            *[out_m[n] for n in WEIGHTS], *[out_v[n] for n in WEIGHTS])
```

```python
import functools
import math

import jax
import jax.numpy as jnp
from jax import lax
from jax.experimental import pallas as pl
from jax.experimental.pallas import tpu as pltpu

F32 = jnp.float32
BF16 = jnp.bfloat16

N_DEV = 8
N_META = 16
HEAD_DIM = 64
CONV_WIDTH = 4
RG_C = 8.0
NORM_EPS = 1e-6
LANES = 128
SUBLANES = 8
ATT_BLOCK = 128
ATT_TQ = 512
NEG_BIG = -1e30
ATT_SCALE = 1.0 / math.sqrt(HEAD_DIM)

ADAM_LR = 0.001
ADAM_B1 = 0.9
ADAM_B2 = 0.999
ADAM_EPS = 1e-08
ADAM_WD = 0.01
ADAM_STEP = 10

VMEM_LIMIT_BYTES = 56 * 1024 * 1024
MESH = pl.DeviceIdType.MESH
ANY = pl.BlockSpec(memory_space=pl.ANY)


def _cparams(*sem):
    return pltpu.CompilerParams(dimension_semantics=sem if sem else None,
                                vmem_limit_bytes=VMEM_LIMIT_BYTES)


def _dot(a, b):
    return jnp.dot(a, b, preferred_element_type=F32)


def _dot_nt(a, b):
    return lax.dot_general(a, b, (((1,), (1,)), ((), ())), preferred_element_type=F32)


def _dot_tn(a, b):
    return lax.dot_general(a, b, (((0,), (0,)), ((), ())), preferred_element_type=F32)


def _sigmoid(x):
    return 0.5 * (1.0 + jnp.tanh(0.5 * x))


def _log_sigmoid(x):
    return jnp.minimum(x, 0.0) - jnp.log(1.0 + jnp.exp(-jnp.abs(x)))


def _expm1(x):
    series = x * (1.0 + x * (0.5 + x * (1.0 / 6.0 + x * (1.0 / 24.0))))
    return jnp.where(jnp.abs(x) < 1e-2, series, jnp.exp(x) - 1.0)


_GELU_K = math.sqrt(2.0 / math.pi)
_GELU_C = 0.044715


def _gelu(x):
    t = jnp.tanh(_GELU_K * (x + _GELU_C * x * x * x))
    return 0.5 * x * (1.0 + t)


def _gelu_grad(x):
    t = jnp.tanh(_GELU_K * (x + _GELU_C * x * x * x))
    return 0.5 * (1.0 + t) + 0.5 * x * (1.0 - t * t) * _GELU_K * (1.0 + 3.0 * _GELU_C * x * x)


def _split3_dot(tri, x):
    hi = x.astype(BF16)
    r1 = x - hi.astype(F32)
    mid = r1.astype(BF16)
    lo = (r1 - mid.astype(F32)).astype(BF16)
    return _dot(tri, hi) + _dot(tri, mid) + _dot(tri, lo)


def _dot_split3(x, sel):
    hi = x.astype(BF16)
    r1 = x - hi.astype(F32)
    mid = r1.astype(BF16)
    lo = (r1 - mid.astype(F32)).astype(BF16)
    return _dot(hi, sel) + _dot(mid, sel) + _dot(lo, sel)


def _rms_fwd(x, g):
    r = lax.rsqrt(jnp.mean(x * x, axis=-1, keepdims=True) + NORM_EPS)
    return x * r * g


def _rms_bwd(x, g, dy):
    r = lax.rsqrt(jnp.mean(x * x, axis=-1, keepdims=True) + NORM_EPS)
    xn = x * r
    dxn = dy * g
    dx = r * (dxn - xn * jnp.mean(dxn * xn, axis=-1, keepdims=True))
    return dx, jnp.sum(dy * xn, axis=0, keepdims=True)


def _accumulate(ref, val, first):
    @pl.when(first)
    def _():
        ref[...] = val

    @pl.when(jnp.logical_not(first))
    def _():
        ref[...] += val


def in_proj(h, g1, w_in_t, wrest_t, nq, tm):
    tp, d = h.shape
    nr = wrest_t.shape[0]

    def body(h_ref, g_ref, wq_ref, wr_ref, z_ref, qkv_ref, rest_ref):
        z = _rms_fwd(h_ref[...], g_ref[...]).astype(BF16)
        z_ref[...] = z
        qkv_ref[...] = _dot_nt(z, wq_ref[...]).astype(BF16)
        rest_ref[...] = _dot_nt(z, wr_ref[...])

    return pl.pallas_call(
        body, name="in_proj", grid=(tp // tm,),
        in_specs=[pl.BlockSpec((tm, d), lambda i: (i, 0)),
                  pl.BlockSpec((1, d), lambda i: (0, 0)),
                  pl.BlockSpec((nq, d), lambda i: (0, 0)),
                  pl.BlockSpec((nr, d), lambda i: (0, 0))],
        out_specs=[pl.BlockSpec((tm, d), lambda i: (i, 0)),
                   pl.BlockSpec((tm, nq), lambda i: (i, 0)),
                   pl.BlockSpec((tm, nr), lambda i: (i, 0))],
        out_shape=[jax.ShapeDtypeStruct((tp, d), BF16),
                   jax.ShapeDtypeStruct((tp, nq), BF16),
                   jax.ShapeDtypeStruct((tp, nr), F32)],
        compiler_params=_cparams("parallel"),
    )(h, g1, w_in_t, wrest_t)


def fgate_fwd(rest, bf_pad, fcol):
    tp = rest.shape[0]
    nb = tp // ATT_BLOCK

    def body(f_ref, b_ref, c_ref, ct_ref):
        r_i = lax.broadcasted_iota(jnp.int32, (ATT_BLOCK, ATT_BLOCK), 0)
        c_i = lax.broadcasted_iota(jnp.int32, (ATT_BLOCK, ATT_BLOCK), 1)
        tri = (r_i >= c_i).astype(BF16)
        carry = jnp.zeros((1, LANES), F32)
        for i in range(nb):
            sl = slice(i * ATT_BLOCK, (i + 1) * ATT_BLOCK)
            lf = _log_sigmoid(f_ref[sl, :] + b_ref[...])
            cs = _split3_dot(tri, lf) + carry
            carry = cs[ATT_BLOCK - 1:ATT_BLOCK, :]
            c_ref[sl, :] = cs
            ct_ref[:, sl] = cs.T[0:SUBLANES, :]

    return pl.pallas_call(
        body, name="fgate_fwd", grid=(1,),
        in_specs=[pl.BlockSpec((tp, LANES), lambda i: (0, fcol)),
                  pl.BlockSpec((1, LANES), lambda i: (0, 0))],
        out_specs=[pl.BlockSpec((tp, LANES), lambda i: (0, 0)),
                   pl.BlockSpec((SUBLANES, tp), lambda i: (0, 0))],
        out_shape=[jax.ShapeDtypeStruct((tp, LANES), F32),
                   jax.ShapeDtypeStruct((SUBLANES, tp), F32)],
        compiler_params=_cparams("arbitrary"),
    )(rest, bf_pad)


def _pick_col(blk, head):
    lane = lax.broadcasted_iota(jnp.int32, blk.shape, 1)
    return jnp.sum(jnp.where(lane == head, blk, 0.0), axis=1, keepdims=True)


def _pick_row(blk, head):
    sub = lax.broadcasted_iota(jnp.int32, blk.shape, 0)
    return jnp.sum(jnp.where(sub == head, blk, 0.0), axis=0, keepdims=True)


def _att_tiles(tp):
    out, r0 = [], 0
    while r0 < tp:
        rows = min(ATT_TQ, tp - r0)
        out.append((r0, rows, r0 + rows))
        r0 += rows
    return out


def attn_fwd(qkv, c, ct, nh):
    tp = qkv.shape[0]
    npair = nh // 2
    tiles = _att_tiles(tp)

    def body(q_ref, k_ref, v_ref, c_ref, ct_ref, o_ref, lset_ref):
        p = pl.program_id(0)
        lset_ref[...] = jnp.zeros_like(lset_ref)
        for r0, nr, nk in tiles:
            rs = slice(r0, r0 + nr)
            causal = (r0 + lax.broadcasted_iota(jnp.int32, (nr, nk), 0)
                      >= lax.broadcasted_iota(jnp.int32, (nr, nk), 1))
            cblk = c_ref[rs, :]
            ctb = ct_ref[:, 0:nk]
            for hh in range(2):
                head = 2 * p + hh
                hs = slice(hh * HEAD_DIM, (hh + 1) * HEAD_DIM)
                q = q_ref[rs, hs] * ATT_SCALE
                s = _dot_nt(q, k_ref[0:nk, hs]) + (_pick_col(cblk, head) - _pick_row(ctb, head))
                s = jnp.where(causal, s, NEG_BIG)
                m = jnp.max(s, axis=1, keepdims=True)
                pm = jnp.exp(s - m)
                l = jnp.sum(pm, axis=1, keepdims=True)
                o_ref[rs, hs] = _dot(pm.astype(BF16), v_ref[0:nk, hs]) / l
                lse = m + jnp.log(l)
                lset_ref[hh:hh + 1, rs] = jnp.broadcast_to(lse, (nr, LANES)).T[0:1, :]

    pair = lambda p: (0, p)
    return pl.pallas_call(
        body, name="attn_fwd", grid=(npair,),
        in_specs=[pl.BlockSpec((tp, LANES), pair),
                  pl.BlockSpec((tp, LANES), lambda p: (0, npair + p)),
                  pl.BlockSpec((tp, LANES), lambda p: (0, 2 * npair + p)),
                  pl.BlockSpec((tp, LANES), lambda p: (0, 0)),
                  pl.BlockSpec((SUBLANES, tp), lambda p: (0, 0))],
        out_specs=[pl.BlockSpec((tp, LANES), pair),
                   pl.BlockSpec((None, SUBLANES, tp), lambda p: (p, 0, 0))],
        out_shape=[jax.ShapeDtypeStruct((tp, nh * HEAD_DIM), F32),
                   jax.ShapeDtypeStruct((npair, SUBLANES, tp), F32)],
        compiler_params=_cparams("parallel"),
    )(qkv, qkv, qkv, c, ct)


def _shift_down(x, k, n):
    if k == 0:
        return x
    rows = lax.broadcasted_iota(jnp.int32, x.shape, 0)
    return jnp.where(rows >= k, pltpu.roll(x, k, 0), 0.0)


def _shift_up(x, k, n):
    if k == 0:
        return x
    rows = lax.broadcasted_iota(jnp.int32, x.shape, 0)
    return jnp.where(rows < n - k, pltpu.roll(x, n - k, 0), 0.0)


def _conv_fwd(xr, cw_ref, cb_ref, n):
    xc = cw_ref[CONV_WIDTH - 1:CONV_WIDTH, :] * xr + cb_ref[...]
    for k in range(1, CONV_WIDTH):
        xc = xc + cw_ref[CONV_WIDTH - 1 - k:CONV_WIDTH - k, :] * _shift_down(xr, k, n)
    return xc


def _gates(xc, wga_ref, bga_ref, wgx_ref, bgx_ref, l_ref):
    xcb = xc.astype(BF16)
    r = _sigmoid(_dot(xcb, wga_ref[...]) + bga_ref[...])
    ig = _sigmoid(_dot(xcb, wgx_ref[...]) + bgx_ref[...])
    ls = _log_sigmoid(l_ref[...])
    log_a = RG_C * r * ls
    a = jnp.exp(log_a)
    mult = jnp.sqrt(-_expm1(2.0 * log_a))
    return xcb, r, ig, ls, log_a, a, mult


SCAN_UNROLL = 4


def _scan_rows(a_s, u_s, out_ref, n, reverse):
    nt = n // SUBLANES
    per = SCAN_UNROLL if nt % SCAN_UNROLL == 0 else 1
    row = lax.broadcasted_iota(jnp.int32, (SUBLANES, LANES), 0)
    last = 0 if reverse else SUBLANES - 1

    def tile_scan(a, u):
        for d in (1, 2, 4):
            if reverse:
                keep = row < SUBLANES - d
                sh = SUBLANES - d
            else:
                keep = row >= d
                sh = d
            a_sh = jnp.where(keep, pltpu.roll(a, sh, 0), 1.0)
            u_sh = jnp.where(keep, pltpu.roll(u, sh, 0), 0.0)
            u = a * u_sh + u
            a = a * a_sh
        return a, u

    def step(t, carry):
        tiles = []
        for k in range(per):
            tt = t * per + k
            if reverse:
                tt = nt - 1 - tt
            off = pl.multiple_of(tt * SUBLANES, SUBLANES)
            a, u = tile_scan(a_s[pl.ds(off, SUBLANES), :], u_s[pl.ds(off, SUBLANES), :])
            tiles.append((off, a, u))
        for off, a, u in tiles:
            out_ref[pl.ds(off, SUBLANES), :] = u + a * carry
            carry = u[last:last + 1, :] + a[last:last + 1, :] * carry
        return carry

    lax.fori_loop(0, nt // per, step, jnp.zeros((1, LANES), F32))


def rec_fwd(rest, convw, convb, wga, bga, wgx, bgx, lru, rw):
    tp = rest.shape[0]
    ng = rw // LANES

    def body(xr_ref, yr_ref, cw_ref, cb_ref, wga_ref, bga_ref, wgx_ref, bgx_ref, l_ref,
             rec_ref, hr_ref, xc_ref, a_s, u_s):
        xc = _conv_fwd(xr_ref[...], cw_ref, cb_ref, tp)
        xc_ref[...] = xc
        _, r, ig, ls, log_a, a, mult = _gates(xc, wga_ref, bga_ref, wgx_ref, bgx_ref, l_ref)
        a_s[...] = a
        u_s[...] = mult * ig * xc
        _scan_rows(a_s, u_s, hr_ref, tp, reverse=False)
        rec_ref[...] = hr_ref[...] * _gelu(yr_ref[...])

    col = lambda g: (0, g)
    vec = pl.BlockSpec((1, LANES), col)
    big = pl.BlockSpec((tp, LANES), col)
    return pl.pallas_call(
        body, name="rec_fwd", grid=(ng,),
        in_specs=[big, pl.BlockSpec((tp, LANES), lambda g: (0, ng + g)),
                  pl.BlockSpec((CONV_WIDTH, LANES), col), vec,
                  pl.BlockSpec((None, LANES, LANES), lambda g: (g, 0, 0)), vec,
                  pl.BlockSpec((None, LANES, LANES), lambda g: (g, 0, 0)), vec, vec],
        out_specs=[big, big, big],
        out_shape=[jax.ShapeDtypeStruct((tp, rw), F32)] * 3,
        scratch_shapes=[pltpu.VMEM((tp, LANES), F32), pltpu.VMEM((tp, LANES), F32)],
        compiler_params=_cparams("parallel"),
    )(rest, rest, convw, convb, wga, bga, wgx, bgx, lru)


def out_proj(h, o, rec, ga, gr, wout, g2, tm):
    tp, d = h.shape
    aw, rw = o.shape[1], rec.shape[1]

    def body(h_ref, o_ref, rec_ref, ga_ref, gr_ref, w_ref, g2_ref, h2_ref, mix_ref, z2_ref):
        mix_ref[:, 0:aw] = _rms_fwd(o_ref[...], ga_ref[...]).astype(BF16)
        mix_ref[:, aw:aw + rw] = _rms_fwd(rec_ref[...], gr_ref[...]).astype(BF16)
        h2 = h_ref[...] + _dot(mix_ref[...], w_ref[...])
        h2_ref[...] = h2
        z2_ref[...] = _rms_fwd(h2, g2_ref[...]).astype(BF16)

    row = lambda i: (i, 0)
    fix = lambda i: (0, 0)
    return pl.pallas_call(
        body, name="out_proj", grid=(tp // tm,),
        in_specs=[pl.BlockSpec((tm, d), row), pl.BlockSpec((tm, aw), row), pl.BlockSpec((tm, rw), row),
                  pl.BlockSpec((1, aw), fix), pl.BlockSpec((1, rw), fix),
                  pl.BlockSpec((d, d), fix), pl.BlockSpec((1, d), fix)],
        out_specs=[pl.BlockSpec((tm, d), row)] * 3,
        out_shape=[jax.ShapeDtypeStruct((tp, d), F32), jax.ShapeDtypeStruct((tp, d), BF16),
                   jax.ShapeDtypeStruct((tp, d), BF16)],
        compiler_params=_cparams("parallel"),
    )(h, o, rec, ga, gr, wout, g2)


MLP_BLOCKS = 2


def mlp_fwd(z2, h2, gup, gdown, tm):
    tp, d = h2.shape
    nf = gup.shape[0]
    tf = gup.shape[2]
    nb = MLP_BLOCKS if nf % MLP_BLOCKS == 0 else 1
    nj = nf // nb

    def body(z_ref, h_ref, wu_ref, wd_ref, u_ref, h3_ref, acc):
        j = pl.program_id(1)
        z = z_ref[...]
        part = None
        for b in range(nb):
            u = jnp.maximum(_dot(z, wu_ref[b]), 0.0)
            u_ref[:, b * tf:(b + 1) * tf] = u.astype(BF16)
            p = _dot((u * u).astype(BF16), wd_ref[b])
            part = p if part is None else part + p

        @pl.when(j == 0)
        def _():
            acc[...] = h_ref[...] + part

        @pl.when(j > 0)
        def _():
            acc[...] += part

        @pl.when(j == nj - 1)
        def _():
            h3_ref[...] = acc[...]

    return pl.pallas_call(
        body, name="mlp_fwd", grid=(tp // tm, nj),
        in_specs=[pl.BlockSpec((tm, d), lambda i, j: (i, 0)),
                  pl.BlockSpec((tm, d), lambda i, j: (i, 0)),
                  pl.BlockSpec((nb, d, tf), lambda i, j: (j, 0, 0)),
                  pl.BlockSpec((nb, tf, d), lambda i, j: (j, 0, 0))],
        out_specs=[pl.BlockSpec((tm, nb * tf), lambda i, j: (i, j)),
                   pl.BlockSpec((tm, d), lambda i, j: (i, 0))],
        out_shape=[jax.ShapeDtypeStruct((tp, nf * tf), BF16), jax.ShapeDtypeStruct((tp, d), F32)],
        scratch_shapes=[pltpu.VMEM((tm, d), F32)],
        compiler_params=_cparams("parallel", "arbitrary"),
    )(z2, h2, gup, gdown)


def loss_head(h, gf, tgt, t_real, tm):
    tp, d = h.shape

    def body(h_ref, g_ref, t_ref, dh_ref, dg_ref, loss_ref):
        i = pl.program_id(0)
        x = h_ref[...]
        g = g_ref[...]
        r = lax.rsqrt(jnp.mean(x * x, axis=-1, keepdims=True) + NORM_EPS)
        xn = x * r
        rows = i * tm + lax.broadcasted_iota(jnp.int32, (tm, 1), 0)
        valid = jnp.logical_and(rows >= N_META, rows < t_real)
        e = jnp.where(valid, xn * g - t_ref[...], 0.0)
        part = 0.5 * jnp.sum(jnp.sum(e * e, axis=1, keepdims=True) / d, axis=0, keepdims=True)
        dy = e / d
        dxn = dy * g
        dh_ref[...] = r * (dxn - xn * jnp.mean(dxn * xn, axis=-1, keepdims=True))
        _accumulate(dg_ref, jnp.sum(dy * xn, axis=0, keepdims=True), i == 0)
        _accumulate(loss_ref, jnp.broadcast_to(part, (1, LANES)), i == 0)

    row = lambda i: (i, 0)
    fix = lambda i: (0, 0)
    return pl.pallas_call(
        body, name="loss_head", grid=(tp // tm,),
        in_specs=[pl.BlockSpec((tm, d), row), pl.BlockSpec((1, d), fix), pl.BlockSpec((tm, d), row)],
        out_specs=[pl.BlockSpec((tm, d), row), pl.BlockSpec((1, d), fix), pl.BlockSpec((1, LANES), fix)],
        out_shape=[jax.ShapeDtypeStruct((tp, d), F32), jax.ShapeDtypeStruct((1, d), F32),
                   jax.ShapeDtypeStruct((1, LANES), F32)],
        compiler_params=_cparams("arbitrary"),
    )(h, gf, tgt)


def mlp_bwd(dh, u, h2, g2, gup, gdown, tm):
    tp, d = dh.shape
    nf = gup.shape[0]
    tf = gup.shape[2]
    nb = MLP_BLOCKS if nf % MLP_BLOCKS == 0 else 1
    nj = nf // nb
    ni = tp // tm

    def body(dh_ref, u_ref, h2_ref, g_ref, wu_ref, wd_ref, dup_ref, dh2_ref, dg_ref, dhb, acc):
        i = pl.program_id(0)
        j = pl.program_id(1)

        @pl.when(j == 0)
        def _():
            dhb[...] = dh_ref[...].astype(BF16)

        part = None
        for b in range(nb):
            cols = slice(b * tf, (b + 1) * tf)
            dup = (_dot_nt(dhb[...], wd_ref[b]) * (2.0 * u_ref[:, cols].astype(F32))).astype(BF16)
            dup_ref[:, cols] = dup
            p = _dot_nt(dup, wu_ref[b])
            part = p if part is None else part + p
        _accumulate(acc, part, j == 0)

        @pl.when(j == nj - 1)
        def _():
            dx, dg = _rms_bwd(h2_ref[...], g_ref[...], acc[...])
            dh2_ref[...] = dh_ref[...] + dx
            _accumulate(dg_ref, dg, i == 0)

    return pl.pallas_call(
        body, name="mlp_bwd", grid=(ni, nj),
        in_specs=[pl.BlockSpec((tm, d), lambda i, j: (i, 0)),
                  pl.BlockSpec((tm, nb * tf), lambda i, j: (i, j)),
                  pl.BlockSpec((tm, d), lambda i, j: (i, 0)),
                  pl.BlockSpec((1, d), lambda i, j: (0, 0)),
                  pl.BlockSpec((nb, d, tf), lambda i, j: (j, 0, 0)),
                  pl.BlockSpec((nb, tf, d), lambda i, j: (j, 0, 0))],
        out_specs=[pl.BlockSpec((tm, nb * tf), lambda i, j: (i, j)),
                   pl.BlockSpec((tm, d), lambda i, j: (i, 0)),
                   pl.BlockSpec((1, d), lambda i, j: (0, 0)),
                   pl.BlockSpec((tm, d), lambda i, j: (i, 0))],
        out_shape=[jax.ShapeDtypeStruct((tp, nf * tf), BF16), jax.ShapeDtypeStruct((tp, d), F32),
                   jax.ShapeDtypeStruct((1, d), F32), jax.ShapeDtypeStruct((tp, d), BF16)],
        scratch_shapes=[pltpu.VMEM((tm, d), F32)],
        compiler_params=_cparams("arbitrary", "arbitrary"),
    )(dh, u, h2, g2, gup, gdown)


def mm_tn(a, b, *, tk, tn, out_dtype, name, square_a=False, blocked_n=False):
    rows, kk = a.shape
    nn = b.shape[1]
    keep_at = nn // tn > 1

    def a_tile(a_ref):
        av = a_ref[...]
        if square_a:
            af = av.astype(F32)
            av = af * af
        return av.astype(BF16)

    def body(a_ref, b_ref, o_ref, *scratch):
        if keep_at:
            at, = scratch

            @pl.when(pl.program_id(1) == 0)
            def _():
                at[...] = a_tile(a_ref).T

            o_ref[...] = _dot(at[...], b_ref[...].astype(BF16)).astype(out_dtype)
        else:
            o_ref[...] = _dot_tn(a_tile(a_ref), b_ref[...].astype(BF16)).astype(out_dtype)

    if blocked_n:
        out_spec = pl.BlockSpec((None, tk, tn), lambda k, n: (n, k, 0))
        out_shape = jax.ShapeDtypeStruct((nn // tn, kk, tn), out_dtype)
    else:
        out_spec = pl.BlockSpec((tk, tn), lambda k, n: (k, n))
        out_shape = jax.ShapeDtypeStruct((kk, nn), out_dtype)
    return pl.pallas_call(
        body, name=name, grid=(kk // tk, nn // tn),
        in_specs=[pl.BlockSpec((rows, tk), lambda k, n: (0, k)),
                  pl.BlockSpec((rows, tn), lambda k, n: (0, n))],
        out_specs=out_spec, out_shape=out_shape,
        scratch_shapes=[pltpu.VMEM((tk, rows), BF16)] if keep_at else [],
        compiler_params=_cparams("parallel", "arbitrary"),
    )(a, b)


def dw_mlp(u, dhb, z2, dup, tf):
    rows, dff = u.shape
    d = z2.shape[1]
    nf = dff // tf

    def body(u_ref, dh_ref, z_ref, dup_ref, dwd_ref, dwu_ref, zt):
        @pl.when(pl.program_id(0) == 0)
        def _():
            zt[...] = z_ref[...].T

        uf = u_ref[...].astype(F32)
        dwd_ref[...] = _dot_tn((uf * uf).astype(BF16), dh_ref[...]).astype(BF16)
        dwu_ref[...] = _dot(zt[...], dup_ref[...]).astype(BF16)

    col = lambda j: (0, j)
    fix = lambda j: (0, 0)
    return pl.pallas_call(
        body, name="dw_mlp", grid=(nf,),
        in_specs=[pl.BlockSpec((rows, tf), col), pl.BlockSpec((rows, d), fix),
                  pl.BlockSpec((rows, d), fix), pl.BlockSpec((rows, tf), col)],
        out_specs=[pl.BlockSpec((None, tf, d), lambda j: (j, 0, 0)),
                   pl.BlockSpec((None, d, tf), lambda j: (j, 0, 0))],
        out_shape=[jax.ShapeDtypeStruct((nf, tf, d), BF16), jax.ShapeDtypeStruct((nf, d, tf), BF16)],
        scratch_shapes=[pltpu.VMEM((d, rows), BF16)],
        compiler_params=_cparams("arbitrary"),
    )(u, dhb, z2, dup)


def out_proj_bwd(dh2, o, rec, ga, gr, wout, tm):
    tp, d = dh2.shape
    aw, rw = o.shape[1], rec.shape[1]

    def body(dh_ref, o_ref, rec_ref, ga_ref, gr_ref, w_ref, do_ref, drec_ref, dga_ref, dgr_ref):
        i = pl.program_id(0)
        dmix = _dot_nt(dh_ref[...].astype(BF16), w_ref[...])
        do, dga = _rms_bwd(o_ref[...], ga_ref[...], dmix[:, 0:aw])
        drec, dgr = _rms_bwd(rec_ref[...], gr_ref[...], dmix[:, aw:aw + rw])
        do_ref[...] = do
        drec_ref[...] = drec
        _accumulate(dga_ref, dga, i == 0)
        _accumulate(dgr_ref, dgr, i == 0)

    row = lambda i: (i, 0)
    fix = lambda i: (0, 0)
    return pl.pallas_call(
        body, name="out_proj_bwd", grid=(tp // tm,),
        in_specs=[pl.BlockSpec((tm, d), row), pl.BlockSpec((tm, aw), row), pl.BlockSpec((tm, rw), row),
                  pl.BlockSpec((1, aw), fix), pl.BlockSpec((1, rw), fix), pl.BlockSpec((d, d), fix)],
        out_specs=[pl.BlockSpec((tm, aw), row), pl.BlockSpec((tm, rw), row),
                   pl.BlockSpec((1, aw), fix), pl.BlockSpec((1, rw), fix)],
        out_shape=[jax.ShapeDtypeStruct((tp, aw), F32), jax.ShapeDtypeStruct((tp, rw), F32),
                   jax.ShapeDtypeStruct((1, aw), F32), jax.ShapeDtypeStruct((1, rw), F32)],
        compiler_params=_cparams("arbitrary"),
    )(dh2, o, rec, ga, gr, wout)


def rec_bwd(drec, hr, xc, rest, convw, convb, wga, bga, wgx, bgx, lru, rw):
    tp = rest.shape[0]
    ng = rw // LANES

    def body(drec_ref, hr_ref, xc_ref, xr_ref, yr_ref, cw_ref, cb_ref, wga_ref, bga_ref, wgx_ref, bgx_ref, l_ref,
             dxr_ref, dyr_ref, dwga_ref, dwgx_ref, vec_ref, a_s, u_s, lam_s):
        xc = xc_ref[...]
        h = hr_ref[...]
        yr = yr_ref[...]
        drec = drec_ref[...]
        xcb, r, ig, ls, log_a, a, mult = _gates(xc, wga_ref, bga_ref, wgx_ref, bgx_ref, l_ref)
        dyr_ref[...] = (drec * h * _gelu_grad(yr)).astype(BF16)
        a_s[...] = _shift_up(a, 1, tp)
        u_s[...] = drec * _gelu(yr)
        _scan_rows(a_s, u_s, lam_s, tp, reverse=True)
        lam = lam_s[...]
        da = lam * _shift_down(h, 1, tp)
        dmult = lam * ig * xc
        dig = lam * mult * xc
        dxc = lam * mult * ig
        a2 = jnp.exp(2.0 * log_a)
        dlog_a = da * a - dmult * a2 / mult
        dr = dlog_a * (RG_C * ls)
        dl = jnp.sum(dlog_a * (RG_C * r), axis=0, keepdims=True) * _sigmoid(-l_ref[...])
        dpa = dr * r * (1.0 - r)
        dpx = dig * ig * (1.0 - ig)
        dpab = dpa.astype(BF16)
        dpxb = dpx.astype(BF16)
        dxc = dxc + _dot_nt(dpab, wga_ref[...]) + _dot_nt(dpxb, wgx_ref[...])
        dwga_ref[...] = _dot_tn(xcb, dpab)
        dwgx_ref[...] = _dot_tn(xcb, dpxb)
        xr = xr_ref[...]
        dxr = cw_ref[CONV_WIDTH - 1:CONV_WIDTH, :] * dxc
        for k in range(1, CONV_WIDTH):
            dxr = dxr + cw_ref[CONV_WIDTH - 1 - k:CONV_WIDTH - k, :] * _shift_up(dxc, k, tp)
        dxr_ref[...] = dxr.astype(BF16)
        for k in range(CONV_WIDTH):
            vec_ref[k:k + 1, :] = jnp.sum(dxc * _shift_down(xr, CONV_WIDTH - 1 - k, tp), axis=0, keepdims=True)
        vec_ref[4:5, :] = jnp.sum(dxc, axis=0, keepdims=True)
        vec_ref[5:6, :] = jnp.sum(dpa, axis=0, keepdims=True)
        vec_ref[6:7, :] = jnp.sum(dpx, axis=0, keepdims=True)
        vec_ref[7:8, :] = dl

    col = lambda g: (0, g)
    vec = pl.BlockSpec((1, LANES), col)
    big = pl.BlockSpec((tp, LANES), col)
    sq = pl.BlockSpec((None, LANES, LANES), lambda g: (g, 0, 0))
    return pl.pallas_call(
        body, name="rec_bwd", grid=(ng,),
        in_specs=[big, big, big, big, pl.BlockSpec((tp, LANES), lambda g: (0, ng + g)),
                  pl.BlockSpec((CONV_WIDTH, LANES), col), vec, sq, vec, sq, vec, vec],
        out_specs=[big, big, sq, sq, pl.BlockSpec((None, SUBLANES, LANES), lambda g: (g, 0, 0))],
        out_shape=[jax.ShapeDtypeStruct((tp, rw), BF16), jax.ShapeDtypeStruct((tp, rw), BF16),
                   jax.ShapeDtypeStruct((ng, LANES, LANES), F32), jax.ShapeDtypeStruct((ng, LANES, LANES), F32),
                   jax.ShapeDtypeStruct((ng, SUBLANES, LANES), F32)],
        scratch_shapes=[pltpu.VMEM((tp, LANES), F32)] * 3,
        compiler_params=_cparams("parallel"),
    )(drec, hr, xc, rest, rest, convw, convb, wga, bga, wgx, bgx, lru)


def attn_bwd(qkv, do, o, lset, c, ct, nh):
    tp = qkv.shape[0]
    npair = nh // 2
    aw = nh * HEAD_DIM
    tiles = _att_tiles(tp)

    def body(q_ref, k_ref, v_ref, do_ref, o_ref, lset_ref, c_ref, ct_ref,
             dq_ref, dk_ref, dv_ref, drow_ref, dcol_ref, dk_acc, dv_acc, dq_t):
        p = pl.program_id(0)
        k_t = k_ref[...].T
        dk_acc[...] = jnp.zeros_like(dk_acc)
        dv_acc[...] = jnp.zeros_like(dv_acc)
        dcol_ref[...] = jnp.zeros_like(dcol_ref)
        drow_ref[...] = jnp.zeros_like(drow_ref)
        for r0, nr, nk in tiles:
            rs = slice(r0, r0 + nr)
            causal = (r0 + lax.broadcasted_iota(jnp.int32, (nk, nr), 1)
                      >= lax.broadcasted_iota(jnp.int32, (nk, nr), 0))
            cblk = c_ref[0:nk, :]
            ctb = ct_ref[:, rs]
            for hh in range(2):
                head = 2 * p + hh
                hs = slice(hh * HEAD_DIM, (hh + 1) * HEAD_DIM)
                q = q_ref[rs, hs]
                k = k_ref[0:nk, hs]
                dof = do_ref[rs, hs]
                do16 = dof.astype(BF16)
                delta = jnp.sum(dof * o_ref[rs, hs], axis=1, keepdims=True)
                delta_row = jnp.broadcast_to(delta, (nr, LANES)).T[0:1, :]
                s_t = _dot_nt(k, q * ATT_SCALE) + (_pick_row(ctb, head) - _pick_col(cblk, head))
                p_t = jnp.where(causal, jnp.exp(s_t - lset_ref[hh:hh + 1, rs]), 0.0)
                ds_t = p_t * (_dot_nt(v_ref[0:nk, hs], do16) - delta_row)
                p16 = p_t.astype(BF16)
                ds16 = ds_t.astype(BF16)
                dv_acc[0:nk, hs] += _dot(p16, do16)
                dk_acc[0:nk, hs] += _dot(ds16, q) * ATT_SCALE
                dq_t[hs, rs] = _dot(k_t[hs, 0:nk], ds16)
                drow_ref[hh:hh + 1, rs] = jnp.sum(ds_t, axis=0, keepdims=True)
                dcol_ref[0:nk, hs] -= jnp.broadcast_to(jnp.sum(ds_t, axis=1, keepdims=True), (nk, HEAD_DIM))
        dk_ref[...] = dk_acc[...].astype(BF16)
        dv_ref[...] = dv_acc[...].astype(BF16)
        dq_ref[...] = (dq_t[...].T * ATT_SCALE).astype(BF16)

    pair = lambda p: (0, p)
    return pl.pallas_call(
        body, name="attn_bwd", grid=(npair,),
        in_specs=[pl.BlockSpec((tp, LANES), pair),
                  pl.BlockSpec((tp, LANES), lambda p: (0, npair + p)),
                  pl.BlockSpec((tp, LANES), lambda p: (0, 2 * npair + p)),
                  pl.BlockSpec((tp, LANES), pair),
                  pl.BlockSpec((tp, LANES), pair),
                  pl.BlockSpec((None, SUBLANES, tp), lambda p: (p, 0, 0)),
                  pl.BlockSpec((tp, LANES), lambda p: (0, 0)),
                  pl.BlockSpec((SUBLANES, tp), lambda p: (0, 0))],
        out_specs=[pl.BlockSpec((tp, LANES), pair), pl.BlockSpec((tp, LANES), pair),
                   pl.BlockSpec((tp, LANES), pair),
                   pl.BlockSpec((None, SUBLANES, tp), lambda p: (p, 0, 0)),
                   pl.BlockSpec((tp, LANES), pair)],
        out_shape=[jax.ShapeDtypeStruct((tp, aw), BF16), jax.ShapeDtypeStruct((tp, aw), BF16),
                   jax.ShapeDtypeStruct((tp, aw), BF16),
                   jax.ShapeDtypeStruct((npair, SUBLANES, tp), F32),
                   jax.ShapeDtypeStruct((tp, aw), F32)],
        scratch_shapes=[pltpu.VMEM((tp, LANES), F32), pltpu.VMEM((tp, LANES), F32),
                        pltpu.VMEM((LANES, tp), F32)],
        compiler_params=_cparams("parallel"),
    )(qkv, qkv, qkv, do, o, lset, c, ct)


def fgate_bwd(dct8, drs, rest, bf_pad, fcol):
    tp = rest.shape[0]
    aw = drs.shape[1]
    nb = tp // ATT_BLOCK
    B = ATT_BLOCK

    def body(d_ref, drs_ref, f_ref, b_ref, dfl_ref, db_ref, pad_s):
        r_i = lax.broadcasted_iota(jnp.int32, (B, B), 0)
        c_i = lax.broadcasted_iota(jnp.int32, (B, B), 1)
        triu = (c_i >= r_i).astype(BF16)
        sel = (lax.broadcasted_iota(jnp.int32, (aw, LANES), 0)
               == HEAD_DIM * lax.broadcasted_iota(jnp.int32, (aw, LANES), 1)).astype(BF16)
        carry = jnp.zeros((1, LANES), F32)
        db = jnp.zeros((1, LANES), F32)
        pad_s[...] = jnp.zeros_like(pad_s)
        for i in range(nb - 1, -1, -1):
            sl = slice(i * B, (i + 1) * B)
            pad_s[0:SUBLANES, :] = d_ref[:, sl]
            dc = pad_s[...].T + _dot_split3(drs_ref[sl, :], sel)
            rc = _split3_dot(triu, dc)
            dlf = rc + carry
            carry = carry + rc[0:1, :]
            dfl = dlf * _sigmoid(-(f_ref[sl, :] + b_ref[...]))
            dfl_ref[sl, :] = dfl.astype(BF16)
            db = db + jnp.sum(dfl, axis=0, keepdims=True)
        db_ref[...] = db

    return pl.pallas_call(
        body, name="fgate_bwd", grid=(1,),
        in_specs=[pl.BlockSpec((SUBLANES, tp), lambda i: (0, 0)),
                  pl.BlockSpec((tp, aw), lambda i: (0, 0)),
                  pl.BlockSpec((tp, LANES), lambda i: (0, fcol)),
                  pl.BlockSpec((1, LANES), lambda i: (0, 0))],
        out_specs=[pl.BlockSpec((tp, LANES), lambda i: (0, 0)),
                   pl.BlockSpec((1, LANES), lambda i: (0, 0))],
        out_shape=[jax.ShapeDtypeStruct((tp, LANES), BF16), jax.ShapeDtypeStruct((1, LANES), F32)],
        scratch_shapes=[pltpu.VMEM((B, B), F32)],
        compiler_params=_cparams("arbitrary"),
    )(dct8, drs, rest, bf_pad)


def in_proj_bwd(dh2, parts, w_in_t, wrest_t, h, g1, tm):
    tp, d = h.shape
    dq, dk, dv, dxr, dyr, dfl = parts
    aw, rw = dq.shape[1], dxr.shape[1]

    def body(dh2_ref, dq_ref, dk_ref, dv_ref, dxr_ref, dyr_ref, dfl_ref, wq_ref, wr_ref, h_ref, g_ref,
             dh_ref, dg_ref):
        i = pl.program_id(0)
        dz = _dot(dq_ref[...], wq_ref[0:aw, :])
        dz += _dot(dk_ref[...], wq_ref[aw:2 * aw, :])
        dz += _dot(dv_ref[...], wq_ref[2 * aw:3 * aw, :])
        dz += _dot(dxr_ref[...], wr_ref[0:rw, :])
        dz += _dot(dyr_ref[...], wr_ref[rw:2 * rw, :])
        dz += _dot(dfl_ref[...], wr_ref[2 * rw:2 * rw + LANES, :])
        dx, dg = _rms_bwd(h_ref[...], g_ref[...], dz)
        dh_ref[...] = dh2_ref[...] + dx
        _accumulate(dg_ref, dg, i == 0)

    row = lambda i: (i, 0)
    fix = lambda i: (0, 0)
    return pl.pallas_call(
        body, name="in_proj_bwd", grid=(tp // tm,),
        in_specs=[pl.BlockSpec((tm, d), row),
                  pl.BlockSpec((tm, aw), row), pl.BlockSpec((tm, aw), row), pl.BlockSpec((tm, aw), row),
                  pl.BlockSpec((tm, rw), row), pl.BlockSpec((tm, rw), row), pl.BlockSpec((tm, LANES), row),
                  pl.BlockSpec((3 * aw, d), fix), pl.BlockSpec(wrest_t.shape, fix),
                  pl.BlockSpec((tm, d), row), pl.BlockSpec((1, d), fix)],
        out_specs=[pl.BlockSpec((tm, d), row), pl.BlockSpec((1, d), fix)],
        out_shape=[jax.ShapeDtypeStruct((tp, d), F32), jax.ShapeDtypeStruct((1, d), F32)],
        compiler_params=_cparams("arbitrary"),
    )(dh2, dq, dk, dv, dxr, dyr, dfl, w_in_t, wrest_t, h, g1)


def dw_in_t(z, parts, nh, tr):
    tp, d = z.shape
    dq, dk, dv, dxr, dyr, dfl = parts
    aw, rw = dq.shape[1], dxr.shape[1]
    d_in = 3 * aw + nh + 2 * rw
    nr = tp // tr
    offs = [(0, aw), (aw, aw), (2 * aw, aw), (3 * aw + nh, rw), (3 * aw + nh + rw, rw)]

    def body(z_ref, dq_ref, dk_ref, dv_ref, dxr_ref, dyr_ref, dfl_ref, o_ref, acc):
        r = pl.program_id(0)

        @pl.when(r == 0)
        def _():
            acc[...] = jnp.zeros_like(acc)

        zt = z_ref[...]
        for (o, n), ref in zip(offs, (dq_ref, dk_ref, dv_ref, dxr_ref, dyr_ref)):
            acc[o:o + n, :] += _dot_tn(ref[...], zt)
        acc[3 * aw:3 * aw + nh, :] += _dot_tn(dfl_ref[...], zt)[0:nh, :]

        @pl.when(r == nr - 1)
        def _():
            o_ref[...] = acc[...].astype(BF16)

    row = lambda r: (r, 0)
    return pl.pallas_call(
        body, name="dw_in", grid=(nr,),
        in_specs=[pl.BlockSpec((tr, d), row),
                  pl.BlockSpec((tr, aw), row), pl.BlockSpec((tr, aw), row), pl.BlockSpec((tr, aw), row),
                  pl.BlockSpec((tr, rw), row), pl.BlockSpec((tr, rw), row), pl.BlockSpec((tr, LANES), row)],
        out_specs=pl.BlockSpec((d_in, d), lambda r: (0, 0)),
        out_shape=jax.ShapeDtypeStruct((d_in, d), BF16),
        scratch_shapes=[pltpu.VMEM((d_in, d), F32)],
        compiler_params=_cparams("arbitrary"),
    )(z, dq, dk, dv, dxr, dyr, dfl)


def _place():
    return lax.axis_index("x"), lax.axis_index("y"), lax.axis_index("c")


HBM = pl.BlockSpec(memory_space=pltpu.HBM)
SEM = pl.BlockSpec(memory_space=pltpu.SEMAPHORE)
EFFECT = pltpu.SideEffectType.DATAFLOW_SIDE_EFFECTING


def _in_hbm(a):
    return pltpu.with_memory_space_constraint(a, pltpu.HBM)


def _as_list(a):
    return list(a) if isinstance(a, (list, tuple)) else [a]


def _gather_targets(x, y, c):
    return [(x, y, 1 - c), (1 - x, y, c), (x, 1 - y, c), (1 - x, 1 - y, c)]


def _slot(t):
    return 4 * t[0] + 2 * t[1] + t[2]


def gather_start(groups, name):
    flat = [a for g in groups for a in g]
    n = len(flat)
    ng = len(groups)
    lands = [lax.empty((N_DEV,) + a.shape, a.dtype) for a in flat]

    def body(*refs):
        src, land = refs[:n], refs[n:2 * n]
        sems = refs[2 * n:2 * n + 2 * ng]
        token = refs[-1]
        x, y, c = _place()
        me = 4 * x + 2 * y + c
        i = 0
        for gi, g in enumerate(groups):
            for a in range(len(g)):
                for k, t in enumerate(_gather_targets(x, y, c)):
                    pltpu.make_async_remote_copy(
                        src_ref=src[i], dst_ref=land[i].at[me],
                        send_sem=sems[2 * gi].at[4 * a + k], recv_sem=sems[2 * gi + 1].at[4 * a + k],
                        device_id=t, device_id_type=MESH).start()
                i += 1
        token[...] = jnp.zeros_like(token)

    sem_shapes = []
    for g in groups:
        sem_shapes += [pltpu.SemaphoreType.DMA((4 * len(g),)), pltpu.SemaphoreType.DMA((4 * len(g),))]
    out = pl.pallas_call(
        body, name=name,
        out_shape=sem_shapes + [pltpu.HBM(a.shape, a.dtype) for a in flat + lands]
        + [jax.ShapeDtypeStruct((SUBLANES, LANES), F32)],
        in_specs=[HBM] * (2 * n),
        out_specs=[SEM] * (2 * ng) + [HBM] * (2 * n) + [pl.BlockSpec(memory_space=pltpu.VMEM)],
        input_output_aliases={i: 2 * ng + i for i in range(2 * n)},
        compiler_params=pltpu.CompilerParams(has_side_effects=EFFECT),
    )(*[_in_hbm(a) for a in flat + lands])
    sems = out[:2 * ng]
    thru = out[2 * ng:2 * ng + 2 * n]
    srcs_t, lands_t = thru[:n], thru[n:]
    res, i = [], 0
    for gi, g in enumerate(groups):
        res.append((sems[2 * gi], sems[2 * gi + 1], srcs_t[i:i + len(g)], lands_t[i:i + len(g)]))
        i += len(g)
    return res, out[-1]


def gather_wait(send, recv, srcs, lands, after, name):
    n = len(srcs)

    def body(*refs):
        src, land = refs[:n], refs[n:2 * n]
        send_sem, recv_sem = refs[2 * n], refs[2 * n + 1]
        x, y, c = _place()
        for a in range(n):
            for k, t in enumerate(_gather_targets(x, y, c)):
                cp = pltpu.make_async_remote_copy(
                    src_ref=src[a], dst_ref=land[a].at[_slot(t)],
                    send_sem=send_sem.at[4 * a + k], recv_sem=recv_sem.at[4 * a + k],
                    device_id=t, device_id_type=MESH)
                cp.wait_send()
                cp.wait_recv()

    out = pl.pallas_call(
        body, name=name,
        out_shape=[pltpu.HBM(a.shape, a.dtype) for a in list(srcs) + list(lands)],
        in_specs=[HBM] * (2 * n) + [SEM, SEM] + [ANY] * len(_as_list(after)),
        out_specs=[HBM] * (2 * n),
        input_output_aliases={i: i for i in range(2 * n)},
        compiler_params=pltpu.CompilerParams(has_side_effects=EFFECT),
    )(*srcs, *lands, send, recv, *_as_list(after))
    return out[:n], out[n:]


def forward_start(lands, name):
    n = len(lands)

    def body(*refs):
        land = refs[:n]
        send_sem, recv_sem = refs[n], refs[n + 1]
        token = refs[-1]
        x, y, c = _place()
        for a in range(n):
            for j, chip in enumerate([(1 - x, y), (x, 1 - y), (1 - x, 1 - y)]):
                blk = land[a].at[_slot((*chip, c))]
                pltpu.make_async_remote_copy(src_ref=blk, dst_ref=blk, send_sem=send_sem.at[3 * a + j],
                                             recv_sem=recv_sem.at[3 * a + j], device_id=(x, y, 1 - c),
                                             device_id_type=MESH).start()
        token[...] = jnp.zeros_like(token)

    out = pl.pallas_call(
        body, name=name,
        out_shape=[pltpu.SemaphoreType.DMA((3 * n,)), pltpu.SemaphoreType.DMA((3 * n,))]
        + [pltpu.HBM(a.shape, a.dtype) for a in lands] + [jax.ShapeDtypeStruct((SUBLANES, LANES), F32)],
        in_specs=[HBM] * n,
        out_specs=[SEM, SEM] + [HBM] * n + [pl.BlockSpec(memory_space=pltpu.VMEM)],
        input_output_aliases={i: 2 + i for i in range(n)},
        compiler_params=pltpu.CompilerParams(has_side_effects=EFFECT),
    )(*[_in_hbm(a) for a in lands])
    return out[0], out[1], out[2:2 + n], out[-1][0, 0]


def forward_wait(send, recv, lands, after, name):
    n = len(lands)

    def body(*refs):
        land = refs[:n]
        send_sem, recv_sem = refs[n], refs[n + 1]
        x, y, c = _place()
        for a in range(n):
            for j, chip in enumerate([(1 - x, y), (x, 1 - y), (1 - x, 1 - y)]):
                cp = pltpu.make_async_remote_copy(
                    src_ref=land[a].at[_slot((*chip, c))], dst_ref=land[a].at[_slot((*chip, 1 - c))],
                    send_sem=send_sem.at[3 * a + j], recv_sem=recv_sem.at[3 * a + j],
                    device_id=(x, y, 1 - c), device_id_type=MESH)
                cp.wait_send()
                cp.wait_recv()

    return pl.pallas_call(
        body, name=name,
        out_shape=[pltpu.HBM(a.shape, a.dtype) for a in lands],
        in_specs=[HBM] * n + [SEM, SEM, ANY],
        out_specs=[HBM] * n,
        input_output_aliases={i: i for i in range(n)},
        compiler_params=pltpu.CompilerParams(has_side_effects=EFFECT),
    )(*lands, send, recv, after)


def _relations():
    return [(dx, dy, dc) for dx in (0, 1) for dy in (0, 1) for dc in (0, 1) if dx + dy + dc]


def _peer(x, y, c, rel):
    return ((1 - x) if rel[0] else x, (1 - y) if rel[1] else y, (1 - c) if rel[2] else c)


def exchange_start(srcs, lands, layer, name):
    n = len(srcs)

    def body(*refs):
        src, land = refs[:n], refs[n:2 * n]
        send_sem, recv_sem = refs[2 * n], refs[2 * n + 1]
        token = refs[-1]
        x, y, c = _place()
        me = 4 * x + 2 * y + c
        for k, rel in enumerate(_relations()):
            peer = _peer(x, y, c, rel)
            for a in range(n):
                pltpu.make_async_remote_copy(
                    src_ref=src[a] if layer is None else src[a].at[_slot(peer)],
                    dst_ref=land[a].at[me] if layer is None else land[a].at[me, layer],
                    send_sem=send_sem.at[7 * a + k], recv_sem=recv_sem.at[7 * a + k],
                    device_id=peer, device_id_type=MESH).start()
        token[...] = jnp.zeros_like(token)

    out = pl.pallas_call(
        body, name=name,
        out_shape=[pltpu.SemaphoreType.DMA((7 * n,)), pltpu.SemaphoreType.DMA((7 * n,))]
        + [pltpu.HBM(a.shape, a.dtype) for a in list(srcs) + list(lands)]
        + [jax.ShapeDtypeStruct((SUBLANES, LANES), F32)],
        in_specs=[HBM] * (2 * n),
        out_specs=[SEM, SEM] + [HBM] * (2 * n) + [pl.BlockSpec(memory_space=pltpu.VMEM)],
        input_output_aliases={i: 2 + i for i in range(2 * n)},
        compiler_params=pltpu.CompilerParams(has_side_effects=EFFECT),
    )(*[_in_hbm(a) for a in list(srcs) + list(lands)])
    return out[0], out[1], out[2:2 + n], out[2 + n:2 + 2 * n], out[-1][0, 0]


def exchange_wait(send, recv, srcs, lands, after, layer, name):
    n = len(srcs)

    def body(*refs):
        src, land = refs[:n], refs[n:2 * n]
        send_sem, recv_sem = refs[2 * n], refs[2 * n + 1]
        x, y, c = _place()
        for k, rel in enumerate(_relations()):
            peer = _peer(x, y, c, rel)
            for a in range(n):
                cp = pltpu.make_async_remote_copy(
                    src_ref=src[a] if layer is None else src[a].at[_slot(peer)],
                    dst_ref=land[a].at[_slot(peer)] if layer is None else land[a].at[_slot(peer), layer],
                    send_sem=send_sem.at[7 * a + k], recv_sem=recv_sem.at[7 * a + k],
                    device_id=peer, device_id_type=MESH)
                cp.wait_send()
                cp.wait_recv()

    out = pl.pallas_call(
        body, name=name,
        out_shape=[pltpu.HBM(a.shape, a.dtype) for a in list(srcs) + list(lands)],
        in_specs=[HBM] * (2 * n) + [SEM, SEM] + [ANY] * len(_as_list(after)),
        out_specs=[HBM] * (2 * n),
        input_output_aliases={i: i for i in range(2 * n)},
        compiler_params=pltpu.CompilerParams(has_side_effects=EFFECT),
    )(*srcs, *lands, send, recv, *_as_list(after))
    return out[:n], out[n:]


def _adamw_math(g, w, m, v):
    m = ADAM_B1 * m + (1.0 - ADAM_B1) * g
    v = ADAM_B2 * v + (1.0 - ADAM_B2) * (g * g)
    m_hat = m / (1.0 - ADAM_B1 ** ADAM_STEP)
    v_hat = v / (1.0 - ADAM_B2 ** ADAM_STEP)
    delta = -ADAM_LR * (m_hat / (jnp.sqrt(v_hat) + ADAM_EPS) + ADAM_WD * w)
    return delta, m, v


def sum_adamw(parts, w, m, v, tr, name):
    npart, rows, cols = parts.shape

    def body(p_ref, w_ref, m_ref, v_ref, g_ref, d_ref, nm_ref, nv_ref):
        g = p_ref[0].astype(F32)
        for p in range(1, npart):
            g = g + p_ref[p].astype(F32)
        delta, nm, nv = _adamw_math(g, w_ref[...], m_ref[...], v_ref[...])
        g_ref[...] = g
        d_ref[...] = delta
        nm_ref[...] = nm
        nv_ref[...] = nv

    blk = pl.BlockSpec((tr, cols), lambda i: (i, 0))
    return pl.pallas_call(
        body, name=name, grid=(rows // tr,),
        in_specs=[pl.BlockSpec((npart, tr, cols), lambda i: (0, i, 0)), blk, blk, blk],
        out_specs=[blk] * 4,
        out_shape=[jax.ShapeDtypeStruct((rows, cols), F32)] * 4,
        compiler_params=_cparams("parallel"),
    )(parts, w, m, v)


def sum_adamw_t(parts, w, m, v, name):
    npart, nl, rows, cols = parts.shape

    def body(p_ref, w_ref, m_ref, v_ref, g_ref, d_ref, nm_ref, nv_ref):
        g = p_ref[0].astype(F32)
        for p in range(1, npart):
            g = g + p_ref[p].astype(F32)
        delta, nm, nv = _adamw_math(g, w_ref[...], m_ref[...], v_ref[...])
        g_ref[...] = g
        d_ref[...] = delta
        nm_ref[...] = nm
        nv_ref[...] = nv

    blk = pl.BlockSpec((None, rows, cols), lambda l: (l, 0, 0))
    return pl.pallas_call(
        body, name=name, grid=(nl,),
        in_specs=[pl.BlockSpec((npart, None, rows, cols), lambda l: (0, l, 0, 0)), blk, blk, blk],
        out_specs=[blk] * 4,
        out_shape=[jax.ShapeDtypeStruct((nl, rows, cols), F32)] * 4,
        compiler_params=_cparams("parallel"),
    )(parts, w, m, v)


def adamw_group(gs, ws, ms, vs, name):
    n = len(gs)

    def body(*refs):
        g, w, m, v, outs = refs[:n], refs[n:2 * n], refs[2 * n:3 * n], refs[3 * n:4 * n], refs[4 * n:]
        for i in range(n):
            delta, nm, nv = _adamw_math(g[i][...], w[i][...], m[i][...], v[i][...])
            outs[i][...] = delta
            outs[n + i][...] = nm
            outs[2 * n + i][...] = nv

    vmem = pl.BlockSpec(memory_space=pltpu.VMEM)
    out = pl.pallas_call(
        body, name=name,
        in_specs=[vmem] * (4 * n), out_specs=[vmem] * (3 * n),
        out_shape=[jax.ShapeDtypeStruct(a.shape, F32) for a in list(ws) * 3],
        compiler_params=_cparams(),
    )(*gs, *ws, *ms, *vs)
    return out[:n], out[n:2 * n], out[2 * n:]


def sum_parts(parts, name):
    npart, rows, cols = parts.shape

    def body(p_ref, g_ref):
        g = p_ref[0].astype(F32)
        for p in range(1, npart):
            g = g + p_ref[p].astype(F32)
        g_ref[...] = g

    return pl.pallas_call(
        body, name=name, grid=(1,),
        in_specs=[pl.BlockSpec((npart, rows, cols), lambda i: (0, 0, 0))],
        out_specs=pl.BlockSpec((rows, cols), lambda i: (0, 0)),
        out_shape=jax.ShapeDtypeStruct((rows, cols), F32),
        compiler_params=_cparams("arbitrary"),
    )(parts)


def _round_up(n, m):
    return (n + m - 1) // m * m


def _block_diag_pairs(w):
    nb, b, _ = w.shape
    per = LANES // b
    ng = nb // per
    w = w.reshape(ng, per, b, b)
    eye = jnp.eye(per, dtype=w.dtype)
    out = jnp.einsum('gpij,pq->gpiqj', w, eye).reshape(ng, LANES, LANES)
    return out.astype(BF16)


def _block_diag_extract(g, b):
    ng = g.shape[0]
    per = LANES // b
    g = g.reshape(ng, per, b, per, b)
    idx = jnp.arange(per)
    return g[:, idx, :, idx, :].transpose(1, 0, 2, 3).reshape(ng * per, b, b)


def _tiles(v):
    v = v.reshape(-1)
    n = _round_up(v.shape[0], SUBLANES * LANES)
    return jnp.pad(v, (0, n - v.shape[0])).reshape(-1, LANES)


SMALL = ['attn_norm_g', 'b_f', 'conv_w', 'conv_b', 'w_gate_a', 'b_gate_a', 'w_gate_x', 'b_gate_x',
         'lru_L', 'attn_out_g', 'rec_out_g', 'mlp_norm_g', 'final_g', 'meta']


def _pack(d):
    return jnp.concatenate([_tiles(d[n]) for n in SMALL], axis=0)


def _unpack(vec, shapes):
    out, r = {}, 0
    for n in SMALL:
        size = math.prod(shapes[n])
        nr = _round_up(size, SUBLANES * LANES) // LANES
        out[n] = vec[r:r + nr].reshape(-1)[:size].reshape(shapes[n])
        r += nr
    return out


def _row_tile(tp):
    return tp // 4 if (tp // 4) % 16 == 0 else tp


def local_step(x, tgt, meta, small, hooks):
    s, d = x.shape
    t_real = s + N_META
    tp = _round_up(t_real, ATT_BLOCK)
    depth = small['attn_norm_g'].shape[0]
    nh = small['b_f'].shape[1]
    rw = small['conv_b'].shape[1]
    blk = small['w_gate_a'].shape[2]
    tm = _row_tile(tp)
    tm2 = tp // 2
    fcol = 2 * rw // LANES

    h = jnp.concatenate([meta, x, jnp.zeros((tp - t_real, d), F32)], axis=0)
    tgt_p = jnp.pad(tgt, ((N_META, tp - t_real), (0, 0)))
    row = lambda v: v.reshape(1, -1)
    bf_pad = jnp.pad(small['b_f'], ((0, 0), (0, LANES - nh)))

    saved = []
    for l in range(depth):
        w_in_t, wrest_t, wout, tok_w = hooks.mixer_weights(l, h)
        wga = _block_diag_pairs(small['w_gate_a'][l])
        wgx = _block_diag_pairs(small['w_gate_x'][l])
        z, qkv, rest = in_proj(h, row(small['attn_norm_g'][l]) + tok_w, w_in_t, wrest_t, 3 * nh * HEAD_DIM, tm)
        c, ct = fgate_fwd(rest, bf_pad[l:l + 1], fcol)
        o, lset = attn_fwd(qkv, c, ct, nh)
        rec, hr, xc = rec_fwd(rest, small['conv_w'][l], row(small['conv_b'][l]), wga, row(small['b_gate_a'][l]),
                              wgx, row(small['b_gate_x'][l]), row(small['lru_L'][l]), rw)
        gup, gdown, tok_w = hooks.mlp_weights(l, rec)
        h2, mix, z2 = out_proj(h, o, rec, row(small['attn_out_g'][l]), row(small['rec_out_g'][l]), wout,
                               row(small['mlp_norm_g'][l]) + tok_w, tm)
        u, h3 = mlp_fwd(z2, h2, gup, gdown, tm2)
        saved.append(dict(h=h, z=z, qkv=qkv, rest=rest, c=c, ct=ct, o=o, lset=lset, rec=rec, hr=hr, xc=xc,
                          h2=h2, mix=mix, z2=z2, u=u, wga=wga, wgx=wgx,
                          w_in_t=w_in_t, wrest_t=wrest_t, wout=wout, gup=gup, gdown=gdown))
        h = h3

    dh, dgf, loss = loss_head(h, row(small['final_g']), tgt_p, t_real, tm)

    gs = {n: [None] * depth for n in SMALL if n not in ('final_g', 'meta')}
    tok = jnp.zeros((), F32)
    for l in reversed(range(depth)):
        sv = saved[l]
        gup, gdown = sv['gup'], sv['gdown']
        tf = gup.shape[2]
        dup, dh2, dg2, dhb = mlp_bwd(dh, sv['u'], sv['h2'], row(small['mlp_norm_g'][l]) + tok, gup, gdown, tm)
        gs['mlp_norm_g'][l] = dg2[0]
        do, drec, dga, dgr = out_proj_bwd(dh2, sv['o'], sv['rec'], row(small['attn_out_g'][l]),
                                          row(small['rec_out_g'][l]), sv['wout'], tm)
        gs['attn_out_g'][l] = dga[0]
        gs['rec_out_g'][l] = dgr[0]
        dw_down, dw_up = dw_mlp(sv['u'], dhb, sv['z2'], dup, tf)
        blocks = dict(
            w_down=dw_down, w_up=dw_up,
            w_out=mm_tn(sv['mix'], dh2, tk=d, tn=d // 2, out_dtype=BF16,
                        name="dw_out").reshape(N_DEV, d // N_DEV, d))
        tok = hooks.grads_ready(l, 'mlp', blocks)
        dxr, dyr, dwga, dwgx, vec = rec_bwd(drec, sv['hr'], sv['xc'], sv['rest'], small['conv_w'][l],
                                            row(small['conv_b'][l]) + tok, sv['wga'], row(small['b_gate_a'][l]),
                                            sv['wgx'], row(small['b_gate_x'][l]), row(small['lru_L'][l]), rw)
        gs['w_gate_a'][l] = _block_diag_extract(dwga, blk)
        gs['w_gate_x'][l] = _block_diag_extract(dwgx, blk)
        vec = vec.transpose(1, 0, 2).reshape(SUBLANES, rw)
        gs['conv_w'][l] = vec[0:CONV_WIDTH]
        gs['conv_b'][l] = vec[4]
        gs['b_gate_a'][l] = vec[5]
        gs['b_gate_x'][l] = vec[6]
        gs['lru_L'][l] = vec[7]
        dq, dk, dv, drow, dcol = attn_bwd(sv['qkv'], do, sv['o'], sv['lset'], sv['c'], sv['ct'] + tok, nh)
        drow8 = drow[:, 0:2, :].reshape(nh, tp)
        if nh < SUBLANES:
            drow8 = jnp.pad(drow8, ((0, SUBLANES - nh), (0, 0)))
        dfl, dbf = fgate_bwd(drow8, dcol, sv['rest'], bf_pad[l:l + 1], fcol)
        gs['b_f'][l] = dbf[0, 0:nh]
        parts = (dq, dk, dv, dxr, dyr, dfl)
        dh, dg1 = in_proj_bwd(dh2, parts, sv['w_in_t'], sv['wrest_t'], sv['h'], row(small['attn_norm_g'][l]), tm)
        gs['attn_norm_g'][l] = dg1[0]
        tok = jnp.zeros((), F32)
        if l == 0:
            grads = {n: jnp.stack(v) for n, v in gs.items()}
            grads['final_g'] = dgf[0]
            grads['meta'] = dh[0:N_META]
            tok = hooks.small_ready(grads)
        dw_in = dw_in_t(sv['z'], parts, nh, tm2)
        dw_in = dw_in.reshape(N_DEV, dw_in.shape[0] // N_DEV, d) + tok.astype(BF16)
        tok = hooks.grads_ready(l, 'in', dict(w_in=dw_in))

    return loss[0, 0], dh, tok


def prep_weights(g_in, g_out, nh, rw):
    d = g_in.shape[2]
    w_in_t = g_in.reshape(-1, d)
    f0 = 3 * nh * HEAD_DIM
    wrest_t = jnp.concatenate([w_in_t[f0 + nh:f0 + nh + 2 * rw],
                               jnp.pad(w_in_t[f0:f0 + nh], ((0, LANES - nh), (0, 0)))], axis=0)
    return w_in_t, wrest_t, g_out.reshape(d, d)


BIG = ['w_in', 'w_out', 'w_up', 'w_down']
EXCHANGE_GROUPS = {'mlp': ['w_down', 'w_up', 'w_out'], 'in': ['w_in']}
WEIGHTS = ['meta', 'attn_norm_g', 'w_in', 'b_f', 'conv_w', 'conv_b', 'w_gate_a', 'b_gate_a', 'w_gate_x', 'b_gate_x',
           'lru_L', 'attn_out_g', 'rec_out_g', 'w_out', 'mlp_norm_g', 'w_up', 'w_down', 'final_g']


def _set_own(arr, own, me):
    return lax.dynamic_update_slice_in_dim(arr, own[None], me, axis=0)


class _Step:
    def __init__(self, w, nh, rw, me):
        self.w, self.nh, self.rw, self.me = w, nh, rw, me
        depth = w['w_in'].shape[0]
        first = [w['w_in_t'][:, 0, :].astype(BF16), w['w_out'][0].astype(BF16), w['meta'], w['conv_w']]
        self.pending, token = gather_start([first], "gather_start_0")
        zero = token[0, 0].astype(BF16)
        groups = [[w['w_up'][0].astype(BF16) + zero, w['w_down'][0].astype(BF16) + zero]]
        for l in range(1, depth):
            groups.append([w['w_in_t'][:, l, :].astype(BF16) + zero, w['w_out'][l].astype(BF16) + zero])
            groups.append([w['w_up'][l].astype(BF16) + zero, w['w_down'][l].astype(BF16) + zero])
        rest, _ = gather_start(groups, "gather_start_1")
        self.pending += rest
        self.first_after = rest[0][2][0]
        self.gathered = {}
        self.passing = {}
        self.token = jnp.zeros((), F32)
        self.lands = {n: lax.empty((N_DEV,) + w[n].shape, BF16) for n in BIG}
        din8, _, d = w['w_in_t'].shape
        self.lands['w_in'] = lax.empty((N_DEV, depth, din8, d), BF16)
        self.started = []
        self.small = None

    def _pass_on(self, gi, after):
        if gi < len(self.pending) and gi not in self.passing:
            send, recv, srcs, lands = self.pending[gi]
            srcs, lands = gather_wait(send, recv, srcs, lands, after, "gather_wait_%d" % gi)
            fsend, frecv, lands, token = forward_start(lands, "forward_start_%d" % gi)
            self.passing[gi] = (fsend, frecv, srcs, lands)
            self.token = token

    def group(self, gi, after):
        if gi not in self.gathered:
            self._pass_on(gi, after)
            fsend, frecv, srcs, lands = self.passing[gi]
            lands = forward_wait(fsend, frecv, lands, after, "forward_wait_%d" % gi)
            self.gathered[gi] = [_set_own(g, own, self.me) for g, own in zip(lands, srcs)]
            if gi >= 2:
                self._pass_on(gi + 1, lands[0])
        return self.gathered[gi]

    def mixer_weights(self, l, after):
        g = self.group(2 * l, after)
        return (*prep_weights(g[0], g[1], self.nh, self.rw), self.token)

    def mlp_weights(self, l, after):
        g = self.group(2 * l + 1, after)
        return g[0], g[1], self.token

    def grads_ready(self, l, group, blocks):
        names = EXCHANGE_GROUPS[group]
        send, recv, srcs, lands, token = exchange_start(
            [blocks[n] for n in names], [self.lands[n] for n in names], l, "exchange_start_%s_%d" % (group, l))
        for n, a in zip(names, lands):
            self.lands[n] = a
        self.started.append((l, group, send, recv, srcs))
        return token

    def small_ready(self, grads):
        self.small_shapes = {n: grads[n].shape for n in SMALL}
        packed = _pack(grads).astype(BF16)
        send, recv, srcs, lands, token = exchange_start(
            [packed], [lax.empty((N_DEV,) + packed.shape, BF16)], None, "small_start")
        self.small = (send, recv, srcs, lands)
        return token

    def small_sum(self, after):
        send, recv, srcs, lands = self.small
        srcs, lands = exchange_wait(send, recv, srcs, lands, after, None, "small_wait")
        parts = _set_own(lands[0], srcs[0], self.me)
        return _unpack(sum_parts(parts, "sum_small_grads"), self.small_shapes)

    def received(self, group, after):
        names = EXCHANGE_GROUPS[group]
        own = {n: [None] * self.w[n].shape[0] for n in names}
        for l, grp, send, recv, srcs in self.started:
            if grp != group:
                continue
            srcs, lands = exchange_wait(send, recv, srcs, [self.lands[n] for n in names], after, l,
                                        "exchange_wait_%s_%d" % (group, l))
            for n, a, sr in zip(names, lands, srcs):
                self.lands[n] = a
                own[n][l] = lax.dynamic_index_in_dim(sr, self.me, 0, keepdims=False)
        return {n: _set_own(self.lands[n], jnp.stack(own[n]), self.me) for n in names}


def kernel(x, meta, attn_norm_g, w_in, b_f, conv_w, conv_b, w_gate_a, b_gate_a, w_gate_x, b_gate_x, lru_L, attn_out_g, rec_out_g, w_out, mlp_norm_g, w_up, w_down, final_g, loss_target, m_meta, m_attn_norm_g, m_w_in, m_b_f, m_conv_w, m_conv_b, m_w_gate_a, m_b_gate_a, m_w_gate_x, m_b_gate_x, m_lru_L, m_attn_out_g, m_rec_out_g, m_w_out, m_mlp_norm_g, m_w_up, m_w_down, m_final_g, v_meta, v_attn_norm_g, v_w_in, v_b_f, v_conv_w, v_conv_b, v_w_gate_a, v_b_gate_a, v_w_gate_x, v_b_gate_x, v_lru_L, v_attn_out_g, v_rec_out_g, v_w_out, v_mlp_norm_g, v_w_up, v_w_down, v_final_g):
    w = dict(meta=meta, attn_norm_g=attn_norm_g, w_in=w_in, b_f=b_f, conv_w=conv_w, conv_b=conv_b,
             w_gate_a=w_gate_a, b_gate_a=b_gate_a, w_gate_x=w_gate_x, b_gate_x=b_gate_x, lru_L=lru_L,
             attn_out_g=attn_out_g, rec_out_g=rec_out_g, w_out=w_out, mlp_norm_g=mlp_norm_g, w_up=w_up,
             w_down=w_down, final_g=final_g)
    mo = dict(meta=m_meta, attn_norm_g=m_attn_norm_g, w_in=m_w_in, b_f=m_b_f, conv_w=m_conv_w, conv_b=m_conv_b,
              w_gate_a=m_w_gate_a, b_gate_a=m_b_gate_a, w_gate_x=m_w_gate_x, b_gate_x=m_b_gate_x, lru_L=m_lru_L,
              attn_out_g=m_attn_out_g, rec_out_g=m_rec_out_g, w_out=m_w_out, mlp_norm_g=m_mlp_norm_g,
              w_up=m_w_up, w_down=m_w_down, final_g=m_final_g)
    vo = dict(meta=v_meta, attn_norm_g=v_attn_norm_g, w_in=v_w_in, b_f=v_b_f, conv_w=v_conv_w, conv_b=v_conv_b,
              w_gate_a=v_w_gate_a, b_gate_a=v_b_gate_a, w_gate_x=v_w_gate_x, b_gate_x=v_b_gate_x, lru_L=v_lru_L,
              attn_out_g=v_attn_out_g, rec_out_g=v_rec_out_g, w_out=v_w_out, mlp_norm_g=v_mlp_norm_g,
              w_up=v_w_up, w_down=v_w_down, final_g=v_final_g)
    depth = w_in.shape[0]
    nh = b_f.shape[1]
    rw = conv_b.shape[1]
    me = 4 * lax.axis_index("x") + 2 * lax.axis_index("y") + lax.axis_index("c")

    w['w_in_t'] = jnp.transpose(w_in, (2, 0, 1))
    swap = lambda a: jnp.swapaxes(a, 1, 2)
    step = _Step(w, nh, rw, me)
    g0 = step.group(0, step.first_after)
    meta_full = g0[2].transpose(1, 0, 2).reshape(N_META, -1)
    conv_full = g0[3].transpose(1, 2, 0, 3).reshape(depth, CONV_WIDTH, rw)
    small = {n: w[n] for n in SMALL}
    small['conv_w'] = conv_full

    loss_part, dh0, tok = local_step(x[0], loss_target[0], meta_full, small, step)
    loss = lax.psum(loss_part, ("x", "y", "c"))
    grad_x = dh0[N_META:N_META + x.shape[1]][None]

    out_g, out_d, out_m, out_v = {}, {}, {}, {}

    def update_big(group, after):
        for n, r in step.received(group, after).items():
            if n == 'w_in':
                out = sum_adamw_t(r, swap(w[n]), swap(mo[n]), swap(vo[n]), "adamw_w_in")
                out = [swap(a) for a in out]
            else:
                shp = w[n].shape
                rows, cols = shp[0] * shp[1], shp[2]
                tr = min(512 if cols <= 512 else 256, rows)
                out = sum_adamw(r.reshape(N_DEV, rows, cols), w[n].reshape(rows, cols), mo[n].reshape(rows, cols),
                                vo[n].reshape(rows, cols), tr, "adamw_" + n)
                out = [a.reshape(shp) for a in out]
            out_g[n], out_d[n], out_m[n], out_v[n] = out
            after = out[0]
        return after

    update_big('mlp', step.started[-1][4][0])

    gsum = step.small_sum([out_g[n] for n in EXCHANGE_GROUPS['mlp']])
    gsum['meta'] = lax.dynamic_slice_in_dim(gsum['meta'], me * meta.shape[1], meta.shape[1], axis=1)
    gsum['conv_w'] = lax.dynamic_slice_in_dim(gsum['conv_w'], me * conv_w.shape[2], conv_w.shape[2], axis=2)
    as2d = lambda a: a.reshape(-1, a.shape[-1])
    deltas, new_m, new_v = adamw_group([as2d(gsum[n]) for n in SMALL], [as2d(w[n]) for n in SMALL],
                                       [as2d(mo[n]) for n in SMALL], [as2d(vo[n]) for n in SMALL], "adamw_small")
    for i, n in enumerate(SMALL):
        out_g[n] = gsum[n]
        out_d[n], out_m[n], out_v[n] = [a[i].reshape(w[n].shape) for a in (deltas, new_m, new_v)]

    update_big('in', deltas[0])

    return (loss, grad_x, *[out_g[n] for n in WEIGHTS], *[out_d[n] for n in WEIGHTS],
            *[out_m[n] for n in WEIGHTS], *[out_v[n] for n in WEIGHTS])
```

```python
import functools
import math

import jax
import jax.numpy as jnp
from jax import lax
from jax.experimental import pallas as pl
from jax.experimental.pallas import tpu as pltpu

F32 = jnp.float32
BF16 = jnp.bfloat16

N_DEV = 8
N_META = 16
HEAD_DIM = 64
CONV_WIDTH = 4
RG_C = 8.0
NORM_EPS = 1e-6
LANES = 128
SUBLANES = 8
ATT_BLOCK = 128
ATT_TQ = 512
NEG_BIG = -1e30
ATT_SCALE = 1.0 / math.sqrt(HEAD_DIM)

ADAM_LR = 0.001
ADAM_B1 = 0.9
ADAM_B2 = 0.999
ADAM_EPS = 1e-08
ADAM_WD = 0.01
ADAM_STEP = 10

VMEM_LIMIT_BYTES = 56 * 1024 * 1024
MESH = pl.DeviceIdType.MESH
ANY = pl.BlockSpec(memory_space=pl.ANY)


def _cparams(*sem):
    return pltpu.CompilerParams(dimension_semantics=sem if sem else None,
                                vmem_limit_bytes=VMEM_LIMIT_BYTES)


def _dot(a, b):
    return jnp.dot(a, b, preferred_element_type=F32)


def _dot_nt(a, b):
    return lax.dot_general(a, b, (((1,), (1,)), ((), ())), preferred_element_type=F32)


def _dot_tn(a, b):
    return lax.dot_general(a, b, (((0,), (0,)), ((), ())), preferred_element_type=F32)


def _sigmoid(x):
    return 0.5 * (1.0 + jnp.tanh(0.5 * x))


def _log_sigmoid(x):
    return jnp.minimum(x, 0.0) - jnp.log(1.0 + jnp.exp(-jnp.abs(x)))


def _expm1(x):
    series = x * (1.0 + x * (0.5 + x * (1.0 / 6.0 + x * (1.0 / 24.0))))
    return jnp.where(jnp.abs(x) < 1e-2, series, jnp.exp(x) - 1.0)


_GELU_K = math.sqrt(2.0 / math.pi)
_GELU_C = 0.044715


def _gelu(x):
    t = jnp.tanh(_GELU_K * (x + _GELU_C * x * x * x))
    return 0.5 * x * (1.0 + t)


def _gelu_grad(x):
    t = jnp.tanh(_GELU_K * (x + _GELU_C * x * x * x))
    return 0.5 * (1.0 + t) + 0.5 * x * (1.0 - t * t) * _GELU_K * (1.0 + 3.0 * _GELU_C * x * x)


def _split3_dot(tri, x):
    hi = x.astype(BF16)
    r1 = x - hi.astype(F32)
    mid = r1.astype(BF16)
    lo = (r1 - mid.astype(F32)).astype(BF16)
    return _dot(tri, hi) + _dot(tri, mid) + _dot(tri, lo)


def _dot_split3(x, sel):
    hi = x.astype(BF16)
    r1 = x - hi.astype(F32)
    mid = r1.astype(BF16)
    lo = (r1 - mid.astype(F32)).astype(BF16)
    return _dot(hi, sel) + _dot(mid, sel) + _dot(lo, sel)


def _rms_fwd(x, g):
    r = lax.rsqrt(jnp.mean(x * x, axis=-1, keepdims=True) + NORM_EPS)
    return x * r * g


def _rms_bwd(x, g, dy):
    r = lax.rsqrt(jnp.mean(x * x, axis=-1, keepdims=True) + NORM_EPS)
    xn = x * r
    dxn = dy * g
    dx = r * (dxn - xn * jnp.mean(dxn * xn, axis=-1, keepdims=True))
    return dx, jnp.sum(dy * xn, axis=0, keepdims=True)


def _accumulate(ref, val, first):
    @pl.when(first)
    def _():
        ref[...] = val

    @pl.when(jnp.logical_not(first))
    def _():
        ref[...] += val


def in_proj(h, g1, w_in_t, wrest_t, nq, tm):
    tp, d = h.shape
    nr = wrest_t.shape[0]

    def body(h_ref, g_ref, wq_ref, wr_ref, z_ref, qkv_ref, rest_ref):
        z = _rms_fwd(h_ref[...], g_ref[...]).astype(BF16)
        z_ref[...] = z
        qkv_ref[...] = _dot_nt(z, wq_ref[...]).astype(BF16)
        rest_ref[...] = _dot_nt(z, wr_ref[...])

    return pl.pallas_call(
        body, name="in_proj", grid=(tp // tm,),
        in_specs=[pl.BlockSpec((tm, d), lambda i: (i, 0)),
                  pl.BlockSpec((1, d), lambda i: (0, 0)),
                  pl.BlockSpec((nq, d), lambda i: (0, 0)),
                  pl.BlockSpec((nr, d), lambda i: (0, 0))],
        out_specs=[pl.BlockSpec((tm, d), lambda i: (i, 0)),
                   pl.BlockSpec((tm, nq), lambda i: (i, 0)),
                   pl.BlockSpec((tm, nr), lambda i: (i, 0))],
        out_shape=[jax.ShapeDtypeStruct((tp, d), BF16),
                   jax.ShapeDtypeStruct((tp, nq), BF16),
                   jax.ShapeDtypeStruct((tp, nr), F32)],
        compiler_params=_cparams("parallel"),
    )(h, g1, w_in_t, wrest_t)


def fgate_fwd(rest, bf_pad, fcol):
    tp = rest.shape[0]
    nb = tp // ATT_BLOCK

    def body(f_ref, b_ref, c_ref, ct_ref):
        r_i = lax.broadcasted_iota(jnp.int32, (ATT_BLOCK, ATT_BLOCK), 0)
        c_i = lax.broadcasted_iota(jnp.int32, (ATT_BLOCK, ATT_BLOCK), 1)
        tri = (r_i >= c_i).astype(BF16)
        carry = jnp.zeros((1, LANES), F32)
        for i in range(nb):
            sl = slice(i * ATT_BLOCK, (i + 1) * ATT_BLOCK)
            lf = _log_sigmoid(f_ref[sl, :] + b_ref[...])
            cs = _split3_dot(tri, lf) + carry
            carry = cs[ATT_BLOCK - 1:ATT_BLOCK, :]
            c_ref[sl, :] = cs
            ct_ref[:, sl] = cs.T[0:SUBLANES, :]

    return pl.pallas_call(
        body, name="fgate_fwd", grid=(1,),
        in_specs=[pl.BlockSpec((tp, LANES), lambda i: (0, fcol)),
                  pl.BlockSpec((1, LANES), lambda i: (0, 0))],
        out_specs=[pl.BlockSpec((tp, LANES), lambda i: (0, 0)),
                   pl.BlockSpec((SUBLANES, tp), lambda i: (0, 0))],
        out_shape=[jax.ShapeDtypeStruct((tp, LANES), F32),
                   jax.ShapeDtypeStruct((SUBLANES, tp), F32)],
        compiler_params=_cparams("arbitrary"),
    )(rest, bf_pad)


def _pick_col(blk, head):
    lane = lax.broadcasted_iota(jnp.int32, blk.shape, 1)
    return jnp.sum(jnp.where(lane == head, blk, 0.0), axis=1, keepdims=True)


def _pick_row(blk, head):
    sub = lax.broadcasted_iota(jnp.int32, blk.shape, 0)
    return jnp.sum(jnp.where(sub == head, blk, 0.0), axis=0, keepdims=True)


def _att_tiles(tp):
    out, r0 = [], 0
    while r0 < tp:
        rows = min(ATT_TQ, tp - r0)
        out.append((r0, rows, r0 + rows))
        r0 += rows
    return out


def attn_fwd(qkv, c, ct, nh):
    tp = qkv.shape[0]
    npair = nh // 2
    tiles = _att_tiles(tp)

    def body(q_ref, k_ref, v_ref, c_ref, ct_ref, o_ref, lset_ref):
        p = pl.program_id(0)
        lset_ref[...] = jnp.zeros_like(lset_ref)
        for r0, nr, nk in tiles:
            rs = slice(r0, r0 + nr)
            causal = (r0 + lax.broadcasted_iota(jnp.int32, (nr, nk), 0)
                      >= lax.broadcasted_iota(jnp.int32, (nr, nk), 1))
            cblk = c_ref[rs, :]
            ctb = ct_ref[:, 0:nk]
            for hh in range(2):
                head = 2 * p + hh
                hs = slice(hh * HEAD_DIM, (hh + 1) * HEAD_DIM)
                q = q_ref[rs, hs] * ATT_SCALE
                s = _dot_nt(q, k_ref[0:nk, hs]) + (_pick_col(cblk, head) - _pick_row(ctb, head))
                s = jnp.where(causal, s, NEG_BIG)
                m = jnp.max(s, axis=1, keepdims=True)
                pm = jnp.exp(s - m)
                l = jnp.sum(pm, axis=1, keepdims=True)
                o_ref[rs, hs] = _dot(pm.astype(BF16), v_ref[0:nk, hs]) / l
                lse = m + jnp.log(l)
                lset_ref[hh:hh + 1, rs] = jnp.broadcast_to(lse, (nr, LANES)).T[0:1, :]

    pair = lambda p: (0, p)
    return pl.pallas_call(
        body, name="attn_fwd", grid=(npair,),
        in_specs=[pl.BlockSpec((tp, LANES), pair),
                  pl.BlockSpec((tp, LANES), lambda p: (0, npair + p)),
                  pl.BlockSpec((tp, LANES), lambda p: (0, 2 * npair + p)),
                  pl.BlockSpec((tp, LANES), lambda p: (0, 0)),
                  pl.BlockSpec((SUBLANES, tp), lambda p: (0, 0))],
        out_specs=[pl.BlockSpec((tp, LANES), pair),
                   pl.BlockSpec((None, SUBLANES, tp), lambda p: (p, 0, 0))],
        out_shape=[jax.ShapeDtypeStruct((tp, nh * HEAD_DIM), F32),
                   jax.ShapeDtypeStruct((npair, SUBLANES, tp), F32)],
        compiler_params=_cparams("parallel"),
    )(qkv, qkv, qkv, c, ct)


def _shift_down(x, k, n):
    if k == 0:
        return x
    rows = lax.broadcasted_iota(jnp.int32, x.shape, 0)
    return jnp.where(rows >= k, pltpu.roll(x, k, 0), 0.0)


def _shift_up(x, k, n):
    if k == 0:
        return x
    rows = lax.broadcasted_iota(jnp.int32, x.shape, 0)
    return jnp.where(rows < n - k, pltpu.roll(x, n - k, 0), 0.0)


def _conv_fwd(xr, cw_ref, cb_ref, n):
    xc = cw_ref[CONV_WIDTH - 1:CONV_WIDTH, :] * xr + cb_ref[...]
    for k in range(1, CONV_WIDTH):
        xc = xc + cw_ref[CONV_WIDTH - 1 - k:CONV_WIDTH - k, :] * _shift_down(xr, k, n)
    return xc


def _gates(xc, wga_ref, bga_ref, wgx_ref, bgx_ref, l_ref):
    xcb = xc.astype(BF16)
    r = _sigmoid(_dot(xcb, wga_ref[...]) + bga_ref[...])
    ig = _sigmoid(_dot(xcb, wgx_ref[...]) + bgx_ref[...])
    ls = _log_sigmoid(l_ref[...])
    log_a = RG_C * r * ls
    a = jnp.exp(log_a)
    mult = jnp.sqrt(-_expm1(2.0 * log_a))
    return xcb, r, ig, ls, log_a, a, mult


SCAN_UNROLL = 4


def _scan_rows(a_s, u_s, out_ref, n, reverse):
    nt = n // SUBLANES
    per = SCAN_UNROLL if nt % SCAN_UNROLL == 0 else 1
    row = lax.broadcasted_iota(jnp.int32, (SUBLANES, LANES), 0)
    last = 0 if reverse else SUBLANES - 1

    def tile_scan(a, u):
        for d in (1, 2, 4):
            if reverse:
                keep = row < SUBLANES - d
                sh = SUBLANES - d
            else:
                keep = row >= d
                sh = d
            a_sh = jnp.where(keep, pltpu.roll(a, sh, 0), 1.0)
            u_sh = jnp.where(keep, pltpu.roll(u, sh, 0), 0.0)
            u = a * u_sh + u
            a = a * a_sh
        return a, u

    def step(t, carry):
        tiles = []
        for k in range(per):
            tt = t * per + k
            if reverse:
                tt = nt - 1 - tt
            off = pl.multiple_of(tt * SUBLANES, SUBLANES)
            a, u = tile_scan(a_s[pl.ds(off, SUBLANES), :], u_s[pl.ds(off, SUBLANES), :])
            tiles.append((off, a, u))
        for off, a, u in tiles:
            out_ref[pl.ds(off, SUBLANES), :] = u + a * carry
            carry = u[last:last + 1, :] + a[last:last + 1, :] * carry
        return carry

    lax.fori_loop(0, nt // per, step, jnp.zeros((1, LANES), F32))


def rec_fwd(rest, convw, convb, wga, bga, wgx, bgx, lru, rw):
    tp = rest.shape[0]
    ng = rw // LANES

    def body(xr_ref, yr_ref, cw_ref, cb_ref, wga_ref, bga_ref, wgx_ref, bgx_ref, l_ref,
             rec_ref, hr_ref, xc_ref, a_s, u_s):
        xc = _conv_fwd(xr_ref[...], cw_ref, cb_ref, tp)
        xc_ref[...] = xc
        _, r, ig, ls, log_a, a, mult = _gates(xc, wga_ref, bga_ref, wgx_ref, bgx_ref, l_ref)
        a_s[...] = a
        u_s[...] = mult * ig * xc
        _scan_rows(a_s, u_s, hr_ref, tp, reverse=False)
        rec_ref[...] = hr_ref[...] * _gelu(yr_ref[...])

    col = lambda g: (0, g)
    vec = pl.BlockSpec((1, LANES), col)
    big = pl.BlockSpec((tp, LANES), col)
    return pl.pallas_call(
        body, name="rec_fwd", grid=(ng,),
        in_specs=[big, pl.BlockSpec((tp, LANES), lambda g: (0, ng + g)),
                  pl.BlockSpec((CONV_WIDTH, LANES), col), vec,
                  pl.BlockSpec((None, LANES, LANES), lambda g: (g, 0, 0)), vec,
                  pl.BlockSpec((None, LANES, LANES), lambda g: (g, 0, 0)), vec, vec],
        out_specs=[big, big, big],
        out_shape=[jax.ShapeDtypeStruct((tp, rw), F32)] * 3,
        scratch_shapes=[pltpu.VMEM((tp, LANES), F32), pltpu.VMEM((tp, LANES), F32)],
        compiler_params=_cparams("parallel"),
    )(rest, rest, convw, convb, wga, bga, wgx, bgx, lru)


def out_proj(h, o, rec, ga, gr, wout, g2, tm):
    tp, d = h.shape
    aw, rw = o.shape[1], rec.shape[1]

    def body(h_ref, o_ref, rec_ref, ga_ref, gr_ref, w_ref, g2_ref, h2_ref, mix_ref, z2_ref):
        mix_ref[:, 0:aw] = _rms_fwd(o_ref[...], ga_ref[...]).astype(BF16)
        mix_ref[:, aw:aw + rw] = _rms_fwd(rec_ref[...], gr_ref[...]).astype(BF16)
        h2 = h_ref[...] + _dot(mix_ref[...], w_ref[...])
        h2_ref[...] = h2
        z2_ref[...] = _rms_fwd(h2, g2_ref[...]).astype(BF16)

    row = lambda i: (i, 0)
    fix = lambda i: (0, 0)
    return pl.pallas_call(
        body, name="out_proj", grid=(tp // tm,),
        in_specs=[pl.BlockSpec((tm, d), row), pl.BlockSpec((tm, aw), row), pl.BlockSpec((tm, rw), row),
                  pl.BlockSpec((1, aw), fix), pl.BlockSpec((1, rw), fix),
                  pl.BlockSpec((d, d), fix), pl.BlockSpec((1, d), fix)],
        out_specs=[pl.BlockSpec((tm, d), row)] * 3,
        out_shape=[jax.ShapeDtypeStruct((tp, d), F32), jax.ShapeDtypeStruct((tp, d), BF16),
                   jax.ShapeDtypeStruct((tp, d), BF16)],
        compiler_params=_cparams("parallel"),
    )(h, o, rec, ga, gr, wout, g2)


MLP_BLOCKS = 2


def mlp_fwd(z2, h2, gup, gdown, tm):
    tp, d = h2.shape
    nf = gup.shape[0]
    tf = gup.shape[2]
    nb = MLP_BLOCKS if nf % MLP_BLOCKS == 0 else 1
    nj = nf // nb

    def body(z_ref, h_ref, wu_ref, wd_ref, u_ref, h3_ref, acc):
        j = pl.program_id(1)
        z = z_ref[...]
        part = None
        for b in range(nb):
            u = jnp.maximum(_dot(z, wu_ref[b]), 0.0)
            u_ref[:, b * tf:(b + 1) * tf] = u.astype(BF16)
            p = _dot((u * u).astype(BF16), wd_ref[b])
            part = p if part is None else part + p

        @pl.when(j == 0)
        def _():
            acc[...] = h_ref[...] + part

        @pl.when(j > 0)
        def _():
            acc[...] += part

        @pl.when(j == nj - 1)
        def _():
            h3_ref[...] = acc[...]

    return pl.pallas_call(
        body, name="mlp_fwd", grid=(tp // tm, nj),
        in_specs=[pl.BlockSpec((tm, d), lambda i, j: (i, 0)),
                  pl.BlockSpec((tm, d), lambda i, j: (i, 0)),
                  pl.BlockSpec((nb, d, tf), lambda i, j: (j, 0, 0)),
                  pl.BlockSpec((nb, tf, d), lambda i, j: (j, 0, 0))],
        out_specs=[pl.BlockSpec((tm, nb * tf), lambda i, j: (i, j)),
                   pl.BlockSpec((tm, d), lambda i, j: (i, 0))],
        out_shape=[jax.ShapeDtypeStruct((tp, nf * tf), BF16), jax.ShapeDtypeStruct((tp, d), F32)],
        scratch_shapes=[pltpu.VMEM((tm, d), F32)],
        compiler_params=_cparams("parallel", "arbitrary"),
    )(z2, h2, gup, gdown)


def loss_head(h, gf, tgt, t_real, tm):
    tp, d = h.shape

    def body(h_ref, g_ref, t_ref, dh_ref, dg_ref, loss_ref):
        i = pl.program_id(0)
        x = h_ref[...]
        g = g_ref[...]
        r = lax.rsqrt(jnp.mean(x * x, axis=-1, keepdims=True) + NORM_EPS)
        xn = x * r
        rows = i * tm + lax.broadcasted_iota(jnp.int32, (tm, 1), 0)
        valid = jnp.logical_and(rows >= N_META, rows < t_real)
        e = jnp.where(valid, xn * g - t_ref[...], 0.0)
        part = 0.5 * jnp.sum(jnp.sum(e * e, axis=1, keepdims=True) / d, axis=0, keepdims=True)
        dy = e / d
        dxn = dy * g
        dh_ref[...] = r * (dxn - xn * jnp.mean(dxn * xn, axis=-1, keepdims=True))
        _accumulate(dg_ref, jnp.sum(dy * xn, axis=0, keepdims=True), i == 0)
        _accumulate(loss_ref, jnp.broadcast_to(part, (1, LANES)), i == 0)

    row = lambda i: (i, 0)
    fix = lambda i: (0, 0)
    return pl.pallas_call(
        body, name="loss_head", grid=(tp // tm,),
        in_specs=[pl.BlockSpec((tm, d), row), pl.BlockSpec((1, d), fix), pl.BlockSpec((tm, d), row)],
        out_specs=[pl.BlockSpec((tm, d), row), pl.BlockSpec((1, d), fix), pl.BlockSpec((1, LANES), fix)],
        out_shape=[jax.ShapeDtypeStruct((tp, d), F32), jax.ShapeDtypeStruct((1, d), F32),
                   jax.ShapeDtypeStruct((1, LANES), F32)],
        compiler_params=_cparams("arbitrary"),
    )(h, gf, tgt)


def mlp_bwd_hidden(dh, u, gdown, tm):
    tp, d = dh.shape
    nf = gdown.shape[0]
    tf = gdown.shape[1]

    def body(dh_ref, u_ref, wd_ref, dup_ref, dhb):
        @pl.when(pl.program_id(1) == 0)
        def _():
            dhb[...] = dh_ref[...].astype(BF16)

        dup_ref[...] = (_dot_nt(dhb[...], wd_ref[...]) * (2.0 * u_ref[...].astype(F32))).astype(BF16)

    return pl.pallas_call(
        body, name="mlp_bwd_hidden", grid=(tp // tm, nf),
        in_specs=[pl.BlockSpec((tm, d), lambda i, j: (i, 0)),
                  pl.BlockSpec((tm, tf), lambda i, j: (i, j)),
                  pl.BlockSpec((None, tf, d), lambda i, j: (j, 0, 0))],
        out_specs=[pl.BlockSpec((tm, tf), lambda i, j: (i, j)),
                   pl.BlockSpec((tm, d), lambda i, j: (i, 0))],
        out_shape=[jax.ShapeDtypeStruct((tp, nf * tf), BF16), jax.ShapeDtypeStruct((tp, d), BF16)],
        compiler_params=_cparams("parallel", "arbitrary"),
    )(dh, u, gdown)


def mlp_bwd_out(dup, dh, h2, g2, gup, tm):
    tp, d = dh.shape
    nf = gup.shape[0]
    tf = gup.shape[2]

    def body(dup_ref, dh_ref, h2_ref, g_ref, wu_ref, dh2_ref, dg_ref):
        i = pl.program_id(0)
        dz = _dot_nt(dup_ref[:, 0:tf], wu_ref[0])
        for b in range(1, nf):
            dz += _dot_nt(dup_ref[:, b * tf:(b + 1) * tf], wu_ref[b])
        dx, dg = _rms_bwd(h2_ref[...], g_ref[...], dz)
        dh2_ref[...] = dh_ref[...] + dx
        _accumulate(dg_ref, dg, i == 0)

    row = lambda i: (i, 0)
    return pl.pallas_call(
        body, name="mlp_bwd_out", grid=(tp // tm,),
        in_specs=[pl.BlockSpec((tm, nf * tf), row), pl.BlockSpec((tm, d), row), pl.BlockSpec((tm, d), row),
                  pl.BlockSpec((1, d), lambda i: (0, 0)),
                  pl.BlockSpec((nf, d, tf), lambda i: (0, 0, 0))],
        out_specs=[pl.BlockSpec((tm, d), row), pl.BlockSpec((1, d), lambda i: (0, 0))],
        out_shape=[jax.ShapeDtypeStruct((tp, d), F32), jax.ShapeDtypeStruct((1, d), F32)],
        compiler_params=_cparams("arbitrary"),
    )(dup, dh, h2, g2, gup)


def mm_tn(a, b, *, tk, tn, out_dtype, name, square_a=False, blocked_n=False):
    rows, kk = a.shape
    nn = b.shape[1]
    keep_at = nn // tn > 1

    def a_tile(a_ref):
        av = a_ref[...]
        if square_a:
            af = av.astype(F32)
            av = af * af
        return av.astype(BF16)

    def body(a_ref, b_ref, o_ref, *scratch):
        if keep_at:
            at, = scratch

            @pl.when(pl.program_id(1) == 0)
            def _():
                at[...] = a_tile(a_ref).T

            o_ref[...] = _dot(at[...], b_ref[...].astype(BF16)).astype(out_dtype)
        else:
            o_ref[...] = _dot_tn(a_tile(a_ref), b_ref[...].astype(BF16)).astype(out_dtype)

    if blocked_n:
        out_spec = pl.BlockSpec((None, tk, tn), lambda k, n: (n, k, 0))
        out_shape = jax.ShapeDtypeStruct((nn // tn, kk, tn), out_dtype)
    else:
        out_spec = pl.BlockSpec((tk, tn), lambda k, n: (k, n))
        out_shape = jax.ShapeDtypeStruct((kk, nn), out_dtype)
    return pl.pallas_call(
        body, name=name, grid=(kk // tk, nn // tn),
        in_specs=[pl.BlockSpec((rows, tk), lambda k, n: (0, k)),
                  pl.BlockSpec((rows, tn), lambda k, n: (0, n))],
        out_specs=out_spec, out_shape=out_shape,
        scratch_shapes=[pltpu.VMEM((tk, rows), BF16)] if keep_at else [],
        compiler_params=_cparams("parallel", "arbitrary"),
    )(a, b)


def dw_mlp(u, dhb, z2, dup, tf):
    rows, dff = u.shape
    d = z2.shape[1]
    nf = dff // tf

    def body(u_ref, dh_ref, z_ref, dup_ref, dwd_ref, dwu_ref, zt):
        @pl.when(pl.program_id(0) == 0)
        def _():
            zt[...] = z_ref[...].T

        uf = u_ref[...].astype(F32)
        dwd_ref[...] = _dot_tn((uf * uf).astype(BF16), dh_ref[...]).astype(BF16)
        dwu_ref[...] = _dot(zt[...], dup_ref[...]).astype(BF16)

    col = lambda j: (0, j)
    fix = lambda j: (0, 0)
    return pl.pallas_call(
        body, name="dw_mlp", grid=(nf,),
        in_specs=[pl.BlockSpec((rows, tf), col), pl.BlockSpec((rows, d), fix),
                  pl.BlockSpec((rows, d), fix), pl.BlockSpec((rows, tf), col)],
        out_specs=[pl.BlockSpec((None, tf, d), lambda j: (j, 0, 0)),
                   pl.BlockSpec((None, d, tf), lambda j: (j, 0, 0))],
        out_shape=[jax.ShapeDtypeStruct((nf, tf, d), BF16), jax.ShapeDtypeStruct((nf, d, tf), BF16)],
        scratch_shapes=[pltpu.VMEM((d, rows), BF16)],
        compiler_params=_cparams("arbitrary"),
    )(u, dhb, z2, dup)


def out_proj_bwd(dh2, o, rec, ga, gr, wout, tm):
    tp, d = dh2.shape
    aw, rw = o.shape[1], rec.shape[1]

    def body(dh_ref, o_ref, rec_ref, ga_ref, gr_ref, w_ref, do_ref, drec_ref, dga_ref, dgr_ref):
        i = pl.program_id(0)
        dmix = _dot_nt(dh_ref[...].astype(BF16), w_ref[...])
        do, dga = _rms_bwd(o_ref[...], ga_ref[...], dmix[:, 0:aw])
        drec, dgr = _rms_bwd(rec_ref[...], gr_ref[...], dmix[:, aw:aw + rw])
        do_ref[...] = do
        drec_ref[...] = drec
        _accumulate(dga_ref, dga, i == 0)
        _accumulate(dgr_ref, dgr, i == 0)

    row = lambda i: (i, 0)
    fix = lambda i: (0, 0)
    return pl.pallas_call(
        body, name="out_proj_bwd", grid=(tp // tm,),
        in_specs=[pl.BlockSpec((tm, d), row), pl.BlockSpec((tm, aw), row), pl.BlockSpec((tm, rw), row),
                  pl.BlockSpec((1, aw), fix), pl.BlockSpec((1, rw), fix), pl.BlockSpec((d, d), fix)],
        out_specs=[pl.BlockSpec((tm, aw), row), pl.BlockSpec((tm, rw), row),
                   pl.BlockSpec((1, aw), fix), pl.BlockSpec((1, rw), fix)],
        out_shape=[jax.ShapeDtypeStruct((tp, aw), F32), jax.ShapeDtypeStruct((tp, rw), F32),
                   jax.ShapeDtypeStruct((1, aw), F32), jax.ShapeDtypeStruct((1, rw), F32)],
        compiler_params=_cparams("arbitrary"),
    )(dh2, o, rec, ga, gr, wout)


def rec_bwd(drec, hr, xc, rest, convw, convb, wga, bga, wgx, bgx, lru, rw):
    tp = rest.shape[0]
    ng = rw // LANES

    def body(drec_ref, hr_ref, xc_ref, xr_ref, yr_ref, cw_ref, cb_ref, wga_ref, bga_ref, wgx_ref, bgx_ref, l_ref,
             dxr_ref, dyr_ref, dwga_ref, dwgx_ref, vec_ref, a_s, u_s, lam_s):
        xc = xc_ref[...]
        h = hr_ref[...]
        yr = yr_ref[...]
        drec = drec_ref[...]
        xcb, r, ig, ls, log_a, a, mult = _gates(xc, wga_ref, bga_ref, wgx_ref, bgx_ref, l_ref)
        dyr_ref[...] = (drec * h * _gelu_grad(yr)).astype(BF16)
        a_s[...] = _shift_up(a, 1, tp)
        u_s[...] = drec * _gelu(yr)
        _scan_rows(a_s, u_s, lam_s, tp, reverse=True)
        lam = lam_s[...]
        da = lam * _shift_down(h, 1, tp)
        dmult = lam * ig * xc
        dig = lam * mult * xc
        dxc = lam * mult * ig
        a2 = jnp.exp(2.0 * log_a)
        dlog_a = da * a - dmult * a2 / mult
        dr = dlog_a * (RG_C * ls)
        dl = jnp.sum(dlog_a * (RG_C * r), axis=0, keepdims=True) * _sigmoid(-l_ref[...])
        dpa = dr * r * (1.0 - r)
        dpx = dig * ig * (1.0 - ig)
        dpab = dpa.astype(BF16)
        dpxb = dpx.astype(BF16)
        dxc = dxc + _dot_nt(dpab, wga_ref[...]) + _dot_nt(dpxb, wgx_ref[...])
        dwga_ref[...] = _dot_tn(xcb, dpab)
        dwgx_ref[...] = _dot_tn(xcb, dpxb)
        xr = xr_ref[...]
        dxr = cw_ref[CONV_WIDTH - 1:CONV_WIDTH, :] * dxc
        for k in range(1, CONV_WIDTH):
            dxr = dxr + cw_ref[CONV_WIDTH - 1 - k:CONV_WIDTH - k, :] * _shift_up(dxc, k, tp)
        dxr_ref[...] = dxr.astype(BF16)
        for k in range(CONV_WIDTH):
            vec_ref[k:k + 1, :] = jnp.sum(dxc * _shift_down(xr, CONV_WIDTH - 1 - k, tp), axis=0, keepdims=True)
        vec_ref[4:5, :] = jnp.sum(dxc, axis=0, keepdims=True)
        vec_ref[5:6, :] = jnp.sum(dpa, axis=0, keepdims=True)
        vec_ref[6:7, :] = jnp.sum(dpx, axis=0, keepdims=True)
        vec_ref[7:8, :] = dl

    col = lambda g: (0, g)
    vec = pl.BlockSpec((1, LANES), col)
    big = pl.BlockSpec((tp, LANES), col)
    sq = pl.BlockSpec((None, LANES, LANES), lambda g: (g, 0, 0))
    return pl.pallas_call(
        body, name="rec_bwd", grid=(ng,),
        in_specs=[big, big, big, big, pl.BlockSpec((tp, LANES), lambda g: (0, ng + g)),
                  pl.BlockSpec((CONV_WIDTH, LANES), col), vec, sq, vec, sq, vec, vec],
        out_specs=[big, big, sq, sq, pl.BlockSpec((None, SUBLANES, LANES), lambda g: (g, 0, 0))],
        out_shape=[jax.ShapeDtypeStruct((tp, rw), BF16), jax.ShapeDtypeStruct((tp, rw), BF16),
                   jax.ShapeDtypeStruct((ng, LANES, LANES), F32), jax.ShapeDtypeStruct((ng, LANES, LANES), F32),
                   jax.ShapeDtypeStruct((ng, SUBLANES, LANES), F32)],
        scratch_shapes=[pltpu.VMEM((tp, LANES), F32)] * 3,
        compiler_params=_cparams("parallel"),
    )(drec, hr, xc, rest, rest, convw, convb, wga, bga, wgx, bgx, lru)


def attn_bwd(qkv, do, o, lset, c, ct, nh):
    tp = qkv.shape[0]
    npair = nh // 2
    aw = nh * HEAD_DIM
    tiles = _att_tiles(tp)

    def body(q_ref, k_ref, v_ref, do_ref, o_ref, lset_ref, c_ref, ct_ref,
             dq_ref, dk_ref, dv_ref, drow_ref, dcol_ref, dk_acc, dv_acc, dq_t):
        p = pl.program_id(0)
        k_t = k_ref[...].T
        dk_acc[...] = jnp.zeros_like(dk_acc)
        dv_acc[...] = jnp.zeros_like(dv_acc)
        dcol_ref[...] = jnp.zeros_like(dcol_ref)
        drow_ref[...] = jnp.zeros_like(drow_ref)
        for r0, nr, nk in tiles:
            rs = slice(r0, r0 + nr)
            causal = (r0 + lax.broadcasted_iota(jnp.int32, (nk, nr), 1)
                      >= lax.broadcasted_iota(jnp.int32, (nk, nr), 0))
            cblk = c_ref[0:nk, :]
            ctb = ct_ref[:, rs]
            for hh in range(2):
                head = 2 * p + hh
                hs = slice(hh * HEAD_DIM, (hh + 1) * HEAD_DIM)
                q = q_ref[rs, hs]
                k = k_ref[0:nk, hs]
                dof = do_ref[rs, hs]
                do16 = dof.astype(BF16)
                delta = jnp.sum(dof * o_ref[rs, hs], axis=1, keepdims=True)
                delta_row = jnp.broadcast_to(delta, (nr, LANES)).T[0:1, :]
                s_t = _dot_nt(k, q * ATT_SCALE) + (_pick_row(ctb, head) - _pick_col(cblk, head))
                p_t = jnp.where(causal, jnp.exp(s_t - lset_ref[hh:hh + 1, rs]), 0.0)
                ds_t = p_t * (_dot_nt(v_ref[0:nk, hs], do16) - delta_row)
                p16 = p_t.astype(BF16)
                ds16 = ds_t.astype(BF16)
                dv_acc[0:nk, hs] += _dot(p16, do16)
                dk_acc[0:nk, hs] += _dot(ds16, q) * ATT_SCALE
                dq_t[hs, rs] = _dot(k_t[hs, 0:nk], ds16)
                drow_ref[hh:hh + 1, rs] = jnp.sum(ds_t, axis=0, keepdims=True)
                dcol_ref[0:nk, hs] -= jnp.broadcast_to(jnp.sum(ds_t, axis=1, keepdims=True), (nk, HEAD_DIM))
        dk_ref[...] = dk_acc[...].astype(BF16)
        dv_ref[...] = dv_acc[...].astype(BF16)
        dq_ref[...] = (dq_t[...].T * ATT_SCALE).astype(BF16)

    pair = lambda p: (0, p)
    return pl.pallas_call(
        body, name="attn_bwd", grid=(npair,),
        in_specs=[pl.BlockSpec((tp, LANES), pair),
                  pl.BlockSpec((tp, LANES), lambda p: (0, npair + p)),
                  pl.BlockSpec((tp, LANES), lambda p: (0, 2 * npair + p)),
                  pl.BlockSpec((tp, LANES), pair),
                  pl.BlockSpec((tp, LANES), pair),
                  pl.BlockSpec((None, SUBLANES, tp), lambda p: (p, 0, 0)),
                  pl.BlockSpec((tp, LANES), lambda p: (0, 0)),
                  pl.BlockSpec((SUBLANES, tp), lambda p: (0, 0))],
        out_specs=[pl.BlockSpec((tp, LANES), pair), pl.BlockSpec((tp, LANES), pair),
                   pl.BlockSpec((tp, LANES), pair),
                   pl.BlockSpec((None, SUBLANES, tp), lambda p: (p, 0, 0)),
                   pl.BlockSpec((tp, LANES), pair)],
        out_shape=[jax.ShapeDtypeStruct((tp, aw), BF16), jax.ShapeDtypeStruct((tp, aw), BF16),
                   jax.ShapeDtypeStruct((tp, aw), BF16),
                   jax.ShapeDtypeStruct((npair, SUBLANES, tp), F32),
                   jax.ShapeDtypeStruct((tp, aw), F32)],
        scratch_shapes=[pltpu.VMEM((tp, LANES), F32), pltpu.VMEM((tp, LANES), F32),
                        pltpu.VMEM((LANES, tp), F32)],
        compiler_params=_cparams("parallel"),
    )(qkv, qkv, qkv, do, o, lset, c, ct)


def fgate_bwd(dct8, drs, rest, bf_pad, fcol):
    tp = rest.shape[0]
    aw = drs.shape[1]
    nb = tp // ATT_BLOCK
    B = ATT_BLOCK

    def body(d_ref, drs_ref, f_ref, b_ref, dfl_ref, db_ref, pad_s):
        r_i = lax.broadcasted_iota(jnp.int32, (B, B), 0)
        c_i = lax.broadcasted_iota(jnp.int32, (B, B), 1)
        triu = (c_i >= r_i).astype(BF16)
        sel = (lax.broadcasted_iota(jnp.int32, (aw, LANES), 0)
               == HEAD_DIM * lax.broadcasted_iota(jnp.int32, (aw, LANES), 1)).astype(BF16)
        carry = jnp.zeros((1, LANES), F32)
        db = jnp.zeros((1, LANES), F32)
        pad_s[...] = jnp.zeros_like(pad_s)
        for i in range(nb - 1, -1, -1):
            sl = slice(i * B, (i + 1) * B)
            pad_s[0:SUBLANES, :] = d_ref[:, sl]
            dc = pad_s[...].T + _dot_split3(drs_ref[sl, :], sel)
            rc = _split3_dot(triu, dc)
            dlf = rc + carry
            carry = carry + rc[0:1, :]
            dfl = dlf * _sigmoid(-(f_ref[sl, :] + b_ref[...]))
            dfl_ref[sl, :] = dfl.astype(BF16)
            db = db + jnp.sum(dfl, axis=0, keepdims=True)
        db_ref[...] = db

    return pl.pallas_call(
        body, name="fgate_bwd", grid=(1,),
        in_specs=[pl.BlockSpec((SUBLANES, tp), lambda i: (0, 0)),
                  pl.BlockSpec((tp, aw), lambda i: (0, 0)),
                  pl.BlockSpec((tp, LANES), lambda i: (0, fcol)),
                  pl.BlockSpec((1, LANES), lambda i: (0, 0))],
        out_specs=[pl.BlockSpec((tp, LANES), lambda i: (0, 0)),
                   pl.BlockSpec((1, LANES), lambda i: (0, 0))],
        out_shape=[jax.ShapeDtypeStruct((tp, LANES), BF16), jax.ShapeDtypeStruct((1, LANES), F32)],
        scratch_shapes=[pltpu.VMEM((B, B), F32)],
        compiler_params=_cparams("arbitrary"),
    )(dct8, drs, rest, bf_pad)


def in_proj_bwd(dh2, parts, w_in_t, wrest_t, h, g1, tm):
    tp, d = h.shape
    dq, dk, dv, dxr, dyr, dfl = parts
    aw, rw = dq.shape[1], dxr.shape[1]

    def body(dh2_ref, dq_ref, dk_ref, dv_ref, dxr_ref, dyr_ref, dfl_ref, wq_ref, wr_ref, h_ref, g_ref,
             dh_ref, dg_ref):
        i = pl.program_id(0)
        dz = _dot(dq_ref[...], wq_ref[0:aw, :])
        dz += _dot(dk_ref[...], wq_ref[aw:2 * aw, :])
        dz += _dot(dv_ref[...], wq_ref[2 * aw:3 * aw, :])
        dz += _dot(dxr_ref[...], wr_ref[0:rw, :])
        dz += _dot(dyr_ref[...], wr_ref[rw:2 * rw, :])
        dz += _dot(dfl_ref[...], wr_ref[2 * rw:2 * rw + LANES, :])
        dx, dg = _rms_bwd(h_ref[...], g_ref[...], dz)
        dh_ref[...] = dh2_ref[...] + dx
        _accumulate(dg_ref, dg, i == 0)

    row = lambda i: (i, 0)
    fix = lambda i: (0, 0)
    return pl.pallas_call(
        body, name="in_proj_bwd", grid=(tp // tm,),
        in_specs=[pl.BlockSpec((tm, d), row),
                  pl.BlockSpec((tm, aw), row), pl.BlockSpec((tm, aw), row), pl.BlockSpec((tm, aw), row),
                  pl.BlockSpec((tm, rw), row), pl.BlockSpec((tm, rw), row), pl.BlockSpec((tm, LANES), row),
                  pl.BlockSpec((3 * aw, d), fix), pl.BlockSpec(wrest_t.shape, fix),
                  pl.BlockSpec((tm, d), row), pl.BlockSpec((1, d), fix)],
        out_specs=[pl.BlockSpec((tm, d), row), pl.BlockSpec((1, d), fix)],
        out_shape=[jax.ShapeDtypeStruct((tp, d), F32), jax.ShapeDtypeStruct((1, d), F32)],
        compiler_params=_cparams("arbitrary"),
    )(dh2, dq, dk, dv, dxr, dyr, dfl, w_in_t, wrest_t, h, g1)


def dw_in_t(z, parts, nh, tr):
    tp, d = z.shape
    dq, dk, dv, dxr, dyr, dfl = parts
    aw, rw = dq.shape[1], dxr.shape[1]
    d_in = 3 * aw + nh + 2 * rw
    nr = tp // tr
    offs = [(0, aw), (aw, aw), (2 * aw, aw), (3 * aw + nh, rw), (3 * aw + nh + rw, rw)]

    def body(z_ref, dq_ref, dk_ref, dv_ref, dxr_ref, dyr_ref, dfl_ref, o_ref, acc):
        r = pl.program_id(0)

        @pl.when(r == 0)
        def _():
            acc[...] = jnp.zeros_like(acc)

        zt = z_ref[...]
        for (o, n), ref in zip(offs, (dq_ref, dk_ref, dv_ref, dxr_ref, dyr_ref)):
            acc[o:o + n, :] += _dot_tn(ref[...], zt)
        acc[3 * aw:3 * aw + nh, :] += _dot_tn(dfl_ref[...], zt)[0:nh, :]

        @pl.when(r == nr - 1)
        def _():
            o_ref[...] = acc[...].astype(BF16)

    row = lambda r: (r, 0)
    return pl.pallas_call(
        body, name="dw_in", grid=(nr,),
        in_specs=[pl.BlockSpec((tr, d), row),
                  pl.BlockSpec((tr, aw), row), pl.BlockSpec((tr, aw), row), pl.BlockSpec((tr, aw), row),
                  pl.BlockSpec((tr, rw), row), pl.BlockSpec((tr, rw), row), pl.BlockSpec((tr, LANES), row)],
        out_specs=pl.BlockSpec((d_in, d), lambda r: (0, 0)),
        out_shape=jax.ShapeDtypeStruct((d_in, d), BF16),
        scratch_shapes=[pltpu.VMEM((d_in, d), F32)],
        compiler_params=_cparams("arbitrary"),
    )(z, dq, dk, dv, dxr, dyr, dfl)


def _place():
    return lax.axis_index("x"), lax.axis_index("y"), lax.axis_index("c")


HBM = pl.BlockSpec(memory_space=pltpu.HBM)
SEM = pl.BlockSpec(memory_space=pltpu.SEMAPHORE)
EFFECT = pltpu.SideEffectType.DATAFLOW_SIDE_EFFECTING


def _in_hbm(a):
    return pltpu.with_memory_space_constraint(a, pltpu.HBM)


def _as_list(a):
    return list(a) if isinstance(a, (list, tuple)) else [a]


def _gather_targets(x, y, c):
    return [(x, y, 1 - c), (1 - x, y, c), (x, 1 - y, c), (1 - x, 1 - y, c)]


def _slot(t):
    return 4 * t[0] + 2 * t[1] + t[2]


def gather_start(groups, name):
    flat = [a for g in groups for a in g]
    n = len(flat)
    ng = len(groups)
    lands = [lax.empty((N_DEV,) + a.shape, a.dtype) for a in flat]

    def body(*refs):
        src, land = refs[:n], refs[n:2 * n]
        sems = refs[2 * n:2 * n + 2 * ng]
        token = refs[-1]
        x, y, c = _place()
        me = 4 * x + 2 * y + c
        i = 0
        for gi, g in enumerate(groups):
            for a in range(len(g)):
                for k, t in enumerate(_gather_targets(x, y, c)):
                    pltpu.make_async_remote_copy(
                        src_ref=src[i], dst_ref=land[i].at[me],
                        send_sem=sems[2 * gi].at[4 * a + k], recv_sem=sems[2 * gi + 1].at[4 * a + k],
                        device_id=t, device_id_type=MESH).start()
                i += 1
        token[...] = jnp.zeros_like(token)

    sem_shapes = []
    for g in groups:
        sem_shapes += [pltpu.SemaphoreType.DMA((4 * len(g),)), pltpu.SemaphoreType.DMA((4 * len(g),))]
    out = pl.pallas_call(
        body, name=name,
        out_shape=sem_shapes + [pltpu.HBM(a.shape, a.dtype) for a in flat + lands]
        + [jax.ShapeDtypeStruct((SUBLANES, LANES), F32)],
        in_specs=[HBM] * (2 * n),
        out_specs=[SEM] * (2 * ng) + [HBM] * (2 * n) + [pl.BlockSpec(memory_space=pltpu.VMEM)],
        input_output_aliases={i: 2 * ng + i for i in range(2 * n)},
        compiler_params=pltpu.CompilerParams(has_side_effects=EFFECT),
    )(*[_in_hbm(a) for a in flat + lands])
    sems = out[:2 * ng]
    thru = out[2 * ng:2 * ng + 2 * n]
    srcs_t, lands_t = thru[:n], thru[n:]
    res, i = [], 0
    for gi, g in enumerate(groups):
        res.append((sems[2 * gi], sems[2 * gi + 1], srcs_t[i:i + len(g)], lands_t[i:i + len(g)]))
        i += len(g)
    return res, out[-1]


def gather_wait(send, recv, srcs, lands, after, name):
    n = len(srcs)

    def body(*refs):
        src, land = refs[:n], refs[n:2 * n]
        send_sem, recv_sem = refs[2 * n], refs[2 * n + 1]
        x, y, c = _place()
        for a in range(n):
            for k, t in enumerate(_gather_targets(x, y, c)):
                cp = pltpu.make_async_remote_copy(
                    src_ref=src[a], dst_ref=land[a].at[_slot(t)],
                    send_sem=send_sem.at[4 * a + k], recv_sem=recv_sem.at[4 * a + k],
                    device_id=t, device_id_type=MESH)
                cp.wait_send()
                cp.wait_recv()

    out = pl.pallas_call(
        body, name=name,
        out_shape=[pltpu.HBM(a.shape, a.dtype) for a in list(srcs) + list(lands)],
        in_specs=[HBM] * (2 * n) + [SEM, SEM] + [ANY] * len(_as_list(after)),
        out_specs=[HBM] * (2 * n),
        input_output_aliases={i: i for i in range(2 * n)},
        compiler_params=pltpu.CompilerParams(has_side_effects=EFFECT),
    )(*srcs, *lands, send, recv, *_as_list(after))
    return out[:n], out[n:]


def forward_start(lands, name):
    n = len(lands)

    def body(*refs):
        land = refs[:n]
        send_sem, recv_sem = refs[n], refs[n + 1]
        token = refs[-1]
        x, y, c = _place()
        for a in range(n):
            for j, chip in enumerate([(1 - x, y), (x, 1 - y), (1 - x, 1 - y)]):
                blk = land[a].at[_slot((*chip, c))]
                pltpu.make_async_remote_copy(src_ref=blk, dst_ref=blk, send_sem=send_sem.at[3 * a + j],
                                             recv_sem=recv_sem.at[3 * a + j], device_id=(x, y, 1 - c),
                                             device_id_type=MESH).start()
        token[...] = jnp.zeros_like(token)

    out = pl.pallas_call(
        body, name=name,
        out_shape=[pltpu.SemaphoreType.DMA((3 * n,)), pltpu.SemaphoreType.DMA((3 * n,))]
        + [pltpu.HBM(a.shape, a.dtype) for a in lands] + [jax.ShapeDtypeStruct((SUBLANES, LANES), F32)],
        in_specs=[HBM] * n,
        out_specs=[SEM, SEM] + [HBM] * n + [pl.BlockSpec(memory_space=pltpu.VMEM)],
        input_output_aliases={i: 2 + i for i in range(n)},
        compiler_params=pltpu.CompilerParams(has_side_effects=EFFECT),
    )(*[_in_hbm(a) for a in lands])
    return out[0], out[1], out[2:2 + n], out[-1][0, 0]


def forward_wait(send, recv, lands, after, name):
    n = len(lands)

    def body(*refs):
        land = refs[:n]
        send_sem, recv_sem = refs[n], refs[n + 1]
        x, y, c = _place()
        for a in range(n):
            for j, chip in enumerate([(1 - x, y), (x, 1 - y), (1 - x, 1 - y)]):
                cp = pltpu.make_async_remote_copy(
                    src_ref=land[a].at[_slot((*chip, c))], dst_ref=land[a].at[_slot((*chip, 1 - c))],
                    send_sem=send_sem.at[3 * a + j], recv_sem=recv_sem.at[3 * a + j],
                    device_id=(x, y, 1 - c), device_id_type=MESH)
                cp.wait_send()
                cp.wait_recv()

    return pl.pallas_call(
        body, name=name,
        out_shape=[pltpu.HBM(a.shape, a.dtype) for a in lands],
        in_specs=[HBM] * n + [SEM, SEM, ANY],
        out_specs=[HBM] * n,
        input_output_aliases={i: i for i in range(n)},
        compiler_params=pltpu.CompilerParams(has_side_effects=EFFECT),
    )(*lands, send, recv, after)


def _relations():
    return [(dx, dy, dc) for dx in (0, 1) for dy in (0, 1) for dc in (0, 1) if dx + dy + dc]


def _peer(x, y, c, rel):
    return ((1 - x) if rel[0] else x, (1 - y) if rel[1] else y, (1 - c) if rel[2] else c)


def exchange_start(srcs, lands, layer, name):
    n = len(srcs)

    def body(*refs):
        src, land = refs[:n], refs[n:2 * n]
        send_sem, recv_sem = refs[2 * n], refs[2 * n + 1]
        token = refs[-1]
        x, y, c = _place()
        me = 4 * x + 2 * y + c
        for k, rel in enumerate(_relations()):
            peer = _peer(x, y, c, rel)
            for a in range(n):
                pltpu.make_async_remote_copy(
                    src_ref=src[a] if layer is None else src[a].at[_slot(peer)],
                    dst_ref=land[a].at[me] if layer is None else land[a].at[me, layer],
                    send_sem=send_sem.at[7 * a + k], recv_sem=recv_sem.at[7 * a + k],
                    device_id=peer, device_id_type=MESH).start()
        token[...] = jnp.zeros_like(token)

    out = pl.pallas_call(
        body, name=name,
        out_shape=[pltpu.SemaphoreType.DMA((7 * n,)), pltpu.SemaphoreType.DMA((7 * n,))]
        + [pltpu.HBM(a.shape, a.dtype) for a in list(srcs) + list(lands)]
        + [jax.ShapeDtypeStruct((SUBLANES, LANES), F32)],
        in_specs=[HBM] * (2 * n),
        out_specs=[SEM, SEM] + [HBM] * (2 * n) + [pl.BlockSpec(memory_space=pltpu.VMEM)],
        input_output_aliases={i: 2 + i for i in range(2 * n)},
        compiler_params=pltpu.CompilerParams(has_side_effects=EFFECT),
    )(*[_in_hbm(a) for a in list(srcs) + list(lands)])
    return out[0], out[1], out[2:2 + n], out[2 + n:2 + 2 * n], out[-1][0, 0]


def exchange_wait(send, recv, srcs, lands, after, layer, name):
    n = len(srcs)

    def body(*refs):
        src, land = refs[:n], refs[n:2 * n]
        send_sem, recv_sem = refs[2 * n], refs[2 * n + 1]
        x, y, c = _place()
        for k, rel in enumerate(_relations()):
            peer = _peer(x, y, c, rel)
            for a in range(n):
                cp = pltpu.make_async_remote_copy(
                    src_ref=src[a] if layer is None else src[a].at[_slot(peer)],
                    dst_ref=land[a].at[_slot(peer)] if layer is None else land[a].at[_slot(peer), layer],
                    send_sem=send_sem.at[7 * a + k], recv_sem=recv_sem.at[7 * a + k],
                    device_id=peer, device_id_type=MESH)
                cp.wait_send()
                cp.wait_recv()

    out = pl.pallas_call(
        body, name=name,
        out_shape=[pltpu.HBM(a.shape, a.dtype) for a in list(srcs) + list(lands)],
        in_specs=[HBM] * (2 * n) + [SEM, SEM] + [ANY] * len(_as_list(after)),
        out_specs=[HBM] * (2 * n),
        input_output_aliases={i: i for i in range(2 * n)},
        compiler_params=pltpu.CompilerParams(has_side_effects=EFFECT),
    )(*srcs, *lands, send, recv, *_as_list(after))
    return out[:n], out[n:]


def _adamw_math(g, w, m, v):
    m = ADAM_B1 * m + (1.0 - ADAM_B1) * g
    v = ADAM_B2 * v + (1.0 - ADAM_B2) * (g * g)
    m_hat = m / (1.0 - ADAM_B1 ** ADAM_STEP)
    v_hat = v / (1.0 - ADAM_B2 ** ADAM_STEP)
    delta = -ADAM_LR * (m_hat / (jnp.sqrt(v_hat) + ADAM_EPS) + ADAM_WD * w)
    return delta, m, v


def sum_adamw(parts, w, m, v, tr, name):
    npart, rows, cols = parts.shape

    def body(p_ref, w_ref, m_ref, v_ref, g_ref, d_ref, nm_ref, nv_ref):
        g = p_ref[0].astype(F32)
        for p in range(1, npart):
            g = g + p_ref[p].astype(F32)
        delta, nm, nv = _adamw_math(g, w_ref[...], m_ref[...], v_ref[...])
        g_ref[...] = g
        d_ref[...] = delta
        nm_ref[...] = nm
        nv_ref[...] = nv

    blk = pl.BlockSpec((tr, cols), lambda i: (i, 0))
    return pl.pallas_call(
        body, name=name, grid=(rows // tr,),
        in_specs=[pl.BlockSpec((npart, tr, cols), lambda i: (0, i, 0)), blk, blk, blk],
        out_specs=[blk] * 4,
        out_shape=[jax.ShapeDtypeStruct((rows, cols), F32)] * 4,
        compiler_params=_cparams("parallel"),
    )(parts, w, m, v)


def sum_adamw_t(parts, w, m, v, name):
    npart, nl, rows, cols = parts.shape

    def body(p_ref, w_ref, m_ref, v_ref, g_ref, d_ref, nm_ref, nv_ref):
        g = p_ref[0].astype(F32)
        for p in range(1, npart):
            g = g + p_ref[p].astype(F32)
        delta, nm, nv = _adamw_math(g, w_ref[...], m_ref[...], v_ref[...])
        g_ref[...] = g
        d_ref[...] = delta
        nm_ref[...] = nm
        nv_ref[...] = nv

    blk = pl.BlockSpec((None, rows, cols), lambda l: (l, 0, 0))
    return pl.pallas_call(
        body, name=name, grid=(nl,),
        in_specs=[pl.BlockSpec((npart, None, rows, cols), lambda l: (0, l, 0, 0)), blk, blk, blk],
        out_specs=[blk] * 4,
        out_shape=[jax.ShapeDtypeStruct((nl, rows, cols), F32)] * 4,
        compiler_params=_cparams("parallel"),
    )(parts, w, m, v)


def adamw_group(gs, ws, ms, vs, name):
    n = len(gs)

    def body(*refs):
        g, w, m, v, outs = refs[:n], refs[n:2 * n], refs[2 * n:3 * n], refs[3 * n:4 * n], refs[4 * n:]
        for i in range(n):
            delta, nm, nv = _adamw_math(g[i][...], w[i][...], m[i][...], v[i][...])
            outs[i][...] = delta
            outs[n + i][...] = nm
            outs[2 * n + i][...] = nv

    vmem = pl.BlockSpec(memory_space=pltpu.VMEM)
    out = pl.pallas_call(
        body, name=name,
        in_specs=[vmem] * (4 * n), out_specs=[vmem] * (3 * n),
        out_shape=[jax.ShapeDtypeStruct(a.shape, F32) for a in list(ws) * 3],
        compiler_params=_cparams(),
    )(*gs, *ws, *ms, *vs)
    return out[:n], out[n:2 * n], out[2 * n:]


def sum_parts(parts, name):
    npart, rows, cols = parts.shape

    def body(p_ref, g_ref):
        g = p_ref[0].astype(F32)
        for p in range(1, npart):
            g = g + p_ref[p].astype(F32)
        g_ref[...] = g

    return pl.pallas_call(
        body, name=name, grid=(1,),
        in_specs=[pl.BlockSpec((npart, rows, cols), lambda i: (0, 0, 0))],
        out_specs=pl.BlockSpec((rows, cols), lambda i: (0, 0)),
        out_shape=jax.ShapeDtypeStruct((rows, cols), F32),
        compiler_params=_cparams("arbitrary"),
    )(parts)


def _round_up(n, m):
    return (n + m - 1) // m * m


def _block_diag_pairs(w):
    nb, b, _ = w.shape
    per = LANES // b
    ng = nb // per
    w = w.reshape(ng, per, b, b)
    eye = jnp.eye(per, dtype=w.dtype)
    out = jnp.einsum('gpij,pq->gpiqj', w, eye).reshape(ng, LANES, LANES)
    return out.astype(BF16)


def _block_diag_extract(g, b):
    ng = g.shape[0]
    per = LANES // b
    g = g.reshape(ng, per, b, per, b)
    idx = jnp.arange(per)
    return g[:, idx, :, idx, :].transpose(1, 0, 2, 3).reshape(ng * per, b, b)


def _tiles(v):
    v = v.reshape(-1)
    n = _round_up(v.shape[0], SUBLANES * LANES)
    return jnp.pad(v, (0, n - v.shape[0])).reshape(-1, LANES)


SMALL = ['attn_norm_g', 'b_f', 'conv_w', 'conv_b', 'w_gate_a', 'b_gate_a', 'w_gate_x', 'b_gate_x',
         'lru_L', 'attn_out_g', 'rec_out_g', 'mlp_norm_g', 'final_g', 'meta']


def _pack(d):
    return jnp.concatenate([_tiles(d[n]) for n in SMALL], axis=0)


def _unpack(vec, shapes):
    out, r = {}, 0
    for n in SMALL:
        size = math.prod(shapes[n])
        nr = _round_up(size, SUBLANES * LANES) // LANES
        out[n] = vec[r:r + nr].reshape(-1)[:size].reshape(shapes[n])
        r += nr
    return out


def _row_tile(tp):
    return tp // 4 if (tp // 4) % 16 == 0 else tp


def local_step(x, tgt, meta, small, hooks):
    s, d = x.shape
    t_real = s + N_META
    tp = _round_up(t_real, ATT_BLOCK)
    depth = small['attn_norm_g'].shape[0]
    nh = small['b_f'].shape[1]
    rw = small['conv_b'].shape[1]
    blk = small['w_gate_a'].shape[2]
    tm = _row_tile(tp)
    tm2 = tp // 2
    fcol = 2 * rw // LANES

    h = jnp.concatenate([meta, x, jnp.zeros((tp - t_real, d), F32)], axis=0)
    tgt_p = jnp.pad(tgt, ((N_META, tp - t_real), (0, 0)))
    row = lambda v: v.reshape(1, -1)
    bf_pad = jnp.pad(small['b_f'], ((0, 0), (0, LANES - nh)))

    saved = []
    for l in range(depth):
        w_in_t, wrest_t, wout, tok_w = hooks.mixer_weights(l, h)
        wga = _block_diag_pairs(small['w_gate_a'][l])
        wgx = _block_diag_pairs(small['w_gate_x'][l])
        z, qkv, rest = in_proj(h, row(small['attn_norm_g'][l]) + tok_w, w_in_t, wrest_t, 3 * nh * HEAD_DIM, tm)
        c, ct = fgate_fwd(rest, bf_pad[l:l + 1], fcol)
        o, lset = attn_fwd(qkv, c, ct, nh)
        rec, hr, xc = rec_fwd(rest, small['conv_w'][l], row(small['conv_b'][l]), wga, row(small['b_gate_a'][l]),
                              wgx, row(small['b_gate_x'][l]), row(small['lru_L'][l]), rw)
        gup, gdown, tok_w = hooks.mlp_weights(l, rec)
        h2, mix, z2 = out_proj(h, o, rec, row(small['attn_out_g'][l]), row(small['rec_out_g'][l]), wout,
                               row(small['mlp_norm_g'][l]) + tok_w, tm)
        u, h3 = mlp_fwd(z2, h2, gup, gdown, tm2)
        saved.append(dict(h=h, z=z, qkv=qkv, rest=rest, c=c, ct=ct, o=o, lset=lset, rec=rec, hr=hr, xc=xc,
                          h2=h2, mix=mix, z2=z2, u=u, wga=wga, wgx=wgx,
                          w_in_t=w_in_t, wrest_t=wrest_t, wout=wout, gup=gup, gdown=gdown))
        h = h3

    dh, dgf, loss = loss_head(h, row(small['final_g']), tgt_p, t_real, tm)

    gs = {n: [None] * depth for n in SMALL if n not in ('final_g', 'meta')}
    tok = jnp.zeros((), F32)
    for l in reversed(range(depth)):
        sv = saved[l]
        gup, gdown = sv['gup'], sv['gdown']
        tf = gup.shape[2]
        dup, dhb = mlp_bwd_hidden(dh, sv['u'], gdown, tm2)
        dh2, dg2 = mlp_bwd_out(dup, dh, sv['h2'], row(small['mlp_norm_g'][l]) + tok, gup, tm)
        gs['mlp_norm_g'][l] = dg2[0]
        do, drec, dga, dgr = out_proj_bwd(dh2, sv['o'], sv['rec'], row(small['attn_out_g'][l]),
                                          row(small['rec_out_g'][l]), sv['wout'], tm)
        gs['attn_out_g'][l] = dga[0]
        gs['rec_out_g'][l] = dgr[0]
        dw_down, dw_up = dw_mlp(sv['u'], dhb, sv['z2'], dup, tf)
        blocks = dict(
            w_down=dw_down, w_up=dw_up,
            w_out=mm_tn(sv['mix'], dh2, tk=d, tn=d // 2, out_dtype=BF16,
                        name="dw_out").reshape(N_DEV, d // N_DEV, d))
        tok = hooks.grads_ready(l, 'mlp', blocks)
        dxr, dyr, dwga, dwgx, vec = rec_bwd(drec, sv['hr'], sv['xc'], sv['rest'], small['conv_w'][l],
                                            row(small['conv_b'][l]) + tok, sv['wga'], row(small['b_gate_a'][l]),
                                            sv['wgx'], row(small['b_gate_x'][l]), row(small['lru_L'][l]), rw)
        gs['w_gate_a'][l] = _block_diag_extract(dwga, blk)
        gs['w_gate_x'][l] = _block_diag_extract(dwgx, blk)
        vec = vec.transpose(1, 0, 2).reshape(SUBLANES, rw)
        gs['conv_w'][l] = vec[0:CONV_WIDTH]
        gs['conv_b'][l] = vec[4]
        gs['b_gate_a'][l] = vec[5]
        gs['b_gate_x'][l] = vec[6]
        gs['lru_L'][l] = vec[7]
        dq, dk, dv, drow, dcol = attn_bwd(sv['qkv'], do, sv['o'], sv['lset'], sv['c'], sv['ct'] + tok, nh)
        drow8 = drow[:, 0:2, :].reshape(nh, tp)
        if nh < SUBLANES:
            drow8 = jnp.pad(drow8, ((0, SUBLANES - nh), (0, 0)))
        dfl, dbf = fgate_bwd(drow8, dcol, sv['rest'], bf_pad[l:l + 1], fcol)
        gs['b_f'][l] = dbf[0, 0:nh]
        parts = (dq, dk, dv, dxr, dyr, dfl)
        dh, dg1 = in_proj_bwd(dh2, parts, sv['w_in_t'], sv['wrest_t'], sv['h'], row(small['attn_norm_g'][l]), tm)
        gs['attn_norm_g'][l] = dg1[0]
        tok = jnp.zeros((), F32)
        if l == 0:
            grads = {n: jnp.stack(v) for n, v in gs.items()}
            grads['final_g'] = dgf[0]
            grads['meta'] = dh[0:N_META]
            tok = hooks.small_ready(grads)
        dw_in = dw_in_t(sv['z'], parts, nh, tm2)
        dw_in = dw_in.reshape(N_DEV, dw_in.shape[0] // N_DEV, d) + tok.astype(BF16)
        tok = hooks.grads_ready(l, 'in', dict(w_in=dw_in))

    return loss[0, 0], dh, tok


def prep_weights(g_in, g_out, nh, rw):
    d = g_in.shape[2]
    w_in_t = g_in.reshape(-1, d)
    f0 = 3 * nh * HEAD_DIM
    wrest_t = jnp.concatenate([w_in_t[f0 + nh:f0 + nh + 2 * rw],
                               jnp.pad(w_in_t[f0:f0 + nh], ((0, LANES - nh), (0, 0)))], axis=0)
    return w_in_t, wrest_t, g_out.reshape(d, d)


BIG = ['w_in', 'w_out', 'w_up', 'w_down']
EXCHANGE_GROUPS = {'mlp': ['w_down', 'w_up', 'w_out'], 'in': ['w_in']}
WEIGHTS = ['meta', 'attn_norm_g', 'w_in', 'b_f', 'conv_w', 'conv_b', 'w_gate_a', 'b_gate_a', 'w_gate_x', 'b_gate_x',
           'lru_L', 'attn_out_g', 'rec_out_g', 'w_out', 'mlp_norm_g', 'w_up', 'w_down', 'final_g']


def _set_own(arr, own, me):
    return lax.dynamic_update_slice_in_dim(arr, own[None], me, axis=0)


class _Step:
    def __init__(self, w, nh, rw, me):
        self.w, self.nh, self.rw, self.me = w, nh, rw, me
        depth = w['w_in'].shape[0]
        first = [w['w_in_t'][:, 0, :].astype(BF16), w['w_out'][0].astype(BF16), w['meta'], w['conv_w']]
        self.pending, token = gather_start([first], "gather_start_0")
        zero = token[0, 0].astype(BF16)
        groups = [[w['w_up'][0].astype(BF16) + zero, w['w_down'][0].astype(BF16) + zero]]
        for l in range(1, depth):
            groups.append([w['w_in_t'][:, l, :].astype(BF16) + zero, w['w_out'][l].astype(BF16) + zero])
            groups.append([w['w_up'][l].astype(BF16) + zero, w['w_down'][l].astype(BF16) + zero])
        rest, _ = gather_start(groups, "gather_start_1")
        self.pending += rest
        self.first_after = rest[0][2][0]
        self.gathered = {}
        self.passing = {}
        self.token = jnp.zeros((), F32)
        self.lands = {n: lax.empty((N_DEV,) + w[n].shape, BF16) for n in BIG}
        din8, _, d = w['w_in_t'].shape
        self.lands['w_in'] = lax.empty((N_DEV, depth, din8, d), BF16)
        self.started = []
        self.small = None

    def _pass_on(self, gi, after):
        if gi < len(self.pending) and gi not in self.passing:
            send, recv, srcs, lands = self.pending[gi]
            srcs, lands = gather_wait(send, recv, srcs, lands, after, "gather_wait_%d" % gi)
            fsend, frecv, lands, token = forward_start(lands, "forward_start_%d" % gi)
            self.passing[gi] = (fsend, frecv, srcs, lands)
            self.token = token

    def group(self, gi, after):
        if gi not in self.gathered:
            self._pass_on(gi, after)
            fsend, frecv, srcs, lands = self.passing[gi]
            lands = forward_wait(fsend, frecv, lands, after, "forward_wait_%d" % gi)
            self.gathered[gi] = [_set_own(g, own, self.me) for g, own in zip(lands, srcs)]
            if gi >= 2:
                self._pass_on(gi + 1, lands[0])
        return self.gathered[gi]

    def mixer_weights(self, l, after):
        g = self.group(2 * l, after)
        return (*prep_weights(g[0], g[1], self.nh, self.rw), self.token)

    def mlp_weights(self, l, after):
        g = self.group(2 * l + 1, after)
        return g[0], g[1], self.token

    def grads_ready(self, l, group, blocks):
        names = EXCHANGE_GROUPS[group]
        send, recv, srcs, lands, token = exchange_start(
            [blocks[n] for n in names], [self.lands[n] for n in names], l, "exchange_start_%s_%d" % (group, l))
        for n, a in zip(names, lands):
            self.lands[n] = a
        self.started.append((l, group, send, recv, srcs))
        return token

    def small_ready(self, grads):
        self.small_shapes = {n: grads[n].shape for n in SMALL}
        packed = _pack(grads).astype(BF16)
        send, recv, srcs, lands, token = exchange_start(
            [packed], [lax.empty((N_DEV,) + packed.shape, BF16)], None, "small_start")
        self.small = (send, recv, srcs, lands)
        return token

    def small_sum(self, after):
        send, recv, srcs, lands = self.small
        srcs, lands = exchange_wait(send, recv, srcs, lands, after, None, "small_wait")
        parts = _set_own(lands[0], srcs[0], self.me)
        return _unpack(sum_parts(parts, "sum_small_grads"), self.small_shapes)

    def received(self, group, after):
        names = EXCHANGE_GROUPS[group]
        own = {n: [None] * self.w[n].shape[0] for n in names}
        for l, grp, send, recv, srcs in self.started:
            if grp != group:
                continue
            srcs, lands = exchange_wait(send, recv, srcs, [self.lands[n] for n in names], after, l,
                                        "exchange_wait_%s_%d" % (group, l))
            for n, a, sr in zip(names, lands, srcs):
                self.lands[n] = a
                own[n][l] = lax.dynamic_index_in_dim(sr, self.me, 0, keepdims=False)
        return {n: _set_own(self.lands[n], jnp.stack(own[n]), self.me) for n in names}


def kernel(x, meta, attn_norm_g, w_in, b_f, conv_w, conv_b, w_gate_a, b_gate_a, w_gate_x, b_gate_x, lru_L, attn_out_g, rec_out_g, w_out, mlp_norm_g, w_up, w_down, final_g, loss_target, m_meta, m_attn_norm_g, m_w_in, m_b_f, m_conv_w, m_conv_b, m_w_gate_a, m_b_gate_a, m_w_gate_x, m_b_gate_x, m_lru_L, m_attn_out_g, m_rec_out_g, m_w_out, m_mlp_norm_g, m_w_up, m_w_down, m_final_g, v_meta, v_attn_norm_g, v_w_in, v_b_f, v_conv_w, v_conv_b, v_w_gate_a, v_b_gate_a, v_w_gate_x, v_b_gate_x, v_lru_L, v_attn_out_g, v_rec_out_g, v_w_out, v_mlp_norm_g, v_w_up, v_w_down, v_final_g):
    w = dict(meta=meta, attn_norm_g=attn_norm_g, w_in=w_in, b_f=b_f, conv_w=conv_w, conv_b=conv_b,
             w_gate_a=w_gate_a, b_gate_a=b_gate_a, w_gate_x=w_gate_x, b_gate_x=b_gate_x, lru_L=lru_L,
             attn_out_g=attn_out_g, rec_out_g=rec_out_g, w_out=w_out, mlp_norm_g=mlp_norm_g, w_up=w_up,
             w_down=w_down, final_g=final_g)
    mo = dict(meta=m_meta, attn_norm_g=m_attn_norm_g, w_in=m_w_in, b_f=m_b_f, conv_w=m_conv_w, conv_b=m_conv_b,
              w_gate_a=m_w_gate_a, b_gate_a=m_b_gate_a, w_gate_x=m_w_gate_x, b_gate_x=m_b_gate_x, lru_L=m_lru_L,
              attn_out_g=m_attn_out_g, rec_out_g=m_rec_out_g, w_out=m_w_out, mlp_norm_g=m_mlp_norm_g,
              w_up=m_w_up, w_down=m_w_down, final_g=m_final_g)
    vo = dict(meta=v_meta, attn_norm_g=v_attn_norm_g, w_in=v_w_in, b_f=v_b_f, conv_w=v_conv_w, conv_b=v_conv_b,
              w_gate_a=v_w_gate_a, b_gate_a=v_b_gate_a, w_gate_x=v_w_gate_x, b_gate_x=v_b_gate_x, lru_L=v_lru_L,
              attn_out_g=v_attn_out_g, rec_out_g=v_rec_out_g, w_out=v_w_out, mlp_norm_g=v_mlp_norm_g,
              w_up=v_w_up, w_down=v_w_down, final_g=v_final_g)
    depth = w_in.shape[0]
    nh = b_f.shape[1]
    rw = conv_b.shape[1]
    me = 4 * lax.axis_index("x") + 2 * lax.axis_index("y") + lax.axis_index("c")

    w['w_in_t'] = jnp.transpose(w_in, (2, 0, 1))
    swap = lambda a: jnp.swapaxes(a, 1, 2)
    step = _Step(w, nh, rw, me)
    g0 = step.group(0, step.first_after)
    meta_full = g0[2].transpose(1, 0, 2).reshape(N_META, -1)
    conv_full = g0[3].transpose(1, 2, 0, 3).reshape(depth, CONV_WIDTH, rw)
    small = {n: w[n] for n in SMALL}
    small['conv_w'] = conv_full

    loss_part, dh0, tok = local_step(x[0], loss_target[0], meta_full, small, step)
    loss = lax.psum(loss_part, ("x", "y", "c"))
    grad_x = dh0[N_META:N_META + x.shape[1]][None]

    out_g, out_d, out_m, out_v = {}, {}, {}, {}

    def update_big(group, after):
        for n, r in step.received(group, after).items():
            if n == 'w_in':
                out = sum_adamw_t(r, swap(w[n]), swap(mo[n]), swap(vo[n]), "adamw_w_in")
                out = [swap(a) for a in out]
            else:
                shp = w[n].shape
                rows, cols = shp[0] * shp[1], shp[2]
                tr = min(512 if cols <= 512 else 256, rows)
                out = sum_adamw(r.reshape(N_DEV, rows, cols), w[n].reshape(rows, cols), mo[n].reshape(rows, cols),
                                vo[n].reshape(rows, cols), tr, "adamw_" + n)
                out = [a.reshape(shp) for a in out]
            out_g[n], out_d[n], out_m[n], out_v[n] = out
            after = out[0]
        return after

    update_big('mlp', step.started[-1][4][0])

    gsum = step.small_sum([out_g[n] for n in EXCHANGE_GROUPS['mlp']])
    gsum['meta'] = lax.dynamic_slice_in_dim(gsum['meta'], me * meta.shape[1], meta.shape[1], axis=1)
    gsum['conv_w'] = lax.dynamic_slice_in_dim(gsum['conv_w'], me * conv_w.shape[2], conv_w.shape[2], axis=2)
    as2d = lambda a: a.reshape(-1, a.shape[-1])
    deltas, new_m, new_v = adamw_group([as2d(gsum[n]) for n in SMALL], [as2d(w[n]) for n in SMALL],
                                       [as2d(mo[n]) for n in SMALL], [as2d(vo[n]) for n in SMALL], "adamw_small")
    for i, n in enumerate(SMALL):
        out_g[n] = gsum[n]
        out_d[n], out_m[n], out_v[n] = [a[i].reshape(w[n].shape) for a in (deltas, new_m, new_v)]

    update_big('in', deltas[0])

    return (loss, grad_x, *[out_g[n] for n in WEIGHTS], *[out_d[n] for n in WEIGHTS],
            *[out_m[n] for n in WEIGHTS], *[out_v[n] for n in WEIGHTS])
```

```python
import functools
import math

import jax
import jax.numpy as jnp
from jax import lax
from jax.experimental import pallas as pl
from jax.experimental.pallas import tpu as pltpu

F32 = jnp.float32
BF16 = jnp.bfloat16

N_DEV = 8
N_META = 16
HEAD_DIM = 64
CONV_WIDTH = 4
RG_C = 8.0
NORM_EPS = 1e-6
LANES = 128
SUBLANES = 8
ATT_BLOCK = 128
ATT_TQ = 512
NEG_BIG = -1e30
ATT_SCALE = 1.0 / math.sqrt(HEAD_DIM)

ADAM_LR = 0.001
ADAM_B1 = 0.9
ADAM_B2 = 0.999
ADAM_EPS = 1e-08
ADAM_WD = 0.01
ADAM_STEP = 10

VMEM_LIMIT_BYTES = 56 * 1024 * 1024
MESH = pl.DeviceIdType.MESH
ANY = pl.BlockSpec(memory_space=pl.ANY)


def _cparams(*sem):
    return pltpu.CompilerParams(dimension_semantics=sem if sem else None,
                                vmem_limit_bytes=VMEM_LIMIT_BYTES)


def _dot(a, b):
    return jnp.dot(a, b, preferred_element_type=F32)


def _dot_nt(a, b):
    return lax.dot_general(a, b, (((1,), (1,)), ((), ())), preferred_element_type=F32)


def _dot_tn(a, b):
    return lax.dot_general(a, b, (((0,), (0,)), ((), ())), preferred_element_type=F32)


def _sigmoid(x):
    return 0.5 * (1.0 + jnp.tanh(0.5 * x))


def _log_sigmoid(x):
    return jnp.minimum(x, 0.0) - jnp.log(1.0 + jnp.exp(-jnp.abs(x)))


def _expm1(x):
    series = x * (1.0 + x * (0.5 + x * (1.0 / 6.0 + x * (1.0 / 24.0))))
    return jnp.where(jnp.abs(x) < 1e-2, series, jnp.exp(x) - 1.0)


_GELU_K = math.sqrt(2.0 / math.pi)
_GELU_C = 0.044715


def _gelu(x):
    t = jnp.tanh(_GELU_K * (x + _GELU_C * x * x * x))
    return 0.5 * x * (1.0 + t)


def _gelu_grad(x):
    t = jnp.tanh(_GELU_K * (x + _GELU_C * x * x * x))
    return 0.5 * (1.0 + t) + 0.5 * x * (1.0 - t * t) * _GELU_K * (1.0 + 3.0 * _GELU_C * x * x)


def _split3_dot(tri, x):
    hi = x.astype(BF16)
    r1 = x - hi.astype(F32)
    mid = r1.astype(BF16)
    lo = (r1 - mid.astype(F32)).astype(BF16)
    return _dot(tri, hi) + _dot(tri, mid) + _dot(tri, lo)


def _dot_split3(x, sel):
    hi = x.astype(BF16)
    r1 = x - hi.astype(F32)
    mid = r1.astype(BF16)
    lo = (r1 - mid.astype(F32)).astype(BF16)
    return _dot(hi, sel) + _dot(mid, sel) + _dot(lo, sel)


def _rms_fwd(x, g):
    r = lax.rsqrt(jnp.mean(x * x, axis=-1, keepdims=True) + NORM_EPS)
    return x * r * g


def _rms_bwd(x, g, dy):
    r = lax.rsqrt(jnp.mean(x * x, axis=-1, keepdims=True) + NORM_EPS)
    xn = x * r
    dxn = dy * g
    dx = r * (dxn - xn * jnp.mean(dxn * xn, axis=-1, keepdims=True))
    return dx, jnp.sum(dy * xn, axis=0, keepdims=True)


def _accumulate(ref, val, first):
    @pl.when(first)
    def _():
        ref[...] = val

    @pl.when(jnp.logical_not(first))
    def _():
        ref[...] += val


def in_proj(h, g1, w_in_t, wrest_t, nq, tm):
    tp, d = h.shape
    nr = wrest_t.shape[0]

    def body(h_ref, g_ref, wq_ref, wr_ref, z_ref, qkv_ref, rest_ref):
        z = _rms_fwd(h_ref[...], g_ref[...]).astype(BF16)
        z_ref[...] = z
        qkv_ref[...] = _dot_nt(z, wq_ref[...]).astype(BF16)
        rest_ref[...] = _dot_nt(z, wr_ref[...])

    return pl.pallas_call(
        body, name="in_proj", grid=(tp // tm,),
        in_specs=[pl.BlockSpec((tm, d), lambda i: (i, 0)),
                  pl.BlockSpec((1, d), lambda i: (0, 0)),
                  pl.BlockSpec((nq, d), lambda i: (0, 0)),
                  pl.BlockSpec((nr, d), lambda i: (0, 0))],
        out_specs=[pl.BlockSpec((tm, d), lambda i: (i, 0)),
                   pl.BlockSpec((tm, nq), lambda i: (i, 0)),
                   pl.BlockSpec((tm, nr), lambda i: (i, 0))],
        out_shape=[jax.ShapeDtypeStruct((tp, d), BF16),
                   jax.ShapeDtypeStruct((tp, nq), BF16),
                   jax.ShapeDtypeStruct((tp, nr), F32)],
        compiler_params=_cparams("parallel"),
    )(h, g1, w_in_t, wrest_t)


def fgate_fwd(rest, bf_pad, fcol):
    tp = rest.shape[0]
    nb = tp // ATT_BLOCK

    def body(f_ref, b_ref, c_ref, ct_ref):
        r_i = lax.broadcasted_iota(jnp.int32, (ATT_BLOCK, ATT_BLOCK), 0)
        c_i = lax.broadcasted_iota(jnp.int32, (ATT_BLOCK, ATT_BLOCK), 1)
        tri = (r_i >= c_i).astype(BF16)
        carry = jnp.zeros((1, LANES), F32)
        for i in range(nb):
            sl = slice(i * ATT_BLOCK, (i + 1) * ATT_BLOCK)
            lf = _log_sigmoid(f_ref[sl, :] + b_ref[...])
            cs = _split3_dot(tri, lf) + carry
            carry = cs[ATT_BLOCK - 1:ATT_BLOCK, :]
            c_ref[sl, :] = cs
            ct_ref[:, sl] = cs.T[0:SUBLANES, :]

    return pl.pallas_call(
        body, name="fgate_fwd", grid=(1,),
        in_specs=[pl.BlockSpec((tp, LANES), lambda i: (0, fcol)),
                  pl.BlockSpec((1, LANES), lambda i: (0, 0))],
        out_specs=[pl.BlockSpec((tp, LANES), lambda i: (0, 0)),
                   pl.BlockSpec((SUBLANES, tp), lambda i: (0, 0))],
        out_shape=[jax.ShapeDtypeStruct((tp, LANES), F32),
                   jax.ShapeDtypeStruct((SUBLANES, tp), F32)],
        compiler_params=_cparams("arbitrary"),
    )(rest, bf_pad)


def _pick_col(blk, head):
    lane = lax.broadcasted_iota(jnp.int32, blk.shape, 1)
    return jnp.sum(jnp.where(lane == head, blk, 0.0), axis=1, keepdims=True)


def _pick_row(blk, head):
    sub = lax.broadcasted_iota(jnp.int32, blk.shape, 0)
    return jnp.sum(jnp.where(sub == head, blk, 0.0), axis=0, keepdims=True)


def _att_tiles(tp):
    out, r0 = [], 0
    while r0 < tp:
        rows = min(ATT_TQ, tp - r0)
        out.append((r0, rows, r0 + rows))
        r0 += rows
    return out


def attn_fwd(qkv, c, ct, nh):
    tp = qkv.shape[0]
    npair = nh // 2
    tiles = _att_tiles(tp)

    def body(q_ref, k_ref, v_ref, c_ref, ct_ref, o_ref, lset_ref):
        p = pl.program_id(0)
        lset_ref[...] = jnp.zeros_like(lset_ref)
        for r0, nr, nk in tiles:
            rs = slice(r0, r0 + nr)
            causal = (r0 + lax.broadcasted_iota(jnp.int32, (nr, nk), 0)
                      >= lax.broadcasted_iota(jnp.int32, (nr, nk), 1))
            cblk = c_ref[rs, :]
            ctb = ct_ref[:, 0:nk]
            for hh in range(2):
                head = 2 * p + hh
                hs = slice(hh * HEAD_DIM, (hh + 1) * HEAD_DIM)
                q = q_ref[rs, hs] * ATT_SCALE
                s = _dot_nt(q, k_ref[0:nk, hs]) + (_pick_col(cblk, head) - _pick_row(ctb, head))
                s = jnp.where(causal, s, NEG_BIG)
                m = jnp.max(s, axis=1, keepdims=True)
                pm = jnp.exp(s - m)
                l = jnp.sum(pm, axis=1, keepdims=True)
                o_ref[rs, hs] = _dot(pm.astype(BF16), v_ref[0:nk, hs]) / l
                lse = m + jnp.log(l)
                lset_ref[hh:hh + 1, rs] = jnp.broadcast_to(lse, (nr, LANES)).T[0:1, :]

    pair = lambda p: (0, p)
    return pl.pallas_call(
        body, name="attn_fwd", grid=(npair,),
        in_specs=[pl.BlockSpec((tp, LANES), pair),
                  pl.BlockSpec((tp, LANES), lambda p: (0, npair + p)),
                  pl.BlockSpec((tp, LANES), lambda p: (0, 2 * npair + p)),
                  pl.BlockSpec((tp, LANES), lambda p: (0, 0)),
                  pl.BlockSpec((SUBLANES, tp), lambda p: (0, 0))],
        out_specs=[pl.BlockSpec((tp, LANES), pair),
                   pl.BlockSpec((None, SUBLANES, tp), lambda p: (p, 0, 0))],
        out_shape=[jax.ShapeDtypeStruct((tp, nh * HEAD_DIM), F32),
                   jax.ShapeDtypeStruct((npair, SUBLANES, tp), F32)],
        compiler_params=_cparams("parallel"),
    )(qkv, qkv, qkv, c, ct)


def _shift_down(x, k, n):
    if k == 0:
        return x
    rows = lax.broadcasted_iota(jnp.int32, x.shape, 0)
    return jnp.where(rows >= k, pltpu.roll(x, k, 0), 0.0)


def _shift_up(x, k, n):
    if k == 0:
        return x
    rows = lax.broadcasted_iota(jnp.int32, x.shape, 0)
    return jnp.where(rows < n - k, pltpu.roll(x, n - k, 0), 0.0)


def _conv_fwd(xr, cw_ref, cb_ref, n):
    xc = cw_ref[CONV_WIDTH - 1:CONV_WIDTH, :] * xr + cb_ref[...]
    for k in range(1, CONV_WIDTH):
        xc = xc + cw_ref[CONV_WIDTH - 1 - k:CONV_WIDTH - k, :] * _shift_down(xr, k, n)
    return xc


def _gates(xc, wga_ref, bga_ref, wgx_ref, bgx_ref, l_ref):
    xcb = xc.astype(BF16)
    r = _sigmoid(_dot(xcb, wga_ref[...]) + bga_ref[...])
    ig = _sigmoid(_dot(xcb, wgx_ref[...]) + bgx_ref[...])
    ls = _log_sigmoid(l_ref[...])
    log_a = RG_C * r * ls
    a = jnp.exp(log_a)
    mult = jnp.sqrt(-_expm1(2.0 * log_a))
    return xcb, r, ig, ls, log_a, a, mult


SCAN_UNROLL = 4


def _scan_rows(a_s, u_s, out_ref, n, reverse):
    nt = n // SUBLANES
    per = SCAN_UNROLL if nt % SCAN_UNROLL == 0 else 1
    row = lax.broadcasted_iota(jnp.int32, (SUBLANES, LANES), 0)
    last = 0 if reverse else SUBLANES - 1

    def tile_scan(a, u):
        for d in (1, 2, 4):
            if reverse:
                keep = row < SUBLANES - d
                sh = SUBLANES - d
            else:
                keep = row >= d
                sh = d
            a_sh = jnp.where(keep, pltpu.roll(a, sh, 0), 1.0)
            u_sh = jnp.where(keep, pltpu.roll(u, sh, 0), 0.0)
            u = a * u_sh + u
            a = a * a_sh
        return a, u

    def step(t, carry):
        tiles = []
        for k in range(per):
            tt = t * per + k
            if reverse:
                tt = nt - 1 - tt
            off = pl.multiple_of(tt * SUBLANES, SUBLANES)
            a, u = tile_scan(a_s[pl.ds(off, SUBLANES), :], u_s[pl.ds(off, SUBLANES), :])
            tiles.append((off, a, u))
        for off, a, u in tiles:
            out_ref[pl.ds(off, SUBLANES), :] = u + a * carry
            carry = u[last:last + 1, :] + a[last:last + 1, :] * carry
        return carry

    lax.fori_loop(0, nt // per, step, jnp.zeros((1, LANES), F32))


def rec_fwd(rest, convw, convb, wga, bga, wgx, bgx, lru, rw):
    tp = rest.shape[0]
    ng = rw // LANES

    def body(xr_ref, yr_ref, cw_ref, cb_ref, wga_ref, bga_ref, wgx_ref, bgx_ref, l_ref,
             rec_ref, hr_ref, xc_ref, a_s, u_s):
        xc = _conv_fwd(xr_ref[...], cw_ref, cb_ref, tp)
        xc_ref[...] = xc
        _, r, ig, ls, log_a, a, mult = _gates(xc, wga_ref, bga_ref, wgx_ref, bgx_ref, l_ref)
        a_s[...] = a
        u_s[...] = mult * ig * xc
        _scan_rows(a_s, u_s, hr_ref, tp, reverse=False)
        rec_ref[...] = hr_ref[...] * _gelu(yr_ref[...])

    col = lambda g: (0, g)
    vec = pl.BlockSpec((1, LANES), col)
    big = pl.BlockSpec((tp, LANES), col)
    return pl.pallas_call(
        body, name="rec_fwd", grid=(ng,),
        in_specs=[big, pl.BlockSpec((tp, LANES), lambda g: (0, ng + g)),
                  pl.BlockSpec((CONV_WIDTH, LANES), col), vec,
                  pl.BlockSpec((None, LANES, LANES), lambda g: (g, 0, 0)), vec,
                  pl.BlockSpec((None, LANES, LANES), lambda g: (g, 0, 0)), vec, vec],
        out_specs=[big, big, big],
        out_shape=[jax.ShapeDtypeStruct((tp, rw), F32)] * 3,
        scratch_shapes=[pltpu.VMEM((tp, LANES), F32), pltpu.VMEM((tp, LANES), F32)],
        compiler_params=_cparams("parallel"),
    )(rest, rest, convw, convb, wga, bga, wgx, bgx, lru)


def out_proj(h, o, rec, ga, gr, wout, g2, tm):
    tp, d = h.shape
    aw, rw = o.shape[1], rec.shape[1]

    def body(h_ref, o_ref, rec_ref, ga_ref, gr_ref, w_ref, g2_ref, h2_ref, mix_ref, z2_ref):
        mix_ref[:, 0:aw] = _rms_fwd(o_ref[...], ga_ref[...]).astype(BF16)
        mix_ref[:, aw:aw + rw] = _rms_fwd(rec_ref[...], gr_ref[...]).astype(BF16)
        h2 = h_ref[...] + _dot(mix_ref[...], w_ref[...])
        h2_ref[...] = h2
        z2_ref[...] = _rms_fwd(h2, g2_ref[...]).astype(BF16)

    row = lambda i: (i, 0)
    fix = lambda i: (0, 0)
    return pl.pallas_call(
        body, name="out_proj", grid=(tp // tm,),
        in_specs=[pl.BlockSpec((tm, d), row), pl.BlockSpec((tm, aw), row), pl.BlockSpec((tm, rw), row),
                  pl.BlockSpec((1, aw), fix), pl.BlockSpec((1, rw), fix),
                  pl.BlockSpec((d, d), fix), pl.BlockSpec((1, d), fix)],
        out_specs=[pl.BlockSpec((tm, d), row)] * 3,
        out_shape=[jax.ShapeDtypeStruct((tp, d), F32), jax.ShapeDtypeStruct((tp, d), BF16),
                   jax.ShapeDtypeStruct((tp, d), BF16)],
        compiler_params=_cparams("parallel"),
    )(h, o, rec, ga, gr, wout, g2)


MLP_BLOCKS = 2


def mlp_fwd(z2, h2, gup, gdown, tm):
    tp, d = h2.shape
    nf = gup.shape[0]
    tf = gup.shape[2]
    nb = MLP_BLOCKS if nf % MLP_BLOCKS == 0 else 1
    nj = nf // nb

    def body(z_ref, h_ref, wu_ref, wd_ref, u_ref, h3_ref, acc):
        j = pl.program_id(1)
        z = z_ref[...]
        part = None
        for b in range(nb):
            u = jnp.maximum(_dot(z, wu_ref[b]), 0.0)
            u_ref[:, b * tf:(b + 1) * tf] = u.astype(BF16)
            p = _dot((u * u).astype(BF16), wd_ref[b])
            part = p if part is None else part + p

        @pl.when(j == 0)
        def _():
            acc[...] = h_ref[...] + part

        @pl.when(j > 0)
        def _():
            acc[...] += part

        @pl.when(j == nj - 1)
        def _():
            h3_ref[...] = acc[...]

    return pl.pallas_call(
        body, name="mlp_fwd", grid=(tp // tm, nj),
        in_specs=[pl.BlockSpec((tm, d), lambda i, j: (i, 0)),
                  pl.BlockSpec((tm, d), lambda i, j: (i, 0)),
                  pl.BlockSpec((nb, d, tf), lambda i, j: (j, 0, 0)),
                  pl.BlockSpec((nb, tf, d), lambda i, j: (j, 0, 0))],
        out_specs=[pl.BlockSpec((tm, nb * tf), lambda i, j: (i, j)),
                   pl.BlockSpec((tm, d), lambda i, j: (i, 0))],
        out_shape=[jax.ShapeDtypeStruct((tp, nf * tf), BF16), jax.ShapeDtypeStruct((tp, d), F32)],
        scratch_shapes=[pltpu.VMEM((tm, d), F32)],
        compiler_params=_cparams("parallel", "arbitrary"),
    )(z2, h2, gup, gdown)


def loss_head(h, gf, tgt, t_real, tm):
    tp, d = h.shape

    def body(h_ref, g_ref, t_ref, dh_ref, dg_ref, loss_ref):
        i = pl.program_id(0)
        x = h_ref[...]
        g = g_ref[...]
        r = lax.rsqrt(jnp.mean(x * x, axis=-1, keepdims=True) + NORM_EPS)
        xn = x * r
        rows = i * tm + lax.broadcasted_iota(jnp.int32, (tm, 1), 0)
        valid = jnp.logical_and(rows >= N_META, rows < t_real)
        e = jnp.where(valid, xn * g - t_ref[...], 0.0)
        part = 0.5 * jnp.sum(jnp.sum(e * e, axis=1, keepdims=True) / d, axis=0, keepdims=True)
        dy = e / d
        dxn = dy * g
        dh_ref[...] = r * (dxn - xn * jnp.mean(dxn * xn, axis=-1, keepdims=True))
        _accumulate(dg_ref, jnp.sum(dy * xn, axis=0, keepdims=True), i == 0)
        _accumulate(loss_ref, jnp.broadcast_to(part, (1, LANES)), i == 0)

    row = lambda i: (i, 0)
    fix = lambda i: (0, 0)
    return pl.pallas_call(
        body, name="loss_head", grid=(tp // tm,),
        in_specs=[pl.BlockSpec((tm, d), row), pl.BlockSpec((1, d), fix), pl.BlockSpec((tm, d), row)],
        out_specs=[pl.BlockSpec((tm, d), row), pl.BlockSpec((1, d), fix), pl.BlockSpec((1, LANES), fix)],
        out_shape=[jax.ShapeDtypeStruct((tp, d), F32), jax.ShapeDtypeStruct((1, d), F32),
                   jax.ShapeDtypeStruct((1, LANES), F32)],
        compiler_params=_cparams("arbitrary"),
    )(h, gf, tgt)


def mlp_bwd(dh, u, h2, g2, gup, gdown, tm):
    tp, d = dh.shape
    nf = gup.shape[0]
    tf = gup.shape[2]
    nb = MLP_BLOCKS if nf % MLP_BLOCKS == 0 else 1
    nj = nf // nb
    ni = tp // tm

    def body(dh_ref, u_ref, h2_ref, g_ref, wu_ref, wd_ref, dup_ref, dh2_ref, dg_ref, dhb, acc):
        i = pl.program_id(0)
        j = pl.program_id(1)

        @pl.when(j == 0)
        def _():
            dhb[...] = dh_ref[...].astype(BF16)

        part = None
        for b in range(nb):
            cols = slice(b * tf, (b + 1) * tf)
            dup = (_dot_nt(dhb[...], wd_ref[b]) * (2.0 * u_ref[:, cols].astype(F32))).astype(BF16)
            dup_ref[:, cols] = dup
            p = _dot_nt(dup, wu_ref[b])
            part = p if part is None else part + p
        _accumulate(acc, part, j == 0)

        @pl.when(j == nj - 1)
        def _():
            dx, dg = _rms_bwd(h2_ref[...], g_ref[...], acc[...])
            dh2_ref[...] = dh_ref[...] + dx
            _accumulate(dg_ref, dg, i == 0)

    return pl.pallas_call(
        body, name="mlp_bwd", grid=(ni, nj),
        in_specs=[pl.BlockSpec((tm, d), lambda i, j: (i, 0)),
                  pl.BlockSpec((tm, nb * tf), lambda i, j: (i, j)),
                  pl.BlockSpec((tm, d), lambda i, j: (i, 0)),
                  pl.BlockSpec((1, d), lambda i, j: (0, 0)),
                  pl.BlockSpec((nb, d, tf), lambda i, j: (j, 0, 0)),
                  pl.BlockSpec((nb, tf, d), lambda i, j: (j, 0, 0))],
        out_specs=[pl.BlockSpec((tm, nb * tf), lambda i, j: (i, j)),
                   pl.BlockSpec((tm, d), lambda i, j: (i, 0)),
                   pl.BlockSpec((1, d), lambda i, j: (0, 0)),
                   pl.BlockSpec((tm, d), lambda i, j: (i, 0))],
        out_shape=[jax.ShapeDtypeStruct((tp, nf * tf), BF16), jax.ShapeDtypeStruct((tp, d), F32),
                   jax.ShapeDtypeStruct((1, d), F32), jax.ShapeDtypeStruct((tp, d), BF16)],
        scratch_shapes=[pltpu.VMEM((tm, d), F32)],
        compiler_params=_cparams("arbitrary", "arbitrary"),
    )(dh, u, h2, g2, gup, gdown)


def mm_tn(a, b, *, tk, tn, out_dtype, name, square_a=False, blocked_n=False):
    rows, kk = a.shape
    nn = b.shape[1]
    keep_at = nn // tn > 1

    def a_tile(a_ref):
        av = a_ref[...]
        if square_a:
            af = av.astype(F32)
            av = af * af
        return av.astype(BF16)

    def body(a_ref, b_ref, o_ref, *scratch):
        if keep_at:
            at, = scratch

            @pl.when(pl.program_id(1) == 0)
            def _():
                at[...] = a_tile(a_ref).T

            o_ref[...] = _dot(at[...], b_ref[...].astype(BF16)).astype(out_dtype)
        else:
            o_ref[...] = _dot_tn(a_tile(a_ref), b_ref[...].astype(BF16)).astype(out_dtype)

    if blocked_n:
        out_spec = pl.BlockSpec((None, tk, tn), lambda k, n: (n, k, 0))
        out_shape = jax.ShapeDtypeStruct((nn // tn, kk, tn), out_dtype)
    else:
        out_spec = pl.BlockSpec((tk, tn), lambda k, n: (k, n))
        out_shape = jax.ShapeDtypeStruct((kk, nn), out_dtype)
    return pl.pallas_call(
        body, name=name, grid=(kk // tk, nn // tn),
        in_specs=[pl.BlockSpec((rows, tk), lambda k, n: (0, k)),
                  pl.BlockSpec((rows, tn), lambda k, n: (0, n))],
        out_specs=out_spec, out_shape=out_shape,
        scratch_shapes=[pltpu.VMEM((tk, rows), BF16)] if keep_at else [],
        compiler_params=_cparams("parallel", "arbitrary"),
    )(a, b)


def dw_mlp(u, dhb, z2, dup, tf):
    rows, dff = u.shape
    d = z2.shape[1]
    nf = dff // tf

    def body(u_ref, dh_ref, z_ref, dup_ref, dwd_ref, dwu_ref, zt):
        @pl.when(pl.program_id(0) == 0)
        def _():
            zt[...] = z_ref[...].T

        uf = u_ref[...].astype(F32)
        dwd_ref[...] = _dot_tn((uf * uf).astype(BF16), dh_ref[...]).astype(BF16)
        dwu_ref[...] = _dot(zt[...], dup_ref[...]).astype(BF16)

    col = lambda j: (0, j)
    fix = lambda j: (0, 0)
    return pl.pallas_call(
        body, name="dw_mlp", grid=(nf,),
        in_specs=[pl.BlockSpec((rows, tf), col), pl.BlockSpec((rows, d), fix),
                  pl.BlockSpec((rows, d), fix), pl.BlockSpec((rows, tf), col)],
        out_specs=[pl.BlockSpec((None, tf, d), lambda j: (j, 0, 0)),
                   pl.BlockSpec((None, d, tf), lambda j: (j, 0, 0))],
        out_shape=[jax.ShapeDtypeStruct((nf, tf, d), BF16), jax.ShapeDtypeStruct((nf, d, tf), BF16)],
        scratch_shapes=[pltpu.VMEM((d, rows), BF16)],
        compiler_params=_cparams("arbitrary"),
    )(u, dhb, z2, dup)


def out_proj_bwd(dh2, o, rec, ga, gr, wout, tm):
    tp, d = dh2.shape
    aw, rw = o.shape[1], rec.shape[1]

    def body(dh_ref, o_ref, rec_ref, ga_ref, gr_ref, w_ref, do_ref, drec_ref, dga_ref, dgr_ref):
        i = pl.program_id(0)
        dmix = _dot_nt(dh_ref[...].astype(BF16), w_ref[...])
        do, dga = _rms_bwd(o_ref[...], ga_ref[...], dmix[:, 0:aw])
        drec, dgr = _rms_bwd(rec_ref[...], gr_ref[...], dmix[:, aw:aw + rw])
        do_ref[...] = do
        drec_ref[...] = drec
        _accumulate(dga_ref, dga, i == 0)
        _accumulate(dgr_ref, dgr, i == 0)

    row = lambda i: (i, 0)
    fix = lambda i: (0, 0)
    return pl.pallas_call(
        body, name="out_proj_bwd", grid=(tp // tm,),
        in_specs=[pl.BlockSpec((tm, d), row), pl.BlockSpec((tm, aw), row), pl.BlockSpec((tm, rw), row),
                  pl.BlockSpec((1, aw), fix), pl.BlockSpec((1, rw), fix), pl.BlockSpec((d, d), fix)],
        out_specs=[pl.BlockSpec((tm, aw), row), pl.BlockSpec((tm, rw), row),
                   pl.BlockSpec((1, aw), fix), pl.BlockSpec((1, rw), fix)],
        out_shape=[jax.ShapeDtypeStruct((tp, aw), F32), jax.ShapeDtypeStruct((tp, rw), F32),
                   jax.ShapeDtypeStruct((1, aw), F32), jax.ShapeDtypeStruct((1, rw), F32)],
        compiler_params=_cparams("arbitrary"),
    )(dh2, o, rec, ga, gr, wout)


def rec_bwd(drec, hr, xc, rest, convw, convb, wga, bga, wgx, bgx, lru, rw):
    tp = rest.shape[0]
    ng = rw // LANES

    def body(drec_ref, hr_ref, xc_ref, xr_ref, yr_ref, cw_ref, cb_ref, wga_ref, bga_ref, wgx_ref, bgx_ref, l_ref,
             dxr_ref, dyr_ref, dwga_ref, dwgx_ref, vec_ref, a_s, u_s, lam_s):
        xc = xc_ref[...]
        h = hr_ref[...]
        yr = yr_ref[...]
        drec = drec_ref[...]
        xcb, r, ig, ls, log_a, a, mult = _gates(xc, wga_ref, bga_ref, wgx_ref, bgx_ref, l_ref)
        dyr_ref[...] = (drec * h * _gelu_grad(yr)).astype(BF16)
        a_s[...] = _shift_up(a, 1, tp)
        u_s[...] = drec * _gelu(yr)
        _scan_rows(a_s, u_s, lam_s, tp, reverse=True)
        lam = lam_s[...]
        da = lam * _shift_down(h, 1, tp)
        dmult = lam * ig * xc
        dig = lam * mult * xc
        dxc = lam * mult * ig
        a2 = jnp.exp(2.0 * log_a)
        dlog_a = da * a - dmult * a2 / mult
        dr = dlog_a * (RG_C * ls)
        dl = jnp.sum(dlog_a * (RG_C * r), axis=0, keepdims=True) * _sigmoid(-l_ref[...])
        dpa = dr * r * (1.0 - r)
        dpx = dig * ig * (1.0 - ig)
        dpab = dpa.astype(BF16)
        dpxb = dpx.astype(BF16)
        dxc = dxc + _dot_nt(dpab, wga_ref[...]) + _dot_nt(dpxb, wgx_ref[...])
        dwga_ref[...] = _dot_tn(xcb, dpab)
        dwgx_ref[...] = _dot_tn(xcb, dpxb)
        xr = xr_ref[...]
        dxr = cw_ref[CONV_WIDTH - 1:CONV_WIDTH, :] * dxc
        for k in range(1, CONV_WIDTH):
            dxr = dxr + cw_ref[CONV_WIDTH - 1 - k:CONV_WIDTH - k, :] * _shift_up(dxc, k, tp)
        dxr_ref[...] = dxr.astype(BF16)
        for k in range(CONV_WIDTH):
            vec_ref[k:k + 1, :] = jnp.sum(dxc * _shift_down(xr, CONV_WIDTH - 1 - k, tp), axis=0, keepdims=True)
        vec_ref[4:5, :] = jnp.sum(dxc, axis=0, keepdims=True)
        vec_ref[5:6, :] = jnp.sum(dpa, axis=0, keepdims=True)
        vec_ref[6:7, :] = jnp.sum(dpx, axis=0, keepdims=True)
        vec_ref[7:8, :] = dl

    col = lambda g: (0, g)
    vec = pl.BlockSpec((1, LANES), col)
    big = pl.BlockSpec((tp, LANES), col)
    sq = pl.BlockSpec((None, LANES, LANES), lambda g: (g, 0, 0))
    return pl.pallas_call(
        body, name="rec_bwd", grid=(ng,),
        in_specs=[big, big, big, big, pl.BlockSpec((tp, LANES), lambda g: (0, ng + g)),
                  pl.BlockSpec((CONV_WIDTH, LANES), col), vec, sq, vec, sq, vec, vec],
        out_specs=[big, big, sq, sq, pl.BlockSpec((None, SUBLANES, LANES), lambda g: (g, 0, 0))],
        out_shape=[jax.ShapeDtypeStruct((tp, rw), BF16), jax.ShapeDtypeStruct((tp, rw), BF16),
                   jax.ShapeDtypeStruct((ng, LANES, LANES), F32), jax.ShapeDtypeStruct((ng, LANES, LANES), F32),
                   jax.ShapeDtypeStruct((ng, SUBLANES, LANES), F32)],
        scratch_shapes=[pltpu.VMEM((tp, LANES), F32)] * 3,
        compiler_params=_cparams("parallel"),
    )(drec, hr, xc, rest, rest, convw, convb, wga, bga, wgx, bgx, lru)


def attn_bwd(qkv, do, o, lset, c, ct, nh):
    tp = qkv.shape[0]
    npair = nh // 2
    aw = nh * HEAD_DIM
    tiles = _att_tiles(tp)

    def body(q_ref, k_ref, v_ref, do_ref, o_ref, lset_ref, c_ref, ct_ref,
             dq_ref, dk_ref, dv_ref, drow_ref, dcol_ref, dk_acc, dv_acc, dq_t):
        p = pl.program_id(0)
        k_t = k_ref[...].T
        dk_acc[...] = jnp.zeros_like(dk_acc)
        dv_acc[...] = jnp.zeros_like(dv_acc)
        dcol_ref[...] = jnp.zeros_like(dcol_ref)
        drow_ref[...] = jnp.zeros_like(drow_ref)
        for r0, nr, nk in tiles:
            rs = slice(r0, r0 + nr)
            causal = (r0 + lax.broadcasted_iota(jnp.int32, (nk, nr), 1)
                      >= lax.broadcasted_iota(jnp.int32, (nk, nr), 0))
            cblk = c_ref[0:nk, :]
            ctb = ct_ref[:, rs]
            for hh in range(2):
                head = 2 * p + hh
                hs = slice(hh * HEAD_DIM, (hh + 1) * HEAD_DIM)
                q = q_ref[rs, hs]
                k = k_ref[0:nk, hs]
                dof = do_ref[rs, hs]
                do16 = dof.astype(BF16)
                delta = jnp.sum(dof * o_ref[rs, hs], axis=1, keepdims=True)
                delta_row = jnp.broadcast_to(delta, (nr, LANES)).T[0:1, :]
                s_t = _dot_nt(k, q * ATT_SCALE) + (_pick_row(ctb, head) - _pick_col(cblk, head))
                p_t = jnp.where(causal, jnp.exp(s_t - lset_ref[hh:hh + 1, rs]), 0.0)
                ds_t = p_t * (_dot_nt(v_ref[0:nk, hs], do16) - delta_row)
                p16 = p_t.astype(BF16)
                ds16 = ds_t.astype(BF16)
                dv_acc[0:nk, hs] += _dot(p16, do16)
                dk_acc[0:nk, hs] += _dot(ds16, q) * ATT_SCALE
                dq_t[hs, rs] = _dot(k_t[hs, 0:nk], ds16)
                drow_ref[hh:hh + 1, rs] = jnp.sum(ds_t, axis=0, keepdims=True)
                dcol_ref[0:nk, hs] -= jnp.broadcast_to(jnp.sum(ds_t, axis=1, keepdims=True), (nk, HEAD_DIM))
        dk_ref[...] = dk_acc[...].astype(BF16)
        dv_ref[...] = dv_acc[...].astype(BF16)
        dq_ref[...] = (dq_t[...].T * ATT_SCALE).astype(BF16)

    pair = lambda p: (0, p)
    return pl.pallas_call(
        body, name="attn_bwd", grid=(npair,),
        in_specs=[pl.BlockSpec((tp, LANES), pair),
                  pl.BlockSpec((tp, LANES), lambda p: (0, npair + p)),
                  pl.BlockSpec((tp, LANES), lambda p: (0, 2 * npair + p)),
                  pl.BlockSpec((tp, LANES), pair),
                  pl.BlockSpec((tp, LANES), pair),
                  pl.BlockSpec((None, SUBLANES, tp), lambda p: (p, 0, 0)),
                  pl.BlockSpec((tp, LANES), lambda p: (0, 0)),
                  pl.BlockSpec((SUBLANES, tp), lambda p: (0, 0))],
        out_specs=[pl.BlockSpec((tp, LANES), pair), pl.BlockSpec((tp, LANES), pair),
                   pl.BlockSpec((tp, LANES), pair),
                   pl.BlockSpec((None, SUBLANES, tp), lambda p: (p, 0, 0)),
                   pl.BlockSpec((tp, LANES), pair)],
        out_shape=[jax.ShapeDtypeStruct((tp, aw), BF16), jax.ShapeDtypeStruct((tp, aw), BF16),
                   jax.ShapeDtypeStruct((tp, aw), BF16),
                   jax.ShapeDtypeStruct((npair, SUBLANES, tp), F32),
                   jax.ShapeDtypeStruct((tp, aw), F32)],
        scratch_shapes=[pltpu.VMEM((tp, LANES), F32), pltpu.VMEM((tp, LANES), F32),
                        pltpu.VMEM((LANES, tp), F32)],
        compiler_params=_cparams("parallel"),
    )(qkv, qkv, qkv, do, o, lset, c, ct)


def fgate_bwd(dct8, drs, rest, bf_pad, fcol):
    tp = rest.shape[0]
    aw = drs.shape[1]
    nb = tp // ATT_BLOCK
    B = ATT_BLOCK

    def body(d_ref, drs_ref, f_ref, b_ref, dfl_ref, db_ref, pad_s):
        r_i = lax.broadcasted_iota(jnp.int32, (B, B), 0)
        c_i = lax.broadcasted_iota(jnp.int32, (B, B), 1)
        triu = (c_i >= r_i).astype(BF16)
        sel = (lax.broadcasted_iota(jnp.int32, (aw, LANES), 0)
               == HEAD_DIM * lax.broadcasted_iota(jnp.int32, (aw, LANES), 1)).astype(BF16)
        carry = jnp.zeros((1, LANES), F32)
        db = jnp.zeros((1, LANES), F32)
        pad_s[...] = jnp.zeros_like(pad_s)
        for i in range(nb - 1, -1, -1):
            sl = slice(i * B, (i + 1) * B)
            pad_s[0:SUBLANES, :] = d_ref[:, sl]
            dc = pad_s[...].T + _dot_split3(drs_ref[sl, :], sel)
            rc = _split3_dot(triu, dc)
            dlf = rc + carry
            carry = carry + rc[0:1, :]
            dfl = dlf * _sigmoid(-(f_ref[sl, :] + b_ref[...]))
            dfl_ref[sl, :] = dfl.astype(BF16)
            db = db + jnp.sum(dfl, axis=0, keepdims=True)
        db_ref[...] = db

    return pl.pallas_call(
        body, name="fgate_bwd", grid=(1,),
        in_specs=[pl.BlockSpec((SUBLANES, tp), lambda i: (0, 0)),
                  pl.BlockSpec((tp, aw), lambda i: (0, 0)),
                  pl.BlockSpec((tp, LANES), lambda i: (0, fcol)),
                  pl.BlockSpec((1, LANES), lambda i: (0, 0))],
        out_specs=[pl.BlockSpec((tp, LANES), lambda i: (0, 0)),
                   pl.BlockSpec((1, LANES), lambda i: (0, 0))],
        out_shape=[jax.ShapeDtypeStruct((tp, LANES), BF16), jax.ShapeDtypeStruct((1, LANES), F32)],
        scratch_shapes=[pltpu.VMEM((B, B), F32)],
        compiler_params=_cparams("arbitrary"),
    )(dct8, drs, rest, bf_pad)


def in_proj_bwd(dh2, parts, w_in_t, wrest_t, h, g1, tm):
    tp, d = h.shape
    dq, dk, dv, dxr, dyr, dfl = parts
    aw, rw = dq.shape[1], dxr.shape[1]

    def body(dh2_ref, dq_ref, dk_ref, dv_ref, dxr_ref, dyr_ref, dfl_ref, wq_ref, wr_ref, h_ref, g_ref,
             dh_ref, dg_ref):
        i = pl.program_id(0)
        dz = _dot(dq_ref[...], wq_ref[0:aw, :])
        dz += _dot(dk_ref[...], wq_ref[aw:2 * aw, :])
        dz += _dot(dv_ref[...], wq_ref[2 * aw:3 * aw, :])
        dz += _dot(dxr_ref[...], wr_ref[0:rw, :])
        dz += _dot(dyr_ref[...], wr_ref[rw:2 * rw, :])
        dz += _dot(dfl_ref[...], wr_ref[2 * rw:2 * rw + LANES, :])
        dx, dg = _rms_bwd(h_ref[...], g_ref[...], dz)
        dh_ref[...] = dh2_ref[...] + dx
        _accumulate(dg_ref, dg, i == 0)

    row = lambda i: (i, 0)
    fix = lambda i: (0, 0)
    return pl.pallas_call(
        body, name="in_proj_bwd", grid=(tp // tm,),
        in_specs=[pl.BlockSpec((tm, d), row),
                  pl.BlockSpec((tm, aw), row), pl.BlockSpec((tm, aw), row), pl.BlockSpec((tm, aw), row),
                  pl.BlockSpec((tm, rw), row), pl.BlockSpec((tm, rw), row), pl.BlockSpec((tm, LANES), row),
                  pl.BlockSpec((3 * aw, d), fix), pl.BlockSpec(wrest_t.shape, fix),
                  pl.BlockSpec((tm, d), row), pl.BlockSpec((1, d), fix)],
        out_specs=[pl.BlockSpec((tm, d), row), pl.BlockSpec((1, d), fix)],
        out_shape=[jax.ShapeDtypeStruct((tp, d), F32), jax.ShapeDtypeStruct((1, d), F32)],
        compiler_params=_cparams("arbitrary"),
    )(dh2, dq, dk, dv, dxr, dyr, dfl, w_in_t, wrest_t, h, g1)


def dw_in_t(z, parts, nh, tr):
    tp, d = z.shape
    dq, dk, dv, dxr, dyr, dfl = parts
    aw, rw = dq.shape[1], dxr.shape[1]
    d_in = 3 * aw + nh + 2 * rw
    nr = tp // tr
    offs = [(0, aw), (aw, aw), (2 * aw, aw), (3 * aw + nh, rw), (3 * aw + nh + rw, rw)]

    def body(z_ref, dq_ref, dk_ref, dv_ref, dxr_ref, dyr_ref, dfl_ref, o_ref, acc):
        r = pl.program_id(0)

        @pl.when(r == 0)
        def _():
            acc[...] = jnp.zeros_like(acc)

        zt = z_ref[...]
        for (o, n), ref in zip(offs, (dq_ref, dk_ref, dv_ref, dxr_ref, dyr_ref)):
            acc[o:o + n, :] += _dot_tn(ref[...], zt)
        acc[3 * aw:3 * aw + nh, :] += _dot_tn(dfl_ref[...], zt)[0:nh, :]

        @pl.when(r == nr - 1)
        def _():
            o_ref[...] = acc[...].astype(BF16)

    row = lambda r: (r, 0)
    return pl.pallas_call(
        body, name="dw_in", grid=(nr,),
        in_specs=[pl.BlockSpec((tr, d), row),
                  pl.BlockSpec((tr, aw), row), pl.BlockSpec((tr, aw), row), pl.BlockSpec((tr, aw), row),
                  pl.BlockSpec((tr, rw), row), pl.BlockSpec((tr, rw), row), pl.BlockSpec((tr, LANES), row)],
        out_specs=pl.BlockSpec((d_in, d), lambda r: (0, 0)),
        out_shape=jax.ShapeDtypeStruct((d_in, d), BF16),
        scratch_shapes=[pltpu.VMEM((d_in, d), F32)],
        compiler_params=_cparams("arbitrary"),
    )(z, dq, dk, dv, dxr, dyr, dfl)


def _place():
    return lax.axis_index("x"), lax.axis_index("y"), lax.axis_index("c")


HBM = pl.BlockSpec(memory_space=pltpu.HBM)
SEM = pl.BlockSpec(memory_space=pltpu.SEMAPHORE)
EFFECT = pltpu.SideEffectType.DATAFLOW_SIDE_EFFECTING


def _in_hbm(a):
    return pltpu.with_memory_space_constraint(a, pltpu.HBM)


def _as_list(a):
    return list(a) if isinstance(a, (list, tuple)) else [a]


def _gather_targets(x, y, c):
    return [(x, y, 1 - c), (1 - x, y, c), (x, 1 - y, c), (1 - x, 1 - y, c)]


def _slot(t):
    return 4 * t[0] + 2 * t[1] + t[2]


def gather_start(groups, name):
    flat = [a for g in groups for a in g]
    n = len(flat)
    ng = len(groups)
    lands = [lax.empty((N_DEV,) + a.shape, a.dtype) for a in flat]

    def body(*refs):
        src, land = refs[:n], refs[n:2 * n]
        sems = refs[2 * n:2 * n + 2 * ng]
        token = refs[-1]
        x, y, c = _place()
        me = 4 * x + 2 * y + c
        i = 0
        for gi, g in enumerate(groups):
            for a in range(len(g)):
                for k, t in enumerate(_gather_targets(x, y, c)):
                    pltpu.make_async_remote_copy(
                        src_ref=src[i], dst_ref=land[i].at[me],
                        send_sem=sems[2 * gi].at[4 * a + k], recv_sem=sems[2 * gi + 1].at[4 * a + k],
                        device_id=t, device_id_type=MESH).start()
                i += 1
        token[...] = jnp.zeros_like(token)

    sem_shapes = []
    for g in groups:
        sem_shapes += [pltpu.SemaphoreType.DMA((4 * len(g),)), pltpu.SemaphoreType.DMA((4 * len(g),))]
    out = pl.pallas_call(
        body, name=name,
        out_shape=sem_shapes + [pltpu.HBM(a.shape, a.dtype) for a in flat + lands]
        + [jax.ShapeDtypeStruct((SUBLANES, LANES), F32)],
        in_specs=[HBM] * (2 * n),
        out_specs=[SEM] * (2 * ng) + [HBM] * (2 * n) + [pl.BlockSpec(memory_space=pltpu.VMEM)],
        input_output_aliases={i: 2 * ng + i for i in range(2 * n)},
        compiler_params=pltpu.CompilerParams(has_side_effects=EFFECT),
    )(*[_in_hbm(a) for a in flat + lands])
    sems = out[:2 * ng]
    thru = out[2 * ng:2 * ng + 2 * n]
    srcs_t, lands_t = thru[:n], thru[n:]
    res, i = [], 0
    for gi, g in enumerate(groups):
        res.append((sems[2 * gi], sems[2 * gi + 1], srcs_t[i:i + len(g)], lands_t[i:i + len(g)]))
        i += len(g)
    return res, out[-1]


def gather_wait(send, recv, srcs, lands, after, name):
    n = len(srcs)

    def body(*refs):
        src, land = refs[:n], refs[n:2 * n]
        send_sem, recv_sem = refs[2 * n], refs[2 * n + 1]
        x, y, c = _place()
        for a in range(n):
            for k, t in enumerate(_gather_targets(x, y, c)):
                cp = pltpu.make_async_remote_copy(
                    src_ref=src[a], dst_ref=land[a].at[_slot(t)],
                    send_sem=send_sem.at[4 * a + k], recv_sem=recv_sem.at[4 * a + k],
                    device_id=t, device_id_type=MESH)
                cp.wait_send()
                cp.wait_recv()

    out = pl.pallas_call(
        body, name=name,
        out_shape=[pltpu.HBM(a.shape, a.dtype) for a in list(srcs) + list(lands)],
        in_specs=[HBM] * (2 * n) + [SEM, SEM] + [ANY] * len(_as_list(after)),
        out_specs=[HBM] * (2 * n),
        input_output_aliases={i: i for i in range(2 * n)},
        compiler_params=pltpu.CompilerParams(has_side_effects=EFFECT),
    )(*srcs, *lands, send, recv, *_as_list(after))
    return out[:n], out[n:]


def forward_start(lands, name):
    n = len(lands)

    def body(*refs):
        land = refs[:n]
        send_sem, recv_sem = refs[n], refs[n + 1]
        token = refs[-1]
        x, y, c = _place()
        for a in range(n):
            for j, chip in enumerate([(1 - x, y), (x, 1 - y), (1 - x, 1 - y)]):
                blk = land[a].at[_slot((*chip, c))]
                pltpu.make_async_remote_copy(src_ref=blk, dst_ref=blk, send_sem=send_sem.at[3 * a + j],
                                             recv_sem=recv_sem.at[3 * a + j], device_id=(x, y, 1 - c),
                                             device_id_type=MESH).start()
        token[...] = jnp.zeros_like(token)

    out = pl.pallas_call(
        body, name=name,
        out_shape=[pltpu.SemaphoreType.DMA((3 * n,)), pltpu.SemaphoreType.DMA((3 * n,))]
        + [pltpu.HBM(a.shape, a.dtype) for a in lands] + [jax.ShapeDtypeStruct((SUBLANES, LANES), F32)],
        in_specs=[HBM] * n,
        out_specs=[SEM, SEM] + [HBM] * n + [pl.BlockSpec(memory_space=pltpu.VMEM)],
        input_output_aliases={i: 2 + i for i in range(n)},
        compiler_params=pltpu.CompilerParams(has_side_effects=EFFECT),
    )(*[_in_hbm(a) for a in lands])
    return out[0], out[1], out[2:2 + n], out[-1][0, 0]


def forward_wait(send, recv, lands, after, name):
    n = len(lands)

    def body(*refs):
        land = refs[:n]
        send_sem, recv_sem = refs[n], refs[n + 1]
        x, y, c = _place()
        for a in range(n):
            for j, chip in enumerate([(1 - x, y), (x, 1 - y), (1 - x, 1 - y)]):
                cp = pltpu.make_async_remote_copy(
                    src_ref=land[a].at[_slot((*chip, c))], dst_ref=land[a].at[_slot((*chip, 1 - c))],
                    send_sem=send_sem.at[3 * a + j], recv_sem=recv_sem.at[3 * a + j],
                    device_id=(x, y, 1 - c), device_id_type=MESH)
                cp.wait_send()
                cp.wait_recv()

    return pl.pallas_call(
        body, name=name,
        out_shape=[pltpu.HBM(a.shape, a.dtype) for a in lands],
        in_specs=[HBM] * n + [SEM, SEM, ANY],
        out_specs=[HBM] * n,
        input_output_aliases={i: i for i in range(n)},
        compiler_params=pltpu.CompilerParams(has_side_effects=EFFECT),
    )(*lands, send, recv, after)


def _relations():
    return [(dx, dy, dc) for dx in (0, 1) for dy in (0, 1) for dc in (0, 1) if dx + dy + dc]


def _peer(x, y, c, rel):
    return ((1 - x) if rel[0] else x, (1 - y) if rel[1] else y, (1 - c) if rel[2] else c)


def exchange_start(srcs, lands, layer, name):
    n = len(srcs)

    def body(*refs):
        src, land = refs[:n], refs[n:2 * n]
        send_sem, recv_sem = refs[2 * n], refs[2 * n + 1]
        token = refs[-1]
        x, y, c = _place()
        me = 4 * x + 2 * y + c
        for k, rel in enumerate(_relations()):
            peer = _peer(x, y, c, rel)
            for a in range(n):
                pltpu.make_async_remote_copy(
                    src_ref=src[a] if layer is None else src[a].at[_slot(peer)],
                    dst_ref=land[a].at[me] if layer is None else land[a].at[me, layer],
                    send_sem=send_sem.at[7 * a + k], recv_sem=recv_sem.at[7 * a + k],
                    device_id=peer, device_id_type=MESH).start()
        token[...] = jnp.zeros_like(token)

    out = pl.pallas_call(
        body, name=name,
        out_shape=[pltpu.SemaphoreType.DMA((7 * n,)), pltpu.SemaphoreType.DMA((7 * n,))]
        + [pltpu.HBM(a.shape, a.dtype) for a in list(srcs) + list(lands)]
        + [jax.ShapeDtypeStruct((SUBLANES, LANES), F32)],
        in_specs=[HBM] * (2 * n),
        out_specs=[SEM, SEM] + [HBM] * (2 * n) + [pl.BlockSpec(memory_space=pltpu.VMEM)],
        input_output_aliases={i: 2 + i for i in range(2 * n)},
        compiler_params=pltpu.CompilerParams(has_side_effects=EFFECT),
    )(*[_in_hbm(a) for a in list(srcs) + list(lands)])
    return out[0], out[1], out[2:2 + n], out[2 + n:2 + 2 * n], out[-1][0, 0]


def exchange_wait(send, recv, srcs, lands, after, layer, name):
    n = len(srcs)

    def body(*refs):
        src, land = refs[:n], refs[n:2 * n]
        send_sem, recv_sem = refs[2 * n], refs[2 * n + 1]
        x, y, c = _place()
        for k, rel in enumerate(_relations()):
            peer = _peer(x, y, c, rel)
            for a in range(n):
                cp = pltpu.make_async_remote_copy(
                    src_ref=src[a] if layer is None else src[a].at[_slot(peer)],
                    dst_ref=land[a].at[_slot(peer)] if layer is None else land[a].at[_slot(peer), layer],
                    send_sem=send_sem.at[7 * a + k], recv_sem=recv_sem.at[7 * a + k],
                    device_id=peer, device_id_type=MESH)
                cp.wait_send()
                cp.wait_recv()

    out = pl.pallas_call(
        body, name=name,
        out_shape=[pltpu.HBM(a.shape, a.dtype) for a in list(srcs) + list(lands)],
        in_specs=[HBM] * (2 * n) + [SEM, SEM] + [ANY] * len(_as_list(after)),
        out_specs=[HBM] * (2 * n),
        input_output_aliases={i: i for i in range(2 * n)},
        compiler_params=pltpu.CompilerParams(has_side_effects=EFFECT),
    )(*srcs, *lands, send, recv, *_as_list(after))
    return out[:n], out[n:]


def _adamw_math(g, w, m, v):
    m = ADAM_B1 * m + (1.0 - ADAM_B1) * g
    v = ADAM_B2 * v + (1.0 - ADAM_B2) * (g * g)
    m_hat = m / (1.0 - ADAM_B1 ** ADAM_STEP)
    v_hat = v / (1.0 - ADAM_B2 ** ADAM_STEP)
    delta = -ADAM_LR * (m_hat / (jnp.sqrt(v_hat) + ADAM_EPS) + ADAM_WD * w)
    return delta, m, v


def _sum_with_own(p_ref, own_refs, layer, me):
    own = own_refs[0][...]
    for k in range(1, len(own_refs)):
        own = jnp.where(layer == k, own_refs[k][...], own)
    g = None
    for p in range(p_ref.shape[0]):
        term = jnp.where(me == p, own, p_ref[p]).astype(F32)
        g = term if g is None else g + term
    return g


def sum_adamw(parts, owns, me, w, m, v, tr, name):
    npart, rows, cols = parts.shape
    nl = len(owns)
    per_layer = rows // nl // tr

    def body(me_ref, p_ref, *refs):
        own_refs = refs[:nl]
        w_ref, m_ref, v_ref, g_ref, d_ref, nm_ref, nv_ref = refs[nl:]
        g = _sum_with_own(p_ref, own_refs, pl.program_id(0) // per_layer, me_ref[0])
        delta, nm, nv = _adamw_math(g, w_ref[...], m_ref[...], v_ref[...])
        g_ref[...] = g
        d_ref[...] = delta
        nm_ref[...] = nm
        nv_ref[...] = nv

    blk = pl.BlockSpec((tr, cols), lambda i, me_ref: (i, 0))
    own_specs = [pl.BlockSpec((None, tr, cols),
                              lambda i, me_ref, l=l: (me_ref[0], jnp.clip(i - l * per_layer, 0, per_layer - 1), 0))
                 for l in range(nl)]
    return pl.pallas_call(
        body, name=name,
        grid_spec=pltpu.PrefetchScalarGridSpec(
            num_scalar_prefetch=1, grid=(rows // tr,),
            in_specs=[pl.BlockSpec((npart, tr, cols), lambda i, me_ref: (0, i, 0))] + own_specs + [blk, blk, blk],
            out_specs=[blk] * 4),
        out_shape=[jax.ShapeDtypeStruct((rows, cols), F32)] * 4,
        compiler_params=_cparams("arbitrary"),
    )(me, parts, *owns, w, m, v)


def sum_adamw_t(parts, owns, me, w, m, v, name):
    npart, nl, rows, cols = parts.shape

    def body(me_ref, p_ref, *refs):
        own_refs = refs[:nl]
        w_ref, m_ref, v_ref, g_ref, d_ref, nm_ref, nv_ref = refs[nl:]
        g = _sum_with_own(p_ref, own_refs, pl.program_id(0), me_ref[0])
        delta, nm, nv = _adamw_math(g, w_ref[...], m_ref[...], v_ref[...])
        g_ref[...] = g
        d_ref[...] = delta
        nm_ref[...] = nm
        nv_ref[...] = nv

    blk = pl.BlockSpec((None, rows, cols), lambda l, me_ref: (l, 0, 0))
    own_specs = [pl.BlockSpec((None, rows, cols), lambda l, me_ref: (me_ref[0], 0, 0)) for _ in range(nl)]
    return pl.pallas_call(
        body, name=name,
        grid_spec=pltpu.PrefetchScalarGridSpec(
            num_scalar_prefetch=1, grid=(nl,),
            in_specs=[pl.BlockSpec((npart, None, rows, cols), lambda l, me_ref: (0, l, 0, 0))] + own_specs
            + [blk, blk, blk],
            out_specs=[blk] * 4),
        out_shape=[jax.ShapeDtypeStruct((nl, rows, cols), F32)] * 4,
        compiler_params=_cparams("arbitrary"),
    )(me, parts, *owns, w, m, v)


def adamw_group(gs, ws, ms, vs, name):
    n = len(gs)

    def body(*refs):
        g, w, m, v, outs = refs[:n], refs[n:2 * n], refs[2 * n:3 * n], refs[3 * n:4 * n], refs[4 * n:]
        for i in range(n):
            delta, nm, nv = _adamw_math(g[i][...], w[i][...], m[i][...], v[i][...])
            outs[i][...] = delta
            outs[n + i][...] = nm
            outs[2 * n + i][...] = nv

    vmem = pl.BlockSpec(memory_space=pltpu.VMEM)
    out = pl.pallas_call(
        body, name=name,
        in_specs=[vmem] * (4 * n), out_specs=[vmem] * (3 * n),
        out_shape=[jax.ShapeDtypeStruct(a.shape, F32) for a in list(ws) * 3],
        compiler_params=_cparams(),
    )(*gs, *ws, *ms, *vs)
    return out[:n], out[n:2 * n], out[2 * n:]


def sum_parts(parts, name):
    npart, rows, cols = parts.shape

    def body(p_ref, g_ref):
        g = p_ref[0].astype(F32)
        for p in range(1, npart):
            g = g + p_ref[p].astype(F32)
        g_ref[...] = g

    return pl.pallas_call(
        body, name=name, grid=(1,),
        in_specs=[pl.BlockSpec((npart, rows, cols), lambda i: (0, 0, 0))],
        out_specs=pl.BlockSpec((rows, cols), lambda i: (0, 0)),
        out_shape=jax.ShapeDtypeStruct((rows, cols), F32),
        compiler_params=_cparams("arbitrary"),
    )(parts)


def _round_up(n, m):
    return (n + m - 1) // m * m


def _block_diag_pairs(w):
    nb, b, _ = w.shape
    per = LANES // b
    ng = nb // per
    w = w.reshape(ng, per, b, b)
    eye = jnp.eye(per, dtype=w.dtype)
    out = jnp.einsum('gpij,pq->gpiqj', w, eye).reshape(ng, LANES, LANES)
    return out.astype(BF16)


def _block_diag_extract(g, b):
    ng = g.shape[0]
    per = LANES // b
    g = g.reshape(ng, per, b, per, b)
    idx = jnp.arange(per)
    return g[:, idx, :, idx, :].transpose(1, 0, 2, 3).reshape(ng * per, b, b)


def _tiles(v):
    v = v.reshape(-1)
    n = _round_up(v.shape[0], SUBLANES * LANES)
    return jnp.pad(v, (0, n - v.shape[0])).reshape(-1, LANES)


SMALL = ['attn_norm_g', 'b_f', 'conv_w', 'conv_b', 'w_gate_a', 'b_gate_a', 'w_gate_x', 'b_gate_x',
         'lru_L', 'attn_out_g', 'rec_out_g', 'mlp_norm_g', 'final_g', 'meta']


def _pack(d):
    return jnp.concatenate([_tiles(d[n]) for n in SMALL], axis=0)


def _unpack(vec, shapes):
    out, r = {}, 0
    for n in SMALL:
        size = math.prod(shapes[n])
        nr = _round_up(size, SUBLANES * LANES) // LANES
        out[n] = vec[r:r + nr].reshape(-1)[:size].reshape(shapes[n])
        r += nr
    return out


def _row_tile(tp):
    return tp // 4 if (tp // 4) % 16 == 0 else tp


def local_step(x, tgt, meta, small, hooks):
    s, d = x.shape
    t_real = s + N_META
    tp = _round_up(t_real, ATT_BLOCK)
    depth = small['attn_norm_g'].shape[0]
    nh = small['b_f'].shape[1]
    rw = small['conv_b'].shape[1]
    blk = small['w_gate_a'].shape[2]
    tm = _row_tile(tp)
    tm2 = tp // 2
    fcol = 2 * rw // LANES

    h = jnp.concatenate([meta, x, jnp.zeros((tp - t_real, d), F32)], axis=0)
    tgt_p = jnp.pad(tgt, ((N_META, tp - t_real), (0, 0)))
    row = lambda v: v.reshape(1, -1)
    bf_pad = jnp.pad(small['b_f'], ((0, 0), (0, LANES - nh)))

    saved = []
    for l in range(depth):
        w_in_t, wrest_t, wout, tok_w = hooks.mixer_weights(l, h)
        wga = _block_diag_pairs(small['w_gate_a'][l])
        wgx = _block_diag_pairs(small['w_gate_x'][l])
        z, qkv, rest = in_proj(h, row(small['attn_norm_g'][l]) + tok_w, w_in_t, wrest_t, 3 * nh * HEAD_DIM, tm)
        c, ct = fgate_fwd(rest, bf_pad[l:l + 1], fcol)
        o, lset = attn_fwd(qkv, c, ct, nh)
        rec, hr, xc = rec_fwd(rest, small['conv_w'][l], row(small['conv_b'][l]), wga, row(small['b_gate_a'][l]),
                              wgx, row(small['b_gate_x'][l]), row(small['lru_L'][l]), rw)
        gup, gdown, tok_w = hooks.mlp_weights(l, rec)
        h2, mix, z2 = out_proj(h, o, rec, row(small['attn_out_g'][l]), row(small['rec_out_g'][l]), wout,
                               row(small['mlp_norm_g'][l]) + tok_w, tm)
        u, h3 = mlp_fwd(z2, h2, gup, gdown, tm2)
        saved.append(dict(h=h, z=z, qkv=qkv, rest=rest, c=c, ct=ct, o=o, lset=lset, rec=rec, hr=hr, xc=xc,
                          h2=h2, mix=mix, z2=z2, u=u, wga=wga, wgx=wgx,
                          w_in_t=w_in_t, wrest_t=wrest_t, wout=wout, gup=gup, gdown=gdown))
        h = h3

    dh, dgf, loss = loss_head(h, row(small['final_g']), tgt_p, t_real, tm)

    gs = {n: [None] * depth for n in SMALL if n not in ('final_g', 'meta')}
    tok = jnp.zeros((), F32)
    for l in reversed(range(depth)):
        sv = saved[l]
        gup, gdown = sv['gup'], sv['gdown']
        tf = gup.shape[2]
        dup, dh2, dg2, dhb = mlp_bwd(dh, sv['u'], sv['h2'], row(small['mlp_norm_g'][l]) + tok, gup, gdown, tm)
        gs['mlp_norm_g'][l] = dg2[0]
        do, drec, dga, dgr = out_proj_bwd(dh2, sv['o'], sv['rec'], row(small['attn_out_g'][l]),
                                          row(small['rec_out_g'][l]), sv['wout'], tm)
        gs['attn_out_g'][l] = dga[0]
        gs['rec_out_g'][l] = dgr[0]
        dw_down, dw_up = dw_mlp(sv['u'], dhb, sv['z2'], dup, tf)
        blocks = dict(
            w_down=dw_down, w_up=dw_up,
            w_out=mm_tn(sv['mix'], dh2, tk=d, tn=d // 2, out_dtype=BF16,
                        name="dw_out").reshape(N_DEV, d // N_DEV, d))
        tok = hooks.grads_ready(l, 'mlp', blocks)
        dxr, dyr, dwga, dwgx, vec = rec_bwd(drec, sv['hr'], sv['xc'], sv['rest'], small['conv_w'][l],
                                            row(small['conv_b'][l]) + tok, sv['wga'], row(small['b_gate_a'][l]),
                                            sv['wgx'], row(small['b_gate_x'][l]), row(small['lru_L'][l]), rw)
        gs['w_gate_a'][l] = _block_diag_extract(dwga, blk)
        gs['w_gate_x'][l] = _block_diag_extract(dwgx, blk)
        vec = vec.transpose(1, 0, 2).reshape(SUBLANES, rw)
        gs['conv_w'][l] = vec[0:CONV_WIDTH]
        gs['conv_b'][l] = vec[4]
        gs['b_gate_a'][l] = vec[5]
        gs['b_gate_x'][l] = vec[6]
        gs['lru_L'][l] = vec[7]
        dq, dk, dv, drow, dcol = attn_bwd(sv['qkv'], do, sv['o'], sv['lset'], sv['c'], sv['ct'] + tok, nh)
        drow8 = drow[:, 0:2, :].reshape(nh, tp)
        if nh < SUBLANES:
            drow8 = jnp.pad(drow8, ((0, SUBLANES - nh), (0, 0)))
        dfl, dbf = fgate_bwd(drow8, dcol, sv['rest'], bf_pad[l:l + 1], fcol)
        gs['b_f'][l] = dbf[0, 0:nh]
        parts = (dq, dk, dv, dxr, dyr, dfl)
        dh, dg1 = in_proj_bwd(dh2, parts, sv['w_in_t'], sv['wrest_t'], sv['h'], row(small['attn_norm_g'][l]), tm)
        gs['attn_norm_g'][l] = dg1[0]
        tok = jnp.zeros((), F32)
        if l == 0:
            grads = {n: jnp.stack(v) for n, v in gs.items()}
            grads['final_g'] = dgf[0]
            grads['meta'] = dh[0:N_META]
            tok = hooks.small_ready(grads)
        dw_in = dw_in_t(sv['z'], parts, nh, tm2)
        dw_in = dw_in.reshape(N_DEV, dw_in.shape[0] // N_DEV, d) + tok.astype(BF16)
        tok = hooks.grads_ready(l, 'in', dict(w_in=dw_in))

    return loss[0, 0], dh, tok


def prep_weights(g_in, g_out, nh, rw):
    d = g_in.shape[2]
    w_in_t = g_in.reshape(-1, d)
    f0 = 3 * nh * HEAD_DIM
    wrest_t = jnp.concatenate([w_in_t[f0 + nh:f0 + nh + 2 * rw],
                               jnp.pad(w_in_t[f0:f0 + nh], ((0, LANES - nh), (0, 0)))], axis=0)
    return w_in_t, wrest_t, g_out.reshape(d, d)


BIG = ['w_in', 'w_out', 'w_up', 'w_down']
EXCHANGE_GROUPS = {'mlp': ['w_down', 'w_up', 'w_out'], 'in': ['w_in']}
WEIGHTS = ['meta', 'attn_norm_g', 'w_in', 'b_f', 'conv_w', 'conv_b', 'w_gate_a', 'b_gate_a', 'w_gate_x', 'b_gate_x',
           'lru_L', 'attn_out_g', 'rec_out_g', 'w_out', 'mlp_norm_g', 'w_up', 'w_down', 'final_g']


def _set_own(arr, own, me):
    return lax.dynamic_update_slice_in_dim(arr, own[None], me, axis=0)


class _Step:
    def __init__(self, w, nh, rw, me):
        self.w, self.nh, self.rw, self.me = w, nh, rw, me
        depth = w['w_in'].shape[0]
        first = [w['w_in_t'][:, 0, :].astype(BF16), w['w_out'][0].astype(BF16), w['meta'], w['conv_w']]
        self.pending, token = gather_start([first], "gather_start_0")
        zero = token[0, 0].astype(BF16)
        groups = [[w['w_up'][0].astype(BF16) + zero, w['w_down'][0].astype(BF16) + zero]]
        for l in range(1, depth):
            groups.append([w['w_in_t'][:, l, :].astype(BF16) + zero, w['w_out'][l].astype(BF16) + zero])
            groups.append([w['w_up'][l].astype(BF16) + zero, w['w_down'][l].astype(BF16) + zero])
        rest, _ = gather_start(groups, "gather_start_1")
        self.pending += rest
        self.first_after = rest[0][2][0]
        self.gathered = {}
        self.passing = {}
        self.token = jnp.zeros((), F32)
        self.lands = {n: lax.empty((N_DEV,) + w[n].shape, BF16) for n in BIG}
        din8, _, d = w['w_in_t'].shape
        self.lands['w_in'] = lax.empty((N_DEV, depth, din8, d), BF16)
        self.started = []
        self.small = None

    def _pass_on(self, gi, after):
        if gi < len(self.pending) and gi not in self.passing:
            send, recv, srcs, lands = self.pending[gi]
            srcs, lands = gather_wait(send, recv, srcs, lands, after, "gather_wait_%d" % gi)
            fsend, frecv, lands, token = forward_start(lands, "forward_start_%d" % gi)
            self.passing[gi] = (fsend, frecv, srcs, lands)
            self.token = token

    def group(self, gi, after):
        if gi not in self.gathered:
            self._pass_on(gi, after)
            fsend, frecv, srcs, lands = self.passing[gi]
            lands = forward_wait(fsend, frecv, lands, after, "forward_wait_%d" % gi)
            self.gathered[gi] = [_set_own(g, own, self.me) for g, own in zip(lands, srcs)]
            if gi >= 2:
                self._pass_on(gi + 1, lands[0])
        return self.gathered[gi]

    def mixer_weights(self, l, after):
        g = self.group(2 * l, after)
        return (*prep_weights(g[0], g[1], self.nh, self.rw), self.token)

    def mlp_weights(self, l, after):
        g = self.group(2 * l + 1, after)
        return g[0], g[1], self.token

    def grads_ready(self, l, group, blocks):
        names = EXCHANGE_GROUPS[group]
        send, recv, srcs, lands, token = exchange_start(
            [blocks[n] for n in names], [self.lands[n] for n in names], l, "exchange_start_%s_%d" % (group, l))
        for n, a in zip(names, lands):
            self.lands[n] = a
        self.started.append((l, group, send, recv, srcs))
        return token

    def small_ready(self, grads):
        self.small_shapes = {n: grads[n].shape for n in SMALL}
        packed = _pack(grads).astype(BF16)
        send, recv, srcs, lands, token = exchange_start(
            [packed], [lax.empty((N_DEV,) + packed.shape, BF16)], None, "small_start")
        self.small = (send, recv, srcs, lands)
        return token

    def small_sum(self, after):
        send, recv, srcs, lands = self.small
        srcs, lands = exchange_wait(send, recv, srcs, lands, after, None, "small_wait")
        parts = _set_own(lands[0], srcs[0], self.me)
        return _unpack(sum_parts(parts, "sum_small_grads"), self.small_shapes)

    def received(self, group, after):
        names = EXCHANGE_GROUPS[group]
        own = {n: [None] * self.w[n].shape[0] for n in names}
        for l, grp, send, recv, srcs in self.started:
            if grp != group:
                continue
            srcs, lands = exchange_wait(send, recv, srcs, [self.lands[n] for n in names], after, l,
                                        "exchange_wait_%s_%d" % (group, l))
            for n, a, sr in zip(names, lands, srcs):
                self.lands[n] = a
                own[n][l] = sr
        return {n: (self.lands[n], own[n]) for n in names}


def kernel(x, meta, attn_norm_g, w_in, b_f, conv_w, conv_b, w_gate_a, b_gate_a, w_gate_x, b_gate_x, lru_L, attn_out_g, rec_out_g, w_out, mlp_norm_g, w_up, w_down, final_g, loss_target, m_meta, m_attn_norm_g, m_w_in, m_b_f, m_conv_w, m_conv_b, m_w_gate_a, m_b_gate_a, m_w_gate_x, m_b_gate_x, m_lru_L, m_attn_out_g, m_rec_out_g, m_w_out, m_mlp_norm_g, m_w_up, m_w_down, m_final_g, v_meta, v_attn_norm_g, v_w_in, v_b_f, v_conv_w, v_conv_b, v_w_gate_a, v_b_gate_a, v_w_gate_x, v_b_gate_x, v_lru_L, v_attn_out_g, v_rec_out_g, v_w_out, v_mlp_norm_g, v_w_up, v_w_down, v_final_g):
    w = dict(meta=meta, attn_norm_g=attn_norm_g, w_in=w_in, b_f=b_f, conv_w=conv_w, conv_b=conv_b,
             w_gate_a=w_gate_a, b_gate_a=b_gate_a, w_gate_x=w_gate_x, b_gate_x=b_gate_x, lru_L=lru_L,
             attn_out_g=attn_out_g, rec_out_g=rec_out_g, w_out=w_out, mlp_norm_g=mlp_norm_g, w_up=w_up,
             w_down=w_down, final_g=final_g)
    mo = dict(meta=m_meta, attn_norm_g=m_attn_norm_g, w_in=m_w_in, b_f=m_b_f, conv_w=m_conv_w, conv_b=m_conv_b,
              w_gate_a=m_w_gate_a, b_gate_a=m_b_gate_a, w_gate_x=m_w_gate_x, b_gate_x=m_b_gate_x, lru_L=m_lru_L,
              attn_out_g=m_attn_out_g, rec_out_g=m_rec_out_g, w_out=m_w_out, mlp_norm_g=m_mlp_norm_g,
              w_up=m_w_up, w_down=m_w_down, final_g=m_final_g)
    vo = dict(meta=v_meta, attn_norm_g=v_attn_norm_g, w_in=v_w_in, b_f=v_b_f, conv_w=v_conv_w, conv_b=v_conv_b,
              w_gate_a=v_w_gate_a, b_gate_a=v_b_gate_a, w_gate_x=v_w_gate_x, b_gate_x=v_b_gate_x, lru_L=v_lru_L,
              attn_out_g=v_attn_out_g, rec_out_g=v_rec_out_g, w_out=v_w_out, mlp_norm_g=v_mlp_norm_g,
              w_up=v_w_up, w_down=v_w_down, final_g=v_final_g)
    depth = w_in.shape[0]
    nh = b_f.shape[1]
    rw = conv_b.shape[1]
    me = 4 * lax.axis_index("x") + 2 * lax.axis_index("y") + lax.axis_index("c")

    w['w_in_t'] = jnp.transpose(w_in, (2, 0, 1))
    swap = lambda a: jnp.swapaxes(a, 1, 2)
    step = _Step(w, nh, rw, me)
    g0 = step.group(0, step.first_after)
    meta_full = g0[2].transpose(1, 0, 2).reshape(N_META, -1)
    conv_full = g0[3].transpose(1, 2, 0, 3).reshape(depth, CONV_WIDTH, rw)
    small = {n: w[n] for n in SMALL}
    small['conv_w'] = conv_full

    loss_part, dh0, tok = local_step(x[0], loss_target[0], meta_full, small, step)
    loss = lax.psum(loss_part, ("x", "y", "c"))
    grad_x = dh0[N_META:N_META + x.shape[1]][None]

    out_g, out_d, out_m, out_v = {}, {}, {}, {}
    me1 = me.reshape(1).astype(jnp.int32)

    def update_big(group, after):
        for n, (r, owns) in step.received(group, after).items():
            if n == 'w_in':
                out = sum_adamw_t(r, owns, me1, swap(w[n]), swap(mo[n]), swap(vo[n]), "adamw_w_in")
                out = [swap(a) for a in out]
            else:
                shp = w[n].shape
                rows, cols = shp[0] * shp[1], shp[2]
                tr = min(512 if cols <= 512 else 256, shp[1])
                out = sum_adamw(r.reshape(N_DEV, rows, cols), owns, me1, w[n].reshape(rows, cols),
                                mo[n].reshape(rows, cols), vo[n].reshape(rows, cols), tr, "adamw_" + n)
                out = [a.reshape(shp) for a in out]
            out_g[n], out_d[n], out_m[n], out_v[n] = out
            after = out[0]
        return after

    update_big('mlp', step.started[-1][4][0])

    gsum = step.small_sum([out_g[n] for n in EXCHANGE_GROUPS['mlp']])
    gsum['meta'] = lax.dynamic_slice_in_dim(gsum['meta'], me * meta.shape[1], meta.shape[1], axis=1)
    gsum['conv_w'] = lax.dynamic_slice_in_dim(gsum['conv_w'], me * conv_w.shape[2], conv_w.shape[2], axis=2)
    as2d = lambda a: a.reshape(-1, a.shape[-1])
    deltas, new_m, new_v = adamw_group([as2d(gsum[n]) for n in SMALL], [as2d(w[n]) for n in SMALL],
                                       [as2d(mo[n]) for n in SMALL], [as2d(vo[n]) for n in SMALL], "adamw_small")
    for i, n in enumerate(SMALL):
        out_g[n] = gsum[n]
        out_d[n], out_m[n], out_v[n] = [a[i].reshape(w[n].shape) for a in (deltas, new_m, new_v)]

    update_big('in', deltas[0])

    return (loss, grad_x, *[out_g[n] for n in WEIGHTS], *[out_d[n] for n in WEIGHTS],
            *[out_m[n] for n in WEIGHTS], *[out_v[n] for n in WEIGHTS])
```

```python
import functools
import math

import jax
import jax.numpy as jnp
from jax import lax
from jax.experimental import pallas as pl
from jax.experimental.pallas import tpu as pltpu

F32 = jnp.float32
BF16 = jnp.bfloat16

N_DEV = 8
N_META = 16
HEAD_DIM = 64
CONV_WIDTH = 4
RG_C = 8.0
NORM_EPS = 1e-6
LANES = 128
SUBLANES = 8
ATT_BLOCK = 128
ATT_TQ = 512
NEG_BIG = -1e30
ATT_SCALE = 1.0 / math.sqrt(HEAD_DIM)

ADAM_LR = 0.001
ADAM_B1 = 0.9
ADAM_B2 = 0.999
ADAM_EPS = 1e-08
ADAM_WD = 0.01
ADAM_STEP = 10

VMEM_LIMIT_BYTES = 56 * 1024 * 1024
MESH = pl.DeviceIdType.MESH
ANY = pl.BlockSpec(memory_space=pl.ANY)


def _cparams(*sem):
    return pltpu.CompilerParams(dimension_semantics=sem if sem else None,
                                vmem_limit_bytes=VMEM_LIMIT_BYTES)


def _dot(a, b):
    return jnp.dot(a, b, preferred_element_type=F32)


def _dot_nt(a, b):
    return lax.dot_general(a, b, (((1,), (1,)), ((), ())), preferred_element_type=F32)


def _dot_tn(a, b):
    return lax.dot_general(a, b, (((0,), (0,)), ((), ())), preferred_element_type=F32)


def _sigmoid(x):
    return 0.5 * (1.0 + jnp.tanh(0.5 * x))


def _log_sigmoid(x):
    return jnp.minimum(x, 0.0) - jnp.log(1.0 + jnp.exp(-jnp.abs(x)))


def _expm1(x):
    series = x * (1.0 + x * (0.5 + x * (1.0 / 6.0 + x * (1.0 / 24.0))))
    return jnp.where(jnp.abs(x) < 1e-2, series, jnp.exp(x) - 1.0)


_GELU_K = math.sqrt(2.0 / math.pi)
_GELU_C = 0.044715


def _gelu(x):
    t = jnp.tanh(_GELU_K * (x + _GELU_C * x * x * x))
    return 0.5 * x * (1.0 + t)


def _gelu_and_grad(x):
    x2 = x * x
    t = jnp.tanh(_GELU_K * (x + _GELU_C * x2 * x))
    half = 0.5 * (1.0 + t)
    return x * half, half + 0.5 * x * (1.0 - t * t) * _GELU_K * (1.0 + 3.0 * _GELU_C * x2)


def _split3_dot(tri, x):
    hi = x.astype(BF16)
    r1 = x - hi.astype(F32)
    mid = r1.astype(BF16)
    lo = (r1 - mid.astype(F32)).astype(BF16)
    return _dot(tri, hi) + _dot(tri, mid) + _dot(tri, lo)


def _dot_split3(x, sel):
    hi = x.astype(BF16)
    r1 = x - hi.astype(F32)
    mid = r1.astype(BF16)
    lo = (r1 - mid.astype(F32)).astype(BF16)
    return _dot(hi, sel) + _dot(mid, sel) + _dot(lo, sel)


def _rms_fwd(x, g):
    r = lax.rsqrt(jnp.mean(x * x, axis=-1, keepdims=True) + NORM_EPS)
    return x * r * g


def _rms_bwd(x, g, dy):
    r = lax.rsqrt(jnp.mean(x * x, axis=-1, keepdims=True) + NORM_EPS)
    xn = x * r
    dxn = dy * g
    dx = r * (dxn - xn * jnp.mean(dxn * xn, axis=-1, keepdims=True))
    return dx, jnp.sum(dy * xn, axis=0, keepdims=True)


def _accumulate(ref, val, first):
    @pl.when(first)
    def _():
        ref[...] = val

    @pl.when(jnp.logical_not(first))
    def _():
        ref[...] += val


def in_proj(h, g1, w_in_t, wrest_t, nq, tm):
    tp, d = h.shape
    nr = wrest_t.shape[0]

    def body(h_ref, g_ref, wq_ref, wr_ref, z_ref, qkv_ref, rest_ref):
        z = _rms_fwd(h_ref[...], g_ref[...]).astype(BF16)
        z_ref[...] = z
        qkv_ref[...] = _dot_nt(z, wq_ref[...]).astype(BF16)
        rest_ref[...] = _dot_nt(z, wr_ref[...])

    return pl.pallas_call(
        body, name="in_proj", grid=(tp // tm,),
        in_specs=[pl.BlockSpec((tm, d), lambda i: (i, 0)),
                  pl.BlockSpec((1, d), lambda i: (0, 0)),
                  pl.BlockSpec((nq, d), lambda i: (0, 0)),
                  pl.BlockSpec((nr, d), lambda i: (0, 0))],
        out_specs=[pl.BlockSpec((tm, d), lambda i: (i, 0)),
                   pl.BlockSpec((tm, nq), lambda i: (i, 0)),
                   pl.BlockSpec((tm, nr), lambda i: (i, 0))],
        out_shape=[jax.ShapeDtypeStruct((tp, d), BF16),
                   jax.ShapeDtypeStruct((tp, nq), BF16),
                   jax.ShapeDtypeStruct((tp, nr), F32)],
        compiler_params=_cparams("parallel"),
    )(h, g1, w_in_t, wrest_t)


def fgate_fwd(rest, bf_pad, fcol):
    tp = rest.shape[0]
    nb = tp // ATT_BLOCK

    def body(f_ref, b_ref, c_ref, ct_ref):
        r_i = lax.broadcasted_iota(jnp.int32, (ATT_BLOCK, ATT_BLOCK), 0)
        c_i = lax.broadcasted_iota(jnp.int32, (ATT_BLOCK, ATT_BLOCK), 1)
        tri = (r_i >= c_i).astype(BF16)
        carry = jnp.zeros((1, LANES), F32)
        for i in range(nb):
            sl = slice(i * ATT_BLOCK, (i + 1) * ATT_BLOCK)
            lf = _log_sigmoid(f_ref[sl, :] + b_ref[...])
            cs = _split3_dot(tri, lf) + carry
            carry = cs[ATT_BLOCK - 1:ATT_BLOCK, :]
            c_ref[sl, :] = cs
            ct_ref[:, sl] = cs.T[0:SUBLANES, :]

    return pl.pallas_call(
        body, name="fgate_fwd", grid=(1,),
        in_specs=[pl.BlockSpec((tp, LANES), lambda i: (0, fcol)),
                  pl.BlockSpec((1, LANES), lambda i: (0, 0))],
        out_specs=[pl.BlockSpec((tp, LANES), lambda i: (0, 0)),
                   pl.BlockSpec((SUBLANES, tp), lambda i: (0, 0))],
        out_shape=[jax.ShapeDtypeStruct((tp, LANES), F32),
                   jax.ShapeDtypeStruct((SUBLANES, tp), F32)],
        compiler_params=_cparams("arbitrary"),
    )(rest, bf_pad)


def _pick_col(blk, head):
    lane = lax.broadcasted_iota(jnp.int32, blk.shape, 1)
    return jnp.sum(jnp.where(lane == head, blk, 0.0), axis=1, keepdims=True)


def _pick_row(blk, head):
    sub = lax.broadcasted_iota(jnp.int32, blk.shape, 0)
    return jnp.sum(jnp.where(sub == head, blk, 0.0), axis=0, keepdims=True)


def _att_tiles(tp):
    out, r0 = [], 0
    while r0 < tp:
        rows = min(ATT_TQ, tp - r0)
        out.append((r0, rows, r0 + rows))
        r0 += rows
    return out


def attn_fwd(qkv, c, ct, nh):
    tp = qkv.shape[0]
    npair = nh // 2
    tiles = _att_tiles(tp)

    def body(q_ref, k_ref, v_ref, c_ref, ct_ref, o_ref, lset_ref):
        p = pl.program_id(0)
        lset_ref[...] = jnp.zeros_like(lset_ref)
        for r0, nr, nk in tiles:
            rs = slice(r0, r0 + nr)
            causal = (r0 + lax.broadcasted_iota(jnp.int32, (nr, nk), 0)
                      >= lax.broadcasted_iota(jnp.int32, (nr, nk), 1))
            cblk = c_ref[rs, :]
            ctb = ct_ref[:, 0:nk]
            for hh in range(2):
                head = 2 * p + hh
                hs = slice(hh * HEAD_DIM, (hh + 1) * HEAD_DIM)
                q = q_ref[rs, hs] * ATT_SCALE
                s = _dot_nt(q, k_ref[0:nk, hs]) + (_pick_col(cblk, head) - _pick_row(ctb, head))
                s = jnp.where(causal, s, NEG_BIG)
                m = jnp.max(s, axis=1, keepdims=True)
                pm = jnp.exp(s - m)
                l = jnp.sum(pm, axis=1, keepdims=True)
                o_ref[rs, hs] = _dot(pm.astype(BF16), v_ref[0:nk, hs]) / l
                lse = m + jnp.log(l)
                lset_ref[hh:hh + 1, rs] = jnp.broadcast_to(lse, (nr, LANES)).T[0:1, :]

    pair = lambda p: (0, p)
    return pl.pallas_call(
        body, name="attn_fwd", grid=(npair,),
        in_specs=[pl.BlockSpec((tp, LANES), pair),
                  pl.BlockSpec((tp, LANES), lambda p: (0, npair + p)),
                  pl.BlockSpec((tp, LANES), lambda p: (0, 2 * npair + p)),
                  pl.BlockSpec((tp, LANES), lambda p: (0, 0)),
                  pl.BlockSpec((SUBLANES, tp), lambda p: (0, 0))],
        out_specs=[pl.BlockSpec((tp, LANES), pair),
                   pl.BlockSpec((None, SUBLANES, tp), lambda p: (p, 0, 0))],
        out_shape=[jax.ShapeDtypeStruct((tp, nh * HEAD_DIM), F32),
                   jax.ShapeDtypeStruct((npair, SUBLANES, tp), F32)],
        compiler_params=_cparams("parallel"),
    )(qkv, qkv, qkv, c, ct)


def _shift_down(x, k, n):
    if k == 0:
        return x
    rows = lax.broadcasted_iota(jnp.int32, x.shape, 0)
    return jnp.where(rows >= k, pltpu.roll(x, k, 0), 0.0)


def _shift_up(x, k, n):
    if k == 0:
        return x
    rows = lax.broadcasted_iota(jnp.int32, x.shape, 0)
    return jnp.where(rows < n - k, pltpu.roll(x, n - k, 0), 0.0)


def _conv_fwd(xr, cw_ref, cb_ref, n):
    xc = cw_ref[CONV_WIDTH - 1:CONV_WIDTH, :] * xr + cb_ref[...]
    for k in range(1, CONV_WIDTH):
        xc = xc + cw_ref[CONV_WIDTH - 1 - k:CONV_WIDTH - k, :] * _shift_down(xr, k, n)
    return xc


def _gates(xc, wga_ref, bga_ref, wgx_ref, bgx_ref, l_ref):
    xcb = xc.astype(BF16)
    r = _sigmoid(_dot(xcb, wga_ref[...]) + bga_ref[...])
    ig = _sigmoid(_dot(xcb, wgx_ref[...]) + bgx_ref[...])
    ls = _log_sigmoid(l_ref[...])
    log_a = RG_C * r * ls
    a = jnp.exp(log_a)
    mult = jnp.sqrt(-_expm1(2.0 * log_a))
    return xcb, r, ig, ls, log_a, a, mult


SCAN_UNROLL = 4


def _scan_rows(a_s, u_s, out_ref, n, reverse):
    nt = n // SUBLANES
    per = SCAN_UNROLL if nt % SCAN_UNROLL == 0 else 1
    row = lax.broadcasted_iota(jnp.int32, (SUBLANES, LANES), 0)
    last = 0 if reverse else SUBLANES - 1

    def tile_scan(a, u):
        for d in (1, 2, 4):
            if reverse:
                keep = row < SUBLANES - d
                sh = SUBLANES - d
            else:
                keep = row >= d
                sh = d
            a_sh = jnp.where(keep, pltpu.roll(a, sh, 0), 1.0)
            u_sh = jnp.where(keep, pltpu.roll(u, sh, 0), 0.0)
            u = a * u_sh + u
            a = a * a_sh
        return a, u

    def step(t, carry):
        tiles = []
        for k in range(per):
            tt = t * per + k
            if reverse:
                tt = nt - 1 - tt
            off = pl.multiple_of(tt * SUBLANES, SUBLANES)
            a, u = tile_scan(a_s[pl.ds(off, SUBLANES), :], u_s[pl.ds(off, SUBLANES), :])
            tiles.append((off, a, u))
        for off, a, u in tiles:
            out_ref[pl.ds(off, SUBLANES), :] = u + a * carry
            carry = u[last:last + 1, :] + a[last:last + 1, :] * carry
        return carry

    lax.fori_loop(0, nt // per, step, jnp.zeros((1, LANES), F32))


def rec_fwd(rest, convw, convb, wga, bga, wgx, bgx, lru, rw):
    tp = rest.shape[0]
    ng = rw // LANES

    def body(xr_ref, yr_ref, cw_ref, cb_ref, wga_ref, bga_ref, wgx_ref, bgx_ref, l_ref,
             rec_ref, hr_ref, xc_ref, r_ref, ig_ref, a_ref, mult_ref, u_s):
        xc = _conv_fwd(xr_ref[...], cw_ref, cb_ref, tp)
        xc_ref[...] = xc
        _, r, ig, ls, log_a, a, mult = _gates(xc, wga_ref, bga_ref, wgx_ref, bgx_ref, l_ref)
        r_ref[...] = r
        ig_ref[...] = ig
        a_ref[...] = a
        mult_ref[...] = mult
        u_s[...] = mult * ig * xc
        _scan_rows(a_ref, u_s, hr_ref, tp, reverse=False)
        rec_ref[...] = hr_ref[...] * _gelu(yr_ref[...])

    col = lambda g: (0, g)
    vec = pl.BlockSpec((1, LANES), col)
    big = pl.BlockSpec((tp, LANES), col)
    return pl.pallas_call(
        body, name="rec_fwd", grid=(ng,),
        in_specs=[big, pl.BlockSpec((tp, LANES), lambda g: (0, ng + g)),
                  pl.BlockSpec((CONV_WIDTH, LANES), col), vec,
                  pl.BlockSpec((None, LANES, LANES), lambda g: (g, 0, 0)), vec,
                  pl.BlockSpec((None, LANES, LANES), lambda g: (g, 0, 0)), vec, vec],
        out_specs=[big] * 7,
        out_shape=[jax.ShapeDtypeStruct((tp, rw), F32)] * 7,
        scratch_shapes=[pltpu.VMEM((tp, LANES), F32)],
        compiler_params=_cparams("parallel"),
    )(rest, rest, convw, convb, wga, bga, wgx, bgx, lru)


def out_proj(h, o, rec, ga, gr, wout, g2, tm):
    tp, d = h.shape
    aw, rw = o.shape[1], rec.shape[1]

    def body(h_ref, o_ref, rec_ref, ga_ref, gr_ref, w_ref, g2_ref, h2_ref, mix_ref, z2_ref):
        mix_ref[:, 0:aw] = _rms_fwd(o_ref[...], ga_ref[...]).astype(BF16)
        mix_ref[:, aw:aw + rw] = _rms_fwd(rec_ref[...], gr_ref[...]).astype(BF16)
        h2 = h_ref[...] + _dot(mix_ref[...], w_ref[...])
        h2_ref[...] = h2
        z2_ref[...] = _rms_fwd(h2, g2_ref[...]).astype(BF16)

    row = lambda i: (i, 0)
    fix = lambda i: (0, 0)
    return pl.pallas_call(
        body, name="out_proj", grid=(tp // tm,),
        in_specs=[pl.BlockSpec((tm, d), row), pl.BlockSpec((tm, aw), row), pl.BlockSpec((tm, rw), row),
                  pl.BlockSpec((1, aw), fix), pl.BlockSpec((1, rw), fix),
                  pl.BlockSpec((d, d), fix), pl.BlockSpec((1, d), fix)],
        out_specs=[pl.BlockSpec((tm, d), row)] * 3,
        out_shape=[jax.ShapeDtypeStruct((tp, d), F32), jax.ShapeDtypeStruct((tp, d), BF16),
                   jax.ShapeDtypeStruct((tp, d), BF16)],
        compiler_params=_cparams("parallel"),
    )(h, o, rec, ga, gr, wout, g2)


MLP_BLOCKS = 2


def mlp_fwd(z2, h2, gup, gdown, tm):
    tp, d = h2.shape
    nf = gup.shape[0]
    tf = gup.shape[2]
    nb = MLP_BLOCKS if nf % MLP_BLOCKS == 0 else 1
    nj = nf // nb

    def body(z_ref, h_ref, wu_ref, wd_ref, u_ref, h3_ref, acc):
        j = pl.program_id(1)
        z = z_ref[...]
        part = None
        for b in range(nb):
            u = jnp.maximum(_dot(z, wu_ref[b]), 0.0)
            u_ref[:, b * tf:(b + 1) * tf] = u.astype(BF16)
            p = _dot((u * u).astype(BF16), wd_ref[b])
            part = p if part is None else part + p

        @pl.when(j == 0)
        def _():
            acc[...] = h_ref[...] + part

        @pl.when(j > 0)
        def _():
            acc[...] += part

        @pl.when(j == nj - 1)
        def _():
            h3_ref[...] = acc[...]

    return pl.pallas_call(
        body, name="mlp_fwd", grid=(tp // tm, nj),
        in_specs=[pl.BlockSpec((tm, d), lambda i, j: (i, 0)),
                  pl.BlockSpec((tm, d), lambda i, j: (i, 0)),
                  pl.BlockSpec((nb, d, tf), lambda i, j: (j, 0, 0)),
                  pl.BlockSpec((nb, tf, d), lambda i, j: (j, 0, 0))],
        out_specs=[pl.BlockSpec((tm, nb * tf), lambda i, j: (i, j)),
                   pl.BlockSpec((tm, d), lambda i, j: (i, 0))],
        out_shape=[jax.ShapeDtypeStruct((tp, nf * tf), BF16), jax.ShapeDtypeStruct((tp, d), F32)],
        scratch_shapes=[pltpu.VMEM((tm, d), F32)],
        compiler_params=_cparams("parallel", "arbitrary"),
    )(z2, h2, gup, gdown)


def loss_head(h, gf, tgt, t_real, tm):
    tp, d = h.shape

    def body(h_ref, g_ref, t_ref, dh_ref, dg_ref, loss_ref):
        i = pl.program_id(0)
        x = h_ref[...]
        g = g_ref[...]
        r = lax.rsqrt(jnp.mean(x * x, axis=-1, keepdims=True) + NORM_EPS)
        xn = x * r
        rows = i * tm + lax.broadcasted_iota(jnp.int32, (tm, 1), 0)
        valid = jnp.logical_and(rows >= N_META, rows < t_real)
        e = jnp.where(valid, xn * g - t_ref[...], 0.0)
        part = 0.5 * jnp.sum(jnp.sum(e * e, axis=1, keepdims=True) / d, axis=0, keepdims=True)
        dy = e / d
        dxn = dy * g
        dh_ref[...] = r * (dxn - xn * jnp.mean(dxn * xn, axis=-1, keepdims=True))
        _accumulate(dg_ref, jnp.sum(dy * xn, axis=0, keepdims=True), i == 0)
        _accumulate(loss_ref, jnp.broadcast_to(part, (1, LANES)), i == 0)

    row = lambda i: (i, 0)
    fix = lambda i: (0, 0)
    return pl.pallas_call(
        body, name="loss_head", grid=(tp // tm,),
        in_specs=[pl.BlockSpec((tm, d), row), pl.BlockSpec((1, d), fix), pl.BlockSpec((tm, d), row)],
        out_specs=[pl.BlockSpec((tm, d), row), pl.BlockSpec((1, d), fix), pl.BlockSpec((1, LANES), fix)],
        out_shape=[jax.ShapeDtypeStruct((tp, d), F32), jax.ShapeDtypeStruct((1, d), F32),
                   jax.ShapeDtypeStruct((1, LANES), F32)],
        compiler_params=_cparams("arbitrary"),
    )(h, gf, tgt)


def mlp_bwd(dh, u, h2, g2, gup, gdown, tm):
    tp, d = dh.shape
    nf = gup.shape[0]
    tf = gup.shape[2]
    nb = MLP_BLOCKS if nf % MLP_BLOCKS == 0 else 1
    nj = nf // nb
    ni = tp // tm

    def body(dh_ref, u_ref, h2_ref, g_ref, wu_ref, wd_ref, dup_ref, dh2_ref, dg_ref, dhb, acc):
        i = pl.program_id(0)
        j = pl.program_id(1)

        @pl.when(j == 0)
        def _():
            dhb[...] = dh_ref[...].astype(BF16)

        part = None
        for b in range(nb):
            cols = slice(b * tf, (b + 1) * tf)
            dup = (_dot_nt(dhb[...], wd_ref[b]) * (2.0 * u_ref[:, cols].astype(F32))).astype(BF16)
            dup_ref[:, cols] = dup
            p = _dot_nt(dup, wu_ref[b])
            part = p if part is None else part + p
        _accumulate(acc, part, j == 0)

        @pl.when(j == nj - 1)
        def _():
            dx, dg = _rms_bwd(h2_ref[...], g_ref[...], acc[...])
            dh2_ref[...] = dh_ref[...] + dx
            _accumulate(dg_ref, dg, i == 0)

    return pl.pallas_call(
        body, name="mlp_bwd", grid=(ni, nj),
        in_specs=[pl.BlockSpec((tm, d), lambda i, j: (i, 0)),
                  pl.BlockSpec((tm, nb * tf), lambda i, j: (i, j)),
                  pl.BlockSpec((tm, d), lambda i, j: (i, 0)),
                  pl.BlockSpec((1, d), lambda i, j: (0, 0)),
                  pl.BlockSpec((nb, d, tf), lambda i, j: (j, 0, 0)),
                  pl.BlockSpec((nb, tf, d), lambda i, j: (j, 0, 0))],
        out_specs=[pl.BlockSpec((tm, nb * tf), lambda i, j: (i, j)),
                   pl.BlockSpec((tm, d), lambda i, j: (i, 0)),
                   pl.BlockSpec((1, d), lambda i, j: (0, 0)),
                   pl.BlockSpec((tm, d), lambda i, j: (i, 0))],
        out_shape=[jax.ShapeDtypeStruct((tp, nf * tf), BF16), jax.ShapeDtypeStruct((tp, d), F32),
                   jax.ShapeDtypeStruct((1, d), F32), jax.ShapeDtypeStruct((tp, d), BF16)],
        scratch_shapes=[pltpu.VMEM((tm, d), F32)],
        compiler_params=_cparams("arbitrary", "arbitrary"),
    )(dh, u, h2, g2, gup, gdown)


def mm_tn(a, b, *, tk, tn, out_dtype, name, square_a=False, blocked_n=False):
    rows, kk = a.shape
    nn = b.shape[1]
    keep_at = nn // tn > 1

    def a_tile(a_ref):
        av = a_ref[...]
        if square_a:
            af = av.astype(F32)
            av = af * af
        return av.astype(BF16)

    def body(a_ref, b_ref, o_ref, *scratch):
        if keep_at:
            at, = scratch

            @pl.when(pl.program_id(1) == 0)
            def _():
                at[...] = a_tile(a_ref).T

            o_ref[...] = _dot(at[...], b_ref[...].astype(BF16)).astype(out_dtype)
        else:
            o_ref[...] = _dot_tn(a_tile(a_ref), b_ref[...].astype(BF16)).astype(out_dtype)

    if blocked_n:
        out_spec = pl.BlockSpec((None, tk, tn), lambda k, n: (n, k, 0))
        out_shape = jax.ShapeDtypeStruct((nn // tn, kk, tn), out_dtype)
    else:
        out_spec = pl.BlockSpec((tk, tn), lambda k, n: (k, n))
        out_shape = jax.ShapeDtypeStruct((kk, nn), out_dtype)
    return pl.pallas_call(
        body, name=name, grid=(kk // tk, nn // tn),
        in_specs=[pl.BlockSpec((rows, tk), lambda k, n: (0, k)),
                  pl.BlockSpec((rows, tn), lambda k, n: (0, n))],
        out_specs=out_spec, out_shape=out_shape,
        scratch_shapes=[pltpu.VMEM((tk, rows), BF16)] if keep_at else [],
        compiler_params=_cparams("parallel", "arbitrary"),
    )(a, b)


def dw_mlp(u, dhb, z2, dup, tf):
    rows, dff = u.shape
    d = z2.shape[1]
    nf = dff // tf

    def body(u_ref, dh_ref, z_ref, dup_ref, dwd_ref, dwu_ref, zt):
        @pl.when(pl.program_id(0) == 0)
        def _():
            zt[...] = z_ref[...].T

        uf = u_ref[...].astype(F32)
        dwd_ref[...] = _dot_tn((uf * uf).astype(BF16), dh_ref[...]).astype(BF16)
        dwu_ref[...] = _dot(zt[...], dup_ref[...]).astype(BF16)

    col = lambda j: (0, j)
    fix = lambda j: (0, 0)
    return pl.pallas_call(
        body, name="dw_mlp", grid=(nf,),
        in_specs=[pl.BlockSpec((rows, tf), col), pl.BlockSpec((rows, d), fix),
                  pl.BlockSpec((rows, d), fix), pl.BlockSpec((rows, tf), col)],
        out_specs=[pl.BlockSpec((None, tf, d), lambda j: (j, 0, 0)),
                   pl.BlockSpec((None, d, tf), lambda j: (j, 0, 0))],
        out_shape=[jax.ShapeDtypeStruct((nf, tf, d), BF16), jax.ShapeDtypeStruct((nf, d, tf), BF16)],
        scratch_shapes=[pltpu.VMEM((d, rows), BF16)],
        compiler_params=_cparams("arbitrary"),
    )(u, dhb, z2, dup)


def out_proj_bwd(dh2, o, rec, ga, gr, wout, tm):
    tp, d = dh2.shape
    aw, rw = o.shape[1], rec.shape[1]

    def body(dh_ref, o_ref, rec_ref, ga_ref, gr_ref, w_ref, do_ref, drec_ref, dga_ref, dgr_ref):
        i = pl.program_id(0)
        dmix = _dot_nt(dh_ref[...].astype(BF16), w_ref[...])
        do, dga = _rms_bwd(o_ref[...], ga_ref[...], dmix[:, 0:aw])
        drec, dgr = _rms_bwd(rec_ref[...], gr_ref[...], dmix[:, aw:aw + rw])
        do_ref[...] = do
        drec_ref[...] = drec
        _accumulate(dga_ref, dga, i == 0)
        _accumulate(dgr_ref, dgr, i == 0)

    row = lambda i: (i, 0)
    fix = lambda i: (0, 0)
    return pl.pallas_call(
        body, name="out_proj_bwd", grid=(tp // tm,),
        in_specs=[pl.BlockSpec((tm, d), row), pl.BlockSpec((tm, aw), row), pl.BlockSpec((tm, rw), row),
                  pl.BlockSpec((1, aw), fix), pl.BlockSpec((1, rw), fix), pl.BlockSpec((d, d), fix)],
        out_specs=[pl.BlockSpec((tm, aw), row), pl.BlockSpec((tm, rw), row),
                   pl.BlockSpec((1, aw), fix), pl.BlockSpec((1, rw), fix)],
        out_shape=[jax.ShapeDtypeStruct((tp, aw), F32), jax.ShapeDtypeStruct((tp, rw), F32),
                   jax.ShapeDtypeStruct((1, aw), F32), jax.ShapeDtypeStruct((1, rw), F32)],
        compiler_params=_cparams("arbitrary"),
    )(dh2, o, rec, ga, gr, wout)


def rec_bwd(drec, hr, xc, gates, rest, convw, wga, wgx, lru, rw):
    tp = rest.shape[0]
    ng = rw // LANES

    def body(drec_ref, hr_ref, xc_ref, r_ref, ig_ref, a_ref, mult_ref, xr_ref, yr_ref, cw_ref, wga_ref, wgx_ref,
             l_ref, dxr_ref, dyr_ref, dwga_ref, dwgx_ref, vec_ref, a_s, u_s, lam_s):
        xc = xc_ref[...]
        h = hr_ref[...]
        drec = drec_ref[...]
        r, ig, a, mult = r_ref[...], ig_ref[...], a_ref[...], mult_ref[...]
        xcb = xc.astype(BF16)
        ls = _log_sigmoid(l_ref[...])
        gelu, gelu_grad = _gelu_and_grad(yr_ref[...])
        dyr_ref[...] = (drec * h * gelu_grad).astype(BF16)
        a_s[...] = _shift_up(a, 1, tp)
        u_s[...] = drec * gelu
        _scan_rows(a_s, u_s, lam_s, tp, reverse=True)
        lam = lam_s[...]
        da = lam * _shift_down(h, 1, tp)
        dmult = lam * ig * xc
        dig = lam * mult * xc
        dxc = lam * mult * ig
        dlog_a = da * a - dmult * (a * a) / mult
        dr = dlog_a * (RG_C * ls)
        dl = jnp.sum(dlog_a * (RG_C * r), axis=0, keepdims=True) * _sigmoid(-l_ref[...])
        dpa = dr * r * (1.0 - r)
        dpx = dig * ig * (1.0 - ig)
        dpab = dpa.astype(BF16)
        dpxb = dpx.astype(BF16)
        dxc = dxc + _dot_nt(dpab, wga_ref[...]) + _dot_nt(dpxb, wgx_ref[...])
        dwga_ref[...] = _dot_tn(xcb, dpab)
        dwgx_ref[...] = _dot_tn(xcb, dpxb)
        xr = xr_ref[...]
        dxr = cw_ref[CONV_WIDTH - 1:CONV_WIDTH, :] * dxc
        for k in range(1, CONV_WIDTH):
            dxr = dxr + cw_ref[CONV_WIDTH - 1 - k:CONV_WIDTH - k, :] * _shift_up(dxc, k, tp)
        dxr_ref[...] = dxr.astype(BF16)
        for k in range(CONV_WIDTH):
            vec_ref[k:k + 1, :] = jnp.sum(dxc * _shift_down(xr, CONV_WIDTH - 1 - k, tp), axis=0, keepdims=True)
        vec_ref[4:5, :] = jnp.sum(dxc, axis=0, keepdims=True)
        vec_ref[5:6, :] = jnp.sum(dpa, axis=0, keepdims=True)
        vec_ref[6:7, :] = jnp.sum(dpx, axis=0, keepdims=True)
        vec_ref[7:8, :] = dl

    col = lambda g: (0, g)
    vec = pl.BlockSpec((1, LANES), col)
    big = pl.BlockSpec((tp, LANES), col)
    sq = pl.BlockSpec((None, LANES, LANES), lambda g: (g, 0, 0))
    return pl.pallas_call(
        body, name="rec_bwd", grid=(ng,),
        in_specs=[big] * 8 + [pl.BlockSpec((tp, LANES), lambda g: (0, ng + g)),
                                pl.BlockSpec((CONV_WIDTH, LANES), col), sq, sq, vec],
        out_specs=[big, big, sq, sq, pl.BlockSpec((None, SUBLANES, LANES), lambda g: (g, 0, 0))],
        out_shape=[jax.ShapeDtypeStruct((tp, rw), BF16), jax.ShapeDtypeStruct((tp, rw), BF16),
                   jax.ShapeDtypeStruct((ng, LANES, LANES), F32), jax.ShapeDtypeStruct((ng, LANES, LANES), F32),
                   jax.ShapeDtypeStruct((ng, SUBLANES, LANES), F32)],
        scratch_shapes=[pltpu.VMEM((tp, LANES), F32)] * 3,
        compiler_params=_cparams("parallel"),
    )(drec, hr, xc, *gates, rest, rest, convw, wga, wgx, lru)


def attn_bwd(qkv, do, o, lset, c, ct, nh):
    tp = qkv.shape[0]
    npair = nh // 2
    aw = nh * HEAD_DIM
    tiles = _att_tiles(tp)

    def body(q_ref, k_ref, v_ref, do_ref, o_ref, lset_ref, c_ref, ct_ref,
             dq_ref, dk_ref, dv_ref, drow_ref, dcol_ref, dk_acc, dv_acc, dq_t):
        p = pl.program_id(0)
        k_t = k_ref[...].T
        dk_acc[...] = jnp.zeros_like(dk_acc)
        dv_acc[...] = jnp.zeros_like(dv_acc)
        dcol_ref[...] = jnp.zeros_like(dcol_ref)
        drow_ref[...] = jnp.zeros_like(drow_ref)
        for r0, nr, nk in tiles:
            rs = slice(r0, r0 + nr)
            causal = (r0 + lax.broadcasted_iota(jnp.int32, (nk, nr), 1)
                      >= lax.broadcasted_iota(jnp.int32, (nk, nr), 0))
            cblk = c_ref[0:nk, :]
            ctb = ct_ref[:, rs]
            for hh in range(2):
                head = 2 * p + hh
                hs = slice(hh * HEAD_DIM, (hh + 1) * HEAD_DIM)
                q = q_ref[rs, hs]
                k = k_ref[0:nk, hs]
                dof = do_ref[rs, hs]
                do16 = dof.astype(BF16)
                delta = jnp.sum(dof * o_ref[rs, hs], axis=1, keepdims=True)
                delta_row = jnp.broadcast_to(delta, (nr, LANES)).T[0:1, :]
                s_t = _dot_nt(k, q * ATT_SCALE) + (_pick_row(ctb, head) - _pick_col(cblk, head))
                p_t = jnp.where(causal, jnp.exp(s_t - lset_ref[hh:hh + 1, rs]), 0.0)
                ds_t = p_t * (_dot_nt(v_ref[0:nk, hs], do16) - delta_row)
                p16 = p_t.astype(BF16)
                ds16 = ds_t.astype(BF16)
                dv_acc[0:nk, hs] += _dot(p16, do16)
                dk_acc[0:nk, hs] += _dot(ds16, q) * ATT_SCALE
                dq_t[hs, rs] = _dot(k_t[hs, 0:nk], ds16)
                drow_ref[hh:hh + 1, rs] = jnp.sum(ds_t, axis=0, keepdims=True)
                dcol_ref[0:nk, hs] -= jnp.broadcast_to(jnp.sum(ds_t, axis=1, keepdims=True), (nk, HEAD_DIM))
        dk_ref[...] = dk_acc[...].astype(BF16)
        dv_ref[...] = dv_acc[...].astype(BF16)
        dq_ref[...] = (dq_t[...].T * ATT_SCALE).astype(BF16)

    pair = lambda p: (0, p)
    return pl.pallas_call(
        body, name="attn_bwd", grid=(npair,),
        in_specs=[pl.BlockSpec((tp, LANES), pair),
                  pl.BlockSpec((tp, LANES), lambda p: (0, npair + p)),
                  pl.BlockSpec((tp, LANES), lambda p: (0, 2 * npair + p)),
                  pl.BlockSpec((tp, LANES), pair),
                  pl.BlockSpec((tp, LANES), pair),
                  pl.BlockSpec((None, SUBLANES, tp), lambda p: (p, 0, 0)),
                  pl.BlockSpec((tp, LANES), lambda p: (0, 0)),
                  pl.BlockSpec((SUBLANES, tp), lambda p: (0, 0))],
        out_specs=[pl.BlockSpec((tp, LANES), pair), pl.BlockSpec((tp, LANES), pair),
                   pl.BlockSpec((tp, LANES), pair),
                   pl.BlockSpec((None, SUBLANES, tp), lambda p: (p, 0, 0)),
                   pl.BlockSpec((tp, LANES), pair)],
        out_shape=[jax.ShapeDtypeStruct((tp, aw), BF16), jax.ShapeDtypeStruct((tp, aw), BF16),
                   jax.ShapeDtypeStruct((tp, aw), BF16),
                   jax.ShapeDtypeStruct((npair, SUBLANES, tp), F32),
                   jax.ShapeDtypeStruct((tp, aw), F32)],
        scratch_shapes=[pltpu.VMEM((tp, LANES), F32), pltpu.VMEM((tp, LANES), F32),
                        pltpu.VMEM((LANES, tp), F32)],
        compiler_params=_cparams("parallel"),
    )(qkv, qkv, qkv, do, o, lset, c, ct)


def fgate_bwd(dct8, drs, rest, bf_pad, fcol):
    tp = rest.shape[0]
    aw = drs.shape[1]
    nb = tp // ATT_BLOCK
    B = ATT_BLOCK

    def body(d_ref, drs_ref, f_ref, b_ref, dfl_ref, db_ref, pad_s):
        r_i = lax.broadcasted_iota(jnp.int32, (B, B), 0)
        c_i = lax.broadcasted_iota(jnp.int32, (B, B), 1)
        triu = (c_i >= r_i).astype(BF16)
        sel = (lax.broadcasted_iota(jnp.int32, (aw, LANES), 0)
               == HEAD_DIM * lax.broadcasted_iota(jnp.int32, (aw, LANES), 1)).astype(BF16)
        carry = jnp.zeros((1, LANES), F32)
        db = jnp.zeros((1, LANES), F32)
        pad_s[...] = jnp.zeros_like(pad_s)
        for i in range(nb - 1, -1, -1):
            sl = slice(i * B, (i + 1) * B)
            pad_s[0:SUBLANES, :] = d_ref[:, sl]
            dc = pad_s[...].T + _dot_split3(drs_ref[sl, :], sel)
            rc = _split3_dot(triu, dc)
            dlf = rc + carry
            carry = carry + rc[0:1, :]
            dfl = dlf * _sigmoid(-(f_ref[sl, :] + b_ref[...]))
            dfl_ref[sl, :] = dfl.astype(BF16)
            db = db + jnp.sum(dfl, axis=0, keepdims=True)
        db_ref[...] = db

    return pl.pallas_call(
        body, name="fgate_bwd", grid=(1,),
        in_specs=[pl.BlockSpec((SUBLANES, tp), lambda i: (0, 0)),
                  pl.BlockSpec((tp, aw), lambda i: (0, 0)),
                  pl.BlockSpec((tp, LANES), lambda i: (0, fcol)),
                  pl.BlockSpec((1, LANES), lambda i: (0, 0))],
        out_specs=[pl.BlockSpec((tp, LANES), lambda i: (0, 0)),
                   pl.BlockSpec((1, LANES), lambda i: (0, 0))],
        out_shape=[jax.ShapeDtypeStruct((tp, LANES), BF16), jax.ShapeDtypeStruct((1, LANES), F32)],
        scratch_shapes=[pltpu.VMEM((B, B), F32)],
        compiler_params=_cparams("arbitrary"),
    )(dct8, drs, rest, bf_pad)


def in_proj_bwd(dh2, parts, w_in_t, wrest_t, h, g1, tm):
    tp, d = h.shape
    dq, dk, dv, dxr, dyr, dfl = parts
    aw, rw = dq.shape[1], dxr.shape[1]

    def body(dh2_ref, dq_ref, dk_ref, dv_ref, dxr_ref, dyr_ref, dfl_ref, wq_ref, wr_ref, h_ref, g_ref,
             dh_ref, dg_ref):
        i = pl.program_id(0)
        dz = _dot(dq_ref[...], wq_ref[0:aw, :])
        dz += _dot(dk_ref[...], wq_ref[aw:2 * aw, :])
        dz += _dot(dv_ref[...], wq_ref[2 * aw:3 * aw, :])
        dz += _dot(dxr_ref[...], wr_ref[0:rw, :])
        dz += _dot(dyr_ref[...], wr_ref[rw:2 * rw, :])
        dz += _dot(dfl_ref[...], wr_ref[2 * rw:2 * rw + LANES, :])
        dx, dg = _rms_bwd(h_ref[...], g_ref[...], dz)
        dh_ref[...] = dh2_ref[...] + dx
        _accumulate(dg_ref, dg, i == 0)

    row = lambda i: (i, 0)
    fix = lambda i: (0, 0)
    return pl.pallas_call(
        body, name="in_proj_bwd", grid=(tp // tm,),
        in_specs=[pl.BlockSpec((tm, d), row),
                  pl.BlockSpec((tm, aw), row), pl.BlockSpec((tm, aw), row), pl.BlockSpec((tm, aw), row),
                  pl.BlockSpec((tm, rw), row), pl.BlockSpec((tm, rw), row), pl.BlockSpec((tm, LANES), row),
                  pl.BlockSpec((3 * aw, d), fix), pl.BlockSpec(wrest_t.shape, fix),
                  pl.BlockSpec((tm, d), row), pl.BlockSpec((1, d), fix)],
        out_specs=[pl.BlockSpec((tm, d), row), pl.BlockSpec((1, d), fix)],
        out_shape=[jax.ShapeDtypeStruct((tp, d), F32), jax.ShapeDtypeStruct((1, d), F32)],
        compiler_params=_cparams("arbitrary"),
    )(dh2, dq, dk, dv, dxr, dyr, dfl, w_in_t, wrest_t, h, g1)


def dw_in_t(z, parts, nh, tr):
    tp, d = z.shape
    dq, dk, dv, dxr, dyr, dfl = parts
    aw, rw = dq.shape[1], dxr.shape[1]
    d_in = 3 * aw + nh + 2 * rw
    nr = tp // tr
    offs = [(0, aw), (aw, aw), (2 * aw, aw), (3 * aw + nh, rw), (3 * aw + nh + rw, rw)]

    def body(z_ref, dq_ref, dk_ref, dv_ref, dxr_ref, dyr_ref, dfl_ref, o_ref, acc):
        r = pl.program_id(0)

        @pl.when(r == 0)
        def _():
            acc[...] = jnp.zeros_like(acc)

        zt = z_ref[...]
        for (o, n), ref in zip(offs, (dq_ref, dk_ref, dv_ref, dxr_ref, dyr_ref)):
            acc[o:o + n, :] += _dot_tn(ref[...], zt)
        acc[3 * aw:3 * aw + nh, :] += _dot_tn(dfl_ref[...], zt)[0:nh, :]

        @pl.when(r == nr - 1)
        def _():
            o_ref[...] = acc[...].astype(BF16)

    row = lambda r: (r, 0)
    return pl.pallas_call(
        body, name="dw_in", grid=(nr,),
        in_specs=[pl.BlockSpec((tr, d), row),
                  pl.BlockSpec((tr, aw), row), pl.BlockSpec((tr, aw), row), pl.BlockSpec((tr, aw), row),
                  pl.BlockSpec((tr, rw), row), pl.BlockSpec((tr, rw), row), pl.BlockSpec((tr, LANES), row)],
        out_specs=pl.BlockSpec((d_in, d), lambda r: (0, 0)),
        out_shape=jax.ShapeDtypeStruct((d_in, d), BF16),
        scratch_shapes=[pltpu.VMEM((d_in, d), F32)],
        compiler_params=_cparams("arbitrary"),
    )(z, dq, dk, dv, dxr, dyr, dfl)


def _place():
    return lax.axis_index("x"), lax.axis_index("y"), lax.axis_index("c")


HBM = pl.BlockSpec(memory_space=pltpu.HBM)
SEM = pl.BlockSpec(memory_space=pltpu.SEMAPHORE)
EFFECT = pltpu.SideEffectType.DATAFLOW_SIDE_EFFECTING


def _in_hbm(a):
    return pltpu.with_memory_space_constraint(a, pltpu.HBM)


def _as_list(a):
    return list(a) if isinstance(a, (list, tuple)) else [a]


def _gather_targets(x, y, c):
    return [(x, y, 1 - c), (1 - x, y, c), (x, 1 - y, c), (1 - x, 1 - y, c)]


def _slot(t):
    return 4 * t[0] + 2 * t[1] + t[2]


def gather_start(groups, name):
    flat = [a for g in groups for a in g]
    n = len(flat)
    ng = len(groups)
    lands = [lax.empty((N_DEV,) + a.shape, a.dtype) for a in flat]

    def body(*refs):
        src, land = refs[:n], refs[n:2 * n]
        sems = refs[2 * n:2 * n + 2 * ng]
        token = refs[-1]
        x, y, c = _place()
        me = 4 * x + 2 * y + c
        i = 0
        for gi, g in enumerate(groups):
            for a in range(len(g)):
                for k, t in enumerate(_gather_targets(x, y, c)):
                    pltpu.make_async_remote_copy(
                        src_ref=src[i], dst_ref=land[i].at[me],
                        send_sem=sems[2 * gi].at[4 * a + k], recv_sem=sems[2 * gi + 1].at[4 * a + k],
                        device_id=t, device_id_type=MESH).start()
                i += 1
        token[...] = jnp.zeros_like(token)

    sem_shapes = []
    for g in groups:
        sem_shapes += [pltpu.SemaphoreType.DMA((4 * len(g),)), pltpu.SemaphoreType.DMA((4 * len(g),))]
    out = pl.pallas_call(
        body, name=name,
        out_shape=sem_shapes + [pltpu.HBM(a.shape, a.dtype) for a in flat + lands]
        + [jax.ShapeDtypeStruct((SUBLANES, LANES), F32)],
        in_specs=[HBM] * (2 * n),
        out_specs=[SEM] * (2 * ng) + [HBM] * (2 * n) + [pl.BlockSpec(memory_space=pltpu.VMEM)],
        input_output_aliases={i: 2 * ng + i for i in range(2 * n)},
        compiler_params=pltpu.CompilerParams(has_side_effects=EFFECT),
    )(*[_in_hbm(a) for a in flat + lands])
    sems = out[:2 * ng]
    thru = out[2 * ng:2 * ng + 2 * n]
    srcs_t, lands_t = thru[:n], thru[n:]
    res, i = [], 0
    for gi, g in enumerate(groups):
        res.append((sems[2 * gi], sems[2 * gi + 1], srcs_t[i:i + len(g)], lands_t[i:i + len(g)]))
        i += len(g)
    return res, out[-1]


def gather_wait(send, recv, srcs, lands, after, name):
    n = len(srcs)

    def body(*refs):
        src, land = refs[:n], refs[n:2 * n]
        send_sem, recv_sem = refs[2 * n], refs[2 * n + 1]
        x, y, c = _place()
        for a in range(n):
            for k, t in enumerate(_gather_targets(x, y, c)):
                cp = pltpu.make_async_remote_copy(
                    src_ref=src[a], dst_ref=land[a].at[_slot(t)],
                    send_sem=send_sem.at[4 * a + k], recv_sem=recv_sem.at[4 * a + k],
                    device_id=t, device_id_type=MESH)
                cp.wait_send()
                cp.wait_recv()

    out = pl.pallas_call(
        body, name=name,
        out_shape=[pltpu.HBM(a.shape, a.dtype) for a in list(srcs) + list(lands)],
        in_specs=[HBM] * (2 * n) + [SEM, SEM] + [ANY] * len(_as_list(after)),
        out_specs=[HBM] * (2 * n),
        input_output_aliases={i: i for i in range(2 * n)},
        compiler_params=pltpu.CompilerParams(has_side_effects=EFFECT),
    )(*srcs, *lands, send, recv, *_as_list(after))
    return out[:n], out[n:]


def forward_start(lands, name):
    n = len(lands)

    def body(*refs):
        land = refs[:n]
        send_sem, recv_sem = refs[n], refs[n + 1]
        token = refs[-1]
        x, y, c = _place()
        for a in range(n):
            for j, chip in enumerate([(1 - x, y), (x, 1 - y), (1 - x, 1 - y)]):
                blk = land[a].at[_slot((*chip, c))]
                pltpu.make_async_remote_copy(src_ref=blk, dst_ref=blk, send_sem=send_sem.at[3 * a + j],
                                             recv_sem=recv_sem.at[3 * a + j], device_id=(x, y, 1 - c),
                                             device_id_type=MESH).start()
        token[...] = jnp.zeros_like(token)

    out = pl.pallas_call(
        body, name=name,
        out_shape=[pltpu.SemaphoreType.DMA((3 * n,)), pltpu.SemaphoreType.DMA((3 * n,))]
        + [pltpu.HBM(a.shape, a.dtype) for a in lands] + [jax.ShapeDtypeStruct((SUBLANES, LANES), F32)],
        in_specs=[HBM] * n,
        out_specs=[SEM, SEM] + [HBM] * n + [pl.BlockSpec(memory_space=pltpu.VMEM)],
        input_output_aliases={i: 2 + i for i in range(n)},
        compiler_params=pltpu.CompilerParams(has_side_effects=EFFECT),
    )(*[_in_hbm(a) for a in lands])
    return out[0], out[1], out[2:2 + n], out[-1][0, 0]


def forward_wait(send, recv, lands, after, name):
    n = len(lands)

    def body(*refs):
        land = refs[:n]
        send_sem, recv_sem = refs[n], refs[n + 1]
        x, y, c = _place()
        for a in range(n):
            for j, chip in enumerate([(1 - x, y), (x, 1 - y), (1 - x, 1 - y)]):
                cp = pltpu.make_async_remote_copy(
                    src_ref=land[a].at[_slot((*chip, c))], dst_ref=land[a].at[_slot((*chip, 1 - c))],
                    send_sem=send_sem.at[3 * a + j], recv_sem=recv_sem.at[3 * a + j],
                    device_id=(x, y, 1 - c), device_id_type=MESH)
                cp.wait_send()
                cp.wait_recv()

    return pl.pallas_call(
        body, name=name,
        out_shape=[pltpu.HBM(a.shape, a.dtype) for a in lands],
        in_specs=[HBM] * n + [SEM, SEM, ANY],
        out_specs=[HBM] * n,
        input_output_aliases={i: i for i in range(n)},
        compiler_params=pltpu.CompilerParams(has_side_effects=EFFECT),
    )(*lands, send, recv, after)


def _relations():
    return [(dx, dy, dc) for dx in (0, 1) for dy in (0, 1) for dc in (0, 1) if dx + dy + dc]


def _peer(x, y, c, rel):
    return ((1 - x) if rel[0] else x, (1 - y) if rel[1] else y, (1 - c) if rel[2] else c)


def exchange_start(srcs, lands, layer, name, after=()):
    n = len(srcs)
    after = _as_list(after)

    def body(*refs):
        src, land = refs[:n], refs[n:2 * n]
        send_sem, recv_sem = refs[2 * n + len(after)], refs[2 * n + len(after) + 1]
        token = refs[-1]
        x, y, c = _place()
        me = 4 * x + 2 * y + c
        for k, rel in enumerate(_relations()):
            peer = _peer(x, y, c, rel)
            for a in range(n):
                pltpu.make_async_remote_copy(
                    src_ref=src[a] if layer is None else src[a].at[_slot(peer)],
                    dst_ref=land[a].at[me] if layer is None else land[a].at[me, layer],
                    send_sem=send_sem.at[7 * a + k], recv_sem=recv_sem.at[7 * a + k],
                    device_id=peer, device_id_type=MESH).start()
        token[...] = jnp.zeros_like(token)

    out = pl.pallas_call(
        body, name=name,
        out_shape=[pltpu.SemaphoreType.DMA((7 * n,)), pltpu.SemaphoreType.DMA((7 * n,))]
        + [pltpu.HBM(a.shape, a.dtype) for a in list(srcs) + list(lands)]
        + [jax.ShapeDtypeStruct((SUBLANES, LANES), F32)],
        in_specs=[HBM] * (2 * n) + [ANY] * len(after),
        out_specs=[SEM, SEM] + [HBM] * (2 * n) + [pl.BlockSpec(memory_space=pltpu.VMEM)],
        input_output_aliases={i: 2 + i for i in range(2 * n)},
        compiler_params=pltpu.CompilerParams(has_side_effects=EFFECT),
    )(*[_in_hbm(a) for a in list(srcs) + list(lands)], *after)
    return out[0], out[1], out[2:2 + n], out[2 + n:2 + 2 * n], out[-1][0, 0]


def exchange_wait(send, recv, srcs, lands, after, layer, name):
    n = len(srcs)

    def body(*refs):
        src, land = refs[:n], refs[n:2 * n]
        send_sem, recv_sem = refs[2 * n], refs[2 * n + 1]
        x, y, c = _place()
        for k, rel in enumerate(_relations()):
            peer = _peer(x, y, c, rel)
            for a in range(n):
                cp = pltpu.make_async_remote_copy(
                    src_ref=src[a] if layer is None else src[a].at[_slot(peer)],
                    dst_ref=land[a].at[_slot(peer)] if layer is None else land[a].at[_slot(peer), layer],
                    send_sem=send_sem.at[7 * a + k], recv_sem=recv_sem.at[7 * a + k],
                    device_id=peer, device_id_type=MESH)
                cp.wait_send()
                cp.wait_recv()

    out = pl.pallas_call(
        body, name=name,
        out_shape=[pltpu.HBM(a.shape, a.dtype) for a in list(srcs) + list(lands)],
        in_specs=[HBM] * (2 * n) + [SEM, SEM] + [ANY] * len(_as_list(after)),
        out_specs=[HBM] * (2 * n),
        input_output_aliases={i: i for i in range(2 * n)},
        compiler_params=pltpu.CompilerParams(has_side_effects=EFFECT),
    )(*srcs, *lands, send, recv, *_as_list(after))
    return out[:n], out[n:]


def _adamw_math(g, w, m, v):
    m = ADAM_B1 * m + (1.0 - ADAM_B1) * g
    v = ADAM_B2 * v + (1.0 - ADAM_B2) * (g * g)
    m_hat = m / (1.0 - ADAM_B1 ** ADAM_STEP)
    v_hat = v / (1.0 - ADAM_B2 ** ADAM_STEP)
    delta = -ADAM_LR * (m_hat / (jnp.sqrt(v_hat) + ADAM_EPS) + ADAM_WD * w)
    return delta, m, v


def _sum_with_own(p_ref, own_refs, layer, me):
    own = own_refs[0][...]
    for k in range(1, len(own_refs)):
        own = jnp.where(layer == k, own_refs[k][...], own)
    g = None
    for p in range(p_ref.shape[0]):
        term = jnp.where(me == p, own, p_ref[p]).astype(F32)
        g = term if g is None else g + term
    return g


def sum_adamw(parts, owns, me, w, m, v, tr, name):
    npart, rows, cols = parts.shape
    nl = len(owns)
    per_layer = rows // nl // tr

    def body(me_ref, p_ref, *refs):
        own_refs = refs[:nl]
        w_ref, m_ref, v_ref, g_ref, d_ref, nm_ref, nv_ref = refs[nl:]
        g = _sum_with_own(p_ref, own_refs, pl.program_id(0) // per_layer, me_ref[0])
        delta, nm, nv = _adamw_math(g, w_ref[...], m_ref[...], v_ref[...])
        g_ref[...] = g
        d_ref[...] = delta
        nm_ref[...] = nm
        nv_ref[...] = nv

    blk = pl.BlockSpec((tr, cols), lambda i, me_ref: (i, 0))
    own_specs = [pl.BlockSpec((None, tr, cols),
                              lambda i, me_ref, l=l: (me_ref[0], jnp.clip(i - l * per_layer, 0, per_layer - 1), 0))
                 for l in range(nl)]
    return pl.pallas_call(
        body, name=name,
        grid_spec=pltpu.PrefetchScalarGridSpec(
            num_scalar_prefetch=1, grid=(rows // tr,),
            in_specs=[pl.BlockSpec((npart, tr, cols), lambda i, me_ref: (0, i, 0))] + own_specs + [blk, blk, blk],
            out_specs=[blk] * 4),
        out_shape=[jax.ShapeDtypeStruct((rows, cols), F32)] * 4,
        compiler_params=_cparams("arbitrary"),
    )(me, parts, *owns, w, m, v)


def sum_adamw_t(parts, owns, me, w, m, v, name):
    npart, nl, rows, cols = parts.shape

    def body(me_ref, p_ref, *refs):
        own_refs = refs[:nl]
        w_ref, m_ref, v_ref, g_ref, d_ref, nm_ref, nv_ref = refs[nl:]
        g = _sum_with_own(p_ref, own_refs, pl.program_id(0), me_ref[0])
        delta, nm, nv = _adamw_math(g, w_ref[...], m_ref[...], v_ref[...])
        g_ref[...] = g
        d_ref[...] = delta
        nm_ref[...] = nm
        nv_ref[...] = nv

    blk = pl.BlockSpec((None, rows, cols), lambda l, me_ref: (l, 0, 0))
    own_specs = [pl.BlockSpec((None, rows, cols), lambda l, me_ref: (me_ref[0], 0, 0)) for _ in range(nl)]
    return pl.pallas_call(
        body, name=name,
        grid_spec=pltpu.PrefetchScalarGridSpec(
            num_scalar_prefetch=1, grid=(nl,),
            in_specs=[pl.BlockSpec((npart, None, rows, cols), lambda l, me_ref: (0, l, 0, 0))] + own_specs
            + [blk, blk, blk],
            out_specs=[blk] * 4),
        out_shape=[jax.ShapeDtypeStruct((nl, rows, cols), F32)] * 4,
        compiler_params=_cparams("arbitrary"),
    )(me, parts, *owns, w, m, v)


def adamw_group(gs, ws, ms, vs, name):
    n = len(gs)

    def body(*refs):
        g, w, m, v, outs = refs[:n], refs[n:2 * n], refs[2 * n:3 * n], refs[3 * n:4 * n], refs[4 * n:]
        for i in range(n):
            delta, nm, nv = _adamw_math(g[i][...], w[i][...], m[i][...], v[i][...])
            outs[i][...] = delta
            outs[n + i][...] = nm
            outs[2 * n + i][...] = nv

    vmem = pl.BlockSpec(memory_space=pltpu.VMEM)
    out = pl.pallas_call(
        body, name=name,
        in_specs=[vmem] * (4 * n), out_specs=[vmem] * (3 * n),
        out_shape=[jax.ShapeDtypeStruct(a.shape, F32) for a in list(ws) * 3],
        compiler_params=_cparams(),
    )(*gs, *ws, *ms, *vs)
    return out[:n], out[n:2 * n], out[2 * n:]


def sum_parts(parts, name):
    npart, rows, cols = parts.shape

    def body(p_ref, g_ref):
        g = p_ref[0].astype(F32)
        for p in range(1, npart):
            g = g + p_ref[p].astype(F32)
        g_ref[...] = g

    return pl.pallas_call(
        body, name=name, grid=(1,),
        in_specs=[pl.BlockSpec((npart, rows, cols), lambda i: (0, 0, 0))],
        out_specs=pl.BlockSpec((rows, cols), lambda i: (0, 0)),
        out_shape=jax.ShapeDtypeStruct((rows, cols), F32),
        compiler_params=_cparams("arbitrary"),
    )(parts)


def _round_up(n, m):
    return (n + m - 1) // m * m


def _block_diag_pairs(w):
    nb, b, _ = w.shape
    per = LANES // b
    ng = nb // per
    w = w.reshape(ng, per, b, b)
    eye = jnp.eye(per, dtype=w.dtype)
    out = jnp.einsum('gpij,pq->gpiqj', w, eye).reshape(ng, LANES, LANES)
    return out.astype(BF16)


def _block_diag_extract(g, b):
    ng = g.shape[0]
    per = LANES // b
    g = g.reshape(ng, per, b, per, b)
    idx = jnp.arange(per)
    return g[:, idx, :, idx, :].transpose(1, 0, 2, 3).reshape(ng * per, b, b)


def _tiles(v):
    v = v.reshape(-1)
    n = _round_up(v.shape[0], SUBLANES * LANES)
    return jnp.pad(v, (0, n - v.shape[0])).reshape(-1, LANES)


SMALL = ['attn_norm_g', 'b_f', 'conv_w', 'conv_b', 'w_gate_a', 'b_gate_a', 'w_gate_x', 'b_gate_x',
         'lru_L', 'attn_out_g', 'rec_out_g', 'mlp_norm_g', 'final_g', 'meta']


def _pack(d):
    return jnp.concatenate([_tiles(d[n]) for n in SMALL], axis=0)


def _unpack(vec, shapes):
    out, r = {}, 0
    for n in SMALL:
        size = math.prod(shapes[n])
        nr = _round_up(size, SUBLANES * LANES) // LANES
        out[n] = vec[r:r + nr].reshape(-1)[:size].reshape(shapes[n])
        r += nr
    return out


def _row_tile(tp):
    return tp // 4 if (tp // 4) % 16 == 0 else tp


def local_step(x, tgt, meta, small, hooks):
    s, d = x.shape
    t_real = s + N_META
    tp = _round_up(t_real, ATT_BLOCK)
    depth = small['attn_norm_g'].shape[0]
    nh = small['b_f'].shape[1]
    rw = small['conv_b'].shape[1]
    blk = small['w_gate_a'].shape[2]
    tm = _row_tile(tp)
    tm2 = tp // 2
    fcol = 2 * rw // LANES

    h = jnp.concatenate([meta, x, jnp.zeros((tp - t_real, d), F32)], axis=0)
    tgt_p = jnp.pad(tgt, ((N_META, tp - t_real), (0, 0)))
    row = lambda v: v.reshape(1, -1)
    bf_pad = jnp.pad(small['b_f'], ((0, 0), (0, LANES - nh)))

    saved = []
    for l in range(depth):
        w_in_t, wrest_t, wout, tok_w = hooks.mixer_weights(l, h)
        wga = _block_diag_pairs(small['w_gate_a'][l])
        wgx = _block_diag_pairs(small['w_gate_x'][l])
        z, qkv, rest = in_proj(h, row(small['attn_norm_g'][l]) + tok_w, w_in_t, wrest_t, 3 * nh * HEAD_DIM, tm)
        c, ct = fgate_fwd(rest, bf_pad[l:l + 1], fcol)
        o, lset = attn_fwd(qkv, c, ct, nh)
        rec, hr, xc, *gates = rec_fwd(rest, small['conv_w'][l], row(small['conv_b'][l]), wga,
                                      row(small['b_gate_a'][l]), wgx, row(small['b_gate_x'][l]),
                                      row(small['lru_L'][l]), rw)
        gup, gdown, tok_w = hooks.mlp_weights(l, rec)
        h2, mix, z2 = out_proj(h, o, rec, row(small['attn_out_g'][l]), row(small['rec_out_g'][l]), wout,
                               row(small['mlp_norm_g'][l]) + tok_w, tm)
        u, h3 = mlp_fwd(z2, h2, gup, gdown, tm2)
        saved.append(dict(h=h, z=z, qkv=qkv, rest=rest, c=c, ct=ct, o=o, lset=lset, rec=rec, hr=hr, xc=xc,
                          h2=h2, mix=mix, z2=z2, u=u, wga=wga, wgx=wgx, gates=gates,
                          w_in_t=w_in_t, wrest_t=wrest_t, wout=wout, gup=gup, gdown=gdown))
        h = h3

    dh, dgf, loss = loss_head(h, row(small['final_g']), tgt_p, t_real, tm)

    gs = {n: [None] * depth for n in SMALL if n not in ('final_g', 'meta')}
    tok = jnp.zeros((), F32)
    for l in reversed(range(depth)):
        sv = saved[l]
        gup, gdown = sv['gup'], sv['gdown']
        tf = gup.shape[2]
        dup, dh2, dg2, dhb = mlp_bwd(dh, sv['u'], sv['h2'], row(small['mlp_norm_g'][l]) + tok, gup, gdown, tm)
        gs['mlp_norm_g'][l] = dg2[0]
        do, drec, dga, dgr = out_proj_bwd(dh2, sv['o'], sv['rec'], row(small['attn_out_g'][l]),
                                          row(small['rec_out_g'][l]), sv['wout'], tm)
        gs['attn_out_g'][l] = dga[0]
        gs['rec_out_g'][l] = dgr[0]
        dw_down, dw_up = dw_mlp(sv['u'], dhb, sv['z2'], dup, tf)
        blocks = dict(
            w_down=dw_down, w_up=dw_up,
            w_out=mm_tn(sv['mix'], dh2, tk=d, tn=d // 2, out_dtype=BF16,
                        name="dw_out").reshape(N_DEV, d // N_DEV, d))
        tok = hooks.grads_ready(l, 'mlp', blocks)
        dxr, dyr, dwga, dwgx, vec = rec_bwd(drec, sv['hr'], sv['xc'], sv['gates'], sv['rest'], small['conv_w'][l],
                                            sv['wga'], sv['wgx'], row(small['lru_L'][l]) + tok, rw)
        gs['w_gate_a'][l] = _block_diag_extract(dwga, blk)
        gs['w_gate_x'][l] = _block_diag_extract(dwgx, blk)
        vec = vec.transpose(1, 0, 2).reshape(SUBLANES, rw)
        gs['conv_w'][l] = vec[0:CONV_WIDTH]
        gs['conv_b'][l] = vec[4]
        gs['b_gate_a'][l] = vec[5]
        gs['b_gate_x'][l] = vec[6]
        gs['lru_L'][l] = vec[7]
        dq, dk, dv, drow, dcol = attn_bwd(sv['qkv'], do, sv['o'], sv['lset'], sv['c'], sv['ct'] + tok, nh)
        drow8 = drow[:, 0:2, :].reshape(nh, tp)
        if nh < SUBLANES:
            drow8 = jnp.pad(drow8, ((0, SUBLANES - nh), (0, 0)))
        dfl, dbf = fgate_bwd(drow8, dcol, sv['rest'], bf_pad[l:l + 1], fcol)
        gs['b_f'][l] = dbf[0, 0:nh]
        parts = (dq, dk, dv, dxr, dyr, dfl)
        dh, dg1 = in_proj_bwd(dh2, parts, sv['w_in_t'], sv['wrest_t'], sv['h'], row(small['attn_norm_g'][l]), tm)
        gs['attn_norm_g'][l] = dg1[0]
        first = ()
        if l == 0:
            grads = {n: jnp.stack(v) for n, v in gs.items()}
            grads['final_g'] = dgf[0]
            grads['meta'] = dh[0:N_META]
            first = hooks.small_ready(grads)
        dw_in = dw_in_t(sv['z'], parts, nh, tm2)
        dw_in = dw_in.reshape(N_DEV, dw_in.shape[0] // N_DEV, d)
        tok = hooks.grads_ready(l, 'in', dict(w_in=dw_in), first)

    return loss[0, 0], dh, tok


def prep_weights(g_in, g_out, nh, rw):
    d = g_in.shape[2]
    w_in_t = g_in.reshape(-1, d)
    f0 = 3 * nh * HEAD_DIM
    wrest_t = jnp.concatenate([w_in_t[f0 + nh:f0 + nh + 2 * rw],
                               jnp.pad(w_in_t[f0:f0 + nh], ((0, LANES - nh), (0, 0)))], axis=0)
    return w_in_t, wrest_t, g_out.reshape(d, d)


BIG = ['w_in', 'w_out', 'w_up', 'w_down']
EXCHANGE_GROUPS = {'mlp': ['w_down', 'w_up', 'w_out'], 'in': ['w_in']}
WEIGHTS = ['meta', 'attn_norm_g', 'w_in', 'b_f', 'conv_w', 'conv_b', 'w_gate_a', 'b_gate_a', 'w_gate_x', 'b_gate_x',
           'lru_L', 'attn_out_g', 'rec_out_g', 'w_out', 'mlp_norm_g', 'w_up', 'w_down', 'final_g']


def _set_own(arr, own, me):
    return lax.dynamic_update_slice_in_dim(arr, own[None], me, axis=0)


class _Step:
    def __init__(self, w, nh, rw, me):
        self.w, self.nh, self.rw, self.me = w, nh, rw, me
        depth = w['w_in'].shape[0]
        first = [w['w_in_t'][:, 0, :].astype(BF16), w['w_out'][0].astype(BF16), w['meta'], w['conv_w']]
        self.pending, token = gather_start([first], "gather_start_0")
        zero = token[0, 0].astype(BF16)
        groups = [[w['w_up'][0].astype(BF16) + zero, w['w_down'][0].astype(BF16) + zero]]
        for l in range(1, depth):
            groups.append([w['w_in_t'][:, l, :].astype(BF16) + zero, w['w_out'][l].astype(BF16) + zero])
            groups.append([w['w_up'][l].astype(BF16) + zero, w['w_down'][l].astype(BF16) + zero])
        rest, _ = gather_start(groups, "gather_start_1")
        self.pending += rest
        self.first_after = rest[0][2][0]
        self.gathered = {}
        self.passing = {}
        self.token = jnp.zeros((), F32)
        self.lands = {n: lax.empty((N_DEV,) + w[n].shape, BF16) for n in BIG}
        din8, _, d = w['w_in_t'].shape
        self.lands['w_in'] = lax.empty((N_DEV, depth, din8, d), BF16)
        self.started = []
        self.small = None

    def _pass_on(self, gi, after):
        if gi < len(self.pending) and gi not in self.passing:
            send, recv, srcs, lands = self.pending[gi]
            srcs, lands = gather_wait(send, recv, srcs, lands, after, "gather_wait_%d" % gi)
            fsend, frecv, lands, token = forward_start(lands, "forward_start_%d" % gi)
            self.passing[gi] = (fsend, frecv, srcs, lands)
            self.token = token

    def group(self, gi, after):
        if gi not in self.gathered:
            self._pass_on(gi, after)
            fsend, frecv, srcs, lands = self.passing[gi]
            lands = forward_wait(fsend, frecv, lands, after, "forward_wait_%d" % gi)
            self.gathered[gi] = [_set_own(g, own, self.me) for g, own in zip(lands, srcs)]
            if gi >= 2:
                self._pass_on(gi + 1, lands[0])
        return self.gathered[gi]

    def mixer_weights(self, l, after):
        g = self.group(2 * l, after)
        return (*prep_weights(g[0], g[1], self.nh, self.rw), self.token)

    def mlp_weights(self, l, after):
        g = self.group(2 * l + 1, after)
        return g[0], g[1], self.token

    def grads_ready(self, l, group, blocks, after=()):
        names = EXCHANGE_GROUPS[group]
        send, recv, srcs, lands, token = exchange_start(
            [blocks[n] for n in names], [self.lands[n] for n in names], l, "exchange_start_%s_%d" % (group, l),
            after)
        for n, a in zip(names, lands):
            self.lands[n] = a
        self.started.append((l, group, send, recv, srcs))
        return token

    def small_ready(self, grads):
        self.small_shapes = {n: grads[n].shape for n in SMALL}
        packed = _pack(grads).astype(BF16)
        send, recv, srcs, lands, token = exchange_start(
            [packed], [lax.empty((N_DEV,) + packed.shape, BF16)], None, "small_start")
        self.small = (send, recv, srcs, lands)
        return srcs[0]

    def small_sum(self, after):
        send, recv, srcs, lands = self.small
        srcs, lands = exchange_wait(send, recv, srcs, lands, after, None, "small_wait")
        parts = _set_own(lands[0], srcs[0], self.me)
        return _unpack(sum_parts(parts, "sum_small_grads"), self.small_shapes)

    def received(self, group, after):
        names = EXCHANGE_GROUPS[group]
        own = {n: [None] * self.w[n].shape[0] for n in names}
        for l, grp, send, recv, srcs in self.started:
            if grp != group:
                continue
            srcs, lands = exchange_wait(send, recv, srcs, [self.lands[n] for n in names], after, l,
                                        "exchange_wait_%s_%d" % (group, l))
            for n, a, sr in zip(names, lands, srcs):
                self.lands[n] = a
                own[n][l] = sr
        return {n: (self.lands[n], own[n]) for n in names}


def kernel(x, meta, attn_norm_g, w_in, b_f, conv_w, conv_b, w_gate_a, b_gate_a, w_gate_x, b_gate_x, lru_L, attn_out_g, rec_out_g, w_out, mlp_norm_g, w_up, w_down, final_g, loss_target, m_meta, m_attn_norm_g, m_w_in, m_b_f, m_conv_w, m_conv_b, m_w_gate_a, m_b_gate_a, m_w_gate_x, m_b_gate_x, m_lru_L, m_attn_out_g, m_rec_out_g, m_w_out, m_mlp_norm_g, m_w_up, m_w_down, m_final_g, v_meta, v_attn_norm_g, v_w_in, v_b_f, v_conv_w, v_conv_b, v_w_gate_a, v_b_gate_a, v_w_gate_x, v_b_gate_x, v_lru_L, v_attn_out_g, v_rec_out_g, v_w_out, v_mlp_norm_g, v_w_up, v_w_down, v_final_g):
    w = dict(meta=meta, attn_norm_g=attn_norm_g, w_in=w_in, b_f=b_f, conv_w=conv_w, conv_b=conv_b,
             w_gate_a=w_gate_a, b_gate_a=b_gate_a, w_gate_x=w_gate_x, b_gate_x=b_gate_x, lru_L=lru_L,
             attn_out_g=attn_out_g, rec_out_g=rec_out_g, w_out=w_out, mlp_norm_g=mlp_norm_g, w_up=w_up,
             w_down=w_down, final_g=final_g)
    mo = dict(meta=m_meta, attn_norm_g=m_attn_norm_g, w_in=m_w_in, b_f=m_b_f, conv_w=m_conv_w, conv_b=m_conv_b,
              w_gate_a=m_w_gate_a, b_gate_a=m_b_gate_a, w_gate_x=m_w_gate_x, b_gate_x=m_b_gate_x, lru_L=m_lru_L,
              attn_out_g=m_attn_out_g, rec_out_g=m_rec_out_g, w_out=m_w_out, mlp_norm_g=m_mlp_norm_g,
              w_up=m_w_up, w_down=m_w_down, final_g=m_final_g)
    vo = dict(meta=v_meta, attn_norm_g=v_attn_norm_g, w_in=v_w_in, b_f=v_b_f, conv_w=v_conv_w, conv_b=v_conv_b,
              w_gate_a=v_w_gate_a, b_gate_a=v_b_gate_a, w_gate_x=v_w_gate_x, b_gate_x=v_b_gate_x, lru_L=v_lru_L,
              attn_out_g=v_attn_out_g, rec_out_g=v_rec_out_g, w_out=v_w_out, mlp_norm_g=v_mlp_norm_g,
              w_up=v_w_up, w_down=v_w_down, final_g=v_final_g)
    depth = w_in.shape[0]
    nh = b_f.shape[1]
    rw = conv_b.shape[1]
    me = 4 * lax.axis_index("x") + 2 * lax.axis_index("y") + lax.axis_index("c")

    w['w_in_t'] = jnp.transpose(w_in, (2, 0, 1))
    swap = lambda a: jnp.swapaxes(a, 1, 2)
    step = _Step(w, nh, rw, me)
    g0 = step.group(0, step.first_after)
    meta_full = g0[2].transpose(1, 0, 2).reshape(N_META, -1)
    conv_full = g0[3].transpose(1, 2, 0, 3).reshape(depth, CONV_WIDTH, rw)
    small = {n: w[n] for n in SMALL}
    small['conv_w'] = conv_full

    loss_part, dh0, tok = local_step(x[0], loss_target[0], meta_full, small, step)
    loss = lax.psum(loss_part, ("x", "y", "c"))
    grad_x = dh0[N_META:N_META + x.shape[1]][None]

    out_g, out_d, out_m, out_v = {}, {}, {}, {}
    me1 = me.reshape(1).astype(jnp.int32)

    def update_big(group, after):
        for n, (r, owns) in step.received(group, after).items():
            if n == 'w_in':
                out = sum_adamw_t(r, owns, me1, swap(w[n]), swap(mo[n]), swap(vo[n]), "adamw_w_in")
                out = [swap(a) for a in out]
            else:
                shp = w[n].shape
                rows, cols = shp[0] * shp[1], shp[2]
                tr = min(512 if cols <= 512 else 256, shp[1])
                out = sum_adamw(r.reshape(N_DEV, rows, cols), owns, me1, w[n].reshape(rows, cols),
                                mo[n].reshape(rows, cols), vo[n].reshape(rows, cols), tr, "adamw_" + n)
                out = [a.reshape(shp) for a in out]
            out_g[n], out_d[n], out_m[n], out_v[n] = out
            after = out[0]
        return after

    update_big('mlp', step.started[-1][4][0])

    gsum = step.small_sum([out_g[n] for n in EXCHANGE_GROUPS['mlp']])
    gsum['meta'] = lax.dynamic_slice_in_dim(gsum['meta'], me * meta.shape[1], meta.shape[1], axis=1)
    gsum['conv_w'] = lax.dynamic_slice_in_dim(gsum['conv_w'], me * conv_w.shape[2], conv_w.shape[2], axis=2)
    as2d = lambda a: a.reshape(-1, a.shape[-1])
    deltas, new_m, new_v = adamw_group([as2d(gsum[n]) for n in SMALL], [as2d(w[n]) for n in SMALL],
                                       [as2d(mo[n]) for n in SMALL], [as2d(vo[n]) for n in SMALL], "adamw_small")
    for i, n in enumerate(SMALL):
        out_g[n] = gsum[n]
        out_d[n], out_m[n], out_v[n] = [a[i].reshape(w[n].shape) for a in (deltas, new_m, new_v)]

    update_big('in', deltas[0])

    return (loss, grad_x, *[out_g[n] for n in WEIGHTS], *[out_d[n] for n in WEIGHTS],
            *[out_m[n] for n in WEIGHTS], *[out_v[n] for n in WEIGHTS])
```

```python
import math

import jax
import jax.numpy as jnp
from jax import lax
from jax.experimental import pallas as pl
from jax.experimental.pallas import tpu as pltpu

F32 = jnp.float32
BF16 = jnp.bfloat16

N_DEV = 8
N_META = 16
HEAD_DIM = 64
CONV_WIDTH = 4
RG_C = 8.0
NORM_EPS = 1e-6
LANES = 128
SUBLANES = 8
ATT_BLOCK = 128
ATT_TQ = 512
NEG_BIG = -1e30
ATT_SCALE = 1.0 / math.sqrt(HEAD_DIM)

ADAM_LR = 0.001
ADAM_B1 = 0.9
ADAM_B2 = 0.999
ADAM_EPS = 1e-08
ADAM_WD = 0.01
ADAM_STEP = 10

VMEM_LIMIT_BYTES = 56 * 1024 * 1024
MESH = pl.DeviceIdType.MESH
ANY = pl.BlockSpec(memory_space=pl.ANY)


def _cparams(*sem):
    return pltpu.CompilerParams(dimension_semantics=sem if sem else None,
                                vmem_limit_bytes=VMEM_LIMIT_BYTES)


def _dot(a, b):
    return jnp.dot(a, b, preferred_element_type=F32)


def _dot_nt(a, b):
    return lax.dot_general(a, b, (((1,), (1,)), ((), ())), preferred_element_type=F32)


def _dot_tn(a, b):
    return lax.dot_general(a, b, (((0,), (0,)), ((), ())), preferred_element_type=F32)


def _sigmoid(x):
    return 0.5 * (1.0 + jnp.tanh(0.5 * x))


def _log_sigmoid(x):
    return jnp.minimum(x, 0.0) - jnp.log(1.0 + jnp.exp(-jnp.abs(x)))


def _expm1(x):
    series = x * (1.0 + x * (0.5 + x * (1.0 / 6.0 + x * (1.0 / 24.0))))
    return jnp.where(jnp.abs(x) < 1e-2, series, jnp.exp(x) - 1.0)


_GELU_K = math.sqrt(2.0 / math.pi)
_GELU_C = 0.044715


def _gelu(x):
    t = jnp.tanh(_GELU_K * (x + _GELU_C * x * x * x))
    return 0.5 * x * (1.0 + t)


def _gelu_and_grad(x):
    x2 = x * x
    t = jnp.tanh(_GELU_K * (x + _GELU_C * x2 * x))
    half = 0.5 * (1.0 + t)
    return x * half, half + 0.5 * x * (1.0 - t * t) * _GELU_K * (1.0 + 3.0 * _GELU_C * x2)


def _split3_dot(tri, x):
    hi = x.astype(BF16)
    r1 = x - hi.astype(F32)
    mid = r1.astype(BF16)
    lo = (r1 - mid.astype(F32)).astype(BF16)
    return _dot(tri, hi) + _dot(tri, mid) + _dot(tri, lo)


def _dot_split3(x, sel):
    hi = x.astype(BF16)
    r1 = x - hi.astype(F32)
    mid = r1.astype(BF16)
    lo = (r1 - mid.astype(F32)).astype(BF16)
    return _dot(hi, sel) + _dot(mid, sel) + _dot(lo, sel)


def _rms_fwd(x, g):
    r = lax.rsqrt(jnp.mean(x * x, axis=-1, keepdims=True) + NORM_EPS)
    return x * r * g


def _rms_bwd(x, g, dy):
    r = lax.rsqrt(jnp.mean(x * x, axis=-1, keepdims=True) + NORM_EPS)
    xn = x * r
    dxn = dy * g
    dx = r * (dxn - xn * jnp.mean(dxn * xn, axis=-1, keepdims=True))
    return dx, jnp.sum(dy * xn, axis=0, keepdims=True)


def _accumulate(ref, val, first):
    @pl.when(first)
    def _():
        ref[...] = val

    @pl.when(jnp.logical_not(first))
    def _():
        ref[...] += val


def in_proj(h, g1, w_in_t, wrest_t, nq, tm):
    tp, d = h.shape
    nr = wrest_t.shape[0]

    def body(h_ref, g_ref, wq_ref, wr_ref, z_ref, qkv_ref, rest_ref):
        z = _rms_fwd(h_ref[...], g_ref[...]).astype(BF16)
        z_ref[...] = z
        qkv_ref[...] = _dot_nt(z, wq_ref[...]).astype(BF16)
        rest_ref[...] = _dot_nt(z, wr_ref[...])

    return pl.pallas_call(
        body, name="in_proj", grid=(tp // tm,),
        in_specs=[pl.BlockSpec((tm, d), lambda i: (i, 0)),
                  pl.BlockSpec((1, d), lambda i: (0, 0)),
                  pl.BlockSpec((nq, d), lambda i: (0, 0)),
                  pl.BlockSpec((nr, d), lambda i: (0, 0))],
        out_specs=[pl.BlockSpec((tm, d), lambda i: (i, 0)),
                   pl.BlockSpec((tm, nq), lambda i: (i, 0)),
                   pl.BlockSpec((tm, nr), lambda i: (i, 0))],
        out_shape=[jax.ShapeDtypeStruct((tp, d), BF16),
                   jax.ShapeDtypeStruct((tp, nq), BF16),
                   jax.ShapeDtypeStruct((tp, nr), F32)],
        compiler_params=_cparams("parallel"),
    )(h, g1, w_in_t, wrest_t)


def fgate_fwd(rest, bf_pad, fcol):
    tp = rest.shape[0]
    nb = tp // ATT_BLOCK

    def body(f_ref, b_ref, c_ref, ct_ref):
        r_i = lax.broadcasted_iota(jnp.int32, (ATT_BLOCK, ATT_BLOCK), 0)
        c_i = lax.broadcasted_iota(jnp.int32, (ATT_BLOCK, ATT_BLOCK), 1)
        tri = (r_i >= c_i).astype(BF16)
        carry = jnp.zeros((1, LANES), F32)
        for i in range(nb):
            sl = slice(i * ATT_BLOCK, (i + 1) * ATT_BLOCK)
            lf = _log_sigmoid(f_ref[sl, :] + b_ref[...])
            cs = _split3_dot(tri, lf) + carry
            carry = cs[ATT_BLOCK - 1:ATT_BLOCK, :]
            c_ref[sl, :] = cs
            ct_ref[:, sl] = cs.T[0:SUBLANES, :]

    return pl.pallas_call(
        body, name="fgate_fwd", grid=(1,),
        in_specs=[pl.BlockSpec((tp, LANES), lambda i: (0, fcol)),
                  pl.BlockSpec((1, LANES), lambda i: (0, 0))],
        out_specs=[pl.BlockSpec((tp, LANES), lambda i: (0, 0)),
                   pl.BlockSpec((SUBLANES, tp), lambda i: (0, 0))],
        out_shape=[jax.ShapeDtypeStruct((tp, LANES), F32),
                   jax.ShapeDtypeStruct((SUBLANES, tp), F32)],
        compiler_params=_cparams("arbitrary"),
    )(rest, bf_pad)


def _pick_col(blk, head):
    lane = lax.broadcasted_iota(jnp.int32, blk.shape, 1)
    return jnp.sum(jnp.where(lane == head, blk, 0.0), axis=1, keepdims=True)


def _pick_row(blk, head):
    sub = lax.broadcasted_iota(jnp.int32, blk.shape, 0)
    return jnp.sum(jnp.where(sub == head, blk, 0.0), axis=0, keepdims=True)


def _att_tiles(tp):
    out, r0 = [], 0
    while r0 < tp:
        rows = min(ATT_TQ, tp - r0)
        out.append((r0, rows, r0 + rows))
        r0 += rows
    return out


def attn_fwd(qkv, c, ct, nh):
    tp = qkv.shape[0]
    npair = nh // 2
    tiles = _att_tiles(tp)

    def body(q_ref, k_ref, v_ref, c_ref, ct_ref, o_ref, lset_ref):
        p = pl.program_id(0)
        lset_ref[...] = jnp.zeros_like(lset_ref)
        for r0, nr, nk in tiles:
            rs = slice(r0, r0 + nr)
            causal = (r0 + lax.broadcasted_iota(jnp.int32, (nr, nk), 0)
                      >= lax.broadcasted_iota(jnp.int32, (nr, nk), 1))
            cblk = c_ref[rs, :]
            ctb = ct_ref[:, 0:nk]
            for hh in range(2):
                head = 2 * p + hh
                hs = slice(hh * HEAD_DIM, (hh + 1) * HEAD_DIM)
                q = q_ref[rs, hs] * ATT_SCALE
                s = _dot_nt(q, k_ref[0:nk, hs]) + (_pick_col(cblk, head) - _pick_row(ctb, head))
                s = jnp.where(causal, s, NEG_BIG)
                m = jnp.max(s, axis=1, keepdims=True)
                pm = jnp.exp(s - m)
                l = jnp.sum(pm, axis=1, keepdims=True)
                o_ref[rs, hs] = _dot(pm.astype(BF16), v_ref[0:nk, hs]) / l
                lse = m + jnp.log(l)
                lset_ref[hh:hh + 1, rs] = jnp.broadcast_to(lse, (nr, LANES)).T[0:1, :]

    pair = lambda p: (0, p)
    return pl.pallas_call(
        body, name="attn_fwd", grid=(npair,),
        in_specs=[pl.BlockSpec((tp, LANES), pair),
                  pl.BlockSpec((tp, LANES), lambda p: (0, npair + p)),
                  pl.BlockSpec((tp, LANES), lambda p: (0, 2 * npair + p)),
                  pl.BlockSpec((tp, LANES), lambda p: (0, 0)),
                  pl.BlockSpec((SUBLANES, tp), lambda p: (0, 0))],
        out_specs=[pl.BlockSpec((tp, LANES), pair),
                   pl.BlockSpec((None, SUBLANES, tp), lambda p: (p, 0, 0))],
        out_shape=[jax.ShapeDtypeStruct((tp, nh * HEAD_DIM), F32),
                   jax.ShapeDtypeStruct((npair, SUBLANES, tp), F32)],
        compiler_params=_cparams("parallel"),
    )(qkv, qkv, qkv, c, ct)


def _shift_down(x, k, n):
    if k == 0:
        return x
    rows = lax.broadcasted_iota(jnp.int32, x.shape, 0)
    return jnp.where(rows >= k, pltpu.roll(x, k, 0), 0.0)


def _shift_up(x, k, n):
    if k == 0:
        return x
    rows = lax.broadcasted_iota(jnp.int32, x.shape, 0)
    return jnp.where(rows < n - k, pltpu.roll(x, n - k, 0), 0.0)


def _conv_fwd(xr, cw_ref, cb_ref, n):
    xc = cw_ref[CONV_WIDTH - 1:CONV_WIDTH, :] * xr + cb_ref[...]
    for k in range(1, CONV_WIDTH):
        xc = xc + cw_ref[CONV_WIDTH - 1 - k:CONV_WIDTH - k, :] * _shift_down(xr, k, n)
    return xc


def _gates(xc, wga_ref, bga_ref, wgx_ref, bgx_ref, l_ref):
    xcb = xc.astype(BF16)
    r = _sigmoid(_dot(xcb, wga_ref[...]) + bga_ref[...])
    ig = _sigmoid(_dot(xcb, wgx_ref[...]) + bgx_ref[...])
    ls = _log_sigmoid(l_ref[...])
    log_a = RG_C * r * ls
    a = jnp.exp(log_a)
    mult = jnp.sqrt(-_expm1(2.0 * log_a))
    return xcb, r, ig, ls, log_a, a, mult


SCAN_UNROLL = 4


def _scan_rows(a_s, u_s, out_ref, n, reverse):
    nt = n // SUBLANES
    per = SCAN_UNROLL if nt % SCAN_UNROLL == 0 else 1
    row = lax.broadcasted_iota(jnp.int32, (SUBLANES, LANES), 0)
    last = 0 if reverse else SUBLANES - 1

    def tile_scan(a, u):
        for d in (1, 2, 4):
            if reverse:
                keep = row < SUBLANES - d
                sh = SUBLANES - d
            else:
                keep = row >= d
                sh = d
            a_sh = jnp.where(keep, pltpu.roll(a, sh, 0), 1.0)
            u_sh = jnp.where(keep, pltpu.roll(u, sh, 0), 0.0)
            u = a * u_sh + u
            a = a * a_sh
        return a, u

    def step(t, carry):
        tiles = []
        for k in range(per):
            tt = t * per + k
            if reverse:
                tt = nt - 1 - tt
            off = pl.multiple_of(tt * SUBLANES, SUBLANES)
            a, u = tile_scan(a_s[pl.ds(off, SUBLANES), :], u_s[pl.ds(off, SUBLANES), :])
            tiles.append((off, a, u))
        for off, a, u in tiles:
            out_ref[pl.ds(off, SUBLANES), :] = u + a * carry
            carry = u[last:last + 1, :] + a[last:last + 1, :] * carry
        return carry

    lax.fori_loop(0, nt // per, step, jnp.zeros((1, LANES), F32))


def rec_fwd(rest, convw, convb, wga, bga, wgx, bgx, lru, rw):
    tp = rest.shape[0]
    ng = rw // LANES

    def body(xr_ref, yr_ref, cw_ref, cb_ref, wga_ref, bga_ref, wgx_ref, bgx_ref, l_ref,
             rec_ref, hr_ref, xc_ref, r_ref, ig_ref, a_ref, mult_ref, u_s):
        xc = _conv_fwd(xr_ref[...], cw_ref, cb_ref, tp)
        xc_ref[...] = xc
        _, r, ig, ls, log_a, a, mult = _gates(xc, wga_ref, bga_ref, wgx_ref, bgx_ref, l_ref)
        r_ref[...] = r
        ig_ref[...] = ig
        a_ref[...] = a
        mult_ref[...] = mult
        u_s[...] = mult * ig * xc
        _scan_rows(a_ref, u_s, hr_ref, tp, reverse=False)
        rec_ref[...] = hr_ref[...] * _gelu(yr_ref[...])

    col = lambda g: (0, g)
    vec = pl.BlockSpec((1, LANES), col)
    big = pl.BlockSpec((tp, LANES), col)
    return pl.pallas_call(
        body, name="rec_fwd", grid=(ng,),
        in_specs=[big, pl.BlockSpec((tp, LANES), lambda g: (0, ng + g)),
                  pl.BlockSpec((CONV_WIDTH, LANES), col), vec,
                  pl.BlockSpec((None, LANES, LANES), lambda g: (g, 0, 0)), vec,
                  pl.BlockSpec((None, LANES, LANES), lambda g: (g, 0, 0)), vec, vec],
        out_specs=[big] * 7,
        out_shape=[jax.ShapeDtypeStruct((tp, rw), F32)] * 7,
        scratch_shapes=[pltpu.VMEM((tp, LANES), F32)],
        compiler_params=_cparams("parallel"),
    )(rest, rest, convw, convb, wga, bga, wgx, bgx, lru)


def out_proj(h, o, rec, ga, gr, wout, g2, tm):
    tp, d = h.shape
    aw, rw = o.shape[1], rec.shape[1]

    def body(h_ref, o_ref, rec_ref, ga_ref, gr_ref, w_ref, g2_ref, h2_ref, mix_ref, z2_ref):
        mix_ref[:, 0:aw] = _rms_fwd(o_ref[...], ga_ref[...]).astype(BF16)
        mix_ref[:, aw:aw + rw] = _rms_fwd(rec_ref[...], gr_ref[...]).astype(BF16)
        h2 = h_ref[...] + _dot(mix_ref[...], w_ref[...])
        h2_ref[...] = h2
        z2_ref[...] = _rms_fwd(h2, g2_ref[...]).astype(BF16)

    row = lambda i: (i, 0)
    fix = lambda i: (0, 0)
    return pl.pallas_call(
        body, name="out_proj", grid=(tp // tm,),
        in_specs=[pl.BlockSpec((tm, d), row), pl.BlockSpec((tm, aw), row), pl.BlockSpec((tm, rw), row),
                  pl.BlockSpec((1, aw), fix), pl.BlockSpec((1, rw), fix),
                  pl.BlockSpec((d, d), fix), pl.BlockSpec((1, d), fix)],
        out_specs=[pl.BlockSpec((tm, d), row)] * 3,
        out_shape=[jax.ShapeDtypeStruct((tp, d), F32), jax.ShapeDtypeStruct((tp, d), BF16),
                   jax.ShapeDtypeStruct((tp, d), BF16)],
        compiler_params=_cparams("parallel"),
    )(h, o, rec, ga, gr, wout, g2)


MLP_BLOCKS = 4


def mlp_fwd(z2, h2, gup, gdown, tm):
    tp, d = h2.shape
    nf = gup.shape[0]
    tf = gup.shape[2]
    nb = MLP_BLOCKS if nf % MLP_BLOCKS == 0 else 1
    nj = nf // nb

    def body(z_ref, h_ref, wu_ref, wd_ref, u_ref, h3_ref, acc):
        j = pl.program_id(1)
        z = z_ref[...]
        part = None
        for b in range(nb):
            u = jnp.maximum(_dot(z, wu_ref[b]), 0.0)
            u_ref[:, b * tf:(b + 1) * tf] = u.astype(BF16)
            p = _dot((u * u).astype(BF16), wd_ref[b])
            part = p if part is None else part + p

        @pl.when(j == 0)
        def _():
            acc[...] = h_ref[...] + part

        @pl.when(j > 0)
        def _():
            acc[...] += part

        @pl.when(j == nj - 1)
        def _():
            h3_ref[...] = acc[...]

    return pl.pallas_call(
        body, name="mlp_fwd", grid=(tp // tm, nj),
        in_specs=[pl.BlockSpec((tm, d), lambda i, j: (i, 0)),
                  pl.BlockSpec((tm, d), lambda i, j: (i, 0)),
                  pl.BlockSpec((nb, d, tf), lambda i, j: (j, 0, 0)),
                  pl.BlockSpec((nb, tf, d), lambda i, j: (j, 0, 0))],
        out_specs=[pl.BlockSpec((tm, nb * tf), lambda i, j: (i, j)),
                   pl.BlockSpec((tm, d), lambda i, j: (i, 0))],
        out_shape=[jax.ShapeDtypeStruct((tp, nf * tf), BF16), jax.ShapeDtypeStruct((tp, d), F32)],
        scratch_shapes=[pltpu.VMEM((tm, d), F32)],
        compiler_params=_cparams("parallel", "arbitrary"),
    )(z2, h2, gup, gdown)


def loss_head(h, gf, tgt, t_real, tm):
    tp, d = h.shape

    def body(h_ref, g_ref, t_ref, dh_ref, dg_ref, loss_ref):
        i = pl.program_id(0)
        x = h_ref[...]
        g = g_ref[...]
        r = lax.rsqrt(jnp.mean(x * x, axis=-1, keepdims=True) + NORM_EPS)
        xn = x * r
        rows = i * tm + lax.broadcasted_iota(jnp.int32, (tm, 1), 0)
        valid = jnp.logical_and(rows >= N_META, rows < t_real)
        e = jnp.where(valid, xn * g - t_ref[...], 0.0)
        part = 0.5 * jnp.sum(jnp.sum(e * e, axis=1, keepdims=True) / d, axis=0, keepdims=True)
        dy = e / d
        dxn = dy * g
        dh_ref[...] = r * (dxn - xn * jnp.mean(dxn * xn, axis=-1, keepdims=True))
        _accumulate(dg_ref, jnp.sum(dy * xn, axis=0, keepdims=True), i == 0)
        _accumulate(loss_ref, jnp.broadcast_to(part, (1, LANES)), i == 0)

    row = lambda i: (i, 0)
    fix = lambda i: (0, 0)
    return pl.pallas_call(
        body, name="loss_head", grid=(tp // tm,),
        in_specs=[pl.BlockSpec((tm, d), row), pl.BlockSpec((1, d), fix), pl.BlockSpec((tm, d), row)],
        out_specs=[pl.BlockSpec((tm, d), row), pl.BlockSpec((1, d), fix), pl.BlockSpec((1, LANES), fix)],
        out_shape=[jax.ShapeDtypeStruct((tp, d), F32), jax.ShapeDtypeStruct((1, d), F32),
                   jax.ShapeDtypeStruct((1, LANES), F32)],
        compiler_params=_cparams("arbitrary"),
    )(h, gf, tgt)


def mlp_bwd(dh, u, h2, g2, gup, gdown, tm):
    tp, d = dh.shape
    nf = gup.shape[0]
    tf = gup.shape[2]
    nb = MLP_BLOCKS if nf % MLP_BLOCKS == 0 else 1
    nj = nf // nb
    ni = tp // tm

    def body(dh_ref, u_ref, h2_ref, g_ref, wu_ref, wd_ref, dup_ref, dh2_ref, dg_ref, dhb, acc):
        i = pl.program_id(0)
        j = pl.program_id(1)

        @pl.when(j == 0)
        def _():
            dhb[...] = dh_ref[...].astype(BF16)

        part = None
        for b in range(nb):
            cols = slice(b * tf, (b + 1) * tf)
            dup = (_dot_nt(dhb[...], wd_ref[b]) * (2.0 * u_ref[:, cols].astype(F32))).astype(BF16)
            dup_ref[:, cols] = dup
            p = _dot_nt(dup, wu_ref[b])
            part = p if part is None else part + p
        _accumulate(acc, part, j == 0)

        @pl.when(j == nj - 1)
        def _():
            dx, dg = _rms_bwd(h2_ref[...], g_ref[...], acc[...])
            dh2_ref[...] = dh_ref[...] + dx
            _accumulate(dg_ref, dg, i == 0)

    return pl.pallas_call(
        body, name="mlp_bwd", grid=(ni, nj),
        in_specs=[pl.BlockSpec((tm, d), lambda i, j: (i, 0)),
                  pl.BlockSpec((tm, nb * tf), lambda i, j: (i, j)),
                  pl.BlockSpec((tm, d), lambda i, j: (i, 0)),
                  pl.BlockSpec((1, d), lambda i, j: (0, 0)),
                  pl.BlockSpec((nb, d, tf), lambda i, j: (j, 0, 0)),
                  pl.BlockSpec((nb, tf, d), lambda i, j: (j, 0, 0))],
        out_specs=[pl.BlockSpec((tm, nb * tf), lambda i, j: (i, j)),
                   pl.BlockSpec((tm, d), lambda i, j: (i, 0)),
                   pl.BlockSpec((1, d), lambda i, j: (0, 0)),
                   pl.BlockSpec((tm, d), lambda i, j: (i, 0))],
        out_shape=[jax.ShapeDtypeStruct((tp, nf * tf), BF16), jax.ShapeDtypeStruct((tp, d), F32),
                   jax.ShapeDtypeStruct((1, d), F32), jax.ShapeDtypeStruct((tp, d), BF16)],
        scratch_shapes=[pltpu.VMEM((tm, d), F32)],
        compiler_params=_cparams("arbitrary", "arbitrary"),
    )(dh, u, h2, g2, gup, gdown)


def mm_tn(a, b, *, tn, name):
    rows, kk = a.shape
    nn = b.shape[1]

    def body(a_ref, b_ref, o_ref, at):
        @pl.when(pl.program_id(0) == 0)
        def _():
            at[...] = a_ref[...].T

        o_ref[...] = _dot(at[...], b_ref[...].astype(BF16)).astype(BF16)

    return pl.pallas_call(
        body, name=name, grid=(nn // tn,),
        in_specs=[pl.BlockSpec((rows, kk), lambda n: (0, 0)),
                  pl.BlockSpec((rows, tn), lambda n: (0, n))],
        out_specs=pl.BlockSpec((kk, tn), lambda n: (0, n)),
        out_shape=jax.ShapeDtypeStruct((kk, nn), BF16),
        scratch_shapes=[pltpu.VMEM((kk, rows), BF16)],
        compiler_params=_cparams("arbitrary"),
    )(a, b)


def dw_mlp(u, dhb, z2, dup, tf):
    rows, dff = u.shape
    d = z2.shape[1]
    nf = dff // tf

    def body(u_ref, dh_ref, z_ref, dup_ref, dwd_ref, dwu_ref, zt):
        @pl.when(pl.program_id(0) == 0)
        def _():
            zt[...] = z_ref[...].T

        uf = u_ref[...].astype(F32)
        dwd_ref[...] = _dot_tn((uf * uf).astype(BF16), dh_ref[...]).astype(BF16)
        dwu_ref[...] = _dot(zt[...], dup_ref[...]).astype(BF16)

    col = lambda j: (0, j)
    fix = lambda j: (0, 0)
    return pl.pallas_call(
        body, name="dw_mlp", grid=(nf,),
        in_specs=[pl.BlockSpec((rows, tf), col), pl.BlockSpec((rows, d), fix),
                  pl.BlockSpec((rows, d), fix), pl.BlockSpec((rows, tf), col)],
        out_specs=[pl.BlockSpec((None, tf, d), lambda j: (j, 0, 0)),
                   pl.BlockSpec((None, d, tf), lambda j: (j, 0, 0))],
        out_shape=[jax.ShapeDtypeStruct((nf, tf, d), BF16), jax.ShapeDtypeStruct((nf, d, tf), BF16)],
        scratch_shapes=[pltpu.VMEM((d, rows), BF16)],
        compiler_params=_cparams("arbitrary"),
    )(u, dhb, z2, dup)


def out_proj_bwd(dh2, o, rec, ga, gr, wout, tm):
    tp, d = dh2.shape
    aw, rw = o.shape[1], rec.shape[1]

    def body(dh_ref, o_ref, rec_ref, ga_ref, gr_ref, w_ref, do_ref, drec_ref, dga_ref, dgr_ref):
        i = pl.program_id(0)
        dmix = _dot_nt(dh_ref[...].astype(BF16), w_ref[...])
        do, dga = _rms_bwd(o_ref[...], ga_ref[...], dmix[:, 0:aw])
        drec, dgr = _rms_bwd(rec_ref[...], gr_ref[...], dmix[:, aw:aw + rw])
        do_ref[...] = do
        drec_ref[...] = drec
        _accumulate(dga_ref, dga, i == 0)
        _accumulate(dgr_ref, dgr, i == 0)

    row = lambda i: (i, 0)
    fix = lambda i: (0, 0)
    return pl.pallas_call(
        body, name="out_proj_bwd", grid=(tp // tm,),
        in_specs=[pl.BlockSpec((tm, d), row), pl.BlockSpec((tm, aw), row), pl.BlockSpec((tm, rw), row),
                  pl.BlockSpec((1, aw), fix), pl.BlockSpec((1, rw), fix), pl.BlockSpec((d, d), fix)],
        out_specs=[pl.BlockSpec((tm, aw), row), pl.BlockSpec((tm, rw), row),
                   pl.BlockSpec((1, aw), fix), pl.BlockSpec((1, rw), fix)],
        out_shape=[jax.ShapeDtypeStruct((tp, aw), F32), jax.ShapeDtypeStruct((tp, rw), F32),
                   jax.ShapeDtypeStruct((1, aw), F32), jax.ShapeDtypeStruct((1, rw), F32)],
        compiler_params=_cparams("arbitrary"),
    )(dh2, o, rec, ga, gr, wout)


def rec_bwd(drec, hr, xc, gates, rest, convw, wga, wgx, lru, rw):
    tp = rest.shape[0]
    ng = rw // LANES

    def body(drec_ref, hr_ref, xc_ref, r_ref, ig_ref, a_ref, mult_ref, xr_ref, yr_ref, cw_ref, wga_ref, wgx_ref,
             l_ref, dxr_ref, dyr_ref, dwga_ref, dwgx_ref, vec_ref, a_s, u_s, lam_s):
        xc = xc_ref[...]
        h = hr_ref[...]
        drec = drec_ref[...]
        r, ig, a, mult = r_ref[...], ig_ref[...], a_ref[...], mult_ref[...]
        xcb = xc.astype(BF16)
        ls = _log_sigmoid(l_ref[...])
        gelu, gelu_grad = _gelu_and_grad(yr_ref[...])
        dyr_ref[...] = (drec * h * gelu_grad).astype(BF16)
        a_s[...] = _shift_up(a, 1, tp)
        u_s[...] = drec * gelu
        _scan_rows(a_s, u_s, lam_s, tp, reverse=True)
        lam = lam_s[...]
        da = lam * _shift_down(h, 1, tp)
        dmult = lam * ig * xc
        dig = lam * mult * xc
        dxc = lam * mult * ig
        dlog_a = da * a - dmult * (a * a) / mult
        dr = dlog_a * (RG_C * ls)
        dl = jnp.sum(dlog_a * (RG_C * r), axis=0, keepdims=True) * _sigmoid(-l_ref[...])
        dpa = dr * r * (1.0 - r)
        dpx = dig * ig * (1.0 - ig)
        dpab = dpa.astype(BF16)
        dpxb = dpx.astype(BF16)
        dxc = dxc + _dot_nt(dpab, wga_ref[...]) + _dot_nt(dpxb, wgx_ref[...])
        dwga_ref[...] = _dot_tn(xcb, dpab)
        dwgx_ref[...] = _dot_tn(xcb, dpxb)
        xr = xr_ref[...]
        dxr = cw_ref[CONV_WIDTH - 1:CONV_WIDTH, :] * dxc
        for k in range(1, CONV_WIDTH):
            dxr = dxr + cw_ref[CONV_WIDTH - 1 - k:CONV_WIDTH - k, :] * _shift_up(dxc, k, tp)
        dxr_ref[...] = dxr.astype(BF16)
        for k in range(CONV_WIDTH):
            vec_ref[k:k + 1, :] = jnp.sum(dxc * _shift_down(xr, CONV_WIDTH - 1 - k, tp), axis=0, keepdims=True)
        vec_ref[4:5, :] = jnp.sum(dxc, axis=0, keepdims=True)
        vec_ref[5:6, :] = jnp.sum(dpa, axis=0, keepdims=True)
        vec_ref[6:7, :] = jnp.sum(dpx, axis=0, keepdims=True)
        vec_ref[7:8, :] = dl

    col = lambda g: (0, g)
    vec = pl.BlockSpec((1, LANES), col)
    big = pl.BlockSpec((tp, LANES), col)
    sq = pl.BlockSpec((None, LANES, LANES), lambda g: (g, 0, 0))
    return pl.pallas_call(
        body, name="rec_bwd", grid=(ng,),
        in_specs=[big] * 8 + [pl.BlockSpec((tp, LANES), lambda g: (0, ng + g)),
                                pl.BlockSpec((CONV_WIDTH, LANES), col), sq, sq, vec],
        out_specs=[big, big, sq, sq, pl.BlockSpec((None, SUBLANES, LANES), lambda g: (g, 0, 0))],
        out_shape=[jax.ShapeDtypeStruct((tp, rw), BF16), jax.ShapeDtypeStruct((tp, rw), BF16),
                   jax.ShapeDtypeStruct((ng, LANES, LANES), F32), jax.ShapeDtypeStruct((ng, LANES, LANES), F32),
                   jax.ShapeDtypeStruct((ng, SUBLANES, LANES), F32)],
        scratch_shapes=[pltpu.VMEM((tp, LANES), F32)] * 3,
        compiler_params=_cparams("parallel"),
    )(drec, hr, xc, *gates, rest, rest, convw, wga, wgx, lru)


def attn_bwd(qkv, do, o, lset, c, ct, nh):
    tp = qkv.shape[0]
    npair = nh // 2
    aw = nh * HEAD_DIM
    tiles = _att_tiles(tp)

    def body(q_ref, k_ref, v_ref, do_ref, o_ref, lset_ref, c_ref, ct_ref,
             dq_ref, dk_ref, dv_ref, drow_ref, dcol_ref, dk_acc, dv_acc, dq_t):
        p = pl.program_id(0)
        k_t = k_ref[...].T
        dk_acc[...] = jnp.zeros_like(dk_acc)
        dv_acc[...] = jnp.zeros_like(dv_acc)
        dcol_ref[...] = jnp.zeros_like(dcol_ref)
        drow_ref[...] = jnp.zeros_like(drow_ref)
        for r0, nr, nk in tiles:
            rs = slice(r0, r0 + nr)
            causal = (r0 + lax.broadcasted_iota(jnp.int32, (nk, nr), 1)
                      >= lax.broadcasted_iota(jnp.int32, (nk, nr), 0))
            cblk = c_ref[0:nk, :]
            ctb = ct_ref[:, rs]
            for hh in range(2):
                head = 2 * p + hh
                hs = slice(hh * HEAD_DIM, (hh + 1) * HEAD_DIM)
                q = q_ref[rs, hs]
                k = k_ref[0:nk, hs]
                dof = do_ref[rs, hs]
                do16 = dof.astype(BF16)
                delta = jnp.sum(dof * o_ref[rs, hs], axis=1, keepdims=True)
                delta_row = jnp.broadcast_to(delta, (nr, LANES)).T[0:1, :]
                s_t = _dot_nt(k, q * ATT_SCALE) + (_pick_row(ctb, head) - _pick_col(cblk, head))
                p_t = jnp.where(causal, jnp.exp(s_t - lset_ref[hh:hh + 1, rs]), 0.0)
                ds_t = p_t * (_dot_nt(v_ref[0:nk, hs], do16) - delta_row)
                p16 = p_t.astype(BF16)
                ds16 = ds_t.astype(BF16)
                dv_acc[0:nk, hs] += _dot(p16, do16)
                dk_acc[0:nk, hs] += _dot(ds16, q) * ATT_SCALE
                dq_t[hs, rs] = _dot(k_t[hs, 0:nk], ds16)
                drow_ref[hh:hh + 1, rs] = jnp.sum(ds_t, axis=0, keepdims=True)
                dcol_ref[0:nk, hs] -= jnp.broadcast_to(jnp.sum(ds_t, axis=1, keepdims=True), (nk, HEAD_DIM))
        dk_ref[...] = dk_acc[...].astype(BF16)
        dv_ref[...] = dv_acc[...].astype(BF16)
        dq_ref[...] = (dq_t[...].T * ATT_SCALE).astype(BF16)

    pair = lambda p: (0, p)
    return pl.pallas_call(
        body, name="attn_bwd", grid=(npair,),
        in_specs=[pl.BlockSpec((tp, LANES), pair),
                  pl.BlockSpec((tp, LANES), lambda p: (0, npair + p)),
                  pl.BlockSpec((tp, LANES), lambda p: (0, 2 * npair + p)),
                  pl.BlockSpec((tp, LANES), pair),
                  pl.BlockSpec((tp, LANES), pair),
                  pl.BlockSpec((None, SUBLANES, tp), lambda p: (p, 0, 0)),
                  pl.BlockSpec((tp, LANES), lambda p: (0, 0)),
                  pl.BlockSpec((SUBLANES, tp), lambda p: (0, 0))],
        out_specs=[pl.BlockSpec((tp, LANES), pair), pl.BlockSpec((tp, LANES), pair),
                   pl.BlockSpec((tp, LANES), pair),
                   pl.BlockSpec((None, SUBLANES, tp), lambda p: (p, 0, 0)),
                   pl.BlockSpec((tp, LANES), pair)],
        out_shape=[jax.ShapeDtypeStruct((tp, aw), BF16), jax.ShapeDtypeStruct((tp, aw), BF16),
                   jax.ShapeDtypeStruct((tp, aw), BF16),
                   jax.ShapeDtypeStruct((npair, SUBLANES, tp), F32),
                   jax.ShapeDtypeStruct((tp, aw), F32)],
        scratch_shapes=[pltpu.VMEM((tp, LANES), F32), pltpu.VMEM((tp, LANES), F32),
                        pltpu.VMEM((LANES, tp), F32)],
        compiler_params=_cparams("parallel"),
    )(qkv, qkv, qkv, do, o, lset, c, ct)


def fgate_bwd(dct8, drs, rest, bf_pad, fcol):
    tp = rest.shape[0]
    aw = drs.shape[1]
    nb = tp // ATT_BLOCK
    B = ATT_BLOCK

    def body(d_ref, drs_ref, f_ref, b_ref, dfl_ref, db_ref, pad_s):
        r_i = lax.broadcasted_iota(jnp.int32, (B, B), 0)
        c_i = lax.broadcasted_iota(jnp.int32, (B, B), 1)
        triu = (c_i >= r_i).astype(BF16)
        sel = (lax.broadcasted_iota(jnp.int32, (aw, LANES), 0)
               == HEAD_DIM * lax.broadcasted_iota(jnp.int32, (aw, LANES), 1)).astype(BF16)
        carry = jnp.zeros((1, LANES), F32)
        db = jnp.zeros((1, LANES), F32)
        pad_s[...] = jnp.zeros_like(pad_s)
        for i in range(nb - 1, -1, -1):
            sl = slice(i * B, (i + 1) * B)
            pad_s[0:SUBLANES, :] = d_ref[:, sl]
            dc = pad_s[...].T + _dot_split3(drs_ref[sl, :], sel)
            rc = _split3_dot(triu, dc)
            dlf = rc + carry
            carry = carry + rc[0:1, :]
            dfl = dlf * _sigmoid(-(f_ref[sl, :] + b_ref[...]))
            dfl_ref[sl, :] = dfl.astype(BF16)
            db = db + jnp.sum(dfl, axis=0, keepdims=True)
        db_ref[...] = db

    return pl.pallas_call(
        body, name="fgate_bwd", grid=(1,),
        in_specs=[pl.BlockSpec((SUBLANES, tp), lambda i: (0, 0)),
                  pl.BlockSpec((tp, aw), lambda i: (0, 0)),
                  pl.BlockSpec((tp, LANES), lambda i: (0, fcol)),
                  pl.BlockSpec((1, LANES), lambda i: (0, 0))],
        out_specs=[pl.BlockSpec((tp, LANES), lambda i: (0, 0)),
                   pl.BlockSpec((1, LANES), lambda i: (0, 0))],
        out_shape=[jax.ShapeDtypeStruct((tp, LANES), BF16), jax.ShapeDtypeStruct((1, LANES), F32)],
        scratch_shapes=[pltpu.VMEM((B, B), F32)],
        compiler_params=_cparams("arbitrary"),
    )(dct8, drs, rest, bf_pad)


def in_proj_bwd(dh2, parts, w_in_t, wrest_t, h, g1, tm):
    tp, d = h.shape
    dq, dk, dv, dxr, dyr, dfl = parts
    aw, rw = dq.shape[1], dxr.shape[1]

    def body(dh2_ref, dq_ref, dk_ref, dv_ref, dxr_ref, dyr_ref, dfl_ref, wq_ref, wr_ref, h_ref, g_ref,
             dh_ref, dg_ref):
        i = pl.program_id(0)
        dz = _dot(dq_ref[...], wq_ref[0:aw, :])
        dz += _dot(dk_ref[...], wq_ref[aw:2 * aw, :])
        dz += _dot(dv_ref[...], wq_ref[2 * aw:3 * aw, :])
        dz += _dot(dxr_ref[...], wr_ref[0:rw, :])
        dz += _dot(dyr_ref[...], wr_ref[rw:2 * rw, :])
        dz += _dot(dfl_ref[...], wr_ref[2 * rw:2 * rw + LANES, :])
        dx, dg = _rms_bwd(h_ref[...], g_ref[...], dz)
        dh_ref[...] = dh2_ref[...] + dx
        _accumulate(dg_ref, dg, i == 0)

    row = lambda i: (i, 0)
    fix = lambda i: (0, 0)
    return pl.pallas_call(
        body, name="in_proj_bwd", grid=(tp // tm,),
        in_specs=[pl.BlockSpec((tm, d), row),
                  pl.BlockSpec((tm, aw), row), pl.BlockSpec((tm, aw), row), pl.BlockSpec((tm, aw), row),
                  pl.BlockSpec((tm, rw), row), pl.BlockSpec((tm, rw), row), pl.BlockSpec((tm, LANES), row),
                  pl.BlockSpec((3 * aw, d), fix), pl.BlockSpec(wrest_t.shape, fix),
                  pl.BlockSpec((tm, d), row), pl.BlockSpec((1, d), fix)],
        out_specs=[pl.BlockSpec((tm, d), row), pl.BlockSpec((1, d), fix)],
        out_shape=[jax.ShapeDtypeStruct((tp, d), F32), jax.ShapeDtypeStruct((1, d), F32)],
        compiler_params=_cparams("arbitrary"),
    )(dh2, dq, dk, dv, dxr, dyr, dfl, w_in_t, wrest_t, h, g1)


def dw_in_t(z, parts, nh, tr):
    tp, d = z.shape
    dq, dk, dv, dxr, dyr, dfl = parts
    aw, rw = dq.shape[1], dxr.shape[1]
    d_in = 3 * aw + nh + 2 * rw
    nr = tp // tr
    offs = [(0, aw), (aw, aw), (2 * aw, aw), (3 * aw + nh, rw), (3 * aw + nh + rw, rw)]

    def body(z_ref, dq_ref, dk_ref, dv_ref, dxr_ref, dyr_ref, dfl_ref, o_ref, acc):
        r = pl.program_id(0)

        @pl.when(r == 0)
        def _():
            acc[...] = jnp.zeros_like(acc)

        zt = z_ref[...]
        for (o, n), ref in zip(offs, (dq_ref, dk_ref, dv_ref, dxr_ref, dyr_ref)):
            acc[o:o + n, :] += _dot_tn(ref[...], zt)
        acc[3 * aw:3 * aw + nh, :] += _dot_tn(dfl_ref[...], zt)[0:nh, :]

        @pl.when(r == nr - 1)
        def _():
            o_ref[...] = acc[...].astype(BF16)

    row = lambda r: (r, 0)
    return pl.pallas_call(
        body, name="dw_in", grid=(nr,),
        in_specs=[pl.BlockSpec((tr, d), row),
                  pl.BlockSpec((tr, aw), row), pl.BlockSpec((tr, aw), row), pl.BlockSpec((tr, aw), row),
                  pl.BlockSpec((tr, rw), row), pl.BlockSpec((tr, rw), row), pl.BlockSpec((tr, LANES), row)],
        out_specs=pl.BlockSpec((d_in, d), lambda r: (0, 0)),
        out_shape=jax.ShapeDtypeStruct((d_in, d), BF16),
        scratch_shapes=[pltpu.VMEM((d_in, d), F32)],
        compiler_params=_cparams("arbitrary"),
    )(z, dq, dk, dv, dxr, dyr, dfl)


def _place():
    return lax.axis_index("x"), lax.axis_index("y"), lax.axis_index("c")


HBM = pl.BlockSpec(memory_space=pltpu.HBM)
SEM = pl.BlockSpec(memory_space=pltpu.SEMAPHORE)
EFFECT = pltpu.SideEffectType.DATAFLOW_SIDE_EFFECTING


def _in_hbm(a):
    return pltpu.with_memory_space_constraint(a, pltpu.HBM)


def _as_list(a):
    return list(a) if isinstance(a, (list, tuple)) else [a]


def _gather_targets(x, y, c):
    return [(x, y, 1 - c), (1 - x, y, c), (x, 1 - y, c), (1 - x, 1 - y, c)]


def _slot(t):
    return 4 * t[0] + 2 * t[1] + t[2]


def gather_start(groups, name):
    flat = [a for g in groups for a in g]
    n = len(flat)
    ng = len(groups)
    lands = [lax.empty((N_DEV,) + a.shape, a.dtype) for a in flat]

    def body(*refs):
        src, land = refs[:n], refs[n:2 * n]
        sems = refs[2 * n:2 * n + 2 * ng]
        token = refs[-1]
        x, y, c = _place()
        me = 4 * x + 2 * y + c
        i = 0
        for gi, g in enumerate(groups):
            for a in range(len(g)):
                for k, t in enumerate(_gather_targets(x, y, c)):
                    pltpu.make_async_remote_copy(
                        src_ref=src[i], dst_ref=land[i].at[me],
                        send_sem=sems[2 * gi].at[4 * a + k], recv_sem=sems[2 * gi + 1].at[4 * a + k],
                        device_id=t, device_id_type=MESH).start()
                i += 1
        token[...] = jnp.zeros_like(token)

    sem_shapes = []
    for g in groups:
        sem_shapes += [pltpu.SemaphoreType.DMA((4 * len(g),)), pltpu.SemaphoreType.DMA((4 * len(g),))]
    out = pl.pallas_call(
        body, name=name,
        out_shape=sem_shapes + [pltpu.HBM(a.shape, a.dtype) for a in flat + lands]
        + [jax.ShapeDtypeStruct((SUBLANES, LANES), F32)],
        in_specs=[HBM] * (2 * n),
        out_specs=[SEM] * (2 * ng) + [HBM] * (2 * n) + [pl.BlockSpec(memory_space=pltpu.VMEM)],
        input_output_aliases={i: 2 * ng + i for i in range(2 * n)},
        compiler_params=pltpu.CompilerParams(has_side_effects=EFFECT),
    )(*[_in_hbm(a) for a in flat + lands])
    sems = out[:2 * ng]
    thru = out[2 * ng:2 * ng + 2 * n]
    srcs_t, lands_t = thru[:n], thru[n:]
    res, i = [], 0
    for gi, g in enumerate(groups):
        res.append((sems[2 * gi], sems[2 * gi + 1], srcs_t[i:i + len(g)], lands_t[i:i + len(g)]))
        i += len(g)
    return res, out[-1]


def gather_wait(send, recv, srcs, lands, after, name):
    n = len(srcs)

    def body(*refs):
        src, land = refs[:n], refs[n:2 * n]
        send_sem, recv_sem = refs[2 * n], refs[2 * n + 1]
        x, y, c = _place()
        for a in range(n):
            for k, t in enumerate(_gather_targets(x, y, c)):
                cp = pltpu.make_async_remote_copy(
                    src_ref=src[a], dst_ref=land[a].at[_slot(t)],
                    send_sem=send_sem.at[4 * a + k], recv_sem=recv_sem.at[4 * a + k],
                    device_id=t, device_id_type=MESH)
                cp.wait_send()
                cp.wait_recv()

    out = pl.pallas_call(
        body, name=name,
        out_shape=[pltpu.HBM(a.shape, a.dtype) for a in list(srcs) + list(lands)],
        in_specs=[HBM] * (2 * n) + [SEM, SEM] + [ANY] * len(_as_list(after)),
        out_specs=[HBM] * (2 * n),
        input_output_aliases={i: i for i in range(2 * n)},
        compiler_params=pltpu.CompilerParams(has_side_effects=EFFECT),
    )(*srcs, *lands, send, recv, *_as_list(after))
    return out[:n], out[n:]


def forward_start(lands, name):
    n = len(lands)

    def body(*refs):
        land = refs[:n]
        send_sem, recv_sem = refs[n], refs[n + 1]
        token = refs[-1]
        x, y, c = _place()
        for a in range(n):
            for j, chip in enumerate([(1 - x, y), (x, 1 - y), (1 - x, 1 - y)]):
                blk = land[a].at[_slot((*chip, c))]
                pltpu.make_async_remote_copy(src_ref=blk, dst_ref=blk, send_sem=send_sem.at[3 * a + j],
                                             recv_sem=recv_sem.at[3 * a + j], device_id=(x, y, 1 - c),
                                             device_id_type=MESH).start()
        token[...] = jnp.zeros_like(token)

    out = pl.pallas_call(
        body, name=name,
        out_shape=[pltpu.SemaphoreType.DMA((3 * n,)), pltpu.SemaphoreType.DMA((3 * n,))]
        + [pltpu.HBM(a.shape, a.dtype) for a in lands] + [jax.ShapeDtypeStruct((SUBLANES, LANES), F32)],
        in_specs=[HBM] * n,
        out_specs=[SEM, SEM] + [HBM] * n + [pl.BlockSpec(memory_space=pltpu.VMEM)],
        input_output_aliases={i: 2 + i for i in range(n)},
        compiler_params=pltpu.CompilerParams(has_side_effects=EFFECT),
    )(*[_in_hbm(a) for a in lands])
    return out[0], out[1], out[2:2 + n], out[-1][0, 0]


def forward_wait(send, recv, lands, after, name):
    n = len(lands)

    def body(*refs):
        land = refs[:n]
        send_sem, recv_sem = refs[n], refs[n + 1]
        x, y, c = _place()
        for a in range(n):
            for j, chip in enumerate([(1 - x, y), (x, 1 - y), (1 - x, 1 - y)]):
                cp = pltpu.make_async_remote_copy(
                    src_ref=land[a].at[_slot((*chip, c))], dst_ref=land[a].at[_slot((*chip, 1 - c))],
                    send_sem=send_sem.at[3 * a + j], recv_sem=recv_sem.at[3 * a + j],
                    device_id=(x, y, 1 - c), device_id_type=MESH)
                cp.wait_send()
                cp.wait_recv()

    return pl.pallas_call(
        body, name=name,
        out_shape=[pltpu.HBM(a.shape, a.dtype) for a in lands],
        in_specs=[HBM] * n + [SEM, SEM, ANY],
        out_specs=[HBM] * n,
        input_output_aliases={i: i for i in range(n)},
        compiler_params=pltpu.CompilerParams(has_side_effects=EFFECT),
    )(*lands, send, recv, after)


def _relations():
    return [(dx, dy, dc) for dx in (0, 1) for dy in (0, 1) for dc in (0, 1) if dx + dy + dc]


def _peer(x, y, c, rel):
    return ((1 - x) if rel[0] else x, (1 - y) if rel[1] else y, (1 - c) if rel[2] else c)


def exchange_start(srcs, lands, layer, name, after=()):
    n = len(srcs)
    after = _as_list(after)

    def body(*refs):
        src, land = refs[:n], refs[n:2 * n]
        send_sem, recv_sem = refs[2 * n + len(after)], refs[2 * n + len(after) + 1]
        token = refs[-1]
        x, y, c = _place()
        me = 4 * x + 2 * y + c
        for k, rel in enumerate(_relations()):
            peer = _peer(x, y, c, rel)
            for a in range(n):
                pltpu.make_async_remote_copy(
                    src_ref=src[a] if layer is None else src[a].at[_slot(peer)],
                    dst_ref=land[a].at[me] if layer is None else land[a].at[me, layer],
                    send_sem=send_sem.at[7 * a + k], recv_sem=recv_sem.at[7 * a + k],
                    device_id=peer, device_id_type=MESH).start()
        token[...] = jnp.zeros_like(token)

    out = pl.pallas_call(
        body, name=name,
        out_shape=[pltpu.SemaphoreType.DMA((7 * n,)), pltpu.SemaphoreType.DMA((7 * n,))]
        + [pltpu.HBM(a.shape, a.dtype) for a in list(srcs) + list(lands)]
        + [jax.ShapeDtypeStruct((SUBLANES, LANES), F32)],
        in_specs=[HBM] * (2 * n) + [ANY] * len(after),
        out_specs=[SEM, SEM] + [HBM] * (2 * n) + [pl.BlockSpec(memory_space=pltpu.VMEM)],
        input_output_aliases={i: 2 + i for i in range(2 * n)},
        compiler_params=pltpu.CompilerParams(has_side_effects=EFFECT),
    )(*[_in_hbm(a) for a in list(srcs) + list(lands)], *after)
    return out[0], out[1], out[2:2 + n], out[2 + n:2 + 2 * n], out[-1][0, 0]


def exchange_wait(send, recv, srcs, lands, after, layer, name):
    n = len(srcs)

    def body(*refs):
        src, land = refs[:n], refs[n:2 * n]
        send_sem, recv_sem = refs[2 * n], refs[2 * n + 1]
        x, y, c = _place()
        for k, rel in enumerate(_relations()):
            peer = _peer(x, y, c, rel)
            for a in range(n):
                cp = pltpu.make_async_remote_copy(
                    src_ref=src[a] if layer is None else src[a].at[_slot(peer)],
                    dst_ref=land[a].at[_slot(peer)] if layer is None else land[a].at[_slot(peer), layer],
                    send_sem=send_sem.at[7 * a + k], recv_sem=recv_sem.at[7 * a + k],
                    device_id=peer, device_id_type=MESH)
                cp.wait_send()
                cp.wait_recv()

    out = pl.pallas_call(
        body, name=name,
        out_shape=[pltpu.HBM(a.shape, a.dtype) for a in list(srcs) + list(lands)],
        in_specs=[HBM] * (2 * n) + [SEM, SEM] + [ANY] * len(_as_list(after)),
        out_specs=[HBM] * (2 * n),
        input_output_aliases={i: i for i in range(2 * n)},
        compiler_params=pltpu.CompilerParams(has_side_effects=EFFECT),
    )(*srcs, *lands, send, recv, *_as_list(after))
    return out[:n], out[n:]


def _adamw_math(g, w, m, v):
    m = ADAM_B1 * m + (1.0 - ADAM_B1) * g
    v = ADAM_B2 * v + (1.0 - ADAM_B2) * (g * g)
    m_hat = m / (1.0 - ADAM_B1 ** ADAM_STEP)
    v_hat = v / (1.0 - ADAM_B2 ** ADAM_STEP)
    delta = -ADAM_LR * (m_hat / (jnp.sqrt(v_hat) + ADAM_EPS) + ADAM_WD * w)
    return delta, m, v


def _sum_with_own(p_ref, own_refs, layer, me):
    own = own_refs[0][...]
    for k in range(1, len(own_refs)):
        own = jnp.where(layer == k, own_refs[k][...], own)
    g = None
    for p in range(p_ref.shape[0]):
        term = jnp.where(me == p, own, p_ref[p]).astype(F32)
        g = term if g is None else g + term
    return g


def sum_adamw(parts, owns, me, w, m, v, tr, name):
    npart, rows, cols = parts.shape
    nl = len(owns)
    per_layer = rows // nl // tr

    def body(me_ref, p_ref, *refs):
        own_refs = refs[:nl]
        w_ref, m_ref, v_ref, g_ref, d_ref, nm_ref, nv_ref = refs[nl:]
        g = _sum_with_own(p_ref, own_refs, pl.program_id(0) // per_layer, me_ref[0])
        delta, nm, nv = _adamw_math(g, w_ref[...], m_ref[...], v_ref[...])
        g_ref[...] = g
        d_ref[...] = delta
        nm_ref[...] = nm
        nv_ref[...] = nv

    blk = pl.BlockSpec((tr, cols), lambda i, me_ref: (i, 0))
    own_specs = [pl.BlockSpec((None, tr, cols),
                              lambda i, me_ref, l=l: (me_ref[0], jnp.clip(i - l * per_layer, 0, per_layer - 1), 0))
                 for l in range(nl)]
    return pl.pallas_call(
        body, name=name,
        grid_spec=pltpu.PrefetchScalarGridSpec(
            num_scalar_prefetch=1, grid=(rows // tr,),
            in_specs=[pl.BlockSpec((npart, tr, cols), lambda i, me_ref: (0, i, 0))] + own_specs + [blk, blk, blk],
            out_specs=[blk] * 4),
        out_shape=[jax.ShapeDtypeStruct((rows, cols), F32)] * 4,
        compiler_params=_cparams("arbitrary"),
    )(me, parts, *owns, w, m, v)


def sum_adamw_t(parts, owns, me, w, m, v, name):
    npart, nl, rows, cols = parts.shape

    def body(me_ref, p_ref, *refs):
        own_refs = refs[:nl]
        w_ref, m_ref, v_ref, g_ref, d_ref, nm_ref, nv_ref = refs[nl:]
        g = _sum_with_own(p_ref, own_refs, pl.program_id(0), me_ref[0])
        delta, nm, nv = _adamw_math(g, w_ref[...], m_ref[...], v_ref[...])
        g_ref[...] = g
        d_ref[...] = delta
        nm_ref[...] = nm
        nv_ref[...] = nv

    blk = pl.BlockSpec((None, rows, cols), lambda l, me_ref: (l, 0, 0))
    own_specs = [pl.BlockSpec((None, rows, cols), lambda l, me_ref: (me_ref[0], 0, 0)) for _ in range(nl)]
    return pl.pallas_call(
        body, name=name,
        grid_spec=pltpu.PrefetchScalarGridSpec(
            num_scalar_prefetch=1, grid=(nl,),
            in_specs=[pl.BlockSpec((npart, None, rows, cols), lambda l, me_ref: (0, l, 0, 0))] + own_specs
            + [blk, blk, blk],
            out_specs=[blk] * 4),
        out_shape=[jax.ShapeDtypeStruct((nl, rows, cols), F32)] * 4,
        compiler_params=_cparams("arbitrary"),
    )(me, parts, *owns, w, m, v)


def adamw_group(gs, ws, ms, vs, name):
    n = len(gs)

    def body(*refs):
        g, w, m, v, outs = refs[:n], refs[n:2 * n], refs[2 * n:3 * n], refs[3 * n:4 * n], refs[4 * n:]
        for i in range(n):
            delta, nm, nv = _adamw_math(g[i][...], w[i][...], m[i][...], v[i][...])
            outs[i][...] = delta
            outs[n + i][...] = nm
            outs[2 * n + i][...] = nv

    vmem = pl.BlockSpec(memory_space=pltpu.VMEM)
    out = pl.pallas_call(
        body, name=name,
        in_specs=[vmem] * (4 * n), out_specs=[vmem] * (3 * n),
        out_shape=[jax.ShapeDtypeStruct(a.shape, F32) for a in list(ws) * 3],
        compiler_params=_cparams(),
    )(*gs, *ws, *ms, *vs)
    return out[:n], out[n:2 * n], out[2 * n:]


def sum_parts(parts, name):
    npart, rows, cols = parts.shape

    def body(p_ref, g_ref):
        g = p_ref[0].astype(F32)
        for p in range(1, npart):
            g = g + p_ref[p].astype(F32)
        g_ref[...] = g

    return pl.pallas_call(
        body, name=name, grid=(1,),
        in_specs=[pl.BlockSpec((npart, rows, cols), lambda i: (0, 0, 0))],
        out_specs=pl.BlockSpec((rows, cols), lambda i: (0, 0)),
        out_shape=jax.ShapeDtypeStruct((rows, cols), F32),
        compiler_params=_cparams("arbitrary"),
    )(parts)


def _round_up(n, m):
    return (n + m - 1) // m * m


def _block_diag_pairs(w):
    nb, b, _ = w.shape
    per = LANES // b
    ng = nb // per
    w = w.reshape(ng, per, b, b)
    eye = jnp.eye(per, dtype=w.dtype)
    out = jnp.einsum('gpij,pq->gpiqj', w, eye).reshape(ng, LANES, LANES)
    return out.astype(BF16)


def _block_diag_extract(g, b):
    ng = g.shape[0]
    per = LANES // b
    g = g.reshape(ng, per, b, per, b)
    idx = jnp.arange(per)
    return g[:, idx, :, idx, :].transpose(1, 0, 2, 3).reshape(ng * per, b, b)


def _tiles(v):
    v = v.reshape(-1)
    n = _round_up(v.shape[0], SUBLANES * LANES)
    return jnp.pad(v, (0, n - v.shape[0])).reshape(-1, LANES)


SMALL = ['attn_norm_g', 'b_f', 'conv_w', 'conv_b', 'w_gate_a', 'b_gate_a', 'w_gate_x', 'b_gate_x',
         'lru_L', 'attn_out_g', 'rec_out_g', 'mlp_norm_g', 'final_g', 'meta']


def _pack(d):
    return jnp.concatenate([_tiles(d[n]) for n in SMALL], axis=0)


def _unpack(vec, shapes):
    out, r = {}, 0
    for n in SMALL:
        size = math.prod(shapes[n])
        nr = _round_up(size, SUBLANES * LANES) // LANES
        out[n] = vec[r:r + nr].reshape(-1)[:size].reshape(shapes[n])
        r += nr
    return out


def _row_tile(tp):
    return tp // 4 if (tp // 4) % 16 == 0 else tp


def local_step(x, tgt, meta, small, hooks):
    s, d = x.shape
    t_real = s + N_META
    tp = _round_up(t_real, ATT_BLOCK)
    depth = small['attn_norm_g'].shape[0]
    nh = small['b_f'].shape[1]
    rw = small['conv_b'].shape[1]
    blk = small['w_gate_a'].shape[2]
    tm = _row_tile(tp)
    tm2 = tp // 2
    fcol = 2 * rw // LANES

    h = jnp.concatenate([meta, x, jnp.zeros((tp - t_real, d), F32)], axis=0)
    tgt_p = jnp.pad(tgt, ((N_META, tp - t_real), (0, 0)))
    row = lambda v: v.reshape(1, -1)
    bf_pad = jnp.pad(small['b_f'], ((0, 0), (0, LANES - nh)))

    saved = []
    for l in range(depth):
        w_in_t, wrest_t, wout, tok_w = hooks.mixer_weights(l, h)
        wga = _block_diag_pairs(small['w_gate_a'][l])
        wgx = _block_diag_pairs(small['w_gate_x'][l])
        z, qkv, rest = in_proj(h, row(small['attn_norm_g'][l]) + tok_w, w_in_t, wrest_t, 3 * nh * HEAD_DIM, tm2)
        c, ct = fgate_fwd(rest, bf_pad[l:l + 1], fcol)
        o, lset = attn_fwd(qkv, c, ct, nh)
        rec, hr, xc, *gates = rec_fwd(rest, small['conv_w'][l], row(small['conv_b'][l]), wga,
                                      row(small['b_gate_a'][l]), wgx, row(small['b_gate_x'][l]),
                                      row(small['lru_L'][l]), rw)
        gup, gdown, tok_w = hooks.mlp_weights(l, rec)
        h2, mix, z2 = out_proj(h, o, rec, row(small['attn_out_g'][l]), row(small['rec_out_g'][l]), wout,
                               row(small['mlp_norm_g'][l]) + tok_w, tm)
        u, h3 = mlp_fwd(z2, h2, gup, gdown, tm)
        saved.append(dict(h=h, z=z, qkv=qkv, rest=rest, c=c, ct=ct, o=o, lset=lset, rec=rec, hr=hr, xc=xc,
                          h2=h2, mix=mix, z2=z2, u=u, wga=wga, wgx=wgx, gates=gates,
                          w_in_t=w_in_t, wrest_t=wrest_t, wout=wout, gup=gup, gdown=gdown))
        h = h3

    dh, dgf, loss = loss_head(h, row(small['final_g']), tgt_p, t_real, tm)

    gs = {n: [None] * depth for n in SMALL if n not in ('final_g', 'meta')}
    tok = jnp.zeros((), F32)
    for l in reversed(range(depth)):
        sv = saved[l]
        gup, gdown = sv['gup'], sv['gdown']
        tf = gup.shape[2]
        dup, dh2, dg2, dhb = mlp_bwd(dh, sv['u'], sv['h2'], row(small['mlp_norm_g'][l]) + tok, gup, gdown, tm)
        gs['mlp_norm_g'][l] = dg2[0]
        do, drec, dga, dgr = out_proj_bwd(dh2, sv['o'], sv['rec'], row(small['attn_out_g'][l]),
                                          row(small['rec_out_g'][l]), sv['wout'], tm)
        gs['attn_out_g'][l] = dga[0]
        gs['rec_out_g'][l] = dgr[0]
        dw_down, dw_up = dw_mlp(sv['u'], dhb, sv['z2'], dup, tf)
        blocks = dict(
            w_down=dw_down, w_up=dw_up,
            w_out=mm_tn(sv['mix'], dh2, tn=d // 2, name="dw_out").reshape(N_DEV, d // N_DEV, d))
        tok = hooks.grads_ready(l, 'mlp', blocks)
        dxr, dyr, dwga, dwgx, vec = rec_bwd(drec, sv['hr'], sv['xc'], sv['gates'], sv['rest'], small['conv_w'][l],
                                            sv['wga'], sv['wgx'], row(small['lru_L'][l]) + tok, rw)
        gs['w_gate_a'][l] = _block_diag_extract(dwga, blk)
        gs['w_gate_x'][l] = _block_diag_extract(dwgx, blk)
        vec = vec.transpose(1, 0, 2).reshape(SUBLANES, rw)
        gs['conv_w'][l] = vec[0:CONV_WIDTH]
        gs['conv_b'][l] = vec[4]
        gs['b_gate_a'][l] = vec[5]
        gs['b_gate_x'][l] = vec[6]
        gs['lru_L'][l] = vec[7]
        dq, dk, dv, drow, dcol = attn_bwd(sv['qkv'], do, sv['o'], sv['lset'], sv['c'], sv['ct'] + tok, nh)
        drow8 = drow[:, 0:2, :].reshape(nh, tp)
        if nh < SUBLANES:
            drow8 = jnp.pad(drow8, ((0, SUBLANES - nh), (0, 0)))
        dfl, dbf = fgate_bwd(drow8, dcol, sv['rest'], bf_pad[l:l + 1], fcol)
        gs['b_f'][l] = dbf[0, 0:nh]
        parts = (dq, dk, dv, dxr, dyr, dfl)
        dh, dg1 = in_proj_bwd(dh2, parts, sv['w_in_t'], sv['wrest_t'], sv['h'], row(small['attn_norm_g'][l]), tm)
        gs['attn_norm_g'][l] = dg1[0]
        first = ()
        if l == 0:
            grads = {n: jnp.stack(v) for n, v in gs.items()}
            grads['final_g'] = dgf[0]
            grads['meta'] = dh[0:N_META]
            first = hooks.small_ready(grads)
        dw_in = dw_in_t(sv['z'], parts, nh, tm2)
        dw_in = dw_in.reshape(N_DEV, dw_in.shape[0] // N_DEV, d)
        tok = hooks.grads_ready(l, 'in', dict(w_in=dw_in), first)

    return loss[0, 0], dh


def prep_weights(g_in, g_out, nh, rw):
    d = g_in.shape[2]
    w_in_t = g_in.reshape(-1, d)
    f0 = 3 * nh * HEAD_DIM
    wrest_t = jnp.concatenate([w_in_t[f0 + nh:f0 + nh + 2 * rw],
                               jnp.pad(w_in_t[f0:f0 + nh], ((0, LANES - nh), (0, 0)))], axis=0)
    return w_in_t, wrest_t, g_out.reshape(d, d)


BIG = ['w_in', 'w_out', 'w_up', 'w_down']
EXCHANGE_GROUPS = {'mlp': ['w_down', 'w_up', 'w_out'], 'in': ['w_in']}
WEIGHTS = ['meta', 'attn_norm_g', 'w_in', 'b_f', 'conv_w', 'conv_b', 'w_gate_a', 'b_gate_a', 'w_gate_x', 'b_gate_x',
           'lru_L', 'attn_out_g', 'rec_out_g', 'w_out', 'mlp_norm_g', 'w_up', 'w_down', 'final_g']


def _set_own(arr, own, me):
    return lax.dynamic_update_slice_in_dim(arr, own[None], me, axis=0)


class _Step:
    def __init__(self, w, nh, rw, me):
        self.w, self.nh, self.rw, self.me = w, nh, rw, me
        depth = w['w_in'].shape[0]
        first = [w['w_in_t'][:, 0, :].astype(BF16), w['w_out'][0].astype(BF16), w['meta'], w['conv_w']]
        self.pending, token = gather_start([first], "gather_start_0")
        zero = token[0, 0].astype(BF16)
        groups = [[w['w_up'][0].astype(BF16) + zero, w['w_down'][0].astype(BF16) + zero]]
        for l in range(1, depth):
            groups.append([w['w_in_t'][:, l, :].astype(BF16) + zero, w['w_out'][l].astype(BF16) + zero])
            groups.append([w['w_up'][l].astype(BF16) + zero, w['w_down'][l].astype(BF16) + zero])
        rest, _ = gather_start(groups, "gather_start_1")
        self.pending += rest
        self.first_after = rest[0][2][0]
        self.gathered = {}
        self.passing = {}
        self.token = jnp.zeros((), F32)
        self.lands = {n: lax.empty((N_DEV,) + w[n].shape, BF16) for n in BIG}
        din8, _, d = w['w_in_t'].shape
        self.lands['w_in'] = lax.empty((N_DEV, depth, din8, d), BF16)
        self.started = []
        self.small = None

    def _pass_on(self, gi, after):
        if gi < len(self.pending) and gi not in self.passing:
            send, recv, srcs, lands = self.pending[gi]
            srcs, lands = gather_wait(send, recv, srcs, lands, after, "gather_wait_%d" % gi)
            fsend, frecv, lands, token = forward_start(lands, "forward_start_%d" % gi)
            self.passing[gi] = (fsend, frecv, srcs, lands)
            self.token = token

    def group(self, gi, after):
        if gi not in self.gathered:
            self._pass_on(gi, after)
            fsend, frecv, srcs, lands = self.passing[gi]
            lands = forward_wait(fsend, frecv, lands, after, "forward_wait_%d" % gi)
            self.gathered[gi] = [_set_own(g, own, self.me) for g, own in zip(lands, srcs)]
            if gi >= 2:
                self._pass_on(gi + 1, lands[0])
        return self.gathered[gi]

    def mixer_weights(self, l, after):
        g = self.group(2 * l, after)
        return (*prep_weights(g[0], g[1], self.nh, self.rw), self.token)

    def mlp_weights(self, l, after):
        g = self.group(2 * l + 1, after)
        return g[0], g[1], self.token

    def grads_ready(self, l, group, blocks, after=()):
        names = EXCHANGE_GROUPS[group]
        send, recv, srcs, lands, token = exchange_start(
            [blocks[n] for n in names], [self.lands[n] for n in names], l, "exchange_start_%s_%d" % (group, l),
            after)
        for n, a in zip(names, lands):
            self.lands[n] = a
        self.started.append((l, group, send, recv, srcs))
        return token

    def small_ready(self, grads):
        self.small_shapes = {n: grads[n].shape for n in SMALL}
        packed = _pack(grads).astype(BF16)
        send, recv, srcs, lands, token = exchange_start(
            [packed], [lax.empty((N_DEV,) + packed.shape, BF16)], None, "small_start")
        self.small = (send, recv, srcs, lands)
        return srcs[0]

    def small_sum(self, after):
        send, recv, srcs, lands = self.small
        srcs, lands = exchange_wait(send, recv, srcs, lands, after, None, "small_wait")
        parts = _set_own(lands[0], srcs[0], self.me)
        return _unpack(sum_parts(parts, "sum_small_grads"), self.small_shapes)

    def received(self, group, after):
        names = EXCHANGE_GROUPS[group]
        own = {n: [None] * self.w[n].shape[0] for n in names}
        for l, grp, send, recv, srcs in self.started:
            if grp != group:
                continue
            srcs, lands = exchange_wait(send, recv, srcs, [self.lands[n] for n in names], after, l,
                                        "exchange_wait_%s_%d" % (group, l))
            for n, a, sr in zip(names, lands, srcs):
                self.lands[n] = a
                own[n][l] = sr
        return {n: (self.lands[n], own[n]) for n in names}


def kernel(x, meta, attn_norm_g, w_in, b_f, conv_w, conv_b, w_gate_a, b_gate_a, w_gate_x, b_gate_x, lru_L, attn_out_g, rec_out_g, w_out, mlp_norm_g, w_up, w_down, final_g, loss_target, m_meta, m_attn_norm_g, m_w_in, m_b_f, m_conv_w, m_conv_b, m_w_gate_a, m_b_gate_a, m_w_gate_x, m_b_gate_x, m_lru_L, m_attn_out_g, m_rec_out_g, m_w_out, m_mlp_norm_g, m_w_up, m_w_down, m_final_g, v_meta, v_attn_norm_g, v_w_in, v_b_f, v_conv_w, v_conv_b, v_w_gate_a, v_b_gate_a, v_w_gate_x, v_b_gate_x, v_lru_L, v_attn_out_g, v_rec_out_g, v_w_out, v_mlp_norm_g, v_w_up, v_w_down, v_final_g):
    w = dict(meta=meta, attn_norm_g=attn_norm_g, w_in=w_in, b_f=b_f, conv_w=conv_w, conv_b=conv_b,
             w_gate_a=w_gate_a, b_gate_a=b_gate_a, w_gate_x=w_gate_x, b_gate_x=b_gate_x, lru_L=lru_L,
             attn_out_g=attn_out_g, rec_out_g=rec_out_g, w_out=w_out, mlp_norm_g=mlp_norm_g, w_up=w_up,
             w_down=w_down, final_g=final_g)
    mo = dict(meta=m_meta, attn_norm_g=m_attn_norm_g, w_in=m_w_in, b_f=m_b_f, conv_w=m_conv_w, conv_b=m_conv_b,
              w_gate_a=m_w_gate_a, b_gate_a=m_b_gate_a, w_gate_x=m_w_gate_x, b_gate_x=m_b_gate_x, lru_L=m_lru_L,
              attn_out_g=m_attn_out_g, rec_out_g=m_rec_out_g, w_out=m_w_out, mlp_norm_g=m_mlp_norm_g,
              w_up=m_w_up, w_down=m_w_down, final_g=m_final_g)
    vo = dict(meta=v_meta, attn_norm_g=v_attn_norm_g, w_in=v_w_in, b_f=v_b_f, conv_w=v_conv_w, conv_b=v_conv_b,
              w_gate_a=v_w_gate_a, b_gate_a=v_b_gate_a, w_gate_x=v_w_gate_x, b_gate_x=v_b_gate_x, lru_L=v_lru_L,
              attn_out_g=v_attn_out_g, rec_out_g=v_rec_out_g, w_out=v_w_out, mlp_norm_g=v_mlp_norm_g,
              w_up=v_w_up, w_down=v_w_down, final_g=v_final_g)
    depth = w_in.shape[0]
    nh = b_f.shape[1]
    rw = conv_b.shape[1]
    me = 4 * lax.axis_index("x") + 2 * lax.axis_index("y") + lax.axis_index("c")

    w['w_in_t'] = jnp.transpose(w_in, (2, 0, 1))
    swap = lambda a: jnp.swapaxes(a, 1, 2)
    step = _Step(w, nh, rw, me)
    g0 = step.group(0, step.first_after)
    meta_full = g0[2].transpose(1, 0, 2).reshape(N_META, -1)
    conv_full = g0[3].transpose(1, 2, 0, 3).reshape(depth, CONV_WIDTH, rw)
    small = {n: w[n] for n in SMALL}
    small['conv_w'] = conv_full

    loss_part, dh0 = local_step(x[0], loss_target[0], meta_full, small, step)
    loss = lax.psum(loss_part, ("x", "y", "c"))
    grad_x = dh0[N_META:N_META + x.shape[1]][None]

    out_g, out_d, out_m, out_v = {}, {}, {}, {}
    me1 = me.reshape(1).astype(jnp.int32)

    def update_big(group, after):
        for n, (r, owns) in step.received(group, after).items():
            if n == 'w_in':
                out = sum_adamw_t(r, owns, me1, swap(w[n]), swap(mo[n]), swap(vo[n]), "adamw_w_in")
                out = [swap(a) for a in out]
            else:
                shp = w[n].shape
                rows, cols = shp[0] * shp[1], shp[2]
                tr = min(512 if cols <= 512 else 256, shp[1])
                out = sum_adamw(r.reshape(N_DEV, rows, cols), owns, me1, w[n].reshape(rows, cols),
                                mo[n].reshape(rows, cols), vo[n].reshape(rows, cols), tr, "adamw_" + n)
                out = [a.reshape(shp) for a in out]
            out_g[n], out_d[n], out_m[n], out_v[n] = out
            after = out[0]
        return after

    update_big('mlp', step.started[-1][4][0])

    gsum = step.small_sum([out_g[n] for n in EXCHANGE_GROUPS['mlp']])
    gsum['meta'] = lax.dynamic_slice_in_dim(gsum['meta'], me * meta.shape[1], meta.shape[1], axis=1)
    gsum['conv_w'] = lax.dynamic_slice_in_dim(gsum['conv_w'], me * conv_w.shape[2], conv_w.shape[2], axis=2)
    as2d = lambda a: a.reshape(-1, a.shape[-1])
    deltas, new_m, new_v = adamw_group([as2d(gsum[n]) for n in SMALL], [as2d(w[n]) for n in SMALL],
                                       [as2d(mo[n]) for n in SMALL], [as2d(vo[n]) for n in SMALL], "adamw_small")
    for i, n in enumerate(SMALL):
        out_g[n] = gsum[n]
        out_d[n], out_m[n], out_v[n] = [a[i].reshape(w[n].shape) for a in (deltas, new_m, new_v)]

    update_big('in', deltas[0])

    return (loss, grad_x, *[out_g[n] for n in WEIGHTS], *[out_d[n] for n in WEIGHTS],
            *[out_m[n] for n in WEIGHTS], *[out_v[n] for n in WEIGHTS])
```

```python
import math

import jax
import jax.numpy as jnp
from jax import lax
from jax.experimental import pallas as pl
from jax.experimental.pallas import tpu as pltpu

F32 = jnp.float32
BF16 = jnp.bfloat16

N_DEV = 8
N_META = 16
HEAD_DIM = 64
CONV_WIDTH = 4
RG_C = 8.0
NORM_EPS = 1e-6
LANES = 128
SUBLANES = 8
ATT_BLOCK = 128
ATT_TQ = 512
NEG_BIG = -1e30
ATT_SCALE = 1.0 / math.sqrt(HEAD_DIM)

ADAM_LR = 0.001
ADAM_B1 = 0.9
ADAM_B2 = 0.999
ADAM_EPS = 1e-08
ADAM_WD = 0.01
ADAM_STEP = 10

VMEM_LIMIT_BYTES = 56 * 1024 * 1024
MESH = pl.DeviceIdType.MESH
ANY = pl.BlockSpec(memory_space=pl.ANY)


def _cparams(*sem):
    return pltpu.CompilerParams(dimension_semantics=sem if sem else None,
                                vmem_limit_bytes=VMEM_LIMIT_BYTES)


def _dot(a, b):
    return jnp.dot(a, b, preferred_element_type=F32)


def _dot_nt(a, b):
    return lax.dot_general(a, b, (((1,), (1,)), ((), ())), preferred_element_type=F32)


def _dot_tn(a, b):
    return lax.dot_general(a, b, (((0,), (0,)), ((), ())), preferred_element_type=F32)


def _sigmoid(x):
    return 0.5 * (1.0 + jnp.tanh(0.5 * x))


def _log_sigmoid(x):
    return jnp.minimum(x, 0.0) - jnp.log(1.0 + jnp.exp(-jnp.abs(x)))


def _expm1(x):
    series = x * (1.0 + x * (0.5 + x * (1.0 / 6.0 + x * (1.0 / 24.0))))
    return jnp.where(jnp.abs(x) < 1e-2, series, jnp.exp(x) - 1.0)


_GELU_K = math.sqrt(2.0 / math.pi)
_GELU_C = 0.044715


def _gelu(x):
    t = jnp.tanh(_GELU_K * (x + _GELU_C * x * x * x))
    return 0.5 * x * (1.0 + t)


def _gelu_and_grad(x):
    x2 = x * x
    t = jnp.tanh(_GELU_K * (x + _GELU_C * x2 * x))
    half = 0.5 * (1.0 + t)
    return x * half, half + 0.5 * x * (1.0 - t * t) * _GELU_K * (1.0 + 3.0 * _GELU_C * x2)


def _split3_dot(tri, x):
    hi = x.astype(BF16)
    r1 = x - hi.astype(F32)
    mid = r1.astype(BF16)
    lo = (r1 - mid.astype(F32)).astype(BF16)
    return _dot(tri, hi) + _dot(tri, mid) + _dot(tri, lo)


def _dot_split3(x, sel):
    hi = x.astype(BF16)
    r1 = x - hi.astype(F32)
    mid = r1.astype(BF16)
    lo = (r1 - mid.astype(F32)).astype(BF16)
    return _dot(hi, sel) + _dot(mid, sel) + _dot(lo, sel)


def _rms_fwd(x, g):
    r = lax.rsqrt(jnp.mean(x * x, axis=-1, keepdims=True) + NORM_EPS)
    return x * r * g


def _rms_bwd(x, g, dy):
    r = lax.rsqrt(jnp.mean(x * x, axis=-1, keepdims=True) + NORM_EPS)
    xn = x * r
    dxn = dy * g
    dx = r * (dxn - xn * jnp.mean(dxn * xn, axis=-1, keepdims=True))
    return dx, jnp.sum(dy * xn, axis=0, keepdims=True)


def _accumulate(ref, val, first):
    @pl.when(first)
    def _():
        ref[...] = val

    @pl.when(jnp.logical_not(first))
    def _():
        ref[...] += val


def in_proj(h, g1, w_in_t, wrest_t, nq, tm):
    tp, d = h.shape
    nr = wrest_t.shape[0]

    def body(h_ref, g_ref, wq_ref, wr_ref, z_ref, qkv_ref, rest_ref):
        z = _rms_fwd(h_ref[...], g_ref[...]).astype(BF16)
        z_ref[...] = z
        qkv_ref[...] = _dot_nt(z, wq_ref[...]).astype(BF16)
        rest_ref[...] = _dot_nt(z, wr_ref[...])

    return pl.pallas_call(
        body, name="in_proj", grid=(tp // tm,),
        in_specs=[pl.BlockSpec((tm, d), lambda i: (i, 0)),
                  pl.BlockSpec((1, d), lambda i: (0, 0)),
                  pl.BlockSpec((nq, d), lambda i: (0, 0)),
                  pl.BlockSpec((nr, d), lambda i: (0, 0))],
        out_specs=[pl.BlockSpec((tm, d), lambda i: (i, 0)),
                   pl.BlockSpec((tm, nq), lambda i: (i, 0)),
                   pl.BlockSpec((tm, nr), lambda i: (i, 0))],
        out_shape=[jax.ShapeDtypeStruct((tp, d), BF16),
                   jax.ShapeDtypeStruct((tp, nq), BF16),
                   jax.ShapeDtypeStruct((tp, nr), F32)],
        compiler_params=_cparams("parallel"),
    )(h, g1, w_in_t, wrest_t)


def fgate_fwd(rest, bf_pad, fcol):
    tp = rest.shape[0]
    nb = tp // ATT_BLOCK

    def body(f_ref, b_ref, c_ref, ct_ref):
        r_i = lax.broadcasted_iota(jnp.int32, (ATT_BLOCK, ATT_BLOCK), 0)
        c_i = lax.broadcasted_iota(jnp.int32, (ATT_BLOCK, ATT_BLOCK), 1)
        tri = (r_i >= c_i).astype(BF16)
        carry = jnp.zeros((1, LANES), F32)
        for i in range(nb):
            sl = slice(i * ATT_BLOCK, (i + 1) * ATT_BLOCK)
            lf = _log_sigmoid(f_ref[sl, :] + b_ref[...])
            cs = _split3_dot(tri, lf) + carry
            carry = cs[ATT_BLOCK - 1:ATT_BLOCK, :]
            c_ref[sl, :] = cs
            ct_ref[:, sl] = cs.T[0:SUBLANES, :]

    return pl.pallas_call(
        body, name="fgate_fwd", grid=(1,),
        in_specs=[pl.BlockSpec((tp, LANES), lambda i: (0, fcol)),
                  pl.BlockSpec((1, LANES), lambda i: (0, 0))],
        out_specs=[pl.BlockSpec((tp, LANES), lambda i: (0, 0)),
                   pl.BlockSpec((SUBLANES, tp), lambda i: (0, 0))],
        out_shape=[jax.ShapeDtypeStruct((tp, LANES), F32),
                   jax.ShapeDtypeStruct((SUBLANES, tp), F32)],
        compiler_params=_cparams("arbitrary"),
    )(rest, bf_pad)


def _pick_col(blk, head):
    lane = lax.broadcasted_iota(jnp.int32, blk.shape, 1)
    return jnp.sum(jnp.where(lane == head, blk, 0.0), axis=1, keepdims=True)


def _pick_row(blk, head):
    sub = lax.broadcasted_iota(jnp.int32, blk.shape, 0)
    return jnp.sum(jnp.where(sub == head, blk, 0.0), axis=0, keepdims=True)


def _att_tiles(tp):
    out, r0 = [], 0
    while r0 < tp:
        rows = min(ATT_TQ, tp - r0)
        out.append((r0, rows, r0 + rows))
        r0 += rows
    return out


def attn_fwd(qkv, c, ct, nh):
    tp = qkv.shape[0]
    npair = nh // 2
    tiles = _att_tiles(tp)

    def body(q_ref, k_ref, v_ref, c_ref, ct_ref, o_ref, lset_ref):
        p = pl.program_id(0)
        lset_ref[...] = jnp.zeros_like(lset_ref)
        for r0, nr, nk in tiles:
            rs = slice(r0, r0 + nr)
            causal = (r0 + lax.broadcasted_iota(jnp.int32, (nr, nk), 0)
                      >= lax.broadcasted_iota(jnp.int32, (nr, nk), 1))
            cblk = c_ref[rs, :]
            ctb = ct_ref[:, 0:nk]
            for hh in range(2):
                head = 2 * p + hh
                hs = slice(hh * HEAD_DIM, (hh + 1) * HEAD_DIM)
                q = q_ref[rs, hs] * ATT_SCALE
                s = _dot_nt(q, k_ref[0:nk, hs]) + (_pick_col(cblk, head) - _pick_row(ctb, head))
                s = jnp.where(causal, s, NEG_BIG)
                m = jnp.max(s, axis=1, keepdims=True)
                pm = jnp.exp(s - m)
                l = jnp.sum(pm, axis=1, keepdims=True)
                o_ref[rs, hs] = _dot(pm.astype(BF16), v_ref[0:nk, hs]) / l
                lse = m + jnp.log(l)
                lset_ref[hh:hh + 1, rs] = jnp.broadcast_to(lse, (nr, LANES)).T[0:1, :]

    pair = lambda p: (0, p)
    return pl.pallas_call(
        body, name="attn_fwd", grid=(npair,),
        in_specs=[pl.BlockSpec((tp, LANES), pair),
                  pl.BlockSpec((tp, LANES), lambda p: (0, npair + p)),
                  pl.BlockSpec((tp, LANES), lambda p: (0, 2 * npair + p)),
                  pl.BlockSpec((tp, LANES), lambda p: (0, 0)),
                  pl.BlockSpec((SUBLANES, tp), lambda p: (0, 0))],
        out_specs=[pl.BlockSpec((tp, LANES), pair),
                   pl.BlockSpec((None, SUBLANES, tp), lambda p: (p, 0, 0))],
        out_shape=[jax.ShapeDtypeStruct((tp, nh * HEAD_DIM), F32),
                   jax.ShapeDtypeStruct((npair, SUBLANES, tp), F32)],
        compiler_params=_cparams("parallel"),
    )(qkv, qkv, qkv, c, ct)


def _shift_down(x, k, n):
    if k == 0:
        return x
    rows = lax.broadcasted_iota(jnp.int32, x.shape, 0)
    return jnp.where(rows >= k, pltpu.roll(x, k, 0), 0.0)


def _shift_up(x, k, n):
    if k == 0:
        return x
    rows = lax.broadcasted_iota(jnp.int32, x.shape, 0)
    return jnp.where(rows < n - k, pltpu.roll(x, n - k, 0), 0.0)


def _conv_fwd(xr, cw_ref, cb_ref, n):
    xc = cw_ref[CONV_WIDTH - 1:CONV_WIDTH, :] * xr + cb_ref[...]
    for k in range(1, CONV_WIDTH):
        xc = xc + cw_ref[CONV_WIDTH - 1 - k:CONV_WIDTH - k, :] * _shift_down(xr, k, n)
    return xc


def _gates(xc, wga_ref, bga_ref, wgx_ref, bgx_ref, l_ref):
    xcb = xc.astype(BF16)
    r = _sigmoid(_dot(xcb, wga_ref[...]) + bga_ref[...])
    ig = _sigmoid(_dot(xcb, wgx_ref[...]) + bgx_ref[...])
    ls = _log_sigmoid(l_ref[...])
    log_a = RG_C * r * ls
    a = jnp.exp(log_a)
    mult = jnp.sqrt(-_expm1(2.0 * log_a))
    return xcb, r, ig, ls, log_a, a, mult


SCAN_UNROLL = 4


def _scan_rows(a_s, u_s, out_ref, n, reverse):
    nt = n // SUBLANES
    per = SCAN_UNROLL if nt % SCAN_UNROLL == 0 else 1
    row = lax.broadcasted_iota(jnp.int32, (SUBLANES, LANES), 0)
    last = 0 if reverse else SUBLANES - 1

    def tile_scan(a, u):
        for d in (1, 2, 4):
            if reverse:
                keep = row < SUBLANES - d
                sh = SUBLANES - d
            else:
                keep = row >= d
                sh = d
            a_sh = jnp.where(keep, pltpu.roll(a, sh, 0), 1.0)
            u_sh = jnp.where(keep, pltpu.roll(u, sh, 0), 0.0)
            u = a * u_sh + u
            a = a * a_sh
        return a, u

    def step(t, carry):
        tiles = []
        for k in range(per):
            tt = t * per + k
            if reverse:
                tt = nt - 1 - tt
            off = pl.multiple_of(tt * SUBLANES, SUBLANES)
            a, u = tile_scan(a_s[pl.ds(off, SUBLANES), :], u_s[pl.ds(off, SUBLANES), :])
            tiles.append((off, a, u))
        for off, a, u in tiles:
            out_ref[pl.ds(off, SUBLANES), :] = u + a * carry
            carry = u[last:last + 1, :] + a[last:last + 1, :] * carry
        return carry

    lax.fori_loop(0, nt // per, step, jnp.zeros((1, LANES), F32))


def rec_fwd(rest, convw, convb, wga, bga, wgx, bgx, lru, rw):
    tp = rest.shape[0]
    ng = rw // LANES

    def body(xr_ref, yr_ref, cw_ref, cb_ref, wga_ref, bga_ref, wgx_ref, bgx_ref, l_ref,
             rec_ref, hr_ref, xc_ref, r_ref, ig_ref, a_ref, mult_ref, u_s):
        xc = _conv_fwd(xr_ref[...], cw_ref, cb_ref, tp)
        xc_ref[...] = xc
        _, r, ig, ls, log_a, a, mult = _gates(xc, wga_ref, bga_ref, wgx_ref, bgx_ref, l_ref)
        r_ref[...] = r
        ig_ref[...] = ig
        a_ref[...] = a
        mult_ref[...] = mult
        u_s[...] = mult * ig * xc
        _scan_rows(a_ref, u_s, hr_ref, tp, reverse=False)
        rec_ref[...] = hr_ref[...] * _gelu(yr_ref[...])

    col = lambda g: (0, g)
    vec = pl.BlockSpec((1, LANES), col)
    big = pl.BlockSpec((tp, LANES), col)
    return pl.pallas_call(
        body, name="rec_fwd", grid=(ng,),
        in_specs=[big, pl.BlockSpec((tp, LANES), lambda g: (0, ng + g)),
                  pl.BlockSpec((CONV_WIDTH, LANES), col), vec,
                  pl.BlockSpec((None, LANES, LANES), lambda g: (g, 0, 0)), vec,
                  pl.BlockSpec((None, LANES, LANES), lambda g: (g, 0, 0)), vec, vec],
        out_specs=[big] * 7,
        out_shape=[jax.ShapeDtypeStruct((tp, rw), F32)] * 7,
        scratch_shapes=[pltpu.VMEM((tp, LANES), F32)],
        compiler_params=_cparams("parallel"),
    )(rest, rest, convw, convb, wga, bga, wgx, bgx, lru)


def out_proj(h, o, rec, ga, gr, wout, g2, tm):
    tp, d = h.shape
    aw, rw = o.shape[1], rec.shape[1]

    def body(h_ref, o_ref, rec_ref, ga_ref, gr_ref, w_ref, g2_ref, h2_ref, mix_ref, z2_ref):
        mix_ref[:, 0:aw] = _rms_fwd(o_ref[...], ga_ref[...]).astype(BF16)
        mix_ref[:, aw:aw + rw] = _rms_fwd(rec_ref[...], gr_ref[...]).astype(BF16)
        h2 = h_ref[...] + _dot(mix_ref[...], w_ref[...])
        h2_ref[...] = h2
        z2_ref[...] = _rms_fwd(h2, g2_ref[...]).astype(BF16)

    row = lambda i: (i, 0)
    fix = lambda i: (0, 0)
    return pl.pallas_call(
        body, name="out_proj", grid=(tp // tm,),
        in_specs=[pl.BlockSpec((tm, d), row), pl.BlockSpec((tm, aw), row), pl.BlockSpec((tm, rw), row),
                  pl.BlockSpec((1, aw), fix), pl.BlockSpec((1, rw), fix),
                  pl.BlockSpec((d, d), fix), pl.BlockSpec((1, d), fix)],
        out_specs=[pl.BlockSpec((tm, d), row)] * 3,
        out_shape=[jax.ShapeDtypeStruct((tp, d), F32), jax.ShapeDtypeStruct((tp, d), BF16),
                   jax.ShapeDtypeStruct((tp, d), BF16)],
        compiler_params=_cparams("parallel"),
    )(h, o, rec, ga, gr, wout, g2)


MLP_BLOCKS = 4


def mlp_fwd(z2, h2, gup, gdown, tm):
    tp, d = h2.shape
    nf = gup.shape[0]
    tf = gup.shape[2]
    nb = MLP_BLOCKS if nf % MLP_BLOCKS == 0 else 1
    nj = nf // nb

    def body(z_ref, h_ref, wu_ref, wd_ref, u_ref, h3_ref, acc):
        j = pl.program_id(1)
        z = z_ref[...]
        part = None
        for b in range(nb):
            u = jnp.maximum(_dot(z, wu_ref[b]), 0.0)
            u_ref[:, b * tf:(b + 1) * tf] = u.astype(BF16)
            p = _dot((u * u).astype(BF16), wd_ref[b])
            part = p if part is None else part + p

        @pl.when(j == 0)
        def _():
            acc[...] = h_ref[...] + part

        @pl.when(j > 0)
        def _():
            acc[...] += part

        @pl.when(j == nj - 1)
        def _():
            h3_ref[...] = acc[...]

    return pl.pallas_call(
        body, name="mlp_fwd", grid=(tp // tm, nj),
        in_specs=[pl.BlockSpec((tm, d), lambda i, j: (i, 0)),
                  pl.BlockSpec((tm, d), lambda i, j: (i, 0)),
                  pl.BlockSpec((nb, d, tf), lambda i, j: (j, 0, 0)),
                  pl.BlockSpec((nb, tf, d), lambda i, j: (j, 0, 0))],
        out_specs=[pl.BlockSpec((tm, nb * tf), lambda i, j: (i, j)),
                   pl.BlockSpec((tm, d), lambda i, j: (i, 0))],
        out_shape=[jax.ShapeDtypeStruct((tp, nf * tf), BF16), jax.ShapeDtypeStruct((tp, d), F32)],
        scratch_shapes=[pltpu.VMEM((tm, d), F32)],
        compiler_params=_cparams("parallel", "arbitrary"),
    )(z2, h2, gup, gdown)


def loss_head(h, gf, tgt, t_real, tm):
    tp, d = h.shape

    def body(h_ref, g_ref, t_ref, dh_ref, dg_ref, loss_ref):
        i = pl.program_id(0)
        x = h_ref[...]
        g = g_ref[...]
        r = lax.rsqrt(jnp.mean(x * x, axis=-1, keepdims=True) + NORM_EPS)
        xn = x * r
        rows = i * tm + lax.broadcasted_iota(jnp.int32, (tm, 1), 0)
        valid = jnp.logical_and(rows >= N_META, rows < t_real)
        e = jnp.where(valid, xn * g - t_ref[...], 0.0)
        part = 0.5 * jnp.sum(jnp.sum(e * e, axis=1, keepdims=True) / d, axis=0, keepdims=True)
        dy = e / d
        dxn = dy * g
        dh_ref[...] = r * (dxn - xn * jnp.mean(dxn * xn, axis=-1, keepdims=True))
        _accumulate(dg_ref, jnp.sum(dy * xn, axis=0, keepdims=True), i == 0)
        _accumulate(loss_ref, jnp.broadcast_to(part, (1, LANES)), i == 0)

    row = lambda i: (i, 0)
    fix = lambda i: (0, 0)
    return pl.pallas_call(
        body, name="loss_head", grid=(tp // tm,),
        in_specs=[pl.BlockSpec((tm, d), row), pl.BlockSpec((1, d), fix), pl.BlockSpec((tm, d), row)],
        out_specs=[pl.BlockSpec((tm, d), row), pl.BlockSpec((1, d), fix), pl.BlockSpec((1, LANES), fix)],
        out_shape=[jax.ShapeDtypeStruct((tp, d), F32), jax.ShapeDtypeStruct((1, d), F32),
                   jax.ShapeDtypeStruct((1, LANES), F32)],
        compiler_params=_cparams("arbitrary"),
    )(h, gf, tgt)


def mlp_bwd(dh, u, h2, g2, gup, gdown, tm):
    tp, d = dh.shape
    nf = gup.shape[0]
    tf = gup.shape[2]
    nb = MLP_BLOCKS if nf % MLP_BLOCKS == 0 else 1
    nj = nf // nb
    ni = tp // tm

    def body(dh_ref, u_ref, h2_ref, g_ref, wu_ref, wd_ref, dup_ref, dh2_ref, dg_ref, dhb, acc):
        i = pl.program_id(0)
        j = pl.program_id(1)

        @pl.when(j == 0)
        def _():
            dhb[...] = dh_ref[...].astype(BF16)

        part = None
        for b in range(nb):
            cols = slice(b * tf, (b + 1) * tf)
            dup = (_dot_nt(dhb[...], wd_ref[b]) * (2.0 * u_ref[:, cols].astype(F32))).astype(BF16)
            dup_ref[:, cols] = dup
            p = _dot_nt(dup, wu_ref[b])
            part = p if part is None else part + p
        _accumulate(acc, part, j == 0)

        @pl.when(j == nj - 1)
        def _():
            dx, dg = _rms_bwd(h2_ref[...], g_ref[...], acc[...])
            dh2_ref[...] = dh_ref[...] + dx
            _accumulate(dg_ref, dg, i == 0)

    return pl.pallas_call(
        body, name="mlp_bwd", grid=(ni, nj),
        in_specs=[pl.BlockSpec((tm, d), lambda i, j: (i, 0)),
                  pl.BlockSpec((tm, nb * tf), lambda i, j: (i, j)),
                  pl.BlockSpec((tm, d), lambda i, j: (i, 0)),
                  pl.BlockSpec((1, d), lambda i, j: (0, 0)),
                  pl.BlockSpec((nb, d, tf), lambda i, j: (j, 0, 0)),
                  pl.BlockSpec((nb, tf, d), lambda i, j: (j, 0, 0))],
        out_specs=[pl.BlockSpec((tm, nb * tf), lambda i, j: (i, j)),
                   pl.BlockSpec((tm, d), lambda i, j: (i, 0)),
                   pl.BlockSpec((1, d), lambda i, j: (0, 0)),
                   pl.BlockSpec((tm, d), lambda i, j: (i, 0))],
        out_shape=[jax.ShapeDtypeStruct((tp, nf * tf), BF16), jax.ShapeDtypeStruct((tp, d), F32),
                   jax.ShapeDtypeStruct((1, d), F32), jax.ShapeDtypeStruct((tp, d), BF16)],
        scratch_shapes=[pltpu.VMEM((tm, d), F32)],
        compiler_params=_cparams("arbitrary", "arbitrary"),
    )(dh, u, h2, g2, gup, gdown)


def mm_tn(a, b, *, tn, name):
    rows, kk = a.shape
    nn = b.shape[1]

    def body(a_ref, b_ref, o_ref, at):
        @pl.when(pl.program_id(0) == 0)
        def _():
            at[...] = a_ref[...].T

        o_ref[...] = _dot(at[...], b_ref[...].astype(BF16)).astype(BF16)

    return pl.pallas_call(
        body, name=name, grid=(nn // tn,),
        in_specs=[pl.BlockSpec((rows, kk), lambda n: (0, 0)),
                  pl.BlockSpec((rows, tn), lambda n: (0, n))],
        out_specs=pl.BlockSpec((kk, tn), lambda n: (0, n)),
        out_shape=jax.ShapeDtypeStruct((kk, nn), BF16),
        scratch_shapes=[pltpu.VMEM((kk, rows), BF16)],
        compiler_params=_cparams("arbitrary"),
    )(a, b)


def dw_mlp(u, dhb, z2, dup, tf):
    rows, dff = u.shape
    d = z2.shape[1]
    nf = dff // tf

    def body(u_ref, dh_ref, z_ref, dup_ref, dwd_ref, dwu_ref, zt):
        @pl.when(pl.program_id(0) == 0)
        def _():
            zt[...] = z_ref[...].T

        uf = u_ref[...].astype(F32)
        dwd_ref[...] = _dot_tn((uf * uf).astype(BF16), dh_ref[...]).astype(BF16)
        dwu_ref[...] = _dot(zt[...], dup_ref[...]).astype(BF16)

    col = lambda j: (0, j)
    fix = lambda j: (0, 0)
    return pl.pallas_call(
        body, name="dw_mlp", grid=(nf,),
        in_specs=[pl.BlockSpec((rows, tf), col), pl.BlockSpec((rows, d), fix),
                  pl.BlockSpec((rows, d), fix), pl.BlockSpec((rows, tf), col)],
        out_specs=[pl.BlockSpec((None, tf, d), lambda j: (j, 0, 0)),
                   pl.BlockSpec((None, d, tf), lambda j: (j, 0, 0))],
        out_shape=[jax.ShapeDtypeStruct((nf, tf, d), BF16), jax.ShapeDtypeStruct((nf, d, tf), BF16)],
        scratch_shapes=[pltpu.VMEM((d, rows), BF16)],
        compiler_params=_cparams("arbitrary"),
    )(u, dhb, z2, dup)


def out_proj_bwd(dh2, o, rec, ga, gr, wout, tm):
    tp, d = dh2.shape
    aw, rw = o.shape[1], rec.shape[1]

    def body(dh_ref, o_ref, rec_ref, ga_ref, gr_ref, w_ref, do_ref, drec_ref, dga_ref, dgr_ref):
        i = pl.program_id(0)
        dmix = _dot_nt(dh_ref[...].astype(BF16), w_ref[...])
        do, dga = _rms_bwd(o_ref[...], ga_ref[...], dmix[:, 0:aw])
        drec, dgr = _rms_bwd(rec_ref[...], gr_ref[...], dmix[:, aw:aw + rw])
        do_ref[...] = do
        drec_ref[...] = drec
        _accumulate(dga_ref, dga, i == 0)
        _accumulate(dgr_ref, dgr, i == 0)

    row = lambda i: (i, 0)
    fix = lambda i: (0, 0)
    return pl.pallas_call(
        body, name="out_proj_bwd", grid=(tp // tm,),
        in_specs=[pl.BlockSpec((tm, d), row), pl.BlockSpec((tm, aw), row), pl.BlockSpec((tm, rw), row),
                  pl.BlockSpec((1, aw), fix), pl.BlockSpec((1, rw), fix), pl.BlockSpec((d, d), fix)],
        out_specs=[pl.BlockSpec((tm, aw), row), pl.BlockSpec((tm, rw), row),
                   pl.BlockSpec((1, aw), fix), pl.BlockSpec((1, rw), fix)],
        out_shape=[jax.ShapeDtypeStruct((tp, aw), F32), jax.ShapeDtypeStruct((tp, rw), F32),
                   jax.ShapeDtypeStruct((1, aw), F32), jax.ShapeDtypeStruct((1, rw), F32)],
        compiler_params=_cparams("arbitrary"),
    )(dh2, o, rec, ga, gr, wout)


def rec_bwd(drec, hr, xc, gates, rest, convw, wga, wgx, lru, rw):
    tp = rest.shape[0]
    ng = rw // LANES

    def body(drec_ref, hr_ref, xc_ref, r_ref, ig_ref, a_ref, mult_ref, xr_ref, yr_ref, cw_ref, wga_ref, wgx_ref,
             l_ref, dxr_ref, dyr_ref, dwga_ref, dwgx_ref, vec_ref, a_s, u_s, lam_s):
        xc = xc_ref[...]
        h = hr_ref[...]
        drec = drec_ref[...]
        r, ig, a, mult = r_ref[...], ig_ref[...], a_ref[...], mult_ref[...]
        xcb = xc.astype(BF16)
        ls = _log_sigmoid(l_ref[...])
        gelu, gelu_grad = _gelu_and_grad(yr_ref[...])
        dyr_ref[...] = (drec * h * gelu_grad).astype(BF16)
        a_s[...] = _shift_up(a, 1, tp)
        u_s[...] = drec * gelu
        _scan_rows(a_s, u_s, lam_s, tp, reverse=True)
        lam = lam_s[...]
        da = lam * _shift_down(h, 1, tp)
        dmult = lam * ig * xc
        dig = lam * mult * xc
        dxc = lam * mult * ig
        dlog_a = da * a - dmult * (a * a) / mult
        dr = dlog_a * (RG_C * ls)
        dl = jnp.sum(dlog_a * (RG_C * r), axis=0, keepdims=True) * _sigmoid(-l_ref[...])
        dpa = dr * r * (1.0 - r)
        dpx = dig * ig * (1.0 - ig)
        dpab = dpa.astype(BF16)
        dpxb = dpx.astype(BF16)
        dxc = dxc + _dot_nt(dpab, wga_ref[...]) + _dot_nt(dpxb, wgx_ref[...])
        dwga_ref[...] = _dot_tn(xcb, dpab)
        dwgx_ref[...] = _dot_tn(xcb, dpxb)
        xr = xr_ref[...]
        dxr = cw_ref[CONV_WIDTH - 1:CONV_WIDTH, :] * dxc
        for k in range(1, CONV_WIDTH):
            dxr = dxr + cw_ref[CONV_WIDTH - 1 - k:CONV_WIDTH - k, :] * _shift_up(dxc, k, tp)
        dxr_ref[...] = dxr.astype(BF16)
        for k in range(CONV_WIDTH):
            vec_ref[k:k + 1, :] = jnp.sum(dxc * _shift_down(xr, CONV_WIDTH - 1 - k, tp), axis=0, keepdims=True)
        vec_ref[4:5, :] = jnp.sum(dxc, axis=0, keepdims=True)
        vec_ref[5:6, :] = jnp.sum(dpa, axis=0, keepdims=True)
        vec_ref[6:7, :] = jnp.sum(dpx, axis=0, keepdims=True)
        vec_ref[7:8, :] = dl

    col = lambda g: (0, g)
    vec = pl.BlockSpec((1, LANES), col)
    big = pl.BlockSpec((tp, LANES), col)
    sq = pl.BlockSpec((None, LANES, LANES), lambda g: (g, 0, 0))
    return pl.pallas_call(
        body, name="rec_bwd", grid=(ng,),
        in_specs=[big] * 8 + [pl.BlockSpec((tp, LANES), lambda g: (0, ng + g)),
                                pl.BlockSpec((CONV_WIDTH, LANES), col), sq, sq, vec],
        out_specs=[big, big, sq, sq, pl.BlockSpec((None, SUBLANES, LANES), lambda g: (g, 0, 0))],
        out_shape=[jax.ShapeDtypeStruct((tp, rw), BF16), jax.ShapeDtypeStruct((tp, rw), BF16),
                   jax.ShapeDtypeStruct((ng, LANES, LANES), F32), jax.ShapeDtypeStruct((ng, LANES, LANES), F32),
                   jax.ShapeDtypeStruct((ng, SUBLANES, LANES), F32)],
        scratch_shapes=[pltpu.VMEM((tp, LANES), F32)] * 3,
        compiler_params=_cparams("parallel"),
    )(drec, hr, xc, *gates, rest, rest, convw, wga, wgx, lru)


def attn_bwd(qkv, do, o, lset, c, ct, nh):
    tp = qkv.shape[0]
    npair = nh // 2
    aw = nh * HEAD_DIM
    tiles = _att_tiles(tp)

    def body(q_ref, k_ref, v_ref, do_ref, o_ref, lset_ref, c_ref, ct_ref,
             dq_ref, dk_ref, dv_ref, drow_ref, dcol_ref, dk_acc, dv_acc, dq_t):
        p = pl.program_id(0)
        k_t = k_ref[...].T
        dk_acc[...] = jnp.zeros_like(dk_acc)
        dv_acc[...] = jnp.zeros_like(dv_acc)
        dcol_ref[...] = jnp.zeros_like(dcol_ref)
        drow_ref[...] = jnp.zeros_like(drow_ref)
        for r0, nr, nk in tiles:
            rs = slice(r0, r0 + nr)
            causal = (r0 + lax.broadcasted_iota(jnp.int32, (nk, nr), 1)
                      >= lax.broadcasted_iota(jnp.int32, (nk, nr), 0))
            cblk = c_ref[0:nk, :]
            ctb = ct_ref[:, rs]
            for hh in range(2):
                head = 2 * p + hh
                hs = slice(hh * HEAD_DIM, (hh + 1) * HEAD_DIM)
                q = q_ref[rs, hs]
                k = k_ref[0:nk, hs]
                dof = do_ref[rs, hs]
                do16 = dof.astype(BF16)
                delta = jnp.sum(dof * o_ref[rs, hs], axis=1, keepdims=True)
                delta_row = jnp.broadcast_to(delta, (nr, LANES)).T[0:1, :]
                s_t = _dot_nt(k, q * ATT_SCALE) + (_pick_row(ctb, head) - _pick_col(cblk, head))
                p_t = jnp.where(causal, jnp.exp(s_t - lset_ref[hh:hh + 1, rs]), 0.0)
                ds_t = p_t * (_dot_nt(v_ref[0:nk, hs], do16) - delta_row)
                p16 = p_t.astype(BF16)
                ds16 = ds_t.astype(BF16)
                dv_acc[0:nk, hs] += _dot(p16, do16)
                dk_acc[0:nk, hs] += _dot(ds16, q) * ATT_SCALE
                dq_t[hs, rs] = _dot(k_t[hs, 0:nk], ds16)
                drow_ref[hh:hh + 1, rs] = jnp.sum(ds_t, axis=0, keepdims=True)
                dcol_ref[0:nk, hs] -= jnp.broadcast_to(jnp.sum(ds_t, axis=1, keepdims=True), (nk, HEAD_DIM))
        dk_ref[...] = dk_acc[...].astype(BF16)
        dv_ref[...] = dv_acc[...].astype(BF16)
        dq_ref[...] = (dq_t[...].T * ATT_SCALE).astype(BF16)

    pair = lambda p: (0, p)
    return pl.pallas_call(
        body, name="attn_bwd", grid=(npair,),
        in_specs=[pl.BlockSpec((tp, LANES), pair),
                  pl.BlockSpec((tp, LANES), lambda p: (0, npair + p)),
                  pl.BlockSpec((tp, LANES), lambda p: (0, 2 * npair + p)),
                  pl.BlockSpec((tp, LANES), pair),
                  pl.BlockSpec((tp, LANES), pair),
                  pl.BlockSpec((None, SUBLANES, tp), lambda p: (p, 0, 0)),
                  pl.BlockSpec((tp, LANES), lambda p: (0, 0)),
                  pl.BlockSpec((SUBLANES, tp), lambda p: (0, 0))],
        out_specs=[pl.BlockSpec((tp, LANES), pair), pl.BlockSpec((tp, LANES), pair),
                   pl.BlockSpec((tp, LANES), pair),
                   pl.BlockSpec((None, SUBLANES, tp), lambda p: (p, 0, 0)),
                   pl.BlockSpec((tp, LANES), pair)],
        out_shape=[jax.ShapeDtypeStruct((tp, aw), BF16), jax.ShapeDtypeStruct((tp, aw), BF16),
                   jax.ShapeDtypeStruct((tp, aw), BF16),
                   jax.ShapeDtypeStruct((npair, SUBLANES, tp), F32),
                   jax.ShapeDtypeStruct((tp, aw), F32)],
        scratch_shapes=[pltpu.VMEM((tp, LANES), F32), pltpu.VMEM((tp, LANES), F32),
                        pltpu.VMEM((LANES, tp), F32)],
        compiler_params=_cparams("parallel"),
    )(qkv, qkv, qkv, do, o, lset, c, ct)


def fgate_bwd(dct8, drs, rest, bf_pad, fcol):
    tp = rest.shape[0]
    aw = drs.shape[1]
    nb = tp // ATT_BLOCK
    B = ATT_BLOCK

    def body(d_ref, drs_ref, f_ref, b_ref, dfl_ref, db_ref, pad_s):
        r_i = lax.broadcasted_iota(jnp.int32, (B, B), 0)
        c_i = lax.broadcasted_iota(jnp.int32, (B, B), 1)
        triu = (c_i >= r_i).astype(BF16)
        sel = (lax.broadcasted_iota(jnp.int32, (aw, LANES), 0)
               == HEAD_DIM * lax.broadcasted_iota(jnp.int32, (aw, LANES), 1)).astype(BF16)
        carry = jnp.zeros((1, LANES), F32)
        db = jnp.zeros((1, LANES), F32)
        pad_s[...] = jnp.zeros_like(pad_s)
        for i in range(nb - 1, -1, -1):
            sl = slice(i * B, (i + 1) * B)
            pad_s[0:SUBLANES, :] = d_ref[:, sl]
            dc = pad_s[...].T + _dot_split3(drs_ref[sl, :], sel)
            rc = _split3_dot(triu, dc)
            dlf = rc + carry
            carry = carry + rc[0:1, :]
            dfl = dlf * _sigmoid(-(f_ref[sl, :] + b_ref[...]))
            dfl_ref[sl, :] = dfl.astype(BF16)
            db = db + jnp.sum(dfl, axis=0, keepdims=True)
        db_ref[...] = db

    return pl.pallas_call(
        body, name="fgate_bwd", grid=(1,),
        in_specs=[pl.BlockSpec((SUBLANES, tp), lambda i: (0, 0)),
                  pl.BlockSpec((tp, aw), lambda i: (0, 0)),
                  pl.BlockSpec((tp, LANES), lambda i: (0, fcol)),
                  pl.BlockSpec((1, LANES), lambda i: (0, 0))],
        out_specs=[pl.BlockSpec((tp, LANES), lambda i: (0, 0)),
                   pl.BlockSpec((1, LANES), lambda i: (0, 0))],
        out_shape=[jax.ShapeDtypeStruct((tp, LANES), BF16), jax.ShapeDtypeStruct((1, LANES), F32)],
        scratch_shapes=[pltpu.VMEM((B, B), F32)],
        compiler_params=_cparams("arbitrary"),
    )(dct8, drs, rest, bf_pad)


def in_proj_bwd(dh2, parts, w_in_t, wrest_t, h, g1, tm):
    tp, d = h.shape
    dq, dk, dv, dxr, dyr, dfl = parts
    aw, rw = dq.shape[1], dxr.shape[1]

    def body(dh2_ref, dq_ref, dk_ref, dv_ref, dxr_ref, dyr_ref, dfl_ref, wq_ref, wr_ref, h_ref, g_ref,
             dh_ref, dg_ref):
        i = pl.program_id(0)
        dz = _dot(dq_ref[...], wq_ref[0:aw, :])
        dz += _dot(dk_ref[...], wq_ref[aw:2 * aw, :])
        dz += _dot(dv_ref[...], wq_ref[2 * aw:3 * aw, :])
        dz += _dot(dxr_ref[...], wr_ref[0:rw, :])
        dz += _dot(dyr_ref[...], wr_ref[rw:2 * rw, :])
        dz += _dot(dfl_ref[...], wr_ref[2 * rw:2 * rw + LANES, :])
        dx, dg = _rms_bwd(h_ref[...], g_ref[...], dz)
        dh_ref[...] = dh2_ref[...] + dx
        _accumulate(dg_ref, dg, i == 0)

    row = lambda i: (i, 0)
    fix = lambda i: (0, 0)
    return pl.pallas_call(
        body, name="in_proj_bwd", grid=(tp // tm,),
        in_specs=[pl.BlockSpec((tm, d), row),
                  pl.BlockSpec((tm, aw), row), pl.BlockSpec((tm, aw), row), pl.BlockSpec((tm, aw), row),
                  pl.BlockSpec((tm, rw), row), pl.BlockSpec((tm, rw), row), pl.BlockSpec((tm, LANES), row),
                  pl.BlockSpec((3 * aw, d), fix), pl.BlockSpec(wrest_t.shape, fix),
                  pl.BlockSpec((tm, d), row), pl.BlockSpec((1, d), fix)],
        out_specs=[pl.BlockSpec((tm, d), row), pl.BlockSpec((1, d), fix)],
        out_shape=[jax.ShapeDtypeStruct((tp, d), F32), jax.ShapeDtypeStruct((1, d), F32)],
        compiler_params=_cparams("arbitrary"),
    )(dh2, dq, dk, dv, dxr, dyr, dfl, w_in_t, wrest_t, h, g1)


def dw_in_t(z, parts, nh, tr):
    tp, d = z.shape
    dq, dk, dv, dxr, dyr, dfl = parts
    aw, rw = dq.shape[1], dxr.shape[1]
    d_in = 3 * aw + nh + 2 * rw
    nr = tp // tr
    offs = [(0, aw), (aw, aw), (2 * aw, aw), (3 * aw + nh, rw), (3 * aw + nh + rw, rw)]

    def body(z_ref, dq_ref, dk_ref, dv_ref, dxr_ref, dyr_ref, dfl_ref, o_ref, acc):
        r = pl.program_id(0)

        @pl.when(r == 0)
        def _():
            acc[...] = jnp.zeros_like(acc)

        zt = z_ref[...]
        for (o, n), ref in zip(offs, (dq_ref, dk_ref, dv_ref, dxr_ref, dyr_ref)):
            acc[o:o + n, :] += _dot_tn(ref[...], zt)
        acc[3 * aw:3 * aw + nh, :] += _dot_tn(dfl_ref[...], zt)[0:nh, :]

        @pl.when(r == nr - 1)
        def _():
            o_ref[...] = acc[...].astype(BF16)

    row = lambda r: (r, 0)
    return pl.pallas_call(
        body, name="dw_in", grid=(nr,),
        in_specs=[pl.BlockSpec((tr, d), row),
                  pl.BlockSpec((tr, aw), row), pl.BlockSpec((tr, aw), row), pl.BlockSpec((tr, aw), row),
                  pl.BlockSpec((tr, rw), row), pl.BlockSpec((tr, rw), row), pl.BlockSpec((tr, LANES), row)],
        out_specs=pl.BlockSpec((d_in, d), lambda r: (0, 0)),
        out_shape=jax.ShapeDtypeStruct((d_in, d), BF16),
        scratch_shapes=[pltpu.VMEM((d_in, d), F32)],
        compiler_params=_cparams("arbitrary"),
    )(z, dq, dk, dv, dxr, dyr, dfl)


def _place():
    return lax.axis_index("x"), lax.axis_index("y"), lax.axis_index("c")


HBM = pl.BlockSpec(memory_space=pltpu.HBM)
SEM = pl.BlockSpec(memory_space=pltpu.SEMAPHORE)
EFFECT = pltpu.SideEffectType.DATAFLOW_SIDE_EFFECTING


def _in_hbm(a):
    return pltpu.with_memory_space_constraint(a, pltpu.HBM)


def _as_list(a):
    return list(a) if isinstance(a, (list, tuple)) else [a]


def _gather_targets(x, y, c):
    return [(x, y, 1 - c), (1 - x, y, c), (x, 1 - y, c), (1 - x, 1 - y, c)]


def _slot(t):
    return 4 * t[0] + 2 * t[1] + t[2]


def gather_start(groups, name):
    flat = [a for g in groups for a in g]
    n = len(flat)
    ng = len(groups)
    lands = [lax.empty((N_DEV,) + a.shape, a.dtype) for a in flat]

    def body(*refs):
        src, land = refs[:n], refs[n:2 * n]
        sems = refs[2 * n:2 * n + 2 * ng]
        token = refs[-1]
        x, y, c = _place()
        me = 4 * x + 2 * y + c
        i = 0
        for gi, g in enumerate(groups):
            for a in range(len(g)):
                for k, t in enumerate(_gather_targets(x, y, c)):
                    pltpu.make_async_remote_copy(
                        src_ref=src[i], dst_ref=land[i].at[me],
                        send_sem=sems[2 * gi].at[4 * a + k], recv_sem=sems[2 * gi + 1].at[4 * a + k],
                        device_id=t, device_id_type=MESH).start()
                i += 1
        token[...] = jnp.zeros_like(token)

    sem_shapes = []
    for g in groups:
        sem_shapes += [pltpu.SemaphoreType.DMA((4 * len(g),)), pltpu.SemaphoreType.DMA((4 * len(g),))]
    out = pl.pallas_call(
        body, name=name,
        out_shape=sem_shapes + [pltpu.HBM(a.shape, a.dtype) for a in flat + lands]
        + [jax.ShapeDtypeStruct((SUBLANES, LANES), F32)],
        in_specs=[HBM] * (2 * n),
        out_specs=[SEM] * (2 * ng) + [HBM] * (2 * n) + [pl.BlockSpec(memory_space=pltpu.VMEM)],
        input_output_aliases={i: 2 * ng + i for i in range(2 * n)},
        compiler_params=pltpu.CompilerParams(has_side_effects=EFFECT),
    )(*[_in_hbm(a) for a in flat + lands])
    sems = out[:2 * ng]
    thru = out[2 * ng:2 * ng + 2 * n]
    srcs_t, lands_t = thru[:n], thru[n:]
    res, i = [], 0
    for gi, g in enumerate(groups):
        res.append((sems[2 * gi], sems[2 * gi + 1], srcs_t[i:i + len(g)], lands_t[i:i + len(g)]))
        i += len(g)
    return res, out[-1]


def gather_wait(send, recv, srcs, lands, after, name):
    n = len(srcs)

    def body(*refs):
        src, land = refs[:n], refs[n:2 * n]
        send_sem, recv_sem = refs[2 * n], refs[2 * n + 1]
        x, y, c = _place()
        for a in range(n):
            for k, t in enumerate(_gather_targets(x, y, c)):
                cp = pltpu.make_async_remote_copy(
                    src_ref=src[a], dst_ref=land[a].at[_slot(t)],
                    send_sem=send_sem.at[4 * a + k], recv_sem=recv_sem.at[4 * a + k],
                    device_id=t, device_id_type=MESH)
                cp.wait_send()
                cp.wait_recv()

    out = pl.pallas_call(
        body, name=name,
        out_shape=[pltpu.HBM(a.shape, a.dtype) for a in list(srcs) + list(lands)],
        in_specs=[HBM] * (2 * n) + [SEM, SEM] + [ANY] * len(_as_list(after)),
        out_specs=[HBM] * (2 * n),
        input_output_aliases={i: i for i in range(2 * n)},
        compiler_params=pltpu.CompilerParams(has_side_effects=EFFECT),
    )(*srcs, *lands, send, recv, *_as_list(after))
    return out[:n], out[n:]


def forward_start(lands, name):
    n = len(lands)

    def body(*refs):
        land = refs[:n]
        send_sem, recv_sem = refs[n], refs[n + 1]
        token = refs[-1]
        x, y, c = _place()
        for a in range(n):
            for j, chip in enumerate([(1 - x, y), (x, 1 - y), (1 - x, 1 - y)]):
                blk = land[a].at[_slot((*chip, c))]
                pltpu.make_async_remote_copy(src_ref=blk, dst_ref=blk, send_sem=send_sem.at[3 * a + j],
                                             recv_sem=recv_sem.at[3 * a + j], device_id=(x, y, 1 - c),
                                             device_id_type=MESH).start()
        token[...] = jnp.zeros_like(token)

    out = pl.pallas_call(
        body, name=name,
        out_shape=[pltpu.SemaphoreType.DMA((3 * n,)), pltpu.SemaphoreType.DMA((3 * n,))]
        + [pltpu.HBM(a.shape, a.dtype) for a in lands] + [jax.ShapeDtypeStruct((SUBLANES, LANES), F32)],
        in_specs=[HBM] * n,
        out_specs=[SEM, SEM] + [HBM] * n + [pl.BlockSpec(memory_space=pltpu.VMEM)],
        input_output_aliases={i: 2 + i for i in range(n)},
        compiler_params=pltpu.CompilerParams(has_side_effects=EFFECT),
    )(*[_in_hbm(a) for a in lands])
    return out[0], out[1], out[2:2 + n], out[-1][0, 0]


def forward_wait(send, recv, lands, after, name):
    n = len(lands)

    def body(*refs):
        land = refs[:n]
        send_sem, recv_sem = refs[n], refs[n + 1]
        x, y, c = _place()
        for a in range(n):
            for j, chip in enumerate([(1 - x, y), (x, 1 - y), (1 - x, 1 - y)]):
                cp = pltpu.make_async_remote_copy(
                    src_ref=land[a].at[_slot((*chip, c))], dst_ref=land[a].at[_slot((*chip, 1 - c))],
                    send_sem=send_sem.at[3 * a + j], recv_sem=recv_sem.at[3 * a + j],
                    device_id=(x, y, 1 - c), device_id_type=MESH)
                cp.wait_send()
                cp.wait_recv()

    return pl.pallas_call(
        body, name=name,
        out_shape=[pltpu.HBM(a.shape, a.dtype) for a in lands],
        in_specs=[HBM] * n + [SEM, SEM, ANY],
        out_specs=[HBM] * n,
        input_output_aliases={i: i for i in range(n)},
        compiler_params=pltpu.CompilerParams(has_side_effects=EFFECT),
    )(*lands, send, recv, after)


def _relations():
    return [(dx, dy, dc) for dx in (0, 1) for dy in (0, 1) for dc in (0, 1) if dx + dy + dc]


def _peer(x, y, c, rel):
    return ((1 - x) if rel[0] else x, (1 - y) if rel[1] else y, (1 - c) if rel[2] else c)


def exchange_start(srcs, lands, layer, name, after=()):
    n = len(srcs)
    after = _as_list(after)

    def body(*refs):
        src, land = refs[:n], refs[n:2 * n]
        send_sem, recv_sem = refs[2 * n + len(after)], refs[2 * n + len(after) + 1]
        token = refs[-1]
        x, y, c = _place()
        me = 4 * x + 2 * y + c
        for k, rel in enumerate(_relations()):
            peer = _peer(x, y, c, rel)
            for a in range(n):
                pltpu.make_async_remote_copy(
                    src_ref=src[a] if layer is None else src[a].at[_slot(peer)],
                    dst_ref=land[a].at[me] if layer is None else land[a].at[me, layer],
                    send_sem=send_sem.at[7 * a + k], recv_sem=recv_sem.at[7 * a + k],
                    device_id=peer, device_id_type=MESH).start()
        token[...] = jnp.zeros_like(token)

    out = pl.pallas_call(
        body, name=name,
        out_shape=[pltpu.SemaphoreType.DMA((7 * n,)), pltpu.SemaphoreType.DMA((7 * n,))]
        + [pltpu.HBM(a.shape, a.dtype) for a in list(srcs) + list(lands)]
        + [jax.ShapeDtypeStruct((SUBLANES, LANES), F32)],
        in_specs=[HBM] * (2 * n) + [ANY] * len(after),
        out_specs=[SEM, SEM] + [HBM] * (2 * n) + [pl.BlockSpec(memory_space=pltpu.VMEM)],
        input_output_aliases={i: 2 + i for i in range(2 * n)},
        compiler_params=pltpu.CompilerParams(has_side_effects=EFFECT),
    )(*[_in_hbm(a) for a in list(srcs) + list(lands)], *after)
    return out[0], out[1], out[2:2 + n], out[2 + n:2 + 2 * n], out[-1][0, 0]


def exchange_wait(send, recv, srcs, lands, after, layer, name):
    n = len(srcs)

    def body(*refs):
        src, land = refs[:n], refs[n:2 * n]
        send_sem, recv_sem = refs[2 * n], refs[2 * n + 1]
        x, y, c = _place()
        for k, rel in enumerate(_relations()):
            peer = _peer(x, y, c, rel)
            for a in range(n):
                cp = pltpu.make_async_remote_copy(
                    src_ref=src[a] if layer is None else src[a].at[_slot(peer)],
                    dst_ref=land[a].at[_slot(peer)] if layer is None else land[a].at[_slot(peer), layer],
                    send_sem=send_sem.at[7 * a + k], recv_sem=recv_sem.at[7 * a + k],
                    device_id=peer, device_id_type=MESH)
                cp.wait_send()
                cp.wait_recv()

    out = pl.pallas_call(
        body, name=name,
        out_shape=[pltpu.HBM(a.shape, a.dtype) for a in list(srcs) + list(lands)],
        in_specs=[HBM] * (2 * n) + [SEM, SEM] + [ANY] * len(_as_list(after)),
        out_specs=[HBM] * (2 * n),
        input_output_aliases={i: i for i in range(2 * n)},
        compiler_params=pltpu.CompilerParams(has_side_effects=EFFECT),
    )(*srcs, *lands, send, recv, *_as_list(after))
    return out[:n], out[n:]


def _adamw_math(g, w, m, v):
    m = ADAM_B1 * m + (1.0 - ADAM_B1) * g
    v = ADAM_B2 * v + (1.0 - ADAM_B2) * (g * g)
    m_hat = m / (1.0 - ADAM_B1 ** ADAM_STEP)
    v_hat = v / (1.0 - ADAM_B2 ** ADAM_STEP)
    delta = -ADAM_LR * (m_hat / (jnp.sqrt(v_hat) + ADAM_EPS) + ADAM_WD * w)
    return delta, m, v


def _sum_with_own(p_ref, own_refs, layer, me):
    own = own_refs[0][...]
    for k in range(1, len(own_refs)):
        own = jnp.where(layer == k, own_refs[k][...], own)
    g = None
    for p in range(p_ref.shape[0]):
        term = jnp.where(me == p, own, p_ref[p]).astype(F32)
        g = term if g is None else g + term
    return g


def sum_adamw(parts, owns, me, w, m, v, tr, name):
    npart, rows, cols = parts.shape
    nl = len(owns)
    per_layer = rows // nl // tr

    def body(me_ref, p_ref, *refs):
        own_refs = refs[:nl]
        w_ref, m_ref, v_ref, g_ref, d_ref, nm_ref, nv_ref = refs[nl:]
        g = _sum_with_own(p_ref, own_refs, pl.program_id(0) // per_layer, me_ref[0])
        delta, nm, nv = _adamw_math(g, w_ref[...], m_ref[...], v_ref[...])
        g_ref[...] = g
        d_ref[...] = delta
        nm_ref[...] = nm
        nv_ref[...] = nv

    blk = pl.BlockSpec((tr, cols), lambda i, me_ref: (i, 0))
    own_specs = [pl.BlockSpec((None, tr, cols),
                              lambda i, me_ref, l=l: (me_ref[0], jnp.clip(i - l * per_layer, 0, per_layer - 1), 0))
                 for l in range(nl)]
    return pl.pallas_call(
        body, name=name,
        grid_spec=pltpu.PrefetchScalarGridSpec(
            num_scalar_prefetch=1, grid=(rows // tr,),
            in_specs=[pl.BlockSpec((npart, tr, cols), lambda i, me_ref: (0, i, 0))] + own_specs + [blk, blk, blk],
            out_specs=[blk] * 4),
        out_shape=[jax.ShapeDtypeStruct((rows, cols), F32)] * 4,
        compiler_params=_cparams("arbitrary"),
    )(me, parts, *owns, w, m, v)


def sum_adamw_t(parts, owns, me, w, m, v, name):
    npart, nl, rows, cols = parts.shape

    def body(me_ref, p_ref, *refs):
        own_refs = refs[:nl]
        w_ref, m_ref, v_ref, g_ref, d_ref, nm_ref, nv_ref = refs[nl:]
        g = _sum_with_own(p_ref, own_refs, pl.program_id(0), me_ref[0])
        delta, nm, nv = _adamw_math(g, w_ref[...], m_ref[...], v_ref[...])
        g_ref[...] = g
        d_ref[...] = delta
        nm_ref[...] = nm
        nv_ref[...] = nv

    blk = pl.BlockSpec((None, rows, cols), lambda l, me_ref: (l, 0, 0))
    own_specs = [pl.BlockSpec((None, rows, cols), lambda l, me_ref: (me_ref[0], 0, 0)) for _ in range(nl)]
    return pl.pallas_call(
        body, name=name,
        grid_spec=pltpu.PrefetchScalarGridSpec(
            num_scalar_prefetch=1, grid=(nl,),
            in_specs=[pl.BlockSpec((npart, None, rows, cols), lambda l, me_ref: (0, l, 0, 0))] + own_specs
            + [blk, blk, blk],
            out_specs=[blk] * 4),
        out_shape=[jax.ShapeDtypeStruct((nl, rows, cols), F32)] * 4,
        compiler_params=_cparams("arbitrary"),
    )(me, parts, *owns, w, m, v)


def adamw_group(gs, ws, ms, vs, name):
    n = len(gs)

    def body(*refs):
        g, w, m, v, outs = refs[:n], refs[n:2 * n], refs[2 * n:3 * n], refs[3 * n:4 * n], refs[4 * n:]
        for i in range(n):
            delta, nm, nv = _adamw_math(g[i][...], w[i][...], m[i][...], v[i][...])
            outs[i][...] = delta
            outs[n + i][...] = nm
            outs[2 * n + i][...] = nv

    vmem = pl.BlockSpec(memory_space=pltpu.VMEM)
    out = pl.pallas_call(
        body, name=name,
        in_specs=[vmem] * (4 * n), out_specs=[vmem] * (3 * n),
        out_shape=[jax.ShapeDtypeStruct(a.shape, F32) for a in list(ws) * 3],
        compiler_params=_cparams(),
    )(*gs, *ws, *ms, *vs)
    return out[:n], out[n:2 * n], out[2 * n:]


def sum_parts(parts, name):
    npart, rows, cols = parts.shape

    def body(p_ref, g_ref):
        g = p_ref[0].astype(F32)
        for p in range(1, npart):
            g = g + p_ref[p].astype(F32)
        g_ref[...] = g

    return pl.pallas_call(
        body, name=name, grid=(1,),
        in_specs=[pl.BlockSpec((npart, rows, cols), lambda i: (0, 0, 0))],
        out_specs=pl.BlockSpec((rows, cols), lambda i: (0, 0)),
        out_shape=jax.ShapeDtypeStruct((rows, cols), F32),
        compiler_params=_cparams("arbitrary"),
    )(parts)


def _round_up(n, m):
    return (n + m - 1) // m * m


def _block_diag_pairs(w):
    nb, b, _ = w.shape
    per = LANES // b
    ng = nb // per
    w = w.reshape(ng, per, b, b)
    eye = jnp.eye(per, dtype=w.dtype)
    out = jnp.einsum('gpij,pq->gpiqj', w, eye).reshape(ng, LANES, LANES)
    return out.astype(BF16)


def _block_diag_extract(g, b):
    ng = g.shape[0]
    per = LANES // b
    g = g.reshape(ng, per, b, per, b)
    idx = jnp.arange(per)
    return g[:, idx, :, idx, :].transpose(1, 0, 2, 3).reshape(ng * per, b, b)


def _tiles(v):
    v = v.reshape(-1)
    n = _round_up(v.shape[0], SUBLANES * LANES)
    return jnp.pad(v, (0, n - v.shape[0])).reshape(-1, LANES)


SMALL = ['attn_norm_g', 'b_f', 'conv_w', 'conv_b', 'w_gate_a', 'b_gate_a', 'w_gate_x', 'b_gate_x',
         'lru_L', 'attn_out_g', 'rec_out_g', 'mlp_norm_g', 'final_g', 'meta']


def _pack(d):
    return jnp.concatenate([_tiles(d[n]) for n in SMALL], axis=0)


def _unpack(vec, shapes):
    out, r = {}, 0
    for n in SMALL:
        size = math.prod(shapes[n])
        nr = _round_up(size, SUBLANES * LANES) // LANES
        out[n] = vec[r:r + nr].reshape(-1)[:size].reshape(shapes[n])
        r += nr
    return out


def _row_tile(tp):
    return tp // 4 if (tp // 4) % 16 == 0 else tp


def local_step(x, tgt, meta, small, hooks):
    s, d = x.shape
    t_real = s + N_META
    tp = _round_up(t_real, ATT_BLOCK)
    depth = small['attn_norm_g'].shape[0]
    nh = small['b_f'].shape[1]
    rw = small['conv_b'].shape[1]
    blk = small['w_gate_a'].shape[2]
    tm = _row_tile(tp)
    tm2 = tp // 2
    tm8 = tp // 8 if (tp // 8) % 16 == 0 else tm
    fcol = 2 * rw // LANES

    h = jnp.concatenate([meta, x, jnp.zeros((tp - t_real, d), F32)], axis=0)
    tgt_p = jnp.pad(tgt, ((N_META, tp - t_real), (0, 0)))
    row = lambda v: v.reshape(1, -1)
    bf_pad = jnp.pad(small['b_f'], ((0, 0), (0, LANES - nh)))

    saved = []
    for l in range(depth):
        w_in_t, wrest_t, wout, tok_w = hooks.mixer_weights(l, h)
        wga = _block_diag_pairs(small['w_gate_a'][l])
        wgx = _block_diag_pairs(small['w_gate_x'][l])
        z, qkv, rest = in_proj(h, row(small['attn_norm_g'][l]) + tok_w, w_in_t, wrest_t, 3 * nh * HEAD_DIM, tm)
        c, ct = fgate_fwd(rest, bf_pad[l:l + 1], fcol)
        o, lset = attn_fwd(qkv, c, ct, nh)
        rec, hr, xc, *gates = rec_fwd(rest, small['conv_w'][l], row(small['conv_b'][l]), wga,
                                      row(small['b_gate_a'][l]), wgx, row(small['b_gate_x'][l]),
                                      row(small['lru_L'][l]), rw)
        gup, gdown, tok_w = hooks.mlp_weights(l, rec)
        h2, mix, z2 = out_proj(h, o, rec, row(small['attn_out_g'][l]), row(small['rec_out_g'][l]), wout,
                               row(small['mlp_norm_g'][l]) + tok_w, tm8)
        u, h3 = mlp_fwd(z2, h2, gup, gdown, tm)
        saved.append(dict(h=h, z=z, qkv=qkv, rest=rest, c=c, ct=ct, o=o, lset=lset, rec=rec, hr=hr, xc=xc,
                          h2=h2, mix=mix, z2=z2, u=u, wga=wga, wgx=wgx, gates=gates,
                          w_in_t=w_in_t, wrest_t=wrest_t, wout=wout, gup=gup, gdown=gdown))
        h = h3

    dh, dgf, loss = loss_head(h, row(small['final_g']), tgt_p, t_real, tm)

    gs = {n: [None] * depth for n in SMALL if n not in ('final_g', 'meta')}
    tok = jnp.zeros((), F32)
    for l in reversed(range(depth)):
        sv = saved[l]
        gup, gdown = sv['gup'], sv['gdown']
        tf = gup.shape[2]
        dup, dh2, dg2, dhb = mlp_bwd(dh, sv['u'], sv['h2'], row(small['mlp_norm_g'][l]) + tok, gup, gdown, tm)
        gs['mlp_norm_g'][l] = dg2[0]
        do, drec, dga, dgr = out_proj_bwd(dh2, sv['o'], sv['rec'], row(small['attn_out_g'][l]),
                                          row(small['rec_out_g'][l]), sv['wout'], tm)
        gs['attn_out_g'][l] = dga[0]
        gs['rec_out_g'][l] = dgr[0]
        dw_down, dw_up = dw_mlp(sv['u'], dhb, sv['z2'], dup, tf)
        blocks = dict(
            w_down=dw_down, w_up=dw_up,
            w_out=mm_tn(sv['mix'], dh2, tn=d // 2, name="dw_out").reshape(N_DEV, d // N_DEV, d))
        tok = hooks.grads_ready(l, 'mlp', blocks)
        dxr, dyr, dwga, dwgx, vec = rec_bwd(drec, sv['hr'], sv['xc'], sv['gates'], sv['rest'], small['conv_w'][l],
                                            sv['wga'], sv['wgx'], row(small['lru_L'][l]) + tok, rw)
        gs['w_gate_a'][l] = _block_diag_extract(dwga, blk)
        gs['w_gate_x'][l] = _block_diag_extract(dwgx, blk)
        vec = vec.transpose(1, 0, 2).reshape(SUBLANES, rw)
        gs['conv_w'][l] = vec[0:CONV_WIDTH]
        gs['conv_b'][l] = vec[4]
        gs['b_gate_a'][l] = vec[5]
        gs['b_gate_x'][l] = vec[6]
        gs['lru_L'][l] = vec[7]
        dq, dk, dv, drow, dcol = attn_bwd(sv['qkv'], do, sv['o'], sv['lset'], sv['c'], sv['ct'] + tok, nh)
        drow8 = drow[:, 0:2, :].reshape(nh, tp)
        if nh < SUBLANES:
            drow8 = jnp.pad(drow8, ((0, SUBLANES - nh), (0, 0)))
        dfl, dbf = fgate_bwd(drow8, dcol, sv['rest'], bf_pad[l:l + 1], fcol)
        gs['b_f'][l] = dbf[0, 0:nh]
        parts = (dq, dk, dv, dxr, dyr, dfl)
        dh, dg1 = in_proj_bwd(dh2, parts, sv['w_in_t'], sv['wrest_t'], sv['h'], row(small['attn_norm_g'][l]), tm8)
        gs['attn_norm_g'][l] = dg1[0]
        first = ()
        if l == 0:
            grads = {n: jnp.stack(v) for n, v in gs.items()}
            grads['final_g'] = dgf[0]
            grads['meta'] = dh[0:N_META]
            first = hooks.small_ready(grads)
        dw_in = dw_in_t(sv['z'], parts, nh, tm2)
        dw_in = dw_in.reshape(N_DEV, dw_in.shape[0] // N_DEV, d)
        tok = hooks.grads_ready(l, 'in', dict(w_in=dw_in), first)

    return loss[0, 0], dh


def prep_weights(g_in, g_out, nh, rw):
    d = g_in.shape[2]
    w_in_t = g_in.reshape(-1, d)
    f0 = 3 * nh * HEAD_DIM
    wrest_t = jnp.concatenate([w_in_t[f0 + nh:f0 + nh + 2 * rw],
                               jnp.pad(w_in_t[f0:f0 + nh], ((0, LANES - nh), (0, 0)))], axis=0)
    return w_in_t, wrest_t, g_out.reshape(d, d)


BIG = ['w_in', 'w_out', 'w_up', 'w_down']
EXCHANGE_GROUPS = {'mlp': ['w_down', 'w_up', 'w_out'], 'in': ['w_in']}
WEIGHTS = ['meta', 'attn_norm_g', 'w_in', 'b_f', 'conv_w', 'conv_b', 'w_gate_a', 'b_gate_a', 'w_gate_x', 'b_gate_x',
           'lru_L', 'attn_out_g', 'rec_out_g', 'w_out', 'mlp_norm_g', 'w_up', 'w_down', 'final_g']


def _set_own(arr, own, me):
    return lax.dynamic_update_slice_in_dim(arr, own[None], me, axis=0)


class _Step:
    def __init__(self, w, nh, rw, me):
        self.w, self.nh, self.rw, self.me = w, nh, rw, me
        depth = w['w_in'].shape[0]
        first = [w['w_in_t'][:, 0, :].astype(BF16), w['w_out'][0].astype(BF16), w['meta'], w['conv_w']]
        self.pending, token = gather_start([first], "gather_start_0")
        zero = token[0, 0].astype(BF16)
        groups = [[w['w_up'][0].astype(BF16) + zero, w['w_down'][0].astype(BF16) + zero]]
        for l in range(1, depth):
            groups.append([w['w_in_t'][:, l, :].astype(BF16) + zero, w['w_out'][l].astype(BF16) + zero])
            groups.append([w['w_up'][l].astype(BF16) + zero, w['w_down'][l].astype(BF16) + zero])
        rest, _ = gather_start(groups, "gather_start_1")
        self.pending += rest
        self.first_after = rest[0][2][0]
        self.gathered = {}
        self.passing = {}
        self.token = jnp.zeros((), F32)
        self.lands = {n: lax.empty((N_DEV,) + w[n].shape, BF16) for n in BIG}
        din8, _, d = w['w_in_t'].shape
        self.lands['w_in'] = lax.empty((N_DEV, depth, din8, d), BF16)
        self.started = []
        self.small = None

    def _pass_on(self, gi, after):
        if gi < len(self.pending) and gi not in self.passing:
            send, recv, srcs, lands = self.pending[gi]
            srcs, lands = gather_wait(send, recv, srcs, lands, after, "gather_wait_%d" % gi)
            fsend, frecv, lands, token = forward_start(lands, "forward_start_%d" % gi)
            self.passing[gi] = (fsend, frecv, srcs, lands)
            self.token = token

    def group(self, gi, after):
        if gi not in self.gathered:
            self._pass_on(gi, after)
            fsend, frecv, srcs, lands = self.passing[gi]
            lands = forward_wait(fsend, frecv, lands, after, "forward_wait_%d" % gi)
            self.gathered[gi] = [_set_own(g, own, self.me) for g, own in zip(lands, srcs)]
            if gi >= 2:
                self._pass_on(gi + 1, lands[0])
        return self.gathered[gi]

    def mixer_weights(self, l, after):
        g = self.group(2 * l, after)
        return (*prep_weights(g[0], g[1], self.nh, self.rw), self.token)

    def mlp_weights(self, l, after):
        g = self.group(2 * l + 1, after)
        return g[0], g[1], self.token

    def grads_ready(self, l, group, blocks, after=()):
        names = EXCHANGE_GROUPS[group]
        send, recv, srcs, lands, token = exchange_start(
            [blocks[n] for n in names], [self.lands[n] for n in names], l, "exchange_start_%s_%d" % (group, l),
            after)
        for n, a in zip(names, lands):
            self.lands[n] = a
        self.started.append((l, group, send, recv, srcs))
        return token

    def small_ready(self, grads):
        self.small_shapes = {n: grads[n].shape for n in SMALL}
        packed = _pack(grads).astype(BF16)
        send, recv, srcs, lands, token = exchange_start(
            [packed], [lax.empty((N_DEV,) + packed.shape, BF16)], None, "small_start")
        self.small = (send, recv, srcs, lands)
        return srcs[0]

    def small_sum(self, after):
        send, recv, srcs, lands = self.small
        srcs, lands = exchange_wait(send, recv, srcs, lands, after, None, "small_wait")
        parts = _set_own(lands[0], srcs[0], self.me)
        return _unpack(sum_parts(parts, "sum_small_grads"), self.small_shapes)

    def received(self, group, after):
        names = EXCHANGE_GROUPS[group]
        own = {n: [None] * self.w[n].shape[0] for n in names}
        for l, grp, send, recv, srcs in self.started:
            if grp != group:
                continue
            srcs, lands = exchange_wait(send, recv, srcs, [self.lands[n] for n in names], after, l,
                                        "exchange_wait_%s_%d" % (group, l))
            for n, a, sr in zip(names, lands, srcs):
                self.lands[n] = a
                own[n][l] = sr
        return {n: (self.lands[n], own[n]) for n in names}


def kernel(x, meta, attn_norm_g, w_in, b_f, conv_w, conv_b, w_gate_a, b_gate_a, w_gate_x, b_gate_x, lru_L, attn_out_g, rec_out_g, w_out, mlp_norm_g, w_up, w_down, final_g, loss_target, m_meta, m_attn_norm_g, m_w_in, m_b_f, m_conv_w, m_conv_b, m_w_gate_a, m_b_gate_a, m_w_gate_x, m_b_gate_x, m_lru_L, m_attn_out_g, m_rec_out_g, m_w_out, m_mlp_norm_g, m_w_up, m_w_down, m_final_g, v_meta, v_attn_norm_g, v_w_in, v_b_f, v_conv_w, v_conv_b, v_w_gate_a, v_b_gate_a, v_w_gate_x, v_b_gate_x, v_lru_L, v_attn_out_g, v_rec_out_g, v_w_out, v_mlp_norm_g, v_w_up, v_w_down, v_final_g):
    w = dict(meta=meta, attn_norm_g=attn_norm_g, w_in=w_in, b_f=b_f, conv_w=conv_w, conv_b=conv_b,
             w_gate_a=w_gate_a, b_gate_a=b_gate_a, w_gate_x=w_gate_x, b_gate_x=b_gate_x, lru_L=lru_L,
             attn_out_g=attn_out_g, rec_out_g=rec_out_g, w_out=w_out, mlp_norm_g=mlp_norm_g, w_up=w_up,
             w_down=w_down, final_g=final_g)
    mo = dict(meta=m_meta, attn_norm_g=m_attn_norm_g, w_in=m_w_in, b_f=m_b_f, conv_w=m_conv_w, conv_b=m_conv_b,
              w_gate_a=m_w_gate_a, b_gate_a=m_b_gate_a, w_gate_x=m_w_gate_x, b_gate_x=m_b_gate_x, lru_L=m_lru_L,
              attn_out_g=m_attn_out_g, rec_out_g=m_rec_out_g, w_out=m_w_out, mlp_norm_g=m_mlp_norm_g,
              w_up=m_w_up, w_down=m_w_down, final_g=m_final_g)
    vo = dict(meta=v_meta, attn_norm_g=v_attn_norm_g, w_in=v_w_in, b_f=v_b_f, conv_w=v_conv_w, conv_b=v_conv_b,
              w_gate_a=v_w_gate_a, b_gate_a=v_b_gate_a, w_gate_x=v_w_gate_x, b_gate_x=v_b_gate_x, lru_L=v_lru_L,
              attn_out_g=v_attn_out_g, rec_out_g=v_rec_out_g, w_out=v_w_out, mlp_norm_g=v_mlp_norm_g,
              w_up=v_w_up, w_down=v_w_down, final_g=v_final_g)
    depth = w_in.shape[0]
    nh = b_f.shape[1]
    rw = conv_b.shape[1]
    me = 4 * lax.axis_index("x") + 2 * lax.axis_index("y") + lax.axis_index("c")

    w['w_in_t'] = jnp.transpose(w_in, (2, 0, 1))
    swap = lambda a: jnp.swapaxes(a, 1, 2)
    step = _Step(w, nh, rw, me)
    g0 = step.group(0, step.first_after)
    meta_full = g0[2].transpose(1, 0, 2).reshape(N_META, -1)
    conv_full = g0[3].transpose(1, 2, 0, 3).reshape(depth, CONV_WIDTH, rw)
    small = {n: w[n] for n in SMALL}
    small['conv_w'] = conv_full

    loss_part, dh0 = local_step(x[0], loss_target[0], meta_full, small, step)
    loss = lax.psum(loss_part, ("x", "y", "c"))
    grad_x = dh0[N_META:N_META + x.shape[1]][None]

    out_g, out_d, out_m, out_v = {}, {}, {}, {}
    me1 = me.reshape(1).astype(jnp.int32)

    def update_big(group, after):
        for n, (r, owns) in step.received(group, after).items():
            if n == 'w_in':
                out = sum_adamw_t(r, owns, me1, swap(w[n]), swap(mo[n]), swap(vo[n]), "adamw_w_in")
                out = [swap(a) for a in out]
            else:
                shp = w[n].shape
                rows, cols = shp[0] * shp[1], shp[2]
                tr = min(512 if cols <= 512 else 256, shp[1])
                out = sum_adamw(r.reshape(N_DEV, rows, cols), owns, me1, w[n].reshape(rows, cols),
                                mo[n].reshape(rows, cols), vo[n].reshape(rows, cols), tr, "adamw_" + n)
                out = [a.reshape(shp) for a in out]
            out_g[n], out_d[n], out_m[n], out_v[n] = out
            after = out[0]
        return after

    update_big('mlp', step.started[-1][4][0])

    gsum = step.small_sum([out_g[n] for n in EXCHANGE_GROUPS['mlp']])
    gsum['meta'] = lax.dynamic_slice_in_dim(gsum['meta'], me * meta.shape[1], meta.shape[1], axis=1)
    gsum['conv_w'] = lax.dynamic_slice_in_dim(gsum['conv_w'], me * conv_w.shape[2], conv_w.shape[2], axis=2)
    as2d = lambda a: a.reshape(-1, a.shape[-1])
    deltas, new_m, new_v = adamw_group([as2d(gsum[n]) for n in SMALL], [as2d(w[n]) for n in SMALL],
                                       [as2d(mo[n]) for n in SMALL], [as2d(vo[n]) for n in SMALL], "adamw_small")
    for i, n in enumerate(SMALL):
        out_g[n] = gsum[n]
        out_d[n], out_m[n], out_v[n] = [a[i].reshape(w[n].shape) for a in (deltas, new_m, new_v)]

    update_big('in', deltas[0])

    return (loss, grad_x, *[out_g[n] for n in WEIGHTS], *[out_d[n] for n in WEIGHTS],
            *[out_m[n] for n in WEIGHTS], *[out_v[n] for n in WEIGHTS])
```

```python
import math

import jax
import jax.numpy as jnp
from jax import lax
from jax.experimental import pallas as pl
from jax.experimental.pallas import tpu as pltpu

F32 = jnp.float32
BF16 = jnp.bfloat16

N_DEV = 8
N_META = 16
HEAD_DIM = 64
CONV_WIDTH = 4
RG_C = 8.0
NORM_EPS = 1e-6
LANES = 128
SUBLANES = 8
ATT_BLOCK = 128
ATT_TQ = 512
NEG_BIG = -1e30
ATT_SCALE = 1.0 / math.sqrt(HEAD_DIM)

ADAM_LR = 0.001
ADAM_B1 = 0.9
ADAM_B2 = 0.999
ADAM_EPS = 1e-08
ADAM_WD = 0.01
ADAM_STEP = 10

VMEM_LIMIT_BYTES = 56 * 1024 * 1024
MESH = pl.DeviceIdType.MESH
ANY = pl.BlockSpec(memory_space=pl.ANY)


def _cparams(*sem):
    return pltpu.CompilerParams(dimension_semantics=sem if sem else None,
                                vmem_limit_bytes=VMEM_LIMIT_BYTES)


def _dot(a, b):
    return jnp.dot(a, b, preferred_element_type=F32)


def _dot_nt(a, b):
    return lax.dot_general(a, b, (((1,), (1,)), ((), ())), preferred_element_type=F32)


def _dot_tn(a, b):
    return lax.dot_general(a, b, (((0,), (0,)), ((), ())), preferred_element_type=F32)


def _sigmoid(x):
    return 0.5 * (1.0 + jnp.tanh(0.5 * x))


def _log_sigmoid(x):
    return jnp.minimum(x, 0.0) - jnp.log(1.0 + jnp.exp(-jnp.abs(x)))


def _expm1(x):
    series = x * (1.0 + x * (0.5 + x * (1.0 / 6.0 + x * (1.0 / 24.0))))
    return jnp.where(jnp.abs(x) < 1e-2, series, jnp.exp(x) - 1.0)


_GELU_K = math.sqrt(2.0 / math.pi)
_GELU_C = 0.044715


def _gelu(x):
    t = jnp.tanh(_GELU_K * (x + _GELU_C * x * x * x))
    return 0.5 * x * (1.0 + t)


def _gelu_and_grad(x):
    x2 = x * x
    t = jnp.tanh(_GELU_K * (x + _GELU_C * x2 * x))
    half = 0.5 * (1.0 + t)
    return x * half, half + 0.5 * x * (1.0 - t * t) * _GELU_K * (1.0 + 3.0 * _GELU_C * x2)


def _split3_dot(tri, x):
    hi = x.astype(BF16)
    r1 = x - hi.astype(F32)
    mid = r1.astype(BF16)
    lo = (r1 - mid.astype(F32)).astype(BF16)
    return _dot(tri, hi) + _dot(tri, mid) + _dot(tri, lo)


def _dot_split3(x, sel):
    hi = x.astype(BF16)
    r1 = x - hi.astype(F32)
    mid = r1.astype(BF16)
    lo = (r1 - mid.astype(F32)).astype(BF16)
    return _dot(hi, sel) + _dot(mid, sel) + _dot(lo, sel)


def _rms_fwd(x, g):
    r = lax.rsqrt(jnp.mean(x * x, axis=-1, keepdims=True) + NORM_EPS)
    return x * r * g


def _rms_bwd(x, g, dy):
    r = lax.rsqrt(jnp.mean(x * x, axis=-1, keepdims=True) + NORM_EPS)
    xn = x * r
    dxn = dy * g
    dx = r * (dxn - xn * jnp.mean(dxn * xn, axis=-1, keepdims=True))
    return dx, jnp.sum(dy * xn, axis=0, keepdims=True)


def _accumulate(ref, val, first):
    @pl.when(first)
    def _():
        ref[...] = val

    @pl.when(jnp.logical_not(first))
    def _():
        ref[...] += val


def in_proj(h, g1, w_in_t, wrest_t, nq, tm):
    tp, d = h.shape
    nr = wrest_t.shape[0]

    def body(h_ref, g_ref, wq_ref, wr_ref, z_ref, qkv_ref, rest_ref):
        z = _rms_fwd(h_ref[...], g_ref[...]).astype(BF16)
        z_ref[...] = z
        qkv_ref[...] = _dot_nt(z, wq_ref[...]).astype(BF16)
        rest_ref[...] = _dot_nt(z, wr_ref[...])

    return pl.pallas_call(
        body, name="in_proj", grid=(tp // tm,),
        in_specs=[pl.BlockSpec((tm, d), lambda i: (i, 0)),
                  pl.BlockSpec((1, d), lambda i: (0, 0)),
                  pl.BlockSpec((nq, d), lambda i: (0, 0)),
                  pl.BlockSpec((nr, d), lambda i: (0, 0))],
        out_specs=[pl.BlockSpec((tm, d), lambda i: (i, 0)),
                   pl.BlockSpec((tm, nq), lambda i: (i, 0)),
                   pl.BlockSpec((tm, nr), lambda i: (i, 0))],
        out_shape=[jax.ShapeDtypeStruct((tp, d), BF16),
                   jax.ShapeDtypeStruct((tp, nq), BF16),
                   jax.ShapeDtypeStruct((tp, nr), F32)],
        compiler_params=_cparams("parallel"),
    )(h, g1, w_in_t, wrest_t)


def fgate_fwd(rest, bf_pad, fcol):
    tp = rest.shape[0]
    nb = tp // ATT_BLOCK

    def body(f_ref, b_ref, c_ref, ct_ref):
        r_i = lax.broadcasted_iota(jnp.int32, (ATT_BLOCK, ATT_BLOCK), 0)
        c_i = lax.broadcasted_iota(jnp.int32, (ATT_BLOCK, ATT_BLOCK), 1)
        tri = (r_i >= c_i).astype(BF16)
        carry = jnp.zeros((1, LANES), F32)
        for i in range(nb):
            sl = slice(i * ATT_BLOCK, (i + 1) * ATT_BLOCK)
            lf = _log_sigmoid(f_ref[sl, :] + b_ref[...])
            cs = _split3_dot(tri, lf) + carry
            carry = cs[ATT_BLOCK - 1:ATT_BLOCK, :]
            c_ref[sl, :] = cs
            ct_ref[:, sl] = cs.T[0:SUBLANES, :]

    return pl.pallas_call(
        body, name="fgate_fwd", grid=(1,),
        in_specs=[pl.BlockSpec((tp, LANES), lambda i: (0, fcol)),
                  pl.BlockSpec((1, LANES), lambda i: (0, 0))],
        out_specs=[pl.BlockSpec((tp, LANES), lambda i: (0, 0)),
                   pl.BlockSpec((SUBLANES, tp), lambda i: (0, 0))],
        out_shape=[jax.ShapeDtypeStruct((tp, LANES), F32),
                   jax.ShapeDtypeStruct((SUBLANES, tp), F32)],
        compiler_params=_cparams("arbitrary"),
    )(rest, bf_pad)


def _pick_col(blk, head):
    lane = lax.broadcasted_iota(jnp.int32, blk.shape, 1)
    return jnp.sum(jnp.where(lane == head, blk, 0.0), axis=1, keepdims=True)


def _pick_row(blk, head):
    sub = lax.broadcasted_iota(jnp.int32, blk.shape, 0)
    return jnp.sum(jnp.where(sub == head, blk, 0.0), axis=0, keepdims=True)


def _att_tiles(tp):
    out, r0 = [], 0
    while r0 < tp:
        rows = min(ATT_TQ, tp - r0)
        out.append((r0, rows, r0 + rows))
        r0 += rows
    return out


def attn_fwd(qkv, c, ct, nh):
    tp = qkv.shape[0]
    npair = nh // 2
    tiles = _att_tiles(tp)

    def body(q_ref, k_ref, v_ref, c_ref, ct_ref, o_ref, lset_ref):
        p = pl.program_id(0)
        lset_ref[...] = jnp.zeros_like(lset_ref)
        for r0, nr, nk in tiles:
            rs = slice(r0, r0 + nr)
            causal = (r0 + lax.broadcasted_iota(jnp.int32, (nr, nk), 0)
                      >= lax.broadcasted_iota(jnp.int32, (nr, nk), 1))
            cblk = c_ref[rs, :]
            ctb = ct_ref[:, 0:nk]
            for hh in range(2):
                head = 2 * p + hh
                hs = slice(hh * HEAD_DIM, (hh + 1) * HEAD_DIM)
                q = q_ref[rs, hs] * ATT_SCALE
                s = _dot_nt(q, k_ref[0:nk, hs]) + (_pick_col(cblk, head) - _pick_row(ctb, head))
                s = jnp.where(causal, s, NEG_BIG)
                m = jnp.max(s, axis=1, keepdims=True)
                pm = jnp.exp(s - m)
                l = jnp.sum(pm, axis=1, keepdims=True)
                o_ref[rs, hs] = _dot(pm.astype(BF16), v_ref[0:nk, hs]) / l
                lse = m + jnp.log(l)
                lset_ref[hh:hh + 1, rs] = jnp.broadcast_to(lse, (nr, LANES)).T[0:1, :]

    pair = lambda p: (0, p)
    return pl.pallas_call(
        body, name="attn_fwd", grid=(npair,),
        in_specs=[pl.BlockSpec((tp, LANES), pair),
                  pl.BlockSpec((tp, LANES), lambda p: (0, npair + p)),
                  pl.BlockSpec((tp, LANES), lambda p: (0, 2 * npair + p)),
                  pl.BlockSpec((tp, LANES), lambda p: (0, 0)),
                  pl.BlockSpec((SUBLANES, tp), lambda p: (0, 0))],
        out_specs=[pl.BlockSpec((tp, LANES), pair),
                   pl.BlockSpec((None, SUBLANES, tp), lambda p: (p, 0, 0))],
        out_shape=[jax.ShapeDtypeStruct((tp, nh * HEAD_DIM), F32),
                   jax.ShapeDtypeStruct((npair, SUBLANES, tp), F32)],
        compiler_params=_cparams("parallel"),
    )(qkv, qkv, qkv, c, ct)


def _shift_down(x, k, n):
    if k == 0:
        return x
    rows = lax.broadcasted_iota(jnp.int32, x.shape, 0)
    return jnp.where(rows >= k, pltpu.roll(x, k, 0), 0.0)


def _shift_up(x, k, n):
    if k == 0:
        return x
    rows = lax.broadcasted_iota(jnp.int32, x.shape, 0)
    return jnp.where(rows < n - k, pltpu.roll(x, n - k, 0), 0.0)


def _conv_fwd(xr, cw_ref, cb_ref, n):
    xc = cw_ref[CONV_WIDTH - 1:CONV_WIDTH, :] * xr + cb_ref[...]
    for k in range(1, CONV_WIDTH):
        xc = xc + cw_ref[CONV_WIDTH - 1 - k:CONV_WIDTH - k, :] * _shift_down(xr, k, n)
    return xc


def _gates(xc, wga_ref, bga_ref, wgx_ref, bgx_ref, l_ref):
    xcb = xc.astype(BF16)
    r = _sigmoid(_dot(xcb, wga_ref[...]) + bga_ref[...])
    ig = _sigmoid(_dot(xcb, wgx_ref[...]) + bgx_ref[...])
    ls = _log_sigmoid(l_ref[...])
    log_a = RG_C * r * ls
    a = jnp.exp(log_a)
    mult = jnp.sqrt(-_expm1(2.0 * log_a))
    return xcb, r, ig, ls, log_a, a, mult


SCAN_UNROLL = 4


def _scan_rows(a_s, u_s, out_ref, n, reverse):
    nt = n // SUBLANES
    per = SCAN_UNROLL if nt % SCAN_UNROLL == 0 else 1
    row = lax.broadcasted_iota(jnp.int32, (SUBLANES, LANES), 0)
    last = 0 if reverse else SUBLANES - 1

    def tile_scan(a, u):
        for d in (1, 2, 4):
            if reverse:
                keep = row < SUBLANES - d
                sh = SUBLANES - d
            else:
                keep = row >= d
                sh = d
            a_sh = jnp.where(keep, pltpu.roll(a, sh, 0), 1.0)
            u_sh = jnp.where(keep, pltpu.roll(u, sh, 0), 0.0)
            u = a * u_sh + u
            a = a * a_sh
        return a, u

    def step(t, carry):
        tiles = []
        for k in range(per):
            tt = t * per + k
            if reverse:
                tt = nt - 1 - tt
            off = pl.multiple_of(tt * SUBLANES, SUBLANES)
            a, u = tile_scan(a_s[pl.ds(off, SUBLANES), :], u_s[pl.ds(off, SUBLANES), :])
            tiles.append((off, a, u))
        for off, a, u in tiles:
            out_ref[pl.ds(off, SUBLANES), :] = u + a * carry
            carry = u[last:last + 1, :] + a[last:last + 1, :] * carry
        return carry

    lax.fori_loop(0, nt // per, step, jnp.zeros((1, LANES), F32))


def rec_fwd(rest, convw, convb, wga, bga, wgx, bgx, lru, rw):
    tp = rest.shape[0]
    ng = rw // LANES

    def body(xr_ref, yr_ref, cw_ref, cb_ref, wga_ref, bga_ref, wgx_ref, bgx_ref, l_ref,
             rec_ref, hr_ref, xc_ref, r_ref, ig_ref, a_ref, mult_ref, u_s):
        xc = _conv_fwd(xr_ref[...], cw_ref, cb_ref, tp)
        xc_ref[...] = xc
        _, r, ig, ls, log_a, a, mult = _gates(xc, wga_ref, bga_ref, wgx_ref, bgx_ref, l_ref)
        r_ref[...] = r
        ig_ref[...] = ig
        a_ref[...] = a
        mult_ref[...] = mult
        u_s[...] = mult * ig * xc
        _scan_rows(a_ref, u_s, hr_ref, tp, reverse=False)
        rec_ref[...] = hr_ref[...] * _gelu(yr_ref[...])

    col = lambda g: (0, g)
    vec = pl.BlockSpec((1, LANES), col)
    big = pl.BlockSpec((tp, LANES), col)
    return pl.pallas_call(
        body, name="rec_fwd", grid=(ng,),
        in_specs=[big, pl.BlockSpec((tp, LANES), lambda g: (0, ng + g)),
                  pl.BlockSpec((CONV_WIDTH, LANES), col), vec,
                  pl.BlockSpec((None, LANES, LANES), lambda g: (g, 0, 0)), vec,
                  pl.BlockSpec((None, LANES, LANES), lambda g: (g, 0, 0)), vec, vec],
        out_specs=[big] * 7,
        out_shape=[jax.ShapeDtypeStruct((tp, rw), F32)] * 7,
        scratch_shapes=[pltpu.VMEM((tp, LANES), F32)],
        compiler_params=_cparams("parallel"),
    )(rest, rest, convw, convb, wga, bga, wgx, bgx, lru)


def out_proj(h, o, rec, ga, gr, wout, g2, tm):
    tp, d = h.shape
    aw, rw = o.shape[1], rec.shape[1]

    def body(h_ref, o_ref, rec_ref, ga_ref, gr_ref, w_ref, g2_ref, h2_ref, mix_ref, z2_ref):
        mix_ref[:, 0:aw] = _rms_fwd(o_ref[...], ga_ref[...]).astype(BF16)
        mix_ref[:, aw:aw + rw] = _rms_fwd(rec_ref[...], gr_ref[...]).astype(BF16)
        h2 = h_ref[...] + _dot(mix_ref[...], w_ref[...])
        h2_ref[...] = h2
        z2_ref[...] = _rms_fwd(h2, g2_ref[...]).astype(BF16)

    row = lambda i: (i, 0)
    fix = lambda i: (0, 0)
    return pl.pallas_call(
        body, name="out_proj", grid=(tp // tm,),
        in_specs=[pl.BlockSpec((tm, d), row), pl.BlockSpec((tm, aw), row), pl.BlockSpec((tm, rw), row),
                  pl.BlockSpec((1, aw), fix), pl.BlockSpec((1, rw), fix),
                  pl.BlockSpec((d, d), fix), pl.BlockSpec((1, d), fix)],
        out_specs=[pl.BlockSpec((tm, d), row)] * 3,
        out_shape=[jax.ShapeDtypeStruct((tp, d), F32), jax.ShapeDtypeStruct((tp, d), BF16),
                   jax.ShapeDtypeStruct((tp, d), BF16)],
        compiler_params=_cparams("parallel"),
    )(h, o, rec, ga, gr, wout, g2)


MLP_BLOCKS = 4


def mlp_fwd(z2, h2, gup, gdown, tm):
    tp, d = h2.shape
    nf = gup.shape[0]
    tf = gup.shape[2]
    nb = MLP_BLOCKS if nf % MLP_BLOCKS == 0 else 1
    nj = nf // nb

    def body(z_ref, h_ref, wu_ref, wd_ref, u_ref, h3_ref, acc):
        j = pl.program_id(1)
        z = z_ref[...]
        part = None
        for b in range(nb):
            u = jnp.maximum(_dot(z, wu_ref[b]), 0.0)
            u_ref[:, b * tf:(b + 1) * tf] = u.astype(BF16)
            p = _dot((u * u).astype(BF16), wd_ref[b])
            part = p if part is None else part + p

        @pl.when(j == 0)
        def _():
            acc[...] = h_ref[...] + part

        @pl.when(j > 0)
        def _():
            acc[...] += part

        @pl.when(j == nj - 1)
        def _():
            h3_ref[...] = acc[...]

    return pl.pallas_call(
        body, name="mlp_fwd", grid=(tp // tm, nj),
        in_specs=[pl.BlockSpec((tm, d), lambda i, j: (i, 0)),
                  pl.BlockSpec((tm, d), lambda i, j: (i, 0)),
                  pl.BlockSpec((nb, d, tf), lambda i, j: (j, 0, 0)),
                  pl.BlockSpec((nb, tf, d), lambda i, j: (j, 0, 0))],
        out_specs=[pl.BlockSpec((tm, nb * tf), lambda i, j: (i, j)),
                   pl.BlockSpec((tm, d), lambda i, j: (i, 0))],
        out_shape=[jax.ShapeDtypeStruct((tp, nf * tf), BF16), jax.ShapeDtypeStruct((tp, d), F32)],
        scratch_shapes=[pltpu.VMEM((tm, d), F32)],
        compiler_params=_cparams("parallel", "arbitrary"),
    )(z2, h2, gup, gdown)


def loss_head(h, gf, tgt, t_real, tm):
    tp, d = h.shape

    def body(h_ref, g_ref, t_ref, dh_ref, dg_ref, loss_ref):
        i = pl.program_id(0)
        x = h_ref[...]
        g = g_ref[...]
        r = lax.rsqrt(jnp.mean(x * x, axis=-1, keepdims=True) + NORM_EPS)
        xn = x * r
        rows = i * tm + lax.broadcasted_iota(jnp.int32, (tm, 1), 0)
        valid = jnp.logical_and(rows >= N_META, rows < t_real)
        e = jnp.where(valid, xn * g - t_ref[...], 0.0)
        part = 0.5 * jnp.sum(jnp.sum(e * e, axis=1, keepdims=True) / d, axis=0, keepdims=True)
        dy = e / d
        dxn = dy * g
        dh_ref[...] = r * (dxn - xn * jnp.mean(dxn * xn, axis=-1, keepdims=True))
        _accumulate(dg_ref, jnp.sum(dy * xn, axis=0, keepdims=True), i == 0)
        _accumulate(loss_ref, jnp.broadcast_to(part, (1, LANES)), i == 0)

    row = lambda i: (i, 0)
    fix = lambda i: (0, 0)
    return pl.pallas_call(
        body, name="loss_head", grid=(tp // tm,),
        in_specs=[pl.BlockSpec((tm, d), row), pl.BlockSpec((1, d), fix), pl.BlockSpec((tm, d), row)],
        out_specs=[pl.BlockSpec((tm, d), row), pl.BlockSpec((1, d), fix), pl.BlockSpec((1, LANES), fix)],
        out_shape=[jax.ShapeDtypeStruct((tp, d), F32), jax.ShapeDtypeStruct((1, d), F32),
                   jax.ShapeDtypeStruct((1, LANES), F32)],
        compiler_params=_cparams("arbitrary"),
    )(h, gf, tgt)


def mlp_bwd(dh, u, h2, g2, gup, gdown, tm):
    tp, d = dh.shape
    nf = gup.shape[0]
    tf = gup.shape[2]
    nb = MLP_BLOCKS if nf % MLP_BLOCKS == 0 else 1
    nj = nf // nb
    ni = tp // tm

    def body(dh_ref, u_ref, h2_ref, g_ref, wu_ref, wd_ref, dup_ref, dh2_ref, dg_ref, dhb, acc):
        i = pl.program_id(0)
        j = pl.program_id(1)

        @pl.when(j == 0)
        def _():
            dhb[...] = dh_ref[...].astype(BF16)

        part = None
        for b in range(nb):
            cols = slice(b * tf, (b + 1) * tf)
            dup = (_dot_nt(dhb[...], wd_ref[b]) * (2.0 * u_ref[:, cols].astype(F32))).astype(BF16)
            dup_ref[:, cols] = dup
            p = _dot_nt(dup, wu_ref[b])
            part = p if part is None else part + p
        _accumulate(acc, part, j == 0)

        @pl.when(j == nj - 1)
        def _():
            dx, dg = _rms_bwd(h2_ref[...], g_ref[...], acc[...])
            dh2_ref[...] = dh_ref[...] + dx
            _accumulate(dg_ref, dg, i == 0)

    return pl.pallas_call(
        body, name="mlp_bwd", grid=(ni, nj),
        in_specs=[pl.BlockSpec((tm, d), lambda i, j: (i, 0)),
                  pl.BlockSpec((tm, nb * tf), lambda i, j: (i, j)),
                  pl.BlockSpec((tm, d), lambda i, j: (i, 0)),
                  pl.BlockSpec((1, d), lambda i, j: (0, 0)),
                  pl.BlockSpec((nb, d, tf), lambda i, j: (j, 0, 0)),
                  pl.BlockSpec((nb, tf, d), lambda i, j: (j, 0, 0))],
        out_specs=[pl.BlockSpec((tm, nb * tf), lambda i, j: (i, j)),
                   pl.BlockSpec((tm, d), lambda i, j: (i, 0)),
                   pl.BlockSpec((1, d), lambda i, j: (0, 0)),
                   pl.BlockSpec((tm, d), lambda i, j: (i, 0))],
        out_shape=[jax.ShapeDtypeStruct((tp, nf * tf), BF16), jax.ShapeDtypeStruct((tp, d), F32),
                   jax.ShapeDtypeStruct((1, d), F32), jax.ShapeDtypeStruct((tp, d), BF16)],
        scratch_shapes=[pltpu.VMEM((tm, d), F32)],
        compiler_params=_cparams("arbitrary", "arbitrary"),
    )(dh, u, h2, g2, gup, gdown)


def mm_tn(a, b, *, tn, name):
    rows, kk = a.shape
    nn = b.shape[1]

    def body(a_ref, b_ref, o_ref, at):
        @pl.when(pl.program_id(0) == 0)
        def _():
            at[...] = a_ref[...].T

        o_ref[...] = _dot(at[...], b_ref[...].astype(BF16)).astype(BF16)

    return pl.pallas_call(
        body, name=name, grid=(nn // tn,),
        in_specs=[pl.BlockSpec((rows, kk), lambda n: (0, 0)),
                  pl.BlockSpec((rows, tn), lambda n: (0, n))],
        out_specs=pl.BlockSpec((kk, tn), lambda n: (0, n)),
        out_shape=jax.ShapeDtypeStruct((kk, nn), BF16),
        scratch_shapes=[pltpu.VMEM((kk, rows), BF16)],
        compiler_params=_cparams("arbitrary"),
    )(a, b)


def dw_mlp(u, dhb, z2, dup, tf):
    rows, dff = u.shape
    d = z2.shape[1]
    nf = dff // tf

    def body(u_ref, dh_ref, z_ref, dup_ref, dwd_ref, dwu_ref, zt):
        @pl.when(pl.program_id(0) == 0)
        def _():
            zt[...] = z_ref[...].T

        uf = u_ref[...].astype(F32)
        dwd_ref[...] = _dot_tn((uf * uf).astype(BF16), dh_ref[...]).astype(BF16)
        dwu_ref[...] = _dot(zt[...], dup_ref[...]).astype(BF16)

    col = lambda j: (0, j)
    fix = lambda j: (0, 0)
    return pl.pallas_call(
        body, name="dw_mlp", grid=(nf,),
        in_specs=[pl.BlockSpec((rows, tf), col), pl.BlockSpec((rows, d), fix),
                  pl.BlockSpec((rows, d), fix), pl.BlockSpec((rows, tf), col)],
        out_specs=[pl.BlockSpec((None, tf, d), lambda j: (j, 0, 0)),
                   pl.BlockSpec((None, d, tf), lambda j: (j, 0, 0))],
        out_shape=[jax.ShapeDtypeStruct((nf, tf, d), BF16), jax.ShapeDtypeStruct((nf, d, tf), BF16)],
        scratch_shapes=[pltpu.VMEM((d, rows), BF16)],
        compiler_params=_cparams("arbitrary"),
    )(u, dhb, z2, dup)


def out_proj_bwd(dh2, o, rec, ga, gr, wout, tm):
    tp, d = dh2.shape
    aw, rw = o.shape[1], rec.shape[1]

    def body(dh_ref, o_ref, rec_ref, ga_ref, gr_ref, w_ref, do_ref, drec_ref, dga_ref, dgr_ref):
        i = pl.program_id(0)
        dmix = _dot_nt(dh_ref[...].astype(BF16), w_ref[...])
        do, dga = _rms_bwd(o_ref[...], ga_ref[...], dmix[:, 0:aw])
        drec, dgr = _rms_bwd(rec_ref[...], gr_ref[...], dmix[:, aw:aw + rw])
        do_ref[...] = do
        drec_ref[...] = drec
        _accumulate(dga_ref, dga, i == 0)
        _accumulate(dgr_ref, dgr, i == 0)

    row = lambda i: (i, 0)
    fix = lambda i: (0, 0)
    return pl.pallas_call(
        body, name="out_proj_bwd", grid=(tp // tm,),
        in_specs=[pl.BlockSpec((tm, d), row), pl.BlockSpec((tm, aw), row), pl.BlockSpec((tm, rw), row),
                  pl.BlockSpec((1, aw), fix), pl.BlockSpec((1, rw), fix), pl.BlockSpec((d, d), fix)],
        out_specs=[pl.BlockSpec((tm, aw), row), pl.BlockSpec((tm, rw), row),
                   pl.BlockSpec((1, aw), fix), pl.BlockSpec((1, rw), fix)],
        out_shape=[jax.ShapeDtypeStruct((tp, aw), F32), jax.ShapeDtypeStruct((tp, rw), F32),
                   jax.ShapeDtypeStruct((1, aw), F32), jax.ShapeDtypeStruct((1, rw), F32)],
        compiler_params=_cparams("arbitrary"),
    )(dh2, o, rec, ga, gr, wout)


def rec_bwd(drec, hr, xc, gates, rest, convw, wga, wgx, lru, rw):
    tp = rest.shape[0]
    ng = rw // LANES

    def body(drec_ref, hr_ref, xc_ref, r_ref, ig_ref, a_ref, mult_ref, xr_ref, yr_ref, cw_ref, wga_ref, wgx_ref,
             l_ref, dxr_ref, dyr_ref, dwga_ref, dwgx_ref, vec_ref, a_s, u_s, lam_s):
        xc = xc_ref[...]
        h = hr_ref[...]
        drec = drec_ref[...]
        r, ig, a, mult = r_ref[...], ig_ref[...], a_ref[...], mult_ref[...]
        xcb = xc.astype(BF16)
        ls = _log_sigmoid(l_ref[...])
        gelu, gelu_grad = _gelu_and_grad(yr_ref[...])
        dyr_ref[...] = (drec * h * gelu_grad).astype(BF16)
        a_s[...] = _shift_up(a, 1, tp)
        u_s[...] = drec * gelu
        _scan_rows(a_s, u_s, lam_s, tp, reverse=True)
        lam = lam_s[...]
        da = lam * _shift_down(h, 1, tp)
        dmult = lam * ig * xc
        dig = lam * mult * xc
        dxc = lam * mult * ig
        dlog_a = da * a - dmult * (a * a) / mult
        dr = dlog_a * (RG_C * ls)
        dl = jnp.sum(dlog_a * (RG_C * r), axis=0, keepdims=True) * _sigmoid(-l_ref[...])
        dpa = dr * r * (1.0 - r)
        dpx = dig * ig * (1.0 - ig)
        dpab = dpa.astype(BF16)
        dpxb = dpx.astype(BF16)
        dxc = dxc + _dot_nt(dpab, wga_ref[...]) + _dot_nt(dpxb, wgx_ref[...])
        dwga_ref[...] = _dot_tn(xcb, dpab)
        dwgx_ref[...] = _dot_tn(xcb, dpxb)
        xr = xr_ref[...]
        dxr = cw_ref[CONV_WIDTH - 1:CONV_WIDTH, :] * dxc
        for k in range(1, CONV_WIDTH):
            dxr = dxr + cw_ref[CONV_WIDTH - 1 - k:CONV_WIDTH - k, :] * _shift_up(dxc, k, tp)
        dxr_ref[...] = dxr.astype(BF16)
        for k in range(CONV_WIDTH):
            vec_ref[k:k + 1, :] = jnp.sum(dxc * _shift_down(xr, CONV_WIDTH - 1 - k, tp), axis=0, keepdims=True)
        vec_ref[4:5, :] = jnp.sum(dxc, axis=0, keepdims=True)
        vec_ref[5:6, :] = jnp.sum(dpa, axis=0, keepdims=True)
        vec_ref[6:7, :] = jnp.sum(dpx, axis=0, keepdims=True)
        vec_ref[7:8, :] = dl

    col = lambda g: (0, g)
    vec = pl.BlockSpec((1, LANES), col)
    big = pl.BlockSpec((tp, LANES), col)
    sq = pl.BlockSpec((None, LANES, LANES), lambda g: (g, 0, 0))
    return pl.pallas_call(
        body, name="rec_bwd", grid=(ng,),
        in_specs=[big] * 8 + [pl.BlockSpec((tp, LANES), lambda g: (0, ng + g)),
                                pl.BlockSpec((CONV_WIDTH, LANES), col), sq, sq, vec],
        out_specs=[big, big, sq, sq, pl.BlockSpec((None, SUBLANES, LANES), lambda g: (g, 0, 0))],
        out_shape=[jax.ShapeDtypeStruct((tp, rw), BF16), jax.ShapeDtypeStruct((tp, rw), BF16),
                   jax.ShapeDtypeStruct((ng, LANES, LANES), F32), jax.ShapeDtypeStruct((ng, LANES, LANES), F32),
                   jax.ShapeDtypeStruct((ng, SUBLANES, LANES), F32)],
        scratch_shapes=[pltpu.VMEM((tp, LANES), F32)] * 3,
        compiler_params=_cparams("parallel"),
    )(drec, hr, xc, *gates, rest, rest, convw, wga, wgx, lru)


def attn_bwd(qkv, do, o, lset, c, ct, nh):
    tp = qkv.shape[0]
    npair = nh // 2
    aw = nh * HEAD_DIM
    tiles = _att_tiles(tp)

    def body(q_ref, k_ref, v_ref, do_ref, o_ref, lset_ref, c_ref, ct_ref,
             dq_ref, dk_ref, dv_ref, drow_ref, dcol_ref, dk_acc, dv_acc, dq_t):
        p = pl.program_id(0)
        k_t = k_ref[...].T
        dk_acc[...] = jnp.zeros_like(dk_acc)
        dv_acc[...] = jnp.zeros_like(dv_acc)
        dcol_ref[...] = jnp.zeros_like(dcol_ref)
        drow_ref[...] = jnp.zeros_like(drow_ref)
        for r0, nr, nk in tiles:
            rs = slice(r0, r0 + nr)
            causal = (r0 + lax.broadcasted_iota(jnp.int32, (nk, nr), 1)
                      >= lax.broadcasted_iota(jnp.int32, (nk, nr), 0))
            cblk = c_ref[0:nk, :]
            ctb = ct_ref[:, rs]
            for hh in range(2):
                head = 2 * p + hh
                hs = slice(hh * HEAD_DIM, (hh + 1) * HEAD_DIM)
                q = q_ref[rs, hs]
                k = k_ref[0:nk, hs]
                dof = do_ref[rs, hs]
                do16 = dof.astype(BF16)
                delta = jnp.sum(dof * o_ref[rs, hs], axis=1, keepdims=True)
                delta_row = jnp.broadcast_to(delta, (nr, LANES)).T[0:1, :]
                s_t = _dot_nt(k, q * ATT_SCALE) + (_pick_row(ctb, head) - _pick_col(cblk, head))
                p_t = jnp.where(causal, jnp.exp(s_t - lset_ref[hh:hh + 1, rs]), 0.0)
                ds_t = p_t * (_dot_nt(v_ref[0:nk, hs], do16) - delta_row)
                p16 = p_t.astype(BF16)
                ds16 = ds_t.astype(BF16)
                dv_acc[0:nk, hs] += _dot(p16, do16)
                dk_acc[0:nk, hs] += _dot(ds16, q) * ATT_SCALE
                dq_t[hs, rs] = _dot(k_t[hs, 0:nk], ds16)
                drow_ref[hh:hh + 1, rs] = jnp.sum(ds_t, axis=0, keepdims=True)
                dcol_ref[0:nk, hs] -= jnp.broadcast_to(jnp.sum(ds_t, axis=1, keepdims=True), (nk, HEAD_DIM))
        dk_ref[...] = dk_acc[...].astype(BF16)
        dv_ref[...] = dv_acc[...].astype(BF16)
        dq_ref[...] = (dq_t[...].T * ATT_SCALE).astype(BF16)

    pair = lambda p: (0, p)
    return pl.pallas_call(
        body, name="attn_bwd", grid=(npair,),
        in_specs=[pl.BlockSpec((tp, LANES), pair),
                  pl.BlockSpec((tp, LANES), lambda p: (0, npair + p)),
                  pl.BlockSpec((tp, LANES), lambda p: (0, 2 * npair + p)),
                  pl.BlockSpec((tp, LANES), pair),
                  pl.BlockSpec((tp, LANES), pair),
                  pl.BlockSpec((None, SUBLANES, tp), lambda p: (p, 0, 0)),
                  pl.BlockSpec((tp, LANES), lambda p: (0, 0)),
                  pl.BlockSpec((SUBLANES, tp), lambda p: (0, 0))],
        out_specs=[pl.BlockSpec((tp, LANES), pair), pl.BlockSpec((tp, LANES), pair),
                   pl.BlockSpec((tp, LANES), pair),
                   pl.BlockSpec((None, SUBLANES, tp), lambda p: (p, 0, 0)),
                   pl.BlockSpec((tp, LANES), pair)],
        out_shape=[jax.ShapeDtypeStruct((tp, aw), BF16), jax.ShapeDtypeStruct((tp, aw), BF16),
                   jax.ShapeDtypeStruct((tp, aw), BF16),
                   jax.ShapeDtypeStruct((npair, SUBLANES, tp), F32),
                   jax.ShapeDtypeStruct((tp, aw), F32)],
        scratch_shapes=[pltpu.VMEM((tp, LANES), F32), pltpu.VMEM((tp, LANES), F32),
                        pltpu.VMEM((LANES, tp), F32)],
        compiler_params=_cparams("parallel"),
    )(qkv, qkv, qkv, do, o, lset, c, ct)


def fgate_bwd(dct8, drs, rest, bf_pad, fcol):
    tp = rest.shape[0]
    aw = drs.shape[1]
    nb = tp // ATT_BLOCK
    B = ATT_BLOCK

    def body(d_ref, drs_ref, f_ref, b_ref, dfl_ref, db_ref, pad_s):
        r_i = lax.broadcasted_iota(jnp.int32, (B, B), 0)
        c_i = lax.broadcasted_iota(jnp.int32, (B, B), 1)
        triu = (c_i >= r_i).astype(BF16)
        sel = (lax.broadcasted_iota(jnp.int32, (aw, LANES), 0)
               == HEAD_DIM * lax.broadcasted_iota(jnp.int32, (aw, LANES), 1)).astype(BF16)
        carry = jnp.zeros((1, LANES), F32)
        db = jnp.zeros((1, LANES), F32)
        pad_s[...] = jnp.zeros_like(pad_s)
        for i in range(nb - 1, -1, -1):
            sl = slice(i * B, (i + 1) * B)
            pad_s[0:SUBLANES, :] = d_ref[:, sl]
            dc = pad_s[...].T + _dot_split3(drs_ref[sl, :], sel)
            rc = _split3_dot(triu, dc)
            dlf = rc + carry
            carry = carry + rc[0:1, :]
            dfl = dlf * _sigmoid(-(f_ref[sl, :] + b_ref[...]))
            dfl_ref[sl, :] = dfl.astype(BF16)
            db = db + jnp.sum(dfl, axis=0, keepdims=True)
        db_ref[...] = db

    return pl.pallas_call(
        body, name="fgate_bwd", grid=(1,),
        in_specs=[pl.BlockSpec((SUBLANES, tp), lambda i: (0, 0)),
                  pl.BlockSpec((tp, aw), lambda i: (0, 0)),
                  pl.BlockSpec((tp, LANES), lambda i: (0, fcol)),
                  pl.BlockSpec((1, LANES), lambda i: (0, 0))],
        out_specs=[pl.BlockSpec((tp, LANES), lambda i: (0, 0)),
                   pl.BlockSpec((1, LANES), lambda i: (0, 0))],
        out_shape=[jax.ShapeDtypeStruct((tp, LANES), BF16), jax.ShapeDtypeStruct((1, LANES), F32)],
        scratch_shapes=[pltpu.VMEM((B, B), F32)],
        compiler_params=_cparams("arbitrary"),
    )(dct8, drs, rest, bf_pad)


def in_proj_bwd(dh2, parts, w_in_t, wrest_t, h, g1, tm):
    tp, d = h.shape
    dq, dk, dv, dxr, dyr, dfl = parts
    aw, rw = dq.shape[1], dxr.shape[1]

    def body(dh2_ref, dq_ref, dk_ref, dv_ref, dxr_ref, dyr_ref, dfl_ref, wq_ref, wr_ref, h_ref, g_ref,
             dh_ref, dg_ref):
        i = pl.program_id(0)
        dz = _dot(dq_ref[...], wq_ref[0:aw, :])
        dz += _dot(dk_ref[...], wq_ref[aw:2 * aw, :])
        dz += _dot(dv_ref[...], wq_ref[2 * aw:3 * aw, :])
        dz += _dot(dxr_ref[...], wr_ref[0:rw, :])
        dz += _dot(dyr_ref[...], wr_ref[rw:2 * rw, :])
        dz += _dot(dfl_ref[...], wr_ref[2 * rw:2 * rw + LANES, :])
        dx, dg = _rms_bwd(h_ref[...], g_ref[...], dz)
        dh_ref[...] = dh2_ref[...] + dx
        _accumulate(dg_ref, dg, i == 0)

    row = lambda i: (i, 0)
    fix = lambda i: (0, 0)
    return pl.pallas_call(
        body, name="in_proj_bwd", grid=(tp // tm,),
        in_specs=[pl.BlockSpec((tm, d), row),
                  pl.BlockSpec((tm, aw), row), pl.BlockSpec((tm, aw), row), pl.BlockSpec((tm, aw), row),
                  pl.BlockSpec((tm, rw), row), pl.BlockSpec((tm, rw), row), pl.BlockSpec((tm, LANES), row),
                  pl.BlockSpec((3 * aw, d), fix), pl.BlockSpec(wrest_t.shape, fix),
                  pl.BlockSpec((tm, d), row), pl.BlockSpec((1, d), fix)],
        out_specs=[pl.BlockSpec((tm, d), row), pl.BlockSpec((1, d), fix)],
        out_shape=[jax.ShapeDtypeStruct((tp, d), F32), jax.ShapeDtypeStruct((1, d), F32)],
        compiler_params=_cparams("arbitrary"),
    )(dh2, dq, dk, dv, dxr, dyr, dfl, w_in_t, wrest_t, h, g1)


def dw_in_t(z, parts, nh, tr):
    tp, d = z.shape
    dq, dk, dv, dxr, dyr, dfl = parts
    aw, rw = dq.shape[1], dxr.shape[1]
    d_in = 3 * aw + nh + 2 * rw
    nr = tp // tr
    offs = [(0, aw), (aw, aw), (2 * aw, aw), (3 * aw + nh, rw), (3 * aw + nh + rw, rw)]

    def body(z_ref, dq_ref, dk_ref, dv_ref, dxr_ref, dyr_ref, dfl_ref, o_ref, acc):
        r = pl.program_id(0)

        @pl.when(r == 0)
        def _():
            acc[...] = jnp.zeros_like(acc)

        zt = z_ref[...]
        for (o, n), ref in zip(offs, (dq_ref, dk_ref, dv_ref, dxr_ref, dyr_ref)):
            acc[o:o + n, :] += _dot_tn(ref[...], zt)
        acc[3 * aw:3 * aw + nh, :] += _dot_tn(dfl_ref[...], zt)[0:nh, :]

        @pl.when(r == nr - 1)
        def _():
            o_ref[...] = acc[...].astype(BF16)

    row = lambda r: (r, 0)
    return pl.pallas_call(
        body, name="dw_in", grid=(nr,),
        in_specs=[pl.BlockSpec((tr, d), row),
                  pl.BlockSpec((tr, aw), row), pl.BlockSpec((tr, aw), row), pl.BlockSpec((tr, aw), row),
                  pl.BlockSpec((tr, rw), row), pl.BlockSpec((tr, rw), row), pl.BlockSpec((tr, LANES), row)],
        out_specs=pl.BlockSpec((d_in, d), lambda r: (0, 0)),
        out_shape=jax.ShapeDtypeStruct((d_in, d), BF16),
        scratch_shapes=[pltpu.VMEM((d_in, d), F32)],
        compiler_params=_cparams("arbitrary"),
    )(z, dq, dk, dv, dxr, dyr, dfl)


def _place():
    return lax.axis_index("x"), lax.axis_index("y"), lax.axis_index("c")


HBM = pl.BlockSpec(memory_space=pltpu.HBM)
SEM = pl.BlockSpec(memory_space=pltpu.SEMAPHORE)
EFFECT = pltpu.SideEffectType.DATAFLOW_SIDE_EFFECTING


def _in_hbm(a):
    return pltpu.with_memory_space_constraint(a, pltpu.HBM)


def _as_list(a):
    return list(a) if isinstance(a, (list, tuple)) else [a]


def _gather_targets(x, y, c):
    return [(x, y, 1 - c), (1 - x, y, c), (x, 1 - y, c), (1 - x, 1 - y, c)]


def _slot(t):
    return 4 * t[0] + 2 * t[1] + t[2]


def gather_start(groups, name):
    flat = [a for g in groups for a in g]
    n = len(flat)
    ng = len(groups)
    lands = [lax.empty((N_DEV,) + a.shape, a.dtype) for a in flat]

    def body(*refs):
        src, land = refs[:n], refs[n:2 * n]
        sems = refs[2 * n:2 * n + 2 * ng]
        token = refs[-1]
        x, y, c = _place()
        me = 4 * x + 2 * y + c
        i = 0
        for gi, g in enumerate(groups):
            for a in range(len(g)):
                for k, t in enumerate(_gather_targets(x, y, c)):
                    pltpu.make_async_remote_copy(
                        src_ref=src[i], dst_ref=land[i].at[me],
                        send_sem=sems[2 * gi].at[4 * a + k], recv_sem=sems[2 * gi + 1].at[4 * a + k],
                        device_id=t, device_id_type=MESH).start()
                i += 1
        token[...] = jnp.zeros_like(token)

    sem_shapes = []
    for g in groups:
        sem_shapes += [pltpu.SemaphoreType.DMA((4 * len(g),)), pltpu.SemaphoreType.DMA((4 * len(g),))]
    out = pl.pallas_call(
        body, name=name,
        out_shape=sem_shapes + [pltpu.HBM(a.shape, a.dtype) for a in flat + lands]
        + [jax.ShapeDtypeStruct((SUBLANES, LANES), F32)],
        in_specs=[HBM] * (2 * n),
        out_specs=[SEM] * (2 * ng) + [HBM] * (2 * n) + [pl.BlockSpec(memory_space=pltpu.VMEM)],
        input_output_aliases={i: 2 * ng + i for i in range(2 * n)},
        compiler_params=pltpu.CompilerParams(has_side_effects=EFFECT),
    )(*[_in_hbm(a) for a in flat + lands])
    sems = out[:2 * ng]
    thru = out[2 * ng:2 * ng + 2 * n]
    srcs_t, lands_t = thru[:n], thru[n:]
    res, i = [], 0
    for gi, g in enumerate(groups):
        res.append((sems[2 * gi], sems[2 * gi + 1], srcs_t[i:i + len(g)], lands_t[i:i + len(g)]))
        i += len(g)
    return res, out[-1]


def gather_wait(send, recv, srcs, lands, after, name):
    n = len(srcs)

    def body(*refs):
        src, land = refs[:n], refs[n:2 * n]
        send_sem, recv_sem = refs[2 * n], refs[2 * n + 1]
        x, y, c = _place()
        for a in range(n):
            for k, t in enumerate(_gather_targets(x, y, c)):
                cp = pltpu.make_async_remote_copy(
                    src_ref=src[a], dst_ref=land[a].at[_slot(t)],
                    send_sem=send_sem.at[4 * a + k], recv_sem=recv_sem.at[4 * a + k],
                    device_id=t, device_id_type=MESH)
                cp.wait_send()
                cp.wait_recv()

    out = pl.pallas_call(
        body, name=name,
        out_shape=[pltpu.HBM(a.shape, a.dtype) for a in list(srcs) + list(lands)],
        in_specs=[HBM] * (2 * n) + [SEM, SEM] + [ANY] * len(_as_list(after)),
        out_specs=[HBM] * (2 * n),
        input_output_aliases={i: i for i in range(2 * n)},
        compiler_params=pltpu.CompilerParams(has_side_effects=EFFECT),
    )(*srcs, *lands, send, recv, *_as_list(after))
    return out[:n], out[n:]


def forward_start(lands, name):
    n = len(lands)

    def body(*refs):
        land = refs[:n]
        send_sem, recv_sem = refs[n], refs[n + 1]
        token = refs[-1]
        x, y, c = _place()
        for a in range(n):
            for j, chip in enumerate([(1 - x, y), (x, 1 - y), (1 - x, 1 - y)]):
                blk = land[a].at[_slot((*chip, c))]
                pltpu.make_async_remote_copy(src_ref=blk, dst_ref=blk, send_sem=send_sem.at[3 * a + j],
                                             recv_sem=recv_sem.at[3 * a + j], device_id=(x, y, 1 - c),
                                             device_id_type=MESH).start()
        token[...] = jnp.zeros_like(token)

    out = pl.pallas_call(
        body, name=name,
        out_shape=[pltpu.SemaphoreType.DMA((3 * n,)), pltpu.SemaphoreType.DMA((3 * n,))]
        + [pltpu.HBM(a.shape, a.dtype) for a in lands] + [jax.ShapeDtypeStruct((SUBLANES, LANES), F32)],
        in_specs=[HBM] * n,
        out_specs=[SEM, SEM] + [HBM] * n + [pl.BlockSpec(memory_space=pltpu.VMEM)],
        input_output_aliases={i: 2 + i for i in range(n)},
        compiler_params=pltpu.CompilerParams(has_side_effects=EFFECT),
    )(*[_in_hbm(a) for a in lands])
    return out[0], out[1], out[2:2 + n], out[-1][0, 0]


def forward_wait(send, recv, lands, after, name):
    n = len(lands)

    def body(*refs):
        land = refs[:n]
        send_sem, recv_sem = refs[n], refs[n + 1]
        x, y, c = _place()
        for a in range(n):
            for j, chip in enumerate([(1 - x, y), (x, 1 - y), (1 - x, 1 - y)]):
                cp = pltpu.make_async_remote_copy(
                    src_ref=land[a].at[_slot((*chip, c))], dst_ref=land[a].at[_slot((*chip, 1 - c))],
                    send_sem=send_sem.at[3 * a + j], recv_sem=recv_sem.at[3 * a + j],
                    device_id=(x, y, 1 - c), device_id_type=MESH)
                cp.wait_send()
                cp.wait_recv()

    return pl.pallas_call(
        body, name=name,
        out_shape=[pltpu.HBM(a.shape, a.dtype) for a in lands],
        in_specs=[HBM] * n + [SEM, SEM, ANY],
        out_specs=[HBM] * n,
        input_output_aliases={i: i for i in range(n)},
        compiler_params=pltpu.CompilerParams(has_side_effects=EFFECT),
    )(*lands, send, recv, after)


def _relations():
    return [(dx, dy, dc) for dx in (0, 1) for dy in (0, 1) for dc in (0, 1) if dx + dy + dc]


def _peer(x, y, c, rel):
    return ((1 - x) if rel[0] else x, (1 - y) if rel[1] else y, (1 - c) if rel[2] else c)


def exchange_start(srcs, lands, layer, name, after=()):
    n = len(srcs)
    after = _as_list(after)

    def body(*refs):
        src, land = refs[:n], refs[n:2 * n]
        send_sem, recv_sem = refs[2 * n + len(after)], refs[2 * n + len(after) + 1]
        token = refs[-1]
        x, y, c = _place()
        me = 4 * x + 2 * y + c
        for k, rel in enumerate(_relations()):
            peer = _peer(x, y, c, rel)
            for a in range(n):
                pltpu.make_async_remote_copy(
                    src_ref=src[a] if layer is None else src[a].at[_slot(peer)],
                    dst_ref=land[a].at[me] if layer is None else land[a].at[me, layer],
                    send_sem=send_sem.at[7 * a + k], recv_sem=recv_sem.at[7 * a + k],
                    device_id=peer, device_id_type=MESH).start()
        token[...] = jnp.zeros_like(token)

    out = pl.pallas_call(
        body, name=name,
        out_shape=[pltpu.SemaphoreType.DMA((7 * n,)), pltpu.SemaphoreType.DMA((7 * n,))]
        + [pltpu.HBM(a.shape, a.dtype) for a in list(srcs) + list(lands)]
        + [jax.ShapeDtypeStruct((SUBLANES, LANES), F32)],
        in_specs=[HBM] * (2 * n) + [ANY] * len(after),
        out_specs=[SEM, SEM] + [HBM] * (2 * n) + [pl.BlockSpec(memory_space=pltpu.VMEM)],
        input_output_aliases={i: 2 + i for i in range(2 * n)},
        compiler_params=pltpu.CompilerParams(has_side_effects=EFFECT),
    )(*[_in_hbm(a) for a in list(srcs) + list(lands)], *after)
    return out[0], out[1], out[2:2 + n], out[2 + n:2 + 2 * n], out[-1][0, 0]


def exchange_wait(send, recv, srcs, lands, after, layer, name):
    n = len(srcs)

    def body(*refs):
        src, land = refs[:n], refs[n:2 * n]
        send_sem, recv_sem = refs[2 * n], refs[2 * n + 1]
        x, y, c = _place()
        for k, rel in enumerate(_relations()):
            peer = _peer(x, y, c, rel)
            for a in range(n):
                cp = pltpu.make_async_remote_copy(
                    src_ref=src[a] if layer is None else src[a].at[_slot(peer)],
                    dst_ref=land[a].at[_slot(peer)] if layer is None else land[a].at[_slot(peer), layer],
                    send_sem=send_sem.at[7 * a + k], recv_sem=recv_sem.at[7 * a + k],
                    device_id=peer, device_id_type=MESH)
                cp.wait_send()
                cp.wait_recv()

    out = pl.pallas_call(
        body, name=name,
        out_shape=[pltpu.HBM(a.shape, a.dtype) for a in list(srcs) + list(lands)],
        in_specs=[HBM] * (2 * n) + [SEM, SEM] + [ANY] * len(_as_list(after)),
        out_specs=[HBM] * (2 * n),
        input_output_aliases={i: i for i in range(2 * n)},
        compiler_params=pltpu.CompilerParams(has_side_effects=EFFECT),
    )(*srcs, *lands, send, recv, *_as_list(after))
    return out[:n], out[n:]


def _adamw_math(g, w, m, v):
    m = ADAM_B1 * m + (1.0 - ADAM_B1) * g
    v = ADAM_B2 * v + (1.0 - ADAM_B2) * (g * g)
    m_hat = m / (1.0 - ADAM_B1 ** ADAM_STEP)
    v_hat = v / (1.0 - ADAM_B2 ** ADAM_STEP)
    delta = -ADAM_LR * (m_hat / (jnp.sqrt(v_hat) + ADAM_EPS) + ADAM_WD * w)
    return delta, m, v


def _sum_with_own(p_ref, own_refs, layer, me):
    own = own_refs[0][...]
    for k in range(1, len(own_refs)):
        own = jnp.where(layer == k, own_refs[k][...], own)
    g = None
    for p in range(p_ref.shape[0]):
        term = jnp.where(me == p, own, p_ref[p]).astype(F32)
        g = term if g is None else g + term
    return g


def sum_adamw(parts, owns, me, w, m, v, tr, name):
    npart, rows, cols = parts.shape
    nl = len(owns)
    per_layer = rows // nl // tr

    def body(me_ref, p_ref, *refs):
        own_refs = refs[:nl]
        w_ref, m_ref, v_ref, g_ref, d_ref, nm_ref, nv_ref = refs[nl:]
        g = _sum_with_own(p_ref, own_refs, pl.program_id(0) // per_layer, me_ref[0])
        delta, nm, nv = _adamw_math(g, w_ref[...], m_ref[...], v_ref[...])
        g_ref[...] = g
        d_ref[...] = delta
        nm_ref[...] = nm
        nv_ref[...] = nv

    blk = pl.BlockSpec((tr, cols), lambda i, me_ref: (i, 0))
    own_specs = [pl.BlockSpec((None, tr, cols),
                              lambda i, me_ref, l=l: (me_ref[0], jnp.clip(i - l * per_layer, 0, per_layer - 1), 0))
                 for l in range(nl)]
    return pl.pallas_call(
        body, name=name,
        grid_spec=pltpu.PrefetchScalarGridSpec(
            num_scalar_prefetch=1, grid=(rows // tr,),
            in_specs=[pl.BlockSpec((npart, tr, cols), lambda i, me_ref: (0, i, 0))] + own_specs + [blk, blk, blk],
            out_specs=[blk] * 4),
        out_shape=[jax.ShapeDtypeStruct((rows, cols), F32)] * 4,
        compiler_params=_cparams("arbitrary"),
    )(me, parts, *owns, w, m, v)


def sum_adamw_t(parts, owns, me, w, m, v, name):
    npart, nl, rows, cols = parts.shape

    def body(me_ref, p_ref, *refs):
        own_refs = refs[:nl]
        w_ref, m_ref, v_ref, g_ref, d_ref, nm_ref, nv_ref = refs[nl:]
        g = _sum_with_own(p_ref, own_refs, pl.program_id(0), me_ref[0])
        delta, nm, nv = _adamw_math(g, w_ref[...], m_ref[...], v_ref[...])
        g_ref[...] = g
        d_ref[...] = delta
        nm_ref[...] = nm
        nv_ref[...] = nv

    blk = pl.BlockSpec((None, rows, cols), lambda l, me_ref: (l, 0, 0))
    own_specs = [pl.BlockSpec((None, rows, cols), lambda l, me_ref: (me_ref[0], 0, 0)) for _ in range(nl)]
    return pl.pallas_call(
        body, name=name,
        grid_spec=pltpu.PrefetchScalarGridSpec(
            num_scalar_prefetch=1, grid=(nl,),
            in_specs=[pl.BlockSpec((npart, None, rows, cols), lambda l, me_ref: (0, l, 0, 0))] + own_specs
            + [blk, blk, blk],
            out_specs=[blk] * 4),
        out_shape=[jax.ShapeDtypeStruct((nl, rows, cols), F32)] * 4,
        compiler_params=_cparams("arbitrary"),
    )(me, parts, *owns, w, m, v)


def adamw_group(gs, ws, ms, vs, name):
    n = len(gs)

    def body(*refs):
        g, w, m, v, outs = refs[:n], refs[n:2 * n], refs[2 * n:3 * n], refs[3 * n:4 * n], refs[4 * n:]
        for i in range(n):
            delta, nm, nv = _adamw_math(g[i][...], w[i][...], m[i][...], v[i][...])
            outs[i][...] = delta
            outs[n + i][...] = nm
            outs[2 * n + i][...] = nv

    vmem = pl.BlockSpec(memory_space=pltpu.VMEM)
    out = pl.pallas_call(
        body, name=name,
        in_specs=[vmem] * (4 * n), out_specs=[vmem] * (3 * n),
        out_shape=[jax.ShapeDtypeStruct(a.shape, F32) for a in list(ws) * 3],
        compiler_params=_cparams(),
    )(*gs, *ws, *ms, *vs)
    return out[:n], out[n:2 * n], out[2 * n:]


def sum_parts(parts, name):
    npart, rows, cols = parts.shape

    def body(p_ref, g_ref):
        g = p_ref[0].astype(F32)
        for p in range(1, npart):
            g = g + p_ref[p].astype(F32)
        g_ref[...] = g

    return pl.pallas_call(
        body, name=name, grid=(1,),
        in_specs=[pl.BlockSpec((npart, rows, cols), lambda i: (0, 0, 0))],
        out_specs=pl.BlockSpec((rows, cols), lambda i: (0, 0)),
        out_shape=jax.ShapeDtypeStruct((rows, cols), F32),
        compiler_params=_cparams("arbitrary"),
    )(parts)


def _round_up(n, m):
    return (n + m - 1) // m * m


def _block_diag_pairs(w):
    nb, b, _ = w.shape
    per = LANES // b
    ng = nb // per
    w = w.reshape(ng, per, b, b)
    eye = jnp.eye(per, dtype=w.dtype)
    out = jnp.einsum('gpij,pq->gpiqj', w, eye).reshape(ng, LANES, LANES)
    return out.astype(BF16)


def _block_diag_extract(g, b):
    ng = g.shape[0]
    per = LANES // b
    g = g.reshape(ng, per, b, per, b)
    idx = jnp.arange(per)
    return g[:, idx, :, idx, :].transpose(1, 0, 2, 3).reshape(ng * per, b, b)


def _tiles(v):
    v = v.reshape(-1)
    n = _round_up(v.shape[0], SUBLANES * LANES)
    return jnp.pad(v, (0, n - v.shape[0])).reshape(-1, LANES)


SMALL = ['attn_norm_g', 'b_f', 'conv_w', 'conv_b', 'w_gate_a', 'b_gate_a', 'w_gate_x', 'b_gate_x',
         'lru_L', 'attn_out_g', 'rec_out_g', 'mlp_norm_g', 'final_g', 'meta']


def _pack(d):
    return jnp.concatenate([_tiles(d[n]) for n in SMALL], axis=0)


def _unpack(vec, shapes):
    out, r = {}, 0
    for n in SMALL:
        size = math.prod(shapes[n])
        nr = _round_up(size, SUBLANES * LANES) // LANES
        out[n] = vec[r:r + nr].reshape(-1)[:size].reshape(shapes[n])
        r += nr
    return out


def _row_tile(tp):
    return tp // 4 if (tp // 4) % 16 == 0 else tp


def local_step(x, tgt, meta, small, hooks):
    s, d = x.shape
    t_real = s + N_META
    tp = _round_up(t_real, ATT_BLOCK)
    depth = small['attn_norm_g'].shape[0]
    nh = small['b_f'].shape[1]
    rw = small['conv_b'].shape[1]
    blk = small['w_gate_a'].shape[2]
    tm = _row_tile(tp)
    tm2 = tp // 2
    fcol = 2 * rw // LANES

    h = jnp.concatenate([meta, x, jnp.zeros((tp - t_real, d), F32)], axis=0)
    tgt_p = jnp.pad(tgt, ((N_META, tp - t_real), (0, 0)))
    row = lambda v: v.reshape(1, -1)
    bf_pad = jnp.pad(small['b_f'], ((0, 0), (0, LANES - nh)))

    saved = []
    for l in range(depth):
        w_in_t, wrest_t, wout, tok_w = hooks.mixer_weights(l, h)
        wga = _block_diag_pairs(small['w_gate_a'][l])
        wgx = _block_diag_pairs(small['w_gate_x'][l])
        z, qkv, rest = in_proj(h, row(small['attn_norm_g'][l]) + tok_w, w_in_t, wrest_t, 3 * nh * HEAD_DIM, tm)
        c, ct = fgate_fwd(rest, bf_pad[l:l + 1], fcol)
        o, lset = attn_fwd(qkv, c, ct, nh)
        rec, hr, xc, *gates = rec_fwd(rest, small['conv_w'][l], row(small['conv_b'][l]), wga,
                                      row(small['b_gate_a'][l]), wgx, row(small['b_gate_x'][l]),
                                      row(small['lru_L'][l]), rw)
        gup, gdown, tok_w = hooks.mlp_weights(l, rec)
        h2, mix, z2 = out_proj(h, o, rec, row(small['attn_out_g'][l]), row(small['rec_out_g'][l]), wout,
                               row(small['mlp_norm_g'][l]) + tok_w, tm)
        u, h3 = mlp_fwd(z2, h2, gup, gdown, tm)
        saved.append(dict(h=h, z=z, qkv=qkv, rest=rest, c=c, ct=ct, o=o, lset=lset, rec=rec, hr=hr, xc=xc,
                          h2=h2, mix=mix, z2=z2, u=u, wga=wga, wgx=wgx, gates=gates,
                          w_in_t=w_in_t, wrest_t=wrest_t, wout=wout, gup=gup, gdown=gdown))
        h = h3

    dh, dgf, loss = loss_head(h, row(small['final_g']), tgt_p, t_real, tm)

    gs = {n: [None] * depth for n in SMALL if n not in ('final_g', 'meta')}
    tok = jnp.zeros((), F32)
    for l in reversed(range(depth)):
        sv = saved[l]
        gup, gdown = sv['gup'], sv['gdown']
        tf = gup.shape[2]
        dup, dh2, dg2, dhb = mlp_bwd(dh, sv['u'], sv['h2'], row(small['mlp_norm_g'][l]) + tok, gup, gdown, tm)
        gs['mlp_norm_g'][l] = dg2[0]
        do, drec, dga, dgr = out_proj_bwd(dh2, sv['o'], sv['rec'], row(small['attn_out_g'][l]),
                                          row(small['rec_out_g'][l]), sv['wout'], tm)
        gs['attn_out_g'][l] = dga[0]
        gs['rec_out_g'][l] = dgr[0]
        dw_down, dw_up = dw_mlp(sv['u'], dhb, sv['z2'], dup, tf)
        blocks = dict(
            w_down=dw_down, w_up=dw_up,
            w_out=mm_tn(sv['mix'], dh2, tn=d // 2, name="dw_out").reshape(N_DEV, d // N_DEV, d))
        tok = hooks.grads_ready(l, 'mlp', blocks)
        dxr, dyr, dwga, dwgx, vec = rec_bwd(drec, sv['hr'], sv['xc'], sv['gates'], sv['rest'], small['conv_w'][l],
                                            sv['wga'], sv['wgx'], row(small['lru_L'][l]) + tok, rw)
        gs['w_gate_a'][l] = _block_diag_extract(dwga, blk)
        gs['w_gate_x'][l] = _block_diag_extract(dwgx, blk)
        vec = vec.transpose(1, 0, 2).reshape(SUBLANES, rw)
        gs['conv_w'][l] = vec[0:CONV_WIDTH]
        gs['conv_b'][l] = vec[4]
        gs['b_gate_a'][l] = vec[5]
        gs['b_gate_x'][l] = vec[6]
        gs['lru_L'][l] = vec[7]
        dq, dk, dv, drow, dcol = attn_bwd(sv['qkv'], do, sv['o'], sv['lset'], sv['c'], sv['ct'] + tok, nh)
        drow8 = drow[:, 0:2, :].reshape(nh, tp)
        if nh < SUBLANES:
            drow8 = jnp.pad(drow8, ((0, SUBLANES - nh), (0, 0)))
        dfl, dbf = fgate_bwd(drow8, dcol, sv['rest'], bf_pad[l:l + 1], fcol)
        gs['b_f'][l] = dbf[0, 0:nh]
        parts = (dq, dk, dv, dxr, dyr, dfl)
        dh, dg1 = in_proj_bwd(dh2, parts, sv['w_in_t'], sv['wrest_t'], sv['h'], row(small['attn_norm_g'][l]), tm)
        gs['attn_norm_g'][l] = dg1[0]
        first = ()
        if l == 0:
            grads = {n: jnp.stack(v) for n, v in gs.items()}
            grads['final_g'] = dgf[0]
            grads['meta'] = dh[0:N_META]
            first = hooks.small_ready(grads)
        dw_in = dw_in_t(sv['z'], parts, nh, tm2)
        dw_in = dw_in.reshape(N_DEV, dw_in.shape[0] // N_DEV, d)
        tok = hooks.grads_ready(l, 'in', dict(w_in=dw_in), first)

    return loss[0, 0], dh


def prep_weights(g_in, g_out, nh, rw):
    d = g_in.shape[2]
    w_in_t = g_in.reshape(-1, d)
    f0 = 3 * nh * HEAD_DIM
    wrest_t = jnp.concatenate([w_in_t[f0 + nh:f0 + nh + 2 * rw],
                               jnp.pad(w_in_t[f0:f0 + nh], ((0, LANES - nh), (0, 0)))], axis=0)
    return w_in_t, wrest_t, g_out.reshape(d, d)


BIG = ['w_in', 'w_out', 'w_up', 'w_down']
EXCHANGE_GROUPS = {'mlp': ['w_down', 'w_up', 'w_out'], 'in': ['w_in']}
WEIGHTS = ['meta', 'attn_norm_g', 'w_in', 'b_f', 'conv_w', 'conv_b', 'w_gate_a', 'b_gate_a', 'w_gate_x', 'b_gate_x',
           'lru_L', 'attn_out_g', 'rec_out_g', 'w_out', 'mlp_norm_g', 'w_up', 'w_down', 'final_g']


def _set_own(arr, own, me):
    return lax.dynamic_update_slice_in_dim(arr, own[None], me, axis=0)


class _Step:
    def __init__(self, w, nh, rw, me):
        self.w, self.nh, self.rw, self.me = w, nh, rw, me
        depth = w['w_in'].shape[0]
        first = [w['w_in_t'][:, 0, :].astype(BF16), w['w_out'][0].astype(BF16), w['meta'], w['conv_w']]
        self.pending, token = gather_start([first], "gather_start_0")
        zero = token[0, 0].astype(BF16)
        groups = [[w['w_up'][0].astype(BF16) + zero, w['w_down'][0].astype(BF16) + zero]]
        for l in range(1, depth):
            groups.append([w['w_in_t'][:, l, :].astype(BF16) + zero, w['w_out'][l].astype(BF16) + zero])
            groups.append([w['w_up'][l].astype(BF16) + zero, w['w_down'][l].astype(BF16) + zero])
        rest, _ = gather_start(groups, "gather_start_1")
        self.pending += rest
        self.first_after = rest[0][2][0]
        self.gathered = {}
        self.passing = {}
        self.token = jnp.zeros((), F32)
        self.lands = {n: lax.empty((N_DEV,) + w[n].shape, BF16) for n in BIG}
        din8, _, d = w['w_in_t'].shape
        self.lands['w_in'] = lax.empty((N_DEV, depth, din8, d), BF16)
        self.started = []
        self.small = None

    def _pass_on(self, gi, after):
        if gi < len(self.pending) and gi not in self.passing:
            send, recv, srcs, lands = self.pending[gi]
            srcs, lands = gather_wait(send, recv, srcs, lands, after, "gather_wait_%d" % gi)
            fsend, frecv, lands, token = forward_start(lands, "forward_start_%d" % gi)
            self.passing[gi] = (fsend, frecv, srcs, lands)
            self.token = token

    def group(self, gi, after):
        if gi not in self.gathered:
            self._pass_on(gi, after)
            fsend, frecv, srcs, lands = self.passing[gi]
            lands = forward_wait(fsend, frecv, lands, after, "forward_wait_%d" % gi)
            self.gathered[gi] = [_set_own(g, own, self.me) for g, own in zip(lands, srcs)]
            if gi >= 2:
                self._pass_on(gi + 1, lands[0])
        return self.gathered[gi]

    def mixer_weights(self, l, after):
        g = self.group(2 * l, after)
        return (*prep_weights(g[0], g[1], self.nh, self.rw), self.token)

    def mlp_weights(self, l, after):
        g = self.group(2 * l + 1, after)
        return g[0], g[1], self.token

    def grads_ready(self, l, group, blocks, after=()):
        names = EXCHANGE_GROUPS[group]
        send, recv, srcs, lands, token = exchange_start(
            [blocks[n] for n in names], [self.lands[n] for n in names], l, "exchange_start_%s_%d" % (group, l),
            after)
        for n, a in zip(names, lands):
            self.lands[n] = a
        self.started.append((l, group, send, recv, srcs))
        return token

    def small_ready(self, grads):
        self.small_shapes = {n: grads[n].shape for n in SMALL}
        packed = _pack(grads).astype(BF16)
        send, recv, srcs, lands, token = exchange_start(
            [packed], [lax.empty((N_DEV,) + packed.shape, BF16)], None, "small_start")
        self.small = (send, recv, srcs, lands)
        return srcs[0]

    def small_sum(self, after):
        send, recv, srcs, lands = self.small
        srcs, lands = exchange_wait(send, recv, srcs, lands, after, None, "small_wait")
        parts = _set_own(lands[0], srcs[0], self.me)
        return _unpack(sum_parts(parts, "sum_small_grads"), self.small_shapes)

    def received(self, group, after):
        names = EXCHANGE_GROUPS[group]
        own = {n: [None] * self.w[n].shape[0] for n in names}
        for l, grp, send, recv, srcs in self.started:
            if grp != group:
                continue
            srcs, lands = exchange_wait(send, recv, srcs, [self.lands[n] for n in names], after, l,
                                        "exchange_wait_%s_%d" % (group, l))
            for n, a, sr in zip(names, lands, srcs):
                self.lands[n] = a
                own[n][l] = sr
        return {n: (self.lands[n], own[n]) for n in names}


def kernel(x, meta, attn_norm_g, w_in, b_f, conv_w, conv_b, w_gate_a, b_gate_a, w_gate_x, b_gate_x, lru_L, attn_out_g, rec_out_g, w_out, mlp_norm_g, w_up, w_down, final_g, loss_target, m_meta, m_attn_norm_g, m_w_in, m_b_f, m_conv_w, m_conv_b, m_w_gate_a, m_b_gate_a, m_w_gate_x, m_b_gate_x, m_lru_L, m_attn_out_g, m_rec_out_g, m_w_out, m_mlp_norm_g, m_w_up, m_w_down, m_final_g, v_meta, v_attn_norm_g, v_w_in, v_b_f, v_conv_w, v_conv_b, v_w_gate_a, v_b_gate_a, v_w_gate_x, v_b_gate_x, v_lru_L, v_attn_out_g, v_rec_out_g, v_w_out, v_mlp_norm_g, v_w_up, v_w_down, v_final_g):
    w = dict(meta=meta, attn_norm_g=attn_norm_g, w_in=w_in, b_f=b_f, conv_w=conv_w, conv_b=conv_b,
             w_gate_a=w_gate_a, b_gate_a=b_gate_a, w_gate_x=w_gate_x, b_gate_x=b_gate_x, lru_L=lru_L,
             attn_out_g=attn_out_g, rec_out_g=rec_out_g, w_out=w_out, mlp_norm_g=mlp_norm_g, w_up=w_up,
             w_down=w_down, final_g=final_g)
    mo = dict(meta=m_meta, attn_norm_g=m_attn_norm_g, w_in=m_w_in, b_f=m_b_f, conv_w=m_conv_w, conv_b=m_conv_b,
              w_gate_a=m_w_gate_a, b_gate_a=m_b_gate_a, w_gate_x=m_w_gate_x, b_gate_x=m_b_gate_x, lru_L=m_lru_L,
              attn_out_g=m_attn_out_g, rec_out_g=m_rec_out_g, w_out=m_w_out, mlp_norm_g=m_mlp_norm_g,
              w_up=m_w_up, w_down=m_w_down, final_g=m_final_g)
    vo = dict(meta=v_meta, attn_norm_g=v_attn_norm_g, w_in=v_w_in, b_f=v_b_f, conv_w=v_conv_w, conv_b=v_conv_b,
              w_gate_a=v_w_gate_a, b_gate_a=v_b_gate_a, w_gate_x=v_w_gate_x, b_gate_x=v_b_gate_x, lru_L=v_lru_L,
              attn_out_g=v_attn_out_g, rec_out_g=v_rec_out_g, w_out=v_w_out, mlp_norm_g=v_mlp_norm_g,
              w_up=v_w_up, w_down=v_w_down, final_g=v_final_g)
    depth = w_in.shape[0]
    nh = b_f.shape[1]
    rw = conv_b.shape[1]
    me = 4 * lax.axis_index("x") + 2 * lax.axis_index("y") + lax.axis_index("c")

    w['w_in_t'] = jnp.transpose(w_in, (2, 0, 1))
    swap = lambda a: jnp.swapaxes(a, 1, 2)
    step = _Step(w, nh, rw, me)
    g0 = step.group(0, step.first_after)
    meta_full = g0[2].transpose(1, 0, 2).reshape(N_META, -1)
    conv_full = g0[3].transpose(1, 2, 0, 3).reshape(depth, CONV_WIDTH, rw)
    small = {n: w[n] for n in SMALL}
    small['conv_w'] = conv_full

    loss_part, dh0 = local_step(x[0], loss_target[0], meta_full, small, step)
    loss = lax.psum(loss_part, ("x", "y", "c"))
    grad_x = dh0[N_META:N_META + x.shape[1]][None]

    out_g, out_d, out_m, out_v = {}, {}, {}, {}
    me1 = me.reshape(1).astype(jnp.int32)

    def update_big(group, after):
        for n, (r, owns) in step.received(group, after).items():
            if n == 'w_in':
                out = sum_adamw_t(r, owns, me1, swap(w[n]), swap(mo[n]), swap(vo[n]), "adamw_w_in")
                out = [swap(a) for a in out]
            else:
                shp = w[n].shape
                rows, cols = shp[0] * shp[1], shp[2]
                tr = min(512 if cols <= 512 else 256, shp[1])
                out = sum_adamw(r.reshape(N_DEV, rows, cols), owns, me1, w[n].reshape(rows, cols),
                                mo[n].reshape(rows, cols), vo[n].reshape(rows, cols), tr, "adamw_" + n)
                out = [a.reshape(shp) for a in out]
            out_g[n], out_d[n], out_m[n], out_v[n] = out
            after = out[0]
        return after

    update_big('mlp', step.started[-1][4][0])

    gsum = step.small_sum([out_g[n] for n in EXCHANGE_GROUPS['mlp']])
    gsum['meta'] = lax.dynamic_slice_in_dim(gsum['meta'], me * meta.shape[1], meta.shape[1], axis=1)
    gsum['conv_w'] = lax.dynamic_slice_in_dim(gsum['conv_w'], me * conv_w.shape[2], conv_w.shape[2], axis=2)
    as2d = lambda a: a.reshape(-1, a.shape[-1])
    deltas, new_m, new_v = adamw_group([as2d(gsum[n]) for n in SMALL], [as2d(w[n]) for n in SMALL],
                                       [as2d(mo[n]) for n in SMALL], [as2d(vo[n]) for n in SMALL], "adamw_small")
    for i, n in enumerate(SMALL):
        out_g[n] = gsum[n]
        out_d[n], out_m[n], out_v[n] = [a[i].reshape(w[n].shape) for a in (deltas, new_m, new_v)]

    update_big('in', deltas[0])

    return (loss, grad_x, *[out_g[n] for n in WEIGHTS], *[out_d[n] for n in WEIGHTS],
            *[out_m[n] for n in WEIGHTS], *[out_v[n] for n in WEIGHTS])
```

```python
import math

import jax
import jax.numpy as jnp
from jax import lax
from jax.experimental import pallas as pl
from jax.experimental.pallas import tpu as pltpu

F32 = jnp.float32
BF16 = jnp.bfloat16

N_DEV = 8
N_META = 16
HEAD_DIM = 64
CONV_WIDTH = 4
RG_C = 8.0
NORM_EPS = 1e-6
LANES = 128
SUBLANES = 8
ATT_BLOCK = 128
ATT_TQ = 512
NEG_BIG = -1e30
ATT_SCALE = 1.0 / math.sqrt(HEAD_DIM)

ADAM_LR = 0.001
ADAM_B1 = 0.9
ADAM_B2 = 0.999
ADAM_EPS = 1e-08
ADAM_WD = 0.01
ADAM_STEP = 10

VMEM_LIMIT_BYTES = 56 * 1024 * 1024
MESH = pl.DeviceIdType.MESH
ANY = pl.BlockSpec(memory_space=pl.ANY)


def _cparams(*sem):
    return pltpu.CompilerParams(dimension_semantics=sem if sem else None,
                                vmem_limit_bytes=VMEM_LIMIT_BYTES)


def _dot(a, b):
    return jnp.dot(a, b, preferred_element_type=F32)


def _dot_nt(a, b):
    return lax.dot_general(a, b, (((1,), (1,)), ((), ())), preferred_element_type=F32)


def _dot_tn(a, b):
    return lax.dot_general(a, b, (((0,), (0,)), ((), ())), preferred_element_type=F32)


def _sigmoid(x):
    return 0.5 * (1.0 + jnp.tanh(0.5 * x))


def _log_sigmoid(x):
    return jnp.minimum(x, 0.0) - jnp.log(1.0 + jnp.exp(-jnp.abs(x)))


def _expm1(x):
    series = x * (1.0 + x * (0.5 + x * (1.0 / 6.0 + x * (1.0 / 24.0))))
    return jnp.where(jnp.abs(x) < 1e-2, series, jnp.exp(x) - 1.0)


_GELU_K = math.sqrt(2.0 / math.pi)
_GELU_C = 0.044715


def _gelu(x):
    t = jnp.tanh(_GELU_K * (x + _GELU_C * x * x * x))
    return 0.5 * x * (1.0 + t)


def _gelu_and_grad(x):
    x2 = x * x
    t = jnp.tanh(_GELU_K * (x + _GELU_C * x2 * x))
    half = 0.5 * (1.0 + t)
    return x * half, half + 0.5 * x * (1.0 - t * t) * _GELU_K * (1.0 + 3.0 * _GELU_C * x2)


def _split3_dot(tri, x):
    hi = x.astype(BF16)
    r1 = x - hi.astype(F32)
    mid = r1.astype(BF16)
    lo = (r1 - mid.astype(F32)).astype(BF16)
    return _dot(tri, hi) + _dot(tri, mid) + _dot(tri, lo)


def _dot_split3(x, sel):
    hi = x.astype(BF16)
    r1 = x - hi.astype(F32)
    mid = r1.astype(BF16)
    lo = (r1 - mid.astype(F32)).astype(BF16)
    return _dot(hi, sel) + _dot(mid, sel) + _dot(lo, sel)


def _rms_fwd(x, g):
    r = lax.rsqrt(jnp.mean(x * x, axis=-1, keepdims=True) + NORM_EPS)
    return x * r * g


def _rms_bwd(x, g, dy):
    r = lax.rsqrt(jnp.mean(x * x, axis=-1, keepdims=True) + NORM_EPS)
    xn = x * r
    dxn = dy * g
    dx = r * (dxn - xn * jnp.mean(dxn * xn, axis=-1, keepdims=True))
    return dx, jnp.sum(dy * xn, axis=0, keepdims=True)


def _accumulate(ref, val, first):
    @pl.when(first)
    def _():
        ref[...] = val

    @pl.when(jnp.logical_not(first))
    def _():
        ref[...] += val


def in_proj(h, g1, w_in_t, wrest_t, nq, tm):
    tp, d = h.shape
    nr = wrest_t.shape[0]

    def body(h_ref, g_ref, wq_ref, wr_ref, z_ref, qkv_ref, rest_ref):
        z = _rms_fwd(h_ref[...], g_ref[...]).astype(BF16)
        z_ref[...] = z
        qkv_ref[...] = _dot_nt(z, wq_ref[...]).astype(BF16)
        rest_ref[...] = _dot_nt(z, wr_ref[...])

    return pl.pallas_call(
        body, name="in_proj", grid=(tp // tm,),
        in_specs=[pl.BlockSpec((tm, d), lambda i: (i, 0)),
                  pl.BlockSpec((1, d), lambda i: (0, 0)),
                  pl.BlockSpec((nq, d), lambda i: (0, 0)),
                  pl.BlockSpec((nr, d), lambda i: (0, 0))],
        out_specs=[pl.BlockSpec((tm, d), lambda i: (i, 0)),
                   pl.BlockSpec((tm, nq), lambda i: (i, 0)),
                   pl.BlockSpec((tm, nr), lambda i: (i, 0))],
        out_shape=[jax.ShapeDtypeStruct((tp, d), BF16),
                   jax.ShapeDtypeStruct((tp, nq), BF16),
                   jax.ShapeDtypeStruct((tp, nr), F32)],
        compiler_params=_cparams("parallel"),
    )(h, g1, w_in_t, wrest_t)


def fgate_fwd(rest, bf_pad, fcol):
    tp = rest.shape[0]
    nb = tp // ATT_BLOCK

    def body(f_ref, b_ref, c_ref, ct_ref):
        r_i = lax.broadcasted_iota(jnp.int32, (ATT_BLOCK, ATT_BLOCK), 0)
        c_i = lax.broadcasted_iota(jnp.int32, (ATT_BLOCK, ATT_BLOCK), 1)
        tri = (r_i >= c_i).astype(BF16)
        carry = jnp.zeros((1, LANES), F32)
        for i in range(nb):
            sl = slice(i * ATT_BLOCK, (i + 1) * ATT_BLOCK)
            lf = _log_sigmoid(f_ref[sl, :] + b_ref[...])
            cs = _split3_dot(tri, lf) + carry
            carry = cs[ATT_BLOCK - 1:ATT_BLOCK, :]
            c_ref[sl, :] = cs
            ct_ref[:, sl] = cs.T[0:SUBLANES, :]

    return pl.pallas_call(
        body, name="fgate_fwd", grid=(1,),
        in_specs=[pl.BlockSpec((tp, LANES), lambda i: (0, fcol)),
                  pl.BlockSpec((1, LANES), lambda i: (0, 0))],
        out_specs=[pl.BlockSpec((tp, LANES), lambda i: (0, 0)),
                   pl.BlockSpec((SUBLANES, tp), lambda i: (0, 0))],
        out_shape=[jax.ShapeDtypeStruct((tp, LANES), F32),
                   jax.ShapeDtypeStruct((SUBLANES, tp), F32)],
        compiler_params=_cparams("arbitrary"),
    )(rest, bf_pad)


def _pick_col(blk, head):
    lane = lax.broadcasted_iota(jnp.int32, blk.shape, 1)
    return jnp.sum(jnp.where(lane == head, blk, 0.0), axis=1, keepdims=True)


def _pick_row(blk, head):
    sub = lax.broadcasted_iota(jnp.int32, blk.shape, 0)
    return jnp.sum(jnp.where(sub == head, blk, 0.0), axis=0, keepdims=True)


def _att_tiles(tp):
    out, r0 = [], 0
    while r0 < tp:
        rows = min(ATT_TQ, tp - r0)
        out.append((r0, rows, r0 + rows))
        r0 += rows
    return out


def attn_fwd(qkv, c, ct, nh):
    tp = qkv.shape[0]
    npair = nh // 2
    tiles = _att_tiles(tp)

    def body(q_ref, k_ref, v_ref, c_ref, ct_ref, o_ref, lset_ref):
        p = pl.program_id(0)
        lset_ref[...] = jnp.zeros_like(lset_ref)
        for r0, nr, nk in tiles:
            rs = slice(r0, r0 + nr)
            causal = (r0 + lax.broadcasted_iota(jnp.int32, (nr, nk), 0)
                      >= lax.broadcasted_iota(jnp.int32, (nr, nk), 1))
            cblk = c_ref[rs, :]
            ctb = ct_ref[:, 0:nk]
            for hh in range(2):
                head = 2 * p + hh
                hs = slice(hh * HEAD_DIM, (hh + 1) * HEAD_DIM)
                q = q_ref[rs, hs] * ATT_SCALE
                s = (_dot_nt(q, k_ref[0:nk, hs]) + _pick_col(cblk, head)) - _pick_row(ctb, head)
                s = jnp.where(causal, s, NEG_BIG)
                m = jnp.max(s, axis=1, keepdims=True)
                pm = jnp.exp(s - m)
                l = jnp.sum(pm, axis=1, keepdims=True)
                o_ref[rs, hs] = _dot(pm.astype(BF16), v_ref[0:nk, hs]) / l
                lse = m + jnp.log(l)
                lset_ref[hh:hh + 1, rs] = jnp.broadcast_to(lse, (nr, LANES)).T[0:1, :]

    pair = lambda p: (0, p)
    return pl.pallas_call(
        body, name="attn_fwd", grid=(npair,),
        in_specs=[pl.BlockSpec((tp, LANES), pair),
                  pl.BlockSpec((tp, LANES), lambda p: (0, npair + p)),
                  pl.BlockSpec((tp, LANES), lambda p: (0, 2 * npair + p)),
                  pl.BlockSpec((tp, LANES), lambda p: (0, 0)),
                  pl.BlockSpec((SUBLANES, tp), lambda p: (0, 0))],
        out_specs=[pl.BlockSpec((tp, LANES), pair),
                   pl.BlockSpec((None, SUBLANES, tp), lambda p: (p, 0, 0))],
        out_shape=[jax.ShapeDtypeStruct((tp, nh * HEAD_DIM), F32),
                   jax.ShapeDtypeStruct((npair, SUBLANES, tp), F32)],
        compiler_params=_cparams("parallel"),
    )(qkv, qkv, qkv, c, ct)


def _shift_down(x, k, n):
    if k == 0:
        return x
    rows = lax.broadcasted_iota(jnp.int32, x.shape, 0)
    return jnp.where(rows >= k, pltpu.roll(x, k, 0), 0.0)


def _shift_up(x, k, n):
    if k == 0:
        return x
    rows = lax.broadcasted_iota(jnp.int32, x.shape, 0)
    return jnp.where(rows < n - k, pltpu.roll(x, n - k, 0), 0.0)


def _conv_fwd(xr, cw_ref, cb_ref, n):
    xc = cw_ref[CONV_WIDTH - 1:CONV_WIDTH, :] * xr + cb_ref[...]
    for k in range(1, CONV_WIDTH):
        xc = xc + cw_ref[CONV_WIDTH - 1 - k:CONV_WIDTH - k, :] * _shift_down(xr, k, n)
    return xc


def _gates(xc, wga_ref, bga_ref, wgx_ref, bgx_ref, l_ref):
    xcb = xc.astype(BF16)
    r = _sigmoid(_dot(xcb, wga_ref[...]) + bga_ref[...])
    ig = _sigmoid(_dot(xcb, wgx_ref[...]) + bgx_ref[...])
    ls = _log_sigmoid(l_ref[...])
    log_a = RG_C * r * ls
    a = jnp.exp(log_a)
    mult = jnp.sqrt(-_expm1(2.0 * log_a))
    return xcb, r, ig, ls, log_a, a, mult


SCAN_UNROLL = 4


def _scan_rows(a_s, u_s, out_ref, n, reverse):
    nt = n // SUBLANES
    per = SCAN_UNROLL if nt % SCAN_UNROLL == 0 else 1
    row = lax.broadcasted_iota(jnp.int32, (SUBLANES, LANES), 0)
    last = 0 if reverse else SUBLANES - 1

    def tile_scan(a, u):
        for d in (1, 2, 4):
            if reverse:
                keep = row < SUBLANES - d
                sh = SUBLANES - d
            else:
                keep = row >= d
                sh = d
            a_sh = jnp.where(keep, pltpu.roll(a, sh, 0), 1.0)
            u_sh = jnp.where(keep, pltpu.roll(u, sh, 0), 0.0)
            u = a * u_sh + u
            a = a * a_sh
        return a, u

    def step(t, carry):
        tiles = []
        for k in range(per):
            tt = t * per + k
            if reverse:
                tt = nt - 1 - tt
            off = pl.multiple_of(tt * SUBLANES, SUBLANES)
            a, u = tile_scan(a_s[pl.ds(off, SUBLANES), :], u_s[pl.ds(off, SUBLANES), :])
            tiles.append((off, a, u))
        for off, a, u in tiles:
            out_ref[pl.ds(off, SUBLANES), :] = u + a * carry
            carry = u[last:last + 1, :] + a[last:last + 1, :] * carry
        return carry

    lax.fori_loop(0, nt // per, step, jnp.zeros((1, LANES), F32))


def rec_fwd(rest, convw, convb, wga, bga, wgx, bgx, lru, rw):
    tp = rest.shape[0]
    ng = rw // LANES

    def body(xr_ref, yr_ref, cw_ref, cb_ref, wga_ref, bga_ref, wgx_ref, bgx_ref, l_ref,
             rec_ref, hr_ref, xc_ref, r_ref, ig_ref, a_ref, mult_ref, u_s):
        xc = _conv_fwd(xr_ref[...], cw_ref, cb_ref, tp)
        xc_ref[...] = xc
        _, r, ig, ls, log_a, a, mult = _gates(xc, wga_ref, bga_ref, wgx_ref, bgx_ref, l_ref)
        r_ref[...] = r
        ig_ref[...] = ig
        a_ref[...] = a
        mult_ref[...] = mult
        u_s[...] = mult * ig * xc
        _scan_rows(a_ref, u_s, hr_ref, tp, reverse=False)
        rec_ref[...] = hr_ref[...] * _gelu(yr_ref[...])

    col = lambda g: (0, g)
    vec = pl.BlockSpec((1, LANES), col)
    big = pl.BlockSpec((tp, LANES), col)
    return pl.pallas_call(
        body, name="rec_fwd", grid=(ng,),
        in_specs=[big, pl.BlockSpec((tp, LANES), lambda g: (0, ng + g)),
                  pl.BlockSpec((CONV_WIDTH, LANES), col), vec,
                  pl.BlockSpec((None, LANES, LANES), lambda g: (g, 0, 0)), vec,
                  pl.BlockSpec((None, LANES, LANES), lambda g: (g, 0, 0)), vec, vec],
        out_specs=[big] * 7,
        out_shape=[jax.ShapeDtypeStruct((tp, rw), F32)] * 7,
        scratch_shapes=[pltpu.VMEM((tp, LANES), F32)],
        compiler_params=_cparams("parallel"),
    )(rest, rest, convw, convb, wga, bga, wgx, bgx, lru)


def out_proj(h, o, rec, ga, gr, wout, g2, tm):
    tp, d = h.shape
    aw, rw = o.shape[1], rec.shape[1]

    def body(h_ref, o_ref, rec_ref, ga_ref, gr_ref, w_ref, g2_ref, h2_ref, mix_ref, z2_ref):
        mix_ref[:, 0:aw] = _rms_fwd(o_ref[...], ga_ref[...]).astype(BF16)
        mix_ref[:, aw:aw + rw] = _rms_fwd(rec_ref[...], gr_ref[...]).astype(BF16)
        h2 = h_ref[...] + _dot(mix_ref[...], w_ref[...])
        h2_ref[...] = h2
        z2_ref[...] = _rms_fwd(h2, g2_ref[...]).astype(BF16)

    row = lambda i: (i, 0)
    fix = lambda i: (0, 0)
    return pl.pallas_call(
        body, name="out_proj", grid=(tp // tm,),
        in_specs=[pl.BlockSpec((tm, d), row), pl.BlockSpec((tm, aw), row), pl.BlockSpec((tm, rw), row),
                  pl.BlockSpec((1, aw), fix), pl.BlockSpec((1, rw), fix),
                  pl.BlockSpec((d, d), fix), pl.BlockSpec((1, d), fix)],
        out_specs=[pl.BlockSpec((tm, d), row)] * 3,
        out_shape=[jax.ShapeDtypeStruct((tp, d), F32), jax.ShapeDtypeStruct((tp, d), BF16),
                   jax.ShapeDtypeStruct((tp, d), BF16)],
        compiler_params=_cparams("parallel"),
    )(h, o, rec, ga, gr, wout, g2)


MLP_BLOCKS = 4


def mlp_fwd(z2, h2, gup, gdown, tm):
    tp, d = h2.shape
    nf = gup.shape[0]
    tf = gup.shape[2]
    nb = MLP_BLOCKS if nf % MLP_BLOCKS == 0 else 1
    nj = nf // nb

    def body(z_ref, h_ref, wu_ref, wd_ref, u_ref, h3_ref, acc):
        j = pl.program_id(1)
        z = z_ref[...]
        part = None
        for b in range(nb):
            u = jnp.maximum(_dot(z, wu_ref[b]), 0.0)
            u_ref[:, b * tf:(b + 1) * tf] = u.astype(BF16)
            p = _dot((u * u).astype(BF16), wd_ref[b])
            part = p if part is None else part + p

        @pl.when(j == 0)
        def _():
            acc[...] = h_ref[...] + part

        @pl.when(j > 0)
        def _():
            acc[...] += part

        @pl.when(j == nj - 1)
        def _():
            h3_ref[...] = acc[...]

    return pl.pallas_call(
        body, name="mlp_fwd", grid=(tp // tm, nj),
        in_specs=[pl.BlockSpec((tm, d), lambda i, j: (i, 0)),
                  pl.BlockSpec((tm, d), lambda i, j: (i, 0)),
                  pl.BlockSpec((nb, d, tf), lambda i, j: (j, 0, 0)),
                  pl.BlockSpec((nb, tf, d), lambda i, j: (j, 0, 0))],
        out_specs=[pl.BlockSpec((tm, nb * tf), lambda i, j: (i, j)),
                   pl.BlockSpec((tm, d), lambda i, j: (i, 0))],
        out_shape=[jax.ShapeDtypeStruct((tp, nf * tf), BF16), jax.ShapeDtypeStruct((tp, d), F32)],
        scratch_shapes=[pltpu.VMEM((tm, d), F32)],
        compiler_params=_cparams("parallel", "arbitrary"),
    )(z2, h2, gup, gdown)


def loss_head(h, gf, tgt, t_real, tm):
    tp, d = h.shape

    def body(h_ref, g_ref, t_ref, dh_ref, dg_ref, loss_ref):
        i = pl.program_id(0)
        x = h_ref[...]
        g = g_ref[...]
        r = lax.rsqrt(jnp.mean(x * x, axis=-1, keepdims=True) + NORM_EPS)
        xn = x * r
        rows = i * tm + lax.broadcasted_iota(jnp.int32, (tm, 1), 0)
        valid = jnp.logical_and(rows >= N_META, rows < t_real)
        e = jnp.where(valid, xn * g - t_ref[...], 0.0)
        part = 0.5 * jnp.sum(jnp.sum(e * e, axis=1, keepdims=True) / d, axis=0, keepdims=True)
        dy = e / d
        dxn = dy * g
        dh_ref[...] = r * (dxn - xn * jnp.mean(dxn * xn, axis=-1, keepdims=True))
        _accumulate(dg_ref, jnp.sum(dy * xn, axis=0, keepdims=True), i == 0)
        _accumulate(loss_ref, jnp.broadcast_to(part, (1, LANES)), i == 0)

    row = lambda i: (i, 0)
    fix = lambda i: (0, 0)
    return pl.pallas_call(
        body, name="loss_head", grid=(tp // tm,),
        in_specs=[pl.BlockSpec((tm, d), row), pl.BlockSpec((1, d), fix), pl.BlockSpec((tm, d), row)],
        out_specs=[pl.BlockSpec((tm, d), row), pl.BlockSpec((1, d), fix), pl.BlockSpec((1, LANES), fix)],
        out_shape=[jax.ShapeDtypeStruct((tp, d), F32), jax.ShapeDtypeStruct((1, d), F32),
                   jax.ShapeDtypeStruct((1, LANES), F32)],
        compiler_params=_cparams("arbitrary"),
    )(h, gf, tgt)


def mlp_bwd(dh, u, h2, g2, gup, gdown, tm):
    tp, d = dh.shape
    nf = gup.shape[0]
    tf = gup.shape[2]
    nb = MLP_BLOCKS if nf % MLP_BLOCKS == 0 else 1
    nj = nf // nb
    ni = tp // tm

    def body(dh_ref, u_ref, h2_ref, g_ref, wu_ref, wd_ref, dup_ref, dh2_ref, dg_ref, dhb, acc):
        i = pl.program_id(0)
        j = pl.program_id(1)

        @pl.when(j == 0)
        def _():
            dhb[...] = dh_ref[...].astype(BF16)

        part = None
        for b in range(nb):
            cols = slice(b * tf, (b + 1) * tf)
            dup = (_dot_nt(dhb[...], wd_ref[b]) * (2.0 * u_ref[:, cols].astype(F32))).astype(BF16)
            dup_ref[:, cols] = dup
            p = _dot_nt(dup, wu_ref[b])
            part = p if part is None else part + p
        _accumulate(acc, part, j == 0)

        @pl.when(j == nj - 1)
        def _():
            dx, dg = _rms_bwd(h2_ref[...], g_ref[...], acc[...])
            dh2_ref[...] = dh_ref[...] + dx
            _accumulate(dg_ref, dg, i == 0)

    return pl.pallas_call(
        body, name="mlp_bwd", grid=(ni, nj),
        in_specs=[pl.BlockSpec((tm, d), lambda i, j: (i, 0)),
                  pl.BlockSpec((tm, nb * tf), lambda i, j: (i, j)),
                  pl.BlockSpec((tm, d), lambda i, j: (i, 0)),
                  pl.BlockSpec((1, d), lambda i, j: (0, 0)),
                  pl.BlockSpec((nb, d, tf), lambda i, j: (j, 0, 0)),
                  pl.BlockSpec((nb, tf, d), lambda i, j: (j, 0, 0))],
        out_specs=[pl.BlockSpec((tm, nb * tf), lambda i, j: (i, j)),
                   pl.BlockSpec((tm, d), lambda i, j: (i, 0)),
                   pl.BlockSpec((1, d), lambda i, j: (0, 0)),
                   pl.BlockSpec((tm, d), lambda i, j: (i, 0))],
        out_shape=[jax.ShapeDtypeStruct((tp, nf * tf), BF16), jax.ShapeDtypeStruct((tp, d), F32),
                   jax.ShapeDtypeStruct((1, d), F32), jax.ShapeDtypeStruct((tp, d), BF16)],
        scratch_shapes=[pltpu.VMEM((tm, d), F32)],
        compiler_params=_cparams("arbitrary", "arbitrary"),
    )(dh, u, h2, g2, gup, gdown)


def mm_tn(a, b, *, tn, name):
    rows, kk = a.shape
    nn = b.shape[1]

    def body(a_ref, b_ref, o_ref, at):
        @pl.when(pl.program_id(0) == 0)
        def _():
            at[...] = a_ref[...].T

        o_ref[...] = _dot(at[...], b_ref[...].astype(BF16)).astype(BF16)

    return pl.pallas_call(
        body, name=name, grid=(nn // tn,),
        in_specs=[pl.BlockSpec((rows, kk), lambda n: (0, 0)),
                  pl.BlockSpec((rows, tn), lambda n: (0, n))],
        out_specs=pl.BlockSpec((kk, tn), lambda n: (0, n)),
        out_shape=jax.ShapeDtypeStruct((kk, nn), BF16),
        scratch_shapes=[pltpu.VMEM((kk, rows), BF16)],
        compiler_params=_cparams("arbitrary"),
    )(a, b)


def dw_mlp(u, dhb, z2, dup, tf):
    rows, dff = u.shape
    d = z2.shape[1]
    nf = dff // tf

    def body(u_ref, dh_ref, z_ref, dup_ref, dwd_ref, dwu_ref, zt):
        @pl.when(pl.program_id(0) == 0)
        def _():
            zt[...] = z_ref[...].T

        uf = u_ref[...].astype(F32)
        dwd_ref[...] = _dot_tn((uf * uf).astype(BF16), dh_ref[...]).astype(BF16)
        dwu_ref[...] = _dot(zt[...], dup_ref[...]).astype(BF16)

    col = lambda j: (0, j)
    fix = lambda j: (0, 0)
    return pl.pallas_call(
        body, name="dw_mlp", grid=(nf,),
        in_specs=[pl.BlockSpec((rows, tf), col), pl.BlockSpec((rows, d), fix),
                  pl.BlockSpec((rows, d), fix), pl.BlockSpec((rows, tf), col)],
        out_specs=[pl.BlockSpec((None, tf, d), lambda j: (j, 0, 0)),
                   pl.BlockSpec((None, d, tf), lambda j: (j, 0, 0))],
        out_shape=[jax.ShapeDtypeStruct((nf, tf, d), BF16), jax.ShapeDtypeStruct((nf, d, tf), BF16)],
        scratch_shapes=[pltpu.VMEM((d, rows), BF16)],
        compiler_params=_cparams("arbitrary"),
    )(u, dhb, z2, dup)


def out_proj_bwd(dh2, o, rec, ga, gr, wout, tm):
    tp, d = dh2.shape
    aw, rw = o.shape[1], rec.shape[1]

    def body(dh_ref, o_ref, rec_ref, ga_ref, gr_ref, w_ref, do_ref, drec_ref, dga_ref, dgr_ref):
        i = pl.program_id(0)
        dmix = _dot_nt(dh_ref[...].astype(BF16), w_ref[...])
        do, dga = _rms_bwd(o_ref[...], ga_ref[...], dmix[:, 0:aw])
        drec, dgr = _rms_bwd(rec_ref[...], gr_ref[...], dmix[:, aw:aw + rw])
        do_ref[...] = do
        drec_ref[...] = drec
        _accumulate(dga_ref, dga, i == 0)
        _accumulate(dgr_ref, dgr, i == 0)

    row = lambda i: (i, 0)
    fix = lambda i: (0, 0)
    return pl.pallas_call(
        body, name="out_proj_bwd", grid=(tp // tm,),
        in_specs=[pl.BlockSpec((tm, d), row), pl.BlockSpec((tm, aw), row), pl.BlockSpec((tm, rw), row),
                  pl.BlockSpec((1, aw), fix), pl.BlockSpec((1, rw), fix), pl.BlockSpec((d, d), fix)],
        out_specs=[pl.BlockSpec((tm, aw), row), pl.BlockSpec((tm, rw), row),
                   pl.BlockSpec((1, aw), fix), pl.BlockSpec((1, rw), fix)],
        out_shape=[jax.ShapeDtypeStruct((tp, aw), F32), jax.ShapeDtypeStruct((tp, rw), F32),
                   jax.ShapeDtypeStruct((1, aw), F32), jax.ShapeDtypeStruct((1, rw), F32)],
        compiler_params=_cparams("arbitrary"),
    )(dh2, o, rec, ga, gr, wout)


def rec_bwd(drec, hr, xc, gates, rest, convw, wga, wgx, lru, rw):
    tp = rest.shape[0]
    ng = rw // LANES

    def body(drec_ref, hr_ref, xc_ref, r_ref, ig_ref, a_ref, mult_ref, xr_ref, yr_ref, cw_ref, wga_ref, wgx_ref,
             l_ref, dxr_ref, dyr_ref, dwga_ref, dwgx_ref, vec_ref, a_s, u_s, lam_s):
        xc = xc_ref[...]
        h = hr_ref[...]
        drec = drec_ref[...]
        r, ig, a, mult = r_ref[...], ig_ref[...], a_ref[...], mult_ref[...]
        xcb = xc.astype(BF16)
        ls = _log_sigmoid(l_ref[...])
        gelu, gelu_grad = _gelu_and_grad(yr_ref[...])
        dyr_ref[...] = (drec * h * gelu_grad).astype(BF16)
        a_s[...] = _shift_up(a, 1, tp)
        u_s[...] = drec * gelu
        _scan_rows(a_s, u_s, lam_s, tp, reverse=True)
        lam = lam_s[...]
        da = lam * _shift_down(h, 1, tp)
        dmult = lam * ig * xc
        dig = lam * mult * xc
        dxc = lam * mult * ig
        dlog_a = da * a - dmult * (a * a) / mult
        dr = dlog_a * (RG_C * ls)
        dl = jnp.sum(dlog_a * (RG_C * r), axis=0, keepdims=True) * _sigmoid(-l_ref[...])
        dpa = dr * r * (1.0 - r)
        dpx = dig * ig * (1.0 - ig)
        dpab = dpa.astype(BF16)
        dpxb = dpx.astype(BF16)
        dxc = dxc + _dot_nt(dpab, wga_ref[...]) + _dot_nt(dpxb, wgx_ref[...])
        dwga_ref[...] = _dot_tn(xcb, dpab)
        dwgx_ref[...] = _dot_tn(xcb, dpxb)
        xr = xr_ref[...]
        dxr = cw_ref[CONV_WIDTH - 1:CONV_WIDTH, :] * dxc
        for k in range(1, CONV_WIDTH):
            dxr = dxr + cw_ref[CONV_WIDTH - 1 - k:CONV_WIDTH - k, :] * _shift_up(dxc, k, tp)
        dxr_ref[...] = dxr.astype(BF16)
        for k in range(CONV_WIDTH):
            vec_ref[k:k + 1, :] = jnp.sum(dxc * _shift_down(xr, CONV_WIDTH - 1 - k, tp), axis=0, keepdims=True)
        vec_ref[4:5, :] = jnp.sum(dxc, axis=0, keepdims=True)
        vec_ref[5:6, :] = jnp.sum(dpa, axis=0, keepdims=True)
        vec_ref[6:7, :] = jnp.sum(dpx, axis=0, keepdims=True)
        vec_ref[7:8, :] = dl

    col = lambda g: (0, g)
    vec = pl.BlockSpec((1, LANES), col)
    big = pl.BlockSpec((tp, LANES), col)
    sq = pl.BlockSpec((None, LANES, LANES), lambda g: (g, 0, 0))
    return pl.pallas_call(
        body, name="rec_bwd", grid=(ng,),
        in_specs=[big] * 8 + [pl.BlockSpec((tp, LANES), lambda g: (0, ng + g)),
                                pl.BlockSpec((CONV_WIDTH, LANES), col), sq, sq, vec],
        out_specs=[big, big, sq, sq, pl.BlockSpec((None, SUBLANES, LANES), lambda g: (g, 0, 0))],
        out_shape=[jax.ShapeDtypeStruct((tp, rw), BF16), jax.ShapeDtypeStruct((tp, rw), BF16),
                   jax.ShapeDtypeStruct((ng, LANES, LANES), F32), jax.ShapeDtypeStruct((ng, LANES, LANES), F32),
                   jax.ShapeDtypeStruct((ng, SUBLANES, LANES), F32)],
        scratch_shapes=[pltpu.VMEM((tp, LANES), F32)] * 3,
        compiler_params=_cparams("parallel"),
    )(drec, hr, xc, *gates, rest, rest, convw, wga, wgx, lru)


def attn_bwd(qkv, do, o, lset, c, ct, nh):
    tp = qkv.shape[0]
    npair = nh // 2
    aw = nh * HEAD_DIM
    tiles = _att_tiles(tp)

    def body(q_ref, k_ref, v_ref, do_ref, o_ref, lset_ref, c_ref, ct_ref,
             dq_ref, dk_ref, dv_ref, drow_ref, dcol_ref, dk_acc, dv_acc, dq_t):
        p = pl.program_id(0)
        k_t = k_ref[...].T
        dk_acc[...] = jnp.zeros_like(dk_acc)
        dv_acc[...] = jnp.zeros_like(dv_acc)
        dcol_ref[...] = jnp.zeros_like(dcol_ref)
        drow_ref[...] = jnp.zeros_like(drow_ref)
        for r0, nr, nk in tiles:
            rs = slice(r0, r0 + nr)
            causal = (r0 + lax.broadcasted_iota(jnp.int32, (nk, nr), 1)
                      >= lax.broadcasted_iota(jnp.int32, (nk, nr), 0))
            cblk = c_ref[0:nk, :]
            ctb = ct_ref[:, rs]
            for hh in range(2):
                head = 2 * p + hh
                hs = slice(hh * HEAD_DIM, (hh + 1) * HEAD_DIM)
                q = q_ref[rs, hs]
                k = k_ref[0:nk, hs]
                dof = do_ref[rs, hs]
                do16 = dof.astype(BF16)
                delta = jnp.sum(dof * o_ref[rs, hs], axis=1, keepdims=True)
                delta_row = jnp.broadcast_to(delta, (nr, LANES)).T[0:1, :]
                s_t = (_dot_nt(k, q * ATT_SCALE) + _pick_row(ctb, head)) - _pick_col(cblk, head)
                p_t = jnp.where(causal, jnp.exp(s_t - lset_ref[hh:hh + 1, rs]), 0.0)
                ds_t = p_t * (_dot_nt(v_ref[0:nk, hs], do16) - delta_row)
                p16 = p_t.astype(BF16)
                ds16 = ds_t.astype(BF16)
                dv_acc[0:nk, hs] += _dot(p16, do16)
                dk_acc[0:nk, hs] += _dot(ds16, q) * ATT_SCALE
                dq_t[hs, rs] = _dot(k_t[hs, 0:nk], ds16)
                drow_ref[hh:hh + 1, rs] = jnp.sum(ds_t, axis=0, keepdims=True)
                dcol_ref[0:nk, hs] -= jnp.broadcast_to(jnp.sum(ds_t, axis=1, keepdims=True), (nk, HEAD_DIM))
        dk_ref[...] = dk_acc[...].astype(BF16)
        dv_ref[...] = dv_acc[...].astype(BF16)
        dq_ref[...] = (dq_t[...].T * ATT_SCALE).astype(BF16)

    pair = lambda p: (0, p)
    return pl.pallas_call(
        body, name="attn_bwd", grid=(npair,),
        in_specs=[pl.BlockSpec((tp, LANES), pair),
                  pl.BlockSpec((tp, LANES), lambda p: (0, npair + p)),
                  pl.BlockSpec((tp, LANES), lambda p: (0, 2 * npair + p)),
                  pl.BlockSpec((tp, LANES), pair),
                  pl.BlockSpec((tp, LANES), pair),
                  pl.BlockSpec((None, SUBLANES, tp), lambda p: (p, 0, 0)),
                  pl.BlockSpec((tp, LANES), lambda p: (0, 0)),
                  pl.BlockSpec((SUBLANES, tp), lambda p: (0, 0))],
        out_specs=[pl.BlockSpec((tp, LANES), pair), pl.BlockSpec((tp, LANES), pair),
                   pl.BlockSpec((tp, LANES), pair),
                   pl.BlockSpec((None, SUBLANES, tp), lambda p: (p, 0, 0)),
                   pl.BlockSpec((tp, LANES), pair)],
        out_shape=[jax.ShapeDtypeStruct((tp, aw), BF16), jax.ShapeDtypeStruct((tp, aw), BF16),
                   jax.ShapeDtypeStruct((tp, aw), BF16),
                   jax.ShapeDtypeStruct((npair, SUBLANES, tp), F32),
                   jax.ShapeDtypeStruct((tp, aw), F32)],
        scratch_shapes=[pltpu.VMEM((tp, LANES), F32), pltpu.VMEM((tp, LANES), F32),
                        pltpu.VMEM((LANES, tp), F32)],
        compiler_params=_cparams("parallel"),
    )(qkv, qkv, qkv, do, o, lset, c, ct)


def fgate_bwd(dct8, drs, rest, bf_pad, fcol):
    tp = rest.shape[0]
    aw = drs.shape[1]
    nb = tp // ATT_BLOCK
    B = ATT_BLOCK

    def body(d_ref, drs_ref, f_ref, b_ref, dfl_ref, db_ref, pad_s):
        r_i = lax.broadcasted_iota(jnp.int32, (B, B), 0)
        c_i = lax.broadcasted_iota(jnp.int32, (B, B), 1)
        triu = (c_i >= r_i).astype(BF16)
        sel = (lax.broadcasted_iota(jnp.int32, (aw, LANES), 0)
               == HEAD_DIM * lax.broadcasted_iota(jnp.int32, (aw, LANES), 1)).astype(BF16)
        carry = jnp.zeros((1, LANES), F32)
        db = jnp.zeros((1, LANES), F32)
        pad_s[...] = jnp.zeros_like(pad_s)
        for i in range(nb - 1, -1, -1):
            sl = slice(i * B, (i + 1) * B)
            pad_s[0:SUBLANES, :] = d_ref[:, sl]
            dc = pad_s[...].T + _dot_split3(drs_ref[sl, :], sel)
            rc = _split3_dot(triu, dc)
            dlf = rc + carry
            carry = carry + rc[0:1, :]
            dfl = dlf * _sigmoid(-(f_ref[sl, :] + b_ref[...]))
            dfl_ref[sl, :] = dfl.astype(BF16)
            db = db + jnp.sum(dfl, axis=0, keepdims=True)
        db_ref[...] = db

    return pl.pallas_call(
        body, name="fgate_bwd", grid=(1,),
        in_specs=[pl.BlockSpec((SUBLANES, tp), lambda i: (0, 0)),
                  pl.BlockSpec((tp, aw), lambda i: (0, 0)),
                  pl.BlockSpec((tp, LANES), lambda i: (0, fcol)),
                  pl.BlockSpec((1, LANES), lambda i: (0, 0))],
        out_specs=[pl.BlockSpec((tp, LANES), lambda i: (0, 0)),
                   pl.BlockSpec((1, LANES), lambda i: (0, 0))],
        out_shape=[jax.ShapeDtypeStruct((tp, LANES), BF16), jax.ShapeDtypeStruct((1, LANES), F32)],
        scratch_shapes=[pltpu.VMEM((B, B), F32)],
        compiler_params=_cparams("arbitrary"),
    )(dct8, drs, rest, bf_pad)


def in_proj_bwd(dh2, parts, w_in_t, wrest_t, h, g1, tm):
    tp, d = h.shape
    dq, dk, dv, dxr, dyr, dfl = parts
    aw, rw = dq.shape[1], dxr.shape[1]

    def body(dh2_ref, dq_ref, dk_ref, dv_ref, dxr_ref, dyr_ref, dfl_ref, wq_ref, wr_ref, h_ref, g_ref,
             dh_ref, dg_ref):
        i = pl.program_id(0)
        dz = _dot(dq_ref[...], wq_ref[0:aw, :])
        dz += _dot(dk_ref[...], wq_ref[aw:2 * aw, :])
        dz += _dot(dv_ref[...], wq_ref[2 * aw:3 * aw, :])
        dz += _dot(dxr_ref[...], wr_ref[0:rw, :])
        dz += _dot(dyr_ref[...], wr_ref[rw:2 * rw, :])
        dz += _dot(dfl_ref[...], wr_ref[2 * rw:2 * rw + LANES, :])
        dx, dg = _rms_bwd(h_ref[...], g_ref[...], dz)
        dh_ref[...] = dh2_ref[...] + dx
        _accumulate(dg_ref, dg, i == 0)

    row = lambda i: (i, 0)
    fix = lambda i: (0, 0)
    return pl.pallas_call(
        body, name="in_proj_bwd", grid=(tp // tm,),
        in_specs=[pl.BlockSpec((tm, d), row),
                  pl.BlockSpec((tm, aw), row), pl.BlockSpec((tm, aw), row), pl.BlockSpec((tm, aw), row),
                  pl.BlockSpec((tm, rw), row), pl.BlockSpec((tm, rw), row), pl.BlockSpec((tm, LANES), row),
                  pl.BlockSpec((3 * aw, d), fix), pl.BlockSpec(wrest_t.shape, fix),
                  pl.BlockSpec((tm, d), row), pl.BlockSpec((1, d), fix)],
        out_specs=[pl.BlockSpec((tm, d), row), pl.BlockSpec((1, d), fix)],
        out_shape=[jax.ShapeDtypeStruct((tp, d), F32), jax.ShapeDtypeStruct((1, d), F32)],
        compiler_params=_cparams("arbitrary"),
    )(dh2, dq, dk, dv, dxr, dyr, dfl, w_in_t, wrest_t, h, g1)


def dw_in_t(z, parts, nh, tr):
    tp, d = z.shape
    dq, dk, dv, dxr, dyr, dfl = parts
    aw, rw = dq.shape[1], dxr.shape[1]
    d_in = 3 * aw + nh + 2 * rw
    nr = tp // tr
    offs = [(0, aw), (aw, aw), (2 * aw, aw), (3 * aw + nh, rw), (3 * aw + nh + rw, rw)]

    def body(z_ref, dq_ref, dk_ref, dv_ref, dxr_ref, dyr_ref, dfl_ref, o_ref, acc):
        r = pl.program_id(0)

        @pl.when(r == 0)
        def _():
            acc[...] = jnp.zeros_like(acc)

        zt = z_ref[...]
        for (o, n), ref in zip(offs, (dq_ref, dk_ref, dv_ref, dxr_ref, dyr_ref)):
            acc[o:o + n, :] += _dot_tn(ref[...], zt)
        acc[3 * aw:3 * aw + nh, :] += _dot_tn(dfl_ref[...], zt)[0:nh, :]

        @pl.when(r == nr - 1)
        def _():
            o_ref[...] = acc[...].astype(BF16)

    row = lambda r: (r, 0)
    return pl.pallas_call(
        body, name="dw_in", grid=(nr,),
        in_specs=[pl.BlockSpec((tr, d), row),
                  pl.BlockSpec((tr, aw), row), pl.BlockSpec((tr, aw), row), pl.BlockSpec((tr, aw), row),
                  pl.BlockSpec((tr, rw), row), pl.BlockSpec((tr, rw), row), pl.BlockSpec((tr, LANES), row)],
        out_specs=pl.BlockSpec((d_in, d), lambda r: (0, 0)),
        out_shape=jax.ShapeDtypeStruct((d_in, d), BF16),
        scratch_shapes=[pltpu.VMEM((d_in, d), F32)],
        compiler_params=_cparams("arbitrary"),
    )(z, dq, dk, dv, dxr, dyr, dfl)


def _place():
    return lax.axis_index("x"), lax.axis_index("y"), lax.axis_index("c")


HBM = pl.BlockSpec(memory_space=pltpu.HBM)
SEM = pl.BlockSpec(memory_space=pltpu.SEMAPHORE)
EFFECT = pltpu.SideEffectType.DATAFLOW_SIDE_EFFECTING


def _in_hbm(a):
    return pltpu.with_memory_space_constraint(a, pltpu.HBM)


def _as_list(a):
    return list(a) if isinstance(a, (list, tuple)) else [a]


def _gather_targets(x, y, c):
    return [(x, y, 1 - c), (1 - x, y, c), (x, 1 - y, c), (1 - x, 1 - y, c)]


def _slot(t):
    return 4 * t[0] + 2 * t[1] + t[2]


def gather_start(groups, name):
    flat = [a for g in groups for a in g]
    n = len(flat)
    ng = len(groups)
    lands = [lax.empty((N_DEV,) + a.shape, a.dtype) for a in flat]

    def body(*refs):
        src, land = refs[:n], refs[n:2 * n]
        sems = refs[2 * n:2 * n + 2 * ng]
        token = refs[-1]
        x, y, c = _place()
        me = 4 * x + 2 * y + c
        i = 0
        for gi, g in enumerate(groups):
            for a in range(len(g)):
                for k, t in enumerate(_gather_targets(x, y, c)):
                    pltpu.make_async_remote_copy(
                        src_ref=src[i], dst_ref=land[i].at[me],
                        send_sem=sems[2 * gi].at[4 * a + k], recv_sem=sems[2 * gi + 1].at[4 * a + k],
                        device_id=t, device_id_type=MESH).start()
                i += 1
        token[...] = jnp.zeros_like(token)

    sem_shapes = []
    for g in groups:
        sem_shapes += [pltpu.SemaphoreType.DMA((4 * len(g),)), pltpu.SemaphoreType.DMA((4 * len(g),))]
    out = pl.pallas_call(
        body, name=name,
        out_shape=sem_shapes + [pltpu.HBM(a.shape, a.dtype) for a in flat + lands]
        + [jax.ShapeDtypeStruct((SUBLANES, LANES), F32)],
        in_specs=[HBM] * (2 * n),
        out_specs=[SEM] * (2 * ng) + [HBM] * (2 * n) + [pl.BlockSpec(memory_space=pltpu.VMEM)],
        input_output_aliases={i: 2 * ng + i for i in range(2 * n)},
        compiler_params=pltpu.CompilerParams(has_side_effects=EFFECT),
    )(*[_in_hbm(a) for a in flat + lands])
    sems = out[:2 * ng]
    thru = out[2 * ng:2 * ng + 2 * n]
    srcs_t, lands_t = thru[:n], thru[n:]
    res, i = [], 0
    for gi, g in enumerate(groups):
        res.append((sems[2 * gi], sems[2 * gi + 1], srcs_t[i:i + len(g)], lands_t[i:i + len(g)]))
        i += len(g)
    return res, out[-1]


def gather_wait(send, recv, srcs, lands, after, name):
    n = len(srcs)

    def body(*refs):
        src, land = refs[:n], refs[n:2 * n]
        send_sem, recv_sem = refs[2 * n], refs[2 * n + 1]
        x, y, c = _place()
        for a in range(n):
            for k, t in enumerate(_gather_targets(x, y, c)):
                cp = pltpu.make_async_remote_copy(
                    src_ref=src[a], dst_ref=land[a].at[_slot(t)],
                    send_sem=send_sem.at[4 * a + k], recv_sem=recv_sem.at[4 * a + k],
                    device_id=t, device_id_type=MESH)
                cp.wait_send()
                cp.wait_recv()

    out = pl.pallas_call(
        body, name=name,
        out_shape=[pltpu.HBM(a.shape, a.dtype) for a in list(srcs) + list(lands)],
        in_specs=[HBM] * (2 * n) + [SEM, SEM] + [ANY] * len(_as_list(after)),
        out_specs=[HBM] * (2 * n),
        input_output_aliases={i: i for i in range(2 * n)},
        compiler_params=pltpu.CompilerParams(has_side_effects=EFFECT),
    )(*srcs, *lands, send, recv, *_as_list(after))
    return out[:n], out[n:]


def forward_start(lands, name):
    n = len(lands)

    def body(*refs):
        land = refs[:n]
        send_sem, recv_sem = refs[n], refs[n + 1]
        token = refs[-1]
        x, y, c = _place()
        for a in range(n):
            for j, chip in enumerate([(1 - x, y), (x, 1 - y), (1 - x, 1 - y)]):
                blk = land[a].at[_slot((*chip, c))]
                pltpu.make_async_remote_copy(src_ref=blk, dst_ref=blk, send_sem=send_sem.at[3 * a + j],
                                             recv_sem=recv_sem.at[3 * a + j], device_id=(x, y, 1 - c),
                                             device_id_type=MESH).start()
        token[...] = jnp.zeros_like(token)

    out = pl.pallas_call(
        body, name=name,
        out_shape=[pltpu.SemaphoreType.DMA((3 * n,)), pltpu.SemaphoreType.DMA((3 * n,))]
        + [pltpu.HBM(a.shape, a.dtype) for a in lands] + [jax.ShapeDtypeStruct((SUBLANES, LANES), F32)],
        in_specs=[HBM] * n,
        out_specs=[SEM, SEM] + [HBM] * n + [pl.BlockSpec(memory_space=pltpu.VMEM)],
        input_output_aliases={i: 2 + i for i in range(n)},
        compiler_params=pltpu.CompilerParams(has_side_effects=EFFECT),
    )(*[_in_hbm(a) for a in lands])
    return out[0], out[1], out[2:2 + n], out[-1][0, 0]


def forward_wait(send, recv, lands, after, name):
    n = len(lands)

    def body(*refs):
        land = refs[:n]
        send_sem, recv_sem = refs[n], refs[n + 1]
        x, y, c = _place()
        for a in range(n):
            for j, chip in enumerate([(1 - x, y), (x, 1 - y), (1 - x, 1 - y)]):
                cp = pltpu.make_async_remote_copy(
                    src_ref=land[a].at[_slot((*chip, c))], dst_ref=land[a].at[_slot((*chip, 1 - c))],
                    send_sem=send_sem.at[3 * a + j], recv_sem=recv_sem.at[3 * a + j],
                    device_id=(x, y, 1 - c), device_id_type=MESH)
                cp.wait_send()
                cp.wait_recv()

    return pl.pallas_call(
        body, name=name,
        out_shape=[pltpu.HBM(a.shape, a.dtype) for a in lands],
        in_specs=[HBM] * n + [SEM, SEM, ANY],
        out_specs=[HBM] * n,
        input_output_aliases={i: i for i in range(n)},
        compiler_params=pltpu.CompilerParams(has_side_effects=EFFECT),
    )(*lands, send, recv, after)


def _relations():
    return [(dx, dy, dc) for dx in (0, 1) for dy in (0, 1) for dc in (0, 1) if dx + dy + dc]


def _peer(x, y, c, rel):
    return ((1 - x) if rel[0] else x, (1 - y) if rel[1] else y, (1 - c) if rel[2] else c)


def exchange_start(srcs, lands, layer, name, after=()):
    n = len(srcs)
    after = _as_list(after)

    def body(*refs):
        src, land = refs[:n], refs[n:2 * n]
        send_sem, recv_sem = refs[2 * n + len(after)], refs[2 * n + len(after) + 1]
        token = refs[-1]
        x, y, c = _place()
        me = 4 * x + 2 * y + c
        for k, rel in enumerate(_relations()):
            peer = _peer(x, y, c, rel)
            for a in range(n):
                pltpu.make_async_remote_copy(
                    src_ref=src[a] if layer is None else src[a].at[_slot(peer)],
                    dst_ref=land[a].at[me] if layer is None else land[a].at[me, layer],
                    send_sem=send_sem.at[7 * a + k], recv_sem=recv_sem.at[7 * a + k],
                    device_id=peer, device_id_type=MESH).start()
        token[...] = jnp.zeros_like(token)

    out = pl.pallas_call(
        body, name=name,
        out_shape=[pltpu.SemaphoreType.DMA((7 * n,)), pltpu.SemaphoreType.DMA((7 * n,))]
        + [pltpu.HBM(a.shape, a.dtype) for a in list(srcs) + list(lands)]
        + [jax.ShapeDtypeStruct((SUBLANES, LANES), F32)],
        in_specs=[HBM] * (2 * n) + [ANY] * len(after),
        out_specs=[SEM, SEM] + [HBM] * (2 * n) + [pl.BlockSpec(memory_space=pltpu.VMEM)],
        input_output_aliases={i: 2 + i for i in range(2 * n)},
        compiler_params=pltpu.CompilerParams(has_side_effects=EFFECT),
    )(*[_in_hbm(a) for a in list(srcs) + list(lands)], *after)
    return out[0], out[1], out[2:2 + n], out[2 + n:2 + 2 * n], out[-1][0, 0]


def exchange_wait(send, recv, srcs, lands, after, layer, name):
    n = len(srcs)

    def body(*refs):
        src, land = refs[:n], refs[n:2 * n]
        send_sem, recv_sem = refs[2 * n], refs[2 * n + 1]
        x, y, c = _place()
        for k, rel in enumerate(_relations()):
            peer = _peer(x, y, c, rel)
            for a in range(n):
                cp = pltpu.make_async_remote_copy(
                    src_ref=src[a] if layer is None else src[a].at[_slot(peer)],
                    dst_ref=land[a].at[_slot(peer)] if layer is None else land[a].at[_slot(peer), layer],
                    send_sem=send_sem.at[7 * a + k], recv_sem=recv_sem.at[7 * a + k],
                    device_id=peer, device_id_type=MESH)
                cp.wait_send()
                cp.wait_recv()

    out = pl.pallas_call(
        body, name=name,
        out_shape=[pltpu.HBM(a.shape, a.dtype) for a in list(srcs) + list(lands)],
        in_specs=[HBM] * (2 * n) + [SEM, SEM] + [ANY] * len(_as_list(after)),
        out_specs=[HBM] * (2 * n),
        input_output_aliases={i: i for i in range(2 * n)},
        compiler_params=pltpu.CompilerParams(has_side_effects=EFFECT),
    )(*srcs, *lands, send, recv, *_as_list(after))
    return out[:n], out[n:]


def _adamw_math(g, w, m, v):
    m = ADAM_B1 * m + (1.0 - ADAM_B1) * g
    v = ADAM_B2 * v + (1.0 - ADAM_B2) * (g * g)
    m_hat = m / (1.0 - ADAM_B1 ** ADAM_STEP)
    v_hat = v / (1.0 - ADAM_B2 ** ADAM_STEP)
    delta = -ADAM_LR * (m_hat / (jnp.sqrt(v_hat) + ADAM_EPS) + ADAM_WD * w)
    return delta, m, v


def _sum_with_own(p_ref, own_refs, layer, me):
    own = own_refs[0][...]
    for k in range(1, len(own_refs)):
        own = jnp.where(layer == k, own_refs[k][...], own)
    g = None
    for p in range(p_ref.shape[0]):
        term = jnp.where(me == p, own, p_ref[p]).astype(F32)
        g = term if g is None else g + term
    return g


def sum_adamw(parts, owns, me, w, m, v, tr, name):
    npart, rows, cols = parts.shape
    nl = len(owns)
    per_layer = rows // nl // tr

    def body(me_ref, p_ref, *refs):
        own_refs = refs[:nl]
        w_ref, m_ref, v_ref, g_ref, d_ref, nm_ref, nv_ref = refs[nl:]
        g = _sum_with_own(p_ref, own_refs, pl.program_id(0) // per_layer, me_ref[0])
        delta, nm, nv = _adamw_math(g, w_ref[...], m_ref[...], v_ref[...])
        g_ref[...] = g
        d_ref[...] = delta
        nm_ref[...] = nm
        nv_ref[...] = nv

    blk = pl.BlockSpec((tr, cols), lambda i, me_ref: (i, 0))
    own_specs = [pl.BlockSpec((None, tr, cols),
                              lambda i, me_ref, l=l: (me_ref[0], jnp.clip(i - l * per_layer, 0, per_layer - 1), 0))
                 for l in range(nl)]
    return pl.pallas_call(
        body, name=name,
        grid_spec=pltpu.PrefetchScalarGridSpec(
            num_scalar_prefetch=1, grid=(rows // tr,),
            in_specs=[pl.BlockSpec((npart, tr, cols), lambda i, me_ref: (0, i, 0))] + own_specs + [blk, blk, blk],
            out_specs=[blk] * 4),
        out_shape=[jax.ShapeDtypeStruct((rows, cols), F32)] * 4,
        compiler_params=_cparams("arbitrary"),
    )(me, parts, *owns, w, m, v)


def sum_adamw_t(parts, owns, me, w, m, v, name):
    npart, nl, rows, cols = parts.shape

    def body(me_ref, p_ref, *refs):
        own_refs = refs[:nl]
        w_ref, m_ref, v_ref, g_ref, d_ref, nm_ref, nv_ref = refs[nl:]
        g = _sum_with_own(p_ref, own_refs, pl.program_id(0), me_ref[0])
        delta, nm, nv = _adamw_math(g, w_ref[...], m_ref[...], v_ref[...])
        g_ref[...] = g
        d_ref[...] = delta
        nm_ref[...] = nm
        nv_ref[...] = nv

    blk = pl.BlockSpec((None, rows, cols), lambda l, me_ref: (l, 0, 0))
    own_specs = [pl.BlockSpec((None, rows, cols), lambda l, me_ref: (me_ref[0], 0, 0)) for _ in range(nl)]
    return pl.pallas_call(
        body, name=name,
        grid_spec=pltpu.PrefetchScalarGridSpec(
            num_scalar_prefetch=1, grid=(nl,),
            in_specs=[pl.BlockSpec((npart, None, rows, cols), lambda l, me_ref: (0, l, 0, 0))] + own_specs
            + [blk, blk, blk],
            out_specs=[blk] * 4),
        out_shape=[jax.ShapeDtypeStruct((nl, rows, cols), F32)] * 4,
        compiler_params=_cparams("arbitrary"),
    )(me, parts, *owns, w, m, v)


def adamw_group(gs, ws, ms, vs, name):
    n = len(gs)

    def body(*refs):
        g, w, m, v, outs = refs[:n], refs[n:2 * n], refs[2 * n:3 * n], refs[3 * n:4 * n], refs[4 * n:]
        for i in range(n):
            delta, nm, nv = _adamw_math(g[i][...], w[i][...], m[i][...], v[i][...])
            outs[i][...] = delta
            outs[n + i][...] = nm
            outs[2 * n + i][...] = nv

    vmem = pl.BlockSpec(memory_space=pltpu.VMEM)
    out = pl.pallas_call(
        body, name=name,
        in_specs=[vmem] * (4 * n), out_specs=[vmem] * (3 * n),
        out_shape=[jax.ShapeDtypeStruct(a.shape, F32) for a in list(ws) * 3],
        compiler_params=_cparams(),
    )(*gs, *ws, *ms, *vs)
    return out[:n], out[n:2 * n], out[2 * n:]


def sum_parts(parts, name):
    npart, rows, cols = parts.shape

    def body(p_ref, g_ref):
        g = p_ref[0].astype(F32)
        for p in range(1, npart):
            g = g + p_ref[p].astype(F32)
        g_ref[...] = g

    return pl.pallas_call(
        body, name=name, grid=(1,),
        in_specs=[pl.BlockSpec((npart, rows, cols), lambda i: (0, 0, 0))],
        out_specs=pl.BlockSpec((rows, cols), lambda i: (0, 0)),
        out_shape=jax.ShapeDtypeStruct((rows, cols), F32),
        compiler_params=_cparams("arbitrary"),
    )(parts)


def _round_up(n, m):
    return (n + m - 1) // m * m


def _block_diag_pairs(w):
    nb, b, _ = w.shape
    per = LANES // b
    ng = nb // per
    w = w.reshape(ng, per, b, b)
    eye = jnp.eye(per, dtype=w.dtype)
    out = jnp.einsum('gpij,pq->gpiqj', w, eye).reshape(ng, LANES, LANES)
    return out.astype(BF16)


def _block_diag_extract(g, b):
    ng = g.shape[0]
    per = LANES // b
    g = g.reshape(ng, per, b, per, b)
    idx = jnp.arange(per)
    return g[:, idx, :, idx, :].transpose(1, 0, 2, 3).reshape(ng * per, b, b)


def _tiles(v):
    v = v.reshape(-1)
    n = _round_up(v.shape[0], SUBLANES * LANES)
    return jnp.pad(v, (0, n - v.shape[0])).reshape(-1, LANES)


SMALL = ['attn_norm_g', 'b_f', 'conv_w', 'conv_b', 'w_gate_a', 'b_gate_a', 'w_gate_x', 'b_gate_x',
         'lru_L', 'attn_out_g', 'rec_out_g', 'mlp_norm_g', 'final_g', 'meta']


def _pack(d):
    return jnp.concatenate([_tiles(d[n]) for n in SMALL], axis=0)


def _unpack(vec, shapes):
    out, r = {}, 0
    for n in SMALL:
        size = math.prod(shapes[n])
        nr = _round_up(size, SUBLANES * LANES) // LANES
        out[n] = vec[r:r + nr].reshape(-1)[:size].reshape(shapes[n])
        r += nr
    return out


def _row_tile(tp):
    return tp // 4 if (tp // 4) % 16 == 0 else tp


def local_step(x, tgt, meta, small, hooks):
    s, d = x.shape
    t_real = s + N_META
    tp = _round_up(t_real, ATT_BLOCK)
    depth = small['attn_norm_g'].shape[0]
    nh = small['b_f'].shape[1]
    rw = small['conv_b'].shape[1]
    blk = small['w_gate_a'].shape[2]
    tm = _row_tile(tp)
    tm2 = tp // 2
    fcol = 2 * rw // LANES

    h = jnp.concatenate([meta, x, jnp.zeros((tp - t_real, d), F32)], axis=0)
    tgt_p = jnp.pad(tgt, ((N_META, tp - t_real), (0, 0)))
    row = lambda v: v.reshape(1, -1)
    bf_pad = jnp.pad(small['b_f'], ((0, 0), (0, LANES - nh)))

    saved = []
    for l in range(depth):
        w_in_t, wrest_t, wout, tok_w = hooks.mixer_weights(l, h)
        wga = _block_diag_pairs(small['w_gate_a'][l])
        wgx = _block_diag_pairs(small['w_gate_x'][l])
        z, qkv, rest = in_proj(h, row(small['attn_norm_g'][l]) + tok_w, w_in_t, wrest_t, 3 * nh * HEAD_DIM, tm)
        c, ct = fgate_fwd(rest, bf_pad[l:l + 1], fcol)
        o, lset = attn_fwd(qkv, c, ct, nh)
        rec, hr, xc, *gates = rec_fwd(rest, small['conv_w'][l], row(small['conv_b'][l]), wga,
                                      row(small['b_gate_a'][l]), wgx, row(small['b_gate_x'][l]),
                                      row(small['lru_L'][l]), rw)
        gup, gdown, tok_w = hooks.mlp_weights(l, rec)
        h2, mix, z2 = out_proj(h, o, rec, row(small['attn_out_g'][l]), row(small['rec_out_g'][l]), wout,
                               row(small['mlp_norm_g'][l]) + tok_w, tm)
        u, h3 = mlp_fwd(z2, h2, gup, gdown, tm)
        saved.append(dict(h=h, z=z, qkv=qkv, rest=rest, c=c, ct=ct, o=o, lset=lset, rec=rec, hr=hr, xc=xc,
                          h2=h2, mix=mix, z2=z2, u=u, wga=wga, wgx=wgx, gates=gates,
                          w_in_t=w_in_t, wrest_t=wrest_t, wout=wout, gup=gup, gdown=gdown))
        h = h3

    dh, dgf, loss = loss_head(h, row(small['final_g']), tgt_p, t_real, tm)

    gs = {n: [None] * depth for n in SMALL if n not in ('final_g', 'meta')}
    tok = jnp.zeros((), F32)
    for l in reversed(range(depth)):
        sv = saved[l]
        gup, gdown = sv['gup'], sv['gdown']
        tf = gup.shape[2]
        dup, dh2, dg2, dhb = mlp_bwd(dh, sv['u'], sv['h2'], row(small['mlp_norm_g'][l]) + tok, gup, gdown, tm)
        gs['mlp_norm_g'][l] = dg2[0]
        do, drec, dga, dgr = out_proj_bwd(dh2, sv['o'], sv['rec'], row(small['attn_out_g'][l]),
                                          row(small['rec_out_g'][l]), sv['wout'], tm)
        gs['attn_out_g'][l] = dga[0]
        gs['rec_out_g'][l] = dgr[0]
        dw_down, dw_up = dw_mlp(sv['u'], dhb, sv['z2'], dup, tf)
        blocks = dict(
            w_down=dw_down, w_up=dw_up,
            w_out=mm_tn(sv['mix'], dh2, tn=d // 2, name="dw_out").reshape(N_DEV, d // N_DEV, d))
        tok = hooks.grads_ready(l, 'mlp', blocks)
        dxr, dyr, dwga, dwgx, vec = rec_bwd(drec, sv['hr'], sv['xc'], sv['gates'], sv['rest'], small['conv_w'][l],
                                            sv['wga'], sv['wgx'], row(small['lru_L'][l]) + tok, rw)
        gs['w_gate_a'][l] = _block_diag_extract(dwga, blk)
        gs['w_gate_x'][l] = _block_diag_extract(dwgx, blk)
        vec = vec.transpose(1, 0, 2).reshape(SUBLANES, rw)
        gs['conv_w'][l] = vec[0:CONV_WIDTH]
        gs['conv_b'][l] = vec[4]
        gs['b_gate_a'][l] = vec[5]
        gs['b_gate_x'][l] = vec[6]
        gs['lru_L'][l] = vec[7]
        dq, dk, dv, drow, dcol = attn_bwd(sv['qkv'], do, sv['o'], sv['lset'], sv['c'], sv['ct'] + tok, nh)
        drow8 = drow[:, 0:2, :].reshape(nh, tp)
        if nh < SUBLANES:
            drow8 = jnp.pad(drow8, ((0, SUBLANES - nh), (0, 0)))
        dfl, dbf = fgate_bwd(drow8, dcol, sv['rest'], bf_pad[l:l + 1], fcol)
        gs['b_f'][l] = dbf[0, 0:nh]
        parts = (dq, dk, dv, dxr, dyr, dfl)
        dh, dg1 = in_proj_bwd(dh2, parts, sv['w_in_t'], sv['wrest_t'], sv['h'], row(small['attn_norm_g'][l]), tm)
        gs['attn_norm_g'][l] = dg1[0]
        first = ()
        if l == 0:
            grads = {n: jnp.stack(v) for n, v in gs.items()}
            grads['final_g'] = dgf[0]
            grads['meta'] = dh[0:N_META]
            first = hooks.small_ready(grads)
        dw_in = dw_in_t(sv['z'], parts, nh, tm2)
        dw_in = dw_in.reshape(N_DEV, dw_in.shape[0] // N_DEV, d)
        tok = hooks.grads_ready(l, 'in', dict(w_in=dw_in), first)

    return loss[0, 0], dh


def prep_weights(g_in, g_out, nh, rw):
    d = g_in.shape[2]
    w_in_t = g_in.reshape(-1, d)
    f0 = 3 * nh * HEAD_DIM
    wrest_t = jnp.concatenate([w_in_t[f0 + nh:f0 + nh + 2 * rw],
                               jnp.pad(w_in_t[f0:f0 + nh], ((0, LANES - nh), (0, 0)))], axis=0)
    return w_in_t, wrest_t, g_out.reshape(d, d)


BIG = ['w_in', 'w_out', 'w_up', 'w_down']
EXCHANGE_GROUPS = {'mlp': ['w_down', 'w_up', 'w_out'], 'in': ['w_in']}
WEIGHTS = ['meta', 'attn_norm_g', 'w_in', 'b_f', 'conv_w', 'conv_b', 'w_gate_a', 'b_gate_a', 'w_gate_x', 'b_gate_x',
           'lru_L', 'attn_out_g', 'rec_out_g', 'w_out', 'mlp_norm_g', 'w_up', 'w_down', 'final_g']


def _set_own(arr, own, me):
    return lax.dynamic_update_slice_in_dim(arr, own[None], me, axis=0)


class _Step:
    def __init__(self, w, nh, rw, me):
        self.w, self.nh, self.rw, self.me = w, nh, rw, me
        depth = w['w_in'].shape[0]
        first = [w['w_in_t'][:, 0, :].astype(BF16), w['w_out'][0].astype(BF16), w['meta'], w['conv_w']]
        self.pending, token = gather_start([first], "gather_start_0")
        zero = token[0, 0].astype(BF16)
        groups = [[w['w_up'][0].astype(BF16) + zero, w['w_down'][0].astype(BF16) + zero]]
        for l in range(1, depth):
            groups.append([w['w_in_t'][:, l, :].astype(BF16) + zero, w['w_out'][l].astype(BF16) + zero])
            groups.append([w['w_up'][l].astype(BF16) + zero, w['w_down'][l].astype(BF16) + zero])
        rest, _ = gather_start(groups, "gather_start_1")
        self.pending += rest
        self.first_after = rest[0][2][0]
        self.gathered = {}
        self.passing = {}
        self.token = jnp.zeros((), F32)
        self.lands = {n: lax.empty((N_DEV,) + w[n].shape, BF16) for n in BIG}
        din8, _, d = w['w_in_t'].shape
        self.lands['w_in'] = lax.empty((N_DEV, depth, din8, d), BF16)
        self.started = []
        self.small = None

    def _pass_on(self, gi, after):
        if gi < len(self.pending) and gi not in self.passing:
            send, recv, srcs, lands = self.pending[gi]
            srcs, lands = gather_wait(send, recv, srcs, lands, after, "gather_wait_%d" % gi)
            fsend, frecv, lands, token = forward_start(lands, "forward_start_%d" % gi)
            self.passing[gi] = (fsend, frecv, srcs, lands)
            self.token = token

    def group(self, gi, after):
        if gi not in self.gathered:
            self._pass_on(gi, after)
            fsend, frecv, srcs, lands = self.passing[gi]
            lands = forward_wait(fsend, frecv, lands, after, "forward_wait_%d" % gi)
            self.gathered[gi] = [_set_own(g, own, self.me) for g, own in zip(lands, srcs)]
            if gi >= 2:
                self._pass_on(gi + 1, lands[0])
        return self.gathered[gi]

    def mixer_weights(self, l, after):
        g = self.group(2 * l, after)
        return (*prep_weights(g[0], g[1], self.nh, self.rw), self.token)

    def mlp_weights(self, l, after):
        g = self.group(2 * l + 1, after)
        return g[0], g[1], self.token

    def grads_ready(self, l, group, blocks, after=()):
        names = EXCHANGE_GROUPS[group]
        send, recv, srcs, lands, token = exchange_start(
            [blocks[n] for n in names], [self.lands[n] for n in names], l, "exchange_start_%s_%d" % (group, l),
            after)
        for n, a in zip(names, lands):
            self.lands[n] = a
        self.started.append((l, group, send, recv, srcs))
        return token

    def small_ready(self, grads):
        self.small_shapes = {n: grads[n].shape for n in SMALL}
        packed = _pack(grads).astype(BF16)
        send, recv, srcs, lands, token = exchange_start(
            [packed], [lax.empty((N_DEV,) + packed.shape, BF16)], None, "small_start")
        self.small = (send, recv, srcs, lands)
        return srcs[0]

    def small_sum(self, after):
        send, recv, srcs, lands = self.small
        srcs, lands = exchange_wait(send, recv, srcs, lands, after, None, "small_wait")
        parts = _set_own(lands[0], srcs[0], self.me)
        return _unpack(sum_parts(parts, "sum_small_grads"), self.small_shapes)

    def received(self, group, after):
        names = EXCHANGE_GROUPS[group]
        own = {n: [None] * self.w[n].shape[0] for n in names}
        for l, grp, send, recv, srcs in self.started:
            if grp != group:
                continue
            srcs, lands = exchange_wait(send, recv, srcs, [self.lands[n] for n in names], after, l,
                                        "exchange_wait_%s_%d" % (group, l))
            for n, a, sr in zip(names, lands, srcs):
                self.lands[n] = a
                own[n][l] = sr
        return {n: (self.lands[n], own[n]) for n in names}


def kernel(x, meta, attn_norm_g, w_in, b_f, conv_w, conv_b, w_gate_a, b_gate_a, w_gate_x, b_gate_x, lru_L, attn_out_g, rec_out_g, w_out, mlp_norm_g, w_up, w_down, final_g, loss_target, m_meta, m_attn_norm_g, m_w_in, m_b_f, m_conv_w, m_conv_b, m_w_gate_a, m_b_gate_a, m_w_gate_x, m_b_gate_x, m_lru_L, m_attn_out_g, m_rec_out_g, m_w_out, m_mlp_norm_g, m_w_up, m_w_down, m_final_g, v_meta, v_attn_norm_g, v_w_in, v_b_f, v_conv_w, v_conv_b, v_w_gate_a, v_b_gate_a, v_w_gate_x, v_b_gate_x, v_lru_L, v_attn_out_g, v_rec_out_g, v_w_out, v_mlp_norm_g, v_w_up, v_w_down, v_final_g):
    w = dict(meta=meta, attn_norm_g=attn_norm_g, w_in=w_in, b_f=b_f, conv_w=conv_w, conv_b=conv_b,
             w_gate_a=w_gate_a, b_gate_a=b_gate_a, w_gate_x=w_gate_x, b_gate_x=b_gate_x, lru_L=lru_L,
             attn_out_g=attn_out_g, rec_out_g=rec_out_g, w_out=w_out, mlp_norm_g=mlp_norm_g, w_up=w_up,
             w_down=w_down, final_g=final_g)
    mo = dict(meta=m_meta, attn_norm_g=m_attn_norm_g, w_in=m_w_in, b_f=m_b_f, conv_w=m_conv_w, conv_b=m_conv_b,
              w_gate_a=m_w_gate_a, b_gate_a=m_b_gate_a, w_gate_x=m_w_gate_x, b_gate_x=m_b_gate_x, lru_L=m_lru_L,
              attn_out_g=m_attn_out_g, rec_out_g=m_rec_out_g, w_out=m_w_out, mlp_norm_g=m_mlp_norm_g,
              w_up=m_w_up, w_down=m_w_down, final_g=m_final_g)
    vo = dict(meta=v_meta, attn_norm_g=v_attn_norm_g, w_in=v_w_in, b_f=v_b_f, conv_w=v_conv_w, conv_b=v_conv_b,
              w_gate_a=v_w_gate_a, b_gate_a=v_b_gate_a, w_gate_x=v_w_gate_x, b_gate_x=v_b_gate_x, lru_L=v_lru_L,
              attn_out_g=v_attn_out_g, rec_out_g=v_rec_out_g, w_out=v_w_out, mlp_norm_g=v_mlp_norm_g,
              w_up=v_w_up, w_down=v_w_down, final_g=v_final_g)
    depth = w_in.shape[0]
    nh = b_f.shape[1]
    rw = conv_b.shape[1]
    me = 4 * lax.axis_index("x") + 2 * lax.axis_index("y") + lax.axis_index("c")

    w['w_in_t'] = jnp.transpose(w_in, (2, 0, 1))
    swap = lambda a: jnp.swapaxes(a, 1, 2)
    step = _Step(w, nh, rw, me)
    g0 = step.group(0, step.first_after)
    meta_full = g0[2].transpose(1, 0, 2).reshape(N_META, -1)
    conv_full = g0[3].transpose(1, 2, 0, 3).reshape(depth, CONV_WIDTH, rw)
    small = {n: w[n] for n in SMALL}
    small['conv_w'] = conv_full

    loss_part, dh0 = local_step(x[0], loss_target[0], meta_full, small, step)
    loss = lax.psum(loss_part, ("x", "y", "c"))
    grad_x = dh0[N_META:N_META + x.shape[1]][None]

    out_g, out_d, out_m, out_v = {}, {}, {}, {}
    me1 = me.reshape(1).astype(jnp.int32)

    def update_big(group, after):
        for n, (r, owns) in step.received(group, after).items():
            if n == 'w_in':
                out = sum_adamw_t(r, owns, me1, swap(w[n]), swap(mo[n]), swap(vo[n]), "adamw_w_in")
                out = [swap(a) for a in out]
            else:
                shp = w[n].shape
                rows, cols = shp[0] * shp[1], shp[2]
                tr = min(512 if cols <= 512 else 256, shp[1])
                out = sum_adamw(r.reshape(N_DEV, rows, cols), owns, me1, w[n].reshape(rows, cols),
                                mo[n].reshape(rows, cols), vo[n].reshape(rows, cols), tr, "adamw_" + n)
                out = [a.reshape(shp) for a in out]
            out_g[n], out_d[n], out_m[n], out_v[n] = out
            after = out[0]
        return after

    update_big('mlp', step.started[-1][4][0])

    gsum = step.small_sum([out_g[n] for n in EXCHANGE_GROUPS['mlp']])
    gsum['meta'] = lax.dynamic_slice_in_dim(gsum['meta'], me * meta.shape[1], meta.shape[1], axis=1)
    gsum['conv_w'] = lax.dynamic_slice_in_dim(gsum['conv_w'], me * conv_w.shape[2], conv_w.shape[2], axis=2)
    as2d = lambda a: a.reshape(-1, a.shape[-1])
    deltas, new_m, new_v = adamw_group([as2d(gsum[n]) for n in SMALL], [as2d(w[n]) for n in SMALL],
                                       [as2d(mo[n]) for n in SMALL], [as2d(vo[n]) for n in SMALL], "adamw_small")
    for i, n in enumerate(SMALL):
        out_g[n] = gsum[n]
        out_d[n], out_m[n], out_v[n] = [a[i].reshape(w[n].shape) for a in (deltas, new_m, new_v)]

    update_big('in', deltas[0])

    return (loss, grad_x, *[out_g[n] for n in WEIGHTS], *[out_d[n] for n in WEIGHTS],
            *[out_m[n] for n in WEIGHTS], *[out_v[n] for n in WEIGHTS])
```

```python
import math

import jax
import jax.numpy as jnp
from jax import lax
from jax.experimental import pallas as pl
from jax.experimental.pallas import tpu as pltpu

F32 = jnp.float32
BF16 = jnp.bfloat16

N_DEV = 8
N_META = 16
HEAD_DIM = 64
CONV_WIDTH = 4
RG_C = 8.0
NORM_EPS = 1e-6
LANES = 128
SUBLANES = 8
ATT_BLOCK = 128
ATT_TQ = 512
NEG_BIG = -1e30
ATT_SCALE = 1.0 / math.sqrt(HEAD_DIM)

ADAM_LR = 0.001
ADAM_B1 = 0.9
ADAM_B2 = 0.999
ADAM_EPS = 1e-08
ADAM_WD = 0.01
ADAM_STEP = 10

VMEM_LIMIT_BYTES = 56 * 1024 * 1024
MESH = pl.DeviceIdType.MESH
ANY = pl.BlockSpec(memory_space=pl.ANY)


def _cparams(*sem):
    return pltpu.CompilerParams(dimension_semantics=sem if sem else None,
                                vmem_limit_bytes=VMEM_LIMIT_BYTES)


def _dot(a, b):
    return jnp.dot(a, b, preferred_element_type=F32)


def _dot_nt(a, b):
    return lax.dot_general(a, b, (((1,), (1,)), ((), ())), preferred_element_type=F32)


def _dot_tn(a, b):
    return lax.dot_general(a, b, (((0,), (0,)), ((), ())), preferred_element_type=F32)


def _sigmoid(x):
    return 0.5 * (1.0 + jnp.tanh(0.5 * x))


def _log_sigmoid(x):
    return jnp.minimum(x, 0.0) - jnp.log(1.0 + jnp.exp(-jnp.abs(x)))


def _expm1(x):
    series = x * (1.0 + x * (0.5 + x * (1.0 / 6.0 + x * (1.0 / 24.0))))
    return jnp.where(jnp.abs(x) < 1e-2, series, jnp.exp(x) - 1.0)


_GELU_K = math.sqrt(2.0 / math.pi)
_GELU_C = 0.044715


def _gelu(x):
    t = jnp.tanh(_GELU_K * (x + _GELU_C * x * x * x))
    return 0.5 * x * (1.0 + t)


def _gelu_and_grad(x):
    x2 = x * x
    t = jnp.tanh(_GELU_K * (x + _GELU_C * x2 * x))
    half = 0.5 * (1.0 + t)
    return x * half, half + 0.5 * x * (1.0 - t * t) * _GELU_K * (1.0 + 3.0 * _GELU_C * x2)


def _split3_dot(tri, x):
    hi = x.astype(BF16)
    r1 = x - hi.astype(F32)
    mid = r1.astype(BF16)
    lo = (r1 - mid.astype(F32)).astype(BF16)
    return _dot(tri, hi) + _dot(tri, mid) + _dot(tri, lo)


def _dot_split3(x, sel):
    hi = x.astype(BF16)
    r1 = x - hi.astype(F32)
    mid = r1.astype(BF16)
    lo = (r1 - mid.astype(F32)).astype(BF16)
    return _dot(hi, sel) + _dot(mid, sel) + _dot(lo, sel)


def _rms_fwd(x, g):
    r = lax.rsqrt(jnp.mean(x * x, axis=-1, keepdims=True) + NORM_EPS)
    return x * r * g


def _rms_bwd(x, g, dy):
    r = lax.rsqrt(jnp.mean(x * x, axis=-1, keepdims=True) + NORM_EPS)
    xn = x * r
    dxn = dy * g
    dx = r * (dxn - xn * jnp.mean(dxn * xn, axis=-1, keepdims=True))
    return dx, jnp.sum(dy * xn, axis=0, keepdims=True)


def _accumulate(ref, val, first):
    @pl.when(first)
    def _():
        ref[...] = val

    @pl.when(jnp.logical_not(first))
    def _():
        ref[...] += val


def in_proj(h, g1, w_in_t, wrest_t, nq, tm):
    tp, d = h.shape
    nr = wrest_t.shape[0]

    def body(h_ref, g_ref, wq_ref, wr_ref, z_ref, qkv_ref, rest_ref):
        z = _rms_fwd(h_ref[...], g_ref[...]).astype(BF16)
        z_ref[...] = z
        qkv_ref[...] = _dot_nt(z, wq_ref[...]).astype(BF16)
        rest_ref[...] = _dot_nt(z, wr_ref[...])

    return pl.pallas_call(
        body, name="in_proj", grid=(tp // tm,),
        in_specs=[pl.BlockSpec((tm, d), lambda i: (i, 0)),
                  pl.BlockSpec((1, d), lambda i: (0, 0)),
                  pl.BlockSpec((nq, d), lambda i: (0, 0)),
                  pl.BlockSpec((nr, d), lambda i: (0, 0))],
        out_specs=[pl.BlockSpec((tm, d), lambda i: (i, 0)),
                   pl.BlockSpec((tm, nq), lambda i: (i, 0)),
                   pl.BlockSpec((tm, nr), lambda i: (i, 0))],
        out_shape=[jax.ShapeDtypeStruct((tp, d), BF16),
                   jax.ShapeDtypeStruct((tp, nq), BF16),
                   jax.ShapeDtypeStruct((tp, nr), F32)],
        compiler_params=_cparams("parallel"),
    )(h, g1, w_in_t, wrest_t)


def fgate_fwd(rest, bf_pad, fcol):
    tp = rest.shape[0]
    nb = tp // ATT_BLOCK

    def body(f_ref, b_ref, c_ref, ct_ref):
        r_i = lax.broadcasted_iota(jnp.int32, (ATT_BLOCK, ATT_BLOCK), 0)
        c_i = lax.broadcasted_iota(jnp.int32, (ATT_BLOCK, ATT_BLOCK), 1)
        tri = (r_i >= c_i).astype(BF16)
        carry = jnp.zeros((1, LANES), F32)
        for i in range(nb):
            sl = slice(i * ATT_BLOCK, (i + 1) * ATT_BLOCK)
            lf = _log_sigmoid(f_ref[sl, :] + b_ref[...])
            cs = _split3_dot(tri, lf) + carry
            carry = cs[ATT_BLOCK - 1:ATT_BLOCK, :]
            c_ref[sl, :] = cs
            ct_ref[:, sl] = cs.T[0:SUBLANES, :]

    return pl.pallas_call(
        body, name="fgate_fwd", grid=(1,),
        in_specs=[pl.BlockSpec((tp, LANES), lambda i: (0, fcol)),
                  pl.BlockSpec((1, LANES), lambda i: (0, 0))],
        out_specs=[pl.BlockSpec((tp, LANES), lambda i: (0, 0)),
                   pl.BlockSpec((SUBLANES, tp), lambda i: (0, 0))],
        out_shape=[jax.ShapeDtypeStruct((tp, LANES), F32),
                   jax.ShapeDtypeStruct((SUBLANES, tp), F32)],
        compiler_params=_cparams("arbitrary"),
    )(rest, bf_pad)


def _pick_col(blk, head):
    lane = lax.broadcasted_iota(jnp.int32, blk.shape, 1)
    return jnp.sum(jnp.where(lane == head, blk, 0.0), axis=1, keepdims=True)


def _pick_row(blk, head):
    sub = lax.broadcasted_iota(jnp.int32, blk.shape, 0)
    return jnp.sum(jnp.where(sub == head, blk, 0.0), axis=0, keepdims=True)


def _att_tiles(tp):
    out, r0 = [], 0
    while r0 < tp:
        rows = min(ATT_TQ, tp - r0)
        out.append((r0, rows, r0 + rows))
        r0 += rows
    return out


def attn_fwd(qkv, c, ct, nh):
    tp = qkv.shape[0]
    npair = nh // 2
    tiles = _att_tiles(tp)

    def body(q_ref, k_ref, v_ref, c_ref, ct_ref, o_ref, lset_ref):
        p = pl.program_id(0)
        lset_ref[...] = jnp.zeros_like(lset_ref)
        for r0, nr, nk in tiles:
            rs = slice(r0, r0 + nr)
            causal = (r0 + lax.broadcasted_iota(jnp.int32, (nr, nk), 0)
                      >= lax.broadcasted_iota(jnp.int32, (nr, nk), 1))
            cblk = c_ref[rs, :]
            ctb = ct_ref[:, 0:nk]
            for hh in range(2):
                head = 2 * p + hh
                hs = slice(hh * HEAD_DIM, (hh + 1) * HEAD_DIM)
                q = q_ref[rs, hs] * ATT_SCALE
                s = _dot_nt(q, k_ref[0:nk, hs]) + (_pick_col(cblk, head) - _pick_row(ctb, head))
                s = jnp.where(causal, s, NEG_BIG)
                m = jnp.max(s, axis=1, keepdims=True)
                pm = jnp.exp(s - m)
                l = jnp.sum(pm, axis=1, keepdims=True)
                o_ref[rs, hs] = _dot(pm.astype(BF16), v_ref[0:nk, hs]) / l
                lse = m + jnp.log(l)
                lset_ref[hh:hh + 1, rs] = jnp.broadcast_to(lse, (nr, LANES)).T[0:1, :]

    pair = lambda p: (0, p)
    return pl.pallas_call(
        body, name="attn_fwd", grid=(npair,),
        in_specs=[pl.BlockSpec((tp, LANES), pair),
                  pl.BlockSpec((tp, LANES), lambda p: (0, npair + p)),
                  pl.BlockSpec((tp, LANES), lambda p: (0, 2 * npair + p)),
                  pl.BlockSpec((tp, LANES), lambda p: (0, 0)),
                  pl.BlockSpec((SUBLANES, tp), lambda p: (0, 0))],
        out_specs=[pl.BlockSpec((tp, LANES), pair),
                   pl.BlockSpec((None, SUBLANES, tp), lambda p: (p, 0, 0))],
        out_shape=[jax.ShapeDtypeStruct((tp, nh * HEAD_DIM), F32),
                   jax.ShapeDtypeStruct((npair, SUBLANES, tp), F32)],
        compiler_params=_cparams("parallel"),
    )(qkv, qkv, qkv, c, ct)


def _shift_down(x, k, n):
    if k == 0:
        return x
    rows = lax.broadcasted_iota(jnp.int32, x.shape, 0)
    return jnp.where(rows >= k, pltpu.roll(x, k, 0), 0.0)


def _shift_up(x, k, n):
    if k == 0:
        return x
    rows = lax.broadcasted_iota(jnp.int32, x.shape, 0)
    return jnp.where(rows < n - k, pltpu.roll(x, n - k, 0), 0.0)


def _conv_fwd(xr, cw_ref, cb_ref, n):
    xc = cw_ref[CONV_WIDTH - 1:CONV_WIDTH, :] * xr + cb_ref[...]
    for k in range(1, CONV_WIDTH):
        xc = xc + cw_ref[CONV_WIDTH - 1 - k:CONV_WIDTH - k, :] * _shift_down(xr, k, n)
    return xc


def _gates(xc, wga_ref, bga_ref, wgx_ref, bgx_ref, l_ref):
    xcb = xc.astype(BF16)
    r = _sigmoid(_dot(xcb, wga_ref[...]) + bga_ref[...])
    ig = _sigmoid(_dot(xcb, wgx_ref[...]) + bgx_ref[...])
    ls = _log_sigmoid(l_ref[...])
    log_a = RG_C * r * ls
    a = jnp.exp(log_a)
    mult = jnp.sqrt(-_expm1(2.0 * log_a))
    return xcb, r, ig, ls, log_a, a, mult


SCAN_UNROLL = 4


def _scan_rows(a_s, u_s, out_ref, n, reverse):
    nt = n // SUBLANES
    per = SCAN_UNROLL if nt % SCAN_UNROLL == 0 else 1
    row = lax.broadcasted_iota(jnp.int32, (SUBLANES, LANES), 0)
    last = 0 if reverse else SUBLANES - 1

    def tile_scan(a, u):
        for d in (1, 2, 4):
            if reverse:
                keep = row < SUBLANES - d
                sh = SUBLANES - d
            else:
                keep = row >= d
                sh = d
            a_sh = jnp.where(keep, pltpu.roll(a, sh, 0), 1.0)
            u_sh = jnp.where(keep, pltpu.roll(u, sh, 0), 0.0)
            u = a * u_sh + u
            a = a * a_sh
        return a, u

    def step(t, carry):
        tiles = []
        for k in range(per):
            tt = t * per + k
            if reverse:
                tt = nt - 1 - tt
            off = pl.multiple_of(tt * SUBLANES, SUBLANES)
            a, u = tile_scan(a_s[pl.ds(off, SUBLANES), :], u_s[pl.ds(off, SUBLANES), :])
            tiles.append((off, a, u))
        for off, a, u in tiles:
            out_ref[pl.ds(off, SUBLANES), :] = u + a * carry
            carry = u[last:last + 1, :] + a[last:last + 1, :] * carry
        return carry

    lax.fori_loop(0, nt // per, step, jnp.zeros((1, LANES), F32))


def rec_fwd(rest, convw, convb, wga, bga, wgx, bgx, lru, rw):
    tp = rest.shape[0]
    ng = rw // LANES

    def body(xr_ref, yr_ref, cw_ref, cb_ref, wga_ref, bga_ref, wgx_ref, bgx_ref, l_ref,
             rec_ref, hr_ref, xc_ref, r_ref, ig_ref, a_ref, mult_ref, u_s):
        xc = _conv_fwd(xr_ref[...], cw_ref, cb_ref, tp)
        xc_ref[...] = xc
        _, r, ig, ls, log_a, a, mult = _gates(xc, wga_ref, bga_ref, wgx_ref, bgx_ref, l_ref)
        r_ref[...] = r
        ig_ref[...] = ig
        a_ref[...] = a
        mult_ref[...] = mult
        u_s[...] = mult * ig * xc
        _scan_rows(a_ref, u_s, hr_ref, tp, reverse=False)
        rec_ref[...] = hr_ref[...] * _gelu(yr_ref[...])

    col = lambda g: (0, g)
    vec = pl.BlockSpec((1, LANES), col)
    big = pl.BlockSpec((tp, LANES), col)
    return pl.pallas_call(
        body, name="rec_fwd", grid=(ng,),
        in_specs=[big, pl.BlockSpec((tp, LANES), lambda g: (0, ng + g)),
                  pl.BlockSpec((CONV_WIDTH, LANES), col), vec,
                  pl.BlockSpec((None, LANES, LANES), lambda g: (g, 0, 0)), vec,
                  pl.BlockSpec((None, LANES, LANES), lambda g: (g, 0, 0)), vec, vec],
        out_specs=[big] * 7,
        out_shape=[jax.ShapeDtypeStruct((tp, rw), F32)] * 7,
        scratch_shapes=[pltpu.VMEM((tp, LANES), F32)],
        compiler_params=_cparams("parallel"),
    )(rest, rest, convw, convb, wga, bga, wgx, bgx, lru)


def out_mlp_fwd(h, o, rec, ga, gr, wout, g2, gup, gdown, tm):
    tp, d = h.shape
    aw, rw = o.shape[1], rec.shape[1]
    nf = gup.shape[0]
    tf = gup.shape[2]
    nb = MLP_BLOCKS if nf % MLP_BLOCKS == 0 else 1
    nj = nf // nb

    def body(h_ref, o_ref, rec_ref, ga_ref, gr_ref, w_ref, g2_ref, wu_ref, wd_ref,
             h2_ref, mix_ref, z2_ref, u_ref, h3_ref, acc):
        j = pl.program_id(1)

        @pl.when(j == 0)
        def _():
            mix_ref[:, 0:aw] = _rms_fwd(o_ref[...], ga_ref[...]).astype(BF16)
            mix_ref[:, aw:aw + rw] = _rms_fwd(rec_ref[...], gr_ref[...]).astype(BF16)
            h2 = h_ref[...] + _dot(mix_ref[...], w_ref[...])
            h2_ref[...] = h2
            acc[...] = h2
            z2_ref[...] = _rms_fwd(h2, g2_ref[...]).astype(BF16)

        z = z2_ref[...]
        part = None
        for b in range(nb):
            u = jnp.maximum(_dot(z, wu_ref[b]), 0.0)
            u_ref[:, b * tf:(b + 1) * tf] = u.astype(BF16)
            p = _dot((u * u).astype(BF16), wd_ref[b])
            part = p if part is None else part + p
        acc[...] += part

        @pl.when(j == nj - 1)
        def _():
            h3_ref[...] = acc[...]

    row = lambda i, j: (i, 0)
    fix = lambda i, j: (0, 0)
    return pl.pallas_call(
        body, name="out_mlp_fwd", grid=(tp // tm, nj),
        in_specs=[pl.BlockSpec((tm, d), row), pl.BlockSpec((tm, aw), row), pl.BlockSpec((tm, rw), row),
                  pl.BlockSpec((1, aw), fix), pl.BlockSpec((1, rw), fix),
                  pl.BlockSpec((d, d), fix), pl.BlockSpec((1, d), fix),
                  pl.BlockSpec((nb, d, tf), lambda i, j: (j, 0, 0)),
                  pl.BlockSpec((nb, tf, d), lambda i, j: (j, 0, 0))],
        out_specs=[pl.BlockSpec((tm, d), row), pl.BlockSpec((tm, d), row), pl.BlockSpec((tm, d), row),
                   pl.BlockSpec((tm, nb * tf), lambda i, j: (i, j)), pl.BlockSpec((tm, d), row)],
        out_shape=[jax.ShapeDtypeStruct((tp, d), F32), jax.ShapeDtypeStruct((tp, d), BF16),
                   jax.ShapeDtypeStruct((tp, d), BF16), jax.ShapeDtypeStruct((tp, nf * tf), BF16),
                   jax.ShapeDtypeStruct((tp, d), F32)],
        scratch_shapes=[pltpu.VMEM((tm, d), F32)],
        compiler_params=_cparams("parallel", "arbitrary"),
    )(h, o, rec, ga, gr, wout, g2, gup, gdown)


MLP_BLOCKS = 4


def loss_head(h, gf, tgt, t_real, tm):
    tp, d = h.shape

    def body(h_ref, g_ref, t_ref, dh_ref, dg_ref, loss_ref):
        i = pl.program_id(0)
        x = h_ref[...]
        g = g_ref[...]
        r = lax.rsqrt(jnp.mean(x * x, axis=-1, keepdims=True) + NORM_EPS)
        xn = x * r
        rows = i * tm + lax.broadcasted_iota(jnp.int32, (tm, 1), 0)
        valid = jnp.logical_and(rows >= N_META, rows < t_real)
        e = jnp.where(valid, xn * g - t_ref[...], 0.0)
        part = 0.5 * jnp.sum(jnp.sum(e * e, axis=1, keepdims=True) / d, axis=0, keepdims=True)
        dy = e / d
        dxn = dy * g
        dh_ref[...] = r * (dxn - xn * jnp.mean(dxn * xn, axis=-1, keepdims=True))
        _accumulate(dg_ref, jnp.sum(dy * xn, axis=0, keepdims=True), i == 0)
        _accumulate(loss_ref, jnp.broadcast_to(part, (1, LANES)), i == 0)

    row = lambda i: (i, 0)
    fix = lambda i: (0, 0)
    return pl.pallas_call(
        body, name="loss_head", grid=(tp // tm,),
        in_specs=[pl.BlockSpec((tm, d), row), pl.BlockSpec((1, d), fix), pl.BlockSpec((tm, d), row)],
        out_specs=[pl.BlockSpec((tm, d), row), pl.BlockSpec((1, d), fix), pl.BlockSpec((1, LANES), fix)],
        out_shape=[jax.ShapeDtypeStruct((tp, d), F32), jax.ShapeDtypeStruct((1, d), F32),
                   jax.ShapeDtypeStruct((1, LANES), F32)],
        compiler_params=_cparams("arbitrary"),
    )(h, gf, tgt)


def mlp_bwd(dh, u, h2, g2, gup, gdown, tm):
    tp, d = dh.shape
    nf = gup.shape[0]
    tf = gup.shape[2]
    nb = MLP_BLOCKS if nf % MLP_BLOCKS == 0 else 1
    nj = nf // nb
    ni = tp // tm

    def body(dh_ref, u_ref, h2_ref, g_ref, wu_ref, wd_ref, dup_ref, dh2_ref, dg_ref, dhb, acc):
        i = pl.program_id(0)
        j = pl.program_id(1)

        @pl.when(j == 0)
        def _():
            dhb[...] = dh_ref[...].astype(BF16)

        part = None
        for b in range(nb):
            cols = slice(b * tf, (b + 1) * tf)
            dup = (_dot_nt(dhb[...], wd_ref[b]) * (2.0 * u_ref[:, cols].astype(F32))).astype(BF16)
            dup_ref[:, cols] = dup
            p = _dot_nt(dup, wu_ref[b])
            part = p if part is None else part + p
        _accumulate(acc, part, j == 0)

        @pl.when(j == nj - 1)
        def _():
            dx, dg = _rms_bwd(h2_ref[...], g_ref[...], acc[...])
            dh2_ref[...] = dh_ref[...] + dx
            _accumulate(dg_ref, dg, i == 0)

    return pl.pallas_call(
        body, name="mlp_bwd", grid=(ni, nj),
        in_specs=[pl.BlockSpec((tm, d), lambda i, j: (i, 0)),
                  pl.BlockSpec((tm, nb * tf), lambda i, j: (i, j)),
                  pl.BlockSpec((tm, d), lambda i, j: (i, 0)),
                  pl.BlockSpec((1, d), lambda i, j: (0, 0)),
                  pl.BlockSpec((nb, d, tf), lambda i, j: (j, 0, 0)),
                  pl.BlockSpec((nb, tf, d), lambda i, j: (j, 0, 0))],
        out_specs=[pl.BlockSpec((tm, nb * tf), lambda i, j: (i, j)),
                   pl.BlockSpec((tm, d), lambda i, j: (i, 0)),
                   pl.BlockSpec((1, d), lambda i, j: (0, 0)),
                   pl.BlockSpec((tm, d), lambda i, j: (i, 0))],
        out_shape=[jax.ShapeDtypeStruct((tp, nf * tf), BF16), jax.ShapeDtypeStruct((tp, d), F32),
                   jax.ShapeDtypeStruct((1, d), F32), jax.ShapeDtypeStruct((tp, d), BF16)],
        scratch_shapes=[pltpu.VMEM((tm, d), F32)],
        compiler_params=_cparams("arbitrary", "arbitrary"),
    )(dh, u, h2, g2, gup, gdown)


def mm_tn(a, b, *, tn, name):
    rows, kk = a.shape
    nn = b.shape[1]

    def body(a_ref, b_ref, o_ref, at):
        @pl.when(pl.program_id(0) == 0)
        def _():
            at[...] = a_ref[...].T

        o_ref[...] = _dot(at[...], b_ref[...].astype(BF16)).astype(BF16)

    return pl.pallas_call(
        body, name=name, grid=(nn // tn,),
        in_specs=[pl.BlockSpec((rows, kk), lambda n: (0, 0)),
                  pl.BlockSpec((rows, tn), lambda n: (0, n))],
        out_specs=pl.BlockSpec((kk, tn), lambda n: (0, n)),
        out_shape=jax.ShapeDtypeStruct((kk, nn), BF16),
        scratch_shapes=[pltpu.VMEM((kk, rows), BF16)],
        compiler_params=_cparams("arbitrary"),
    )(a, b)


def dw_mlp(u, dhb, z2, dup, tf):
    rows, dff = u.shape
    d = z2.shape[1]
    nf = dff // tf

    def body(u_ref, dh_ref, z_ref, dup_ref, dwd_ref, dwu_ref, zt):
        @pl.when(pl.program_id(0) == 0)
        def _():
            zt[...] = z_ref[...].T

        uf = u_ref[...].astype(F32)
        dwd_ref[...] = _dot_tn((uf * uf).astype(BF16), dh_ref[...]).astype(BF16)
        dwu_ref[...] = _dot(zt[...], dup_ref[...]).astype(BF16)

    col = lambda j: (0, j)
    fix = lambda j: (0, 0)
    return pl.pallas_call(
        body, name="dw_mlp", grid=(nf,),
        in_specs=[pl.BlockSpec((rows, tf), col), pl.BlockSpec((rows, d), fix),
                  pl.BlockSpec((rows, d), fix), pl.BlockSpec((rows, tf), col)],
        out_specs=[pl.BlockSpec((None, tf, d), lambda j: (j, 0, 0)),
                   pl.BlockSpec((None, d, tf), lambda j: (j, 0, 0))],
        out_shape=[jax.ShapeDtypeStruct((nf, tf, d), BF16), jax.ShapeDtypeStruct((nf, d, tf), BF16)],
        scratch_shapes=[pltpu.VMEM((d, rows), BF16)],
        compiler_params=_cparams("arbitrary"),
    )(u, dhb, z2, dup)


def out_proj_bwd(dh2, o, rec, ga, gr, wout, tm):
    tp, d = dh2.shape
    aw, rw = o.shape[1], rec.shape[1]

    def body(dh_ref, o_ref, rec_ref, ga_ref, gr_ref, w_ref, do_ref, drec_ref, dga_ref, dgr_ref):
        i = pl.program_id(0)
        dmix = _dot_nt(dh_ref[...].astype(BF16), w_ref[...])
        do, dga = _rms_bwd(o_ref[...], ga_ref[...], dmix[:, 0:aw])
        drec, dgr = _rms_bwd(rec_ref[...], gr_ref[...], dmix[:, aw:aw + rw])
        do_ref[...] = do
        drec_ref[...] = drec
        _accumulate(dga_ref, dga, i == 0)
        _accumulate(dgr_ref, dgr, i == 0)

    row = lambda i: (i, 0)
    fix = lambda i: (0, 0)
    return pl.pallas_call(
        body, name="out_proj_bwd", grid=(tp // tm,),
        in_specs=[pl.BlockSpec((tm, d), row), pl.BlockSpec((tm, aw), row), pl.BlockSpec((tm, rw), row),
                  pl.BlockSpec((1, aw), fix), pl.BlockSpec((1, rw), fix), pl.BlockSpec((d, d), fix)],
        out_specs=[pl.BlockSpec((tm, aw), row), pl.BlockSpec((tm, rw), row),
                   pl.BlockSpec((1, aw), fix), pl.BlockSpec((1, rw), fix)],
        out_shape=[jax.ShapeDtypeStruct((tp, aw), F32), jax.ShapeDtypeStruct((tp, rw), F32),
                   jax.ShapeDtypeStruct((1, aw), F32), jax.ShapeDtypeStruct((1, rw), F32)],
        compiler_params=_cparams("arbitrary"),
    )(dh2, o, rec, ga, gr, wout)


def rec_bwd(drec, hr, xc, gates, rest, convw, wga, wgx, lru, rw):
    tp = rest.shape[0]
    ng = rw // LANES

    def body(drec_ref, hr_ref, xc_ref, r_ref, ig_ref, a_ref, mult_ref, xr_ref, yr_ref, cw_ref, wga_ref, wgx_ref,
             l_ref, dxr_ref, dyr_ref, dwga_ref, dwgx_ref, vec_ref, a_s, u_s, lam_s):
        xc = xc_ref[...]
        h = hr_ref[...]
        drec = drec_ref[...]
        r, ig, a, mult = r_ref[...], ig_ref[...], a_ref[...], mult_ref[...]
        xcb = xc.astype(BF16)
        ls = _log_sigmoid(l_ref[...])
        gelu, gelu_grad = _gelu_and_grad(yr_ref[...])
        dyr_ref[...] = (drec * h * gelu_grad).astype(BF16)
        a_s[...] = _shift_up(a, 1, tp)
        u_s[...] = drec * gelu
        _scan_rows(a_s, u_s, lam_s, tp, reverse=True)
        lam = lam_s[...]
        da = lam * _shift_down(h, 1, tp)
        dmult = lam * ig * xc
        dig = lam * mult * xc
        dxc = lam * mult * ig
        dlog_a = da * a - dmult * (a * a) / mult
        dr = dlog_a * (RG_C * ls)
        dl = jnp.sum(dlog_a * (RG_C * r), axis=0, keepdims=True) * _sigmoid(-l_ref[...])
        dpa = dr * r * (1.0 - r)
        dpx = dig * ig * (1.0 - ig)
        dpab = dpa.astype(BF16)
        dpxb = dpx.astype(BF16)
        dxc = dxc + _dot_nt(dpab, wga_ref[...]) + _dot_nt(dpxb, wgx_ref[...])
        dwga_ref[...] = _dot_tn(xcb, dpab)
        dwgx_ref[...] = _dot_tn(xcb, dpxb)
        xr = xr_ref[...]
        dxr = cw_ref[CONV_WIDTH - 1:CONV_WIDTH, :] * dxc
        for k in range(1, CONV_WIDTH):
            dxr = dxr + cw_ref[CONV_WIDTH - 1 - k:CONV_WIDTH - k, :] * _shift_up(dxc, k, tp)
        dxr_ref[...] = dxr.astype(BF16)
        for k in range(CONV_WIDTH):
            vec_ref[k:k + 1, :] = jnp.sum(dxc * _shift_down(xr, CONV_WIDTH - 1 - k, tp), axis=0, keepdims=True)
        vec_ref[4:5, :] = jnp.sum(dxc, axis=0, keepdims=True)
        vec_ref[5:6, :] = jnp.sum(dpa, axis=0, keepdims=True)
        vec_ref[6:7, :] = jnp.sum(dpx, axis=0, keepdims=True)
        vec_ref[7:8, :] = dl

    col = lambda g: (0, g)
    vec = pl.BlockSpec((1, LANES), col)
    big = pl.BlockSpec((tp, LANES), col)
    sq = pl.BlockSpec((None, LANES, LANES), lambda g: (g, 0, 0))
    return pl.pallas_call(
        body, name="rec_bwd", grid=(ng,),
        in_specs=[big] * 8 + [pl.BlockSpec((tp, LANES), lambda g: (0, ng + g)),
                                pl.BlockSpec((CONV_WIDTH, LANES), col), sq, sq, vec],
        out_specs=[big, big, sq, sq, pl.BlockSpec((None, SUBLANES, LANES), lambda g: (g, 0, 0))],
        out_shape=[jax.ShapeDtypeStruct((tp, rw), BF16), jax.ShapeDtypeStruct((tp, rw), BF16),
                   jax.ShapeDtypeStruct((ng, LANES, LANES), F32), jax.ShapeDtypeStruct((ng, LANES, LANES), F32),
                   jax.ShapeDtypeStruct((ng, SUBLANES, LANES), F32)],
        scratch_shapes=[pltpu.VMEM((tp, LANES), F32)] * 3,
        compiler_params=_cparams("parallel"),
    )(drec, hr, xc, *gates, rest, rest, convw, wga, wgx, lru)


def attn_bwd(qkv, do, o, lset, c, ct, nh):
    tp = qkv.shape[0]
    npair = nh // 2
    aw = nh * HEAD_DIM
    tiles = _att_tiles(tp)

    def body(q_ref, k_ref, v_ref, do_ref, o_ref, lset_ref, c_ref, ct_ref,
             dq_ref, dk_ref, dv_ref, drow_ref, dcol_ref, dk_acc, dv_acc, dq_t):
        p = pl.program_id(0)
        k_t = k_ref[...].T
        dk_acc[...] = jnp.zeros_like(dk_acc)
        dv_acc[...] = jnp.zeros_like(dv_acc)
        dcol_ref[...] = jnp.zeros_like(dcol_ref)
        drow_ref[...] = jnp.zeros_like(drow_ref)
        for r0, nr, nk in tiles:
            rs = slice(r0, r0 + nr)
            causal = (r0 + lax.broadcasted_iota(jnp.int32, (nk, nr), 1)
                      >= lax.broadcasted_iota(jnp.int32, (nk, nr), 0))
            cblk = c_ref[0:nk, :]
            ctb = ct_ref[:, rs]
            for hh in range(2):
                head = 2 * p + hh
                hs = slice(hh * HEAD_DIM, (hh + 1) * HEAD_DIM)
                q = q_ref[rs, hs]
                k = k_ref[0:nk, hs]
                dof = do_ref[rs, hs]
                do16 = dof.astype(BF16)
                delta = jnp.sum(dof * o_ref[rs, hs], axis=1, keepdims=True)
                delta_row = jnp.broadcast_to(delta, (nr, LANES)).T[0:1, :]
                s_t = _dot_nt(k, q * ATT_SCALE) + (_pick_row(ctb, head) - _pick_col(cblk, head))
                p_t = jnp.where(causal, jnp.exp(s_t - lset_ref[hh:hh + 1, rs]), 0.0)
                ds_t = p_t * (_dot_nt(v_ref[0:nk, hs], do16) - delta_row)
                p16 = p_t.astype(BF16)
                ds16 = ds_t.astype(BF16)
                dv_acc[0:nk, hs] += _dot(p16, do16)
                dk_acc[0:nk, hs] += _dot(ds16, q) * ATT_SCALE
                dq_t[hs, rs] = _dot(k_t[hs, 0:nk], ds16)
                drow_ref[hh:hh + 1, rs] = jnp.sum(ds_t, axis=0, keepdims=True)
                dcol_ref[0:nk, hs] -= jnp.broadcast_to(jnp.sum(ds_t, axis=1, keepdims=True), (nk, HEAD_DIM))
        dk_ref[...] = dk_acc[...].astype(BF16)
        dv_ref[...] = dv_acc[...].astype(BF16)
        dq_ref[...] = (dq_t[...].T * ATT_SCALE).astype(BF16)

    pair = lambda p: (0, p)
    return pl.pallas_call(
        body, name="attn_bwd", grid=(npair,),
        in_specs=[pl.BlockSpec((tp, LANES), pair),
                  pl.BlockSpec((tp, LANES), lambda p: (0, npair + p)),
                  pl.BlockSpec((tp, LANES), lambda p: (0, 2 * npair + p)),
                  pl.BlockSpec((tp, LANES), pair),
                  pl.BlockSpec((tp, LANES), pair),
                  pl.BlockSpec((None, SUBLANES, tp), lambda p: (p, 0, 0)),
                  pl.BlockSpec((tp, LANES), lambda p: (0, 0)),
                  pl.BlockSpec((SUBLANES, tp), lambda p: (0, 0))],
        out_specs=[pl.BlockSpec((tp, LANES), pair), pl.BlockSpec((tp, LANES), pair),
                   pl.BlockSpec((tp, LANES), pair),
                   pl.BlockSpec((None, SUBLANES, tp), lambda p: (p, 0, 0)),
                   pl.BlockSpec((tp, LANES), pair)],
        out_shape=[jax.ShapeDtypeStruct((tp, aw), BF16), jax.ShapeDtypeStruct((tp, aw), BF16),
                   jax.ShapeDtypeStruct((tp, aw), BF16),
                   jax.ShapeDtypeStruct((npair, SUBLANES, tp), F32),
                   jax.ShapeDtypeStruct((tp, aw), F32)],
        scratch_shapes=[pltpu.VMEM((tp, LANES), F32), pltpu.VMEM((tp, LANES), F32),
                        pltpu.VMEM((LANES, tp), F32)],
        compiler_params=_cparams("parallel"),
    )(qkv, qkv, qkv, do, o, lset, c, ct)


def fgate_bwd(dct8, drs, rest, bf_pad, fcol):
    tp = rest.shape[0]
    aw = drs.shape[1]
    nb = tp // ATT_BLOCK
    B = ATT_BLOCK

    def body(d_ref, drs_ref, f_ref, b_ref, dfl_ref, db_ref, pad_s):
        r_i = lax.broadcasted_iota(jnp.int32, (B, B), 0)
        c_i = lax.broadcasted_iota(jnp.int32, (B, B), 1)
        triu = (c_i >= r_i).astype(BF16)
        sel = (lax.broadcasted_iota(jnp.int32, (aw, LANES), 0)
               == HEAD_DIM * lax.broadcasted_iota(jnp.int32, (aw, LANES), 1)).astype(BF16)
        carry = jnp.zeros((1, LANES), F32)
        db = jnp.zeros((1, LANES), F32)
        pad_s[...] = jnp.zeros_like(pad_s)
        for i in range(nb - 1, -1, -1):
            sl = slice(i * B, (i + 1) * B)
            pad_s[0:SUBLANES, :] = d_ref[:, sl]
            dc = pad_s[...].T + _dot_split3(drs_ref[sl, :], sel)
            rc = _split3_dot(triu, dc)
            dlf = rc + carry
            carry = carry + rc[0:1, :]
            dfl = dlf * _sigmoid(-(f_ref[sl, :] + b_ref[...]))
            dfl_ref[sl, :] = dfl.astype(BF16)
            db = db + jnp.sum(dfl, axis=0, keepdims=True)
        db_ref[...] = db

    return pl.pallas_call(
        body, name="fgate_bwd", grid=(1,),
        in_specs=[pl.BlockSpec((SUBLANES, tp), lambda i: (0, 0)),
                  pl.BlockSpec((tp, aw), lambda i: (0, 0)),
                  pl.BlockSpec((tp, LANES), lambda i: (0, fcol)),
                  pl.BlockSpec((1, LANES), lambda i: (0, 0))],
        out_specs=[pl.BlockSpec((tp, LANES), lambda i: (0, 0)),
                   pl.BlockSpec((1, LANES), lambda i: (0, 0))],
        out_shape=[jax.ShapeDtypeStruct((tp, LANES), BF16), jax.ShapeDtypeStruct((1, LANES), F32)],
        scratch_shapes=[pltpu.VMEM((B, B), F32)],
        compiler_params=_cparams("arbitrary"),
    )(dct8, drs, rest, bf_pad)


def in_proj_bwd(dh2, parts, w_in_t, wrest_t, h, g1, tm):
    tp, d = h.shape
    dq, dk, dv, dxr, dyr, dfl = parts
    aw, rw = dq.shape[1], dxr.shape[1]

    def body(dh2_ref, dq_ref, dk_ref, dv_ref, dxr_ref, dyr_ref, dfl_ref, wq_ref, wr_ref, h_ref, g_ref,
             dh_ref, dg_ref):
        i = pl.program_id(0)
        dz = _dot(dq_ref[...], wq_ref[0:aw, :])
        dz += _dot(dk_ref[...], wq_ref[aw:2 * aw, :])
        dz += _dot(dv_ref[...], wq_ref[2 * aw:3 * aw, :])
        dz += _dot(dxr_ref[...], wr_ref[0:rw, :])
        dz += _dot(dyr_ref[...], wr_ref[rw:2 * rw, :])
        dz += _dot(dfl_ref[...], wr_ref[2 * rw:2 * rw + LANES, :])
        dx, dg = _rms_bwd(h_ref[...], g_ref[...], dz)
        dh_ref[...] = dh2_ref[...] + dx
        _accumulate(dg_ref, dg, i == 0)

    row = lambda i: (i, 0)
    fix = lambda i: (0, 0)
    return pl.pallas_call(
        body, name="in_proj_bwd", grid=(tp // tm,),
        in_specs=[pl.BlockSpec((tm, d), row),
                  pl.BlockSpec((tm, aw), row), pl.BlockSpec((tm, aw), row), pl.BlockSpec((tm, aw), row),
                  pl.BlockSpec((tm, rw), row), pl.BlockSpec((tm, rw), row), pl.BlockSpec((tm, LANES), row),
                  pl.BlockSpec((3 * aw, d), fix), pl.BlockSpec(wrest_t.shape, fix),
                  pl.BlockSpec((tm, d), row), pl.BlockSpec((1, d), fix)],
        out_specs=[pl.BlockSpec((tm, d), row), pl.BlockSpec((1, d), fix)],
        out_shape=[jax.ShapeDtypeStruct((tp, d), F32), jax.ShapeDtypeStruct((1, d), F32)],
        compiler_params=_cparams("arbitrary"),
    )(dh2, dq, dk, dv, dxr, dyr, dfl, w_in_t, wrest_t, h, g1)


def dw_in_t(z, parts, nh, tr):
    tp, d = z.shape
    dq, dk, dv, dxr, dyr, dfl = parts
    aw, rw = dq.shape[1], dxr.shape[1]
    d_in = 3 * aw + nh + 2 * rw
    nr = tp // tr
    offs = [(0, aw), (aw, aw), (2 * aw, aw), (3 * aw + nh, rw), (3 * aw + nh + rw, rw)]

    def body(z_ref, dq_ref, dk_ref, dv_ref, dxr_ref, dyr_ref, dfl_ref, o_ref, acc):
        r = pl.program_id(0)

        @pl.when(r == 0)
        def _():
            acc[...] = jnp.zeros_like(acc)

        zt = z_ref[...]
        for (o, n), ref in zip(offs, (dq_ref, dk_ref, dv_ref, dxr_ref, dyr_ref)):
            acc[o:o + n, :] += _dot_tn(ref[...], zt)
        acc[3 * aw:3 * aw + nh, :] += _dot_tn(dfl_ref[...], zt)[0:nh, :]

        @pl.when(r == nr - 1)
        def _():
            o_ref[...] = acc[...].astype(BF16)

    row = lambda r: (r, 0)
    return pl.pallas_call(
        body, name="dw_in", grid=(nr,),
        in_specs=[pl.BlockSpec((tr, d), row),
                  pl.BlockSpec((tr, aw), row), pl.BlockSpec((tr, aw), row), pl.BlockSpec((tr, aw), row),
                  pl.BlockSpec((tr, rw), row), pl.BlockSpec((tr, rw), row), pl.BlockSpec((tr, LANES), row)],
        out_specs=pl.BlockSpec((d_in, d), lambda r: (0, 0)),
        out_shape=jax.ShapeDtypeStruct((d_in, d), BF16),
        scratch_shapes=[pltpu.VMEM((d_in, d), F32)],
        compiler_params=_cparams("arbitrary"),
    )(z, dq, dk, dv, dxr, dyr, dfl)


def _place():
    return lax.axis_index("x"), lax.axis_index("y"), lax.axis_index("c")


HBM = pl.BlockSpec(memory_space=pltpu.HBM)
SEM = pl.BlockSpec(memory_space=pltpu.SEMAPHORE)
EFFECT = pltpu.SideEffectType.DATAFLOW_SIDE_EFFECTING


def _in_hbm(a):
    return pltpu.with_memory_space_constraint(a, pltpu.HBM)


def _as_list(a):
    return list(a) if isinstance(a, (list, tuple)) else [a]


def _gather_targets(x, y, c):
    return [(x, y, 1 - c), (1 - x, y, c), (x, 1 - y, c), (1 - x, 1 - y, c)]


def _slot(t):
    return 4 * t[0] + 2 * t[1] + t[2]


def gather_start(groups, name):
    flat = [a for g in groups for a in g]
    n = len(flat)
    ng = len(groups)
    lands = [lax.empty((N_DEV,) + a.shape, a.dtype) for a in flat]

    def body(*refs):
        src, land = refs[:n], refs[n:2 * n]
        sems = refs[2 * n:2 * n + 2 * ng]
        token = refs[-1]
        x, y, c = _place()
        me = 4 * x + 2 * y + c
        i = 0
        for gi, g in enumerate(groups):
            for a in range(len(g)):
                for k, t in enumerate(_gather_targets(x, y, c)):
                    pltpu.make_async_remote_copy(
                        src_ref=src[i], dst_ref=land[i].at[me],
                        send_sem=sems[2 * gi].at[4 * a + k], recv_sem=sems[2 * gi + 1].at[4 * a + k],
                        device_id=t, device_id_type=MESH).start()
                i += 1
        token[...] = jnp.zeros_like(token)

    sem_shapes = []
    for g in groups:
        sem_shapes += [pltpu.SemaphoreType.DMA((4 * len(g),)), pltpu.SemaphoreType.DMA((4 * len(g),))]
    out = pl.pallas_call(
        body, name=name,
        out_shape=sem_shapes + [pltpu.HBM(a.shape, a.dtype) for a in flat + lands]
        + [jax.ShapeDtypeStruct((SUBLANES, LANES), F32)],
        in_specs=[HBM] * (2 * n),
        out_specs=[SEM] * (2 * ng) + [HBM] * (2 * n) + [pl.BlockSpec(memory_space=pltpu.VMEM)],
        input_output_aliases={i: 2 * ng + i for i in range(2 * n)},
        compiler_params=pltpu.CompilerParams(has_side_effects=EFFECT),
    )(*[_in_hbm(a) for a in flat + lands])
    sems = out[:2 * ng]
    thru = out[2 * ng:2 * ng + 2 * n]
    srcs_t, lands_t = thru[:n], thru[n:]
    res, i = [], 0
    for gi, g in enumerate(groups):
        res.append((sems[2 * gi], sems[2 * gi + 1], srcs_t[i:i + len(g)], lands_t[i:i + len(g)]))
        i += len(g)
    return res, out[-1]


def gather_wait(send, recv, srcs, lands, after, name):
    n = len(srcs)

    def body(*refs):
        src, land = refs[:n], refs[n:2 * n]
        send_sem, recv_sem = refs[2 * n], refs[2 * n + 1]
        x, y, c = _place()
        for a in range(n):
            for k, t in enumerate(_gather_targets(x, y, c)):
                cp = pltpu.make_async_remote_copy(
                    src_ref=src[a], dst_ref=land[a].at[_slot(t)],
                    send_sem=send_sem.at[4 * a + k], recv_sem=recv_sem.at[4 * a + k],
                    device_id=t, device_id_type=MESH)
                cp.wait_send()
                cp.wait_recv()

    out = pl.pallas_call(
        body, name=name,
        out_shape=[pltpu.HBM(a.shape, a.dtype) for a in list(srcs) + list(lands)],
        in_specs=[HBM] * (2 * n) + [SEM, SEM] + [ANY] * len(_as_list(after)),
        out_specs=[HBM] * (2 * n),
        input_output_aliases={i: i for i in range(2 * n)},
        compiler_params=pltpu.CompilerParams(has_side_effects=EFFECT),
    )(*srcs, *lands, send, recv, *_as_list(after))
    return out[:n], out[n:]


def forward_start(lands, name):
    n = len(lands)

    def body(*refs):
        land = refs[:n]
        send_sem, recv_sem = refs[n], refs[n + 1]
        token = refs[-1]
        x, y, c = _place()
        for a in range(n):
            for j, chip in enumerate([(1 - x, y), (x, 1 - y), (1 - x, 1 - y)]):
                blk = land[a].at[_slot((*chip, c))]
                pltpu.make_async_remote_copy(src_ref=blk, dst_ref=blk, send_sem=send_sem.at[3 * a + j],
                                             recv_sem=recv_sem.at[3 * a + j], device_id=(x, y, 1 - c),
                                             device_id_type=MESH).start()
        token[...] = jnp.zeros_like(token)

    out = pl.pallas_call(
        body, name=name,
        out_shape=[pltpu.SemaphoreType.DMA((3 * n,)), pltpu.SemaphoreType.DMA((3 * n,))]
        + [pltpu.HBM(a.shape, a.dtype) for a in lands] + [jax.ShapeDtypeStruct((SUBLANES, LANES), F32)],
        in_specs=[HBM] * n,
        out_specs=[SEM, SEM] + [HBM] * n + [pl.BlockSpec(memory_space=pltpu.VMEM)],
        input_output_aliases={i: 2 + i for i in range(n)},
        compiler_params=pltpu.CompilerParams(has_side_effects=EFFECT),
    )(*[_in_hbm(a) for a in lands])
    return out[0], out[1], out[2:2 + n], out[-1][0, 0]


def forward_wait(send, recv, lands, after, name):
    n = len(lands)

    def body(*refs):
        land = refs[:n]
        send_sem, recv_sem = refs[n], refs[n + 1]
        x, y, c = _place()
        for a in range(n):
            for j, chip in enumerate([(1 - x, y), (x, 1 - y), (1 - x, 1 - y)]):
                cp = pltpu.make_async_remote_copy(
                    src_ref=land[a].at[_slot((*chip, c))], dst_ref=land[a].at[_slot((*chip, 1 - c))],
                    send_sem=send_sem.at[3 * a + j], recv_sem=recv_sem.at[3 * a + j],
                    device_id=(x, y, 1 - c), device_id_type=MESH)
                cp.wait_send()
                cp.wait_recv()

    return pl.pallas_call(
        body, name=name,
        out_shape=[pltpu.HBM(a.shape, a.dtype) for a in lands],
        in_specs=[HBM] * n + [SEM, SEM, ANY],
        out_specs=[HBM] * n,
        input_output_aliases={i: i for i in range(n)},
        compiler_params=pltpu.CompilerParams(has_side_effects=EFFECT),
    )(*lands, send, recv, after)


def _relations():
    return [(dx, dy, dc) for dx in (0, 1) for dy in (0, 1) for dc in (0, 1) if dx + dy + dc]


def _peer(x, y, c, rel):
    return ((1 - x) if rel[0] else x, (1 - y) if rel[1] else y, (1 - c) if rel[2] else c)


def exchange_start(srcs, lands, layer, name, after=()):
    n = len(srcs)
    after = _as_list(after)

    def body(*refs):
        src, land = refs[:n], refs[n:2 * n]
        send_sem, recv_sem = refs[2 * n + len(after)], refs[2 * n + len(after) + 1]
        token = refs[-1]
        x, y, c = _place()
        me = 4 * x + 2 * y + c
        for k, rel in enumerate(_relations()):
            peer = _peer(x, y, c, rel)
            for a in range(n):
                pltpu.make_async_remote_copy(
                    src_ref=src[a] if layer is None else src[a].at[_slot(peer)],
                    dst_ref=land[a].at[me] if layer is None else land[a].at[me, layer],
                    send_sem=send_sem.at[7 * a + k], recv_sem=recv_sem.at[7 * a + k],
                    device_id=peer, device_id_type=MESH).start()
        token[...] = jnp.zeros_like(token)

    out = pl.pallas_call(
        body, name=name,
        out_shape=[pltpu.SemaphoreType.DMA((7 * n,)), pltpu.SemaphoreType.DMA((7 * n,))]
        + [pltpu.HBM(a.shape, a.dtype) for a in list(srcs) + list(lands)]
        + [jax.ShapeDtypeStruct((SUBLANES, LANES), F32)],
        in_specs=[HBM] * (2 * n) + [ANY] * len(after),
        out_specs=[SEM, SEM] + [HBM] * (2 * n) + [pl.BlockSpec(memory_space=pltpu.VMEM)],
        input_output_aliases={i: 2 + i for i in range(2 * n)},
        compiler_params=pltpu.CompilerParams(has_side_effects=EFFECT),
    )(*[_in_hbm(a) for a in list(srcs) + list(lands)], *after)
    return out[0], out[1], out[2:2 + n], out[2 + n:2 + 2 * n], out[-1][0, 0]


def exchange_wait(send, recv, srcs, lands, after, layer, name):
    n = len(srcs)

    def body(*refs):
        src, land = refs[:n], refs[n:2 * n]
        send_sem, recv_sem = refs[2 * n], refs[2 * n + 1]
        x, y, c = _place()
        for k, rel in enumerate(_relations()):
            peer = _peer(x, y, c, rel)
            for a in range(n):
                cp = pltpu.make_async_remote_copy(
                    src_ref=src[a] if layer is None else src[a].at[_slot(peer)],
                    dst_ref=land[a].at[_slot(peer)] if layer is None else land[a].at[_slot(peer), layer],
                    send_sem=send_sem.at[7 * a + k], recv_sem=recv_sem.at[7 * a + k],
                    device_id=peer, device_id_type=MESH)
                cp.wait_send()
                cp.wait_recv()

    out = pl.pallas_call(
        body, name=name,
        out_shape=[pltpu.HBM(a.shape, a.dtype) for a in list(srcs) + list(lands)],
        in_specs=[HBM] * (2 * n) + [SEM, SEM] + [ANY] * len(_as_list(after)),
        out_specs=[HBM] * (2 * n),
        input_output_aliases={i: i for i in range(2 * n)},
        compiler_params=pltpu.CompilerParams(has_side_effects=EFFECT),
    )(*srcs, *lands, send, recv, *_as_list(after))
    return out[:n], out[n:]


def _adamw_math(g, w, m, v):
    m = ADAM_B1 * m + (1.0 - ADAM_B1) * g
    v = ADAM_B2 * v + (1.0 - ADAM_B2) * (g * g)
    m_hat = m / (1.0 - ADAM_B1 ** ADAM_STEP)
    v_hat = v / (1.0 - ADAM_B2 ** ADAM_STEP)
    delta = -ADAM_LR * (m_hat / (jnp.sqrt(v_hat) + ADAM_EPS) + ADAM_WD * w)
    return delta, m, v


def _sum_with_own(p_ref, own_refs, layer, me):
    own = own_refs[0][...]
    for k in range(1, len(own_refs)):
        own = jnp.where(layer == k, own_refs[k][...], own)
    g = None
    for p in range(p_ref.shape[0]):
        term = jnp.where(me == p, own, p_ref[p]).astype(F32)
        g = term if g is None else g + term
    return g


def sum_adamw(parts, owns, me, w, m, v, tr, name):
    npart, rows, cols = parts.shape
    nl = len(owns)
    per_layer = rows // nl // tr

    def body(me_ref, p_ref, *refs):
        own_refs = refs[:nl]
        w_ref, m_ref, v_ref, g_ref, d_ref, nm_ref, nv_ref = refs[nl:]
        g = _sum_with_own(p_ref, own_refs, pl.program_id(0) // per_layer, me_ref[0])
        delta, nm, nv = _adamw_math(g, w_ref[...], m_ref[...], v_ref[...])
        g_ref[...] = g
        d_ref[...] = delta
        nm_ref[...] = nm
        nv_ref[...] = nv

    blk = pl.BlockSpec((tr, cols), lambda i, me_ref: (i, 0))
    own_specs = [pl.BlockSpec((None, tr, cols),
                              lambda i, me_ref, l=l: (me_ref[0], jnp.clip(i - l * per_layer, 0, per_layer - 1), 0))
                 for l in range(nl)]
    return pl.pallas_call(
        body, name=name,
        grid_spec=pltpu.PrefetchScalarGridSpec(
            num_scalar_prefetch=1, grid=(rows // tr,),
            in_specs=[pl.BlockSpec((npart, tr, cols), lambda i, me_ref: (0, i, 0))] + own_specs + [blk, blk, blk],
            out_specs=[blk] * 4),
        out_shape=[jax.ShapeDtypeStruct((rows, cols), F32)] * 4,
        compiler_params=_cparams("arbitrary"),
    )(me, parts, *owns, w, m, v)


def sum_adamw_t(parts, owns, me, w, m, v, name):
    npart, nl, rows, cols = parts.shape

    def body(me_ref, p_ref, *refs):
        own_refs = refs[:nl]
        w_ref, m_ref, v_ref, g_ref, d_ref, nm_ref, nv_ref = refs[nl:]
        g = _sum_with_own(p_ref, own_refs, pl.program_id(0), me_ref[0])
        delta, nm, nv = _adamw_math(g, w_ref[...], m_ref[...], v_ref[...])
        g_ref[...] = g
        d_ref[...] = delta
        nm_ref[...] = nm
        nv_ref[...] = nv

    blk = pl.BlockSpec((None, rows, cols), lambda l, me_ref: (l, 0, 0))
    own_specs = [pl.BlockSpec((None, rows, cols), lambda l, me_ref: (me_ref[0], 0, 0)) for _ in range(nl)]
    return pl.pallas_call(
        body, name=name,
        grid_spec=pltpu.PrefetchScalarGridSpec(
            num_scalar_prefetch=1, grid=(nl,),
            in_specs=[pl.BlockSpec((npart, None, rows, cols), lambda l, me_ref: (0, l, 0, 0))] + own_specs
            + [blk, blk, blk],
            out_specs=[blk] * 4),
        out_shape=[jax.ShapeDtypeStruct((nl, rows, cols), F32)] * 4,
        compiler_params=_cparams("arbitrary"),
    )(me, parts, *owns, w, m, v)


def adamw_group(gs, ws, ms, vs, name):
    n = len(gs)

    def body(*refs):
        g, w, m, v, outs = refs[:n], refs[n:2 * n], refs[2 * n:3 * n], refs[3 * n:4 * n], refs[4 * n:]
        for i in range(n):
            delta, nm, nv = _adamw_math(g[i][...], w[i][...], m[i][...], v[i][...])
            outs[i][...] = delta
            outs[n + i][...] = nm
            outs[2 * n + i][...] = nv

    vmem = pl.BlockSpec(memory_space=pltpu.VMEM)
    out = pl.pallas_call(
        body, name=name,
        in_specs=[vmem] * (4 * n), out_specs=[vmem] * (3 * n),
        out_shape=[jax.ShapeDtypeStruct(a.shape, F32) for a in list(ws) * 3],
        compiler_params=_cparams(),
    )(*gs, *ws, *ms, *vs)
    return out[:n], out[n:2 * n], out[2 * n:]


def sum_parts(parts, name):
    npart, rows, cols = parts.shape

    def body(p_ref, g_ref):
        g = p_ref[0].astype(F32)
        for p in range(1, npart):
            g = g + p_ref[p].astype(F32)
        g_ref[...] = g

    return pl.pallas_call(
        body, name=name, grid=(1,),
        in_specs=[pl.BlockSpec((npart, rows, cols), lambda i: (0, 0, 0))],
        out_specs=pl.BlockSpec((rows, cols), lambda i: (0, 0)),
        out_shape=jax.ShapeDtypeStruct((rows, cols), F32),
        compiler_params=_cparams("arbitrary"),
    )(parts)


def _round_up(n, m):
    return (n + m - 1) // m * m


def _block_diag_pairs(w):
    nb, b, _ = w.shape
    per = LANES // b
    ng = nb // per
    w = w.reshape(ng, per, b, b)
    eye = jnp.eye(per, dtype=w.dtype)
    out = jnp.einsum('gpij,pq->gpiqj', w, eye).reshape(ng, LANES, LANES)
    return out.astype(BF16)


def _block_diag_extract(g, b):
    ng = g.shape[0]
    per = LANES // b
    g = g.reshape(ng, per, b, per, b)
    idx = jnp.arange(per)
    return g[:, idx, :, idx, :].transpose(1, 0, 2, 3).reshape(ng * per, b, b)


def _tiles(v):
    v = v.reshape(-1)
    n = _round_up(v.shape[0], SUBLANES * LANES)
    return jnp.pad(v, (0, n - v.shape[0])).reshape(-1, LANES)


SMALL = ['attn_norm_g', 'b_f', 'conv_w', 'conv_b', 'w_gate_a', 'b_gate_a', 'w_gate_x', 'b_gate_x',
         'lru_L', 'attn_out_g', 'rec_out_g', 'mlp_norm_g', 'final_g', 'meta']


def _pack(d):
    return jnp.concatenate([_tiles(d[n]) for n in SMALL], axis=0)


def _unpack(vec, shapes):
    out, r = {}, 0
    for n in SMALL:
        size = math.prod(shapes[n])
        nr = _round_up(size, SUBLANES * LANES) // LANES
        out[n] = vec[r:r + nr].reshape(-1)[:size].reshape(shapes[n])
        r += nr
    return out


def _row_tile(tp):
    return tp // 4 if (tp // 4) % 16 == 0 else tp


def local_step(x, tgt, meta, small, hooks):
    s, d = x.shape
    t_real = s + N_META
    tp = _round_up(t_real, ATT_BLOCK)
    depth = small['attn_norm_g'].shape[0]
    nh = small['b_f'].shape[1]
    rw = small['conv_b'].shape[1]
    blk = small['w_gate_a'].shape[2]
    tm = _row_tile(tp)
    tm2 = tp // 2
    fcol = 2 * rw // LANES

    h = jnp.concatenate([meta, x, jnp.zeros((tp - t_real, d), F32)], axis=0)
    tgt_p = jnp.pad(tgt, ((N_META, tp - t_real), (0, 0)))
    row = lambda v: v.reshape(1, -1)
    bf_pad = jnp.pad(small['b_f'], ((0, 0), (0, LANES - nh)))

    saved = []
    for l in range(depth):
        w_in_t, wrest_t, wout, tok_w = hooks.mixer_weights(l, h)
        wga = _block_diag_pairs(small['w_gate_a'][l])
        wgx = _block_diag_pairs(small['w_gate_x'][l])
        z, qkv, rest = in_proj(h, row(small['attn_norm_g'][l]) + tok_w, w_in_t, wrest_t, 3 * nh * HEAD_DIM, tm)
        c, ct = fgate_fwd(rest, bf_pad[l:l + 1], fcol)
        o, lset = attn_fwd(qkv, c, ct, nh)
        rec, hr, xc, *gates = rec_fwd(rest, small['conv_w'][l], row(small['conv_b'][l]), wga,
                                      row(small['b_gate_a'][l]), wgx, row(small['b_gate_x'][l]),
                                      row(small['lru_L'][l]), rw)
        gup, gdown, tok_w = hooks.mlp_weights(l, rec)
        h2, mix, z2, u, h3 = out_mlp_fwd(h, o, rec, row(small['attn_out_g'][l]), row(small['rec_out_g'][l]), wout,
                                         row(small['mlp_norm_g'][l]) + tok_w, gup, gdown, tm)
        saved.append(dict(h=h, z=z, qkv=qkv, rest=rest, c=c, ct=ct, o=o, lset=lset, rec=rec, hr=hr, xc=xc,
                          h2=h2, mix=mix, z2=z2, u=u, wga=wga, wgx=wgx, gates=gates,
                          w_in_t=w_in_t, wrest_t=wrest_t, wout=wout, gup=gup, gdown=gdown))
        h = h3

    dh, dgf, loss = loss_head(h, row(small['final_g']), tgt_p, t_real, tm)

    gs = {n: [None] * depth for n in SMALL if n not in ('final_g', 'meta')}
    tok = jnp.zeros((), F32)
    for l in reversed(range(depth)):
        sv = saved[l]
        gup, gdown = sv['gup'], sv['gdown']
        tf = gup.shape[2]
        dup, dh2, dg2, dhb = mlp_bwd(dh, sv['u'], sv['h2'], row(small['mlp_norm_g'][l]) + tok, gup, gdown, tm)
        gs['mlp_norm_g'][l] = dg2[0]
        do, drec, dga, dgr = out_proj_bwd(dh2, sv['o'], sv['rec'], row(small['attn_out_g'][l]),
                                          row(small['rec_out_g'][l]), sv['wout'], tm)
        gs['attn_out_g'][l] = dga[0]
        gs['rec_out_g'][l] = dgr[0]
        dw_down, dw_up = dw_mlp(sv['u'], dhb, sv['z2'], dup, tf)
        blocks = dict(
            w_down=dw_down, w_up=dw_up,
            w_out=mm_tn(sv['mix'], dh2, tn=d // 2, name="dw_out").reshape(N_DEV, d // N_DEV, d))
        tok = hooks.grads_ready(l, 'mlp', blocks)
        dxr, dyr, dwga, dwgx, vec = rec_bwd(drec, sv['hr'], sv['xc'], sv['gates'], sv['rest'], small['conv_w'][l],
                                            sv['wga'], sv['wgx'], row(small['lru_L'][l]) + tok, rw)
        gs['w_gate_a'][l] = _block_diag_extract(dwga, blk)
        gs['w_gate_x'][l] = _block_diag_extract(dwgx, blk)
        vec = vec.transpose(1, 0, 2).reshape(SUBLANES, rw)
        gs['conv_w'][l] = vec[0:CONV_WIDTH]
        gs['conv_b'][l] = vec[4]
        gs['b_gate_a'][l] = vec[5]
        gs['b_gate_x'][l] = vec[6]
        gs['lru_L'][l] = vec[7]
        dq, dk, dv, drow, dcol = attn_bwd(sv['qkv'], do, sv['o'], sv['lset'], sv['c'], sv['ct'] + tok, nh)
        drow8 = drow[:, 0:2, :].reshape(nh, tp)
        if nh < SUBLANES:
            drow8 = jnp.pad(drow8, ((0, SUBLANES - nh), (0, 0)))
        dfl, dbf = fgate_bwd(drow8, dcol, sv['rest'], bf_pad[l:l + 1], fcol)
        gs['b_f'][l] = dbf[0, 0:nh]
        parts = (dq, dk, dv, dxr, dyr, dfl)
        dh, dg1 = in_proj_bwd(dh2, parts, sv['w_in_t'], sv['wrest_t'], sv['h'], row(small['attn_norm_g'][l]), tm)
        gs['attn_norm_g'][l] = dg1[0]
        first = ()
        if l == 0:
            grads = {n: jnp.stack(v) for n, v in gs.items()}
            grads['final_g'] = dgf[0]
            grads['meta'] = dh[0:N_META]
            first = hooks.small_ready(grads)
        dw_in = dw_in_t(sv['z'], parts, nh, tm2)
        dw_in = dw_in.reshape(N_DEV, dw_in.shape[0] // N_DEV, d)
        tok = hooks.grads_ready(l, 'in', dict(w_in=dw_in), first)

    return loss[0, 0], dh


def prep_weights(g_in, g_out, nh, rw):
    d = g_in.shape[2]
    w_in_t = g_in.reshape(-1, d)
    f0 = 3 * nh * HEAD_DIM
    wrest_t = jnp.concatenate([w_in_t[f0 + nh:f0 + nh + 2 * rw],
                               jnp.pad(w_in_t[f0:f0 + nh], ((0, LANES - nh), (0, 0)))], axis=0)
    return w_in_t, wrest_t, g_out.reshape(d, d)


BIG = ['w_in', 'w_out', 'w_up', 'w_down']
EXCHANGE_GROUPS = {'mlp': ['w_down', 'w_up', 'w_out'], 'in': ['w_in']}
WEIGHTS = ['meta', 'attn_norm_g', 'w_in', 'b_f', 'conv_w', 'conv_b', 'w_gate_a', 'b_gate_a', 'w_gate_x', 'b_gate_x',
           'lru_L', 'attn_out_g', 'rec_out_g', 'w_out', 'mlp_norm_g', 'w_up', 'w_down', 'final_g']


def _set_own(arr, own, me):
    return lax.dynamic_update_slice_in_dim(arr, own[None], me, axis=0)


class _Step:
    def __init__(self, w, nh, rw, me):
        self.w, self.nh, self.rw, self.me = w, nh, rw, me
        depth = w['w_in'].shape[0]
        first = [w['w_in_t'][:, 0, :].astype(BF16), w['w_out'][0].astype(BF16), w['meta'], w['conv_w']]
        self.pending, token = gather_start([first], "gather_start_0")
        zero = token[0, 0].astype(BF16)
        groups = [[w['w_up'][0].astype(BF16) + zero, w['w_down'][0].astype(BF16) + zero]]
        for l in range(1, depth):
            groups.append([w['w_in_t'][:, l, :].astype(BF16) + zero, w['w_out'][l].astype(BF16) + zero])
            groups.append([w['w_up'][l].astype(BF16) + zero, w['w_down'][l].astype(BF16) + zero])
        rest, _ = gather_start(groups, "gather_start_1")
        self.pending += rest
        self.first_after = rest[0][2][0]
        self.gathered = {}
        self.passing = {}
        self.token = jnp.zeros((), F32)
        self.lands = {n: lax.empty((N_DEV,) + w[n].shape, BF16) for n in BIG}
        din8, _, d = w['w_in_t'].shape
        self.lands['w_in'] = lax.empty((N_DEV, depth, din8, d), BF16)
        self.started = []
        self.small = None

    def _pass_on(self, gi, after):
        if gi < len(self.pending) and gi not in self.passing:
            send, recv, srcs, lands = self.pending[gi]
            srcs, lands = gather_wait(send, recv, srcs, lands, after, "gather_wait_%d" % gi)
            fsend, frecv, lands, token = forward_start(lands, "forward_start_%d" % gi)
            self.passing[gi] = (fsend, frecv, srcs, lands)
            self.token = token

    def group(self, gi, after):
        if gi not in self.gathered:
            self._pass_on(gi, after)
            fsend, frecv, srcs, lands = self.passing[gi]
            lands = forward_wait(fsend, frecv, lands, after, "forward_wait_%d" % gi)
            self.gathered[gi] = [_set_own(g, own, self.me) for g, own in zip(lands, srcs)]
            if gi >= 2:
                self._pass_on(gi + 1, lands[0])
        return self.gathered[gi]

    def mixer_weights(self, l, after):
        g = self.group(2 * l, after)
        return (*prep_weights(g[0], g[1], self.nh, self.rw), self.token)

    def mlp_weights(self, l, after):
        g = self.group(2 * l + 1, after)
        return g[0], g[1], self.token

    def grads_ready(self, l, group, blocks, after=()):
        names = EXCHANGE_GROUPS[group]
        send, recv, srcs, lands, token = exchange_start(
            [blocks[n] for n in names], [self.lands[n] for n in names], l, "exchange_start_%s_%d" % (group, l),
            after)
        for n, a in zip(names, lands):
            self.lands[n] = a
        self.started.append((l, group, send, recv, srcs))
        return token

    def small_ready(self, grads):
        self.small_shapes = {n: grads[n].shape for n in SMALL}
        packed = _pack(grads).astype(BF16)
        send, recv, srcs, lands, token = exchange_start(
            [packed], [lax.empty((N_DEV,) + packed.shape, BF16)], None, "small_start")
        self.small = (send, recv, srcs, lands)
        return srcs[0]

    def small_sum(self, after):
        send, recv, srcs, lands = self.small
        srcs, lands = exchange_wait(send, recv, srcs, lands, after, None, "small_wait")
        parts = _set_own(lands[0], srcs[0], self.me)
        return _unpack(sum_parts(parts, "sum_small_grads"), self.small_shapes)

    def received(self, group, after):
        names = EXCHANGE_GROUPS[group]
        own = {n: [None] * self.w[n].shape[0] for n in names}
        for l, grp, send, recv, srcs in self.started:
            if grp != group:
                continue
            srcs, lands = exchange_wait(send, recv, srcs, [self.lands[n] for n in names], after, l,
                                        "exchange_wait_%s_%d" % (group, l))
            for n, a, sr in zip(names, lands, srcs):
                self.lands[n] = a
                own[n][l] = sr
        return {n: (self.lands[n], own[n]) for n in names}


def kernel(x, meta, attn_norm_g, w_in, b_f, conv_w, conv_b, w_gate_a, b_gate_a, w_gate_x, b_gate_x, lru_L, attn_out_g, rec_out_g, w_out, mlp_norm_g, w_up, w_down, final_g, loss_target, m_meta, m_attn_norm_g, m_w_in, m_b_f, m_conv_w, m_conv_b, m_w_gate_a, m_b_gate_a, m_w_gate_x, m_b_gate_x, m_lru_L, m_attn_out_g, m_rec_out_g, m_w_out, m_mlp_norm_g, m_w_up, m_w_down, m_final_g, v_meta, v_attn_norm_g, v_w_in, v_b_f, v_conv_w, v_conv_b, v_w_gate_a, v_b_gate_a, v_w_gate_x, v_b_gate_x, v_lru_L, v_attn_out_g, v_rec_out_g, v_w_out, v_mlp_norm_g, v_w_up, v_w_down, v_final_g):
    w = dict(meta=meta, attn_norm_g=attn_norm_g, w_in=w_in, b_f=b_f, conv_w=conv_w, conv_b=conv_b,
             w_gate_a=w_gate_a, b_gate_a=b_gate_a, w_gate_x=w_gate_x, b_gate_x=b_gate_x, lru_L=lru_L,
             attn_out_g=attn_out_g, rec_out_g=rec_out_g, w_out=w_out, mlp_norm_g=mlp_norm_g, w_up=w_up,
             w_down=w_down, final_g=final_g)
    mo = dict(meta=m_meta, attn_norm_g=m_attn_norm_g, w_in=m_w_in, b_f=m_b_f, conv_w=m_conv_w, conv_b=m_conv_b,
              w_gate_a=m_w_gate_a, b_gate_a=m_b_gate_a, w_gate_x=m_w_gate_x, b_gate_x=m_b_gate_x, lru_L=m_lru_L,
              attn_out_g=m_attn_out_g, rec_out_g=m_rec_out_g, w_out=m_w_out, mlp_norm_g=m_mlp_norm_g,
              w_up=m_w_up, w_down=m_w_down, final_g=m_final_g)
    vo = dict(meta=v_meta, attn_norm_g=v_attn_norm_g, w_in=v_w_in, b_f=v_b_f, conv_w=v_conv_w, conv_b=v_conv_b,
              w_gate_a=v_w_gate_a, b_gate_a=v_b_gate_a, w_gate_x=v_w_gate_x, b_gate_x=v_b_gate_x, lru_L=v_lru_L,
              attn_out_g=v_attn_out_g, rec_out_g=v_rec_out_g, w_out=v_w_out, mlp_norm_g=v_mlp_norm_g,
              w_up=v_w_up, w_down=v_w_down, final_g=v_final_g)
    depth = w_in.shape[0]
    nh = b_f.shape[1]
    rw = conv_b.shape[1]
    me = 4 * lax.axis_index("x") + 2 * lax.axis_index("y") + lax.axis_index("c")

    w['w_in_t'] = jnp.transpose(w_in, (2, 0, 1))
    swap = lambda a: jnp.swapaxes(a, 1, 2)
    step = _Step(w, nh, rw, me)
    g0 = step.group(0, step.first_after)
    meta_full = g0[2].transpose(1, 0, 2).reshape(N_META, -1)
    conv_full = g0[3].transpose(1, 2, 0, 3).reshape(depth, CONV_WIDTH, rw)
    small = {n: w[n] for n in SMALL}
    small['conv_w'] = conv_full

    loss_part, dh0 = local_step(x[0], loss_target[0], meta_full, small, step)
    loss = lax.psum(loss_part, ("x", "y", "c"))
    grad_x = dh0[N_META:N_META + x.shape[1]][None]

    out_g, out_d, out_m, out_v = {}, {}, {}, {}
    me1 = me.reshape(1).astype(jnp.int32)

    def update_big(group, after):
        for n, (r, owns) in step.received(group, after).items():
            if n == 'w_in':
                out = sum_adamw_t(r, owns, me1, swap(w[n]), swap(mo[n]), swap(vo[n]), "adamw_w_in")
                out = [swap(a) for a in out]
            else:
                shp = w[n].shape
                rows, cols = shp[0] * shp[1], shp[2]
                tr = min(512 if cols <= 512 else 256, shp[1])
                out = sum_adamw(r.reshape(N_DEV, rows, cols), owns, me1, w[n].reshape(rows, cols),
                                mo[n].reshape(rows, cols), vo[n].reshape(rows, cols), tr, "adamw_" + n)
                out = [a.reshape(shp) for a in out]
            out_g[n], out_d[n], out_m[n], out_v[n] = out
            after = out[0]
        return after

    update_big('mlp', step.started[-1][4][0])

    gsum = step.small_sum([out_g[n] for n in EXCHANGE_GROUPS['mlp']])
    gsum['meta'] = lax.dynamic_slice_in_dim(gsum['meta'], me * meta.shape[1], meta.shape[1], axis=1)
    gsum['conv_w'] = lax.dynamic_slice_in_dim(gsum['conv_w'], me * conv_w.shape[2], conv_w.shape[2], axis=2)
    as2d = lambda a: a.reshape(-1, a.shape[-1])
    deltas, new_m, new_v = adamw_group([as2d(gsum[n]) for n in SMALL], [as2d(w[n]) for n in SMALL],
                                       [as2d(mo[n]) for n in SMALL], [as2d(vo[n]) for n in SMALL], "adamw_small")
    for i, n in enumerate(SMALL):
        out_g[n] = gsum[n]
        out_d[n], out_m[n], out_v[n] = [a[i].reshape(w[n].shape) for a in (deltas, new_m, new_v)]

    update_big('in', deltas[0])

    return (loss, grad_x, *[out_g[n] for n in WEIGHTS], *[out_d[n] for n in WEIGHTS],
            *[out_m[n] for n in WEIGHTS], *[out_v[n] for n in WEIGHTS])
```

```python
import math

import jax
import jax.numpy as jnp
from jax import lax
from jax.experimental import pallas as pl
from jax.experimental.pallas import tpu as pltpu

F32 = jnp.float32
BF16 = jnp.bfloat16

N_DEV = 8
N_META = 16
HEAD_DIM = 64
CONV_WIDTH = 4
RG_C = 8.0
NORM_EPS = 1e-6
LANES = 128
SUBLANES = 8
ATT_BLOCK = 128
ATT_TQ = 512
NEG_BIG = -1e30
ATT_SCALE = 1.0 / math.sqrt(HEAD_DIM)

ADAM_LR = 0.001
ADAM_B1 = 0.9
ADAM_B2 = 0.999
ADAM_EPS = 1e-08
ADAM_WD = 0.01
ADAM_STEP = 10

VMEM_LIMIT_BYTES = 56 * 1024 * 1024
MESH = pl.DeviceIdType.MESH
ANY = pl.BlockSpec(memory_space=pl.ANY)


def _cparams(*sem):
    return pltpu.CompilerParams(dimension_semantics=sem if sem else None,
                                vmem_limit_bytes=VMEM_LIMIT_BYTES)


def _dot(a, b):
    return jnp.dot(a, b, preferred_element_type=F32)


def _dot_nt(a, b):
    return lax.dot_general(a, b, (((1,), (1,)), ((), ())), preferred_element_type=F32)


def _dot_tn(a, b):
    return lax.dot_general(a, b, (((0,), (0,)), ((), ())), preferred_element_type=F32)


def _sigmoid(x):
    return 0.5 * (1.0 + jnp.tanh(0.5 * x))


def _log_sigmoid(x):
    return jnp.minimum(x, 0.0) - jnp.log(1.0 + jnp.exp(-jnp.abs(x)))


def _expm1(x):
    series = x * (1.0 + x * (0.5 + x * (1.0 / 6.0 + x * (1.0 / 24.0))))
    return jnp.where(jnp.abs(x) < 1e-2, series, jnp.exp(x) - 1.0)


_GELU_K = math.sqrt(2.0 / math.pi)
_GELU_C = 0.044715


def _gelu(x):
    t = jnp.tanh(_GELU_K * (x + _GELU_C * x * x * x))
    return 0.5 * x * (1.0 + t)


def _gelu_and_grad(x):
    x2 = x * x
    t = jnp.tanh(_GELU_K * (x + _GELU_C * x2 * x))
    half = 0.5 * (1.0 + t)
    return x * half, half + 0.5 * x * (1.0 - t * t) * _GELU_K * (1.0 + 3.0 * _GELU_C * x2)


def _split3_dot(tri, x):
    hi = x.astype(BF16)
    r1 = x - hi.astype(F32)
    mid = r1.astype(BF16)
    lo = (r1 - mid.astype(F32)).astype(BF16)
    return _dot(tri, hi) + _dot(tri, mid) + _dot(tri, lo)


def _dot_split3(x, sel):
    hi = x.astype(BF16)
    r1 = x - hi.astype(F32)
    mid = r1.astype(BF16)
    lo = (r1 - mid.astype(F32)).astype(BF16)
    return _dot(hi, sel) + _dot(mid, sel) + _dot(lo, sel)


def _rms_fwd(x, g):
    r = lax.rsqrt(jnp.mean(x * x, axis=-1, keepdims=True) + NORM_EPS)
    return x * r * g


def _rms_bwd(x, g, dy):
    r = lax.rsqrt(jnp.mean(x * x, axis=-1, keepdims=True) + NORM_EPS)
    xn = x * r
    dxn = dy * g
    dx = r * (dxn - xn * jnp.mean(dxn * xn, axis=-1, keepdims=True))
    return dx, jnp.sum(dy * xn, axis=0, keepdims=True)


def _accumulate(ref, val, first):
    @pl.when(first)
    def _():
        ref[...] = val

    @pl.when(jnp.logical_not(first))
    def _():
        ref[...] += val


def in_proj(h, g1, w_in_t, wrest_t, nq, tm):
    tp, d = h.shape
    nr = wrest_t.shape[0]

    def body(h_ref, g_ref, wq_ref, wr_ref, z_ref, qkv_ref, rest_ref):
        z = _rms_fwd(h_ref[...], g_ref[...]).astype(BF16)
        z_ref[...] = z
        qkv_ref[...] = _dot_nt(z, wq_ref[...]).astype(BF16)
        rest_ref[...] = _dot_nt(z, wr_ref[...])

    return pl.pallas_call(
        body, name="in_proj", grid=(tp // tm,),
        in_specs=[pl.BlockSpec((tm, d), lambda i: (i, 0)),
                  pl.BlockSpec((1, d), lambda i: (0, 0)),
                  pl.BlockSpec((nq, d), lambda i: (0, 0)),
                  pl.BlockSpec((nr, d), lambda i: (0, 0))],
        out_specs=[pl.BlockSpec((tm, d), lambda i: (i, 0)),
                   pl.BlockSpec((tm, nq), lambda i: (i, 0)),
                   pl.BlockSpec((tm, nr), lambda i: (i, 0))],
        out_shape=[jax.ShapeDtypeStruct((tp, d), BF16),
                   jax.ShapeDtypeStruct((tp, nq), BF16),
                   jax.ShapeDtypeStruct((tp, nr), F32)],
        compiler_params=_cparams("parallel"),
    )(h, g1, w_in_t, wrest_t)


def fgate_fwd(rest, bf_pad, fcol):
    tp = rest.shape[0]
    nb = tp // ATT_BLOCK

    def body(f_ref, b_ref, c_ref, ct_ref):
        r_i = lax.broadcasted_iota(jnp.int32, (ATT_BLOCK, ATT_BLOCK), 0)
        c_i = lax.broadcasted_iota(jnp.int32, (ATT_BLOCK, ATT_BLOCK), 1)
        tri = (r_i >= c_i).astype(BF16)
        carry = jnp.zeros((1, LANES), F32)
        for i in range(nb):
            sl = slice(i * ATT_BLOCK, (i + 1) * ATT_BLOCK)
            lf = _log_sigmoid(f_ref[sl, :] + b_ref[...])
            cs = _split3_dot(tri, lf) + carry
            carry = cs[ATT_BLOCK - 1:ATT_BLOCK, :]
            c_ref[sl, :] = cs
            ct_ref[:, sl] = cs.T[0:SUBLANES, :]

    return pl.pallas_call(
        body, name="fgate_fwd", grid=(1,),
        in_specs=[pl.BlockSpec((tp, LANES), lambda i: (0, fcol)),
                  pl.BlockSpec((1, LANES), lambda i: (0, 0))],
        out_specs=[pl.BlockSpec((tp, LANES), lambda i: (0, 0)),
                   pl.BlockSpec((SUBLANES, tp), lambda i: (0, 0))],
        out_shape=[jax.ShapeDtypeStruct((tp, LANES), F32),
                   jax.ShapeDtypeStruct((SUBLANES, tp), F32)],
        compiler_params=_cparams("arbitrary"),
    )(rest, bf_pad)


def _pick_col(blk, head):
    lane = lax.broadcasted_iota(jnp.int32, blk.shape, 1)
    return jnp.sum(jnp.where(lane == head, blk, 0.0), axis=1, keepdims=True)


def _pick_row(blk, head):
    sub = lax.broadcasted_iota(jnp.int32, blk.shape, 0)
    return jnp.sum(jnp.where(sub == head, blk, 0.0), axis=0, keepdims=True)


def _att_tiles(tp):
    out, r0 = [], 0
    while r0 < tp:
        rows = min(ATT_TQ, tp - r0)
        out.append((r0, rows, r0 + rows))
        r0 += rows
    return out


def attn_fwd(qkv, c, ct, nh):
    tp = qkv.shape[0]
    npair = nh // 2
    tiles = _att_tiles(tp)

    def body(q_ref, k_ref, v_ref, c_ref, ct_ref, o_ref, lset_ref):
        p = pl.program_id(0)
        lset_ref[...] = jnp.zeros_like(lset_ref)
        for r0, nr, nk in tiles:
            rs = slice(r0, r0 + nr)
            causal = (r0 + lax.broadcasted_iota(jnp.int32, (nr, nk), 0)
                      >= lax.broadcasted_iota(jnp.int32, (nr, nk), 1))
            cblk = c_ref[rs, :]
            ctb = ct_ref[:, 0:nk]
            for hh in range(2):
                head = 2 * p + hh
                hs = slice(hh * HEAD_DIM, (hh + 1) * HEAD_DIM)
                q = q_ref[rs, hs] * ATT_SCALE
                s = _dot_nt(q, k_ref[0:nk, hs]) + (_pick_col(cblk, head) - _pick_row(ctb, head))
                s = jnp.where(causal, s, NEG_BIG)
                m = jnp.max(s, axis=1, keepdims=True)
                pm = jnp.exp(s - m)
                l = jnp.sum(pm, axis=1, keepdims=True)
                o_ref[rs, hs] = _dot(pm.astype(BF16), v_ref[0:nk, hs]) / l
                lse = m + jnp.log(l)
                lset_ref[hh:hh + 1, rs] = jnp.broadcast_to(lse, (nr, LANES)).T[0:1, :]

    pair = lambda p: (0, p)
    return pl.pallas_call(
        body, name="attn_fwd", grid=(npair,),
        in_specs=[pl.BlockSpec((tp, LANES), pair),
                  pl.BlockSpec((tp, LANES), lambda p: (0, npair + p)),
                  pl.BlockSpec((tp, LANES), lambda p: (0, 2 * npair + p)),
                  pl.BlockSpec((tp, LANES), lambda p: (0, 0)),
                  pl.BlockSpec((SUBLANES, tp), lambda p: (0, 0))],
        out_specs=[pl.BlockSpec((tp, LANES), pair),
                   pl.BlockSpec((None, SUBLANES, tp), lambda p: (p, 0, 0))],
        out_shape=[jax.ShapeDtypeStruct((tp, nh * HEAD_DIM), F32),
                   jax.ShapeDtypeStruct((npair, SUBLANES, tp), F32)],
        compiler_params=_cparams("parallel"),
    )(qkv, qkv, qkv, c, ct)


def _shift_down(x, k, n):
    if k == 0:
        return x
    rows = lax.broadcasted_iota(jnp.int32, x.shape, 0)
    return jnp.where(rows >= k, pltpu.roll(x, k, 0), 0.0)


def _shift_up(x, k, n):
    if k == 0:
        return x
    rows = lax.broadcasted_iota(jnp.int32, x.shape, 0)
    return jnp.where(rows < n - k, pltpu.roll(x, n - k, 0), 0.0)


def _conv_fwd(xr, cw_ref, cb_ref, n):
    xc = cw_ref[CONV_WIDTH - 1:CONV_WIDTH, :] * xr + cb_ref[...]
    for k in range(1, CONV_WIDTH):
        xc = xc + cw_ref[CONV_WIDTH - 1 - k:CONV_WIDTH - k, :] * _shift_down(xr, k, n)
    return xc


def _gates(xc, wga_ref, bga_ref, wgx_ref, bgx_ref, l_ref):
    xcb = xc.astype(BF16)
    r = _sigmoid(_dot(xcb, wga_ref[...]) + bga_ref[...])
    ig = _sigmoid(_dot(xcb, wgx_ref[...]) + bgx_ref[...])
    ls = _log_sigmoid(l_ref[...])
    log_a = RG_C * r * ls
    a = jnp.exp(log_a)
    mult = jnp.sqrt(-_expm1(2.0 * log_a))
    return xcb, r, ig, ls, log_a, a, mult


SCAN_UNROLL = 4


def _scan_rows(a_s, u_s, out_ref, n, reverse):
    nt = n // SUBLANES
    per = SCAN_UNROLL if nt % SCAN_UNROLL == 0 else 1
    row = lax.broadcasted_iota(jnp.int32, (SUBLANES, LANES), 0)
    last = 0 if reverse else SUBLANES - 1

    def tile_scan(a, u):
        for d in (1, 2, 4):
            if reverse:
                keep = row < SUBLANES - d
                sh = SUBLANES - d
            else:
                keep = row >= d
                sh = d
            a_sh = jnp.where(keep, pltpu.roll(a, sh, 0), 1.0)
            u_sh = jnp.where(keep, pltpu.roll(u, sh, 0), 0.0)
            u = a * u_sh + u
            a = a * a_sh
        return a, u

    def step(t, carry):
        tiles = []
        for k in range(per):
            tt = t * per + k
            if reverse:
                tt = nt - 1 - tt
            off = pl.multiple_of(tt * SUBLANES, SUBLANES)
            a, u = tile_scan(a_s[pl.ds(off, SUBLANES), :], u_s[pl.ds(off, SUBLANES), :])
            tiles.append((off, a, u))
        for off, a, u in tiles:
            out_ref[pl.ds(off, SUBLANES), :] = u + a * carry
            carry = u[last:last + 1, :] + a[last:last + 1, :] * carry
        return carry

    lax.fori_loop(0, nt // per, step, jnp.zeros((1, LANES), F32))


def rec_fwd(rest, convw, convb, wga, bga, wgx, bgx, lru, rw):
    tp = rest.shape[0]
    ng = rw // LANES

    def body(xr_ref, yr_ref, cw_ref, cb_ref, wga_ref, bga_ref, wgx_ref, bgx_ref, l_ref,
             rec_ref, hr_ref, xc_ref, r_ref, ig_ref, a_ref, mult_ref, u_s):
        xc = _conv_fwd(xr_ref[...], cw_ref, cb_ref, tp)
        xc_ref[...] = xc
        _, r, ig, ls, log_a, a, mult = _gates(xc, wga_ref, bga_ref, wgx_ref, bgx_ref, l_ref)
        r_ref[...] = r
        ig_ref[...] = ig
        a_ref[...] = a
        mult_ref[...] = mult
        u_s[...] = mult * ig * xc
        _scan_rows(a_ref, u_s, hr_ref, tp, reverse=False)
        rec_ref[...] = hr_ref[...] * _gelu(yr_ref[...])

    col = lambda g: (0, g)
    vec = pl.BlockSpec((1, LANES), col)
    big = pl.BlockSpec((tp, LANES), col)
    return pl.pallas_call(
        body, name="rec_fwd", grid=(ng,),
        in_specs=[big, pl.BlockSpec((tp, LANES), lambda g: (0, ng + g)),
                  pl.BlockSpec((CONV_WIDTH, LANES), col), vec,
                  pl.BlockSpec((None, LANES, LANES), lambda g: (g, 0, 0)), vec,
                  pl.BlockSpec((None, LANES, LANES), lambda g: (g, 0, 0)), vec, vec],
        out_specs=[big] * 7,
        out_shape=[jax.ShapeDtypeStruct((tp, rw), F32)] * 7,
        scratch_shapes=[pltpu.VMEM((tp, LANES), F32)],
        compiler_params=_cparams("parallel"),
    )(rest, rest, convw, convb, wga, bga, wgx, bgx, lru)


def out_mlp_fwd(h, o, rec, ga, gr, wout, g2, gup, gdown, tm):
    tp, d = h.shape
    aw, rw = o.shape[1], rec.shape[1]
    nf = gup.shape[0]
    tf = gup.shape[2]
    nb = MLP_BLOCKS if nf % MLP_BLOCKS == 0 else 1
    nj = nf // nb

    def body(h_ref, o_ref, rec_ref, ga_ref, gr_ref, w_ref, g2_ref, wu_ref, wd_ref,
             h2_ref, mix_ref, z2_ref, u_ref, h3_ref, acc):
        j = pl.program_id(1)

        @pl.when(j == 0)
        def _():
            mix_ref[:, 0:aw] = _rms_fwd(o_ref[...], ga_ref[...]).astype(BF16)
            mix_ref[:, aw:aw + rw] = _rms_fwd(rec_ref[...], gr_ref[...]).astype(BF16)
            h2 = h_ref[...] + _dot(mix_ref[...], w_ref[...])
            h2_ref[...] = h2
            acc[...] = h2
            z2_ref[...] = _rms_fwd(h2, g2_ref[...]).astype(BF16)

        z = z2_ref[...]
        part = None
        for b in range(nb):
            u = jnp.maximum(_dot(z, wu_ref[b]), 0.0)
            u_ref[:, b * tf:(b + 1) * tf] = u.astype(BF16)
            p = _dot((u * u).astype(BF16), wd_ref[b])
            part = p if part is None else part + p
        acc[...] += part

        @pl.when(j == nj - 1)
        def _():
            h3_ref[...] = acc[...]

    row = lambda i, j: (i, 0)
    fix = lambda i, j: (0, 0)
    return pl.pallas_call(
        body, name="out_mlp_fwd", grid=(tp // tm, nj),
        in_specs=[pl.BlockSpec((tm, d), row), pl.BlockSpec((tm, aw), row), pl.BlockSpec((tm, rw), row),
                  pl.BlockSpec((1, aw), fix), pl.BlockSpec((1, rw), fix),
                  pl.BlockSpec((d, d), fix), pl.BlockSpec((1, d), fix),
                  pl.BlockSpec((nb, d, tf), lambda i, j: (j, 0, 0)),
                  pl.BlockSpec((nb, tf, d), lambda i, j: (j, 0, 0))],
        out_specs=[pl.BlockSpec((tm, d), row), pl.BlockSpec((tm, d), row), pl.BlockSpec((tm, d), row),
                   pl.BlockSpec((tm, nb * tf), lambda i, j: (i, j)), pl.BlockSpec((tm, d), row)],
        out_shape=[jax.ShapeDtypeStruct((tp, d), F32), jax.ShapeDtypeStruct((tp, d), BF16),
                   jax.ShapeDtypeStruct((tp, d), BF16), jax.ShapeDtypeStruct((tp, nf * tf), BF16),
                   jax.ShapeDtypeStruct((tp, d), F32)],
        scratch_shapes=[pltpu.VMEM((tm, d), F32)],
        compiler_params=_cparams("parallel", "arbitrary"),
    )(h, o, rec, ga, gr, wout, g2, gup, gdown)


MLP_BLOCKS = 4


def loss_head(h, gf, tgt, t_real, tm):
    tp, d = h.shape

    def body(h_ref, g_ref, t_ref, dh_ref, dg_ref, loss_ref):
        i = pl.program_id(0)
        x = h_ref[...]
        g = g_ref[...]
        r = lax.rsqrt(jnp.mean(x * x, axis=-1, keepdims=True) + NORM_EPS)
        xn = x * r
        rows = i * tm + lax.broadcasted_iota(jnp.int32, (tm, 1), 0)
        valid = jnp.logical_and(rows >= N_META, rows < t_real)
        e = jnp.where(valid, xn * g - t_ref[...], 0.0)
        part = 0.5 * jnp.sum(jnp.sum(e * e, axis=1, keepdims=True) / d, axis=0, keepdims=True)
        dy = e / d
        dxn = dy * g
        dh_ref[...] = r * (dxn - xn * jnp.mean(dxn * xn, axis=-1, keepdims=True))
        _accumulate(dg_ref, jnp.sum(dy * xn, axis=0, keepdims=True), i == 0)
        _accumulate(loss_ref, jnp.broadcast_to(part, (1, LANES)), i == 0)

    row = lambda i: (i, 0)
    fix = lambda i: (0, 0)
    return pl.pallas_call(
        body, name="loss_head", grid=(tp // tm,),
        in_specs=[pl.BlockSpec((tm, d), row), pl.BlockSpec((1, d), fix), pl.BlockSpec((tm, d), row)],
        out_specs=[pl.BlockSpec((tm, d), row), pl.BlockSpec((1, d), fix), pl.BlockSpec((1, LANES), fix)],
        out_shape=[jax.ShapeDtypeStruct((tp, d), F32), jax.ShapeDtypeStruct((1, d), F32),
                   jax.ShapeDtypeStruct((1, LANES), F32)],
        compiler_params=_cparams("arbitrary"),
    )(h, gf, tgt)


def mlp_bwd(dh, u, h2, g2, gup, gdown, tm):
    tp, d = dh.shape
    nf = gup.shape[0]
    tf = gup.shape[2]
    nb = MLP_BLOCKS if nf % MLP_BLOCKS == 0 else 1
    nj = nf // nb
    ni = tp // tm

    def body(dh_ref, u_ref, h2_ref, g_ref, wu_ref, wd_ref, dup_ref, dh2_ref, dg_ref, dhb, acc):
        i = pl.program_id(0)
        j = pl.program_id(1)

        @pl.when(j == 0)
        def _():
            dhb[...] = dh_ref[...].astype(BF16)

        part = None
        for b in range(nb):
            cols = slice(b * tf, (b + 1) * tf)
            dup = (_dot_nt(dhb[...], wd_ref[b]) * (2.0 * u_ref[:, cols].astype(F32))).astype(BF16)
            dup_ref[:, cols] = dup
            p = _dot_nt(dup, wu_ref[b])
            part = p if part is None else part + p
        _accumulate(acc, part, j == 0)

        @pl.when(j == nj - 1)
        def _():
            dx, dg = _rms_bwd(h2_ref[...], g_ref[...], acc[...])
            dh2_ref[...] = dh_ref[...] + dx
            _accumulate(dg_ref, dg, i == 0)

    return pl.pallas_call(
        body, name="mlp_bwd", grid=(ni, nj),
        in_specs=[pl.BlockSpec((tm, d), lambda i, j: (i, 0)),
                  pl.BlockSpec((tm, nb * tf), lambda i, j: (i, j)),
                  pl.BlockSpec((tm, d), lambda i, j: (i, 0)),
                  pl.BlockSpec((1, d), lambda i, j: (0, 0)),
                  pl.BlockSpec((nb, d, tf), lambda i, j: (j, 0, 0)),
                  pl.BlockSpec((nb, tf, d), lambda i, j: (j, 0, 0))],
        out_specs=[pl.BlockSpec((tm, nb * tf), lambda i, j: (i, j)),
                   pl.BlockSpec((tm, d), lambda i, j: (i, 0)),
                   pl.BlockSpec((1, d), lambda i, j: (0, 0)),
                   pl.BlockSpec((tm, d), lambda i, j: (i, 0))],
        out_shape=[jax.ShapeDtypeStruct((tp, nf * tf), BF16), jax.ShapeDtypeStruct((tp, d), F32),
                   jax.ShapeDtypeStruct((1, d), F32), jax.ShapeDtypeStruct((tp, d), BF16)],
        scratch_shapes=[pltpu.VMEM((tm, d), F32)],
        compiler_params=_cparams("arbitrary", "arbitrary"),
    )(dh, u, h2, g2, gup, gdown)


def dw_mlp(u, dhb, z2, dup, tf):
    rows, dff = u.shape
    d = z2.shape[1]
    nf = dff // tf

    def body(u_ref, dh_ref, z_ref, dup_ref, dwd_ref, dwu_ref, zt):
        @pl.when(pl.program_id(0) == 0)
        def _():
            zt[...] = z_ref[...].T

        uf = u_ref[...].astype(F32)
        dwd_ref[...] = _dot_tn((uf * uf).astype(BF16), dh_ref[...]).astype(BF16)
        dwu_ref[...] = _dot(zt[...], dup_ref[...]).astype(BF16)

    col = lambda j: (0, j)
    fix = lambda j: (0, 0)
    return pl.pallas_call(
        body, name="dw_mlp", grid=(nf,),
        in_specs=[pl.BlockSpec((rows, tf), col), pl.BlockSpec((rows, d), fix),
                  pl.BlockSpec((rows, d), fix), pl.BlockSpec((rows, tf), col)],
        out_specs=[pl.BlockSpec((None, tf, d), lambda j: (j, 0, 0)),
                   pl.BlockSpec((None, d, tf), lambda j: (j, 0, 0))],
        out_shape=[jax.ShapeDtypeStruct((nf, tf, d), BF16), jax.ShapeDtypeStruct((nf, d, tf), BF16)],
        scratch_shapes=[pltpu.VMEM((d, rows), BF16)],
        compiler_params=_cparams("arbitrary"),
    )(u, dhb, z2, dup)


def out_proj_bwd(dh2, mix, o, rec, ga, gr, wout, tm):
    tp, d = dh2.shape
    aw, rw = o.shape[1], rec.shape[1]
    ni = tp // tm

    def body(dh_ref, mix_ref, o_ref, rec_ref, ga_ref, gr_ref, w_ref, do_ref, drec_ref, dga_ref, dgr_ref, dw_ref, acc):
        i = pl.program_id(0)
        dhb = dh_ref[...].astype(BF16)
        dmix = _dot_nt(dhb, w_ref[...])
        do, dga = _rms_bwd(o_ref[...], ga_ref[...], dmix[:, 0:aw])
        drec, dgr = _rms_bwd(rec_ref[...], gr_ref[...], dmix[:, aw:aw + rw])
        do_ref[...] = do
        drec_ref[...] = drec
        _accumulate(dga_ref, dga, i == 0)
        _accumulate(dgr_ref, dgr, i == 0)
        _accumulate(acc, _dot_tn(mix_ref[...], dhb), i == 0)

        @pl.when(i == ni - 1)
        def _():
            dw_ref[...] = acc[...].astype(BF16)

    row = lambda i: (i, 0)
    fix = lambda i: (0, 0)
    return pl.pallas_call(
        body, name="out_proj_bwd", grid=(ni,),
        in_specs=[pl.BlockSpec((tm, d), row), pl.BlockSpec((tm, d), row),
                  pl.BlockSpec((tm, aw), row), pl.BlockSpec((tm, rw), row),
                  pl.BlockSpec((1, aw), fix), pl.BlockSpec((1, rw), fix), pl.BlockSpec((d, d), fix)],
        out_specs=[pl.BlockSpec((tm, aw), row), pl.BlockSpec((tm, rw), row),
                   pl.BlockSpec((1, aw), fix), pl.BlockSpec((1, rw), fix), pl.BlockSpec((d, d), fix)],
        out_shape=[jax.ShapeDtypeStruct((tp, aw), F32), jax.ShapeDtypeStruct((tp, rw), F32),
                   jax.ShapeDtypeStruct((1, aw), F32), jax.ShapeDtypeStruct((1, rw), F32),
                   jax.ShapeDtypeStruct((d, d), BF16)],
        scratch_shapes=[pltpu.VMEM((d, d), F32)],
        compiler_params=_cparams("arbitrary"),
    )(dh2, mix, o, rec, ga, gr, wout)


def rec_bwd(drec, hr, xc, gates, rest, convw, wga, wgx, lru, rw):
    tp = rest.shape[0]
    ng = rw // LANES

    def body(drec_ref, hr_ref, xc_ref, r_ref, ig_ref, a_ref, mult_ref, xr_ref, yr_ref, cw_ref, wga_ref, wgx_ref,
             l_ref, dxr_ref, dyr_ref, dwga_ref, dwgx_ref, vec_ref, a_s, u_s, lam_s):
        xc = xc_ref[...]
        h = hr_ref[...]
        drec = drec_ref[...]
        r, ig, a, mult = r_ref[...], ig_ref[...], a_ref[...], mult_ref[...]
        xcb = xc.astype(BF16)
        ls = _log_sigmoid(l_ref[...])
        gelu, gelu_grad = _gelu_and_grad(yr_ref[...])
        dyr_ref[...] = (drec * h * gelu_grad).astype(BF16)
        a_s[...] = _shift_up(a, 1, tp)
        u_s[...] = drec * gelu
        _scan_rows(a_s, u_s, lam_s, tp, reverse=True)
        lam = lam_s[...]
        da = lam * _shift_down(h, 1, tp)
        dmult = lam * ig * xc
        dig = lam * mult * xc
        dxc = lam * mult * ig
        dlog_a = da * a - dmult * (a * a) / mult
        dr = dlog_a * (RG_C * ls)
        dl = jnp.sum(dlog_a * (RG_C * r), axis=0, keepdims=True) * _sigmoid(-l_ref[...])
        dpa = dr * r * (1.0 - r)
        dpx = dig * ig * (1.0 - ig)
        dpab = dpa.astype(BF16)
        dpxb = dpx.astype(BF16)
        dxc = dxc + _dot_nt(dpab, wga_ref[...]) + _dot_nt(dpxb, wgx_ref[...])
        dwga_ref[...] = _dot_tn(xcb, dpab)
        dwgx_ref[...] = _dot_tn(xcb, dpxb)
        xr = xr_ref[...]
        dxr = cw_ref[CONV_WIDTH - 1:CONV_WIDTH, :] * dxc
        for k in range(1, CONV_WIDTH):
            dxr = dxr + cw_ref[CONV_WIDTH - 1 - k:CONV_WIDTH - k, :] * _shift_up(dxc, k, tp)
        dxr_ref[...] = dxr.astype(BF16)
        for k in range(CONV_WIDTH):
            vec_ref[k:k + 1, :] = jnp.sum(dxc * _shift_down(xr, CONV_WIDTH - 1 - k, tp), axis=0, keepdims=True)
        vec_ref[4:5, :] = jnp.sum(dxc, axis=0, keepdims=True)
        vec_ref[5:6, :] = jnp.sum(dpa, axis=0, keepdims=True)
        vec_ref[6:7, :] = jnp.sum(dpx, axis=0, keepdims=True)
        vec_ref[7:8, :] = dl

    col = lambda g: (0, g)
    vec = pl.BlockSpec((1, LANES), col)
    big = pl.BlockSpec((tp, LANES), col)
    sq = pl.BlockSpec((None, LANES, LANES), lambda g: (g, 0, 0))
    return pl.pallas_call(
        body, name="rec_bwd", grid=(ng,),
        in_specs=[big] * 8 + [pl.BlockSpec((tp, LANES), lambda g: (0, ng + g)),
                                pl.BlockSpec((CONV_WIDTH, LANES), col), sq, sq, vec],
        out_specs=[big, big, sq, sq, pl.BlockSpec((None, SUBLANES, LANES), lambda g: (g, 0, 0))],
        out_shape=[jax.ShapeDtypeStruct((tp, rw), BF16), jax.ShapeDtypeStruct((tp, rw), BF16),
                   jax.ShapeDtypeStruct((ng, LANES, LANES), F32), jax.ShapeDtypeStruct((ng, LANES, LANES), F32),
                   jax.ShapeDtypeStruct((ng, SUBLANES, LANES), F32)],
        scratch_shapes=[pltpu.VMEM((tp, LANES), F32)] * 3,
        compiler_params=_cparams("parallel"),
    )(drec, hr, xc, *gates, rest, rest, convw, wga, wgx, lru)


def attn_bwd(qkv, do, o, lset, c, ct, nh):
    tp = qkv.shape[0]
    npair = nh // 2
    aw = nh * HEAD_DIM
    tiles = _att_tiles(tp)

    def body(q_ref, k_ref, v_ref, do_ref, o_ref, lset_ref, c_ref, ct_ref,
             dq_ref, dk_ref, dv_ref, drow_ref, dcol_ref, dk_acc, dv_acc, dq_t):
        p = pl.program_id(0)
        k_t = k_ref[...].T
        dk_acc[...] = jnp.zeros_like(dk_acc)
        dv_acc[...] = jnp.zeros_like(dv_acc)
        dcol_ref[...] = jnp.zeros_like(dcol_ref)
        drow_ref[...] = jnp.zeros_like(drow_ref)
        for r0, nr, nk in tiles:
            rs = slice(r0, r0 + nr)
            causal = (r0 + lax.broadcasted_iota(jnp.int32, (nk, nr), 1)
                      >= lax.broadcasted_iota(jnp.int32, (nk, nr), 0))
            cblk = c_ref[0:nk, :]
            ctb = ct_ref[:, rs]
            for hh in range(2):
                head = 2 * p + hh
                hs = slice(hh * HEAD_DIM, (hh + 1) * HEAD_DIM)
                q = q_ref[rs, hs]
                k = k_ref[0:nk, hs]
                dof = do_ref[rs, hs]
                do16 = dof.astype(BF16)
                delta = jnp.sum(dof * o_ref[rs, hs], axis=1, keepdims=True)
                delta_row = jnp.broadcast_to(delta, (nr, LANES)).T[0:1, :]
                s_t = _dot_nt(k, q * ATT_SCALE) + (_pick_row(ctb, head) - _pick_col(cblk, head))
                p_t = jnp.where(causal, jnp.exp(s_t - lset_ref[hh:hh + 1, rs]), 0.0)
                ds_t = p_t * (_dot_nt(v_ref[0:nk, hs], do16) - delta_row)
                p16 = p_t.astype(BF16)
                ds16 = ds_t.astype(BF16)
                dv_acc[0:nk, hs] += _dot(p16, do16)
                dk_acc[0:nk, hs] += _dot(ds16, q) * ATT_SCALE
                dq_t[hs, rs] = _dot(k_t[hs, 0:nk], ds16)
                drow_ref[hh:hh + 1, rs] = jnp.sum(ds_t, axis=0, keepdims=True)
                dcol_ref[0:nk, hs] -= jnp.broadcast_to(jnp.sum(ds_t, axis=1, keepdims=True), (nk, HEAD_DIM))
        dk_ref[...] = dk_acc[...].astype(BF16)
        dv_ref[...] = dv_acc[...].astype(BF16)
        dq_ref[...] = (dq_t[...].T * ATT_SCALE).astype(BF16)

    pair = lambda p: (0, p)
    return pl.pallas_call(
        body, name="attn_bwd", grid=(npair,),
        in_specs=[pl.BlockSpec((tp, LANES), pair),
                  pl.BlockSpec((tp, LANES), lambda p: (0, npair + p)),
                  pl.BlockSpec((tp, LANES), lambda p: (0, 2 * npair + p)),
                  pl.BlockSpec((tp, LANES), pair),
                  pl.BlockSpec((tp, LANES), pair),
                  pl.BlockSpec((None, SUBLANES, tp), lambda p: (p, 0, 0)),
                  pl.BlockSpec((tp, LANES), lambda p: (0, 0)),
                  pl.BlockSpec((SUBLANES, tp), lambda p: (0, 0))],
        out_specs=[pl.BlockSpec((tp, LANES), pair), pl.BlockSpec((tp, LANES), pair),
                   pl.BlockSpec((tp, LANES), pair),
                   pl.BlockSpec((None, SUBLANES, tp), lambda p: (p, 0, 0)),
                   pl.BlockSpec((tp, LANES), pair)],
        out_shape=[jax.ShapeDtypeStruct((tp, aw), BF16), jax.ShapeDtypeStruct((tp, aw), BF16),
                   jax.ShapeDtypeStruct((tp, aw), BF16),
                   jax.ShapeDtypeStruct((npair, SUBLANES, tp), F32),
                   jax.ShapeDtypeStruct((tp, aw), F32)],
        scratch_shapes=[pltpu.VMEM((tp, LANES), F32), pltpu.VMEM((tp, LANES), F32),
                        pltpu.VMEM((LANES, tp), F32)],
        compiler_params=_cparams("parallel"),
    )(qkv, qkv, qkv, do, o, lset, c, ct)


def fgate_bwd(dct8, drs, rest, bf_pad, fcol):
    tp = rest.shape[0]
    aw = drs.shape[1]
    nb = tp // ATT_BLOCK
    B = ATT_BLOCK

    def body(d_ref, drs_ref, f_ref, b_ref, dfl_ref, db_ref, pad_s, dc_s):
        r_i = lax.broadcasted_iota(jnp.int32, (B, B), 0)
        c_i = lax.broadcasted_iota(jnp.int32, (B, B), 1)
        triu = (c_i >= r_i).astype(BF16)
        sel = (lax.broadcasted_iota(jnp.int32, (aw, LANES), 0)
               == HEAD_DIM * lax.broadcasted_iota(jnp.int32, (aw, LANES), 1)).astype(BF16)
        pad_s[...] = jnp.zeros_like(pad_s)
        pad_s[0:SUBLANES, :] = d_ref[...]
        dc_s[...] = pad_s[...].T + _dot_split3(drs_ref[...], sel)
        carry = jnp.zeros((1, LANES), F32)
        for i in range(nb - 1, -1, -1):
            sl = slice(i * B, (i + 1) * B)
            rc = _split3_dot(triu, dc_s[sl, :])
            dc_s[sl, :] = rc + carry
            carry = carry + rc[0:1, :]
        dfl = dc_s[...] * _sigmoid(-(f_ref[...] + b_ref[...]))
        dfl_ref[...] = dfl.astype(BF16)
        db_ref[...] = jnp.sum(dfl, axis=0, keepdims=True)

    return pl.pallas_call(
        body, name="fgate_bwd", grid=(1,),
        in_specs=[pl.BlockSpec((SUBLANES, tp), lambda i: (0, 0)),
                  pl.BlockSpec((tp, aw), lambda i: (0, 0)),
                  pl.BlockSpec((tp, LANES), lambda i: (0, fcol)),
                  pl.BlockSpec((1, LANES), lambda i: (0, 0))],
        out_specs=[pl.BlockSpec((tp, LANES), lambda i: (0, 0)),
                   pl.BlockSpec((1, LANES), lambda i: (0, 0))],
        out_shape=[jax.ShapeDtypeStruct((tp, LANES), BF16), jax.ShapeDtypeStruct((1, LANES), F32)],
        scratch_shapes=[pltpu.VMEM((LANES, tp), F32), pltpu.VMEM((tp, LANES), F32)],
        compiler_params=_cparams("arbitrary"),
    )(dct8, drs, rest, bf_pad)


def in_proj_bwd(dh2, parts, w_in_t, wrest_t, h, g1, tm):
    tp, d = h.shape
    dq, dk, dv, dxr, dyr, dfl = parts
    aw, rw = dq.shape[1], dxr.shape[1]

    def body(dh2_ref, dq_ref, dk_ref, dv_ref, dxr_ref, dyr_ref, dfl_ref, wq_ref, wr_ref, h_ref, g_ref,
             dh_ref, dg_ref):
        i = pl.program_id(0)
        dz = _dot(dq_ref[...], wq_ref[0:aw, :])
        dz += _dot(dk_ref[...], wq_ref[aw:2 * aw, :])
        dz += _dot(dv_ref[...], wq_ref[2 * aw:3 * aw, :])
        dz += _dot(dxr_ref[...], wr_ref[0:rw, :])
        dz += _dot(dyr_ref[...], wr_ref[rw:2 * rw, :])
        dz += _dot(dfl_ref[...], wr_ref[2 * rw:2 * rw + LANES, :])
        dx, dg = _rms_bwd(h_ref[...], g_ref[...], dz)
        dh_ref[...] = dh2_ref[...] + dx
        _accumulate(dg_ref, dg, i == 0)

    row = lambda i: (i, 0)
    fix = lambda i: (0, 0)
    return pl.pallas_call(
        body, name="in_proj_bwd", grid=(tp // tm,),
        in_specs=[pl.BlockSpec((tm, d), row),
                  pl.BlockSpec((tm, aw), row), pl.BlockSpec((tm, aw), row), pl.BlockSpec((tm, aw), row),
                  pl.BlockSpec((tm, rw), row), pl.BlockSpec((tm, rw), row), pl.BlockSpec((tm, LANES), row),
                  pl.BlockSpec((3 * aw, d), fix), pl.BlockSpec(wrest_t.shape, fix),
                  pl.BlockSpec((tm, d), row), pl.BlockSpec((1, d), fix)],
        out_specs=[pl.BlockSpec((tm, d), row), pl.BlockSpec((1, d), fix)],
        out_shape=[jax.ShapeDtypeStruct((tp, d), F32), jax.ShapeDtypeStruct((1, d), F32)],
        compiler_params=_cparams("arbitrary"),
    )(dh2, dq, dk, dv, dxr, dyr, dfl, w_in_t, wrest_t, h, g1)


def dw_in_t(z, parts, nh, tr):
    tp, d = z.shape
    dq, dk, dv, dxr, dyr, dfl = parts
    aw, rw = dq.shape[1], dxr.shape[1]
    d_in = 3 * aw + nh + 2 * rw
    nr = tp // tr
    offs = [(0, aw), (aw, aw), (2 * aw, aw), (3 * aw + nh, rw), (3 * aw + nh + rw, rw)]

    def body(z_ref, dq_ref, dk_ref, dv_ref, dxr_ref, dyr_ref, dfl_ref, o_ref, acc):
        r = pl.program_id(0)

        @pl.when(r == 0)
        def _():
            acc[...] = jnp.zeros_like(acc)

        zt = z_ref[...]
        for (o, n), ref in zip(offs, (dq_ref, dk_ref, dv_ref, dxr_ref, dyr_ref)):
            acc[o:o + n, :] += _dot_tn(ref[...], zt)
        acc[3 * aw:3 * aw + nh, :] += _dot_tn(dfl_ref[...], zt)[0:nh, :]

        @pl.when(r == nr - 1)
        def _():
            o_ref[...] = acc[...].astype(BF16)

    row = lambda r: (r, 0)
    return pl.pallas_call(
        body, name="dw_in", grid=(nr,),
        in_specs=[pl.BlockSpec((tr, d), row),
                  pl.BlockSpec((tr, aw), row), pl.BlockSpec((tr, aw), row), pl.BlockSpec((tr, aw), row),
                  pl.BlockSpec((tr, rw), row), pl.BlockSpec((tr, rw), row), pl.BlockSpec((tr, LANES), row)],
        out_specs=pl.BlockSpec((d_in, d), lambda r: (0, 0)),
        out_shape=jax.ShapeDtypeStruct((d_in, d), BF16),
        scratch_shapes=[pltpu.VMEM((d_in, d), F32)],
        compiler_params=_cparams("arbitrary"),
    )(z, dq, dk, dv, dxr, dyr, dfl)


def _place():
    return lax.axis_index("x"), lax.axis_index("y"), lax.axis_index("c")


HBM = pl.BlockSpec(memory_space=pltpu.HBM)
SEM = pl.BlockSpec(memory_space=pltpu.SEMAPHORE)
EFFECT = pltpu.SideEffectType.DATAFLOW_SIDE_EFFECTING


def _in_hbm(a):
    return pltpu.with_memory_space_constraint(a, pltpu.HBM)


def _as_list(a):
    return list(a) if isinstance(a, (list, tuple)) else [a]


def _gather_targets(x, y, c):
    return [(x, y, 1 - c), (1 - x, y, c), (x, 1 - y, c), (1 - x, 1 - y, c)]


def _slot(t):
    return 4 * t[0] + 2 * t[1] + t[2]


def gather_start(groups, name):
    flat = [a for g in groups for a in g]
    n = len(flat)
    ng = len(groups)
    lands = [lax.empty((N_DEV,) + a.shape, a.dtype) for a in flat]

    def body(*refs):
        src, land = refs[:n], refs[n:2 * n]
        sems = refs[2 * n:2 * n + 2 * ng]
        token = refs[-1]
        x, y, c = _place()
        me = 4 * x + 2 * y + c
        i = 0
        for gi, g in enumerate(groups):
            for a in range(len(g)):
                for k, t in enumerate(_gather_targets(x, y, c)):
                    pltpu.make_async_remote_copy(
                        src_ref=src[i], dst_ref=land[i].at[me],
                        send_sem=sems[2 * gi].at[4 * a + k], recv_sem=sems[2 * gi + 1].at[4 * a + k],
                        device_id=t, device_id_type=MESH).start()
                i += 1
        token[...] = jnp.zeros_like(token)

    sem_shapes = []
    for g in groups:
        sem_shapes += [pltpu.SemaphoreType.DMA((4 * len(g),)), pltpu.SemaphoreType.DMA((4 * len(g),))]
    out = pl.pallas_call(
        body, name=name,
        out_shape=sem_shapes + [pltpu.HBM(a.shape, a.dtype) for a in flat + lands]
        + [jax.ShapeDtypeStruct((SUBLANES, LANES), F32)],
        in_specs=[HBM] * (2 * n),
        out_specs=[SEM] * (2 * ng) + [HBM] * (2 * n) + [pl.BlockSpec(memory_space=pltpu.VMEM)],
        input_output_aliases={i: 2 * ng + i for i in range(2 * n)},
        compiler_params=pltpu.CompilerParams(has_side_effects=EFFECT),
    )(*[_in_hbm(a) for a in flat + lands])
    sems = out[:2 * ng]
    thru = out[2 * ng:2 * ng + 2 * n]
    srcs_t, lands_t = thru[:n], thru[n:]
    res, i = [], 0
    for gi, g in enumerate(groups):
        res.append((sems[2 * gi], sems[2 * gi + 1], srcs_t[i:i + len(g)], lands_t[i:i + len(g)]))
        i += len(g)
    return res, out[-1]


def gather_wait(send, recv, srcs, lands, after, name):
    n = len(srcs)

    def body(*refs):
        src, land = refs[:n], refs[n:2 * n]
        send_sem, recv_sem = refs[2 * n], refs[2 * n + 1]
        x, y, c = _place()
        for a in range(n):
            for k, t in enumerate(_gather_targets(x, y, c)):
                cp = pltpu.make_async_remote_copy(
                    src_ref=src[a], dst_ref=land[a].at[_slot(t)],
                    send_sem=send_sem.at[4 * a + k], recv_sem=recv_sem.at[4 * a + k],
                    device_id=t, device_id_type=MESH)
                cp.wait_send()
                cp.wait_recv()

    out = pl.pallas_call(
        body, name=name,
        out_shape=[pltpu.HBM(a.shape, a.dtype) for a in list(srcs) + list(lands)],
        in_specs=[HBM] * (2 * n) + [SEM, SEM] + [ANY] * len(_as_list(after)),
        out_specs=[HBM] * (2 * n),
        input_output_aliases={i: i for i in range(2 * n)},
        compiler_params=pltpu.CompilerParams(has_side_effects=EFFECT),
    )(*srcs, *lands, send, recv, *_as_list(after))
    return out[:n], out[n:]


def forward_start(lands, name):
    n = len(lands)

    def body(*refs):
        land = refs[:n]
        send_sem, recv_sem = refs[n], refs[n + 1]
        token = refs[-1]
        x, y, c = _place()
        for a in range(n):
            for j, chip in enumerate([(1 - x, y), (x, 1 - y), (1 - x, 1 - y)]):
                blk = land[a].at[_slot((*chip, c))]
                pltpu.make_async_remote_copy(src_ref=blk, dst_ref=blk, send_sem=send_sem.at[3 * a + j],
                                             recv_sem=recv_sem.at[3 * a + j], device_id=(x, y, 1 - c),
                                             device_id_type=MESH).start()
        token[...] = jnp.zeros_like(token)

    out = pl.pallas_call(
        body, name=name,
        out_shape=[pltpu.SemaphoreType.DMA((3 * n,)), pltpu.SemaphoreType.DMA((3 * n,))]
        + [pltpu.HBM(a.shape, a.dtype) for a in lands] + [jax.ShapeDtypeStruct((SUBLANES, LANES), F32)],
        in_specs=[HBM] * n,
        out_specs=[SEM, SEM] + [HBM] * n + [pl.BlockSpec(memory_space=pltpu.VMEM)],
        input_output_aliases={i: 2 + i for i in range(n)},
        compiler_params=pltpu.CompilerParams(has_side_effects=EFFECT),
    )(*[_in_hbm(a) for a in lands])
    return out[0], out[1], out[2:2 + n], out[-1][0, 0]


def forward_wait(send, recv, lands, after, name):
    n = len(lands)

    def body(*refs):
        land = refs[:n]
        send_sem, recv_sem = refs[n], refs[n + 1]
        x, y, c = _place()
        for a in range(n):
            for j, chip in enumerate([(1 - x, y), (x, 1 - y), (1 - x, 1 - y)]):
                cp = pltpu.make_async_remote_copy(
                    src_ref=land[a].at[_slot((*chip, c))], dst_ref=land[a].at[_slot((*chip, 1 - c))],
                    send_sem=send_sem.at[3 * a + j], recv_sem=recv_sem.at[3 * a + j],
                    device_id=(x, y, 1 - c), device_id_type=MESH)
                cp.wait_send()
                cp.wait_recv()

    return pl.pallas_call(
        body, name=name,
        out_shape=[pltpu.HBM(a.shape, a.dtype) for a in lands],
        in_specs=[HBM] * n + [SEM, SEM, ANY],
        out_specs=[HBM] * n,
        input_output_aliases={i: i for i in range(n)},
        compiler_params=pltpu.CompilerParams(has_side_effects=EFFECT),
    )(*lands, send, recv, after)


def _relations():
    return [(dx, dy, dc) for dx in (0, 1) for dy in (0, 1) for dc in (0, 1) if dx + dy + dc]


def _peer(x, y, c, rel):
    return ((1 - x) if rel[0] else x, (1 - y) if rel[1] else y, (1 - c) if rel[2] else c)


def exchange_start(srcs, lands, layer, name, after=()):
    n = len(srcs)
    after = _as_list(after)

    def body(*refs):
        src, land = refs[:n], refs[n:2 * n]
        send_sem, recv_sem = refs[2 * n + len(after)], refs[2 * n + len(after) + 1]
        token = refs[-1]
        x, y, c = _place()
        me = 4 * x + 2 * y + c
        for k, rel in enumerate(_relations()):
            peer = _peer(x, y, c, rel)
            for a in range(n):
                pltpu.make_async_remote_copy(
                    src_ref=src[a] if layer is None else src[a].at[_slot(peer)],
                    dst_ref=land[a].at[me] if layer is None else land[a].at[me, layer],
                    send_sem=send_sem.at[7 * a + k], recv_sem=recv_sem.at[7 * a + k],
                    device_id=peer, device_id_type=MESH).start()
        token[...] = jnp.zeros_like(token)

    out = pl.pallas_call(
        body, name=name,
        out_shape=[pltpu.SemaphoreType.DMA((7 * n,)), pltpu.SemaphoreType.DMA((7 * n,))]
        + [pltpu.HBM(a.shape, a.dtype) for a in list(srcs) + list(lands)]
        + [jax.ShapeDtypeStruct((SUBLANES, LANES), F32)],
        in_specs=[HBM] * (2 * n) + [ANY] * len(after),
        out_specs=[SEM, SEM] + [HBM] * (2 * n) + [pl.BlockSpec(memory_space=pltpu.VMEM)],
        input_output_aliases={i: 2 + i for i in range(2 * n)},
        compiler_params=pltpu.CompilerParams(has_side_effects=EFFECT),
    )(*[_in_hbm(a) for a in list(srcs) + list(lands)], *after)
    return out[0], out[1], out[2:2 + n], out[2 + n:2 + 2 * n], out[-1][0, 0]


def exchange_wait(send, recv, srcs, lands, after, layer, name):
    n = len(srcs)

    def body(*refs):
        src, land = refs[:n], refs[n:2 * n]
        send_sem, recv_sem = refs[2 * n], refs[2 * n + 1]
        x, y, c = _place()
        for k, rel in enumerate(_relations()):
            peer = _peer(x, y, c, rel)
            for a in range(n):
                cp = pltpu.make_async_remote_copy(
                    src_ref=src[a] if layer is None else src[a].at[_slot(peer)],
                    dst_ref=land[a].at[_slot(peer)] if layer is None else land[a].at[_slot(peer), layer],
                    send_sem=send_sem.at[7 * a + k], recv_sem=recv_sem.at[7 * a + k],
                    device_id=peer, device_id_type=MESH)
                cp.wait_send()
                cp.wait_recv()

    out = pl.pallas_call(
        body, name=name,
        out_shape=[pltpu.HBM(a.shape, a.dtype) for a in list(srcs) + list(lands)],
        in_specs=[HBM] * (2 * n) + [SEM, SEM] + [ANY] * len(_as_list(after)),
        out_specs=[HBM] * (2 * n),
        input_output_aliases={i: i for i in range(2 * n)},
        compiler_params=pltpu.CompilerParams(has_side_effects=EFFECT),
    )(*srcs, *lands, send, recv, *_as_list(after))
    return out[:n], out[n:]


def _adamw_math(g, w, m, v):
    m = ADAM_B1 * m + (1.0 - ADAM_B1) * g
    v = ADAM_B2 * v + (1.0 - ADAM_B2) * (g * g)
    m_hat = m / (1.0 - ADAM_B1 ** ADAM_STEP)
    v_hat = v / (1.0 - ADAM_B2 ** ADAM_STEP)
    delta = -ADAM_LR * (m_hat / (jnp.sqrt(v_hat) + ADAM_EPS) + ADAM_WD * w)
    return delta, m, v


def _sum_with_own(p_ref, own_refs, layer, me):
    own = own_refs[0][...]
    for k in range(1, len(own_refs)):
        own = jnp.where(layer == k, own_refs[k][...], own)
    g = None
    for p in range(p_ref.shape[0]):
        term = jnp.where(me == p, own, p_ref[p]).astype(F32)
        g = term if g is None else g + term
    return g


def sum_adamw(parts, owns, me, w, m, v, tr, name):
    npart, rows, cols = parts.shape
    nl = len(owns)
    per_layer = rows // nl // tr

    def body(me_ref, p_ref, *refs):
        own_refs = refs[:nl]
        w_ref, m_ref, v_ref, g_ref, d_ref, nm_ref, nv_ref = refs[nl:]
        g = _sum_with_own(p_ref, own_refs, pl.program_id(0) // per_layer, me_ref[0])
        delta, nm, nv = _adamw_math(g, w_ref[...], m_ref[...], v_ref[...])
        g_ref[...] = g
        d_ref[...] = delta
        nm_ref[...] = nm
        nv_ref[...] = nv

    blk = pl.BlockSpec((tr, cols), lambda i, me_ref: (i, 0))
    own_specs = [pl.BlockSpec((None, tr, cols),
                              lambda i, me_ref, l=l: (me_ref[0], jnp.clip(i - l * per_layer, 0, per_layer - 1), 0))
                 for l in range(nl)]
    return pl.pallas_call(
        body, name=name,
        grid_spec=pltpu.PrefetchScalarGridSpec(
            num_scalar_prefetch=1, grid=(rows // tr,),
            in_specs=[pl.BlockSpec((npart, tr, cols), lambda i, me_ref: (0, i, 0))] + own_specs + [blk, blk, blk],
            out_specs=[blk] * 4),
        out_shape=[jax.ShapeDtypeStruct((rows, cols), F32)] * 4,
        compiler_params=_cparams("arbitrary"),
    )(me, parts, *owns, w, m, v)


def sum_adamw_t(parts, owns, me, w, m, v, name):
    npart, nl, rows, cols = parts.shape

    def body(me_ref, p_ref, *refs):
        own_refs = refs[:nl]
        w_ref, m_ref, v_ref, g_ref, d_ref, nm_ref, nv_ref = refs[nl:]
        g = _sum_with_own(p_ref, own_refs, pl.program_id(0), me_ref[0])
        delta, nm, nv = _adamw_math(g, w_ref[...], m_ref[...], v_ref[...])
        g_ref[...] = g
        d_ref[...] = delta
        nm_ref[...] = nm
        nv_ref[...] = nv

    blk = pl.BlockSpec((None, rows, cols), lambda l, me_ref: (l, 0, 0))
    own_specs = [pl.BlockSpec((None, rows, cols), lambda l, me_ref: (me_ref[0], 0, 0)) for _ in range(nl)]
    return pl.pallas_call(
        body, name=name,
        grid_spec=pltpu.PrefetchScalarGridSpec(
            num_scalar_prefetch=1, grid=(nl,),
            in_specs=[pl.BlockSpec((npart, None, rows, cols), lambda l, me_ref: (0, l, 0, 0))] + own_specs
            + [blk, blk, blk],
            out_specs=[blk] * 4),
        out_shape=[jax.ShapeDtypeStruct((nl, rows, cols), F32)] * 4,
        compiler_params=_cparams("arbitrary"),
    )(me, parts, *owns, w, m, v)


def adamw_group(gs, ws, ms, vs, name):
    n = len(gs)

    def body(*refs):
        g, w, m, v, outs = refs[:n], refs[n:2 * n], refs[2 * n:3 * n], refs[3 * n:4 * n], refs[4 * n:]
        for i in range(n):
            delta, nm, nv = _adamw_math(g[i][...], w[i][...], m[i][...], v[i][...])
            outs[i][...] = delta
            outs[n + i][...] = nm
            outs[2 * n + i][...] = nv

    vmem = pl.BlockSpec(memory_space=pltpu.VMEM)
    out = pl.pallas_call(
        body, name=name,
        in_specs=[vmem] * (4 * n), out_specs=[vmem] * (3 * n),
        out_shape=[jax.ShapeDtypeStruct(a.shape, F32) for a in list(ws) * 3],
        compiler_params=_cparams(),
    )(*gs, *ws, *ms, *vs)
    return out[:n], out[n:2 * n], out[2 * n:]


def sum_parts(parts, name):
    npart, rows, cols = parts.shape

    def body(p_ref, g_ref):
        g = p_ref[0].astype(F32)
        for p in range(1, npart):
            g = g + p_ref[p].astype(F32)
        g_ref[...] = g

    return pl.pallas_call(
        body, name=name, grid=(1,),
        in_specs=[pl.BlockSpec((npart, rows, cols), lambda i: (0, 0, 0))],
        out_specs=pl.BlockSpec((rows, cols), lambda i: (0, 0)),
        out_shape=jax.ShapeDtypeStruct((rows, cols), F32),
        compiler_params=_cparams("arbitrary"),
    )(parts)


def _round_up(n, m):
    return (n + m - 1) // m * m


def _block_diag_pairs(w):
    nb, b, _ = w.shape
    per = LANES // b
    ng = nb // per
    w = w.reshape(ng, per, b, b)
    eye = jnp.eye(per, dtype=w.dtype)
    out = jnp.einsum('gpij,pq->gpiqj', w, eye).reshape(ng, LANES, LANES)
    return out.astype(BF16)


def _block_diag_extract(g, b):
    ng = g.shape[0]
    per = LANES // b
    g = g.reshape(ng, per, b, per, b)
    idx = jnp.arange(per)
    return g[:, idx, :, idx, :].transpose(1, 0, 2, 3).reshape(ng * per, b, b)


def _tiles(v):
    v = v.reshape(-1)
    n = _round_up(v.shape[0], SUBLANES * LANES)
    return jnp.pad(v, (0, n - v.shape[0])).reshape(-1, LANES)


SMALL = ['attn_norm_g', 'b_f', 'conv_w', 'conv_b', 'w_gate_a', 'b_gate_a', 'w_gate_x', 'b_gate_x',
         'lru_L', 'attn_out_g', 'rec_out_g', 'mlp_norm_g', 'final_g', 'meta']


def _pack(d):
    return jnp.concatenate([_tiles(d[n]) for n in SMALL], axis=0)


def _unpack(vec, shapes):
    out, r = {}, 0
    for n in SMALL:
        size = math.prod(shapes[n])
        nr = _round_up(size, SUBLANES * LANES) // LANES
        out[n] = vec[r:r + nr].reshape(-1)[:size].reshape(shapes[n])
        r += nr
    return out


def _row_tile(tp):
    return tp // 4 if (tp // 4) % 16 == 0 else tp


def local_step(x, tgt, meta, small, hooks):
    s, d = x.shape
    t_real = s + N_META
    tp = _round_up(t_real, ATT_BLOCK)
    depth = small['attn_norm_g'].shape[0]
    nh = small['b_f'].shape[1]
    rw = small['conv_b'].shape[1]
    blk = small['w_gate_a'].shape[2]
    tm = _row_tile(tp)
    tm2 = tp // 2
    fcol = 2 * rw // LANES

    h = jnp.concatenate([meta, x, jnp.zeros((tp - t_real, d), F32)], axis=0)
    tgt_p = jnp.pad(tgt, ((N_META, tp - t_real), (0, 0)))
    row = lambda v: v.reshape(1, -1)
    bf_pad = jnp.pad(small['b_f'], ((0, 0), (0, LANES - nh)))

    saved = []
    for l in range(depth):
        w_in_t, wrest_t, wout, tok_w = hooks.mixer_weights(l, h)
        wga = _block_diag_pairs(small['w_gate_a'][l])
        wgx = _block_diag_pairs(small['w_gate_x'][l])
        z, qkv, rest = in_proj(h, row(small['attn_norm_g'][l]) + tok_w, w_in_t, wrest_t, 3 * nh * HEAD_DIM, tm)
        c, ct = fgate_fwd(rest, bf_pad[l:l + 1], fcol)
        o, lset = attn_fwd(qkv, c, ct, nh)
        rec, hr, xc, *gates = rec_fwd(rest, small['conv_w'][l], row(small['conv_b'][l]), wga,
                                      row(small['b_gate_a'][l]), wgx, row(small['b_gate_x'][l]),
                                      row(small['lru_L'][l]), rw)
        gup, gdown, tok_w = hooks.mlp_weights(l, rec)
        h2, mix, z2, u, h3 = out_mlp_fwd(h, o, rec, row(small['attn_out_g'][l]), row(small['rec_out_g'][l]), wout,
                                         row(small['mlp_norm_g'][l]) + tok_w, gup, gdown, tm)
        saved.append(dict(h=h, z=z, qkv=qkv, rest=rest, c=c, ct=ct, o=o, lset=lset, rec=rec, hr=hr, xc=xc,
                          h2=h2, mix=mix, z2=z2, u=u, wga=wga, wgx=wgx, gates=gates,
                          w_in_t=w_in_t, wrest_t=wrest_t, wout=wout, gup=gup, gdown=gdown))
        h = h3

    dh, dgf, loss = loss_head(h, row(small['final_g']), tgt_p, t_real, tm)

    gs = {n: [None] * depth for n in SMALL if n not in ('final_g', 'meta')}
    tok = jnp.zeros((), F32)
    for l in reversed(range(depth)):
        sv = saved[l]
        gup, gdown = sv['gup'], sv['gdown']
        tf = gup.shape[2]
        dup, dh2, dg2, dhb = mlp_bwd(dh, sv['u'], sv['h2'], row(small['mlp_norm_g'][l]) + tok, gup, gdown, tm)
        gs['mlp_norm_g'][l] = dg2[0]
        do, drec, dga, dgr, dw_out = out_proj_bwd(dh2, sv['mix'], sv['o'], sv['rec'], row(small['attn_out_g'][l]),
                                                  row(small['rec_out_g'][l]), sv['wout'], tm)
        gs['attn_out_g'][l] = dga[0]
        gs['rec_out_g'][l] = dgr[0]
        dw_down, dw_up = dw_mlp(sv['u'], dhb, sv['z2'], dup, tf)
        blocks = dict(w_down=dw_down, w_up=dw_up, w_out=dw_out.reshape(N_DEV, d // N_DEV, d))
        tok = hooks.grads_ready(l, 'mlp', blocks)
        dxr, dyr, dwga, dwgx, vec = rec_bwd(drec, sv['hr'], sv['xc'], sv['gates'], sv['rest'], small['conv_w'][l],
                                            sv['wga'], sv['wgx'], row(small['lru_L'][l]) + tok, rw)
        gs['w_gate_a'][l] = _block_diag_extract(dwga, blk)
        gs['w_gate_x'][l] = _block_diag_extract(dwgx, blk)
        vec = vec.transpose(1, 0, 2).reshape(SUBLANES, rw)
        gs['conv_w'][l] = vec[0:CONV_WIDTH]
        gs['conv_b'][l] = vec[4]
        gs['b_gate_a'][l] = vec[5]
        gs['b_gate_x'][l] = vec[6]
        gs['lru_L'][l] = vec[7]
        dq, dk, dv, drow, dcol = attn_bwd(sv['qkv'], do, sv['o'], sv['lset'], sv['c'], sv['ct'] + tok, nh)
        drow8 = drow[:, 0:2, :].reshape(nh, tp)
        if nh < SUBLANES:
            drow8 = jnp.pad(drow8, ((0, SUBLANES - nh), (0, 0)))
        dfl, dbf = fgate_bwd(drow8, dcol, sv['rest'], bf_pad[l:l + 1], fcol)
        gs['b_f'][l] = dbf[0, 0:nh]
        parts = (dq, dk, dv, dxr, dyr, dfl)
        dh, dg1 = in_proj_bwd(dh2, parts, sv['w_in_t'], sv['wrest_t'], sv['h'], row(small['attn_norm_g'][l]), tm)
        gs['attn_norm_g'][l] = dg1[0]
        first = ()
        if l == 0:
            grads = {n: jnp.stack(v) for n, v in gs.items()}
            grads['final_g'] = dgf[0]
            grads['meta'] = dh[0:N_META]
            first = hooks.small_ready(grads)
        dw_in = dw_in_t(sv['z'], parts, nh, tm2)
        dw_in = dw_in.reshape(N_DEV, dw_in.shape[0] // N_DEV, d)
        tok = hooks.grads_ready(l, 'in', dict(w_in=dw_in), first)

    return loss[0, 0], dh


def prep_weights(g_in, g_out, nh, rw):
    d = g_in.shape[2]
    w_in_t = g_in.reshape(-1, d)
    f0 = 3 * nh * HEAD_DIM
    wrest_t = jnp.concatenate([w_in_t[f0 + nh:f0 + nh + 2 * rw],
                               jnp.pad(w_in_t[f0:f0 + nh], ((0, LANES - nh), (0, 0)))], axis=0)
    return w_in_t, wrest_t, g_out.reshape(d, d)


BIG = ['w_in', 'w_out', 'w_up', 'w_down']
EXCHANGE_GROUPS = {'mlp': ['w_down', 'w_up', 'w_out'], 'in': ['w_in']}
WEIGHTS = ['meta', 'attn_norm_g', 'w_in', 'b_f', 'conv_w', 'conv_b', 'w_gate_a', 'b_gate_a', 'w_gate_x', 'b_gate_x',
           'lru_L', 'attn_out_g', 'rec_out_g', 'w_out', 'mlp_norm_g', 'w_up', 'w_down', 'final_g']


def _set_own(arr, own, me):
    return lax.dynamic_update_slice_in_dim(arr, own[None], me, axis=0)


class _Step:
    def __init__(self, w, nh, rw, me):
        self.w, self.nh, self.rw, self.me = w, nh, rw, me
        depth = w['w_in'].shape[0]
        first = [w['w_in_t'][:, 0, :].astype(BF16), w['w_out'][0].astype(BF16), w['meta'], w['conv_w']]
        self.pending, token = gather_start([first], "gather_start_0")
        zero = token[0, 0].astype(BF16)
        groups = [[w['w_up'][0].astype(BF16) + zero, w['w_down'][0].astype(BF16) + zero]]
        for l in range(1, depth):
            groups.append([w['w_in_t'][:, l, :].astype(BF16) + zero, w['w_out'][l].astype(BF16) + zero])
            groups.append([w['w_up'][l].astype(BF16) + zero, w['w_down'][l].astype(BF16) + zero])
        rest, _ = gather_start(groups, "gather_start_1")
        self.pending += rest
        self.first_after = rest[0][2][0]
        self.gathered = {}
        self.passing = {}
        self.token = jnp.zeros((), F32)
        self.lands = {n: lax.empty((N_DEV,) + w[n].shape, BF16) for n in BIG}
        din8, _, d = w['w_in_t'].shape
        self.lands['w_in'] = lax.empty((N_DEV, depth, din8, d), BF16)
        self.started = []
        self.small = None

    def _pass_on(self, gi, after):
        if gi < len(self.pending) and gi not in self.passing:
            send, recv, srcs, lands = self.pending[gi]
            srcs, lands = gather_wait(send, recv, srcs, lands, after, "gather_wait_%d" % gi)
            fsend, frecv, lands, token = forward_start(lands, "forward_start_%d" % gi)
            self.passing[gi] = (fsend, frecv, srcs, lands)
            self.token = token

    def group(self, gi, after):
        if gi not in self.gathered:
            self._pass_on(gi, after)
            fsend, frecv, srcs, lands = self.passing[gi]
            lands = forward_wait(fsend, frecv, lands, after, "forward_wait_%d" % gi)
            self.gathered[gi] = [_set_own(g, own, self.me) for g, own in zip(lands, srcs)]
            if gi >= 2:
                self._pass_on(gi + 1, lands[0])
        return self.gathered[gi]

    def mixer_weights(self, l, after):
        g = self.group(2 * l, after)
        return (*prep_weights(g[0], g[1], self.nh, self.rw), self.token)

    def mlp_weights(self, l, after):
        g = self.group(2 * l + 1, after)
        return g[0], g[1], self.token

    def grads_ready(self, l, group, blocks, after=()):
        names = EXCHANGE_GROUPS[group]
        send, recv, srcs, lands, token = exchange_start(
            [blocks[n] for n in names], [self.lands[n] for n in names], l, "exchange_start_%s_%d" % (group, l),
            after)
        for n, a in zip(names, lands):
            self.lands[n] = a
        self.started.append((l, group, send, recv, srcs))
        return token

    def small_ready(self, grads):
        self.small_shapes = {n: grads[n].shape for n in SMALL}
        packed = _pack(grads).astype(BF16)
        send, recv, srcs, lands, token = exchange_start(
            [packed], [lax.empty((N_DEV,) + packed.shape, BF16)], None, "small_start")
        self.small = (send, recv, srcs, lands)
        return srcs[0]

    def small_sum(self, after):
        send, recv, srcs, lands = self.small
        srcs, lands = exchange_wait(send, recv, srcs, lands, after, None, "small_wait")
        parts = _set_own(lands[0], srcs[0], self.me)
        return _unpack(sum_parts(parts, "sum_small_grads"), self.small_shapes)

    def received(self, group, after):
        names = EXCHANGE_GROUPS[group]
        own = {n: [None] * self.w[n].shape[0] for n in names}
        for l, grp, send, recv, srcs in self.started:
            if grp != group:
                continue
            srcs, lands = exchange_wait(send, recv, srcs, [self.lands[n] for n in names], after, l,
                                        "exchange_wait_%s_%d" % (group, l))
            for n, a, sr in zip(names, lands, srcs):
                self.lands[n] = a
                own[n][l] = sr
        return {n: (self.lands[n], own[n]) for n in names}


def kernel(x, meta, attn_norm_g, w_in, b_f, conv_w, conv_b, w_gate_a, b_gate_a, w_gate_x, b_gate_x, lru_L, attn_out_g, rec_out_g, w_out, mlp_norm_g, w_up, w_down, final_g, loss_target, m_meta, m_attn_norm_g, m_w_in, m_b_f, m_conv_w, m_conv_b, m_w_gate_a, m_b_gate_a, m_w_gate_x, m_b_gate_x, m_lru_L, m_attn_out_g, m_rec_out_g, m_w_out, m_mlp_norm_g, m_w_up, m_w_down, m_final_g, v_meta, v_attn_norm_g, v_w_in, v_b_f, v_conv_w, v_conv_b, v_w_gate_a, v_b_gate_a, v_w_gate_x, v_b_gate_x, v_lru_L, v_attn_out_g, v_rec_out_g, v_w_out, v_mlp_norm_g, v_w_up, v_w_down, v_final_g):
    w = dict(meta=meta, attn_norm_g=attn_norm_g, w_in=w_in, b_f=b_f, conv_w=conv_w, conv_b=conv_b,
             w_gate_a=w_gate_a, b_gate_a=b_gate_a, w_gate_x=w_gate_x, b_gate_x=b_gate_x, lru_L=lru_L,
             attn_out_g=attn_out_g, rec_out_g=rec_out_g, w_out=w_out, mlp_norm_g=mlp_norm_g, w_up=w_up,
             w_down=w_down, final_g=final_g)
    mo = dict(meta=m_meta, attn_norm_g=m_attn_norm_g, w_in=m_w_in, b_f=m_b_f, conv_w=m_conv_w, conv_b=m_conv_b,
              w_gate_a=m_w_gate_a, b_gate_a=m_b_gate_a, w_gate_x=m_w_gate_x, b_gate_x=m_b_gate_x, lru_L=m_lru_L,
              attn_out_g=m_attn_out_g, rec_out_g=m_rec_out_g, w_out=m_w_out, mlp_norm_g=m_mlp_norm_g,
              w_up=m_w_up, w_down=m_w_down, final_g=m_final_g)
    vo = dict(meta=v_meta, attn_norm_g=v_attn_norm_g, w_in=v_w_in, b_f=v_b_f, conv_w=v_conv_w, conv_b=v_conv_b,
              w_gate_a=v_w_gate_a, b_gate_a=v_b_gate_a, w_gate_x=v_w_gate_x, b_gate_x=v_b_gate_x, lru_L=v_lru_L,
              attn_out_g=v_attn_out_g, rec_out_g=v_rec_out_g, w_out=v_w_out, mlp_norm_g=v_mlp_norm_g,
              w_up=v_w_up, w_down=v_w_down, final_g=v_final_g)
    depth = w_in.shape[0]
    nh = b_f.shape[1]
    rw = conv_b.shape[1]
    me = 4 * lax.axis_index("x") + 2 * lax.axis_index("y") + lax.axis_index("c")

    w['w_in_t'] = jnp.transpose(w_in, (2, 0, 1))
    swap = lambda a: jnp.swapaxes(a, 1, 2)
    step = _Step(w, nh, rw, me)
    g0 = step.group(0, step.first_after)
    meta_full = g0[2].transpose(1, 0, 2).reshape(N_META, -1)
    conv_full = g0[3].transpose(1, 2, 0, 3).reshape(depth, CONV_WIDTH, rw)
    small = {n: w[n] for n in SMALL}
    small['conv_w'] = conv_full

    loss_part, dh0 = local_step(x[0], loss_target[0], meta_full, small, step)
    loss = lax.psum(loss_part, ("x", "y", "c"))
    grad_x = dh0[N_META:N_META + x.shape[1]][None]

    out_g, out_d, out_m, out_v = {}, {}, {}, {}
    me1 = me.reshape(1).astype(jnp.int32)

    def update_big(group, after):
        for n, (r, owns) in step.received(group, after).items():
            if n == 'w_in':
                out = sum_adamw_t(r, owns, me1, swap(w[n]), swap(mo[n]), swap(vo[n]), "adamw_w_in")
                out = [swap(a) for a in out]
            else:
                shp = w[n].shape
                rows, cols = shp[0] * shp[1], shp[2]
                tr = min(512 if cols <= 512 else 256, shp[1])
                out = sum_adamw(r.reshape(N_DEV, rows, cols), owns, me1, w[n].reshape(rows, cols),
                                mo[n].reshape(rows, cols), vo[n].reshape(rows, cols), tr, "adamw_" + n)
                out = [a.reshape(shp) for a in out]
            out_g[n], out_d[n], out_m[n], out_v[n] = out
            after = out[0]
        return after

    update_big('mlp', step.started[-1][4][0])

    gsum = step.small_sum([out_g[n] for n in EXCHANGE_GROUPS['mlp']])
    gsum['meta'] = lax.dynamic_slice_in_dim(gsum['meta'], me * meta.shape[1], meta.shape[1], axis=1)
    gsum['conv_w'] = lax.dynamic_slice_in_dim(gsum['conv_w'], me * conv_w.shape[2], conv_w.shape[2], axis=2)
    as2d = lambda a: a.reshape(-1, a.shape[-1])
    deltas, new_m, new_v = adamw_group([as2d(gsum[n]) for n in SMALL], [as2d(w[n]) for n in SMALL],
                                       [as2d(mo[n]) for n in SMALL], [as2d(vo[n]) for n in SMALL], "adamw_small")
    for i, n in enumerate(SMALL):
        out_g[n] = gsum[n]
        out_d[n], out_m[n], out_v[n] = [a[i].reshape(w[n].shape) for a in (deltas, new_m, new_v)]

    update_big('in', deltas[0])

    return (loss, grad_x, *[out_g[n] for n in WEIGHTS], *[out_d[n] for n in WEIGHTS],
            *[out_m[n] for n in WEIGHTS], *[out_v[n] for n in WEIGHTS])
```

```python
import math

import jax
import jax.numpy as jnp
from jax import lax
from jax.experimental import pallas as pl
from jax.experimental.pallas import tpu as pltpu

F32 = jnp.float32
BF16 = jnp.bfloat16

N_DEV = 8
N_META = 16
HEAD_DIM = 64
CONV_WIDTH = 4
RG_C = 8.0
NORM_EPS = 1e-6
LANES = 128
SUBLANES = 8
ATT_BLOCK = 128
ATT_TQ = 512
NEG_BIG = -1e30
ATT_SCALE = 1.0 / math.sqrt(HEAD_DIM)

ADAM_LR = 0.001
ADAM_B1 = 0.9
ADAM_B2 = 0.999
ADAM_EPS = 1e-08
ADAM_WD = 0.01
ADAM_STEP = 10

VMEM_LIMIT_BYTES = 56 * 1024 * 1024
MESH = pl.DeviceIdType.MESH
ANY = pl.BlockSpec(memory_space=pl.ANY)


def _cparams(*sem):
    return pltpu.CompilerParams(dimension_semantics=sem if sem else None,
                                vmem_limit_bytes=VMEM_LIMIT_BYTES)


def _dot(a, b):
    return jnp.dot(a, b, preferred_element_type=F32)


def _dot_nt(a, b):
    return lax.dot_general(a, b, (((1,), (1,)), ((), ())), preferred_element_type=F32)


def _dot_tn(a, b):
    return lax.dot_general(a, b, (((0,), (0,)), ((), ())), preferred_element_type=F32)


def _sigmoid(x):
    return 0.5 * (1.0 + jnp.tanh(0.5 * x))


def _log_sigmoid(x):
    return jnp.minimum(x, 0.0) - jnp.log(1.0 + jnp.exp(-jnp.abs(x)))


def _expm1(x):
    series = x * (1.0 + x * (0.5 + x * (1.0 / 6.0 + x * (1.0 / 24.0))))
    return jnp.where(jnp.abs(x) < 1e-2, series, jnp.exp(x) - 1.0)


_GELU_K = math.sqrt(2.0 / math.pi)
_GELU_C = 0.044715


def _gelu(x):
    t = jnp.tanh(_GELU_K * (x + _GELU_C * x * x * x))
    return 0.5 * x * (1.0 + t)


def _gelu_and_grad(x):
    x2 = x * x
    t = jnp.tanh(_GELU_K * (x + _GELU_C * x2 * x))
    half = 0.5 * (1.0 + t)
    return x * half, half + 0.5 * x * (1.0 - t * t) * _GELU_K * (1.0 + 3.0 * _GELU_C * x2)


def _split3_dot(tri, x):
    hi = x.astype(BF16)
    r1 = x - hi.astype(F32)
    mid = r1.astype(BF16)
    lo = (r1 - mid.astype(F32)).astype(BF16)
    return _dot(tri, hi) + _dot(tri, mid) + _dot(tri, lo)


def _dot_split3(x, sel):
    hi = x.astype(BF16)
    r1 = x - hi.astype(F32)
    mid = r1.astype(BF16)
    lo = (r1 - mid.astype(F32)).astype(BF16)
    return _dot(hi, sel) + _dot(mid, sel) + _dot(lo, sel)


def _rms_fwd(x, g):
    r = lax.rsqrt(jnp.mean(x * x, axis=-1, keepdims=True) + NORM_EPS)
    return x * r * g


def _rms_bwd(x, g, dy):
    r = lax.rsqrt(jnp.mean(x * x, axis=-1, keepdims=True) + NORM_EPS)
    xn = x * r
    dxn = dy * g
    dx = r * (dxn - xn * jnp.mean(dxn * xn, axis=-1, keepdims=True))
    return dx, jnp.sum(dy * xn, axis=0, keepdims=True)


def _accumulate(ref, val, first):
    @pl.when(first)
    def _():
        ref[...] = val

    @pl.when(jnp.logical_not(first))
    def _():
        ref[...] += val


def in_proj(h, g1, w_in_t, wrest_t, nq, tm):
    tp, d = h.shape
    nr = wrest_t.shape[0]

    def body(h_ref, g_ref, wq_ref, wr_ref, z_ref, qkv_ref, rest_ref):
        z = _rms_fwd(h_ref[...], g_ref[...]).astype(BF16)
        z_ref[...] = z
        qkv_ref[...] = _dot_nt(z, wq_ref[...]).astype(BF16)
        rest_ref[...] = _dot_nt(z, wr_ref[...])

    return pl.pallas_call(
        body, name="in_proj", grid=(tp // tm,),
        in_specs=[pl.BlockSpec((tm, d), lambda i: (i, 0)),
                  pl.BlockSpec((1, d), lambda i: (0, 0)),
                  pl.BlockSpec((nq, d), lambda i: (0, 0)),
                  pl.BlockSpec((nr, d), lambda i: (0, 0))],
        out_specs=[pl.BlockSpec((tm, d), lambda i: (i, 0)),
                   pl.BlockSpec((tm, nq), lambda i: (i, 0)),
                   pl.BlockSpec((tm, nr), lambda i: (i, 0))],
        out_shape=[jax.ShapeDtypeStruct((tp, d), BF16),
                   jax.ShapeDtypeStruct((tp, nq), BF16),
                   jax.ShapeDtypeStruct((tp, nr), F32)],
        compiler_params=_cparams("parallel"),
    )(h, g1, w_in_t, wrest_t)


def fgate_fwd(rest, bf_pad, fcol):
    tp = rest.shape[0]
    nb = tp // ATT_BLOCK

    def body(f_ref, b_ref, c_ref, ct_ref):
        r_i = lax.broadcasted_iota(jnp.int32, (ATT_BLOCK, ATT_BLOCK), 0)
        c_i = lax.broadcasted_iota(jnp.int32, (ATT_BLOCK, ATT_BLOCK), 1)
        tri = (r_i >= c_i).astype(BF16)
        carry = jnp.zeros((1, LANES), F32)
        for i in range(nb):
            sl = slice(i * ATT_BLOCK, (i + 1) * ATT_BLOCK)
            lf = _log_sigmoid(f_ref[sl, :] + b_ref[...])
            cs = _split3_dot(tri, lf) + carry
            carry = cs[ATT_BLOCK - 1:ATT_BLOCK, :]
            c_ref[sl, :] = cs
            ct_ref[:, sl] = cs.T[0:SUBLANES, :]

    return pl.pallas_call(
        body, name="fgate_fwd", grid=(1,),
        in_specs=[pl.BlockSpec((tp, LANES), lambda i: (0, fcol)),
                  pl.BlockSpec((1, LANES), lambda i: (0, 0))],
        out_specs=[pl.BlockSpec((tp, LANES), lambda i: (0, 0)),
                   pl.BlockSpec((SUBLANES, tp), lambda i: (0, 0))],
        out_shape=[jax.ShapeDtypeStruct((tp, LANES), F32),
                   jax.ShapeDtypeStruct((SUBLANES, tp), F32)],
        compiler_params=_cparams("arbitrary"),
    )(rest, bf_pad)


def _pick_col(blk, head):
    lane = lax.broadcasted_iota(jnp.int32, blk.shape, 1)
    return jnp.sum(jnp.where(lane == head, blk, 0.0), axis=1, keepdims=True)


def _pick_row(blk, head):
    sub = lax.broadcasted_iota(jnp.int32, blk.shape, 0)
    return jnp.sum(jnp.where(sub == head, blk, 0.0), axis=0, keepdims=True)


def _att_tiles(tp):
    out, r0 = [], 0
    while r0 < tp:
        rows = min(ATT_TQ, tp - r0)
        out.append((r0, rows, r0 + rows))
        r0 += rows
    return out


def attn_fwd(qkv, c, ct, nh):
    tp = qkv.shape[0]
    npair = nh // 2
    tiles = _att_tiles(tp)

    def body(q_ref, k_ref, v_ref, c_ref, ct_ref, o_ref, lset_ref):
        p = pl.program_id(0)
        lset_ref[...] = jnp.zeros_like(lset_ref)
        for r0, nr, nk in tiles:
            rs = slice(r0, r0 + nr)
            causal = (r0 + lax.broadcasted_iota(jnp.int32, (nr, nk), 0)
                      >= lax.broadcasted_iota(jnp.int32, (nr, nk), 1))
            cblk = c_ref[rs, :]
            ctb = ct_ref[:, 0:nk]
            for hh in range(2):
                head = 2 * p + hh
                hs = slice(hh * HEAD_DIM, (hh + 1) * HEAD_DIM)
                q = q_ref[rs, hs] * ATT_SCALE
                s = _dot_nt(q, k_ref[0:nk, hs]) + (_pick_col(cblk, head) - _pick_row(ctb, head))
                s = jnp.where(causal, s, NEG_BIG)
                m = jnp.max(s, axis=1, keepdims=True)
                pm = jnp.exp(s - m)
                l = jnp.sum(pm, axis=1, keepdims=True)
                o_ref[rs, hs] = _dot(pm.astype(BF16), v_ref[0:nk, hs]) / l
                lse = m + jnp.log(l)
                lset_ref[hh:hh + 1, rs] = jnp.broadcast_to(lse, (nr, LANES)).T[0:1, :]

    pair = lambda p: (0, p)
    return pl.pallas_call(
        body, name="attn_fwd", grid=(npair,),
        in_specs=[pl.BlockSpec((tp, LANES), pair),
                  pl.BlockSpec((tp, LANES), lambda p: (0, npair + p)),
                  pl.BlockSpec((tp, LANES), lambda p: (0, 2 * npair + p)),
                  pl.BlockSpec((tp, LANES), lambda p: (0, 0)),
                  pl.BlockSpec((SUBLANES, tp), lambda p: (0, 0))],
        out_specs=[pl.BlockSpec((tp, LANES), pair),
                   pl.BlockSpec((None, SUBLANES, tp), lambda p: (p, 0, 0))],
        out_shape=[jax.ShapeDtypeStruct((tp, nh * HEAD_DIM), F32),
                   jax.ShapeDtypeStruct((npair, SUBLANES, tp), F32)],
        compiler_params=_cparams("parallel"),
    )(qkv, qkv, qkv, c, ct)


def _shift_down(x, k, n):
    if k == 0:
        return x
    rows = lax.broadcasted_iota(jnp.int32, x.shape, 0)
    return jnp.where(rows >= k, pltpu.roll(x, k, 0), 0.0)


def _shift_up(x, k, n):
    if k == 0:
        return x
    rows = lax.broadcasted_iota(jnp.int32, x.shape, 0)
    return jnp.where(rows < n - k, pltpu.roll(x, n - k, 0), 0.0)


def _conv_fwd(xr, cw_ref, cb_ref, n):
    xc = cw_ref[CONV_WIDTH - 1:CONV_WIDTH, :] * xr + cb_ref[...]
    for k in range(1, CONV_WIDTH):
        xc = xc + cw_ref[CONV_WIDTH - 1 - k:CONV_WIDTH - k, :] * _shift_down(xr, k, n)
    return xc


def _gates(xc, wga_ref, bga_ref, wgx_ref, bgx_ref, l_ref):
    xcb = xc.astype(BF16)
    r = _sigmoid(_dot(xcb, wga_ref[...]) + bga_ref[...])
    ig = _sigmoid(_dot(xcb, wgx_ref[...]) + bgx_ref[...])
    ls = _log_sigmoid(l_ref[...])
    log_a = RG_C * r * ls
    a = jnp.exp(log_a)
    mult = jnp.sqrt(-_expm1(2.0 * log_a))
    return xcb, r, ig, ls, log_a, a, mult


SCAN_UNROLL = 4


def _scan_rows(a_s, u_s, out_ref, n, reverse):
    nt = n // SUBLANES
    per = SCAN_UNROLL if nt % SCAN_UNROLL == 0 else 1
    row = lax.broadcasted_iota(jnp.int32, (SUBLANES, LANES), 0)
    last = 0 if reverse else SUBLANES - 1

    def tile_scan(a, u):
        for d in (1, 2, 4):
            if reverse:
                keep = row < SUBLANES - d
                sh = SUBLANES - d
            else:
                keep = row >= d
                sh = d
            a_sh = jnp.where(keep, pltpu.roll(a, sh, 0), 1.0)
            u_sh = jnp.where(keep, pltpu.roll(u, sh, 0), 0.0)
            u = a * u_sh + u
            a = a * a_sh
        return a, u

    def step(t, carry):
        tiles = []
        for k in range(per):
            tt = t * per + k
            if reverse:
                tt = nt - 1 - tt
            off = pl.multiple_of(tt * SUBLANES, SUBLANES)
            a, u = tile_scan(a_s[pl.ds(off, SUBLANES), :], u_s[pl.ds(off, SUBLANES), :])
            tiles.append((off, a, u))
        for off, a, u in tiles:
            out_ref[pl.ds(off, SUBLANES), :] = u + a * carry
            carry = u[last:last + 1, :] + a[last:last + 1, :] * carry
        return carry

    lax.fori_loop(0, nt // per, step, jnp.zeros((1, LANES), F32))


def rec_fwd(rest, convw, convb, wga, bga, wgx, bgx, lru, rw):
    tp = rest.shape[0]
    ng = rw // LANES

    def body(xr_ref, yr_ref, cw_ref, cb_ref, wga_ref, bga_ref, wgx_ref, bgx_ref, l_ref,
             rec_ref, hr_ref, xc_ref, r_ref, ig_ref, a_ref, mult_ref, u_s):
        xc = _conv_fwd(xr_ref[...], cw_ref, cb_ref, tp)
        xc_ref[...] = xc
        _, r, ig, ls, log_a, a, mult = _gates(xc, wga_ref, bga_ref, wgx_ref, bgx_ref, l_ref)
        r_ref[...] = r
        ig_ref[...] = ig
        a_ref[...] = a
        mult_ref[...] = mult
        u_s[...] = mult * ig * xc
        _scan_rows(a_ref, u_s, hr_ref, tp, reverse=False)
        rec_ref[...] = hr_ref[...] * _gelu(yr_ref[...])

    col = lambda g: (0, g)
    vec = pl.BlockSpec((1, LANES), col)
    big = pl.BlockSpec((tp, LANES), col)
    return pl.pallas_call(
        body, name="rec_fwd", grid=(ng,),
        in_specs=[big, pl.BlockSpec((tp, LANES), lambda g: (0, ng + g)),
                  pl.BlockSpec((CONV_WIDTH, LANES), col), vec,
                  pl.BlockSpec((None, LANES, LANES), lambda g: (g, 0, 0)), vec,
                  pl.BlockSpec((None, LANES, LANES), lambda g: (g, 0, 0)), vec, vec],
        out_specs=[big] * 7,
        out_shape=[jax.ShapeDtypeStruct((tp, rw), F32)] * 7,
        scratch_shapes=[pltpu.VMEM((tp, LANES), F32)],
        compiler_params=_cparams("parallel"),
    )(rest, rest, convw, convb, wga, bga, wgx, bgx, lru)


def out_mlp_fwd(h, o, rec, ga, gr, wout, g2, gup, gdown, tm):
    tp, d = h.shape
    aw, rw = o.shape[1], rec.shape[1]
    nf = gup.shape[0]
    tf = gup.shape[2]
    nb = MLP_BLOCKS if nf % MLP_BLOCKS == 0 else 1
    nj = nf // nb

    def body(h_ref, o_ref, rec_ref, ga_ref, gr_ref, w_ref, g2_ref, wu_ref, wd_ref,
             h2_ref, mix_ref, z2_ref, u_ref, h3_ref, acc):
        j = pl.program_id(1)

        @pl.when(j == 0)
        def _():
            mix_ref[:, 0:aw] = _rms_fwd(o_ref[...], ga_ref[...]).astype(BF16)
            mix_ref[:, aw:aw + rw] = _rms_fwd(rec_ref[...], gr_ref[...]).astype(BF16)
            h2 = h_ref[...] + _dot(mix_ref[...], w_ref[...])
            h2_ref[...] = h2
            acc[...] = h2
            z2_ref[...] = _rms_fwd(h2, g2_ref[...]).astype(BF16)

        z = z2_ref[...]
        part = None
        for b in range(nb):
            u = jnp.maximum(_dot(z, wu_ref[b]), 0.0)
            u_ref[:, b * tf:(b + 1) * tf] = u.astype(BF16)
            p = _dot((u * u).astype(BF16), wd_ref[b])
            part = p if part is None else part + p
        acc[...] += part

        @pl.when(j == nj - 1)
        def _():
            h3_ref[...] = acc[...]

    row = lambda i, j: (i, 0)
    fix = lambda i, j: (0, 0)
    return pl.pallas_call(
        body, name="out_mlp_fwd", grid=(tp // tm, nj),
        in_specs=[pl.BlockSpec((tm, d), row), pl.BlockSpec((tm, aw), row), pl.BlockSpec((tm, rw), row),
                  pl.BlockSpec((1, aw), fix), pl.BlockSpec((1, rw), fix),
                  pl.BlockSpec((d, d), fix), pl.BlockSpec((1, d), fix),
                  pl.BlockSpec((nb, d, tf), lambda i, j: (j, 0, 0)),
                  pl.BlockSpec((nb, tf, d), lambda i, j: (j, 0, 0))],
        out_specs=[pl.BlockSpec((tm, d), row), pl.BlockSpec((tm, d), row), pl.BlockSpec((tm, d), row),
                   pl.BlockSpec((tm, nb * tf), lambda i, j: (i, j)), pl.BlockSpec((tm, d), row)],
        out_shape=[jax.ShapeDtypeStruct((tp, d), F32), jax.ShapeDtypeStruct((tp, d), BF16),
                   jax.ShapeDtypeStruct((tp, d), BF16), jax.ShapeDtypeStruct((tp, nf * tf), BF16),
                   jax.ShapeDtypeStruct((tp, d), F32)],
        scratch_shapes=[pltpu.VMEM((tm, d), F32)],
        compiler_params=_cparams("parallel", "arbitrary"),
    )(h, o, rec, ga, gr, wout, g2, gup, gdown)


MLP_BLOCKS = 4


def loss_head(h, gf, tgt, t_real, tm):
    tp, d = h.shape

    def body(h_ref, g_ref, t_ref, dh_ref, dg_ref, loss_ref):
        i = pl.program_id(0)
        x = h_ref[...]
        g = g_ref[...]
        r = lax.rsqrt(jnp.mean(x * x, axis=-1, keepdims=True) + NORM_EPS)
        xn = x * r
        rows = i * tm + lax.broadcasted_iota(jnp.int32, (tm, 1), 0)
        valid = jnp.logical_and(rows >= N_META, rows < t_real)
        e = jnp.where(valid, xn * g - t_ref[...], 0.0)
        part = 0.5 * jnp.sum(jnp.sum(e * e, axis=1, keepdims=True) / d, axis=0, keepdims=True)
        dy = e / d
        dxn = dy * g
        dh_ref[...] = r * (dxn - xn * jnp.mean(dxn * xn, axis=-1, keepdims=True))
        _accumulate(dg_ref, jnp.sum(dy * xn, axis=0, keepdims=True), i == 0)
        _accumulate(loss_ref, jnp.broadcast_to(part, (1, LANES)), i == 0)

    row = lambda i: (i, 0)
    fix = lambda i: (0, 0)
    return pl.pallas_call(
        body, name="loss_head", grid=(tp // tm,),
        in_specs=[pl.BlockSpec((tm, d), row), pl.BlockSpec((1, d), fix), pl.BlockSpec((tm, d), row)],
        out_specs=[pl.BlockSpec((tm, d), row), pl.BlockSpec((1, d), fix), pl.BlockSpec((1, LANES), fix)],
        out_shape=[jax.ShapeDtypeStruct((tp, d), F32), jax.ShapeDtypeStruct((1, d), F32),
                   jax.ShapeDtypeStruct((1, LANES), F32)],
        compiler_params=_cparams("arbitrary"),
    )(h, gf, tgt)


def mlp_bwd(dh, u, h2, g2, gup, gdown, tm):
    tp, d = dh.shape
    nf = gup.shape[0]
    tf = gup.shape[2]
    nb = MLP_BLOCKS if nf % MLP_BLOCKS == 0 else 1
    nj = nf // nb
    ni = tp // tm

    def body(dh_ref, u_ref, h2_ref, g_ref, wu_ref, wd_ref, dup_ref, dh2_ref, dg_ref, dhb, acc):
        i = pl.program_id(0)
        j = pl.program_id(1)

        @pl.when(j == 0)
        def _():
            dhb[...] = dh_ref[...].astype(BF16)

        part = None
        for b in range(nb):
            cols = slice(b * tf, (b + 1) * tf)
            dup = (_dot_nt(dhb[...], wd_ref[b]) * (2.0 * u_ref[:, cols].astype(F32))).astype(BF16)
            dup_ref[:, cols] = dup
            p = _dot_nt(dup, wu_ref[b])
            part = p if part is None else part + p
        _accumulate(acc, part, j == 0)

        @pl.when(j == nj - 1)
        def _():
            dx, dg = _rms_bwd(h2_ref[...], g_ref[...], acc[...])
            dh2_ref[...] = dh_ref[...] + dx
            _accumulate(dg_ref, dg, i == 0)

    return pl.pallas_call(
        body, name="mlp_bwd", grid=(ni, nj),
        in_specs=[pl.BlockSpec((tm, d), lambda i, j: (i, 0)),
                  pl.BlockSpec((tm, nb * tf), lambda i, j: (i, j)),
                  pl.BlockSpec((tm, d), lambda i, j: (i, 0)),
                  pl.BlockSpec((1, d), lambda i, j: (0, 0)),
                  pl.BlockSpec((nb, d, tf), lambda i, j: (j, 0, 0)),
                  pl.BlockSpec((nb, tf, d), lambda i, j: (j, 0, 0))],
        out_specs=[pl.BlockSpec((tm, nb * tf), lambda i, j: (i, j)),
                   pl.BlockSpec((tm, d), lambda i, j: (i, 0)),
                   pl.BlockSpec((1, d), lambda i, j: (0, 0)),
                   pl.BlockSpec((tm, d), lambda i, j: (i, 0))],
        out_shape=[jax.ShapeDtypeStruct((tp, nf * tf), BF16), jax.ShapeDtypeStruct((tp, d), F32),
                   jax.ShapeDtypeStruct((1, d), F32), jax.ShapeDtypeStruct((tp, d), BF16)],
        scratch_shapes=[pltpu.VMEM((tm, d), F32)],
        compiler_params=_cparams("arbitrary", "arbitrary"),
    )(dh, u, h2, g2, gup, gdown)


def dw_mlp(u, dhb, z2, dup, tf):
    rows, dff = u.shape
    d = z2.shape[1]
    nf = dff // tf

    def body(u_ref, dh_ref, z_ref, dup_ref, dwd_ref, dwu_ref, zt):
        @pl.when(pl.program_id(0) == 0)
        def _():
            zt[...] = z_ref[...].T

        uf = u_ref[...].astype(F32)
        dwd_ref[...] = _dot_tn((uf * uf).astype(BF16), dh_ref[...]).astype(BF16)
        dwu_ref[...] = _dot(zt[...], dup_ref[...]).astype(BF16)

    col = lambda j: (0, j)
    fix = lambda j: (0, 0)
    return pl.pallas_call(
        body, name="dw_mlp", grid=(nf,),
        in_specs=[pl.BlockSpec((rows, tf), col), pl.BlockSpec((rows, d), fix),
                  pl.BlockSpec((rows, d), fix), pl.BlockSpec((rows, tf), col)],
        out_specs=[pl.BlockSpec((None, tf, d), lambda j: (j, 0, 0)),
                   pl.BlockSpec((None, d, tf), lambda j: (j, 0, 0))],
        out_shape=[jax.ShapeDtypeStruct((nf, tf, d), BF16), jax.ShapeDtypeStruct((nf, d, tf), BF16)],
        scratch_shapes=[pltpu.VMEM((d, rows), BF16)],
        compiler_params=_cparams("arbitrary"),
    )(u, dhb, z2, dup)


def out_proj_bwd(dh2, mix, o, rec, ga, gr, wout, tm):
    tp, d = dh2.shape
    aw, rw = o.shape[1], rec.shape[1]
    ni = tp // tm

    def body(dh_ref, mix_ref, o_ref, rec_ref, ga_ref, gr_ref, w_ref, do_ref, drec_ref, dga_ref, dgr_ref, dw_ref, acc):
        i = pl.program_id(0)
        dhb = dh_ref[...].astype(BF16)
        dmix = _dot_nt(dhb, w_ref[...])
        do, dga = _rms_bwd(o_ref[...], ga_ref[...], dmix[:, 0:aw])
        drec, dgr = _rms_bwd(rec_ref[...], gr_ref[...], dmix[:, aw:aw + rw])
        do_ref[...] = do
        drec_ref[...] = drec
        _accumulate(dga_ref, dga, i == 0)
        _accumulate(dgr_ref, dgr, i == 0)
        _accumulate(acc, _dot_tn(mix_ref[...], dhb), i == 0)

        @pl.when(i == ni - 1)
        def _():
            dw_ref[...] = acc[...].astype(BF16)

    row = lambda i: (i, 0)
    fix = lambda i: (0, 0)
    return pl.pallas_call(
        body, name="out_proj_bwd", grid=(ni,),
        in_specs=[pl.BlockSpec((tm, d), row), pl.BlockSpec((tm, d), row),
                  pl.BlockSpec((tm, aw), row), pl.BlockSpec((tm, rw), row),
                  pl.BlockSpec((1, aw), fix), pl.BlockSpec((1, rw), fix), pl.BlockSpec((d, d), fix)],
        out_specs=[pl.BlockSpec((tm, aw), row), pl.BlockSpec((tm, rw), row),
                   pl.BlockSpec((1, aw), fix), pl.BlockSpec((1, rw), fix), pl.BlockSpec((d, d), fix)],
        out_shape=[jax.ShapeDtypeStruct((tp, aw), F32), jax.ShapeDtypeStruct((tp, rw), F32),
                   jax.ShapeDtypeStruct((1, aw), F32), jax.ShapeDtypeStruct((1, rw), F32),
                   jax.ShapeDtypeStruct((d, d), BF16)],
        scratch_shapes=[pltpu.VMEM((d, d), F32)],
        compiler_params=_cparams("arbitrary"),
    )(dh2, mix, o, rec, ga, gr, wout)


def rec_bwd(drec, hr, xc, gates, rest, convw, wga, wgx, lru, rw):
    tp = rest.shape[0]
    ng = rw // LANES

    def body(drec_ref, hr_ref, xc_ref, r_ref, ig_ref, a_ref, mult_ref, xr_ref, yr_ref, cw_ref, wga_ref, wgx_ref,
             l_ref, dxr_ref, dyr_ref, dwga_ref, dwgx_ref, vec_ref, a_s, u_s, lam_s):
        xc = xc_ref[...]
        h = hr_ref[...]
        drec = drec_ref[...]
        r, ig, a, mult = r_ref[...], ig_ref[...], a_ref[...], mult_ref[...]
        xcb = xc.astype(BF16)
        ls = _log_sigmoid(l_ref[...])
        gelu, gelu_grad = _gelu_and_grad(yr_ref[...])
        dyr_ref[...] = (drec * h * gelu_grad).astype(BF16)
        a_s[...] = _shift_up(a, 1, tp)
        u_s[...] = drec * gelu
        _scan_rows(a_s, u_s, lam_s, tp, reverse=True)
        lam = lam_s[...]
        da = lam * _shift_down(h, 1, tp)
        dmult = lam * ig * xc
        dig = lam * mult * xc
        dxc = lam * mult * ig
        dlog_a = da * a - dmult * (a * a) / mult
        dr = dlog_a * (RG_C * ls)
        dl = jnp.sum(dlog_a * (RG_C * r), axis=0, keepdims=True) * _sigmoid(-l_ref[...])
        dpa = dr * r * (1.0 - r)
        dpx = dig * ig * (1.0 - ig)
        dpab = dpa.astype(BF16)
        dpxb = dpx.astype(BF16)
        dxc = dxc + _dot_nt(dpab, wga_ref[...]) + _dot_nt(dpxb, wgx_ref[...])
        dwga_ref[...] = _dot_tn(xcb, dpab)
        dwgx_ref[...] = _dot_tn(xcb, dpxb)
        xr = xr_ref[...]
        dxr = cw_ref[CONV_WIDTH - 1:CONV_WIDTH, :] * dxc
        for k in range(1, CONV_WIDTH):
            dxr = dxr + cw_ref[CONV_WIDTH - 1 - k:CONV_WIDTH - k, :] * _shift_up(dxc, k, tp)
        dxr_ref[...] = dxr.astype(BF16)
        for k in range(CONV_WIDTH):
            vec_ref[k:k + 1, :] = jnp.sum(dxc * _shift_down(xr, CONV_WIDTH - 1 - k, tp), axis=0, keepdims=True)
        vec_ref[4:5, :] = jnp.sum(dxc, axis=0, keepdims=True)
        vec_ref[5:6, :] = jnp.sum(dpa, axis=0, keepdims=True)
        vec_ref[6:7, :] = jnp.sum(dpx, axis=0, keepdims=True)
        vec_ref[7:8, :] = dl

    col = lambda g: (0, g)
    vec = pl.BlockSpec((1, LANES), col)
    big = pl.BlockSpec((tp, LANES), col)
    sq = pl.BlockSpec((None, LANES, LANES), lambda g: (g, 0, 0))
    return pl.pallas_call(
        body, name="rec_bwd", grid=(ng,),
        in_specs=[big] * 8 + [pl.BlockSpec((tp, LANES), lambda g: (0, ng + g)),
                                pl.BlockSpec((CONV_WIDTH, LANES), col), sq, sq, vec],
        out_specs=[big, big, sq, sq, pl.BlockSpec((None, SUBLANES, LANES), lambda g: (g, 0, 0))],
        out_shape=[jax.ShapeDtypeStruct((tp, rw), BF16), jax.ShapeDtypeStruct((tp, rw), BF16),
                   jax.ShapeDtypeStruct((ng, LANES, LANES), F32), jax.ShapeDtypeStruct((ng, LANES, LANES), F32),
                   jax.ShapeDtypeStruct((ng, SUBLANES, LANES), F32)],
        scratch_shapes=[pltpu.VMEM((tp, LANES), F32)] * 3,
        compiler_params=_cparams("parallel"),
    )(drec, hr, xc, *gates, rest, rest, convw, wga, wgx, lru)


def attn_bwd(qkv, do, o, lset, c, ct, nh):
    tp = qkv.shape[0]
    npair = nh // 2
    aw = nh * HEAD_DIM
    tiles = _att_tiles(tp)

    def body(q_ref, k_ref, v_ref, do_ref, o_ref, lset_ref, c_ref, ct_ref,
             dq_ref, dk_ref, dv_ref, drow_ref, dcol_ref, dk_acc, dv_acc, dq_t):
        p = pl.program_id(0)
        k_t = k_ref[...].T
        dk_acc[...] = jnp.zeros_like(dk_acc)
        dv_acc[...] = jnp.zeros_like(dv_acc)
        dcol_ref[...] = jnp.zeros_like(dcol_ref)
        drow_ref[...] = jnp.zeros_like(drow_ref)
        for r0, nr, nk in tiles:
            rs = slice(r0, r0 + nr)
            causal = (r0 + lax.broadcasted_iota(jnp.int32, (nk, nr), 1)
                      >= lax.broadcasted_iota(jnp.int32, (nk, nr), 0))
            cblk = c_ref[0:nk, :]
            ctb = ct_ref[:, rs]
            for hh in range(2):
                head = 2 * p + hh
                hs = slice(hh * HEAD_DIM, (hh + 1) * HEAD_DIM)
                q = q_ref[rs, hs]
                k = k_ref[0:nk, hs]
                dof = do_ref[rs, hs]
                do16 = dof.astype(BF16)
                delta = jnp.sum(dof * o_ref[rs, hs], axis=1, keepdims=True)
                delta_row = jnp.broadcast_to(delta, (nr, LANES)).T[0:1, :]
                s_t = _dot_nt(k, q * ATT_SCALE) + (_pick_row(ctb, head) - _pick_col(cblk, head))
                p_t = jnp.where(causal, jnp.exp(s_t - lset_ref[hh:hh + 1, rs]), 0.0)
                ds_t = p_t * (_dot_nt(v_ref[0:nk, hs], do16) - delta_row)
                p16 = p_t.astype(BF16)
                ds16 = ds_t.astype(BF16)
                dv_acc[0:nk, hs] += _dot(p16, do16)
                dk_acc[0:nk, hs] += _dot(ds16, q) * ATT_SCALE
                dq_t[hs, rs] = _dot(k_t[hs, 0:nk], ds16)
                drow_ref[hh:hh + 1, rs] = jnp.sum(ds_t, axis=0, keepdims=True)
                dcol_ref[0:nk, hs] -= jnp.broadcast_to(jnp.sum(ds_t, axis=1, keepdims=True), (nk, HEAD_DIM))
        dk_ref[...] = dk_acc[...].astype(BF16)
        dv_ref[...] = dv_acc[...].astype(BF16)
        dq_ref[...] = (dq_t[...].T * ATT_SCALE).astype(BF16)

    pair = lambda p: (0, p)
    return pl.pallas_call(
        body, name="attn_bwd", grid=(npair,),
        in_specs=[pl.BlockSpec((tp, LANES), pair),
                  pl.BlockSpec((tp, LANES), lambda p: (0, npair + p)),
                  pl.BlockSpec((tp, LANES), lambda p: (0, 2 * npair + p)),
                  pl.BlockSpec((tp, LANES), pair),
                  pl.BlockSpec((tp, LANES), pair),
                  pl.BlockSpec((None, SUBLANES, tp), lambda p: (p, 0, 0)),
                  pl.BlockSpec((tp, LANES), lambda p: (0, 0)),
                  pl.BlockSpec((SUBLANES, tp), lambda p: (0, 0))],
        out_specs=[pl.BlockSpec((tp, LANES), pair), pl.BlockSpec((tp, LANES), pair),
                   pl.BlockSpec((tp, LANES), pair),
                   pl.BlockSpec((None, SUBLANES, tp), lambda p: (p, 0, 0)),
                   pl.BlockSpec((tp, LANES), pair)],
        out_shape=[jax.ShapeDtypeStruct((tp, aw), BF16), jax.ShapeDtypeStruct((tp, aw), BF16),
                   jax.ShapeDtypeStruct((tp, aw), BF16),
                   jax.ShapeDtypeStruct((npair, SUBLANES, tp), F32),
                   jax.ShapeDtypeStruct((tp, aw), F32)],
        scratch_shapes=[pltpu.VMEM((tp, LANES), F32), pltpu.VMEM((tp, LANES), F32),
                        pltpu.VMEM((LANES, tp), F32)],
        compiler_params=_cparams("parallel"),
    )(qkv, qkv, qkv, do, o, lset, c, ct)


def fgate_bwd(dct8, drs, rest, bf_pad, fcol):
    tp = rest.shape[0]
    aw = drs.shape[1]
    nb = tp // ATT_BLOCK
    B = ATT_BLOCK

    def body(d_ref, drs_ref, f_ref, b_ref, dfl_ref, db_ref, pad_s, dc_s):
        r_i = lax.broadcasted_iota(jnp.int32, (B, B), 0)
        c_i = lax.broadcasted_iota(jnp.int32, (B, B), 1)
        triu = (c_i >= r_i).astype(BF16)
        sel = (lax.broadcasted_iota(jnp.int32, (aw, LANES), 0)
               == HEAD_DIM * lax.broadcasted_iota(jnp.int32, (aw, LANES), 1)).astype(BF16)
        pad_s[...] = jnp.zeros_like(pad_s)
        pad_s[0:SUBLANES, :] = d_ref[...]
        dc_s[...] = pad_s[...].T + _dot_split3(drs_ref[...], sel)
        carry = jnp.zeros((1, LANES), F32)
        for i in range(nb - 1, -1, -1):
            sl = slice(i * B, (i + 1) * B)
            rc = _split3_dot(triu, dc_s[sl, :])
            dc_s[sl, :] = rc + carry
            carry = carry + rc[0:1, :]
        dfl = dc_s[...] * _sigmoid(-(f_ref[...] + b_ref[...]))
        dfl_ref[...] = dfl.astype(BF16)
        db_ref[...] = jnp.sum(dfl, axis=0, keepdims=True)

    return pl.pallas_call(
        body, name="fgate_bwd", grid=(1,),
        in_specs=[pl.BlockSpec((SUBLANES, tp), lambda i: (0, 0)),
                  pl.BlockSpec((tp, aw), lambda i: (0, 0)),
                  pl.BlockSpec((tp, LANES), lambda i: (0, fcol)),
                  pl.BlockSpec((1, LANES), lambda i: (0, 0))],
        out_specs=[pl.BlockSpec((tp, LANES), lambda i: (0, 0)),
                   pl.BlockSpec((1, LANES), lambda i: (0, 0))],
        out_shape=[jax.ShapeDtypeStruct((tp, LANES), BF16), jax.ShapeDtypeStruct((1, LANES), F32)],
        scratch_shapes=[pltpu.VMEM((LANES, tp), F32), pltpu.VMEM((tp, LANES), F32)],
        compiler_params=_cparams("arbitrary"),
    )(dct8, drs, rest, bf_pad)


def in_proj_bwd(dh2, parts, w_in_t, wrest_t, h, g1, tm):
    tp, d = h.shape
    dq, dk, dv, dxr, dyr, dfl = parts
    aw, rw = dq.shape[1], dxr.shape[1]

    def body(dh2_ref, dq_ref, dk_ref, dv_ref, dxr_ref, dyr_ref, dfl_ref, wq_ref, wr_ref, h_ref, g_ref,
             dh_ref, dg_ref):
        i = pl.program_id(0)
        dz = _dot(dq_ref[...], wq_ref[0:aw, :])
        dz += _dot(dk_ref[...], wq_ref[aw:2 * aw, :])
        dz += _dot(dv_ref[...], wq_ref[2 * aw:3 * aw, :])
        dz += _dot(dxr_ref[...], wr_ref[0:rw, :])
        dz += _dot(dyr_ref[...], wr_ref[rw:2 * rw, :])
        dz += _dot(dfl_ref[...], wr_ref[2 * rw:2 * rw + LANES, :])
        dx, dg = _rms_bwd(h_ref[...], g_ref[...], dz)
        dh_ref[...] = dh2_ref[...] + dx
        _accumulate(dg_ref, dg, i == 0)

    row = lambda i: (i, 0)
    fix = lambda i: (0, 0)
    return pl.pallas_call(
        body, name="in_proj_bwd", grid=(tp // tm,),
        in_specs=[pl.BlockSpec((tm, d), row),
                  pl.BlockSpec((tm, aw), row), pl.BlockSpec((tm, aw), row), pl.BlockSpec((tm, aw), row),
                  pl.BlockSpec((tm, rw), row), pl.BlockSpec((tm, rw), row), pl.BlockSpec((tm, LANES), row),
                  pl.BlockSpec((3 * aw, d), fix), pl.BlockSpec(wrest_t.shape, fix),
                  pl.BlockSpec((tm, d), row), pl.BlockSpec((1, d), fix)],
        out_specs=[pl.BlockSpec((tm, d), row), pl.BlockSpec((1, d), fix)],
        out_shape=[jax.ShapeDtypeStruct((tp, d), F32), jax.ShapeDtypeStruct((1, d), F32)],
        compiler_params=_cparams("arbitrary"),
    )(dh2, dq, dk, dv, dxr, dyr, dfl, w_in_t, wrest_t, h, g1)


def dw_in_t(z, parts, nh, tr):
    tp, d = z.shape
    dq, dk, dv, dxr, dyr, dfl = parts
    aw, rw = dq.shape[1], dxr.shape[1]
    d_in = 3 * aw + nh + 2 * rw
    blk = d_in // N_DEV
    nr = tp // tr
    offs = [(0, aw), (aw, aw), (2 * aw, aw), (3 * aw + nh, rw), (3 * aw + nh + rw, rw)]

    def body(z_ref, dq_ref, dk_ref, dv_ref, dxr_ref, dyr_ref, dfl_ref, o_ref, acc):
        r = pl.program_id(0)

        @pl.when(r == 0)
        def _():
            acc[...] = jnp.zeros_like(acc)

        zt = z_ref[...]
        for (o, n), ref in zip(offs, (dq_ref, dk_ref, dv_ref, dxr_ref, dyr_ref)):
            acc[o:o + n, :] += _dot_tn(ref[...], zt)
        acc[3 * aw:3 * aw + nh, :] += _dot_tn(dfl_ref[...], zt)[0:nh, :]

        @pl.when(r == nr - 1)
        def _():
            for p in range(N_DEV):
                o_ref[p] = acc[p * blk:(p + 1) * blk, :].astype(BF16)

    row = lambda r: (r, 0)
    return pl.pallas_call(
        body, name="dw_in", grid=(nr,),
        in_specs=[pl.BlockSpec((tr, d), row),
                  pl.BlockSpec((tr, aw), row), pl.BlockSpec((tr, aw), row), pl.BlockSpec((tr, aw), row),
                  pl.BlockSpec((tr, rw), row), pl.BlockSpec((tr, rw), row), pl.BlockSpec((tr, LANES), row)],
        out_specs=pl.BlockSpec((N_DEV, blk, d), lambda r: (0, 0, 0)),
        out_shape=jax.ShapeDtypeStruct((N_DEV, blk, d), BF16),
        scratch_shapes=[pltpu.VMEM((d_in, d), F32)],
        compiler_params=_cparams("arbitrary"),
    )(z, dq, dk, dv, dxr, dyr, dfl)


def _place():
    return lax.axis_index("x"), lax.axis_index("y"), lax.axis_index("c")


HBM = pl.BlockSpec(memory_space=pltpu.HBM)
SEM = pl.BlockSpec(memory_space=pltpu.SEMAPHORE)
EFFECT = pltpu.SideEffectType.DATAFLOW_SIDE_EFFECTING


def _in_hbm(a):
    return pltpu.with_memory_space_constraint(a, pltpu.HBM)


def _as_list(a):
    return list(a) if isinstance(a, (list, tuple)) else [a]


def _gather_targets(x, y, c):
    return [(x, y, 1 - c), (1 - x, y, c), (x, 1 - y, c), (1 - x, 1 - y, c)]


def _slot(t):
    return 4 * t[0] + 2 * t[1] + t[2]


def gather_start(groups, name):
    flat = [a for g in groups for a in g]
    n = len(flat)
    ng = len(groups)
    lands = [lax.empty((N_DEV,) + a.shape, a.dtype) for a in flat]

    def body(*refs):
        src, land = refs[:n], refs[n:2 * n]
        sems = refs[2 * n:2 * n + 2 * ng]
        token = refs[-1]
        x, y, c = _place()
        me = 4 * x + 2 * y + c
        i = 0
        for gi, g in enumerate(groups):
            for a in range(len(g)):
                for k, t in enumerate(_gather_targets(x, y, c)):
                    pltpu.make_async_remote_copy(
                        src_ref=src[i], dst_ref=land[i].at[me],
                        send_sem=sems[2 * gi].at[4 * a + k], recv_sem=sems[2 * gi + 1].at[4 * a + k],
                        device_id=t, device_id_type=MESH).start()
                i += 1
        token[...] = jnp.zeros_like(token)

    sem_shapes = []
    for g in groups:
        sem_shapes += [pltpu.SemaphoreType.DMA((4 * len(g),)), pltpu.SemaphoreType.DMA((4 * len(g),))]
    out = pl.pallas_call(
        body, name=name,
        out_shape=sem_shapes + [pltpu.HBM(a.shape, a.dtype) for a in flat + lands]
        + [jax.ShapeDtypeStruct((SUBLANES, LANES), F32)],
        in_specs=[HBM] * (2 * n),
        out_specs=[SEM] * (2 * ng) + [HBM] * (2 * n) + [pl.BlockSpec(memory_space=pltpu.VMEM)],
        input_output_aliases={i: 2 * ng + i for i in range(2 * n)},
        compiler_params=pltpu.CompilerParams(has_side_effects=EFFECT),
    )(*[_in_hbm(a) for a in flat + lands])
    sems = out[:2 * ng]
    thru = out[2 * ng:2 * ng + 2 * n]
    srcs_t, lands_t = thru[:n], thru[n:]
    res, i = [], 0
    for gi, g in enumerate(groups):
        res.append((sems[2 * gi], sems[2 * gi + 1], srcs_t[i:i + len(g)], lands_t[i:i + len(g)]))
        i += len(g)
    return res, out[-1]


def gather_wait(send, recv, srcs, lands, after, name):
    n = len(srcs)

    def body(*refs):
        src, land = refs[:n], refs[n:2 * n]
        send_sem, recv_sem = refs[2 * n], refs[2 * n + 1]
        x, y, c = _place()
        for a in range(n):
            for k, t in enumerate(_gather_targets(x, y, c)):
                cp = pltpu.make_async_remote_copy(
                    src_ref=src[a], dst_ref=land[a].at[_slot(t)],
                    send_sem=send_sem.at[4 * a + k], recv_sem=recv_sem.at[4 * a + k],
                    device_id=t, device_id_type=MESH)
                cp.wait_send()
                cp.wait_recv()

    out = pl.pallas_call(
        body, name=name,
        out_shape=[pltpu.HBM(a.shape, a.dtype) for a in list(srcs) + list(lands)],
        in_specs=[HBM] * (2 * n) + [SEM, SEM] + [ANY] * len(_as_list(after)),
        out_specs=[HBM] * (2 * n),
        input_output_aliases={i: i for i in range(2 * n)},
        compiler_params=pltpu.CompilerParams(has_side_effects=EFFECT),
    )(*srcs, *lands, send, recv, *_as_list(after))
    return out[:n], out[n:]


def forward_start(lands, name):
    n = len(lands)

    def body(*refs):
        land = refs[:n]
        send_sem, recv_sem = refs[n], refs[n + 1]
        token = refs[-1]
        x, y, c = _place()
        for a in range(n):
            for j, chip in enumerate([(1 - x, y), (x, 1 - y), (1 - x, 1 - y)]):
                blk = land[a].at[_slot((*chip, c))]
                pltpu.make_async_remote_copy(src_ref=blk, dst_ref=blk, send_sem=send_sem.at[3 * a + j],
                                             recv_sem=recv_sem.at[3 * a + j], device_id=(x, y, 1 - c),
                                             device_id_type=MESH).start()
        token[...] = jnp.zeros_like(token)

    out = pl.pallas_call(
        body, name=name,
        out_shape=[pltpu.SemaphoreType.DMA((3 * n,)), pltpu.SemaphoreType.DMA((3 * n,))]
        + [pltpu.HBM(a.shape, a.dtype) for a in lands] + [jax.ShapeDtypeStruct((SUBLANES, LANES), F32)],
        in_specs=[HBM] * n,
        out_specs=[SEM, SEM] + [HBM] * n + [pl.BlockSpec(memory_space=pltpu.VMEM)],
        input_output_aliases={i: 2 + i for i in range(n)},
        compiler_params=pltpu.CompilerParams(has_side_effects=EFFECT),
    )(*[_in_hbm(a) for a in lands])
    return out[0], out[1], out[2:2 + n], out[-1][0, 0]


def forward_wait(send, recv, lands, after, name):
    n = len(lands)

    def body(*refs):
        land = refs[:n]
        send_sem, recv_sem = refs[n], refs[n + 1]
        x, y, c = _place()
        for a in range(n):
            for j, chip in enumerate([(1 - x, y), (x, 1 - y), (1 - x, 1 - y)]):
                cp = pltpu.make_async_remote_copy(
                    src_ref=land[a].at[_slot((*chip, c))], dst_ref=land[a].at[_slot((*chip, 1 - c))],
                    send_sem=send_sem.at[3 * a + j], recv_sem=recv_sem.at[3 * a + j],
                    device_id=(x, y, 1 - c), device_id_type=MESH)
                cp.wait_send()
                cp.wait_recv()

    return pl.pallas_call(
        body, name=name,
        out_shape=[pltpu.HBM(a.shape, a.dtype) for a in lands],
        in_specs=[HBM] * n + [SEM, SEM, ANY],
        out_specs=[HBM] * n,
        input_output_aliases={i: i for i in range(n)},
        compiler_params=pltpu.CompilerParams(has_side_effects=EFFECT),
    )(*lands, send, recv, after)


def _relations():
    return [(dx, dy, dc) for dx in (0, 1) for dy in (0, 1) for dc in (0, 1) if dx + dy + dc]


def _peer(x, y, c, rel):
    return ((1 - x) if rel[0] else x, (1 - y) if rel[1] else y, (1 - c) if rel[2] else c)


def exchange_start(srcs, lands, layer, name, after=()):
    n = len(srcs)
    after = _as_list(after)

    def body(*refs):
        src, land = refs[:n], refs[n:2 * n]
        send_sem, recv_sem = refs[2 * n + len(after)], refs[2 * n + len(after) + 1]
        token = refs[-1]
        x, y, c = _place()
        me = 4 * x + 2 * y + c
        for k, rel in enumerate(_relations()):
            peer = _peer(x, y, c, rel)
            for a in range(n):
                pltpu.make_async_remote_copy(
                    src_ref=src[a] if layer is None else src[a].at[_slot(peer)],
                    dst_ref=land[a].at[me] if layer is None else land[a].at[me, layer],
                    send_sem=send_sem.at[7 * a + k], recv_sem=recv_sem.at[7 * a + k],
                    device_id=peer, device_id_type=MESH).start()
        token[...] = jnp.zeros_like(token)

    out = pl.pallas_call(
        body, name=name,
        out_shape=[pltpu.SemaphoreType.DMA((7 * n,)), pltpu.SemaphoreType.DMA((7 * n,))]
        + [pltpu.HBM(a.shape, a.dtype) for a in list(srcs) + list(lands)]
        + [jax.ShapeDtypeStruct((SUBLANES, LANES), F32)],
        in_specs=[HBM] * (2 * n) + [ANY] * len(after),
        out_specs=[SEM, SEM] + [HBM] * (2 * n) + [pl.BlockSpec(memory_space=pltpu.VMEM)],
        input_output_aliases={i: 2 + i for i in range(2 * n)},
        compiler_params=pltpu.CompilerParams(has_side_effects=EFFECT),
    )(*[_in_hbm(a) for a in list(srcs) + list(lands)], *after)
    return out[0], out[1], out[2:2 + n], out[2 + n:2 + 2 * n], out[-1][0, 0]


def exchange_wait(send, recv, srcs, lands, after, layer, name):
    n = len(srcs)

    def body(*refs):
        src, land = refs[:n], refs[n:2 * n]
        send_sem, recv_sem = refs[2 * n], refs[2 * n + 1]
        x, y, c = _place()
        for k, rel in enumerate(_relations()):
            peer = _peer(x, y, c, rel)
            for a in range(n):
                cp = pltpu.make_async_remote_copy(
                    src_ref=src[a] if layer is None else src[a].at[_slot(peer)],
                    dst_ref=land[a].at[_slot(peer)] if layer is None else land[a].at[_slot(peer), layer],
                    send_sem=send_sem.at[7 * a + k], recv_sem=recv_sem.at[7 * a + k],
                    device_id=peer, device_id_type=MESH)
                cp.wait_send()
                cp.wait_recv()

    out = pl.pallas_call(
        body, name=name,
        out_shape=[pltpu.HBM(a.shape, a.dtype) for a in list(srcs) + list(lands)],
        in_specs=[HBM] * (2 * n) + [SEM, SEM] + [ANY] * len(_as_list(after)),
        out_specs=[HBM] * (2 * n),
        input_output_aliases={i: i for i in range(2 * n)},
        compiler_params=pltpu.CompilerParams(has_side_effects=EFFECT),
    )(*srcs, *lands, send, recv, *_as_list(after))
    return out[:n], out[n:]


def _adamw_math(g, w, m, v):
    m = ADAM_B1 * m + (1.0 - ADAM_B1) * g
    v = ADAM_B2 * v + (1.0 - ADAM_B2) * (g * g)
    m_hat = m / (1.0 - ADAM_B1 ** ADAM_STEP)
    v_hat = v / (1.0 - ADAM_B2 ** ADAM_STEP)
    delta = -ADAM_LR * (m_hat / (jnp.sqrt(v_hat) + ADAM_EPS) + ADAM_WD * w)
    return delta, m, v


def _sum_with_own(p_ref, own_refs, layer, me):
    own = own_refs[0][...]
    for k in range(1, len(own_refs)):
        own = jnp.where(layer == k, own_refs[k][...], own)
    g = None
    for p in range(p_ref.shape[0]):
        term = jnp.where(me == p, own, p_ref[p]).astype(F32)
        g = term if g is None else g + term
    return g


def sum_adamw(parts, owns, me, w, m, v, tr, name):
    npart, rows, cols = parts.shape
    nl = len(owns)
    per_layer = rows // nl // tr

    def body(me_ref, p_ref, *refs):
        own_refs = refs[:nl]
        w_ref, m_ref, v_ref, g_ref, d_ref, nm_ref, nv_ref = refs[nl:]
        g = _sum_with_own(p_ref, own_refs, pl.program_id(0) // per_layer, me_ref[0])
        delta, nm, nv = _adamw_math(g, w_ref[...], m_ref[...], v_ref[...])
        g_ref[...] = g
        d_ref[...] = delta
        nm_ref[...] = nm
        nv_ref[...] = nv

    blk = pl.BlockSpec((tr, cols), lambda i, me_ref: (i, 0))
    own_specs = [pl.BlockSpec((None, tr, cols),
                              lambda i, me_ref, l=l: (me_ref[0], jnp.clip(i - l * per_layer, 0, per_layer - 1), 0))
                 for l in range(nl)]
    return pl.pallas_call(
        body, name=name,
        grid_spec=pltpu.PrefetchScalarGridSpec(
            num_scalar_prefetch=1, grid=(rows // tr,),
            in_specs=[pl.BlockSpec((npart, tr, cols), lambda i, me_ref: (0, i, 0))] + own_specs + [blk, blk, blk],
            out_specs=[blk] * 4),
        out_shape=[jax.ShapeDtypeStruct((rows, cols), F32)] * 4,
        compiler_params=_cparams("arbitrary"),
    )(me, parts, *owns, w, m, v)


def sum_adamw_t(parts, owns, me, w, m, v, name):
    npart, nl, rows, cols = parts.shape

    def body(me_ref, p_ref, *refs):
        own_refs = refs[:nl]
        w_ref, m_ref, v_ref, g_ref, d_ref, nm_ref, nv_ref = refs[nl:]
        g = _sum_with_own(p_ref, own_refs, pl.program_id(0), me_ref[0])
        delta, nm, nv = _adamw_math(g, w_ref[...], m_ref[...], v_ref[...])
        g_ref[...] = g
        d_ref[...] = delta
        nm_ref[...] = nm
        nv_ref[...] = nv

    blk = pl.BlockSpec((None, rows, cols), lambda l, me_ref: (l, 0, 0))
    own_specs = [pl.BlockSpec((None, rows, cols), lambda l, me_ref: (me_ref[0], 0, 0)) for _ in range(nl)]
    return pl.pallas_call(
        body, name=name,
        grid_spec=pltpu.PrefetchScalarGridSpec(
            num_scalar_prefetch=1, grid=(nl,),
            in_specs=[pl.BlockSpec((npart, None, rows, cols), lambda l, me_ref: (0, l, 0, 0))] + own_specs
            + [blk, blk, blk],
            out_specs=[blk] * 4),
        out_shape=[jax.ShapeDtypeStruct((nl, rows, cols), F32)] * 4,
        compiler_params=_cparams("arbitrary"),
    )(me, parts, *owns, w, m, v)


def adamw_group(gs, ws, ms, vs, name):
    n = len(gs)

    def body(*refs):
        g, w, m, v, outs = refs[:n], refs[n:2 * n], refs[2 * n:3 * n], refs[3 * n:4 * n], refs[4 * n:]
        for i in range(n):
            delta, nm, nv = _adamw_math(g[i][...], w[i][...], m[i][...], v[i][...])
            outs[i][...] = delta
            outs[n + i][...] = nm
            outs[2 * n + i][...] = nv

    vmem = pl.BlockSpec(memory_space=pltpu.VMEM)
    out = pl.pallas_call(
        body, name=name,
        in_specs=[vmem] * (4 * n), out_specs=[vmem] * (3 * n),
        out_shape=[jax.ShapeDtypeStruct(a.shape, F32) for a in list(ws) * 3],
        compiler_params=_cparams(),
    )(*gs, *ws, *ms, *vs)
    return out[:n], out[n:2 * n], out[2 * n:]


def sum_parts(parts, name):
    npart, rows, cols = parts.shape

    def body(p_ref, g_ref):
        g = p_ref[0].astype(F32)
        for p in range(1, npart):
            g = g + p_ref[p].astype(F32)
        g_ref[...] = g

    return pl.pallas_call(
        body, name=name, grid=(1,),
        in_specs=[pl.BlockSpec((npart, rows, cols), lambda i: (0, 0, 0))],
        out_specs=pl.BlockSpec((rows, cols), lambda i: (0, 0)),
        out_shape=jax.ShapeDtypeStruct((rows, cols), F32),
        compiler_params=_cparams("arbitrary"),
    )(parts)


def _round_up(n, m):
    return (n + m - 1) // m * m


def _block_diag_pairs(w):
    nb, b, _ = w.shape
    per = LANES // b
    ng = nb // per
    w = w.reshape(ng, per, b, b)
    eye = jnp.eye(per, dtype=w.dtype)
    out = jnp.einsum('gpij,pq->gpiqj', w, eye).reshape(ng, LANES, LANES)
    return out.astype(BF16)


def _block_diag_extract(g, b):
    ng = g.shape[0]
    per = LANES // b
    g = g.reshape(ng, per, b, per, b)
    idx = jnp.arange(per)
    return g[:, idx, :, idx, :].transpose(1, 0, 2, 3).reshape(ng * per, b, b)


def _tiles(v):
    v = v.reshape(-1)
    n = _round_up(v.shape[0], SUBLANES * LANES)
    return jnp.pad(v, (0, n - v.shape[0])).reshape(-1, LANES)


SMALL = ['attn_norm_g', 'b_f', 'conv_w', 'conv_b', 'w_gate_a', 'b_gate_a', 'w_gate_x', 'b_gate_x',
         'lru_L', 'attn_out_g', 'rec_out_g', 'mlp_norm_g', 'final_g', 'meta']


def _pack(d):
    return jnp.concatenate([_tiles(d[n]) for n in SMALL], axis=0)


def _unpack(vec, shapes):
    out, r = {}, 0
    for n in SMALL:
        size = math.prod(shapes[n])
        nr = _round_up(size, SUBLANES * LANES) // LANES
        out[n] = vec[r:r + nr].reshape(-1)[:size].reshape(shapes[n])
        r += nr
    return out


def _row_tile(tp):
    return tp // 4 if (tp // 4) % 16 == 0 else tp


def local_step(x, tgt, meta, small, hooks):
    s, d = x.shape
    t_real = s + N_META
    tp = _round_up(t_real, ATT_BLOCK)
    depth = small['attn_norm_g'].shape[0]
    nh = small['b_f'].shape[1]
    rw = small['conv_b'].shape[1]
    blk = small['w_gate_a'].shape[2]
    tm = _row_tile(tp)
    tm2 = tp // 2
    fcol = 2 * rw // LANES

    h = jnp.concatenate([meta, x, jnp.zeros((tp - t_real, d), F32)], axis=0)
    tgt_p = jnp.pad(tgt, ((N_META, tp - t_real), (0, 0)))
    row = lambda v: v.reshape(1, -1)
    bf_pad = jnp.pad(small['b_f'], ((0, 0), (0, LANES - nh)))

    saved = []
    for l in range(depth):
        w_in_t, wrest_t, wout, tok_w = hooks.mixer_weights(l, h)
        wga = _block_diag_pairs(small['w_gate_a'][l])
        wgx = _block_diag_pairs(small['w_gate_x'][l])
        z, qkv, rest = in_proj(h, row(small['attn_norm_g'][l]) + tok_w, w_in_t, wrest_t, 3 * nh * HEAD_DIM, tm)
        c, ct = fgate_fwd(rest, bf_pad[l:l + 1], fcol)
        o, lset = attn_fwd(qkv, c, ct, nh)
        rec, hr, xc, *gates = rec_fwd(rest, small['conv_w'][l], row(small['conv_b'][l]), wga,
                                      row(small['b_gate_a'][l]), wgx, row(small['b_gate_x'][l]),
                                      row(small['lru_L'][l]), rw)
        gup, gdown, tok_w = hooks.mlp_weights(l, rec)
        h2, mix, z2, u, h3 = out_mlp_fwd(h, o, rec, row(small['attn_out_g'][l]), row(small['rec_out_g'][l]), wout,
                                         row(small['mlp_norm_g'][l]) + tok_w, gup, gdown, tm)
        saved.append(dict(h=h, z=z, qkv=qkv, rest=rest, c=c, ct=ct, o=o, lset=lset, rec=rec, hr=hr, xc=xc,
                          h2=h2, mix=mix, z2=z2, u=u, wga=wga, wgx=wgx, gates=gates,
                          w_in_t=w_in_t, wrest_t=wrest_t, wout=wout, gup=gup, gdown=gdown))
        h = h3

    dh, dgf, loss = loss_head(h, row(small['final_g']), tgt_p, t_real, tm)

    gs = {n: [None] * depth for n in SMALL if n not in ('final_g', 'meta')}
    tok = jnp.zeros((), F32)
    for l in reversed(range(depth)):
        sv = saved[l]
        gup, gdown = sv['gup'], sv['gdown']
        tf = gup.shape[2]
        dup, dh2, dg2, dhb = mlp_bwd(dh, sv['u'], sv['h2'], row(small['mlp_norm_g'][l]) + tok, gup, gdown, tm)
        gs['mlp_norm_g'][l] = dg2[0]
        do, drec, dga, dgr, dw_out = out_proj_bwd(dh2, sv['mix'], sv['o'], sv['rec'], row(small['attn_out_g'][l]),
                                                  row(small['rec_out_g'][l]), sv['wout'], tm)
        gs['attn_out_g'][l] = dga[0]
        gs['rec_out_g'][l] = dgr[0]
        dw_down, dw_up = dw_mlp(sv['u'], dhb, sv['z2'], dup, tf)
        blocks = dict(w_down=dw_down, w_up=dw_up, w_out=dw_out.reshape(N_DEV, d // N_DEV, d))
        tok = hooks.grads_ready(l, 'mlp', blocks)
        dxr, dyr, dwga, dwgx, vec = rec_bwd(drec, sv['hr'], sv['xc'], sv['gates'], sv['rest'], small['conv_w'][l],
                                            sv['wga'], sv['wgx'], row(small['lru_L'][l]) + tok, rw)
        gs['w_gate_a'][l] = _block_diag_extract(dwga, blk)
        gs['w_gate_x'][l] = _block_diag_extract(dwgx, blk)
        vec = vec.transpose(1, 0, 2).reshape(SUBLANES, rw)
        gs['conv_w'][l] = vec[0:CONV_WIDTH]
        gs['conv_b'][l] = vec[4]
        gs['b_gate_a'][l] = vec[5]
        gs['b_gate_x'][l] = vec[6]
        gs['lru_L'][l] = vec[7]
        dq, dk, dv, drow, dcol = attn_bwd(sv['qkv'], do, sv['o'], sv['lset'], sv['c'], sv['ct'] + tok, nh)
        drow8 = drow[:, 0:2, :].reshape(nh, tp)
        if nh < SUBLANES:
            drow8 = jnp.pad(drow8, ((0, SUBLANES - nh), (0, 0)))
        dfl, dbf = fgate_bwd(drow8, dcol, sv['rest'], bf_pad[l:l + 1], fcol)
        gs['b_f'][l] = dbf[0, 0:nh]
        parts = (dq, dk, dv, dxr, dyr, dfl)
        dh, dg1 = in_proj_bwd(dh2, parts, sv['w_in_t'], sv['wrest_t'], sv['h'], row(small['attn_norm_g'][l]), tm)
        gs['attn_norm_g'][l] = dg1[0]
        first = ()
        if l == 0:
            grads = {n: jnp.stack(v) for n, v in gs.items()}
            grads['final_g'] = dgf[0]
            grads['meta'] = dh[0:N_META]
            first = hooks.small_ready(grads)
        dw_in = dw_in_t(sv['z'], parts, nh, tm2)
        tok = hooks.grads_ready(l, 'in', dict(w_in=dw_in), first)

    return loss[0, 0], dh


def prep_weights(g_in, g_out, nh, rw):
    d = g_in.shape[2]
    w_in_t = g_in.reshape(-1, d)
    f0 = 3 * nh * HEAD_DIM
    wrest_t = jnp.concatenate([w_in_t[f0 + nh:f0 + nh + 2 * rw],
                               jnp.pad(w_in_t[f0:f0 + nh], ((0, LANES - nh), (0, 0)))], axis=0)
    return w_in_t, wrest_t, g_out.reshape(d, d)


BIG = ['w_in', 'w_out', 'w_up', 'w_down']
EXCHANGE_GROUPS = {'mlp': ['w_down', 'w_up', 'w_out'], 'in': ['w_in']}
WEIGHTS = ['meta', 'attn_norm_g', 'w_in', 'b_f', 'conv_w', 'conv_b', 'w_gate_a', 'b_gate_a', 'w_gate_x', 'b_gate_x',
           'lru_L', 'attn_out_g', 'rec_out_g', 'w_out', 'mlp_norm_g', 'w_up', 'w_down', 'final_g']


def _set_own(arr, own, me):
    return lax.dynamic_update_slice_in_dim(arr, own[None], me, axis=0)


class _Step:
    def __init__(self, w, nh, rw, me):
        self.w, self.nh, self.rw, self.me = w, nh, rw, me
        depth = w['w_in'].shape[0]
        first = [w['w_in_t'][:, 0, :].astype(BF16), w['w_out'][0].astype(BF16), w['meta'], w['conv_w']]
        self.pending, token = gather_start([first], "gather_start_0")
        zero = token[0, 0].astype(BF16)
        groups = [[w['w_up'][0].astype(BF16) + zero, w['w_down'][0].astype(BF16) + zero]]
        for l in range(1, depth):
            groups.append([w['w_in_t'][:, l, :].astype(BF16) + zero, w['w_out'][l].astype(BF16) + zero])
            groups.append([w['w_up'][l].astype(BF16) + zero, w['w_down'][l].astype(BF16) + zero])
        rest, _ = gather_start(groups, "gather_start_1")
        self.pending += rest
        self.first_after = rest[0][2][0]
        self.gathered = {}
        self.passing = {}
        self.token = jnp.zeros((), F32)
        self.lands = {n: lax.empty((N_DEV,) + w[n].shape, BF16) for n in BIG}
        din8, _, d = w['w_in_t'].shape
        self.lands['w_in'] = lax.empty((N_DEV, depth, din8, d), BF16)
        self.started = []
        self.small = None

    def _pass_on(self, gi, after):
        if gi < len(self.pending) and gi not in self.passing:
            send, recv, srcs, lands = self.pending[gi]
            srcs, lands = gather_wait(send, recv, srcs, lands, after, "gather_wait_%d" % gi)
            fsend, frecv, lands, token = forward_start(lands, "forward_start_%d" % gi)
            self.passing[gi] = (fsend, frecv, srcs, lands)
            self.token = token

    def group(self, gi, after):
        if gi not in self.gathered:
            self._pass_on(gi, after)
            fsend, frecv, srcs, lands = self.passing[gi]
            lands = forward_wait(fsend, frecv, lands, after, "forward_wait_%d" % gi)
            self.gathered[gi] = [_set_own(g, own, self.me) for g, own in zip(lands, srcs)]
            if gi >= 2:
                self._pass_on(gi + 1, lands[0])
        return self.gathered[gi]

    def mixer_weights(self, l, after):
        g = self.group(2 * l, after)
        return (*prep_weights(g[0], g[1], self.nh, self.rw), self.token)

    def mlp_weights(self, l, after):
        g = self.group(2 * l + 1, after)
        return g[0], g[1], self.token

    def grads_ready(self, l, group, blocks, after=()):
        names = EXCHANGE_GROUPS[group]
        send, recv, srcs, lands, token = exchange_start(
            [blocks[n] for n in names], [self.lands[n] for n in names], l, "exchange_start_%s_%d" % (group, l),
            after)
        for n, a in zip(names, lands):
            self.lands[n] = a
        self.started.append((l, group, send, recv, srcs))
        return token

    def small_ready(self, grads):
        self.small_shapes = {n: grads[n].shape for n in SMALL}
        packed = _pack(grads).astype(BF16)
        send, recv, srcs, lands, token = exchange_start(
            [packed], [lax.empty((N_DEV,) + packed.shape, BF16)], None, "small_start")
        self.small = (send, recv, srcs, lands)
        return srcs[0]

    def small_sum(self, after):
        send, recv, srcs, lands = self.small
        srcs, lands = exchange_wait(send, recv, srcs, lands, after, None, "small_wait")
        parts = _set_own(lands[0], srcs[0], self.me)
        return _unpack(sum_parts(parts, "sum_small_grads"), self.small_shapes)

    def received(self, group, after):
        names = EXCHANGE_GROUPS[group]
        own = {n: [None] * self.w[n].shape[0] for n in names}
        for l, grp, send, recv, srcs in self.started:
            if grp != group:
                continue
            srcs, lands = exchange_wait(send, recv, srcs, [self.lands[n] for n in names], after, l,
                                        "exchange_wait_%s_%d" % (group, l))
            for n, a, sr in zip(names, lands, srcs):
                self.lands[n] = a
                own[n][l] = sr
        return {n: (self.lands[n], own[n]) for n in names}


def kernel(x, meta, attn_norm_g, w_in, b_f, conv_w, conv_b, w_gate_a, b_gate_a, w_gate_x, b_gate_x, lru_L, attn_out_g, rec_out_g, w_out, mlp_norm_g, w_up, w_down, final_g, loss_target, m_meta, m_attn_norm_g, m_w_in, m_b_f, m_conv_w, m_conv_b, m_w_gate_a, m_b_gate_a, m_w_gate_x, m_b_gate_x, m_lru_L, m_attn_out_g, m_rec_out_g, m_w_out, m_mlp_norm_g, m_w_up, m_w_down, m_final_g, v_meta, v_attn_norm_g, v_w_in, v_b_f, v_conv_w, v_conv_b, v_w_gate_a, v_b_gate_a, v_w_gate_x, v_b_gate_x, v_lru_L, v_attn_out_g, v_rec_out_g, v_w_out, v_mlp_norm_g, v_w_up, v_w_down, v_final_g):
    w = dict(meta=meta, attn_norm_g=attn_norm_g, w_in=w_in, b_f=b_f, conv_w=conv_w, conv_b=conv_b,
             w_gate_a=w_gate_a, b_gate_a=b_gate_a, w_gate_x=w_gate_x, b_gate_x=b_gate_x, lru_L=lru_L,
             attn_out_g=attn_out_g, rec_out_g=rec_out_g, w_out=w_out, mlp_norm_g=mlp_norm_g, w_up=w_up,
             w_down=w_down, final_g=final_g)
    mo = dict(meta=m_meta, attn_norm_g=m_attn_norm_g, w_in=m_w_in, b_f=m_b_f, conv_w=m_conv_w, conv_b=m_conv_b,
              w_gate_a=m_w_gate_a, b_gate_a=m_b_gate_a, w_gate_x=m_w_gate_x, b_gate_x=m_b_gate_x, lru_L=m_lru_L,
              attn_out_g=m_attn_out_g, rec_out_g=m_rec_out_g, w_out=m_w_out, mlp_norm_g=m_mlp_norm_g,
              w_up=m_w_up, w_down=m_w_down, final_g=m_final_g)
    vo = dict(meta=v_meta, attn_norm_g=v_attn_norm_g, w_in=v_w_in, b_f=v_b_f, conv_w=v_conv_w, conv_b=v_conv_b,
              w_gate_a=v_w_gate_a, b_gate_a=v_b_gate_a, w_gate_x=v_w_gate_x, b_gate_x=v_b_gate_x, lru_L=v_lru_L,
              attn_out_g=v_attn_out_g, rec_out_g=v_rec_out_g, w_out=v_w_out, mlp_norm_g=v_mlp_norm_g,
              w_up=v_w_up, w_down=v_w_down, final_g=v_final_g)
    depth = w_in.shape[0]
    nh = b_f.shape[1]
    rw = conv_b.shape[1]
    me = 4 * lax.axis_index("x") + 2 * lax.axis_index("y") + lax.axis_index("c")

    w['w_in_t'] = jnp.transpose(w_in, (2, 0, 1))
    swap = lambda a: jnp.swapaxes(a, 1, 2)
    step = _Step(w, nh, rw, me)
    g0 = step.group(0, step.first_after)
    meta_full = g0[2].transpose(1, 0, 2).reshape(N_META, -1)
    conv_full = g0[3].transpose(1, 2, 0, 3).reshape(depth, CONV_WIDTH, rw)
    small = {n: w[n] for n in SMALL}
    small['conv_w'] = conv_full

    loss_part, dh0 = local_step(x[0], loss_target[0], meta_full, small, step)
    loss = lax.psum(loss_part, ("x", "y", "c"))
    grad_x = dh0[N_META:N_META + x.shape[1]][None]

    out_g, out_d, out_m, out_v = {}, {}, {}, {}
    me1 = me.reshape(1).astype(jnp.int32)

    def update_big(group, after):
        for n, (r, owns) in step.received(group, after).items():
            if n == 'w_in':
                out = sum_adamw_t(r, owns, me1, swap(w[n]), swap(mo[n]), swap(vo[n]), "adamw_w_in")
                out = [swap(a) for a in out]
            else:
                shp = w[n].shape
                rows, cols = shp[0] * shp[1], shp[2]
                tr = min(512 if cols <= 512 else 256, shp[1])
                out = sum_adamw(r.reshape(N_DEV, rows, cols), owns, me1, w[n].reshape(rows, cols),
                                mo[n].reshape(rows, cols), vo[n].reshape(rows, cols), tr, "adamw_" + n)
                out = [a.reshape(shp) for a in out]
            out_g[n], out_d[n], out_m[n], out_v[n] = out
            after = out[0]
        return after

    update_big('mlp', step.started[-1][4][0])

    gsum = step.small_sum([out_g[n] for n in EXCHANGE_GROUPS['mlp']])
    gsum['meta'] = lax.dynamic_slice_in_dim(gsum['meta'], me * meta.shape[1], meta.shape[1], axis=1)
    gsum['conv_w'] = lax.dynamic_slice_in_dim(gsum['conv_w'], me * conv_w.shape[2], conv_w.shape[2], axis=2)
    as2d = lambda a: a.reshape(-1, a.shape[-1])
    deltas, new_m, new_v = adamw_group([as2d(gsum[n]) for n in SMALL], [as2d(w[n]) for n in SMALL],
                                       [as2d(mo[n]) for n in SMALL], [as2d(vo[n]) for n in SMALL], "adamw_small")
    for i, n in enumerate(SMALL):
        out_g[n] = gsum[n]
        out_d[n], out_m[n], out_v[n] = [a[i].reshape(w[n].shape) for a in (deltas, new_m, new_v)]

    update_big('in', deltas[0])

    return (loss, grad_x, *[out_g[n] for n in WEIGHTS], *[out_d[n] for n in WEIGHTS],
            *[out_m[n] for n in WEIGHTS], *[out_v[n] for n in WEIGHTS])
```

```python
import math

import jax
import jax.numpy as jnp
from jax import lax
from jax.experimental import pallas as pl
from jax.experimental.pallas import tpu as pltpu

F32 = jnp.float32
BF16 = jnp.bfloat16

N_DEV = 8
N_META = 16
HEAD_DIM = 64
CONV_WIDTH = 4
RG_C = 8.0
NORM_EPS = 1e-6
LANES = 128
SUBLANES = 8
ATT_BLOCK = 128
ATT_TQ = 512
NEG_BIG = -1e30
ATT_SCALE = 1.0 / math.sqrt(HEAD_DIM)

ADAM_LR = 0.001
ADAM_B1 = 0.9
ADAM_B2 = 0.999
ADAM_EPS = 1e-08
ADAM_WD = 0.01
ADAM_STEP = 10

VMEM_LIMIT_BYTES = 56 * 1024 * 1024
MESH = pl.DeviceIdType.MESH
ANY = pl.BlockSpec(memory_space=pl.ANY)


def _cparams(*sem):
    return pltpu.CompilerParams(dimension_semantics=sem if sem else None,
                                vmem_limit_bytes=VMEM_LIMIT_BYTES)


def _dot(a, b):
    return jnp.dot(a, b, preferred_element_type=F32)


def _dot_nt(a, b):
    return lax.dot_general(a, b, (((1,), (1,)), ((), ())), preferred_element_type=F32)


def _dot_tn(a, b):
    return lax.dot_general(a, b, (((0,), (0,)), ((), ())), preferred_element_type=F32)


def _sigmoid(x):
    return 0.5 * (1.0 + jnp.tanh(0.5 * x))


def _log_sigmoid(x):
    return jnp.minimum(x, 0.0) - jnp.log(1.0 + jnp.exp(-jnp.abs(x)))


def _expm1(x):
    series = x * (1.0 + x * (0.5 + x * (1.0 / 6.0 + x * (1.0 / 24.0))))
    return jnp.where(jnp.abs(x) < 1e-2, series, jnp.exp(x) - 1.0)


_GELU_K = math.sqrt(2.0 / math.pi)
_GELU_C = 0.044715


def _gelu(x):
    t = jnp.tanh(_GELU_K * (x + _GELU_C * x * x * x))
    return 0.5 * x * (1.0 + t)


def _gelu_and_grad(x):
    x2 = x * x
    t = jnp.tanh(_GELU_K * (x + _GELU_C * x2 * x))
    half = 0.5 * (1.0 + t)
    return x * half, half + 0.5 * x * (1.0 - t * t) * _GELU_K * (1.0 + 3.0 * _GELU_C * x2)


def _split3_dot(tri, x):
    hi = x.astype(BF16)
    r1 = x - hi.astype(F32)
    mid = r1.astype(BF16)
    lo = (r1 - mid.astype(F32)).astype(BF16)
    return _dot(tri, hi) + _dot(tri, mid) + _dot(tri, lo)


def _dot_split3(x, sel):
    hi = x.astype(BF16)
    r1 = x - hi.astype(F32)
    mid = r1.astype(BF16)
    lo = (r1 - mid.astype(F32)).astype(BF16)
    return _dot(hi, sel) + _dot(mid, sel) + _dot(lo, sel)


def _rms_fwd(x, g):
    r = lax.rsqrt(jnp.mean(x * x, axis=-1, keepdims=True) + NORM_EPS)
    return x * r * g


def _rms_bwd(x, g, dy):
    r = lax.rsqrt(jnp.mean(x * x, axis=-1, keepdims=True) + NORM_EPS)
    xn = x * r
    dxn = dy * g
    dx = r * (dxn - xn * jnp.mean(dxn * xn, axis=-1, keepdims=True))
    return dx, jnp.sum(dy * xn, axis=0, keepdims=True)


def _accumulate(ref, val, first):
    @pl.when(first)
    def _():
        ref[...] = val

    @pl.when(jnp.logical_not(first))
    def _():
        ref[...] += val


def in_proj(h, g1, w_in_t, wrest_t, nq, tm):
    tp, d = h.shape
    nr = wrest_t.shape[0]

    def body(h_ref, g_ref, wq_ref, wr_ref, z_ref, qkv_ref, rest_ref):
        z = _rms_fwd(h_ref[...], g_ref[...]).astype(BF16)
        z_ref[...] = z
        qkv_ref[...] = _dot_nt(z, wq_ref[...]).astype(BF16)
        rest_ref[...] = _dot_nt(z, wr_ref[...])

    return pl.pallas_call(
        body, name="in_proj", grid=(tp // tm,),
        in_specs=[pl.BlockSpec((tm, d), lambda i: (i, 0)),
                  pl.BlockSpec((1, d), lambda i: (0, 0)),
                  pl.BlockSpec((nq, d), lambda i: (0, 0)),
                  pl.BlockSpec((nr, d), lambda i: (0, 0))],
        out_specs=[pl.BlockSpec((tm, d), lambda i: (i, 0)),
                   pl.BlockSpec((tm, nq), lambda i: (i, 0)),
                   pl.BlockSpec((tm, nr), lambda i: (i, 0))],
        out_shape=[jax.ShapeDtypeStruct((tp, d), BF16),
                   jax.ShapeDtypeStruct((tp, nq), BF16),
                   jax.ShapeDtypeStruct((tp, nr), F32)],
        compiler_params=_cparams("parallel"),
    )(h, g1, w_in_t, wrest_t)


def fgate_fwd(rest, bf_pad, fcol):
    tp = rest.shape[0]
    nb = tp // ATT_BLOCK

    def body(f_ref, b_ref, c_ref, ct_ref):
        r_i = lax.broadcasted_iota(jnp.int32, (ATT_BLOCK, ATT_BLOCK), 0)
        c_i = lax.broadcasted_iota(jnp.int32, (ATT_BLOCK, ATT_BLOCK), 1)
        tri = (r_i >= c_i).astype(BF16)
        carry = jnp.zeros((1, LANES), F32)
        for i in range(nb):
            sl = slice(i * ATT_BLOCK, (i + 1) * ATT_BLOCK)
            lf = _log_sigmoid(f_ref[sl, :] + b_ref[...])
            cs = _split3_dot(tri, lf) + carry
            carry = cs[ATT_BLOCK - 1:ATT_BLOCK, :]
            c_ref[sl, :] = cs
            ct_ref[:, sl] = cs.T[0:SUBLANES, :]

    return pl.pallas_call(
        body, name="fgate_fwd", grid=(1,),
        in_specs=[pl.BlockSpec((tp, LANES), lambda i: (0, fcol)),
                  pl.BlockSpec((1, LANES), lambda i: (0, 0))],
        out_specs=[pl.BlockSpec((tp, LANES), lambda i: (0, 0)),
                   pl.BlockSpec((SUBLANES, tp), lambda i: (0, 0))],
        out_shape=[jax.ShapeDtypeStruct((tp, LANES), F32),
                   jax.ShapeDtypeStruct((SUBLANES, tp), F32)],
        compiler_params=_cparams("arbitrary"),
    )(rest, bf_pad)


def _pick_col(blk, head):
    lane = lax.broadcasted_iota(jnp.int32, blk.shape, 1)
    return jnp.sum(jnp.where(lane == head, blk, 0.0), axis=1, keepdims=True)


def _pick_row(blk, head):
    sub = lax.broadcasted_iota(jnp.int32, blk.shape, 0)
    return jnp.sum(jnp.where(sub == head, blk, 0.0), axis=0, keepdims=True)


def _att_tiles(tp):
    out, r0 = [], 0
    while r0 < tp:
        rows = min(ATT_TQ, tp - r0)
        out.append((r0, rows, r0 + rows))
        r0 += rows
    return out


def attn_fwd(qkv, c, ct, nh):
    tp = qkv.shape[0]
    npair = nh // 2
    tiles = _att_tiles(tp)

    def body(q_ref, k_ref, v_ref, c_ref, ct_ref, o_ref, lset_ref):
        p = pl.program_id(0)
        lset_ref[...] = jnp.zeros_like(lset_ref)
        for r0, nr, nk in tiles:
            rs = slice(r0, r0 + nr)
            causal = (r0 + lax.broadcasted_iota(jnp.int32, (nr, nk), 0)
                      >= lax.broadcasted_iota(jnp.int32, (nr, nk), 1))
            cblk = c_ref[rs, :]
            ctb = ct_ref[:, 0:nk]
            for hh in range(2):
                head = 2 * p + hh
                hs = slice(hh * HEAD_DIM, (hh + 1) * HEAD_DIM)
                q = q_ref[rs, hs] * ATT_SCALE
                s = _dot_nt(q, k_ref[0:nk, hs]) + (_pick_col(cblk, head) - _pick_row(ctb, head))
                s = jnp.where(causal, s, NEG_BIG)
                m = jnp.max(s, axis=1, keepdims=True)
                pm = jnp.exp(s - m)
                l = jnp.sum(pm, axis=1, keepdims=True)
                o_ref[rs, hs] = _dot(pm.astype(BF16), v_ref[0:nk, hs]) / l
                lse = m + jnp.log(l)
                lset_ref[hh:hh + 1, rs] = jnp.broadcast_to(lse, (nr, LANES)).T[0:1, :]

    pair = lambda p: (0, p)
    return pl.pallas_call(
        body, name="attn_fwd", grid=(npair,),
        in_specs=[pl.BlockSpec((tp, LANES), pair),
                  pl.BlockSpec((tp, LANES), lambda p: (0, npair + p)),
                  pl.BlockSpec((tp, LANES), lambda p: (0, 2 * npair + p)),
                  pl.BlockSpec((tp, LANES), lambda p: (0, 0)),
                  pl.BlockSpec((SUBLANES, tp), lambda p: (0, 0))],
        out_specs=[pl.BlockSpec((tp, LANES), pair),
                   pl.BlockSpec((None, SUBLANES, tp), lambda p: (p, 0, 0))],
        out_shape=[jax.ShapeDtypeStruct((tp, nh * HEAD_DIM), F32),
                   jax.ShapeDtypeStruct((npair, SUBLANES, tp), F32)],
        compiler_params=_cparams("parallel"),
    )(qkv, qkv, qkv, c, ct)


def _shift_down(x, k, n):
    if k == 0:
        return x
    rows = lax.broadcasted_iota(jnp.int32, x.shape, 0)
    return jnp.where(rows >= k, pltpu.roll(x, k, 0), 0.0)


def _shift_up(x, k, n):
    if k == 0:
        return x
    rows = lax.broadcasted_iota(jnp.int32, x.shape, 0)
    return jnp.where(rows < n - k, pltpu.roll(x, n - k, 0), 0.0)


def _conv_fwd(xr, cw_ref, cb_ref, n):
    xc = cw_ref[CONV_WIDTH - 1:CONV_WIDTH, :] * xr + cb_ref[...]
    for k in range(1, CONV_WIDTH):
        xc = xc + cw_ref[CONV_WIDTH - 1 - k:CONV_WIDTH - k, :] * _shift_down(xr, k, n)
    return xc


def _gates(xc, wga_ref, bga_ref, wgx_ref, bgx_ref, l_ref):
    xcb = xc.astype(BF16)
    r = _sigmoid(_dot(xcb, wga_ref[...]) + bga_ref[...])
    ig = _sigmoid(_dot(xcb, wgx_ref[...]) + bgx_ref[...])
    ls = _log_sigmoid(l_ref[...])
    log_a = RG_C * r * ls
    a = jnp.exp(log_a)
    mult = jnp.sqrt(-_expm1(2.0 * log_a))
    return xcb, r, ig, ls, log_a, a, mult


SCAN_UNROLL = 4


def _scan_rows(a_s, u_s, out_ref, n, reverse):
    nt = n // SUBLANES
    per = SCAN_UNROLL if nt % SCAN_UNROLL == 0 else 1
    row = lax.broadcasted_iota(jnp.int32, (SUBLANES, LANES), 0)
    last = 0 if reverse else SUBLANES - 1

    def tile_scan(a, u):
        for d in (1, 2, 4):
            if reverse:
                keep = row < SUBLANES - d
                sh = SUBLANES - d
            else:
                keep = row >= d
                sh = d
            a_sh = jnp.where(keep, pltpu.roll(a, sh, 0), 1.0)
            u_sh = jnp.where(keep, pltpu.roll(u, sh, 0), 0.0)
            u = a * u_sh + u
            a = a * a_sh
        return a, u

    def step(t, carry):
        tiles = []
        for k in range(per):
            tt = t * per + k
            if reverse:
                tt = nt - 1 - tt
            off = pl.multiple_of(tt * SUBLANES, SUBLANES)
            a, u = tile_scan(a_s[pl.ds(off, SUBLANES), :], u_s[pl.ds(off, SUBLANES), :])
            tiles.append((off, a, u))
        for off, a, u in tiles:
            out_ref[pl.ds(off, SUBLANES), :] = u + a * carry
            carry = u[last:last + 1, :] + a[last:last + 1, :] * carry
        return carry

    lax.fori_loop(0, nt // per, step, jnp.zeros((1, LANES), F32))


def rec_fwd(rest, convw, convb, wga, bga, wgx, bgx, lru, rw):
    tp = rest.shape[0]
    ng = rw // LANES

    def body(xr_ref, yr_ref, cw_ref, cb_ref, wga_ref, bga_ref, wgx_ref, bgx_ref, l_ref,
             rec_ref, hr_ref, xc_ref, r_ref, ig_ref, a_ref, mult_ref, u_s):
        xc = _conv_fwd(xr_ref[...], cw_ref, cb_ref, tp)
        xc_ref[...] = xc
        _, r, ig, ls, log_a, a, mult = _gates(xc, wga_ref, bga_ref, wgx_ref, bgx_ref, l_ref)
        r_ref[...] = r
        ig_ref[...] = ig
        a_ref[...] = a
        mult_ref[...] = mult
        u_s[...] = mult * ig * xc
        _scan_rows(a_ref, u_s, hr_ref, tp, reverse=False)
        rec_ref[...] = hr_ref[...] * _gelu(yr_ref[...])

    col = lambda g: (0, g)
    vec = pl.BlockSpec((1, LANES), col)
    big = pl.BlockSpec((tp, LANES), col)
    return pl.pallas_call(
        body, name="rec_fwd", grid=(ng,),
        in_specs=[big, pl.BlockSpec((tp, LANES), lambda g: (0, ng + g)),
                  pl.BlockSpec((CONV_WIDTH, LANES), col), vec,
                  pl.BlockSpec((None, LANES, LANES), lambda g: (g, 0, 0)), vec,
                  pl.BlockSpec((None, LANES, LANES), lambda g: (g, 0, 0)), vec, vec],
        out_specs=[big] * 7,
        out_shape=[jax.ShapeDtypeStruct((tp, rw), F32)] * 7,
        scratch_shapes=[pltpu.VMEM((tp, LANES), F32)],
        compiler_params=_cparams("parallel"),
    )(rest, rest, convw, convb, wga, bga, wgx, bgx, lru)


def out_mlp_fwd(h, o, rec, ga, gr, wout, g2, gup, gdown, tm):
    tp, d = h.shape
    aw, rw = o.shape[1], rec.shape[1]
    nf = gup.shape[0]
    tf = gup.shape[2]
    nb = MLP_BLOCKS if nf % MLP_BLOCKS == 0 else 1
    nj = nf // nb

    def body(h_ref, o_ref, rec_ref, ga_ref, gr_ref, w_ref, g2_ref, wu_ref, wd_ref,
             h2_ref, mix_ref, z2_ref, u_ref, h3_ref, acc):
        j = pl.program_id(1)

        @pl.when(j == 0)
        def _():
            mix_ref[:, 0:aw] = _rms_fwd(o_ref[...], ga_ref[...]).astype(BF16)
            mix_ref[:, aw:aw + rw] = _rms_fwd(rec_ref[...], gr_ref[...]).astype(BF16)
            h2 = h_ref[...] + _dot(mix_ref[...], w_ref[...])
            h2_ref[...] = h2
            acc[...] = h2
            z2_ref[...] = _rms_fwd(h2, g2_ref[...]).astype(BF16)

        z = z2_ref[...]
        part = None
        for b in range(nb):
            u = jnp.maximum(_dot(z, wu_ref[b]), 0.0)
            u_ref[:, b * tf:(b + 1) * tf] = u.astype(BF16)
            p = _dot((u * u).astype(BF16), wd_ref[b])
            part = p if part is None else part + p
        acc[...] += part

        @pl.when(j == nj - 1)
        def _():
            h3_ref[...] = acc[...]

    row = lambda i, j: (i, 0)
    fix = lambda i, j: (0, 0)
    return pl.pallas_call(
        body, name="out_mlp_fwd", grid=(tp // tm, nj),
        in_specs=[pl.BlockSpec((tm, d), row), pl.BlockSpec((tm, aw), row), pl.BlockSpec((tm, rw), row),
                  pl.BlockSpec((1, aw), fix), pl.BlockSpec((1, rw), fix),
                  pl.BlockSpec((d, d), fix), pl.BlockSpec((1, d), fix),
                  pl.BlockSpec((nb, d, tf), lambda i, j: (j, 0, 0)),
                  pl.BlockSpec((nb, tf, d), lambda i, j: (j, 0, 0))],
        out_specs=[pl.BlockSpec((tm, d), row), pl.BlockSpec((tm, d), row), pl.BlockSpec((tm, d), row),
                   pl.BlockSpec((tm, nb * tf), lambda i, j: (i, j)), pl.BlockSpec((tm, d), row)],
        out_shape=[jax.ShapeDtypeStruct((tp, d), F32), jax.ShapeDtypeStruct((tp, d), BF16),
                   jax.ShapeDtypeStruct((tp, d), BF16), jax.ShapeDtypeStruct((tp, nf * tf), BF16),
                   jax.ShapeDtypeStruct((tp, d), F32)],
        scratch_shapes=[pltpu.VMEM((tm, d), F32)],
        compiler_params=_cparams("parallel", "arbitrary"),
    )(h, o, rec, ga, gr, wout, g2, gup, gdown)


MLP_BLOCKS = 4


def loss_head(h, gf, tgt, t_real, tm):
    tp, d = h.shape

    def body(h_ref, g_ref, t_ref, dh_ref, dg_ref, loss_ref):
        i = pl.program_id(0)
        x = h_ref[...]
        g = g_ref[...]
        r = lax.rsqrt(jnp.mean(x * x, axis=-1, keepdims=True) + NORM_EPS)
        xn = x * r
        rows = i * tm + lax.broadcasted_iota(jnp.int32, (tm, 1), 0)
        valid = jnp.logical_and(rows >= N_META, rows < t_real)
        e = jnp.where(valid, xn * g - t_ref[...], 0.0)
        part = 0.5 * jnp.sum(jnp.sum(e * e, axis=1, keepdims=True) / d, axis=0, keepdims=True)
        dy = e / d
        dxn = dy * g
        dh_ref[...] = r * (dxn - xn * jnp.mean(dxn * xn, axis=-1, keepdims=True))
        _accumulate(dg_ref, jnp.sum(dy * xn, axis=0, keepdims=True), i == 0)
        _accumulate(loss_ref, jnp.broadcast_to(part, (1, LANES)), i == 0)

    row = lambda i: (i, 0)
    fix = lambda i: (0, 0)
    return pl.pallas_call(
        body, name="loss_head", grid=(tp // tm,),
        in_specs=[pl.BlockSpec((tm, d), row), pl.BlockSpec((1, d), fix), pl.BlockSpec((tm, d), row)],
        out_specs=[pl.BlockSpec((tm, d), row), pl.BlockSpec((1, d), fix), pl.BlockSpec((1, LANES), fix)],
        out_shape=[jax.ShapeDtypeStruct((tp, d), F32), jax.ShapeDtypeStruct((1, d), F32),
                   jax.ShapeDtypeStruct((1, LANES), F32)],
        compiler_params=_cparams("arbitrary"),
    )(h, gf, tgt)


def mlp_bwd(dh, u, h2, g2, gup, gdown, tm):
    tp, d = dh.shape
    nf = gup.shape[0]
    tf = gup.shape[2]
    nb = MLP_BLOCKS if nf % MLP_BLOCKS == 0 else 1
    nj = nf // nb
    ni = tp // tm

    def body(dh_ref, u_ref, h2_ref, g_ref, wu_ref, wd_ref, dup_ref, dh2_ref, dg_ref, dhb, acc):
        i = pl.program_id(0)
        j = pl.program_id(1)

        @pl.when(j == 0)
        def _():
            dhb[...] = dh_ref[...].astype(BF16)

        part = None
        for b in range(nb):
            cols = slice(b * tf, (b + 1) * tf)
            dup = (_dot_nt(dhb[...], wd_ref[b]) * (2.0 * u_ref[:, cols].astype(F32))).astype(BF16)
            dup_ref[:, cols] = dup
            p = _dot_nt(dup, wu_ref[b])
            part = p if part is None else part + p
        _accumulate(acc, part, j == 0)

        @pl.when(j == nj - 1)
        def _():
            dx, dg = _rms_bwd(h2_ref[...], g_ref[...], acc[...])
            dh2_ref[...] = dh_ref[...] + dx
            _accumulate(dg_ref, dg, i == 0)

    return pl.pallas_call(
        body, name="mlp_bwd", grid=(ni, nj),
        in_specs=[pl.BlockSpec((tm, d), lambda i, j: (i, 0)),
                  pl.BlockSpec((tm, nb * tf), lambda i, j: (i, j)),
                  pl.BlockSpec((tm, d), lambda i, j: (i, 0)),
                  pl.BlockSpec((1, d), lambda i, j: (0, 0)),
                  pl.BlockSpec((nb, d, tf), lambda i, j: (j, 0, 0)),
                  pl.BlockSpec((nb, tf, d), lambda i, j: (j, 0, 0))],
        out_specs=[pl.BlockSpec((tm, nb * tf), lambda i, j: (i, j)),
                   pl.BlockSpec((tm, d), lambda i, j: (i, 0)),
                   pl.BlockSpec((1, d), lambda i, j: (0, 0)),
                   pl.BlockSpec((tm, d), lambda i, j: (i, 0))],
        out_shape=[jax.ShapeDtypeStruct((tp, nf * tf), BF16), jax.ShapeDtypeStruct((tp, d), F32),
                   jax.ShapeDtypeStruct((1, d), F32), jax.ShapeDtypeStruct((tp, d), BF16)],
        scratch_shapes=[pltpu.VMEM((tm, d), F32)],
        compiler_params=_cparams("arbitrary", "arbitrary"),
    )(dh, u, h2, g2, gup, gdown)


def dw_mlp(u, dhb, z2, dup, tf):
    rows, dff = u.shape
    d = z2.shape[1]
    nf = dff // tf

    def body(u_ref, dh_ref, z_ref, dup_ref, dwd_ref, dwu_ref, zt):
        @pl.when(pl.program_id(0) == 0)
        def _():
            zt[...] = z_ref[...].T

        uf = u_ref[...].astype(F32)
        dwd_ref[...] = _dot_tn((uf * uf).astype(BF16), dh_ref[...]).astype(BF16)
        dwu_ref[...] = _dot(zt[...], dup_ref[...]).astype(BF16)

    col = lambda j: (0, j)
    fix = lambda j: (0, 0)
    return pl.pallas_call(
        body, name="dw_mlp", grid=(nf,),
        in_specs=[pl.BlockSpec((rows, tf), col), pl.BlockSpec((rows, d), fix),
                  pl.BlockSpec((rows, d), fix), pl.BlockSpec((rows, tf), col)],
        out_specs=[pl.BlockSpec((None, tf, d), lambda j: (j, 0, 0)),
                   pl.BlockSpec((None, d, tf), lambda j: (j, 0, 0))],
        out_shape=[jax.ShapeDtypeStruct((nf, tf, d), BF16), jax.ShapeDtypeStruct((nf, d, tf), BF16)],
        scratch_shapes=[pltpu.VMEM((d, rows), BF16)],
        compiler_params=_cparams("arbitrary"),
    )(u, dhb, z2, dup)


def out_proj_bwd(dh2, mix, o, rec, ga, gr, wout, tm):
    tp, d = dh2.shape
    aw, rw = o.shape[1], rec.shape[1]
    ni = tp // tm

    def body(dh_ref, mix_ref, o_ref, rec_ref, ga_ref, gr_ref, w_ref, do_ref, drec_ref, dga_ref, dgr_ref, dw_ref, acc):
        i = pl.program_id(0)
        dhb = dh_ref[...].astype(BF16)
        dmix = _dot_nt(dhb, w_ref[...])
        do, dga = _rms_bwd(o_ref[...], ga_ref[...], dmix[:, 0:aw])
        drec, dgr = _rms_bwd(rec_ref[...], gr_ref[...], dmix[:, aw:aw + rw])
        do_ref[...] = do
        drec_ref[...] = drec
        _accumulate(dga_ref, dga, i == 0)
        _accumulate(dgr_ref, dgr, i == 0)
        _accumulate(acc, _dot_tn(mix_ref[...], dhb), i == 0)

        @pl.when(i == ni - 1)
        def _():
            dw_ref[...] = acc[...].astype(BF16)

    row = lambda i: (i, 0)
    fix = lambda i: (0, 0)
    return pl.pallas_call(
        body, name="out_proj_bwd", grid=(ni,),
        in_specs=[pl.BlockSpec((tm, d), row), pl.BlockSpec((tm, d), row),
                  pl.BlockSpec((tm, aw), row), pl.BlockSpec((tm, rw), row),
                  pl.BlockSpec((1, aw), fix), pl.BlockSpec((1, rw), fix), pl.BlockSpec((d, d), fix)],
        out_specs=[pl.BlockSpec((tm, aw), row), pl.BlockSpec((tm, rw), row),
                   pl.BlockSpec((1, aw), fix), pl.BlockSpec((1, rw), fix), pl.BlockSpec((d, d), fix)],
        out_shape=[jax.ShapeDtypeStruct((tp, aw), F32), jax.ShapeDtypeStruct((tp, rw), F32),
                   jax.ShapeDtypeStruct((1, aw), F32), jax.ShapeDtypeStruct((1, rw), F32),
                   jax.ShapeDtypeStruct((d, d), BF16)],
        scratch_shapes=[pltpu.VMEM((d, d), F32)],
        compiler_params=_cparams("arbitrary"),
    )(dh2, mix, o, rec, ga, gr, wout)


def rec_bwd(drec, hr, xc, gates, rest, convw, wga, wgx, lru, rw):
    tp = rest.shape[0]
    ng = rw // LANES

    def body(drec_ref, hr_ref, xc_ref, r_ref, ig_ref, a_ref, mult_ref, xr_ref, yr_ref, cw_ref, wga_ref, wgx_ref,
             l_ref, dxr_ref, dyr_ref, dwga_ref, dwgx_ref, vec_ref, a_s, u_s, lam_s):
        xc = xc_ref[...]
        h = hr_ref[...]
        drec = drec_ref[...]
        r, ig, a, mult = r_ref[...], ig_ref[...], a_ref[...], mult_ref[...]
        xcb = xc.astype(BF16)
        ls = _log_sigmoid(l_ref[...])
        gelu, gelu_grad = _gelu_and_grad(yr_ref[...])
        dyr_ref[...] = (drec * h * gelu_grad).astype(BF16)
        a_s[...] = _shift_up(a, 1, tp)
        u_s[...] = drec * gelu
        _scan_rows(a_s, u_s, lam_s, tp, reverse=True)
        lam = lam_s[...]
        da = lam * _shift_down(h, 1, tp)
        dmult = lam * ig * xc
        dig = lam * mult * xc
        dxc = lam * mult * ig
        dlog_a = da * a - dmult * (a * a) / mult
        dr = dlog_a * (RG_C * ls)
        dl = jnp.sum(dlog_a * (RG_C * r), axis=0, keepdims=True) * _sigmoid(-l_ref[...])
        dpa = dr * r * (1.0 - r)
        dpx = dig * ig * (1.0 - ig)
        dpab = dpa.astype(BF16)
        dpxb = dpx.astype(BF16)
        dxc = dxc + _dot_nt(dpab, wga_ref[...]) + _dot_nt(dpxb, wgx_ref[...])
        dwga_ref[...] = _dot_tn(xcb, dpab)
        dwgx_ref[...] = _dot_tn(xcb, dpxb)
        xr = xr_ref[...]
        dxr = cw_ref[CONV_WIDTH - 1:CONV_WIDTH, :] * dxc
        for k in range(1, CONV_WIDTH):
            dxr = dxr + cw_ref[CONV_WIDTH - 1 - k:CONV_WIDTH - k, :] * _shift_up(dxc, k, tp)
        dxr_ref[...] = dxr.astype(BF16)
        for k in range(CONV_WIDTH):
            vec_ref[k:k + 1, :] = jnp.sum(dxc * _shift_down(xr, CONV_WIDTH - 1 - k, tp), axis=0, keepdims=True)
        vec_ref[4:5, :] = jnp.sum(dxc, axis=0, keepdims=True)
        vec_ref[5:6, :] = jnp.sum(dpa, axis=0, keepdims=True)
        vec_ref[6:7, :] = jnp.sum(dpx, axis=0, keepdims=True)
        vec_ref[7:8, :] = dl

    col = lambda g: (0, g)
    vec = pl.BlockSpec((1, LANES), col)
    big = pl.BlockSpec((tp, LANES), col)
    sq = pl.BlockSpec((None, LANES, LANES), lambda g: (g, 0, 0))
    return pl.pallas_call(
        body, name="rec_bwd", grid=(ng,),
        in_specs=[big] * 8 + [pl.BlockSpec((tp, LANES), lambda g: (0, ng + g)),
                                pl.BlockSpec((CONV_WIDTH, LANES), col), sq, sq, vec],
        out_specs=[big, big, sq, sq, pl.BlockSpec((None, SUBLANES, LANES), lambda g: (g, 0, 0))],
        out_shape=[jax.ShapeDtypeStruct((tp, rw), BF16), jax.ShapeDtypeStruct((tp, rw), BF16),
                   jax.ShapeDtypeStruct((ng, LANES, LANES), F32), jax.ShapeDtypeStruct((ng, LANES, LANES), F32),
                   jax.ShapeDtypeStruct((ng, SUBLANES, LANES), F32)],
        scratch_shapes=[pltpu.VMEM((tp, LANES), F32)] * 3,
        compiler_params=_cparams("parallel"),
    )(drec, hr, xc, *gates, rest, rest, convw, wga, wgx, lru)


def attn_bwd(qkv, do, o, lset, c, ct, nh):
    tp = qkv.shape[0]
    npair = nh // 2
    aw = nh * HEAD_DIM
    tiles = _att_tiles(tp)

    def body(q_ref, k_ref, v_ref, do_ref, o_ref, lset_ref, c_ref, ct_ref,
             dq_ref, dk_ref, dv_ref, drow_ref, dcol_ref, dk_acc, dv_acc, dq_t):
        p = pl.program_id(0)
        k_t = k_ref[...].T
        dk_acc[...] = jnp.zeros_like(dk_acc)
        dv_acc[...] = jnp.zeros_like(dv_acc)
        dcol_ref[...] = jnp.zeros_like(dcol_ref)
        drow_ref[...] = jnp.zeros_like(drow_ref)
        for r0, nr, nk in tiles:
            rs = slice(r0, r0 + nr)
            causal = (r0 + lax.broadcasted_iota(jnp.int32, (nk, nr), 1)
                      >= lax.broadcasted_iota(jnp.int32, (nk, nr), 0))
            cblk = c_ref[0:nk, :]
            ctb = ct_ref[:, rs]
            for hh in range(2):
                head = 2 * p + hh
                hs = slice(hh * HEAD_DIM, (hh + 1) * HEAD_DIM)
                q = q_ref[rs, hs]
                k = k_ref[0:nk, hs]
                dof = do_ref[rs, hs]
                do16 = dof.astype(BF16)
                delta = jnp.sum(dof * o_ref[rs, hs], axis=1, keepdims=True)
                delta_row = jnp.broadcast_to(delta, (nr, LANES)).T[0:1, :]
                s_t = _dot_nt(k, q * ATT_SCALE) + (_pick_row(ctb, head) - _pick_col(cblk, head))
                p_t = jnp.where(causal, jnp.exp(s_t - lset_ref[hh:hh + 1, rs]), 0.0)
                ds_t = p_t * (_dot_nt(v_ref[0:nk, hs], do16) - delta_row)
                p16 = p_t.astype(BF16)
                ds16 = ds_t.astype(BF16)
                dv_acc[0:nk, hs] += _dot(p16, do16)
                dk_acc[0:nk, hs] += _dot(ds16, q) * ATT_SCALE
                dq_t[hs, rs] = _dot(k_t[hs, 0:nk], ds16)
                drow_ref[hh:hh + 1, rs] = jnp.sum(ds_t, axis=0, keepdims=True)
                dcol_ref[0:nk, hs] -= jnp.broadcast_to(jnp.sum(ds_t, axis=1, keepdims=True), (nk, HEAD_DIM))
        dk_ref[...] = dk_acc[...].astype(BF16)
        dv_ref[...] = dv_acc[...].astype(BF16)
        dq_ref[...] = (dq_t[...].T * ATT_SCALE).astype(BF16)

    pair = lambda p: (0, p)
    return pl.pallas_call(
        body, name="attn_bwd", grid=(npair,),
        in_specs=[pl.BlockSpec((tp, LANES), pair),
                  pl.BlockSpec((tp, LANES), lambda p: (0, npair + p)),
                  pl.BlockSpec((tp, LANES), lambda p: (0, 2 * npair + p)),
                  pl.BlockSpec((tp, LANES), pair),
                  pl.BlockSpec((tp, LANES), pair),
                  pl.BlockSpec((None, SUBLANES, tp), lambda p: (p, 0, 0)),
                  pl.BlockSpec((tp, LANES), lambda p: (0, 0)),
                  pl.BlockSpec((SUBLANES, tp), lambda p: (0, 0))],
        out_specs=[pl.BlockSpec((tp, LANES), pair), pl.BlockSpec((tp, LANES), pair),
                   pl.BlockSpec((tp, LANES), pair),
                   pl.BlockSpec((None, SUBLANES, tp), lambda p: (p, 0, 0)),
                   pl.BlockSpec((tp, LANES), pair)],
        out_shape=[jax.ShapeDtypeStruct((tp, aw), BF16), jax.ShapeDtypeStruct((tp, aw), BF16),
                   jax.ShapeDtypeStruct((tp, aw), BF16),
                   jax.ShapeDtypeStruct((npair, SUBLANES, tp), F32),
                   jax.ShapeDtypeStruct((tp, aw), F32)],
        scratch_shapes=[pltpu.VMEM((tp, LANES), F32), pltpu.VMEM((tp, LANES), F32),
                        pltpu.VMEM((LANES, tp), F32)],
        compiler_params=_cparams("parallel"),
    )(qkv, qkv, qkv, do, o, lset, c, ct)


def fgate_bwd(dct8, drs, rest, bf_pad, fcol):
    tp = rest.shape[0]
    aw = drs.shape[1]
    nb = tp // ATT_BLOCK
    B = ATT_BLOCK

    def body(d_ref, drs_ref, f_ref, b_ref, dfl_ref, db_ref, pad_s, dc_s):
        r_i = lax.broadcasted_iota(jnp.int32, (B, B), 0)
        c_i = lax.broadcasted_iota(jnp.int32, (B, B), 1)
        triu = (c_i >= r_i).astype(BF16)
        sel = (lax.broadcasted_iota(jnp.int32, (aw, LANES), 0)
               == HEAD_DIM * lax.broadcasted_iota(jnp.int32, (aw, LANES), 1)).astype(BF16)
        pad_s[...] = jnp.zeros_like(pad_s)
        pad_s[0:SUBLANES, :] = d_ref[...]
        dc_s[...] = pad_s[...].T + _dot_split3(drs_ref[...], sel)
        carry = jnp.zeros((1, LANES), F32)
        for i in range(nb - 1, -1, -1):
            sl = slice(i * B, (i + 1) * B)
            rc = _split3_dot(triu, dc_s[sl, :])
            dc_s[sl, :] = rc + carry
            carry = carry + rc[0:1, :]
        dfl = dc_s[...] * _sigmoid(-(f_ref[...] + b_ref[...]))
        dfl_ref[...] = dfl.astype(BF16)
        db_ref[...] = jnp.sum(dfl, axis=0, keepdims=True)

    return pl.pallas_call(
        body, name="fgate_bwd", grid=(1,),
        in_specs=[pl.BlockSpec((SUBLANES, tp), lambda i: (0, 0)),
                  pl.BlockSpec((tp, aw), lambda i: (0, 0)),
                  pl.BlockSpec((tp, LANES), lambda i: (0, fcol)),
                  pl.BlockSpec((1, LANES), lambda i: (0, 0))],
        out_specs=[pl.BlockSpec((tp, LANES), lambda i: (0, 0)),
                   pl.BlockSpec((1, LANES), lambda i: (0, 0))],
        out_shape=[jax.ShapeDtypeStruct((tp, LANES), BF16), jax.ShapeDtypeStruct((1, LANES), F32)],
        scratch_shapes=[pltpu.VMEM((LANES, tp), F32), pltpu.VMEM((tp, LANES), F32)],
        compiler_params=_cparams("arbitrary"),
    )(dct8, drs, rest, bf_pad)


def in_proj_bwd(dh2, parts, w_in_t, wrest_t, h, g1, tm):
    tp, d = h.shape
    dq, dk, dv, dxr, dyr, dfl = parts
    aw, rw = dq.shape[1], dxr.shape[1]

    def body(dh2_ref, dq_ref, dk_ref, dv_ref, dxr_ref, dyr_ref, dfl_ref, wq_ref, wr_ref, h_ref, g_ref,
             dh_ref, dg_ref):
        i = pl.program_id(0)
        dz = _dot(dq_ref[...], wq_ref[0:aw, :])
        dz += _dot(dk_ref[...], wq_ref[aw:2 * aw, :])
        dz += _dot(dv_ref[...], wq_ref[2 * aw:3 * aw, :])
        dz += _dot(dxr_ref[...], wr_ref[0:rw, :])
        dz += _dot(dyr_ref[...], wr_ref[rw:2 * rw, :])
        dz += _dot(dfl_ref[...], wr_ref[2 * rw:2 * rw + LANES, :])
        dx, dg = _rms_bwd(h_ref[...], g_ref[...], dz)
        dh_ref[...] = dh2_ref[...] + dx
        _accumulate(dg_ref, dg, i == 0)

    row = lambda i: (i, 0)
    fix = lambda i: (0, 0)
    return pl.pallas_call(
        body, name="in_proj_bwd", grid=(tp // tm,),
        in_specs=[pl.BlockSpec((tm, d), row),
                  pl.BlockSpec((tm, aw), row), pl.BlockSpec((tm, aw), row), pl.BlockSpec((tm, aw), row),
                  pl.BlockSpec((tm, rw), row), pl.BlockSpec((tm, rw), row), pl.BlockSpec((tm, LANES), row),
                  pl.BlockSpec((3 * aw, d), fix), pl.BlockSpec(wrest_t.shape, fix),
                  pl.BlockSpec((tm, d), row), pl.BlockSpec((1, d), fix)],
        out_specs=[pl.BlockSpec((tm, d), row), pl.BlockSpec((1, d), fix)],
        out_shape=[jax.ShapeDtypeStruct((tp, d), F32), jax.ShapeDtypeStruct((1, d), F32)],
        compiler_params=_cparams("arbitrary"),
    )(dh2, dq, dk, dv, dxr, dyr, dfl, w_in_t, wrest_t, h, g1)


def dw_in_t(z, parts, nh, tr):
    tp, d = z.shape
    dq, dk, dv, dxr, dyr, dfl = parts
    aw, rw = dq.shape[1], dxr.shape[1]
    d_in = 3 * aw + nh + 2 * rw
    blk = d_in // N_DEV
    nr = tp // tr
    offs = [(0, aw), (aw, aw), (2 * aw, aw), (3 * aw + nh, rw), (3 * aw + nh + rw, rw)]

    def body(z_ref, dq_ref, dk_ref, dv_ref, dxr_ref, dyr_ref, dfl_ref, o_ref, acc):
        r = pl.program_id(0)

        @pl.when(r == 0)
        def _():
            acc[...] = jnp.zeros_like(acc)

        zt = z_ref[...]
        for (o, n), ref in zip(offs, (dq_ref, dk_ref, dv_ref, dxr_ref, dyr_ref)):
            acc[o:o + n, :] += _dot_tn(ref[...], zt)
        acc[3 * aw:3 * aw + nh, :] += _dot_tn(dfl_ref[...], zt)[0:nh, :]

        @pl.when(r == nr - 1)
        def _():
            for p in range(N_DEV):
                o_ref[p] = acc[p * blk:(p + 1) * blk, :].astype(BF16)

    row = lambda r: (r, 0)
    return pl.pallas_call(
        body, name="dw_in", grid=(nr,),
        in_specs=[pl.BlockSpec((tr, d), row),
                  pl.BlockSpec((tr, aw), row), pl.BlockSpec((tr, aw), row), pl.BlockSpec((tr, aw), row),
                  pl.BlockSpec((tr, rw), row), pl.BlockSpec((tr, rw), row), pl.BlockSpec((tr, LANES), row)],
        out_specs=pl.BlockSpec((N_DEV, blk, d), lambda r: (0, 0, 0)),
        out_shape=jax.ShapeDtypeStruct((N_DEV, blk, d), BF16),
        scratch_shapes=[pltpu.VMEM((d_in, d), F32)],
        compiler_params=_cparams("arbitrary"),
    )(z, dq, dk, dv, dxr, dyr, dfl)


def _place():
    return lax.axis_index("x"), lax.axis_index("y"), lax.axis_index("c")


HBM = pl.BlockSpec(memory_space=pltpu.HBM)
SEM = pl.BlockSpec(memory_space=pltpu.SEMAPHORE)
EFFECT = pltpu.SideEffectType.DATAFLOW_SIDE_EFFECTING


def _in_hbm(a):
    return pltpu.with_memory_space_constraint(a, pltpu.HBM)


def _as_list(a):
    return list(a) if isinstance(a, (list, tuple)) else [a]


def _gather_targets(x, y, c):
    return [(x, y, 1 - c), (1 - x, y, c), (x, 1 - y, c), (1 - x, 1 - y, c)]


def _slot(t):
    return 4 * t[0] + 2 * t[1] + t[2]


def gather_start(groups, name):
    flat = [a for g in groups for a in g]
    n = len(flat)
    ng = len(groups)
    lands = [lax.empty((N_DEV,) + a.shape, a.dtype) for a in flat]

    def body(*refs):
        src, land = refs[:n], refs[n:2 * n]
        sems = refs[2 * n:2 * n + 2 * ng]
        token = refs[-1]
        x, y, c = _place()
        me = 4 * x + 2 * y + c
        i = 0
        for gi, g in enumerate(groups):
            for a in range(len(g)):
                for k, t in enumerate(_gather_targets(x, y, c)):
                    pltpu.make_async_remote_copy(
                        src_ref=src[i], dst_ref=land[i].at[me],
                        send_sem=sems[2 * gi].at[4 * a + k], recv_sem=sems[2 * gi + 1].at[4 * a + k],
                        device_id=t, device_id_type=MESH).start()
                i += 1
        token[...] = jnp.zeros_like(token)

    sem_shapes = []
    for g in groups:
        sem_shapes += [pltpu.SemaphoreType.DMA((4 * len(g),)), pltpu.SemaphoreType.DMA((4 * len(g),))]
    out = pl.pallas_call(
        body, name=name,
        out_shape=sem_shapes + [pltpu.HBM(a.shape, a.dtype) for a in flat + lands]
        + [jax.ShapeDtypeStruct((SUBLANES, LANES), F32)],
        in_specs=[HBM] * (2 * n),
        out_specs=[SEM] * (2 * ng) + [HBM] * (2 * n) + [pl.BlockSpec(memory_space=pltpu.VMEM)],
        input_output_aliases={i: 2 * ng + i for i in range(2 * n)},
        compiler_params=pltpu.CompilerParams(has_side_effects=EFFECT),
    )(*[_in_hbm(a) for a in flat + lands])
    sems = out[:2 * ng]
    thru = out[2 * ng:2 * ng + 2 * n]
    srcs_t, lands_t = thru[:n], thru[n:]
    res, i = [], 0
    for gi, g in enumerate(groups):
        res.append((sems[2 * gi], sems[2 * gi + 1], srcs_t[i:i + len(g)], lands_t[i:i + len(g)]))
        i += len(g)
    return res, out[-1]


def gather_wait(send, recv, srcs, lands, after, name):
    n = len(srcs)

    def body(*refs):
        src, land = refs[:n], refs[n:2 * n]
        send_sem, recv_sem = refs[2 * n], refs[2 * n + 1]
        x, y, c = _place()
        for a in range(n):
            for k, t in enumerate(_gather_targets(x, y, c)):
                cp = pltpu.make_async_remote_copy(
                    src_ref=src[a], dst_ref=land[a].at[_slot(t)],
                    send_sem=send_sem.at[4 * a + k], recv_sem=recv_sem.at[4 * a + k],
                    device_id=t, device_id_type=MESH)
                cp.wait_send()
                cp.wait_recv()

    out = pl.pallas_call(
        body, name=name,
        out_shape=[pltpu.HBM(a.shape, a.dtype) for a in list(srcs) + list(lands)],
        in_specs=[HBM] * (2 * n) + [SEM, SEM] + [ANY] * len(_as_list(after)),
        out_specs=[HBM] * (2 * n),
        input_output_aliases={i: i for i in range(2 * n)},
        compiler_params=pltpu.CompilerParams(has_side_effects=EFFECT),
    )(*srcs, *lands, send, recv, *_as_list(after))
    return out[:n], out[n:]


def forward_start(lands, name):
    n = len(lands)

    def body(*refs):
        land = refs[:n]
        send_sem, recv_sem = refs[n], refs[n + 1]
        token = refs[-1]
        x, y, c = _place()
        for a in range(n):
            for j, chip in enumerate([(1 - x, y), (x, 1 - y), (1 - x, 1 - y)]):
                blk = land[a].at[_slot((*chip, c))]
                pltpu.make_async_remote_copy(src_ref=blk, dst_ref=blk, send_sem=send_sem.at[3 * a + j],
                                             recv_sem=recv_sem.at[3 * a + j], device_id=(x, y, 1 - c),
                                             device_id_type=MESH).start()
        token[...] = jnp.zeros_like(token)

    out = pl.pallas_call(
        body, name=name,
        out_shape=[pltpu.SemaphoreType.DMA((3 * n,)), pltpu.SemaphoreType.DMA((3 * n,))]
        + [pltpu.HBM(a.shape, a.dtype) for a in lands] + [jax.ShapeDtypeStruct((SUBLANES, LANES), F32)],
        in_specs=[HBM] * n,
        out_specs=[SEM, SEM] + [HBM] * n + [pl.BlockSpec(memory_space=pltpu.VMEM)],
        input_output_aliases={i: 2 + i for i in range(n)},
        compiler_params=pltpu.CompilerParams(has_side_effects=EFFECT),
    )(*[_in_hbm(a) for a in lands])
    return out[0], out[1], out[2:2 + n], out[-1][0, 0]


def forward_wait(send, recv, lands, after, name):
    n = len(lands)

    def body(*refs):
        land = refs[:n]
        send_sem, recv_sem = refs[n], refs[n + 1]
        x, y, c = _place()
        for a in range(n):
            for j, chip in enumerate([(1 - x, y), (x, 1 - y), (1 - x, 1 - y)]):
                cp = pltpu.make_async_remote_copy(
                    src_ref=land[a].at[_slot((*chip, c))], dst_ref=land[a].at[_slot((*chip, 1 - c))],
                    send_sem=send_sem.at[3 * a + j], recv_sem=recv_sem.at[3 * a + j],
                    device_id=(x, y, 1 - c), device_id_type=MESH)
                cp.wait_send()
                cp.wait_recv()

    return pl.pallas_call(
        body, name=name,
        out_shape=[pltpu.HBM(a.shape, a.dtype) for a in lands],
        in_specs=[HBM] * n + [SEM, SEM, ANY],
        out_specs=[HBM] * n,
        input_output_aliases={i: i for i in range(n)},
        compiler_params=pltpu.CompilerParams(has_side_effects=EFFECT),
    )(*lands, send, recv, after)


def _relations():
    return [(dx, dy, dc) for dx in (0, 1) for dy in (0, 1) for dc in (0, 1) if dx + dy + dc]


def _peer(x, y, c, rel):
    return ((1 - x) if rel[0] else x, (1 - y) if rel[1] else y, (1 - c) if rel[2] else c)


def exchange_start(srcs, lands, layer, name, after=()):
    n = len(srcs)
    after = _as_list(after)

    def body(*refs):
        src, land = refs[:n], refs[n:2 * n]
        send_sem, recv_sem = refs[2 * n + len(after)], refs[2 * n + len(after) + 1]
        token = refs[-1]
        x, y, c = _place()
        me = 4 * x + 2 * y + c
        for k, rel in enumerate(_relations()):
            peer = _peer(x, y, c, rel)
            for a in range(n):
                pltpu.make_async_remote_copy(
                    src_ref=src[a] if layer is None else src[a].at[_slot(peer)],
                    dst_ref=land[a].at[me] if layer is None else land[a].at[me, layer],
                    send_sem=send_sem.at[7 * a + k], recv_sem=recv_sem.at[7 * a + k],
                    device_id=peer, device_id_type=MESH).start()
        token[...] = jnp.zeros_like(token)

    out = pl.pallas_call(
        body, name=name,
        out_shape=[pltpu.SemaphoreType.DMA((7 * n,)), pltpu.SemaphoreType.DMA((7 * n,))]
        + [pltpu.HBM(a.shape, a.dtype) for a in list(srcs) + list(lands)]
        + [jax.ShapeDtypeStruct((SUBLANES, LANES), F32)],
        in_specs=[HBM] * (2 * n) + [ANY] * len(after),
        out_specs=[SEM, SEM] + [HBM] * (2 * n) + [pl.BlockSpec(memory_space=pltpu.VMEM)],
        input_output_aliases={i: 2 + i for i in range(2 * n)},
        compiler_params=pltpu.CompilerParams(has_side_effects=EFFECT),
    )(*[_in_hbm(a) for a in list(srcs) + list(lands)], *after)
    return out[0], out[1], out[2:2 + n], out[2 + n:2 + 2 * n], out[-1][0, 0]


def exchange_wait(send, recv, srcs, lands, after, layer, name):
    n = len(srcs)

    def body(*refs):
        src, land = refs[:n], refs[n:2 * n]
        send_sem, recv_sem = refs[2 * n], refs[2 * n + 1]
        x, y, c = _place()
        for k, rel in enumerate(_relations()):
            peer = _peer(x, y, c, rel)
            for a in range(n):
                cp = pltpu.make_async_remote_copy(
                    src_ref=src[a] if layer is None else src[a].at[_slot(peer)],
                    dst_ref=land[a].at[_slot(peer)] if layer is None else land[a].at[_slot(peer), layer],
                    send_sem=send_sem.at[7 * a + k], recv_sem=recv_sem.at[7 * a + k],
                    device_id=peer, device_id_type=MESH)
                cp.wait_send()
                cp.wait_recv()

    out = pl.pallas_call(
        body, name=name,
        out_shape=[pltpu.HBM(a.shape, a.dtype) for a in list(srcs) + list(lands)],
        in_specs=[HBM] * (2 * n) + [SEM, SEM] + [ANY] * len(_as_list(after)),
        out_specs=[HBM] * (2 * n),
        input_output_aliases={i: i for i in range(2 * n)},
        compiler_params=pltpu.CompilerParams(has_side_effects=EFFECT),
    )(*srcs, *lands, send, recv, *_as_list(after))
    return out[:n], out[n:]


def _adamw_math(g, w, m, v):
    m = ADAM_B1 * m + (1.0 - ADAM_B1) * g
    v = ADAM_B2 * v + (1.0 - ADAM_B2) * (g * g)
    m_hat = m / (1.0 - ADAM_B1 ** ADAM_STEP)
    v_hat = v / (1.0 - ADAM_B2 ** ADAM_STEP)
    delta = -ADAM_LR * (m_hat / (jnp.sqrt(v_hat) + ADAM_EPS) + ADAM_WD * w)
    return delta, m, v


def _sum_with_own(p_ref, own_refs, layer, me):
    own = own_refs[0][...]
    for k in range(1, len(own_refs)):
        own = jnp.where(layer == k, own_refs[k][...], own)
    g = None
    for p in range(p_ref.shape[0]):
        term = jnp.where(me == p, own, p_ref[p]).astype(F32)
        g = term if g is None else g + term
    return g


def sum_adamw(parts, owns, me, w, m, v, tr, name):
    npart, rows, cols = parts.shape
    nl = len(owns)
    per_layer = rows // nl // tr

    def body(me_ref, p_ref, *refs):
        own_refs = refs[:nl]
        w_ref, m_ref, v_ref, g_ref, d_ref, nm_ref, nv_ref = refs[nl:]
        g = _sum_with_own(p_ref, own_refs, pl.program_id(0) // per_layer, me_ref[0])
        delta, nm, nv = _adamw_math(g, w_ref[...], m_ref[...], v_ref[...])
        g_ref[...] = g
        d_ref[...] = delta
        nm_ref[...] = nm
        nv_ref[...] = nv

    blk = pl.BlockSpec((tr, cols), lambda i, me_ref: (i, 0))
    own_specs = [pl.BlockSpec((None, tr, cols),
                              lambda i, me_ref, l=l: (me_ref[0], jnp.clip(i - l * per_layer, 0, per_layer - 1), 0))
                 for l in range(nl)]
    return pl.pallas_call(
        body, name=name,
        grid_spec=pltpu.PrefetchScalarGridSpec(
            num_scalar_prefetch=1, grid=(rows // tr,),
            in_specs=[pl.BlockSpec((npart, tr, cols), lambda i, me_ref: (0, i, 0))] + own_specs + [blk, blk, blk],
            out_specs=[blk] * 4),
        out_shape=[jax.ShapeDtypeStruct((rows, cols), F32)] * 4,
        compiler_params=_cparams("arbitrary"),
    )(me, parts, *owns, w, m, v)


def sum_adamw_t(parts, owns, me, w, m, v, name):
    npart, nl, rows, cols = parts.shape

    def body(me_ref, p_ref, *refs):
        own_refs = refs[:nl]
        w_ref, m_ref, v_ref, g_ref, d_ref, nm_ref, nv_ref = refs[nl:]
        g = _sum_with_own(p_ref, own_refs, pl.program_id(0), me_ref[0])
        delta, nm, nv = _adamw_math(g, w_ref[...], m_ref[...], v_ref[...])
        g_ref[...] = g
        d_ref[...] = delta
        nm_ref[...] = nm
        nv_ref[...] = nv

    blk = pl.BlockSpec((None, rows, cols), lambda l, me_ref: (l, 0, 0))
    own_specs = [pl.BlockSpec((None, rows, cols), lambda l, me_ref: (me_ref[0], 0, 0)) for _ in range(nl)]
    return pl.pallas_call(
        body, name=name,
        grid_spec=pltpu.PrefetchScalarGridSpec(
            num_scalar_prefetch=1, grid=(nl,),
            in_specs=[pl.BlockSpec((npart, None, rows, cols), lambda l, me_ref: (0, l, 0, 0))] + own_specs
            + [blk, blk, blk],
            out_specs=[blk] * 4),
        out_shape=[jax.ShapeDtypeStruct((nl, rows, cols), F32)] * 4,
        compiler_params=_cparams("arbitrary"),
    )(me, parts, *owns, w, m, v)


def adamw_group(gs, ws, ms, vs, name):
    n = len(gs)

    def body(*refs):
        g, w, m, v, outs = refs[:n], refs[n:2 * n], refs[2 * n:3 * n], refs[3 * n:4 * n], refs[4 * n:]
        for i in range(n):
            delta, nm, nv = _adamw_math(g[i][...], w[i][...], m[i][...], v[i][...])
            outs[i][...] = delta
            outs[n + i][...] = nm
            outs[2 * n + i][...] = nv

    vmem = pl.BlockSpec(memory_space=pltpu.VMEM)
    out = pl.pallas_call(
        body, name=name,
        in_specs=[vmem] * (4 * n), out_specs=[vmem] * (3 * n),
        out_shape=[jax.ShapeDtypeStruct(a.shape, F32) for a in list(ws) * 3],
        compiler_params=_cparams(),
    )(*gs, *ws, *ms, *vs)
    return out[:n], out[n:2 * n], out[2 * n:]


def sum_parts(parts, name):
    npart, rows, cols = parts.shape

    def body(p_ref, g_ref):
        g = p_ref[0].astype(F32)
        for p in range(1, npart):
            g = g + p_ref[p].astype(F32)
        g_ref[...] = g

    return pl.pallas_call(
        body, name=name, grid=(1,),
        in_specs=[pl.BlockSpec((npart, rows, cols), lambda i: (0, 0, 0))],
        out_specs=pl.BlockSpec((rows, cols), lambda i: (0, 0)),
        out_shape=jax.ShapeDtypeStruct((rows, cols), F32),
        compiler_params=_cparams("arbitrary"),
    )(parts)


def _round_up(n, m):
    return (n + m - 1) // m * m


def _block_diag_pairs(w):
    nb, b, _ = w.shape
    per = LANES // b
    ng = nb // per
    w = w.reshape(ng, per, b, b)
    eye = jnp.eye(per, dtype=w.dtype)
    out = jnp.einsum('gpij,pq->gpiqj', w, eye).reshape(ng, LANES, LANES)
    return out.astype(BF16)


def _block_diag_extract(g, b):
    ng = g.shape[0]
    per = LANES // b
    g = g.reshape(ng, per, b, per, b)
    idx = jnp.arange(per)
    return g[:, idx, :, idx, :].transpose(1, 0, 2, 3).reshape(ng * per, b, b)


def _tiles(v):
    v = v.reshape(-1)
    n = _round_up(v.shape[0], SUBLANES * LANES)
    return jnp.pad(v, (0, n - v.shape[0])).reshape(-1, LANES)


SMALL = ['attn_norm_g', 'b_f', 'conv_w', 'conv_b', 'w_gate_a', 'b_gate_a', 'w_gate_x', 'b_gate_x',
         'lru_L', 'attn_out_g', 'rec_out_g', 'mlp_norm_g', 'final_g', 'meta']


def _pack(d):
    return jnp.concatenate([_tiles(d[n]) for n in SMALL], axis=0)


def _unpack(vec, shapes):
    out, r = {}, 0
    for n in SMALL:
        size = math.prod(shapes[n])
        nr = _round_up(size, SUBLANES * LANES) // LANES
        out[n] = vec[r:r + nr].reshape(-1)[:size].reshape(shapes[n])
        r += nr
    return out


def _row_tile(tp):
    return tp // 4 if (tp // 4) % 16 == 0 else tp


def local_step(x, tgt, meta, small, hooks):
    s, d = x.shape
    t_real = s + N_META
    tp = _round_up(t_real, ATT_BLOCK)
    depth = small['attn_norm_g'].shape[0]
    nh = small['b_f'].shape[1]
    rw = small['conv_b'].shape[1]
    blk = small['w_gate_a'].shape[2]
    tm = _row_tile(tp)
    tm2 = tp // 2
    fcol = 2 * rw // LANES

    h = jnp.concatenate([meta, x, jnp.zeros((tp - t_real, d), F32)], axis=0)
    tgt_p = jnp.pad(tgt, ((N_META, tp - t_real), (0, 0)))
    row = lambda v: v.reshape(1, -1)
    bf_pad = jnp.pad(small['b_f'], ((0, 0), (0, LANES - nh)))

    saved = []
    for l in range(depth):
        w_in_t, wrest_t, tok_w = hooks.mixer_weights(l, h)
        wga = _block_diag_pairs(small['w_gate_a'][l])
        wgx = _block_diag_pairs(small['w_gate_x'][l])
        z, qkv, rest = in_proj(h, row(small['attn_norm_g'][l]) + tok_w, w_in_t, wrest_t, 3 * nh * HEAD_DIM, tm)
        c, ct = fgate_fwd(rest, bf_pad[l:l + 1], fcol)
        o, lset = attn_fwd(qkv, c, ct, nh)
        rec, hr, xc, *gates = rec_fwd(rest, small['conv_w'][l], row(small['conv_b'][l]), wga,
                                      row(small['b_gate_a'][l]), wgx, row(small['b_gate_x'][l]),
                                      row(small['lru_L'][l]), rw)
        wout, gup, gdown, tok_w = hooks.mlp_weights(l, rec)
        h2, mix, z2, u, h3 = out_mlp_fwd(h, o, rec, row(small['attn_out_g'][l]), row(small['rec_out_g'][l]), wout,
                                         row(small['mlp_norm_g'][l]) + tok_w, gup, gdown, tm)
        saved.append(dict(h=h, z=z, qkv=qkv, rest=rest, c=c, ct=ct, o=o, lset=lset, rec=rec, hr=hr, xc=xc,
                          h2=h2, mix=mix, z2=z2, u=u, wga=wga, wgx=wgx, gates=gates,
                          w_in_t=w_in_t, wrest_t=wrest_t, wout=wout, gup=gup, gdown=gdown))
        h = h3

    dh, dgf, loss = loss_head(h, row(small['final_g']), tgt_p, t_real, tm)

    gs = {n: [None] * depth for n in SMALL if n not in ('final_g', 'meta')}
    tok = jnp.zeros((), F32)
    for l in reversed(range(depth)):
        sv = saved[l]
        gup, gdown = sv['gup'], sv['gdown']
        tf = gup.shape[2]
        dup, dh2, dg2, dhb = mlp_bwd(dh, sv['u'], sv['h2'], row(small['mlp_norm_g'][l]) + tok, gup, gdown, tm)
        gs['mlp_norm_g'][l] = dg2[0]
        do, drec, dga, dgr, dw_out = out_proj_bwd(dh2, sv['mix'], sv['o'], sv['rec'], row(small['attn_out_g'][l]),
                                                  row(small['rec_out_g'][l]), sv['wout'], tm)
        gs['attn_out_g'][l] = dga[0]
        gs['rec_out_g'][l] = dgr[0]
        dw_down, dw_up = dw_mlp(sv['u'], dhb, sv['z2'], dup, tf)
        blocks = dict(w_down=dw_down, w_up=dw_up, w_out=dw_out.reshape(N_DEV, d // N_DEV, d))
        tok = hooks.grads_ready(l, 'mlp', blocks)
        dxr, dyr, dwga, dwgx, vec = rec_bwd(drec, sv['hr'], sv['xc'], sv['gates'], sv['rest'], small['conv_w'][l],
                                            sv['wga'], sv['wgx'], row(small['lru_L'][l]) + tok, rw)
        gs['w_gate_a'][l] = _block_diag_extract(dwga, blk)
        gs['w_gate_x'][l] = _block_diag_extract(dwgx, blk)
        vec = vec.transpose(1, 0, 2).reshape(SUBLANES, rw)
        gs['conv_w'][l] = vec[0:CONV_WIDTH]
        gs['conv_b'][l] = vec[4]
        gs['b_gate_a'][l] = vec[5]
        gs['b_gate_x'][l] = vec[6]
        gs['lru_L'][l] = vec[7]
        dq, dk, dv, drow, dcol = attn_bwd(sv['qkv'], do, sv['o'], sv['lset'], sv['c'], sv['ct'] + tok, nh)
        drow8 = drow[:, 0:2, :].reshape(nh, tp)
        if nh < SUBLANES:
            drow8 = jnp.pad(drow8, ((0, SUBLANES - nh), (0, 0)))
        dfl, dbf = fgate_bwd(drow8, dcol, sv['rest'], bf_pad[l:l + 1], fcol)
        gs['b_f'][l] = dbf[0, 0:nh]
        parts = (dq, dk, dv, dxr, dyr, dfl)
        dh, dg1 = in_proj_bwd(dh2, parts, sv['w_in_t'], sv['wrest_t'], sv['h'], row(small['attn_norm_g'][l]), tm)
        gs['attn_norm_g'][l] = dg1[0]
        first = ()
        if l == 0:
            grads = {n: jnp.stack(v) for n, v in gs.items()}
            grads['final_g'] = dgf[0]
            grads['meta'] = dh[0:N_META]
            first = hooks.small_ready(grads)
        dw_in = dw_in_t(sv['z'], parts, nh, tm2)
        tok = hooks.grads_ready(l, 'in', dict(w_in=dw_in), first)

    return loss[0, 0], dh


def prep_weights(g_in, nh, rw):
    d = g_in.shape[2]
    w_in_t = g_in.reshape(-1, d)
    f0 = 3 * nh * HEAD_DIM
    wrest_t = jnp.concatenate([w_in_t[f0 + nh:f0 + nh + 2 * rw],
                               jnp.pad(w_in_t[f0:f0 + nh], ((0, LANES - nh), (0, 0)))], axis=0)
    return w_in_t, wrest_t


BIG = ['w_in', 'w_out', 'w_up', 'w_down']
EXCHANGE_GROUPS = {'mlp': ['w_down', 'w_up', 'w_out'], 'in': ['w_in']}
WEIGHTS = ['meta', 'attn_norm_g', 'w_in', 'b_f', 'conv_w', 'conv_b', 'w_gate_a', 'b_gate_a', 'w_gate_x', 'b_gate_x',
           'lru_L', 'attn_out_g', 'rec_out_g', 'w_out', 'mlp_norm_g', 'w_up', 'w_down', 'final_g']


def _set_own(arr, own, me):
    return lax.dynamic_update_slice_in_dim(arr, own[None], me, axis=0)


class _Step:
    def __init__(self, w, nh, rw, me):
        self.w, self.nh, self.rw, self.me = w, nh, rw, me
        depth = w['w_in'].shape[0]
        first = [w['w_in_t'][:, 0, :].astype(BF16), w['meta'], w['conv_w']]
        self.pending, token = gather_start([first], "gather_start_0")
        zero = token[0, 0].astype(BF16)
        mlp_group = lambda l: [w[n][l].astype(BF16) + zero for n in ('w_out', 'w_up', 'w_down')]
        groups = [mlp_group(0)]
        for l in range(1, depth):
            groups.append([w['w_in_t'][:, l, :].astype(BF16) + zero])
            groups.append(mlp_group(l))
        rest, _ = gather_start(groups, "gather_start_1")
        self.pending += rest
        self.first_after = rest[0][2][0]
        self.gathered = {}
        self.passing = {}
        self.token = jnp.zeros((), F32)
        self.lands = {n: lax.empty((N_DEV,) + w[n].shape, BF16) for n in BIG}
        din8, _, d = w['w_in_t'].shape
        self.lands['w_in'] = lax.empty((N_DEV, depth, din8, d), BF16)
        self.started = []
        self.small = None

    def _pass_on(self, gi, after):
        if gi < len(self.pending) and gi not in self.passing:
            send, recv, srcs, lands = self.pending[gi]
            srcs, lands = gather_wait(send, recv, srcs, lands, after, "gather_wait_%d" % gi)
            fsend, frecv, lands, token = forward_start(lands, "forward_start_%d" % gi)
            self.passing[gi] = (fsend, frecv, srcs, lands)
            self.token = token

    def group(self, gi, after):
        if gi not in self.gathered:
            self._pass_on(gi, after)
            fsend, frecv, srcs, lands = self.passing[gi]
            lands = forward_wait(fsend, frecv, lands, after, "forward_wait_%d" % gi)
            self.gathered[gi] = [_set_own(g, own, self.me) for g, own in zip(lands, srcs)]
            if gi >= 2:
                self._pass_on(gi + 1, lands[0])
        return self.gathered[gi]

    def mixer_weights(self, l, after):
        g = self.group(2 * l, after)
        return (*prep_weights(g[0], self.nh, self.rw), self.token)

    def mlp_weights(self, l, after):
        g = self.group(2 * l + 1, after)
        d = g[0].shape[2]
        return g[0].reshape(d, d), g[1], g[2], self.token

    def grads_ready(self, l, group, blocks, after=()):
        names = EXCHANGE_GROUPS[group]
        send, recv, srcs, lands, token = exchange_start(
            [blocks[n] for n in names], [self.lands[n] for n in names], l, "exchange_start_%s_%d" % (group, l),
            after)
        for n, a in zip(names, lands):
            self.lands[n] = a
        self.started.append((l, group, send, recv, srcs))
        return token

    def small_ready(self, grads):
        self.small_shapes = {n: grads[n].shape for n in SMALL}
        packed = _pack(grads).astype(BF16)
        send, recv, srcs, lands, token = exchange_start(
            [packed], [lax.empty((N_DEV,) + packed.shape, BF16)], None, "small_start")
        self.small = (send, recv, srcs, lands)
        return srcs[0]

    def small_sum(self, after):
        send, recv, srcs, lands = self.small
        srcs, lands = exchange_wait(send, recv, srcs, lands, after, None, "small_wait")
        parts = _set_own(lands[0], srcs[0], self.me)
        return _unpack(sum_parts(parts, "sum_small_grads"), self.small_shapes)

    def received(self, group, after):
        names = EXCHANGE_GROUPS[group]
        own = {n: [None] * self.w[n].shape[0] for n in names}
        for l, grp, send, recv, srcs in self.started:
            if grp != group:
                continue
            srcs, lands = exchange_wait(send, recv, srcs, [self.lands[n] for n in names], after, l,
                                        "exchange_wait_%s_%d" % (group, l))
            for n, a, sr in zip(names, lands, srcs):
                self.lands[n] = a
                own[n][l] = sr
        return {n: (self.lands[n], own[n]) for n in names}


def kernel(x, meta, attn_norm_g, w_in, b_f, conv_w, conv_b, w_gate_a, b_gate_a, w_gate_x, b_gate_x, lru_L, attn_out_g, rec_out_g, w_out, mlp_norm_g, w_up, w_down, final_g, loss_target, m_meta, m_attn_norm_g, m_w_in, m_b_f, m_conv_w, m_conv_b, m_w_gate_a, m_b_gate_a, m_w_gate_x, m_b_gate_x, m_lru_L, m_attn_out_g, m_rec_out_g, m_w_out, m_mlp_norm_g, m_w_up, m_w_down, m_final_g, v_meta, v_attn_norm_g, v_w_in, v_b_f, v_conv_w, v_conv_b, v_w_gate_a, v_b_gate_a, v_w_gate_x, v_b_gate_x, v_lru_L, v_attn_out_g, v_rec_out_g, v_w_out, v_mlp_norm_g, v_w_up, v_w_down, v_final_g):
    w = dict(meta=meta, attn_norm_g=attn_norm_g, w_in=w_in, b_f=b_f, conv_w=conv_w, conv_b=conv_b,
             w_gate_a=w_gate_a, b_gate_a=b_gate_a, w_gate_x=w_gate_x, b_gate_x=b_gate_x, lru_L=lru_L,
             attn_out_g=attn_out_g, rec_out_g=rec_out_g, w_out=w_out, mlp_norm_g=mlp_norm_g, w_up=w_up,
             w_down=w_down, final_g=final_g)
    mo = dict(meta=m_meta, attn_norm_g=m_attn_norm_g, w_in=m_w_in, b_f=m_b_f, conv_w=m_conv_w, conv_b=m_conv_b,
              w_gate_a=m_w_gate_a, b_gate_a=m_b_gate_a, w_gate_x=m_w_gate_x, b_gate_x=m_b_gate_x, lru_L=m_lru_L,
              attn_out_g=m_attn_out_g, rec_out_g=m_rec_out_g, w_out=m_w_out, mlp_norm_g=m_mlp_norm_g,
              w_up=m_w_up, w_down=m_w_down, final_g=m_final_g)
    vo = dict(meta=v_meta, attn_norm_g=v_attn_norm_g, w_in=v_w_in, b_f=v_b_f, conv_w=v_conv_w, conv_b=v_conv_b,
              w_gate_a=v_w_gate_a, b_gate_a=v_b_gate_a, w_gate_x=v_w_gate_x, b_gate_x=v_b_gate_x, lru_L=v_lru_L,
              attn_out_g=v_attn_out_g, rec_out_g=v_rec_out_g, w_out=v_w_out, mlp_norm_g=v_mlp_norm_g,
              w_up=v_w_up, w_down=v_w_down, final_g=v_final_g)
    depth = w_in.shape[0]
    nh = b_f.shape[1]
    rw = conv_b.shape[1]
    me = 4 * lax.axis_index("x") + 2 * lax.axis_index("y") + lax.axis_index("c")

    w['w_in_t'] = jnp.transpose(w_in, (2, 0, 1))
    swap = lambda a: jnp.swapaxes(a, 1, 2)
    step = _Step(w, nh, rw, me)
    g0 = step.group(0, step.first_after)
    meta_full = g0[1].transpose(1, 0, 2).reshape(N_META, -1)
    conv_full = g0[2].transpose(1, 2, 0, 3).reshape(depth, CONV_WIDTH, rw)
    small = {n: w[n] for n in SMALL}
    small['conv_w'] = conv_full

    loss_part, dh0 = local_step(x[0], loss_target[0], meta_full, small, step)
    loss = lax.psum(loss_part, ("x", "y", "c"))
    grad_x = dh0[N_META:N_META + x.shape[1]][None]

    out_g, out_d, out_m, out_v = {}, {}, {}, {}
    me1 = me.reshape(1).astype(jnp.int32)

    def update_big(group, after):
        for n, (r, owns) in step.received(group, after).items():
            if n == 'w_in':
                out = sum_adamw_t(r, owns, me1, swap(w[n]), swap(mo[n]), swap(vo[n]), "adamw_w_in")
                out = [swap(a) for a in out]
            else:
                shp = w[n].shape
                rows, cols = shp[0] * shp[1], shp[2]
                tr = min(512 if cols <= 512 else 256, shp[1])
                out = sum_adamw(r.reshape(N_DEV, rows, cols), owns, me1, w[n].reshape(rows, cols),
                                mo[n].reshape(rows, cols), vo[n].reshape(rows, cols), tr, "adamw_" + n)
                out = [a.reshape(shp) for a in out]
            out_g[n], out_d[n], out_m[n], out_v[n] = out
            after = out[0]
        return after

    update_big('mlp', step.started[-1][4][0])

    gsum = step.small_sum([out_g[n] for n in EXCHANGE_GROUPS['mlp']])
    gsum['meta'] = lax.dynamic_slice_in_dim(gsum['meta'], me * meta.shape[1], meta.shape[1], axis=1)
    gsum['conv_w'] = lax.dynamic_slice_in_dim(gsum['conv_w'], me * conv_w.shape[2], conv_w.shape[2], axis=2)
    as2d = lambda a: a.reshape(-1, a.shape[-1])
    deltas, new_m, new_v = adamw_group([as2d(gsum[n]) for n in SMALL], [as2d(w[n]) for n in SMALL],
                                       [as2d(mo[n]) for n in SMALL], [as2d(vo[n]) for n in SMALL], "adamw_small")
    for i, n in enumerate(SMALL):
        out_g[n] = gsum[n]
        out_d[n], out_m[n], out_v[n] = [a[i].reshape(w[n].shape) for a in (deltas, new_m, new_v)]

    update_big('in', deltas[0])

    return (loss, grad_x, *[out_g[n] for n in WEIGHTS], *[out_d[n] for n in WEIGHTS],
            *[out_m[n] for n in WEIGHTS], *[out_v[n] for n in WEIGHTS])
```

```python
import math

import jax
import jax.numpy as jnp
from jax import lax
from jax.experimental import pallas as pl
from jax.experimental.pallas import tpu as pltpu

F32 = jnp.float32
BF16 = jnp.bfloat16

N_DEV = 8
N_META = 16
HEAD_DIM = 64
CONV_WIDTH = 4
RG_C = 8.0
NORM_EPS = 1e-6
LANES = 128
SUBLANES = 8
ATT_BLOCK = 128
ATT_TQ = 512
NEG_BIG = -1e30
ATT_SCALE = 1.0 / math.sqrt(HEAD_DIM)

ADAM_LR = 0.001
ADAM_B1 = 0.9
ADAM_B2 = 0.999
ADAM_EPS = 1e-08
ADAM_WD = 0.01
ADAM_STEP = 10

VMEM_LIMIT_BYTES = 56 * 1024 * 1024
MESH = pl.DeviceIdType.MESH
ANY = pl.BlockSpec(memory_space=pl.ANY)


def _cparams(*sem):
    return pltpu.CompilerParams(dimension_semantics=sem if sem else None,
                                vmem_limit_bytes=VMEM_LIMIT_BYTES)


def _dot(a, b):
    return jnp.dot(a, b, preferred_element_type=F32)


def _dot_nt(a, b):
    return lax.dot_general(a, b, (((1,), (1,)), ((), ())), preferred_element_type=F32)


def _dot_tn(a, b):
    return lax.dot_general(a, b, (((0,), (0,)), ((), ())), preferred_element_type=F32)


def _sigmoid(x):
    return 0.5 * (1.0 + jnp.tanh(0.5 * x))


def _log_sigmoid(x):
    return jnp.minimum(x, 0.0) - jnp.log(1.0 + jnp.exp(-jnp.abs(x)))


def _expm1(x):
    series = x * (1.0 + x * (0.5 + x * (1.0 / 6.0 + x * (1.0 / 24.0))))
    return jnp.where(jnp.abs(x) < 1e-2, series, jnp.exp(x) - 1.0)


_GELU_K = math.sqrt(2.0 / math.pi)
_GELU_C = 0.044715


def _gelu(x):
    t = jnp.tanh(_GELU_K * (x + _GELU_C * x * x * x))
    return 0.5 * x * (1.0 + t)


def _gelu_and_grad(x):
    x2 = x * x
    t = jnp.tanh(_GELU_K * (x + _GELU_C * x2 * x))
    half = 0.5 * (1.0 + t)
    return x * half, half + 0.5 * x * (1.0 - t * t) * _GELU_K * (1.0 + 3.0 * _GELU_C * x2)


def _split3_dot(tri, x):
    hi = x.astype(BF16)
    r1 = x - hi.astype(F32)
    mid = r1.astype(BF16)
    lo = (r1 - mid.astype(F32)).astype(BF16)
    return _dot(tri, hi) + _dot(tri, mid) + _dot(tri, lo)


def _dot_split3(x, sel):
    hi = x.astype(BF16)
    r1 = x - hi.astype(F32)
    mid = r1.astype(BF16)
    lo = (r1 - mid.astype(F32)).astype(BF16)
    return _dot(hi, sel) + _dot(mid, sel) + _dot(lo, sel)


def _rms_fwd(x, g):
    r = lax.rsqrt(jnp.mean(x * x, axis=-1, keepdims=True) + NORM_EPS)
    return x * r * g


def _rms_bwd(x, g, dy):
    r = lax.rsqrt(jnp.mean(x * x, axis=-1, keepdims=True) + NORM_EPS)
    xn = x * r
    dxn = dy * g
    dx = r * (dxn - xn * jnp.mean(dxn * xn, axis=-1, keepdims=True))
    return dx, jnp.sum(dy * xn, axis=0, keepdims=True)


def _accumulate(ref, val, first):
    @pl.when(first)
    def _():
        ref[...] = val

    @pl.when(jnp.logical_not(first))
    def _():
        ref[...] += val


def in_proj(h, g1, w_in_t, wrest_t, nq, tm):
    tp, d = h.shape
    nr = wrest_t.shape[0]

    def body(h_ref, g_ref, wq_ref, wr_ref, z_ref, qkv_ref, rest_ref):
        z = _rms_fwd(h_ref[...], g_ref[...]).astype(BF16)
        z_ref[...] = z
        qkv_ref[...] = _dot_nt(z, wq_ref[...]).astype(BF16)
        rest_ref[...] = _dot_nt(z, wr_ref[...])

    return pl.pallas_call(
        body, name="in_proj", grid=(tp // tm,),
        in_specs=[pl.BlockSpec((tm, d), lambda i: (i, 0)),
                  pl.BlockSpec((1, d), lambda i: (0, 0)),
                  pl.BlockSpec((nq, d), lambda i: (0, 0)),
                  pl.BlockSpec((nr, d), lambda i: (0, 0))],
        out_specs=[pl.BlockSpec((tm, d), lambda i: (i, 0)),
                   pl.BlockSpec((tm, nq), lambda i: (i, 0)),
                   pl.BlockSpec((tm, nr), lambda i: (i, 0))],
        out_shape=[jax.ShapeDtypeStruct((tp, d), BF16),
                   jax.ShapeDtypeStruct((tp, nq), BF16),
                   jax.ShapeDtypeStruct((tp, nr), F32)],
        compiler_params=_cparams("parallel"),
    )(h, g1, w_in_t, wrest_t)


def fgate_fwd(rest, bf_pad, fcol):
    tp = rest.shape[0]
    nb = tp // ATT_BLOCK

    def body(f_ref, b_ref, c_ref, ct_ref):
        r_i = lax.broadcasted_iota(jnp.int32, (ATT_BLOCK, ATT_BLOCK), 0)
        c_i = lax.broadcasted_iota(jnp.int32, (ATT_BLOCK, ATT_BLOCK), 1)
        tri = (r_i >= c_i).astype(BF16)
        c_ref[...] = _log_sigmoid(f_ref[...] + b_ref[...])
        carry = jnp.zeros((1, LANES), F32)
        for i in range(nb):
            sl = slice(i * ATT_BLOCK, (i + 1) * ATT_BLOCK)
            cs = _split3_dot(tri, c_ref[sl, :]) + carry
            carry = cs[ATT_BLOCK - 1:ATT_BLOCK, :]
            c_ref[sl, :] = cs
        ct_ref[...] = c_ref[...].T[0:SUBLANES, :]

    return pl.pallas_call(
        body, name="fgate_fwd", grid=(1,),
        in_specs=[pl.BlockSpec((tp, LANES), lambda i: (0, fcol)),
                  pl.BlockSpec((1, LANES), lambda i: (0, 0))],
        out_specs=[pl.BlockSpec((tp, LANES), lambda i: (0, 0)),
                   pl.BlockSpec((SUBLANES, tp), lambda i: (0, 0))],
        out_shape=[jax.ShapeDtypeStruct((tp, LANES), F32),
                   jax.ShapeDtypeStruct((SUBLANES, tp), F32)],
        compiler_params=_cparams("arbitrary"),
    )(rest, bf_pad)


def _pick_col(blk, head):
    lane = lax.broadcasted_iota(jnp.int32, blk.shape, 1)
    return jnp.sum(jnp.where(lane == head, blk, 0.0), axis=1, keepdims=True)


def _pick_row(blk, head):
    sub = lax.broadcasted_iota(jnp.int32, blk.shape, 0)
    return jnp.sum(jnp.where(sub == head, blk, 0.0), axis=0, keepdims=True)


def _att_tiles(tp):
    out, r0 = [], 0
    while r0 < tp:
        rows = min(ATT_TQ, tp - r0)
        out.append((r0, rows, r0 + rows))
        r0 += rows
    return out


def attn_fwd(qkv, c, ct, nh):
    tp = qkv.shape[0]
    npair = nh // 2
    tiles = _att_tiles(tp)

    def body(q_ref, k_ref, v_ref, c_ref, ct_ref, o_ref, lset_ref):
        p = pl.program_id(0)
        lset_ref[...] = jnp.zeros_like(lset_ref)
        for r0, nr, nk in tiles:
            rs = slice(r0, r0 + nr)
            causal = (r0 + lax.broadcasted_iota(jnp.int32, (nr, nk), 0)
                      >= lax.broadcasted_iota(jnp.int32, (nr, nk), 1))
            cblk = c_ref[rs, :]
            ctb = ct_ref[:, 0:nk]
            for hh in range(2):
                head = 2 * p + hh
                hs = slice(hh * HEAD_DIM, (hh + 1) * HEAD_DIM)
                q = q_ref[rs, hs] * ATT_SCALE
                s = _dot_nt(q, k_ref[0:nk, hs]) + (_pick_col(cblk, head) - _pick_row(ctb, head))
                s = jnp.where(causal, s, NEG_BIG)
                m = jnp.max(s, axis=1, keepdims=True)
                pm = jnp.exp(s - m)
                l = jnp.sum(pm, axis=1, keepdims=True)
                o_ref[rs, hs] = _dot(pm.astype(BF16), v_ref[0:nk, hs]) / l
                lse = m + jnp.log(l)
                lset_ref[hh:hh + 1, rs] = jnp.broadcast_to(lse, (nr, LANES)).T[0:1, :]

    pair = lambda p: (0, p)
    return pl.pallas_call(
        body, name="attn_fwd", grid=(npair,),
        in_specs=[pl.BlockSpec((tp, LANES), pair),
                  pl.BlockSpec((tp, LANES), lambda p: (0, npair + p)),
                  pl.BlockSpec((tp, LANES), lambda p: (0, 2 * npair + p)),
                  pl.BlockSpec((tp, LANES), lambda p: (0, 0)),
                  pl.BlockSpec((SUBLANES, tp), lambda p: (0, 0))],
        out_specs=[pl.BlockSpec((tp, LANES), pair),
                   pl.BlockSpec((None, SUBLANES, tp), lambda p: (p, 0, 0))],
        out_shape=[jax.ShapeDtypeStruct((tp, nh * HEAD_DIM), F32),
                   jax.ShapeDtypeStruct((npair, SUBLANES, tp), F32)],
        compiler_params=_cparams("parallel"),
    )(qkv, qkv, qkv, c, ct)


def _shift_down(x, k, n):
    if k == 0:
        return x
    rows = lax.broadcasted_iota(jnp.int32, x.shape, 0)
    return jnp.where(rows >= k, pltpu.roll(x, k, 0), 0.0)


def _shift_up(x, k, n):
    if k == 0:
        return x
    rows = lax.broadcasted_iota(jnp.int32, x.shape, 0)
    return jnp.where(rows < n - k, pltpu.roll(x, n - k, 0), 0.0)


def _conv_fwd(xr, cw_ref, cb_ref, n):
    xc = cw_ref[CONV_WIDTH - 1:CONV_WIDTH, :] * xr + cb_ref[...]
    for k in range(1, CONV_WIDTH):
        xc = xc + cw_ref[CONV_WIDTH - 1 - k:CONV_WIDTH - k, :] * _shift_down(xr, k, n)
    return xc


def _gates(xc, wga_ref, bga_ref, wgx_ref, bgx_ref, l_ref):
    xcb = xc.astype(BF16)
    r = _sigmoid(_dot(xcb, wga_ref[...]) + bga_ref[...])
    ig = _sigmoid(_dot(xcb, wgx_ref[...]) + bgx_ref[...])
    ls = _log_sigmoid(l_ref[...])
    log_a = RG_C * r * ls
    a = jnp.exp(log_a)
    mult = jnp.sqrt(-_expm1(2.0 * log_a))
    return xcb, r, ig, ls, log_a, a, mult


SCAN_UNROLL = 8


def _scan_rows(a_s, u_s, out_ref, n, reverse):
    nt = n // SUBLANES
    per = SCAN_UNROLL if nt % SCAN_UNROLL == 0 else 1
    row = lax.broadcasted_iota(jnp.int32, (SUBLANES, LANES), 0)
    last = 0 if reverse else SUBLANES - 1

    def tile_scan(a, u):
        for d in (1, 2, 4):
            if reverse:
                keep = row < SUBLANES - d
                sh = SUBLANES - d
            else:
                keep = row >= d
                sh = d
            a_sh = jnp.where(keep, pltpu.roll(a, sh, 0), 1.0)
            u_sh = jnp.where(keep, pltpu.roll(u, sh, 0), 0.0)
            u = a * u_sh + u
            a = a * a_sh
        return a, u

    def step(t, carry):
        tiles = []
        for k in range(per):
            tt = t * per + k
            if reverse:
                tt = nt - 1 - tt
            off = pl.multiple_of(tt * SUBLANES, SUBLANES)
            a, u = tile_scan(a_s[pl.ds(off, SUBLANES), :], u_s[pl.ds(off, SUBLANES), :])
            tiles.append((off, a, u))
        for off, a, u in tiles:
            out_ref[pl.ds(off, SUBLANES), :] = u + a * carry
            carry = u[last:last + 1, :] + a[last:last + 1, :] * carry
        return carry

    lax.fori_loop(0, nt // per, step, jnp.zeros((1, LANES), F32))


def rec_fwd(rest, convw, convb, wga, bga, wgx, bgx, lru, rw):
    tp = rest.shape[0]
    ng = rw // LANES

    def body(xr_ref, yr_ref, cw_ref, cb_ref, wga_ref, bga_ref, wgx_ref, bgx_ref, l_ref,
             rec_ref, hr_ref, xc_ref, r_ref, ig_ref, a_ref, mult_ref, u_s):
        xc = _conv_fwd(xr_ref[...], cw_ref, cb_ref, tp)
        xc_ref[...] = xc
        _, r, ig, ls, log_a, a, mult = _gates(xc, wga_ref, bga_ref, wgx_ref, bgx_ref, l_ref)
        r_ref[...] = r
        ig_ref[...] = ig
        a_ref[...] = a
        mult_ref[...] = mult
        u_s[...] = mult * ig * xc
        _scan_rows(a_ref, u_s, hr_ref, tp, reverse=False)
        rec_ref[...] = hr_ref[...] * _gelu(yr_ref[...])

    col = lambda g: (0, g)
    vec = pl.BlockSpec((1, LANES), col)
    big = pl.BlockSpec((tp, LANES), col)
    return pl.pallas_call(
        body, name="rec_fwd", grid=(ng,),
        in_specs=[big, pl.BlockSpec((tp, LANES), lambda g: (0, ng + g)),
                  pl.BlockSpec((CONV_WIDTH, LANES), col), vec,
                  pl.BlockSpec((None, LANES, LANES), lambda g: (g, 0, 0)), vec,
                  pl.BlockSpec((None, LANES, LANES), lambda g: (g, 0, 0)), vec, vec],
        out_specs=[big] * 7,
        out_shape=[jax.ShapeDtypeStruct((tp, rw), F32)] * 7,
        scratch_shapes=[pltpu.VMEM((tp, LANES), F32)],
        compiler_params=_cparams("parallel"),
    )(rest, rest, convw, convb, wga, bga, wgx, bgx, lru)


def out_mlp_fwd(h, o, rec, ga, gr, wout, g2, gup, gdown, tm):
    tp, d = h.shape
    aw, rw = o.shape[1], rec.shape[1]
    nf = gup.shape[0]
    tf = gup.shape[2]
    nb = MLP_BLOCKS if nf % MLP_BLOCKS == 0 else 1
    nj = nf // nb

    def body(h_ref, o_ref, rec_ref, ga_ref, gr_ref, w_ref, g2_ref, wu_ref, wd_ref,
             h2_ref, mix_ref, z2_ref, u_ref, h3_ref, acc):
        j = pl.program_id(1)

        @pl.when(j == 0)
        def _():
            mix_ref[:, 0:aw] = _rms_fwd(o_ref[...], ga_ref[...]).astype(BF16)
            mix_ref[:, aw:aw + rw] = _rms_fwd(rec_ref[...], gr_ref[...]).astype(BF16)
            h2 = h_ref[...] + _dot(mix_ref[...], w_ref[...])
            h2_ref[...] = h2
            acc[...] = h2
            z2_ref[...] = _rms_fwd(h2, g2_ref[...]).astype(BF16)

        z = z2_ref[...]
        part = None
        for b in range(nb):
            u = jnp.maximum(_dot(z, wu_ref[b]), 0.0)
            u_ref[:, b * tf:(b + 1) * tf] = u.astype(BF16)
            p = _dot((u * u).astype(BF16), wd_ref[b])
            part = p if part is None else part + p
        acc[...] += part

        @pl.when(j == nj - 1)
        def _():
            h3_ref[...] = acc[...]

    row = lambda i, j: (i, 0)
    fix = lambda i, j: (0, 0)
    return pl.pallas_call(
        body, name="out_mlp_fwd", grid=(tp // tm, nj),
        in_specs=[pl.BlockSpec((tm, d), row), pl.BlockSpec((tm, aw), row), pl.BlockSpec((tm, rw), row),
                  pl.BlockSpec((1, aw), fix), pl.BlockSpec((1, rw), fix),
                  pl.BlockSpec((d, d), fix), pl.BlockSpec((1, d), fix),
                  pl.BlockSpec((nb, d, tf), lambda i, j: (j, 0, 0)),
                  pl.BlockSpec((nb, tf, d), lambda i, j: (j, 0, 0))],
        out_specs=[pl.BlockSpec((tm, d), row), pl.BlockSpec((tm, d), row), pl.BlockSpec((tm, d), row),
                   pl.BlockSpec((tm, nb * tf), lambda i, j: (i, j)), pl.BlockSpec((tm, d), row)],
        out_shape=[jax.ShapeDtypeStruct((tp, d), F32), jax.ShapeDtypeStruct((tp, d), BF16),
                   jax.ShapeDtypeStruct((tp, d), BF16), jax.ShapeDtypeStruct((tp, nf * tf), BF16),
                   jax.ShapeDtypeStruct((tp, d), F32)],
        scratch_shapes=[pltpu.VMEM((tm, d), F32)],
        compiler_params=_cparams("parallel", "arbitrary"),
    )(h, o, rec, ga, gr, wout, g2, gup, gdown)


MLP_BLOCKS = 4


def loss_head(h, gf, tgt, t_real, tm):
    tp, d = h.shape

    def body(h_ref, g_ref, t_ref, dh_ref, dg_ref, loss_ref):
        i = pl.program_id(0)
        x = h_ref[...]
        g = g_ref[...]
        r = lax.rsqrt(jnp.mean(x * x, axis=-1, keepdims=True) + NORM_EPS)
        xn = x * r
        rows = i * tm + lax.broadcasted_iota(jnp.int32, (tm, 1), 0)
        valid = jnp.logical_and(rows >= N_META, rows < t_real)
        e = jnp.where(valid, xn * g - t_ref[...], 0.0)
        part = 0.5 * jnp.sum(jnp.sum(e * e, axis=1, keepdims=True) / d, axis=0, keepdims=True)
        dy = e / d
        dxn = dy * g
        dh_ref[...] = r * (dxn - xn * jnp.mean(dxn * xn, axis=-1, keepdims=True))
        _accumulate(dg_ref, jnp.sum(dy * xn, axis=0, keepdims=True), i == 0)
        _accumulate(loss_ref, jnp.broadcast_to(part, (1, LANES)), i == 0)

    row = lambda i: (i, 0)
    fix = lambda i: (0, 0)
    return pl.pallas_call(
        body, name="loss_head", grid=(tp // tm,),
        in_specs=[pl.BlockSpec((tm, d), row), pl.BlockSpec((1, d), fix), pl.BlockSpec((tm, d), row)],
        out_specs=[pl.BlockSpec((tm, d), row), pl.BlockSpec((1, d), fix), pl.BlockSpec((1, LANES), fix)],
        out_shape=[jax.ShapeDtypeStruct((tp, d), F32), jax.ShapeDtypeStruct((1, d), F32),
                   jax.ShapeDtypeStruct((1, LANES), F32)],
        compiler_params=_cparams("arbitrary"),
    )(h, gf, tgt)


def mlp_bwd(dh, u, h2, g2, gup, gdown, tm):
    tp, d = dh.shape
    nf = gup.shape[0]
    tf = gup.shape[2]
    nb = MLP_BLOCKS if nf % MLP_BLOCKS == 0 else 1
    nj = nf // nb
    ni = tp // tm

    def body(dh_ref, u_ref, h2_ref, g_ref, wu_ref, wd_ref, dup_ref, dh2_ref, dg_ref, dhb, acc):
        i = pl.program_id(0)
        j = pl.program_id(1)

        @pl.when(j == 0)
        def _():
            dhb[...] = dh_ref[...].astype(BF16)

        part = None
        for b in range(nb):
            cols = slice(b * tf, (b + 1) * tf)
            dup = (_dot_nt(dhb[...], wd_ref[b]) * (2.0 * u_ref[:, cols].astype(F32))).astype(BF16)
            dup_ref[:, cols] = dup
            p = _dot_nt(dup, wu_ref[b])
            part = p if part is None else part + p
        _accumulate(acc, part, j == 0)

        @pl.when(j == nj - 1)
        def _():
            dx, dg = _rms_bwd(h2_ref[...], g_ref[...], acc[...])
            dh2_ref[...] = dh_ref[...] + dx
            _accumulate(dg_ref, dg, i == 0)

    return pl.pallas_call(
        body, name="mlp_bwd", grid=(ni, nj),
        in_specs=[pl.BlockSpec((tm, d), lambda i, j: (i, 0)),
                  pl.BlockSpec((tm, nb * tf), lambda i, j: (i, j)),
                  pl.BlockSpec((tm, d), lambda i, j: (i, 0)),
                  pl.BlockSpec((1, d), lambda i, j: (0, 0)),
                  pl.BlockSpec((nb, d, tf), lambda i, j: (j, 0, 0)),
                  pl.BlockSpec((nb, tf, d), lambda i, j: (j, 0, 0))],
        out_specs=[pl.BlockSpec((tm, nb * tf), lambda i, j: (i, j)),
                   pl.BlockSpec((tm, d), lambda i, j: (i, 0)),
                   pl.BlockSpec((1, d), lambda i, j: (0, 0)),
                   pl.BlockSpec((tm, d), lambda i, j: (i, 0))],
        out_shape=[jax.ShapeDtypeStruct((tp, nf * tf), BF16), jax.ShapeDtypeStruct((tp, d), F32),
                   jax.ShapeDtypeStruct((1, d), F32), jax.ShapeDtypeStruct((tp, d), BF16)],
        scratch_shapes=[pltpu.VMEM((tm, d), F32)],
        compiler_params=_cparams("arbitrary", "arbitrary"),
    )(dh, u, h2, g2, gup, gdown)


def dw_mlp(u, dhb, z2, dup, tf):
    rows, dff = u.shape
    d = z2.shape[1]
    nf = dff // tf

    def body(u_ref, dh_ref, z_ref, dup_ref, dwd_ref, dwu_ref, zt):
        @pl.when(pl.program_id(0) == 0)
        def _():
            zt[...] = z_ref[...].T

        uf = u_ref[...].astype(F32)
        dwd_ref[...] = _dot_tn((uf * uf).astype(BF16), dh_ref[...]).astype(BF16)
        dwu_ref[...] = _dot(zt[...], dup_ref[...]).astype(BF16)

    col = lambda j: (0, j)
    fix = lambda j: (0, 0)
    return pl.pallas_call(
        body, name="dw_mlp", grid=(nf,),
        in_specs=[pl.BlockSpec((rows, tf), col), pl.BlockSpec((rows, d), fix),
                  pl.BlockSpec((rows, d), fix), pl.BlockSpec((rows, tf), col)],
        out_specs=[pl.BlockSpec((None, tf, d), lambda j: (j, 0, 0)),
                   pl.BlockSpec((None, d, tf), lambda j: (j, 0, 0))],
        out_shape=[jax.ShapeDtypeStruct((nf, tf, d), BF16), jax.ShapeDtypeStruct((nf, d, tf), BF16)],
        scratch_shapes=[pltpu.VMEM((d, rows), BF16)],
        compiler_params=_cparams("arbitrary"),
    )(u, dhb, z2, dup)


def out_proj_bwd(dh2, mix, o, rec, ga, gr, wout, tm):
    tp, d = dh2.shape
    aw, rw = o.shape[1], rec.shape[1]
    ni = tp // tm

    def body(dh_ref, mix_ref, o_ref, rec_ref, ga_ref, gr_ref, w_ref, do_ref, drec_ref, dga_ref, dgr_ref, dw_ref, acc):
        i = pl.program_id(0)
        dhb = dh_ref[...].astype(BF16)
        dmix = _dot_nt(dhb, w_ref[...])
        do, dga = _rms_bwd(o_ref[...], ga_ref[...], dmix[:, 0:aw])
        drec, dgr = _rms_bwd(rec_ref[...], gr_ref[...], dmix[:, aw:aw + rw])
        do_ref[...] = do
        drec_ref[...] = drec
        _accumulate(dga_ref, dga, i == 0)
        _accumulate(dgr_ref, dgr, i == 0)
        _accumulate(acc, _dot_tn(mix_ref[...], dhb), i == 0)

        @pl.when(i == ni - 1)
        def _():
            dw_ref[...] = acc[...].astype(BF16)

    row = lambda i: (i, 0)
    fix = lambda i: (0, 0)
    return pl.pallas_call(
        body, name="out_proj_bwd", grid=(ni,),
        in_specs=[pl.BlockSpec((tm, d), row), pl.BlockSpec((tm, d), row),
                  pl.BlockSpec((tm, aw), row), pl.BlockSpec((tm, rw), row),
                  pl.BlockSpec((1, aw), fix), pl.BlockSpec((1, rw), fix), pl.BlockSpec((d, d), fix)],
        out_specs=[pl.BlockSpec((tm, aw), row), pl.BlockSpec((tm, rw), row),
                   pl.BlockSpec((1, aw), fix), pl.BlockSpec((1, rw), fix), pl.BlockSpec((d, d), fix)],
        out_shape=[jax.ShapeDtypeStruct((tp, aw), F32), jax.ShapeDtypeStruct((tp, rw), F32),
                   jax.ShapeDtypeStruct((1, aw), F32), jax.ShapeDtypeStruct((1, rw), F32),
                   jax.ShapeDtypeStruct((d, d), BF16)],
        scratch_shapes=[pltpu.VMEM((d, d), F32)],
        compiler_params=_cparams("arbitrary"),
    )(dh2, mix, o, rec, ga, gr, wout)


def rec_bwd(drec, hr, xc, gates, rest, convw, wga, wgx, lru, rw):
    tp = rest.shape[0]
    ng = rw // LANES

    def body(drec_ref, hr_ref, xc_ref, r_ref, ig_ref, a_ref, mult_ref, xr_ref, yr_ref, cw_ref, wga_ref, wgx_ref,
             l_ref, dxr_ref, dyr_ref, dwga_ref, dwgx_ref, vec_ref, a_s, u_s, lam_s):
        xc = xc_ref[...]
        h = hr_ref[...]
        drec = drec_ref[...]
        r, ig, a, mult = r_ref[...], ig_ref[...], a_ref[...], mult_ref[...]
        xcb = xc.astype(BF16)
        ls = _log_sigmoid(l_ref[...])
        gelu, gelu_grad = _gelu_and_grad(yr_ref[...])
        dyr_ref[...] = (drec * h * gelu_grad).astype(BF16)
        a_s[...] = _shift_up(a, 1, tp)
        u_s[...] = drec * gelu
        _scan_rows(a_s, u_s, lam_s, tp, reverse=True)
        lam = lam_s[...]
        da = lam * _shift_down(h, 1, tp)
        dmult = lam * ig * xc
        dig = lam * mult * xc
        dxc = lam * mult * ig
        dlog_a = da * a - dmult * (a * a) / mult
        dr = dlog_a * (RG_C * ls)
        dl = jnp.sum(dlog_a * (RG_C * r), axis=0, keepdims=True) * _sigmoid(-l_ref[...])
        dpa = dr * r * (1.0 - r)
        dpx = dig * ig * (1.0 - ig)
        dpab = dpa.astype(BF16)
        dpxb = dpx.astype(BF16)
        dxc = dxc + _dot_nt(dpab, wga_ref[...]) + _dot_nt(dpxb, wgx_ref[...])
        dwga_ref[...] = _dot_tn(xcb, dpab)
        dwgx_ref[...] = _dot_tn(xcb, dpxb)
        xr = xr_ref[...]
        dxr = cw_ref[CONV_WIDTH - 1:CONV_WIDTH, :] * dxc
        for k in range(1, CONV_WIDTH):
            dxr = dxr + cw_ref[CONV_WIDTH - 1 - k:CONV_WIDTH - k, :] * _shift_up(dxc, k, tp)
        dxr_ref[...] = dxr.astype(BF16)
        for k in range(CONV_WIDTH):
            vec_ref[k:k + 1, :] = jnp.sum(dxc * _shift_down(xr, CONV_WIDTH - 1 - k, tp), axis=0, keepdims=True)
        vec_ref[4:5, :] = jnp.sum(dxc, axis=0, keepdims=True)
        vec_ref[5:6, :] = jnp.sum(dpa, axis=0, keepdims=True)
        vec_ref[6:7, :] = jnp.sum(dpx, axis=0, keepdims=True)
        vec_ref[7:8, :] = dl

    col = lambda g: (0, g)
    vec = pl.BlockSpec((1, LANES), col)
    big = pl.BlockSpec((tp, LANES), col)
    sq = pl.BlockSpec((None, LANES, LANES), lambda g: (g, 0, 0))
    return pl.pallas_call(
        body, name="rec_bwd", grid=(ng,),
        in_specs=[big] * 8 + [pl.BlockSpec((tp, LANES), lambda g: (0, ng + g)),
                                pl.BlockSpec((CONV_WIDTH, LANES), col), sq, sq, vec],
        out_specs=[big, big, sq, sq, pl.BlockSpec((None, SUBLANES, LANES), lambda g: (g, 0, 0))],
        out_shape=[jax.ShapeDtypeStruct((tp, rw), BF16), jax.ShapeDtypeStruct((tp, rw), BF16),
                   jax.ShapeDtypeStruct((ng, LANES, LANES), F32), jax.ShapeDtypeStruct((ng, LANES, LANES), F32),
                   jax.ShapeDtypeStruct((ng, SUBLANES, LANES), F32)],
        scratch_shapes=[pltpu.VMEM((tp, LANES), F32)] * 3,
        compiler_params=_cparams("parallel"),
    )(drec, hr, xc, *gates, rest, rest, convw, wga, wgx, lru)


def attn_bwd(qkv, do, o, lset, c, ct, nh):
    tp = qkv.shape[0]
    npair = nh // 2
    aw = nh * HEAD_DIM
    tiles = _att_tiles(tp)

    def body(q_ref, k_ref, v_ref, do_ref, o_ref, lset_ref, c_ref, ct_ref,
             dq_ref, dk_ref, dv_ref, drow_ref, dcol_ref, dk_acc, dv_acc, dq_t):
        p = pl.program_id(0)
        k_t = k_ref[...].T
        dk_acc[...] = jnp.zeros_like(dk_acc)
        dv_acc[...] = jnp.zeros_like(dv_acc)
        dcol_ref[...] = jnp.zeros_like(dcol_ref)
        drow_ref[...] = jnp.zeros_like(drow_ref)
        for r0, nr, nk in tiles:
            rs = slice(r0, r0 + nr)
            causal = (r0 + lax.broadcasted_iota(jnp.int32, (nk, nr), 1)
                      >= lax.broadcasted_iota(jnp.int32, (nk, nr), 0))
            cblk = c_ref[0:nk, :]
            ctb = ct_ref[:, rs]
            for hh in range(2):
                head = 2 * p + hh
                hs = slice(hh * HEAD_DIM, (hh + 1) * HEAD_DIM)
                q = q_ref[rs, hs]
                k = k_ref[0:nk, hs]
                dof = do_ref[rs, hs]
                do16 = dof.astype(BF16)
                delta = jnp.sum(dof * o_ref[rs, hs], axis=1, keepdims=True)
                delta_row = jnp.broadcast_to(delta, (nr, LANES)).T[0:1, :]
                s_t = _dot_nt(k, q * ATT_SCALE) + (_pick_row(ctb, head) - _pick_col(cblk, head))
                p_t = jnp.where(causal, jnp.exp(s_t - lset_ref[hh:hh + 1, rs]), 0.0)
                ds_t = p_t * (_dot_nt(v_ref[0:nk, hs], do16) - delta_row)
                p16 = p_t.astype(BF16)
                ds16 = ds_t.astype(BF16)
                dv_acc[0:nk, hs] += _dot(p16, do16)
                dk_acc[0:nk, hs] += _dot(ds16, q) * ATT_SCALE
                dq_t[hs, rs] = _dot(k_t[hs, 0:nk], ds16)
                drow_ref[hh:hh + 1, rs] = jnp.sum(ds_t, axis=0, keepdims=True)
                dcol_ref[0:nk, hs] -= jnp.broadcast_to(jnp.sum(ds_t, axis=1, keepdims=True), (nk, HEAD_DIM))
        dk_ref[...] = dk_acc[...].astype(BF16)
        dv_ref[...] = dv_acc[...].astype(BF16)
        dq_ref[...] = (dq_t[...].T * ATT_SCALE).astype(BF16)

    pair = lambda p: (0, p)
    return pl.pallas_call(
        body, name="attn_bwd", grid=(npair,),
        in_specs=[pl.BlockSpec((tp, LANES), pair),
                  pl.BlockSpec((tp, LANES), lambda p: (0, npair + p)),
                  pl.BlockSpec((tp, LANES), lambda p: (0, 2 * npair + p)),
                  pl.BlockSpec((tp, LANES), pair),
                  pl.BlockSpec((tp, LANES), pair),
                  pl.BlockSpec((None, SUBLANES, tp), lambda p: (p, 0, 0)),
                  pl.BlockSpec((tp, LANES), lambda p: (0, 0)),
                  pl.BlockSpec((SUBLANES, tp), lambda p: (0, 0))],
        out_specs=[pl.BlockSpec((tp, LANES), pair), pl.BlockSpec((tp, LANES), pair),
                   pl.BlockSpec((tp, LANES), pair),
                   pl.BlockSpec((None, SUBLANES, tp), lambda p: (p, 0, 0)),
                   pl.BlockSpec((tp, LANES), pair)],
        out_shape=[jax.ShapeDtypeStruct((tp, aw), BF16), jax.ShapeDtypeStruct((tp, aw), BF16),
                   jax.ShapeDtypeStruct((tp, aw), BF16),
                   jax.ShapeDtypeStruct((npair, SUBLANES, tp), F32),
                   jax.ShapeDtypeStruct((tp, aw), F32)],
        scratch_shapes=[pltpu.VMEM((tp, LANES), F32), pltpu.VMEM((tp, LANES), F32),
                        pltpu.VMEM((LANES, tp), F32)],
        compiler_params=_cparams("parallel"),
    )(qkv, qkv, qkv, do, o, lset, c, ct)


def fgate_bwd(dct8, drs, rest, bf_pad, fcol):
    tp = rest.shape[0]
    aw = drs.shape[1]
    nb = tp // ATT_BLOCK
    B = ATT_BLOCK

    def body(d_ref, drs_ref, f_ref, b_ref, dfl_ref, db_ref, pad_s, dc_s):
        r_i = lax.broadcasted_iota(jnp.int32, (B, B), 0)
        c_i = lax.broadcasted_iota(jnp.int32, (B, B), 1)
        triu = (c_i >= r_i).astype(BF16)
        sel = (lax.broadcasted_iota(jnp.int32, (aw, LANES), 0)
               == HEAD_DIM * lax.broadcasted_iota(jnp.int32, (aw, LANES), 1)).astype(BF16)
        pad_s[...] = jnp.zeros_like(pad_s)
        pad_s[0:SUBLANES, :] = d_ref[...]
        dc_s[...] = pad_s[...].T + _dot_split3(drs_ref[...], sel)
        carry = jnp.zeros((1, LANES), F32)
        for i in range(nb - 1, -1, -1):
            sl = slice(i * B, (i + 1) * B)
            rc = _split3_dot(triu, dc_s[sl, :])
            dc_s[sl, :] = rc + carry
            carry = carry + rc[0:1, :]
        dfl = dc_s[...] * _sigmoid(-(f_ref[...] + b_ref[...]))
        dfl_ref[...] = dfl.astype(BF16)
        db_ref[...] = jnp.sum(dfl, axis=0, keepdims=True)

    return pl.pallas_call(
        body, name="fgate_bwd", grid=(1,),
        in_specs=[pl.BlockSpec((SUBLANES, tp), lambda i: (0, 0)),
                  pl.BlockSpec((tp, aw), lambda i: (0, 0)),
                  pl.BlockSpec((tp, LANES), lambda i: (0, fcol)),
                  pl.BlockSpec((1, LANES), lambda i: (0, 0))],
        out_specs=[pl.BlockSpec((tp, LANES), lambda i: (0, 0)),
                   pl.BlockSpec((1, LANES), lambda i: (0, 0))],
        out_shape=[jax.ShapeDtypeStruct((tp, LANES), BF16), jax.ShapeDtypeStruct((1, LANES), F32)],
        scratch_shapes=[pltpu.VMEM((LANES, tp), F32), pltpu.VMEM((tp, LANES), F32)],
        compiler_params=_cparams("arbitrary"),
    )(dct8, drs, rest, bf_pad)


def in_proj_bwd(dh2, parts, w_in_t, wrest_t, h, g1, tm):
    tp, d = h.shape
    dq, dk, dv, dxr, dyr, dfl = parts
    aw, rw = dq.shape[1], dxr.shape[1]

    def body(dh2_ref, dq_ref, dk_ref, dv_ref, dxr_ref, dyr_ref, dfl_ref, wq_ref, wr_ref, h_ref, g_ref,
             dh_ref, dg_ref):
        i = pl.program_id(0)
        dz = _dot(dq_ref[...], wq_ref[0:aw, :])
        dz += _dot(dk_ref[...], wq_ref[aw:2 * aw, :])
        dz += _dot(dv_ref[...], wq_ref[2 * aw:3 * aw, :])
        dz += _dot(dxr_ref[...], wr_ref[0:rw, :])
        dz += _dot(dyr_ref[...], wr_ref[rw:2 * rw, :])
        dz += _dot(dfl_ref[...], wr_ref[2 * rw:2 * rw + LANES, :])
        dx, dg = _rms_bwd(h_ref[...], g_ref[...], dz)
        dh_ref[...] = dh2_ref[...] + dx
        _accumulate(dg_ref, dg, i == 0)

    row = lambda i: (i, 0)
    fix = lambda i: (0, 0)
    return pl.pallas_call(
        body, name="in_proj_bwd", grid=(tp // tm,),
        in_specs=[pl.BlockSpec((tm, d), row),
                  pl.BlockSpec((tm, aw), row), pl.BlockSpec((tm, aw), row), pl.BlockSpec((tm, aw), row),
                  pl.BlockSpec((tm, rw), row), pl.BlockSpec((tm, rw), row), pl.BlockSpec((tm, LANES), row),
                  pl.BlockSpec((3 * aw, d), fix), pl.BlockSpec(wrest_t.shape, fix),
                  pl.BlockSpec((tm, d), row), pl.BlockSpec((1, d), fix)],
        out_specs=[pl.BlockSpec((tm, d), row), pl.BlockSpec((1, d), fix)],
        out_shape=[jax.ShapeDtypeStruct((tp, d), F32), jax.ShapeDtypeStruct((1, d), F32)],
        compiler_params=_cparams("arbitrary"),
    )(dh2, dq, dk, dv, dxr, dyr, dfl, w_in_t, wrest_t, h, g1)


def dw_in_t(z, parts, nh, tr):
    tp, d = z.shape
    dq, dk, dv, dxr, dyr, dfl = parts
    aw, rw = dq.shape[1], dxr.shape[1]
    d_in = 3 * aw + nh + 2 * rw
    blk = d_in // N_DEV
    nr = tp // tr
    offs = [(0, aw), (aw, aw), (2 * aw, aw), (3 * aw + nh, rw), (3 * aw + nh + rw, rw)]

    def body(z_ref, dq_ref, dk_ref, dv_ref, dxr_ref, dyr_ref, dfl_ref, o_ref, acc):
        r = pl.program_id(0)

        @pl.when(r == 0)
        def _():
            acc[...] = jnp.zeros_like(acc)

        zt = z_ref[...]
        for (o, n), ref in zip(offs, (dq_ref, dk_ref, dv_ref, dxr_ref, dyr_ref)):
            acc[o:o + n, :] += _dot_tn(ref[...], zt)
        acc[3 * aw:3 * aw + nh, :] += _dot_tn(dfl_ref[...], zt)[0:nh, :]

        @pl.when(r == nr - 1)
        def _():
            for p in range(N_DEV):
                o_ref[p] = acc[p * blk:(p + 1) * blk, :].astype(BF16)

    row = lambda r: (r, 0)
    return pl.pallas_call(
        body, name="dw_in", grid=(nr,),
        in_specs=[pl.BlockSpec((tr, d), row),
                  pl.BlockSpec((tr, aw), row), pl.BlockSpec((tr, aw), row), pl.BlockSpec((tr, aw), row),
                  pl.BlockSpec((tr, rw), row), pl.BlockSpec((tr, rw), row), pl.BlockSpec((tr, LANES), row)],
        out_specs=pl.BlockSpec((N_DEV, blk, d), lambda r: (0, 0, 0)),
        out_shape=jax.ShapeDtypeStruct((N_DEV, blk, d), BF16),
        scratch_shapes=[pltpu.VMEM((d_in, d), F32)],
        compiler_params=_cparams("arbitrary"),
    )(z, dq, dk, dv, dxr, dyr, dfl)


def _place():
    return lax.axis_index("x"), lax.axis_index("y"), lax.axis_index("c")


HBM = pl.BlockSpec(memory_space=pltpu.HBM)
SEM = pl.BlockSpec(memory_space=pltpu.SEMAPHORE)
EFFECT = pltpu.SideEffectType.DATAFLOW_SIDE_EFFECTING


def _in_hbm(a):
    return pltpu.with_memory_space_constraint(a, pltpu.HBM)


def _as_list(a):
    return list(a) if isinstance(a, (list, tuple)) else [a]


def _gather_targets(x, y, c):
    return [(x, y, 1 - c), (1 - x, y, c), (x, 1 - y, c), (1 - x, 1 - y, c)]


def _slot(t):
    return 4 * t[0] + 2 * t[1] + t[2]


def gather_start(groups, name):
    flat = [a for g in groups for a in g]
    n = len(flat)
    ng = len(groups)
    lands = [lax.empty((N_DEV,) + a.shape, a.dtype) for a in flat]

    def body(*refs):
        src, land = refs[:n], refs[n:2 * n]
        sems = refs[2 * n:2 * n + 2 * ng]
        token = refs[-1]
        x, y, c = _place()
        me = 4 * x + 2 * y + c
        i = 0
        for gi, g in enumerate(groups):
            for a in range(len(g)):
                for k, t in enumerate(_gather_targets(x, y, c)):
                    pltpu.make_async_remote_copy(
                        src_ref=src[i], dst_ref=land[i].at[me],
                        send_sem=sems[2 * gi].at[4 * a + k], recv_sem=sems[2 * gi + 1].at[4 * a + k],
                        device_id=t, device_id_type=MESH).start()
                i += 1
        token[...] = jnp.zeros_like(token)

    sem_shapes = []
    for g in groups:
        sem_shapes += [pltpu.SemaphoreType.DMA((4 * len(g),)), pltpu.SemaphoreType.DMA((4 * len(g),))]
    out = pl.pallas_call(
        body, name=name,
        out_shape=sem_shapes + [pltpu.HBM(a.shape, a.dtype) for a in flat + lands]
        + [jax.ShapeDtypeStruct((SUBLANES, LANES), F32)],
        in_specs=[HBM] * (2 * n),
        out_specs=[SEM] * (2 * ng) + [HBM] * (2 * n) + [pl.BlockSpec(memory_space=pltpu.VMEM)],
        input_output_aliases={i: 2 * ng + i for i in range(2 * n)},
        compiler_params=pltpu.CompilerParams(has_side_effects=EFFECT),
    )(*[_in_hbm(a) for a in flat + lands])
    sems = out[:2 * ng]
    thru = out[2 * ng:2 * ng + 2 * n]
    srcs_t, lands_t = thru[:n], thru[n:]
    res, i = [], 0
    for gi, g in enumerate(groups):
        res.append((sems[2 * gi], sems[2 * gi + 1], srcs_t[i:i + len(g)], lands_t[i:i + len(g)]))
        i += len(g)
    return res, out[-1]


def gather_wait(send, recv, srcs, lands, after, name):
    n = len(srcs)

    def body(*refs):
        src, land = refs[:n], refs[n:2 * n]
        send_sem, recv_sem = refs[2 * n], refs[2 * n + 1]
        x, y, c = _place()
        for a in range(n):
            for k, t in enumerate(_gather_targets(x, y, c)):
                cp = pltpu.make_async_remote_copy(
                    src_ref=src[a], dst_ref=land[a].at[_slot(t)],
                    send_sem=send_sem.at[4 * a + k], recv_sem=recv_sem.at[4 * a + k],
                    device_id=t, device_id_type=MESH)
                cp.wait_send()
                cp.wait_recv()

    out = pl.pallas_call(
        body, name=name,
        out_shape=[pltpu.HBM(a.shape, a.dtype) for a in list(srcs) + list(lands)],
        in_specs=[HBM] * (2 * n) + [SEM, SEM] + [ANY] * len(_as_list(after)),
        out_specs=[HBM] * (2 * n),
        input_output_aliases={i: i for i in range(2 * n)},
        compiler_params=pltpu.CompilerParams(has_side_effects=EFFECT),
    )(*srcs, *lands, send, recv, *_as_list(after))
    return out[:n], out[n:]


def forward_start(lands, name):
    n = len(lands)

    def body(*refs):
        land = refs[:n]
        send_sem, recv_sem = refs[n], refs[n + 1]
        token = refs[-1]
        x, y, c = _place()
        for a in range(n):
            for j, chip in enumerate([(1 - x, y), (x, 1 - y), (1 - x, 1 - y)]):
                blk = land[a].at[_slot((*chip, c))]
                pltpu.make_async_remote_copy(src_ref=blk, dst_ref=blk, send_sem=send_sem.at[3 * a + j],
                                             recv_sem=recv_sem.at[3 * a + j], device_id=(x, y, 1 - c),
                                             device_id_type=MESH).start()
        token[...] = jnp.zeros_like(token)

    out = pl.pallas_call(
        body, name=name,
        out_shape=[pltpu.SemaphoreType.DMA((3 * n,)), pltpu.SemaphoreType.DMA((3 * n,))]
        + [pltpu.HBM(a.shape, a.dtype) for a in lands] + [jax.ShapeDtypeStruct((SUBLANES, LANES), F32)],
        in_specs=[HBM] * n,
        out_specs=[SEM, SEM] + [HBM] * n + [pl.BlockSpec(memory_space=pltpu.VMEM)],
        input_output_aliases={i: 2 + i for i in range(n)},
        compiler_params=pltpu.CompilerParams(has_side_effects=EFFECT),
    )(*[_in_hbm(a) for a in lands])
    return out[0], out[1], out[2:2 + n], out[-1][0, 0]


def forward_wait(send, recv, lands, after, name):
    n = len(lands)

    def body(*refs):
        land = refs[:n]
        send_sem, recv_sem = refs[n], refs[n + 1]
        x, y, c = _place()
        for a in range(n):
            for j, chip in enumerate([(1 - x, y), (x, 1 - y), (1 - x, 1 - y)]):
                cp = pltpu.make_async_remote_copy(
                    src_ref=land[a].at[_slot((*chip, c))], dst_ref=land[a].at[_slot((*chip, 1 - c))],
                    send_sem=send_sem.at[3 * a + j], recv_sem=recv_sem.at[3 * a + j],
                    device_id=(x, y, 1 - c), device_id_type=MESH)
                cp.wait_send()
                cp.wait_recv()

    return pl.pallas_call(
        body, name=name,
        out_shape=[pltpu.HBM(a.shape, a.dtype) for a in lands],
        in_specs=[HBM] * n + [SEM, SEM, ANY],
        out_specs=[HBM] * n,
        input_output_aliases={i: i for i in range(n)},
        compiler_params=pltpu.CompilerParams(has_side_effects=EFFECT),
    )(*lands, send, recv, after)


def _relations():
    return [(dx, dy, dc) for dx in (0, 1) for dy in (0, 1) for dc in (0, 1) if dx + dy + dc]


def _peer(x, y, c, rel):
    return ((1 - x) if rel[0] else x, (1 - y) if rel[1] else y, (1 - c) if rel[2] else c)


def exchange_start(srcs, lands, layer, name, after=()):
    n = len(srcs)
    after = _as_list(after)

    def body(*refs):
        src, land = refs[:n], refs[n:2 * n]
        send_sem, recv_sem = refs[2 * n + len(after)], refs[2 * n + len(after) + 1]
        token = refs[-1]
        x, y, c = _place()
        me = 4 * x + 2 * y + c
        for k, rel in enumerate(_relations()):
            peer = _peer(x, y, c, rel)
            for a in range(n):
                pltpu.make_async_remote_copy(
                    src_ref=src[a] if layer is None else src[a].at[_slot(peer)],
                    dst_ref=land[a].at[me] if layer is None else land[a].at[me, layer],
                    send_sem=send_sem.at[7 * a + k], recv_sem=recv_sem.at[7 * a + k],
                    device_id=peer, device_id_type=MESH).start()
        token[...] = jnp.zeros_like(token)

    out = pl.pallas_call(
        body, name=name,
        out_shape=[pltpu.SemaphoreType.DMA((7 * n,)), pltpu.SemaphoreType.DMA((7 * n,))]
        + [pltpu.HBM(a.shape, a.dtype) for a in list(srcs) + list(lands)]
        + [jax.ShapeDtypeStruct((SUBLANES, LANES), F32)],
        in_specs=[HBM] * (2 * n) + [ANY] * len(after),
        out_specs=[SEM, SEM] + [HBM] * (2 * n) + [pl.BlockSpec(memory_space=pltpu.VMEM)],
        input_output_aliases={i: 2 + i for i in range(2 * n)},
        compiler_params=pltpu.CompilerParams(has_side_effects=EFFECT),
    )(*[_in_hbm(a) for a in list(srcs) + list(lands)], *after)
    return out[0], out[1], out[2:2 + n], out[2 + n:2 + 2 * n], out[-1][0, 0]


def exchange_wait(send, recv, srcs, lands, after, layer, name):
    n = len(srcs)

    def body(*refs):
        src, land = refs[:n], refs[n:2 * n]
        send_sem, recv_sem = refs[2 * n], refs[2 * n + 1]
        x, y, c = _place()
        for k, rel in enumerate(_relations()):
            peer = _peer(x, y, c, rel)
            for a in range(n):
                cp = pltpu.make_async_remote_copy(
                    src_ref=src[a] if layer is None else src[a].at[_slot(peer)],
                    dst_ref=land[a].at[_slot(peer)] if layer is None else land[a].at[_slot(peer), layer],
                    send_sem=send_sem.at[7 * a + k], recv_sem=recv_sem.at[7 * a + k],
                    device_id=peer, device_id_type=MESH)
                cp.wait_send()
                cp.wait_recv()

    out = pl.pallas_call(
        body, name=name,
        out_shape=[pltpu.HBM(a.shape, a.dtype) for a in list(srcs) + list(lands)],
        in_specs=[HBM] * (2 * n) + [SEM, SEM] + [ANY] * len(_as_list(after)),
        out_specs=[HBM] * (2 * n),
        input_output_aliases={i: i for i in range(2 * n)},
        compiler_params=pltpu.CompilerParams(has_side_effects=EFFECT),
    )(*srcs, *lands, send, recv, *_as_list(after))
    return out[:n], out[n:]


def _adamw_math(g, w, m, v):
    m = ADAM_B1 * m + (1.0 - ADAM_B1) * g
    v = ADAM_B2 * v + (1.0 - ADAM_B2) * (g * g)
    m_hat = m / (1.0 - ADAM_B1 ** ADAM_STEP)
    v_hat = v / (1.0 - ADAM_B2 ** ADAM_STEP)
    delta = -ADAM_LR * (m_hat / (jnp.sqrt(v_hat) + ADAM_EPS) + ADAM_WD * w)
    return delta, m, v


def _sum_with_own(p_ref, own_refs, layer, me):
    own = own_refs[0][...]
    for k in range(1, len(own_refs)):
        own = jnp.where(layer == k, own_refs[k][...], own)
    g = None
    for p in range(p_ref.shape[0]):
        term = jnp.where(me == p, own, p_ref[p]).astype(F32)
        g = term if g is None else g + term
    return g


def sum_adamw(parts, owns, me, w, m, v, tr, name):
    npart, rows, cols = parts.shape
    nl = len(owns)
    per_layer = rows // nl // tr

    def body(me_ref, p_ref, *refs):
        own_refs = refs[:nl]
        w_ref, m_ref, v_ref, g_ref, d_ref, nm_ref, nv_ref = refs[nl:]
        g = _sum_with_own(p_ref, own_refs, pl.program_id(0) // per_layer, me_ref[0])
        delta, nm, nv = _adamw_math(g, w_ref[...], m_ref[...], v_ref[...])
        g_ref[...] = g
        d_ref[...] = delta
        nm_ref[...] = nm
        nv_ref[...] = nv

    blk = pl.BlockSpec((tr, cols), lambda i, me_ref: (i, 0))
    own_specs = [pl.BlockSpec((None, tr, cols),
                              lambda i, me_ref, l=l: (me_ref[0], jnp.clip(i - l * per_layer, 0, per_layer - 1), 0))
                 for l in range(nl)]
    return pl.pallas_call(
        body, name=name,
        grid_spec=pltpu.PrefetchScalarGridSpec(
            num_scalar_prefetch=1, grid=(rows // tr,),
            in_specs=[pl.BlockSpec((npart, tr, cols), lambda i, me_ref: (0, i, 0))] + own_specs + [blk, blk, blk],
            out_specs=[blk] * 4),
        out_shape=[jax.ShapeDtypeStruct((rows, cols), F32)] * 4,
        compiler_params=_cparams("arbitrary"),
    )(me, parts, *owns, w, m, v)


def sum_adamw_t(parts, owns, me, w, m, v, name):
    npart, nl, rows, cols = parts.shape

    def body(me_ref, p_ref, *refs):
        own_refs = refs[:nl]
        w_ref, m_ref, v_ref, g_ref, d_ref, nm_ref, nv_ref = refs[nl:]
        g = _sum_with_own(p_ref, own_refs, pl.program_id(0), me_ref[0])
        delta, nm, nv = _adamw_math(g, w_ref[...], m_ref[...], v_ref[...])
        g_ref[...] = g
        d_ref[...] = delta
        nm_ref[...] = nm
        nv_ref[...] = nv

    blk = pl.BlockSpec((None, rows, cols), lambda l, me_ref: (l, 0, 0))
    own_specs = [pl.BlockSpec((None, rows, cols), lambda l, me_ref: (me_ref[0], 0, 0)) for _ in range(nl)]
    return pl.pallas_call(
        body, name=name,
        grid_spec=pltpu.PrefetchScalarGridSpec(
            num_scalar_prefetch=1, grid=(nl,),
            in_specs=[pl.BlockSpec((npart, None, rows, cols), lambda l, me_ref: (0, l, 0, 0))] + own_specs
            + [blk, blk, blk],
            out_specs=[blk] * 4),
        out_shape=[jax.ShapeDtypeStruct((nl, rows, cols), F32)] * 4,
        compiler_params=_cparams("arbitrary"),
    )(me, parts, *owns, w, m, v)


def adamw_group(gs, ws, ms, vs, name):
    n = len(gs)

    def body(*refs):
        g, w, m, v, outs = refs[:n], refs[n:2 * n], refs[2 * n:3 * n], refs[3 * n:4 * n], refs[4 * n:]
        for i in range(n):
            delta, nm, nv = _adamw_math(g[i][...], w[i][...], m[i][...], v[i][...])
            outs[i][...] = delta
            outs[n + i][...] = nm
            outs[2 * n + i][...] = nv

    vmem = pl.BlockSpec(memory_space=pltpu.VMEM)
    out = pl.pallas_call(
        body, name=name,
        in_specs=[vmem] * (4 * n), out_specs=[vmem] * (3 * n),
        out_shape=[jax.ShapeDtypeStruct(a.shape, F32) for a in list(ws) * 3],
        compiler_params=_cparams(),
    )(*gs, *ws, *ms, *vs)
    return out[:n], out[n:2 * n], out[2 * n:]


def sum_parts(parts, name):
    npart, rows, cols = parts.shape

    def body(p_ref, g_ref):
        g = p_ref[0].astype(F32)
        for p in range(1, npart):
            g = g + p_ref[p].astype(F32)
        g_ref[...] = g

    return pl.pallas_call(
        body, name=name, grid=(1,),
        in_specs=[pl.BlockSpec((npart, rows, cols), lambda i: (0, 0, 0))],
        out_specs=pl.BlockSpec((rows, cols), lambda i: (0, 0)),
        out_shape=jax.ShapeDtypeStruct((rows, cols), F32),
        compiler_params=_cparams("arbitrary"),
    )(parts)


def _round_up(n, m):
    return (n + m - 1) // m * m


def _block_diag_pairs(w):
    nb, b, _ = w.shape
    per = LANES // b
    ng = nb // per
    w = w.reshape(ng, per, b, b)
    eye = jnp.eye(per, dtype=w.dtype)
    out = jnp.einsum('gpij,pq->gpiqj', w, eye).reshape(ng, LANES, LANES)
    return out.astype(BF16)


def _block_diag_extract(g, b):
    ng = g.shape[0]
    per = LANES // b
    g = g.reshape(ng, per, b, per, b)
    idx = jnp.arange(per)
    return g[:, idx, :, idx, :].transpose(1, 0, 2, 3).reshape(ng * per, b, b)


def _tiles(v):
    v = v.reshape(-1)
    n = _round_up(v.shape[0], SUBLANES * LANES)
    return jnp.pad(v, (0, n - v.shape[0])).reshape(-1, LANES)


SMALL = ['attn_norm_g', 'b_f', 'conv_w', 'conv_b', 'w_gate_a', 'b_gate_a', 'w_gate_x', 'b_gate_x',
         'lru_L', 'attn_out_g', 'rec_out_g', 'mlp_norm_g', 'final_g', 'meta']


def _pack(d):
    return jnp.concatenate([_tiles(d[n]) for n in SMALL], axis=0)


def _unpack(vec, shapes):
    out, r = {}, 0
    for n in SMALL:
        size = math.prod(shapes[n])
        nr = _round_up(size, SUBLANES * LANES) // LANES
        out[n] = vec[r:r + nr].reshape(-1)[:size].reshape(shapes[n])
        r += nr
    return out


def _row_tile(tp):
    return tp // 4 if (tp // 4) % 16 == 0 else tp


def local_step(x, tgt, meta, small, hooks):
    s, d = x.shape
    t_real = s + N_META
    tp = _round_up(t_real, ATT_BLOCK)
    depth = small['attn_norm_g'].shape[0]
    nh = small['b_f'].shape[1]
    rw = small['conv_b'].shape[1]
    blk = small['w_gate_a'].shape[2]
    tm = _row_tile(tp)
    tm2 = tp // 2
    fcol = 2 * rw // LANES

    h = jnp.concatenate([meta, x, jnp.zeros((tp - t_real, d), F32)], axis=0)
    tgt_p = jnp.pad(tgt, ((N_META, tp - t_real), (0, 0)))
    row = lambda v: v.reshape(1, -1)
    bf_pad = jnp.pad(small['b_f'], ((0, 0), (0, LANES - nh)))

    saved = []
    for l in range(depth):
        w_in_t, wrest_t, wout, tok_w = hooks.mixer_weights(l, h)
        wga = _block_diag_pairs(small['w_gate_a'][l])
        wgx = _block_diag_pairs(small['w_gate_x'][l])
        z, qkv, rest = in_proj(h, row(small['attn_norm_g'][l]) + tok_w, w_in_t, wrest_t, 3 * nh * HEAD_DIM, tm)
        c, ct = fgate_fwd(rest, bf_pad[l:l + 1], fcol)
        o, lset = attn_fwd(qkv, c, ct, nh)
        rec, hr, xc, *gates = rec_fwd(rest, small['conv_w'][l], row(small['conv_b'][l]), wga,
                                      row(small['b_gate_a'][l]), wgx, row(small['b_gate_x'][l]),
                                      row(small['lru_L'][l]), rw)
        gup, gdown, tok_w = hooks.mlp_weights(l, rec)
        h2, mix, z2, u, h3 = out_mlp_fwd(h, o, rec, row(small['attn_out_g'][l]), row(small['rec_out_g'][l]), wout,
                                         row(small['mlp_norm_g'][l]) + tok_w, gup, gdown, tm)
        saved.append(dict(h=h, z=z, qkv=qkv, rest=rest, c=c, ct=ct, o=o, lset=lset, rec=rec, hr=hr, xc=xc,
                          h2=h2, mix=mix, z2=z2, u=u, wga=wga, wgx=wgx, gates=gates,
                          w_in_t=w_in_t, wrest_t=wrest_t, wout=wout, gup=gup, gdown=gdown))
        h = h3

    dh, dgf, loss = loss_head(h, row(small['final_g']), tgt_p, t_real, tm)

    gs = {n: [None] * depth for n in SMALL if n not in ('final_g', 'meta')}
    tok = jnp.zeros((), F32)
    for l in reversed(range(depth)):
        sv = saved[l]
        gup, gdown = sv['gup'], sv['gdown']
        tf = gup.shape[2]
        dup, dh2, dg2, dhb = mlp_bwd(dh, sv['u'], sv['h2'], row(small['mlp_norm_g'][l]) + tok, gup, gdown, tm)
        gs['mlp_norm_g'][l] = dg2[0]
        do, drec, dga, dgr, dw_out = out_proj_bwd(dh2, sv['mix'], sv['o'], sv['rec'], row(small['attn_out_g'][l]),
                                                  row(small['rec_out_g'][l]), sv['wout'], tm)
        gs['attn_out_g'][l] = dga[0]
        gs['rec_out_g'][l] = dgr[0]
        dw_down, dw_up = dw_mlp(sv['u'], dhb, sv['z2'], dup, tf)
        blocks = dict(w_down=dw_down, w_up=dw_up, w_out=dw_out.reshape(N_DEV, d // N_DEV, d))
        tok = hooks.grads_ready(l, 'mlp', blocks)
        dxr, dyr, dwga, dwgx, vec = rec_bwd(drec, sv['hr'], sv['xc'], sv['gates'], sv['rest'], small['conv_w'][l],
                                            sv['wga'], sv['wgx'], row(small['lru_L'][l]) + tok, rw)
        gs['w_gate_a'][l] = _block_diag_extract(dwga, blk)
        gs['w_gate_x'][l] = _block_diag_extract(dwgx, blk)
        vec = vec.transpose(1, 0, 2).reshape(SUBLANES, rw)
        gs['conv_w'][l] = vec[0:CONV_WIDTH]
        gs['conv_b'][l] = vec[4]
        gs['b_gate_a'][l] = vec[5]
        gs['b_gate_x'][l] = vec[6]
        gs['lru_L'][l] = vec[7]
        dq, dk, dv, drow, dcol = attn_bwd(sv['qkv'], do, sv['o'], sv['lset'], sv['c'], sv['ct'] + tok, nh)
        drow8 = drow[:, 0:2, :].reshape(nh, tp)
        if nh < SUBLANES:
            drow8 = jnp.pad(drow8, ((0, SUBLANES - nh), (0, 0)))
        dfl, dbf = fgate_bwd(drow8, dcol, sv['rest'], bf_pad[l:l + 1], fcol)
        gs['b_f'][l] = dbf[0, 0:nh]
        parts = (dq, dk, dv, dxr, dyr, dfl)
        dh, dg1 = in_proj_bwd(dh2, parts, sv['w_in_t'], sv['wrest_t'], sv['h'], row(small['attn_norm_g'][l]), tm)
        gs['attn_norm_g'][l] = dg1[0]
        first = ()
        if l == 0:
            grads = {n: jnp.stack(v) for n, v in gs.items()}
            grads['final_g'] = dgf[0]
            grads['meta'] = dh[0:N_META]
            first = hooks.small_ready(grads)
        dw_in = dw_in_t(sv['z'], parts, nh, tm2)
        tok = hooks.grads_ready(l, 'in', dict(w_in=dw_in), first)

    return loss[0, 0], dh


def prep_weights(g_in, g_out, nh, rw):
    d = g_in.shape[2]
    w_in_t = g_in.reshape(-1, d)
    f0 = 3 * nh * HEAD_DIM
    wrest_t = jnp.concatenate([w_in_t[f0 + nh:f0 + nh + 2 * rw],
                               jnp.pad(w_in_t[f0:f0 + nh], ((0, LANES - nh), (0, 0)))], axis=0)
    return w_in_t, wrest_t, g_out.reshape(d, d)


BIG = ['w_in', 'w_out', 'w_up', 'w_down']
EXCHANGE_GROUPS = {'mlp': ['w_down', 'w_up', 'w_out'], 'in': ['w_in']}
WEIGHTS = ['meta', 'attn_norm_g', 'w_in', 'b_f', 'conv_w', 'conv_b', 'w_gate_a', 'b_gate_a', 'w_gate_x', 'b_gate_x',
           'lru_L', 'attn_out_g', 'rec_out_g', 'w_out', 'mlp_norm_g', 'w_up', 'w_down', 'final_g']


def _set_own(arr, own, me):
    return lax.dynamic_update_slice_in_dim(arr, own[None], me, axis=0)


class _Step:
    def __init__(self, w, nh, rw, me):
        self.w, self.nh, self.rw, self.me = w, nh, rw, me
        depth = w['w_in'].shape[0]
        first = [w['w_in_t'][:, 0, :].astype(BF16), w['w_out'][0].astype(BF16), w['meta'], w['conv_w']]
        self.pending, token = gather_start([first], "gather_start_0")
        zero = token[0, 0].astype(BF16)
        groups = [[w['w_up'][0].astype(BF16) + zero, w['w_down'][0].astype(BF16) + zero]]
        for l in range(1, depth):
            groups.append([w['w_in_t'][:, l, :].astype(BF16) + zero, w['w_out'][l].astype(BF16) + zero])
            groups.append([w['w_up'][l].astype(BF16) + zero, w['w_down'][l].astype(BF16) + zero])
        rest, _ = gather_start(groups, "gather_start_1")
        self.pending += rest
        self.first_after = rest[0][2][0]
        self.gathered = {}
        self.passing = {}
        self.token = jnp.zeros((), F32)
        self.lands = {n: lax.empty((N_DEV,) + w[n].shape, BF16) for n in BIG}
        din8, _, d = w['w_in_t'].shape
        self.lands['w_in'] = lax.empty((N_DEV, depth, din8, d), BF16)
        self.started = []
        self.small = None

    def _pass_on(self, gi, after):
        if gi < len(self.pending) and gi not in self.passing:
            send, recv, srcs, lands = self.pending[gi]
            srcs, lands = gather_wait(send, recv, srcs, lands, after, "gather_wait_%d" % gi)
            fsend, frecv, lands, token = forward_start(lands, "forward_start_%d" % gi)
            self.passing[gi] = (fsend, frecv, srcs, lands)
            self.token = token

    def group(self, gi, after):
        if gi not in self.gathered:
            self._pass_on(gi, after)
            fsend, frecv, srcs, lands = self.passing[gi]
            lands = forward_wait(fsend, frecv, lands, after, "forward_wait_%d" % gi)
            self.gathered[gi] = [_set_own(g, own, self.me) for g, own in zip(lands, srcs)]
            if gi >= 2:
                self._pass_on(gi + 1, lands[0])
        return self.gathered[gi]

    def mixer_weights(self, l, after):
        g = self.group(2 * l, after)
        return (*prep_weights(g[0], g[1], self.nh, self.rw), self.token)

    def mlp_weights(self, l, after):
        g = self.group(2 * l + 1, after)
        return g[0], g[1], self.token

    def grads_ready(self, l, group, blocks, after=()):
        names = EXCHANGE_GROUPS[group]
        send, recv, srcs, lands, token = exchange_start(
            [blocks[n] for n in names], [self.lands[n] for n in names], l, "exchange_start_%s_%d" % (group, l),
            after)
        for n, a in zip(names, lands):
            self.lands[n] = a
        self.started.append((l, group, send, recv, srcs))
        return token

    def small_ready(self, grads):
        self.small_shapes = {n: grads[n].shape for n in SMALL}
        packed = _pack(grads).astype(BF16)
        send, recv, srcs, lands, token = exchange_start(
            [packed], [lax.empty((N_DEV,) + packed.shape, BF16)], None, "small_start")
        self.small = (send, recv, srcs, lands)
        return srcs[0]

    def small_sum(self, after):
        send, recv, srcs, lands = self.small
        srcs, lands = exchange_wait(send, recv, srcs, lands, after, None, "small_wait")
        parts = _set_own(lands[0], srcs[0], self.me)
        return _unpack(sum_parts(parts, "sum_small_grads"), self.small_shapes)

    def received(self, group, after):
        names = EXCHANGE_GROUPS[group]
        own = {n: [None] * self.w[n].shape[0] for n in names}
        for l, grp, send, recv, srcs in self.started:
            if grp != group:
                continue
            srcs, lands = exchange_wait(send, recv, srcs, [self.lands[n] for n in names], after, l,
                                        "exchange_wait_%s_%d" % (group, l))
            for n, a, sr in zip(names, lands, srcs):
                self.lands[n] = a
                own[n][l] = sr
        return {n: (self.lands[n], own[n]) for n in names}


def kernel(x, meta, attn_norm_g, w_in, b_f, conv_w, conv_b, w_gate_a, b_gate_a, w_gate_x, b_gate_x, lru_L, attn_out_g, rec_out_g, w_out, mlp_norm_g, w_up, w_down, final_g, loss_target, m_meta, m_attn_norm_g, m_w_in, m_b_f, m_conv_w, m_conv_b, m_w_gate_a, m_b_gate_a, m_w_gate_x, m_b_gate_x, m_lru_L, m_attn_out_g, m_rec_out_g, m_w_out, m_mlp_norm_g, m_w_up, m_w_down, m_final_g, v_meta, v_attn_norm_g, v_w_in, v_b_f, v_conv_w, v_conv_b, v_w_gate_a, v_b_gate_a, v_w_gate_x, v_b_gate_x, v_lru_L, v_attn_out_g, v_rec_out_g, v_w_out, v_mlp_norm_g, v_w_up, v_w_down, v_final_g):
    w = dict(meta=meta, attn_norm_g=attn_norm_g, w_in=w_in, b_f=b_f, conv_w=conv_w, conv_b=conv_b,
             w_gate_a=w_gate_a, b_gate_a=b_gate_a, w_gate_x=w_gate_x, b_gate_x=b_gate_x, lru_L=lru_L,
             attn_out_g=attn_out_g, rec_out_g=rec_out_g, w_out=w_out, mlp_norm_g=mlp_norm_g, w_up=w_up,
             w_down=w_down, final_g=final_g)
    mo = dict(meta=m_meta, attn_norm_g=m_attn_norm_g, w_in=m_w_in, b_f=m_b_f, conv_w=m_conv_w, conv_b=m_conv_b,
              w_gate_a=m_w_gate_a, b_gate_a=m_b_gate_a, w_gate_x=m_w_gate_x, b_gate_x=m_b_gate_x, lru_L=m_lru_L,
              attn_out_g=m_attn_out_g, rec_out_g=m_rec_out_g, w_out=m_w_out, mlp_norm_g=m_mlp_norm_g,
              w_up=m_w_up, w_down=m_w_down, final_g=m_final_g)
    vo = dict(meta=v_meta, attn_norm_g=v_attn_norm_g, w_in=v_w_in, b_f=v_b_f, conv_w=v_conv_w, conv_b=v_conv_b,
              w_gate_a=v_w_gate_a, b_gate_a=v_b_gate_a, w_gate_x=v_w_gate_x, b_gate_x=v_b_gate_x, lru_L=v_lru_L,
              attn_out_g=v_attn_out_g, rec_out_g=v_rec_out_g, w_out=v_w_out, mlp_norm_g=v_mlp_norm_g,
              w_up=v_w_up, w_down=v_w_down, final_g=v_final_g)
    depth = w_in.shape[0]
    nh = b_f.shape[1]
    rw = conv_b.shape[1]
    me = 4 * lax.axis_index("x") + 2 * lax.axis_index("y") + lax.axis_index("c")

    w['w_in_t'] = jnp.transpose(w_in, (2, 0, 1))
    swap = lambda a: jnp.swapaxes(a, 1, 2)
    step = _Step(w, nh, rw, me)
    g0 = step.group(0, step.first_after)
    meta_full = g0[2].transpose(1, 0, 2).reshape(N_META, -1)
    conv_full = g0[3].transpose(1, 2, 0, 3).reshape(depth, CONV_WIDTH, rw)
    small = {n: w[n] for n in SMALL}
    small['conv_w'] = conv_full

    loss_part, dh0 = local_step(x[0], loss_target[0], meta_full, small, step)
    loss = lax.psum(loss_part, ("x", "y", "c"))
    grad_x = dh0[N_META:N_META + x.shape[1]][None]

    out_g, out_d, out_m, out_v = {}, {}, {}, {}
    me1 = me.reshape(1).astype(jnp.int32)

    def update_big(group, after):
        for n, (r, owns) in step.received(group, after).items():
            if n == 'w_in':
                out = sum_adamw_t(r, owns, me1, swap(w[n]), swap(mo[n]), swap(vo[n]), "adamw_w_in")
                out = [swap(a) for a in out]
            else:
                shp = w[n].shape
                rows, cols = shp[0] * shp[1], shp[2]
                tr = min(512 if cols <= 512 else 256, shp[1])
                out = sum_adamw(r.reshape(N_DEV, rows, cols), owns, me1, w[n].reshape(rows, cols),
                                mo[n].reshape(rows, cols), vo[n].reshape(rows, cols), tr, "adamw_" + n)
                out = [a.reshape(shp) for a in out]
            out_g[n], out_d[n], out_m[n], out_v[n] = out
            after = out[0]
        return after

    update_big('mlp', step.started[-1][4][0])

    gsum = step.small_sum([out_g[n] for n in EXCHANGE_GROUPS['mlp']])
    gsum['meta'] = lax.dynamic_slice_in_dim(gsum['meta'], me * meta.shape[1], meta.shape[1], axis=1)
    gsum['conv_w'] = lax.dynamic_slice_in_dim(gsum['conv_w'], me * conv_w.shape[2], conv_w.shape[2], axis=2)
    as2d = lambda a: a.reshape(-1, a.shape[-1])
    deltas, new_m, new_v = adamw_group([as2d(gsum[n]) for n in SMALL], [as2d(w[n]) for n in SMALL],
                                       [as2d(mo[n]) for n in SMALL], [as2d(vo[n]) for n in SMALL], "adamw_small")
    for i, n in enumerate(SMALL):
        out_g[n] = gsum[n]
        out_d[n], out_m[n], out_v[n] = [a[i].reshape(w[n].shape) for a in (deltas, new_m, new_v)]

    update_big('in', deltas[0])

    return (loss, grad_x, *[out_g[n] for n in WEIGHTS], *[out_d[n] for n in WEIGHTS],
            *[out_m[n] for n in WEIGHTS], *[out_v[n] for n in WEIGHTS])
```

```python
import math

import jax
import jax.numpy as jnp
from jax import lax
from jax.experimental import pallas as pl
from jax.experimental.pallas import tpu as pltpu

F32 = jnp.float32
BF16 = jnp.bfloat16

N_DEV = 8
N_META = 16
HEAD_DIM = 64
CONV_WIDTH = 4
RG_C = 8.0
NORM_EPS = 1e-6
LANES = 128
SUBLANES = 8
ATT_BLOCK = 128
ATT_TQ = 512
NEG_BIG = -1e30
ATT_SCALE = 1.0 / math.sqrt(HEAD_DIM)

ADAM_LR = 0.001
ADAM_B1 = 0.9
ADAM_B2 = 0.999
ADAM_EPS = 1e-08
ADAM_WD = 0.01
ADAM_STEP = 10

VMEM_LIMIT_BYTES = 56 * 1024 * 1024
MESH = pl.DeviceIdType.MESH
ANY = pl.BlockSpec(memory_space=pl.ANY)


def _cparams(*sem):
    return pltpu.CompilerParams(dimension_semantics=sem if sem else None,
                                vmem_limit_bytes=VMEM_LIMIT_BYTES)


def _dot(a, b):
    return jnp.dot(a, b, preferred_element_type=F32)


def _dot_nt(a, b):
    return lax.dot_general(a, b, (((1,), (1,)), ((), ())), preferred_element_type=F32)


def _dot_tn(a, b):
    return lax.dot_general(a, b, (((0,), (0,)), ((), ())), preferred_element_type=F32)


def _sigmoid(x):
    return 0.5 * (1.0 + jnp.tanh(0.5 * x))


def _log_sigmoid(x):
    return jnp.minimum(x, 0.0) - jnp.log(1.0 + jnp.exp(-jnp.abs(x)))


def _expm1(x):
    series = x * (1.0 + x * (0.5 + x * (1.0 / 6.0 + x * (1.0 / 24.0))))
    return jnp.where(jnp.abs(x) < 1e-2, series, jnp.exp(x) - 1.0)


_GELU_K = math.sqrt(2.0 / math.pi)
_GELU_C = 0.044715


def _gelu(x):
    t = jnp.tanh(_GELU_K * (x + _GELU_C * x * x * x))
    return 0.5 * x * (1.0 + t)


def _gelu_and_grad(x):
    x2 = x * x
    t = jnp.tanh(_GELU_K * (x + _GELU_C * x2 * x))
    half = 0.5 * (1.0 + t)
    return x * half, half + 0.5 * x * (1.0 - t * t) * _GELU_K * (1.0 + 3.0 * _GELU_C * x2)


def _split3_dot(tri, x):
    hi = x.astype(BF16)
    r1 = x - hi.astype(F32)
    mid = r1.astype(BF16)
    lo = (r1 - mid.astype(F32)).astype(BF16)
    return _dot(tri, hi) + _dot(tri, mid) + _dot(tri, lo)


def _dot_split3(x, sel):
    hi = x.astype(BF16)
    r1 = x - hi.astype(F32)
    mid = r1.astype(BF16)
    lo = (r1 - mid.astype(F32)).astype(BF16)
    return _dot(hi, sel) + _dot(mid, sel) + _dot(lo, sel)


def _rms_fwd(x, g):
    r = lax.rsqrt(jnp.mean(x * x, axis=-1, keepdims=True) + NORM_EPS)
    return x * r * g


def _rms_bwd(x, g, dy):
    r = lax.rsqrt(jnp.mean(x * x, axis=-1, keepdims=True) + NORM_EPS)
    xn = x * r
    dxn = dy * g
    dx = r * (dxn - xn * jnp.mean(dxn * xn, axis=-1, keepdims=True))
    return dx, jnp.sum(dy * xn, axis=0, keepdims=True)


def _accumulate(ref, val, first):
    @pl.when(first)
    def _():
        ref[...] = val

    @pl.when(jnp.logical_not(first))
    def _():
        ref[...] += val


def in_proj(h, g1, w_in_t, wrest_t, nq, tm):
    tp, d = h.shape
    nr = wrest_t.shape[0]

    def body(h_ref, g_ref, wq_ref, wr_ref, z_ref, qkv_ref, rest_ref):
        z = _rms_fwd(h_ref[...], g_ref[...]).astype(BF16)
        z_ref[...] = z
        qkv_ref[...] = _dot_nt(z, wq_ref[...]).astype(BF16)
        rest_ref[...] = _dot_nt(z, wr_ref[...])

    return pl.pallas_call(
        body, name="in_proj", grid=(tp // tm,),
        in_specs=[pl.BlockSpec((tm, d), lambda i: (i, 0)),
                  pl.BlockSpec((1, d), lambda i: (0, 0)),
                  pl.BlockSpec((nq, d), lambda i: (0, 0)),
                  pl.BlockSpec((nr, d), lambda i: (0, 0))],
        out_specs=[pl.BlockSpec((tm, d), lambda i: (i, 0)),
                   pl.BlockSpec((tm, nq), lambda i: (i, 0)),
                   pl.BlockSpec((tm, nr), lambda i: (i, 0))],
        out_shape=[jax.ShapeDtypeStruct((tp, d), BF16),
                   jax.ShapeDtypeStruct((tp, nq), BF16),
                   jax.ShapeDtypeStruct((tp, nr), F32)],
        compiler_params=_cparams("parallel"),
    )(h, g1, w_in_t, wrest_t)


def fgate_fwd(rest, bf_pad, fcol):
    tp = rest.shape[0]
    nb = tp // ATT_BLOCK

    def body(f_ref, b_ref, c_ref, ct_ref):
        r_i = lax.broadcasted_iota(jnp.int32, (ATT_BLOCK, ATT_BLOCK), 0)
        c_i = lax.broadcasted_iota(jnp.int32, (ATT_BLOCK, ATT_BLOCK), 1)
        tri = (r_i >= c_i).astype(BF16)
        c_ref[...] = _log_sigmoid(f_ref[...] + b_ref[...])
        carry = jnp.zeros((1, LANES), F32)
        for i in range(nb):
            sl = slice(i * ATT_BLOCK, (i + 1) * ATT_BLOCK)
            cs = _split3_dot(tri, c_ref[sl, :]) + carry
            carry = cs[ATT_BLOCK - 1:ATT_BLOCK, :]
            c_ref[sl, :] = cs
        ct_ref[...] = c_ref[...].T[0:SUBLANES, :]

    return pl.pallas_call(
        body, name="fgate_fwd", grid=(1,),
        in_specs=[pl.BlockSpec((tp, LANES), lambda i: (0, fcol)),
                  pl.BlockSpec((1, LANES), lambda i: (0, 0))],
        out_specs=[pl.BlockSpec((tp, LANES), lambda i: (0, 0)),
                   pl.BlockSpec((SUBLANES, tp), lambda i: (0, 0))],
        out_shape=[jax.ShapeDtypeStruct((tp, LANES), F32),
                   jax.ShapeDtypeStruct((SUBLANES, tp), F32)],
        compiler_params=_cparams("arbitrary"),
    )(rest, bf_pad)


def _pick_col(blk, head):
    lane = lax.broadcasted_iota(jnp.int32, blk.shape, 1)
    return jnp.sum(jnp.where(lane == head, blk, 0.0), axis=1, keepdims=True)


def _pick_row(blk, head):
    sub = lax.broadcasted_iota(jnp.int32, blk.shape, 0)
    return jnp.sum(jnp.where(sub == head, blk, 0.0), axis=0, keepdims=True)


def _att_tiles(tp):
    out, r0 = [], 0
    while r0 < tp:
        rows = min(ATT_TQ, tp - r0)
        out.append((r0, rows, r0 + rows))
        r0 += rows
    return out


def attn_fwd(qkv, c, ct, nh):
    tp = qkv.shape[0]
    npair = nh // 2
    tiles = _att_tiles(tp)

    def body(q_ref, k_ref, v_ref, c_ref, ct_ref, o_ref, lset_ref):
        p = pl.program_id(0)
        lset_ref[...] = jnp.zeros_like(lset_ref)
        for r0, nr, nk in tiles:
            rs = slice(r0, r0 + nr)
            causal = (r0 + lax.broadcasted_iota(jnp.int32, (nr, nk), 0)
                      >= lax.broadcasted_iota(jnp.int32, (nr, nk), 1))
            cblk = c_ref[rs, :]
            ctb = ct_ref[:, 0:nk]
            for hh in range(2):
                head = 2 * p + hh
                hs = slice(hh * HEAD_DIM, (hh + 1) * HEAD_DIM)
                q = q_ref[rs, hs] * ATT_SCALE
                s = _dot_nt(q, k_ref[0:nk, hs]) + (_pick_col(cblk, head) - _pick_row(ctb, head))
                s = jnp.where(causal, s, NEG_BIG)
                m = jnp.max(s, axis=1, keepdims=True)
                pm = jnp.exp(s - m)
                l = jnp.sum(pm, axis=1, keepdims=True)
                o_ref[rs, hs] = _dot(pm.astype(BF16), v_ref[0:nk, hs]) / l
                lse = m + jnp.log(l)
                lset_ref[hh:hh + 1, rs] = jnp.broadcast_to(lse, (nr, LANES)).T[0:1, :]

    pair = lambda p: (0, p)
    return pl.pallas_call(
        body, name="attn_fwd", grid=(npair,),
        in_specs=[pl.BlockSpec((tp, LANES), pair),
                  pl.BlockSpec((tp, LANES), lambda p: (0, npair + p)),
                  pl.BlockSpec((tp, LANES), lambda p: (0, 2 * npair + p)),
                  pl.BlockSpec((tp, LANES), lambda p: (0, 0)),
                  pl.BlockSpec((SUBLANES, tp), lambda p: (0, 0))],
        out_specs=[pl.BlockSpec((tp, LANES), pair),
                   pl.BlockSpec((None, SUBLANES, tp), lambda p: (p, 0, 0))],
        out_shape=[jax.ShapeDtypeStruct((tp, nh * HEAD_DIM), F32),
                   jax.ShapeDtypeStruct((npair, SUBLANES, tp), F32)],
        compiler_params=_cparams("parallel"),
    )(qkv, qkv, qkv, c, ct)


def _shift_down(x, k, n):
    if k == 0:
        return x
    rows = lax.broadcasted_iota(jnp.int32, x.shape, 0)
    return jnp.where(rows >= k, pltpu.roll(x, k, 0), 0.0)


def _shift_up(x, k, n):
    if k == 0:
        return x
    rows = lax.broadcasted_iota(jnp.int32, x.shape, 0)
    return jnp.where(rows < n - k, pltpu.roll(x, n - k, 0), 0.0)


def _conv_fwd(xr, cw_ref, cb_ref, n):
    xc = cw_ref[CONV_WIDTH - 1:CONV_WIDTH, :] * xr + cb_ref[...]
    for k in range(1, CONV_WIDTH):
        xc = xc + cw_ref[CONV_WIDTH - 1 - k:CONV_WIDTH - k, :] * _shift_down(xr, k, n)
    return xc


def _gates(xc, wga_ref, bga_ref, wgx_ref, bgx_ref, l_ref):
    xcb = xc.astype(BF16)
    r = _sigmoid(_dot(xcb, wga_ref[...]) + bga_ref[...])
    ig = _sigmoid(_dot(xcb, wgx_ref[...]) + bgx_ref[...])
    ls = _log_sigmoid(l_ref[...])
    log_a = RG_C * r * ls
    a = jnp.exp(log_a)
    mult = jnp.sqrt(-_expm1(2.0 * log_a))
    return xcb, r, ig, ls, log_a, a, mult


SCAN_UNROLL = 16


def _scan_rows(a_s, u_s, out_ref, n, reverse):
    nt = n // SUBLANES
    per = SCAN_UNROLL if nt % SCAN_UNROLL == 0 else 1
    row = lax.broadcasted_iota(jnp.int32, (SUBLANES, LANES), 0)
    last = 0 if reverse else SUBLANES - 1

    def tile_scan(a, u):
        for d in (1, 2, 4):
            if reverse:
                keep = row < SUBLANES - d
                sh = SUBLANES - d
            else:
                keep = row >= d
                sh = d
            a_sh = jnp.where(keep, pltpu.roll(a, sh, 0), 1.0)
            u_sh = jnp.where(keep, pltpu.roll(u, sh, 0), 0.0)
            u = a * u_sh + u
            a = a * a_sh
        return a, u

    def step(t, carry):
        tiles = []
        for k in range(per):
            tt = t * per + k
            if reverse:
                tt = nt - 1 - tt
            off = pl.multiple_of(tt * SUBLANES, SUBLANES)
            a, u = tile_scan(a_s[pl.ds(off, SUBLANES), :], u_s[pl.ds(off, SUBLANES), :])
            tiles.append((off, a, u))
        for off, a, u in tiles:
            out_ref[pl.ds(off, SUBLANES), :] = u + a * carry
            carry = u[last:last + 1, :] + a[last:last + 1, :] * carry
        return carry

    lax.fori_loop(0, nt // per, step, jnp.zeros((1, LANES), F32))


def rec_fwd(rest, convw, convb, wga, bga, wgx, bgx, lru, rw):
    tp = rest.shape[0]
    ng = rw // LANES

    def body(xr_ref, yr_ref, cw_ref, cb_ref, wga_ref, bga_ref, wgx_ref, bgx_ref, l_ref,
             rec_ref, hr_ref, xc_ref, r_ref, ig_ref, a_ref, mult_ref, u_s):
        xc = _conv_fwd(xr_ref[...], cw_ref, cb_ref, tp)
        xc_ref[...] = xc
        _, r, ig, ls, log_a, a, mult = _gates(xc, wga_ref, bga_ref, wgx_ref, bgx_ref, l_ref)
        r_ref[...] = r
        ig_ref[...] = ig
        a_ref[...] = a
        mult_ref[...] = mult
        u_s[...] = mult * ig * xc
        _scan_rows(a_ref, u_s, hr_ref, tp, reverse=False)
        rec_ref[...] = hr_ref[...] * _gelu(yr_ref[...])

    col = lambda g: (0, g)
    vec = pl.BlockSpec((1, LANES), col)
    big = pl.BlockSpec((tp, LANES), col)
    return pl.pallas_call(
        body, name="rec_fwd", grid=(ng,),
        in_specs=[big, pl.BlockSpec((tp, LANES), lambda g: (0, ng + g)),
                  pl.BlockSpec((CONV_WIDTH, LANES), col), vec,
                  pl.BlockSpec((None, LANES, LANES), lambda g: (g, 0, 0)), vec,
                  pl.BlockSpec((None, LANES, LANES), lambda g: (g, 0, 0)), vec, vec],
        out_specs=[big] * 7,
        out_shape=[jax.ShapeDtypeStruct((tp, rw), F32)] * 7,
        scratch_shapes=[pltpu.VMEM((tp, LANES), F32)],
        compiler_params=_cparams("parallel"),
    )(rest, rest, convw, convb, wga, bga, wgx, bgx, lru)


def out_mlp_fwd(h, o, rec, ga, gr, wout, g2, gup, gdown, tm):
    tp, d = h.shape
    aw, rw = o.shape[1], rec.shape[1]
    nf = gup.shape[0]
    tf = gup.shape[2]
    nb = MLP_BLOCKS if nf % MLP_BLOCKS == 0 else 1
    nj = nf // nb

    def body(h_ref, o_ref, rec_ref, ga_ref, gr_ref, w_ref, g2_ref, wu_ref, wd_ref,
             h2_ref, mix_ref, z2_ref, u_ref, h3_ref, acc):
        j = pl.program_id(1)

        @pl.when(j == 0)
        def _():
            mix_ref[:, 0:aw] = _rms_fwd(o_ref[...], ga_ref[...]).astype(BF16)
            mix_ref[:, aw:aw + rw] = _rms_fwd(rec_ref[...], gr_ref[...]).astype(BF16)
            h2 = h_ref[...] + _dot(mix_ref[...], w_ref[...])
            h2_ref[...] = h2
            acc[...] = h2
            z2_ref[...] = _rms_fwd(h2, g2_ref[...]).astype(BF16)

        z = z2_ref[...]
        part = None
        for b in range(nb):
            u = jnp.maximum(_dot(z, wu_ref[b]), 0.0)
            u_ref[:, b * tf:(b + 1) * tf] = u.astype(BF16)
            p = _dot((u * u).astype(BF16), wd_ref[b])
            part = p if part is None else part + p
        acc[...] += part

        @pl.when(j == nj - 1)
        def _():
            h3_ref[...] = acc[...]

    row = lambda i, j: (i, 0)
    fix = lambda i, j: (0, 0)
    return pl.pallas_call(
        body, name="out_mlp_fwd", grid=(tp // tm, nj),
        in_specs=[pl.BlockSpec((tm, d), row), pl.BlockSpec((tm, aw), row), pl.BlockSpec((tm, rw), row),
                  pl.BlockSpec((1, aw), fix), pl.BlockSpec((1, rw), fix),
                  pl.BlockSpec((d, d), fix), pl.BlockSpec((1, d), fix),
                  pl.BlockSpec((nb, d, tf), lambda i, j: (j, 0, 0)),
                  pl.BlockSpec((nb, tf, d), lambda i, j: (j, 0, 0))],
        out_specs=[pl.BlockSpec((tm, d), row), pl.BlockSpec((tm, d), row), pl.BlockSpec((tm, d), row),
                   pl.BlockSpec((tm, nb * tf), lambda i, j: (i, j)), pl.BlockSpec((tm, d), row)],
        out_shape=[jax.ShapeDtypeStruct((tp, d), F32), jax.ShapeDtypeStruct((tp, d), BF16),
                   jax.ShapeDtypeStruct((tp, d), BF16), jax.ShapeDtypeStruct((tp, nf * tf), BF16),
                   jax.ShapeDtypeStruct((tp, d), F32)],
        scratch_shapes=[pltpu.VMEM((tm, d), F32)],
        compiler_params=_cparams("parallel", "arbitrary"),
    )(h, o, rec, ga, gr, wout, g2, gup, gdown)


MLP_BLOCKS = 4


def loss_head(h, gf, tgt, t_real, tm):
    tp, d = h.shape

    def body(h_ref, g_ref, t_ref, dh_ref, dg_ref, loss_ref):
        i = pl.program_id(0)
        x = h_ref[...]
        g = g_ref[...]
        r = lax.rsqrt(jnp.mean(x * x, axis=-1, keepdims=True) + NORM_EPS)
        xn = x * r
        rows = i * tm + lax.broadcasted_iota(jnp.int32, (tm, 1), 0)
        valid = jnp.logical_and(rows >= N_META, rows < t_real)
        e = jnp.where(valid, xn * g - t_ref[...], 0.0)
        part = 0.5 * jnp.sum(jnp.sum(e * e, axis=1, keepdims=True) / d, axis=0, keepdims=True)
        dy = e / d
        dxn = dy * g
        dh_ref[...] = r * (dxn - xn * jnp.mean(dxn * xn, axis=-1, keepdims=True))
        _accumulate(dg_ref, jnp.sum(dy * xn, axis=0, keepdims=True), i == 0)
        _accumulate(loss_ref, jnp.broadcast_to(part, (1, LANES)), i == 0)

    row = lambda i: (i, 0)
    fix = lambda i: (0, 0)
    return pl.pallas_call(
        body, name="loss_head", grid=(tp // tm,),
        in_specs=[pl.BlockSpec((tm, d), row), pl.BlockSpec((1, d), fix), pl.BlockSpec((tm, d), row)],
        out_specs=[pl.BlockSpec((tm, d), row), pl.BlockSpec((1, d), fix), pl.BlockSpec((1, LANES), fix)],
        out_shape=[jax.ShapeDtypeStruct((tp, d), F32), jax.ShapeDtypeStruct((1, d), F32),
                   jax.ShapeDtypeStruct((1, LANES), F32)],
        compiler_params=_cparams("arbitrary"),
    )(h, gf, tgt)


def mlp_bwd(dh, u, h2, g2, gup, gdown, tm):
    tp, d = dh.shape
    nf = gup.shape[0]
    tf = gup.shape[2]
    nb = MLP_BLOCKS if nf % MLP_BLOCKS == 0 else 1
    nj = nf // nb
    ni = tp // tm

    def body(dh_ref, u_ref, h2_ref, g_ref, wu_ref, wd_ref, dup_ref, dh2_ref, dg_ref, dhb, acc):
        i = pl.program_id(0)
        j = pl.program_id(1)

        @pl.when(j == 0)
        def _():
            dhb[...] = dh_ref[...].astype(BF16)

        part = None
        for b in range(nb):
            cols = slice(b * tf, (b + 1) * tf)
            dup = (_dot_nt(dhb[...], wd_ref[b]) * (2.0 * u_ref[:, cols].astype(F32))).astype(BF16)
            dup_ref[:, cols] = dup
            p = _dot_nt(dup, wu_ref[b])
            part = p if part is None else part + p
        _accumulate(acc, part, j == 0)

        @pl.when(j == nj - 1)
        def _():
            dx, dg = _rms_bwd(h2_ref[...], g_ref[...], acc[...])
            dh2_ref[...] = dh_ref[...] + dx
            _accumulate(dg_ref, dg, i == 0)

    return pl.pallas_call(
        body, name="mlp_bwd", grid=(ni, nj),
        in_specs=[pl.BlockSpec((tm, d), lambda i, j: (i, 0)),
                  pl.BlockSpec((tm, nb * tf), lambda i, j: (i, j)),
                  pl.BlockSpec((tm, d), lambda i, j: (i, 0)),
                  pl.BlockSpec((1, d), lambda i, j: (0, 0)),
                  pl.BlockSpec((nb, d, tf), lambda i, j: (j, 0, 0)),
                  pl.BlockSpec((nb, tf, d), lambda i, j: (j, 0, 0))],
        out_specs=[pl.BlockSpec((tm, nb * tf), lambda i, j: (i, j)),
                   pl.BlockSpec((tm, d), lambda i, j: (i, 0)),
                   pl.BlockSpec((1, d), lambda i, j: (0, 0)),
                   pl.BlockSpec((tm, d), lambda i, j: (i, 0))],
        out_shape=[jax.ShapeDtypeStruct((tp, nf * tf), BF16), jax.ShapeDtypeStruct((tp, d), F32),
                   jax.ShapeDtypeStruct((1, d), F32), jax.ShapeDtypeStruct((tp, d), BF16)],
        scratch_shapes=[pltpu.VMEM((tm, d), F32)],
        compiler_params=_cparams("arbitrary", "arbitrary"),
    )(dh, u, h2, g2, gup, gdown)


def dw_mlp(u, dhb, z2, dup, tf):
    rows, dff = u.shape
    d = z2.shape[1]
    nf = dff // tf

    def body(u_ref, dh_ref, z_ref, dup_ref, dwd_ref, dwu_ref, zt):
        @pl.when(pl.program_id(0) == 0)
        def _():
            zt[...] = z_ref[...].T

        uf = u_ref[...].astype(F32)
        dwd_ref[...] = _dot_tn((uf * uf).astype(BF16), dh_ref[...]).astype(BF16)
        dwu_ref[...] = _dot(zt[...], dup_ref[...]).astype(BF16)

    col = lambda j: (0, j)
    fix = lambda j: (0, 0)
    return pl.pallas_call(
        body, name="dw_mlp", grid=(nf,),
        in_specs=[pl.BlockSpec((rows, tf), col), pl.BlockSpec((rows, d), fix),
                  pl.BlockSpec((rows, d), fix), pl.BlockSpec((rows, tf), col)],
        out_specs=[pl.BlockSpec((None, tf, d), lambda j: (j, 0, 0)),
                   pl.BlockSpec((None, d, tf), lambda j: (j, 0, 0))],
        out_shape=[jax.ShapeDtypeStruct((nf, tf, d), BF16), jax.ShapeDtypeStruct((nf, d, tf), BF16)],
        scratch_shapes=[pltpu.VMEM((d, rows), BF16)],
        compiler_params=_cparams("arbitrary"),
    )(u, dhb, z2, dup)


def out_proj_bwd(dh2, mix, o, rec, ga, gr, wout, tm):
    tp, d = dh2.shape
    aw, rw = o.shape[1], rec.shape[1]
    ni = tp // tm

    def body(dh_ref, mix_ref, o_ref, rec_ref, ga_ref, gr_ref, w_ref, do_ref, drec_ref, dga_ref, dgr_ref, dw_ref, acc):
        i = pl.program_id(0)
        dhb = dh_ref[...].astype(BF16)
        dmix = _dot_nt(dhb, w_ref[...])
        do, dga = _rms_bwd(o_ref[...], ga_ref[...], dmix[:, 0:aw])
        drec, dgr = _rms_bwd(rec_ref[...], gr_ref[...], dmix[:, aw:aw + rw])
        do_ref[...] = do
        drec_ref[...] = drec
        _accumulate(dga_ref, dga, i == 0)
        _accumulate(dgr_ref, dgr, i == 0)
        _accumulate(acc, _dot_tn(mix_ref[...], dhb), i == 0)

        @pl.when(i == ni - 1)
        def _():
            dw_ref[...] = acc[...].astype(BF16)

    row = lambda i: (i, 0)
    fix = lambda i: (0, 0)
    return pl.pallas_call(
        body, name="out_proj_bwd", grid=(ni,),
        in_specs=[pl.BlockSpec((tm, d), row), pl.BlockSpec((tm, d), row),
                  pl.BlockSpec((tm, aw), row), pl.BlockSpec((tm, rw), row),
                  pl.BlockSpec((1, aw), fix), pl.BlockSpec((1, rw), fix), pl.BlockSpec((d, d), fix)],
        out_specs=[pl.BlockSpec((tm, aw), row), pl.BlockSpec((tm, rw), row),
                   pl.BlockSpec((1, aw), fix), pl.BlockSpec((1, rw), fix), pl.BlockSpec((d, d), fix)],
        out_shape=[jax.ShapeDtypeStruct((tp, aw), F32), jax.ShapeDtypeStruct((tp, rw), F32),
                   jax.ShapeDtypeStruct((1, aw), F32), jax.ShapeDtypeStruct((1, rw), F32),
                   jax.ShapeDtypeStruct((d, d), BF16)],
        scratch_shapes=[pltpu.VMEM((d, d), F32)],
        compiler_params=_cparams("arbitrary"),
    )(dh2, mix, o, rec, ga, gr, wout)


def rec_bwd(drec, hr, xc, gates, rest, convw, wga, wgx, lru, rw):
    tp = rest.shape[0]
    ng = rw // LANES

    def body(drec_ref, hr_ref, xc_ref, r_ref, ig_ref, a_ref, mult_ref, xr_ref, yr_ref, cw_ref, wga_ref, wgx_ref,
             l_ref, dxr_ref, dyr_ref, dwga_ref, dwgx_ref, vec_ref, a_s, u_s, lam_s):
        xc = xc_ref[...]
        h = hr_ref[...]
        drec = drec_ref[...]
        r, ig, a, mult = r_ref[...], ig_ref[...], a_ref[...], mult_ref[...]
        xcb = xc.astype(BF16)
        ls = _log_sigmoid(l_ref[...])
        gelu, gelu_grad = _gelu_and_grad(yr_ref[...])
        dyr_ref[...] = (drec * h * gelu_grad).astype(BF16)
        a_s[...] = _shift_up(a, 1, tp)
        u_s[...] = drec * gelu
        _scan_rows(a_s, u_s, lam_s, tp, reverse=True)
        lam = lam_s[...]
        da = lam * _shift_down(h, 1, tp)
        dmult = lam * ig * xc
        dig = lam * mult * xc
        dxc = lam * mult * ig
        dlog_a = da * a - dmult * (a * a) / mult
        dr = dlog_a * (RG_C * ls)
        dl = jnp.sum(dlog_a * (RG_C * r), axis=0, keepdims=True) * _sigmoid(-l_ref[...])
        dpa = dr * r * (1.0 - r)
        dpx = dig * ig * (1.0 - ig)
        dpab = dpa.astype(BF16)
        dpxb = dpx.astype(BF16)
        dxc = dxc + _dot_nt(dpab, wga_ref[...]) + _dot_nt(dpxb, wgx_ref[...])
        dwga_ref[...] = _dot_tn(xcb, dpab)
        dwgx_ref[...] = _dot_tn(xcb, dpxb)
        xr = xr_ref[...]
        dxr = cw_ref[CONV_WIDTH - 1:CONV_WIDTH, :] * dxc
        for k in range(1, CONV_WIDTH):
            dxr = dxr + cw_ref[CONV_WIDTH - 1 - k:CONV_WIDTH - k, :] * _shift_up(dxc, k, tp)
        dxr_ref[...] = dxr.astype(BF16)
        for k in range(CONV_WIDTH):
            vec_ref[k:k + 1, :] = jnp.sum(dxc * _shift_down(xr, CONV_WIDTH - 1 - k, tp), axis=0, keepdims=True)
        vec_ref[4:5, :] = jnp.sum(dxc, axis=0, keepdims=True)
        vec_ref[5:6, :] = jnp.sum(dpa, axis=0, keepdims=True)
        vec_ref[6:7, :] = jnp.sum(dpx, axis=0, keepdims=True)
        vec_ref[7:8, :] = dl

    col = lambda g: (0, g)
    vec = pl.BlockSpec((1, LANES), col)
    big = pl.BlockSpec((tp, LANES), col)
    sq = pl.BlockSpec((None, LANES, LANES), lambda g: (g, 0, 0))
    return pl.pallas_call(
        body, name="rec_bwd", grid=(ng,),
        in_specs=[big] * 8 + [pl.BlockSpec((tp, LANES), lambda g: (0, ng + g)),
                                pl.BlockSpec((CONV_WIDTH, LANES), col), sq, sq, vec],
        out_specs=[big, big, sq, sq, pl.BlockSpec((None, SUBLANES, LANES), lambda g: (g, 0, 0))],
        out_shape=[jax.ShapeDtypeStruct((tp, rw), BF16), jax.ShapeDtypeStruct((tp, rw), BF16),
                   jax.ShapeDtypeStruct((ng, LANES, LANES), F32), jax.ShapeDtypeStruct((ng, LANES, LANES), F32),
                   jax.ShapeDtypeStruct((ng, SUBLANES, LANES), F32)],
        scratch_shapes=[pltpu.VMEM((tp, LANES), F32)] * 3,
        compiler_params=_cparams("parallel"),
    )(drec, hr, xc, *gates, rest, rest, convw, wga, wgx, lru)


def attn_bwd(qkv, do, o, lset, c, ct, nh):
    tp = qkv.shape[0]
    npair = nh // 2
    aw = nh * HEAD_DIM
    tiles = _att_tiles(tp)

    def body(q_ref, k_ref, v_ref, do_ref, o_ref, lset_ref, c_ref, ct_ref,
             dq_ref, dk_ref, dv_ref, drow_ref, dcol_ref, dk_acc, dv_acc, dq_t):
        p = pl.program_id(0)
        k_t = k_ref[...].T
        dk_acc[...] = jnp.zeros_like(dk_acc)
        dv_acc[...] = jnp.zeros_like(dv_acc)
        dcol_ref[...] = jnp.zeros_like(dcol_ref)
        drow_ref[...] = jnp.zeros_like(drow_ref)
        for r0, nr, nk in tiles:
            rs = slice(r0, r0 + nr)
            causal = (r0 + lax.broadcasted_iota(jnp.int32, (nk, nr), 1)
                      >= lax.broadcasted_iota(jnp.int32, (nk, nr), 0))
            cblk = c_ref[0:nk, :]
            ctb = ct_ref[:, rs]
            for hh in range(2):
                head = 2 * p + hh
                hs = slice(hh * HEAD_DIM, (hh + 1) * HEAD_DIM)
                q = q_ref[rs, hs]
                k = k_ref[0:nk, hs]
                dof = do_ref[rs, hs]
                do16 = dof.astype(BF16)
                delta = jnp.sum(dof * o_ref[rs, hs], axis=1, keepdims=True)
                delta_row = jnp.broadcast_to(delta, (nr, LANES)).T[0:1, :]
                s_t = _dot_nt(k, q * ATT_SCALE) + (_pick_row(ctb, head) - _pick_col(cblk, head))
                p_t = jnp.where(causal, jnp.exp(s_t - lset_ref[hh:hh + 1, rs]), 0.0)
                ds_t = p_t * (_dot_nt(v_ref[0:nk, hs], do16) - delta_row)
                p16 = p_t.astype(BF16)
                ds16 = ds_t.astype(BF16)
                dv_acc[0:nk, hs] += _dot(p16, do16)
                dk_acc[0:nk, hs] += _dot(ds16, q) * ATT_SCALE
                dq_t[hs, rs] = _dot(k_t[hs, 0:nk], ds16)
                drow_ref[hh:hh + 1, rs] = jnp.sum(ds_t, axis=0, keepdims=True)
                dcol_ref[0:nk, hs] -= jnp.broadcast_to(jnp.sum(ds_t, axis=1, keepdims=True), (nk, HEAD_DIM))
        dk_ref[...] = dk_acc[...].astype(BF16)
        dv_ref[...] = dv_acc[...].astype(BF16)
        dq_ref[...] = (dq_t[...].T * ATT_SCALE).astype(BF16)

    pair = lambda p: (0, p)
    return pl.pallas_call(
        body, name="attn_bwd", grid=(npair,),
        in_specs=[pl.BlockSpec((tp, LANES), pair),
                  pl.BlockSpec((tp, LANES), lambda p: (0, npair + p)),
                  pl.BlockSpec((tp, LANES), lambda p: (0, 2 * npair + p)),
                  pl.BlockSpec((tp, LANES), pair),
                  pl.BlockSpec((tp, LANES), pair),
                  pl.BlockSpec((None, SUBLANES, tp), lambda p: (p, 0, 0)),
                  pl.BlockSpec((tp, LANES), lambda p: (0, 0)),
                  pl.BlockSpec((SUBLANES, tp), lambda p: (0, 0))],
        out_specs=[pl.BlockSpec((tp, LANES), pair), pl.BlockSpec((tp, LANES), pair),
                   pl.BlockSpec((tp, LANES), pair),
                   pl.BlockSpec((None, SUBLANES, tp), lambda p: (p, 0, 0)),
                   pl.BlockSpec((tp, LANES), pair)],
        out_shape=[jax.ShapeDtypeStruct((tp, aw), BF16), jax.ShapeDtypeStruct((tp, aw), BF16),
                   jax.ShapeDtypeStruct((tp, aw), BF16),
                   jax.ShapeDtypeStruct((npair, SUBLANES, tp), F32),
                   jax.ShapeDtypeStruct((tp, aw), F32)],
        scratch_shapes=[pltpu.VMEM((tp, LANES), F32), pltpu.VMEM((tp, LANES), F32),
                        pltpu.VMEM((LANES, tp), F32)],
        compiler_params=_cparams("parallel"),
    )(qkv, qkv, qkv, do, o, lset, c, ct)


def fgate_bwd(dct8, drs, rest, bf_pad, fcol):
    tp = rest.shape[0]
    aw = drs.shape[1]
    nb = tp // ATT_BLOCK
    B = ATT_BLOCK

    def body(d_ref, drs_ref, f_ref, b_ref, dfl_ref, db_ref, pad_s, dc_s):
        r_i = lax.broadcasted_iota(jnp.int32, (B, B), 0)
        c_i = lax.broadcasted_iota(jnp.int32, (B, B), 1)
        triu = (c_i >= r_i).astype(BF16)
        sel = (lax.broadcasted_iota(jnp.int32, (aw, LANES), 0)
               == HEAD_DIM * lax.broadcasted_iota(jnp.int32, (aw, LANES), 1)).astype(BF16)
        pad_s[...] = jnp.zeros_like(pad_s)
        pad_s[0:SUBLANES, :] = d_ref[...]
        dc_s[...] = pad_s[...].T + _dot_split3(drs_ref[...], sel)
        carry = jnp.zeros((1, LANES), F32)
        for i in range(nb - 1, -1, -1):
            sl = slice(i * B, (i + 1) * B)
            rc = _split3_dot(triu, dc_s[sl, :])
            dc_s[sl, :] = rc + carry
            carry = carry + rc[0:1, :]
        dfl = dc_s[...] * _sigmoid(-(f_ref[...] + b_ref[...]))
        dfl_ref[...] = dfl.astype(BF16)
        db_ref[...] = jnp.sum(dfl, axis=0, keepdims=True)

    return pl.pallas_call(
        body, name="fgate_bwd", grid=(1,),
        in_specs=[pl.BlockSpec((SUBLANES, tp), lambda i: (0, 0)),
                  pl.BlockSpec((tp, aw), lambda i: (0, 0)),
                  pl.BlockSpec((tp, LANES), lambda i: (0, fcol)),
                  pl.BlockSpec((1, LANES), lambda i: (0, 0))],
        out_specs=[pl.BlockSpec((tp, LANES), lambda i: (0, 0)),
                   pl.BlockSpec((1, LANES), lambda i: (0, 0))],
        out_shape=[jax.ShapeDtypeStruct((tp, LANES), BF16), jax.ShapeDtypeStruct((1, LANES), F32)],
        scratch_shapes=[pltpu.VMEM((LANES, tp), F32), pltpu.VMEM((tp, LANES), F32)],
        compiler_params=_cparams("arbitrary"),
    )(dct8, drs, rest, bf_pad)


def in_proj_bwd(dh2, parts, w_in_t, wrest_t, h, g1, tm):
    tp, d = h.shape
    dq, dk, dv, dxr, dyr, dfl = parts
    aw, rw = dq.shape[1], dxr.shape[1]

    def body(dh2_ref, dq_ref, dk_ref, dv_ref, dxr_ref, dyr_ref, dfl_ref, wq_ref, wr_ref, h_ref, g_ref,
             dh_ref, dg_ref):
        i = pl.program_id(0)
        dz = _dot(dq_ref[...], wq_ref[0:aw, :])
        dz += _dot(dk_ref[...], wq_ref[aw:2 * aw, :])
        dz += _dot(dv_ref[...], wq_ref[2 * aw:3 * aw, :])
        dz += _dot(dxr_ref[...], wr_ref[0:rw, :])
        dz += _dot(dyr_ref[...], wr_ref[rw:2 * rw, :])
        dz += _dot(dfl_ref[...], wr_ref[2 * rw:2 * rw + LANES, :])
        dx, dg = _rms_bwd(h_ref[...], g_ref[...], dz)
        dh_ref[...] = dh2_ref[...] + dx
        _accumulate(dg_ref, dg, i == 0)

    row = lambda i: (i, 0)
    fix = lambda i: (0, 0)
    return pl.pallas_call(
        body, name="in_proj_bwd", grid=(tp // tm,),
        in_specs=[pl.BlockSpec((tm, d), row),
                  pl.BlockSpec((tm, aw), row), pl.BlockSpec((tm, aw), row), pl.BlockSpec((tm, aw), row),
                  pl.BlockSpec((tm, rw), row), pl.BlockSpec((tm, rw), row), pl.BlockSpec((tm, LANES), row),
                  pl.BlockSpec((3 * aw, d), fix), pl.BlockSpec(wrest_t.shape, fix),
                  pl.BlockSpec((tm, d), row), pl.BlockSpec((1, d), fix)],
        out_specs=[pl.BlockSpec((tm, d), row), pl.BlockSpec((1, d), fix)],
        out_shape=[jax.ShapeDtypeStruct((tp, d), F32), jax.ShapeDtypeStruct((1, d), F32)],
        compiler_params=_cparams("arbitrary"),
    )(dh2, dq, dk, dv, dxr, dyr, dfl, w_in_t, wrest_t, h, g1)


def dw_in_t(z, parts, nh, tr):
    tp, d = z.shape
    dq, dk, dv, dxr, dyr, dfl = parts
    aw, rw = dq.shape[1], dxr.shape[1]
    d_in = 3 * aw + nh + 2 * rw
    blk = d_in // N_DEV
    nr = tp // tr
    offs = [(0, aw), (aw, aw), (2 * aw, aw), (3 * aw + nh, rw), (3 * aw + nh + rw, rw)]

    def body(z_ref, dq_ref, dk_ref, dv_ref, dxr_ref, dyr_ref, dfl_ref, o_ref, acc):
        r = pl.program_id(0)

        @pl.when(r == 0)
        def _():
            acc[...] = jnp.zeros_like(acc)

        zt = z_ref[...]
        for (o, n), ref in zip(offs, (dq_ref, dk_ref, dv_ref, dxr_ref, dyr_ref)):
            acc[o:o + n, :] += _dot_tn(ref[...], zt)
        acc[3 * aw:3 * aw + nh, :] += _dot_tn(dfl_ref[...], zt)[0:nh, :]

        @pl.when(r == nr - 1)
        def _():
            for p in range(N_DEV):
                o_ref[p] = acc[p * blk:(p + 1) * blk, :].astype(BF16)

    row = lambda r: (r, 0)
    return pl.pallas_call(
        body, name="dw_in", grid=(nr,),
        in_specs=[pl.BlockSpec((tr, d), row),
                  pl.BlockSpec((tr, aw), row), pl.BlockSpec((tr, aw), row), pl.BlockSpec((tr, aw), row),
                  pl.BlockSpec((tr, rw), row), pl.BlockSpec((tr, rw), row), pl.BlockSpec((tr, LANES), row)],
        out_specs=pl.BlockSpec((N_DEV, blk, d), lambda r: (0, 0, 0)),
        out_shape=jax.ShapeDtypeStruct((N_DEV, blk, d), BF16),
        scratch_shapes=[pltpu.VMEM((d_in, d), F32)],
        compiler_params=_cparams("arbitrary"),
    )(z, dq, dk, dv, dxr, dyr, dfl)


def _place():
    return lax.axis_index("x"), lax.axis_index("y"), lax.axis_index("c")


HBM = pl.BlockSpec(memory_space=pltpu.HBM)
SEM = pl.BlockSpec(memory_space=pltpu.SEMAPHORE)
EFFECT = pltpu.SideEffectType.DATAFLOW_SIDE_EFFECTING


def _in_hbm(a):
    return pltpu.with_memory_space_constraint(a, pltpu.HBM)


def _as_list(a):
    return list(a) if isinstance(a, (list, tuple)) else [a]


def _gather_targets(x, y, c):
    return [(x, y, 1 - c), (1 - x, y, c), (x, 1 - y, c), (1 - x, 1 - y, c)]


def _slot(t):
    return 4 * t[0] + 2 * t[1] + t[2]


def gather_start(groups, name):
    flat = [a for g in groups for a in g]
    n = len(flat)
    ng = len(groups)
    lands = [lax.empty((N_DEV,) + a.shape, a.dtype) for a in flat]

    def body(*refs):
        src, land = refs[:n], refs[n:2 * n]
        sems = refs[2 * n:2 * n + 2 * ng]
        token = refs[-1]
        x, y, c = _place()
        me = 4 * x + 2 * y + c
        i = 0
        for gi, g in enumerate(groups):
            for a in range(len(g)):
                for k, t in enumerate(_gather_targets(x, y, c)):
                    pltpu.make_async_remote_copy(
                        src_ref=src[i], dst_ref=land[i].at[me],
                        send_sem=sems[2 * gi].at[4 * a + k], recv_sem=sems[2 * gi + 1].at[4 * a + k],
                        device_id=t, device_id_type=MESH).start()
                i += 1
        token[...] = jnp.zeros_like(token)

    sem_shapes = []
    for g in groups:
        sem_shapes += [pltpu.SemaphoreType.DMA((4 * len(g),)), pltpu.SemaphoreType.DMA((4 * len(g),))]
    out = pl.pallas_call(
        body, name=name,
        out_shape=sem_shapes + [pltpu.HBM(a.shape, a.dtype) for a in flat + lands]
        + [jax.ShapeDtypeStruct((SUBLANES, LANES), F32)],
        in_specs=[HBM] * (2 * n),
        out_specs=[SEM] * (2 * ng) + [HBM] * (2 * n) + [pl.BlockSpec(memory_space=pltpu.VMEM)],
        input_output_aliases={i: 2 * ng + i for i in range(2 * n)},
        compiler_params=pltpu.CompilerParams(has_side_effects=EFFECT),
    )(*[_in_hbm(a) for a in flat + lands])
    sems = out[:2 * ng]
    thru = out[2 * ng:2 * ng + 2 * n]
    srcs_t, lands_t = thru[:n], thru[n:]
    res, i = [], 0
    for gi, g in enumerate(groups):
        res.append((sems[2 * gi], sems[2 * gi + 1], srcs_t[i:i + len(g)], lands_t[i:i + len(g)]))
        i += len(g)
    return res, out[-1]


def gather_wait(send, recv, srcs, lands, after, name):
    n = len(srcs)

    def body(*refs):
        src, land = refs[:n], refs[n:2 * n]
        send_sem, recv_sem = refs[2 * n], refs[2 * n + 1]
        x, y, c = _place()
        for a in range(n):
            for k, t in enumerate(_gather_targets(x, y, c)):
                cp = pltpu.make_async_remote_copy(
                    src_ref=src[a], dst_ref=land[a].at[_slot(t)],
                    send_sem=send_sem.at[4 * a + k], recv_sem=recv_sem.at[4 * a + k],
                    device_id=t, device_id_type=MESH)
                cp.wait_send()
                cp.wait_recv()

    out = pl.pallas_call(
        body, name=name,
        out_shape=[pltpu.HBM(a.shape, a.dtype) for a in list(srcs) + list(lands)],
        in_specs=[HBM] * (2 * n) + [SEM, SEM] + [ANY] * len(_as_list(after)),
        out_specs=[HBM] * (2 * n),
        input_output_aliases={i: i for i in range(2 * n)},
        compiler_params=pltpu.CompilerParams(has_side_effects=EFFECT),
    )(*srcs, *lands, send, recv, *_as_list(after))
    return out[:n], out[n:]


def forward_start(lands, name):
    n = len(lands)

    def body(*refs):
        land = refs[:n]
        send_sem, recv_sem = refs[n], refs[n + 1]
        token = refs[-1]
        x, y, c = _place()
        for a in range(n):
            for j, chip in enumerate([(1 - x, y), (x, 1 - y), (1 - x, 1 - y)]):
                blk = land[a].at[_slot((*chip, c))]
                pltpu.make_async_remote_copy(src_ref=blk, dst_ref=blk, send_sem=send_sem.at[3 * a + j],
                                             recv_sem=recv_sem.at[3 * a + j], device_id=(x, y, 1 - c),
                                             device_id_type=MESH).start()
        token[...] = jnp.zeros_like(token)

    out = pl.pallas_call(
        body, name=name,
        out_shape=[pltpu.SemaphoreType.DMA((3 * n,)), pltpu.SemaphoreType.DMA((3 * n,))]
        + [pltpu.HBM(a.shape, a.dtype) for a in lands] + [jax.ShapeDtypeStruct((SUBLANES, LANES), F32)],
        in_specs=[HBM] * n,
        out_specs=[SEM, SEM] + [HBM] * n + [pl.BlockSpec(memory_space=pltpu.VMEM)],
        input_output_aliases={i: 2 + i for i in range(n)},
        compiler_params=pltpu.CompilerParams(has_side_effects=EFFECT),
    )(*[_in_hbm(a) for a in lands])
    return out[0], out[1], out[2:2 + n], out[-1][0, 0]


def forward_wait(send, recv, lands, after, name):
    n = len(lands)

    def body(*refs):
        land = refs[:n]
        send_sem, recv_sem = refs[n], refs[n + 1]
        x, y, c = _place()
        for a in range(n):
            for j, chip in enumerate([(1 - x, y), (x, 1 - y), (1 - x, 1 - y)]):
                cp = pltpu.make_async_remote_copy(
                    src_ref=land[a].at[_slot((*chip, c))], dst_ref=land[a].at[_slot((*chip, 1 - c))],
                    send_sem=send_sem.at[3 * a + j], recv_sem=recv_sem.at[3 * a + j],
                    device_id=(x, y, 1 - c), device_id_type=MESH)
                cp.wait_send()
                cp.wait_recv()

    return pl.pallas_call(
        body, name=name,
        out_shape=[pltpu.HBM(a.shape, a.dtype) for a in lands],
        in_specs=[HBM] * n + [SEM, SEM, ANY],
        out_specs=[HBM] * n,
        input_output_aliases={i: i for i in range(n)},
        compiler_params=pltpu.CompilerParams(has_side_effects=EFFECT),
    )(*lands, send, recv, after)


def _relations():
    return [(dx, dy, dc) for dx in (0, 1) for dy in (0, 1) for dc in (0, 1) if dx + dy + dc]


def _peer(x, y, c, rel):
    return ((1 - x) if rel[0] else x, (1 - y) if rel[1] else y, (1 - c) if rel[2] else c)


def exchange_start(srcs, lands, layer, name, after=()):
    n = len(srcs)
    after = _as_list(after)

    def body(*refs):
        src, land = refs[:n], refs[n:2 * n]
        send_sem, recv_sem = refs[2 * n + len(after)], refs[2 * n + len(after) + 1]
        token = refs[-1]
        x, y, c = _place()
        me = 4 * x + 2 * y + c
        for k, rel in enumerate(_relations()):
            peer = _peer(x, y, c, rel)
            for a in range(n):
                pltpu.make_async_remote_copy(
                    src_ref=src[a] if layer is None else src[a].at[_slot(peer)],
                    dst_ref=land[a].at[me] if layer is None else land[a].at[me, layer],
                    send_sem=send_sem.at[7 * a + k], recv_sem=recv_sem.at[7 * a + k],
                    device_id=peer, device_id_type=MESH).start()
        token[...] = jnp.zeros_like(token)

    out = pl.pallas_call(
        body, name=name,
        out_shape=[pltpu.SemaphoreType.DMA((7 * n,)), pltpu.SemaphoreType.DMA((7 * n,))]
        + [pltpu.HBM(a.shape, a.dtype) for a in list(srcs) + list(lands)]
        + [jax.ShapeDtypeStruct((SUBLANES, LANES), F32)],
        in_specs=[HBM] * (2 * n) + [ANY] * len(after),
        out_specs=[SEM, SEM] + [HBM] * (2 * n) + [pl.BlockSpec(memory_space=pltpu.VMEM)],
        input_output_aliases={i: 2 + i for i in range(2 * n)},
        compiler_params=pltpu.CompilerParams(has_side_effects=EFFECT),
    )(*[_in_hbm(a) for a in list(srcs) + list(lands)], *after)
    return out[0], out[1], out[2:2 + n], out[2 + n:2 + 2 * n], out[-1][0, 0]


def exchange_wait(send, recv, srcs, lands, after, layer, name):
    n = len(srcs)

    def body(*refs):
        src, land = refs[:n], refs[n:2 * n]
        send_sem, recv_sem = refs[2 * n], refs[2 * n + 1]
        x, y, c = _place()
        for k, rel in enumerate(_relations()):
            peer = _peer(x, y, c, rel)
            for a in range(n):
                cp = pltpu.make_async_remote_copy(
                    src_ref=src[a] if layer is None else src[a].at[_slot(peer)],
                    dst_ref=land[a].at[_slot(peer)] if layer is None else land[a].at[_slot(peer), layer],
                    send_sem=send_sem.at[7 * a + k], recv_sem=recv_sem.at[7 * a + k],
                    device_id=peer, device_id_type=MESH)
                cp.wait_send()
                cp.wait_recv()

    out = pl.pallas_call(
        body, name=name,
        out_shape=[pltpu.HBM(a.shape, a.dtype) for a in list(srcs) + list(lands)],
        in_specs=[HBM] * (2 * n) + [SEM, SEM] + [ANY] * len(_as_list(after)),
        out_specs=[HBM] * (2 * n),
        input_output_aliases={i: i for i in range(2 * n)},
        compiler_params=pltpu.CompilerParams(has_side_effects=EFFECT),
    )(*srcs, *lands, send, recv, *_as_list(after))
    return out[:n], out[n:]


def _adamw_math(g, w, m, v):
    m = ADAM_B1 * m + (1.0 - ADAM_B1) * g
    v = ADAM_B2 * v + (1.0 - ADAM_B2) * (g * g)
    m_hat = m / (1.0 - ADAM_B1 ** ADAM_STEP)
    v_hat = v / (1.0 - ADAM_B2 ** ADAM_STEP)
    delta = -ADAM_LR * (m_hat / (jnp.sqrt(v_hat) + ADAM_EPS) + ADAM_WD * w)
    return delta, m, v


def _sum_with_own(p_ref, own_refs, layer, me):
    own = own_refs[0][...]
    for k in range(1, len(own_refs)):
        own = jnp.where(layer == k, own_refs[k][...], own)
    g = None
    for p in range(p_ref.shape[0]):
        term = jnp.where(me == p, own, p_ref[p]).astype(F32)
        g = term if g is None else g + term
    return g


def sum_adamw(parts, owns, me, w, m, v, tr, name):
    npart, rows, cols = parts.shape
    nl = len(owns)
    per_layer = rows // nl // tr

    def body(me_ref, p_ref, *refs):
        own_refs = refs[:nl]
        w_ref, m_ref, v_ref, g_ref, d_ref, nm_ref, nv_ref = refs[nl:]
        g = _sum_with_own(p_ref, own_refs, pl.program_id(0) // per_layer, me_ref[0])
        delta, nm, nv = _adamw_math(g, w_ref[...], m_ref[...], v_ref[...])
        g_ref[...] = g
        d_ref[...] = delta
        nm_ref[...] = nm
        nv_ref[...] = nv

    blk = pl.BlockSpec((tr, cols), lambda i, me_ref: (i, 0))
    own_specs = [pl.BlockSpec((None, tr, cols),
                              lambda i, me_ref, l=l: (me_ref[0], jnp.clip(i - l * per_layer, 0, per_layer - 1), 0))
                 for l in range(nl)]
    return pl.pallas_call(
        body, name=name,
        grid_spec=pltpu.PrefetchScalarGridSpec(
            num_scalar_prefetch=1, grid=(rows // tr,),
            in_specs=[pl.BlockSpec((npart, tr, cols), lambda i, me_ref: (0, i, 0))] + own_specs + [blk, blk, blk],
            out_specs=[blk] * 4),
        out_shape=[jax.ShapeDtypeStruct((rows, cols), F32)] * 4,
        compiler_params=_cparams("arbitrary"),
    )(me, parts, *owns, w, m, v)


def sum_adamw_t(parts, owns, me, w, m, v, name):
    npart, nl, rows, cols = parts.shape

    def body(me_ref, p_ref, *refs):
        own_refs = refs[:nl]
        w_ref, m_ref, v_ref, g_ref, d_ref, nm_ref, nv_ref = refs[nl:]
        g = _sum_with_own(p_ref, own_refs, pl.program_id(0), me_ref[0])
        delta, nm, nv = _adamw_math(g, w_ref[...], m_ref[...], v_ref[...])
        g_ref[...] = g
        d_ref[...] = delta
        nm_ref[...] = nm
        nv_ref[...] = nv

    blk = pl.BlockSpec((None, rows, cols), lambda l, me_ref: (l, 0, 0))
    own_specs = [pl.BlockSpec((None, rows, cols), lambda l, me_ref: (me_ref[0], 0, 0)) for _ in range(nl)]
    return pl.pallas_call(
        body, name=name,
        grid_spec=pltpu.PrefetchScalarGridSpec(
            num_scalar_prefetch=1, grid=(nl,),
            in_specs=[pl.BlockSpec((npart, None, rows, cols), lambda l, me_ref: (0, l, 0, 0))] + own_specs
            + [blk, blk, blk],
            out_specs=[blk] * 4),
        out_shape=[jax.ShapeDtypeStruct((nl, rows, cols), F32)] * 4,
        compiler_params=_cparams("arbitrary"),
    )(me, parts, *owns, w, m, v)


def adamw_group(gs, ws, ms, vs, name):
    n = len(gs)

    def body(*refs):
        g, w, m, v, outs = refs[:n], refs[n:2 * n], refs[2 * n:3 * n], refs[3 * n:4 * n], refs[4 * n:]
        for i in range(n):
            delta, nm, nv = _adamw_math(g[i][...], w[i][...], m[i][...], v[i][...])
            outs[i][...] = delta
            outs[n + i][...] = nm
            outs[2 * n + i][...] = nv

    vmem = pl.BlockSpec(memory_space=pltpu.VMEM)
    out = pl.pallas_call(
        body, name=name,
        in_specs=[vmem] * (4 * n), out_specs=[vmem] * (3 * n),
        out_shape=[jax.ShapeDtypeStruct(a.shape, F32) for a in list(ws) * 3],
        compiler_params=_cparams(),
    )(*gs, *ws, *ms, *vs)
    return out[:n], out[n:2 * n], out[2 * n:]


def sum_parts(parts, name):
    npart, rows, cols = parts.shape

    def body(p_ref, g_ref):
        g = p_ref[0].astype(F32)
        for p in range(1, npart):
            g = g + p_ref[p].astype(F32)
        g_ref[...] = g

    return pl.pallas_call(
        body, name=name, grid=(1,),
        in_specs=[pl.BlockSpec((npart, rows, cols), lambda i: (0, 0, 0))],
        out_specs=pl.BlockSpec((rows, cols), lambda i: (0, 0)),
        out_shape=jax.ShapeDtypeStruct((rows, cols), F32),
        compiler_params=_cparams("arbitrary"),
    )(parts)


def _round_up(n, m):
    return (n + m - 1) // m * m


def _block_diag_pairs(w):
    nb, b, _ = w.shape
    per = LANES // b
    ng = nb // per
    w = w.reshape(ng, per, b, b)
    eye = jnp.eye(per, dtype=w.dtype)
    out = jnp.einsum('gpij,pq->gpiqj', w, eye).reshape(ng, LANES, LANES)
    return out.astype(BF16)


def _block_diag_extract(g, b):
    ng = g.shape[0]
    per = LANES // b
    g = g.reshape(ng, per, b, per, b)
    idx = jnp.arange(per)
    return g[:, idx, :, idx, :].transpose(1, 0, 2, 3).reshape(ng * per, b, b)


def _tiles(v):
    v = v.reshape(-1)
    n = _round_up(v.shape[0], SUBLANES * LANES)
    return jnp.pad(v, (0, n - v.shape[0])).reshape(-1, LANES)


SMALL = ['attn_norm_g', 'b_f', 'conv_w', 'conv_b', 'w_gate_a', 'b_gate_a', 'w_gate_x', 'b_gate_x',
         'lru_L', 'attn_out_g', 'rec_out_g', 'mlp_norm_g', 'final_g', 'meta']


def _pack(d):
    return jnp.concatenate([_tiles(d[n]) for n in SMALL], axis=0)


def _unpack(vec, shapes):
    out, r = {}, 0
    for n in SMALL:
        size = math.prod(shapes[n])
        nr = _round_up(size, SUBLANES * LANES) // LANES
        out[n] = vec[r:r + nr].reshape(-1)[:size].reshape(shapes[n])
        r += nr
    return out


def _row_tile(tp):
    return tp // 4 if (tp // 4) % 16 == 0 else tp


def local_step(x, tgt, meta, small, hooks):
    s, d = x.shape
    t_real = s + N_META
    tp = _round_up(t_real, ATT_BLOCK)
    depth = small['attn_norm_g'].shape[0]
    nh = small['b_f'].shape[1]
    rw = small['conv_b'].shape[1]
    blk = small['w_gate_a'].shape[2]
    tm = _row_tile(tp)
    tm2 = tp // 2
    fcol = 2 * rw // LANES

    h = jnp.concatenate([meta, x, jnp.zeros((tp - t_real, d), F32)], axis=0)
    tgt_p = jnp.pad(tgt, ((N_META, tp - t_real), (0, 0)))
    row = lambda v: v.reshape(1, -1)
    bf_pad = jnp.pad(small['b_f'], ((0, 0), (0, LANES - nh)))

    saved = []
    for l in range(depth):
        w_in_t, wrest_t, wout, tok_w = hooks.mixer_weights(l, h)
        wga = _block_diag_pairs(small['w_gate_a'][l])
        wgx = _block_diag_pairs(small['w_gate_x'][l])
        z, qkv, rest = in_proj(h, row(small['attn_norm_g'][l]) + tok_w, w_in_t, wrest_t, 3 * nh * HEAD_DIM, tm)
        c, ct = fgate_fwd(rest, bf_pad[l:l + 1], fcol)
        o, lset = attn_fwd(qkv, c, ct, nh)
        rec, hr, xc, *gates = rec_fwd(rest, small['conv_w'][l], row(small['conv_b'][l]), wga,
                                      row(small['b_gate_a'][l]), wgx, row(small['b_gate_x'][l]),
                                      row(small['lru_L'][l]), rw)
        gup, gdown, tok_w = hooks.mlp_weights(l, rec)
        h2, mix, z2, u, h3 = out_mlp_fwd(h, o, rec, row(small['attn_out_g'][l]), row(small['rec_out_g'][l]), wout,
                                         row(small['mlp_norm_g'][l]) + tok_w, gup, gdown, tm)
        saved.append(dict(h=h, z=z, qkv=qkv, rest=rest, c=c, ct=ct, o=o, lset=lset, rec=rec, hr=hr, xc=xc,
                          h2=h2, mix=mix, z2=z2, u=u, wga=wga, wgx=wgx, gates=gates,
                          w_in_t=w_in_t, wrest_t=wrest_t, wout=wout, gup=gup, gdown=gdown))
        h = h3

    dh, dgf, loss = loss_head(h, row(small['final_g']), tgt_p, t_real, tm)

    gs = {n: [None] * depth for n in SMALL if n not in ('final_g', 'meta')}
    tok = jnp.zeros((), F32)
    for l in reversed(range(depth)):
        sv = saved[l]
        gup, gdown = sv['gup'], sv['gdown']
        tf = gup.shape[2]
        dup, dh2, dg2, dhb = mlp_bwd(dh, sv['u'], sv['h2'], row(small['mlp_norm_g'][l]) + tok, gup, gdown, tm)
        gs['mlp_norm_g'][l] = dg2[0]
        do, drec, dga, dgr, dw_out = out_proj_bwd(dh2, sv['mix'], sv['o'], sv['rec'], row(small['attn_out_g'][l]),
                                                  row(small['rec_out_g'][l]), sv['wout'], tm)
        gs['attn_out_g'][l] = dga[0]
        gs['rec_out_g'][l] = dgr[0]
        dw_down, dw_up = dw_mlp(sv['u'], dhb, sv['z2'], dup, tf)
        blocks = dict(w_down=dw_down, w_up=dw_up, w_out=dw_out.reshape(N_DEV, d // N_DEV, d))
        tok = hooks.grads_ready(l, 'mlp', blocks)
        dxr, dyr, dwga, dwgx, vec = rec_bwd(drec, sv['hr'], sv['xc'], sv['gates'], sv['rest'], small['conv_w'][l],
                                            sv['wga'], sv['wgx'], row(small['lru_L'][l]) + tok, rw)
        gs['w_gate_a'][l] = _block_diag_extract(dwga, blk)
        gs['w_gate_x'][l] = _block_diag_extract(dwgx, blk)
        vec = vec.transpose(1, 0, 2).reshape(SUBLANES, rw)
        gs['conv_w'][l] = vec[0:CONV_WIDTH]
        gs['conv_b'][l] = vec[4]
        gs['b_gate_a'][l] = vec[5]
        gs['b_gate_x'][l] = vec[6]
        gs['lru_L'][l] = vec[7]
        dq, dk, dv, drow, dcol = attn_bwd(sv['qkv'], do, sv['o'], sv['lset'], sv['c'], sv['ct'] + tok, nh)
        drow8 = drow[:, 0:2, :].reshape(nh, tp)
        if nh < SUBLANES:
            drow8 = jnp.pad(drow8, ((0, SUBLANES - nh), (0, 0)))
        dfl, dbf = fgate_bwd(drow8, dcol, sv['rest'], bf_pad[l:l + 1], fcol)
        gs['b_f'][l] = dbf[0, 0:nh]
        parts = (dq, dk, dv, dxr, dyr, dfl)
        dh, dg1 = in_proj_bwd(dh2, parts, sv['w_in_t'], sv['wrest_t'], sv['h'], row(small['attn_norm_g'][l]), tm)
        gs['attn_norm_g'][l] = dg1[0]
        first = ()
        if l == 0:
            grads = {n: jnp.stack(v) for n, v in gs.items()}
            grads['final_g'] = dgf[0]
            grads['meta'] = dh[0:N_META]
            first = hooks.small_ready(grads)
        dw_in = dw_in_t(sv['z'], parts, nh, tm2)
        tok = hooks.grads_ready(l, 'in', dict(w_in=dw_in), first)

    return loss[0, 0], dh


def prep_weights(g_in, g_out, nh, rw):
    d = g_in.shape[2]
    w_in_t = g_in.reshape(-1, d)
    f0 = 3 * nh * HEAD_DIM
    wrest_t = jnp.concatenate([w_in_t[f0 + nh:f0 + nh + 2 * rw],
                               jnp.pad(w_in_t[f0:f0 + nh], ((0, LANES - nh), (0, 0)))], axis=0)
    return w_in_t, wrest_t, g_out.reshape(d, d)


BIG = ['w_in', 'w_out', 'w_up', 'w_down']
EXCHANGE_GROUPS = {'mlp': ['w_down', 'w_up', 'w_out'], 'in': ['w_in']}
WEIGHTS = ['meta', 'attn_norm_g', 'w_in', 'b_f', 'conv_w', 'conv_b', 'w_gate_a', 'b_gate_a', 'w_gate_x', 'b_gate_x',
           'lru_L', 'attn_out_g', 'rec_out_g', 'w_out', 'mlp_norm_g', 'w_up', 'w_down', 'final_g']


def _set_own(arr, own, me):
    return lax.dynamic_update_slice_in_dim(arr, own[None], me, axis=0)


class _Step:
    def __init__(self, w, nh, rw, me):
        self.w, self.nh, self.rw, self.me = w, nh, rw, me
        depth = w['w_in'].shape[0]
        first = [w['w_in_t'][:, 0, :].astype(BF16), w['w_out'][0].astype(BF16), w['meta'], w['conv_w']]
        self.pending, token = gather_start([first], "gather_start_0")
        zero = token[0, 0].astype(BF16)
        groups = [[w['w_up'][0].astype(BF16) + zero, w['w_down'][0].astype(BF16) + zero]]
        for l in range(1, depth):
            groups.append([w['w_in_t'][:, l, :].astype(BF16) + zero, w['w_out'][l].astype(BF16) + zero])
            groups.append([w['w_up'][l].astype(BF16) + zero, w['w_down'][l].astype(BF16) + zero])
        rest, _ = gather_start(groups, "gather_start_1")
        self.pending += rest
        self.first_after = rest[0][2][0]
        self.gathered = {}
        self.passing = {}
        self.token = jnp.zeros((), F32)
        self.lands = {n: lax.empty((N_DEV,) + w[n].shape, BF16) for n in BIG}
        din8, _, d = w['w_in_t'].shape
        self.lands['w_in'] = lax.empty((N_DEV, depth, din8, d), BF16)
        self.started = []
        self.small = None

    def _pass_on(self, gi, after):
        if gi < len(self.pending) and gi not in self.passing:
            send, recv, srcs, lands = self.pending[gi]
            srcs, lands = gather_wait(send, recv, srcs, lands, after, "gather_wait_%d" % gi)
            fsend, frecv, lands, token = forward_start(lands, "forward_start_%d" % gi)
            self.passing[gi] = (fsend, frecv, srcs, lands)
            self.token = token

    def group(self, gi, after):
        if gi not in self.gathered:
            self._pass_on(gi, after)
            fsend, frecv, srcs, lands = self.passing[gi]
            lands = forward_wait(fsend, frecv, lands, after, "forward_wait_%d" % gi)
            self.gathered[gi] = [_set_own(g, own, self.me) for g, own in zip(lands, srcs)]
            if gi >= 2:
                self._pass_on(gi + 1, lands[0])
        return self.gathered[gi]

    def mixer_weights(self, l, after):
        g = self.group(2 * l, after)
        return (*prep_weights(g[0], g[1], self.nh, self.rw), self.token)

    def mlp_weights(self, l, after):
        g = self.group(2 * l + 1, after)
        return g[0], g[1], self.token

    def grads_ready(self, l, group, blocks, after=()):
        names = EXCHANGE_GROUPS[group]
        send, recv, srcs, lands, token = exchange_start(
            [blocks[n] for n in names], [self.lands[n] for n in names], l, "exchange_start_%s_%d" % (group, l),
            after)
        for n, a in zip(names, lands):
            self.lands[n] = a
        self.started.append((l, group, send, recv, srcs))
        return token

    def small_ready(self, grads):
        self.small_shapes = {n: grads[n].shape for n in SMALL}
        packed = _pack(grads).astype(BF16)
        send, recv, srcs, lands, token = exchange_start(
            [packed], [lax.empty((N_DEV,) + packed.shape, BF16)], None, "small_start")
        self.small = (send, recv, srcs, lands)
        return srcs[0]

    def small_sum(self, after):
        send, recv, srcs, lands = self.small
        srcs, lands = exchange_wait(send, recv, srcs, lands, after, None, "small_wait")
        parts = _set_own(lands[0], srcs[0], self.me)
        return _unpack(sum_parts(parts, "sum_small_grads"), self.small_shapes)

    def received(self, group, after):
        names = EXCHANGE_GROUPS[group]
        own = {n: [None] * self.w[n].shape[0] for n in names}
        for l, grp, send, recv, srcs in self.started:
            if grp != group:
                continue
            srcs, lands = exchange_wait(send, recv, srcs, [self.lands[n] for n in names], after, l,
                                        "exchange_wait_%s_%d" % (group, l))
            for n, a, sr in zip(names, lands, srcs):
                self.lands[n] = a
                own[n][l] = sr
        return {n: (self.lands[n], own[n]) for n in names}


def kernel(x, meta, attn_norm_g, w_in, b_f, conv_w, conv_b, w_gate_a, b_gate_a, w_gate_x, b_gate_x, lru_L, attn_out_g, rec_out_g, w_out, mlp_norm_g, w_up, w_down, final_g, loss_target, m_meta, m_attn_norm_g, m_w_in, m_b_f, m_conv_w, m_conv_b, m_w_gate_a, m_b_gate_a, m_w_gate_x, m_b_gate_x, m_lru_L, m_attn_out_g, m_rec_out_g, m_w_out, m_mlp_norm_g, m_w_up, m_w_down, m_final_g, v_meta, v_attn_norm_g, v_w_in, v_b_f, v_conv_w, v_conv_b, v_w_gate_a, v_b_gate_a, v_w_gate_x, v_b_gate_x, v_lru_L, v_attn_out_g, v_rec_out_g, v_w_out, v_mlp_norm_g, v_w_up, v_w_down, v_final_g):
    w = dict(meta=meta, attn_norm_g=attn_norm_g, w_in=w_in, b_f=b_f, conv_w=conv_w, conv_b=conv_b,
             w_gate_a=w_gate_a, b_gate_a=b_gate_a, w_gate_x=w_gate_x, b_gate_x=b_gate_x, lru_L=lru_L,
             attn_out_g=attn_out_g, rec_out_g=rec_out_g, w_out=w_out, mlp_norm_g=mlp_norm_g, w_up=w_up,
             w_down=w_down, final_g=final_g)
    mo = dict(meta=m_meta, attn_norm_g=m_attn_norm_g, w_in=m_w_in, b_f=m_b_f, conv_w=m_conv_w, conv_b=m_conv_b,
              w_gate_a=m_w_gate_a, b_gate_a=m_b_gate_a, w_gate_x=m_w_gate_x, b_gate_x=m_b_gate_x, lru_L=m_lru_L,
              attn_out_g=m_attn_out_g, rec_out_g=m_rec_out_g, w_out=m_w_out, mlp_norm_g=m_mlp_norm_g,
              w_up=m_w_up, w_down=m_w_down, final_g=m_final_g)
    vo = dict(meta=v_meta, attn_norm_g=v_attn_norm_g, w_in=v_w_in, b_f=v_b_f, conv_w=v_conv_w, conv_b=v_conv_b,
              w_gate_a=v_w_gate_a, b_gate_a=v_b_gate_a, w_gate_x=v_w_gate_x, b_gate_x=v_b_gate_x, lru_L=v_lru_L,
              attn_out_g=v_attn_out_g, rec_out_g=v_rec_out_g, w_out=v_w_out, mlp_norm_g=v_mlp_norm_g,
              w_up=v_w_up, w_down=v_w_down, final_g=v_final_g)
    depth = w_in.shape[0]
    nh = b_f.shape[1]
    rw = conv_b.shape[1]
    me = 4 * lax.axis_index("x") + 2 * lax.axis_index("y") + lax.axis_index("c")

    w['w_in_t'] = jnp.transpose(w_in, (2, 0, 1))
    swap = lambda a: jnp.swapaxes(a, 1, 2)
    step = _Step(w, nh, rw, me)
    g0 = step.group(0, step.first_after)
    meta_full = g0[2].transpose(1, 0, 2).reshape(N_META, -1)
    conv_full = g0[3].transpose(1, 2, 0, 3).reshape(depth, CONV_WIDTH, rw)
    small = {n: w[n] for n in SMALL}
    small['conv_w'] = conv_full

    loss_part, dh0 = local_step(x[0], loss_target[0], meta_full, small, step)
    loss = lax.psum(loss_part, ("x", "y", "c"))
    grad_x = dh0[N_META:N_META + x.shape[1]][None]

    out_g, out_d, out_m, out_v = {}, {}, {}, {}
    me1 = me.reshape(1).astype(jnp.int32)

    def update_big(group, after):
        for n, (r, owns) in step.received(group, after).items():
            if n == 'w_in':
                out = sum_adamw_t(r, owns, me1, swap(w[n]), swap(mo[n]), swap(vo[n]), "adamw_w_in")
                out = [swap(a) for a in out]
            else:
                shp = w[n].shape
                rows, cols = shp[0] * shp[1], shp[2]
                tr = min(512 if cols <= 512 else 256, shp[1])
                out = sum_adamw(r.reshape(N_DEV, rows, cols), owns, me1, w[n].reshape(rows, cols),
                                mo[n].reshape(rows, cols), vo[n].reshape(rows, cols), tr, "adamw_" + n)
                out = [a.reshape(shp) for a in out]
            out_g[n], out_d[n], out_m[n], out_v[n] = out
            after = out[0]
        return after

    update_big('mlp', step.started[-1][4][0])

    gsum = step.small_sum([out_g[n] for n in EXCHANGE_GROUPS['mlp']])
    gsum['meta'] = lax.dynamic_slice_in_dim(gsum['meta'], me * meta.shape[1], meta.shape[1], axis=1)
    gsum['conv_w'] = lax.dynamic_slice_in_dim(gsum['conv_w'], me * conv_w.shape[2], conv_w.shape[2], axis=2)
    as2d = lambda a: a.reshape(-1, a.shape[-1])
    deltas, new_m, new_v = adamw_group([as2d(gsum[n]) for n in SMALL], [as2d(w[n]) for n in SMALL],
                                       [as2d(mo[n]) for n in SMALL], [as2d(vo[n]) for n in SMALL], "adamw_small")
    for i, n in enumerate(SMALL):
        out_g[n] = gsum[n]
        out_d[n], out_m[n], out_v[n] = [a[i].reshape(w[n].shape) for a in (deltas, new_m, new_v)]

    update_big('in', deltas[0])

    return (loss, grad_x, *[out_g[n] for n in WEIGHTS], *[out_d[n] for n in WEIGHTS],
            *[out_m[n] for n in WEIGHTS], *[out_v[n] for n in WEIGHTS])
```

```python
import math

import jax
import jax.numpy as jnp
from jax import lax
from jax.experimental import pallas as pl
from jax.experimental.pallas import tpu as pltpu

F32 = jnp.float32
BF16 = jnp.bfloat16

N_DEV = 8
N_META = 16
HEAD_DIM = 64
CONV_WIDTH = 4
RG_C = 8.0
NORM_EPS = 1e-6
LANES = 128
SUBLANES = 8
ATT_BLOCK = 128
ATT_TQ = 512
NEG_BIG = -1e30
ATT_SCALE = 1.0 / math.sqrt(HEAD_DIM)

ADAM_LR = 0.001
ADAM_B1 = 0.9
ADAM_B2 = 0.999
ADAM_EPS = 1e-08
ADAM_WD = 0.01
ADAM_STEP = 10

VMEM_LIMIT_BYTES = 56 * 1024 * 1024
MESH = pl.DeviceIdType.MESH
ANY = pl.BlockSpec(memory_space=pl.ANY)


def _cparams(*sem):
    return pltpu.CompilerParams(dimension_semantics=sem if sem else None,
                                vmem_limit_bytes=VMEM_LIMIT_BYTES)


def _dot(a, b):
    return jnp.dot(a, b, preferred_element_type=F32)


def _dot_nt(a, b):
    return lax.dot_general(a, b, (((1,), (1,)), ((), ())), preferred_element_type=F32)


def _dot_tn(a, b):
    return lax.dot_general(a, b, (((0,), (0,)), ((), ())), preferred_element_type=F32)


def _sigmoid(x):
    return 0.5 * (1.0 + jnp.tanh(0.5 * x))


def _log_sigmoid(x):
    return jnp.minimum(x, 0.0) - jnp.log(1.0 + jnp.exp(-jnp.abs(x)))


def _expm1(x):
    series = x * (1.0 + x * (0.5 + x * (1.0 / 6.0 + x * (1.0 / 24.0))))
    return jnp.where(jnp.abs(x) < 1e-2, series, jnp.exp(x) - 1.0)


_GELU_K = math.sqrt(2.0 / math.pi)
_GELU_C = 0.044715


def _gelu(x):
    t = jnp.tanh(_GELU_K * (x + _GELU_C * x * x * x))
    return 0.5 * x * (1.0 + t)


def _gelu_and_grad(x):
    x2 = x * x
    t = jnp.tanh(_GELU_K * (x + _GELU_C * x2 * x))
    half = 0.5 * (1.0 + t)
    return x * half, half + 0.5 * x * (1.0 - t * t) * _GELU_K * (1.0 + 3.0 * _GELU_C * x2)


def _split3_dot(tri, x):
    hi = x.astype(BF16)
    r1 = x - hi.astype(F32)
    mid = r1.astype(BF16)
    lo = (r1 - mid.astype(F32)).astype(BF16)
    return _dot(tri, hi) + _dot(tri, mid) + _dot(tri, lo)


def _dot_split3(x, sel):
    hi = x.astype(BF16)
    r1 = x - hi.astype(F32)
    mid = r1.astype(BF16)
    lo = (r1 - mid.astype(F32)).astype(BF16)
    return _dot(hi, sel) + _dot(mid, sel) + _dot(lo, sel)


def _rms_fwd(x, g):
    r = lax.rsqrt(jnp.mean(x * x, axis=-1, keepdims=True) + NORM_EPS)
    return x * r * g


def _rms_bwd(x, g, dy):
    r = lax.rsqrt(jnp.mean(x * x, axis=-1, keepdims=True) + NORM_EPS)
    xn = x * r
    dxn = dy * g
    dx = r * (dxn - xn * jnp.mean(dxn * xn, axis=-1, keepdims=True))
    return dx, jnp.sum(dy * xn, axis=0, keepdims=True)


def _accumulate(ref, val, first):
    @pl.when(first)
    def _():
        ref[...] = val

    @pl.when(jnp.logical_not(first))
    def _():
        ref[...] += val


def in_proj(h, g1, w_in_t, wrest_t, nq, tm):
    tp, d = h.shape
    nr = wrest_t.shape[0]

    def body(h_ref, g_ref, wq_ref, wr_ref, z_ref, qkv_ref, rest_ref):
        z = _rms_fwd(h_ref[...], g_ref[...]).astype(BF16)
        z_ref[...] = z
        qkv_ref[...] = _dot_nt(z, wq_ref[...]).astype(BF16)
        rest_ref[...] = _dot_nt(z, wr_ref[...])

    return pl.pallas_call(
        body, name="in_proj", grid=(tp // tm,),
        in_specs=[pl.BlockSpec((tm, d), lambda i: (i, 0)),
                  pl.BlockSpec((1, d), lambda i: (0, 0)),
                  pl.BlockSpec((nq, d), lambda i: (0, 0)),
                  pl.BlockSpec((nr, d), lambda i: (0, 0))],
        out_specs=[pl.BlockSpec((tm, d), lambda i: (i, 0)),
                   pl.BlockSpec((tm, nq), lambda i: (i, 0)),
                   pl.BlockSpec((tm, nr), lambda i: (i, 0))],
        out_shape=[jax.ShapeDtypeStruct((tp, d), BF16),
                   jax.ShapeDtypeStruct((tp, nq), BF16),
                   jax.ShapeDtypeStruct((tp, nr), F32)],
        compiler_params=_cparams("parallel"),
    )(h, g1, w_in_t, wrest_t)


def fgate_fwd(rest, bf_pad, fcol):
    tp = rest.shape[0]
    nb = tp // ATT_BLOCK

    def body(f_ref, b_ref, c_ref, ct_ref):
        r_i = lax.broadcasted_iota(jnp.int32, (ATT_BLOCK, ATT_BLOCK), 0)
        c_i = lax.broadcasted_iota(jnp.int32, (ATT_BLOCK, ATT_BLOCK), 1)
        tri = (r_i >= c_i).astype(BF16)
        c_ref[...] = _log_sigmoid(f_ref[...] + b_ref[...])
        carry = jnp.zeros((1, LANES), F32)
        for i in range(nb):
            sl = slice(i * ATT_BLOCK, (i + 1) * ATT_BLOCK)
            cs = _split3_dot(tri, c_ref[sl, :]) + carry
            carry = cs[ATT_BLOCK - 1:ATT_BLOCK, :]
            c_ref[sl, :] = cs
        ct_ref[...] = c_ref[...].T[0:SUBLANES, :]

    return pl.pallas_call(
        body, name="fgate_fwd", grid=(1,),
        in_specs=[pl.BlockSpec((tp, LANES), lambda i: (0, fcol)),
                  pl.BlockSpec((1, LANES), lambda i: (0, 0))],
        out_specs=[pl.BlockSpec((tp, LANES), lambda i: (0, 0)),
                   pl.BlockSpec((SUBLANES, tp), lambda i: (0, 0))],
        out_shape=[jax.ShapeDtypeStruct((tp, LANES), F32),
                   jax.ShapeDtypeStruct((SUBLANES, tp), F32)],
        compiler_params=_cparams("arbitrary"),
    )(rest, bf_pad)


def _pick_col(blk, head):
    lane = lax.broadcasted_iota(jnp.int32, blk.shape, 1)
    return jnp.sum(jnp.where(lane == head, blk, 0.0), axis=1, keepdims=True)


def _pick_row(blk, head):
    sub = lax.broadcasted_iota(jnp.int32, blk.shape, 0)
    return jnp.sum(jnp.where(sub == head, blk, 0.0), axis=0, keepdims=True)


def _att_tiles(tp):
    out, r0 = [], 0
    while r0 < tp:
        rows = min(ATT_TQ, tp - r0)
        out.append((r0, rows, r0 + rows))
        r0 += rows
    return out


def attn_fwd(qkv, c, ct, nh):
    tp = qkv.shape[0]
    npair = nh // 2
    tiles = _att_tiles(tp)

    def body(q_ref, k_ref, v_ref, c_ref, ct_ref, o_ref, lset_ref):
        p = pl.program_id(0)
        lset_ref[...] = jnp.zeros_like(lset_ref)
        for r0, nr, nk in tiles:
            rs = slice(r0, r0 + nr)
            causal = (r0 + lax.broadcasted_iota(jnp.int32, (nr, nk), 0)
                      >= lax.broadcasted_iota(jnp.int32, (nr, nk), 1))
            cblk = c_ref[rs, :]
            ctb = ct_ref[:, 0:nk]
            for hh in range(2):
                head = 2 * p + hh
                hs = slice(hh * HEAD_DIM, (hh + 1) * HEAD_DIM)
                q = q_ref[rs, hs] * ATT_SCALE
                s = _dot_nt(q, k_ref[0:nk, hs]) + (_pick_col(cblk, head) - _pick_row(ctb, head))
                s = jnp.where(causal, s, NEG_BIG)
                m = jnp.max(s, axis=1, keepdims=True)
                pm = jnp.exp(s - m)
                l = jnp.sum(pm, axis=1, keepdims=True)
                o_ref[rs, hs] = _dot(pm.astype(BF16), v_ref[0:nk, hs]) / l
                lse = m + jnp.log(l)
                lset_ref[hh:hh + 1, rs] = jnp.broadcast_to(lse, (nr, LANES)).T[0:1, :]

    pair = lambda p: (0, p)
    return pl.pallas_call(
        body, name="attn_fwd", grid=(npair,),
        in_specs=[pl.BlockSpec((tp, LANES), pair),
                  pl.BlockSpec((tp, LANES), lambda p: (0, npair + p)),
                  pl.BlockSpec((tp, LANES), lambda p: (0, 2 * npair + p)),
                  pl.BlockSpec((tp, LANES), lambda p: (0, 0)),
                  pl.BlockSpec((SUBLANES, tp), lambda p: (0, 0))],
        out_specs=[pl.BlockSpec((tp, LANES), pair),
                   pl.BlockSpec((None, SUBLANES, tp), lambda p: (p, 0, 0))],
        out_shape=[jax.ShapeDtypeStruct((tp, nh * HEAD_DIM), F32),
                   jax.ShapeDtypeStruct((npair, SUBLANES, tp), F32)],
        compiler_params=_cparams("parallel"),
    )(qkv, qkv, qkv, c, ct)


def _shift_down(x, k, n):
    if k == 0:
        return x
    rows = lax.broadcasted_iota(jnp.int32, x.shape, 0)
    return jnp.where(rows >= k, pltpu.roll(x, k, 0), 0.0)


def _shift_up(x, k, n):
    if k == 0:
        return x
    rows = lax.broadcasted_iota(jnp.int32, x.shape, 0)
    return jnp.where(rows < n - k, pltpu.roll(x, n - k, 0), 0.0)


def _conv_fwd(xr, cw_ref, cb_ref, n):
    xc = cw_ref[CONV_WIDTH - 1:CONV_WIDTH, :] * xr + cb_ref[...]
    for k in range(1, CONV_WIDTH):
        xc = xc + cw_ref[CONV_WIDTH - 1 - k:CONV_WIDTH - k, :] * _shift_down(xr, k, n)
    return xc


def _gates(xc, wga_ref, bga_ref, wgx_ref, bgx_ref, l_ref):
    xcb = xc.astype(BF16)
    r = _sigmoid(_dot(xcb, wga_ref[...]) + bga_ref[...])
    ig = _sigmoid(_dot(xcb, wgx_ref[...]) + bgx_ref[...])
    ls = _log_sigmoid(l_ref[...])
    log_a = RG_C * r * ls
    a = jnp.exp(log_a)
    mult = jnp.sqrt(-_expm1(2.0 * log_a))
    return xcb, r, ig, ls, log_a, a, mult


SCAN_UNROLL = 16


def _scan_rows(a_s, u_s, out_ref, n, reverse):
    nt = n // SUBLANES
    per = SCAN_UNROLL if nt % SCAN_UNROLL == 0 else 1
    row = lax.broadcasted_iota(jnp.int32, (SUBLANES, LANES), 0)
    last = 0 if reverse else SUBLANES - 1

    def tile_scan(a, u):
        for d in (1, 2, 4):
            if reverse:
                keep = row < SUBLANES - d
                sh = SUBLANES - d
            else:
                keep = row >= d
                sh = d
            a_sh = jnp.where(keep, pltpu.roll(a, sh, 0), 1.0)
            u_sh = jnp.where(keep, pltpu.roll(u, sh, 0), 0.0)
            u = a * u_sh + u
            a = a * a_sh
        return a, u

    def step(t, carry):
        tiles = []
        for k in range(per):
            tt = t * per + k
            if reverse:
                tt = nt - 1 - tt
            off = pl.multiple_of(tt * SUBLANES, SUBLANES)
            a, u = tile_scan(a_s[pl.ds(off, SUBLANES), :], u_s[pl.ds(off, SUBLANES), :])
            tiles.append((off, a, u))
        for off, a, u in tiles:
            out_ref[pl.ds(off, SUBLANES), :] = u + a * carry
            carry = u[last:last + 1, :] + a[last:last + 1, :] * carry
        return carry

    lax.fori_loop(0, nt // per, step, jnp.zeros((1, LANES), F32))


def rec_fwd(rest, convw, convb, wga, bga, wgx, bgx, lru, rw):
    tp = rest.shape[0]
    ng = rw // LANES

    def body(xr_ref, yr_ref, cw_ref, cb_ref, wga_ref, bga_ref, wgx_ref, bgx_ref, l_ref,
             rec_ref, hr_ref, xc_ref, r_ref, ig_ref, a_ref, mult_ref, u_s):
        xc = _conv_fwd(xr_ref[...], cw_ref, cb_ref, tp)
        xc_ref[...] = xc
        _, r, ig, ls, log_a, a, mult = _gates(xc, wga_ref, bga_ref, wgx_ref, bgx_ref, l_ref)
        r_ref[...] = r
        ig_ref[...] = ig
        a_ref[...] = a
        mult_ref[...] = mult
        u_s[...] = mult * ig * xc
        _scan_rows(a_ref, u_s, hr_ref, tp, reverse=False)
        rec_ref[...] = hr_ref[...] * _gelu(yr_ref[...])

    col = lambda g: (0, g)
    vec = pl.BlockSpec((1, LANES), col)
    big = pl.BlockSpec((tp, LANES), col)
    return pl.pallas_call(
        body, name="rec_fwd", grid=(ng,),
        in_specs=[big, pl.BlockSpec((tp, LANES), lambda g: (0, ng + g)),
                  pl.BlockSpec((CONV_WIDTH, LANES), col), vec,
                  pl.BlockSpec((None, LANES, LANES), lambda g: (g, 0, 0)), vec,
                  pl.BlockSpec((None, LANES, LANES), lambda g: (g, 0, 0)), vec, vec],
        out_specs=[big] * 7,
        out_shape=[jax.ShapeDtypeStruct((tp, rw), F32)] * 7,
        scratch_shapes=[pltpu.VMEM((tp, LANES), F32)],
        compiler_params=_cparams("parallel"),
    )(rest, rest, convw, convb, wga, bga, wgx, bgx, lru)


def out_mlp_fwd(h, o, rec, ga, gr, wout, g2, gup, gdown, tm):
    tp, d = h.shape
    aw, rw = o.shape[1], rec.shape[1]
    nf = gup.shape[0]
    tf = gup.shape[2]
    nb = MLP_BLOCKS if nf % MLP_BLOCKS == 0 else 1
    nj = nf // nb

    def body(h_ref, o_ref, rec_ref, ga_ref, gr_ref, w_ref, g2_ref, wu_ref, wd_ref,
             h2_ref, mix_ref, z2_ref, u_ref, h3_ref, acc):
        j = pl.program_id(1)

        @pl.when(j == 0)
        def _():
            mix_ref[:, 0:aw] = _rms_fwd(o_ref[...], ga_ref[...]).astype(BF16)
            mix_ref[:, aw:aw + rw] = _rms_fwd(rec_ref[...], gr_ref[...]).astype(BF16)
            h2 = h_ref[...] + _dot(mix_ref[...], w_ref[...])
            h2_ref[...] = h2
            acc[...] = h2
            z2_ref[...] = _rms_fwd(h2, g2_ref[...]).astype(BF16)

        z = z2_ref[...]
        part = None
        for b in range(nb):
            u = jnp.maximum(_dot(z, wu_ref[b]), 0.0)
            u_ref[:, b * tf:(b + 1) * tf] = u.astype(BF16)
            p = _dot((u * u).astype(BF16), wd_ref[b])
            part = p if part is None else part + p
        acc[...] += part

        @pl.when(j == nj - 1)
        def _():
            h3_ref[...] = acc[...]

    row = lambda i, j: (i, 0)
    fix = lambda i, j: (0, 0)
    return pl.pallas_call(
        body, name="out_mlp_fwd", grid=(tp // tm, nj),
        in_specs=[pl.BlockSpec((tm, d), row), pl.BlockSpec((tm, aw), row), pl.BlockSpec((tm, rw), row),
                  pl.BlockSpec((1, aw), fix), pl.BlockSpec((1, rw), fix),
                  pl.BlockSpec((d, d), fix), pl.BlockSpec((1, d), fix),
                  pl.BlockSpec((nb, d, tf), lambda i, j: (j, 0, 0)),
                  pl.BlockSpec((nb, tf, d), lambda i, j: (j, 0, 0))],
        out_specs=[pl.BlockSpec((tm, d), row), pl.BlockSpec((tm, d), row), pl.BlockSpec((tm, d), row),
                   pl.BlockSpec((tm, nb * tf), lambda i, j: (i, j)), pl.BlockSpec((tm, d), row)],
        out_shape=[jax.ShapeDtypeStruct((tp, d), F32), jax.ShapeDtypeStruct((tp, d), BF16),
                   jax.ShapeDtypeStruct((tp, d), BF16), jax.ShapeDtypeStruct((tp, nf * tf), BF16),
                   jax.ShapeDtypeStruct((tp, d), F32)],
        scratch_shapes=[pltpu.VMEM((tm, d), F32)],
        compiler_params=_cparams("parallel", "arbitrary"),
    )(h, o, rec, ga, gr, wout, g2, gup, gdown)


MLP_BLOCKS = 4


def loss_head(h, gf, tgt, t_real, tm):
    tp, d = h.shape

    def body(h_ref, g_ref, t_ref, dh_ref, dg_ref, loss_ref):
        i = pl.program_id(0)
        x = h_ref[...]
        g = g_ref[...]
        r = lax.rsqrt(jnp.mean(x * x, axis=-1, keepdims=True) + NORM_EPS)
        xn = x * r
        rows = i * tm + lax.broadcasted_iota(jnp.int32, (tm, 1), 0)
        valid = jnp.logical_and(rows >= N_META, rows < t_real)
        e = jnp.where(valid, xn * g - t_ref[...], 0.0)
        part = 0.5 * jnp.sum(jnp.sum(e * e, axis=1, keepdims=True) / d, axis=0, keepdims=True)
        dy = e / d
        dxn = dy * g
        dh_ref[...] = r * (dxn - xn * jnp.mean(dxn * xn, axis=-1, keepdims=True))
        _accumulate(dg_ref, jnp.sum(dy * xn, axis=0, keepdims=True), i == 0)
        _accumulate(loss_ref, jnp.broadcast_to(part, (1, LANES)), i == 0)

    row = lambda i: (i, 0)
    fix = lambda i: (0, 0)
    return pl.pallas_call(
        body, name="loss_head", grid=(tp // tm,),
        in_specs=[pl.BlockSpec((tm, d), row), pl.BlockSpec((1, d), fix), pl.BlockSpec((tm, d), row)],
        out_specs=[pl.BlockSpec((tm, d), row), pl.BlockSpec((1, d), fix), pl.BlockSpec((1, LANES), fix)],
        out_shape=[jax.ShapeDtypeStruct((tp, d), F32), jax.ShapeDtypeStruct((1, d), F32),
                   jax.ShapeDtypeStruct((1, LANES), F32)],
        compiler_params=_cparams("arbitrary"),
    )(h, gf, tgt)


def mlp_bwd(dh, u, h2, g2, gup, gdown, tm):
    tp, d = dh.shape
    nf = gup.shape[0]
    tf = gup.shape[2]
    nb = MLP_BLOCKS if nf % MLP_BLOCKS == 0 else 1
    nj = nf // nb
    ni = tp // tm

    def body(dh_ref, u_ref, h2_ref, g_ref, wu_ref, wd_ref, dup_ref, dh2_ref, dg_ref, dhb, acc):
        i = pl.program_id(0)
        j = pl.program_id(1)

        @pl.when(j == 0)
        def _():
            dhb[...] = dh_ref[...].astype(BF16)

        part = None
        for b in range(nb):
            cols = slice(b * tf, (b + 1) * tf)
            dup = (_dot_nt(dhb[...], wd_ref[b]) * (2.0 * u_ref[:, cols].astype(F32))).astype(BF16)
            dup_ref[:, cols] = dup
            p = _dot_nt(dup, wu_ref[b])
            part = p if part is None else part + p
        _accumulate(acc, part, j == 0)

        @pl.when(j == nj - 1)
        def _():
            dx, dg = _rms_bwd(h2_ref[...], g_ref[...], acc[...])
            dh2_ref[...] = dh_ref[...] + dx
            _accumulate(dg_ref, dg, i == 0)

    return pl.pallas_call(
        body, name="mlp_bwd", grid=(ni, nj),
        in_specs=[pl.BlockSpec((tm, d), lambda i, j: (i, 0)),
                  pl.BlockSpec((tm, nb * tf), lambda i, j: (i, j)),
                  pl.BlockSpec((tm, d), lambda i, j: (i, 0)),
                  pl.BlockSpec((1, d), lambda i, j: (0, 0)),
                  pl.BlockSpec((nb, d, tf), lambda i, j: (j, 0, 0)),
                  pl.BlockSpec((nb, tf, d), lambda i, j: (j, 0, 0))],
        out_specs=[pl.BlockSpec((tm, nb * tf), lambda i, j: (i, j)),
                   pl.BlockSpec((tm, d), lambda i, j: (i, 0)),
                   pl.BlockSpec((1, d), lambda i, j: (0, 0)),
                   pl.BlockSpec((tm, d), lambda i, j: (i, 0))],
        out_shape=[jax.ShapeDtypeStruct((tp, nf * tf), BF16), jax.ShapeDtypeStruct((tp, d), F32),
                   jax.ShapeDtypeStruct((1, d), F32), jax.ShapeDtypeStruct((tp, d), BF16)],
        scratch_shapes=[pltpu.VMEM((tm, d), F32)],
        compiler_params=_cparams("arbitrary", "arbitrary"),
    )(dh, u, h2, g2, gup, gdown)


def dw_mlp(u, dhb, z2, dup, tf):
    rows, dff = u.shape
    d = z2.shape[1]
    nf = dff // tf

    def body(u_ref, dh_ref, z_ref, dup_ref, dwd_ref, dwu_ref, zt):
        @pl.when(pl.program_id(0) == 0)
        def _():
            zt[...] = z_ref[...].T

        uf = u_ref[...].astype(F32)
        dwd_ref[...] = _dot_tn((uf * uf).astype(BF16), dh_ref[...]).astype(BF16)
        dwu_ref[...] = _dot(zt[...], dup_ref[...]).astype(BF16)

    col = lambda j: (0, j)
    fix = lambda j: (0, 0)
    return pl.pallas_call(
        body, name="dw_mlp", grid=(nf,),
        in_specs=[pl.BlockSpec((rows, tf), col), pl.BlockSpec((rows, d), fix),
                  pl.BlockSpec((rows, d), fix), pl.BlockSpec((rows, tf), col)],
        out_specs=[pl.BlockSpec((None, tf, d), lambda j: (j, 0, 0)),
                   pl.BlockSpec((None, d, tf), lambda j: (j, 0, 0))],
        out_shape=[jax.ShapeDtypeStruct((nf, tf, d), BF16), jax.ShapeDtypeStruct((nf, d, tf), BF16)],
        scratch_shapes=[pltpu.VMEM((d, rows), BF16)],
        compiler_params=_cparams("arbitrary"),
    )(u, dhb, z2, dup)


def out_proj_bwd(dh2, mix, o, rec, ga, gr, wout, tm):
    tp, d = dh2.shape
    aw, rw = o.shape[1], rec.shape[1]
    ni = tp // tm

    def body(dh_ref, mix_ref, o_ref, rec_ref, ga_ref, gr_ref, w_ref, do_ref, drec_ref, dga_ref, dgr_ref, dw_ref, acc):
        i = pl.program_id(0)
        dhb = dh_ref[...].astype(BF16)
        dmix = _dot_nt(dhb, w_ref[...])
        do, dga = _rms_bwd(o_ref[...], ga_ref[...], dmix[:, 0:aw])
        drec, dgr = _rms_bwd(rec_ref[...], gr_ref[...], dmix[:, aw:aw + rw])
        do_ref[...] = do
        drec_ref[...] = drec
        _accumulate(dga_ref, dga, i == 0)
        _accumulate(dgr_ref, dgr, i == 0)
        _accumulate(acc, _dot_tn(mix_ref[...], dhb), i == 0)

        @pl.when(i == ni - 1)
        def _():
            dw_ref[...] = acc[...].astype(BF16)

    row = lambda i: (i, 0)
    fix = lambda i: (0, 0)
    return pl.pallas_call(
        body, name="out_proj_bwd", grid=(ni,),
        in_specs=[pl.BlockSpec((tm, d), row), pl.BlockSpec((tm, d), row),
                  pl.BlockSpec((tm, aw), row), pl.BlockSpec((tm, rw), row),
                  pl.BlockSpec((1, aw), fix), pl.BlockSpec((1, rw), fix), pl.BlockSpec((d, d), fix)],
        out_specs=[pl.BlockSpec((tm, aw), row), pl.BlockSpec((tm, rw), row),
                   pl.BlockSpec((1, aw), fix), pl.BlockSpec((1, rw), fix), pl.BlockSpec((d, d), fix)],
        out_shape=[jax.ShapeDtypeStruct((tp, aw), F32), jax.ShapeDtypeStruct((tp, rw), F32),
                   jax.ShapeDtypeStruct((1, aw), F32), jax.ShapeDtypeStruct((1, rw), F32),
                   jax.ShapeDtypeStruct((d, d), BF16)],
        scratch_shapes=[pltpu.VMEM((d, d), F32)],
        compiler_params=_cparams("arbitrary"),
    )(dh2, mix, o, rec, ga, gr, wout)


def rec_bwd(drec, hr, xc, gates, rest, convw, wga, wgx, lru, rw):
    tp = rest.shape[0]
    ng = rw // LANES

    def body(drec_ref, hr_ref, xc_ref, r_ref, ig_ref, a_ref, mult_ref, xr_ref, yr_ref, cw_ref, wga_ref, wgx_ref,
             l_ref, dxr_ref, dyr_ref, dwga_ref, dwgx_ref, vec_ref, a_s, u_s, lam_s):
        xc = xc_ref[...]
        h = hr_ref[...]
        drec = drec_ref[...]
        r, ig, a, mult = r_ref[...], ig_ref[...], a_ref[...], mult_ref[...]
        xcb = xc.astype(BF16)
        ls = _log_sigmoid(l_ref[...])
        gelu, gelu_grad = _gelu_and_grad(yr_ref[...])
        dyr_ref[...] = (drec * h * gelu_grad).astype(BF16)
        a_s[...] = _shift_up(a, 1, tp)
        u_s[...] = drec * gelu
        _scan_rows(a_s, u_s, lam_s, tp, reverse=True)
        lam = lam_s[...]
        da = lam * _shift_down(h, 1, tp)
        dmult = lam * ig * xc
        dig = lam * mult * xc
        dxc = lam * mult * ig
        dlog_a = da * a - dmult * (a * a) / mult
        dr = dlog_a * (RG_C * ls)
        dl = jnp.sum(dlog_a * (RG_C * r), axis=0, keepdims=True) * _sigmoid(-l_ref[...])
        dpa = dr * r * (1.0 - r)
        dpx = dig * ig * (1.0 - ig)
        dpab = dpa.astype(BF16)
        dpxb = dpx.astype(BF16)
        dxc = dxc + _dot_nt(dpab, wga_ref[...]) + _dot_nt(dpxb, wgx_ref[...])
        dwga_ref[...] = _dot_tn(xcb, dpab)
        dwgx_ref[...] = _dot_tn(xcb, dpxb)
        xr = xr_ref[...]
        dxr = cw_ref[CONV_WIDTH - 1:CONV_WIDTH, :] * dxc
        for k in range(1, CONV_WIDTH):
            dxr = dxr + cw_ref[CONV_WIDTH - 1 - k:CONV_WIDTH - k, :] * _shift_up(dxc, k, tp)
        dxr_ref[...] = dxr.astype(BF16)
        for k in range(CONV_WIDTH):
            vec_ref[k:k + 1, :] = jnp.sum(dxc * _shift_down(xr, CONV_WIDTH - 1 - k, tp), axis=0, keepdims=True)
        vec_ref[4:5, :] = jnp.sum(dxc, axis=0, keepdims=True)
        vec_ref[5:6, :] = jnp.sum(dpa, axis=0, keepdims=True)
        vec_ref[6:7, :] = jnp.sum(dpx, axis=0, keepdims=True)
        vec_ref[7:8, :] = dl

    col = lambda g: (0, g)
    vec = pl.BlockSpec((1, LANES), col)
    big = pl.BlockSpec((tp, LANES), col)
    sq = pl.BlockSpec((None, LANES, LANES), lambda g: (g, 0, 0))
    return pl.pallas_call(
        body, name="rec_bwd", grid=(ng,),
        in_specs=[big] * 8 + [pl.BlockSpec((tp, LANES), lambda g: (0, ng + g)),
                                pl.BlockSpec((CONV_WIDTH, LANES), col), sq, sq, vec],
        out_specs=[big, big, sq, sq, pl.BlockSpec((None, SUBLANES, LANES), lambda g: (g, 0, 0))],
        out_shape=[jax.ShapeDtypeStruct((tp, rw), BF16), jax.ShapeDtypeStruct((tp, rw), BF16),
                   jax.ShapeDtypeStruct((ng, LANES, LANES), F32), jax.ShapeDtypeStruct((ng, LANES, LANES), F32),
                   jax.ShapeDtypeStruct((ng, SUBLANES, LANES), F32)],
        scratch_shapes=[pltpu.VMEM((tp, LANES), F32)] * 3,
        compiler_params=_cparams("parallel"),
    )(drec, hr, xc, *gates, rest, rest, convw, wga, wgx, lru)


def attn_bwd(qkv, do, o, lset, c, ct, nh):
    tp = qkv.shape[0]
    npair = nh // 2
    aw = nh * HEAD_DIM
    tiles = _att_tiles(tp)

    def body(q_ref, k_ref, v_ref, do_ref, o_ref, lset_ref, c_ref, ct_ref,
             dq_ref, dk_ref, dv_ref, drow_ref, dcol_ref, dk_acc, dv_acc, dq_t):
        p = pl.program_id(0)
        k_t = k_ref[...].T
        dk_acc[...] = jnp.zeros_like(dk_acc)
        dv_acc[...] = jnp.zeros_like(dv_acc)
        dcol_ref[...] = jnp.zeros_like(dcol_ref)
        drow_ref[...] = jnp.zeros_like(drow_ref)
        for r0, nr, nk in tiles:
            rs = slice(r0, r0 + nr)
            causal = (r0 + lax.broadcasted_iota(jnp.int32, (nk, nr), 1)
                      >= lax.broadcasted_iota(jnp.int32, (nk, nr), 0))
            cblk = c_ref[0:nk, :]
            ctb = ct_ref[:, rs]
            for hh in range(2):
                head = 2 * p + hh
                hs = slice(hh * HEAD_DIM, (hh + 1) * HEAD_DIM)
                q = q_ref[rs, hs]
                k = k_ref[0:nk, hs]
                dof = do_ref[rs, hs]
                do16 = dof.astype(BF16)
                delta = jnp.sum(dof * o_ref[rs, hs], axis=1, keepdims=True)
                delta_row = jnp.broadcast_to(delta, (nr, LANES)).T[0:1, :]
                s_t = _dot_nt(k, q * ATT_SCALE) + (_pick_row(ctb, head) - _pick_col(cblk, head))
                p_t = jnp.where(causal, jnp.exp(s_t - lset_ref[hh:hh + 1, rs]), 0.0)
                ds_t = p_t * (_dot_nt(v_ref[0:nk, hs], do16) - delta_row)
                p16 = p_t.astype(BF16)
                ds16 = ds_t.astype(BF16)
                dv_acc[0:nk, hs] += _dot(p16, do16)
                dk_acc[0:nk, hs] += _dot(ds16, q) * ATT_SCALE
                dq_t[hs, rs] = _dot(k_t[hs, 0:nk], ds16)
                drow_ref[hh:hh + 1, rs] = jnp.sum(ds_t, axis=0, keepdims=True)
                dcol_ref[0:nk, hs] -= jnp.broadcast_to(jnp.sum(ds_t, axis=1, keepdims=True), (nk, HEAD_DIM))
        dk_ref[...] = dk_acc[...].astype(BF16)
        dv_ref[...] = dv_acc[...].astype(BF16)
        dq_ref[...] = (dq_t[...].T * ATT_SCALE).astype(BF16)

    pair = lambda p: (0, p)
    return pl.pallas_call(
        body, name="attn_bwd", grid=(npair,),
        in_specs=[pl.BlockSpec((tp, LANES), pair),
                  pl.BlockSpec((tp, LANES), lambda p: (0, npair + p)),
                  pl.BlockSpec((tp, LANES), lambda p: (0, 2 * npair + p)),
                  pl.BlockSpec((tp, LANES), pair),
                  pl.BlockSpec((tp, LANES), pair),
                  pl.BlockSpec((None, SUBLANES, tp), lambda p: (p, 0, 0)),
                  pl.BlockSpec((tp, LANES), lambda p: (0, 0)),
                  pl.BlockSpec((SUBLANES, tp), lambda p: (0, 0))],
        out_specs=[pl.BlockSpec((tp, LANES), pair), pl.BlockSpec((tp, LANES), pair),
                   pl.BlockSpec((tp, LANES), pair),
                   pl.BlockSpec((None, SUBLANES, tp), lambda p: (p, 0, 0)),
                   pl.BlockSpec((tp, LANES), pair)],
        out_shape=[jax.ShapeDtypeStruct((tp, aw), BF16), jax.ShapeDtypeStruct((tp, aw), BF16),
                   jax.ShapeDtypeStruct((tp, aw), BF16),
                   jax.ShapeDtypeStruct((npair, SUBLANES, tp), F32),
                   jax.ShapeDtypeStruct((tp, aw), F32)],
        scratch_shapes=[pltpu.VMEM((tp, LANES), F32), pltpu.VMEM((tp, LANES), F32),
                        pltpu.VMEM((LANES, tp), F32)],
        compiler_params=_cparams("parallel"),
    )(qkv, qkv, qkv, do, o, lset, c, ct)


def fgate_bwd(dct8, drs, rest, bf_pad, fcol):
    tp = rest.shape[0]
    aw = drs.shape[1]
    nb = tp // ATT_BLOCK
    B = ATT_BLOCK

    def body(d_ref, drs_ref, f_ref, b_ref, dfl_ref, db_ref, pad_s, dc_s):
        r_i = lax.broadcasted_iota(jnp.int32, (B, B), 0)
        c_i = lax.broadcasted_iota(jnp.int32, (B, B), 1)
        triu = (c_i >= r_i).astype(BF16)
        sel = (lax.broadcasted_iota(jnp.int32, (aw, LANES), 0)
               == HEAD_DIM * lax.broadcasted_iota(jnp.int32, (aw, LANES), 1)).astype(BF16)
        pad_s[...] = jnp.zeros_like(pad_s)
        pad_s[0:SUBLANES, :] = d_ref[...]
        dc_s[...] = pad_s[...].T + _dot_split3(drs_ref[...], sel)
        carry = jnp.zeros((1, LANES), F32)
        for i in range(nb - 1, -1, -1):
            sl = slice(i * B, (i + 1) * B)
            rc = _split3_dot(triu, dc_s[sl, :])
            dc_s[sl, :] = rc + carry
            carry = carry + rc[0:1, :]
        dfl = dc_s[...] * _sigmoid(-(f_ref[...] + b_ref[...]))
        dfl_ref[...] = dfl.astype(BF16)
        db_ref[...] = jnp.sum(dfl, axis=0, keepdims=True)

    return pl.pallas_call(
        body, name="fgate_bwd", grid=(1,),
        in_specs=[pl.BlockSpec((SUBLANES, tp), lambda i: (0, 0)),
                  pl.BlockSpec((tp, aw), lambda i: (0, 0)),
                  pl.BlockSpec((tp, LANES), lambda i: (0, fcol)),
                  pl.BlockSpec((1, LANES), lambda i: (0, 0))],
        out_specs=[pl.BlockSpec((tp, LANES), lambda i: (0, 0)),
                   pl.BlockSpec((1, LANES), lambda i: (0, 0))],
        out_shape=[jax.ShapeDtypeStruct((tp, LANES), BF16), jax.ShapeDtypeStruct((1, LANES), F32)],
        scratch_shapes=[pltpu.VMEM((LANES, tp), F32), pltpu.VMEM((tp, LANES), F32)],
        compiler_params=_cparams("arbitrary"),
    )(dct8, drs, rest, bf_pad)


def in_proj_bwd(dh2, parts, w_in_t, wrest_t, h, g1, tm, x_rows=0):
    tp, d = h.shape
    dq, dk, dv, dxr, dyr, dfl = parts
    aw, rw = dq.shape[1], dxr.shape[1]
    ni = tp // tm

    def body(dh2_ref, dq_ref, dk_ref, dv_ref, dxr_ref, dyr_ref, dfl_ref, wq_ref, wr_ref, h_ref, g_ref,
             dh_ref, dg_ref, *gx_ref):
        i = pl.program_id(0)
        dz = _dot(dq_ref[...], wq_ref[0:aw, :])
        dz += _dot(dk_ref[...], wq_ref[aw:2 * aw, :])
        dz += _dot(dv_ref[...], wq_ref[2 * aw:3 * aw, :])
        dz += _dot(dxr_ref[...], wr_ref[0:rw, :])
        dz += _dot(dyr_ref[...], wr_ref[rw:2 * rw, :])
        dz += _dot(dfl_ref[...], wr_ref[2 * rw:2 * rw + LANES, :])
        dx, dg = _rms_bwd(h_ref[...], g_ref[...], dz)
        dh_ref[...] = dh2_ref[...] + dx
        _accumulate(dg_ref, dg, i == 0)
        for k in range(ni if x_rows else 0):
            lo, hi = max(k * tm, N_META), min((k + 1) * tm, N_META + x_rows)
            if hi > lo:
                @pl.when(i == k)
                def _(lo=lo, hi=hi, k=k):
                    pltpu.sync_copy(dh_ref.at[pl.ds(lo - k * tm, hi - lo), :],
                                    gx_ref[0].at[pl.ds(lo - N_META, hi - lo), :])

    row = lambda i: (i, 0)
    fix = lambda i: (0, 0)
    return pl.pallas_call(
        body, name="in_proj_bwd", grid=(ni,),
        in_specs=[pl.BlockSpec((tm, d), row),
                  pl.BlockSpec((tm, aw), row), pl.BlockSpec((tm, aw), row), pl.BlockSpec((tm, aw), row),
                  pl.BlockSpec((tm, rw), row), pl.BlockSpec((tm, rw), row), pl.BlockSpec((tm, LANES), row),
                  pl.BlockSpec((3 * aw, d), fix), pl.BlockSpec(wrest_t.shape, fix),
                  pl.BlockSpec((tm, d), row), pl.BlockSpec((1, d), fix)],
        out_specs=[pl.BlockSpec((tm, d), row), pl.BlockSpec((1, d), fix)] + ([ANY] if x_rows else []),
        out_shape=[jax.ShapeDtypeStruct((tp, d), F32), jax.ShapeDtypeStruct((1, d), F32)]
        + ([jax.ShapeDtypeStruct((x_rows, d), F32)] if x_rows else []),
        compiler_params=_cparams("arbitrary"),
    )(dh2, dq, dk, dv, dxr, dyr, dfl, w_in_t, wrest_t, h, g1)


def dw_in_t(z, parts, nh, tr):
    tp, d = z.shape
    dq, dk, dv, dxr, dyr, dfl = parts
    aw, rw = dq.shape[1], dxr.shape[1]
    d_in = 3 * aw + nh + 2 * rw
    blk = d_in // N_DEV
    nr = tp // tr
    offs = [(0, aw), (aw, aw), (2 * aw, aw), (3 * aw + nh, rw), (3 * aw + nh + rw, rw)]

    def body(z_ref, dq_ref, dk_ref, dv_ref, dxr_ref, dyr_ref, dfl_ref, o_ref, acc):
        r = pl.program_id(0)

        @pl.when(r == 0)
        def _():
            acc[...] = jnp.zeros_like(acc)

        zt = z_ref[...]
        for (o, n), ref in zip(offs, (dq_ref, dk_ref, dv_ref, dxr_ref, dyr_ref)):
            acc[o:o + n, :] += _dot_tn(ref[...], zt)
        acc[3 * aw:3 * aw + nh, :] += _dot_tn(dfl_ref[...], zt)[0:nh, :]

        @pl.when(r == nr - 1)
        def _():
            for p in range(N_DEV):
                o_ref[p] = acc[p * blk:(p + 1) * blk, :].astype(BF16)

    row = lambda r: (r, 0)
    return pl.pallas_call(
        body, name="dw_in", grid=(nr,),
        in_specs=[pl.BlockSpec((tr, d), row),
                  pl.BlockSpec((tr, aw), row), pl.BlockSpec((tr, aw), row), pl.BlockSpec((tr, aw), row),
                  pl.BlockSpec((tr, rw), row), pl.BlockSpec((tr, rw), row), pl.BlockSpec((tr, LANES), row)],
        out_specs=pl.BlockSpec((N_DEV, blk, d), lambda r: (0, 0, 0)),
        out_shape=jax.ShapeDtypeStruct((N_DEV, blk, d), BF16),
        scratch_shapes=[pltpu.VMEM((d_in, d), F32)],
        compiler_params=_cparams("arbitrary"),
    )(z, dq, dk, dv, dxr, dyr, dfl)


def _place():
    return lax.axis_index("x"), lax.axis_index("y"), lax.axis_index("c")


HBM = pl.BlockSpec(memory_space=pltpu.HBM)
SEM = pl.BlockSpec(memory_space=pltpu.SEMAPHORE)
EFFECT = pltpu.SideEffectType.DATAFLOW_SIDE_EFFECTING


def _in_hbm(a):
    return pltpu.with_memory_space_constraint(a, pltpu.HBM)


def _as_list(a):
    return list(a) if isinstance(a, (list, tuple)) else [a]


def _gather_targets(x, y, c):
    return [(x, y, 1 - c), (1 - x, y, c), (x, 1 - y, c), (1 - x, 1 - y, c)]


def _slot(t):
    return 4 * t[0] + 2 * t[1] + t[2]


def gather_start(groups, name):
    flat = [a for g in groups for a in g]
    n = len(flat)
    ng = len(groups)
    lands = [lax.empty((N_DEV,) + a.shape, a.dtype) for a in flat]

    def body(*refs):
        src, land = refs[:n], refs[n:2 * n]
        sems = refs[2 * n:2 * n + 2 * ng]
        token = refs[-1]
        x, y, c = _place()
        me = 4 * x + 2 * y + c
        i = 0
        for gi, g in enumerate(groups):
            for a in range(len(g)):
                for k, t in enumerate(_gather_targets(x, y, c)):
                    pltpu.make_async_remote_copy(
                        src_ref=src[i], dst_ref=land[i].at[me],
                        send_sem=sems[2 * gi].at[4 * a + k], recv_sem=sems[2 * gi + 1].at[4 * a + k],
                        device_id=t, device_id_type=MESH).start()
                i += 1
        token[...] = jnp.zeros_like(token)

    sem_shapes = []
    for g in groups:
        sem_shapes += [pltpu.SemaphoreType.DMA((4 * len(g),)), pltpu.SemaphoreType.DMA((4 * len(g),))]
    out = pl.pallas_call(
        body, name=name,
        out_shape=sem_shapes + [pltpu.HBM(a.shape, a.dtype) for a in flat + lands]
        + [jax.ShapeDtypeStruct((SUBLANES, LANES), F32)],
        in_specs=[HBM] * (2 * n),
        out_specs=[SEM] * (2 * ng) + [HBM] * (2 * n) + [pl.BlockSpec(memory_space=pltpu.VMEM)],
        input_output_aliases={i: 2 * ng + i for i in range(2 * n)},
        compiler_params=pltpu.CompilerParams(has_side_effects=EFFECT),
    )(*[_in_hbm(a) for a in flat + lands])
    sems = out[:2 * ng]
    thru = out[2 * ng:2 * ng + 2 * n]
    srcs_t, lands_t = thru[:n], thru[n:]
    res, i = [], 0
    for gi, g in enumerate(groups):
        res.append((sems[2 * gi], sems[2 * gi + 1], srcs_t[i:i + len(g)], lands_t[i:i + len(g)]))
        i += len(g)
    return res, out[-1]


def gather_wait(send, recv, srcs, lands, after, name):
    n = len(srcs)

    def body(*refs):
        src, land = refs[:n], refs[n:2 * n]
        send_sem, recv_sem = refs[2 * n], refs[2 * n + 1]
        x, y, c = _place()
        for a in range(n):
            for k, t in enumerate(_gather_targets(x, y, c)):
                cp = pltpu.make_async_remote_copy(
                    src_ref=src[a], dst_ref=land[a].at[_slot(t)],
                    send_sem=send_sem.at[4 * a + k], recv_sem=recv_sem.at[4 * a + k],
                    device_id=t, device_id_type=MESH)
                cp.wait_send()
                cp.wait_recv()

    out = pl.pallas_call(
        body, name=name,
        out_shape=[pltpu.HBM(a.shape, a.dtype) for a in list(srcs) + list(lands)],
        in_specs=[HBM] * (2 * n) + [SEM, SEM] + [ANY] * len(_as_list(after)),
        out_specs=[HBM] * (2 * n),
        input_output_aliases={i: i for i in range(2 * n)},
        compiler_params=pltpu.CompilerParams(has_side_effects=EFFECT),
    )(*srcs, *lands, send, recv, *_as_list(after))
    return out[:n], out[n:]


def forward_start(lands, name):
    n = len(lands)

    def body(*refs):
        land = refs[:n]
        send_sem, recv_sem = refs[n], refs[n + 1]
        token = refs[-1]
        x, y, c = _place()
        for a in range(n):
            for j, chip in enumerate([(1 - x, y), (x, 1 - y), (1 - x, 1 - y)]):
                blk = land[a].at[_slot((*chip, c))]
                pltpu.make_async_remote_copy(src_ref=blk, dst_ref=blk, send_sem=send_sem.at[3 * a + j],
                                             recv_sem=recv_sem.at[3 * a + j], device_id=(x, y, 1 - c),
                                             device_id_type=MESH).start()
        token[...] = jnp.zeros_like(token)

    out = pl.pallas_call(
        body, name=name,
        out_shape=[pltpu.SemaphoreType.DMA((3 * n,)), pltpu.SemaphoreType.DMA((3 * n,))]
        + [pltpu.HBM(a.shape, a.dtype) for a in lands] + [jax.ShapeDtypeStruct((SUBLANES, LANES), F32)],
        in_specs=[HBM] * n,
        out_specs=[SEM, SEM] + [HBM] * n + [pl.BlockSpec(memory_space=pltpu.VMEM)],
        input_output_aliases={i: 2 + i for i in range(n)},
        compiler_params=pltpu.CompilerParams(has_side_effects=EFFECT),
    )(*[_in_hbm(a) for a in lands])
    return out[0], out[1], out[2:2 + n], out[-1][0, 0]


def forward_wait(send, recv, lands, after, name):
    n = len(lands)

    def body(*refs):
        land = refs[:n]
        send_sem, recv_sem = refs[n], refs[n + 1]
        x, y, c = _place()
        for a in range(n):
            for j, chip in enumerate([(1 - x, y), (x, 1 - y), (1 - x, 1 - y)]):
                cp = pltpu.make_async_remote_copy(
                    src_ref=land[a].at[_slot((*chip, c))], dst_ref=land[a].at[_slot((*chip, 1 - c))],
                    send_sem=send_sem.at[3 * a + j], recv_sem=recv_sem.at[3 * a + j],
                    device_id=(x, y, 1 - c), device_id_type=MESH)
                cp.wait_send()
                cp.wait_recv()

    return pl.pallas_call(
        body, name=name,
        out_shape=[pltpu.HBM(a.shape, a.dtype) for a in lands],
        in_specs=[HBM] * n + [SEM, SEM, ANY],
        out_specs=[HBM] * n,
        input_output_aliases={i: i for i in range(n)},
        compiler_params=pltpu.CompilerParams(has_side_effects=EFFECT),
    )(*lands, send, recv, after)


def _relations():
    return [(dx, dy, dc) for dx in (0, 1) for dy in (0, 1) for dc in (0, 1) if dx + dy + dc]


def _peer(x, y, c, rel):
    return ((1 - x) if rel[0] else x, (1 - y) if rel[1] else y, (1 - c) if rel[2] else c)


def exchange_start(srcs, lands, layer, name, after=()):
    n = len(srcs)
    after = _as_list(after)

    def body(*refs):
        src, land = refs[:n], refs[n:2 * n]
        send_sem, recv_sem = refs[2 * n + len(after)], refs[2 * n + len(after) + 1]
        token = refs[-1]
        x, y, c = _place()
        me = 4 * x + 2 * y + c
        for k, rel in enumerate(_relations()):
            peer = _peer(x, y, c, rel)
            for a in range(n):
                pltpu.make_async_remote_copy(
                    src_ref=src[a] if layer is None else src[a].at[_slot(peer)],
                    dst_ref=land[a].at[me] if layer is None else land[a].at[me, layer],
                    send_sem=send_sem.at[7 * a + k], recv_sem=recv_sem.at[7 * a + k],
                    device_id=peer, device_id_type=MESH).start()
        token[...] = jnp.zeros_like(token)

    out = pl.pallas_call(
        body, name=name,
        out_shape=[pltpu.SemaphoreType.DMA((7 * n,)), pltpu.SemaphoreType.DMA((7 * n,))]
        + [pltpu.HBM(a.shape, a.dtype) for a in list(srcs) + list(lands)]
        + [jax.ShapeDtypeStruct((SUBLANES, LANES), F32)],
        in_specs=[HBM] * (2 * n) + [ANY] * len(after),
        out_specs=[SEM, SEM] + [HBM] * (2 * n) + [pl.BlockSpec(memory_space=pltpu.VMEM)],
        input_output_aliases={i: 2 + i for i in range(2 * n)},
        compiler_params=pltpu.CompilerParams(has_side_effects=EFFECT),
    )(*[_in_hbm(a) for a in list(srcs) + list(lands)], *after)
    return out[0], out[1], out[2:2 + n], out[2 + n:2 + 2 * n], out[-1][0, 0]


def exchange_wait(send, recv, srcs, lands, after, layer, name):
    n = len(srcs)

    def body(*refs):
        src, land = refs[:n], refs[n:2 * n]
        send_sem, recv_sem = refs[2 * n], refs[2 * n + 1]
        x, y, c = _place()
        for k, rel in enumerate(_relations()):
            peer = _peer(x, y, c, rel)
            for a in range(n):
                cp = pltpu.make_async_remote_copy(
                    src_ref=src[a] if layer is None else src[a].at[_slot(peer)],
                    dst_ref=land[a].at[_slot(peer)] if layer is None else land[a].at[_slot(peer), layer],
                    send_sem=send_sem.at[7 * a + k], recv_sem=recv_sem.at[7 * a + k],
                    device_id=peer, device_id_type=MESH)
                cp.wait_send()
                cp.wait_recv()

    out = pl.pallas_call(
        body, name=name,
        out_shape=[pltpu.HBM(a.shape, a.dtype) for a in list(srcs) + list(lands)],
        in_specs=[HBM] * (2 * n) + [SEM, SEM] + [ANY] * len(_as_list(after)),
        out_specs=[HBM] * (2 * n),
        input_output_aliases={i: i for i in range(2 * n)},
        compiler_params=pltpu.CompilerParams(has_side_effects=EFFECT),
    )(*srcs, *lands, send, recv, *_as_list(after))
    return out[:n], out[n:]


def _adamw_math(g, w, m, v):
    m = ADAM_B1 * m + (1.0 - ADAM_B1) * g
    v = ADAM_B2 * v + (1.0 - ADAM_B2) * (g * g)
    m_hat = m / (1.0 - ADAM_B1 ** ADAM_STEP)
    v_hat = v / (1.0 - ADAM_B2 ** ADAM_STEP)
    delta = -ADAM_LR * (m_hat / (jnp.sqrt(v_hat) + ADAM_EPS) + ADAM_WD * w)
    return delta, m, v


def _sum_with_own(p_ref, own_refs, layer, me):
    own = own_refs[0][...]
    for k in range(1, len(own_refs)):
        own = jnp.where(layer == k, own_refs[k][...], own)
    g = None
    for p in range(p_ref.shape[0]):
        term = jnp.where(me == p, own, p_ref[p]).astype(F32)
        g = term if g is None else g + term
    return g


def sum_adamw(parts, owns, me, w, m, v, tr, name):
    npart, rows, cols = parts.shape
    nl = len(owns)
    per_layer = rows // nl // tr

    def body(me_ref, p_ref, *refs):
        own_refs = refs[:nl]
        w_ref, m_ref, v_ref, g_ref, d_ref, nm_ref, nv_ref = refs[nl:]
        g = _sum_with_own(p_ref, own_refs, pl.program_id(0) // per_layer, me_ref[0])
        delta, nm, nv = _adamw_math(g, w_ref[...], m_ref[...], v_ref[...])
        g_ref[...] = g
        d_ref[...] = delta
        nm_ref[...] = nm
        nv_ref[...] = nv

    blk = pl.BlockSpec((tr, cols), lambda i, me_ref: (i, 0))
    own_specs = [pl.BlockSpec((None, tr, cols),
                              lambda i, me_ref, l=l: (me_ref[0], jnp.clip(i - l * per_layer, 0, per_layer - 1), 0))
                 for l in range(nl)]
    return pl.pallas_call(
        body, name=name,
        grid_spec=pltpu.PrefetchScalarGridSpec(
            num_scalar_prefetch=1, grid=(rows // tr,),
            in_specs=[pl.BlockSpec((npart, tr, cols), lambda i, me_ref: (0, i, 0))] + own_specs + [blk, blk, blk],
            out_specs=[blk] * 4),
        out_shape=[jax.ShapeDtypeStruct((rows, cols), F32)] * 4,
        compiler_params=_cparams("arbitrary"),
    )(me, parts, *owns, w, m, v)


def sum_adamw_t(parts, owns, me, w, m, v, name):
    npart, nl, rows, cols = parts.shape

    def body(me_ref, p_ref, *refs):
        own_refs = refs[:nl]
        w_ref, m_ref, v_ref, g_ref, d_ref, nm_ref, nv_ref = refs[nl:]
        g = _sum_with_own(p_ref, own_refs, pl.program_id(0), me_ref[0])
        delta, nm, nv = _adamw_math(g, w_ref[...], m_ref[...], v_ref[...])
        g_ref[...] = g
        d_ref[...] = delta
        nm_ref[...] = nm
        nv_ref[...] = nv

    blk = pl.BlockSpec((None, rows, cols), lambda l, me_ref: (l, 0, 0))
    own_specs = [pl.BlockSpec((None, rows, cols), lambda l, me_ref: (me_ref[0], 0, 0)) for _ in range(nl)]
    return pl.pallas_call(
        body, name=name,
        grid_spec=pltpu.PrefetchScalarGridSpec(
            num_scalar_prefetch=1, grid=(nl,),
            in_specs=[pl.BlockSpec((npart, None, rows, cols), lambda l, me_ref: (0, l, 0, 0))] + own_specs
            + [blk, blk, blk],
            out_specs=[blk] * 4),
        out_shape=[jax.ShapeDtypeStruct((nl, rows, cols), F32)] * 4,
        compiler_params=_cparams("arbitrary"),
    )(me, parts, *owns, w, m, v)


def adamw_group(gs, ws, ms, vs, name):
    n = len(gs)

    def body(*refs):
        g, w, m, v, outs = refs[:n], refs[n:2 * n], refs[2 * n:3 * n], refs[3 * n:4 * n], refs[4 * n:]
        for i in range(n):
            delta, nm, nv = _adamw_math(g[i][...], w[i][...], m[i][...], v[i][...])
            outs[i][...] = delta
            outs[n + i][...] = nm
            outs[2 * n + i][...] = nv

    vmem = pl.BlockSpec(memory_space=pltpu.VMEM)
    out = pl.pallas_call(
        body, name=name,
        in_specs=[vmem] * (4 * n), out_specs=[vmem] * (3 * n),
        out_shape=[jax.ShapeDtypeStruct(a.shape, F32) for a in list(ws) * 3],
        compiler_params=_cparams(),
    )(*gs, *ws, *ms, *vs)
    return out[:n], out[n:2 * n], out[2 * n:]


def sum_parts(parts, name):
    npart, rows, cols = parts.shape

    def body(p_ref, g_ref):
        g = p_ref[0].astype(F32)
        for p in range(1, npart):
            g = g + p_ref[p].astype(F32)
        g_ref[...] = g

    return pl.pallas_call(
        body, name=name, grid=(1,),
        in_specs=[pl.BlockSpec((npart, rows, cols), lambda i: (0, 0, 0))],
        out_specs=pl.BlockSpec((rows, cols), lambda i: (0, 0)),
        out_shape=jax.ShapeDtypeStruct((rows, cols), F32),
        compiler_params=_cparams("arbitrary"),
    )(parts)


def _round_up(n, m):
    return (n + m - 1) // m * m


def _block_diag_pairs(w):
    nb, b, _ = w.shape
    per = LANES // b
    ng = nb // per
    w = w.reshape(ng, per, b, b)
    eye = jnp.eye(per, dtype=w.dtype)
    out = jnp.einsum('gpij,pq->gpiqj', w, eye).reshape(ng, LANES, LANES)
    return out.astype(BF16)


def _block_diag_extract(g, b):
    ng = g.shape[0]
    per = LANES // b
    g = g.reshape(ng, per, b, per, b)
    idx = jnp.arange(per)
    return g[:, idx, :, idx, :].transpose(1, 0, 2, 3).reshape(ng * per, b, b)


def _tiles(v):
    v = v.reshape(-1)
    n = _round_up(v.shape[0], SUBLANES * LANES)
    return jnp.pad(v, (0, n - v.shape[0])).reshape(-1, LANES)


SMALL = ['attn_norm_g', 'b_f', 'conv_w', 'conv_b', 'w_gate_a', 'b_gate_a', 'w_gate_x', 'b_gate_x',
         'lru_L', 'attn_out_g', 'rec_out_g', 'mlp_norm_g', 'final_g', 'meta']


def _pack(d):
    return jnp.concatenate([_tiles(d[n]) for n in SMALL], axis=0)


def _unpack(vec, shapes):
    out, r = {}, 0
    for n in SMALL:
        size = math.prod(shapes[n])
        nr = _round_up(size, SUBLANES * LANES) // LANES
        out[n] = vec[r:r + nr].reshape(-1)[:size].reshape(shapes[n])
        r += nr
    return out


def _row_tile(tp):
    return tp // 4 if (tp // 4) % 16 == 0 else tp


def local_step(x, tgt, meta, small, hooks):
    s, d = x.shape
    t_real = s + N_META
    tp = _round_up(t_real, ATT_BLOCK)
    depth = small['attn_norm_g'].shape[0]
    nh = small['b_f'].shape[1]
    rw = small['conv_b'].shape[1]
    blk = small['w_gate_a'].shape[2]
    tm = _row_tile(tp)
    tm2 = tp // 2
    fcol = 2 * rw // LANES

    h = jnp.concatenate([meta, x, jnp.zeros((tp - t_real, d), F32)], axis=0)
    tgt_p = jnp.pad(tgt, ((N_META, tp - t_real), (0, 0)))
    row = lambda v: v.reshape(1, -1)
    bf_pad = jnp.pad(small['b_f'], ((0, 0), (0, LANES - nh)))

    saved = []
    for l in range(depth):
        w_in_t, wrest_t, wout, tok_w = hooks.mixer_weights(l, h)
        wga = _block_diag_pairs(small['w_gate_a'][l])
        wgx = _block_diag_pairs(small['w_gate_x'][l])
        z, qkv, rest = in_proj(h, row(small['attn_norm_g'][l]) + tok_w, w_in_t, wrest_t, 3 * nh * HEAD_DIM, tm)
        c, ct = fgate_fwd(rest, bf_pad[l:l + 1], fcol)
        o, lset = attn_fwd(qkv, c, ct, nh)
        rec, hr, xc, *gates = rec_fwd(rest, small['conv_w'][l], row(small['conv_b'][l]), wga,
                                      row(small['b_gate_a'][l]), wgx, row(small['b_gate_x'][l]),
                                      row(small['lru_L'][l]), rw)
        gup, gdown, tok_w = hooks.mlp_weights(l, rec)
        h2, mix, z2, u, h3 = out_mlp_fwd(h, o, rec, row(small['attn_out_g'][l]), row(small['rec_out_g'][l]), wout,
                                         row(small['mlp_norm_g'][l]) + tok_w, gup, gdown, tm)
        saved.append(dict(h=h, z=z, qkv=qkv, rest=rest, c=c, ct=ct, o=o, lset=lset, rec=rec, hr=hr, xc=xc,
                          h2=h2, mix=mix, z2=z2, u=u, wga=wga, wgx=wgx, gates=gates,
                          w_in_t=w_in_t, wrest_t=wrest_t, wout=wout, gup=gup, gdown=gdown))
        h = h3

    dh, dgf, loss = loss_head(h, row(small['final_g']), tgt_p, t_real, tm)

    gs = {n: [None] * depth for n in SMALL if n not in ('final_g', 'meta')}
    tok = jnp.zeros((), F32)
    for l in reversed(range(depth)):
        sv = saved[l]
        gup, gdown = sv['gup'], sv['gdown']
        tf = gup.shape[2]
        dup, dh2, dg2, dhb = mlp_bwd(dh, sv['u'], sv['h2'], row(small['mlp_norm_g'][l]) + tok, gup, gdown, tm)
        gs['mlp_norm_g'][l] = dg2[0]
        do, drec, dga, dgr, dw_out = out_proj_bwd(dh2, sv['mix'], sv['o'], sv['rec'], row(small['attn_out_g'][l]),
                                                  row(small['rec_out_g'][l]), sv['wout'], tm)
        gs['attn_out_g'][l] = dga[0]
        gs['rec_out_g'][l] = dgr[0]
        dw_down, dw_up = dw_mlp(sv['u'], dhb, sv['z2'], dup, tf)
        blocks = dict(w_down=dw_down, w_up=dw_up, w_out=dw_out.reshape(N_DEV, d // N_DEV, d))
        tok = hooks.grads_ready(l, 'mlp', blocks)
        dxr, dyr, dwga, dwgx, vec = rec_bwd(drec, sv['hr'], sv['xc'], sv['gates'], sv['rest'], small['conv_w'][l],
                                            sv['wga'], sv['wgx'], row(small['lru_L'][l]) + tok, rw)
        gs['w_gate_a'][l] = _block_diag_extract(dwga, blk)
        gs['w_gate_x'][l] = _block_diag_extract(dwgx, blk)
        vec = vec.transpose(1, 0, 2).reshape(SUBLANES, rw)
        gs['conv_w'][l] = vec[0:CONV_WIDTH]
        gs['conv_b'][l] = vec[4]
        gs['b_gate_a'][l] = vec[5]
        gs['b_gate_x'][l] = vec[6]
        gs['lru_L'][l] = vec[7]
        dq, dk, dv, drow, dcol = attn_bwd(sv['qkv'], do, sv['o'], sv['lset'], sv['c'], sv['ct'] + tok, nh)
        drow8 = drow[:, 0:2, :].reshape(nh, tp)
        if nh < SUBLANES:
            drow8 = jnp.pad(drow8, ((0, SUBLANES - nh), (0, 0)))
        dfl, dbf = fgate_bwd(drow8, dcol, sv['rest'], bf_pad[l:l + 1], fcol)
        gs['b_f'][l] = dbf[0, 0:nh]
        parts = (dq, dk, dv, dxr, dyr, dfl)
        dh, dg1, *dx_rows = in_proj_bwd(dh2, parts, sv['w_in_t'], sv['wrest_t'], sv['h'],
                                         row(small['attn_norm_g'][l]), tm, 0 if l else s)
        gs['attn_norm_g'][l] = dg1[0]
        first = ()
        if l == 0:
            grads = {n: jnp.stack(v) for n, v in gs.items()}
            grads['final_g'] = dgf[0]
            grads['meta'] = dh[0:N_META]
            first = hooks.small_ready(grads)
        dw_in = dw_in_t(sv['z'], parts, nh, tm2)
        tok = hooks.grads_ready(l, 'in', dict(w_in=dw_in), first)

    return loss[0, 0], dx_rows[0]


def prep_weights(g_in, g_out, nh, rw):
    d = g_in.shape[2]
    w_in_t = g_in.reshape(-1, d)
    f0 = 3 * nh * HEAD_DIM
    wrest_t = jnp.concatenate([w_in_t[f0 + nh:f0 + nh + 2 * rw],
                               jnp.pad(w_in_t[f0:f0 + nh], ((0, LANES - nh), (0, 0)))], axis=0)
    return w_in_t, wrest_t, g_out.reshape(d, d)


BIG = ['w_in', 'w_out', 'w_up', 'w_down']
EXCHANGE_GROUPS = {'mlp': ['w_down', 'w_up', 'w_out'], 'in': ['w_in']}
WEIGHTS = ['meta', 'attn_norm_g', 'w_in', 'b_f', 'conv_w', 'conv_b', 'w_gate_a', 'b_gate_a', 'w_gate_x', 'b_gate_x',
           'lru_L', 'attn_out_g', 'rec_out_g', 'w_out', 'mlp_norm_g', 'w_up', 'w_down', 'final_g']


def _set_own(arr, own, me):
    return lax.dynamic_update_slice_in_dim(arr, own[None], me, axis=0)


class _Step:
    def __init__(self, w, nh, rw, me):
        self.w, self.nh, self.rw, self.me = w, nh, rw, me
        depth = w['w_in'].shape[0]
        first = [w['w_in_t'][:, 0, :].astype(BF16), w['w_out'][0].astype(BF16), w['meta'], w['conv_w']]
        self.pending, token = gather_start([first], "gather_start_0")
        zero = token[0, 0].astype(BF16)
        groups = [[w['w_up'][0].astype(BF16) + zero, w['w_down'][0].astype(BF16) + zero]]
        for l in range(1, depth):
            groups.append([w['w_in_t'][:, l, :].astype(BF16) + zero, w['w_out'][l].astype(BF16) + zero])
            groups.append([w['w_up'][l].astype(BF16) + zero, w['w_down'][l].astype(BF16) + zero])
        rest, _ = gather_start(groups, "gather_start_1")
        self.pending += rest
        self.first_after = rest[0][2][0]
        self.gathered = {}
        self.passing = {}
        self.token = jnp.zeros((), F32)
        self.lands = {n: lax.empty((N_DEV,) + w[n].shape, BF16) for n in BIG}
        din8, _, d = w['w_in_t'].shape
        self.lands['w_in'] = lax.empty((N_DEV, depth, din8, d), BF16)
        self.started = []
        self.small = None

    def _pass_on(self, gi, after):
        if gi < len(self.pending) and gi not in self.passing:
            send, recv, srcs, lands = self.pending[gi]
            srcs, lands = gather_wait(send, recv, srcs, lands, after, "gather_wait_%d" % gi)
            fsend, frecv, lands, token = forward_start(lands, "forward_start_%d" % gi)
            self.passing[gi] = (fsend, frecv, srcs, lands)
            self.token = token

    def group(self, gi, after):
        if gi not in self.gathered:
            self._pass_on(gi, after)
            fsend, frecv, srcs, lands = self.passing[gi]
            lands = forward_wait(fsend, frecv, lands, after, "forward_wait_%d" % gi)
            self.gathered[gi] = [_set_own(g, own, self.me) for g, own in zip(lands, srcs)]
            if gi >= 2:
                self._pass_on(gi + 1, lands[0])
        return self.gathered[gi]

    def mixer_weights(self, l, after):
        g = self.group(2 * l, after)
        return (*prep_weights(g[0], g[1], self.nh, self.rw), self.token)

    def mlp_weights(self, l, after):
        g = self.group(2 * l + 1, after)
        return g[0], g[1], self.token

    def grads_ready(self, l, group, blocks, after=()):
        names = EXCHANGE_GROUPS[group]
        send, recv, srcs, lands, token = exchange_start(
            [blocks[n] for n in names], [self.lands[n] for n in names], l, "exchange_start_%s_%d" % (group, l),
            after)
        for n, a in zip(names, lands):
            self.lands[n] = a
        self.started.append((l, group, send, recv, srcs))
        return token

    def small_ready(self, grads):
        self.small_shapes = {n: grads[n].shape for n in SMALL}
        packed = _pack(grads).astype(BF16)
        send, recv, srcs, lands, token = exchange_start(
            [packed], [lax.empty((N_DEV,) + packed.shape, BF16)], None, "small_start")
        self.small = (send, recv, srcs, lands)
        return srcs[0]

    def small_sum(self, after):
        send, recv, srcs, lands = self.small
        srcs, lands = exchange_wait(send, recv, srcs, lands, after, None, "small_wait")
        parts = _set_own(lands[0], srcs[0], self.me)
        return _unpack(sum_parts(parts, "sum_small_grads"), self.small_shapes)

    def received(self, group, after):
        names = EXCHANGE_GROUPS[group]
        own = {n: [None] * self.w[n].shape[0] for n in names}
        for l, grp, send, recv, srcs in self.started:
            if grp != group:
                continue
            srcs, lands = exchange_wait(send, recv, srcs, [self.lands[n] for n in names], after, l,
                                        "exchange_wait_%s_%d" % (group, l))
            for n, a, sr in zip(names, lands, srcs):
                self.lands[n] = a
                own[n][l] = sr
        return {n: (self.lands[n], own[n]) for n in names}


def kernel(x, meta, attn_norm_g, w_in, b_f, conv_w, conv_b, w_gate_a, b_gate_a, w_gate_x, b_gate_x, lru_L, attn_out_g, rec_out_g, w_out, mlp_norm_g, w_up, w_down, final_g, loss_target, m_meta, m_attn_norm_g, m_w_in, m_b_f, m_conv_w, m_conv_b, m_w_gate_a, m_b_gate_a, m_w_gate_x, m_b_gate_x, m_lru_L, m_attn_out_g, m_rec_out_g, m_w_out, m_mlp_norm_g, m_w_up, m_w_down, m_final_g, v_meta, v_attn_norm_g, v_w_in, v_b_f, v_conv_w, v_conv_b, v_w_gate_a, v_b_gate_a, v_w_gate_x, v_b_gate_x, v_lru_L, v_attn_out_g, v_rec_out_g, v_w_out, v_mlp_norm_g, v_w_up, v_w_down, v_final_g):
    w = dict(meta=meta, attn_norm_g=attn_norm_g, w_in=w_in, b_f=b_f, conv_w=conv_w, conv_b=conv_b,
             w_gate_a=w_gate_a, b_gate_a=b_gate_a, w_gate_x=w_gate_x, b_gate_x=b_gate_x, lru_L=lru_L,
             attn_out_g=attn_out_g, rec_out_g=rec_out_g, w_out=w_out, mlp_norm_g=mlp_norm_g, w_up=w_up,
             w_down=w_down, final_g=final_g)
    mo = dict(meta=m_meta, attn_norm_g=m_attn_norm_g, w_in=m_w_in, b_f=m_b_f, conv_w=m_conv_w, conv_b=m_conv_b,
              w_gate_a=m_w_gate_a, b_gate_a=m_b_gate_a, w_gate_x=m_w_gate_x, b_gate_x=m_b_gate_x, lru_L=m_lru_L,
              attn_out_g=m_attn_out_g, rec_out_g=m_rec_out_g, w_out=m_w_out, mlp_norm_g=m_mlp_norm_g,
              w_up=m_w_up, w_down=m_w_down, final_g=m_final_g)
    vo = dict(meta=v_meta, attn_norm_g=v_attn_norm_g, w_in=v_w_in, b_f=v_b_f, conv_w=v_conv_w, conv_b=v_conv_b,
              w_gate_a=v_w_gate_a, b_gate_a=v_b_gate_a, w_gate_x=v_w_gate_x, b_gate_x=v_b_gate_x, lru_L=v_lru_L,
              attn_out_g=v_attn_out_g, rec_out_g=v_rec_out_g, w_out=v_w_out, mlp_norm_g=v_mlp_norm_g,
              w_up=v_w_up, w_down=v_w_down, final_g=v_final_g)
    depth = w_in.shape[0]
    nh = b_f.shape[1]
    rw = conv_b.shape[1]
    me = 4 * lax.axis_index("x") + 2 * lax.axis_index("y") + lax.axis_index("c")

    w['w_in_t'] = jnp.transpose(w_in, (2, 0, 1))
    swap = lambda a: jnp.swapaxes(a, 1, 2)
    step = _Step(w, nh, rw, me)
    g0 = step.group(0, step.first_after)
    meta_full = g0[2].transpose(1, 0, 2).reshape(N_META, -1)
    conv_full = g0[3].transpose(1, 2, 0, 3).reshape(depth, CONV_WIDTH, rw)
    small = {n: w[n] for n in SMALL}
    small['conv_w'] = conv_full

    loss_part, dx = local_step(x[0], loss_target[0], meta_full, small, step)
    loss = lax.psum(loss_part, ("x", "y", "c"))
    grad_x = dx[None]

    out_g, out_d, out_m, out_v = {}, {}, {}, {}
    me1 = me.reshape(1).astype(jnp.int32)

    def update_big(group, after):
        for n, (r, owns) in step.received(group, after).items():
            if n == 'w_in':
                out = sum_adamw_t(r, owns, me1, swap(w[n]), swap(mo[n]), swap(vo[n]), "adamw_w_in")
                out = [swap(a) for a in out]
            else:
                shp = w[n].shape
                rows, cols = shp[0] * shp[1], shp[2]
                tr = min(512 if cols <= 512 else 256, shp[1])
                out = sum_adamw(r.reshape(N_DEV, rows, cols), owns, me1, w[n].reshape(rows, cols),
                                mo[n].reshape(rows, cols), vo[n].reshape(rows, cols), tr, "adamw_" + n)
                out = [a.reshape(shp) for a in out]
            out_g[n], out_d[n], out_m[n], out_v[n] = out
            after = out[0]
        return after

    update_big('mlp', step.started[-1][4][0])

    gsum = step.small_sum([out_g[n] for n in EXCHANGE_GROUPS['mlp']])
    gsum['meta'] = lax.dynamic_slice_in_dim(gsum['meta'], me * meta.shape[1], meta.shape[1], axis=1)
    gsum['conv_w'] = lax.dynamic_slice_in_dim(gsum['conv_w'], me * conv_w.shape[2], conv_w.shape[2], axis=2)
    as2d = lambda a: a.reshape(-1, a.shape[-1])
    deltas, new_m, new_v = adamw_group([as2d(gsum[n]) for n in SMALL], [as2d(w[n]) for n in SMALL],
                                       [as2d(mo[n]) for n in SMALL], [as2d(vo[n]) for n in SMALL], "adamw_small")
    for i, n in enumerate(SMALL):
        out_g[n] = gsum[n]
        out_d[n], out_m[n], out_v[n] = [a[i].reshape(w[n].shape) for a in (deltas, new_m, new_v)]

    update_big('in', deltas[0])

    return (loss, grad_x, *[out_g[n] for n in WEIGHTS], *[out_d[n] for n in WEIGHTS],
            *[out_m[n] for n in WEIGHTS], *[out_v[n] for n in WEIGHTS])
```

```python
import math

import jax
import jax.numpy as jnp
from jax import lax
from jax.experimental import pallas as pl
from jax.experimental.pallas import tpu as pltpu

F32 = jnp.float32
BF16 = jnp.bfloat16

N_DEV = 8
N_META = 16
HEAD_DIM = 64
CONV_WIDTH = 4
RG_C = 8.0
NORM_EPS = 1e-6
LANES = 128
SUBLANES = 8
ATT_BLOCK = 128
ATT_TQ = 512
NEG_BIG = -1e30
ATT_SCALE = 1.0 / math.sqrt(HEAD_DIM)

ADAM_LR = 0.001
ADAM_B1 = 0.9
ADAM_B2 = 0.999
ADAM_EPS = 1e-08
ADAM_WD = 0.01
ADAM_STEP = 10

VMEM_LIMIT_BYTES = 56 * 1024 * 1024
MESH = pl.DeviceIdType.MESH
ANY = pl.BlockSpec(memory_space=pl.ANY)


def _cparams(*sem):
    return pltpu.CompilerParams(dimension_semantics=sem if sem else None,
                                vmem_limit_bytes=VMEM_LIMIT_BYTES)


def _dot(a, b):
    return jnp.dot(a, b, preferred_element_type=F32)


def _dot_nt(a, b):
    return lax.dot_general(a, b, (((1,), (1,)), ((), ())), preferred_element_type=F32)


def _dot_tn(a, b):
    return lax.dot_general(a, b, (((0,), (0,)), ((), ())), preferred_element_type=F32)


def _sigmoid(x):
    return 0.5 * (1.0 + jnp.tanh(0.5 * x))


def _log_sigmoid(x):
    return jnp.minimum(x, 0.0) - jnp.log(1.0 + jnp.exp(-jnp.abs(x)))


def _expm1(x):
    series = x * (1.0 + x * (0.5 + x * (1.0 / 6.0 + x * (1.0 / 24.0))))
    return jnp.where(jnp.abs(x) < 1e-2, series, jnp.exp(x) - 1.0)


_GELU_K = math.sqrt(2.0 / math.pi)
_GELU_C = 0.044715


def _gelu(x):
    t = jnp.tanh(_GELU_K * (x + _GELU_C * x * x * x))
    return 0.5 * x * (1.0 + t)


def _gelu_and_grad(x):
    x2 = x * x
    t = jnp.tanh(_GELU_K * (x + _GELU_C * x2 * x))
    half = 0.5 * (1.0 + t)
    return x * half, half + 0.5 * x * (1.0 - t * t) * _GELU_K * (1.0 + 3.0 * _GELU_C * x2)


def _split3_dot(tri, x):
    hi = x.astype(BF16)
    r1 = x - hi.astype(F32)
    mid = r1.astype(BF16)
    lo = (r1 - mid.astype(F32)).astype(BF16)
    return _dot(tri, hi) + _dot(tri, mid) + _dot(tri, lo)


def _dot_split3(x, sel):
    hi = x.astype(BF16)
    r1 = x - hi.astype(F32)
    mid = r1.astype(BF16)
    lo = (r1 - mid.astype(F32)).astype(BF16)
    return _dot(hi, sel) + _dot(mid, sel) + _dot(lo, sel)


def _rms_fwd(x, g):
    r = lax.rsqrt(jnp.mean(x * x, axis=-1, keepdims=True) + NORM_EPS)
    return x * r * g


def _rms_bwd(x, g, dy):
    r = lax.rsqrt(jnp.mean(x * x, axis=-1, keepdims=True) + NORM_EPS)
    xn = x * r
    dxn = dy * g
    dx = r * (dxn - xn * jnp.mean(dxn * xn, axis=-1, keepdims=True))
    return dx, jnp.sum(dy * xn, axis=0, keepdims=True)


def _accumulate(ref, val, first):
    @pl.when(first)
    def _():
        ref[...] = val

    @pl.when(jnp.logical_not(first))
    def _():
        ref[...] += val


def in_proj(h, g1, w_in_t, wrest_t, nq, tm):
    tp, d = h.shape
    nr = wrest_t.shape[0]

    def body(h_ref, g_ref, wq_ref, wr_ref, z_ref, qkv_ref, rest_ref):
        z = _rms_fwd(h_ref[...], g_ref[...]).astype(BF16)
        z_ref[...] = z
        qkv_ref[...] = _dot_nt(z, wq_ref[...]).astype(BF16)
        rest_ref[...] = _dot_nt(z, wr_ref[...])

    return pl.pallas_call(
        body, name="in_proj", grid=(tp // tm,),
        in_specs=[pl.BlockSpec((tm, d), lambda i: (i, 0)),
                  pl.BlockSpec((1, d), lambda i: (0, 0)),
                  pl.BlockSpec((nq, d), lambda i: (0, 0)),
                  pl.BlockSpec((nr, d), lambda i: (0, 0))],
        out_specs=[pl.BlockSpec((tm, d), lambda i: (i, 0)),
                   pl.BlockSpec((tm, nq), lambda i: (i, 0)),
                   pl.BlockSpec((tm, nr), lambda i: (i, 0))],
        out_shape=[jax.ShapeDtypeStruct((tp, d), BF16),
                   jax.ShapeDtypeStruct((tp, nq), BF16),
                   jax.ShapeDtypeStruct((tp, nr), F32)],
        compiler_params=_cparams("parallel"),
    )(h, g1, w_in_t, wrest_t)


def fgate_fwd(rest, bf_pad, fcol):
    tp = rest.shape[0]
    nb = tp // ATT_BLOCK

    def body(f_ref, b_ref, c_ref, ct_ref):
        r_i = lax.broadcasted_iota(jnp.int32, (ATT_BLOCK, ATT_BLOCK), 0)
        c_i = lax.broadcasted_iota(jnp.int32, (ATT_BLOCK, ATT_BLOCK), 1)
        tri = (r_i >= c_i).astype(BF16)
        c_ref[...] = _log_sigmoid(f_ref[...] + b_ref[...])
        carry = jnp.zeros((1, LANES), F32)
        for i in range(nb):
            sl = slice(i * ATT_BLOCK, (i + 1) * ATT_BLOCK)
            cs = _split3_dot(tri, c_ref[sl, :]) + carry
            carry = cs[ATT_BLOCK - 1:ATT_BLOCK, :]
            c_ref[sl, :] = cs
        ct_ref[...] = c_ref[...].T[0:SUBLANES, :]

    return pl.pallas_call(
        body, name="fgate_fwd", grid=(1,),
        in_specs=[pl.BlockSpec((tp, LANES), lambda i: (0, fcol)),
                  pl.BlockSpec((1, LANES), lambda i: (0, 0))],
        out_specs=[pl.BlockSpec((tp, LANES), lambda i: (0, 0)),
                   pl.BlockSpec((SUBLANES, tp), lambda i: (0, 0))],
        out_shape=[jax.ShapeDtypeStruct((tp, LANES), F32),
                   jax.ShapeDtypeStruct((SUBLANES, tp), F32)],
        compiler_params=_cparams("arbitrary"),
    )(rest, bf_pad)


def _pick_col(blk, head):
    lane = lax.broadcasted_iota(jnp.int32, blk.shape, 1)
    return jnp.sum(jnp.where(lane == head, blk, 0.0), axis=1, keepdims=True)


def _pick_row(blk, head):
    sub = lax.broadcasted_iota(jnp.int32, blk.shape, 0)
    return jnp.sum(jnp.where(sub == head, blk, 0.0), axis=0, keepdims=True)


def _att_tiles(tp):
    out, r0 = [], 0
    while r0 < tp:
        rows = min(ATT_TQ, tp - r0)
        out.append((r0, rows, r0 + rows))
        r0 += rows
    return out


def attn_fwd(qkv, c, ct, nh):
    tp = qkv.shape[0]
    npair = nh // 2
    tiles = _att_tiles(tp)

    def body(q_ref, k_ref, v_ref, c_ref, ct_ref, o_ref, lset_ref):
        p = pl.program_id(0)
        lset_ref[...] = jnp.zeros_like(lset_ref)
        for r0, nr, nk in tiles:
            rs = slice(r0, r0 + nr)
            causal = (r0 + lax.broadcasted_iota(jnp.int32, (nr, nk), 0)
                      >= lax.broadcasted_iota(jnp.int32, (nr, nk), 1))
            cblk = c_ref[rs, :]
            ctb = ct_ref[:, 0:nk]
            for hh in range(2):
                head = 2 * p + hh
                hs = slice(hh * HEAD_DIM, (hh + 1) * HEAD_DIM)
                q = q_ref[rs, hs] * ATT_SCALE
                s = _dot_nt(q, k_ref[0:nk, hs]) + (_pick_col(cblk, head) - _pick_row(ctb, head))
                s = jnp.where(causal, s, NEG_BIG)
                m = jnp.max(s, axis=1, keepdims=True)
                pm = jnp.exp(s - m)
                l = jnp.sum(pm, axis=1, keepdims=True)
                o_ref[rs, hs] = _dot(pm.astype(BF16), v_ref[0:nk, hs]) / l
                lse = m + jnp.log(l)
                lset_ref[hh:hh + 1, rs] = jnp.broadcast_to(lse, (nr, LANES)).T[0:1, :]

    pair = lambda p: (0, p)
    return pl.pallas_call(
        body, name="attn_fwd", grid=(npair,),
        in_specs=[pl.BlockSpec((tp, LANES), pair),
                  pl.BlockSpec((tp, LANES), lambda p: (0, npair + p)),
                  pl.BlockSpec((tp, LANES), lambda p: (0, 2 * npair + p)),
                  pl.BlockSpec((tp, LANES), lambda p: (0, 0)),
                  pl.BlockSpec((SUBLANES, tp), lambda p: (0, 0))],
        out_specs=[pl.BlockSpec((tp, LANES), pair),
                   pl.BlockSpec((None, SUBLANES, tp), lambda p: (p, 0, 0))],
        out_shape=[jax.ShapeDtypeStruct((tp, nh * HEAD_DIM), F32),
                   jax.ShapeDtypeStruct((npair, SUBLANES, tp), F32)],
        compiler_params=_cparams("parallel"),
    )(qkv, qkv, qkv, c, ct)


def _shift_down(x, k, n):
    if k == 0:
        return x
    rows = lax.broadcasted_iota(jnp.int32, x.shape, 0)
    return jnp.where(rows >= k, pltpu.roll(x, k, 0), 0.0)


def _shift_up(x, k, n):
    if k == 0:
        return x
    rows = lax.broadcasted_iota(jnp.int32, x.shape, 0)
    return jnp.where(rows < n - k, pltpu.roll(x, n - k, 0), 0.0)


def _conv_fwd(xr, cw_ref, cb_ref, n):
    xc = cw_ref[CONV_WIDTH - 1:CONV_WIDTH, :] * xr + cb_ref[...]
    for k in range(1, CONV_WIDTH):
        xc = xc + cw_ref[CONV_WIDTH - 1 - k:CONV_WIDTH - k, :] * _shift_down(xr, k, n)
    return xc


def _gates(xc, wga_ref, bga_ref, wgx_ref, bgx_ref, l_ref):
    xcb = xc.astype(BF16)
    r = _sigmoid(_dot(xcb, wga_ref[...]) + bga_ref[...])
    ig = _sigmoid(_dot(xcb, wgx_ref[...]) + bgx_ref[...])
    ls = _log_sigmoid(l_ref[...])
    log_a = RG_C * r * ls
    a = jnp.exp(log_a)
    mult = jnp.sqrt(-_expm1(2.0 * log_a))
    return xcb, r, ig, ls, log_a, a, mult


SCAN_UNROLL = 16


def _scan_rows(a_s, u_s, out_ref, n, reverse):
    nt = n // SUBLANES
    per = SCAN_UNROLL if nt % SCAN_UNROLL == 0 else 1
    row = lax.broadcasted_iota(jnp.int32, (SUBLANES, LANES), 0)
    last = 0 if reverse else SUBLANES - 1

    def tile_scan(a, u):
        for d in (1, 2, 4):
            if reverse:
                keep = row < SUBLANES - d
                sh = SUBLANES - d
            else:
                keep = row >= d
                sh = d
            a_sh = jnp.where(keep, pltpu.roll(a, sh, 0), 1.0)
            u_sh = jnp.where(keep, pltpu.roll(u, sh, 0), 0.0)
            u = a * u_sh + u
            a = a * a_sh
        return a, u

    def step(t, carry):
        tiles = []
        for k in range(per):
            tt = t * per + k
            if reverse:
                tt = nt - 1 - tt
            off = pl.multiple_of(tt * SUBLANES, SUBLANES)
            a, u = tile_scan(a_s[pl.ds(off, SUBLANES), :], u_s[pl.ds(off, SUBLANES), :])
            tiles.append((off, a, u))
        for off, a, u in tiles:
            out_ref[pl.ds(off, SUBLANES), :] = u + a * carry
            carry = u[last:last + 1, :] + a[last:last + 1, :] * carry
        return carry

    lax.fori_loop(0, nt // per, step, jnp.zeros((1, LANES), F32))


def rec_fwd(rest, convw, convb, wga, bga, wgx, bgx, lru, rw):
    tp = rest.shape[0]
    ng = rw // LANES

    def body(xr_ref, yr_ref, cw_ref, cb_ref, wga_ref, bga_ref, wgx_ref, bgx_ref, l_ref,
             rec_ref, hr_ref, xc_ref, r_ref, ig_ref, a_ref, mult_ref, u_s):
        xc = _conv_fwd(xr_ref[...], cw_ref, cb_ref, tp)
        xc_ref[...] = xc
        _, r, ig, ls, log_a, a, mult = _gates(xc, wga_ref, bga_ref, wgx_ref, bgx_ref, l_ref)
        r_ref[...] = r
        ig_ref[...] = ig
        a_ref[...] = a
        mult_ref[...] = mult
        u_s[...] = mult * ig * xc
        _scan_rows(a_ref, u_s, hr_ref, tp, reverse=False)
        rec_ref[...] = hr_ref[...] * _gelu(yr_ref[...])

    col = lambda g: (0, g)
    vec = pl.BlockSpec((1, LANES), col)
    big = pl.BlockSpec((tp, LANES), col)
    return pl.pallas_call(
        body, name="rec_fwd", grid=(ng,),
        in_specs=[big, pl.BlockSpec((tp, LANES), lambda g: (0, ng + g)),
                  pl.BlockSpec((CONV_WIDTH, LANES), col), vec,
                  pl.BlockSpec((None, LANES, LANES), lambda g: (g, 0, 0)), vec,
                  pl.BlockSpec((None, LANES, LANES), lambda g: (g, 0, 0)), vec, vec],
        out_specs=[big] * 7,
        out_shape=[jax.ShapeDtypeStruct((tp, rw), F32)] * 7,
        scratch_shapes=[pltpu.VMEM((tp, LANES), F32)],
        compiler_params=_cparams("parallel"),
    )(rest, rest, convw, convb, wga, bga, wgx, bgx, lru)


def out_mlp_fwd(h, o, rec, ga, gr, wout, g2, gup, gdown, tm):
    tp, d = h.shape
    aw, rw = o.shape[1], rec.shape[1]
    nf = gup.shape[0]
    tf = gup.shape[2]
    nb = MLP_BLOCKS if nf % MLP_BLOCKS == 0 else 1
    nj = nf // nb

    def body(h_ref, o_ref, rec_ref, ga_ref, gr_ref, w_ref, g2_ref, wu_ref, wd_ref,
             h2_ref, mix_ref, z2_ref, u_ref, h3_ref, acc):
        j = pl.program_id(1)

        @pl.when(j == 0)
        def _():
            mix_ref[:, 0:aw] = _rms_fwd(o_ref[...], ga_ref[...]).astype(BF16)
            mix_ref[:, aw:aw + rw] = _rms_fwd(rec_ref[...], gr_ref[...]).astype(BF16)
            h2 = h_ref[...] + _dot(mix_ref[...], w_ref[...])
            h2_ref[...] = h2
            acc[...] = h2
            z2_ref[...] = _rms_fwd(h2, g2_ref[...]).astype(BF16)

        z = z2_ref[...]
        part = None
        for b in range(nb):
            u = jnp.maximum(_dot(z, wu_ref[b]), 0.0)
            u_ref[:, b * tf:(b + 1) * tf] = u.astype(BF16)
            p = _dot((u * u).astype(BF16), wd_ref[b])
            part = p if part is None else part + p
        acc[...] += part

        @pl.when(j == nj - 1)
        def _():
            h3_ref[...] = acc[...]

    row = lambda i, j: (i, 0)
    fix = lambda i, j: (0, 0)
    return pl.pallas_call(
        body, name="out_mlp_fwd", grid=(tp // tm, nj),
        in_specs=[pl.BlockSpec((tm, d), row), pl.BlockSpec((tm, aw), row), pl.BlockSpec((tm, rw), row),
                  pl.BlockSpec((1, aw), fix), pl.BlockSpec((1, rw), fix),
                  pl.BlockSpec((d, d), fix), pl.BlockSpec((1, d), fix),
                  pl.BlockSpec((nb, d, tf), lambda i, j: (j, 0, 0)),
                  pl.BlockSpec((nb, tf, d), lambda i, j: (j, 0, 0))],
        out_specs=[pl.BlockSpec((tm, d), row), pl.BlockSpec((tm, d), row), pl.BlockSpec((tm, d), row),
                   pl.BlockSpec((tm, nb * tf), lambda i, j: (i, j)), pl.BlockSpec((tm, d), row)],
        out_shape=[jax.ShapeDtypeStruct((tp, d), F32), jax.ShapeDtypeStruct((tp, d), BF16),
                   jax.ShapeDtypeStruct((tp, d), BF16), jax.ShapeDtypeStruct((tp, nf * tf), BF16),
                   jax.ShapeDtypeStruct((tp, d), F32)],
        scratch_shapes=[pltpu.VMEM((tm, d), F32)],
        compiler_params=_cparams("parallel", "arbitrary"),
    )(h, o, rec, ga, gr, wout, g2, gup, gdown)


MLP_BLOCKS = 4


def loss_head(h, gf, tgt, t_real, tm):
    tp, d = h.shape

    def body(h_ref, g_ref, t_ref, dh_ref, dg_ref, loss_ref):
        i = pl.program_id(0)
        x = h_ref[...]
        g = g_ref[...]
        r = lax.rsqrt(jnp.mean(x * x, axis=-1, keepdims=True) + NORM_EPS)
        xn = x * r
        rows = i * tm + lax.broadcasted_iota(jnp.int32, (tm, 1), 0)
        valid = jnp.logical_and(rows >= N_META, rows < t_real)
        e = jnp.where(valid, xn * g - t_ref[...], 0.0)
        part = 0.5 * jnp.sum(jnp.sum(e * e, axis=1, keepdims=True) / d, axis=0, keepdims=True)
        dy = e / d
        dxn = dy * g
        dh_ref[...] = r * (dxn - xn * jnp.mean(dxn * xn, axis=-1, keepdims=True))
        _accumulate(dg_ref, jnp.sum(dy * xn, axis=0, keepdims=True), i == 0)
        _accumulate(loss_ref, jnp.broadcast_to(part, (1, LANES)), i == 0)

    row = lambda i: (i, 0)
    fix = lambda i: (0, 0)
    return pl.pallas_call(
        body, name="loss_head", grid=(tp // tm,),
        in_specs=[pl.BlockSpec((tm, d), row), pl.BlockSpec((1, d), fix), pl.BlockSpec((tm, d), row)],
        out_specs=[pl.BlockSpec((tm, d), row), pl.BlockSpec((1, d), fix), pl.BlockSpec((1, LANES), fix)],
        out_shape=[jax.ShapeDtypeStruct((tp, d), F32), jax.ShapeDtypeStruct((1, d), F32),
                   jax.ShapeDtypeStruct((1, LANES), F32)],
        compiler_params=_cparams("arbitrary"),
    )(h, gf, tgt)


def mlp_bwd(dh, u, h2, g2, gup, gdown, tm):
    tp, d = dh.shape
    nf = gup.shape[0]
    tf = gup.shape[2]
    nb = MLP_BLOCKS if nf % MLP_BLOCKS == 0 else 1
    nj = nf // nb
    ni = tp // tm

    def body(dh_ref, u_ref, h2_ref, g_ref, wu_ref, wd_ref, dup_ref, dh2_ref, dg_ref, dhb, acc):
        i = pl.program_id(0)
        j = pl.program_id(1)

        @pl.when(j == 0)
        def _():
            dhb[...] = dh_ref[...].astype(BF16)

        part = None
        for b in range(nb):
            cols = slice(b * tf, (b + 1) * tf)
            dup = (_dot_nt(dhb[...], wd_ref[b]) * (2.0 * u_ref[:, cols].astype(F32))).astype(BF16)
            dup_ref[:, cols] = dup
            p = _dot_nt(dup, wu_ref[b])
            part = p if part is None else part + p
        _accumulate(acc, part, j == 0)

        @pl.when(j == nj - 1)
        def _():
            dx, dg = _rms_bwd(h2_ref[...], g_ref[...], acc[...])
            dh2_ref[...] = dh_ref[...] + dx
            _accumulate(dg_ref, dg, i == 0)

    return pl.pallas_call(
        body, name="mlp_bwd", grid=(ni, nj),
        in_specs=[pl.BlockSpec((tm, d), lambda i, j: (i, 0)),
                  pl.BlockSpec((tm, nb * tf), lambda i, j: (i, j)),
                  pl.BlockSpec((tm, d), lambda i, j: (i, 0)),
                  pl.BlockSpec((1, d), lambda i, j: (0, 0)),
                  pl.BlockSpec((nb, d, tf), lambda i, j: (j, 0, 0)),
                  pl.BlockSpec((nb, tf, d), lambda i, j: (j, 0, 0))],
        out_specs=[pl.BlockSpec((tm, nb * tf), lambda i, j: (i, j)),
                   pl.BlockSpec((tm, d), lambda i, j: (i, 0)),
                   pl.BlockSpec((1, d), lambda i, j: (0, 0)),
                   pl.BlockSpec((tm, d), lambda i, j: (i, 0))],
        out_shape=[jax.ShapeDtypeStruct((tp, nf * tf), BF16), jax.ShapeDtypeStruct((tp, d), F32),
                   jax.ShapeDtypeStruct((1, d), F32), jax.ShapeDtypeStruct((tp, d), BF16)],
        scratch_shapes=[pltpu.VMEM((tm, d), F32)],
        compiler_params=_cparams("arbitrary", "arbitrary"),
    )(dh, u, h2, g2, gup, gdown)


def dw_mlp(u, dhb, z2, dup, tf):
    rows, dff = u.shape
    d = z2.shape[1]
    nf = dff // tf
    nb = 2 if nf % 2 == 0 else 1

    def body(u_ref, dh_ref, z_ref, dup_ref, dwd_ref, dwu_ref, zt):
        @pl.when(pl.program_id(0) == 0)
        def _():
            zt[...] = z_ref[...].T

        for b in range(nb):
            cols = slice(b * tf, (b + 1) * tf)
            uf = u_ref[:, cols].astype(F32)
            dwd_ref[b] = _dot_tn((uf * uf).astype(BF16), dh_ref[...]).astype(BF16)
            dwu_ref[b] = _dot(zt[...], dup_ref[:, cols]).astype(BF16)

    col = lambda j: (0, j)
    fix = lambda j: (0, 0)
    return pl.pallas_call(
        body, name="dw_mlp", grid=(nf // nb,),
        in_specs=[pl.BlockSpec((rows, nb * tf), col), pl.BlockSpec((rows, d), fix),
                  pl.BlockSpec((rows, d), fix), pl.BlockSpec((rows, nb * tf), col)],
        out_specs=[pl.BlockSpec((nb, tf, d), lambda j: (j, 0, 0)),
                   pl.BlockSpec((nb, d, tf), lambda j: (j, 0, 0))],
        out_shape=[jax.ShapeDtypeStruct((nf, tf, d), BF16), jax.ShapeDtypeStruct((nf, d, tf), BF16)],
        scratch_shapes=[pltpu.VMEM((d, rows), BF16)],
        compiler_params=_cparams("arbitrary"),
    )(u, dhb, z2, dup)


def out_proj_bwd(dh2, mix, o, rec, ga, gr, wout, tm):
    tp, d = dh2.shape
    aw, rw = o.shape[1], rec.shape[1]
    ni = tp // tm

    def body(dh_ref, mix_ref, o_ref, rec_ref, ga_ref, gr_ref, w_ref, do_ref, drec_ref, dga_ref, dgr_ref, dw_ref, acc):
        i = pl.program_id(0)
        dhb = dh_ref[...].astype(BF16)
        dmix = _dot_nt(dhb, w_ref[...])
        do, dga = _rms_bwd(o_ref[...], ga_ref[...], dmix[:, 0:aw])
        drec, dgr = _rms_bwd(rec_ref[...], gr_ref[...], dmix[:, aw:aw + rw])
        do_ref[...] = do
        drec_ref[...] = drec
        _accumulate(dga_ref, dga, i == 0)
        _accumulate(dgr_ref, dgr, i == 0)
        _accumulate(acc, _dot_tn(mix_ref[...], dhb), i == 0)

        @pl.when(i == ni - 1)
        def _():
            dw_ref[...] = acc[...].astype(BF16)

    row = lambda i: (i, 0)
    fix = lambda i: (0, 0)
    return pl.pallas_call(
        body, name="out_proj_bwd", grid=(ni,),
        in_specs=[pl.BlockSpec((tm, d), row), pl.BlockSpec((tm, d), row),
                  pl.BlockSpec((tm, aw), row), pl.BlockSpec((tm, rw), row),
                  pl.BlockSpec((1, aw), fix), pl.BlockSpec((1, rw), fix), pl.BlockSpec((d, d), fix)],
        out_specs=[pl.BlockSpec((tm, aw), row), pl.BlockSpec((tm, rw), row),
                   pl.BlockSpec((1, aw), fix), pl.BlockSpec((1, rw), fix), pl.BlockSpec((d, d), fix)],
        out_shape=[jax.ShapeDtypeStruct((tp, aw), F32), jax.ShapeDtypeStruct((tp, rw), F32),
                   jax.ShapeDtypeStruct((1, aw), F32), jax.ShapeDtypeStruct((1, rw), F32),
                   jax.ShapeDtypeStruct((d, d), BF16)],
        scratch_shapes=[pltpu.VMEM((d, d), F32)],
        compiler_params=_cparams("arbitrary"),
    )(dh2, mix, o, rec, ga, gr, wout)


def rec_bwd(drec, hr, xc, gates, rest, convw, wga, wgx, lru, rw):
    tp = rest.shape[0]
    ng = rw // LANES

    def body(drec_ref, hr_ref, xc_ref, r_ref, ig_ref, a_ref, mult_ref, xr_ref, yr_ref, cw_ref, wga_ref, wgx_ref,
             l_ref, dxr_ref, dyr_ref, dwga_ref, dwgx_ref, vec_ref, a_s, u_s, lam_s):
        xc = xc_ref[...]
        h = hr_ref[...]
        drec = drec_ref[...]
        r, ig, a, mult = r_ref[...], ig_ref[...], a_ref[...], mult_ref[...]
        xcb = xc.astype(BF16)
        ls = _log_sigmoid(l_ref[...])
        gelu, gelu_grad = _gelu_and_grad(yr_ref[...])
        dyr_ref[...] = (drec * h * gelu_grad).astype(BF16)
        a_s[...] = _shift_up(a, 1, tp)
        u_s[...] = drec * gelu
        _scan_rows(a_s, u_s, lam_s, tp, reverse=True)
        lam = lam_s[...]
        da = lam * _shift_down(h, 1, tp)
        dmult = lam * ig * xc
        dig = lam * mult * xc
        dxc = lam * mult * ig
        dlog_a = da * a - dmult * (a * a) / mult
        dr = dlog_a * (RG_C * ls)
        dl = jnp.sum(dlog_a * (RG_C * r), axis=0, keepdims=True) * _sigmoid(-l_ref[...])
        dpa = dr * r * (1.0 - r)
        dpx = dig * ig * (1.0 - ig)
        dpab = dpa.astype(BF16)
        dpxb = dpx.astype(BF16)
        dxc = dxc + _dot_nt(dpab, wga_ref[...]) + _dot_nt(dpxb, wgx_ref[...])
        dwga_ref[...] = _dot_tn(xcb, dpab)
        dwgx_ref[...] = _dot_tn(xcb, dpxb)
        xr = xr_ref[...]
        dxr = cw_ref[CONV_WIDTH - 1:CONV_WIDTH, :] * dxc
        for k in range(1, CONV_WIDTH):
            dxr = dxr + cw_ref[CONV_WIDTH - 1 - k:CONV_WIDTH - k, :] * _shift_up(dxc, k, tp)
        dxr_ref[...] = dxr.astype(BF16)
        for k in range(CONV_WIDTH):
            vec_ref[k:k + 1, :] = jnp.sum(dxc * _shift_down(xr, CONV_WIDTH - 1 - k, tp), axis=0, keepdims=True)
        vec_ref[4:5, :] = jnp.sum(dxc, axis=0, keepdims=True)
        vec_ref[5:6, :] = jnp.sum(dpa, axis=0, keepdims=True)
        vec_ref[6:7, :] = jnp.sum(dpx, axis=0, keepdims=True)
        vec_ref[7:8, :] = dl

    col = lambda g: (0, g)
    vec = pl.BlockSpec((1, LANES), col)
    big = pl.BlockSpec((tp, LANES), col)
    sq = pl.BlockSpec((None, LANES, LANES), lambda g: (g, 0, 0))
    return pl.pallas_call(
        body, name="rec_bwd", grid=(ng,),
        in_specs=[big] * 8 + [pl.BlockSpec((tp, LANES), lambda g: (0, ng + g)),
                                pl.BlockSpec((CONV_WIDTH, LANES), col), sq, sq, vec],
        out_specs=[big, big, sq, sq, pl.BlockSpec((None, SUBLANES, LANES), lambda g: (g, 0, 0))],
        out_shape=[jax.ShapeDtypeStruct((tp, rw), BF16), jax.ShapeDtypeStruct((tp, rw), BF16),
                   jax.ShapeDtypeStruct((ng, LANES, LANES), F32), jax.ShapeDtypeStruct((ng, LANES, LANES), F32),
                   jax.ShapeDtypeStruct((ng, SUBLANES, LANES), F32)],
        scratch_shapes=[pltpu.VMEM((tp, LANES), F32)] * 3,
        compiler_params=_cparams("parallel"),
    )(drec, hr, xc, *gates, rest, rest, convw, wga, wgx, lru)


def attn_bwd(qkv, do, o, lset, c, ct, nh):
    tp = qkv.shape[0]
    npair = nh // 2
    aw = nh * HEAD_DIM
    tiles = _att_tiles(tp)

    def body(q_ref, k_ref, v_ref, do_ref, o_ref, lset_ref, c_ref, ct_ref,
             dq_ref, dk_ref, dv_ref, drow_ref, dcol_ref, dk_acc, dv_acc, dq_t):
        p = pl.program_id(0)
        k_t = k_ref[...].T
        dk_acc[...] = jnp.zeros_like(dk_acc)
        dv_acc[...] = jnp.zeros_like(dv_acc)
        dcol_ref[...] = jnp.zeros_like(dcol_ref)
        drow_ref[...] = jnp.zeros_like(drow_ref)
        for r0, nr, nk in tiles:
            rs = slice(r0, r0 + nr)
            causal = (r0 + lax.broadcasted_iota(jnp.int32, (nk, nr), 1)
                      >= lax.broadcasted_iota(jnp.int32, (nk, nr), 0))
            cblk = c_ref[0:nk, :]
            ctb = ct_ref[:, rs]
            for hh in range(2):
                head = 2 * p + hh
                hs = slice(hh * HEAD_DIM, (hh + 1) * HEAD_DIM)
                q = q_ref[rs, hs]
                k = k_ref[0:nk, hs]
                dof = do_ref[rs, hs]
                do16 = dof.astype(BF16)
                delta = jnp.sum(dof * o_ref[rs, hs], axis=1, keepdims=True)
                delta_row = jnp.broadcast_to(delta, (nr, LANES)).T[0:1, :]
                s_t = _dot_nt(k, q * ATT_SCALE) + (_pick_row(ctb, head) - _pick_col(cblk, head))
                p_t = jnp.where(causal, jnp.exp(s_t - lset_ref[hh:hh + 1, rs]), 0.0)
                ds_t = p_t * (_dot_nt(v_ref[0:nk, hs], do16) - delta_row)
                p16 = p_t.astype(BF16)
                ds16 = ds_t.astype(BF16)
                dv_acc[0:nk, hs] += _dot(p16, do16)
                dk_acc[0:nk, hs] += _dot(ds16, q) * ATT_SCALE
                dq_t[hs, rs] = _dot(k_t[hs, 0:nk], ds16)
                drow_ref[hh:hh + 1, rs] = jnp.sum(ds_t, axis=0, keepdims=True)
                dcol_ref[0:nk, hs] -= jnp.broadcast_to(jnp.sum(ds_t, axis=1, keepdims=True), (nk, HEAD_DIM))
        dk_ref[...] = dk_acc[...].astype(BF16)
        dv_ref[...] = dv_acc[...].astype(BF16)
        dq_ref[...] = (dq_t[...].T * ATT_SCALE).astype(BF16)

    pair = lambda p: (0, p)
    return pl.pallas_call(
        body, name="attn_bwd", grid=(npair,),
        in_specs=[pl.BlockSpec((tp, LANES), pair),
                  pl.BlockSpec((tp, LANES), lambda p: (0, npair + p)),
                  pl.BlockSpec((tp, LANES), lambda p: (0, 2 * npair + p)),
                  pl.BlockSpec((tp, LANES), pair),
                  pl.BlockSpec((tp, LANES), pair),
                  pl.BlockSpec((None, SUBLANES, tp), lambda p: (p, 0, 0)),
                  pl.BlockSpec((tp, LANES), lambda p: (0, 0)),
                  pl.BlockSpec((SUBLANES, tp), lambda p: (0, 0))],
        out_specs=[pl.BlockSpec((tp, LANES), pair), pl.BlockSpec((tp, LANES), pair),
                   pl.BlockSpec((tp, LANES), pair),
                   pl.BlockSpec((None, SUBLANES, tp), lambda p: (p, 0, 0)),
                   pl.BlockSpec((tp, LANES), pair)],
        out_shape=[jax.ShapeDtypeStruct((tp, aw), BF16), jax.ShapeDtypeStruct((tp, aw), BF16),
                   jax.ShapeDtypeStruct((tp, aw), BF16),
                   jax.ShapeDtypeStruct((npair, SUBLANES, tp), F32),
                   jax.ShapeDtypeStruct((tp, aw), F32)],
        scratch_shapes=[pltpu.VMEM((tp, LANES), F32), pltpu.VMEM((tp, LANES), F32),
                        pltpu.VMEM((LANES, tp), F32)],
        compiler_params=_cparams("parallel"),
    )(qkv, qkv, qkv, do, o, lset, c, ct)


def fgate_bwd(dct8, drs, rest, bf_pad, fcol):
    tp = rest.shape[0]
    aw = drs.shape[1]
    nb = tp // ATT_BLOCK
    B = ATT_BLOCK

    def body(d_ref, drs_ref, f_ref, b_ref, dfl_ref, db_ref, pad_s, dc_s):
        r_i = lax.broadcasted_iota(jnp.int32, (B, B), 0)
        c_i = lax.broadcasted_iota(jnp.int32, (B, B), 1)
        triu = (c_i >= r_i).astype(BF16)
        sel = (lax.broadcasted_iota(jnp.int32, (aw, LANES), 0)
               == HEAD_DIM * lax.broadcasted_iota(jnp.int32, (aw, LANES), 1)).astype(BF16)
        pad_s[...] = jnp.zeros_like(pad_s)
        pad_s[0:SUBLANES, :] = d_ref[...]
        dc_s[...] = pad_s[...].T + _dot_split3(drs_ref[...], sel)
        carry = jnp.zeros((1, LANES), F32)
        for i in range(nb - 1, -1, -1):
            sl = slice(i * B, (i + 1) * B)
            rc = _split3_dot(triu, dc_s[sl, :])
            dc_s[sl, :] = rc + carry
            carry = carry + rc[0:1, :]
        dfl = dc_s[...] * _sigmoid(-(f_ref[...] + b_ref[...]))
        dfl_ref[...] = dfl.astype(BF16)
        db_ref[...] = jnp.sum(dfl, axis=0, keepdims=True)

    return pl.pallas_call(
        body, name="fgate_bwd", grid=(1,),
        in_specs=[pl.BlockSpec((SUBLANES, tp), lambda i: (0, 0)),
                  pl.BlockSpec((tp, aw), lambda i: (0, 0)),
                  pl.BlockSpec((tp, LANES), lambda i: (0, fcol)),
                  pl.BlockSpec((1, LANES), lambda i: (0, 0))],
        out_specs=[pl.BlockSpec((tp, LANES), lambda i: (0, 0)),
                   pl.BlockSpec((1, LANES), lambda i: (0, 0))],
        out_shape=[jax.ShapeDtypeStruct((tp, LANES), BF16), jax.ShapeDtypeStruct((1, LANES), F32)],
        scratch_shapes=[pltpu.VMEM((LANES, tp), F32), pltpu.VMEM((tp, LANES), F32)],
        compiler_params=_cparams("arbitrary"),
    )(dct8, drs, rest, bf_pad)


def in_proj_bwd(dh2, parts, w_in_t, wrest_t, h, g1, tm):
    tp, d = h.shape
    dq, dk, dv, dxr, dyr, dfl = parts
    aw, rw = dq.shape[1], dxr.shape[1]

    def body(dh2_ref, dq_ref, dk_ref, dv_ref, dxr_ref, dyr_ref, dfl_ref, wq_ref, wr_ref, h_ref, g_ref,
             dh_ref, dg_ref):
        i = pl.program_id(0)
        dz = _dot(dq_ref[...], wq_ref[0:aw, :])
        dz += _dot(dk_ref[...], wq_ref[aw:2 * aw, :])
        dz += _dot(dv_ref[...], wq_ref[2 * aw:3 * aw, :])
        dz += _dot(dxr_ref[...], wr_ref[0:rw, :])
        dz += _dot(dyr_ref[...], wr_ref[rw:2 * rw, :])
        dz += _dot(dfl_ref[...], wr_ref[2 * rw:2 * rw + LANES, :])
        dx, dg = _rms_bwd(h_ref[...], g_ref[...], dz)
        dh_ref[...] = dh2_ref[...] + dx
        _accumulate(dg_ref, dg, i == 0)

    row = lambda i: (i, 0)
    fix = lambda i: (0, 0)
    return pl.pallas_call(
        body, name="in_proj_bwd", grid=(tp // tm,),
        in_specs=[pl.BlockSpec((tm, d), row),
                  pl.BlockSpec((tm, aw), row), pl.BlockSpec((tm, aw), row), pl.BlockSpec((tm, aw), row),
                  pl.BlockSpec((tm, rw), row), pl.BlockSpec((tm, rw), row), pl.BlockSpec((tm, LANES), row),
                  pl.BlockSpec((3 * aw, d), fix), pl.BlockSpec(wrest_t.shape, fix),
                  pl.BlockSpec((tm, d), row), pl.BlockSpec((1, d), fix)],
        out_specs=[pl.BlockSpec((tm, d), row), pl.BlockSpec((1, d), fix)],
        out_shape=[jax.ShapeDtypeStruct((tp, d), F32), jax.ShapeDtypeStruct((1, d), F32)],
        compiler_params=_cparams("arbitrary"),
    )(dh2, dq, dk, dv, dxr, dyr, dfl, w_in_t, wrest_t, h, g1)


def dw_in_t(z, parts, nh, tr):
    tp, d = z.shape
    dq, dk, dv, dxr, dyr, dfl = parts
    aw, rw = dq.shape[1], dxr.shape[1]
    d_in = 3 * aw + nh + 2 * rw
    blk = d_in // N_DEV
    nr = tp // tr
    offs = [(0, aw), (aw, aw), (2 * aw, aw), (3 * aw + nh, rw), (3 * aw + nh + rw, rw)]

    def body(z_ref, dq_ref, dk_ref, dv_ref, dxr_ref, dyr_ref, dfl_ref, o_ref, acc):
        r = pl.program_id(0)

        @pl.when(r == 0)
        def _():
            acc[...] = jnp.zeros_like(acc)

        zt = z_ref[...]
        for (o, n), ref in zip(offs, (dq_ref, dk_ref, dv_ref, dxr_ref, dyr_ref)):
            acc[o:o + n, :] += _dot_tn(ref[...], zt)
        acc[3 * aw:3 * aw + nh, :] += _dot_tn(dfl_ref[...], zt)[0:nh, :]

        @pl.when(r == nr - 1)
        def _():
            for p in range(N_DEV):
                o_ref[p] = acc[p * blk:(p + 1) * blk, :].astype(BF16)

    row = lambda r: (r, 0)
    return pl.pallas_call(
        body, name="dw_in", grid=(nr,),
        in_specs=[pl.BlockSpec((tr, d), row),
                  pl.BlockSpec((tr, aw), row), pl.BlockSpec((tr, aw), row), pl.BlockSpec((tr, aw), row),
                  pl.BlockSpec((tr, rw), row), pl.BlockSpec((tr, rw), row), pl.BlockSpec((tr, LANES), row)],
        out_specs=pl.BlockSpec((N_DEV, blk, d), lambda r: (0, 0, 0)),
        out_shape=jax.ShapeDtypeStruct((N_DEV, blk, d), BF16),
        scratch_shapes=[pltpu.VMEM((d_in, d), F32)],
        compiler_params=_cparams("arbitrary"),
    )(z, dq, dk, dv, dxr, dyr, dfl)


def _place():
    return lax.axis_index("x"), lax.axis_index("y"), lax.axis_index("c")


HBM = pl.BlockSpec(memory_space=pltpu.HBM)
SEM = pl.BlockSpec(memory_space=pltpu.SEMAPHORE)
EFFECT = pltpu.SideEffectType.DATAFLOW_SIDE_EFFECTING


def _in_hbm(a):
    return pltpu.with_memory_space_constraint(a, pltpu.HBM)


def _as_list(a):
    return list(a) if isinstance(a, (list, tuple)) else [a]


def _gather_targets(x, y, c):
    return [(x, y, 1 - c), (1 - x, y, c), (x, 1 - y, c), (1 - x, 1 - y, c)]


def _slot(t):
    return 4 * t[0] + 2 * t[1] + t[2]


def gather_start(groups, name):
    flat = [a for g in groups for a in g]
    n = len(flat)
    ng = len(groups)
    lands = [lax.empty((N_DEV,) + a.shape, a.dtype) for a in flat]

    def body(*refs):
        src, land = refs[:n], refs[n:2 * n]
        sems = refs[2 * n:2 * n + 2 * ng]
        token = refs[-1]
        x, y, c = _place()
        me = 4 * x + 2 * y + c
        i = 0
        for gi, g in enumerate(groups):
            for a in range(len(g)):
                for k, t in enumerate(_gather_targets(x, y, c)):
                    pltpu.make_async_remote_copy(
                        src_ref=src[i], dst_ref=land[i].at[me],
                        send_sem=sems[2 * gi].at[4 * a + k], recv_sem=sems[2 * gi + 1].at[4 * a + k],
                        device_id=t, device_id_type=MESH).start()
                i += 1
        token[...] = jnp.zeros_like(token)

    sem_shapes = []
    for g in groups:
        sem_shapes += [pltpu.SemaphoreType.DMA((4 * len(g),)), pltpu.SemaphoreType.DMA((4 * len(g),))]
    out = pl.pallas_call(
        body, name=name,
        out_shape=sem_shapes + [pltpu.HBM(a.shape, a.dtype) for a in flat + lands]
        + [jax.ShapeDtypeStruct((SUBLANES, LANES), F32)],
        in_specs=[HBM] * (2 * n),
        out_specs=[SEM] * (2 * ng) + [HBM] * (2 * n) + [pl.BlockSpec(memory_space=pltpu.VMEM)],
        input_output_aliases={i: 2 * ng + i for i in range(2 * n)},
        compiler_params=pltpu.CompilerParams(has_side_effects=EFFECT),
    )(*[_in_hbm(a) for a in flat + lands])
    sems = out[:2 * ng]
    thru = out[2 * ng:2 * ng + 2 * n]
    srcs_t, lands_t = thru[:n], thru[n:]
    res, i = [], 0
    for gi, g in enumerate(groups):
        res.append((sems[2 * gi], sems[2 * gi + 1], srcs_t[i:i + len(g)], lands_t[i:i + len(g)]))
        i += len(g)
    return res, out[-1]


def gather_wait(send, recv, srcs, lands, after, name):
    n = len(srcs)

    def body(*refs):
        src, land = refs[:n], refs[n:2 * n]
        send_sem, recv_sem = refs[2 * n], refs[2 * n + 1]
        x, y, c = _place()
        for a in range(n):
            for k, t in enumerate(_gather_targets(x, y, c)):
                cp = pltpu.make_async_remote_copy(
                    src_ref=src[a], dst_ref=land[a].at[_slot(t)],
                    send_sem=send_sem.at[4 * a + k], recv_sem=recv_sem.at[4 * a + k],
                    device_id=t, device_id_type=MESH)
                cp.wait_send()
                cp.wait_recv()

    out = pl.pallas_call(
        body, name=name,
        out_shape=[pltpu.HBM(a.shape, a.dtype) for a in list(srcs) + list(lands)],
        in_specs=[HBM] * (2 * n) + [SEM, SEM] + [ANY] * len(_as_list(after)),
        out_specs=[HBM] * (2 * n),
        input_output_aliases={i: i for i in range(2 * n)},
        compiler_params=pltpu.CompilerParams(has_side_effects=EFFECT),
    )(*srcs, *lands, send, recv, *_as_list(after))
    return out[:n], out[n:]


def forward_start(lands, name):
    n = len(lands)

    def body(*refs):
        land = refs[:n]
        send_sem, recv_sem = refs[n], refs[n + 1]
        token = refs[-1]
        x, y, c = _place()
        for a in range(n):
            for j, chip in enumerate([(1 - x, y), (x, 1 - y), (1 - x, 1 - y)]):
                blk = land[a].at[_slot((*chip, c))]
                pltpu.make_async_remote_copy(src_ref=blk, dst_ref=blk, send_sem=send_sem.at[3 * a + j],
                                             recv_sem=recv_sem.at[3 * a + j], device_id=(x, y, 1 - c),
                                             device_id_type=MESH).start()
        token[...] = jnp.zeros_like(token)

    out = pl.pallas_call(
        body, name=name,
        out_shape=[pltpu.SemaphoreType.DMA((3 * n,)), pltpu.SemaphoreType.DMA((3 * n,))]
        + [pltpu.HBM(a.shape, a.dtype) for a in lands] + [jax.ShapeDtypeStruct((SUBLANES, LANES), F32)],
        in_specs=[HBM] * n,
        out_specs=[SEM, SEM] + [HBM] * n + [pl.BlockSpec(memory_space=pltpu.VMEM)],
        input_output_aliases={i: 2 + i for i in range(n)},
        compiler_params=pltpu.CompilerParams(has_side_effects=EFFECT),
    )(*[_in_hbm(a) for a in lands])
    return out[0], out[1], out[2:2 + n], out[-1][0, 0]


def forward_wait(send, recv, lands, after, name):
    n = len(lands)

    def body(*refs):
        land = refs[:n]
        send_sem, recv_sem = refs[n], refs[n + 1]
        x, y, c = _place()
        for a in range(n):
            for j, chip in enumerate([(1 - x, y), (x, 1 - y), (1 - x, 1 - y)]):
                cp = pltpu.make_async_remote_copy(
                    src_ref=land[a].at[_slot((*chip, c))], dst_ref=land[a].at[_slot((*chip, 1 - c))],
                    send_sem=send_sem.at[3 * a + j], recv_sem=recv_sem.at[3 * a + j],
                    device_id=(x, y, 1 - c), device_id_type=MESH)
                cp.wait_send()
                cp.wait_recv()

    return pl.pallas_call(
        body, name=name,
        out_shape=[pltpu.HBM(a.shape, a.dtype) for a in lands],
        in_specs=[HBM] * n + [SEM, SEM, ANY],
        out_specs=[HBM] * n,
        input_output_aliases={i: i for i in range(n)},
        compiler_params=pltpu.CompilerParams(has_side_effects=EFFECT),
    )(*lands, send, recv, after)


def _relations():
    return [(dx, dy, dc) for dx in (0, 1) for dy in (0, 1) for dc in (0, 1) if dx + dy + dc]


def _peer(x, y, c, rel):
    return ((1 - x) if rel[0] else x, (1 - y) if rel[1] else y, (1 - c) if rel[2] else c)


def exchange_start(srcs, lands, layer, name, after=()):
    n = len(srcs)
    after = _as_list(after)

    def body(*refs):
        src, land = refs[:n], refs[n:2 * n]
        send_sem, recv_sem = refs[2 * n + len(after)], refs[2 * n + len(after) + 1]
        token = refs[-1]
        x, y, c = _place()
        me = 4 * x + 2 * y + c
        for k, rel in enumerate(_relations()):
            peer = _peer(x, y, c, rel)
            for a in range(n):
                pltpu.make_async_remote_copy(
                    src_ref=src[a] if layer is None else src[a].at[_slot(peer)],
                    dst_ref=land[a].at[me] if layer is None else land[a].at[me, layer],
                    send_sem=send_sem.at[7 * a + k], recv_sem=recv_sem.at[7 * a + k],
                    device_id=peer, device_id_type=MESH).start()
        token[...] = jnp.zeros_like(token)

    out = pl.pallas_call(
        body, name=name,
        out_shape=[pltpu.SemaphoreType.DMA((7 * n,)), pltpu.SemaphoreType.DMA((7 * n,))]
        + [pltpu.HBM(a.shape, a.dtype) for a in list(srcs) + list(lands)]
        + [jax.ShapeDtypeStruct((SUBLANES, LANES), F32)],
        in_specs=[HBM] * (2 * n) + [ANY] * len(after),
        out_specs=[SEM, SEM] + [HBM] * (2 * n) + [pl.BlockSpec(memory_space=pltpu.VMEM)],
        input_output_aliases={i: 2 + i for i in range(2 * n)},
        compiler_params=pltpu.CompilerParams(has_side_effects=EFFECT),
    )(*[_in_hbm(a) for a in list(srcs) + list(lands)], *after)
    return out[0], out[1], out[2:2 + n], out[2 + n:2 + 2 * n], out[-1][0, 0]


def exchange_wait(send, recv, srcs, lands, after, layer, name):
    n = len(srcs)

    def body(*refs):
        src, land = refs[:n], refs[n:2 * n]
        send_sem, recv_sem = refs[2 * n], refs[2 * n + 1]
        x, y, c = _place()
        for k, rel in enumerate(_relations()):
            peer = _peer(x, y, c, rel)
            for a in range(n):
                cp = pltpu.make_async_remote_copy(
                    src_ref=src[a] if layer is None else src[a].at[_slot(peer)],
                    dst_ref=land[a].at[_slot(peer)] if layer is None else land[a].at[_slot(peer), layer],
                    send_sem=send_sem.at[7 * a + k], recv_sem=recv_sem.at[7 * a + k],
                    device_id=peer, device_id_type=MESH)
                cp.wait_send()
                cp.wait_recv()

    out = pl.pallas_call(
        body, name=name,
        out_shape=[pltpu.HBM(a.shape, a.dtype) for a in list(srcs) + list(lands)],
        in_specs=[HBM] * (2 * n) + [SEM, SEM] + [ANY] * len(_as_list(after)),
        out_specs=[HBM] * (2 * n),
        input_output_aliases={i: i for i in range(2 * n)},
        compiler_params=pltpu.CompilerParams(has_side_effects=EFFECT),
    )(*srcs, *lands, send, recv, *_as_list(after))
    return out[:n], out[n:]


def _adamw_math(g, w, m, v):
    m = ADAM_B1 * m + (1.0 - ADAM_B1) * g
    v = ADAM_B2 * v + (1.0 - ADAM_B2) * (g * g)
    m_hat = m / (1.0 - ADAM_B1 ** ADAM_STEP)
    v_hat = v / (1.0 - ADAM_B2 ** ADAM_STEP)
    delta = -ADAM_LR * (m_hat / (jnp.sqrt(v_hat) + ADAM_EPS) + ADAM_WD * w)
    return delta, m, v


def _sum_with_own(p_ref, own_refs, layer, me):
    own = own_refs[0][...]
    for k in range(1, len(own_refs)):
        own = jnp.where(layer == k, own_refs[k][...], own)
    g = None
    for p in range(p_ref.shape[0]):
        term = jnp.where(me == p, own, p_ref[p]).astype(F32)
        g = term if g is None else g + term
    return g


def sum_adamw(parts, owns, me, w, m, v, tr, name):
    npart, rows, cols = parts.shape
    nl = len(owns)
    per_layer = rows // nl // tr

    def body(me_ref, p_ref, *refs):
        own_refs = refs[:nl]
        w_ref, m_ref, v_ref, g_ref, d_ref, nm_ref, nv_ref = refs[nl:]
        g = _sum_with_own(p_ref, own_refs, pl.program_id(0) // per_layer, me_ref[0])
        delta, nm, nv = _adamw_math(g, w_ref[...], m_ref[...], v_ref[...])
        g_ref[...] = g
        d_ref[...] = delta
        nm_ref[...] = nm
        nv_ref[...] = nv

    blk = pl.BlockSpec((tr, cols), lambda i, me_ref: (i, 0))
    own_specs = [pl.BlockSpec((None, tr, cols),
                              lambda i, me_ref, l=l: (me_ref[0], jnp.clip(i - l * per_layer, 0, per_layer - 1), 0))
                 for l in range(nl)]
    return pl.pallas_call(
        body, name=name,
        grid_spec=pltpu.PrefetchScalarGridSpec(
            num_scalar_prefetch=1, grid=(rows // tr,),
            in_specs=[pl.BlockSpec((npart, tr, cols), lambda i, me_ref: (0, i, 0))] + own_specs + [blk, blk, blk],
            out_specs=[blk] * 4),
        out_shape=[jax.ShapeDtypeStruct((rows, cols), F32)] * 4,
        compiler_params=_cparams("arbitrary"),
    )(me, parts, *owns, w, m, v)


def sum_adamw_t(parts, owns, me, w, m, v, name):
    npart, nl, rows, cols = parts.shape

    def body(me_ref, p_ref, *refs):
        own_refs = refs[:nl]
        w_ref, m_ref, v_ref, g_ref, d_ref, nm_ref, nv_ref = refs[nl:]
        g = _sum_with_own(p_ref, own_refs, pl.program_id(0), me_ref[0])
        delta, nm, nv = _adamw_math(g, w_ref[...], m_ref[...], v_ref[...])
        g_ref[...] = g
        d_ref[...] = delta
        nm_ref[...] = nm
        nv_ref[...] = nv

    blk = pl.BlockSpec((None, rows, cols), lambda l, me_ref: (l, 0, 0))
    own_specs = [pl.BlockSpec((None, rows, cols), lambda l, me_ref: (me_ref[0], 0, 0)) for _ in range(nl)]
    return pl.pallas_call(
        body, name=name,
        grid_spec=pltpu.PrefetchScalarGridSpec(
            num_scalar_prefetch=1, grid=(nl,),
            in_specs=[pl.BlockSpec((npart, None, rows, cols), lambda l, me_ref: (0, l, 0, 0))] + own_specs
            + [blk, blk, blk],
            out_specs=[blk] * 4),
        out_shape=[jax.ShapeDtypeStruct((nl, rows, cols), F32)] * 4,
        compiler_params=_cparams("arbitrary"),
    )(me, parts, *owns, w, m, v)


def adamw_group(gs, ws, ms, vs, name):
    n = len(gs)

    def body(*refs):
        g, w, m, v, outs = refs[:n], refs[n:2 * n], refs[2 * n:3 * n], refs[3 * n:4 * n], refs[4 * n:]
        for i in range(n):
            delta, nm, nv = _adamw_math(g[i][...], w[i][...], m[i][...], v[i][...])
            outs[i][...] = delta
            outs[n + i][...] = nm
            outs[2 * n + i][...] = nv

    vmem = pl.BlockSpec(memory_space=pltpu.VMEM)
    out = pl.pallas_call(
        body, name=name,
        in_specs=[vmem] * (4 * n), out_specs=[vmem] * (3 * n),
        out_shape=[jax.ShapeDtypeStruct(a.shape, F32) for a in list(ws) * 3],
        compiler_params=_cparams(),
    )(*gs, *ws, *ms, *vs)
    return out[:n], out[n:2 * n], out[2 * n:]


def sum_parts(parts, name):
    npart, rows, cols = parts.shape

    def body(p_ref, g_ref):
        g = p_ref[0].astype(F32)
        for p in range(1, npart):
            g = g + p_ref[p].astype(F32)
        g_ref[...] = g

    return pl.pallas_call(
        body, name=name, grid=(1,),
        in_specs=[pl.BlockSpec((npart, rows, cols), lambda i: (0, 0, 0))],
        out_specs=pl.BlockSpec((rows, cols), lambda i: (0, 0)),
        out_shape=jax.ShapeDtypeStruct((rows, cols), F32),
        compiler_params=_cparams("arbitrary"),
    )(parts)


def _round_up(n, m):
    return (n + m - 1) // m * m


def _block_diag_pairs(w):
    nb, b, _ = w.shape
    per = LANES // b
    ng = nb // per
    w = w.reshape(ng, per, b, b)
    eye = jnp.eye(per, dtype=w.dtype)
    out = jnp.einsum('gpij,pq->gpiqj', w, eye).reshape(ng, LANES, LANES)
    return out.astype(BF16)


def _block_diag_extract(g, b):
    ng = g.shape[0]
    per = LANES // b
    g = g.reshape(ng, per, b, per, b)
    idx = jnp.arange(per)
    return g[:, idx, :, idx, :].transpose(1, 0, 2, 3).reshape(ng * per, b, b)


def _tiles(v):
    v = v.reshape(-1)
    n = _round_up(v.shape[0], SUBLANES * LANES)
    return jnp.pad(v, (0, n - v.shape[0])).reshape(-1, LANES)


SMALL = ['attn_norm_g', 'b_f', 'conv_w', 'conv_b', 'w_gate_a', 'b_gate_a', 'w_gate_x', 'b_gate_x',
         'lru_L', 'attn_out_g', 'rec_out_g', 'mlp_norm_g', 'final_g', 'meta']


def _pack(d):
    return jnp.concatenate([_tiles(d[n]) for n in SMALL], axis=0)


def _unpack(vec, shapes):
    out, r = {}, 0
    for n in SMALL:
        size = math.prod(shapes[n])
        nr = _round_up(size, SUBLANES * LANES) // LANES
        out[n] = vec[r:r + nr].reshape(-1)[:size].reshape(shapes[n])
        r += nr
    return out


def _row_tile(tp):
    return tp // 4 if (tp // 4) % 16 == 0 else tp


def local_step(x, tgt, meta, small, hooks):
    s, d = x.shape
    t_real = s + N_META
    tp = _round_up(t_real, ATT_BLOCK)
    depth = small['attn_norm_g'].shape[0]
    nh = small['b_f'].shape[1]
    rw = small['conv_b'].shape[1]
    blk = small['w_gate_a'].shape[2]
    tm = _row_tile(tp)
    tm2 = tp // 2
    fcol = 2 * rw // LANES

    h = jnp.concatenate([meta, x, jnp.zeros((tp - t_real, d), F32)], axis=0)
    tgt_p = jnp.pad(tgt, ((N_META, tp - t_real), (0, 0)))
    row = lambda v: v.reshape(1, -1)
    bf_pad = jnp.pad(small['b_f'], ((0, 0), (0, LANES - nh)))

    saved = []
    for l in range(depth):
        w_in_t, wrest_t, wout, tok_w = hooks.mixer_weights(l, h)
        wga = _block_diag_pairs(small['w_gate_a'][l])
        wgx = _block_diag_pairs(small['w_gate_x'][l])
        z, qkv, rest = in_proj(h, row(small['attn_norm_g'][l]) + tok_w, w_in_t, wrest_t, 3 * nh * HEAD_DIM, tm)
        c, ct = fgate_fwd(rest, bf_pad[l:l + 1], fcol)
        o, lset = attn_fwd(qkv, c, ct, nh)
        rec, hr, xc, *gates = rec_fwd(rest, small['conv_w'][l], row(small['conv_b'][l]), wga,
                                      row(small['b_gate_a'][l]), wgx, row(small['b_gate_x'][l]),
                                      row(small['lru_L'][l]), rw)
        gup, gdown, tok_w = hooks.mlp_weights(l, rec)
        h2, mix, z2, u, h3 = out_mlp_fwd(h, o, rec, row(small['attn_out_g'][l]), row(small['rec_out_g'][l]), wout,
                                         row(small['mlp_norm_g'][l]) + tok_w, gup, gdown, tm)
        saved.append(dict(h=h, z=z, qkv=qkv, rest=rest, c=c, ct=ct, o=o, lset=lset, rec=rec, hr=hr, xc=xc,
                          h2=h2, mix=mix, z2=z2, u=u, wga=wga, wgx=wgx, gates=gates,
                          w_in_t=w_in_t, wrest_t=wrest_t, wout=wout, gup=gup, gdown=gdown))
        h = h3

    dh, dgf, loss = loss_head(h, row(small['final_g']), tgt_p, t_real, tm)

    gs = {n: [None] * depth for n in SMALL if n not in ('final_g', 'meta')}
    tok = jnp.zeros((), F32)
    for l in reversed(range(depth)):
        sv = saved[l]
        gup, gdown = sv['gup'], sv['gdown']
        tf = gup.shape[2]
        dup, dh2, dg2, dhb = mlp_bwd(dh, sv['u'], sv['h2'], row(small['mlp_norm_g'][l]) + tok, gup, gdown, tm)
        gs['mlp_norm_g'][l] = dg2[0]
        do, drec, dga, dgr, dw_out = out_proj_bwd(dh2, sv['mix'], sv['o'], sv['rec'], row(small['attn_out_g'][l]),
                                                  row(small['rec_out_g'][l]), sv['wout'], tm)
        gs['attn_out_g'][l] = dga[0]
        gs['rec_out_g'][l] = dgr[0]
        dw_down, dw_up = dw_mlp(sv['u'], dhb, sv['z2'], dup, tf)
        blocks = dict(w_down=dw_down, w_up=dw_up, w_out=dw_out.reshape(N_DEV, d // N_DEV, d))
        tok = hooks.grads_ready(l, 'mlp', blocks)
        dxr, dyr, dwga, dwgx, vec = rec_bwd(drec, sv['hr'], sv['xc'], sv['gates'], sv['rest'], small['conv_w'][l],
                                            sv['wga'], sv['wgx'], row(small['lru_L'][l]) + tok, rw)
        gs['w_gate_a'][l] = _block_diag_extract(dwga, blk)
        gs['w_gate_x'][l] = _block_diag_extract(dwgx, blk)
        vec = vec.transpose(1, 0, 2).reshape(SUBLANES, rw)
        gs['conv_w'][l] = vec[0:CONV_WIDTH]
        gs['conv_b'][l] = vec[4]
        gs['b_gate_a'][l] = vec[5]
        gs['b_gate_x'][l] = vec[6]
        gs['lru_L'][l] = vec[7]
        dq, dk, dv, drow, dcol = attn_bwd(sv['qkv'], do, sv['o'], sv['lset'], sv['c'], sv['ct'] + tok, nh)
        drow8 = drow[:, 0:2, :].reshape(nh, tp)
        if nh < SUBLANES:
            drow8 = jnp.pad(drow8, ((0, SUBLANES - nh), (0, 0)))
        dfl, dbf = fgate_bwd(drow8, dcol, sv['rest'], bf_pad[l:l + 1], fcol)
        gs['b_f'][l] = dbf[0, 0:nh]
        parts = (dq, dk, dv, dxr, dyr, dfl)
        dh, dg1 = in_proj_bwd(dh2, parts, sv['w_in_t'], sv['wrest_t'], sv['h'], row(small['attn_norm_g'][l]), tm)
        gs['attn_norm_g'][l] = dg1[0]
        first = ()
        if l == 0:
            grads = {n: jnp.stack(v) for n, v in gs.items()}
            grads['final_g'] = dgf[0]
            grads['meta'] = dh[0:N_META]
            first = hooks.small_ready(grads)
        dw_in = dw_in_t(sv['z'], parts, nh, tm2)
        tok = hooks.grads_ready(l, 'in', dict(w_in=dw_in), first)

    return loss[0, 0], dh


def prep_weights(g_in, g_out, nh, rw):
    d = g_in.shape[2]
    w_in_t = g_in.reshape(-1, d)
    f0 = 3 * nh * HEAD_DIM
    wrest_t = jnp.concatenate([w_in_t[f0 + nh:f0 + nh + 2 * rw],
                               jnp.pad(w_in_t[f0:f0 + nh], ((0, LANES - nh), (0, 0)))], axis=0)
    return w_in_t, wrest_t, g_out.reshape(d, d)


BIG = ['w_in', 'w_out', 'w_up', 'w_down']
EXCHANGE_GROUPS = {'mlp': ['w_down', 'w_up', 'w_out'], 'in': ['w_in']}
WEIGHTS = ['meta', 'attn_norm_g', 'w_in', 'b_f', 'conv_w', 'conv_b', 'w_gate_a', 'b_gate_a', 'w_gate_x', 'b_gate_x',
           'lru_L', 'attn_out_g', 'rec_out_g', 'w_out', 'mlp_norm_g', 'w_up', 'w_down', 'final_g']


def _set_own(arr, own, me):
    return lax.dynamic_update_slice_in_dim(arr, own[None], me, axis=0)


class _Step:
    def __init__(self, w, nh, rw, me):
        self.w, self.nh, self.rw, self.me = w, nh, rw, me
        depth = w['w_in'].shape[0]
        first = [w['w_in_t'][:, 0, :].astype(BF16), w['w_out'][0].astype(BF16), w['meta'], w['conv_w']]
        self.pending, token = gather_start([first], "gather_start_0")
        zero = token[0, 0].astype(BF16)
        groups = [[w['w_up'][0].astype(BF16) + zero, w['w_down'][0].astype(BF16) + zero]]
        for l in range(1, depth):
            groups.append([w['w_in_t'][:, l, :].astype(BF16) + zero, w['w_out'][l].astype(BF16) + zero])
            groups.append([w['w_up'][l].astype(BF16) + zero, w['w_down'][l].astype(BF16) + zero])
        rest, _ = gather_start(groups, "gather_start_1")
        self.pending += rest
        self.first_after = rest[0][2][0]
        self.gathered = {}
        self.passing = {}
        self.token = jnp.zeros((), F32)
        self.lands = {n: lax.empty((N_DEV,) + w[n].shape, BF16) for n in BIG}
        din8, _, d = w['w_in_t'].shape
        self.lands['w_in'] = lax.empty((N_DEV, depth, din8, d), BF16)
        self.started = []
        self.small = None

    def _pass_on(self, gi, after):
        if gi < len(self.pending) and gi not in self.passing:
            send, recv, srcs, lands = self.pending[gi]
            srcs, lands = gather_wait(send, recv, srcs, lands, after, "gather_wait_%d" % gi)
            fsend, frecv, lands, token = forward_start(lands, "forward_start_%d" % gi)
            self.passing[gi] = (fsend, frecv, srcs, lands)
            self.token = token

    def group(self, gi, after):
        if gi not in self.gathered:
            self._pass_on(gi, after)
            fsend, frecv, srcs, lands = self.passing[gi]
            lands = forward_wait(fsend, frecv, lands, after, "forward_wait_%d" % gi)
            self.gathered[gi] = [_set_own(g, own, self.me) for g, own in zip(lands, srcs)]
            if gi >= 2:
                self._pass_on(gi + 1, lands[0])
        return self.gathered[gi]

    def mixer_weights(self, l, after):
        g = self.group(2 * l, after)
        return (*prep_weights(g[0], g[1], self.nh, self.rw), self.token)

    def mlp_weights(self, l, after):
        g = self.group(2 * l + 1, after)
        return g[0], g[1], self.token

    def grads_ready(self, l, group, blocks, after=()):
        names = EXCHANGE_GROUPS[group]
        send, recv, srcs, lands, token = exchange_start(
            [blocks[n] for n in names], [self.lands[n] for n in names], l, "exchange_start_%s_%d" % (group, l),
            after)
        for n, a in zip(names, lands):
            self.lands[n] = a
        self.started.append((l, group, send, recv, srcs))
        return token

    def small_ready(self, grads):
        self.small_shapes = {n: grads[n].shape for n in SMALL}
        packed = _pack(grads).astype(BF16)
        send, recv, srcs, lands, token = exchange_start(
            [packed], [lax.empty((N_DEV,) + packed.shape, BF16)], None, "small_start")
        self.small = (send, recv, srcs, lands)
        return srcs[0]

    def small_sum(self, after):
        send, recv, srcs, lands = self.small
        srcs, lands = exchange_wait(send, recv, srcs, lands, after, None, "small_wait")
        parts = _set_own(lands[0], srcs[0], self.me)
        return _unpack(sum_parts(parts, "sum_small_grads"), self.small_shapes)

    def received(self, group, after):
        names = EXCHANGE_GROUPS[group]
        own = {n: [None] * self.w[n].shape[0] for n in names}
        for l, grp, send, recv, srcs in self.started:
            if grp != group:
                continue
            srcs, lands = exchange_wait(send, recv, srcs, [self.lands[n] for n in names], after, l,
                                        "exchange_wait_%s_%d" % (group, l))
            for n, a, sr in zip(names, lands, srcs):
                self.lands[n] = a
                own[n][l] = sr
        return {n: (self.lands[n], own[n]) for n in names}


def kernel(x, meta, attn_norm_g, w_in, b_f, conv_w, conv_b, w_gate_a, b_gate_a, w_gate_x, b_gate_x, lru_L, attn_out_g, rec_out_g, w_out, mlp_norm_g, w_up, w_down, final_g, loss_target, m_meta, m_attn_norm_g, m_w_in, m_b_f, m_conv_w, m_conv_b, m_w_gate_a, m_b_gate_a, m_w_gate_x, m_b_gate_x, m_lru_L, m_attn_out_g, m_rec_out_g, m_w_out, m_mlp_norm_g, m_w_up, m_w_down, m_final_g, v_meta, v_attn_norm_g, v_w_in, v_b_f, v_conv_w, v_conv_b, v_w_gate_a, v_b_gate_a, v_w_gate_x, v_b_gate_x, v_lru_L, v_attn_out_g, v_rec_out_g, v_w_out, v_mlp_norm_g, v_w_up, v_w_down, v_final_g):
    w = dict(meta=meta, attn_norm_g=attn_norm_g, w_in=w_in, b_f=b_f, conv_w=conv_w, conv_b=conv_b,
             w_gate_a=w_gate_a, b_gate_a=b_gate_a, w_gate_x=w_gate_x, b_gate_x=b_gate_x, lru_L=lru_L,
             attn_out_g=attn_out_g, rec_out_g=rec_out_g, w_out=w_out, mlp_norm_g=mlp_norm_g, w_up=w_up,
             w_down=w_down, final_g=final_g)
    mo = dict(meta=m_meta, attn_norm_g=m_attn_norm_g, w_in=m_w_in, b_f=m_b_f, conv_w=m_conv_w, conv_b=m_conv_b,
              w_gate_a=m_w_gate_a, b_gate_a=m_b_gate_a, w_gate_x=m_w_gate_x, b_gate_x=m_b_gate_x, lru_L=m_lru_L,
              attn_out_g=m_attn_out_g, rec_out_g=m_rec_out_g, w_out=m_w_out, mlp_norm_g=m_mlp_norm_g,
              w_up=m_w_up, w_down=m_w_down, final_g=m_final_g)
    vo = dict(meta=v_meta, attn_norm_g=v_attn_norm_g, w_in=v_w_in, b_f=v_b_f, conv_w=v_conv_w, conv_b=v_conv_b,
              w_gate_a=v_w_gate_a, b_gate_a=v_b_gate_a, w_gate_x=v_w_gate_x, b_gate_x=v_b_gate_x, lru_L=v_lru_L,
              attn_out_g=v_attn_out_g, rec_out_g=v_rec_out_g, w_out=v_w_out, mlp_norm_g=v_mlp_norm_g,
              w_up=v_w_up, w_down=v_w_down, final_g=v_final_g)
    depth = w_in.shape[0]
    nh = b_f.shape[1]
    rw = conv_b.shape[1]
    me = 4 * lax.axis_index("x") + 2 * lax.axis_index("y") + lax.axis_index("c")

    w['w_in_t'] = jnp.transpose(w_in, (2, 0, 1))
    swap = lambda a: jnp.swapaxes(a, 1, 2)
    step = _Step(w, nh, rw, me)
    g0 = step.group(0, step.first_after)
    meta_full = g0[2].transpose(1, 0, 2).reshape(N_META, -1)
    conv_full = g0[3].transpose(1, 2, 0, 3).reshape(depth, CONV_WIDTH, rw)
    small = {n: w[n] for n in SMALL}
    small['conv_w'] = conv_full

    loss_part, dh0 = local_step(x[0], loss_target[0], meta_full, small, step)
    loss = lax.psum(loss_part, ("x", "y", "c"))
    grad_x = dh0[N_META:N_META + x.shape[1]][None]

    out_g, out_d, out_m, out_v = {}, {}, {}, {}
    me1 = me.reshape(1).astype(jnp.int32)

    def update_big(group, after):
        for n, (r, owns) in step.received(group, after).items():
            if n == 'w_in':
                out = sum_adamw_t(r, owns, me1, swap(w[n]), swap(mo[n]), swap(vo[n]), "adamw_w_in")
                out = [swap(a) for a in out]
            else:
                shp = w[n].shape
                rows, cols = shp[0] * shp[1], shp[2]
                tr = min(512 if cols <= 512 else 256, shp[1])
                out = sum_adamw(r.reshape(N_DEV, rows, cols), owns, me1, w[n].reshape(rows, cols),
                                mo[n].reshape(rows, cols), vo[n].reshape(rows, cols), tr, "adamw_" + n)
                out = [a.reshape(shp) for a in out]
            out_g[n], out_d[n], out_m[n], out_v[n] = out
            after = out[0]
        return after

    update_big('mlp', step.started[-1][4][0])

    gsum = step.small_sum([out_g[n] for n in EXCHANGE_GROUPS['mlp']])
    gsum['meta'] = lax.dynamic_slice_in_dim(gsum['meta'], me * meta.shape[1], meta.shape[1], axis=1)
    gsum['conv_w'] = lax.dynamic_slice_in_dim(gsum['conv_w'], me * conv_w.shape[2], conv_w.shape[2], axis=2)
    as2d = lambda a: a.reshape(-1, a.shape[-1])
    deltas, new_m, new_v = adamw_group([as2d(gsum[n]) for n in SMALL], [as2d(w[n]) for n in SMALL],
                                       [as2d(mo[n]) for n in SMALL], [as2d(vo[n]) for n in SMALL], "adamw_small")
    for i, n in enumerate(SMALL):
        out_g[n] = gsum[n]
        out_d[n], out_m[n], out_v[n] = [a[i].reshape(w[n].shape) for a in (deltas, new_m, new_v)]

    update_big('in', deltas[0])

    return (loss, grad_x, *[out_g[n] for n in WEIGHTS], *[out_d[n] for n in WEIGHTS],
            *[out_m[n] for n in WEIGHTS], *[out_v[n] for n in WEIGHTS])
```

```python
import math

import jax
import jax.numpy as jnp
from jax import lax
from jax.experimental import pallas as pl
from jax.experimental.pallas import tpu as pltpu

F32 = jnp.float32
BF16 = jnp.bfloat16

N_DEV = 8
N_META = 16
HEAD_DIM = 64
CONV_WIDTH = 4
RG_C = 8.0
NORM_EPS = 1e-6
LANES = 128
SUBLANES = 8
ATT_BLOCK = 128
ATT_TQ = 512
NEG_BIG = -1e30
ATT_SCALE = 1.0 / math.sqrt(HEAD_DIM)

ADAM_LR = 0.001
ADAM_B1 = 0.9
ADAM_B2 = 0.999
ADAM_EPS = 1e-08
ADAM_WD = 0.01
ADAM_STEP = 10

VMEM_LIMIT_BYTES = 56 * 1024 * 1024
MESH = pl.DeviceIdType.MESH
ANY = pl.BlockSpec(memory_space=pl.ANY)


def _cparams(*sem):
    return pltpu.CompilerParams(dimension_semantics=sem if sem else None,
                                vmem_limit_bytes=VMEM_LIMIT_BYTES)


def _dot(a, b):
    return jnp.dot(a, b, preferred_element_type=F32)


def _dot_nt(a, b):
    return lax.dot_general(a, b, (((1,), (1,)), ((), ())), preferred_element_type=F32)


def _dot_tn(a, b):
    return lax.dot_general(a, b, (((0,), (0,)), ((), ())), preferred_element_type=F32)


def _sigmoid(x):
    return 0.5 * (1.0 + jnp.tanh(0.5 * x))


def _log_sigmoid(x):
    return jnp.minimum(x, 0.0) - jnp.log(1.0 + jnp.exp(-jnp.abs(x)))


def _expm1(x):
    series = x * (1.0 + x * (0.5 + x * (1.0 / 6.0 + x * (1.0 / 24.0))))
    return jnp.where(jnp.abs(x) < 1e-2, series, jnp.exp(x) - 1.0)


_GELU_K = math.sqrt(2.0 / math.pi)
_GELU_C = 0.044715


def _gelu(x):
    t = jnp.tanh(_GELU_K * (x + _GELU_C * x * x * x))
    return 0.5 * x * (1.0 + t)


def _gelu_and_grad(x):
    x2 = x * x
    t = jnp.tanh(_GELU_K * (x + _GELU_C * x2 * x))
    half = 0.5 * (1.0 + t)
    return x * half, half + 0.5 * x * (1.0 - t * t) * _GELU_K * (1.0 + 3.0 * _GELU_C * x2)


def _split3_dot(tri, x):
    hi = x.astype(BF16)
    r1 = x - hi.astype(F32)
    mid = r1.astype(BF16)
    lo = (r1 - mid.astype(F32)).astype(BF16)
    return _dot(tri, hi) + _dot(tri, mid) + _dot(tri, lo)


def _dot_split3(x, sel):
    hi = x.astype(BF16)
    r1 = x - hi.astype(F32)
    mid = r1.astype(BF16)
    lo = (r1 - mid.astype(F32)).astype(BF16)
    return _dot(hi, sel) + _dot(mid, sel) + _dot(lo, sel)


def _rms_fwd(x, g):
    r = lax.rsqrt(jnp.mean(x * x, axis=-1, keepdims=True) + NORM_EPS)
    return x * r * g


def _rms_bwd(x, g, dy):
    r = lax.rsqrt(jnp.mean(x * x, axis=-1, keepdims=True) + NORM_EPS)
    xn = x * r
    dxn = dy * g
    dx = r * (dxn - xn * jnp.mean(dxn * xn, axis=-1, keepdims=True))
    return dx, jnp.sum(dy * xn, axis=0, keepdims=True)


def _accumulate(ref, val, first):
    @pl.when(first)
    def _():
        ref[...] = val

    @pl.when(jnp.logical_not(first))
    def _():
        ref[...] += val


def in_proj(h, g1, w_in_t, wrest_t, nq, tm):
    tp, d = h.shape
    nr = wrest_t.shape[0]

    def body(h_ref, g_ref, wq_ref, wr_ref, z_ref, qkv_ref, rest_ref):
        z = _rms_fwd(h_ref[...], g_ref[...]).astype(BF16)
        z_ref[...] = z
        qkv_ref[...] = _dot_nt(z, wq_ref[...]).astype(BF16)
        rest_ref[...] = _dot_nt(z, wr_ref[...])

    return pl.pallas_call(
        body, name="in_proj", grid=(tp // tm,),
        in_specs=[pl.BlockSpec((tm, d), lambda i: (i, 0)),
                  pl.BlockSpec((1, d), lambda i: (0, 0)),
                  pl.BlockSpec((nq, d), lambda i: (0, 0)),
                  pl.BlockSpec((nr, d), lambda i: (0, 0))],
        out_specs=[pl.BlockSpec((tm, d), lambda i: (i, 0)),
                   pl.BlockSpec((tm, nq), lambda i: (i, 0)),
                   pl.BlockSpec((tm, nr), lambda i: (i, 0))],
        out_shape=[jax.ShapeDtypeStruct((tp, d), BF16),
                   jax.ShapeDtypeStruct((tp, nq), BF16),
                   jax.ShapeDtypeStruct((tp, nr), F32)],
        compiler_params=_cparams("parallel"),
    )(h, g1, w_in_t, wrest_t)


def fgate_fwd(rest, bf_pad, fcol):
    tp = rest.shape[0]
    nb = tp // ATT_BLOCK

    def body(f_ref, b_ref, c_ref, ct_ref):
        r_i = lax.broadcasted_iota(jnp.int32, (ATT_BLOCK, ATT_BLOCK), 0)
        c_i = lax.broadcasted_iota(jnp.int32, (ATT_BLOCK, ATT_BLOCK), 1)
        tri = (r_i >= c_i).astype(BF16)
        c_ref[...] = _log_sigmoid(f_ref[...] + b_ref[...])
        carry = jnp.zeros((1, LANES), F32)
        for i in range(nb):
            sl = slice(i * ATT_BLOCK, (i + 1) * ATT_BLOCK)
            cs = _split3_dot(tri, c_ref[sl, :]) + carry
            carry = cs[ATT_BLOCK - 1:ATT_BLOCK, :]
            c_ref[sl, :] = cs
        ct_ref[...] = c_ref[...].T[0:SUBLANES, :]

    return pl.pallas_call(
        body, name="fgate_fwd", grid=(1,),
        in_specs=[pl.BlockSpec((tp, LANES), lambda i: (0, fcol)),
                  pl.BlockSpec((1, LANES), lambda i: (0, 0))],
        out_specs=[pl.BlockSpec((tp, LANES), lambda i: (0, 0)),
                   pl.BlockSpec((SUBLANES, tp), lambda i: (0, 0))],
        out_shape=[jax.ShapeDtypeStruct((tp, LANES), F32),
                   jax.ShapeDtypeStruct((SUBLANES, tp), F32)],
        compiler_params=_cparams("arbitrary"),
    )(rest, bf_pad)


def _pick_col(blk, head):
    lane = lax.broadcasted_iota(jnp.int32, blk.shape, 1)
    return jnp.sum(jnp.where(lane == head, blk, 0.0), axis=1, keepdims=True)


def _pick_row(blk, head):
    sub = lax.broadcasted_iota(jnp.int32, blk.shape, 0)
    return jnp.sum(jnp.where(sub == head, blk, 0.0), axis=0, keepdims=True)


def _att_tiles(tp):
    out, r0 = [], 0
    while r0 < tp:
        rows = min(ATT_TQ, tp - r0)
        out.append((r0, rows, r0 + rows))
        r0 += rows
    return out


def attn_fwd(qkv, c, ct, nh):
    tp = qkv.shape[0]
    npair = nh // 2
    tiles = _att_tiles(tp)

    def body(q_ref, k_ref, v_ref, c_ref, ct_ref, o_ref, lset_ref):
        p = pl.program_id(0)
        lset_ref[...] = jnp.zeros_like(lset_ref)
        halves = [(r0 + h * (nr // 2), nr // 2, nk) if nr == ATT_TQ else (r0, nr, nk)
                  for r0, nr, nk in tiles for h in range(2 if nr == ATT_TQ else 1)]
        for r0, nr, nk in halves:
            rs = slice(r0, r0 + nr)
            causal = (r0 + lax.broadcasted_iota(jnp.int32, (nr, nk), 0)
                      >= lax.broadcasted_iota(jnp.int32, (nr, nk), 1))
            cblk = c_ref[rs, :]
            ctb = ct_ref[:, 0:nk]
            for hh in range(2):
                head = 2 * p + hh
                hs = slice(hh * HEAD_DIM, (hh + 1) * HEAD_DIM)
                q = q_ref[rs, hs] * ATT_SCALE
                s = _dot_nt(q, k_ref[0:nk, hs]) + (_pick_col(cblk, head) - _pick_row(ctb, head))
                s = jnp.where(causal, s, NEG_BIG)
                m = jnp.max(s, axis=1, keepdims=True)
                pm = jnp.exp(s - m)
                l = jnp.sum(pm, axis=1, keepdims=True)
                o_ref[rs, hs] = _dot(pm.astype(BF16), v_ref[0:nk, hs]) / l
                lse = m + jnp.log(l)
                lset_ref[hh:hh + 1, rs] = jnp.broadcast_to(lse, (nr, LANES)).T[0:1, :]

    pair = lambda p: (0, p)
    return pl.pallas_call(
        body, name="attn_fwd", grid=(npair,),
        in_specs=[pl.BlockSpec((tp, LANES), pair),
                  pl.BlockSpec((tp, LANES), lambda p: (0, npair + p)),
                  pl.BlockSpec((tp, LANES), lambda p: (0, 2 * npair + p)),
                  pl.BlockSpec((tp, LANES), lambda p: (0, 0)),
                  pl.BlockSpec((SUBLANES, tp), lambda p: (0, 0))],
        out_specs=[pl.BlockSpec((tp, LANES), pair),
                   pl.BlockSpec((None, SUBLANES, tp), lambda p: (p, 0, 0))],
        out_shape=[jax.ShapeDtypeStruct((tp, nh * HEAD_DIM), F32),
                   jax.ShapeDtypeStruct((npair, SUBLANES, tp), F32)],
        compiler_params=_cparams("parallel"),
    )(qkv, qkv, qkv, c, ct)


def _shift_down(x, k, n):
    if k == 0:
        return x
    rows = lax.broadcasted_iota(jnp.int32, x.shape, 0)
    return jnp.where(rows >= k, pltpu.roll(x, k, 0), 0.0)


def _shift_up(x, k, n):
    if k == 0:
        return x
    rows = lax.broadcasted_iota(jnp.int32, x.shape, 0)
    return jnp.where(rows < n - k, pltpu.roll(x, n - k, 0), 0.0)


def _conv_fwd(xr, cw_ref, cb_ref, n):
    xc = cw_ref[CONV_WIDTH - 1:CONV_WIDTH, :] * xr + cb_ref[...]
    for k in range(1, CONV_WIDTH):
        xc = xc + cw_ref[CONV_WIDTH - 1 - k:CONV_WIDTH - k, :] * _shift_down(xr, k, n)
    return xc


def _gates(xc, wga_ref, bga_ref, wgx_ref, bgx_ref, l_ref):
    xcb = xc.astype(BF16)
    r = _sigmoid(_dot(xcb, wga_ref[...]) + bga_ref[...])
    ig = _sigmoid(_dot(xcb, wgx_ref[...]) + bgx_ref[...])
    ls = _log_sigmoid(l_ref[...])
    log_a = RG_C * r * ls
    a = jnp.exp(log_a)
    mult = jnp.sqrt(-_expm1(2.0 * log_a))
    return xcb, r, ig, ls, log_a, a, mult


SCAN_UNROLL = 16


def _scan_rows(a_s, u_s, out_ref, n, reverse):
    nt = n // SUBLANES
    per = SCAN_UNROLL if nt % SCAN_UNROLL == 0 else 1
    row = lax.broadcasted_iota(jnp.int32, (SUBLANES, LANES), 0)
    last = 0 if reverse else SUBLANES - 1

    def tile_scan(a, u):
        for d in (1, 2, 4):
            if reverse:
                keep = row < SUBLANES - d
                sh = SUBLANES - d
            else:
                keep = row >= d
                sh = d
            a_sh = jnp.where(keep, pltpu.roll(a, sh, 0), 1.0)
            u_sh = jnp.where(keep, pltpu.roll(u, sh, 0), 0.0)
            u = a * u_sh + u
            a = a * a_sh
        return a, u

    def step(t, carry):
        tiles = []
        for k in range(per):
            tt = t * per + k
            if reverse:
                tt = nt - 1 - tt
            off = pl.multiple_of(tt * SUBLANES, SUBLANES)
            a, u = tile_scan(a_s[pl.ds(off, SUBLANES), :], u_s[pl.ds(off, SUBLANES), :])
            tiles.append((off, a, u))
        for off, a, u in tiles:
            out_ref[pl.ds(off, SUBLANES), :] = u + a * carry
            carry = u[last:last + 1, :] + a[last:last + 1, :] * carry
        return carry

    lax.fori_loop(0, nt // per, step, jnp.zeros((1, LANES), F32))


def rec_fwd(rest, convw, convb, wga, bga, wgx, bgx, lru, rw):
    tp = rest.shape[0]
    ng = rw // LANES

    def body(xr_ref, yr_ref, cw_ref, cb_ref, wga_ref, bga_ref, wgx_ref, bgx_ref, l_ref,
             rec_ref, hr_ref, xc_ref, r_ref, ig_ref, a_ref, mult_ref, u_s):
        xc = _conv_fwd(xr_ref[...], cw_ref, cb_ref, tp)
        xc_ref[...] = xc
        _, r, ig, ls, log_a, a, mult = _gates(xc, wga_ref, bga_ref, wgx_ref, bgx_ref, l_ref)
        r_ref[...] = r
        ig_ref[...] = ig
        a_ref[...] = a
        mult_ref[...] = mult
        u_s[...] = mult * ig * xc
        _scan_rows(a_ref, u_s, hr_ref, tp, reverse=False)
        rec_ref[...] = hr_ref[...] * _gelu(yr_ref[...])

    col = lambda g: (0, g)
    vec = pl.BlockSpec((1, LANES), col)
    big = pl.BlockSpec((tp, LANES), col)
    return pl.pallas_call(
        body, name="rec_fwd", grid=(ng,),
        in_specs=[big, pl.BlockSpec((tp, LANES), lambda g: (0, ng + g)),
                  pl.BlockSpec((CONV_WIDTH, LANES), col), vec,
                  pl.BlockSpec((None, LANES, LANES), lambda g: (g, 0, 0)), vec,
                  pl.BlockSpec((None, LANES, LANES), lambda g: (g, 0, 0)), vec, vec],
        out_specs=[big] * 7,
        out_shape=[jax.ShapeDtypeStruct((tp, rw), F32)] * 7,
        scratch_shapes=[pltpu.VMEM((tp, LANES), F32)],
        compiler_params=_cparams("parallel"),
    )(rest, rest, convw, convb, wga, bga, wgx, bgx, lru)


def out_mlp_fwd(h, o, rec, ga, gr, wout, g2, gup, gdown, tm):
    tp, d = h.shape
    aw, rw = o.shape[1], rec.shape[1]
    nf = gup.shape[0]
    tf = gup.shape[2]
    nb = MLP_BLOCKS if nf % MLP_BLOCKS == 0 else 1
    nj = nf // nb

    def body(h_ref, o_ref, rec_ref, ga_ref, gr_ref, w_ref, g2_ref, wu_ref, wd_ref,
             h2_ref, mix_ref, z2_ref, u_ref, h3_ref, acc):
        j = pl.program_id(1)

        @pl.when(j == 0)
        def _():
            mix_ref[:, 0:aw] = _rms_fwd(o_ref[...], ga_ref[...]).astype(BF16)
            mix_ref[:, aw:aw + rw] = _rms_fwd(rec_ref[...], gr_ref[...]).astype(BF16)
            h2 = h_ref[...] + _dot(mix_ref[...], w_ref[...])
            h2_ref[...] = h2
            acc[...] = h2
            z2_ref[...] = _rms_fwd(h2, g2_ref[...]).astype(BF16)

        z = z2_ref[...]
        part = None
        for b in range(nb):
            u = jnp.maximum(_dot(z, wu_ref[b]), 0.0)
            u_ref[:, b * tf:(b + 1) * tf] = u.astype(BF16)
            p = _dot((u * u).astype(BF16), wd_ref[b])
            part = p if part is None else part + p
        acc[...] += part

        @pl.when(j == nj - 1)
        def _():
            h3_ref[...] = acc[...]

    row = lambda i, j: (i, 0)
    fix = lambda i, j: (0, 0)
    return pl.pallas_call(
        body, name="out_mlp_fwd", grid=(tp // tm, nj),
        in_specs=[pl.BlockSpec((tm, d), row), pl.BlockSpec((tm, aw), row), pl.BlockSpec((tm, rw), row),
                  pl.BlockSpec((1, aw), fix), pl.BlockSpec((1, rw), fix),
                  pl.BlockSpec((d, d), fix), pl.BlockSpec((1, d), fix),
                  pl.BlockSpec((nb, d, tf), lambda i, j: (j, 0, 0)),
                  pl.BlockSpec((nb, tf, d), lambda i, j: (j, 0, 0))],
        out_specs=[pl.BlockSpec((tm, d), row), pl.BlockSpec((tm, d), row), pl.BlockSpec((tm, d), row),
                   pl.BlockSpec((tm, nb * tf), lambda i, j: (i, j)), pl.BlockSpec((tm, d), row)],
        out_shape=[jax.ShapeDtypeStruct((tp, d), F32), jax.ShapeDtypeStruct((tp, d), BF16),
                   jax.ShapeDtypeStruct((tp, d), BF16), jax.ShapeDtypeStruct((tp, nf * tf), BF16),
                   jax.ShapeDtypeStruct((tp, d), F32)],
        scratch_shapes=[pltpu.VMEM((tm, d), F32)],
        compiler_params=_cparams("parallel", "arbitrary"),
    )(h, o, rec, ga, gr, wout, g2, gup, gdown)


MLP_BLOCKS = 4


def loss_head(h, gf, tgt, t_real, tm):
    tp, d = h.shape

    def body(h_ref, g_ref, t_ref, dh_ref, dg_ref, loss_ref):
        i = pl.program_id(0)
        x = h_ref[...]
        g = g_ref[...]
        r = lax.rsqrt(jnp.mean(x * x, axis=-1, keepdims=True) + NORM_EPS)
        xn = x * r
        rows = i * tm + lax.broadcasted_iota(jnp.int32, (tm, 1), 0)
        valid = jnp.logical_and(rows >= N_META, rows < t_real)
        e = jnp.where(valid, xn * g - t_ref[...], 0.0)
        part = 0.5 * jnp.sum(jnp.sum(e * e, axis=1, keepdims=True) / d, axis=0, keepdims=True)
        dy = e / d
        dxn = dy * g
        dh_ref[...] = r * (dxn - xn * jnp.mean(dxn * xn, axis=-1, keepdims=True))
        _accumulate(dg_ref, jnp.sum(dy * xn, axis=0, keepdims=True), i == 0)
        _accumulate(loss_ref, jnp.broadcast_to(part, (1, LANES)), i == 0)

    row = lambda i: (i, 0)
    fix = lambda i: (0, 0)
    return pl.pallas_call(
        body, name="loss_head", grid=(tp // tm,),
        in_specs=[pl.BlockSpec((tm, d), row), pl.BlockSpec((1, d), fix), pl.BlockSpec((tm, d), row)],
        out_specs=[pl.BlockSpec((tm, d), row), pl.BlockSpec((1, d), fix), pl.BlockSpec((1, LANES), fix)],
        out_shape=[jax.ShapeDtypeStruct((tp, d), F32), jax.ShapeDtypeStruct((1, d), F32),
                   jax.ShapeDtypeStruct((1, LANES), F32)],
        compiler_params=_cparams("arbitrary"),
    )(h, gf, tgt)


def mlp_bwd(dh, u, h2, g2, gup, gdown, tm):
    tp, d = dh.shape
    nf = gup.shape[0]
    tf = gup.shape[2]
    nb = MLP_BLOCKS if nf % MLP_BLOCKS == 0 else 1
    nj = nf // nb
    ni = tp // tm

    def body(dh_ref, u_ref, h2_ref, g_ref, wu_ref, wd_ref, dup_ref, dh2_ref, dg_ref, dhb, acc):
        i = pl.program_id(0)
        j = pl.program_id(1)

        @pl.when(j == 0)
        def _():
            dhb[...] = dh_ref[...].astype(BF16)

        part = None
        for b in range(nb):
            cols = slice(b * tf, (b + 1) * tf)
            dup = (_dot_nt(dhb[...], wd_ref[b]) * (2.0 * u_ref[:, cols].astype(F32))).astype(BF16)
            dup_ref[:, cols] = dup
            p = _dot_nt(dup, wu_ref[b])
            part = p if part is None else part + p
        _accumulate(acc, part, j == 0)

        @pl.when(j == nj - 1)
        def _():
            dx, dg = _rms_bwd(h2_ref[...], g_ref[...], acc[...])
            dh2_ref[...] = dh_ref[...] + dx
            _accumulate(dg_ref, dg, i == 0)

    return pl.pallas_call(
        body, name="mlp_bwd", grid=(ni, nj),
        in_specs=[pl.BlockSpec((tm, d), lambda i, j: (i, 0)),
                  pl.BlockSpec((tm, nb * tf), lambda i, j: (i, j)),
                  pl.BlockSpec((tm, d), lambda i, j: (i, 0)),
                  pl.BlockSpec((1, d), lambda i, j: (0, 0)),
                  pl.BlockSpec((nb, d, tf), lambda i, j: (j, 0, 0)),
                  pl.BlockSpec((nb, tf, d), lambda i, j: (j, 0, 0))],
        out_specs=[pl.BlockSpec((tm, nb * tf), lambda i, j: (i, j)),
                   pl.BlockSpec((tm, d), lambda i, j: (i, 0)),
                   pl.BlockSpec((1, d), lambda i, j: (0, 0)),
                   pl.BlockSpec((tm, d), lambda i, j: (i, 0))],
        out_shape=[jax.ShapeDtypeStruct((tp, nf * tf), BF16), jax.ShapeDtypeStruct((tp, d), F32),
                   jax.ShapeDtypeStruct((1, d), F32), jax.ShapeDtypeStruct((tp, d), BF16)],
        scratch_shapes=[pltpu.VMEM((tm, d), F32)],
        compiler_params=_cparams("arbitrary", "arbitrary"),
    )(dh, u, h2, g2, gup, gdown)


def dw_mlp(u, dhb, z2, dup, tf):
    rows, dff = u.shape
    d = z2.shape[1]
    nf = dff // tf

    def body(u_ref, dh_ref, z_ref, dup_ref, dwd_ref, dwu_ref, zt):
        @pl.when(pl.program_id(0) == 0)
        def _():
            zt[...] = z_ref[...].T

        uf = u_ref[...].astype(F32)
        dwd_ref[...] = _dot_tn((uf * uf).astype(BF16), dh_ref[...]).astype(BF16)
        dwu_ref[...] = _dot(zt[...], dup_ref[...]).astype(BF16)

    col = lambda j: (0, j)
    fix = lambda j: (0, 0)
    return pl.pallas_call(
        body, name="dw_mlp", grid=(nf,),
        in_specs=[pl.BlockSpec((rows, tf), col), pl.BlockSpec((rows, d), fix),
                  pl.BlockSpec((rows, d), fix), pl.BlockSpec((rows, tf), col)],
        out_specs=[pl.BlockSpec((None, tf, d), lambda j: (j, 0, 0)),
                   pl.BlockSpec((None, d, tf), lambda j: (j, 0, 0))],
        out_shape=[jax.ShapeDtypeStruct((nf, tf, d), BF16), jax.ShapeDtypeStruct((nf, d, tf), BF16)],
        scratch_shapes=[pltpu.VMEM((d, rows), BF16)],
        compiler_params=_cparams("arbitrary"),
    )(u, dhb, z2, dup)


def out_proj_bwd(dh2, mix, o, rec, ga, gr, wout, tm):
    tp, d = dh2.shape
    aw, rw = o.shape[1], rec.shape[1]
    ni = tp // tm

    def body(dh_ref, mix_ref, o_ref, rec_ref, ga_ref, gr_ref, w_ref, do_ref, drec_ref, dga_ref, dgr_ref, dw_ref, acc):
        i = pl.program_id(0)
        dhb = dh_ref[...].astype(BF16)
        dmix = _dot_nt(dhb, w_ref[...])
        do, dga = _rms_bwd(o_ref[...], ga_ref[...], dmix[:, 0:aw])
        drec, dgr = _rms_bwd(rec_ref[...], gr_ref[...], dmix[:, aw:aw + rw])
        do_ref[...] = do
        drec_ref[...] = drec
        _accumulate(dga_ref, dga, i == 0)
        _accumulate(dgr_ref, dgr, i == 0)
        _accumulate(acc, _dot_tn(mix_ref[...], dhb), i == 0)

        @pl.when(i == ni - 1)
        def _():
            dw_ref[...] = acc[...].astype(BF16)

    row = lambda i: (i, 0)
    fix = lambda i: (0, 0)
    return pl.pallas_call(
        body, name="out_proj_bwd", grid=(ni,),
        in_specs=[pl.BlockSpec((tm, d), row), pl.BlockSpec((tm, d), row),
                  pl.BlockSpec((tm, aw), row), pl.BlockSpec((tm, rw), row),
                  pl.BlockSpec((1, aw), fix), pl.BlockSpec((1, rw), fix), pl.BlockSpec((d, d), fix)],
        out_specs=[pl.BlockSpec((tm, aw), row), pl.BlockSpec((tm, rw), row),
                   pl.BlockSpec((1, aw), fix), pl.BlockSpec((1, rw), fix), pl.BlockSpec((d, d), fix)],
        out_shape=[jax.ShapeDtypeStruct((tp, aw), F32), jax.ShapeDtypeStruct((tp, rw), F32),
                   jax.ShapeDtypeStruct((1, aw), F32), jax.ShapeDtypeStruct((1, rw), F32),
                   jax.ShapeDtypeStruct((d, d), BF16)],
        scratch_shapes=[pltpu.VMEM((d, d), F32)],
        compiler_params=_cparams("arbitrary"),
    )(dh2, mix, o, rec, ga, gr, wout)


def rec_bwd(drec, hr, xc, gates, rest, convw, wga, wgx, lru, rw):
    tp = rest.shape[0]
    ng = rw // LANES

    def body(drec_ref, hr_ref, xc_ref, r_ref, ig_ref, a_ref, mult_ref, xr_ref, yr_ref, cw_ref, wga_ref, wgx_ref,
             l_ref, dxr_ref, dyr_ref, dwga_ref, dwgx_ref, vec_ref, a_s, u_s, lam_s):
        xc = xc_ref[...]
        h = hr_ref[...]
        drec = drec_ref[...]
        r, ig, a, mult = r_ref[...], ig_ref[...], a_ref[...], mult_ref[...]
        xcb = xc.astype(BF16)
        ls = _log_sigmoid(l_ref[...])
        gelu, gelu_grad = _gelu_and_grad(yr_ref[...])
        dyr_ref[...] = (drec * h * gelu_grad).astype(BF16)
        a_s[...] = _shift_up(a, 1, tp)
        u_s[...] = drec * gelu
        _scan_rows(a_s, u_s, lam_s, tp, reverse=True)
        lam = lam_s[...]
        da = lam * _shift_down(h, 1, tp)
        dmult = lam * ig * xc
        dig = lam * mult * xc
        dxc = lam * mult * ig
        dlog_a = da * a - dmult * (a * a) / mult
        dr = dlog_a * (RG_C * ls)
        dl = jnp.sum(dlog_a * (RG_C * r), axis=0, keepdims=True) * _sigmoid(-l_ref[...])
        dpa = dr * r * (1.0 - r)
        dpx = dig * ig * (1.0 - ig)
        dpab = dpa.astype(BF16)
        dpxb = dpx.astype(BF16)
        dxc = dxc + _dot_nt(dpab, wga_ref[...]) + _dot_nt(dpxb, wgx_ref[...])
        dwga_ref[...] = _dot_tn(xcb, dpab)
        dwgx_ref[...] = _dot_tn(xcb, dpxb)
        xr = xr_ref[...]
        dxr = cw_ref[CONV_WIDTH - 1:CONV_WIDTH, :] * dxc
        for k in range(1, CONV_WIDTH):
            dxr = dxr + cw_ref[CONV_WIDTH - 1 - k:CONV_WIDTH - k, :] * _shift_up(dxc, k, tp)
        dxr_ref[...] = dxr.astype(BF16)
        for k in range(CONV_WIDTH):
            vec_ref[k:k + 1, :] = jnp.sum(dxc * _shift_down(xr, CONV_WIDTH - 1 - k, tp), axis=0, keepdims=True)
        vec_ref[4:5, :] = jnp.sum(dxc, axis=0, keepdims=True)
        vec_ref[5:6, :] = jnp.sum(dpa, axis=0, keepdims=True)
        vec_ref[6:7, :] = jnp.sum(dpx, axis=0, keepdims=True)
        vec_ref[7:8, :] = dl

    col = lambda g: (0, g)
    vec = pl.BlockSpec((1, LANES), col)
    big = pl.BlockSpec((tp, LANES), col)
    sq = pl.BlockSpec((None, LANES, LANES), lambda g: (g, 0, 0))
    return pl.pallas_call(
        body, name="rec_bwd", grid=(ng,),
        in_specs=[big] * 8 + [pl.BlockSpec((tp, LANES), lambda g: (0, ng + g)),
                                pl.BlockSpec((CONV_WIDTH, LANES), col), sq, sq, vec],
        out_specs=[big, big, sq, sq, pl.BlockSpec((None, SUBLANES, LANES), lambda g: (g, 0, 0))],
        out_shape=[jax.ShapeDtypeStruct((tp, rw), BF16), jax.ShapeDtypeStruct((tp, rw), BF16),
                   jax.ShapeDtypeStruct((ng, LANES, LANES), F32), jax.ShapeDtypeStruct((ng, LANES, LANES), F32),
                   jax.ShapeDtypeStruct((ng, SUBLANES, LANES), F32)],
        scratch_shapes=[pltpu.VMEM((tp, LANES), F32)] * 3,
        compiler_params=_cparams("parallel"),
    )(drec, hr, xc, *gates, rest, rest, convw, wga, wgx, lru)


def attn_bwd(qkv, do, o, lset, c, ct, nh):
    tp = qkv.shape[0]
    npair = nh // 2
    aw = nh * HEAD_DIM
    tiles = _att_tiles(tp)

    def body(q_ref, k_ref, v_ref, do_ref, o_ref, lset_ref, c_ref, ct_ref,
             dq_ref, dk_ref, dv_ref, drow_ref, dcol_ref, dk_acc, dv_acc, dq_t):
        p = pl.program_id(0)
        k_t = k_ref[...].T
        dk_acc[...] = jnp.zeros_like(dk_acc)
        dv_acc[...] = jnp.zeros_like(dv_acc)
        dcol_ref[...] = jnp.zeros_like(dcol_ref)
        drow_ref[...] = jnp.zeros_like(drow_ref)
        for r0, nr, nk in tiles:
            rs = slice(r0, r0 + nr)
            causal = (r0 + lax.broadcasted_iota(jnp.int32, (nk, nr), 1)
                      >= lax.broadcasted_iota(jnp.int32, (nk, nr), 0))
            cblk = c_ref[0:nk, :]
            ctb = ct_ref[:, rs]
            for hh in range(2):
                head = 2 * p + hh
                hs = slice(hh * HEAD_DIM, (hh + 1) * HEAD_DIM)
                q = q_ref[rs, hs]
                k = k_ref[0:nk, hs]
                dof = do_ref[rs, hs]
                do16 = dof.astype(BF16)
                delta = jnp.sum(dof * o_ref[rs, hs], axis=1, keepdims=True)
                delta_row = jnp.broadcast_to(delta, (nr, LANES)).T[0:1, :]
                s_t = _dot_nt(k, q * ATT_SCALE) + (_pick_row(ctb, head) - _pick_col(cblk, head))
                p_t = jnp.where(causal, jnp.exp(s_t - lset_ref[hh:hh + 1, rs]), 0.0)
                ds_t = p_t * (_dot_nt(v_ref[0:nk, hs], do16) - delta_row)
                p16 = p_t.astype(BF16)
                ds16 = ds_t.astype(BF16)
                dv_acc[0:nk, hs] += _dot(p16, do16)
                dk_acc[0:nk, hs] += _dot(ds16, q) * ATT_SCALE
                dq_t[hs, rs] = _dot(k_t[hs, 0:nk], ds16)
                drow_ref[hh:hh + 1, rs] = jnp.sum(ds_t, axis=0, keepdims=True)
                dcol_ref[0:nk, hs] -= jnp.broadcast_to(jnp.sum(ds_t, axis=1, keepdims=True), (nk, HEAD_DIM))
        dk_ref[...] = dk_acc[...].astype(BF16)
        dv_ref[...] = dv_acc[...].astype(BF16)
        dq_ref[...] = (dq_t[...].T * ATT_SCALE).astype(BF16)

    pair = lambda p: (0, p)
    return pl.pallas_call(
        body, name="attn_bwd", grid=(npair,),
        in_specs=[pl.BlockSpec((tp, LANES), pair),
                  pl.BlockSpec((tp, LANES), lambda p: (0, npair + p)),
                  pl.BlockSpec((tp, LANES), lambda p: (0, 2 * npair + p)),
                  pl.BlockSpec((tp, LANES), pair),
                  pl.BlockSpec((tp, LANES), pair),
                  pl.BlockSpec((None, SUBLANES, tp), lambda p: (p, 0, 0)),
                  pl.BlockSpec((tp, LANES), lambda p: (0, 0)),
                  pl.BlockSpec((SUBLANES, tp), lambda p: (0, 0))],
        out_specs=[pl.BlockSpec((tp, LANES), pair), pl.BlockSpec((tp, LANES), pair),
                   pl.BlockSpec((tp, LANES), pair),
                   pl.BlockSpec((None, SUBLANES, tp), lambda p: (p, 0, 0)),
                   pl.BlockSpec((tp, LANES), pair)],
        out_shape=[jax.ShapeDtypeStruct((tp, aw), BF16), jax.ShapeDtypeStruct((tp, aw), BF16),
                   jax.ShapeDtypeStruct((tp, aw), BF16),
                   jax.ShapeDtypeStruct((npair, SUBLANES, tp), F32),
                   jax.ShapeDtypeStruct((tp, aw), F32)],
        scratch_shapes=[pltpu.VMEM((tp, LANES), F32), pltpu.VMEM((tp, LANES), F32),
                        pltpu.VMEM((LANES, tp), F32)],
        compiler_params=_cparams("parallel"),
    )(qkv, qkv, qkv, do, o, lset, c, ct)


def fgate_bwd(dct8, drs, rest, bf_pad, fcol):
    tp = rest.shape[0]
    aw = drs.shape[1]
    nb = tp // ATT_BLOCK
    B = ATT_BLOCK

    def body(d_ref, drs_ref, f_ref, b_ref, dfl_ref, db_ref, pad_s, dc_s):
        r_i = lax.broadcasted_iota(jnp.int32, (B, B), 0)
        c_i = lax.broadcasted_iota(jnp.int32, (B, B), 1)
        triu = (c_i >= r_i).astype(BF16)
        sel = (lax.broadcasted_iota(jnp.int32, (aw, LANES), 0)
               == HEAD_DIM * lax.broadcasted_iota(jnp.int32, (aw, LANES), 1)).astype(BF16)
        pad_s[...] = jnp.zeros_like(pad_s)
        pad_s[0:SUBLANES, :] = d_ref[...]
        dc_s[...] = pad_s[...].T + _dot_split3(drs_ref[...], sel)
        carry = jnp.zeros((1, LANES), F32)
        for i in range(nb - 1, -1, -1):
            sl = slice(i * B, (i + 1) * B)
            rc = _split3_dot(triu, dc_s[sl, :])
            dc_s[sl, :] = rc + carry
            carry = carry + rc[0:1, :]
        dfl = dc_s[...] * _sigmoid(-(f_ref[...] + b_ref[...]))
        dfl_ref[...] = dfl.astype(BF16)
        db_ref[...] = jnp.sum(dfl, axis=0, keepdims=True)

    return pl.pallas_call(
        body, name="fgate_bwd", grid=(1,),
        in_specs=[pl.BlockSpec((SUBLANES, tp), lambda i: (0, 0)),
                  pl.BlockSpec((tp, aw), lambda i: (0, 0)),
                  pl.BlockSpec((tp, LANES), lambda i: (0, fcol)),
                  pl.BlockSpec((1, LANES), lambda i: (0, 0))],
        out_specs=[pl.BlockSpec((tp, LANES), lambda i: (0, 0)),
                   pl.BlockSpec((1, LANES), lambda i: (0, 0))],
        out_shape=[jax.ShapeDtypeStruct((tp, LANES), BF16), jax.ShapeDtypeStruct((1, LANES), F32)],
        scratch_shapes=[pltpu.VMEM((LANES, tp), F32), pltpu.VMEM((tp, LANES), F32)],
        compiler_params=_cparams("arbitrary"),
    )(dct8, drs, rest, bf_pad)


def in_proj_bwd(dh2, parts, w_in_t, wrest_t, h, g1, tm):
    tp, d = h.shape
    dq, dk, dv, dxr, dyr, dfl = parts
    aw, rw = dq.shape[1], dxr.shape[1]

    def body(dh2_ref, dq_ref, dk_ref, dv_ref, dxr_ref, dyr_ref, dfl_ref, wq_ref, wr_ref, h_ref, g_ref,
             dh_ref, dg_ref):
        i = pl.program_id(0)
        dz = _dot(dq_ref[...], wq_ref[0:aw, :])
        dz += _dot(dk_ref[...], wq_ref[aw:2 * aw, :])
        dz += _dot(dv_ref[...], wq_ref[2 * aw:3 * aw, :])
        dz += _dot(dxr_ref[...], wr_ref[0:rw, :])
        dz += _dot(dyr_ref[...], wr_ref[rw:2 * rw, :])
        dz += _dot(dfl_ref[...], wr_ref[2 * rw:2 * rw + LANES, :])
        dx, dg = _rms_bwd(h_ref[...], g_ref[...], dz)
        dh_ref[...] = dh2_ref[...] + dx
        _accumulate(dg_ref, dg, i == 0)

    row = lambda i: (i, 0)
    fix = lambda i: (0, 0)
    return pl.pallas_call(
        body, name="in_proj_bwd", grid=(tp // tm,),
        in_specs=[pl.BlockSpec((tm, d), row),
                  pl.BlockSpec((tm, aw), row), pl.BlockSpec((tm, aw), row), pl.BlockSpec((tm, aw), row),
                  pl.BlockSpec((tm, rw), row), pl.BlockSpec((tm, rw), row), pl.BlockSpec((tm, LANES), row),
                  pl.BlockSpec((3 * aw, d), fix), pl.BlockSpec(wrest_t.shape, fix),
                  pl.BlockSpec((tm, d), row), pl.BlockSpec((1, d), fix)],
        out_specs=[pl.BlockSpec((tm, d), row), pl.BlockSpec((1, d), fix)],
        out_shape=[jax.ShapeDtypeStruct((tp, d), F32), jax.ShapeDtypeStruct((1, d), F32)],
        compiler_params=_cparams("arbitrary"),
    )(dh2, dq, dk, dv, dxr, dyr, dfl, w_in_t, wrest_t, h, g1)


def dw_in_t(z, parts, nh, tr):
    tp, d = z.shape
    dq, dk, dv, dxr, dyr, dfl = parts
    aw, rw = dq.shape[1], dxr.shape[1]
    d_in = 3 * aw + nh + 2 * rw
    blk = d_in // N_DEV
    nr = tp // tr
    offs = [(0, aw), (aw, aw), (2 * aw, aw), (3 * aw + nh, rw), (3 * aw + nh + rw, rw)]

    def body(z_ref, dq_ref, dk_ref, dv_ref, dxr_ref, dyr_ref, dfl_ref, o_ref, acc):
        r = pl.program_id(0)

        @pl.when(r == 0)
        def _():
            acc[...] = jnp.zeros_like(acc)

        zt = z_ref[...]
        for (o, n), ref in zip(offs, (dq_ref, dk_ref, dv_ref, dxr_ref, dyr_ref)):
            acc[o:o + n, :] += _dot_tn(ref[...], zt)
        acc[3 * aw:3 * aw + nh, :] += _dot_tn(dfl_ref[...], zt)[0:nh, :]

        @pl.when(r == nr - 1)
        def _():
            for p in range(N_DEV):
                o_ref[p] = acc[p * blk:(p + 1) * blk, :].astype(BF16)

    row = lambda r: (r, 0)
    return pl.pallas_call(
        body, name="dw_in", grid=(nr,),
        in_specs=[pl.BlockSpec((tr, d), row),
                  pl.BlockSpec((tr, aw), row), pl.BlockSpec((tr, aw), row), pl.BlockSpec((tr, aw), row),
                  pl.BlockSpec((tr, rw), row), pl.BlockSpec((tr, rw), row), pl.BlockSpec((tr, LANES), row)],
        out_specs=pl.BlockSpec((N_DEV, blk, d), lambda r: (0, 0, 0)),
        out_shape=jax.ShapeDtypeStruct((N_DEV, blk, d), BF16),
        scratch_shapes=[pltpu.VMEM((d_in, d), F32)],
        compiler_params=_cparams("arbitrary"),
    )(z, dq, dk, dv, dxr, dyr, dfl)


def _place():
    return lax.axis_index("x"), lax.axis_index("y"), lax.axis_index("c")


HBM = pl.BlockSpec(memory_space=pltpu.HBM)
SEM = pl.BlockSpec(memory_space=pltpu.SEMAPHORE)
EFFECT = pltpu.SideEffectType.DATAFLOW_SIDE_EFFECTING


def _in_hbm(a):
    return pltpu.with_memory_space_constraint(a, pltpu.HBM)


def _as_list(a):
    return list(a) if isinstance(a, (list, tuple)) else [a]


def _gather_targets(x, y, c):
    return [(x, y, 1 - c), (1 - x, y, c), (x, 1 - y, c), (1 - x, 1 - y, c)]


def _slot(t):
    return 4 * t[0] + 2 * t[1] + t[2]


def gather_start(groups, name):
    flat = [a for g in groups for a in g]
    n = len(flat)
    ng = len(groups)
    lands = [lax.empty((N_DEV,) + a.shape, a.dtype) for a in flat]

    def body(*refs):
        src, land = refs[:n], refs[n:2 * n]
        sems = refs[2 * n:2 * n + 2 * ng]
        token = refs[-1]
        x, y, c = _place()
        me = 4 * x + 2 * y + c
        i = 0
        for gi, g in enumerate(groups):
            for a in range(len(g)):
                for k, t in enumerate(_gather_targets(x, y, c)):
                    pltpu.make_async_remote_copy(
                        src_ref=src[i], dst_ref=land[i].at[me],
                        send_sem=sems[2 * gi].at[4 * a + k], recv_sem=sems[2 * gi + 1].at[4 * a + k],
                        device_id=t, device_id_type=MESH).start()
                i += 1
        token[...] = jnp.zeros_like(token)

    sem_shapes = []
    for g in groups:
        sem_shapes += [pltpu.SemaphoreType.DMA((4 * len(g),)), pltpu.SemaphoreType.DMA((4 * len(g),))]
    out = pl.pallas_call(
        body, name=name,
        out_shape=sem_shapes + [pltpu.HBM(a.shape, a.dtype) for a in flat + lands]
        + [jax.ShapeDtypeStruct((SUBLANES, LANES), F32)],
        in_specs=[HBM] * (2 * n),
        out_specs=[SEM] * (2 * ng) + [HBM] * (2 * n) + [pl.BlockSpec(memory_space=pltpu.VMEM)],
        input_output_aliases={i: 2 * ng + i for i in range(2 * n)},
        compiler_params=pltpu.CompilerParams(has_side_effects=EFFECT),
    )(*[_in_hbm(a) for a in flat + lands])
    sems = out[:2 * ng]
    thru = out[2 * ng:2 * ng + 2 * n]
    srcs_t, lands_t = thru[:n], thru[n:]
    res, i = [], 0
    for gi, g in enumerate(groups):
        res.append((sems[2 * gi], sems[2 * gi + 1], srcs_t[i:i + len(g)], lands_t[i:i + len(g)]))
        i += len(g)
    return res, out[-1]


def gather_wait(send, recv, srcs, lands, after, name):
    n = len(srcs)

    def body(*refs):
        src, land = refs[:n], refs[n:2 * n]
        send_sem, recv_sem = refs[2 * n], refs[2 * n + 1]
        x, y, c = _place()
        for a in range(n):
            for k, t in enumerate(_gather_targets(x, y, c)):
                cp = pltpu.make_async_remote_copy(
                    src_ref=src[a], dst_ref=land[a].at[_slot(t)],
                    send_sem=send_sem.at[4 * a + k], recv_sem=recv_sem.at[4 * a + k],
                    device_id=t, device_id_type=MESH)
                cp.wait_send()
                cp.wait_recv()

    out = pl.pallas_call(
        body, name=name,
        out_shape=[pltpu.HBM(a.shape, a.dtype) for a in list(srcs) + list(lands)],
        in_specs=[HBM] * (2 * n) + [SEM, SEM] + [ANY] * len(_as_list(after)),
        out_specs=[HBM] * (2 * n),
        input_output_aliases={i: i for i in range(2 * n)},
        compiler_params=pltpu.CompilerParams(has_side_effects=EFFECT),
    )(*srcs, *lands, send, recv, *_as_list(after))
    return out[:n], out[n:]


def forward_start(lands, name):
    n = len(lands)

    def body(*refs):
        land = refs[:n]
        send_sem, recv_sem = refs[n], refs[n + 1]
        token = refs[-1]
        x, y, c = _place()
        for a in range(n):
            for j, chip in enumerate([(1 - x, y), (x, 1 - y), (1 - x, 1 - y)]):
                blk = land[a].at[_slot((*chip, c))]
                pltpu.make_async_remote_copy(src_ref=blk, dst_ref=blk, send_sem=send_sem.at[3 * a + j],
                                             recv_sem=recv_sem.at[3 * a + j], device_id=(x, y, 1 - c),
                                             device_id_type=MESH).start()
        token[...] = jnp.zeros_like(token)

    out = pl.pallas_call(
        body, name=name,
        out_shape=[pltpu.SemaphoreType.DMA((3 * n,)), pltpu.SemaphoreType.DMA((3 * n,))]
        + [pltpu.HBM(a.shape, a.dtype) for a in lands] + [jax.ShapeDtypeStruct((SUBLANES, LANES), F32)],
        in_specs=[HBM] * n,
        out_specs=[SEM, SEM] + [HBM] * n + [pl.BlockSpec(memory_space=pltpu.VMEM)],
        input_output_aliases={i: 2 + i for i in range(n)},
        compiler_params=pltpu.CompilerParams(has_side_effects=EFFECT),
    )(*[_in_hbm(a) for a in lands])
    return out[0], out[1], out[2:2 + n], out[-1][0, 0]


def forward_wait(send, recv, lands, after, name):
    n = len(lands)

    def body(*refs):
        land = refs[:n]
        send_sem, recv_sem = refs[n], refs[n + 1]
        x, y, c = _place()
        for a in range(n):
            for j, chip in enumerate([(1 - x, y), (x, 1 - y), (1 - x, 1 - y)]):
                cp = pltpu.make_async_remote_copy(
                    src_ref=land[a].at[_slot((*chip, c))], dst_ref=land[a].at[_slot((*chip, 1 - c))],
                    send_sem=send_sem.at[3 * a + j], recv_sem=recv_sem.at[3 * a + j],
                    device_id=(x, y, 1 - c), device_id_type=MESH)
                cp.wait_send()
                cp.wait_recv()

    return pl.pallas_call(
        body, name=name,
        out_shape=[pltpu.HBM(a.shape, a.dtype) for a in lands],
        in_specs=[HBM] * n + [SEM, SEM, ANY],
        out_specs=[HBM] * n,
        input_output_aliases={i: i for i in range(n)},
        compiler_params=pltpu.CompilerParams(has_side_effects=EFFECT),
    )(*lands, send, recv, after)


def _relations():
    return [(dx, dy, dc) for dx in (0, 1) for dy in (0, 1) for dc in (0, 1) if dx + dy + dc]


def _peer(x, y, c, rel):
    return ((1 - x) if rel[0] else x, (1 - y) if rel[1] else y, (1 - c) if rel[2] else c)


def exchange_start(srcs, lands, layer, name, after=()):
    n = len(srcs)
    after = _as_list(after)

    def body(*refs):
        src, land = refs[:n], refs[n:2 * n]
        send_sem, recv_sem = refs[2 * n + len(after)], refs[2 * n + len(after) + 1]
        token = refs[-1]
        x, y, c = _place()
        me = 4 * x + 2 * y + c
        for k, rel in enumerate(_relations()):
            peer = _peer(x, y, c, rel)
            for a in range(n):
                pltpu.make_async_remote_copy(
                    src_ref=src[a] if layer is None else src[a].at[_slot(peer)],
                    dst_ref=land[a].at[me] if layer is None else land[a].at[me, layer],
                    send_sem=send_sem.at[7 * a + k], recv_sem=recv_sem.at[7 * a + k],
                    device_id=peer, device_id_type=MESH).start()
        token[...] = jnp.zeros_like(token)

    out = pl.pallas_call(
        body, name=name,
        out_shape=[pltpu.SemaphoreType.DMA((7 * n,)), pltpu.SemaphoreType.DMA((7 * n,))]
        + [pltpu.HBM(a.shape, a.dtype) for a in list(srcs) + list(lands)]
        + [jax.ShapeDtypeStruct((SUBLANES, LANES), F32)],
        in_specs=[HBM] * (2 * n) + [ANY] * len(after),
        out_specs=[SEM, SEM] + [HBM] * (2 * n) + [pl.BlockSpec(memory_space=pltpu.VMEM)],
        input_output_aliases={i: 2 + i for i in range(2 * n)},
        compiler_params=pltpu.CompilerParams(has_side_effects=EFFECT),
    )(*[_in_hbm(a) for a in list(srcs) + list(lands)], *after)
    return out[0], out[1], out[2:2 + n], out[2 + n:2 + 2 * n], out[-1][0, 0]


def exchange_wait(send, recv, srcs, lands, after, layer, name):
    n = len(srcs)

    def body(*refs):
        src, land = refs[:n], refs[n:2 * n]
        send_sem, recv_sem = refs[2 * n], refs[2 * n + 1]
        x, y, c = _place()
        for k, rel in enumerate(_relations()):
            peer = _peer(x, y, c, rel)
            for a in range(n):
                cp = pltpu.make_async_remote_copy(
                    src_ref=src[a] if layer is None else src[a].at[_slot(peer)],
                    dst_ref=land[a].at[_slot(peer)] if layer is None else land[a].at[_slot(peer), layer],
                    send_sem=send_sem.at[7 * a + k], recv_sem=recv_sem.at[7 * a + k],
                    device_id=peer, device_id_type=MESH)
                cp.wait_send()
                cp.wait_recv()

    out = pl.pallas_call(
        body, name=name,
        out_shape=[pltpu.HBM(a.shape, a.dtype) for a in list(srcs) + list(lands)],
        in_specs=[HBM] * (2 * n) + [SEM, SEM] + [ANY] * len(_as_list(after)),
        out_specs=[HBM] * (2 * n),
        input_output_aliases={i: i for i in range(2 * n)},
        compiler_params=pltpu.CompilerParams(has_side_effects=EFFECT),
    )(*srcs, *lands, send, recv, *_as_list(after))
    return out[:n], out[n:]


def _adamw_math(g, w, m, v):
    m = ADAM_B1 * m + (1.0 - ADAM_B1) * g
    v = ADAM_B2 * v + (1.0 - ADAM_B2) * (g * g)
    m_hat = m / (1.0 - ADAM_B1 ** ADAM_STEP)
    v_hat = v / (1.0 - ADAM_B2 ** ADAM_STEP)
    delta = -ADAM_LR * (m_hat / (jnp.sqrt(v_hat) + ADAM_EPS) + ADAM_WD * w)
    return delta, m, v


def _sum_with_own(p_ref, own_refs, layer, me):
    own = own_refs[0][...]
    for k in range(1, len(own_refs)):
        own = jnp.where(layer == k, own_refs[k][...], own)
    g = None
    for p in range(p_ref.shape[0]):
        term = jnp.where(me == p, own, p_ref[p]).astype(F32)
        g = term if g is None else g + term
    return g


def sum_adamw(parts, owns, me, w, m, v, tr, name):
    npart, rows, cols = parts.shape
    nl = len(owns)
    per_layer = rows // nl // tr

    def body(me_ref, p_ref, *refs):
        own_refs = refs[:nl]
        w_ref, m_ref, v_ref, g_ref, d_ref, nm_ref, nv_ref = refs[nl:]
        g = _sum_with_own(p_ref, own_refs, pl.program_id(0) // per_layer, me_ref[0])
        delta, nm, nv = _adamw_math(g, w_ref[...], m_ref[...], v_ref[...])
        g_ref[...] = g
        d_ref[...] = delta
        nm_ref[...] = nm
        nv_ref[...] = nv

    blk = pl.BlockSpec((tr, cols), lambda i, me_ref: (i, 0))
    own_specs = [pl.BlockSpec((None, tr, cols),
                              lambda i, me_ref, l=l: (me_ref[0], jnp.clip(i - l * per_layer, 0, per_layer - 1), 0))
                 for l in range(nl)]
    return pl.pallas_call(
        body, name=name,
        grid_spec=pltpu.PrefetchScalarGridSpec(
            num_scalar_prefetch=1, grid=(rows // tr,),
            in_specs=[pl.BlockSpec((npart, tr, cols), lambda i, me_ref: (0, i, 0))] + own_specs + [blk, blk, blk],
            out_specs=[blk] * 4),
        out_shape=[jax.ShapeDtypeStruct((rows, cols), F32)] * 4,
        compiler_params=_cparams("arbitrary"),
    )(me, parts, *owns, w, m, v)


def sum_adamw_t(parts, owns, me, w, m, v, name):
    npart, nl, rows, cols = parts.shape

    def body(me_ref, p_ref, *refs):
        own_refs = refs[:nl]
        w_ref, m_ref, v_ref, g_ref, d_ref, nm_ref, nv_ref = refs[nl:]
        g = _sum_with_own(p_ref, own_refs, pl.program_id(0), me_ref[0])
        delta, nm, nv = _adamw_math(g, w_ref[...], m_ref[...], v_ref[...])
        g_ref[...] = g
        d_ref[...] = delta
        nm_ref[...] = nm
        nv_ref[...] = nv

    blk = pl.BlockSpec((None, rows, cols), lambda l, me_ref: (l, 0, 0))
    own_specs = [pl.BlockSpec((None, rows, cols), lambda l, me_ref: (me_ref[0], 0, 0)) for _ in range(nl)]
    return pl.pallas_call(
        body, name=name,
        grid_spec=pltpu.PrefetchScalarGridSpec(
            num_scalar_prefetch=1, grid=(nl,),
            in_specs=[pl.BlockSpec((npart, None, rows, cols), lambda l, me_ref: (0, l, 0, 0))] + own_specs
            + [blk, blk, blk],
            out_specs=[blk] * 4),
        out_shape=[jax.ShapeDtypeStruct((nl, rows, cols), F32)] * 4,
        compiler_params=_cparams("arbitrary"),
    )(me, parts, *owns, w, m, v)


def adamw_group(gs, ws, ms, vs, name):
    n = len(gs)

    def body(*refs):
        g, w, m, v, outs = refs[:n], refs[n:2 * n], refs[2 * n:3 * n], refs[3 * n:4 * n], refs[4 * n:]
        for i in range(n):
            delta, nm, nv = _adamw_math(g[i][...], w[i][...], m[i][...], v[i][...])
            outs[i][...] = delta
            outs[n + i][...] = nm
            outs[2 * n + i][...] = nv

    vmem = pl.BlockSpec(memory_space=pltpu.VMEM)
    out = pl.pallas_call(
        body, name=name,
        in_specs=[vmem] * (4 * n), out_specs=[vmem] * (3 * n),
        out_shape=[jax.ShapeDtypeStruct(a.shape, F32) for a in list(ws) * 3],
        compiler_params=_cparams(),
    )(*gs, *ws, *ms, *vs)
    return out[:n], out[n:2 * n], out[2 * n:]


def sum_parts(parts, name):
    npart, rows, cols = parts.shape

    def body(p_ref, g_ref):
        g = p_ref[0].astype(F32)
        for p in range(1, npart):
            g = g + p_ref[p].astype(F32)
        g_ref[...] = g

    return pl.pallas_call(
        body, name=name, grid=(1,),
        in_specs=[pl.BlockSpec((npart, rows, cols), lambda i: (0, 0, 0))],
        out_specs=pl.BlockSpec((rows, cols), lambda i: (0, 0)),
        out_shape=jax.ShapeDtypeStruct((rows, cols), F32),
        compiler_params=_cparams("arbitrary"),
    )(parts)


def _round_up(n, m):
    return (n + m - 1) // m * m


def _block_diag_pairs(w):
    nb, b, _ = w.shape
    per = LANES // b
    ng = nb // per
    w = w.reshape(ng, per, b, b)
    eye = jnp.eye(per, dtype=w.dtype)
    out = jnp.einsum('gpij,pq->gpiqj', w, eye).reshape(ng, LANES, LANES)
    return out.astype(BF16)


def _block_diag_extract(g, b):
    ng = g.shape[0]
    per = LANES // b
    g = g.reshape(ng, per, b, per, b)
    idx = jnp.arange(per)
    return g[:, idx, :, idx, :].transpose(1, 0, 2, 3).reshape(ng * per, b, b)


def _tiles(v):
    v = v.reshape(-1)
    n = _round_up(v.shape[0], SUBLANES * LANES)
    return jnp.pad(v, (0, n - v.shape[0])).reshape(-1, LANES)


SMALL = ['attn_norm_g', 'b_f', 'conv_w', 'conv_b', 'w_gate_a', 'b_gate_a', 'w_gate_x', 'b_gate_x',
         'lru_L', 'attn_out_g', 'rec_out_g', 'mlp_norm_g', 'final_g', 'meta']


def _pack(d):
    return jnp.concatenate([_tiles(d[n]) for n in SMALL], axis=0)


def _unpack(vec, shapes):
    out, r = {}, 0
    for n in SMALL:
        size = math.prod(shapes[n])
        nr = _round_up(size, SUBLANES * LANES) // LANES
        out[n] = vec[r:r + nr].reshape(-1)[:size].reshape(shapes[n])
        r += nr
    return out


def _row_tile(tp):
    return tp // 4 if (tp // 4) % 16 == 0 else tp


def local_step(x, tgt, meta, small, hooks):
    s, d = x.shape
    t_real = s + N_META
    tp = _round_up(t_real, ATT_BLOCK)
    depth = small['attn_norm_g'].shape[0]
    nh = small['b_f'].shape[1]
    rw = small['conv_b'].shape[1]
    blk = small['w_gate_a'].shape[2]
    tm = _row_tile(tp)
    tm2 = tp // 2
    fcol = 2 * rw // LANES

    h = jnp.concatenate([meta, x, jnp.zeros((tp - t_real, d), F32)], axis=0)
    tgt_p = jnp.pad(tgt, ((N_META, tp - t_real), (0, 0)))
    row = lambda v: v.reshape(1, -1)
    bf_pad = jnp.pad(small['b_f'], ((0, 0), (0, LANES - nh)))

    saved = []
    for l in range(depth):
        w_in_t, wrest_t, wout, tok_w = hooks.mixer_weights(l, h)
        wga = _block_diag_pairs(small['w_gate_a'][l])
        wgx = _block_diag_pairs(small['w_gate_x'][l])
        z, qkv, rest = in_proj(h, row(small['attn_norm_g'][l]) + tok_w, w_in_t, wrest_t, 3 * nh * HEAD_DIM, tm)
        c, ct = fgate_fwd(rest, bf_pad[l:l + 1], fcol)
        o, lset = attn_fwd(qkv, c, ct, nh)
        rec, hr, xc, *gates = rec_fwd(rest, small['conv_w'][l], row(small['conv_b'][l]), wga,
                                      row(small['b_gate_a'][l]), wgx, row(small['b_gate_x'][l]),
                                      row(small['lru_L'][l]), rw)
        gup, gdown, tok_w = hooks.mlp_weights(l, rec)
        h2, mix, z2, u, h3 = out_mlp_fwd(h, o, rec, row(small['attn_out_g'][l]), row(small['rec_out_g'][l]), wout,
                                         row(small['mlp_norm_g'][l]) + tok_w, gup, gdown, tm)
        saved.append(dict(h=h, z=z, qkv=qkv, rest=rest, c=c, ct=ct, o=o, lset=lset, rec=rec, hr=hr, xc=xc,
                          h2=h2, mix=mix, z2=z2, u=u, wga=wga, wgx=wgx, gates=gates,
                          w_in_t=w_in_t, wrest_t=wrest_t, wout=wout, gup=gup, gdown=gdown))
        h = h3

    dh, dgf, loss = loss_head(h, row(small['final_g']), tgt_p, t_real, tm)

    gs = {n: [None] * depth for n in SMALL if n not in ('final_g', 'meta')}
    tok = jnp.zeros((), F32)
    for l in reversed(range(depth)):
        sv = saved[l]
        gup, gdown = sv['gup'], sv['gdown']
        tf = gup.shape[2]
        dup, dh2, dg2, dhb = mlp_bwd(dh, sv['u'], sv['h2'], row(small['mlp_norm_g'][l]) + tok, gup, gdown, tm)
        gs['mlp_norm_g'][l] = dg2[0]
        do, drec, dga, dgr, dw_out = out_proj_bwd(dh2, sv['mix'], sv['o'], sv['rec'], row(small['attn_out_g'][l]),
                                                  row(small['rec_out_g'][l]), sv['wout'], tm)
        gs['attn_out_g'][l] = dga[0]
        gs['rec_out_g'][l] = dgr[0]
        dw_down, dw_up = dw_mlp(sv['u'], dhb, sv['z2'], dup, tf)
        blocks = dict(w_down=dw_down, w_up=dw_up, w_out=dw_out.reshape(N_DEV, d // N_DEV, d))
        tok = hooks.grads_ready(l, 'mlp', blocks)
        dxr, dyr, dwga, dwgx, vec = rec_bwd(drec, sv['hr'], sv['xc'], sv['gates'], sv['rest'], small['conv_w'][l],
                                            sv['wga'], sv['wgx'], row(small['lru_L'][l]) + tok, rw)
        gs['w_gate_a'][l] = _block_diag_extract(dwga, blk)
        gs['w_gate_x'][l] = _block_diag_extract(dwgx, blk)
        vec = vec.transpose(1, 0, 2).reshape(SUBLANES, rw)
        gs['conv_w'][l] = vec[0:CONV_WIDTH]
        gs['conv_b'][l] = vec[4]
        gs['b_gate_a'][l] = vec[5]
        gs['b_gate_x'][l] = vec[6]
        gs['lru_L'][l] = vec[7]
        dq, dk, dv, drow, dcol = attn_bwd(sv['qkv'], do, sv['o'], sv['lset'], sv['c'], sv['ct'] + tok, nh)
        drow8 = drow[:, 0:2, :].reshape(nh, tp)
        if nh < SUBLANES:
            drow8 = jnp.pad(drow8, ((0, SUBLANES - nh), (0, 0)))
        dfl, dbf = fgate_bwd(drow8, dcol, sv['rest'], bf_pad[l:l + 1], fcol)
        gs['b_f'][l] = dbf[0, 0:nh]
        parts = (dq, dk, dv, dxr, dyr, dfl)
        dh, dg1 = in_proj_bwd(dh2, parts, sv['w_in_t'], sv['wrest_t'], sv['h'], row(small['attn_norm_g'][l]), tm)
        gs['attn_norm_g'][l] = dg1[0]
        first = ()
        if l == 0:
            grads = {n: jnp.stack(v) for n, v in gs.items()}
            grads['final_g'] = dgf[0]
            grads['meta'] = dh[0:N_META]
            first = hooks.small_ready(grads)
        dw_in = dw_in_t(sv['z'], parts, nh, tm2)
        tok = hooks.grads_ready(l, 'in', dict(w_in=dw_in), first)

    return loss[0, 0], dh


def prep_weights(g_in, g_out, nh, rw):
    d = g_in.shape[2]
    w_in_t = g_in.reshape(-1, d)
    f0 = 3 * nh * HEAD_DIM
    wrest_t = jnp.concatenate([w_in_t[f0 + nh:f0 + nh + 2 * rw],
                               jnp.pad(w_in_t[f0:f0 + nh], ((0, LANES - nh), (0, 0)))], axis=0)
    return w_in_t, wrest_t, g_out.reshape(d, d)


BIG = ['w_in', 'w_out', 'w_up', 'w_down']
EXCHANGE_GROUPS = {'mlp': ['w_down', 'w_up', 'w_out'], 'in': ['w_in']}
WEIGHTS = ['meta', 'attn_norm_g', 'w_in', 'b_f', 'conv_w', 'conv_b', 'w_gate_a', 'b_gate_a', 'w_gate_x', 'b_gate_x',
           'lru_L', 'attn_out_g', 'rec_out_g', 'w_out', 'mlp_norm_g', 'w_up', 'w_down', 'final_g']


def _set_own(arr, own, me):
    return lax.dynamic_update_slice_in_dim(arr, own[None], me, axis=0)


class _Step:
    def __init__(self, w, nh, rw, me):
        self.w, self.nh, self.rw, self.me = w, nh, rw, me
        depth = w['w_in'].shape[0]
        first = [w['w_in_t'][:, 0, :].astype(BF16), w['w_out'][0].astype(BF16), w['meta'], w['conv_w']]
        self.pending, token = gather_start([first], "gather_start_0")
        zero = token[0, 0].astype(BF16)
        groups = [[w['w_up'][0].astype(BF16) + zero, w['w_down'][0].astype(BF16) + zero]]
        for l in range(1, depth):
            groups.append([w['w_in_t'][:, l, :].astype(BF16) + zero, w['w_out'][l].astype(BF16) + zero])
            groups.append([w['w_up'][l].astype(BF16) + zero, w['w_down'][l].astype(BF16) + zero])
        rest, _ = gather_start(groups, "gather_start_1")
        self.pending += rest
        self.first_after = rest[0][2][0]
        self.gathered = {}
        self.passing = {}
        self.token = jnp.zeros((), F32)
        self.lands = {n: lax.empty((N_DEV,) + w[n].shape, BF16) for n in BIG}
        din8, _, d = w['w_in_t'].shape
        self.lands['w_in'] = lax.empty((N_DEV, depth, din8, d), BF16)
        self.started = []
        self.small = None

    def _pass_on(self, gi, after):
        if gi < len(self.pending) and gi not in self.passing:
            send, recv, srcs, lands = self.pending[gi]
            srcs, lands = gather_wait(send, recv, srcs, lands, after, "gather_wait_%d" % gi)
            fsend, frecv, lands, token = forward_start(lands, "forward_start_%d" % gi)
            self.passing[gi] = (fsend, frecv, srcs, lands)
            self.token = token

    def group(self, gi, after):
        if gi not in self.gathered:
            self._pass_on(gi, after)
            fsend, frecv, srcs, lands = self.passing[gi]
            lands = forward_wait(fsend, frecv, lands, after, "forward_wait_%d" % gi)
            self.gathered[gi] = [_set_own(g, own, self.me) for g, own in zip(lands, srcs)]
            if gi >= 2:
                self._pass_on(gi + 1, lands[0])
        return self.gathered[gi]

    def mixer_weights(self, l, after):
        g = self.group(2 * l, after)
        return (*prep_weights(g[0], g[1], self.nh, self.rw), self.token)

    def mlp_weights(self, l, after):
        g = self.group(2 * l + 1, after)
        return g[0], g[1], self.token

    def grads_ready(self, l, group, blocks, after=()):
        names = EXCHANGE_GROUPS[group]
        send, recv, srcs, lands, token = exchange_start(
            [blocks[n] for n in names], [self.lands[n] for n in names], l, "exchange_start_%s_%d" % (group, l),
            after)
        for n, a in zip(names, lands):
            self.lands[n] = a
        self.started.append((l, group, send, recv, srcs))
        return token

    def small_ready(self, grads):
        self.small_shapes = {n: grads[n].shape for n in SMALL}
        packed = _pack(grads).astype(BF16)
        send, recv, srcs, lands, token = exchange_start(
            [packed], [lax.empty((N_DEV,) + packed.shape, BF16)], None, "small_start")
        self.small = (send, recv, srcs, lands)
        return srcs[0]

    def small_sum(self, after):
        send, recv, srcs, lands = self.small
        srcs, lands = exchange_wait(send, recv, srcs, lands, after, None, "small_wait")
        parts = _set_own(lands[0], srcs[0], self.me)
        return _unpack(sum_parts(parts, "sum_small_grads"), self.small_shapes)

    def received(self, group, after):
        names = EXCHANGE_GROUPS[group]
        own = {n: [None] * self.w[n].shape[0] for n in names}
        for l, grp, send, recv, srcs in self.started:
            if grp != group:
                continue
            srcs, lands = exchange_wait(send, recv, srcs, [self.lands[n] for n in names], after, l,
                                        "exchange_wait_%s_%d" % (group, l))
            for n, a, sr in zip(names, lands, srcs):
                self.lands[n] = a
                own[n][l] = sr
        return {n: (self.lands[n], own[n]) for n in names}


def kernel(x, meta, attn_norm_g, w_in, b_f, conv_w, conv_b, w_gate_a, b_gate_a, w_gate_x, b_gate_x, lru_L, attn_out_g, rec_out_g, w_out, mlp_norm_g, w_up, w_down, final_g, loss_target, m_meta, m_attn_norm_g, m_w_in, m_b_f, m_conv_w, m_conv_b, m_w_gate_a, m_b_gate_a, m_w_gate_x, m_b_gate_x, m_lru_L, m_attn_out_g, m_rec_out_g, m_w_out, m_mlp_norm_g, m_w_up, m_w_down, m_final_g, v_meta, v_attn_norm_g, v_w_in, v_b_f, v_conv_w, v_conv_b, v_w_gate_a, v_b_gate_a, v_w_gate_x, v_b_gate_x, v_lru_L, v_attn_out_g, v_rec_out_g, v_w_out, v_mlp_norm_g, v_w_up, v_w_down, v_final_g):
    w = dict(meta=meta, attn_norm_g=attn_norm_g, w_in=w_in, b_f=b_f, conv_w=conv_w, conv_b=conv_b,
             w_gate_a=w_gate_a, b_gate_a=b_gate_a, w_gate_x=w_gate_x, b_gate_x=b_gate_x, lru_L=lru_L,
             attn_out_g=attn_out_g, rec_out_g=rec_out_g, w_out=w_out, mlp_norm_g=mlp_norm_g, w_up=w_up,
             w_down=w_down, final_g=final_g)
    mo = dict(meta=m_meta, attn_norm_g=m_attn_norm_g, w_in=m_w_in, b_f=m_b_f, conv_w=m_conv_w, conv_b=m_conv_b,
              w_gate_a=m_w_gate_a, b_gate_a=m_b_gate_a, w_gate_x=m_w_gate_x, b_gate_x=m_b_gate_x, lru_L=m_lru_L,
              attn_out_g=m_attn_out_g, rec_out_g=m_rec_out_g, w_out=m_w_out, mlp_norm_g=m_mlp_norm_g,
              w_up=m_w_up, w_down=m_w_down, final_g=m_final_g)
    vo = dict(meta=v_meta, attn_norm_g=v_attn_norm_g, w_in=v_w_in, b_f=v_b_f, conv_w=v_conv_w, conv_b=v_conv_b,
              w_gate_a=v_w_gate_a, b_gate_a=v_b_gate_a, w_gate_x=v_w_gate_x, b_gate_x=v_b_gate_x, lru_L=v_lru_L,
              attn_out_g=v_attn_out_g, rec_out_g=v_rec_out_g, w_out=v_w_out, mlp_norm_g=v_mlp_norm_g,
              w_up=v_w_up, w_down=v_w_down, final_g=v_final_g)
    depth = w_in.shape[0]
    nh = b_f.shape[1]
    rw = conv_b.shape[1]
    me = 4 * lax.axis_index("x") + 2 * lax.axis_index("y") + lax.axis_index("c")

    w['w_in_t'] = jnp.transpose(w_in, (2, 0, 1))
    swap = lambda a: jnp.swapaxes(a, 1, 2)
    step = _Step(w, nh, rw, me)
    g0 = step.group(0, step.first_after)
    meta_full = g0[2].transpose(1, 0, 2).reshape(N_META, -1)
    conv_full = g0[3].transpose(1, 2, 0, 3).reshape(depth, CONV_WIDTH, rw)
    small = {n: w[n] for n in SMALL}
    small['conv_w'] = conv_full

    loss_part, dh0 = local_step(x[0], loss_target[0], meta_full, small, step)
    loss = lax.psum(loss_part, ("x", "y", "c"))
    grad_x = dh0[N_META:N_META + x.shape[1]][None]

    out_g, out_d, out_m, out_v = {}, {}, {}, {}
    me1 = me.reshape(1).astype(jnp.int32)

    def update_big(group, after):
        for n, (r, owns) in step.received(group, after).items():
            if n == 'w_in':
                out = sum_adamw_t(r, owns, me1, swap(w[n]), swap(mo[n]), swap(vo[n]), "adamw_w_in")
                out = [swap(a) for a in out]
            else:
                shp = w[n].shape
                rows, cols = shp[0] * shp[1], shp[2]
                tr = min(512 if cols <= 512 else 256, shp[1])
                out = sum_adamw(r.reshape(N_DEV, rows, cols), owns, me1, w[n].reshape(rows, cols),
                                mo[n].reshape(rows, cols), vo[n].reshape(rows, cols), tr, "adamw_" + n)
                out = [a.reshape(shp) for a in out]
            out_g[n], out_d[n], out_m[n], out_v[n] = out
            after = out[0]
        return after

    update_big('mlp', step.started[-1][4][0])

    gsum = step.small_sum([out_g[n] for n in EXCHANGE_GROUPS['mlp']])
    gsum['meta'] = lax.dynamic_slice_in_dim(gsum['meta'], me * meta.shape[1], meta.shape[1], axis=1)
    gsum['conv_w'] = lax.dynamic_slice_in_dim(gsum['conv_w'], me * conv_w.shape[2], conv_w.shape[2], axis=2)
    as2d = lambda a: a.reshape(-1, a.shape[-1])
    deltas, new_m, new_v = adamw_group([as2d(gsum[n]) for n in SMALL], [as2d(w[n]) for n in SMALL],
                                       [as2d(mo[n]) for n in SMALL], [as2d(vo[n]) for n in SMALL], "adamw_small")
    for i, n in enumerate(SMALL):
        out_g[n] = gsum[n]
        out_d[n], out_m[n], out_v[n] = [a[i].reshape(w[n].shape) for a in (deltas, new_m, new_v)]

    update_big('in', deltas[0])

    return (loss, grad_x, *[out_g[n] for n in WEIGHTS], *[out_d[n] for n in WEIGHTS],
            *[out_m[n] for n in WEIGHTS], *[out_v[n] for n in WEIGHTS])
```
